```python
import math
import jax, jax.numpy as jnp
from jax import lax
import numpy as np

D_MODEL = 1024
BATCH = 8
SEQ = 4096
DEPTH = 1

SSM_WIDTH = 512
SSM_GROUP = 16
SSM_GROUPS = SSM_WIDTH // SSM_GROUP
SSM_STATE = 64
DT_MIN = 1e-3
DT_MAX = 1e-1
SGU_WIDTH = 512
SGU_GROUPS = 8
SGU_GROUP_DIM = SGU_WIDTH // SGU_GROUPS
CHUNK = 128
N_BRANCH = 2
IN_COLS = SSM_WIDTH + 2 * SGU_WIDTH + N_BRANCH * D_MODEL
D_FF = 2816
CONV_WIDTH = 3
EPS = 1e-6

kernel_name = "hybrid_s5_gmlp_gated_block"


def rms_norm(x, g):
    x32 = x.astype(jnp.float32)
    y = x32 * lax.rsqrt(jnp.mean(x32 * x32, axis=-1, keepdims=True) + EPS)
    return (y * g.astype(jnp.float32)).astype(x.dtype)


def s5_branch(u, a_re, a_im, log_dt, b_re, b_im, c_re, c_im, d_skip, w_glu, b_glu):
    bsz, s, _ = u.shape
    f32 = jnp.float32
    u32 = u.astype(f32).reshape(bsz, s, SSM_GROUPS, SSM_GROUP)
    dt = jnp.exp(log_dt.astype(f32))[:, None]
    ar = a_re.astype(f32)
    ai = a_im.astype(f32)
    mag = jnp.exp(dt * ar)
    abar_re = mag * jnp.cos(dt * ai)
    abar_im = mag * jnp.sin(dt * ai)
    den = ar * ar + ai * ai
    nr = abar_re - 1.0
    ni = abar_im
    f_re = (nr * ar + ni * ai) / den
    f_im = (ni * ar - nr * ai) / den
    br = b_re.astype(f32)
    bi = b_im.astype(f32)
    bbar_re = f_re[..., None] * br - f_im[..., None] * bi
    bbar_im = f_re[..., None] * bi + f_im[..., None] * br
    bu_re = jnp.einsum("bsgh,gph->bsgp", u32, bbar_re)
    bu_im = jnp.einsum("bsgh,gph->bsgp", u32, bbar_im)
    a_seq_re = jnp.broadcast_to(abar_re[None, None], (1, s, SSM_GROUPS, SSM_STATE))
    a_seq_im = jnp.broadcast_to(abar_im[None, None], (1, s, SSM_GROUPS, SSM_STATE))

    def combine(left, right):
        alr, ali, blr, bli = left
        arr, ari, brr, bri = right
        return (arr * alr - ari * ali,
                arr * ali + ari * alr,
                arr * blr - ari * bli + brr,
                arr * bli + ari * blr + bri)

    _, _, st_re, st_im = lax.associative_scan(
        combine, (a_seq_re, a_seq_im, bu_re, bu_im), axis=1)
    y = (jnp.einsum("bsgp,ghp->bsgh", st_re, c_re.astype(f32))
         - jnp.einsum("bsgp,ghp->bsgh", st_im, c_im.astype(f32)))
    y = y + d_skip.astype(f32).reshape(SSM_GROUPS, SSM_GROUP) * u32
    y = jax.nn.gelu(y.reshape(bsz, s, SSM_WIDTH))
    y = y * jax.nn.sigmoid(y @ w_glu.astype(f32) + b_glu.astype(f32))
    return y.astype(u.dtype)


def sgu_branch(uv, g_sgu, w_s, b_s):
    bsz, s, _ = uv.shape
    uv = jax.nn.gelu(uv)
    u, v = jnp.split(uv, 2, axis=-1)
    v = rms_norm(v, g_sgu)
    n_chunks = s // CHUNK
    v = v.reshape(bsz, n_chunks, CHUNK, SGU_GROUPS, SGU_GROUP_DIM)
    mask = jnp.tril(jnp.ones((CHUNK, CHUNK), dtype=bool))
    ws = jnp.where(mask[None], w_s, 0.0)
    mixed = jnp.einsum("gts,bcsgd->bctgd", ws, v) + b_s.T[:, :, None]
    mixed = mixed.reshape(bsz, s, SGU_WIDTH)
    return u * mixed


def causal_depthwise_conv(y, w, b):
    s = y.shape[1]
    yp = jnp.pad(y, ((0, 0), (CONV_WIDTH - 1, 0), (0, 0)))
    return sum(yp[:, k:k + s] * w[k] for k in range(CONV_WIDTH)) + b


def _fwd_setup_inputs(seed: int = 0) -> dict:
    key = jax.random.key(seed)
    ks = jax.random.split(key, 25)
    L = DEPTH
    f32 = jnp.float32

    def nrm(k, shape, scale):
        return jax.random.normal(k, shape, f32) * scale

    n_idx = jnp.arange(SSM_STATE, dtype=f32)
    return {
        "x": nrm(ks[0], (BATCH, SEQ, D_MODEL), 1.0),
        "g_mix": 1.0 + nrm(ks[1], (L, D_MODEL), 0.02),
        "w_in": nrm(ks[2], (L, D_MODEL, IN_COLS), D_MODEL ** -0.5),
        "a_re": -0.5 + nrm(ks[3], (L, SSM_GROUPS, SSM_STATE), 0.01),
        "a_im": math.pi * n_idx + nrm(ks[4], (L, SSM_GROUPS, SSM_STATE), 0.01),
        "log_dt": jax.random.uniform(ks[5], (L, SSM_GROUPS), f32,
                                     minval=math.log(DT_MIN), maxval=math.log(DT_MAX)),
        "b_re": nrm(ks[6], (L, SSM_GROUPS, SSM_STATE, SSM_GROUP), (2 * SSM_GROUP) ** -0.5),
        "b_im": nrm(ks[7], (L, SSM_GROUPS, SSM_STATE, SSM_GROUP), (2 * SSM_GROUP) ** -0.5),
        "c_re": nrm(ks[8], (L, SSM_GROUPS, SSM_GROUP, SSM_STATE), SSM_STATE ** -0.5),
        "c_im": nrm(ks[9], (L, SSM_GROUPS, SSM_GROUP, SSM_STATE), SSM_STATE ** -0.5),
        "d_skip": nrm(ks[10], (L, SSM_WIDTH), 1.0),
        "w_glu": nrm(ks[11], (L, SSM_WIDTH, SSM_WIDTH), SSM_WIDTH ** -0.5),
        "b_glu": nrm(ks[12], (L, SSM_WIDTH), 0.01),
        "w_proj_a": nrm(ks[13], (L, SSM_WIDTH, D_MODEL), SSM_WIDTH ** -0.5),
        "g_sgu": 1.0 + nrm(ks[14], (L, SGU_WIDTH), 0.02),
        "w_s": nrm(ks[15], (L, SGU_GROUPS, CHUNK, CHUNK), CHUNK ** -0.5),
        "b_s": 1.0 + nrm(ks[16], (L, SGU_GROUPS, CHUNK), 0.01),
        "w_proj_b": nrm(ks[17], (L, SGU_WIDTH, D_MODEL), SGU_WIDTH ** -0.5),
        "w_out": nrm(ks[18], (L, D_MODEL, D_MODEL), D_MODEL ** -0.5),
        "g_ffn": 1.0 + nrm(ks[19], (L, D_MODEL), 0.02),
        "w_up": nrm(ks[20], (L, D_MODEL, 2 * D_FF), D_MODEL ** -0.5),
        "conv_w": nrm(ks[21], (L, CONV_WIDTH, 2 * D_FF), CONV_WIDTH ** -0.5),
        "conv_b": nrm(ks[22], (L, 2 * D_FF), 0.01),
        "w_down": nrm(ks[23], (L, D_FF, D_MODEL), D_FF ** -0.5),
        "g_final": 1.0 + nrm(ks[24], (D_MODEL,), 0.02),
    }


def _fwd_reference(x, g_mix, w_in, a_re, a_im, log_dt, b_re, b_im, c_re, c_im, d_skip,
              w_glu, b_glu, w_proj_a, g_sgu, w_s, b_s, w_proj_b, w_out, g_ffn,
              w_up, conv_w, conv_b, w_down, g_final):
    for l in range(DEPTH):
        h = rms_norm(x, g_mix[l])
        p = h @ w_in[l]
        u_ssm = p[..., :SSM_WIDTH]
        uv_sgu = p[..., SSM_WIDTH:SSM_WIDTH + 2 * SGU_WIDTH]
        gate_logits = p[..., SSM_WIDTH + 2 * SGU_WIDTH:]
        y_a = s5_branch(u_ssm, a_re[l], a_im[l], log_dt[l], b_re[l], b_im[l],
                        c_re[l], c_im[l], d_skip[l], w_glu[l], b_glu[l]) @ w_proj_a[l]
        y_b = sgu_branch(uv_sgu, g_sgu[l], w_s[l], b_s[l]) @ w_proj_b[l]
        g_a, g_b = jnp.split(jax.nn.sigmoid(gate_logits), N_BRANCH, axis=-1)
        x = x + (g_a * y_a + g_b * y_b) @ w_out[l]
        h = rms_norm(x, g_ffn[l])
        up = causal_depthwise_conv(h @ w_up[l], conv_w[l], conv_b[l])
        a, b = jnp.split(up, 2, axis=-1)
        x = x + (jax.nn.silu(a) * b) @ w_down[l]
    return rms_norm(x, g_final)


import jax as _jax
import jax.numpy as _jnp

TWIN_FORMAT = 'train_step'
FWD_PARAMS = ['x', 'g_mix', 'w_in', 'a_re', 'a_im', 'log_dt', 'b_re', 'b_im', 'c_re', 'c_im', 'd_skip', 'w_glu', 'b_glu', 'w_proj_a', 'g_sgu', 'w_s', 'b_s', 'w_proj_b', 'w_out', 'g_ffn', 'w_up', 'conv_w', 'conv_b', 'w_down', 'g_final']
TWIN_WEIGHTS = ['g_mix', 'w_in', 'a_re', 'a_im', 'log_dt', 'b_re', 'b_im', 'c_re', 'c_im', 'd_skip', 'w_glu', 'b_glu', 'w_proj_a', 'g_sgu', 'w_s', 'b_s', 'w_proj_b', 'w_out', 'g_ffn', 'w_up', 'conv_w', 'conv_b', 'w_down', 'g_final']
TWIN_DIFF_INPUT = 'x'
TWIN_INPUTS = ['x', 'g_mix', 'w_in', 'a_re', 'a_im', 'log_dt', 'b_re', 'b_im', 'c_re', 'c_im', 'd_skip', 'w_glu', 'b_glu', 'w_proj_a', 'g_sgu', 'w_s', 'b_s', 'w_proj_b', 'w_out', 'g_ffn', 'w_up', 'conv_w', 'conv_b', 'w_down', 'g_final', 'loss_target', 'm_g_mix', 'm_w_in', 'm_a_re', 'm_a_im', 'm_log_dt', 'm_b_re', 'm_b_im', 'm_c_re', 'm_c_im', 'm_d_skip', 'm_w_glu', 'm_b_glu', 'm_w_proj_a', 'm_g_sgu', 'm_w_s', 'm_b_s', 'm_w_proj_b', 'm_w_out', 'm_g_ffn', 'm_w_up', 'm_conv_w', 'm_conv_b', 'm_w_down', 'm_g_final', 'v_g_mix', 'v_w_in', 'v_a_re', 'v_a_im', 'v_log_dt', 'v_b_re', 'v_b_im', 'v_c_re', 'v_c_im', 'v_d_skip', 'v_w_glu', 'v_b_glu', 'v_w_proj_a', 'v_g_sgu', 'v_w_s', 'v_b_s', 'v_w_proj_b', 'v_w_out', 'v_g_ffn', 'v_w_up', 'v_conv_w', 'v_conv_b', 'v_w_down', 'v_g_final']
TWIN_OUTPUTS = ['loss', 'grad_x', 'grad_g_mix', 'grad_w_in', 'grad_a_re', 'grad_a_im', 'grad_log_dt', 'grad_b_re', 'grad_b_im', 'grad_c_re', 'grad_c_im', 'grad_d_skip', 'grad_w_glu', 'grad_b_glu', 'grad_w_proj_a', 'grad_g_sgu', 'grad_w_s', 'grad_b_s', 'grad_w_proj_b', 'grad_w_out', 'grad_g_ffn', 'grad_w_up', 'grad_conv_w', 'grad_conv_b', 'grad_w_down', 'grad_g_final', 'delta_g_mix', 'delta_w_in', 'delta_a_re', 'delta_a_im', 'delta_log_dt', 'delta_b_re', 'delta_b_im', 'delta_c_re', 'delta_c_im', 'delta_d_skip', 'delta_w_glu', 'delta_b_glu', 'delta_w_proj_a', 'delta_g_sgu', 'delta_w_s', 'delta_b_s', 'delta_w_proj_b', 'delta_w_out', 'delta_g_ffn', 'delta_w_up', 'delta_conv_w', 'delta_conv_b', 'delta_w_down', 'delta_g_final', 'new_m_g_mix', 'new_m_w_in', 'new_m_a_re', 'new_m_a_im', 'new_m_log_dt', 'new_m_b_re', 'new_m_b_im', 'new_m_c_re', 'new_m_c_im', 'new_m_d_skip', 'new_m_w_glu', 'new_m_b_glu', 'new_m_w_proj_a', 'new_m_g_sgu', 'new_m_w_s', 'new_m_b_s', 'new_m_w_proj_b', 'new_m_w_out', 'new_m_g_ffn', 'new_m_w_up', 'new_m_conv_w', 'new_m_conv_b', 'new_m_w_down', 'new_m_g_final', 'new_v_g_mix', 'new_v_w_in', 'new_v_a_re', 'new_v_a_im', 'new_v_log_dt', 'new_v_b_re', 'new_v_b_im', 'new_v_c_re', 'new_v_c_im', 'new_v_d_skip', 'new_v_w_glu', 'new_v_b_glu', 'new_v_w_proj_a', 'new_v_g_sgu', 'new_v_w_s', 'new_v_b_s', 'new_v_w_proj_b', 'new_v_w_out', 'new_v_g_ffn', 'new_v_w_up', 'new_v_conv_w', 'new_v_conv_b', 'new_v_w_down', 'new_v_g_final']
TWIN_LEAF_KINDS = {'loss': 'loss', 'grad_x': 'grad_x', 'grad_g_mix': 'grad_w', 'grad_w_in': 'grad_w', 'grad_a_re': 'grad_w', 'grad_a_im': 'grad_w', 'grad_log_dt': 'grad_w', 'grad_b_re': 'grad_w', 'grad_b_im': 'grad_w', 'grad_c_re': 'grad_w', 'grad_c_im': 'grad_w', 'grad_d_skip': 'grad_w', 'grad_w_glu': 'grad_w', 'grad_b_glu': 'grad_w', 'grad_w_proj_a': 'grad_w', 'grad_g_sgu': 'grad_w', 'grad_w_s': 'grad_w', 'grad_b_s': 'grad_w', 'grad_w_proj_b': 'grad_w', 'grad_w_out': 'grad_w', 'grad_g_ffn': 'grad_w', 'grad_w_up': 'grad_w', 'grad_conv_w': 'grad_w', 'grad_conv_b': 'grad_w', 'grad_w_down': 'grad_w', 'grad_g_final': 'grad_w', 'delta_g_mix': 'delta_w', 'delta_w_in': 'delta_w', 'delta_a_re': 'delta_w', 'delta_a_im': 'delta_w', 'delta_log_dt': 'delta_w', 'delta_b_re': 'delta_w', 'delta_b_im': 'delta_w', 'delta_c_re': 'delta_w', 'delta_c_im': 'delta_w', 'delta_d_skip': 'delta_w', 'delta_w_glu': 'delta_w', 'delta_b_glu': 'delta_w', 'delta_w_proj_a': 'delta_w', 'delta_g_sgu': 'delta_w', 'delta_w_s': 'delta_w', 'delta_b_s': 'delta_w', 'delta_w_proj_b': 'delta_w', 'delta_w_out': 'delta_w', 'delta_g_ffn': 'delta_w', 'delta_w_up': 'delta_w', 'delta_conv_w': 'delta_w', 'delta_conv_b': 'delta_w', 'delta_w_down': 'delta_w', 'delta_g_final': 'delta_w', 'new_m_g_mix': 'new_m', 'new_m_w_in': 'new_m', 'new_m_a_re': 'new_m', 'new_m_a_im': 'new_m', 'new_m_log_dt': 'new_m', 'new_m_b_re': 'new_m', 'new_m_b_im': 'new_m', 'new_m_c_re': 'new_m', 'new_m_c_im': 'new_m', 'new_m_d_skip': 'new_m', 'new_m_w_glu': 'new_m', 'new_m_b_glu': 'new_m', 'new_m_w_proj_a': 'new_m', 'new_m_g_sgu': 'new_m', 'new_m_w_s': 'new_m', 'new_m_b_s': 'new_m', 'new_m_w_proj_b': 'new_m', 'new_m_w_out': 'new_m', 'new_m_g_ffn': 'new_m', 'new_m_w_up': 'new_m', 'new_m_conv_w': 'new_m', 'new_m_conv_b': 'new_m', 'new_m_w_down': 'new_m', 'new_m_g_final': 'new_m', 'new_v_g_mix': 'new_v', 'new_v_w_in': 'new_v', 'new_v_a_re': 'new_v', 'new_v_a_im': 'new_v', 'new_v_log_dt': 'new_v', 'new_v_b_re': 'new_v', 'new_v_b_im': 'new_v', 'new_v_c_re': 'new_v', 'new_v_c_im': 'new_v', 'new_v_d_skip': 'new_v', 'new_v_w_glu': 'new_v', 'new_v_b_glu': 'new_v', 'new_v_w_proj_a': 'new_v', 'new_v_g_sgu': 'new_v', 'new_v_w_s': 'new_v', 'new_v_b_s': 'new_v', 'new_v_w_proj_b': 'new_v', 'new_v_w_out': 'new_v', 'new_v_g_ffn': 'new_v', 'new_v_w_up': 'new_v', 'new_v_conv_w': 'new_v', 'new_v_conv_b': 'new_v', 'new_v_w_down': 'new_v', 'new_v_g_final': 'new_v'}


def _forward(args):
    return _fwd_reference(*[args[k] for k in FWD_PARAMS])


def _output_shape():
    out = _jax.eval_shape(lambda: _forward(_fwd_setup_inputs(0)))
    return out.shape, out.dtype

N_MICROBATCH = 1
ADAM_LR = 0.001
ADAM_B1 = 0.9
ADAM_B2 = 0.999
ADAM_EPS = 1e-08
ADAM_WD = 0.01
ADAM_STEP = 10
PER_EXAMPLE_BATCH_AXIS = {'x': 0, 'loss_target': 0}
SHARED_INPUTS = []
_WEIGHT_DTYPES = {'g_mix': _jnp.float32, 'w_in': _jnp.float32, 'a_re': _jnp.float32, 'a_im': _jnp.float32, 'log_dt': _jnp.float32, 'b_re': _jnp.float32, 'b_im': _jnp.float32, 'c_re': _jnp.float32, 'c_im': _jnp.float32, 'd_skip': _jnp.float32, 'w_glu': _jnp.float32, 'b_glu': _jnp.float32, 'w_proj_a': _jnp.float32, 'g_sgu': _jnp.float32, 'w_s': _jnp.float32, 'b_s': _jnp.float32, 'w_proj_b': _jnp.float32, 'w_out': _jnp.float32, 'g_ffn': _jnp.float32, 'w_up': _jnp.float32, 'conv_w': _jnp.float32, 'conv_b': _jnp.float32, 'w_down': _jnp.float32, 'g_final': _jnp.float32}
MOMENT_SCALE = {'g_mix': 1.104619e-01, 'w_in': 5.617159e-02, 'a_re': 4.260433e-03, 'a_im': 3.517866e-03, 'log_dt': 3.772538e+00, 'b_re': 2.486628e-03, 'b_im': 2.500358e-03, 'c_re': 3.494411e-03, 'c_im': 3.611753e-03, 'd_skip': 5.156074e-02, 'w_glu': 1.567616e-02, 'b_glu': 2.360514e-02, 'w_proj_a': 3.653923e-02, 'g_sgu': 6.459871e-02, 'w_s': 4.480098e-02, 'b_s': 6.094333e-02, 'w_proj_b': 7.904035e-02, 'w_out': 8.561094e-02, 'g_ffn': 1.309672e-01, 'w_up': 5.318507e-02, 'conv_w': 5.242691e-02, 'conv_b': 5.301094e-02, 'w_down': 8.709321e-02, 'g_final': 3.196341e+01}


def _to_microbatches(a, axis):
    t = _jnp.moveaxis(a, axis, 0)
    t = t.reshape((N_MICROBATCH, t.shape[0] // N_MICROBATCH) + t.shape[1:])
    return _jnp.moveaxis(t, 1, axis + 1)


def setup_inputs(seed: int = 0) -> dict:
    inp = _fwd_setup_inputs(seed)
    key = _jax.random.fold_in(_jax.random.key(seed), 7919)
    shape, _ = _output_shape()
    out = dict(inp)
    out["loss_target"] = _jax.random.normal(_jax.random.fold_in(key, 0), shape, _jnp.float32)
    for i, name in enumerate(TWIN_WEIGHTS):
        w = inp[name].astype(_jnp.float32)
        if MOMENT_SCALE is None:
            s = _jnp.sqrt(_jnp.mean(_jnp.square(w)) + 1e-30)
        else:
            s = MOMENT_SCALE[name]
        km, kv = _jax.random.split(_jax.random.fold_in(key, i + 1))
        out[name] = w
        out["m_" + name] = s * _jax.random.normal(km, w.shape, _jnp.float32)
        out["v_" + name] = (s * s) * _jax.random.uniform(kv, w.shape, _jnp.float32, 0.5, 1.5)
    if N_MICROBATCH > 1:
        for name, axis in PER_EXAMPLE_BATCH_AXIS.items():
            out[name] = _to_microbatches(out[name], axis)
    return {'x': out['x'], 'g_mix': out['g_mix'], 'w_in': out['w_in'], 'a_re': out['a_re'], 'a_im': out['a_im'], 'log_dt': out['log_dt'], 'b_re': out['b_re'], 'b_im': out['b_im'], 'c_re': out['c_re'], 'c_im': out['c_im'], 'd_skip': out['d_skip'], 'w_glu': out['w_glu'], 'b_glu': out['b_glu'], 'w_proj_a': out['w_proj_a'], 'g_sgu': out['g_sgu'], 'w_s': out['w_s'], 'b_s': out['b_s'], 'w_proj_b': out['w_proj_b'], 'w_out': out['w_out'], 'g_ffn': out['g_ffn'], 'w_up': out['w_up'], 'conv_w': out['conv_w'], 'conv_b': out['conv_b'], 'w_down': out['w_down'], 'g_final': out['g_final'], 'loss_target': out['loss_target'], 'm_g_mix': out['m_g_mix'], 'm_w_in': out['m_w_in'], 'm_a_re': out['m_a_re'], 'm_a_im': out['m_a_im'], 'm_log_dt': out['m_log_dt'], 'm_b_re': out['m_b_re'], 'm_b_im': out['m_b_im'], 'm_c_re': out['m_c_re'], 'm_c_im': out['m_c_im'], 'm_d_skip': out['m_d_skip'], 'm_w_glu': out['m_w_glu'], 'm_b_glu': out['m_b_glu'], 'm_w_proj_a': out['m_w_proj_a'], 'm_g_sgu': out['m_g_sgu'], 'm_w_s': out['m_w_s'], 'm_b_s': out['m_b_s'], 'm_w_proj_b': out['m_w_proj_b'], 'm_w_out': out['m_w_out'], 'm_g_ffn': out['m_g_ffn'], 'm_w_up': out['m_w_up'], 'm_conv_w': out['m_conv_w'], 'm_conv_b': out['m_conv_b'], 'm_w_down': out['m_w_down'], 'm_g_final': out['m_g_final'], 'v_g_mix': out['v_g_mix'], 'v_w_in': out['v_w_in'], 'v_a_re': out['v_a_re'], 'v_a_im': out['v_a_im'], 'v_log_dt': out['v_log_dt'], 'v_b_re': out['v_b_re'], 'v_b_im': out['v_b_im'], 'v_c_re': out['v_c_re'], 'v_c_im': out['v_c_im'], 'v_d_skip': out['v_d_skip'], 'v_w_glu': out['v_w_glu'], 'v_b_glu': out['v_b_glu'], 'v_w_proj_a': out['v_w_proj_a'], 'v_g_sgu': out['v_g_sgu'], 'v_w_s': out['v_w_s'], 'v_b_s': out['v_b_s'], 'v_w_proj_b': out['v_w_proj_b'], 'v_w_out': out['v_w_out'], 'v_g_ffn': out['v_g_ffn'], 'v_w_up': out['v_w_up'], 'v_conv_w': out['v_conv_w'], 'v_conv_b': out['v_conv_b'], 'v_w_down': out['v_w_down'], 'v_g_final': out['v_g_final']}


def _loss(weights, diff, rest, loss_target):
    with _jax.named_scope("forward"):
        args = {**rest, TWIN_DIFF_INPUT: diff, **{k: w.astype(_WEIGHT_DTYPES[k]) for k, w in weights.items()}}
        y = _forward(args)
    with _jax.named_scope("loss_head"):
        err = _jnp.square(y.astype(_jnp.float32) - loss_target)
        return 0.5 * _jnp.sum(_jnp.mean(err, axis=-1)) if err.ndim else 0.5 * err


def _adamw(w, g, m, v):
    m = ADAM_B1 * m + (1.0 - ADAM_B1) * g
    v = ADAM_B2 * v + (1.0 - ADAM_B2) * _jnp.square(g)
    m_hat = m / (1.0 - ADAM_B1 ** ADAM_STEP)
    v_hat = v / (1.0 - ADAM_B2 ** ADAM_STEP)
    delta = -ADAM_LR * (m_hat / (_jnp.sqrt(v_hat) + ADAM_EPS) + ADAM_WD * w)
    return delta, m, v


def reference(x, g_mix, w_in, a_re, a_im, log_dt, b_re, b_im, c_re, c_im, d_skip, w_glu, b_glu, w_proj_a, g_sgu, w_s, b_s, w_proj_b, w_out, g_ffn, w_up, conv_w, conv_b, w_down, g_final, loss_target, m_g_mix, m_w_in, m_a_re, m_a_im, m_log_dt, m_b_re, m_b_im, m_c_re, m_c_im, m_d_skip, m_w_glu, m_b_glu, m_w_proj_a, m_g_sgu, m_w_s, m_b_s, m_w_proj_b, m_w_out, m_g_ffn, m_w_up, m_conv_w, m_conv_b, m_w_down, m_g_final, v_g_mix, v_w_in, v_a_re, v_a_im, v_log_dt, v_b_re, v_b_im, v_c_re, v_c_im, v_d_skip, v_w_glu, v_b_glu, v_w_proj_a, v_g_sgu, v_w_s, v_b_s, v_w_proj_b, v_w_out, v_g_ffn, v_w_up, v_conv_w, v_conv_b, v_w_down, v_g_final):
    given = dict(x=x, g_mix=g_mix, w_in=w_in, a_re=a_re, a_im=a_im, log_dt=log_dt, b_re=b_re, b_im=b_im, c_re=c_re, c_im=c_im, d_skip=d_skip, w_glu=w_glu, b_glu=b_glu, w_proj_a=w_proj_a, g_sgu=g_sgu, w_s=w_s, b_s=b_s, w_proj_b=w_proj_b, w_out=w_out, g_ffn=g_ffn, w_up=w_up, conv_w=conv_w, conv_b=conv_b, w_down=w_down, g_final=g_final, loss_target=loss_target, m_g_mix=m_g_mix, m_w_in=m_w_in, m_a_re=m_a_re, m_a_im=m_a_im, m_log_dt=m_log_dt, m_b_re=m_b_re, m_b_im=m_b_im, m_c_re=m_c_re, m_c_im=m_c_im, m_d_skip=m_d_skip, m_w_glu=m_w_glu, m_b_glu=m_b_glu, m_w_proj_a=m_w_proj_a, m_g_sgu=m_g_sgu, m_w_s=m_w_s, m_b_s=m_b_s, m_w_proj_b=m_w_proj_b, m_w_out=m_w_out, m_g_ffn=m_g_ffn, m_w_up=m_w_up, m_conv_w=m_conv_w, m_conv_b=m_conv_b, m_w_down=m_w_down, m_g_final=m_g_final, v_g_mix=v_g_mix, v_w_in=v_w_in, v_a_re=v_a_re, v_a_im=v_a_im, v_log_dt=v_log_dt, v_b_re=v_b_re, v_b_im=v_b_im, v_c_re=v_c_re, v_c_im=v_c_im, v_d_skip=v_d_skip, v_w_glu=v_w_glu, v_b_glu=v_b_glu, v_w_proj_a=v_w_proj_a, v_g_sgu=v_g_sgu, v_w_s=v_w_s, v_b_s=v_b_s, v_w_proj_b=v_w_proj_b, v_w_out=v_w_out, v_g_ffn=v_g_ffn, v_w_up=v_w_up, v_conv_w=v_conv_w, v_conv_b=v_conv_b, v_w_down=v_w_down, v_g_final=v_g_final)
    weights = {n: given[n] for n in TWIN_WEIGHTS}
    shared = {n: given[n] for n in SHARED_INPUTS}
    per_example = {n: given[n] for n in ['x']}
    grad_fn = _jax.value_and_grad(_loss, argnums=(0, 1))

    def one_microbatch(ex, loss_target):
        ex = dict(ex)
        diff = ex.pop(TWIN_DIFF_INPUT)
        return grad_fn(weights, diff, {**shared, **ex}, loss_target)

    if N_MICROBATCH == 1:
        loss, (grad_w, grad_x) = one_microbatch(per_example, given["loss_target"])
    else:
        def body(carry, xs):
            loss_sum, grad_sum = carry
            l_k, (gw_k, gx_k) = one_microbatch(xs[0], xs[1])
            with _jax.named_scope("update"):
                return (loss_sum + l_k, _jax.tree.map(_jnp.add, grad_sum, gw_k)), gx_k

        init = (_jnp.zeros((), _jnp.float32), _jax.tree.map(_jnp.zeros_like, weights))
        (loss, grad_w), grad_x = _jax.lax.scan(body, init, (per_example, given["loss_target"]))
    with _jax.named_scope("update"):
        delta_w, new_m, new_v = {}, {}, {}
        for n in TWIN_WEIGHTS:
            delta_w[n], new_m[n], new_v[n] = _adamw(weights[n], grad_w[n], given["m_" + n], given["v_" + n])
    return (loss, grad_x, *[grad_w[n] for n in TWIN_WEIGHTS], *[delta_w[n] for n in TWIN_WEIGHTS],
            *[new_m[n] for n in TWIN_WEIGHTS], *[new_v[n] for n in TWIN_WEIGHTS])
```

```python
import functools
import math

import jax
import jax.numpy as jnp
from jax import lax
from jax.experimental import pallas as pl
from jax.experimental.pallas import tpu as pltpu

F32 = jnp.float32
BF16 = jnp.bfloat16
MESH = pl.DeviceIdType.MESH

D_MODEL = 1024
SSM_W = 512
SSM_G = 32
SSM_H = 16
SSM_P = 64
N_STATE = SSM_G * SSM_P
SGU_W = 512
SGU_G = 8
SGU_D = 64
CHUNK = 128
D_FF = 2816
IN_COLS = 3584
EPS = 1e-6
N_CHIP = 4

ADAM_LR = 0.001
ADAM_B1 = 0.9
ADAM_B2 = 0.999
ADAM_EPS = 1e-08
ADAM_WD = 0.01
ADAM_STEP = 10

SUBLANE = 8
LANE = 128
VMEM_LIMIT = 56 * 1024 * 1024
TB = 256
SCAN_LANES = 256
HALO = SUBLANE

BIG = ("w_in", "w_up", "w_down", "w_out", "w_proj_a", "w_proj_b", "w_glu")
SMALL = ("g_mix", "a_re", "a_im", "log_dt", "b_re", "b_im", "c_re", "c_im", "d_skip", "b_glu",
         "g_sgu", "w_s", "b_s", "g_ffn", "conv_b", "g_final")
WEIGHTS = ("g_mix", "w_in", "a_re", "a_im", "log_dt", "b_re", "b_im", "c_re", "c_im", "d_skip",
           "w_glu", "b_glu", "w_proj_a", "g_sgu", "w_s", "b_s", "w_proj_b", "w_out", "g_ffn",
           "w_up", "conv_w", "conv_b", "w_down", "g_final")


def _params(*sem):
    return pltpu.CompilerParams(dimension_semantics=sem if sem else None,
                                vmem_limit_bytes=VMEM_LIMIT)


def _whole():
    return pl.BlockSpec(memory_space=pltpu.VMEM)


def _rows(tb, ncol):
    return pl.BlockSpec((tb, ncol), lambda i: (i, 0))


def _acc(nrow, ncol):
    return pl.BlockSpec((nrow, ncol), lambda i: (0, 0))


def _dot(a, b):
    return jnp.dot(a.astype(BF16), b.astype(BF16), preferred_element_type=F32)


def _dot_nt(a, b):
    return lax.dot_general(a.astype(BF16), b.astype(BF16), (((1,), (1,)), ((), ())),
                           preferred_element_type=F32)


def _sigmoid(v):
    return 1.0 / (1.0 + jnp.exp(-v))


_GELU_C = math.sqrt(2.0 / math.pi)


def _gelu(v):
    return 0.5 * v * (1.0 + jnp.tanh(_GELU_C * (v + 0.044715 * v * v * v)))


def _gelu_grad(v):
    t = jnp.tanh(_GELU_C * (v + 0.044715 * v * v * v))
    return 0.5 * (1.0 + t) + 0.5 * v * (1.0 - t * t) * _GELU_C * (1.0 + 3.0 * 0.044715 * v * v)


def _rms_stats(v):
    r = lax.rsqrt(jnp.mean(v * v, axis=-1, keepdims=True) + EPS)
    return r, v * r


def _rms_bwd(dxh, xh, r):
    return r * (dxh - xh * jnp.mean(dxh * xh, axis=-1, keepdims=True))


def _fwd_in(x, g_mix, w_in, bre, bim):
    t_len = x.shape[0]
    cs = IN_COLS // N_CHIP

    def body(x_ref, g_ref, w_ref, bre_ref, bim_ref, p_ref, h_ref, bur_ref, bui_ref):
        xv = x_ref[...]
        r, xh = _rms_stats(xv)
        h = (xh * g_ref[...]).astype(BF16)
        h_ref[...] = h
        for k in range(N_CHIP):
            p_ref[:, k * cs:(k + 1) * cs] = jnp.dot(h, w_ref[k], preferred_element_type=F32)
        u = p_ref[:, 0:SSM_W].astype(BF16)
        bur_ref[...] = jnp.dot(u, bre_ref[...], preferred_element_type=F32)
        bui_ref[...] = jnp.dot(u, bim_ref[...], preferred_element_type=F32)

    return pl.pallas_call(
        body, name="fwd_in", grid=(t_len // TB,),
        in_specs=[_rows(TB, D_MODEL), _whole(), _whole(), _whole(), _whole()],
        out_specs=[_rows(TB, IN_COLS), _rows(TB, D_MODEL), _rows(TB, N_STATE), _rows(TB, N_STATE)],
        out_shape=[jax.ShapeDtypeStruct((t_len, IN_COLS), F32),
                   jax.ShapeDtypeStruct((t_len, D_MODEL), BF16),
                   jax.ShapeDtypeStruct((t_len, N_STATE), F32),
                   jax.ShapeDtypeStruct((t_len, N_STATE), F32)],
        compiler_params=_params("arbitrary"),
    )(x, g_mix, w_in, bre, bim)


def _scan_block(xr, xi, tab, shifts, cr, ci):
    for q, s in enumerate(shifts):
        ar, ai = tab[2 * q], tab[2 * q + 1]
        rr = pltpu.roll(xr, s, 0)
        ri = pltpu.roll(xi, s, 0)
        xr, xi = xr + ar * rr - ai * ri, xi + ar * ri + ai * rr
    pr, pi = tab[6], tab[7]
    return xr + pr * cr - pi * ci, xi + pr * ci + pi * cr


def _scan_fwd(bur, bui, tab):
    t_len = bur.shape[0]
    nblk = t_len // SUBLANE
    lb = SCAN_LANES

    def body(br_ref, bi_ref, tab_ref, sr_ref, si_ref):
        tab_v = [tab_ref[q] for q in range(8)]

        def step(k, carry):
            cr, ci = carry
            r0 = pl.multiple_of(k * SUBLANE, SUBLANE)
            xr, xi = _scan_block(br_ref[pl.ds(r0, SUBLANE), :], bi_ref[pl.ds(r0, SUBLANE), :],
                                 tab_v, (1, 2, 4), cr, ci)
            sr_ref[pl.ds(r0, SUBLANE), :] = xr
            si_ref[pl.ds(r0, SUBLANE), :] = xi
            return xr[SUBLANE - 1:SUBLANE, :], xi[SUBLANE - 1:SUBLANE, :]

        zero = jnp.zeros((1, lb), F32)
        lax.fori_loop(0, nblk, step, (zero, zero), unroll=2)

    col = pl.BlockSpec((t_len, lb), lambda j: (0, j))
    return pl.pallas_call(
        body, name="scan_fwd", grid=(N_STATE // lb,),
        in_specs=[col, col, pl.BlockSpec((8, SUBLANE, lb), lambda j: (0, 0, j))],
        out_specs=[col, col],
        out_shape=[jax.ShapeDtypeStruct((t_len, N_STATE), F32)] * 2,
        compiler_params=_params("arbitrary"),
    )(bur, bui, tab)


def _sgu_mix(v, ws_ref, lane_lo):
    rows = []
    for c0 in range(0, v.shape[0], CHUNK):
        slabs = []
        for j in range(SGU_W // LANE):
            prod = jnp.dot(ws_ref[j], v[c0:c0 + CHUNK, j * LANE:(j + 1) * LANE].astype(BF16),
                           preferred_element_type=F32)
            slabs.append(jnp.where(lane_lo, prod[:CHUNK], prod[CHUNK:]))
        rows.append(jnp.concatenate(slabs, axis=1))
    return jnp.concatenate(rows, axis=0) if len(rows) > 1 else rows[0]


def _fwd_mix(x, p, str_, sti, cre, cim, d_skip, w_glu, b_glu, w_pa, g_sgu, ws_st, bmat, w_pb, w_out):
    t_len = x.shape[0]
    pc = D_MODEL // N_CHIP

    def body(x_ref, p_ref, sr_ref, si_ref, cre_ref, cim_ref, dsk_ref, wg_ref, bg_ref, wpa_ref,
             gs_ref, ws_ref, bm_ref, wpb_ref, wo_ref,
             x2_ref, y0_ref, z_ref, mx_ref, ya_ref, yb_ref):
        u = p_ref[:, 0:SSM_W]
        y0 = _dot(sr_ref[...], cre_ref[...]) - _dot(si_ref[...], cim_ref[...]) + dsk_ref[...] * u
        y0_ref[...] = y0
        y1 = _gelu(y0)
        z = _dot(y1, wg_ref[...]) + bg_ref[...]
        z_ref[...] = z
        ya_pre = (y1 * _sigmoid(z)).astype(BF16)
        for k in range(N_CHIP):
            ya_ref[:, k * pc:(k + 1) * pc] = jnp.dot(ya_pre, wpa_ref[k], preferred_element_type=F32)

        uvg = _gelu(p_ref[:, SSM_W:SSM_W + 2 * SGU_W])
        u2 = uvg[:, :SGU_W]
        _, vh = _rms_stats(uvg[:, SGU_W:])
        v3 = vh * gs_ref[...]
        lane_lo = lax.broadcasted_iota(jnp.int32, (CHUNK, LANE), 1) < SGU_D
        bias = jnp.concatenate([bm_ref[...]] * (TB // CHUNK), axis=0)
        mixed = _sgu_mix(v3, ws_ref, lane_lo) + bias
        mx_ref[...] = mixed
        sgu = (u2 * mixed).astype(BF16)
        for k in range(N_CHIP):
            yb_ref[:, k * pc:(k + 1) * pc] = jnp.dot(sgu, wpb_ref[k], preferred_element_type=F32)

        lg0 = SSM_W + 2 * SGU_W
        ga = _sigmoid(p_ref[:, lg0:lg0 + D_MODEL])
        gb = _sigmoid(p_ref[:, lg0 + D_MODEL:lg0 + 2 * D_MODEL])
        mrg = ga * ya_ref[...] + gb * yb_ref[...]
        x2_ref[...] = x_ref[...] + _dot(mrg, wo_ref[...])

    n_w = 11
    return pl.pallas_call(
        body, name="fwd_mix", grid=(t_len // TB,),
        in_specs=[_rows(TB, D_MODEL), _rows(TB, IN_COLS), _rows(TB, N_STATE), _rows(TB, N_STATE)]
        + [_whole()] * n_w,
        out_specs=[_rows(TB, D_MODEL), _rows(TB, SSM_W), _rows(TB, SSM_W), _rows(TB, SGU_W),
                   _rows(TB, D_MODEL), _rows(TB, D_MODEL)],
        out_shape=[jax.ShapeDtypeStruct((t_len, D_MODEL), F32),
                   jax.ShapeDtypeStruct((t_len, SSM_W), F32),
                   jax.ShapeDtypeStruct((t_len, SSM_W), F32),
                   jax.ShapeDtypeStruct((t_len, SGU_W), F32),
                   jax.ShapeDtypeStruct((t_len, D_MODEL), F32),
                   jax.ShapeDtypeStruct((t_len, D_MODEL), F32)],
        compiler_params=_params("arbitrary"),
    )(x, p, str_, sti, cre, cim, d_skip, w_glu, b_glu, w_pa, g_sgu, ws_st, bmat, w_pb, w_out)


def _conv_taps(v, cw_ref, c0, width):
    w0 = cw_ref[0:1, c0:c0 + width]
    w1 = cw_ref[1:2, c0:c0 + width]
    w2 = cw_ref[2:3, c0:c0 + width]
    return w0 * pltpu.roll(v, 2, 0) + w1 * pltpu.roll(v, 1, 0) + w2 * v


def _fwd_ffn(x2, target, g_ffn, w_up, conv_w, conv_b, w_down, g_final):
    t_len = x2.shape[0]
    half = D_FF // 2
    blocks_per_halo = TB // HALO

    def body(x2_ref, xp_ref, tg_ref, gf_ref, wu_ref, cw_ref, cb_ref, wd_ref, gl_ref,
             up_ref, f_ref, h2_ref, dx3_ref, sm_ref):
        i = pl.program_id(0)
        xe = jnp.concatenate([xp_ref[...], x2_ref[...]], axis=0)
        _, xh = _rms_stats(xe)
        h2 = (xh * gf_ref[...]).astype(BF16)
        h2_ref[...] = h2[HALO:]
        keep = jnp.where(jnp.logical_and(
            i == 0, lax.broadcasted_iota(jnp.int32, (TB + HALO, 1), 0) < HALO), 0.0, 1.0)
        acc = jnp.zeros((TB, D_MODEL), F32)
        for hc in range(2):
            ca = hc * half
            cb = D_FF + hc * half
            ua = jnp.dot(h2, wu_ref[hc], preferred_element_type=F32) * keep
            ub = jnp.dot(h2, wu_ref[2 + hc], preferred_element_type=F32) * keep
            up_ref[:, ca:ca + half] = ua[HALO:]
            up_ref[:, cb:cb + half] = ub[HALO:]
            ac = _conv_taps(ua, cw_ref, ca, half)[HALO:] + cb_ref[:, ca:ca + half]
            bc = _conv_taps(ub, cw_ref, cb, half)[HALO:] + cb_ref[:, cb:cb + half]
            f = (ac * _sigmoid(ac) * bc).astype(BF16)
            f_ref[:, ca:ca + half] = f
            acc = acc + jnp.dot(f, wd_ref[ca:ca + half, :], preferred_element_type=F32)
        x3 = x2_ref[...] + acc
        r3, xh3 = _rms_stats(x3)
        err = xh3 * gl_ref[...] - tg_ref[...]
        dout = err * (1.0 / D_MODEL)
        dx3_ref[...] = _rms_bwd(dout * gl_ref[...], xh3, r3)
        dgl = jnp.sum(dout * xh3, axis=0, keepdims=True)
        loss = 0.5 * jnp.sum(jnp.mean(err * err, axis=-1, keepdims=True), axis=0, keepdims=True)
        upd = jnp.concatenate([dgl, jnp.broadcast_to(loss, (1, D_MODEL)),
                               jnp.zeros((SUBLANE - 2, D_MODEL), F32)], axis=0)

        @pl.when(i == 0)
        def _():
            sm_ref[...] = upd

        @pl.when(i > 0)
        def _():
            sm_ref[...] += upd

    prev = pl.BlockSpec((HALO, D_MODEL), lambda i: (jnp.maximum(i * blocks_per_halo - 1, 0), 0))
    return pl.pallas_call(
        body, name="fwd_ffn", grid=(t_len // TB,),
        in_specs=[_rows(TB, D_MODEL), prev, _rows(TB, D_MODEL)] + [_whole()] * 6,
        out_specs=[_rows(TB, 2 * D_FF), _rows(TB, D_FF), _rows(TB, D_MODEL), _rows(TB, D_MODEL),
                   _acc(SUBLANE, D_MODEL)],
        out_shape=[jax.ShapeDtypeStruct((t_len, 2 * D_FF), F32),
                   jax.ShapeDtypeStruct((t_len, D_FF), BF16),
                   jax.ShapeDtypeStruct((t_len, D_MODEL), BF16),
                   jax.ShapeDtypeStruct((t_len, D_MODEL), F32),
                   jax.ShapeDtypeStruct((SUBLANE, D_MODEL), F32)],
        compiler_params=_params("arbitrary"),
    )(x2, x2, target, g_ffn, w_up, conv_w, conv_b, w_down, g_final)


def _bwd_ffn(dx3, up, x2, g_ffn, w_up, conv_w, conv_b, w_down):
    t_len = x2.shape[0]
    half = D_FF // 2
    nblk = t_len // TB
    bph = TB // HALO
    n_halo = t_len // HALO

    def body(dx_ref, dxn_ref, up_ref, upp_ref, upn_ref, x2_ref, gf_ref, wu_ref, cw_ref, cb_ref,
             wd_ref, dx2_ref, dup_ref, smw_ref, smg_ref):
        i = pl.program_id(0)
        rows_e = lax.broadcasted_iota(jnp.int32, (TB + 2 * HALO, 1), 0)
        keep_e = jnp.where(jnp.logical_and(i == 0, rows_e < HALO), 0.0, 1.0)
        rows_d = lax.broadcasted_iota(jnp.int32, (TB + HALO, 1), 0)
        keep_d = jnp.where(jnp.logical_and(i == nblk - 1, rows_d >= TB), 0.0, 1.0)
        dxe = jnp.concatenate([dx_ref[...], dxn_ref[...]], axis=0).astype(BF16)
        dh2 = jnp.zeros((TB, D_MODEL), F32)
        zpad = jnp.zeros((1, half), F32)
        for hc in range(2):
            ca = hc * half
            cb = D_FF + hc * half
            uea = jnp.concatenate([upp_ref[:, ca:ca + half], up_ref[:, ca:ca + half],
                                   upn_ref[:, ca:ca + half]], axis=0) * keep_e
            ueb = jnp.concatenate([upp_ref[:, cb:cb + half], up_ref[:, cb:cb + half],
                                   upn_ref[:, cb:cb + half]], axis=0) * keep_e
            ua1, ua2 = pltpu.roll(uea, 1, 0), pltpu.roll(uea, 2, 0)
            ub1, ub2 = pltpu.roll(ueb, 1, 0), pltpu.roll(ueb, 2, 0)
            wa = [cw_ref[k:k + 1, ca:ca + half] for k in range(3)]
            wb = [cw_ref[k:k + 1, cb:cb + half] for k in range(3)]
            ac = (wa[0] * ua2 + wa[1] * ua1 + wa[2] * uea)[HALO:] + cb_ref[:, ca:ca + half]
            bc = (wb[0] * ub2 + wb[1] * ub1 + wb[2] * ueb)[HALO:] + cb_ref[:, cb:cb + half]
            df = lax.dot_general(dxe, wd_ref[ca:ca + half, :], (((1,), (1,)), ((), ())),
                                 preferred_element_type=F32)
            sg = _sigmoid(ac)
            da = df * bc * sg * (1.0 + ac * (1.0 - sg)) * keep_d
            db = df * ac * sg * keep_d
            n_e = TB + HALO
            dua = (wa[2] * da + wa[1] * pltpu.roll(da, n_e - 1, 0)
                   + wa[0] * pltpu.roll(da, n_e - 2, 0))[:TB]
            dub = (wb[2] * db + wb[1] * pltpu.roll(db, n_e - 1, 0)
                   + wb[0] * pltpu.roll(db, n_e - 2, 0))[:TB]
            dup_ref[:, ca:ca + half] = dua.astype(BF16)
            dup_ref[:, cb:cb + half] = dub.astype(BF16)
            dh2 = dh2 + _dot_nt(dua, wu_ref[hc]) + _dot_nt(dub, wu_ref[2 + hc])
            dab = da[:TB]
            dbb = db[:TB]
            rows = []
            for d_, u0, u1, u2 in ((dab, ua2, ua1, uea), (dbb, ub2, ub1, ueb)):
                rows.append([jnp.sum(d_ * u0[HALO:HALO + TB], axis=0, keepdims=True),
                             jnp.sum(d_ * u1[HALO:HALO + TB], axis=0, keepdims=True),
                             jnp.sum(d_ * u2[HALO:HALO + TB], axis=0, keepdims=True),
                             jnp.sum(d_, axis=0, keepdims=True)])
            for c0, rws in ((ca, rows[0]), (cb, rows[1])):
                upd = jnp.concatenate(rws + [zpad] * (SUBLANE - 4), axis=0)

                @pl.when(i == 0)
                def _(upd=upd, c0=c0):
                    smw_ref[:, c0:c0 + half] = upd

                @pl.when(i > 0)
                def _(upd=upd, c0=c0):
                    smw_ref[:, c0:c0 + half] += upd

        r2, xh2 = _rms_stats(x2_ref[...])
        dx2_ref[...] = dx_ref[...] + _rms_bwd(dh2 * gf_ref[...], xh2, r2)
        updg = jnp.concatenate([jnp.sum(dh2 * xh2, axis=0, keepdims=True),
                                jnp.zeros((SUBLANE - 1, D_MODEL), F32)], axis=0)

        @pl.when(i == 0)
        def _():
            smg_ref[...] = updg

        @pl.when(i > 0)
        def _():
            smg_ref[...] += updg

    nxt_d = pl.BlockSpec((HALO, D_MODEL), lambda i: (jnp.minimum((i + 1) * bph, n_halo - 1), 0))
    prv_u = pl.BlockSpec((HALO, 2 * D_FF), lambda i: (jnp.maximum(i * bph - 1, 0), 0))
    nxt_u = pl.BlockSpec((HALO, 2 * D_FF), lambda i: (jnp.minimum((i + 1) * bph, n_halo - 1), 0))
    return pl.pallas_call(
        body, name="bwd_ffn", grid=(nblk,),
        in_specs=[_rows(TB, D_MODEL), nxt_d, _rows(TB, 2 * D_FF), prv_u, nxt_u, _rows(TB, D_MODEL)]
        + [_whole()] * 5,
        out_specs=[_rows(TB, D_MODEL), _rows(TB, 2 * D_FF), _acc(SUBLANE, 2 * D_FF),
                   _acc(SUBLANE, D_MODEL)],
        out_shape=[jax.ShapeDtypeStruct((t_len, D_MODEL), F32),
                   jax.ShapeDtypeStruct((t_len, 2 * D_FF), BF16),
                   jax.ShapeDtypeStruct((SUBLANE, 2 * D_FF), F32),
                   jax.ShapeDtypeStruct((SUBLANE, D_MODEL), F32)],
        compiler_params=_params("arbitrary"),
    )(dx3, dx3, up, up, up, x2, g_ffn, w_up, conv_w, conv_b, w_down)


def _bwd_mix(dx2, p, y0, z, mixed, ya, yb, w_out, w_pa, w_pb, w_glu, cre, cim, ws_st, wst_st,
             d_skip, g_sgu):
    t_len = dx2.shape[0]
    pc = D_MODEL // N_CHIP
    n_slab = SGU_W // LANE

    def body(dx_ref, p_ref, y0_ref, z_ref, mx_ref, ya_ref, yb_ref, wo_ref, wpa_ref, wpb_ref,
             wg_ref, cre_ref, cim_ref, ws_ref, wst_ref, dsk_ref, gs_ref,
             dsr_ref, dsi_ref, du_ref, drest_ref, mrg_ref, dya_ref, dyb_ref, yap_ref, dz_ref,
             y1_ref, sgu_ref, dy0_ref, sm_ref, dbm_ref, dws_ref):
        i = pl.program_id(0)
        first = i == 0
        lg0 = SSM_W + 2 * SGU_W
        dmrg = _dot_nt(dx_ref[...], wo_ref[...])
        ga = _sigmoid(p_ref[:, lg0:lg0 + D_MODEL])
        gb = _sigmoid(p_ref[:, lg0 + D_MODEL:lg0 + 2 * D_MODEL])
        yav = ya_ref[...]
        ybv = yb_ref[...]
        mrg_ref[...] = (ga * yav + gb * ybv).astype(BF16)
        drest_ref[:, 2 * SGU_W:2 * SGU_W + D_MODEL] = (dmrg * yav * ga * (1.0 - ga)).astype(BF16)
        drest_ref[:, 2 * SGU_W + D_MODEL:] = (dmrg * ybv * gb * (1.0 - gb)).astype(BF16)
        dya = (dmrg * ga).astype(BF16)
        dyb = (dmrg * gb).astype(BF16)
        dya_ref[...] = dya
        dyb_ref[...] = dyb

        y0v = y0_ref[...]
        y1 = _gelu(y0v)
        sz = _sigmoid(z_ref[...])
        y1_ref[...] = y1.astype(BF16)
        yap_ref[...] = (y1 * sz).astype(BF16)
        dyap = jnp.zeros((TB, SSM_W), F32)
        for k in range(N_CHIP):
            dyap = dyap + _dot_nt(dya[:, k * pc:(k + 1) * pc], wpa_ref[k])
        dz = dyap * y1 * sz * (1.0 - sz)
        dz_ref[...] = dz.astype(BF16)
        dy0 = (dyap * sz + _dot_nt(dz, wg_ref[...])) * _gelu_grad(y0v)
        dy0_ref[...] = dy0.astype(BF16)
        u = p_ref[:, 0:SSM_W]
        du_ref[...] = dy0 * dsk_ref[...]
        dsr_ref[...] = _dot_nt(dy0, cre_ref[...])
        dsi_ref[...] = -_dot_nt(dy0, cim_ref[...])

        uv = p_ref[:, SSM_W:lg0]
        uvg = _gelu(uv)
        u2 = uvg[:, :SGU_W]
        rv, vh = _rms_stats(uvg[:, SGU_W:])
        v3 = vh * gs_ref[...]
        mixed = mx_ref[...]
        dsgu = jnp.zeros((TB, SGU_W), F32)
        for k in range(N_CHIP):
            dsgu = dsgu + _dot_nt(dyb[:, k * pc:(k + 1) * pc], wpb_ref[k])
        sgu_ref[...] = (u2 * mixed).astype(BF16)
        du2 = dsgu * mixed
        dmix = dsgu * u2
        lane_lo = lax.broadcasted_iota(jnp.int32, (CHUNK, LANE), 1) < SGU_D
        dv3 = _sgu_mix(dmix, wst_ref, lane_lo)
        dbm = jnp.zeros((CHUNK, SGU_W), F32)
        for c0 in range(0, TB, CHUNK):
            dbm = dbm + dmix[c0:c0 + CHUNK]
        for j in range(n_slab):
            lo = jnp.zeros((CHUNK, CHUNK), F32)
            hi = jnp.zeros((CHUNK, CHUNK), F32)
            for c0 in range(0, TB, CHUNK):
                dsl = dmix[c0:c0 + CHUNK, j * LANE:(j + 1) * LANE]
                vsl = v3[c0:c0 + CHUNK, j * LANE:(j + 1) * LANE]
                lo = lo + _dot_nt(jnp.where(lane_lo, dsl, 0.0), vsl)
                hi = hi + _dot_nt(jnp.where(lane_lo, 0.0, dsl), vsl)

            @pl.when(first)
            def _(lo=lo, hi=hi, j=j):
                dws_ref[2 * j] = lo
                dws_ref[2 * j + 1] = hi

            @pl.when(jnp.logical_not(first))
            def _(lo=lo, hi=hi, j=j):
                dws_ref[2 * j] += lo
                dws_ref[2 * j + 1] += hi

        dv2 = _rms_bwd(dv3 * gs_ref[...], vh, rv)
        gg = _gelu_grad(uv)
        drest_ref[:, 0:SGU_W] = (du2 * gg[:, :SGU_W]).astype(BF16)
        drest_ref[:, SGU_W:2 * SGU_W] = (dv2 * gg[:, SGU_W:]).astype(BF16)

        upd = jnp.concatenate([jnp.sum(dy0 * u, axis=0, keepdims=True),
                               jnp.sum(dz, axis=0, keepdims=True),
                               jnp.sum(dv3 * vh, axis=0, keepdims=True),
                               jnp.zeros((SUBLANE - 3, SSM_W), F32)], axis=0)

        @pl.when(first)
        def _():
            sm_ref[...] = upd
            dbm_ref[...] = dbm

        @pl.when(jnp.logical_not(first))
        def _():
            sm_ref[...] += upd
            dbm_ref[...] += dbm

    rest = 2 * SGU_W + 2 * D_MODEL
    return pl.pallas_call(
        body, name="bwd_mix", grid=(t_len // TB,),
        in_specs=[_rows(TB, D_MODEL), _rows(TB, IN_COLS), _rows(TB, SSM_W), _rows(TB, SSM_W),
                  _rows(TB, SGU_W), _rows(TB, D_MODEL), _rows(TB, D_MODEL)] + [_whole()] * 10,
        out_specs=[_rows(TB, N_STATE), _rows(TB, N_STATE), _rows(TB, SSM_W), _rows(TB, rest),
                   _rows(TB, D_MODEL), _rows(TB, D_MODEL), _rows(TB, D_MODEL), _rows(TB, SSM_W),
                   _rows(TB, SSM_W), _rows(TB, SSM_W), _rows(TB, SGU_W), _rows(TB, SSM_W),
                   _acc(SUBLANE, SSM_W), _acc(CHUNK, SGU_W),
                   pl.BlockSpec((SGU_G, CHUNK, CHUNK), lambda i: (0, 0, 0))],
        out_shape=[jax.ShapeDtypeStruct((t_len, N_STATE), F32),
                   jax.ShapeDtypeStruct((t_len, N_STATE), F32),
                   jax.ShapeDtypeStruct((t_len, SSM_W), F32),
                   jax.ShapeDtypeStruct((t_len, rest), BF16),
                   jax.ShapeDtypeStruct((t_len, D_MODEL), BF16),
                   jax.ShapeDtypeStruct((t_len, D_MODEL), BF16),
                   jax.ShapeDtypeStruct((t_len, D_MODEL), BF16),
                   jax.ShapeDtypeStruct((t_len, SSM_W), BF16),
                   jax.ShapeDtypeStruct((t_len, SSM_W), BF16),
                   jax.ShapeDtypeStruct((t_len, SSM_W), BF16),
                   jax.ShapeDtypeStruct((t_len, SGU_W), BF16),
                   jax.ShapeDtypeStruct((t_len, SSM_W), BF16),
                   jax.ShapeDtypeStruct((SUBLANE, SSM_W), F32),
                   jax.ShapeDtypeStruct((CHUNK, SGU_W), F32),
                   jax.ShapeDtypeStruct((SGU_G, CHUNK, CHUNK), F32)],
        compiler_params=_params("arbitrary"),
    )(dx2, p, y0, z, mixed, ya, yb, w_out, w_pa, w_pb, w_glu, cre, cim, ws_st, wst_st, d_skip, g_sgu)


def _scan_bwd(dsr, dsi, str_, sti, tab_rev):
    t_len = dsr.shape[0]
    nblk = t_len // SUBLANE
    lb = SCAN_LANES

    def body(dr_ref, di_ref, sr_ref, si_ref, tab_ref, lr_ref, li_ref, dar_ref, dai_ref):
        tab_v = [tab_ref[q] for q in range(8)]
        row0 = lax.broadcasted_iota(jnp.int32, (SUBLANE, lb), 0) == 0

        def step(k, carry):
            cr, ci, acr, aci = carry
            kk = nblk - 1 - k
            r0 = pl.multiple_of(kk * SUBLANE, SUBLANE)
            xr, xi = _scan_block(dr_ref[pl.ds(r0, SUBLANE), :], di_ref[pl.ds(r0, SUBLANE), :],
                                 tab_v, (7, 6, 4), cr, ci)
            lr_ref[pl.ds(r0, SUBLANE), :] = xr
            li_ref[pl.ds(r0, SUBLANE), :] = xi
            rp = pl.multiple_of(jnp.maximum(kk - 1, 0) * SUBLANE, SUBLANE)
            has_prev = jnp.where(kk > 0, 1.0, 0.0)
            pr = sr_ref[pl.ds(rp, SUBLANE), :][SUBLANE - 1:SUBLANE, :] * has_prev
            pi = si_ref[pl.ds(rp, SUBLANE), :][SUBLANE - 1:SUBLANE, :] * has_prev
            s_r = jnp.where(row0, pr, pltpu.roll(sr_ref[pl.ds(r0, SUBLANE), :], 1, 0))
            s_i = jnp.where(row0, pi, pltpu.roll(si_ref[pl.ds(r0, SUBLANE), :], 1, 0))
            acr = acr + xr * s_r + xi * s_i
            aci = aci + xi * s_r - xr * s_i
            return xr[0:1, :], xi[0:1, :], acr, aci

        zero = jnp.zeros((1, lb), F32)
        zacc = jnp.zeros((SUBLANE, lb), F32)
        _, _, acr, aci = lax.fori_loop(0, nblk, step, (zero, zero, zacc, zacc), unroll=2)
        dar_ref[...] = acr
        dai_ref[...] = aci

    col = pl.BlockSpec((t_len, lb), lambda j: (0, j))
    small = pl.BlockSpec((SUBLANE, lb), lambda j: (0, j))
    return pl.pallas_call(
        body, name="scan_bwd", grid=(N_STATE // lb,),
        in_specs=[col, col, col, col, pl.BlockSpec((8, SUBLANE, lb), lambda j: (0, 0, j))],
        out_specs=[col, col, small, small],
        out_shape=[jax.ShapeDtypeStruct((t_len, N_STATE), F32)] * 2
        + [jax.ShapeDtypeStruct((SUBLANE, N_STATE), F32)] * 2,
        compiler_params=_params("arbitrary"),
    )(dsr, dsi, str_, sti, tab_rev)


def _bwd_in(lam_r, lam_i, du_part, drest, x, dx2, g_mix, w_in, bre, bim):
    t_len = x.shape[0]
    cs = IN_COLS // N_CHIP

    def body(lr_ref, li_ref, du_ref, dr_ref, x_ref, dx2_ref, g_ref, w_ref, bre_ref, bim_ref,
             gx_ref, dp_ref, sm_ref):
        i = pl.program_id(0)
        du = du_ref[...] + _dot_nt(lr_ref[...], bre_ref[...]) + _dot_nt(li_ref[...], bim_ref[...])
        dp_ref[:, 0:SSM_W] = du.astype(BF16)
        dp_ref[:, SSM_W:] = dr_ref[...]
        dh = jnp.zeros((TB, D_MODEL), F32)
        for k in range(N_CHIP):
            dh = dh + _dot_nt(dp_ref[:, k * cs:(k + 1) * cs], w_ref[k])
        r, xh = _rms_stats(x_ref[...])
        gx_ref[...] = dx2_ref[...] + _rms_bwd(dh * g_ref[...], xh, r)
        upd = jnp.concatenate([jnp.sum(dh * xh, axis=0, keepdims=True),
                               jnp.zeros((SUBLANE - 1, D_MODEL), F32)], axis=0)

        @pl.when(i == 0)
        def _():
            sm_ref[...] = upd

        @pl.when(i > 0)
        def _():
            sm_ref[...] += upd

    return pl.pallas_call(
        body, name="bwd_in", grid=(t_len // TB,),
        in_specs=[_rows(TB, N_STATE), _rows(TB, N_STATE), _rows(TB, SSM_W),
                  _rows(TB, IN_COLS - SSM_W), _rows(TB, D_MODEL), _rows(TB, D_MODEL)]
        + [_whole()] * 4,
        out_specs=[_rows(TB, D_MODEL), _rows(TB, IN_COLS), _acc(SUBLANE, D_MODEL)],
        out_shape=[jax.ShapeDtypeStruct((t_len, D_MODEL), F32),
                   jax.ShapeDtypeStruct((t_len, IN_COLS), BF16),
                   jax.ShapeDtypeStruct((SUBLANE, D_MODEL), F32)],
        compiler_params=_params("arbitrary"),
    )(lam_r, lam_i, du_part, drest, x, dx2, g_mix, w_in, bre, bim)


def _matmul_tn(a, b, name, out_shape, grid_ij, tk, a_blk, a_map, b_blk, b_map, o_blk, o_map):
    nk = a.shape[0] // tk
    assert nk * tk == a.shape[0] and nk > 0

    def body(a_ref, b_ref, o_ref, acc_ref):
        k = pl.program_id(2)

        @pl.when(k == 0)
        def _():
            acc_ref[...] = jnp.zeros_like(acc_ref)

        acc_ref[...] += lax.dot_general(a_ref[...].astype(BF16), b_ref[...].astype(BF16),
                                        (((0,), (0,)), ((), ())), preferred_element_type=F32)

        @pl.when(k == nk - 1)
        def _():
            o_ref[...] = acc_ref[...].reshape(o_ref.shape)

    acc_shape = (a_blk[1], b_blk[1])
    return pl.pallas_call(
        body, name=name, grid=(grid_ij[0], grid_ij[1], nk),
        in_specs=[pl.BlockSpec(a_blk, a_map), pl.BlockSpec(b_blk, b_map)],
        out_specs=pl.BlockSpec(o_blk, o_map),
        out_shape=jax.ShapeDtypeStruct(out_shape, F32),
        scratch_shapes=[pltpu.VMEM(acc_shape, F32)],
        compiler_params=_params("arbitrary", "arbitrary", "arbitrary"),
    )(a, b)


def _dw_shards(a, b, name, tk=512):
    m, n = a.shape[1], b.shape[1]
    tn = n // N_CHIP
    return _matmul_tn(a, b, name, (N_CHIP, m, tn), (1, N_CHIP), tk,
                      (tk, m), lambda i, j, k: (k, 0), (tk, tn), lambda i, j, k: (k, j),
                      (None, m, tn), lambda i, j, k: (j, 0, 0))


def _dw_rows(a, b, name, tm, tk=512):
    m, n = a.shape[1], b.shape[1]
    return _matmul_tn(a, b, name, (m, n), (m // tm, 1), tk,
                      (tk, tm), lambda i, j, k: (k, i), (tk, n), lambda i, j, k: (k, 0),
                      (tm, n), lambda i, j, k: (i, 0))


def _dw_diag(a, b, name, ta, tb_, tk=512):
    n_t = b.shape[1] // tb_
    return _matmul_tn(a, b, name, (n_t * ta, tb_), (n_t, 1), tk,
                      (tk, ta), lambda i, j, k: (k, i), (tk, tb_), lambda i, j, k: (k, i),
                      (ta, tb_), lambda i, j, k: (i, 0))


def _ssm_discretize(a_re, a_im, log_dt, b_re, b_im):
    dt = jnp.exp(log_dt)[:, None]
    mag = jnp.exp(dt * a_re)
    abr = mag * jnp.cos(dt * a_im)
    abi = mag * jnp.sin(dt * a_im)
    den = a_re * a_re + a_im * a_im
    nr = abr - 1.0
    ni = abi
    f_re = (nr * a_re + ni * a_im) / den
    f_im = (ni * a_re - nr * a_im) / den
    bbr = f_re[..., None] * b_re - f_im[..., None] * b_im
    bbi = f_re[..., None] * b_im + f_im[..., None] * b_re
    return abr, abi, bbr, bbi


def _scan_tables(abr, abi):
    ar = abr.reshape(1, N_STATE)
    ai = abi.reshape(1, N_STATE)
    pr, pi = [ar], [ai]
    for _ in range(SUBLANE - 1):
        pr, pi = pr + [pr[-1] * ar - pi[-1] * ai], pi + [pr[-1] * ai + pi[-1] * ar]
    row = jnp.arange(SUBLANE)[:, None]
    tabs = []
    for d in (1, 2, 4):
        tabs.append(jnp.where(row >= d, pr[d - 1], 0.0))
        tabs.append(jnp.where(row >= d, pi[d - 1], 0.0))
    tabs.append(jnp.concatenate(pr, axis=0))
    tabs.append(jnp.concatenate(pi, axis=0))
    fwd = jnp.stack(tabs)
    sign = jnp.array([1.0, -1.0] * 4, F32)[:, None, None]
    return fwd, fwd[:, ::-1, :] * sign


def _block_diag_b(bb):
    eye = jnp.eye(SSM_G, dtype=F32)
    return jnp.einsum("gph,gk->ghkp", bb, eye).reshape(SSM_W, N_STATE)


def _block_diag_c(cc):
    eye = jnp.eye(SSM_G, dtype=F32)
    return jnp.einsum("ghp,gk->gpkh", cc, eye).reshape(N_STATE, SSM_W)


def _group_onehot():
    g = jnp.arange(SSM_G)
    return (g[:, None] % 8 == jnp.arange(8)[None, :]).astype(F32)


def _local_step(x, target, big, small):
    a_re, a_im, log_dt = small["a_re"], small["a_im"], small["log_dt"]
    (abr, abi, bbr, bbi), disc_vjp = jax.vjp(_ssm_discretize, a_re, a_im, log_dt,
                                             small["b_re"], small["b_im"])
    tab_f, tab_r = _scan_tables(abr, abi)
    bre = _block_diag_b(bbr).astype(BF16)
    bim = _block_diag_b(bbi).astype(BF16)
    cre = _block_diag_c(small["c_re"]).astype(BF16)
    cim = _block_diag_c(small["c_im"]).astype(BF16)
    tril = jnp.tril(jnp.ones((CHUNK, CHUNK), dtype=bool))
    ws = jnp.where(tril[None], small["w_s"], 0.0)
    ws_st = ws.reshape(SGU_G // 2, 2 * CHUNK, CHUNK).astype(BF16)
    wst_st = ws.transpose(0, 2, 1).reshape(SGU_G // 2, 2 * CHUNK, CHUNK).astype(BF16)
    bmat = jnp.repeat(small["b_s"].T, SGU_D, axis=1)
    g_mix = small["g_mix"].reshape(1, D_MODEL)
    g_ffn = small["g_ffn"].reshape(1, D_MODEL)
    g_final = small["g_final"].reshape(1, D_MODEL)
    g_sgu = small["g_sgu"].reshape(1, SGU_W)
    d_skip = small["d_skip"].reshape(1, SSM_W)
    b_glu = small["b_glu"].reshape(1, SSM_W)
    conv_b = small["conv_b"].reshape(1, 2 * D_FF)
    conv_w = big["conv_w"]
    w_glu = big["w_glu"].reshape(SSM_W, SSM_W)
    w_out = big["w_out"].reshape(D_MODEL, D_MODEL)
    w_down = big["w_down"].reshape(D_FF, D_MODEL)

    p, h1, bur, bui = _fwd_in(x, g_mix, big["w_in"], bre, bim)
    str_, sti = _scan_fwd(bur, bui, tab_f)
    x2, y0, z, mixed, ya, yb = _fwd_mix(x, p, str_, sti, cre, cim, d_skip, w_glu, b_glu,
                                        big["w_proj_a"], g_sgu, ws_st, bmat, big["w_proj_b"], w_out)
    up, f, h2, dx3, sm_ffn = _fwd_ffn(x2, target, g_ffn, big["w_up"], conv_w, conv_b, w_down, g_final)

    dx2, dup, sm_conv, sm_gffn = _bwd_ffn(dx3, up, x2, g_ffn, big["w_up"], conv_w, conv_b, w_down)
    (dsr, dsi, du_part, drest, mrg, dya, dyb, yap, dz, y1, sgu, dy0, sm_mix, dbm, dws) = _bwd_mix(
        dx2, p, y0, z, mixed, ya, yb, w_out, big["w_proj_a"], big["w_proj_b"], w_glu, cre, cim,
        ws_st, wst_st, d_skip, g_sgu)
    lam_r, lam_i, dar8, dai8 = _scan_bwd(dsr, dsi, str_, sti, tab_r)
    grad_x, dp, sm_gmix = _bwd_in(lam_r, lam_i, du_part, drest, x, dx2, g_mix, big["w_in"], bre, bim)

    gbig = {
        "w_in": _dw_shards(h1, dp, "dw_in"),
        "w_up": _dw_shards(h2, dup, "dw_up"),
        "w_down": _dw_rows(f, dx3, "dw_down", D_FF // 2).reshape(N_CHIP, D_FF // N_CHIP, D_MODEL),
        "w_out": _dw_rows(mrg, dx2, "dw_out", 512).reshape(N_CHIP, D_MODEL // N_CHIP, D_MODEL),
        "w_proj_a": _dw_shards(yap, dya, "dw_proj_a"),
        "w_proj_b": _dw_shards(sgu, dyb, "dw_proj_b"),
        "w_glu": _dw_rows(y1, dz, "dw_glu", SSM_W).reshape(N_CHIP, SSM_W // N_CHIP, SSM_W),
    }
    dbd_r = _dw_diag(p, lam_r, "db_re", LANE, 4 * LANE)
    dbd_i = _dw_diag(p, lam_i, "db_im", LANE, 4 * LANE)
    dcd_r = _dw_diag(str_, dy0, "dc_re", 4 * LANE, LANE)
    dcd_i = _dw_diag(sti, dy0, "dc_im", 4 * LANE, LANE)
    onehot = _group_onehot()

    def pick_b(m):
        return jnp.einsum("ghlp,gl->gph", m.reshape(SSM_G, SSM_H, 8, SSM_P), onehot)

    def pick_c(m):
        return jnp.einsum("gplh,gl->ghp", m.reshape(SSM_G, SSM_P, 8, SSM_H), onehot)

    dabr = jnp.sum(dar8, axis=0).reshape(SSM_G, SSM_P)
    dabi = jnp.sum(dai8, axis=0).reshape(SSM_G, SSM_P)
    d_a_re, d_a_im, d_log_dt, d_b_re, d_b_im = disc_vjp((dabr, dabi, pick_b(dbd_r), pick_b(dbd_i)))
    gsmall = {
        "g_mix": sm_gmix[0], "a_re": d_a_re, "a_im": d_a_im, "log_dt": d_log_dt,
        "b_re": d_b_re, "b_im": d_b_im, "c_re": pick_c(dcd_r), "c_im": -pick_c(dcd_i),
        "d_skip": sm_mix[0], "b_glu": sm_mix[1], "g_sgu": sm_mix[2],
        "w_s": jnp.where(tril[None], dws, 0.0),
        "b_s": dbm.reshape(CHUNK, SGU_G, SGU_D).sum(-1).T,
        "g_ffn": sm_gffn[0], "conv_b": sm_conv[3], "g_final": sm_ffn[0],
        "conv_w": sm_conv[0:3], "loss": sm_ffn[1, 0:1],
    }
    return grad_x, gbig, gsmall


ANY = pl.BlockSpec(memory_space=pl.ANY)


def _place():
    x, y, c = lax.axis_index("x"), lax.axis_index("y"), lax.axis_index("c")
    chips = [(1 - x, y), (x, 1 - y), (1 - x, 1 - y)]
    return x, y, c, chips


def _chip_index(chip):
    return 2 * chip[0] + chip[1]


def _remote(src, dst, send_sem, recv_sem, device):
    return pltpu.make_async_remote_copy(src_ref=src, dst_ref=dst, send_sem=send_sem,
                                        recv_sem=recv_sem, device_id=device, device_id_type=MESH)


def _ag_weights(shards, conv_w):
    n = len(shards)

    def body(*refs):
        ins, outs = refs[:n + 1], refs[n + 1:2 * n + 2]
        s1, r1, s2, r2, loc = refs[2 * n + 2:]
        x, y, c, chips = _place()
        k_me = 2 * x + y
        sib = (x, y, 1 - c)
        local = [pltpu.make_async_copy(ins[t], outs[t].at[k_me], loc.at[t]) for t in range(n + 1)]
        for cp in local:
            cp.start()
        sends = []
        for t in range(n):
            hr = shards[t].shape[0] // 2
            mine = pl.ds(c * hr, hr)
            for j, ch in enumerate(chips):
                sends.append(_remote(ins[t].at[mine, :], outs[t].at[k_me, mine, :],
                                     s1.at[3 * t + j], r1.at[3 * t + j], (*ch, c)))
        for j, ch in enumerate(chips):
            sends.append(_remote(ins[n], outs[n].at[k_me], s1.at[3 * n + j], r1.at[3 * n + j],
                                 (*ch, c)))
        for cp in sends:
            cp.start()
        passed = []
        for t in range(n):
            hr = shards[t].shape[0] // 2
            mine = pl.ds(c * hr, hr)
            for j, ch in enumerate(chips):
                landed = outs[t].at[_chip_index(ch), mine, :]
                _remote(landed, landed, s1.at[3 * t + j], r1.at[3 * t + j], sib).wait_recv()
                cp = _remote(landed, landed, s2.at[3 * t + j], r2.at[3 * t + j], sib)
                cp.start()
                passed.append(cp)
        for t in range(n):
            hr = shards[t].shape[0] // 2
            theirs = pl.ds((1 - c) * hr, hr)
            for j, ch in enumerate(chips):
                landed = outs[t].at[_chip_index(ch), theirs, :]
                _remote(landed, landed, s2.at[3 * t + j], r2.at[3 * t + j], sib).wait_recv()
        for j, ch in enumerate(chips):
            landed = outs[n].at[_chip_index(ch)]
            _remote(landed, landed, s1.at[3 * n + j], r1.at[3 * n + j], sib).wait_recv()
        for cp in sends + passed:
            cp.wait_send()
        for cp in local:
            cp.wait()

    out_shape = [jax.ShapeDtypeStruct((N_CHIP,) + s.shape, s.dtype) for s in shards]
    out_shape.append(jax.ShapeDtypeStruct((N_CHIP,) + conv_w.shape, conv_w.dtype))
    return pl.pallas_call(
        body, name="ag_weights", in_specs=[ANY] * (n + 1), out_specs=[ANY] * (n + 1),
        out_shape=out_shape,
        scratch_shapes=[pltpu.SemaphoreType.DMA((3 * n + 3,)), pltpu.SemaphoreType.DMA((3 * n + 3,)),
                        pltpu.SemaphoreType.DMA((3 * n,)), pltpu.SemaphoreType.DMA((3 * n,)),
                        pltpu.SemaphoreType.DMA((n + 1,))],
    )(*shards, conv_w)


def _rs_a(grads):
    n = len(grads)

    def body(*refs):
        ins, outs = refs[:n], refs[n:2 * n]
        s, r = refs[2 * n:]
        x, y, c, _ = _place()
        sib = (x, y, 1 - c)
        cps = []
        for t in range(n):
            hr = grads[t].shape[1] // 2
            cps.append(_remote(ins[t].at[:, pl.ds((1 - c) * hr, hr), :], outs[t],
                               s.at[t], r.at[t], sib))
        for cp in cps:
            cp.start()
        for cp in cps:
            cp.wait()

    return pl.pallas_call(
        body, name="rs_a", in_specs=[ANY] * n, out_specs=[ANY] * n,
        out_shape=[jax.ShapeDtypeStruct((N_CHIP, g.shape[1] // 2, g.shape[2]), F32) for g in grads],
        scratch_shapes=[pltpu.SemaphoreType.DMA((n,)), pltpu.SemaphoreType.DMA((n,))],
    )(*grads)


def _add_a(g, got, c_arr, name):
    _, rs, cs = g.shape
    hr = rs // 2

    def body(c_ref, g_ref, got_ref, o_ref):
        o_ref[...] = (g_ref[...] + got_ref[...]).astype(BF16)

    return pl.pallas_call(
        body, name=name,
        grid_spec=pltpu.PrefetchScalarGridSpec(
            num_scalar_prefetch=1, grid=(N_CHIP,),
            in_specs=[pl.BlockSpec((None, hr, cs), lambda k, c_ref: (k, c_ref[0], 0)),
                      pl.BlockSpec((None, hr, cs), lambda k, c_ref: (k, 0, 0))],
            out_specs=pl.BlockSpec((None, hr, cs), lambda k, c_ref: (k, 0, 0))),
        out_shape=jax.ShapeDtypeStruct((N_CHIP, hr, cs), BF16),
        compiler_params=_params("arbitrary"),
    )(c_arr, g, got)


def _rs_b(sums):
    n = len(sums)

    def body(*refs):
        ins, outs = refs[:n], refs[n:2 * n]
        s, r, loc = refs[2 * n:]
        x, y, c, chips = _place()
        k_me = 2 * x + y
        local = [pltpu.make_async_copy(ins[t].at[k_me], outs[t].at[k_me], loc.at[t]) for t in range(n)]
        cps = []
        for t in range(n):
            for j, ch in enumerate(chips):
                cps.append(_remote(ins[t].at[_chip_index(ch)], outs[t].at[k_me],
                                   s.at[3 * t + j], r.at[3 * t + j], (*ch, c)))
        for cp in local + cps:
            cp.start()
        for t in range(n):
            for j, ch in enumerate(chips):
                slot = outs[t].at[_chip_index(ch)]
                _remote(slot, slot, s.at[3 * t + j], r.at[3 * t + j], (*ch, c)).wait_recv()
        for cp in cps:
            cp.wait_send()
        for cp in local:
            cp.wait()

    return pl.pallas_call(
        body, name="rs_b", in_specs=[ANY] * n, out_specs=[ANY] * n,
        out_shape=[jax.ShapeDtypeStruct(g.shape, g.dtype) for g in sums],
        scratch_shapes=[pltpu.SemaphoreType.DMA((3 * n,)), pltpu.SemaphoreType.DMA((3 * n,)),
                        pltpu.SemaphoreType.DMA((n,))],
    )(*sums)


def _add_b(slots, name):
    _, hr, cs = slots.shape

    def body(s_ref, o_ref):
        o_ref[...] = ((s_ref[0].astype(F32) + s_ref[1].astype(F32)) + s_ref[2].astype(F32)) \
            + s_ref[3].astype(F32)

    return pl.pallas_call(
        body, name=name, in_specs=[_whole()], out_specs=_whole(),
        out_shape=jax.ShapeDtypeStruct((hr, cs), F32), compiler_params=_params(),
    )(slots)


def _rs_c(halves):
    n = len(halves)

    def body(*refs):
        ins, outs = refs[:n], refs[n:2 * n]
        s, r, loc = refs[2 * n:]
        x, y, c, _ = _place()
        sib = (x, y, 1 - c)
        local, cps = [], []
        for t in range(n):
            hr = halves[t].shape[0]
            mine = pl.ds(c * hr, hr)
            local.append(pltpu.make_async_copy(ins[t], outs[t].at[mine, :], loc.at[t]))
            cps.append(_remote(ins[t], outs[t].at[mine, :], s.at[t], r.at[t], sib))
        for cp in local + cps:
            cp.start()
        for t in range(n):
            hr = halves[t].shape[0]
            theirs = outs[t].at[pl.ds((1 - c) * hr, hr), :]
            _remote(theirs, theirs, s.at[t], r.at[t], sib).wait_recv()
        for cp in cps:
            cp.wait_send()
        for cp in local:
            cp.wait()

    return pl.pallas_call(
        body, name="rs_c", in_specs=[ANY] * n, out_specs=[ANY] * n,
        out_shape=[jax.ShapeDtypeStruct((2 * h.shape[0], h.shape[1]), F32) for h in halves],
        scratch_shapes=[pltpu.SemaphoreType.DMA((n,)), pltpu.SemaphoreType.DMA((n,)),
                        pltpu.SemaphoreType.DMA((n,))],
    )(*halves)


def _small_allreduce(pack):
    rows = pack.shape[0]
    half = rows // 2

    def body(in_ref, out_ref, sib_ref, slots_ref, s_a, r_a, s_b, r_b, s_c, r_c):
        x, y, c, chips = _place()
        k_me = 2 * x + y
        sib = (x, y, 1 - c)
        first = _remote(in_ref, sib_ref, s_a, r_a, sib)
        first.start()
        first.wait()
        mine = pl.ds(pl.multiple_of(c * half, SUBLANE), half)
        slots_ref[k_me] = in_ref[mine, :] + sib_ref[mine, :]
        cps = [_remote(slots_ref.at[k_me], slots_ref.at[k_me], s_b.at[j], r_b.at[j], (*ch, c))
               for j, ch in enumerate(chips)]
        for cp in cps:
            cp.start()
        for j, ch in enumerate(chips):
            slot = slots_ref.at[_chip_index(ch)]
            _remote(slot, slot, s_b.at[j], r_b.at[j], (*ch, c)).wait_recv()
        for cp in cps:
            cp.wait_send()
        out_ref[mine, :] = ((slots_ref[0] + slots_ref[1]) + slots_ref[2]) + slots_ref[3]
        last = _remote(out_ref.at[mine, :], out_ref.at[mine, :], s_c, r_c, sib)
        last.start()
        theirs = out_ref.at[pl.ds(pl.multiple_of((1 - c) * half, SUBLANE), half), :]
        _remote(theirs, theirs, s_c, r_c, sib).wait_recv()
        last.wait_send()

    return pl.pallas_call(
        body, name="small_allreduce", in_specs=[_whole()], out_specs=_whole(),
        out_shape=jax.ShapeDtypeStruct(pack.shape, F32),
        scratch_shapes=[pltpu.VMEM(pack.shape, F32), pltpu.VMEM((N_CHIP, half, LANE), F32),
                        pltpu.SemaphoreType.DMA, pltpu.SemaphoreType.DMA,
                        pltpu.SemaphoreType.DMA((3,)), pltpu.SemaphoreType.DMA((3,)),
                        pltpu.SemaphoreType.DMA, pltpu.SemaphoreType.DMA],
        compiler_params=_params(),
    )(pack)


def _adamw(w, g, m, v, name, tr):
    rows, cols = w.shape

    def body(w_ref, g_ref, m_ref, v_ref, d_ref, mo_ref, vo_ref):
        gv = g_ref[...]
        mn = ADAM_B1 * m_ref[...] + (1.0 - ADAM_B1) * gv
        vn = ADAM_B2 * v_ref[...] + (1.0 - ADAM_B2) * (gv * gv)
        mo_ref[...] = mn
        vo_ref[...] = vn
        m_hat = mn / (1.0 - ADAM_B1 ** ADAM_STEP)
        v_hat = vn / (1.0 - ADAM_B2 ** ADAM_STEP)
        d_ref[...] = -ADAM_LR * (m_hat / (jnp.sqrt(v_hat) + ADAM_EPS) + ADAM_WD * w_ref[...])

    blk = _rows(tr, cols)
    return pl.pallas_call(
        body, name=name, grid=(rows // tr,), in_specs=[blk] * 4, out_specs=[blk] * 3,
        out_shape=[jax.ShapeDtypeStruct(w.shape, F32)] * 3,
        compiler_params=_params("arbitrary"),
    )(w, g, m, v)


SMALL_SHAPES = {
    "g_mix": (D_MODEL,), "a_re": (SSM_G, SSM_P), "a_im": (SSM_G, SSM_P), "log_dt": (SSM_G,),
    "b_re": (SSM_G, SSM_P, SSM_H), "b_im": (SSM_G, SSM_P, SSM_H),
    "c_re": (SSM_G, SSM_H, SSM_P), "c_im": (SSM_G, SSM_H, SSM_P),
    "d_skip": (SSM_W,), "b_glu": (SSM_W,), "g_sgu": (SGU_W,), "w_s": (SGU_G, CHUNK, CHUNK),
    "b_s": (SGU_G, CHUNK), "g_ffn": (D_MODEL,), "conv_b": (2 * D_FF,), "g_final": (D_MODEL,),
}
PACK_ITEMS = [("loss", (1,))] + [(n, SMALL_SHAPES[n]) for n in SMALL] + [("conv_w", (3, 2 * D_FF))]
PACK_TILE = 2 * SUBLANE * LANE


def _padded(shape):
    return -(-math.prod(shape) // LANE) * LANE


PACK_SIZE = -(-sum(_padded(s) for _, s in PACK_ITEMS) // PACK_TILE) * PACK_TILE


def _pack(values):
    parts, used = [], 0
    for name, shape in PACK_ITEMS:
        size, room = math.prod(shape), _padded(shape)
        if name in values:
            parts.append(values[name].astype(F32).reshape(size))
            if room > size:
                parts.append(jnp.zeros((room - size,), F32))
        else:
            parts.append(jnp.zeros((room,), F32))
        used += room
    if PACK_SIZE > used:
        parts.append(jnp.zeros((PACK_SIZE - used,), F32))
    return jnp.concatenate(parts).reshape(PACK_SIZE // LANE, LANE)


def _unpack(pack):
    flat, out, off = pack.reshape(PACK_SIZE), {}, 0
    for name, shape in PACK_ITEMS:
        out[name] = flat[off:off + math.prod(shape)].reshape(shape)
        off += _padded(shape)
    return out


ADAM_ROWS = {"w_in": 256, "w_up": 256, "w_down": 352, "w_out": 256, "w_proj_a": 256,
             "w_proj_b": 256, "w_glu": 128}


def kernel(x, g_mix, w_in, a_re, a_im, log_dt, b_re, b_im, c_re, c_im, d_skip, w_glu, b_glu, w_proj_a, g_sgu, w_s, b_s, w_proj_b, w_out, g_ffn, w_up, conv_w, conv_b, w_down, g_final, loss_target, m_g_mix, m_w_in, m_a_re, m_a_im, m_log_dt, m_b_re, m_b_im, m_c_re, m_c_im, m_d_skip, m_w_glu, m_b_glu, m_w_proj_a, m_g_sgu, m_w_s, m_b_s, m_w_proj_b, m_w_out, m_g_ffn, m_w_up, m_conv_w, m_conv_b, m_w_down, m_g_final, v_g_mix, v_w_in, v_a_re, v_a_im, v_log_dt, v_b_re, v_b_im, v_c_re, v_c_im, v_d_skip, v_w_glu, v_b_glu, v_w_proj_a, v_g_sgu, v_w_s, v_b_s, v_w_proj_b, v_w_out, v_g_ffn, v_w_up, v_conv_w, v_conv_b, v_w_down, v_g_final):
    given = dict(locals())
    w = {n: given[n] for n in WEIGHTS}
    m = {n: given["m_" + n] for n in WEIGHTS}
    v = {n: given["v_" + n] for n in WEIGHTS}

    def shard2d(a):
        return a.reshape(a.shape[-2], a.shape[-1])

    gathered = _ag_weights([shard2d(w[n]).astype(BF16) for n in BIG], shard2d(w["conv_w"]))
    big = dict(zip(BIG, gathered[:-1]))
    big["conv_w"] = gathered[-1].transpose(1, 0, 2).reshape(3, 2 * D_FF)
    small = {n: w[n].reshape(SMALL_SHAPES[n]) for n in SMALL}

    grad_x, gbig, gsmall = _local_step(x[0], loss_target[0], big, small)

    c_arr = lax.axis_index("c").astype(jnp.int32).reshape(1)
    chip = 2 * lax.axis_index("x") + lax.axis_index("y")
    parts = [gbig[n] for n in BIG]
    got = _rs_a(parts)
    sums = [_add_a(g, r, c_arr, "add_a_" + n) for n, g, r in zip(BIG, parts, got)]
    slots = _rs_b(sums)
    halves = [_add_b(s, "add_b_" + n) for n, s in zip(BIG, slots)]
    grads = dict(zip(BIG, _rs_c(halves)))

    total = _unpack(_small_allreduce(_pack(gsmall)))
    cs = 2 * D_FF // N_CHIP
    grads["conv_w"] = lax.dynamic_slice(total["conv_w"], (0, chip * cs), (3, cs))
    for n in SMALL:
        grads[n] = total[n]

    delta, new_m, new_v = {}, {}, {}
    for n in BIG + ("conv_w",):
        tr = ADAM_ROWS.get(n, 3)
        delta[n], new_m[n], new_v[n] = _adamw(shard2d(w[n]), grads[n], shard2d(m[n]), shard2d(v[n]),
                                              "adamw_" + n, tr)
    pw, pm, pv = (_pack({n: d[n] for n in SMALL}) for d in (w, m, v))
    pd, pmn, pvn = _adamw(pw, _pack({n: total[n] for n in SMALL}), pm, pv, "adamw_small", 256)
    ud, um, uv = _unpack(pd), _unpack(pmn), _unpack(pvn)
    for n in SMALL:
        delta[n], new_m[n], new_v[n] = ud[n], um[n], uv[n]

    def like(d):
        return [d[n].reshape(w[n].shape) for n in WEIGHTS]

    return (total["loss"].reshape(()), grad_x.reshape(x.shape), *like(grads), *like(delta),
            *like(new_m), *like(new_v))
```

```python
import math

import jax
import jax.numpy as jnp
from jax import lax
from jax.experimental import pallas as pl
from jax.experimental.pallas import tpu as pltpu

F32 = jnp.float32
BF16 = jnp.bfloat16
MESH = pl.DeviceIdType.MESH

D_MODEL = 1024
SSM_W = 512
SSM_G = 32
SSM_H = 16
SSM_P = 64
N_STATE = SSM_G * SSM_P
SGU_W = 512
SGU_G = 8
SGU_D = 64
CHUNK = 128
D_FF = 2816
IN_COLS = 3584
EPS = 1e-6
N_CHIP = 4

ADAM_LR = 0.001
ADAM_B1 = 0.9
ADAM_B2 = 0.999
ADAM_EPS = 1e-08
ADAM_WD = 0.01
ADAM_STEP = 10

SUBLANE = 8
LANE = 128
VMEM_LIMIT = 56 * 1024 * 1024
TB = 256
TK = 512
SCAN_LANES = 256
HALO = SUBLANE

BIG = ("w_in", "w_up", "w_down", "w_out", "w_proj_a", "w_proj_b", "w_glu")
SMALL = ("g_mix", "a_re", "a_im", "log_dt", "b_re", "b_im", "c_re", "c_im", "d_skip", "b_glu",
         "g_sgu", "w_s", "b_s", "g_ffn", "conv_b", "g_final")
WEIGHTS = ("g_mix", "w_in", "a_re", "a_im", "log_dt", "b_re", "b_im", "c_re", "c_im", "d_skip",
           "w_glu", "b_glu", "w_proj_a", "g_sgu", "w_s", "b_s", "w_proj_b", "w_out", "g_ffn",
           "w_up", "conv_w", "conv_b", "w_down", "g_final")

ANY = pl.BlockSpec(memory_space=pl.ANY)


def _params(n_grid):
    return pltpu.CompilerParams(dimension_semantics=("arbitrary",) * n_grid if n_grid else None,
                                vmem_limit_bytes=VMEM_LIMIT)


def _whole():
    return pl.BlockSpec(memory_space=pltpu.VMEM)


def _rows(tb, ncol):
    return pl.BlockSpec((tb, ncol), lambda i: (i, 0))


def _acc(nrow, ncol):
    return pl.BlockSpec((nrow, ncol), lambda i: (0, 0))


def _dot(a, b):
    return jnp.dot(a.astype(BF16), b.astype(BF16), preferred_element_type=F32)


def _dot_nt(a, b):
    return lax.dot_general(a.astype(BF16), b.astype(BF16), (((1,), (1,)), ((), ())),
                           preferred_element_type=F32)


def _sigmoid(v):
    return 1.0 / (1.0 + jnp.exp(-v))


_GELU_C = math.sqrt(2.0 / math.pi)


def _gelu(v):
    return 0.5 * v * (1.0 + jnp.tanh(_GELU_C * (v + 0.044715 * v * v * v)))


def _gelu_grad(v):
    t = jnp.tanh(_GELU_C * (v + 0.044715 * v * v * v))
    return 0.5 * (1.0 + t) + 0.5 * v * (1.0 - t * t) * _GELU_C * (1.0 + 3.0 * 0.044715 * v * v)


def _rms_stats(v):
    r = lax.rsqrt(jnp.mean(v * v, axis=-1, keepdims=True) + EPS)
    return r, v * r


def _rms_bwd(dxh, xh, r):
    return r * (dxh - xh * jnp.mean(dxh * xh, axis=-1, keepdims=True))


def _place():
    x, y, c = lax.axis_index("x"), lax.axis_index("y"), lax.axis_index("c")
    chips = [(1 - x, y), (x, 1 - y), (1 - x, 1 - y)]
    return x, y, c, chips


def _chip_index(chip):
    return 2 * chip[0] + chip[1]


def _remote(src, dst, send_sem, recv_sem, device):
    return pltpu.make_async_remote_copy(src_ref=src, dst_ref=dst, send_sem=send_sem,
                                        recv_sem=recv_sem, device_id=device, device_id_type=MESH)


def _half(ref_rows, c):
    hr = ref_rows // 2
    return pl.ds(pl.multiple_of(c * hr, SUBLANE), hr)


class _Job:
    def __init__(self, start, finish, n_sem, ins=(), inouts=(), outs=()):
        self.start, self.finish, self.n_sem = start, finish, n_sem
        self.ins, self.inouts, self.outs = list(ins), list(inouts), list(outs)


def _job_gather_ici(bufs, whole=()):
    n = len(bufs)

    def copies(io):
        x, y, c, chips = _place()
        k_me = 2 * x + y
        out = []
        for t in range(n):
            for j, ch in enumerate(chips):
                if t in whole:
                    src, land = io[t].at[k_me], io[t].at[_chip_index(ch)]
                else:
                    mine = _half(bufs[t].shape[1], c)
                    src, land = io[t].at[k_me, mine, :], io[t].at[_chip_index(ch), mine, :]
                out.append((src, land, 3 * t + j, (*ch, c)))
        return out

    def start(ins, io, outs, ssem, rsem):
        for src, _, i, dev in copies(io):
            _remote(src, src, ssem(i), rsem(i), dev).start()

    def finish(ins, io, outs, ssem, rsem):
        cps = copies(io)
        for _, land, i, dev in cps:
            _remote(land, land, ssem(i), rsem(i), dev).wait_recv()
        for src, _, i, dev in cps:
            _remote(src, src, ssem(i), rsem(i), dev).wait_send()

    return _Job(start, finish, 3 * n, inouts=bufs)


def _job_gather_sibling(bufs):
    n = len(bufs)

    def copies(io):
        x, y, c, chips = _place()
        out = []
        for t in range(n):
            rows = bufs[t].shape[1]
            for j, ch in enumerate(chips):
                k = _chip_index(ch)
                out.append((io[t].at[k, _half(rows, c), :], io[t].at[k, _half(rows, 1 - c), :],
                            3 * t + j, (x, y, 1 - c)))
        return out

    def start(ins, io, outs, ssem, rsem):
        for src, _, i, dev in copies(io):
            _remote(src, src, ssem(i), rsem(i), dev).start()

    def finish(ins, io, outs, ssem, rsem):
        cps = copies(io)
        for _, land, i, dev in cps:
            _remote(land, land, ssem(i), rsem(i), dev).wait_recv()
        for src, _, i, dev in cps:
            _remote(src, src, ssem(i), rsem(i), dev).wait_send()

    return _Job(start, finish, 3 * n, inouts=bufs)


def _job_sibling_halves(grads):
    n = len(grads)

    def build(ins, outs, ssem, rsem):
        x, y, c, _ = _place()
        return [_remote(ins[t].at[:, _half(grads[t].shape[1], 1 - c), :], outs[t], ssem(t), rsem(t),
                        (x, y, 1 - c)) for t in range(n)]

    def start(ins, io, outs, ssem, rsem):
        for cp in build(ins, outs, ssem, rsem):
            cp.start()

    def finish(ins, io, outs, ssem, rsem):
        for cp in build(ins, outs, ssem, rsem):
            cp.wait()

    return _Job(start, finish, n, ins=grads,
                outs=[jax.ShapeDtypeStruct((N_CHIP, g.shape[1] // 2, g.shape[2]), F32) for g in grads])


def _job_to_owner(sums):
    n = len(sums)

    def build(ins, outs, ssem, rsem):
        x, y, c, chips = _place()
        return [_remote(ins[t].at[_chip_index(ch)], outs[t].at[j], ssem(3 * t + j), rsem(3 * t + j),
                        (*ch, c)) for t in range(n) for j, ch in enumerate(chips)]

    def start(ins, io, outs, ssem, rsem):
        for cp in build(ins, outs, ssem, rsem):
            cp.start()

    def finish(ins, io, outs, ssem, rsem):
        for cp in build(ins, outs, ssem, rsem):
            cp.wait()

    return _Job(start, finish, 3 * n, ins=sums,
                outs=[jax.ShapeDtypeStruct((3,) + s.shape[1:], s.dtype) for s in sums])


def _job_swap_halves(bufs):
    n = len(bufs)

    def start(ins, io, outs, ssem, rsem):
        x, y, c, _ = _place()
        for t in range(n):
            mine = io[t].at[_half(bufs[t].shape[0], c), :]
            _remote(mine, mine, ssem(t), rsem(t), (x, y, 1 - c)).start()

    def finish(ins, io, outs, ssem, rsem):
        x, y, c, _ = _place()
        for t in range(n):
            theirs = io[t].at[_half(bufs[t].shape[0], 1 - c), :]
            _remote(theirs, theirs, ssem(t), rsem(t), (x, y, 1 - c)).wait_recv()
        for t in range(n):
            mine = io[t].at[_half(bufs[t].shape[0], c), :]
            _remote(mine, mine, ssem(t), rsem(t), (x, y, 1 - c)).wait_send()

    return _Job(start, finish, n, inouts=bufs)


def _call(body, name, grid, in_specs, out_specs, out_shape, args, jobs=(), scratch=()):
    n_in, n_out, n_scr = len(args), len(out_shape), len(scratch)
    job_in = [a for jb in jobs for a in jb.ins + jb.inouts]
    job_out = [s for jb in jobs
               for s in [jax.ShapeDtypeStruct(a.shape, a.dtype) for a in jb.inouts] + jb.outs]
    aliases, pos_in, pos_out = {}, n_in, n_out
    for jb in jobs:
        pos_in += len(jb.ins)
        for _ in jb.inouts:
            aliases[pos_in] = pos_out
            pos_in += 1
            pos_out += 1
        pos_out += len(jb.outs)
    n_sem = sum(jb.n_sem for jb in jobs)

    def wrapped(*refs):
        c_in = refs[:n_in]
        j_in = refs[n_in:n_in + len(job_in)]
        c_out = refs[n_in + len(job_in):n_in + len(job_in) + n_out]
        j_out = refs[n_in + len(job_in) + n_out:n_in + len(job_in) + n_out + len(job_out)]
        rest = refs[n_in + len(job_in) + n_out + len(job_out):]
        c_scr = rest[:n_scr]
        views, pi, po, ps = [], 0, 0, 0
        for jb in jobs:
            ins = j_in[pi:pi + len(jb.ins)]
            pi += len(jb.ins) + len(jb.inouts)
            io = j_out[po:po + len(jb.inouts)]
            new = j_out[po + len(jb.inouts):po + len(jb.inouts) + len(jb.outs)]
            po += len(jb.inouts) + len(jb.outs)
            send = (lambda i, o=ps: rest[n_scr].at[o + i])
            recv = (lambda i, o=ps: rest[n_scr + 1].at[o + i])
            ps += jb.n_sem
            views.append((ins, io, new, send, recv))

        def run(which):
            for jb, vw in zip(jobs, views):
                (jb.start if which == 0 else jb.finish)(*vw)

        if not grid:
            run(0)
            run(1)
            return
        if jobs:
            first = pl.program_id(0) == 0
            last = pl.program_id(0) == grid[0] - 1
            for d in range(1, len(grid)):
                first = jnp.logical_and(first, pl.program_id(d) == 0)
                last = jnp.logical_and(last, pl.program_id(d) == grid[d] - 1)
            pl.when(first)(lambda: run(0))
        body(*c_in, *c_out, *c_scr)
        if jobs:
            pl.when(last)(lambda: run(1))

    sems = [pltpu.SemaphoreType.DMA((n_sem,)), pltpu.SemaphoreType.DMA((n_sem,))] if jobs else []
    kwargs = dict(grid=grid) if grid else {}
    res = pl.pallas_call(
        wrapped, name=name, in_specs=list(in_specs) + [ANY] * len(job_in),
        out_specs=list(out_specs) + [ANY] * len(job_out),
        out_shape=list(out_shape) + job_out, scratch_shapes=list(scratch) + sems,
        input_output_aliases=aliases, compiler_params=_params(len(grid)), **kwargs,
    )(*args, *job_in)
    outs, pos, per_job = list(res[:n_out]), n_out, []
    for jb in jobs:
        k = len(jb.inouts) + len(jb.outs)
        per_job.append(list(res[pos:pos + k]))
        pos += k
    return outs, per_job


def _comm(name, jobs):
    return _call(None, name, (), [], [], [], [], jobs)[1]


def _fwd_in(x, g_mix, w_in, bre, bim, jobs=()):
    t_len = x.shape[0]
    cs = IN_COLS // N_CHIP

    def body(x_ref, g_ref, w_ref, bre_ref, bim_ref, p_ref, h_ref, bur_ref, bui_ref):
        xv = x_ref[...]
        r, xh = _rms_stats(xv)
        h = (xh * g_ref[...]).astype(BF16)
        h_ref[...] = h
        for k in range(N_CHIP):
            p_ref[:, k * cs:(k + 1) * cs] = jnp.dot(h, w_ref[k], preferred_element_type=F32)
        u = p_ref[:, 0:SSM_W].astype(BF16)
        bur_ref[...] = jnp.dot(u, bre_ref[...], preferred_element_type=F32)
        bui_ref[...] = jnp.dot(u, bim_ref[...], preferred_element_type=F32)

    return _call(
        body, "fwd_in", (t_len // TB,),
        [_rows(TB, D_MODEL), _whole(), _whole(), _whole(), _whole()],
        [_rows(TB, IN_COLS), _rows(TB, D_MODEL), _rows(TB, N_STATE), _rows(TB, N_STATE)],
        [jax.ShapeDtypeStruct((t_len, IN_COLS), F32), jax.ShapeDtypeStruct((t_len, D_MODEL), BF16),
         jax.ShapeDtypeStruct((t_len, N_STATE), F32), jax.ShapeDtypeStruct((t_len, N_STATE), F32)],
        [x, g_mix, w_in, bre, bim], jobs)


def _scan_block(xr, xi, tab, shifts, cr, ci):
    for q, s in enumerate(shifts):
        ar, ai = tab[2 * q], tab[2 * q + 1]
        rr = pltpu.roll(xr, s, 0)
        ri = pltpu.roll(xi, s, 0)
        xr, xi = xr + ar * rr - ai * ri, xi + ar * ri + ai * rr
    pr, pi = tab[6], tab[7]
    return xr + pr * cr - pi * ci, xi + pr * ci + pi * cr


def _scan_fwd(bur, bui, tab, jobs=()):
    t_len = bur.shape[0]
    nblk = t_len // SUBLANE
    lb = SCAN_LANES

    def body(br_ref, bi_ref, tab_ref, sr_ref, si_ref):
        tab_v = [tab_ref[q] for q in range(8)]

        def step(k, carry):
            cr, ci = carry
            r0 = pl.multiple_of(k * SUBLANE, SUBLANE)
            xr, xi = _scan_block(br_ref[pl.ds(r0, SUBLANE), :], bi_ref[pl.ds(r0, SUBLANE), :],
                                 tab_v, (1, 2, 4), cr, ci)
            sr_ref[pl.ds(r0, SUBLANE), :] = xr
            si_ref[pl.ds(r0, SUBLANE), :] = xi
            return xr[SUBLANE - 1:SUBLANE, :], xi[SUBLANE - 1:SUBLANE, :]

        zero = jnp.zeros((1, lb), F32)
        lax.fori_loop(0, nblk, step, (zero, zero), unroll=2)

    col = pl.BlockSpec((t_len, lb), lambda j: (0, j))
    return _call(
        body, "scan_fwd", (N_STATE // lb,),
        [col, col, pl.BlockSpec((8, SUBLANE, lb), lambda j: (0, 0, j))], [col, col],
        [jax.ShapeDtypeStruct((t_len, N_STATE), F32)] * 2, [bur, bui, tab], jobs)


def _sgu_mix(v, ws_ref, lane_lo):
    rows = []
    for c0 in range(0, v.shape[0], CHUNK):
        slabs = []
        for j in range(SGU_W // LANE):
            prod = jnp.dot(ws_ref[j], v[c0:c0 + CHUNK, j * LANE:(j + 1) * LANE].astype(BF16),
                           preferred_element_type=F32)
            slabs.append(jnp.where(lane_lo, prod[:CHUNK], prod[CHUNK:]))
        rows.append(jnp.concatenate(slabs, axis=1))
    return jnp.concatenate(rows, axis=0) if len(rows) > 1 else rows[0]


def _fwd_mix(x, p, str_, sti, cre, cim, d_skip, w_glu, b_glu, w_pa, g_sgu, ws_st, bmat, w_pb, w_out,
             jobs=()):
    t_len = x.shape[0]
    pc = D_MODEL // N_CHIP

    def body(x_ref, p_ref, sr_ref, si_ref, cre_ref, cim_ref, dsk_ref, wg_ref, bg_ref, wpa_ref,
             gs_ref, ws_ref, bm_ref, wpb_ref, wo_ref,
             x2_ref, y0_ref, z_ref, mx_ref, ya_ref, yb_ref):
        u = p_ref[:, 0:SSM_W]
        y0 = _dot(sr_ref[...], cre_ref[...]) - _dot(si_ref[...], cim_ref[...]) + dsk_ref[...] * u
        y0_ref[...] = y0
        y1 = _gelu(y0)
        z = _dot(y1, wg_ref[...]) + bg_ref[...]
        z_ref[...] = z
        ya_pre = (y1 * _sigmoid(z)).astype(BF16)
        for k in range(N_CHIP):
            ya_ref[:, k * pc:(k + 1) * pc] = jnp.dot(ya_pre, wpa_ref[k], preferred_element_type=F32)

        uvg = _gelu(p_ref[:, SSM_W:SSM_W + 2 * SGU_W])
        u2 = uvg[:, :SGU_W]
        _, vh = _rms_stats(uvg[:, SGU_W:])
        v3 = vh * gs_ref[...]
        lane_lo = lax.broadcasted_iota(jnp.int32, (CHUNK, LANE), 1) < SGU_D
        bias = jnp.concatenate([bm_ref[...]] * (TB // CHUNK), axis=0)
        mixed = _sgu_mix(v3, ws_ref, lane_lo) + bias
        mx_ref[...] = mixed
        sgu = (u2 * mixed).astype(BF16)
        for k in range(N_CHIP):
            yb_ref[:, k * pc:(k + 1) * pc] = jnp.dot(sgu, wpb_ref[k], preferred_element_type=F32)

        lg0 = SSM_W + 2 * SGU_W
        ga = _sigmoid(p_ref[:, lg0:lg0 + D_MODEL])
        gb = _sigmoid(p_ref[:, lg0 + D_MODEL:lg0 + 2 * D_MODEL])
        mrg = ga * ya_ref[...] + gb * yb_ref[...]
        x2_ref[...] = x_ref[...] + _dot(mrg, wo_ref[...])

    return _call(
        body, "fwd_mix", (t_len // TB,),
        [_rows(TB, D_MODEL), _rows(TB, IN_COLS), _rows(TB, N_STATE), _rows(TB, N_STATE)]
        + [_whole()] * 11,
        [_rows(TB, D_MODEL), _rows(TB, SSM_W), _rows(TB, SSM_W), _rows(TB, SGU_W),
         _rows(TB, D_MODEL), _rows(TB, D_MODEL)],
        [jax.ShapeDtypeStruct((t_len, D_MODEL), F32), jax.ShapeDtypeStruct((t_len, SSM_W), F32),
         jax.ShapeDtypeStruct((t_len, SSM_W), F32), jax.ShapeDtypeStruct((t_len, SGU_W), F32),
         jax.ShapeDtypeStruct((t_len, D_MODEL), F32), jax.ShapeDtypeStruct((t_len, D_MODEL), F32)],
        [x, p, str_, sti, cre, cim, d_skip, w_glu, b_glu, w_pa, g_sgu, ws_st, bmat, w_pb, w_out], jobs)


def _conv_taps(v, cw_ref, c0, width):
    w0 = cw_ref[0:1, c0:c0 + width]
    w1 = cw_ref[1:2, c0:c0 + width]
    w2 = cw_ref[2:3, c0:c0 + width]
    return w0 * pltpu.roll(v, 2, 0) + w1 * pltpu.roll(v, 1, 0) + w2 * v


def _fwd_ffn(x2, target, g_ffn, w_up, conv_w, conv_b, w_down, g_final):
    t_len = x2.shape[0]
    half = D_FF // 2
    blocks_per_halo = TB // HALO

    def body(x2_ref, xp_ref, tg_ref, gf_ref, wu_ref, cw_ref, cb_ref, wd_ref, gl_ref,
             up_ref, f_ref, h2_ref, dx3_ref, sm_ref):
        i = pl.program_id(0)
        xe = jnp.concatenate([xp_ref[...], x2_ref[...]], axis=0)
        _, xh = _rms_stats(xe)
        h2 = (xh * gf_ref[...]).astype(BF16)
        h2_ref[...] = h2[HALO:]
        keep = jnp.where(jnp.logical_and(
            i == 0, lax.broadcasted_iota(jnp.int32, (TB + HALO, 1), 0) < HALO), 0.0, 1.0)
        acc = jnp.zeros((TB, D_MODEL), F32)
        for hc in range(2):
            ca = hc * half
            cb = D_FF + hc * half
            ua = jnp.dot(h2, wu_ref[hc], preferred_element_type=F32) * keep
            ub = jnp.dot(h2, wu_ref[2 + hc], preferred_element_type=F32) * keep
            up_ref[:, ca:ca + half] = ua[HALO:]
            up_ref[:, cb:cb + half] = ub[HALO:]
            ac = _conv_taps(ua, cw_ref, ca, half)[HALO:] + cb_ref[:, ca:ca + half]
            bc = _conv_taps(ub, cw_ref, cb, half)[HALO:] + cb_ref[:, cb:cb + half]
            f = (ac * _sigmoid(ac) * bc).astype(BF16)
            f_ref[:, ca:ca + half] = f
            acc = acc + jnp.dot(f, wd_ref[ca:ca + half, :], preferred_element_type=F32)
        x3 = x2_ref[...] + acc
        r3, xh3 = _rms_stats(x3)
        err = xh3 * gl_ref[...] - tg_ref[...]
        dout = err * (1.0 / D_MODEL)
        dx3_ref[...] = _rms_bwd(dout * gl_ref[...], xh3, r3)
        dgl = jnp.sum(dout * xh3, axis=0, keepdims=True)
        loss = 0.5 * jnp.sum(jnp.mean(err * err, axis=-1, keepdims=True), axis=0, keepdims=True)
        upd = jnp.concatenate([dgl, jnp.broadcast_to(loss, (1, D_MODEL)),
                               jnp.zeros((SUBLANE - 2, D_MODEL), F32)], axis=0)

        @pl.when(i == 0)
        def _():
            sm_ref[...] = upd

        @pl.when(i > 0)
        def _():
            sm_ref[...] += upd

    prev = pl.BlockSpec((HALO, D_MODEL), lambda i: (jnp.maximum(i * blocks_per_halo - 1, 0), 0))
    return _call(
        body, "fwd_ffn", (t_len // TB,),
        [_rows(TB, D_MODEL), prev, _rows(TB, D_MODEL)] + [_whole()] * 6,
        [_rows(TB, 2 * D_FF), _rows(TB, D_FF), _rows(TB, D_MODEL), _rows(TB, D_MODEL),
         _acc(SUBLANE, D_MODEL)],
        [jax.ShapeDtypeStruct((t_len, 2 * D_FF), F32), jax.ShapeDtypeStruct((t_len, D_FF), BF16),
         jax.ShapeDtypeStruct((t_len, D_MODEL), BF16), jax.ShapeDtypeStruct((t_len, D_MODEL), F32),
         jax.ShapeDtypeStruct((SUBLANE, D_MODEL), F32)],
        [x2, x2, target, g_ffn, w_up, conv_w, conv_b, w_down, g_final])[0]


def _bwd_ffn(dx3, up, x2, g_ffn, w_up, conv_w, conv_b, w_down):
    t_len = x2.shape[0]
    half = D_FF // 2
    nblk = t_len // TB
    bph = TB // HALO
    n_halo = t_len // HALO

    def body(dx_ref, dxn_ref, up_ref, upp_ref, upn_ref, x2_ref, gf_ref, wu_ref, cw_ref, cb_ref,
             wd_ref, dx2_ref, dup_ref, smw_ref, smg_ref):
        i = pl.program_id(0)
        rows_e = lax.broadcasted_iota(jnp.int32, (TB + 2 * HALO, 1), 0)
        keep_e = jnp.where(jnp.logical_and(i == 0, rows_e < HALO), 0.0, 1.0)
        rows_d = lax.broadcasted_iota(jnp.int32, (TB + HALO, 1), 0)
        keep_d = jnp.where(jnp.logical_and(i == nblk - 1, rows_d >= TB), 0.0, 1.0)
        dxe = jnp.concatenate([dx_ref[...], dxn_ref[...]], axis=0).astype(BF16)
        dh2 = jnp.zeros((TB, D_MODEL), F32)
        zpad = jnp.zeros((1, half), F32)
        for hc in range(2):
            ca = hc * half
            cb = D_FF + hc * half
            uea = jnp.concatenate([upp_ref[:, ca:ca + half], up_ref[:, ca:ca + half],
                                   upn_ref[:, ca:ca + half]], axis=0) * keep_e
            ueb = jnp.concatenate([upp_ref[:, cb:cb + half], up_ref[:, cb:cb + half],
                                   upn_ref[:, cb:cb + half]], axis=0) * keep_e
            ua1, ua2 = pltpu.roll(uea, 1, 0), pltpu.roll(uea, 2, 0)
            ub1, ub2 = pltpu.roll(ueb, 1, 0), pltpu.roll(ueb, 2, 0)
            wa = [cw_ref[k:k + 1, ca:ca + half] for k in range(3)]
            wb = [cw_ref[k:k + 1, cb:cb + half] for k in range(3)]
            ac = (wa[0] * ua2 + wa[1] * ua1 + wa[2] * uea)[HALO:] + cb_ref[:, ca:ca + half]
            bc = (wb[0] * ub2 + wb[1] * ub1 + wb[2] * ueb)[HALO:] + cb_ref[:, cb:cb + half]
            df = lax.dot_general(dxe, wd_ref[ca:ca + half, :], (((1,), (1,)), ((), ())),
                                 preferred_element_type=F32)
            sg = _sigmoid(ac)
            da = df * bc * sg * (1.0 + ac * (1.0 - sg)) * keep_d
            db = df * ac * sg * keep_d
            n_e = TB + HALO
            dua = (wa[2] * da + wa[1] * pltpu.roll(da, n_e - 1, 0)
                   + wa[0] * pltpu.roll(da, n_e - 2, 0))[:TB]
            dub = (wb[2] * db + wb[1] * pltpu.roll(db, n_e - 1, 0)
                   + wb[0] * pltpu.roll(db, n_e - 2, 0))[:TB]
            dup_ref[:, ca:ca + half] = dua.astype(BF16)
            dup_ref[:, cb:cb + half] = dub.astype(BF16)
            dh2 = dh2 + _dot_nt(dua, wu_ref[hc]) + _dot_nt(dub, wu_ref[2 + hc])
            dab = da[:TB]
            dbb = db[:TB]
            rows = []
            for d_, u0, u1, u2 in ((dab, ua2, ua1, uea), (dbb, ub2, ub1, ueb)):
                rows.append([jnp.sum(d_ * u0[HALO:HALO + TB], axis=0, keepdims=True),
                             jnp.sum(d_ * u1[HALO:HALO + TB], axis=0, keepdims=True),
                             jnp.sum(d_ * u2[HALO:HALO + TB], axis=0, keepdims=True),
                             jnp.sum(d_, axis=0, keepdims=True)])
            for c0, rws in ((ca, rows[0]), (cb, rows[1])):
                upd = jnp.concatenate(rws + [zpad] * (SUBLANE - 4), axis=0)

                @pl.when(i == 0)
                def _(upd=upd, c0=c0):
                    smw_ref[:, c0:c0 + half] = upd

                @pl.when(i > 0)
                def _(upd=upd, c0=c0):
                    smw_ref[:, c0:c0 + half] += upd

        r2, xh2 = _rms_stats(x2_ref[...])
        dx2_ref[...] = dx_ref[...] + _rms_bwd(dh2 * gf_ref[...], xh2, r2)
        updg = jnp.concatenate([jnp.sum(dh2 * xh2, axis=0, keepdims=True),
                                jnp.zeros((SUBLANE - 1, D_MODEL), F32)], axis=0)

        @pl.when(i == 0)
        def _():
            smg_ref[...] = updg

        @pl.when(i > 0)
        def _():
            smg_ref[...] += updg

    nxt_d = pl.BlockSpec((HALO, D_MODEL), lambda i: (jnp.minimum((i + 1) * bph, n_halo - 1), 0))
    prv_u = pl.BlockSpec((HALO, 2 * D_FF), lambda i: (jnp.maximum(i * bph - 1, 0), 0))
    nxt_u = pl.BlockSpec((HALO, 2 * D_FF), lambda i: (jnp.minimum((i + 1) * bph, n_halo - 1), 0))
    return _call(
        body, "bwd_ffn", (nblk,),
        [_rows(TB, D_MODEL), nxt_d, _rows(TB, 2 * D_FF), prv_u, nxt_u, _rows(TB, D_MODEL)]
        + [_whole()] * 5,
        [_rows(TB, D_MODEL), _rows(TB, 2 * D_FF), _acc(SUBLANE, 2 * D_FF), _acc(SUBLANE, D_MODEL)],
        [jax.ShapeDtypeStruct((t_len, D_MODEL), F32), jax.ShapeDtypeStruct((t_len, 2 * D_FF), BF16),
         jax.ShapeDtypeStruct((SUBLANE, 2 * D_FF), F32), jax.ShapeDtypeStruct((SUBLANE, D_MODEL), F32)],
        [dx3, dx3, up, up, up, x2, g_ffn, w_up, conv_w, conv_b, w_down])[0]


def _bwd_mix(dx2, p, y0, z, mixed, ya, yb, w_out, w_pa, w_pb, w_glu, cre, cim, ws_st, wst_st,
             d_skip, g_sgu, jobs=()):
    t_len = dx2.shape[0]
    pc = D_MODEL // N_CHIP
    n_slab = SGU_W // LANE

    def body(dx_ref, p_ref, y0_ref, z_ref, mx_ref, ya_ref, yb_ref, wo_ref, wpa_ref, wpb_ref,
             wg_ref, cre_ref, cim_ref, ws_ref, wst_ref, dsk_ref, gs_ref,
             dsr_ref, dsi_ref, du_ref, drest_ref, mrg_ref, dya_ref, dyb_ref, yap_ref, dz_ref,
             y1_ref, sgu_ref, dy0_ref, sm_ref, dbm_ref, dws_ref):
        i = pl.program_id(0)
        first = i == 0
        lg0 = SSM_W + 2 * SGU_W
        dmrg = _dot_nt(dx_ref[...], wo_ref[...])
        ga = _sigmoid(p_ref[:, lg0:lg0 + D_MODEL])
        gb = _sigmoid(p_ref[:, lg0 + D_MODEL:lg0 + 2 * D_MODEL])
        yav = ya_ref[...]
        ybv = yb_ref[...]
        mrg_ref[...] = (ga * yav + gb * ybv).astype(BF16)
        drest_ref[:, 2 * SGU_W:2 * SGU_W + D_MODEL] = (dmrg * yav * ga * (1.0 - ga)).astype(BF16)
        drest_ref[:, 2 * SGU_W + D_MODEL:] = (dmrg * ybv * gb * (1.0 - gb)).astype(BF16)
        dya = (dmrg * ga).astype(BF16)
        dyb = (dmrg * gb).astype(BF16)
        dya_ref[...] = dya
        dyb_ref[...] = dyb

        y0v = y0_ref[...]
        y1 = _gelu(y0v)
        sz = _sigmoid(z_ref[...])
        y1_ref[...] = y1.astype(BF16)
        yap_ref[...] = (y1 * sz).astype(BF16)
        dyap = jnp.zeros((TB, SSM_W), F32)
        for k in range(N_CHIP):
            dyap = dyap + _dot_nt(dya[:, k * pc:(k + 1) * pc], wpa_ref[k])
        dz = dyap * y1 * sz * (1.0 - sz)
        dz_ref[...] = dz.astype(BF16)
        dy0 = (dyap * sz + _dot_nt(dz, wg_ref[...])) * _gelu_grad(y0v)
        dy0_ref[...] = dy0.astype(BF16)
        u = p_ref[:, 0:SSM_W]
        du_ref[...] = dy0 * dsk_ref[...]
        dsr_ref[...] = _dot_nt(dy0, cre_ref[...])
        dsi_ref[...] = -_dot_nt(dy0, cim_ref[...])

        uv = p_ref[:, SSM_W:lg0]
        uvg = _gelu(uv)
        u2 = uvg[:, :SGU_W]
        rv, vh = _rms_stats(uvg[:, SGU_W:])
        v3 = vh * gs_ref[...]
        mixed = mx_ref[...]
        dsgu = jnp.zeros((TB, SGU_W), F32)
        for k in range(N_CHIP):
            dsgu = dsgu + _dot_nt(dyb[:, k * pc:(k + 1) * pc], wpb_ref[k])
        sgu_ref[...] = (u2 * mixed).astype(BF16)
        du2 = dsgu * mixed
        dmix = dsgu * u2
        lane_lo = lax.broadcasted_iota(jnp.int32, (CHUNK, LANE), 1) < SGU_D
        dv3 = _sgu_mix(dmix, wst_ref, lane_lo)
        dbm = jnp.zeros((CHUNK, SGU_W), F32)
        for c0 in range(0, TB, CHUNK):
            dbm = dbm + dmix[c0:c0 + CHUNK]
        for j in range(n_slab):
            lo = jnp.zeros((CHUNK, CHUNK), F32)
            hi = jnp.zeros((CHUNK, CHUNK), F32)
            for c0 in range(0, TB, CHUNK):
                dsl = dmix[c0:c0 + CHUNK, j * LANE:(j + 1) * LANE]
                vsl = v3[c0:c0 + CHUNK, j * LANE:(j + 1) * LANE]
                lo = lo + _dot_nt(jnp.where(lane_lo, dsl, 0.0), vsl)
                hi = hi + _dot_nt(jnp.where(lane_lo, 0.0, dsl), vsl)

            @pl.when(first)
            def _(lo=lo, hi=hi, j=j):
                dws_ref[2 * j] = lo
                dws_ref[2 * j + 1] = hi

            @pl.when(jnp.logical_not(first))
            def _(lo=lo, hi=hi, j=j):
                dws_ref[2 * j] += lo
                dws_ref[2 * j + 1] += hi

        dv2 = _rms_bwd(dv3 * gs_ref[...], vh, rv)
        gg = _gelu_grad(uv)
        drest_ref[:, 0:SGU_W] = (du2 * gg[:, :SGU_W]).astype(BF16)
        drest_ref[:, SGU_W:2 * SGU_W] = (dv2 * gg[:, SGU_W:]).astype(BF16)

        upd = jnp.concatenate([jnp.sum(dy0 * u, axis=0, keepdims=True),
                               jnp.sum(dz, axis=0, keepdims=True),
                               jnp.sum(dv3 * vh, axis=0, keepdims=True),
                               jnp.zeros((SUBLANE - 3, SSM_W), F32)], axis=0)

        @pl.when(first)
        def _():
            sm_ref[...] = upd
            dbm_ref[...] = dbm

        @pl.when(jnp.logical_not(first))
        def _():
            sm_ref[...] += upd
            dbm_ref[...] += dbm

    rest = 2 * SGU_W + 2 * D_MODEL
    bf_d, bf_s = jax.ShapeDtypeStruct((t_len, D_MODEL), BF16), jax.ShapeDtypeStruct((t_len, SSM_W), BF16)
    return _call(
        body, "bwd_mix", (t_len // TB,),
        [_rows(TB, D_MODEL), _rows(TB, IN_COLS), _rows(TB, SSM_W), _rows(TB, SSM_W),
         _rows(TB, SGU_W), _rows(TB, D_MODEL), _rows(TB, D_MODEL)] + [_whole()] * 10,
        [_rows(TB, N_STATE), _rows(TB, N_STATE), _rows(TB, SSM_W), _rows(TB, rest),
         _rows(TB, D_MODEL), _rows(TB, D_MODEL), _rows(TB, D_MODEL), _rows(TB, SSM_W),
         _rows(TB, SSM_W), _rows(TB, SSM_W), _rows(TB, SGU_W), _rows(TB, SSM_W),
         _acc(SUBLANE, SSM_W), _acc(CHUNK, SGU_W),
         pl.BlockSpec((SGU_G, CHUNK, CHUNK), lambda i: (0, 0, 0))],
        [jax.ShapeDtypeStruct((t_len, N_STATE), F32), jax.ShapeDtypeStruct((t_len, N_STATE), F32),
         jax.ShapeDtypeStruct((t_len, SSM_W), F32), jax.ShapeDtypeStruct((t_len, rest), BF16),
         bf_d, bf_d, bf_d, bf_s, bf_s, bf_s, bf_s, bf_s,
         jax.ShapeDtypeStruct((SUBLANE, SSM_W), F32), jax.ShapeDtypeStruct((CHUNK, SGU_W), F32),
         jax.ShapeDtypeStruct((SGU_G, CHUNK, CHUNK), F32)],
        [dx2, p, y0, z, mixed, ya, yb, w_out, w_pa, w_pb, w_glu, cre, cim, ws_st, wst_st, d_skip,
         g_sgu], jobs)


def _scan_bwd(dsr, dsi, str_, sti, tab_rev, jobs=()):
    t_len = dsr.shape[0]
    nblk = t_len // SUBLANE
    lb = SCAN_LANES

    def body(dr_ref, di_ref, sr_ref, si_ref, tab_ref, lr_ref, li_ref, dar_ref, dai_ref):
        tab_v = [tab_ref[q] for q in range(8)]
        row0 = lax.broadcasted_iota(jnp.int32, (SUBLANE, lb), 0) == 0

        def step(k, carry):
            cr, ci, acr, aci = carry
            kk = nblk - 1 - k
            r0 = pl.multiple_of(kk * SUBLANE, SUBLANE)
            xr, xi = _scan_block(dr_ref[pl.ds(r0, SUBLANE), :], di_ref[pl.ds(r0, SUBLANE), :],
                                 tab_v, (7, 6, 4), cr, ci)
            lr_ref[pl.ds(r0, SUBLANE), :] = xr
            li_ref[pl.ds(r0, SUBLANE), :] = xi
            rp = pl.multiple_of(jnp.maximum(kk - 1, 0) * SUBLANE, SUBLANE)
            has_prev = jnp.where(kk > 0, 1.0, 0.0)
            pr = sr_ref[pl.ds(rp, SUBLANE), :][SUBLANE - 1:SUBLANE, :] * has_prev
            pi = si_ref[pl.ds(rp, SUBLANE), :][SUBLANE - 1:SUBLANE, :] * has_prev
            s_r = jnp.where(row0, pr, pltpu.roll(sr_ref[pl.ds(r0, SUBLANE), :], 1, 0))
            s_i = jnp.where(row0, pi, pltpu.roll(si_ref[pl.ds(r0, SUBLANE), :], 1, 0))
            acr = acr + xr * s_r + xi * s_i
            aci = aci + xi * s_r - xr * s_i
            return xr[0:1, :], xi[0:1, :], acr, aci

        zero = jnp.zeros((1, lb), F32)
        zacc = jnp.zeros((SUBLANE, lb), F32)
        _, _, acr, aci = lax.fori_loop(0, nblk, step, (zero, zero, zacc, zacc), unroll=2)
        dar_ref[...] = acr
        dai_ref[...] = aci

    col = pl.BlockSpec((t_len, lb), lambda j: (0, j))
    small = pl.BlockSpec((SUBLANE, lb), lambda j: (0, j))
    return _call(
        body, "scan_bwd", (N_STATE // lb,),
        [col, col, col, col, pl.BlockSpec((8, SUBLANE, lb), lambda j: (0, 0, j))],
        [col, col, small, small],
        [jax.ShapeDtypeStruct((t_len, N_STATE), F32)] * 2
        + [jax.ShapeDtypeStruct((SUBLANE, N_STATE), F32)] * 2,
        [dsr, dsi, str_, sti, tab_rev], jobs)


def _bwd_in(lam_r, lam_i, du_part, drest, x, dx2, g_mix, w_in, bre, bim, jobs=()):
    t_len = x.shape[0]
    cs = IN_COLS // N_CHIP

    def body(lr_ref, li_ref, du_ref, dr_ref, x_ref, dx2_ref, g_ref, w_ref, bre_ref, bim_ref,
             gx_ref, dp_ref, sm_ref):
        i = pl.program_id(0)
        du = du_ref[...] + _dot_nt(lr_ref[...], bre_ref[...]) + _dot_nt(li_ref[...], bim_ref[...])
        dp_ref[:, 0:SSM_W] = du.astype(BF16)
        dp_ref[:, SSM_W:] = dr_ref[...]
        dh = jnp.zeros((TB, D_MODEL), F32)
        for k in range(N_CHIP):
            dh = dh + _dot_nt(dp_ref[:, k * cs:(k + 1) * cs], w_ref[k])
        r, xh = _rms_stats(x_ref[...])
        gx_ref[...] = dx2_ref[...] + _rms_bwd(dh * g_ref[...], xh, r)
        upd = jnp.concatenate([jnp.sum(dh * xh, axis=0, keepdims=True),
                               jnp.zeros((SUBLANE - 1, D_MODEL), F32)], axis=0)

        @pl.when(i == 0)
        def _():
            sm_ref[...] = upd

        @pl.when(i > 0)
        def _():
            sm_ref[...] += upd

    return _call(
        body, "bwd_in", (t_len // TB,),
        [_rows(TB, N_STATE), _rows(TB, N_STATE), _rows(TB, SSM_W), _rows(TB, IN_COLS - SSM_W),
         _rows(TB, D_MODEL), _rows(TB, D_MODEL)] + [_whole()] * 4,
        [_rows(TB, D_MODEL), _rows(TB, IN_COLS), _acc(SUBLANE, D_MODEL)],
        [jax.ShapeDtypeStruct((t_len, D_MODEL), F32), jax.ShapeDtypeStruct((t_len, IN_COLS), BF16),
         jax.ShapeDtypeStruct((SUBLANE, D_MODEL), F32)],
        [lam_r, lam_i, du_part, drest, x, dx2, g_mix, w_in, bre, bim], jobs)


def _matmul_tn(a, b, name, out_shape, grid_ij, a_blk, a_map, b_blk, b_map, o_blk, o_map, jobs=()):
    tk = a_blk[0]
    nk = a.shape[0] // tk
    assert nk * tk == a.shape[0] and nk > 0

    def body(a_ref, b_ref, o_ref, acc_ref):
        k = pl.program_id(2)

        @pl.when(k == 0)
        def _():
            acc_ref[...] = jnp.zeros_like(acc_ref)

        acc_ref[...] += lax.dot_general(a_ref[...].astype(BF16), b_ref[...].astype(BF16),
                                        (((0,), (0,)), ((), ())), preferred_element_type=F32)

        @pl.when(k == nk - 1)
        def _():
            o_ref[...] = acc_ref[...]

    outs, per_job = _call(
        body, name, (grid_ij[0], grid_ij[1], nk),
        [pl.BlockSpec(a_blk, a_map), pl.BlockSpec(b_blk, b_map)], [pl.BlockSpec(o_blk, o_map)],
        [jax.ShapeDtypeStruct(out_shape, F32)], [a, b], jobs,
        scratch=[pltpu.VMEM((a_blk[1], b_blk[1]), F32)])
    return outs[0], per_job


def _dw_shards(a, b, name):
    m, n = a.shape[1], b.shape[1]
    tn = n // N_CHIP
    tk = min(TK, a.shape[0])
    return _matmul_tn(a, b, name, (N_CHIP, m, tn), (1, N_CHIP),
                      (tk, m), lambda i, j, k: (k, 0), (tk, tn), lambda i, j, k: (k, j),
                      (None, m, tn), lambda i, j, k: (j, 0, 0))[0]


def _dw_rows(a, b, name, tm):
    m, n = a.shape[1], b.shape[1]
    tk = min(TK, a.shape[0])
    return _matmul_tn(a, b, name, (m, n), (m // tm, 1),
                      (tk, tm), lambda i, j, k: (k, i), (tk, n), lambda i, j, k: (k, 0),
                      (tm, n), lambda i, j, k: (i, 0))[0]


def _dw_diag(a, b, name, ta, tb_, jobs=()):
    n_t = b.shape[1] // tb_
    tk = min(TK, a.shape[0])
    return _matmul_tn(a, b, name, (n_t * ta, tb_), (n_t, 1),
                      (tk, ta), lambda i, j, k: (k, i), (tk, tb_), lambda i, j, k: (k, i),
                      (ta, tb_), lambda i, j, k: (i, 0), jobs)


def _prefetch_call(body, name, grid, scalars, in_specs, out_specs, out_shape, args):
    return pl.pallas_call(
        body, name=name,
        grid_spec=pltpu.PrefetchScalarGridSpec(num_scalar_prefetch=1, grid=grid, in_specs=in_specs,
                                               out_specs=out_specs),
        out_shape=out_shape, compiler_params=_params(len(grid)),
    )(scalars, *args)


def _place_shard(w, where, name, dtype, tr):
    rows, cols = w.shape

    def body(s_ref, w_ref, o_ref):
        o_ref[...] = w_ref[...].astype(dtype)

    return _prefetch_call(
        body, name, (rows // tr,), where,
        [pl.BlockSpec((tr, cols), lambda i, s: (i, 0))],
        pl.BlockSpec((None, tr, cols), lambda i, s: (s[0], i, 0)),
        jax.ShapeDtypeStruct((N_CHIP, rows, cols), dtype), [w])


def _add_sibling(g, got, where, name):
    _, rs, cs = g.shape
    hr = rs // 2

    def body(s_ref, g_ref, got_ref, o_ref):
        o_ref[...] = (g_ref[...] + got_ref[...]).astype(BF16)

    return _prefetch_call(
        body, name, (N_CHIP,), where,
        [pl.BlockSpec((None, hr, cs), lambda k, s: (k, s[1], 0)),
         pl.BlockSpec((None, hr, cs), lambda k, s: (k, 0, 0))],
        pl.BlockSpec((None, hr, cs), lambda k, s: (k, 0, 0)),
        jax.ShapeDtypeStruct((N_CHIP, hr, cs), BF16), [g, got])


def _add_chips(sums, got, where, name):
    _, hr, cs = sums.shape

    def body(s_ref, own_ref, got_ref, o_ref):
        o_ref[...] = ((own_ref[...].astype(F32) + got_ref[0].astype(F32))
                      + got_ref[1].astype(F32)) + got_ref[2].astype(F32)

    return _prefetch_call(
        body, name, (1,), where,
        [pl.BlockSpec((None, hr, cs), lambda i, s: (s[0], 0, 0)),
         pl.BlockSpec((3, hr, cs), lambda i, s: (0, 0, 0))],
        pl.BlockSpec((hr, cs), lambda i, s: (s[1], 0)),
        jax.ShapeDtypeStruct((2 * hr, cs), F32), [sums, got])


def _small_allreduce(pack):
    rows = pack.shape[0]
    half = rows // 2

    def body(in_ref, out_ref, sib_ref, slots_ref, s_a, r_a, s_b, r_b, s_c, r_c):
        x, y, c, chips = _place()
        k_me = 2 * x + y
        sib = (x, y, 1 - c)
        first = _remote(in_ref, sib_ref, s_a, r_a, sib)
        first.start()
        first.wait()
        mine = _half(rows, c)
        slots_ref[k_me] = in_ref[mine, :] + sib_ref[mine, :]
        cps = [_remote(slots_ref.at[k_me], slots_ref.at[k_me], s_b.at[j], r_b.at[j], (*ch, c))
               for j, ch in enumerate(chips)]
        for cp in cps:
            cp.start()
        for j, ch in enumerate(chips):
            slot = slots_ref.at[_chip_index(ch)]
            _remote(slot, slot, s_b.at[j], r_b.at[j], (*ch, c)).wait_recv()
        for cp in cps:
            cp.wait_send()
        out_ref[mine, :] = ((slots_ref[0] + slots_ref[1]) + slots_ref[2]) + slots_ref[3]
        last = _remote(out_ref.at[mine, :], out_ref.at[mine, :], s_c, r_c, sib)
        last.start()
        theirs = out_ref.at[_half(rows, 1 - c), :]
        _remote(theirs, theirs, s_c, r_c, sib).wait_recv()
        last.wait_send()

    return pl.pallas_call(
        body, name="small_allreduce", in_specs=[_whole()], out_specs=_whole(),
        out_shape=jax.ShapeDtypeStruct(pack.shape, F32),
        scratch_shapes=[pltpu.VMEM(pack.shape, F32), pltpu.VMEM((N_CHIP, half, LANE), F32),
                        pltpu.SemaphoreType.DMA, pltpu.SemaphoreType.DMA,
                        pltpu.SemaphoreType.DMA((3,)), pltpu.SemaphoreType.DMA((3,)),
                        pltpu.SemaphoreType.DMA, pltpu.SemaphoreType.DMA],
        compiler_params=_params(0),
    )(pack)


def _adamw(w, g, m, v, name, tr):
    rows, cols = w.shape

    def body(w_ref, g_ref, m_ref, v_ref, d_ref, mo_ref, vo_ref):
        gv = g_ref[...]
        mn = ADAM_B1 * m_ref[...] + (1.0 - ADAM_B1) * gv
        vn = ADAM_B2 * v_ref[...] + (1.0 - ADAM_B2) * (gv * gv)
        mo_ref[...] = mn
        vo_ref[...] = vn
        m_hat = mn / (1.0 - ADAM_B1 ** ADAM_STEP)
        v_hat = vn / (1.0 - ADAM_B2 ** ADAM_STEP)
        d_ref[...] = -ADAM_LR * (m_hat / (jnp.sqrt(v_hat) + ADAM_EPS) + ADAM_WD * w_ref[...])

    blk = _rows(tr, cols)
    return _call(body, name, (rows // tr,), [blk] * 4, [blk] * 3,
                 [jax.ShapeDtypeStruct(w.shape, F32)] * 3, [w, g, m, v])[0]


def _ssm_discretize(a_re, a_im, log_dt, b_re, b_im):
    dt = jnp.exp(log_dt)[:, None]
    mag = jnp.exp(dt * a_re)
    abr = mag * jnp.cos(dt * a_im)
    abi = mag * jnp.sin(dt * a_im)
    den = a_re * a_re + a_im * a_im
    nr = abr - 1.0
    ni = abi
    f_re = (nr * a_re + ni * a_im) / den
    f_im = (ni * a_re - nr * a_im) / den
    bbr = f_re[..., None] * b_re - f_im[..., None] * b_im
    bbi = f_re[..., None] * b_im + f_im[..., None] * b_re
    return abr, abi, bbr, bbi


def _scan_tables(abr, abi):
    ar = abr.reshape(1, N_STATE)
    ai = abi.reshape(1, N_STATE)
    pr, pi = [ar], [ai]
    for _ in range(SUBLANE - 1):
        pr, pi = pr + [pr[-1] * ar - pi[-1] * ai], pi + [pr[-1] * ai + pi[-1] * ar]
    row = jnp.arange(SUBLANE)[:, None]
    tabs = []
    for d in (1, 2, 4):
        tabs.append(jnp.where(row >= d, pr[d - 1], 0.0))
        tabs.append(jnp.where(row >= d, pi[d - 1], 0.0))
    tabs.append(jnp.concatenate(pr, axis=0))
    tabs.append(jnp.concatenate(pi, axis=0))
    fwd = jnp.stack(tabs)
    sign = jnp.array([1.0, -1.0] * 4, F32)[:, None, None]
    return fwd, fwd[:, ::-1, :] * sign


def _block_diag_b(bb):
    eye = jnp.eye(SSM_G, dtype=F32)
    return jnp.einsum("gph,gk->ghkp", bb, eye).reshape(SSM_W, N_STATE)


def _block_diag_c(cc):
    eye = jnp.eye(SSM_G, dtype=F32)
    return jnp.einsum("ghp,gk->gpkh", cc, eye).reshape(N_STATE, SSM_W)


def _group_onehot():
    g = jnp.arange(SSM_G)
    return (g[:, None] % 8 == jnp.arange(8)[None, :]).astype(F32)


SMALL_SHAPES = {
    "g_mix": (D_MODEL,), "a_re": (SSM_G, SSM_P), "a_im": (SSM_G, SSM_P), "log_dt": (SSM_G,),
    "b_re": (SSM_G, SSM_P, SSM_H), "b_im": (SSM_G, SSM_P, SSM_H),
    "c_re": (SSM_G, SSM_H, SSM_P), "c_im": (SSM_G, SSM_H, SSM_P),
    "d_skip": (SSM_W,), "b_glu": (SSM_W,), "g_sgu": (SGU_W,), "w_s": (SGU_G, CHUNK, CHUNK),
    "b_s": (SGU_G, CHUNK), "g_ffn": (D_MODEL,), "conv_b": (2 * D_FF,), "g_final": (D_MODEL,),
}
PACK_ITEMS = [("loss", (1,))] + [(n, SMALL_SHAPES[n]) for n in SMALL] + [("conv_w", (3, 2 * D_FF))]
TILE = SUBLANE * LANE


def _item_rows(shape):
    return -(-math.prod(shape) // TILE) * SUBLANE


PACK_ROWS = -(-sum(_item_rows(s) for _, s in PACK_ITEMS) // (2 * SUBLANE)) * (2 * SUBLANE)


def _pack(values):
    parts, used = [], 0
    for name, shape in PACK_ITEMS:
        size, rows = math.prod(shape), _item_rows(shape)
        if name in values:
            flat = values[name].astype(F32).reshape(size)
            if rows * LANE > size:
                flat = jnp.pad(flat, (0, rows * LANE - size))
            parts.append(flat.reshape(rows, LANE))
        else:
            parts.append(jnp.zeros((rows, LANE), F32))
        used += rows
    if PACK_ROWS > used:
        parts.append(jnp.zeros((PACK_ROWS - used, LANE), F32))
    return jnp.concatenate(parts, axis=0)


def _unpack(pack):
    out, off = {}, 0
    for name, shape in PACK_ITEMS:
        rows = _item_rows(shape)
        out[name] = pack[off:off + rows].reshape(rows * LANE)[:math.prod(shape)].reshape(shape)
        off += rows
    return out


PLACE_ROWS = {"w_in": 256, "w_up": 256, "w_down": 352, "w_out": 256, "w_proj_a": 256,
              "w_proj_b": 256, "w_glu": 128}


def kernel(x, g_mix, w_in, a_re, a_im, log_dt, b_re, b_im, c_re, c_im, d_skip, w_glu, b_glu, w_proj_a, g_sgu, w_s, b_s, w_proj_b, w_out, g_ffn, w_up, conv_w, conv_b, w_down, g_final, loss_target, m_g_mix, m_w_in, m_a_re, m_a_im, m_log_dt, m_b_re, m_b_im, m_c_re, m_c_im, m_d_skip, m_w_glu, m_b_glu, m_w_proj_a, m_g_sgu, m_w_s, m_b_s, m_w_proj_b, m_w_out, m_g_ffn, m_w_up, m_conv_w, m_conv_b, m_w_down, m_g_final, v_g_mix, v_w_in, v_a_re, v_a_im, v_log_dt, v_b_re, v_b_im, v_c_re, v_c_im, v_d_skip, v_w_glu, v_b_glu, v_w_proj_a, v_g_sgu, v_w_s, v_b_s, v_w_proj_b, v_w_out, v_g_ffn, v_w_up, v_conv_w, v_conv_b, v_w_down, v_g_final):
    given = dict(locals())
    w = {n: given[n] for n in WEIGHTS}
    m = {n: given["m_" + n] for n in WEIGHTS}
    v = {n: given["v_" + n] for n in WEIGHTS}

    def shard2d(a):
        return a.reshape(a.shape[-2], a.shape[-1])

    chip = 2 * lax.axis_index("x") + lax.axis_index("y")
    where = jnp.stack([chip, lax.axis_index("c")]).astype(jnp.int32)
    xs, target = x[0], loss_target[0]
    small = {n: w[n].reshape(SMALL_SHAPES[n]) for n in SMALL}

    (abr, abi, bbr, bbi), disc_vjp = jax.vjp(_ssm_discretize, small["a_re"], small["a_im"],
                                             small["log_dt"], small["b_re"], small["b_im"])
    tab_f, tab_r = _scan_tables(abr, abi)
    bre = _block_diag_b(bbr).astype(BF16)
    bim = _block_diag_b(bbi).astype(BF16)
    cre = _block_diag_c(small["c_re"]).astype(BF16)
    cim = _block_diag_c(small["c_im"]).astype(BF16)
    tril = jnp.tril(jnp.ones((CHUNK, CHUNK), dtype=bool))
    ws = jnp.where(tril[None], small["w_s"], 0.0)
    ws_st = ws.reshape(SGU_G // 2, 2 * CHUNK, CHUNK).astype(BF16)
    wst_st = ws.transpose(0, 2, 1).reshape(SGU_G // 2, 2 * CHUNK, CHUNK).astype(BF16)
    bmat = jnp.repeat(small["b_s"].T, SGU_D, axis=1)
    g_mix2 = small["g_mix"].reshape(1, D_MODEL)
    g_ffn2 = small["g_ffn"].reshape(1, D_MODEL)
    g_final2 = small["g_final"].reshape(1, D_MODEL)
    g_sgu2 = small["g_sgu"].reshape(1, SGU_W)
    d_skip2 = small["d_skip"].reshape(1, SSM_W)
    b_glu2 = small["b_glu"].reshape(1, SSM_W)
    conv_b2 = small["conv_b"].reshape(1, 2 * D_FF)

    gat = {n: _place_shard(shard2d(w[n]), where, "place_" + n, BF16, PLACE_ROWS[n]) for n in BIG}
    gat["conv_w"] = _place_shard(shard2d(w["conv_w"]), where, "place_conv_w", F32, 3)
    (gat["w_in"],), = _comm("gather_in_ici", [_job_gather_ici([gat["w_in"]])])
    (gat["w_in"],), = _comm("gather_in_sibling", [_job_gather_sibling([gat["w_in"]])])
    mixers = ["w_glu", "w_proj_a", "w_proj_b", "w_out", "conv_w"]
    ffn = ["w_up", "w_down"]

    (p, h1, bur, bui), (got,) = _fwd_in(
        xs, g_mix2, gat["w_in"], bre, bim,
        [_job_gather_ici([gat[n] for n in mixers], whole=(4,))])
    gat.update(zip(mixers, got))
    (str_, sti), (got_m, got_f) = _scan_fwd(
        bur, bui, tab_f,
        [_job_gather_sibling([gat[n] for n in mixers[:4]]), _job_gather_ici([gat[n] for n in ffn])])
    gat.update(zip(mixers[:4], got_m))
    gat.update(zip(ffn, got_f))
    w_glu_f = gat["w_glu"].reshape(SSM_W, SSM_W)
    w_out_f = gat["w_out"].reshape(D_MODEL, D_MODEL)
    conv_w_f = gat["conv_w"].transpose(1, 0, 2).reshape(3, 2 * D_FF)
    (x2, y0, z, mixed, ya, yb), (got_f,) = _fwd_mix(
        xs, p, str_, sti, cre, cim, d_skip2, w_glu_f, b_glu2, gat["w_proj_a"], g_sgu2, ws_st, bmat,
        gat["w_proj_b"], w_out_f, [_job_gather_sibling([gat[n] for n in ffn])])
    gat.update(zip(ffn, got_f))
    w_down_f = gat["w_down"].reshape(D_FF, D_MODEL)
    up, f, h2, dx3, sm_ffn = _fwd_ffn(x2, target, g_ffn2, gat["w_up"], conv_w_f, conv_b2, w_down_f,
                                      g_final2)

    dx2, dup, sm_conv, sm_gffn = _bwd_ffn(dx3, up, x2, g_ffn2, gat["w_up"], conv_w_f, conv_b2, w_down_f)
    part = {"w_up": _dw_shards(h2, dup, "dw_up"),
            "w_down": _dw_rows(f, dx3, "dw_down", D_FF // 2).reshape(N_CHIP, D_FF // N_CHIP, D_MODEL)}
    ((dsr, dsi, du_part, drest, mrg, dya, dyb, yap, dz, y1, sgu, dy0, sm_mix, dbm, dws),
     (sib_f,)) = _bwd_mix(dx2, p, y0, z, mixed, ya, yb, w_out_f, gat["w_proj_a"], gat["w_proj_b"],
                          w_glu_f, cre, cim, ws_st, wst_st, d_skip2, g_sgu2,
                          [_job_sibling_halves([part[n] for n in ffn])])
    sums_f = [_add_sibling(part[n], s, where, "add_sibling_" + n) for n, s in zip(ffn, sib_f)]
    (lam_r, lam_i, dar8, dai8), (own_f,) = _scan_bwd(dsr, dsi, str_, sti, tab_r,
                                                     [_job_to_owner(sums_f)])
    red = {n: _add_chips(s, o, where, "add_chips_" + n) for n, s, o in zip(ffn, sums_f, own_f)}
    mix4 = ["w_out", "w_proj_a", "w_proj_b", "w_glu"]
    part["w_out"] = _dw_rows(mrg, dx2, "dw_out", 512).reshape(N_CHIP, D_MODEL // N_CHIP, D_MODEL)
    part["w_proj_a"] = _dw_shards(yap, dya, "dw_proj_a")
    part["w_proj_b"] = _dw_shards(sgu, dyb, "dw_proj_b")
    part["w_glu"] = _dw_rows(y1, dz, "dw_glu", SSM_W).reshape(N_CHIP, SSM_W // N_CHIP, SSM_W)
    (grad_x, dp, sm_gmix), (sib_m, done_f) = _bwd_in(
        lam_r, lam_i, du_part, drest, xs, dx2, g_mix2, gat["w_in"], bre, bim,
        [_job_sibling_halves([part[n] for n in mix4]), _job_swap_halves([red[n] for n in ffn])])
    red.update(zip(ffn, done_f))
    sums_m = [_add_sibling(part[n], s, where, "add_sibling_" + n) for n, s in zip(mix4, sib_m)]
    part["w_in"] = _dw_shards(h1, dp, "dw_in")
    dbd_r, ((sib_i,), own_m) = _dw_diag(
        p, lam_r, "db_re", LANE, 4 * LANE,
        [_job_sibling_halves([part["w_in"]]), _job_to_owner(sums_m)])
    sum_i = _add_sibling(part["w_in"], sib_i, where, "add_sibling_w_in")
    red_m = [_add_chips(s, o, where, "add_chips_" + n) for n, s, o in zip(mix4, sums_m, own_m)]
    dbd_i, ((own_i,), done_m) = _dw_diag(
        p, lam_i, "db_im", LANE, 4 * LANE, [_job_to_owner([sum_i]), _job_swap_halves(red_m)])
    red.update(zip(mix4, done_m))
    red_i = _add_chips(sum_i, own_i, where, "add_chips_w_in")
    dcd_r, ((red["w_in"],),) = _dw_diag(str_, dy0, "dc_re", 4 * LANE, LANE, [_job_swap_halves([red_i])])
    dcd_i, _ = _dw_diag(sti, dy0, "dc_im", 4 * LANE, LANE)

    onehot = _group_onehot()

    def pick_b(blk):
        return jnp.einsum("ghlp,gl->gph", blk.reshape(SSM_G, SSM_H, 8, SSM_P), onehot)

    def pick_c(blk):
        return jnp.einsum("gplh,gl->ghp", blk.reshape(SSM_G, SSM_P, 8, SSM_H), onehot)

    dabr = jnp.sum(dar8, axis=0).reshape(SSM_G, SSM_P)
    dabi = jnp.sum(dai8, axis=0).reshape(SSM_G, SSM_P)
    d_a_re, d_a_im, d_log_dt, d_b_re, d_b_im = disc_vjp((dabr, dabi, pick_b(dbd_r), pick_b(dbd_i)))
    gsmall = {
        "g_mix": sm_gmix[0], "a_re": d_a_re, "a_im": d_a_im, "log_dt": d_log_dt,
        "b_re": d_b_re, "b_im": d_b_im, "c_re": pick_c(dcd_r), "c_im": -pick_c(dcd_i),
        "d_skip": sm_mix[0], "b_glu": sm_mix[1], "g_sgu": sm_mix[2],
        "w_s": jnp.where(tril[None], dws, 0.0),
        "b_s": dbm.reshape(CHUNK, SGU_G, SGU_D).sum(-1).T,
        "g_ffn": sm_gffn[0], "conv_b": sm_conv[3], "g_final": sm_ffn[0],
        "conv_w": sm_conv[0:3], "loss": sm_ffn[1, 0:1],
    }

    total_pack = _small_allreduce(_pack(gsmall))
    total = _unpack(total_pack)
    grads = dict(red)
    cs = 2 * D_FF // N_CHIP
    grads["conv_w"] = lax.dynamic_slice(total["conv_w"], (0, chip * cs), (3, cs))
    for n in SMALL:
        grads[n] = total[n]
    delta, new_m, new_v = {}, {}, {}
    for n in BIG + ("conv_w",):
        delta[n], new_m[n], new_v[n] = _adamw(shard2d(w[n]), grads[n], shard2d(m[n]), shard2d(v[n]),
                                              "adamw_" + n, PLACE_ROWS.get(n, 3))
    pw, pm, pv = (_pack({n: d[n] for n in SMALL}) for d in (w, m, v))
    ud, um, uv = (_unpack(a) for a in _adamw(pw, total_pack, pm, pv, "adamw_small", PACK_ROWS // 2))
    for n in SMALL:
        delta[n], new_m[n], new_v[n] = ud[n], um[n], uv[n]

    def like(d):
        return [d[n].reshape(w[n].shape) for n in WEIGHTS]

    return (total["loss"].reshape(()), grad_x.reshape(x.shape), *like(grads), *like(delta),
            *like(new_m), *like(new_v))
```

```python
import math

import jax
import jax.numpy as jnp
from jax import lax
from jax.experimental import pallas as pl
from jax.experimental.pallas import tpu as pltpu

F32 = jnp.float32
BF16 = jnp.bfloat16
MESH = pl.DeviceIdType.MESH

D_MODEL = 1024
SSM_W = 512
SSM_G = 32
SSM_H = 16
SSM_P = 64
N_STATE = SSM_G * SSM_P
SGU_W = 512
SGU_G = 8
SGU_D = 64
CHUNK = 128
D_FF = 2816
IN_COLS = 3584
EPS = 1e-6
N_CHIP = 4

ADAM_LR = 0.001
ADAM_B1 = 0.9
ADAM_B2 = 0.999
ADAM_EPS = 1e-08
ADAM_WD = 0.01
ADAM_STEP = 10

SUBLANE = 8
LANE = 128
VMEM_LIMIT = 56 * 1024 * 1024
TB = 256
TK = 512
SCAN_LANES = 256
SCAN_UNROLL = 4
HALO = SUBLANE

BIG = ("w_in", "w_up", "w_down", "w_out", "w_proj_a", "w_proj_b", "w_glu")
SMALL = ("g_mix", "a_re", "a_im", "log_dt", "b_re", "b_im", "c_re", "c_im", "d_skip", "b_glu",
         "g_sgu", "w_s", "b_s", "g_ffn", "conv_b", "g_final")
WEIGHTS = ("g_mix", "w_in", "a_re", "a_im", "log_dt", "b_re", "b_im", "c_re", "c_im", "d_skip",
           "w_glu", "b_glu", "w_proj_a", "g_sgu", "w_s", "b_s", "w_proj_b", "w_out", "g_ffn",
           "w_up", "conv_w", "conv_b", "w_down", "g_final")

ANY = pl.BlockSpec(memory_space=pl.ANY)


def _params(n_grid):
    return pltpu.CompilerParams(dimension_semantics=("arbitrary",) * n_grid if n_grid else None,
                                vmem_limit_bytes=VMEM_LIMIT)


def _whole():
    return pl.BlockSpec(memory_space=pltpu.VMEM)


def _rows(tb, ncol):
    return pl.BlockSpec((tb, ncol), lambda i: (i, 0))


def _acc(nrow, ncol):
    return pl.BlockSpec((nrow, ncol), lambda i: (0, 0))


def _dot(a, b):
    return jnp.dot(a.astype(BF16), b.astype(BF16), preferred_element_type=F32)


def _dot_nt(a, b):
    return lax.dot_general(a.astype(BF16), b.astype(BF16), (((1,), (1,)), ((), ())),
                           preferred_element_type=F32)


def _sigmoid(v):
    return 1.0 / (1.0 + jnp.exp(-v))


_GELU_C = math.sqrt(2.0 / math.pi)


def _gelu(v):
    return 0.5 * v * (1.0 + jnp.tanh(_GELU_C * (v + 0.044715 * v * v * v)))


def _gelu_grad(v):
    t = jnp.tanh(_GELU_C * (v + 0.044715 * v * v * v))
    return 0.5 * (1.0 + t) + 0.5 * v * (1.0 - t * t) * _GELU_C * (1.0 + 3.0 * 0.044715 * v * v)


def _rms_stats(v):
    r = lax.rsqrt(jnp.mean(v * v, axis=-1, keepdims=True) + EPS)
    return r, v * r


def _rms_bwd(dxh, xh, r):
    return r * (dxh - xh * jnp.mean(dxh * xh, axis=-1, keepdims=True))


def _place():
    x, y, c = lax.axis_index("x"), lax.axis_index("y"), lax.axis_index("c")
    chips = [(1 - x, y), (x, 1 - y), (1 - x, 1 - y)]
    return x, y, c, chips


def _chip_index(chip):
    return 2 * chip[0] + chip[1]


def _remote(src, dst, send_sem, recv_sem, device):
    return pltpu.make_async_remote_copy(src_ref=src, dst_ref=dst, send_sem=send_sem,
                                        recv_sem=recv_sem, device_id=device, device_id_type=MESH)


def _half(ref_rows, c):
    hr = ref_rows // 2
    return pl.ds(pl.multiple_of(c * hr, SUBLANE), hr)


class _Job:
    def __init__(self, start, finish, n_sem, ins=(), inouts=(), outs=()):
        self.start, self.finish, self.n_sem = start, finish, n_sem
        self.ins, self.inouts, self.outs = list(ins), list(inouts), list(outs)


def _job_gather_ici(bufs, whole=()):
    n = len(bufs)

    def copies(io):
        x, y, c, chips = _place()
        k_me = 2 * x + y
        out = []
        for t in range(n):
            for j, ch in enumerate(chips):
                if t in whole:
                    src, land = io[t].at[k_me], io[t].at[_chip_index(ch)]
                else:
                    mine = _half(bufs[t].shape[1], c)
                    src, land = io[t].at[k_me, mine, :], io[t].at[_chip_index(ch), mine, :]
                out.append((src, land, 3 * t + j, (*ch, c)))
        return out

    def start(ins, io, outs, ssem, rsem):
        for src, _, i, dev in copies(io):
            _remote(src, src, ssem(i), rsem(i), dev).start()

    def finish(ins, io, outs, ssem, rsem):
        cps = copies(io)
        for _, land, i, dev in cps:
            _remote(land, land, ssem(i), rsem(i), dev).wait_recv()
        for src, _, i, dev in cps:
            _remote(src, src, ssem(i), rsem(i), dev).wait_send()

    return _Job(start, finish, 3 * n, inouts=bufs)


def _job_gather_sibling(bufs):
    n = len(bufs)

    def copies(io):
        x, y, c, chips = _place()
        out = []
        for t in range(n):
            rows = bufs[t].shape[1]
            for j, ch in enumerate(chips):
                k = _chip_index(ch)
                out.append((io[t].at[k, _half(rows, c), :], io[t].at[k, _half(rows, 1 - c), :],
                            3 * t + j, (x, y, 1 - c)))
        return out

    def start(ins, io, outs, ssem, rsem):
        for src, _, i, dev in copies(io):
            _remote(src, src, ssem(i), rsem(i), dev).start()

    def finish(ins, io, outs, ssem, rsem):
        cps = copies(io)
        for _, land, i, dev in cps:
            _remote(land, land, ssem(i), rsem(i), dev).wait_recv()
        for src, _, i, dev in cps:
            _remote(src, src, ssem(i), rsem(i), dev).wait_send()

    return _Job(start, finish, 3 * n, inouts=bufs)


def _job_sibling_halves(grads):
    n = len(grads)

    def build(ins, outs, ssem, rsem):
        x, y, c, _ = _place()
        return [_remote(ins[t].at[:, _half(grads[t].shape[1], 1 - c), :], outs[t], ssem(t), rsem(t),
                        (x, y, 1 - c)) for t in range(n)]

    def start(ins, io, outs, ssem, rsem):
        for cp in build(ins, outs, ssem, rsem):
            cp.start()

    def finish(ins, io, outs, ssem, rsem):
        for cp in build(ins, outs, ssem, rsem):
            cp.wait()

    return _Job(start, finish, n, ins=grads,
                outs=[jax.ShapeDtypeStruct((N_CHIP, g.shape[1] // 2, g.shape[2]), F32) for g in grads])


def _job_to_owner(sums):
    n = len(sums)

    def build(ins, outs, ssem, rsem):
        x, y, c, chips = _place()
        return [_remote(ins[t].at[_chip_index(ch)], outs[t].at[j], ssem(3 * t + j), rsem(3 * t + j),
                        (*ch, c)) for t in range(n) for j, ch in enumerate(chips)]

    def start(ins, io, outs, ssem, rsem):
        for cp in build(ins, outs, ssem, rsem):
            cp.start()

    def finish(ins, io, outs, ssem, rsem):
        for cp in build(ins, outs, ssem, rsem):
            cp.wait()

    return _Job(start, finish, 3 * n, ins=sums,
                outs=[jax.ShapeDtypeStruct((3,) + s.shape[1:], s.dtype) for s in sums])


def _job_swap_halves(bufs):
    n = len(bufs)

    def start(ins, io, outs, ssem, rsem):
        x, y, c, _ = _place()
        for t in range(n):
            mine = io[t].at[_half(bufs[t].shape[0], c), :]
            _remote(mine, mine, ssem(t), rsem(t), (x, y, 1 - c)).start()

    def finish(ins, io, outs, ssem, rsem):
        x, y, c, _ = _place()
        for t in range(n):
            theirs = io[t].at[_half(bufs[t].shape[0], 1 - c), :]
            _remote(theirs, theirs, ssem(t), rsem(t), (x, y, 1 - c)).wait_recv()
        for t in range(n):
            mine = io[t].at[_half(bufs[t].shape[0], c), :]
            _remote(mine, mine, ssem(t), rsem(t), (x, y, 1 - c)).wait_send()

    return _Job(start, finish, n, inouts=bufs)


def _call(body, name, grid, in_specs, out_specs, out_shape, args, jobs=(), scratch=()):
    n_in, n_out, n_scr = len(args), len(out_shape), len(scratch)
    job_in = [a for jb in jobs for a in jb.ins + jb.inouts]
    job_out = [s for jb in jobs
               for s in [jax.ShapeDtypeStruct(a.shape, a.dtype) for a in jb.inouts] + jb.outs]
    aliases, pos_in, pos_out = {}, n_in, n_out
    for jb in jobs:
        pos_in += len(jb.ins)
        for _ in jb.inouts:
            aliases[pos_in] = pos_out
            pos_in += 1
            pos_out += 1
        pos_out += len(jb.outs)
    n_sem = sum(jb.n_sem for jb in jobs)

    def wrapped(*refs):
        c_in = refs[:n_in]
        j_in = refs[n_in:n_in + len(job_in)]
        c_out = refs[n_in + len(job_in):n_in + len(job_in) + n_out]
        j_out = refs[n_in + len(job_in) + n_out:n_in + len(job_in) + n_out + len(job_out)]
        rest = refs[n_in + len(job_in) + n_out + len(job_out):]
        c_scr = rest[:n_scr]
        views, pi, po, ps = [], 0, 0, 0
        for jb in jobs:
            ins = j_in[pi:pi + len(jb.ins)]
            pi += len(jb.ins) + len(jb.inouts)
            io = j_out[po:po + len(jb.inouts)]
            new = j_out[po + len(jb.inouts):po + len(jb.inouts) + len(jb.outs)]
            po += len(jb.inouts) + len(jb.outs)
            send = (lambda i, o=ps: rest[n_scr].at[o + i])
            recv = (lambda i, o=ps: rest[n_scr + 1].at[o + i])
            ps += jb.n_sem
            views.append((ins, io, new, send, recv))

        def run(which):
            for jb, vw in zip(jobs, views):
                (jb.start if which == 0 else jb.finish)(*vw)

        if not grid:
            run(0)
            run(1)
            return
        if jobs:
            first = pl.program_id(0) == 0
            last = pl.program_id(0) == grid[0] - 1
            for d in range(1, len(grid)):
                first = jnp.logical_and(first, pl.program_id(d) == 0)
                last = jnp.logical_and(last, pl.program_id(d) == grid[d] - 1)
            pl.when(first)(lambda: run(0))
        body(*c_in, *c_out, *c_scr)
        if jobs:
            pl.when(last)(lambda: run(1))

    sems = [pltpu.SemaphoreType.DMA((n_sem,)), pltpu.SemaphoreType.DMA((n_sem,))] if jobs else []
    kwargs = dict(grid=grid) if grid else {}
    res = pl.pallas_call(
        wrapped, name=name, in_specs=list(in_specs) + [ANY] * len(job_in),
        out_specs=list(out_specs) + [ANY] * len(job_out),
        out_shape=list(out_shape) + job_out, scratch_shapes=list(scratch) + sems,
        input_output_aliases=aliases, compiler_params=_params(len(grid)), **kwargs,
    )(*args, *job_in)
    outs, pos, per_job = list(res[:n_out]), n_out, []
    for jb in jobs:
        k = len(jb.inouts) + len(jb.outs)
        per_job.append(list(res[pos:pos + k]))
        pos += k
    return outs, per_job


def _comm(name, jobs):
    return _call(None, name, (), [], [], [], [], jobs)[1]


def _fwd_in(x, g_mix, w_in, bre, bim, jobs=()):
    t_len = x.shape[0]
    cs = IN_COLS // N_CHIP

    def body(x_ref, g_ref, w_ref, bre_ref, bim_ref, p_ref, h_ref, bur_ref, bui_ref):
        xv = x_ref[...]
        r, xh = _rms_stats(xv)
        h = (xh * g_ref[...]).astype(BF16)
        h_ref[...] = h
        for k in range(N_CHIP):
            p_ref[:, k * cs:(k + 1) * cs] = jnp.dot(h, w_ref[k], preferred_element_type=F32)
        u = p_ref[:, 0:SSM_W].astype(BF16)
        bur_ref[...] = jnp.dot(u, bre_ref[...], preferred_element_type=F32)
        bui_ref[...] = jnp.dot(u, bim_ref[...], preferred_element_type=F32)

    return _call(
        body, "fwd_in", (t_len // TB,),
        [_rows(TB, D_MODEL), _whole(), _whole(), _whole(), _whole()],
        [_rows(TB, IN_COLS), _rows(TB, D_MODEL), _rows(TB, N_STATE), _rows(TB, N_STATE)],
        [jax.ShapeDtypeStruct((t_len, IN_COLS), F32), jax.ShapeDtypeStruct((t_len, D_MODEL), BF16),
         jax.ShapeDtypeStruct((t_len, N_STATE), F32), jax.ShapeDtypeStruct((t_len, N_STATE), F32)],
        [x, g_mix, w_in, bre, bim], jobs)


def _scan_local(xr, xi, tab, shifts):
    for q, s in enumerate(shifts):
        ar, ai = tab[2 * q], tab[2 * q + 1]
        rr = pltpu.roll(xr, s, 0)
        ri = pltpu.roll(xi, s, 0)
        xr, xi = xr + ar * rr - ai * ri, xi + ar * ri + ai * rr
    return xr, xi


def _scan_carry(xr, xi, tab, cr, ci):
    pr, pi = tab[6], tab[7]
    return xr + pr * cr - pi * ci, xi + pr * ci + pi * cr


def _scan_fwd(bur, bui, tab, jobs=()):
    t_len = bur.shape[0]
    nblk = t_len // SUBLANE
    lb = SCAN_LANES

    def body(br_ref, bi_ref, tab_ref, sr_ref, si_ref):
        tab_v = [tab_ref[q] for q in range(8)]

        def step(k, carry):
            cr, ci = carry
            rows = [pl.multiple_of((k * SCAN_UNROLL + u) * SUBLANE, SUBLANE)
                    for u in range(SCAN_UNROLL)]
            local = [_scan_local(br_ref[pl.ds(r0, SUBLANE), :], bi_ref[pl.ds(r0, SUBLANE), :],
                                 tab_v, (1, 2, 4)) for r0 in rows]
            for r0, (xr, xi) in zip(rows, local):
                xr, xi = _scan_carry(xr, xi, tab_v, cr, ci)
                sr_ref[pl.ds(r0, SUBLANE), :] = xr
                si_ref[pl.ds(r0, SUBLANE), :] = xi
                cr, ci = xr[SUBLANE - 1:SUBLANE, :], xi[SUBLANE - 1:SUBLANE, :]
            return cr, ci

        zero = jnp.zeros((1, lb), F32)
        lax.fori_loop(0, nblk // SCAN_UNROLL, step, (zero, zero))

    col = pl.BlockSpec((t_len, lb), lambda j: (0, j))
    return _call(
        body, "scan_fwd", (N_STATE // lb,),
        [col, col, pl.BlockSpec((8, SUBLANE, lb), lambda j: (0, 0, j))], [col, col],
        [jax.ShapeDtypeStruct((t_len, N_STATE), F32)] * 2, [bur, bui, tab], jobs)


def _sgu_mix(v, ws_ref, lane_lo):
    rows = []
    for c0 in range(0, v.shape[0], CHUNK):
        slabs = []
        for j in range(SGU_W // LANE):
            prod = jnp.dot(ws_ref[j], v[c0:c0 + CHUNK, j * LANE:(j + 1) * LANE].astype(BF16),
                           preferred_element_type=F32)
            slabs.append(jnp.where(lane_lo, prod[:CHUNK], prod[CHUNK:]))
        rows.append(jnp.concatenate(slabs, axis=1))
    return jnp.concatenate(rows, axis=0) if len(rows) > 1 else rows[0]


def _fwd_mix(x, p, str_, sti, cre, cim, d_skip, w_glu, b_glu, w_pa, g_sgu, ws_st, bmat, w_pb, w_out,
             jobs=()):
    t_len = x.shape[0]
    pc = D_MODEL // N_CHIP

    def body(x_ref, p_ref, sr_ref, si_ref, cre_ref, cim_ref, dsk_ref, wg_ref, bg_ref, wpa_ref,
             gs_ref, ws_ref, bm_ref, wpb_ref, wo_ref,
             x2_ref, y0_ref, z_ref, mx_ref, ya_ref, yb_ref):
        u = p_ref[:, 0:SSM_W]
        y0 = _dot(sr_ref[...], cre_ref[...]) - _dot(si_ref[...], cim_ref[...]) + dsk_ref[...] * u
        y0_ref[...] = y0
        y1 = _gelu(y0)
        z = _dot(y1, wg_ref[...]) + bg_ref[...]
        z_ref[...] = z
        ya_pre = (y1 * _sigmoid(z)).astype(BF16)
        for k in range(N_CHIP):
            ya_ref[:, k * pc:(k + 1) * pc] = jnp.dot(ya_pre, wpa_ref[k], preferred_element_type=F32)

        uvg = _gelu(p_ref[:, SSM_W:SSM_W + 2 * SGU_W])
        u2 = uvg[:, :SGU_W]
        _, vh = _rms_stats(uvg[:, SGU_W:])
        v3 = vh * gs_ref[...]
        lane_lo = lax.broadcasted_iota(jnp.int32, (CHUNK, LANE), 1) < SGU_D
        bias = jnp.concatenate([bm_ref[...]] * (TB // CHUNK), axis=0)
        mixed = _sgu_mix(v3, ws_ref, lane_lo) + bias
        mx_ref[...] = mixed
        sgu = (u2 * mixed).astype(BF16)
        for k in range(N_CHIP):
            yb_ref[:, k * pc:(k + 1) * pc] = jnp.dot(sgu, wpb_ref[k], preferred_element_type=F32)

        lg0 = SSM_W + 2 * SGU_W
        ga = _sigmoid(p_ref[:, lg0:lg0 + D_MODEL])
        gb = _sigmoid(p_ref[:, lg0 + D_MODEL:lg0 + 2 * D_MODEL])
        mrg = ga * ya_ref[...] + gb * yb_ref[...]
        x2_ref[...] = x_ref[...] + _dot(mrg, wo_ref[...])

    return _call(
        body, "fwd_mix", (t_len // TB,),
        [_rows(TB, D_MODEL), _rows(TB, IN_COLS), _rows(TB, N_STATE), _rows(TB, N_STATE)]
        + [_whole()] * 11,
        [_rows(TB, D_MODEL), _rows(TB, SSM_W), _rows(TB, SSM_W), _rows(TB, SGU_W),
         _rows(TB, D_MODEL), _rows(TB, D_MODEL)],
        [jax.ShapeDtypeStruct((t_len, D_MODEL), F32), jax.ShapeDtypeStruct((t_len, SSM_W), F32),
         jax.ShapeDtypeStruct((t_len, SSM_W), F32), jax.ShapeDtypeStruct((t_len, SGU_W), F32),
         jax.ShapeDtypeStruct((t_len, D_MODEL), F32), jax.ShapeDtypeStruct((t_len, D_MODEL), F32)],
        [x, p, str_, sti, cre, cim, d_skip, w_glu, b_glu, w_pa, g_sgu, ws_st, bmat, w_pb, w_out], jobs)


def _conv_taps(v, cw_ref, c0, width):
    w0 = cw_ref[0:1, c0:c0 + width]
    w1 = cw_ref[1:2, c0:c0 + width]
    w2 = cw_ref[2:3, c0:c0 + width]
    return w0 * pltpu.roll(v, 2, 0) + w1 * pltpu.roll(v, 1, 0) + w2 * v


def _fwd_ffn(x2, target, g_ffn, w_up, conv_w, conv_b, w_down, g_final):
    t_len = x2.shape[0]
    half = D_FF // 2
    blocks_per_halo = TB // HALO

    def body(x2_ref, xp_ref, tg_ref, gf_ref, wu_ref, cw_ref, cb_ref, wd_ref, gl_ref,
             up_ref, f_ref, h2_ref, dx3_ref, sm_ref):
        i = pl.program_id(0)
        xe = jnp.concatenate([xp_ref[...] * jnp.where(i == 0, 0.0, 1.0), x2_ref[...]], axis=0)
        _, xh = _rms_stats(xe)
        h2 = (xh * gf_ref[...]).astype(BF16)
        h2_ref[...] = h2[HALO:]
        acc = jnp.zeros((TB, D_MODEL), F32)
        for hc in range(2):
            ca = hc * half
            cb = D_FF + hc * half
            ua = jnp.dot(h2, wu_ref[hc], preferred_element_type=F32)
            ub = jnp.dot(h2, wu_ref[2 + hc], preferred_element_type=F32)
            up_ref[:, ca:ca + half] = ua[HALO:]
            up_ref[:, cb:cb + half] = ub[HALO:]
            ac = _conv_taps(ua, cw_ref, ca, half)[HALO:] + cb_ref[:, ca:ca + half]
            bc = _conv_taps(ub, cw_ref, cb, half)[HALO:] + cb_ref[:, cb:cb + half]
            f = (ac * _sigmoid(ac) * bc).astype(BF16)
            f_ref[:, ca:ca + half] = f
            acc = acc + jnp.dot(f, wd_ref[ca:ca + half, :], preferred_element_type=F32)
        x3 = x2_ref[...] + acc
        r3, xh3 = _rms_stats(x3)
        err = xh3 * gl_ref[...] - tg_ref[...]
        dout = err * (1.0 / D_MODEL)
        dx3_ref[...] = _rms_bwd(dout * gl_ref[...], xh3, r3)
        dgl = jnp.sum(dout * xh3, axis=0, keepdims=True)
        loss = 0.5 * jnp.sum(jnp.mean(err * err, axis=-1, keepdims=True), axis=0, keepdims=True)
        upd = jnp.concatenate([dgl, jnp.broadcast_to(loss, (1, D_MODEL)),
                               jnp.zeros((SUBLANE - 2, D_MODEL), F32)], axis=0)

        @pl.when(i == 0)
        def _():
            sm_ref[...] = upd

        @pl.when(i > 0)
        def _():
            sm_ref[...] += upd

    prev = pl.BlockSpec((HALO, D_MODEL), lambda i: (jnp.maximum(i * blocks_per_halo - 1, 0), 0))
    return _call(
        body, "fwd_ffn", (t_len // TB,),
        [_rows(TB, D_MODEL), prev, _rows(TB, D_MODEL)] + [_whole()] * 6,
        [_rows(TB, 2 * D_FF), _rows(TB, D_FF), _rows(TB, D_MODEL), _rows(TB, D_MODEL),
         _acc(SUBLANE, D_MODEL)],
        [jax.ShapeDtypeStruct((t_len, 2 * D_FF), F32), jax.ShapeDtypeStruct((t_len, D_FF), BF16),
         jax.ShapeDtypeStruct((t_len, D_MODEL), BF16), jax.ShapeDtypeStruct((t_len, D_MODEL), F32),
         jax.ShapeDtypeStruct((SUBLANE, D_MODEL), F32)],
        [x2, x2, target, g_ffn, w_up, conv_w, conv_b, w_down, g_final])[0]


def _bwd_ffn(dx3, up, x2, g_ffn, w_up, conv_w, conv_b, w_down):
    t_len = x2.shape[0]
    half = D_FF // 2
    nblk = t_len // TB
    bph = TB // HALO
    n_halo = t_len // HALO

    def body(dx_ref, dxn_ref, up_ref, upp_ref, upn_ref, x2_ref, gf_ref, wu_ref, cw_ref, cb_ref,
             wd_ref, dx2_ref, dup_ref, smw_ref, smg_ref):
        i = pl.program_id(0)
        keep_first = jnp.where(i == 0, 0.0, 1.0)
        keep_last = jnp.where(i == nblk - 1, 0.0, 1.0)
        dxe = jnp.concatenate([dx_ref[...], dxn_ref[...] * keep_last], axis=0).astype(BF16)
        dh2 = jnp.zeros((TB, D_MODEL), F32)
        zpad = jnp.zeros((1, half), F32)
        for hc in range(2):
            ca = hc * half
            cb = D_FF + hc * half
            uea = jnp.concatenate([upp_ref[:, ca:ca + half] * keep_first, up_ref[:, ca:ca + half],
                                   upn_ref[:, ca:ca + half]], axis=0)
            ueb = jnp.concatenate([upp_ref[:, cb:cb + half] * keep_first, up_ref[:, cb:cb + half],
                                   upn_ref[:, cb:cb + half]], axis=0)
            ua1, ua2 = pltpu.roll(uea, 1, 0), pltpu.roll(uea, 2, 0)
            ub1, ub2 = pltpu.roll(ueb, 1, 0), pltpu.roll(ueb, 2, 0)
            wa = [cw_ref[k:k + 1, ca:ca + half] for k in range(3)]
            wb = [cw_ref[k:k + 1, cb:cb + half] for k in range(3)]
            ac = (wa[0] * ua2 + wa[1] * ua1 + wa[2] * uea)[HALO:] + cb_ref[:, ca:ca + half]
            bc = (wb[0] * ub2 + wb[1] * ub1 + wb[2] * ueb)[HALO:] + cb_ref[:, cb:cb + half]
            df = lax.dot_general(dxe, wd_ref[ca:ca + half, :], (((1,), (1,)), ((), ())),
                                 preferred_element_type=F32)
            sg = _sigmoid(ac)
            da = df * bc * sg * (1.0 + ac * (1.0 - sg))
            db = df * ac * sg
            n_e = TB + HALO
            dua = (wa[2] * da + wa[1] * pltpu.roll(da, n_e - 1, 0)
                   + wa[0] * pltpu.roll(da, n_e - 2, 0))[:TB]
            dub = (wb[2] * db + wb[1] * pltpu.roll(db, n_e - 1, 0)
                   + wb[0] * pltpu.roll(db, n_e - 2, 0))[:TB]
            dup_ref[:, ca:ca + half] = dua.astype(BF16)
            dup_ref[:, cb:cb + half] = dub.astype(BF16)
            dh2 = dh2 + _dot_nt(dua, wu_ref[hc]) + _dot_nt(dub, wu_ref[2 + hc])
            dab = da[:TB]
            dbb = db[:TB]
            rows = []
            for d_, u0, u1, u2 in ((dab, ua2, ua1, uea), (dbb, ub2, ub1, ueb)):
                rows.append([jnp.sum(d_ * u0[HALO:HALO + TB], axis=0, keepdims=True),
                             jnp.sum(d_ * u1[HALO:HALO + TB], axis=0, keepdims=True),
                             jnp.sum(d_ * u2[HALO:HALO + TB], axis=0, keepdims=True),
                             jnp.sum(d_, axis=0, keepdims=True)])
            for c0, rws in ((ca, rows[0]), (cb, rows[1])):
                upd = jnp.concatenate(rws + [zpad] * (SUBLANE - 4), axis=0)

                @pl.when(i == 0)
                def _(upd=upd, c0=c0):
                    smw_ref[:, c0:c0 + half] = upd

                @pl.when(i > 0)
                def _(upd=upd, c0=c0):
                    smw_ref[:, c0:c0 + half] += upd

        r2, xh2 = _rms_stats(x2_ref[...])
        dx2_ref[...] = dx_ref[...] + _rms_bwd(dh2 * gf_ref[...], xh2, r2)
        updg = jnp.concatenate([jnp.sum(dh2 * xh2, axis=0, keepdims=True),
                                jnp.zeros((SUBLANE - 1, D_MODEL), F32)], axis=0)

        @pl.when(i == 0)
        def _():
            smg_ref[...] = updg

        @pl.when(i > 0)
        def _():
            smg_ref[...] += updg

    nxt_d = pl.BlockSpec((HALO, D_MODEL), lambda i: (jnp.minimum((i + 1) * bph, n_halo - 1), 0))
    prv_u = pl.BlockSpec((HALO, 2 * D_FF), lambda i: (jnp.maximum(i * bph - 1, 0), 0))
    nxt_u = pl.BlockSpec((HALO, 2 * D_FF), lambda i: (jnp.minimum((i + 1) * bph, n_halo - 1), 0))
    return _call(
        body, "bwd_ffn", (nblk,),
        [_rows(TB, D_MODEL), nxt_d, _rows(TB, 2 * D_FF), prv_u, nxt_u, _rows(TB, D_MODEL)]
        + [_whole()] * 5,
        [_rows(TB, D_MODEL), _rows(TB, 2 * D_FF), _acc(SUBLANE, 2 * D_FF), _acc(SUBLANE, D_MODEL)],
        [jax.ShapeDtypeStruct((t_len, D_MODEL), F32), jax.ShapeDtypeStruct((t_len, 2 * D_FF), BF16),
         jax.ShapeDtypeStruct((SUBLANE, 2 * D_FF), F32), jax.ShapeDtypeStruct((SUBLANE, D_MODEL), F32)],
        [dx3, dx3, up, up, up, x2, g_ffn, w_up, conv_w, conv_b, w_down])[0]


def _bwd_mix(dx2, p, y0, z, mixed, ya, yb, w_out, w_pa, w_pb, w_glu, cre, cim, ws_st, wst_st,
             d_skip, g_sgu, jobs=()):
    t_len = dx2.shape[0]
    pc = D_MODEL // N_CHIP
    n_slab = SGU_W // LANE

    def body(dx_ref, p_ref, y0_ref, z_ref, mx_ref, ya_ref, yb_ref, wo_ref, wpa_ref, wpb_ref,
             wg_ref, cre_ref, cim_ref, ws_ref, wst_ref, dsk_ref, gs_ref,
             dsr_ref, dsi_ref, du_ref, drest_ref, mrg_ref, dya_ref, dyb_ref, yap_ref, dz_ref,
             y1_ref, sgu_ref, dy0_ref, sm_ref, dbm_ref, dws_ref):
        i = pl.program_id(0)
        first = i == 0
        lg0 = SSM_W + 2 * SGU_W
        dmrg = _dot_nt(dx_ref[...], wo_ref[...])
        ga = _sigmoid(p_ref[:, lg0:lg0 + D_MODEL])
        gb = _sigmoid(p_ref[:, lg0 + D_MODEL:lg0 + 2 * D_MODEL])
        yav = ya_ref[...]
        ybv = yb_ref[...]
        mrg_ref[...] = (ga * yav + gb * ybv).astype(BF16)
        drest_ref[:, 2 * SGU_W:2 * SGU_W + D_MODEL] = (dmrg * yav * ga * (1.0 - ga)).astype(BF16)
        drest_ref[:, 2 * SGU_W + D_MODEL:] = (dmrg * ybv * gb * (1.0 - gb)).astype(BF16)
        dya = (dmrg * ga).astype(BF16)
        dyb = (dmrg * gb).astype(BF16)
        dya_ref[...] = dya
        dyb_ref[...] = dyb

        y0v = y0_ref[...]
        y1 = _gelu(y0v)
        sz = _sigmoid(z_ref[...])
        y1_ref[...] = y1.astype(BF16)
        yap_ref[...] = (y1 * sz).astype(BF16)
        dyap = jnp.zeros((TB, SSM_W), F32)
        for k in range(N_CHIP):
            dyap = dyap + _dot_nt(dya[:, k * pc:(k + 1) * pc], wpa_ref[k])
        dz = dyap * y1 * sz * (1.0 - sz)
        dz_ref[...] = dz.astype(BF16)
        dy0 = (dyap * sz + _dot_nt(dz, wg_ref[...])) * _gelu_grad(y0v)
        dy0_ref[...] = dy0.astype(BF16)
        u = p_ref[:, 0:SSM_W]
        du_ref[...] = dy0 * dsk_ref[...]
        dsr_ref[...] = _dot_nt(dy0, cre_ref[...])
        dsi_ref[...] = -_dot_nt(dy0, cim_ref[...])

        uv = p_ref[:, SSM_W:lg0]
        uvg = _gelu(uv)
        u2 = uvg[:, :SGU_W]
        rv, vh = _rms_stats(uvg[:, SGU_W:])
        v3 = vh * gs_ref[...]
        mixed = mx_ref[...]
        dsgu = jnp.zeros((TB, SGU_W), F32)
        for k in range(N_CHIP):
            dsgu = dsgu + _dot_nt(dyb[:, k * pc:(k + 1) * pc], wpb_ref[k])
        sgu_ref[...] = (u2 * mixed).astype(BF16)
        du2 = dsgu * mixed
        dmix = dsgu * u2
        lane_lo = lax.broadcasted_iota(jnp.int32, (CHUNK, LANE), 1) < SGU_D
        dv3 = _sgu_mix(dmix, wst_ref, lane_lo)
        dbm = jnp.zeros((CHUNK, SGU_W), F32)
        for c0 in range(0, TB, CHUNK):
            dbm = dbm + dmix[c0:c0 + CHUNK]
        for j in range(n_slab):
            lo = jnp.zeros((CHUNK, CHUNK), F32)
            hi = jnp.zeros((CHUNK, CHUNK), F32)
            for c0 in range(0, TB, CHUNK):
                dsl = dmix[c0:c0 + CHUNK, j * LANE:(j + 1) * LANE]
                vsl = v3[c0:c0 + CHUNK, j * LANE:(j + 1) * LANE]
                lo = lo + _dot_nt(jnp.where(lane_lo, dsl, 0.0), vsl)
                hi = hi + _dot_nt(jnp.where(lane_lo, 0.0, dsl), vsl)

            @pl.when(first)
            def _(lo=lo, hi=hi, j=j):
                dws_ref[2 * j] = lo
                dws_ref[2 * j + 1] = hi

            @pl.when(jnp.logical_not(first))
            def _(lo=lo, hi=hi, j=j):
                dws_ref[2 * j] += lo
                dws_ref[2 * j + 1] += hi

        dv2 = _rms_bwd(dv3 * gs_ref[...], vh, rv)
        gg = _gelu_grad(uv)
        drest_ref[:, 0:SGU_W] = (du2 * gg[:, :SGU_W]).astype(BF16)
        drest_ref[:, SGU_W:2 * SGU_W] = (dv2 * gg[:, SGU_W:]).astype(BF16)

        upd = jnp.concatenate([jnp.sum(dy0 * u, axis=0, keepdims=True),
                               jnp.sum(dz, axis=0, keepdims=True),
                               jnp.sum(dv3 * vh, axis=0, keepdims=True),
                               jnp.zeros((SUBLANE - 3, SSM_W), F32)], axis=0)

        @pl.when(first)
        def _():
            sm_ref[...] = upd
            dbm_ref[...] = dbm

        @pl.when(jnp.logical_not(first))
        def _():
            sm_ref[...] += upd
            dbm_ref[...] += dbm

    rest = 2 * SGU_W + 2 * D_MODEL
    bf_d, bf_s = jax.ShapeDtypeStruct((t_len, D_MODEL), BF16), jax.ShapeDtypeStruct((t_len, SSM_W), BF16)
    return _call(
        body, "bwd_mix", (t_len // TB,),
        [_rows(TB, D_MODEL), _rows(TB, IN_COLS), _rows(TB, SSM_W), _rows(TB, SSM_W),
         _rows(TB, SGU_W), _rows(TB, D_MODEL), _rows(TB, D_MODEL)] + [_whole()] * 10,
        [_rows(TB, N_STATE), _rows(TB, N_STATE), _rows(TB, SSM_W), _rows(TB, rest),
         _rows(TB, D_MODEL), _rows(TB, D_MODEL), _rows(TB, D_MODEL), _rows(TB, SSM_W),
         _rows(TB, SSM_W), _rows(TB, SSM_W), _rows(TB, SGU_W), _rows(TB, SSM_W),
         _acc(SUBLANE, SSM_W), _acc(CHUNK, SGU_W),
         pl.BlockSpec((SGU_G, CHUNK, CHUNK), lambda i: (0, 0, 0))],
        [jax.ShapeDtypeStruct((t_len, N_STATE), F32), jax.ShapeDtypeStruct((t_len, N_STATE), F32),
         jax.ShapeDtypeStruct((t_len, SSM_W), F32), jax.ShapeDtypeStruct((t_len, rest), BF16),
         bf_d, bf_d, bf_d, bf_s, bf_s, bf_s, bf_s, bf_s,
         jax.ShapeDtypeStruct((SUBLANE, SSM_W), F32), jax.ShapeDtypeStruct((CHUNK, SGU_W), F32),
         jax.ShapeDtypeStruct((SGU_G, CHUNK, CHUNK), F32)],
        [dx2, p, y0, z, mixed, ya, yb, w_out, w_pa, w_pb, w_glu, cre, cim, ws_st, wst_st, d_skip,
         g_sgu], jobs)


def _scan_bwd(dsr, dsi, str_, sti, tab_rev, jobs=()):
    t_len = dsr.shape[0]
    nblk = t_len // SUBLANE
    lb = SCAN_LANES

    def body(dr_ref, di_ref, sr_ref, si_ref, tab_ref, lr_ref, li_ref, dar_ref, dai_ref):
        tab_v = [tab_ref[q] for q in range(8)]
        row0 = lax.broadcasted_iota(jnp.int32, (SUBLANE, lb), 0) == 0

        def step(k, carry):
            cr, ci, acr, aci = carry
            blocks = [nblk - 1 - (k * SCAN_UNROLL + u) for u in range(SCAN_UNROLL)]
            rows = [pl.multiple_of(kk * SUBLANE, SUBLANE) for kk in blocks]
            local = [_scan_local(dr_ref[pl.ds(r0, SUBLANE), :], di_ref[pl.ds(r0, SUBLANE), :],
                                 tab_v, (7, 6, 4)) for r0 in rows]
            for kk, r0, (xr, xi) in zip(blocks, rows, local):
                xr, xi = _scan_carry(xr, xi, tab_v, cr, ci)
                lr_ref[pl.ds(r0, SUBLANE), :] = xr
                li_ref[pl.ds(r0, SUBLANE), :] = xi
                cr, ci = xr[0:1, :], xi[0:1, :]
                rp = pl.multiple_of(jnp.maximum(kk - 1, 0) * SUBLANE, SUBLANE)
                has_prev = jnp.where(kk > 0, 1.0, 0.0)
                pr = sr_ref[pl.ds(rp, SUBLANE), :][SUBLANE - 1:SUBLANE, :] * has_prev
                pi = si_ref[pl.ds(rp, SUBLANE), :][SUBLANE - 1:SUBLANE, :] * has_prev
                s_r = jnp.where(row0, pr, pltpu.roll(sr_ref[pl.ds(r0, SUBLANE), :], 1, 0))
                s_i = jnp.where(row0, pi, pltpu.roll(si_ref[pl.ds(r0, SUBLANE), :], 1, 0))
                acr = acr + xr * s_r + xi * s_i
                aci = aci + xi * s_r - xr * s_i
            return cr, ci, acr, aci

        zero = jnp.zeros((1, lb), F32)
        zacc = jnp.zeros((SUBLANE, lb), F32)
        _, _, acr, aci = lax.fori_loop(0, nblk // SCAN_UNROLL, step, (zero, zero, zacc, zacc))
        dar_ref[...] = acr
        dai_ref[...] = aci

    col = pl.BlockSpec((t_len, lb), lambda j: (0, j))
    small = pl.BlockSpec((SUBLANE, lb), lambda j: (0, j))
    return _call(
        body, "scan_bwd", (N_STATE // lb,),
        [col, col, col, col, pl.BlockSpec((8, SUBLANE, lb), lambda j: (0, 0, j))],
        [col, col, small, small],
        [jax.ShapeDtypeStruct((t_len, N_STATE), F32)] * 2
        + [jax.ShapeDtypeStruct((SUBLANE, N_STATE), F32)] * 2,
        [dsr, dsi, str_, sti, tab_rev], jobs)


def _bwd_in(lam_r, lam_i, du_part, drest, x, dx2, g_mix, w_in, bre, bim, jobs=()):
    t_len = x.shape[0]
    cs = IN_COLS // N_CHIP

    def body(lr_ref, li_ref, du_ref, dr_ref, x_ref, dx2_ref, g_ref, w_ref, bre_ref, bim_ref,
             gx_ref, dp_ref, sm_ref):
        i = pl.program_id(0)
        du = du_ref[...] + _dot_nt(lr_ref[...], bre_ref[...]) + _dot_nt(li_ref[...], bim_ref[...])
        dp_ref[:, 0:SSM_W] = du.astype(BF16)
        dp_ref[:, SSM_W:] = dr_ref[...]
        dh = jnp.zeros((TB, D_MODEL), F32)
        for k in range(N_CHIP):
            dh = dh + _dot_nt(dp_ref[:, k * cs:(k + 1) * cs], w_ref[k])
        r, xh = _rms_stats(x_ref[...])
        gx_ref[...] = dx2_ref[...] + _rms_bwd(dh * g_ref[...], xh, r)
        upd = jnp.concatenate([jnp.sum(dh * xh, axis=0, keepdims=True),
                               jnp.zeros((SUBLANE - 1, D_MODEL), F32)], axis=0)

        @pl.when(i == 0)
        def _():
            sm_ref[...] = upd

        @pl.when(i > 0)
        def _():
            sm_ref[...] += upd

    return _call(
        body, "bwd_in", (t_len // TB,),
        [_rows(TB, N_STATE), _rows(TB, N_STATE), _rows(TB, SSM_W), _rows(TB, IN_COLS - SSM_W),
         _rows(TB, D_MODEL), _rows(TB, D_MODEL)] + [_whole()] * 4,
        [_rows(TB, D_MODEL), _rows(TB, IN_COLS), _acc(SUBLANE, D_MODEL)],
        [jax.ShapeDtypeStruct((t_len, D_MODEL), F32), jax.ShapeDtypeStruct((t_len, IN_COLS), BF16),
         jax.ShapeDtypeStruct((SUBLANE, D_MODEL), F32)],
        [lam_r, lam_i, du_part, drest, x, dx2, g_mix, w_in, bre, bim], jobs)


def _matmul_tn(a, b, name, out_shape, grid_ij, a_blk, a_map, b_blk, b_map, o_blk, o_map, jobs=()):
    tk = a_blk[0]
    nk = a.shape[0] // tk
    assert nk * tk == a.shape[0] and nk > 0

    def body(a_ref, b_ref, o_ref, acc_ref):
        k = pl.program_id(2)

        @pl.when(k == 0)
        def _():
            acc_ref[...] = jnp.zeros_like(acc_ref)

        acc_ref[...] += lax.dot_general(a_ref[...].astype(BF16), b_ref[...].astype(BF16),
                                        (((0,), (0,)), ((), ())), preferred_element_type=F32)

        @pl.when(k == nk - 1)
        def _():
            o_ref[...] = acc_ref[...]

    outs, per_job = _call(
        body, name, (grid_ij[0], grid_ij[1], nk),
        [pl.BlockSpec(a_blk, a_map), pl.BlockSpec(b_blk, b_map)], [pl.BlockSpec(o_blk, o_map)],
        [jax.ShapeDtypeStruct(out_shape, F32)], [a, b], jobs,
        scratch=[pltpu.VMEM((a_blk[1], b_blk[1]), F32)])
    return outs[0], per_job


def _dw_shards(a, b, name):
    m, n = a.shape[1], b.shape[1]
    tn = n // N_CHIP
    tk = min(TK, a.shape[0])
    return _matmul_tn(a, b, name, (N_CHIP, m, tn), (1, N_CHIP),
                      (tk, m), lambda i, j, k: (k, 0), (tk, tn), lambda i, j, k: (k, j),
                      (None, m, tn), lambda i, j, k: (j, 0, 0))[0]


def _dw_rows(a, b, name, tm):
    m, n = a.shape[1], b.shape[1]
    tk = min(TK, a.shape[0])
    return _matmul_tn(a, b, name, (m, n), (m // tm, 1),
                      (tk, tm), lambda i, j, k: (k, i), (tk, n), lambda i, j, k: (k, 0),
                      (tm, n), lambda i, j, k: (i, 0))[0]


def _dw_full(a, b, name, n_shard):
    m, n = a.shape[1], b.shape[1]
    tn = n // n_shard
    tk = min(TK, a.shape[0])

    def body(a_ref, b_ref, o_ref):
        k = pl.program_id(0)
        a_t = a_ref[...].astype(BF16).T
        for j in range(n_shard):
            piece = jnp.dot(a_t, b_ref[:, j * tn:(j + 1) * tn].astype(BF16),
                            preferred_element_type=F32)

            @pl.when(k == 0)
            def _(piece=piece, j=j):
                o_ref[j] = piece

            @pl.when(k > 0)
            def _(piece=piece, j=j):
                o_ref[j] += piece

    return _call(body, name, (a.shape[0] // tk,),
                 [pl.BlockSpec((tk, m), lambda k: (k, 0)), pl.BlockSpec((tk, n), lambda k: (k, 0))],
                 [pl.BlockSpec((n_shard, m, tn), lambda k: (0, 0, 0))],
                 [jax.ShapeDtypeStruct((n_shard, m, tn), F32)], [a, b])[0][0]


def _dw_diag(a, b, name, ta, tb_, jobs=()):
    n_t = b.shape[1] // tb_
    tk = min(TK, a.shape[0])
    return _matmul_tn(a, b, name, (n_t * ta, tb_), (n_t, 1),
                      (tk, ta), lambda i, j, k: (k, i), (tk, tb_), lambda i, j, k: (k, i),
                      (ta, tb_), lambda i, j, k: (i, 0), jobs)


def _prefetch_call(body, name, grid, scalars, in_specs, out_specs, out_shape, args):
    return pl.pallas_call(
        body, name=name,
        grid_spec=pltpu.PrefetchScalarGridSpec(num_scalar_prefetch=1, grid=grid, in_specs=in_specs,
                                               out_specs=out_specs),
        out_shape=out_shape, compiler_params=_params(len(grid)),
    )(scalars, *args)


def _place_shard(w, where, name, dtype, tr):
    rows, cols = w.shape

    def body(s_ref, w_ref, o_ref):
        o_ref[...] = w_ref[...].astype(dtype)

    return _prefetch_call(
        body, name, (rows // tr,), where,
        [pl.BlockSpec((tr, cols), lambda i, s: (i, 0))],
        pl.BlockSpec((None, tr, cols), lambda i, s: (s[0], i, 0)),
        jax.ShapeDtypeStruct((N_CHIP, rows, cols), dtype), [w])


def _add_sibling(g, got, where, name):
    _, rs, cs = g.shape
    hr = rs // 2

    def body(s_ref, g_ref, got_ref, o_ref):
        o_ref[...] = (g_ref[...] + got_ref[...]).astype(BF16)

    return _prefetch_call(
        body, name, (N_CHIP,), where,
        [pl.BlockSpec((None, hr, cs), lambda k, s: (k, s[1], 0)),
         pl.BlockSpec((None, hr, cs), lambda k, s: (k, 0, 0))],
        pl.BlockSpec((None, hr, cs), lambda k, s: (k, 0, 0)),
        jax.ShapeDtypeStruct((N_CHIP, hr, cs), BF16), [g, got])


def _add_chips(sums, got, where, name):
    _, hr, cs = sums.shape

    def body(s_ref, own_ref, got_ref, o_ref):
        o_ref[...] = ((own_ref[...].astype(F32) + got_ref[0].astype(F32))
                      + got_ref[1].astype(F32)) + got_ref[2].astype(F32)

    return _prefetch_call(
        body, name, (1,), where,
        [pl.BlockSpec((None, hr, cs), lambda i, s: (s[0], 0, 0)),
         pl.BlockSpec((3, hr, cs), lambda i, s: (0, 0, 0))],
        pl.BlockSpec((hr, cs), lambda i, s: (s[1], 0)),
        jax.ShapeDtypeStruct((2 * hr, cs), F32), [sums, got])


def _small_allreduce(pack):
    rows = pack.shape[0]
    half = rows // 2

    def body(in_ref, out_ref, sib_ref, slots_ref, s_a, r_a, s_b, r_b, s_c, r_c):
        x, y, c, chips = _place()
        k_me = 2 * x + y
        sib = (x, y, 1 - c)
        first = _remote(in_ref, sib_ref, s_a, r_a, sib)
        first.start()
        first.wait()
        mine = _half(rows, c)
        slots_ref[k_me] = in_ref[mine, :] + sib_ref[mine, :]
        cps = [_remote(slots_ref.at[k_me], slots_ref.at[k_me], s_b.at[j], r_b.at[j], (*ch, c))
               for j, ch in enumerate(chips)]
        for cp in cps:
            cp.start()
        for j, ch in enumerate(chips):
            slot = slots_ref.at[_chip_index(ch)]
            _remote(slot, slot, s_b.at[j], r_b.at[j], (*ch, c)).wait_recv()
        for cp in cps:
            cp.wait_send()
        out_ref[mine, :] = ((slots_ref[0] + slots_ref[1]) + slots_ref[2]) + slots_ref[3]
        last = _remote(out_ref.at[mine, :], out_ref.at[mine, :], s_c, r_c, sib)
        last.start()
        theirs = out_ref.at[_half(rows, 1 - c), :]
        _remote(theirs, theirs, s_c, r_c, sib).wait_recv()
        last.wait_send()

    return pl.pallas_call(
        body, name="small_allreduce", in_specs=[_whole()], out_specs=_whole(),
        out_shape=jax.ShapeDtypeStruct(pack.shape, F32),
        scratch_shapes=[pltpu.VMEM(pack.shape, F32), pltpu.VMEM((N_CHIP, half, LANE), F32),
                        pltpu.SemaphoreType.DMA, pltpu.SemaphoreType.DMA,
                        pltpu.SemaphoreType.DMA((3,)), pltpu.SemaphoreType.DMA((3,)),
                        pltpu.SemaphoreType.DMA, pltpu.SemaphoreType.DMA],
        compiler_params=_params(0),
    )(pack)


def _adamw_update(w_ref, g_ref, m_ref, v_ref, d_ref, mo_ref, vo_ref):
    gv = g_ref[...]
    mn = ADAM_B1 * m_ref[...] + (1.0 - ADAM_B1) * gv
    vn = ADAM_B2 * v_ref[...] + (1.0 - ADAM_B2) * (gv * gv)
    mo_ref[...] = mn
    vo_ref[...] = vn
    m_hat = mn / (1.0 - ADAM_B1 ** ADAM_STEP)
    v_hat = vn / (1.0 - ADAM_B2 ** ADAM_STEP)
    d_ref[...] = -ADAM_LR * (m_hat / (jnp.sqrt(v_hat) + ADAM_EPS) + ADAM_WD * w_ref[...])


def _adamw(w, g, m, v, name, tr):
    rows, cols = w.shape
    blk = _rows(tr, cols)
    return _call(_adamw_update, name, (rows // tr,), [blk] * 4, [blk] * 3,
                 [jax.ShapeDtypeStruct(w.shape, F32)] * 3, [w, g, m, v])[0]


def _adamw_many(ws, gs, ms, vs, name):
    n = len(ws)

    def body(*refs):
        for t in range(n):
            _adamw_update(*[refs[q * n + t] for q in range(7)])

    outs = pl.pallas_call(
        body, name=name, in_specs=[_whole()] * (4 * n), out_specs=[_whole()] * (3 * n),
        out_shape=[jax.ShapeDtypeStruct(a.shape, F32) for _ in range(3) for a in ws],
        compiler_params=_params(0),
    )(*ws, *gs, *ms, *vs)
    return outs[:n], outs[n:2 * n], outs[2 * n:]


def _ssm_discretize(a_re, a_im, log_dt, b_re, b_im):
    dt = jnp.exp(log_dt)[:, None]
    mag = jnp.exp(dt * a_re)
    abr = mag * jnp.cos(dt * a_im)
    abi = mag * jnp.sin(dt * a_im)
    den = a_re * a_re + a_im * a_im
    nr = abr - 1.0
    ni = abi
    f_re = (nr * a_re + ni * a_im) / den
    f_im = (ni * a_re - nr * a_im) / den
    bbr = f_re[..., None] * b_re - f_im[..., None] * b_im
    bbi = f_re[..., None] * b_im + f_im[..., None] * b_re
    return abr, abi, bbr, bbi


def _scan_tables(abr, abi):
    ar = abr.reshape(1, N_STATE)
    ai = abi.reshape(1, N_STATE)
    pr, pi = [ar], [ai]
    for _ in range(SUBLANE - 1):
        pr, pi = pr + [pr[-1] * ar - pi[-1] * ai], pi + [pr[-1] * ai + pi[-1] * ar]
    row = jnp.arange(SUBLANE)[:, None]
    tabs = []
    for d in (1, 2, 4):
        tabs.append(jnp.where(row >= d, pr[d - 1], 0.0))
        tabs.append(jnp.where(row >= d, pi[d - 1], 0.0))
    tabs.append(jnp.concatenate(pr, axis=0))
    tabs.append(jnp.concatenate(pi, axis=0))
    fwd = jnp.stack(tabs)
    sign = jnp.array([1.0, -1.0] * 4, F32)[:, None, None]
    return fwd, fwd[:, ::-1, :] * sign


def _block_diag_b(bb):
    eye = jnp.eye(SSM_G, dtype=F32)
    return jnp.einsum("gph,gk->ghkp", bb, eye).reshape(SSM_W, N_STATE)


def _block_diag_c(cc):
    eye = jnp.eye(SSM_G, dtype=F32)
    return jnp.einsum("ghp,gk->gpkh", cc, eye).reshape(N_STATE, SSM_W)


def _group_onehot():
    g = jnp.arange(SSM_G)
    return (g[:, None] % 8 == jnp.arange(8)[None, :]).astype(F32)


SMALL_SHAPES = {
    "g_mix": (D_MODEL,), "a_re": (SSM_G, SSM_P), "a_im": (SSM_G, SSM_P), "log_dt": (SSM_G,),
    "b_re": (SSM_G, SSM_P, SSM_H), "b_im": (SSM_G, SSM_P, SSM_H),
    "c_re": (SSM_G, SSM_H, SSM_P), "c_im": (SSM_G, SSM_H, SSM_P),
    "d_skip": (SSM_W,), "b_glu": (SSM_W,), "g_sgu": (SGU_W,), "w_s": (SGU_G, CHUNK, CHUNK),
    "b_s": (SGU_G, CHUNK), "g_ffn": (D_MODEL,), "conv_b": (2 * D_FF,), "g_final": (D_MODEL,),
}
PACK_ITEMS = [("loss", (1,))] + [(n, SMALL_SHAPES[n]) for n in SMALL] + [("conv_w", (3, 2 * D_FF))]
TILE = SUBLANE * LANE


def _item_rows(shape):
    return -(-math.prod(shape) // TILE) * SUBLANE


PACK_ROWS = -(-sum(_item_rows(s) for _, s in PACK_ITEMS) // (2 * SUBLANE)) * (2 * SUBLANE)


def _pack(values):
    parts, used = [], 0
    for name, shape in PACK_ITEMS:
        size, rows = math.prod(shape), _item_rows(shape)
        if name in values:
            flat = values[name].astype(F32).reshape(size)
            if rows * LANE > size:
                flat = jnp.pad(flat, (0, rows * LANE - size))
            parts.append(flat.reshape(rows, LANE))
        else:
            parts.append(jnp.zeros((rows, LANE), F32))
        used += rows
    if PACK_ROWS > used:
        parts.append(jnp.zeros((PACK_ROWS - used, LANE), F32))
    return jnp.concatenate(parts, axis=0)


def _unpack(pack):
    out, off = {}, 0
    for name, shape in PACK_ITEMS:
        rows = _item_rows(shape)
        out[name] = pack[off:off + rows].reshape(rows * LANE)[:math.prod(shape)].reshape(shape)
        off += rows
    return out


PLACE_ROWS = {"w_in": 256, "w_up": 256, "w_down": 352, "w_out": 256, "w_proj_a": 256,
              "w_proj_b": 256, "w_glu": 128}


def kernel(x, g_mix, w_in, a_re, a_im, log_dt, b_re, b_im, c_re, c_im, d_skip, w_glu, b_glu, w_proj_a, g_sgu, w_s, b_s, w_proj_b, w_out, g_ffn, w_up, conv_w, conv_b, w_down, g_final, loss_target, m_g_mix, m_w_in, m_a_re, m_a_im, m_log_dt, m_b_re, m_b_im, m_c_re, m_c_im, m_d_skip, m_w_glu, m_b_glu, m_w_proj_a, m_g_sgu, m_w_s, m_b_s, m_w_proj_b, m_w_out, m_g_ffn, m_w_up, m_conv_w, m_conv_b, m_w_down, m_g_final, v_g_mix, v_w_in, v_a_re, v_a_im, v_log_dt, v_b_re, v_b_im, v_c_re, v_c_im, v_d_skip, v_w_glu, v_b_glu, v_w_proj_a, v_g_sgu, v_w_s, v_b_s, v_w_proj_b, v_w_out, v_g_ffn, v_w_up, v_conv_w, v_conv_b, v_w_down, v_g_final):
    given = dict(locals())
    w = {n: given[n] for n in WEIGHTS}
    m = {n: given["m_" + n] for n in WEIGHTS}
    v = {n: given["v_" + n] for n in WEIGHTS}

    def shard2d(a):
        return a.reshape(a.shape[-2], a.shape[-1])

    chip = 2 * lax.axis_index("x") + lax.axis_index("y")
    where = jnp.stack([chip, lax.axis_index("c")]).astype(jnp.int32)
    xs, target = x[0], loss_target[0]
    small = {n: w[n].reshape(SMALL_SHAPES[n]) for n in SMALL}

    (abr, abi, bbr, bbi), disc_vjp = jax.vjp(_ssm_discretize, small["a_re"], small["a_im"],
                                             small["log_dt"], small["b_re"], small["b_im"])
    tab_f, tab_r = _scan_tables(abr, abi)
    bre = _block_diag_b(bbr).astype(BF16)
    bim = _block_diag_b(bbi).astype(BF16)
    cre = _block_diag_c(small["c_re"]).astype(BF16)
    cim = _block_diag_c(small["c_im"]).astype(BF16)
    tril = jnp.tril(jnp.ones((CHUNK, CHUNK), dtype=bool))
    ws = jnp.where(tril[None], small["w_s"], 0.0)
    ws_st = ws.reshape(SGU_G // 2, 2 * CHUNK, CHUNK).astype(BF16)
    wst_st = ws.transpose(0, 2, 1).reshape(SGU_G // 2, 2 * CHUNK, CHUNK).astype(BF16)
    bmat = jnp.repeat(small["b_s"].T, SGU_D, axis=1)
    g_mix2 = small["g_mix"].reshape(1, D_MODEL)
    g_ffn2 = small["g_ffn"].reshape(1, D_MODEL)
    g_final2 = small["g_final"].reshape(1, D_MODEL)
    g_sgu2 = small["g_sgu"].reshape(1, SGU_W)
    d_skip2 = small["d_skip"].reshape(1, SSM_W)
    b_glu2 = small["b_glu"].reshape(1, SSM_W)
    conv_b2 = small["conv_b"].reshape(1, 2 * D_FF)

    gat = {n: _place_shard(shard2d(w[n]), where, "place_" + n, BF16, PLACE_ROWS[n]) for n in BIG}
    gat["conv_w"] = _place_shard(shard2d(w["conv_w"]), where, "place_conv_w", F32, 3)
    (gat["w_in"],), = _comm("gather_in_ici", [_job_gather_ici([gat["w_in"]])])
    (gat["w_in"],), = _comm("gather_in_sibling", [_job_gather_sibling([gat["w_in"]])])
    mixers = ["w_glu", "w_proj_a", "w_proj_b", "w_out", "conv_w"]
    ffn = ["w_up", "w_down"]

    (p, h1, bur, bui), (got,) = _fwd_in(
        xs, g_mix2, gat["w_in"], bre, bim,
        [_job_gather_ici([gat[n] for n in mixers], whole=(4,))])
    gat.update(zip(mixers, got))
    (str_, sti), (got_m, got_f) = _scan_fwd(
        bur, bui, tab_f,
        [_job_gather_sibling([gat[n] for n in mixers[:4]]), _job_gather_ici([gat[n] for n in ffn])])
    gat.update(zip(mixers[:4], got_m))
    gat.update(zip(ffn, got_f))
    w_glu_f = gat["w_glu"].reshape(SSM_W, SSM_W)
    w_out_f = gat["w_out"].reshape(D_MODEL, D_MODEL)
    conv_w_f = gat["conv_w"].transpose(1, 0, 2).reshape(3, 2 * D_FF)
    (x2, y0, z, mixed, ya, yb), (got_f,) = _fwd_mix(
        xs, p, str_, sti, cre, cim, d_skip2, w_glu_f, b_glu2, gat["w_proj_a"], g_sgu2, ws_st, bmat,
        gat["w_proj_b"], w_out_f, [_job_gather_sibling([gat[n] for n in ffn])])
    gat.update(zip(ffn, got_f))
    w_down_f = gat["w_down"].reshape(D_FF, D_MODEL)
    up, f, h2, dx3, sm_ffn = _fwd_ffn(x2, target, g_ffn2, gat["w_up"], conv_w_f, conv_b2, w_down_f,
                                      g_final2)

    dx2, dup, sm_conv, sm_gffn = _bwd_ffn(dx3, up, x2, g_ffn2, gat["w_up"], conv_w_f, conv_b2, w_down_f)
    part = {"w_up": _dw_shards(h2, dup, "dw_up"),
            "w_down": _dw_rows(f, dx3, "dw_down", D_FF // 2).reshape(N_CHIP, D_FF // N_CHIP, D_MODEL)}
    ((dsr, dsi, du_part, drest, mrg, dya, dyb, yap, dz, y1, sgu, dy0, sm_mix, dbm, dws),
     (sib_f,)) = _bwd_mix(dx2, p, y0, z, mixed, ya, yb, w_out_f, gat["w_proj_a"], gat["w_proj_b"],
                          w_glu_f, cre, cim, ws_st, wst_st, d_skip2, g_sgu2,
                          [_job_sibling_halves([part[n] for n in ffn])])
    sums_f = [_add_sibling(part[n], s, where, "add_sibling_" + n) for n, s in zip(ffn, sib_f)]
    (lam_r, lam_i, dar8, dai8), (own_f,) = _scan_bwd(dsr, dsi, str_, sti, tab_r,
                                                     [_job_to_owner(sums_f)])
    red = {n: _add_chips(s, o, where, "add_chips_" + n) for n, s, o in zip(ffn, sums_f, own_f)}
    mix4 = ["w_out", "w_proj_a", "w_proj_b", "w_glu"]
    part["w_out"] = _dw_full(mrg, dx2, "dw_out", 1).reshape(N_CHIP, D_MODEL // N_CHIP, D_MODEL)
    part["w_proj_a"] = _dw_full(yap, dya, "dw_proj_a", N_CHIP)
    part["w_proj_b"] = _dw_full(sgu, dyb, "dw_proj_b", N_CHIP)
    part["w_glu"] = _dw_full(y1, dz, "dw_glu", 1).reshape(N_CHIP, SSM_W // N_CHIP, SSM_W)
    (grad_x, dp, sm_gmix), (sib_m, done_f) = _bwd_in(
        lam_r, lam_i, du_part, drest, xs, dx2, g_mix2, gat["w_in"], bre, bim,
        [_job_sibling_halves([part[n] for n in mix4]), _job_swap_halves([red[n] for n in ffn])])
    red.update(zip(ffn, done_f))
    sums_m = [_add_sibling(part[n], s, where, "add_sibling_" + n) for n, s in zip(mix4, sib_m)]
    part["w_in"] = _dw_full(h1, dp, "dw_in", N_CHIP)
    dbd_r, ((sib_i,), own_m) = _dw_diag(
        p, lam_r, "db_re", LANE, 4 * LANE,
        [_job_sibling_halves([part["w_in"]]), _job_to_owner(sums_m)])
    sum_i = _add_sibling(part["w_in"], sib_i, where, "add_sibling_w_in")
    red_m = [_add_chips(s, o, where, "add_chips_" + n) for n, s, o in zip(mix4, sums_m, own_m)]
    dbd_i, ((own_i,), done_m) = _dw_diag(
        p, lam_i, "db_im", LANE, 4 * LANE, [_job_to_owner([sum_i]), _job_swap_halves(red_m)])
    red.update(zip(mix4, done_m))
    red_i = _add_chips(sum_i, own_i, where, "add_chips_w_in")
    dcd_r, ((red["w_in"],),) = _dw_diag(str_, dy0, "dc_re", 4 * LANE, LANE, [_job_swap_halves([red_i])])
    dcd_i, _ = _dw_diag(sti, dy0, "dc_im", 4 * LANE, LANE)

    onehot = _group_onehot()

    def pick_b(blk):
        return jnp.einsum("ghlp,gl->gph", blk.reshape(SSM_G, SSM_H, 8, SSM_P), onehot)

    def pick_c(blk):
        return jnp.einsum("gplh,gl->ghp", blk.reshape(SSM_G, SSM_P, 8, SSM_H), onehot)

    dabr = jnp.sum(dar8, axis=0).reshape(SSM_G, SSM_P)
    dabi = jnp.sum(dai8, axis=0).reshape(SSM_G, SSM_P)
    d_a_re, d_a_im, d_log_dt, d_b_re, d_b_im = disc_vjp((dabr, dabi, pick_b(dbd_r), pick_b(dbd_i)))
    gsmall = {
        "g_mix": sm_gmix[0], "a_re": d_a_re, "a_im": d_a_im, "log_dt": d_log_dt,
        "b_re": d_b_re, "b_im": d_b_im, "c_re": pick_c(dcd_r), "c_im": -pick_c(dcd_i),
        "d_skip": sm_mix[0], "b_glu": sm_mix[1], "g_sgu": sm_mix[2],
        "w_s": jnp.where(tril[None], dws, 0.0),
        "b_s": dbm.reshape(CHUNK, SGU_G, SGU_D).sum(-1).T,
        "g_ffn": sm_gffn[0], "conv_b": sm_conv[3], "g_final": sm_ffn[0],
        "conv_w": sm_conv[0:3], "loss": sm_ffn[1, 0:1],
    }

    total_pack = _small_allreduce(_pack(gsmall))
    total = _unpack(total_pack)
    grads = dict(red)
    cs = 2 * D_FF // N_CHIP
    grads["conv_w"] = lax.dynamic_slice(total["conv_w"], (0, chip * cs), (3, cs))
    for n in SMALL:
        grads[n] = total[n]
    delta, new_m, new_v = {}, {}, {}
    for n in BIG + ("conv_w",):
        delta[n], new_m[n], new_v[n] = _adamw(shard2d(w[n]), grads[n], shard2d(m[n]), shard2d(v[n]),
                                              "adamw_" + n, PLACE_ROWS.get(n, 3))
    def flat2d(a):
        return a.reshape(-1, a.shape[-1])

    ud, um, uv = _adamw_many(*[[flat2d(d[n]) for n in SMALL] for d in (w, total, m, v)], "adamw_small")
    for i, n in enumerate(SMALL):
        delta[n], new_m[n], new_v[n] = ud[i], um[i], uv[i]

    def like(d):
        return [d[n].reshape(w[n].shape) for n in WEIGHTS]

    return (total["loss"].reshape(()), grad_x.reshape(x.shape), *like(grads), *like(delta),
            *like(new_m), *like(new_v))
```

```python
import math

import jax
import jax.numpy as jnp
from jax import lax
from jax.experimental import pallas as pl
from jax.experimental.pallas import tpu as pltpu

F32 = jnp.float32
BF16 = jnp.bfloat16
MESH = pl.DeviceIdType.MESH

D_MODEL = 1024
SSM_W = 512
SSM_G = 32
SSM_H = 16
SSM_P = 64
N_STATE = SSM_G * SSM_P
SGU_W = 512
SGU_G = 8
SGU_D = 64
CHUNK = 128
D_FF = 2816
IN_COLS = 3584
EPS = 1e-6
N_CHIP = 4

ADAM_LR = 0.001
ADAM_B1 = 0.9
ADAM_B2 = 0.999
ADAM_EPS = 1e-08
ADAM_WD = 0.01
ADAM_STEP = 10

SUBLANE = 8
LANE = 128
VMEM_LIMIT = 56 * 1024 * 1024
TB = 256
TK = 512
SCAN_LANES = 256
SCAN_UNROLL = 4
HALO = SUBLANE

BIG = ("w_in", "w_up", "w_down", "w_out", "w_proj_a", "w_proj_b", "w_glu")
SMALL = ("g_mix", "a_re", "a_im", "log_dt", "b_re", "b_im", "c_re", "c_im", "d_skip", "b_glu",
         "g_sgu", "w_s", "b_s", "g_ffn", "conv_b", "g_final")
WEIGHTS = ("g_mix", "w_in", "a_re", "a_im", "log_dt", "b_re", "b_im", "c_re", "c_im", "d_skip",
           "w_glu", "b_glu", "w_proj_a", "g_sgu", "w_s", "b_s", "w_proj_b", "w_out", "g_ffn",
           "w_up", "conv_w", "conv_b", "w_down", "g_final")

ANY = pl.BlockSpec(memory_space=pl.ANY)


def _params(n_grid):
    return pltpu.CompilerParams(dimension_semantics=("arbitrary",) * n_grid if n_grid else None,
                                vmem_limit_bytes=VMEM_LIMIT)


def _whole():
    return pl.BlockSpec(memory_space=pltpu.VMEM)


def _rows(tb, ncol):
    return pl.BlockSpec((tb, ncol), lambda i: (i, 0))


def _acc(nrow, ncol):
    return pl.BlockSpec((nrow, ncol), lambda i: (0, 0))


def _dot(a, b):
    return jnp.dot(a.astype(BF16), b.astype(BF16), preferred_element_type=F32)


def _dot_nt(a, b):
    return lax.dot_general(a.astype(BF16), b.astype(BF16), (((1,), (1,)), ((), ())),
                           preferred_element_type=F32)


def _sigmoid(v):
    return 0.5 * jnp.tanh(0.5 * v) + 0.5


_GELU_C = math.sqrt(2.0 / math.pi)


def _gelu(v):
    return 0.5 * v * (1.0 + jnp.tanh(_GELU_C * (v + 0.044715 * v * v * v)))


def _gelu_and_grad(v):
    v2 = v * v
    t = jnp.tanh(_GELU_C * v * (1.0 + 0.044715 * v2))
    half = 0.5 * (1.0 + t)
    return v * half, half + 0.5 * v * (1.0 - t * t) * _GELU_C * (1.0 + 3.0 * 0.044715 * v2)


def _rms_stats(v):
    r = lax.rsqrt(jnp.mean(v * v, axis=-1, keepdims=True) + EPS)
    return r, v * r


def _rms_bwd(dxh, xh, r):
    return r * (dxh - xh * jnp.mean(dxh * xh, axis=-1, keepdims=True))


def _place():
    x, y, c = lax.axis_index("x"), lax.axis_index("y"), lax.axis_index("c")
    chips = [(1 - x, y), (x, 1 - y), (1 - x, 1 - y)]
    return x, y, c, chips


def _chip_index(chip):
    return 2 * chip[0] + chip[1]


def _remote(src, dst, send_sem, recv_sem, device):
    return pltpu.make_async_remote_copy(src_ref=src, dst_ref=dst, send_sem=send_sem,
                                        recv_sem=recv_sem, device_id=device, device_id_type=MESH)


def _half(ref_rows, c):
    hr = ref_rows // 2
    return pl.ds(pl.multiple_of(c * hr, SUBLANE), hr)


class _Job:
    def __init__(self, start, finish, n_sem, ins=(), inouts=(), outs=()):
        self.start, self.finish, self.n_sem = start, finish, n_sem
        self.ins, self.inouts, self.outs = list(ins), list(inouts), list(outs)


def _job_gather_ici(bufs, whole=()):
    n = len(bufs)

    def copies(io):
        x, y, c, chips = _place()
        k_me = 2 * x + y
        out = []
        for t in range(n):
            for j, ch in enumerate(chips):
                if t in whole:
                    src, land = io[t].at[k_me], io[t].at[_chip_index(ch)]
                else:
                    mine = _half(bufs[t].shape[1], c)
                    src, land = io[t].at[k_me, mine, :], io[t].at[_chip_index(ch), mine, :]
                out.append((src, land, 3 * t + j, (*ch, c)))
        return out

    def start(ins, io, outs, ssem, rsem):
        for src, _, i, dev in copies(io):
            _remote(src, src, ssem(i), rsem(i), dev).start()

    def finish(ins, io, outs, ssem, rsem):
        cps = copies(io)
        for _, land, i, dev in cps:
            _remote(land, land, ssem(i), rsem(i), dev).wait_recv()
        for src, _, i, dev in cps:
            _remote(src, src, ssem(i), rsem(i), dev).wait_send()

    return _Job(start, finish, 3 * n, inouts=bufs)


def _job_gather_sibling(bufs):
    n = len(bufs)

    def copies(io):
        x, y, c, chips = _place()
        out = []
        for t in range(n):
            rows = bufs[t].shape[1]
            for j, ch in enumerate(chips):
                k = _chip_index(ch)
                out.append((io[t].at[k, _half(rows, c), :], io[t].at[k, _half(rows, 1 - c), :],
                            3 * t + j, (x, y, 1 - c)))
        return out

    def start(ins, io, outs, ssem, rsem):
        for src, _, i, dev in copies(io):
            _remote(src, src, ssem(i), rsem(i), dev).start()

    def finish(ins, io, outs, ssem, rsem):
        cps = copies(io)
        for _, land, i, dev in cps:
            _remote(land, land, ssem(i), rsem(i), dev).wait_recv()
        for src, _, i, dev in cps:
            _remote(src, src, ssem(i), rsem(i), dev).wait_send()

    return _Job(start, finish, 3 * n, inouts=bufs)


def _job_sibling_halves(grads):
    n = len(grads)

    def build(ins, outs, ssem, rsem):
        x, y, c, _ = _place()
        return [_remote(ins[t].at[:, _half(grads[t].shape[1], 1 - c), :], outs[t], ssem(t), rsem(t),
                        (x, y, 1 - c)) for t in range(n)]

    def start(ins, io, outs, ssem, rsem):
        for cp in build(ins, outs, ssem, rsem):
            cp.start()

    def finish(ins, io, outs, ssem, rsem):
        for cp in build(ins, outs, ssem, rsem):
            cp.wait()

    return _Job(start, finish, n, ins=grads,
                outs=[jax.ShapeDtypeStruct((N_CHIP, g.shape[1] // 2, g.shape[2]), F32) for g in grads])


def _job_to_owner(sums):
    n = len(sums)

    def build(ins, outs, ssem, rsem):
        x, y, c, chips = _place()
        return [_remote(ins[t].at[_chip_index(ch)], outs[t].at[j], ssem(3 * t + j), rsem(3 * t + j),
                        (*ch, c)) for t in range(n) for j, ch in enumerate(chips)]

    def start(ins, io, outs, ssem, rsem):
        for cp in build(ins, outs, ssem, rsem):
            cp.start()

    def finish(ins, io, outs, ssem, rsem):
        for cp in build(ins, outs, ssem, rsem):
            cp.wait()

    return _Job(start, finish, 3 * n, ins=sums,
                outs=[jax.ShapeDtypeStruct((3,) + s.shape[1:], s.dtype) for s in sums])


def _job_swap_halves(bufs):
    n = len(bufs)

    def start(ins, io, outs, ssem, rsem):
        x, y, c, _ = _place()
        for t in range(n):
            mine = io[t].at[_half(bufs[t].shape[0], c), :]
            _remote(mine, mine, ssem(t), rsem(t), (x, y, 1 - c)).start()

    def finish(ins, io, outs, ssem, rsem):
        x, y, c, _ = _place()
        for t in range(n):
            theirs = io[t].at[_half(bufs[t].shape[0], 1 - c), :]
            _remote(theirs, theirs, ssem(t), rsem(t), (x, y, 1 - c)).wait_recv()
        for t in range(n):
            mine = io[t].at[_half(bufs[t].shape[0], c), :]
            _remote(mine, mine, ssem(t), rsem(t), (x, y, 1 - c)).wait_send()

    return _Job(start, finish, n, inouts=bufs)


def _call(body, name, grid, in_specs, out_specs, out_shape, args, jobs=(), scratch=()):
    n_in, n_out, n_scr = len(args), len(out_shape), len(scratch)
    job_in = [a for jb in jobs for a in jb.ins + jb.inouts]
    job_out = [s for jb in jobs
               for s in [jax.ShapeDtypeStruct(a.shape, a.dtype) for a in jb.inouts] + jb.outs]
    aliases, pos_in, pos_out = {}, n_in, n_out
    for jb in jobs:
        pos_in += len(jb.ins)
        for _ in jb.inouts:
            aliases[pos_in] = pos_out
            pos_in += 1
            pos_out += 1
        pos_out += len(jb.outs)
    n_sem = sum(jb.n_sem for jb in jobs)

    def wrapped(*refs):
        c_in = refs[:n_in]
        j_in = refs[n_in:n_in + len(job_in)]
        c_out = refs[n_in + len(job_in):n_in + len(job_in) + n_out]
        j_out = refs[n_in + len(job_in) + n_out:n_in + len(job_in) + n_out + len(job_out)]
        rest = refs[n_in + len(job_in) + n_out + len(job_out):]
        c_scr = rest[:n_scr]
        views, pi, po, ps = [], 0, 0, 0
        for jb in jobs:
            ins = j_in[pi:pi + len(jb.ins)]
            pi += len(jb.ins) + len(jb.inouts)
            io = j_out[po:po + len(jb.inouts)]
            new = j_out[po + len(jb.inouts):po + len(jb.inouts) + len(jb.outs)]
            po += len(jb.inouts) + len(jb.outs)
            send = (lambda i, o=ps: rest[n_scr].at[o + i])
            recv = (lambda i, o=ps: rest[n_scr + 1].at[o + i])
            ps += jb.n_sem
            views.append((ins, io, new, send, recv))

        def run(which):
            for jb, vw in zip(jobs, views):
                (jb.start if which == 0 else jb.finish)(*vw)

        if not grid:
            run(0)
            run(1)
            return
        if jobs:
            first = pl.program_id(0) == 0
            last = pl.program_id(0) == grid[0] - 1
            for d in range(1, len(grid)):
                first = jnp.logical_and(first, pl.program_id(d) == 0)
                last = jnp.logical_and(last, pl.program_id(d) == grid[d] - 1)
            pl.when(first)(lambda: run(0))
        body(*c_in, *c_out, *c_scr)
        if jobs:
            pl.when(last)(lambda: run(1))

    sems = [pltpu.SemaphoreType.DMA((n_sem,)), pltpu.SemaphoreType.DMA((n_sem,))] if jobs else []
    kwargs = dict(grid=grid) if grid else {}
    res = pl.pallas_call(
        wrapped, name=name, in_specs=list(in_specs) + [ANY] * len(job_in),
        out_specs=list(out_specs) + [ANY] * len(job_out),
        out_shape=list(out_shape) + job_out, scratch_shapes=list(scratch) + sems,
        input_output_aliases=aliases, compiler_params=_params(len(grid)), **kwargs,
    )(*args, *job_in)
    outs, pos, per_job = list(res[:n_out]), n_out, []
    for jb in jobs:
        k = len(jb.inouts) + len(jb.outs)
        per_job.append(list(res[pos:pos + k]))
        pos += k
    return outs, per_job


def _comm(name, jobs):
    return _call(None, name, (), [], [], [], [], jobs)[1]


def _fwd_in(x, g_mix, w_in, bre, bim, jobs=()):
    t_len = x.shape[0]
    cs = IN_COLS // N_CHIP

    def body(x_ref, g_ref, w_ref, bre_ref, bim_ref, p_ref, h_ref, bur_ref, bui_ref):
        xv = x_ref[...]
        r, xh = _rms_stats(xv)
        h = (xh * g_ref[...]).astype(BF16)
        h_ref[...] = h
        for k in range(N_CHIP):
            p_ref[:, k * cs:(k + 1) * cs] = jnp.dot(h, w_ref[k], preferred_element_type=F32)
        u = p_ref[:, 0:SSM_W].astype(BF16)
        bur_ref[...] = jnp.dot(u, bre_ref[...], preferred_element_type=F32)
        bui_ref[...] = jnp.dot(u, bim_ref[...], preferred_element_type=F32)

    return _call(
        body, "fwd_in", (t_len // TB,),
        [_rows(TB, D_MODEL), _whole(), _whole(), _whole(), _whole()],
        [_rows(TB, IN_COLS), _rows(TB, D_MODEL), _rows(TB, N_STATE), _rows(TB, N_STATE)],
        [jax.ShapeDtypeStruct((t_len, IN_COLS), F32), jax.ShapeDtypeStruct((t_len, D_MODEL), BF16),
         jax.ShapeDtypeStruct((t_len, N_STATE), F32), jax.ShapeDtypeStruct((t_len, N_STATE), F32)],
        [x, g_mix, w_in, bre, bim], jobs)


def _scan_local(xr, xi, tab, shifts):
    for q, s in enumerate(shifts):
        ar, ai = tab[2 * q], tab[2 * q + 1]
        rr = pltpu.roll(xr, s, 0)
        ri = pltpu.roll(xi, s, 0)
        xr, xi = xr + ar * rr - ai * ri, xi + ar * ri + ai * rr
    return xr, xi


def _scan_carry(xr, xi, tab, cr, ci):
    pr, pi = tab[6], tab[7]
    return xr + pr * cr - pi * ci, xi + pr * ci + pi * cr


def _scan_fwd(bur, bui, tab, jobs=()):
    t_len = bur.shape[0]
    nblk = t_len // SUBLANE
    lb = SCAN_LANES

    def body(br_ref, bi_ref, tab_ref, sr_ref, si_ref):
        tab_v = [tab_ref[q] for q in range(8)]

        def step(k, carry):
            cr, ci = carry
            rows = [pl.multiple_of((k * SCAN_UNROLL + u) * SUBLANE, SUBLANE)
                    for u in range(SCAN_UNROLL)]
            local = [_scan_local(br_ref[pl.ds(r0, SUBLANE), :], bi_ref[pl.ds(r0, SUBLANE), :],
                                 tab_v, (1, 2, 4)) for r0 in rows]
            for r0, (xr, xi) in zip(rows, local):
                xr, xi = _scan_carry(xr, xi, tab_v, cr, ci)
                sr_ref[pl.ds(r0, SUBLANE), :] = xr
                si_ref[pl.ds(r0, SUBLANE), :] = xi
                cr, ci = xr[SUBLANE - 1:SUBLANE, :], xi[SUBLANE - 1:SUBLANE, :]
            return cr, ci

        zero = jnp.zeros((1, lb), F32)
        lax.fori_loop(0, nblk // SCAN_UNROLL, step, (zero, zero))

    col = pl.BlockSpec((t_len, lb), lambda j: (0, j))
    return _call(
        body, "scan_fwd", (N_STATE // lb,),
        [col, col, pl.BlockSpec((8, SUBLANE, lb), lambda j: (0, 0, j))], [col, col],
        [jax.ShapeDtypeStruct((t_len, N_STATE), F32)] * 2, [bur, bui, tab], jobs)


def _sgu_mix(v, ws_ref, lane_lo):
    rows = []
    for c0 in range(0, v.shape[0], CHUNK):
        slabs = []
        for j in range(SGU_W // LANE):
            prod = jnp.dot(ws_ref[j], v[c0:c0 + CHUNK, j * LANE:(j + 1) * LANE].astype(BF16),
                           preferred_element_type=F32)
            slabs.append(jnp.where(lane_lo, prod[:CHUNK], prod[CHUNK:]))
        rows.append(jnp.concatenate(slabs, axis=1))
    return jnp.concatenate(rows, axis=0) if len(rows) > 1 else rows[0]


def _fwd_mix(x, p, str_, sti, cre, cim, d_skip, w_glu, b_glu, w_pa, g_sgu, ws_st, bmat, w_pb, w_out,
             jobs=()):
    t_len = x.shape[0]
    pc = D_MODEL // N_CHIP

    def body(x_ref, p_ref, sr_ref, si_ref, cre_ref, cim_ref, dsk_ref, wg_ref, bg_ref, wpa_ref,
             gs_ref, ws_ref, bm_ref, wpb_ref, wo_ref,
             x2_ref, y0_ref, z_ref, mx_ref, ya_ref, yb_ref):
        u = p_ref[:, 0:SSM_W]
        y0 = _dot(sr_ref[...], cre_ref[...]) - _dot(si_ref[...], cim_ref[...]) + dsk_ref[...] * u
        y0_ref[...] = y0
        y1 = _gelu(y0)
        z = _dot(y1, wg_ref[...]) + bg_ref[...]
        z_ref[...] = z
        ya_pre = (y1 * _sigmoid(z)).astype(BF16)
        for k in range(N_CHIP):
            ya_ref[:, k * pc:(k + 1) * pc] = jnp.dot(ya_pre, wpa_ref[k], preferred_element_type=F32)

        uvg = _gelu(p_ref[:, SSM_W:SSM_W + 2 * SGU_W])
        u2 = uvg[:, :SGU_W]
        _, vh = _rms_stats(uvg[:, SGU_W:])
        v3 = vh * gs_ref[...]
        lane_lo = lax.broadcasted_iota(jnp.int32, (CHUNK, LANE), 1) < SGU_D
        bias = jnp.concatenate([bm_ref[...]] * (TB // CHUNK), axis=0)
        mixed = _sgu_mix(v3, ws_ref, lane_lo) + bias
        mx_ref[...] = mixed
        sgu = (u2 * mixed).astype(BF16)
        for k in range(N_CHIP):
            yb_ref[:, k * pc:(k + 1) * pc] = jnp.dot(sgu, wpb_ref[k], preferred_element_type=F32)

        lg0 = SSM_W + 2 * SGU_W
        ga = _sigmoid(p_ref[:, lg0:lg0 + D_MODEL])
        gb = _sigmoid(p_ref[:, lg0 + D_MODEL:lg0 + 2 * D_MODEL])
        mrg = ga * ya_ref[...] + gb * yb_ref[...]
        x2_ref[...] = x_ref[...] + _dot(mrg, wo_ref[...])

    return _call(
        body, "fwd_mix", (t_len // TB,),
        [_rows(TB, D_MODEL), _rows(TB, IN_COLS), _rows(TB, N_STATE), _rows(TB, N_STATE)]
        + [_whole()] * 11,
        [_rows(TB, D_MODEL), _rows(TB, SSM_W), _rows(TB, SSM_W), _rows(TB, SGU_W),
         _rows(TB, D_MODEL), _rows(TB, D_MODEL)],
        [jax.ShapeDtypeStruct((t_len, D_MODEL), F32), jax.ShapeDtypeStruct((t_len, SSM_W), F32),
         jax.ShapeDtypeStruct((t_len, SSM_W), F32), jax.ShapeDtypeStruct((t_len, SGU_W), F32),
         jax.ShapeDtypeStruct((t_len, D_MODEL), F32), jax.ShapeDtypeStruct((t_len, D_MODEL), F32)],
        [x, p, str_, sti, cre, cim, d_skip, w_glu, b_glu, w_pa, g_sgu, ws_st, bmat, w_pb, w_out], jobs)


def _conv_taps(v, cw_ref, c0, width):
    w0 = cw_ref[0:1, c0:c0 + width]
    w1 = cw_ref[1:2, c0:c0 + width]
    w2 = cw_ref[2:3, c0:c0 + width]
    return w0 * pltpu.roll(v, 2, 0) + w1 * pltpu.roll(v, 1, 0) + w2 * v


def _fwd_ffn(x2, target, g_ffn, w_up, conv_w, conv_b, w_down, g_final):
    t_len = x2.shape[0]
    half = D_FF // 2
    blocks_per_halo = TB // HALO

    def body(x2_ref, xp_ref, tg_ref, gf_ref, wu_ref, cw_ref, cb_ref, wd_ref, gl_ref,
             up_ref, f_ref, h2_ref, dx3_ref, sm_ref):
        i = pl.program_id(0)
        xe = jnp.concatenate([xp_ref[...] * jnp.where(i == 0, 0.0, 1.0), x2_ref[...]], axis=0)
        _, xh = _rms_stats(xe)
        h2 = (xh * gf_ref[...]).astype(BF16)
        h2_ref[...] = h2[HALO:]
        acc = jnp.zeros((TB, D_MODEL), F32)
        for hc in range(2):
            ca = hc * half
            cb = D_FF + hc * half
            ua = jnp.dot(h2, wu_ref[hc], preferred_element_type=F32)
            ub = jnp.dot(h2, wu_ref[2 + hc], preferred_element_type=F32)
            up_ref[:, ca:ca + half] = ua[HALO:]
            up_ref[:, cb:cb + half] = ub[HALO:]
            ac = _conv_taps(ua, cw_ref, ca, half)[HALO:] + cb_ref[:, ca:ca + half]
            bc = _conv_taps(ub, cw_ref, cb, half)[HALO:] + cb_ref[:, cb:cb + half]
            f = (ac * _sigmoid(ac) * bc).astype(BF16)
            f_ref[:, ca:ca + half] = f
            acc = acc + jnp.dot(f, wd_ref[ca:ca + half, :], preferred_element_type=F32)
        x3 = x2_ref[...] + acc
        r3, xh3 = _rms_stats(x3)
        err = xh3 * gl_ref[...] - tg_ref[...]
        dout = err * (1.0 / D_MODEL)
        dx3_ref[...] = _rms_bwd(dout * gl_ref[...], xh3, r3)
        dgl = jnp.sum(dout * xh3, axis=0, keepdims=True)
        loss = 0.5 * jnp.sum(jnp.mean(err * err, axis=-1, keepdims=True), axis=0, keepdims=True)
        upd = jnp.concatenate([dgl, jnp.broadcast_to(loss, (1, D_MODEL)),
                               jnp.zeros((SUBLANE - 2, D_MODEL), F32)], axis=0)

        @pl.when(i == 0)
        def _():
            sm_ref[...] = upd

        @pl.when(i > 0)
        def _():
            sm_ref[...] += upd

    prev = pl.BlockSpec((HALO, D_MODEL), lambda i: (jnp.maximum(i * blocks_per_halo - 1, 0), 0))
    return _call(
        body, "fwd_ffn", (t_len // TB,),
        [_rows(TB, D_MODEL), prev, _rows(TB, D_MODEL)] + [_whole()] * 6,
        [_rows(TB, 2 * D_FF), _rows(TB, D_FF), _rows(TB, D_MODEL), _rows(TB, D_MODEL),
         _acc(SUBLANE, D_MODEL)],
        [jax.ShapeDtypeStruct((t_len, 2 * D_FF), F32), jax.ShapeDtypeStruct((t_len, D_FF), BF16),
         jax.ShapeDtypeStruct((t_len, D_MODEL), BF16), jax.ShapeDtypeStruct((t_len, D_MODEL), F32),
         jax.ShapeDtypeStruct((SUBLANE, D_MODEL), F32)],
        [x2, x2, target, g_ffn, w_up, conv_w, conv_b, w_down, g_final])[0]


def _bwd_ffn(dx3, up, x2, g_ffn, w_up, conv_w, conv_b, w_down, jobs=()):
    t_len = x2.shape[0]
    half = D_FF // 2
    nblk = t_len // TB
    bph = TB // HALO
    n_halo = t_len // HALO

    def body(dx_ref, dxn_ref, up_ref, upp_ref, upn_ref, x2_ref, gf_ref, wu_ref, cw_ref, cb_ref,
             wd_ref, dx2_ref, dup_ref, smw_ref, smg_ref):
        i = pl.program_id(0)
        keep_first = jnp.where(i == 0, 0.0, 1.0)
        keep_last = jnp.where(i == nblk - 1, 0.0, 1.0)
        dxe = jnp.concatenate([dx_ref[...], dxn_ref[...] * keep_last], axis=0).astype(BF16)
        dh2 = jnp.zeros((TB, D_MODEL), F32)
        zpad = jnp.zeros((1, half), F32)
        for hc in range(2):
            ca = hc * half
            cb = D_FF + hc * half
            uea = jnp.concatenate([upp_ref[:, ca:ca + half] * keep_first, up_ref[:, ca:ca + half],
                                   upn_ref[:, ca:ca + half]], axis=0)
            ueb = jnp.concatenate([upp_ref[:, cb:cb + half] * keep_first, up_ref[:, cb:cb + half],
                                   upn_ref[:, cb:cb + half]], axis=0)
            ua1, ua2 = pltpu.roll(uea, 1, 0), pltpu.roll(uea, 2, 0)
            ub1, ub2 = pltpu.roll(ueb, 1, 0), pltpu.roll(ueb, 2, 0)
            wa = [cw_ref[k:k + 1, ca:ca + half] for k in range(3)]
            wb = [cw_ref[k:k + 1, cb:cb + half] for k in range(3)]
            ac = (wa[0] * ua2 + wa[1] * ua1 + wa[2] * uea)[HALO:] + cb_ref[:, ca:ca + half]
            bc = (wb[0] * ub2 + wb[1] * ub1 + wb[2] * ueb)[HALO:] + cb_ref[:, cb:cb + half]
            df = lax.dot_general(dxe, wd_ref[ca:ca + half, :], (((1,), (1,)), ((), ())),
                                 preferred_element_type=F32)
            sg = _sigmoid(ac)
            da = df * bc * sg * (1.0 + ac * (1.0 - sg))
            db = df * ac * sg
            n_e = TB + HALO
            dua = (wa[2] * da + wa[1] * pltpu.roll(da, n_e - 1, 0)
                   + wa[0] * pltpu.roll(da, n_e - 2, 0))[:TB]
            dub = (wb[2] * db + wb[1] * pltpu.roll(db, n_e - 1, 0)
                   + wb[0] * pltpu.roll(db, n_e - 2, 0))[:TB]
            dup_ref[:, ca:ca + half] = dua.astype(BF16)
            dup_ref[:, cb:cb + half] = dub.astype(BF16)
            dh2 = dh2 + _dot_nt(dua, wu_ref[hc]) + _dot_nt(dub, wu_ref[2 + hc])
            dab = da[:TB]
            dbb = db[:TB]
            rows = []
            for d_, u0, u1, u2 in ((dab, ua2, ua1, uea), (dbb, ub2, ub1, ueb)):
                rows.append([jnp.sum(d_ * u0[HALO:HALO + TB], axis=0, keepdims=True),
                             jnp.sum(d_ * u1[HALO:HALO + TB], axis=0, keepdims=True),
                             jnp.sum(d_ * u2[HALO:HALO + TB], axis=0, keepdims=True),
                             jnp.sum(d_, axis=0, keepdims=True)])
            for c0, rws in ((ca, rows[0]), (cb, rows[1])):
                upd = jnp.concatenate(rws + [zpad] * (SUBLANE - 4), axis=0)

                @pl.when(i == 0)
                def _(upd=upd, c0=c0):
                    smw_ref[:, c0:c0 + half] = upd

                @pl.when(i > 0)
                def _(upd=upd, c0=c0):
                    smw_ref[:, c0:c0 + half] += upd

        r2, xh2 = _rms_stats(x2_ref[...])
        dx2_ref[...] = dx_ref[...] + _rms_bwd(dh2 * gf_ref[...], xh2, r2)
        updg = jnp.concatenate([jnp.sum(dh2 * xh2, axis=0, keepdims=True),
                                jnp.zeros((SUBLANE - 1, D_MODEL), F32)], axis=0)

        @pl.when(i == 0)
        def _():
            smg_ref[...] = updg

        @pl.when(i > 0)
        def _():
            smg_ref[...] += updg

    nxt_d = pl.BlockSpec((HALO, D_MODEL), lambda i: (jnp.minimum((i + 1) * bph, n_halo - 1), 0))
    prv_u = pl.BlockSpec((HALO, 2 * D_FF), lambda i: (jnp.maximum(i * bph - 1, 0), 0))
    nxt_u = pl.BlockSpec((HALO, 2 * D_FF), lambda i: (jnp.minimum((i + 1) * bph, n_halo - 1), 0))
    return _call(
        body, "bwd_ffn", (nblk,),
        [_rows(TB, D_MODEL), nxt_d, _rows(TB, 2 * D_FF), prv_u, nxt_u, _rows(TB, D_MODEL)]
        + [_whole()] * 5,
        [_rows(TB, D_MODEL), _rows(TB, 2 * D_FF), _acc(SUBLANE, 2 * D_FF), _acc(SUBLANE, D_MODEL)],
        [jax.ShapeDtypeStruct((t_len, D_MODEL), F32), jax.ShapeDtypeStruct((t_len, 2 * D_FF), BF16),
         jax.ShapeDtypeStruct((SUBLANE, 2 * D_FF), F32), jax.ShapeDtypeStruct((SUBLANE, D_MODEL), F32)],
        [dx3, dx3, up, up, up, x2, g_ffn, w_up, conv_w, conv_b, w_down], jobs)


def _bwd_mix(dx2, p, y0, z, mixed, ya, yb, w_out, w_pa, w_pb, w_glu, cre, cim, ws_st, wst_st,
             d_skip, g_sgu, jobs=()):
    t_len = dx2.shape[0]
    pc = D_MODEL // N_CHIP
    n_slab = SGU_W // LANE

    def body(dx_ref, p_ref, y0_ref, z_ref, mx_ref, ya_ref, yb_ref, wo_ref, wpa_ref, wpb_ref,
             wg_ref, cre_ref, cim_ref, ws_ref, wst_ref, dsk_ref, gs_ref,
             dsr_ref, dsi_ref, du_ref, drest_ref, mrg_ref, dya_ref, dyb_ref, yap_ref, dz_ref,
             y1_ref, sgu_ref, dy0_ref, sm_ref, dbm_ref, dws_ref):
        i = pl.program_id(0)
        first = i == 0
        lg0 = SSM_W + 2 * SGU_W
        dmrg = _dot_nt(dx_ref[...], wo_ref[...])
        ga = _sigmoid(p_ref[:, lg0:lg0 + D_MODEL])
        gb = _sigmoid(p_ref[:, lg0 + D_MODEL:lg0 + 2 * D_MODEL])
        yav = ya_ref[...]
        ybv = yb_ref[...]
        mrg_ref[...] = (ga * yav + gb * ybv).astype(BF16)
        drest_ref[:, 2 * SGU_W:2 * SGU_W + D_MODEL] = (dmrg * yav * ga * (1.0 - ga)).astype(BF16)
        drest_ref[:, 2 * SGU_W + D_MODEL:] = (dmrg * ybv * gb * (1.0 - gb)).astype(BF16)
        dya = (dmrg * ga).astype(BF16)
        dyb = (dmrg * gb).astype(BF16)
        dya_ref[...] = dya
        dyb_ref[...] = dyb

        y0v = y0_ref[...]
        y1, y1_grad = _gelu_and_grad(y0v)
        sz = _sigmoid(z_ref[...])
        y1_ref[...] = y1.astype(BF16)
        yap_ref[...] = (y1 * sz).astype(BF16)
        dyap = jnp.zeros((TB, SSM_W), F32)
        for k in range(N_CHIP):
            dyap = dyap + _dot_nt(dya[:, k * pc:(k + 1) * pc], wpa_ref[k])
        dz = dyap * y1 * sz * (1.0 - sz)
        dz_ref[...] = dz.astype(BF16)
        dy0 = (dyap * sz + _dot_nt(dz, wg_ref[...])) * y1_grad
        dy0_ref[...] = dy0.astype(BF16)
        u = p_ref[:, 0:SSM_W]
        du_ref[...] = dy0 * dsk_ref[...]
        dsr_ref[...] = _dot_nt(dy0, cre_ref[...])
        dsi_ref[...] = -_dot_nt(dy0, cim_ref[...])

        uv = p_ref[:, SSM_W:lg0]
        uvg, gg = _gelu_and_grad(uv)
        u2 = uvg[:, :SGU_W]
        rv, vh = _rms_stats(uvg[:, SGU_W:])
        v3 = vh * gs_ref[...]
        mixed = mx_ref[...]
        dsgu = jnp.zeros((TB, SGU_W), F32)
        for k in range(N_CHIP):
            dsgu = dsgu + _dot_nt(dyb[:, k * pc:(k + 1) * pc], wpb_ref[k])
        sgu_ref[...] = (u2 * mixed).astype(BF16)
        du2 = dsgu * mixed
        dmix = dsgu * u2
        lane_lo = lax.broadcasted_iota(jnp.int32, (CHUNK, LANE), 1) < SGU_D
        dv3 = _sgu_mix(dmix, wst_ref, lane_lo)
        dbm = jnp.zeros((CHUNK, SGU_W), F32)
        for c0 in range(0, TB, CHUNK):
            dbm = dbm + dmix[c0:c0 + CHUNK]
        for j in range(n_slab):
            lo = jnp.zeros((CHUNK, CHUNK), F32)
            hi = jnp.zeros((CHUNK, CHUNK), F32)
            for c0 in range(0, TB, CHUNK):
                dsl = dmix[c0:c0 + CHUNK, j * LANE:(j + 1) * LANE]
                vsl = v3[c0:c0 + CHUNK, j * LANE:(j + 1) * LANE]
                lo = lo + _dot_nt(jnp.where(lane_lo, dsl, 0.0), vsl)
                hi = hi + _dot_nt(jnp.where(lane_lo, 0.0, dsl), vsl)

            @pl.when(first)
            def _(lo=lo, hi=hi, j=j):
                dws_ref[2 * j] = lo
                dws_ref[2 * j + 1] = hi

            @pl.when(jnp.logical_not(first))
            def _(lo=lo, hi=hi, j=j):
                dws_ref[2 * j] += lo
                dws_ref[2 * j + 1] += hi

        dv2 = _rms_bwd(dv3 * gs_ref[...], vh, rv)
        drest_ref[:, 0:SGU_W] = (du2 * gg[:, :SGU_W]).astype(BF16)
        drest_ref[:, SGU_W:2 * SGU_W] = (dv2 * gg[:, SGU_W:]).astype(BF16)

        upd = jnp.concatenate([jnp.sum(dy0 * u, axis=0, keepdims=True),
                               jnp.sum(dz, axis=0, keepdims=True),
                               jnp.sum(dv3 * vh, axis=0, keepdims=True),
                               jnp.zeros((SUBLANE - 3, SSM_W), F32)], axis=0)

        @pl.when(first)
        def _():
            sm_ref[...] = upd
            dbm_ref[...] = dbm

        @pl.when(jnp.logical_not(first))
        def _():
            sm_ref[...] += upd
            dbm_ref[...] += dbm

    rest = 2 * SGU_W + 2 * D_MODEL
    bf_d, bf_s = jax.ShapeDtypeStruct((t_len, D_MODEL), BF16), jax.ShapeDtypeStruct((t_len, SSM_W), BF16)
    return _call(
        body, "bwd_mix", (t_len // TB,),
        [_rows(TB, D_MODEL), _rows(TB, IN_COLS), _rows(TB, SSM_W), _rows(TB, SSM_W),
         _rows(TB, SGU_W), _rows(TB, D_MODEL), _rows(TB, D_MODEL)] + [_whole()] * 10,
        [_rows(TB, N_STATE), _rows(TB, N_STATE), _rows(TB, SSM_W), _rows(TB, rest),
         _rows(TB, D_MODEL), _rows(TB, D_MODEL), _rows(TB, D_MODEL), _rows(TB, SSM_W),
         _rows(TB, SSM_W), _rows(TB, SSM_W), _rows(TB, SGU_W), _rows(TB, SSM_W),
         _acc(SUBLANE, SSM_W), _acc(CHUNK, SGU_W),
         pl.BlockSpec((SGU_G, CHUNK, CHUNK), lambda i: (0, 0, 0))],
        [jax.ShapeDtypeStruct((t_len, N_STATE), F32), jax.ShapeDtypeStruct((t_len, N_STATE), F32),
         jax.ShapeDtypeStruct((t_len, SSM_W), F32), jax.ShapeDtypeStruct((t_len, rest), BF16),
         bf_d, bf_d, bf_d, bf_s, bf_s, bf_s, bf_s, bf_s,
         jax.ShapeDtypeStruct((SUBLANE, SSM_W), F32), jax.ShapeDtypeStruct((CHUNK, SGU_W), F32),
         jax.ShapeDtypeStruct((SGU_G, CHUNK, CHUNK), F32)],
        [dx2, p, y0, z, mixed, ya, yb, w_out, w_pa, w_pb, w_glu, cre, cim, ws_st, wst_st, d_skip,
         g_sgu], jobs)


def _scan_bwd(dsr, dsi, str_, sti, tab_rev, jobs=()):
    t_len = dsr.shape[0]
    nblk = t_len // SUBLANE
    lb = SCAN_LANES

    def body(dr_ref, di_ref, sr_ref, si_ref, tab_ref, lr_ref, li_ref, dar_ref, dai_ref):
        tab_v = [tab_ref[q] for q in range(8)]
        row0 = lax.broadcasted_iota(jnp.int32, (SUBLANE, lb), 0) == 0

        def step(k, carry):
            cr, ci, acr, aci = carry
            blocks = [nblk - 1 - (k * SCAN_UNROLL + u) for u in range(SCAN_UNROLL)]
            rows = [pl.multiple_of(kk * SUBLANE, SUBLANE) for kk in blocks]
            local = [_scan_local(dr_ref[pl.ds(r0, SUBLANE), :], di_ref[pl.ds(r0, SUBLANE), :],
                                 tab_v, (7, 6, 4)) for r0 in rows]
            for kk, r0, (xr, xi) in zip(blocks, rows, local):
                xr, xi = _scan_carry(xr, xi, tab_v, cr, ci)
                lr_ref[pl.ds(r0, SUBLANE), :] = xr
                li_ref[pl.ds(r0, SUBLANE), :] = xi
                cr, ci = xr[0:1, :], xi[0:1, :]
                rp = pl.multiple_of(jnp.maximum(kk - 1, 0) * SUBLANE, SUBLANE)
                has_prev = jnp.where(kk > 0, 1.0, 0.0)
                pr = sr_ref[pl.ds(rp, SUBLANE), :][SUBLANE - 1:SUBLANE, :] * has_prev
                pi = si_ref[pl.ds(rp, SUBLANE), :][SUBLANE - 1:SUBLANE, :] * has_prev
                s_r = jnp.where(row0, pr, pltpu.roll(sr_ref[pl.ds(r0, SUBLANE), :], 1, 0))
                s_i = jnp.where(row0, pi, pltpu.roll(si_ref[pl.ds(r0, SUBLANE), :], 1, 0))
                acr = acr + xr * s_r + xi * s_i
                aci = aci + xi * s_r - xr * s_i
            return cr, ci, acr, aci

        zero = jnp.zeros((1, lb), F32)
        zacc = jnp.zeros((SUBLANE, lb), F32)
        _, _, acr, aci = lax.fori_loop(0, nblk // SCAN_UNROLL, step, (zero, zero, zacc, zacc))
        dar_ref[...] = acr
        dai_ref[...] = aci

    col = pl.BlockSpec((t_len, lb), lambda j: (0, j))
    small = pl.BlockSpec((SUBLANE, lb), lambda j: (0, j))
    return _call(
        body, "scan_bwd", (N_STATE // lb,),
        [col, col, col, col, pl.BlockSpec((8, SUBLANE, lb), lambda j: (0, 0, j))],
        [col, col, small, small],
        [jax.ShapeDtypeStruct((t_len, N_STATE), F32)] * 2
        + [jax.ShapeDtypeStruct((SUBLANE, N_STATE), F32)] * 2,
        [dsr, dsi, str_, sti, tab_rev], jobs)


def _bwd_in(lam_r, lam_i, du_part, drest, x, dx2, g_mix, w_in, bre, bim, jobs=()):
    t_len = x.shape[0]
    cs = IN_COLS // N_CHIP

    def body(lr_ref, li_ref, du_ref, dr_ref, x_ref, dx2_ref, g_ref, w_ref, bre_ref, bim_ref,
             gx_ref, dp_ref, sm_ref):
        i = pl.program_id(0)
        du = du_ref[...] + _dot_nt(lr_ref[...], bre_ref[...]) + _dot_nt(li_ref[...], bim_ref[...])
        dp_ref[:, 0:SSM_W] = du.astype(BF16)
        dp_ref[:, SSM_W:] = dr_ref[...]
        dh = jnp.zeros((TB, D_MODEL), F32)
        for k in range(N_CHIP):
            dh = dh + _dot_nt(dp_ref[:, k * cs:(k + 1) * cs], w_ref[k])
        r, xh = _rms_stats(x_ref[...])
        gx_ref[...] = dx2_ref[...] + _rms_bwd(dh * g_ref[...], xh, r)
        upd = jnp.concatenate([jnp.sum(dh * xh, axis=0, keepdims=True),
                               jnp.zeros((SUBLANE - 1, D_MODEL), F32)], axis=0)

        @pl.when(i == 0)
        def _():
            sm_ref[...] = upd

        @pl.when(i > 0)
        def _():
            sm_ref[...] += upd

    return _call(
        body, "bwd_in", (t_len // TB,),
        [_rows(TB, N_STATE), _rows(TB, N_STATE), _rows(TB, SSM_W), _rows(TB, IN_COLS - SSM_W),
         _rows(TB, D_MODEL), _rows(TB, D_MODEL)] + [_whole()] * 4,
        [_rows(TB, D_MODEL), _rows(TB, IN_COLS), _acc(SUBLANE, D_MODEL)],
        [jax.ShapeDtypeStruct((t_len, D_MODEL), F32), jax.ShapeDtypeStruct((t_len, IN_COLS), BF16),
         jax.ShapeDtypeStruct((SUBLANE, D_MODEL), F32)],
        [lam_r, lam_i, du_part, drest, x, dx2, g_mix, w_in, bre, bim], jobs)


def _matmul_tn(a, b, name, out_shape, grid_ij, a_blk, a_map, b_blk, b_map, o_blk, o_map, jobs=()):
    tk = a_blk[0]
    nk = a.shape[0] // tk
    assert nk * tk == a.shape[0] and nk > 0

    def body(a_ref, b_ref, o_ref, acc_ref):
        k = pl.program_id(2)

        @pl.when(k == 0)
        def _():
            acc_ref[...] = jnp.zeros_like(acc_ref)

        acc_ref[...] += lax.dot_general(a_ref[...].astype(BF16), b_ref[...].astype(BF16),
                                        (((0,), (0,)), ((), ())), preferred_element_type=F32)

        @pl.when(k == nk - 1)
        def _():
            o_ref[...] = acc_ref[...]

    outs, per_job = _call(
        body, name, (grid_ij[0], grid_ij[1], nk),
        [pl.BlockSpec(a_blk, a_map), pl.BlockSpec(b_blk, b_map)], [pl.BlockSpec(o_blk, o_map)],
        [jax.ShapeDtypeStruct(out_shape, F32)], [a, b], jobs,
        scratch=[pltpu.VMEM((a_blk[1], b_blk[1]), F32)])
    return outs[0], per_job


def _dw_shards(a, b, name, tk, jobs=()):
    m, n = a.shape[1], b.shape[1]
    tn = n // N_CHIP
    tk = min(tk, a.shape[0])
    return _matmul_tn(a, b, name, (N_CHIP, m, tn), (1, N_CHIP),
                      (tk, m), lambda i, j, k: (k, 0), (tk, tn), lambda i, j, k: (k, j),
                      (None, m, tn), lambda i, j, k: (j, 0, 0), jobs)


def _dw_rows(a, b, name, tm, tk):
    m, n = a.shape[1], b.shape[1]
    tk = min(tk, a.shape[0])
    return _matmul_tn(a, b, name, (m, n), (m // tm, 1),
                      (tk, tm), lambda i, j, k: (k, i), (tk, n), lambda i, j, k: (k, 0),
                      (tm, n), lambda i, j, k: (i, 0))[0]


def _dw_full(a, b, name, n_shard, tk=TK, jobs=()):
    m, n = a.shape[1], b.shape[1]
    tn = n // n_shard
    tk = min(tk, a.shape[0])

    def body(a_ref, b_ref, o_ref):
        k = pl.program_id(0)
        a_t = a_ref[...].astype(BF16).T
        for j in range(n_shard):
            piece = jnp.dot(a_t, b_ref[:, j * tn:(j + 1) * tn].astype(BF16),
                            preferred_element_type=F32)

            @pl.when(k == 0)
            def _(piece=piece, j=j):
                o_ref[j] = piece

            @pl.when(k > 0)
            def _(piece=piece, j=j):
                o_ref[j] += piece

    return _call(body, name, (a.shape[0] // tk,),
                 [pl.BlockSpec((tk, m), lambda k: (k, 0)), pl.BlockSpec((tk, n), lambda k: (k, 0))],
                 [pl.BlockSpec((n_shard, m, tn), lambda k: (0, 0, 0))],
                 [jax.ShapeDtypeStruct((n_shard, m, tn), F32)], [a, b], jobs)


def _dw_pair(a, m, b1, b2, name, jobs=()):
    n = b1.shape[1]
    tk = min(TK, a.shape[0])

    def body(a_ref, b1_ref, b2_ref, o1_ref, o2_ref):
        k = pl.program_id(0)
        a_t = a_ref[...].astype(BF16).T
        for b_ref, o_ref in ((b1_ref, o1_ref), (b2_ref, o2_ref)):
            piece = jnp.dot(a_t, b_ref[...].astype(BF16), preferred_element_type=F32)

            @pl.when(k == 0)
            def _(piece=piece, o_ref=o_ref):
                o_ref[...] = piece

            @pl.when(k > 0)
            def _(piece=piece, o_ref=o_ref):
                o_ref[...] += piece

    tok = pl.BlockSpec((tk, n), lambda k: (k, 0))
    out = pl.BlockSpec((m, n), lambda k: (0, 0))
    return _call(body, name, (a.shape[0] // tk,),
                 [pl.BlockSpec((tk, m), lambda k: (k, 0)), tok, tok], [out, out],
                 [jax.ShapeDtypeStruct((m, n), F32)] * 2, [a, b1, b2], jobs)


def _prefetch_call(body, name, grid, scalars, in_specs, out_specs, out_shape, args):
    return pl.pallas_call(
        body, name=name,
        grid_spec=pltpu.PrefetchScalarGridSpec(num_scalar_prefetch=1, grid=grid, in_specs=in_specs,
                                               out_specs=out_specs),
        out_shape=out_shape, compiler_params=_params(len(grid)),
    )(scalars, *args)


def _place_shard(w, where, name, dtype, tr):
    rows, cols = w.shape

    def body(s_ref, w_ref, o_ref):
        o_ref[...] = w_ref[...].astype(dtype)

    return _prefetch_call(
        body, name, (rows // tr,), where,
        [pl.BlockSpec((tr, cols), lambda i, s: (i, 0))],
        pl.BlockSpec((None, tr, cols), lambda i, s: (s[0], i, 0)),
        jax.ShapeDtypeStruct((N_CHIP, rows, cols), dtype), [w])


def _add_sibling(g, got, where, name):
    _, rs, cs = g.shape
    hr = rs // 2

    def body(s_ref, g_ref, got_ref, o_ref):
        o_ref[...] = (g_ref[...] + got_ref[...]).astype(BF16)

    return _prefetch_call(
        body, name, (N_CHIP,), where,
        [pl.BlockSpec((None, hr, cs), lambda k, s: (k, s[1], 0)),
         pl.BlockSpec((None, hr, cs), lambda k, s: (k, 0, 0))],
        pl.BlockSpec((None, hr, cs), lambda k, s: (k, 0, 0)),
        jax.ShapeDtypeStruct((N_CHIP, hr, cs), BF16), [g, got])


def _add_chips(sums, got, where, name):
    _, hr, cs = sums.shape

    def body(s_ref, own_ref, got_ref, o_ref):
        o_ref[...] = ((own_ref[...].astype(F32) + got_ref[0].astype(F32))
                      + got_ref[1].astype(F32)) + got_ref[2].astype(F32)

    return _prefetch_call(
        body, name, (1,), where,
        [pl.BlockSpec((None, hr, cs), lambda i, s: (s[0], 0, 0)),
         pl.BlockSpec((3, hr, cs), lambda i, s: (0, 0, 0))],
        pl.BlockSpec((hr, cs), lambda i, s: (s[1], 0)),
        jax.ShapeDtypeStruct((2 * hr, cs), F32), [sums, got])


def _small_allreduce(pack):
    rows = pack.shape[0]
    half = rows // 2

    def body(in_ref, out_ref, sib_ref, slots_ref, s_a, r_a, s_b, r_b, s_c, r_c):
        x, y, c, chips = _place()
        k_me = 2 * x + y
        sib = (x, y, 1 - c)
        first = _remote(in_ref, sib_ref, s_a, r_a, sib)
        first.start()
        first.wait()
        mine = _half(rows, c)
        slots_ref[k_me] = in_ref[mine, :] + sib_ref[mine, :]
        cps = [_remote(slots_ref.at[k_me], slots_ref.at[k_me], s_b.at[j], r_b.at[j], (*ch, c))
               for j, ch in enumerate(chips)]
        for cp in cps:
            cp.start()
        for j, ch in enumerate(chips):
            slot = slots_ref.at[_chip_index(ch)]
            _remote(slot, slot, s_b.at[j], r_b.at[j], (*ch, c)).wait_recv()
        for cp in cps:
            cp.wait_send()
        out_ref[mine, :] = ((slots_ref[0] + slots_ref[1]) + slots_ref[2]) + slots_ref[3]
        last = _remote(out_ref.at[mine, :], out_ref.at[mine, :], s_c, r_c, sib)
        last.start()
        theirs = out_ref.at[_half(rows, 1 - c), :]
        _remote(theirs, theirs, s_c, r_c, sib).wait_recv()
        last.wait_send()

    return pl.pallas_call(
        body, name="small_allreduce", in_specs=[_whole()], out_specs=_whole(),
        out_shape=jax.ShapeDtypeStruct(pack.shape, F32),
        scratch_shapes=[pltpu.VMEM(pack.shape, F32), pltpu.VMEM((N_CHIP, half, LANE), F32),
                        pltpu.SemaphoreType.DMA, pltpu.SemaphoreType.DMA,
                        pltpu.SemaphoreType.DMA((3,)), pltpu.SemaphoreType.DMA((3,)),
                        pltpu.SemaphoreType.DMA, pltpu.SemaphoreType.DMA],
        compiler_params=_params(0),
    )(pack)


def _adamw_update(w_ref, g_ref, m_ref, v_ref, d_ref, mo_ref, vo_ref):
    gv = g_ref[...]
    mn = ADAM_B1 * m_ref[...] + (1.0 - ADAM_B1) * gv
    vn = ADAM_B2 * v_ref[...] + (1.0 - ADAM_B2) * (gv * gv)
    mo_ref[...] = mn
    vo_ref[...] = vn
    m_hat = mn / (1.0 - ADAM_B1 ** ADAM_STEP)
    v_hat = vn / (1.0 - ADAM_B2 ** ADAM_STEP)
    d_ref[...] = -ADAM_LR * (m_hat / (jnp.sqrt(v_hat) + ADAM_EPS) + ADAM_WD * w_ref[...])


def _adamw(w, g, m, v, name, tr):
    rows, cols = w.shape
    blk = _rows(tr, cols)
    return _call(_adamw_update, name, (rows // tr,), [blk] * 4, [blk] * 3,
                 [jax.ShapeDtypeStruct(w.shape, F32)] * 3, [w, g, m, v])[0]


def _adamw_many(ws, gs, ms, vs, name):
    n = len(ws)

    def body(*refs):
        for t in range(n):
            _adamw_update(*[refs[q * n + t] for q in range(7)])

    outs = pl.pallas_call(
        body, name=name, in_specs=[_whole()] * (4 * n), out_specs=[_whole()] * (3 * n),
        out_shape=[jax.ShapeDtypeStruct(a.shape, F32) for _ in range(3) for a in ws],
        compiler_params=_params(0),
    )(*ws, *gs, *ms, *vs)
    return outs[:n], outs[n:2 * n], outs[2 * n:]


def _ssm_discretize(a_re, a_im, log_dt, b_re, b_im):
    dt = jnp.exp(log_dt)[:, None]
    mag = jnp.exp(dt * a_re)
    abr = mag * jnp.cos(dt * a_im)
    abi = mag * jnp.sin(dt * a_im)
    den = a_re * a_re + a_im * a_im
    nr = abr - 1.0
    ni = abi
    f_re = (nr * a_re + ni * a_im) / den
    f_im = (ni * a_re - nr * a_im) / den
    bbr = f_re[..., None] * b_re - f_im[..., None] * b_im
    bbi = f_re[..., None] * b_im + f_im[..., None] * b_re
    return abr, abi, bbr, bbi


def _scan_tables(abr, abi):
    ar = abr.reshape(1, N_STATE)
    ai = abi.reshape(1, N_STATE)
    pr, pi = [ar], [ai]
    for _ in range(SUBLANE - 1):
        pr, pi = pr + [pr[-1] * ar - pi[-1] * ai], pi + [pr[-1] * ai + pi[-1] * ar]
    row = jnp.arange(SUBLANE)[:, None]
    tabs = []
    for d in (1, 2, 4):
        tabs.append(jnp.where(row >= d, pr[d - 1], 0.0))
        tabs.append(jnp.where(row >= d, pi[d - 1], 0.0))
    tabs.append(jnp.concatenate(pr, axis=0))
    tabs.append(jnp.concatenate(pi, axis=0))
    fwd = jnp.stack(tabs)
    sign = jnp.array([1.0, -1.0] * 4, F32)[:, None, None]
    return fwd, fwd[:, ::-1, :] * sign


def _block_diag_b(bb):
    eye = jnp.eye(SSM_G, dtype=F32)
    return jnp.einsum("gph,gk->ghkp", bb, eye).reshape(SSM_W, N_STATE)


def _block_diag_c(cc):
    eye = jnp.eye(SSM_G, dtype=F32)
    return jnp.einsum("ghp,gk->gpkh", cc, eye).reshape(N_STATE, SSM_W)


SMALL_SHAPES = {
    "g_mix": (D_MODEL,), "a_re": (SSM_G, SSM_P), "a_im": (SSM_G, SSM_P), "log_dt": (SSM_G,),
    "b_re": (SSM_G, SSM_P, SSM_H), "b_im": (SSM_G, SSM_P, SSM_H),
    "c_re": (SSM_G, SSM_H, SSM_P), "c_im": (SSM_G, SSM_H, SSM_P),
    "d_skip": (SSM_W,), "b_glu": (SSM_W,), "g_sgu": (SGU_W,), "w_s": (SGU_G, CHUNK, CHUNK),
    "b_s": (SGU_G, CHUNK), "g_ffn": (D_MODEL,), "conv_b": (2 * D_FF,), "g_final": (D_MODEL,),
}
PACK_ITEMS = [("loss", (1,))] + [(n, SMALL_SHAPES[n]) for n in SMALL] + [("conv_w", (3, 2 * D_FF))]
TILE = SUBLANE * LANE


def _item_rows(shape):
    return -(-math.prod(shape) // TILE) * SUBLANE


PACK_ROWS = -(-sum(_item_rows(s) for _, s in PACK_ITEMS) // (2 * SUBLANE)) * (2 * SUBLANE)


def _pack(values):
    parts, used = [], 0
    for name, shape in PACK_ITEMS:
        size, rows = math.prod(shape), _item_rows(shape)
        if name in values:
            flat = values[name].astype(F32).reshape(size)
            if rows * LANE > size:
                flat = jnp.pad(flat, (0, rows * LANE - size))
            parts.append(flat.reshape(rows, LANE))
        else:
            parts.append(jnp.zeros((rows, LANE), F32))
        used += rows
    if PACK_ROWS > used:
        parts.append(jnp.zeros((PACK_ROWS - used, LANE), F32))
    return jnp.concatenate(parts, axis=0)


def _unpack(pack):
    out, off = {}, 0
    for name, shape in PACK_ITEMS:
        rows = _item_rows(shape)
        out[name] = pack[off:off + rows].reshape(rows * LANE)[:math.prod(shape)].reshape(shape)
        off += rows
    return out


PLACE_ROWS = {"w_in": 256, "w_up": 256, "w_down": 352, "w_out": 256, "w_proj_a": 256,
              "w_proj_b": 256, "w_glu": 128}


def kernel(x, g_mix, w_in, a_re, a_im, log_dt, b_re, b_im, c_re, c_im, d_skip, w_glu, b_glu, w_proj_a, g_sgu, w_s, b_s, w_proj_b, w_out, g_ffn, w_up, conv_w, conv_b, w_down, g_final, loss_target, m_g_mix, m_w_in, m_a_re, m_a_im, m_log_dt, m_b_re, m_b_im, m_c_re, m_c_im, m_d_skip, m_w_glu, m_b_glu, m_w_proj_a, m_g_sgu, m_w_s, m_b_s, m_w_proj_b, m_w_out, m_g_ffn, m_w_up, m_conv_w, m_conv_b, m_w_down, m_g_final, v_g_mix, v_w_in, v_a_re, v_a_im, v_log_dt, v_b_re, v_b_im, v_c_re, v_c_im, v_d_skip, v_w_glu, v_b_glu, v_w_proj_a, v_g_sgu, v_w_s, v_b_s, v_w_proj_b, v_w_out, v_g_ffn, v_w_up, v_conv_w, v_conv_b, v_w_down, v_g_final):
    given = dict(locals())
    w = {n: given[n] for n in WEIGHTS}
    m = {n: given["m_" + n] for n in WEIGHTS}
    v = {n: given["v_" + n] for n in WEIGHTS}

    def shard2d(a):
        return a.reshape(a.shape[-2], a.shape[-1])

    chip = 2 * lax.axis_index("x") + lax.axis_index("y")
    where = jnp.stack([chip, lax.axis_index("c")]).astype(jnp.int32)
    xs, target = x[0], loss_target[0]
    small = {n: w[n].reshape(SMALL_SHAPES[n]) for n in SMALL}

    (abr, abi, bbr, bbi), disc_vjp = jax.vjp(_ssm_discretize, small["a_re"], small["a_im"],
                                             small["log_dt"], small["b_re"], small["b_im"])
    tab_f, tab_r = _scan_tables(abr, abi)
    bre = _block_diag_b(bbr).astype(BF16)
    bim = _block_diag_b(bbi).astype(BF16)
    cre = _block_diag_c(small["c_re"]).astype(BF16)
    cim = _block_diag_c(small["c_im"]).astype(BF16)
    tril = jnp.tril(jnp.ones((CHUNK, CHUNK), dtype=bool))
    ws = jnp.where(tril[None], small["w_s"], 0.0)
    ws_st = ws.reshape(SGU_G // 2, 2 * CHUNK, CHUNK).astype(BF16)
    wst_st = ws.transpose(0, 2, 1).reshape(SGU_G // 2, 2 * CHUNK, CHUNK).astype(BF16)
    bmat = jnp.repeat(small["b_s"].T, SGU_D, axis=1)
    g_mix2 = small["g_mix"].reshape(1, D_MODEL)
    g_ffn2 = small["g_ffn"].reshape(1, D_MODEL)
    g_final2 = small["g_final"].reshape(1, D_MODEL)
    g_sgu2 = small["g_sgu"].reshape(1, SGU_W)
    d_skip2 = small["d_skip"].reshape(1, SSM_W)
    b_glu2 = small["b_glu"].reshape(1, SSM_W)
    conv_b2 = small["conv_b"].reshape(1, 2 * D_FF)

    gat = {n: _place_shard(shard2d(w[n]), where, "place_" + n, BF16, PLACE_ROWS[n]) for n in BIG}
    gat["conv_w"] = _place_shard(shard2d(w["conv_w"]), where, "place_conv_w", F32, 3)
    (gat["w_in"],), = _comm("gather_in_ici", [_job_gather_ici([gat["w_in"]])])
    (gat["w_in"],), = _comm("gather_in_sibling", [_job_gather_sibling([gat["w_in"]])])
    early = ["w_glu", "w_proj_a", "w_proj_b", "w_out", "w_down", "conv_w"]

    (p, h1, bur, bui), (got,) = _fwd_in(
        xs, g_mix2, gat["w_in"], bre, bim,
        [_job_gather_ici([gat[n] for n in early], whole=(5,))])
    gat.update(zip(early, got))
    (str_, sti), (got_e, (gat["w_up"],)) = _scan_fwd(
        bur, bui, tab_f,
        [_job_gather_sibling([gat[n] for n in early[:5]]), _job_gather_ici([gat["w_up"]])])
    gat.update(zip(early[:5], got_e))
    w_glu_f = gat["w_glu"].reshape(SSM_W, SSM_W)
    w_out_f = gat["w_out"].reshape(D_MODEL, D_MODEL)
    conv_w_f = gat["conv_w"].transpose(1, 0, 2).reshape(3, 2 * D_FF)
    (x2, y0, z, mixed, ya, yb), ((gat["w_up"],),) = _fwd_mix(
        xs, p, str_, sti, cre, cim, d_skip2, w_glu_f, b_glu2, gat["w_proj_a"], g_sgu2, ws_st, bmat,
        gat["w_proj_b"], w_out_f, [_job_gather_sibling([gat["w_up"]])])
    w_down_f = gat["w_down"].reshape(D_FF, D_MODEL)
    up, f, h2, dx3, sm_ffn = _fwd_ffn(x2, target, g_ffn2, gat["w_up"], conv_w_f, conv_b2, w_down_f,
                                      g_final2)

    def leg1_done(names, got):
        return [_add_sibling(part[n], s, where, "add_sibling_" + n) for n, s in zip(names, got)]

    def leg2_done(names, sums, got):
        return [_add_chips(s, o, where, "add_chips_" + n) for n, s, o in zip(names, sums, got)]

    part, red = {}, {}
    part["w_down"] = _dw_rows(f, dx3, "dw_down", D_FF // 2, 2 * TK).reshape(
        N_CHIP, D_FF // N_CHIP, D_MODEL)
    (dx2, dup, sm_conv, sm_gffn), (got,) = _bwd_ffn(
        dx3, up, x2, g_ffn2, gat["w_up"], conv_w_f, conv_b2, w_down_f,
        [_job_sibling_halves([part["w_down"]])])
    sum_down = leg1_done(["w_down"], got)
    part["w_up"], (got,) = _dw_shards(h2, dup, "dw_up", 4 * TK, [_job_to_owner(sum_down)])
    red_down = leg2_done(["w_down"], sum_down, got)
    ((dsr, dsi, du_part, drest, mrg, dya, dyb, yap, dz, y1, sgu, dy0, sm_mix, dbm, dws),
     (got, (red["w_down"],))) = _bwd_mix(
        dx2, p, y0, z, mixed, ya, yb, w_out_f, gat["w_proj_a"], gat["w_proj_b"], w_glu_f, cre, cim,
        ws_st, wst_st, d_skip2, g_sgu2,
        [_job_sibling_halves([part["w_up"]]), _job_swap_halves(red_down)])
    sum_up = leg1_done(["w_up"], got)
    (lam_r, lam_i, dar8, dai8), (got,) = _scan_bwd(dsr, dsi, str_, sti, tab_r, [_job_to_owner(sum_up)])
    red_up = leg2_done(["w_up"], sum_up, got)
    mix4 = ["w_out", "w_proj_a", "w_proj_b", "w_glu"]
    part["w_out"] = _dw_full(mrg, dx2, "dw_out", 1)[0][0].reshape(N_CHIP, D_MODEL // N_CHIP, D_MODEL)
    part["w_proj_a"] = _dw_full(yap, dya, "dw_proj_a", N_CHIP)[0][0]
    part["w_proj_b"] = _dw_full(sgu, dyb, "dw_proj_b", N_CHIP)[0][0]
    part["w_glu"] = _dw_full(y1, dz, "dw_glu", 1)[0][0].reshape(N_CHIP, SSM_W // N_CHIP, SSM_W)
    (grad_x, dp, sm_gmix), (got, (red["w_up"],)) = _bwd_in(
        lam_r, lam_i, du_part, drest, xs, dx2, g_mix2, gat["w_in"], bre, bim,
        [_job_sibling_halves([part[n] for n in mix4]), _job_swap_halves(red_up)])
    sums_m = leg1_done(mix4, got)
    (part["w_in"],), (got,) = _dw_full(h1, dp, "dw_in", N_CHIP, 2 * TK, [_job_to_owner(sums_m)])
    red_m = leg2_done(mix4, sums_m, got)
    (dbd_r, dbd_i), (got, done_m) = _dw_pair(
        p, SSM_W, lam_r, lam_i, "db_bar",
        [_job_sibling_halves([part["w_in"]]), _job_swap_halves(red_m)])
    red.update(zip(mix4, done_m))
    sum_in = leg1_done(["w_in"], got)
    (dcd_r, dcd_i), (got,) = _dw_pair(dy0, SSM_W, str_, sti, "dc", [_job_to_owner(sum_in)])
    red_in = leg2_done(["w_in"], sum_in, got)
    (red["w_in"],), = _comm("swap_w_in", [_job_swap_halves(red_in)])

    def pick_b(full):
        return jnp.einsum("ghgp->gph", full.reshape(SSM_G, SSM_H, SSM_G, SSM_P))

    def pick_c(full):
        return jnp.einsum("ghgp->ghp", full.reshape(SSM_G, SSM_H, SSM_G, SSM_P))

    dabr = jnp.sum(dar8, axis=0).reshape(SSM_G, SSM_P)
    dabi = jnp.sum(dai8, axis=0).reshape(SSM_G, SSM_P)
    d_a_re, d_a_im, d_log_dt, d_b_re, d_b_im = disc_vjp((dabr, dabi, pick_b(dbd_r), pick_b(dbd_i)))
    gsmall = {
        "g_mix": sm_gmix[0], "a_re": d_a_re, "a_im": d_a_im, "log_dt": d_log_dt,
        "b_re": d_b_re, "b_im": d_b_im, "c_re": pick_c(dcd_r), "c_im": -pick_c(dcd_i),
        "d_skip": sm_mix[0], "b_glu": sm_mix[1], "g_sgu": sm_mix[2],
        "w_s": jnp.where(tril[None], dws, 0.0),
        "b_s": dbm.reshape(CHUNK, SGU_G, SGU_D).sum(-1).T,
        "g_ffn": sm_gffn[0], "conv_b": sm_conv[3], "g_final": sm_ffn[0],
        "conv_w": sm_conv[0:3], "loss": sm_ffn[1, 0:1],
    }

    total_pack = _small_allreduce(_pack(gsmall))
    total = _unpack(total_pack)
    grads = dict(red)
    cs = 2 * D_FF // N_CHIP
    grads["conv_w"] = lax.dynamic_slice(total["conv_w"], (0, chip * cs), (3, cs))
    for n in SMALL:
        grads[n] = total[n]
    delta, new_m, new_v = {}, {}, {}
    for n in BIG + ("conv_w",):
        delta[n], new_m[n], new_v[n] = _adamw(shard2d(w[n]), grads[n], shard2d(m[n]), shard2d(v[n]),
                                              "adamw_" + n, PLACE_ROWS.get(n, 3))
    def flat2d(a):
        return a.reshape(-1, a.shape[-1])

    ud, um, uv = _adamw_many(*[[flat2d(d[n]) for n in SMALL] for d in (w, total, m, v)], "adamw_small")
    for i, n in enumerate(SMALL):
        delta[n], new_m[n], new_v[n] = ud[i], um[i], uv[i]

    def like(d):
        return [d[n].reshape(w[n].shape) for n in WEIGHTS]

    return (total["loss"].reshape(()), grad_x.reshape(x.shape), *like(grads), *like(delta),
            *like(new_m), *like(new_v))
```

```python
import math

import jax
import jax.numpy as jnp
from jax import lax
from jax.experimental import pallas as pl
from jax.experimental.pallas import tpu as pltpu

F32 = jnp.float32
BF16 = jnp.bfloat16
MESH = pl.DeviceIdType.MESH

D_MODEL = 1024
SSM_W = 512
SSM_G = 32
SSM_H = 16
SSM_P = 64
N_STATE = SSM_G * SSM_P
SGU_W = 512
SGU_G = 8
SGU_D = 64
CHUNK = 128
D_FF = 2816
IN_COLS = 3584
EPS = 1e-6
N_CHIP = 4

ADAM_LR = 0.001
ADAM_B1 = 0.9
ADAM_B2 = 0.999
ADAM_EPS = 1e-08
ADAM_WD = 0.01
ADAM_STEP = 10

SUBLANE = 8
LANE = 128
VMEM_LIMIT = 56 * 1024 * 1024
TB = 256
TK = 512
SCAN_LANES = 256
SCAN_UNROLL = 4
HALO = SUBLANE

BIG = ("w_in", "w_up", "w_down", "w_out", "w_proj_a", "w_proj_b", "w_glu")
SMALL = ("g_mix", "a_re", "a_im", "log_dt", "b_re", "b_im", "c_re", "c_im", "d_skip", "b_glu",
         "g_sgu", "w_s", "b_s", "g_ffn", "conv_b", "g_final")
WEIGHTS = ("g_mix", "w_in", "a_re", "a_im", "log_dt", "b_re", "b_im", "c_re", "c_im", "d_skip",
           "w_glu", "b_glu", "w_proj_a", "g_sgu", "w_s", "b_s", "w_proj_b", "w_out", "g_ffn",
           "w_up", "conv_w", "conv_b", "w_down", "g_final")

ANY = pl.BlockSpec(memory_space=pl.ANY)


def _params(n_grid):
    return pltpu.CompilerParams(dimension_semantics=("arbitrary",) * n_grid if n_grid else None,
                                vmem_limit_bytes=VMEM_LIMIT)


def _whole():
    return pl.BlockSpec(memory_space=pltpu.VMEM)


def _rows(tb, ncol):
    return pl.BlockSpec((tb, ncol), lambda i: (i, 0))


def _acc(nrow, ncol):
    return pl.BlockSpec((nrow, ncol), lambda i: (0, 0))


def _dot(a, b):
    return jnp.dot(a.astype(BF16), b.astype(BF16), preferred_element_type=F32)


def _dot_nt(a, b):
    return lax.dot_general(a.astype(BF16), b.astype(BF16), (((1,), (1,)), ((), ())),
                           preferred_element_type=F32)


def _sigmoid(v):
    return 0.5 * jnp.tanh(0.5 * v) + 0.5


_GELU_C = math.sqrt(2.0 / math.pi)


def _gelu(v):
    return 0.5 * v * (1.0 + jnp.tanh(_GELU_C * (v + 0.044715 * v * v * v)))


def _gelu_and_grad(v):
    v2 = v * v
    t = jnp.tanh(_GELU_C * v * (1.0 + 0.044715 * v2))
    half = 0.5 * (1.0 + t)
    return v * half, half + 0.5 * v * (1.0 - t * t) * _GELU_C * (1.0 + 3.0 * 0.044715 * v2)


def _rms_stats(v):
    r = lax.rsqrt(jnp.mean(v * v, axis=-1, keepdims=True) + EPS)
    return r, v * r


def _rms_bwd(dxh, xh, r):
    return r * (dxh - xh * jnp.mean(dxh * xh, axis=-1, keepdims=True))


def _place():
    x, y, c = lax.axis_index("x"), lax.axis_index("y"), lax.axis_index("c")
    chips = [(1 - x, y), (x, 1 - y), (1 - x, 1 - y)]
    return x, y, c, chips


def _chip_index(chip):
    return 2 * chip[0] + chip[1]


def _remote(src, dst, send_sem, recv_sem, device):
    return pltpu.make_async_remote_copy(src_ref=src, dst_ref=dst, send_sem=send_sem,
                                        recv_sem=recv_sem, device_id=device, device_id_type=MESH)


def _half(ref_rows, c):
    hr = ref_rows // 2
    return pl.ds(pl.multiple_of(c * hr, SUBLANE), hr)


class _Job:
    def __init__(self, start, finish, n_sem, ins=(), inouts=(), outs=()):
        self.start, self.finish, self.n_sem = start, finish, n_sem
        self.ins, self.inouts, self.outs = list(ins), list(inouts), list(outs)


def _job_gather_ici(bufs, whole=()):
    n = len(bufs)

    def copies(io):
        x, y, c, chips = _place()
        k_me = 2 * x + y
        out = []
        for t in range(n):
            for j, ch in enumerate(chips):
                if t in whole:
                    src, land = io[t].at[k_me], io[t].at[_chip_index(ch)]
                else:
                    mine = _half(bufs[t].shape[1], c)
                    src, land = io[t].at[k_me, mine, :], io[t].at[_chip_index(ch), mine, :]
                out.append((src, land, 3 * t + j, (*ch, c)))
        return out

    def start(ins, io, outs, ssem, rsem):
        for src, _, i, dev in copies(io):
            _remote(src, src, ssem(i), rsem(i), dev).start()

    def finish(ins, io, outs, ssem, rsem):
        cps = copies(io)
        for _, land, i, dev in cps:
            _remote(land, land, ssem(i), rsem(i), dev).wait_recv()
        for src, _, i, dev in cps:
            _remote(src, src, ssem(i), rsem(i), dev).wait_send()

    return _Job(start, finish, 3 * n, inouts=bufs)


def _job_gather_sibling(bufs):
    n = len(bufs)

    def copies(io):
        x, y, c, chips = _place()
        out = []
        for t in range(n):
            rows = bufs[t].shape[1]
            for j, ch in enumerate(chips):
                k = _chip_index(ch)
                out.append((io[t].at[k, _half(rows, c), :], io[t].at[k, _half(rows, 1 - c), :],
                            3 * t + j, (x, y, 1 - c)))
        return out

    def start(ins, io, outs, ssem, rsem):
        for src, _, i, dev in copies(io):
            _remote(src, src, ssem(i), rsem(i), dev).start()

    def finish(ins, io, outs, ssem, rsem):
        cps = copies(io)
        for _, land, i, dev in cps:
            _remote(land, land, ssem(i), rsem(i), dev).wait_recv()
        for src, _, i, dev in cps:
            _remote(src, src, ssem(i), rsem(i), dev).wait_send()

    return _Job(start, finish, 3 * n, inouts=bufs)


def _job_sibling_halves(grads):
    n = len(grads)

    def build(ins, outs, ssem, rsem):
        x, y, c, _ = _place()
        return [_remote(ins[t].at[:, _half(grads[t].shape[1], 1 - c), :], outs[t], ssem(t), rsem(t),
                        (x, y, 1 - c)) for t in range(n)]

    def start(ins, io, outs, ssem, rsem):
        for cp in build(ins, outs, ssem, rsem):
            cp.start()

    def finish(ins, io, outs, ssem, rsem):
        for cp in build(ins, outs, ssem, rsem):
            cp.wait()

    return _Job(start, finish, n, ins=grads,
                outs=[jax.ShapeDtypeStruct((N_CHIP, g.shape[1] // 2, g.shape[2]), F32) for g in grads])


def _job_to_owner(sums):
    n = len(sums)

    def build(ins, outs, ssem, rsem):
        x, y, c, chips = _place()
        return [_remote(ins[t].at[_chip_index(ch)], outs[t].at[j], ssem(3 * t + j), rsem(3 * t + j),
                        (*ch, c)) for t in range(n) for j, ch in enumerate(chips)]

    def start(ins, io, outs, ssem, rsem):
        for cp in build(ins, outs, ssem, rsem):
            cp.start()

    def finish(ins, io, outs, ssem, rsem):
        for cp in build(ins, outs, ssem, rsem):
            cp.wait()

    return _Job(start, finish, 3 * n, ins=sums,
                outs=[jax.ShapeDtypeStruct((3,) + s.shape[1:], s.dtype) for s in sums])


def _job_swap_halves(bufs):
    n = len(bufs)

    def start(ins, io, outs, ssem, rsem):
        x, y, c, _ = _place()
        for t in range(n):
            mine = io[t].at[_half(bufs[t].shape[0], c), :]
            _remote(mine, mine, ssem(t), rsem(t), (x, y, 1 - c)).start()

    def finish(ins, io, outs, ssem, rsem):
        x, y, c, _ = _place()
        for t in range(n):
            theirs = io[t].at[_half(bufs[t].shape[0], 1 - c), :]
            _remote(theirs, theirs, ssem(t), rsem(t), (x, y, 1 - c)).wait_recv()
        for t in range(n):
            mine = io[t].at[_half(bufs[t].shape[0], c), :]
            _remote(mine, mine, ssem(t), rsem(t), (x, y, 1 - c)).wait_send()

    return _Job(start, finish, n, inouts=bufs)


def _call(body, name, grid, in_specs, out_specs, out_shape, args, jobs=(), scratch=()):
    n_in, n_out, n_scr = len(args), len(out_shape), len(scratch)
    job_in = [a for jb in jobs for a in jb.ins + jb.inouts]
    job_out = [s for jb in jobs
               for s in [jax.ShapeDtypeStruct(a.shape, a.dtype) for a in jb.inouts] + jb.outs]
    aliases, pos_in, pos_out = {}, n_in, n_out
    for jb in jobs:
        pos_in += len(jb.ins)
        for _ in jb.inouts:
            aliases[pos_in] = pos_out
            pos_in += 1
            pos_out += 1
        pos_out += len(jb.outs)
    n_sem = sum(jb.n_sem for jb in jobs)

    def wrapped(*refs):
        c_in = refs[:n_in]
        j_in = refs[n_in:n_in + len(job_in)]
        c_out = refs[n_in + len(job_in):n_in + len(job_in) + n_out]
        j_out = refs[n_in + len(job_in) + n_out:n_in + len(job_in) + n_out + len(job_out)]
        rest = refs[n_in + len(job_in) + n_out + len(job_out):]
        c_scr = rest[:n_scr]
        views, pi, po, ps = [], 0, 0, 0
        for jb in jobs:
            ins = j_in[pi:pi + len(jb.ins)]
            pi += len(jb.ins) + len(jb.inouts)
            io = j_out[po:po + len(jb.inouts)]
            new = j_out[po + len(jb.inouts):po + len(jb.inouts) + len(jb.outs)]
            po += len(jb.inouts) + len(jb.outs)
            send = (lambda i, o=ps: rest[n_scr].at[o + i])
            recv = (lambda i, o=ps: rest[n_scr + 1].at[o + i])
            ps += jb.n_sem
            views.append((ins, io, new, send, recv))

        def run(which):
            for jb, vw in zip(jobs, views):
                (jb.start if which == 0 else jb.finish)(*vw)

        if not grid:
            run(0)
            run(1)
            return
        if jobs:
            first = pl.program_id(0) == 0
            last = pl.program_id(0) == grid[0] - 1
            for d in range(1, len(grid)):
                first = jnp.logical_and(first, pl.program_id(d) == 0)
                last = jnp.logical_and(last, pl.program_id(d) == grid[d] - 1)
            pl.when(first)(lambda: run(0))
        body(*c_in, *c_out, *c_scr)
        if jobs:
            pl.when(last)(lambda: run(1))

    sems = [pltpu.SemaphoreType.DMA((n_sem,)), pltpu.SemaphoreType.DMA((n_sem,))] if jobs else []
    kwargs = dict(grid=grid) if grid else {}
    res = pl.pallas_call(
        wrapped, name=name, in_specs=list(in_specs) + [ANY] * len(job_in),
        out_specs=list(out_specs) + [ANY] * len(job_out),
        out_shape=list(out_shape) + job_out, scratch_shapes=list(scratch) + sems,
        input_output_aliases=aliases, compiler_params=_params(len(grid)), **kwargs,
    )(*args, *job_in)
    outs, pos, per_job = list(res[:n_out]), n_out, []
    for jb in jobs:
        k = len(jb.inouts) + len(jb.outs)
        per_job.append(list(res[pos:pos + k]))
        pos += k
    return outs, per_job


def _comm(name, jobs):
    return _call(None, name, (), [], [], [], [], jobs)[1]


def _fwd_in(x, g_mix, w_in, bre, bim, jobs=()):
    t_len = x.shape[0]
    cs = IN_COLS // N_CHIP

    def body(x_ref, g_ref, w_ref, bre_ref, bim_ref, p_ref, h_ref, bur_ref, bui_ref):
        xv = x_ref[...]
        r, xh = _rms_stats(xv)
        h = (xh * g_ref[...]).astype(BF16)
        h_ref[...] = h
        for k in range(N_CHIP):
            p_ref[:, k * cs:(k + 1) * cs] = jnp.dot(h, w_ref[k], preferred_element_type=F32)
        u = p_ref[:, 0:SSM_W].astype(BF16)
        bur_ref[...] = jnp.dot(u, bre_ref[...], preferred_element_type=F32)
        bui_ref[...] = jnp.dot(u, bim_ref[...], preferred_element_type=F32)

    return _call(
        body, "fwd_in", (t_len // TB,),
        [_rows(TB, D_MODEL), _whole(), _whole(), _whole(), _whole()],
        [_rows(TB, IN_COLS), _rows(TB, D_MODEL), _rows(TB, N_STATE), _rows(TB, N_STATE)],
        [jax.ShapeDtypeStruct((t_len, IN_COLS), F32), jax.ShapeDtypeStruct((t_len, D_MODEL), BF16),
         jax.ShapeDtypeStruct((t_len, N_STATE), F32), jax.ShapeDtypeStruct((t_len, N_STATE), F32)],
        [x, g_mix, w_in, bre, bim], jobs)


def _scan_local(xr, xi, tab, shifts):
    for q, s in enumerate(shifts):
        ar, ai = tab[2 * q], tab[2 * q + 1]
        rr = pltpu.roll(xr, s, 0)
        ri = pltpu.roll(xi, s, 0)
        xr, xi = xr + ar * rr - ai * ri, xi + ar * ri + ai * rr
    return xr, xi


def _scan_carry(xr, xi, tab, cr, ci):
    pr, pi = tab[6], tab[7]
    return xr + pr * cr - pi * ci, xi + pr * ci + pi * cr


def _scan_fwd(bur, bui, tab, jobs=()):
    t_len = bur.shape[0]
    nblk = t_len // SUBLANE
    lb = SCAN_LANES

    def body(br_ref, bi_ref, tab_ref, sr_ref, si_ref):
        tab_v = [tab_ref[q] for q in range(8)]

        def step(k, carry):
            cr, ci = carry
            rows = [pl.multiple_of((k * SCAN_UNROLL + u) * SUBLANE, SUBLANE)
                    for u in range(SCAN_UNROLL)]
            local = [_scan_local(br_ref[pl.ds(r0, SUBLANE), :], bi_ref[pl.ds(r0, SUBLANE), :],
                                 tab_v, (1, 2, 4)) for r0 in rows]
            for r0, (xr, xi) in zip(rows, local):
                xr, xi = _scan_carry(xr, xi, tab_v, cr, ci)
                sr_ref[pl.ds(r0, SUBLANE), :] = xr
                si_ref[pl.ds(r0, SUBLANE), :] = xi
                cr, ci = xr[SUBLANE - 1:SUBLANE, :], xi[SUBLANE - 1:SUBLANE, :]
            return cr, ci

        zero = jnp.zeros((1, lb), F32)
        lax.fori_loop(0, nblk // SCAN_UNROLL, step, (zero, zero))

    col = pl.BlockSpec((t_len, lb), lambda j: (0, j))
    return _call(
        body, "scan_fwd", (N_STATE // lb,),
        [col, col, pl.BlockSpec((8, SUBLANE, lb), lambda j: (0, 0, j))], [col, col],
        [jax.ShapeDtypeStruct((t_len, N_STATE), F32)] * 2, [bur, bui, tab], jobs)


def _sgu_mix(v, ws_ref, lane_lo):
    rows = []
    for c0 in range(0, v.shape[0], CHUNK):
        slabs = []
        for j in range(SGU_W // LANE):
            prod = jnp.dot(ws_ref[j], v[c0:c0 + CHUNK, j * LANE:(j + 1) * LANE].astype(BF16),
                           preferred_element_type=F32)
            slabs.append(jnp.where(lane_lo, prod[:CHUNK], prod[CHUNK:]))
        rows.append(jnp.concatenate(slabs, axis=1))
    return jnp.concatenate(rows, axis=0) if len(rows) > 1 else rows[0]


def _fwd_mix(x, p, str_, sti, cre, cim, d_skip, w_glu, b_glu, w_pa, g_sgu, ws_st, bmat, w_pb, w_out,
             jobs=()):
    t_len = x.shape[0]
    pc = D_MODEL // N_CHIP

    def body(x_ref, p_ref, sr_ref, si_ref, cre_ref, cim_ref, dsk_ref, wg_ref, bg_ref, wpa_ref,
             gs_ref, ws_ref, bm_ref, wpb_ref, wo_ref,
             x2_ref, y0_ref, z_ref, mx_ref, ya_ref, yb_ref):
        u = p_ref[:, 0:SSM_W]
        y0 = _dot(sr_ref[...], cre_ref[...]) - _dot(si_ref[...], cim_ref[...]) + dsk_ref[...] * u
        y0_ref[...] = y0
        y1 = _gelu(y0)
        z = _dot(y1, wg_ref[...]) + bg_ref[...]
        z_ref[...] = z
        ya_pre = (y1 * _sigmoid(z)).astype(BF16)
        for k in range(N_CHIP):
            ya_ref[:, k * pc:(k + 1) * pc] = jnp.dot(ya_pre, wpa_ref[k], preferred_element_type=F32)

        uvg = _gelu(p_ref[:, SSM_W:SSM_W + 2 * SGU_W])
        u2 = uvg[:, :SGU_W]
        _, vh = _rms_stats(uvg[:, SGU_W:])
        v3 = vh * gs_ref[...]
        lane_lo = lax.broadcasted_iota(jnp.int32, (CHUNK, LANE), 1) < SGU_D
        bias = jnp.concatenate([bm_ref[...]] * (TB // CHUNK), axis=0)
        mixed = _sgu_mix(v3, ws_ref, lane_lo) + bias
        mx_ref[...] = mixed
        sgu = (u2 * mixed).astype(BF16)
        for k in range(N_CHIP):
            yb_ref[:, k * pc:(k + 1) * pc] = jnp.dot(sgu, wpb_ref[k], preferred_element_type=F32)

        lg0 = SSM_W + 2 * SGU_W
        ga = _sigmoid(p_ref[:, lg0:lg0 + D_MODEL])
        gb = _sigmoid(p_ref[:, lg0 + D_MODEL:lg0 + 2 * D_MODEL])
        mrg = ga * ya_ref[...] + gb * yb_ref[...]
        x2_ref[...] = x_ref[...] + _dot(mrg, wo_ref[...])

    return _call(
        body, "fwd_mix", (t_len // TB,),
        [_rows(TB, D_MODEL), _rows(TB, IN_COLS), _rows(TB, N_STATE), _rows(TB, N_STATE)]
        + [_whole()] * 11,
        [_rows(TB, D_MODEL), _rows(TB, SSM_W), _rows(TB, SSM_W), _rows(TB, SGU_W),
         _rows(TB, D_MODEL), _rows(TB, D_MODEL)],
        [jax.ShapeDtypeStruct((t_len, D_MODEL), F32), jax.ShapeDtypeStruct((t_len, SSM_W), F32),
         jax.ShapeDtypeStruct((t_len, SSM_W), F32), jax.ShapeDtypeStruct((t_len, SGU_W), F32),
         jax.ShapeDtypeStruct((t_len, D_MODEL), F32), jax.ShapeDtypeStruct((t_len, D_MODEL), F32)],
        [x, p, str_, sti, cre, cim, d_skip, w_glu, b_glu, w_pa, g_sgu, ws_st, bmat, w_pb, w_out], jobs)


def _conv_taps(v, cw_ref, c0, width):
    w0 = cw_ref[0:1, c0:c0 + width]
    w1 = cw_ref[1:2, c0:c0 + width]
    w2 = cw_ref[2:3, c0:c0 + width]
    return w0 * pltpu.roll(v, 2, 0) + w1 * pltpu.roll(v, 1, 0) + w2 * v


def _fwd_ffn(x2, target, g_ffn, w_up, conv_w, conv_b, w_down, g_final):
    t_len = x2.shape[0]
    half = D_FF // 2
    blocks_per_halo = TB // HALO

    def body(x2_ref, xp_ref, tg_ref, gf_ref, wu_ref, cw_ref, cb_ref, wd_ref, gl_ref,
             up_ref, act_ref, f_ref, h2_ref, dx3_ref, sm_ref):
        i = pl.program_id(0)
        xe = jnp.concatenate([xp_ref[...] * jnp.where(i == 0, 0.0, 1.0), x2_ref[...]], axis=0)
        _, xh = _rms_stats(xe)
        h2 = (xh * gf_ref[...]).astype(BF16)
        h2_ref[...] = h2[HALO:]
        acc = jnp.zeros((TB, D_MODEL), F32)
        for hc in range(2):
            ca = hc * half
            cb = D_FF + hc * half
            ua = jnp.dot(h2, wu_ref[hc], preferred_element_type=F32)
            ub = jnp.dot(h2, wu_ref[2 + hc], preferred_element_type=F32)
            up_ref[:, ca:ca + half] = ua[HALO:].astype(BF16)
            up_ref[:, cb:cb + half] = ub[HALO:].astype(BF16)
            ac = _conv_taps(ua, cw_ref, ca, half)[HALO:] + cb_ref[:, ca:ca + half]
            bc = _conv_taps(ub, cw_ref, cb, half)[HALO:] + cb_ref[:, cb:cb + half]
            act_ref[:, ca:ca + half] = ac.astype(BF16)
            act_ref[:, cb:cb + half] = bc.astype(BF16)
            f = (ac * _sigmoid(ac) * bc).astype(BF16)
            f_ref[:, ca:ca + half] = f
            acc = acc + jnp.dot(f, wd_ref[ca:ca + half, :], preferred_element_type=F32)
        x3 = x2_ref[...] + acc
        r3, xh3 = _rms_stats(x3)
        err = xh3 * gl_ref[...] - tg_ref[...]
        dout = err * (1.0 / D_MODEL)
        dx3_ref[...] = _rms_bwd(dout * gl_ref[...], xh3, r3)
        dgl = jnp.sum(dout * xh3, axis=0, keepdims=True)
        loss = 0.5 * jnp.sum(jnp.mean(err * err, axis=-1, keepdims=True), axis=0, keepdims=True)
        upd = jnp.concatenate([dgl, jnp.broadcast_to(loss, (1, D_MODEL)),
                               jnp.zeros((SUBLANE - 2, D_MODEL), F32)], axis=0)

        @pl.when(i == 0)
        def _():
            sm_ref[...] = upd

        @pl.when(i > 0)
        def _():
            sm_ref[...] += upd

    prev = pl.BlockSpec((HALO, D_MODEL), lambda i: (jnp.maximum(i * blocks_per_halo - 1, 0), 0))
    return _call(
        body, "fwd_ffn", (t_len // TB,),
        [_rows(TB, D_MODEL), prev, _rows(TB, D_MODEL)] + [_whole()] * 6,
        [_rows(TB, 2 * D_FF), _rows(TB, 2 * D_FF), _rows(TB, D_FF), _rows(TB, D_MODEL),
         _rows(TB, D_MODEL), _acc(SUBLANE, D_MODEL)],
        [jax.ShapeDtypeStruct((t_len, 2 * D_FF), BF16), jax.ShapeDtypeStruct((t_len, 2 * D_FF), BF16),
         jax.ShapeDtypeStruct((t_len, D_FF), BF16), jax.ShapeDtypeStruct((t_len, D_MODEL), BF16),
         jax.ShapeDtypeStruct((t_len, D_MODEL), F32), jax.ShapeDtypeStruct((SUBLANE, D_MODEL), F32)],
        [x2, x2, target, g_ffn, w_up, conv_w, conv_b, w_down, g_final])[0]


def _bwd_ffn(dx3, up, act, x2, g_ffn, w_up, conv_w, w_down, jobs=()):
    t_len = x2.shape[0]
    half = D_FF // 2
    nblk = t_len // TB
    halo_b = 2 * HALO
    n_e = TB + HALO

    def body(dx_ref, dxn_ref, up_ref, act_ref, actn_ref, x2_ref, gf_ref, wu_ref, cw_ref,
             wd_ref, dx2_ref, dup_ref, smw_ref, smg_ref):
        i = pl.program_id(0)
        keep_last = jnp.where(i == nblk - 1, 0.0, 1.0)
        dxe = jnp.concatenate([dx_ref[...], dxn_ref[...] * keep_last], axis=0).astype(BF16)
        dh2 = jnp.zeros((TB, D_MODEL), F32)
        zpad = jnp.zeros((1, half), F32)
        for hc in range(2):
            ca = hc * half
            cb = D_FF + hc * half
            ac = jnp.concatenate([act_ref[:, ca:ca + half].astype(F32),
                                  actn_ref[:, ca:ca + half].astype(F32)[:HALO]], axis=0)
            bc = jnp.concatenate([act_ref[:, cb:cb + half].astype(F32),
                                  actn_ref[:, cb:cb + half].astype(F32)[:HALO]], axis=0)
            wa = [cw_ref[k:k + 1, ca:ca + half] for k in range(3)]
            wb = [cw_ref[k:k + 1, cb:cb + half] for k in range(3)]
            df = lax.dot_general(dxe, wd_ref[ca:ca + half, :], (((1,), (1,)), ((), ())),
                                 preferred_element_type=F32)
            sg = _sigmoid(ac)
            da = df * bc * sg * (1.0 + ac * (1.0 - sg))
            db = df * ac * sg
            da1, da2 = pltpu.roll(da, n_e - 1, 0), pltpu.roll(da, n_e - 2, 0)
            db1, db2 = pltpu.roll(db, n_e - 1, 0), pltpu.roll(db, n_e - 2, 0)
            dua = (wa[2] * da + wa[1] * da1 + wa[0] * da2)[:TB]
            dub = (wb[2] * db + wb[1] * db1 + wb[0] * db2)[:TB]
            dup_ref[:, ca:ca + half] = dua.astype(BF16)
            dup_ref[:, cb:cb + half] = dub.astype(BF16)
            dh2 = dh2 + _dot_nt(dua, wu_ref[hc]) + _dot_nt(dub, wu_ref[2 + hc])
            rows = []
            for u_, d0, d1, d2 in ((up_ref[:, ca:ca + half].astype(F32), da, da1, da2),
                                   (up_ref[:, cb:cb + half].astype(F32), db, db1, db2)):
                rows.append([jnp.sum(u_ * d2[:TB], axis=0, keepdims=True),
                             jnp.sum(u_ * d1[:TB], axis=0, keepdims=True),
                             jnp.sum(u_ * d0[:TB], axis=0, keepdims=True),
                             jnp.sum(d0[:TB], axis=0, keepdims=True)])
            for c0, rws in ((ca, rows[0]), (cb, rows[1])):
                upd = jnp.concatenate(rws + [zpad] * (SUBLANE - 4), axis=0)

                @pl.when(i == 0)
                def _(upd=upd, c0=c0):
                    smw_ref[:, c0:c0 + half] = upd

                @pl.when(i > 0)
                def _(upd=upd, c0=c0):
                    smw_ref[:, c0:c0 + half] += upd

        r2, xh2 = _rms_stats(x2_ref[...])
        dx2_ref[...] = dx_ref[...] + _rms_bwd(dh2 * gf_ref[...], xh2, r2)
        updg = jnp.concatenate([jnp.sum(dh2 * xh2, axis=0, keepdims=True),
                                jnp.zeros((SUBLANE - 1, D_MODEL), F32)], axis=0)

        @pl.when(i == 0)
        def _():
            smg_ref[...] = updg

        @pl.when(i > 0)
        def _():
            smg_ref[...] += updg

    nxt_d = pl.BlockSpec((HALO, D_MODEL),
                         lambda i: (jnp.minimum((i + 1) * (TB // HALO), t_len // HALO - 1), 0))
    nxt_a = pl.BlockSpec((halo_b, 2 * D_FF),
                         lambda i: (jnp.minimum((i + 1) * (TB // halo_b), t_len // halo_b - 1), 0))
    return _call(
        body, "bwd_ffn", (nblk,),
        [_rows(TB, D_MODEL), nxt_d, _rows(TB, 2 * D_FF), _rows(TB, 2 * D_FF), nxt_a,
         _rows(TB, D_MODEL)] + [_whole()] * 4,
        [_rows(TB, D_MODEL), _rows(TB, 2 * D_FF), _acc(SUBLANE, 2 * D_FF), _acc(SUBLANE, D_MODEL)],
        [jax.ShapeDtypeStruct((t_len, D_MODEL), F32), jax.ShapeDtypeStruct((t_len, 2 * D_FF), BF16),
         jax.ShapeDtypeStruct((SUBLANE, 2 * D_FF), F32), jax.ShapeDtypeStruct((SUBLANE, D_MODEL), F32)],
        [dx3, dx3, up, act, act, x2, g_ffn, w_up, conv_w, w_down], jobs)


def _bwd_mix(dx2, p, y0, z, mixed, ya, yb, w_out, w_pa, w_pb, w_glu, cre, cim, ws_st, wst_st,
             d_skip, g_sgu, jobs=()):
    t_len = dx2.shape[0]
    pc = D_MODEL // N_CHIP
    n_slab = SGU_W // LANE

    def body(dx_ref, p_ref, y0_ref, z_ref, mx_ref, ya_ref, yb_ref, wo_ref, wpa_ref, wpb_ref,
             wg_ref, cre_ref, cim_ref, ws_ref, wst_ref, dsk_ref, gs_ref,
             dsr_ref, dsi_ref, du_ref, drest_ref, mrg_ref, dya_ref, dyb_ref, yap_ref, dz_ref,
             y1_ref, sgu_ref, dy0_ref, sm_ref, dbm_ref, dws_ref):
        i = pl.program_id(0)
        first = i == 0
        lg0 = SSM_W + 2 * SGU_W
        dmrg = _dot_nt(dx_ref[...], wo_ref[...])
        ga = _sigmoid(p_ref[:, lg0:lg0 + D_MODEL])
        gb = _sigmoid(p_ref[:, lg0 + D_MODEL:lg0 + 2 * D_MODEL])
        yav = ya_ref[...]
        ybv = yb_ref[...]
        mrg_ref[...] = (ga * yav + gb * ybv).astype(BF16)
        drest_ref[:, 2 * SGU_W:2 * SGU_W + D_MODEL] = (dmrg * yav * ga * (1.0 - ga)).astype(BF16)
        drest_ref[:, 2 * SGU_W + D_MODEL:] = (dmrg * ybv * gb * (1.0 - gb)).astype(BF16)
        dya = (dmrg * ga).astype(BF16)
        dyb = (dmrg * gb).astype(BF16)
        dya_ref[...] = dya
        dyb_ref[...] = dyb

        y0v = y0_ref[...]
        y1, y1_grad = _gelu_and_grad(y0v)
        sz = _sigmoid(z_ref[...])
        y1_ref[...] = y1.astype(BF16)
        yap_ref[...] = (y1 * sz).astype(BF16)
        dyap = jnp.zeros((TB, SSM_W), F32)
        for k in range(N_CHIP):
            dyap = dyap + _dot_nt(dya[:, k * pc:(k + 1) * pc], wpa_ref[k])
        dz = dyap * y1 * sz * (1.0 - sz)
        dz_ref[...] = dz.astype(BF16)
        dy0 = (dyap * sz + _dot_nt(dz, wg_ref[...])) * y1_grad
        dy0_ref[...] = dy0.astype(BF16)
        u = p_ref[:, 0:SSM_W]
        du_ref[...] = dy0 * dsk_ref[...]
        dsr_ref[...] = _dot_nt(dy0, cre_ref[...])
        dsi_ref[...] = -_dot_nt(dy0, cim_ref[...])

        uv = p_ref[:, SSM_W:lg0]
        uvg, gg = _gelu_and_grad(uv)
        u2 = uvg[:, :SGU_W]
        rv, vh = _rms_stats(uvg[:, SGU_W:])
        v3 = vh * gs_ref[...]
        mixed = mx_ref[...]
        dsgu = jnp.zeros((TB, SGU_W), F32)
        for k in range(N_CHIP):
            dsgu = dsgu + _dot_nt(dyb[:, k * pc:(k + 1) * pc], wpb_ref[k])
        sgu_ref[...] = (u2 * mixed).astype(BF16)
        du2 = dsgu * mixed
        dmix = dsgu * u2
        lane_lo = lax.broadcasted_iota(jnp.int32, (CHUNK, LANE), 1) < SGU_D
        dv3 = _sgu_mix(dmix, wst_ref, lane_lo)
        dbm = jnp.zeros((CHUNK, SGU_W), F32)
        for c0 in range(0, TB, CHUNK):
            dbm = dbm + dmix[c0:c0 + CHUNK]
        for j in range(n_slab):
            lo = jnp.zeros((CHUNK, CHUNK), F32)
            hi = jnp.zeros((CHUNK, CHUNK), F32)
            for c0 in range(0, TB, CHUNK):
                dsl = dmix[c0:c0 + CHUNK, j * LANE:(j + 1) * LANE]
                vsl = v3[c0:c0 + CHUNK, j * LANE:(j + 1) * LANE]
                lo = lo + _dot_nt(jnp.where(lane_lo, dsl, 0.0), vsl)
                hi = hi + _dot_nt(jnp.where(lane_lo, 0.0, dsl), vsl)

            @pl.when(first)
            def _(lo=lo, hi=hi, j=j):
                dws_ref[2 * j] = lo
                dws_ref[2 * j + 1] = hi

            @pl.when(jnp.logical_not(first))
            def _(lo=lo, hi=hi, j=j):
                dws_ref[2 * j] += lo
                dws_ref[2 * j + 1] += hi

        dv2 = _rms_bwd(dv3 * gs_ref[...], vh, rv)
        drest_ref[:, 0:SGU_W] = (du2 * gg[:, :SGU_W]).astype(BF16)
        drest_ref[:, SGU_W:2 * SGU_W] = (dv2 * gg[:, SGU_W:]).astype(BF16)

        upd = jnp.concatenate([jnp.sum(dy0 * u, axis=0, keepdims=True),
                               jnp.sum(dz, axis=0, keepdims=True),
                               jnp.sum(dv3 * vh, axis=0, keepdims=True),
                               jnp.zeros((SUBLANE - 3, SSM_W), F32)], axis=0)

        @pl.when(first)
        def _():
            sm_ref[...] = upd
            dbm_ref[...] = dbm

        @pl.when(jnp.logical_not(first))
        def _():
            sm_ref[...] += upd
            dbm_ref[...] += dbm

    rest = 2 * SGU_W + 2 * D_MODEL
    bf_d, bf_s = jax.ShapeDtypeStruct((t_len, D_MODEL), BF16), jax.ShapeDtypeStruct((t_len, SSM_W), BF16)
    return _call(
        body, "bwd_mix", (t_len // TB,),
        [_rows(TB, D_MODEL), _rows(TB, IN_COLS), _rows(TB, SSM_W), _rows(TB, SSM_W),
         _rows(TB, SGU_W), _rows(TB, D_MODEL), _rows(TB, D_MODEL)] + [_whole()] * 10,
        [_rows(TB, N_STATE), _rows(TB, N_STATE), _rows(TB, SSM_W), _rows(TB, rest),
         _rows(TB, D_MODEL), _rows(TB, D_MODEL), _rows(TB, D_MODEL), _rows(TB, SSM_W),
         _rows(TB, SSM_W), _rows(TB, SSM_W), _rows(TB, SGU_W), _rows(TB, SSM_W),
         _acc(SUBLANE, SSM_W), _acc(CHUNK, SGU_W),
         pl.BlockSpec((SGU_G, CHUNK, CHUNK), lambda i: (0, 0, 0))],
        [jax.ShapeDtypeStruct((t_len, N_STATE), F32), jax.ShapeDtypeStruct((t_len, N_STATE), F32),
         jax.ShapeDtypeStruct((t_len, SSM_W), F32), jax.ShapeDtypeStruct((t_len, rest), BF16),
         bf_d, bf_d, bf_d, bf_s, bf_s, bf_s, bf_s, bf_s,
         jax.ShapeDtypeStruct((SUBLANE, SSM_W), F32), jax.ShapeDtypeStruct((CHUNK, SGU_W), F32),
         jax.ShapeDtypeStruct((SGU_G, CHUNK, CHUNK), F32)],
        [dx2, p, y0, z, mixed, ya, yb, w_out, w_pa, w_pb, w_glu, cre, cim, ws_st, wst_st, d_skip,
         g_sgu], jobs)


def _scan_bwd(dsr, dsi, str_, sti, tab_rev, jobs=()):
    t_len = dsr.shape[0]
    nblk = t_len // SUBLANE
    lb = SCAN_LANES

    def body(dr_ref, di_ref, sr_ref, si_ref, tab_ref, lr_ref, li_ref, dar_ref, dai_ref):
        tab_v = [tab_ref[q] for q in range(8)]
        row0 = lax.broadcasted_iota(jnp.int32, (SUBLANE, lb), 0) == 0

        def step(k, carry):
            cr, ci, acr, aci = carry
            blocks = [nblk - 1 - (k * SCAN_UNROLL + u) for u in range(SCAN_UNROLL)]
            rows = [pl.multiple_of(kk * SUBLANE, SUBLANE) for kk in blocks]
            local = [_scan_local(dr_ref[pl.ds(r0, SUBLANE), :], di_ref[pl.ds(r0, SUBLANE), :],
                                 tab_v, (7, 6, 4)) for r0 in rows]
            for kk, r0, (xr, xi) in zip(blocks, rows, local):
                xr, xi = _scan_carry(xr, xi, tab_v, cr, ci)
                lr_ref[pl.ds(r0, SUBLANE), :] = xr
                li_ref[pl.ds(r0, SUBLANE), :] = xi
                cr, ci = xr[0:1, :], xi[0:1, :]
                rp = pl.multiple_of(jnp.maximum(kk - 1, 0) * SUBLANE, SUBLANE)
                has_prev = jnp.where(kk > 0, 1.0, 0.0)
                pr = sr_ref[pl.ds(rp, SUBLANE), :][SUBLANE - 1:SUBLANE, :] * has_prev
                pi = si_ref[pl.ds(rp, SUBLANE), :][SUBLANE - 1:SUBLANE, :] * has_prev
                s_r = jnp.where(row0, pr, pltpu.roll(sr_ref[pl.ds(r0, SUBLANE), :], 1, 0))
                s_i = jnp.where(row0, pi, pltpu.roll(si_ref[pl.ds(r0, SUBLANE), :], 1, 0))
                acr = acr + xr * s_r + xi * s_i
                aci = aci + xi * s_r - xr * s_i
            return cr, ci, acr, aci

        zero = jnp.zeros((1, lb), F32)
        zacc = jnp.zeros((SUBLANE, lb), F32)
        _, _, acr, aci = lax.fori_loop(0, nblk // SCAN_UNROLL, step, (zero, zero, zacc, zacc))
        dar_ref[...] = acr
        dai_ref[...] = aci

    col = pl.BlockSpec((t_len, lb), lambda j: (0, j))
    small = pl.BlockSpec((SUBLANE, lb), lambda j: (0, j))
    return _call(
        body, "scan_bwd", (N_STATE // lb,),
        [col, col, col, col, pl.BlockSpec((8, SUBLANE, lb), lambda j: (0, 0, j))],
        [col, col, small, small],
        [jax.ShapeDtypeStruct((t_len, N_STATE), F32)] * 2
        + [jax.ShapeDtypeStruct((SUBLANE, N_STATE), F32)] * 2,
        [dsr, dsi, str_, sti, tab_rev], jobs)


def _bwd_in(lam_r, lam_i, du_part, drest, x, dx2, g_mix, w_in, bre, bim, jobs=()):
    t_len = x.shape[0]
    cs = IN_COLS // N_CHIP

    def body(lr_ref, li_ref, du_ref, dr_ref, x_ref, dx2_ref, g_ref, w_ref, bre_ref, bim_ref,
             gx_ref, dp_ref, sm_ref):
        i = pl.program_id(0)
        du = du_ref[...] + _dot_nt(lr_ref[...], bre_ref[...]) + _dot_nt(li_ref[...], bim_ref[...])
        dp_ref[:, 0:SSM_W] = du.astype(BF16)
        dp_ref[:, SSM_W:] = dr_ref[...]
        dh = jnp.zeros((TB, D_MODEL), F32)
        for k in range(N_CHIP):
            dh = dh + _dot_nt(dp_ref[:, k * cs:(k + 1) * cs], w_ref[k])
        r, xh = _rms_stats(x_ref[...])
        gx_ref[...] = dx2_ref[...] + _rms_bwd(dh * g_ref[...], xh, r)
        upd = jnp.concatenate([jnp.sum(dh * xh, axis=0, keepdims=True),
                               jnp.zeros((SUBLANE - 1, D_MODEL), F32)], axis=0)

        @pl.when(i == 0)
        def _():
            sm_ref[...] = upd

        @pl.when(i > 0)
        def _():
            sm_ref[...] += upd

    return _call(
        body, "bwd_in", (t_len // TB,),
        [_rows(TB, N_STATE), _rows(TB, N_STATE), _rows(TB, SSM_W), _rows(TB, IN_COLS - SSM_W),
         _rows(TB, D_MODEL), _rows(TB, D_MODEL)] + [_whole()] * 4,
        [_rows(TB, D_MODEL), _rows(TB, IN_COLS), _acc(SUBLANE, D_MODEL)],
        [jax.ShapeDtypeStruct((t_len, D_MODEL), F32), jax.ShapeDtypeStruct((t_len, IN_COLS), BF16),
         jax.ShapeDtypeStruct((SUBLANE, D_MODEL), F32)],
        [lam_r, lam_i, du_part, drest, x, dx2, g_mix, w_in, bre, bim], jobs)


def _matmul_tn(a, b, name, out_shape, grid_ij, a_blk, a_map, b_blk, b_map, o_blk, o_map, jobs=()):
    tk = a_blk[0]
    nk = a.shape[0] // tk
    assert nk * tk == a.shape[0] and nk > 0

    def body(a_ref, b_ref, o_ref, acc_ref):
        k = pl.program_id(2)

        @pl.when(k == 0)
        def _():
            acc_ref[...] = jnp.zeros_like(acc_ref)

        acc_ref[...] += lax.dot_general(a_ref[...].astype(BF16), b_ref[...].astype(BF16),
                                        (((0,), (0,)), ((), ())), preferred_element_type=F32)

        @pl.when(k == nk - 1)
        def _():
            o_ref[...] = acc_ref[...]

    outs, per_job = _call(
        body, name, (grid_ij[0], grid_ij[1], nk),
        [pl.BlockSpec(a_blk, a_map), pl.BlockSpec(b_blk, b_map)], [pl.BlockSpec(o_blk, o_map)],
        [jax.ShapeDtypeStruct(out_shape, F32)], [a, b], jobs,
        scratch=[pltpu.VMEM((a_blk[1], b_blk[1]), F32)])
    return outs[0], per_job


def _dw_shards(a, b, name, tk, jobs=()):
    m, n = a.shape[1], b.shape[1]
    tn = n // N_CHIP
    tk = min(tk, a.shape[0])
    return _matmul_tn(a, b, name, (N_CHIP, m, tn), (1, N_CHIP),
                      (tk, m), lambda i, j, k: (k, 0), (tk, tn), lambda i, j, k: (k, j),
                      (None, m, tn), lambda i, j, k: (j, 0, 0), jobs)


def _dw_rows(a, b, name, tm, tk):
    m, n = a.shape[1], b.shape[1]
    tk = min(tk, a.shape[0])
    return _matmul_tn(a, b, name, (m, n), (m // tm, 1),
                      (tk, tm), lambda i, j, k: (k, i), (tk, n), lambda i, j, k: (k, 0),
                      (tm, n), lambda i, j, k: (i, 0))[0]


def _dw_full(a, b, name, n_shard, tk=TK, jobs=()):
    m, n = a.shape[1], b.shape[1]
    tn = n // n_shard
    tk = min(tk, a.shape[0])

    def body(a_ref, b_ref, o_ref):
        k = pl.program_id(0)
        a_t = a_ref[...].astype(BF16).T
        for j in range(n_shard):
            piece = jnp.dot(a_t, b_ref[:, j * tn:(j + 1) * tn].astype(BF16),
                            preferred_element_type=F32)

            @pl.when(k == 0)
            def _(piece=piece, j=j):
                o_ref[j] = piece

            @pl.when(k > 0)
            def _(piece=piece, j=j):
                o_ref[j] += piece

    return _call(body, name, (a.shape[0] // tk,),
                 [pl.BlockSpec((tk, m), lambda k: (k, 0)), pl.BlockSpec((tk, n), lambda k: (k, 0))],
                 [pl.BlockSpec((n_shard, m, tn), lambda k: (0, 0, 0))],
                 [jax.ShapeDtypeStruct((n_shard, m, tn), F32)], [a, b], jobs)


def _dw_pair(a, m, b1, b2, name, jobs=()):
    n = b1.shape[1]
    tk = min(TK, a.shape[0])
    nk = a.shape[0] // tk
    rows_per_slab = m * LANE // n

    def body(a_ref, b1_ref, b2_ref, o1_ref, o2_ref, acc1_ref, acc2_ref):
        k = pl.program_id(0)
        a_t = a_ref[...].astype(BF16).T
        for b_ref, o_ref, acc_ref in ((b1_ref, o1_ref, acc1_ref), (b2_ref, o2_ref, acc2_ref)):
            piece = jnp.dot(a_t, b_ref[...].astype(BF16), preferred_element_type=F32)

            @pl.when(k == 0)
            def _(piece=piece, acc_ref=acc_ref):
                acc_ref[...] = piece

            @pl.when(k > 0)
            def _(piece=piece, acc_ref=acc_ref):
                acc_ref[...] += piece

            @pl.when(k == nk - 1)
            def _(o_ref=o_ref, acc_ref=acc_ref):
                for j in range(n // LANE):
                    rows = slice(j * rows_per_slab, (j + 1) * rows_per_slab)
                    o_ref[rows, :] = acc_ref[rows, j * LANE:(j + 1) * LANE]

    tok = pl.BlockSpec((tk, n), lambda k: (k, 0))
    out = pl.BlockSpec((m, LANE), lambda k: (0, 0))
    return _call(body, name, (nk,),
                 [pl.BlockSpec((tk, m), lambda k: (k, 0)), tok, tok], [out, out],
                 [jax.ShapeDtypeStruct((m, LANE), F32)] * 2, [a, b1, b2], jobs,
                 scratch=[pltpu.VMEM((m, n), F32)] * 2)


def _prefetch_call(body, name, grid, scalars, in_specs, out_specs, out_shape, args):
    return pl.pallas_call(
        body, name=name,
        grid_spec=pltpu.PrefetchScalarGridSpec(num_scalar_prefetch=1, grid=grid, in_specs=in_specs,
                                               out_specs=out_specs),
        out_shape=out_shape, compiler_params=_params(len(grid)),
    )(scalars, *args)


def _place_shard(w, where, name, dtype, tr):
    rows, cols = w.shape

    def body(s_ref, w_ref, o_ref):
        o_ref[...] = w_ref[...].astype(dtype)

    return _prefetch_call(
        body, name, (rows // tr,), where,
        [pl.BlockSpec((tr, cols), lambda i, s: (i, 0))],
        pl.BlockSpec((None, tr, cols), lambda i, s: (s[0], i, 0)),
        jax.ShapeDtypeStruct((N_CHIP, rows, cols), dtype), [w])


def _add_sibling(g, got, where, name):
    _, rs, cs = g.shape
    hr = rs // 2

    def body(s_ref, g_ref, got_ref, o_ref):
        o_ref[...] = (g_ref[...] + got_ref[...]).astype(BF16)

    return _prefetch_call(
        body, name, (N_CHIP,), where,
        [pl.BlockSpec((None, hr, cs), lambda k, s: (k, s[1], 0)),
         pl.BlockSpec((None, hr, cs), lambda k, s: (k, 0, 0))],
        pl.BlockSpec((None, hr, cs), lambda k, s: (k, 0, 0)),
        jax.ShapeDtypeStruct((N_CHIP, hr, cs), BF16), [g, got])


def _add_chips(sums, got, where, name):
    _, hr, cs = sums.shape

    def body(s_ref, own_ref, got_ref, o_ref):
        o_ref[...] = ((own_ref[...].astype(F32) + got_ref[0].astype(F32))
                      + got_ref[1].astype(F32)) + got_ref[2].astype(F32)

    return _prefetch_call(
        body, name, (1,), where,
        [pl.BlockSpec((None, hr, cs), lambda i, s: (s[0], 0, 0)),
         pl.BlockSpec((3, hr, cs), lambda i, s: (0, 0, 0))],
        pl.BlockSpec((hr, cs), lambda i, s: (s[1], 0)),
        jax.ShapeDtypeStruct((2 * hr, cs), F32), [sums, got])


def _small_allreduce(pack):
    rows = pack.shape[0]
    half = rows // 2

    def body(in_ref, out_ref, sib_ref, slots_ref, s_a, r_a, s_b, r_b, s_c, r_c):
        x, y, c, chips = _place()
        k_me = 2 * x + y
        sib = (x, y, 1 - c)
        first = _remote(in_ref, sib_ref, s_a, r_a, sib)
        first.start()
        first.wait()
        mine = _half(rows, c)
        slots_ref[k_me] = in_ref[mine, :] + sib_ref[mine, :]
        cps = [_remote(slots_ref.at[k_me], slots_ref.at[k_me], s_b.at[j], r_b.at[j], (*ch, c))
               for j, ch in enumerate(chips)]
        for cp in cps:
            cp.start()
        for j, ch in enumerate(chips):
            slot = slots_ref.at[_chip_index(ch)]
            _remote(slot, slot, s_b.at[j], r_b.at[j], (*ch, c)).wait_recv()
        for cp in cps:
            cp.wait_send()
        out_ref[mine, :] = ((slots_ref[0] + slots_ref[1]) + slots_ref[2]) + slots_ref[3]
        last = _remote(out_ref.at[mine, :], out_ref.at[mine, :], s_c, r_c, sib)
        last.start()
        theirs = out_ref.at[_half(rows, 1 - c), :]
        _remote(theirs, theirs, s_c, r_c, sib).wait_recv()
        last.wait_send()

    return pl.pallas_call(
        body, name="small_allreduce", in_specs=[_whole()], out_specs=_whole(),
        out_shape=jax.ShapeDtypeStruct(pack.shape, F32),
        scratch_shapes=[pltpu.VMEM(pack.shape, F32), pltpu.VMEM((N_CHIP, half, LANE), F32),
                        pltpu.SemaphoreType.DMA, pltpu.SemaphoreType.DMA,
                        pltpu.SemaphoreType.DMA((3,)), pltpu.SemaphoreType.DMA((3,)),
                        pltpu.SemaphoreType.DMA, pltpu.SemaphoreType.DMA],
        compiler_params=_params(0),
    )(pack)


def _adamw_update(w_ref, g_ref, m_ref, v_ref, d_ref, mo_ref, vo_ref):
    gv = g_ref[...]
    mn = ADAM_B1 * m_ref[...] + (1.0 - ADAM_B1) * gv
    vn = ADAM_B2 * v_ref[...] + (1.0 - ADAM_B2) * (gv * gv)
    mo_ref[...] = mn
    vo_ref[...] = vn
    m_hat = mn / (1.0 - ADAM_B1 ** ADAM_STEP)
    v_hat = vn / (1.0 - ADAM_B2 ** ADAM_STEP)
    d_ref[...] = -ADAM_LR * (m_hat / (jnp.sqrt(v_hat) + ADAM_EPS) + ADAM_WD * w_ref[...])


def _adamw(w, g, m, v, name, tr):
    rows, cols = w.shape
    blk = _rows(tr, cols)
    return _call(_adamw_update, name, (rows // tr,), [blk] * 4, [blk] * 3,
                 [jax.ShapeDtypeStruct(w.shape, F32)] * 3, [w, g, m, v])[0]


def _adamw_many(ws, gs, ms, vs, name):
    n = len(ws)

    def body(*refs):
        for t in range(n):
            _adamw_update(*[refs[q * n + t] for q in range(7)])

    outs = pl.pallas_call(
        body, name=name, in_specs=[_whole()] * (4 * n), out_specs=[_whole()] * (3 * n),
        out_shape=[jax.ShapeDtypeStruct(a.shape, F32) for _ in range(3) for a in ws],
        compiler_params=_params(0),
    )(*ws, *gs, *ms, *vs)
    return outs[:n], outs[n:2 * n], outs[2 * n:]


def _ssm_discretize(a_re, a_im, log_dt, b_re, b_im):
    dt = jnp.exp(log_dt)[:, None]
    mag = jnp.exp(dt * a_re)
    abr = mag * jnp.cos(dt * a_im)
    abi = mag * jnp.sin(dt * a_im)
    den = a_re * a_re + a_im * a_im
    nr = abr - 1.0
    ni = abi
    f_re = (nr * a_re + ni * a_im) / den
    f_im = (ni * a_re - nr * a_im) / den
    bbr = f_re[..., None] * b_re - f_im[..., None] * b_im
    bbi = f_re[..., None] * b_im + f_im[..., None] * b_re
    return abr, abi, bbr, bbi


def _scan_tables(abr, abi):
    ar = abr.reshape(1, N_STATE)
    ai = abi.reshape(1, N_STATE)
    pr, pi = [ar], [ai]
    for _ in range(SUBLANE - 1):
        pr, pi = pr + [pr[-1] * ar - pi[-1] * ai], pi + [pr[-1] * ai + pi[-1] * ar]
    row = jnp.arange(SUBLANE)[:, None]
    tabs = []
    for d in (1, 2, 4):
        tabs.append(jnp.where(row >= d, pr[d - 1], 0.0))
        tabs.append(jnp.where(row >= d, pi[d - 1], 0.0))
    tabs.append(jnp.concatenate(pr, axis=0))
    tabs.append(jnp.concatenate(pi, axis=0))
    fwd = jnp.stack(tabs)
    sign = jnp.array([1.0, -1.0] * 4, F32)[:, None, None]
    return fwd, fwd[:, ::-1, :] * sign


def _block_diag_b(bb):
    eye = jnp.eye(SSM_G, dtype=F32)
    return jnp.einsum("gph,gk->ghkp", bb, eye).reshape(SSM_W, N_STATE)


def _block_diag_c(cc):
    eye = jnp.eye(SSM_G, dtype=F32)
    return jnp.einsum("ghp,gk->gpkh", cc, eye).reshape(N_STATE, SSM_W)


SMALL_SHAPES = {
    "g_mix": (D_MODEL,), "a_re": (SSM_G, SSM_P), "a_im": (SSM_G, SSM_P), "log_dt": (SSM_G,),
    "b_re": (SSM_G, SSM_P, SSM_H), "b_im": (SSM_G, SSM_P, SSM_H),
    "c_re": (SSM_G, SSM_H, SSM_P), "c_im": (SSM_G, SSM_H, SSM_P),
    "d_skip": (SSM_W,), "b_glu": (SSM_W,), "g_sgu": (SGU_W,), "w_s": (SGU_G, CHUNK, CHUNK),
    "b_s": (SGU_G, CHUNK), "g_ffn": (D_MODEL,), "conv_b": (2 * D_FF,), "g_final": (D_MODEL,),
}
PACK_ITEMS = [("loss", (1,))] + [(n, SMALL_SHAPES[n]) for n in SMALL] + [("conv_w", (3, 2 * D_FF))]
TILE = SUBLANE * LANE


def _item_rows(shape):
    return -(-math.prod(shape) // TILE) * SUBLANE


PACK_ROWS = -(-sum(_item_rows(s) for _, s in PACK_ITEMS) // (2 * SUBLANE)) * (2 * SUBLANE)


def _pack(values):
    parts, used = [], 0
    for name, shape in PACK_ITEMS:
        size, rows = math.prod(shape), _item_rows(shape)
        if name in values:
            flat = values[name].astype(F32).reshape(size)
            if rows * LANE > size:
                flat = jnp.pad(flat, (0, rows * LANE - size))
            parts.append(flat.reshape(rows, LANE))
        else:
            parts.append(jnp.zeros((rows, LANE), F32))
        used += rows
    if PACK_ROWS > used:
        parts.append(jnp.zeros((PACK_ROWS - used, LANE), F32))
    return jnp.concatenate(parts, axis=0)


def _unpack(pack):
    out, off = {}, 0
    for name, shape in PACK_ITEMS:
        rows = _item_rows(shape)
        out[name] = pack[off:off + rows].reshape(rows * LANE)[:math.prod(shape)].reshape(shape)
        off += rows
    return out


PLACE_ROWS = {"w_in": 256, "w_up": 256, "w_down": 352, "w_out": 256, "w_proj_a": 256,
              "w_proj_b": 256, "w_glu": 128}


def kernel(x, g_mix, w_in, a_re, a_im, log_dt, b_re, b_im, c_re, c_im, d_skip, w_glu, b_glu, w_proj_a, g_sgu, w_s, b_s, w_proj_b, w_out, g_ffn, w_up, conv_w, conv_b, w_down, g_final, loss_target, m_g_mix, m_w_in, m_a_re, m_a_im, m_log_dt, m_b_re, m_b_im, m_c_re, m_c_im, m_d_skip, m_w_glu, m_b_glu, m_w_proj_a, m_g_sgu, m_w_s, m_b_s, m_w_proj_b, m_w_out, m_g_ffn, m_w_up, m_conv_w, m_conv_b, m_w_down, m_g_final, v_g_mix, v_w_in, v_a_re, v_a_im, v_log_dt, v_b_re, v_b_im, v_c_re, v_c_im, v_d_skip, v_w_glu, v_b_glu, v_w_proj_a, v_g_sgu, v_w_s, v_b_s, v_w_proj_b, v_w_out, v_g_ffn, v_w_up, v_conv_w, v_conv_b, v_w_down, v_g_final):
    given = dict(locals())
    w = {n: given[n] for n in WEIGHTS}
    m = {n: given["m_" + n] for n in WEIGHTS}
    v = {n: given["v_" + n] for n in WEIGHTS}

    def shard2d(a):
        return a.reshape(a.shape[-2], a.shape[-1])

    chip = 2 * lax.axis_index("x") + lax.axis_index("y")
    where = jnp.stack([chip, lax.axis_index("c")]).astype(jnp.int32)
    xs, target = x[0], loss_target[0]
    small = {n: w[n].reshape(SMALL_SHAPES[n]) for n in SMALL}

    (abr, abi, bbr, bbi), disc_vjp = jax.vjp(_ssm_discretize, small["a_re"], small["a_im"],
                                             small["log_dt"], small["b_re"], small["b_im"])
    tab_f, tab_r = _scan_tables(abr, abi)
    bre = _block_diag_b(bbr).astype(BF16)
    bim = _block_diag_b(bbi).astype(BF16)
    cre = _block_diag_c(small["c_re"]).astype(BF16)
    cim = _block_diag_c(small["c_im"]).astype(BF16)
    tril = jnp.tril(jnp.ones((CHUNK, CHUNK), dtype=bool))
    ws = jnp.where(tril[None], small["w_s"], 0.0)
    ws_st = ws.reshape(SGU_G // 2, 2 * CHUNK, CHUNK).astype(BF16)
    wst_st = ws.transpose(0, 2, 1).reshape(SGU_G // 2, 2 * CHUNK, CHUNK).astype(BF16)
    bmat = jnp.repeat(small["b_s"].T, SGU_D, axis=1)
    g_mix2 = small["g_mix"].reshape(1, D_MODEL)
    g_ffn2 = small["g_ffn"].reshape(1, D_MODEL)
    g_final2 = small["g_final"].reshape(1, D_MODEL)
    g_sgu2 = small["g_sgu"].reshape(1, SGU_W)
    d_skip2 = small["d_skip"].reshape(1, SSM_W)
    b_glu2 = small["b_glu"].reshape(1, SSM_W)
    conv_b2 = small["conv_b"].reshape(1, 2 * D_FF)

    gat = {n: _place_shard(shard2d(w[n]), where, "place_" + n, BF16, PLACE_ROWS[n]) for n in BIG}
    gat["conv_w"] = _place_shard(shard2d(w["conv_w"]), where, "place_conv_w", F32, 3)
    (gat["w_in"],), = _comm("gather_in_ici", [_job_gather_ici([gat["w_in"]])])
    (gat["w_in"],), = _comm("gather_in_sibling", [_job_gather_sibling([gat["w_in"]])])
    early = ["w_glu", "w_proj_a", "w_proj_b", "w_out", "w_down", "conv_w"]

    (p, h1, bur, bui), (got,) = _fwd_in(
        xs, g_mix2, gat["w_in"], bre, bim,
        [_job_gather_ici([gat[n] for n in early], whole=(5,))])
    gat.update(zip(early, got))
    (str_, sti), (got_e, (gat["w_up"],)) = _scan_fwd(
        bur, bui, tab_f,
        [_job_gather_sibling([gat[n] for n in early[:5]]), _job_gather_ici([gat["w_up"]])])
    gat.update(zip(early[:5], got_e))
    w_glu_f = gat["w_glu"].reshape(SSM_W, SSM_W)
    w_out_f = gat["w_out"].reshape(D_MODEL, D_MODEL)
    conv_w_f = gat["conv_w"].transpose(1, 0, 2).reshape(3, 2 * D_FF)
    (x2, y0, z, mixed, ya, yb), ((gat["w_up"],),) = _fwd_mix(
        xs, p, str_, sti, cre, cim, d_skip2, w_glu_f, b_glu2, gat["w_proj_a"], g_sgu2, ws_st, bmat,
        gat["w_proj_b"], w_out_f, [_job_gather_sibling([gat["w_up"]])])
    w_down_f = gat["w_down"].reshape(D_FF, D_MODEL)
    up, act, f, h2, dx3, sm_ffn = _fwd_ffn(x2, target, g_ffn2, gat["w_up"], conv_w_f, conv_b2,
                                           w_down_f, g_final2)

    def leg1_done(names, got):
        return [_add_sibling(part[n], s, where, "add_sibling_" + n) for n, s in zip(names, got)]

    def leg2_done(names, sums, got):
        return [_add_chips(s, o, where, "add_chips_" + n) for n, s, o in zip(names, sums, got)]

    part, red = {}, {}
    part["w_down"] = _dw_rows(f, dx3, "dw_down", D_FF // 2, 2 * TK).reshape(
        N_CHIP, D_FF // N_CHIP, D_MODEL)
    (dx2, dup, sm_conv, sm_gffn), (got,) = _bwd_ffn(
        dx3, up, act, x2, g_ffn2, gat["w_up"], conv_w_f, w_down_f,
        [_job_sibling_halves([part["w_down"]])])
    sum_down = leg1_done(["w_down"], got)
    part["w_up"], (got,) = _dw_shards(h2, dup, "dw_up", 4 * TK, [_job_to_owner(sum_down)])
    red_down = leg2_done(["w_down"], sum_down, got)
    ((dsr, dsi, du_part, drest, mrg, dya, dyb, yap, dz, y1, sgu, dy0, sm_mix, dbm, dws),
     (got, (red["w_down"],))) = _bwd_mix(
        dx2, p, y0, z, mixed, ya, yb, w_out_f, gat["w_proj_a"], gat["w_proj_b"], w_glu_f, cre, cim,
        ws_st, wst_st, d_skip2, g_sgu2,
        [_job_sibling_halves([part["w_up"]]), _job_swap_halves(red_down)])
    sum_up = leg1_done(["w_up"], got)
    (lam_r, lam_i, dar8, dai8), (got,) = _scan_bwd(dsr, dsi, str_, sti, tab_r, [_job_to_owner(sum_up)])
    red_up = leg2_done(["w_up"], sum_up, got)
    mix4 = ["w_out", "w_proj_a", "w_proj_b", "w_glu"]
    part["w_out"] = _dw_full(mrg, dx2, "dw_out", 1)[0][0].reshape(N_CHIP, D_MODEL // N_CHIP, D_MODEL)
    part["w_proj_a"] = _dw_full(yap, dya, "dw_proj_a", N_CHIP)[0][0]
    part["w_proj_b"] = _dw_full(sgu, dyb, "dw_proj_b", N_CHIP)[0][0]
    part["w_glu"] = _dw_full(y1, dz, "dw_glu", 1)[0][0].reshape(N_CHIP, SSM_W // N_CHIP, SSM_W)
    (grad_x, dp, sm_gmix), (got, (red["w_up"],)) = _bwd_in(
        lam_r, lam_i, du_part, drest, xs, dx2, g_mix2, gat["w_in"], bre, bim,
        [_job_sibling_halves([part[n] for n in mix4]), _job_swap_halves(red_up)])
    sums_m = leg1_done(mix4, got)
    (part["w_in"],), (got,) = _dw_full(h1, dp, "dw_in", N_CHIP, 2 * TK, [_job_to_owner(sums_m)])
    red_m = leg2_done(mix4, sums_m, got)
    (dbd_r, dbd_i), (got, done_m) = _dw_pair(
        p, SSM_W, lam_r, lam_i, "db_bar",
        [_job_sibling_halves([part["w_in"]]), _job_swap_halves(red_m)])
    red.update(zip(mix4, done_m))
    sum_in = leg1_done(["w_in"], got)
    (dcd_r, dcd_i), (got,) = _dw_pair(dy0, SSM_W, str_, sti, "dc", [_job_to_owner(sum_in)])
    red_in = leg2_done(["w_in"], sum_in, got)
    (red["w_in"],), = _comm("swap_w_in", [_job_swap_halves(red_in)])

    def pick_c(slabs):
        two = LANE // SSM_P
        return jnp.einsum("jshsp->jshp", slabs.reshape(SSM_G // two, two, SSM_H, two, SSM_P)
                          ).reshape(SSM_G, SSM_H, SSM_P)

    def pick_b(slabs):
        return pick_c(slabs).transpose(0, 2, 1)

    dabr = jnp.sum(dar8, axis=0).reshape(SSM_G, SSM_P)
    dabi = jnp.sum(dai8, axis=0).reshape(SSM_G, SSM_P)
    d_a_re, d_a_im, d_log_dt, d_b_re, d_b_im = disc_vjp((dabr, dabi, pick_b(dbd_r), pick_b(dbd_i)))
    gsmall = {
        "g_mix": sm_gmix[0], "a_re": d_a_re, "a_im": d_a_im, "log_dt": d_log_dt,
        "b_re": d_b_re, "b_im": d_b_im, "c_re": pick_c(dcd_r), "c_im": -pick_c(dcd_i),
        "d_skip": sm_mix[0], "b_glu": sm_mix[1], "g_sgu": sm_mix[2],
        "w_s": jnp.where(tril[None], dws, 0.0),
        "b_s": dbm.reshape(CHUNK, SGU_G, SGU_D).sum(-1).T,
        "g_ffn": sm_gffn[0], "conv_b": sm_conv[3], "g_final": sm_ffn[0],
        "conv_w": sm_conv[0:3], "loss": sm_ffn[1, 0:1],
    }

    total_pack = _small_allreduce(_pack(gsmall))
    total = _unpack(total_pack)
    grads = dict(red)
    cs = 2 * D_FF // N_CHIP
    grads["conv_w"] = lax.dynamic_slice(total["conv_w"], (0, chip * cs), (3, cs))
    for n in SMALL:
        grads[n] = total[n]
    delta, new_m, new_v = {}, {}, {}
    for n in BIG + ("conv_w",):
        delta[n], new_m[n], new_v[n] = _adamw(shard2d(w[n]), grads[n], shard2d(m[n]), shard2d(v[n]),
                                              "adamw_" + n, PLACE_ROWS.get(n, 3))
    def flat2d(a):
        return a.reshape(-1, a.shape[-1])

    ud, um, uv = _adamw_many(*[[flat2d(d[n]) for n in SMALL] for d in (w, total, m, v)], "adamw_small")
    for i, n in enumerate(SMALL):
        delta[n], new_m[n], new_v[n] = ud[i], um[i], uv[i]

    def like(d):
        return [d[n].reshape(w[n].shape) for n in WEIGHTS]

    return (total["loss"].reshape(()), grad_x.reshape(x.shape), *like(grads), *like(delta),
            *like(new_m), *like(new_v))
```

```python
import math

import jax
import jax.numpy as jnp
from jax import lax
from jax.experimental import pallas as pl
from jax.experimental.pallas import tpu as pltpu

F32 = jnp.float32
BF16 = jnp.bfloat16
MESH = pl.DeviceIdType.MESH

D_MODEL = 1024
SSM_W = 512
SSM_G = 32
SSM_H = 16
SSM_P = 64
N_STATE = SSM_G * SSM_P
SGU_W = 512
SGU_G = 8
SGU_D = 64
CHUNK = 128
D_FF = 2816
IN_COLS = 3584
EPS = 1e-6
N_CHIP = 4

ADAM_LR = 0.001
ADAM_B1 = 0.9
ADAM_B2 = 0.999
ADAM_EPS = 1e-08
ADAM_WD = 0.01
ADAM_STEP = 10

SUBLANE = 8
LANE = 128
VMEM_LIMIT = 56 * 1024 * 1024
TB = 256
TK = 512
SCAN_LANES = 256
SCAN_UNROLL = 4
HALO = SUBLANE

BIG = ("w_in", "w_up", "w_down", "w_out", "w_proj_a", "w_proj_b", "w_glu")
SMALL = ("g_mix", "a_re", "a_im", "log_dt", "b_re", "b_im", "c_re", "c_im", "d_skip", "b_glu",
         "g_sgu", "w_s", "b_s", "g_ffn", "conv_b", "g_final")
WEIGHTS = ("g_mix", "w_in", "a_re", "a_im", "log_dt", "b_re", "b_im", "c_re", "c_im", "d_skip",
           "w_glu", "b_glu", "w_proj_a", "g_sgu", "w_s", "b_s", "w_proj_b", "w_out", "g_ffn",
           "w_up", "conv_w", "conv_b", "w_down", "g_final")

ANY = pl.BlockSpec(memory_space=pl.ANY)


def _params(n_grid):
    return pltpu.CompilerParams(dimension_semantics=("arbitrary",) * n_grid if n_grid else None,
                                vmem_limit_bytes=VMEM_LIMIT)


def _whole():
    return pl.BlockSpec(memory_space=pltpu.VMEM)


def _rows(tb, ncol):
    return pl.BlockSpec((tb, ncol), lambda i: (i, 0))


def _acc(nrow, ncol):
    return pl.BlockSpec((nrow, ncol), lambda i: (0, 0))


def _dot(a, b):
    return jnp.dot(a.astype(BF16), b.astype(BF16), preferred_element_type=F32)


def _dot_nt(a, b):
    return lax.dot_general(a.astype(BF16), b.astype(BF16), (((1,), (1,)), ((), ())),
                           preferred_element_type=F32)


def _sigmoid(v):
    return 0.5 * jnp.tanh(0.5 * v) + 0.5


_GELU_C = math.sqrt(2.0 / math.pi)


def _gelu(v):
    return 0.5 * v * (1.0 + jnp.tanh(_GELU_C * (v + 0.044715 * v * v * v)))


def _gelu_and_grad(v):
    v2 = v * v
    t = jnp.tanh(_GELU_C * v * (1.0 + 0.044715 * v2))
    half = 0.5 * (1.0 + t)
    return v * half, half + 0.5 * v * (1.0 - t * t) * _GELU_C * (1.0 + 3.0 * 0.044715 * v2)


def _rms_stats(v):
    r = lax.rsqrt(jnp.mean(v * v, axis=-1, keepdims=True) + EPS)
    return r, v * r


def _rms_bwd(dxh, xh, r):
    return r * (dxh - xh * jnp.mean(dxh * xh, axis=-1, keepdims=True))


def _place():
    x, y, c = lax.axis_index("x"), lax.axis_index("y"), lax.axis_index("c")
    chips = [(1 - x, y), (x, 1 - y), (1 - x, 1 - y)]
    return x, y, c, chips


def _chip_index(chip):
    return 2 * chip[0] + chip[1]


def _remote(src, dst, send_sem, recv_sem, device):
    return pltpu.make_async_remote_copy(src_ref=src, dst_ref=dst, send_sem=send_sem,
                                        recv_sem=recv_sem, device_id=device, device_id_type=MESH)


def _half(ref_rows, c):
    hr = ref_rows // 2
    return pl.ds(pl.multiple_of(c * hr, SUBLANE), hr)


class _Job:
    def __init__(self, start, finish, n_sem, ins=(), inouts=(), outs=()):
        self.start, self.finish, self.n_sem = start, finish, n_sem
        self.ins, self.inouts, self.outs = list(ins), list(inouts), list(outs)


def _job_gather_ici(bufs, whole=()):
    n = len(bufs)

    def copies(io):
        x, y, c, chips = _place()
        k_me = 2 * x + y
        out = []
        for t in range(n):
            for j, ch in enumerate(chips):
                if t in whole:
                    src, land = io[t].at[k_me], io[t].at[_chip_index(ch)]
                else:
                    mine = _half(bufs[t].shape[1], c)
                    src, land = io[t].at[k_me, mine, :], io[t].at[_chip_index(ch), mine, :]
                out.append((src, land, 3 * t + j, (*ch, c)))
        return out

    def start(ins, io, outs, ssem, rsem):
        for src, _, i, dev in copies(io):
            _remote(src, src, ssem(i), rsem(i), dev).start()

    def finish(ins, io, outs, ssem, rsem):
        cps = copies(io)
        for _, land, i, dev in cps:
            _remote(land, land, ssem(i), rsem(i), dev).wait_recv()
        for src, _, i, dev in cps:
            _remote(src, src, ssem(i), rsem(i), dev).wait_send()

    return _Job(start, finish, 3 * n, inouts=bufs)


def _job_gather_sibling(bufs):
    n = len(bufs)

    def copies(io):
        x, y, c, chips = _place()
        out = []
        for t in range(n):
            rows = bufs[t].shape[1]
            for j, ch in enumerate(chips):
                k = _chip_index(ch)
                out.append((io[t].at[k, _half(rows, c), :], io[t].at[k, _half(rows, 1 - c), :],
                            3 * t + j, (x, y, 1 - c)))
        return out

    def start(ins, io, outs, ssem, rsem):
        for src, _, i, dev in copies(io):
            _remote(src, src, ssem(i), rsem(i), dev).start()

    def finish(ins, io, outs, ssem, rsem):
        cps = copies(io)
        for _, land, i, dev in cps:
            _remote(land, land, ssem(i), rsem(i), dev).wait_recv()
        for src, _, i, dev in cps:
            _remote(src, src, ssem(i), rsem(i), dev).wait_send()

    return _Job(start, finish, 3 * n, inouts=bufs)


def _job_sibling_halves(grads):
    n = len(grads)

    def build(ins, outs, ssem, rsem):
        x, y, c, _ = _place()
        return [_remote(ins[t].at[:, _half(grads[t].shape[1], 1 - c), :], outs[t], ssem(t), rsem(t),
                        (x, y, 1 - c)) for t in range(n)]

    def start(ins, io, outs, ssem, rsem):
        for cp in build(ins, outs, ssem, rsem):
            cp.start()

    def finish(ins, io, outs, ssem, rsem):
        for cp in build(ins, outs, ssem, rsem):
            cp.wait()

    return _Job(start, finish, n, ins=grads,
                outs=[jax.ShapeDtypeStruct((N_CHIP, g.shape[1] // 2, g.shape[2]), F32) for g in grads])


def _job_to_owner(sums):
    n = len(sums)

    def build(ins, outs, ssem, rsem):
        x, y, c, chips = _place()
        return [_remote(ins[t].at[_chip_index(ch)], outs[t].at[j], ssem(3 * t + j), rsem(3 * t + j),
                        (*ch, c)) for t in range(n) for j, ch in enumerate(chips)]

    def start(ins, io, outs, ssem, rsem):
        for cp in build(ins, outs, ssem, rsem):
            cp.start()

    def finish(ins, io, outs, ssem, rsem):
        for cp in build(ins, outs, ssem, rsem):
            cp.wait()

    return _Job(start, finish, 3 * n, ins=sums,
                outs=[jax.ShapeDtypeStruct((3,) + s.shape[1:], s.dtype) for s in sums])


def _job_swap_halves(bufs):
    n = len(bufs)

    def start(ins, io, outs, ssem, rsem):
        x, y, c, _ = _place()
        for t in range(n):
            mine = io[t].at[_half(bufs[t].shape[0], c), :]
            _remote(mine, mine, ssem(t), rsem(t), (x, y, 1 - c)).start()

    def finish(ins, io, outs, ssem, rsem):
        x, y, c, _ = _place()
        for t in range(n):
            theirs = io[t].at[_half(bufs[t].shape[0], 1 - c), :]
            _remote(theirs, theirs, ssem(t), rsem(t), (x, y, 1 - c)).wait_recv()
        for t in range(n):
            mine = io[t].at[_half(bufs[t].shape[0], c), :]
            _remote(mine, mine, ssem(t), rsem(t), (x, y, 1 - c)).wait_send()

    return _Job(start, finish, n, inouts=bufs)


def _call(body, name, grid, in_specs, out_specs, out_shape, args, jobs=(), scratch=()):
    n_in, n_out, n_scr = len(args), len(out_shape), len(scratch)
    job_in = [a for jb in jobs for a in jb.ins + jb.inouts]
    job_out = [s for jb in jobs
               for s in [jax.ShapeDtypeStruct(a.shape, a.dtype) for a in jb.inouts] + jb.outs]
    aliases, pos_in, pos_out = {}, n_in, n_out
    for jb in jobs:
        pos_in += len(jb.ins)
        for _ in jb.inouts:
            aliases[pos_in] = pos_out
            pos_in += 1
            pos_out += 1
        pos_out += len(jb.outs)
    n_sem = sum(jb.n_sem for jb in jobs)

    def wrapped(*refs):
        c_in = refs[:n_in]
        j_in = refs[n_in:n_in + len(job_in)]
        c_out = refs[n_in + len(job_in):n_in + len(job_in) + n_out]
        j_out = refs[n_in + len(job_in) + n_out:n_in + len(job_in) + n_out + len(job_out)]
        rest = refs[n_in + len(job_in) + n_out + len(job_out):]
        c_scr = rest[:n_scr]
        views, pi, po, ps = [], 0, 0, 0
        for jb in jobs:
            ins = j_in[pi:pi + len(jb.ins)]
            pi += len(jb.ins) + len(jb.inouts)
            io = j_out[po:po + len(jb.inouts)]
            new = j_out[po + len(jb.inouts):po + len(jb.inouts) + len(jb.outs)]
            po += len(jb.inouts) + len(jb.outs)
            send = (lambda i, o=ps: rest[n_scr].at[o + i])
            recv = (lambda i, o=ps: rest[n_scr + 1].at[o + i])
            ps += jb.n_sem
            views.append((ins, io, new, send, recv))

        def run(which):
            for jb, vw in zip(jobs, views):
                (jb.start if which == 0 else jb.finish)(*vw)

        if not grid:
            run(0)
            run(1)
            return
        if jobs:
            first = pl.program_id(0) == 0
            last = pl.program_id(0) == grid[0] - 1
            for d in range(1, len(grid)):
                first = jnp.logical_and(first, pl.program_id(d) == 0)
                last = jnp.logical_and(last, pl.program_id(d) == grid[d] - 1)
            pl.when(first)(lambda: run(0))
        body(*c_in, *c_out, *c_scr)
        if jobs:
            pl.when(last)(lambda: run(1))

    sems = [pltpu.SemaphoreType.DMA((n_sem,)), pltpu.SemaphoreType.DMA((n_sem,))] if jobs else []
    kwargs = dict(grid=grid) if grid else {}
    res = pl.pallas_call(
        wrapped, name=name, in_specs=list(in_specs) + [ANY] * len(job_in),
        out_specs=list(out_specs) + [ANY] * len(job_out),
        out_shape=list(out_shape) + job_out, scratch_shapes=list(scratch) + sems,
        input_output_aliases=aliases, compiler_params=_params(len(grid)), **kwargs,
    )(*args, *job_in)
    outs, pos, per_job = list(res[:n_out]), n_out, []
    for jb in jobs:
        k = len(jb.inouts) + len(jb.outs)
        per_job.append(list(res[pos:pos + k]))
        pos += k
    return outs, per_job


def _comm(name, jobs):
    return _call(None, name, (), [], [], [], [], jobs)[1]


def _fwd_in(x, g_mix, w_in, bre, bim, jobs=()):
    t_len = x.shape[0]
    cs = IN_COLS // N_CHIP

    def body(x_ref, g_ref, w_ref, bre_ref, bim_ref, p_ref, h_ref, bur_ref, bui_ref):
        xv = x_ref[...]
        r, xh = _rms_stats(xv)
        h = (xh * g_ref[...]).astype(BF16)
        h_ref[...] = h
        for k in range(N_CHIP):
            p_ref[:, k * cs:(k + 1) * cs] = jnp.dot(h, w_ref[k], preferred_element_type=F32)
        u = p_ref[:, 0:SSM_W].astype(BF16)
        bur_ref[...] = jnp.dot(u, bre_ref[...], preferred_element_type=F32)
        bui_ref[...] = jnp.dot(u, bim_ref[...], preferred_element_type=F32)

    return _call(
        body, "fwd_in", (t_len // TB,),
        [_rows(TB, D_MODEL), _whole(), _whole(), _whole(), _whole()],
        [_rows(TB, IN_COLS), _rows(TB, D_MODEL), _rows(TB, N_STATE), _rows(TB, N_STATE)],
        [jax.ShapeDtypeStruct((t_len, IN_COLS), F32), jax.ShapeDtypeStruct((t_len, D_MODEL), BF16),
         jax.ShapeDtypeStruct((t_len, N_STATE), F32), jax.ShapeDtypeStruct((t_len, N_STATE), F32)],
        [x, g_mix, w_in, bre, bim], jobs)


def _scan_local(xr, xi, tab, shifts):
    for q, s in enumerate(shifts):
        ar, ai = tab[2 * q], tab[2 * q + 1]
        rr = pltpu.roll(xr, s, 0)
        ri = pltpu.roll(xi, s, 0)
        xr, xi = xr + ar * rr - ai * ri, xi + ar * ri + ai * rr
    return xr, xi


def _scan_carry(xr, xi, tab, cr, ci):
    pr, pi = tab[6], tab[7]
    return xr + pr * cr - pi * ci, xi + pr * ci + pi * cr


def _scan_fwd(bur, bui, tab, jobs=()):
    t_len = bur.shape[0]
    nblk = t_len // SUBLANE
    lb = SCAN_LANES

    def body(br_ref, bi_ref, tab_ref, sr_ref, si_ref):
        tab_v = [tab_ref[q] for q in range(8)]

        def step(k, carry):
            cr, ci = carry
            rows = [pl.multiple_of((k * SCAN_UNROLL + u) * SUBLANE, SUBLANE)
                    for u in range(SCAN_UNROLL)]
            local = [_scan_local(br_ref[pl.ds(r0, SUBLANE), :], bi_ref[pl.ds(r0, SUBLANE), :],
                                 tab_v, (1, 2, 4)) for r0 in rows]
            for r0, (xr, xi) in zip(rows, local):
                xr, xi = _scan_carry(xr, xi, tab_v, cr, ci)
                sr_ref[pl.ds(r0, SUBLANE), :] = xr
                si_ref[pl.ds(r0, SUBLANE), :] = xi
                cr, ci = xr[SUBLANE - 1:SUBLANE, :], xi[SUBLANE - 1:SUBLANE, :]
            return cr, ci

        zero = jnp.zeros((1, lb), F32)
        lax.fori_loop(0, nblk // SCAN_UNROLL, step, (zero, zero))

    col = pl.BlockSpec((t_len, lb), lambda j: (0, j))
    return _call(
        body, "scan_fwd", (N_STATE // lb,),
        [col, col, pl.BlockSpec((8, SUBLANE, lb), lambda j: (0, 0, j))], [col, col],
        [jax.ShapeDtypeStruct((t_len, N_STATE), F32)] * 2, [bur, bui, tab], jobs)


def _sgu_mix(v, ws_ref, lane_lo):
    rows = []
    for c0 in range(0, v.shape[0], CHUNK):
        slabs = []
        for j in range(SGU_W // LANE):
            prod = jnp.dot(ws_ref[j], v[c0:c0 + CHUNK, j * LANE:(j + 1) * LANE].astype(BF16),
                           preferred_element_type=F32)
            slabs.append(jnp.where(lane_lo, prod[:CHUNK], prod[CHUNK:]))
        rows.append(jnp.concatenate(slabs, axis=1))
    return jnp.concatenate(rows, axis=0) if len(rows) > 1 else rows[0]


def _fwd_mix(x, p, str_, sti, cre, cim, d_skip, w_glu, b_glu, w_pa, g_sgu, ws_st, bmat, w_pb, w_out,
             jobs=()):
    t_len = x.shape[0]
    pc = D_MODEL // N_CHIP

    def body(x_ref, p_ref, sr_ref, si_ref, cre_ref, cim_ref, dsk_ref, wg_ref, bg_ref, wpa_ref,
             gs_ref, ws_ref, bm_ref, wpb_ref, wo_ref,
             x2_ref, y0_ref, z_ref, mx_ref, ya_ref, yb_ref):
        u = p_ref[:, 0:SSM_W]
        y0 = _dot(sr_ref[...], cre_ref[...]) - _dot(si_ref[...], cim_ref[...]) + dsk_ref[...] * u
        y0_ref[...] = y0
        y1 = _gelu(y0)
        z = _dot(y1, wg_ref[...]) + bg_ref[...]
        z_ref[...] = z
        ya_pre = (y1 * _sigmoid(z)).astype(BF16)
        for k in range(N_CHIP):
            ya_ref[:, k * pc:(k + 1) * pc] = jnp.dot(ya_pre, wpa_ref[k], preferred_element_type=F32)

        uvg = _gelu(p_ref[:, SSM_W:SSM_W + 2 * SGU_W])
        u2 = uvg[:, :SGU_W]
        _, vh = _rms_stats(uvg[:, SGU_W:])
        v3 = vh * gs_ref[...]
        lane_lo = lax.broadcasted_iota(jnp.int32, (CHUNK, LANE), 1) < SGU_D
        bias = jnp.concatenate([bm_ref[...]] * (TB // CHUNK), axis=0)
        mixed = _sgu_mix(v3, ws_ref, lane_lo) + bias
        mx_ref[...] = mixed
        sgu = (u2 * mixed).astype(BF16)
        for k in range(N_CHIP):
            yb_ref[:, k * pc:(k + 1) * pc] = jnp.dot(sgu, wpb_ref[k], preferred_element_type=F32)

        lg0 = SSM_W + 2 * SGU_W
        ga = _sigmoid(p_ref[:, lg0:lg0 + D_MODEL])
        gb = _sigmoid(p_ref[:, lg0 + D_MODEL:lg0 + 2 * D_MODEL])
        mrg = ga * ya_ref[...] + gb * yb_ref[...]
        x2_ref[...] = x_ref[...] + _dot(mrg, wo_ref[...])

    return _call(
        body, "fwd_mix", (t_len // TB,),
        [_rows(TB, D_MODEL), _rows(TB, IN_COLS), _rows(TB, N_STATE), _rows(TB, N_STATE)]
        + [_whole()] * 11,
        [_rows(TB, D_MODEL), _rows(TB, SSM_W), _rows(TB, SSM_W), _rows(TB, SGU_W),
         _rows(TB, D_MODEL), _rows(TB, D_MODEL)],
        [jax.ShapeDtypeStruct((t_len, D_MODEL), F32), jax.ShapeDtypeStruct((t_len, SSM_W), F32),
         jax.ShapeDtypeStruct((t_len, SSM_W), F32), jax.ShapeDtypeStruct((t_len, SGU_W), F32),
         jax.ShapeDtypeStruct((t_len, D_MODEL), F32), jax.ShapeDtypeStruct((t_len, D_MODEL), F32)],
        [x, p, str_, sti, cre, cim, d_skip, w_glu, b_glu, w_pa, g_sgu, ws_st, bmat, w_pb, w_out], jobs)


def _conv_taps(v, cw_ref, c0, width):
    w0 = cw_ref[0:1, c0:c0 + width]
    w1 = cw_ref[1:2, c0:c0 + width]
    w2 = cw_ref[2:3, c0:c0 + width]
    return w0 * pltpu.roll(v, 2, 0) + w1 * pltpu.roll(v, 1, 0) + w2 * v


def _fwd_ffn(x2, target, g_ffn, w_up, conv_w, conv_b, w_down, g_final):
    t_len = x2.shape[0]
    half = D_FF // 2
    blocks_per_halo = TB // HALO

    def body(x2_ref, xp_ref, tg_ref, gf_ref, wu_ref, cw_ref, cb_ref, wd_ref, gl_ref,
             up_ref, act_ref, f_ref, h2_ref, dx3_ref, sm_ref):
        i = pl.program_id(0)
        xe = jnp.concatenate([xp_ref[...] * jnp.where(i == 0, 0.0, 1.0), x2_ref[...]], axis=0)
        _, xh = _rms_stats(xe)
        h2 = (xh * gf_ref[...]).astype(BF16)
        h2_ref[...] = h2[HALO:]
        acc = jnp.zeros((TB, D_MODEL), F32)
        for hc in range(2):
            ca = hc * half
            cb = D_FF + hc * half
            ua = jnp.dot(h2, wu_ref[hc], preferred_element_type=F32)
            ub = jnp.dot(h2, wu_ref[2 + hc], preferred_element_type=F32)
            up_ref[:, ca:ca + half] = ua[HALO:].astype(BF16)
            up_ref[:, cb:cb + half] = ub[HALO:].astype(BF16)
            ac = _conv_taps(ua, cw_ref, ca, half)[HALO:] + cb_ref[:, ca:ca + half]
            bc = _conv_taps(ub, cw_ref, cb, half)[HALO:] + cb_ref[:, cb:cb + half]
            act_ref[:, ca:ca + half] = ac.astype(BF16)
            act_ref[:, cb:cb + half] = bc.astype(BF16)
            f = (ac * _sigmoid(ac) * bc).astype(BF16)
            f_ref[:, ca:ca + half] = f
            acc = acc + jnp.dot(f, wd_ref[ca:ca + half, :], preferred_element_type=F32)
        x3 = x2_ref[...] + acc
        r3, xh3 = _rms_stats(x3)
        err = xh3 * gl_ref[...] - tg_ref[...]
        dout = err * (1.0 / D_MODEL)
        dx3_ref[...] = _rms_bwd(dout * gl_ref[...], xh3, r3)
        dgl = jnp.sum(dout * xh3, axis=0, keepdims=True)
        loss = 0.5 * jnp.sum(jnp.mean(err * err, axis=-1, keepdims=True), axis=0, keepdims=True)
        upd = jnp.concatenate([dgl, jnp.broadcast_to(loss, (1, D_MODEL)),
                               jnp.zeros((SUBLANE - 2, D_MODEL), F32)], axis=0)

        @pl.when(i == 0)
        def _():
            sm_ref[...] = upd

        @pl.when(i > 0)
        def _():
            sm_ref[...] += upd

    prev = pl.BlockSpec((HALO, D_MODEL), lambda i: (jnp.maximum(i * blocks_per_halo - 1, 0), 0))
    return _call(
        body, "fwd_ffn", (t_len // TB,),
        [_rows(TB, D_MODEL), prev, _rows(TB, D_MODEL)] + [_whole()] * 6,
        [_rows(TB, 2 * D_FF), _rows(TB, 2 * D_FF), _rows(TB, D_FF), _rows(TB, D_MODEL),
         _rows(TB, D_MODEL), _acc(SUBLANE, D_MODEL)],
        [jax.ShapeDtypeStruct((t_len, 2 * D_FF), BF16), jax.ShapeDtypeStruct((t_len, 2 * D_FF), BF16),
         jax.ShapeDtypeStruct((t_len, D_FF), BF16), jax.ShapeDtypeStruct((t_len, D_MODEL), BF16),
         jax.ShapeDtypeStruct((t_len, D_MODEL), F32), jax.ShapeDtypeStruct((SUBLANE, D_MODEL), F32)],
        [x2, x2, target, g_ffn, w_up, conv_w, conv_b, w_down, g_final])[0]


def _bwd_ffn(dx3, up, act, x2, g_ffn, w_up, conv_w, w_down, jobs=()):
    t_len = x2.shape[0]
    half = D_FF // 2
    nblk = t_len // TB
    halo_b = 2 * HALO
    n_e = TB + HALO

    def body(dx_ref, dxn_ref, up_ref, act_ref, actn_ref, x2_ref, gf_ref, wu_ref, cw_ref,
             wd_ref, dx2_ref, dup_ref, smw_ref, smg_ref):
        i = pl.program_id(0)
        keep_last = jnp.where(i == nblk - 1, 0.0, 1.0)
        dxe = jnp.concatenate([dx_ref[...], dxn_ref[...] * keep_last], axis=0).astype(BF16)
        dh2 = jnp.zeros((TB, D_MODEL), F32)
        zpad = jnp.zeros((1, half), F32)
        for hc in range(2):
            ca = hc * half
            cb = D_FF + hc * half
            ac = jnp.concatenate([act_ref[:, ca:ca + half].astype(F32),
                                  actn_ref[:, ca:ca + half].astype(F32)[:HALO]], axis=0)
            bc = jnp.concatenate([act_ref[:, cb:cb + half].astype(F32),
                                  actn_ref[:, cb:cb + half].astype(F32)[:HALO]], axis=0)
            wa = [cw_ref[k:k + 1, ca:ca + half] for k in range(3)]
            wb = [cw_ref[k:k + 1, cb:cb + half] for k in range(3)]
            df = lax.dot_general(dxe, wd_ref[ca:ca + half, :], (((1,), (1,)), ((), ())),
                                 preferred_element_type=F32)
            sg = _sigmoid(ac)
            da = df * bc * sg * (1.0 + ac * (1.0 - sg))
            db = df * ac * sg
            da1, da2 = pltpu.roll(da, n_e - 1, 0), pltpu.roll(da, n_e - 2, 0)
            db1, db2 = pltpu.roll(db, n_e - 1, 0), pltpu.roll(db, n_e - 2, 0)
            dua = (wa[2] * da + wa[1] * da1 + wa[0] * da2)[:TB]
            dub = (wb[2] * db + wb[1] * db1 + wb[0] * db2)[:TB]
            dup_ref[:, ca:ca + half] = dua.astype(BF16)
            dup_ref[:, cb:cb + half] = dub.astype(BF16)
            dh2 = dh2 + _dot_nt(dua, wu_ref[hc]) + _dot_nt(dub, wu_ref[2 + hc])
            rows = []
            for u_, d0, d1, d2 in ((up_ref[:, ca:ca + half].astype(F32), da, da1, da2),
                                   (up_ref[:, cb:cb + half].astype(F32), db, db1, db2)):
                rows.append([jnp.sum(u_ * d2[:TB], axis=0, keepdims=True),
                             jnp.sum(u_ * d1[:TB], axis=0, keepdims=True),
                             jnp.sum(u_ * d0[:TB], axis=0, keepdims=True),
                             jnp.sum(d0[:TB], axis=0, keepdims=True)])
            for c0, rws in ((ca, rows[0]), (cb, rows[1])):
                upd = jnp.concatenate(rws + [zpad] * (SUBLANE - 4), axis=0)

                @pl.when(i == 0)
                def _(upd=upd, c0=c0):
                    smw_ref[:, c0:c0 + half] = upd

                @pl.when(i > 0)
                def _(upd=upd, c0=c0):
                    smw_ref[:, c0:c0 + half] += upd

        r2, xh2 = _rms_stats(x2_ref[...])
        dx2_ref[...] = dx_ref[...] + _rms_bwd(dh2 * gf_ref[...], xh2, r2)
        updg = jnp.concatenate([jnp.sum(dh2 * xh2, axis=0, keepdims=True),
                                jnp.zeros((SUBLANE - 1, D_MODEL), F32)], axis=0)

        @pl.when(i == 0)
        def _():
            smg_ref[...] = updg

        @pl.when(i > 0)
        def _():
            smg_ref[...] += updg

    nxt_d = pl.BlockSpec((HALO, D_MODEL),
                         lambda i: (jnp.minimum((i + 1) * (TB // HALO), t_len // HALO - 1), 0))
    nxt_a = pl.BlockSpec((halo_b, 2 * D_FF),
                         lambda i: (jnp.minimum((i + 1) * (TB // halo_b), t_len // halo_b - 1), 0))
    return _call(
        body, "bwd_ffn", (nblk,),
        [_rows(TB, D_MODEL), nxt_d, _rows(TB, 2 * D_FF), _rows(TB, 2 * D_FF), nxt_a,
         _rows(TB, D_MODEL)] + [_whole()] * 4,
        [_rows(TB, D_MODEL), _rows(TB, 2 * D_FF), _acc(SUBLANE, 2 * D_FF), _acc(SUBLANE, D_MODEL)],
        [jax.ShapeDtypeStruct((t_len, D_MODEL), F32), jax.ShapeDtypeStruct((t_len, 2 * D_FF), BF16),
         jax.ShapeDtypeStruct((SUBLANE, 2 * D_FF), F32), jax.ShapeDtypeStruct((SUBLANE, D_MODEL), F32)],
        [dx3, dx3, up, act, act, x2, g_ffn, w_up, conv_w, w_down], jobs)


def _bwd_mix(dx2, p, y0, z, mixed, ya, yb, w_out, w_pa, w_pb, w_glu, cre, cim, ws_st, wst_st,
             d_skip, g_sgu, jobs=()):
    t_len = dx2.shape[0]
    pc = D_MODEL // N_CHIP
    n_slab = SGU_W // LANE

    def body(dx_ref, p_ref, y0_ref, z_ref, mx_ref, ya_ref, yb_ref, wo_ref, wpa_ref, wpb_ref,
             wg_ref, cre_ref, cim_ref, ws_ref, wst_ref, dsk_ref, gs_ref,
             dsr_ref, dsi_ref, du_ref, drest_ref, mrg_ref, dya_ref, dyb_ref, yap_ref, dz_ref,
             y1_ref, sgu_ref, dy0_ref, sm_ref, dbm_ref, dws_ref):
        i = pl.program_id(0)
        first = i == 0
        lg0 = SSM_W + 2 * SGU_W
        dmrg = _dot_nt(dx_ref[...], wo_ref[...])
        ga = _sigmoid(p_ref[:, lg0:lg0 + D_MODEL])
        gb = _sigmoid(p_ref[:, lg0 + D_MODEL:lg0 + 2 * D_MODEL])
        yav = ya_ref[...]
        ybv = yb_ref[...]
        mrg_ref[...] = (ga * yav + gb * ybv).astype(BF16)
        drest_ref[:, 2 * SGU_W:2 * SGU_W + D_MODEL] = (dmrg * yav * ga * (1.0 - ga)).astype(BF16)
        drest_ref[:, 2 * SGU_W + D_MODEL:] = (dmrg * ybv * gb * (1.0 - gb)).astype(BF16)
        dya = (dmrg * ga).astype(BF16)
        dyb = (dmrg * gb).astype(BF16)
        dya_ref[...] = dya
        dyb_ref[...] = dyb

        y0v = y0_ref[...]
        y1, y1_grad = _gelu_and_grad(y0v)
        sz = _sigmoid(z_ref[...])
        y1_ref[...] = y1.astype(BF16)
        yap_ref[...] = (y1 * sz).astype(BF16)
        dyap = jnp.zeros((TB, SSM_W), F32)
        for k in range(N_CHIP):
            dyap = dyap + _dot_nt(dya[:, k * pc:(k + 1) * pc], wpa_ref[k])
        dz = dyap * y1 * sz * (1.0 - sz)
        dz_ref[...] = dz.astype(BF16)
        dy0 = (dyap * sz + _dot_nt(dz, wg_ref[...])) * y1_grad
        dy0_ref[...] = dy0.astype(BF16)
        u = p_ref[:, 0:SSM_W]
        du_ref[...] = dy0 * dsk_ref[...]
        dsr_ref[...] = _dot_nt(dy0, cre_ref[...])
        dsi_ref[...] = -_dot_nt(dy0, cim_ref[...])

        uv = p_ref[:, SSM_W:lg0]
        uvg, gg = _gelu_and_grad(uv)
        u2 = uvg[:, :SGU_W]
        rv, vh = _rms_stats(uvg[:, SGU_W:])
        v3 = vh * gs_ref[...]
        mixed = mx_ref[...]
        dsgu = jnp.zeros((TB, SGU_W), F32)
        for k in range(N_CHIP):
            dsgu = dsgu + _dot_nt(dyb[:, k * pc:(k + 1) * pc], wpb_ref[k])
        sgu_ref[...] = (u2 * mixed).astype(BF16)
        du2 = dsgu * mixed
        dmix = dsgu * u2
        lane_lo = lax.broadcasted_iota(jnp.int32, (CHUNK, LANE), 1) < SGU_D
        dv3 = _sgu_mix(dmix, wst_ref, lane_lo)
        dbm = jnp.zeros((CHUNK, SGU_W), F32)
        for c0 in range(0, TB, CHUNK):
            dbm = dbm + dmix[c0:c0 + CHUNK]
        for j in range(n_slab):
            lo = jnp.zeros((CHUNK, CHUNK), F32)
            hi = jnp.zeros((CHUNK, CHUNK), F32)
            for c0 in range(0, TB, CHUNK):
                dsl = dmix[c0:c0 + CHUNK, j * LANE:(j + 1) * LANE]
                vsl = v3[c0:c0 + CHUNK, j * LANE:(j + 1) * LANE]
                lo = lo + _dot_nt(jnp.where(lane_lo, dsl, 0.0), vsl)
                hi = hi + _dot_nt(jnp.where(lane_lo, 0.0, dsl), vsl)

            @pl.when(first)
            def _(lo=lo, hi=hi, j=j):
                dws_ref[2 * j] = lo
                dws_ref[2 * j + 1] = hi

            @pl.when(jnp.logical_not(first))
            def _(lo=lo, hi=hi, j=j):
                dws_ref[2 * j] += lo
                dws_ref[2 * j + 1] += hi

        dv2 = _rms_bwd(dv3 * gs_ref[...], vh, rv)
        drest_ref[:, 0:SGU_W] = (du2 * gg[:, :SGU_W]).astype(BF16)
        drest_ref[:, SGU_W:2 * SGU_W] = (dv2 * gg[:, SGU_W:]).astype(BF16)

        upd = jnp.concatenate([jnp.sum(dy0 * u, axis=0, keepdims=True),
                               jnp.sum(dz, axis=0, keepdims=True),
                               jnp.sum(dv3 * vh, axis=0, keepdims=True),
                               jnp.zeros((SUBLANE - 3, SSM_W), F32)], axis=0)

        @pl.when(first)
        def _():
            sm_ref[...] = upd
            dbm_ref[...] = dbm

        @pl.when(jnp.logical_not(first))
        def _():
            sm_ref[...] += upd
            dbm_ref[...] += dbm

    rest = 2 * SGU_W + 2 * D_MODEL
    bf_d, bf_s = jax.ShapeDtypeStruct((t_len, D_MODEL), BF16), jax.ShapeDtypeStruct((t_len, SSM_W), BF16)
    return _call(
        body, "bwd_mix", (t_len // TB,),
        [_rows(TB, D_MODEL), _rows(TB, IN_COLS), _rows(TB, SSM_W), _rows(TB, SSM_W),
         _rows(TB, SGU_W), _rows(TB, D_MODEL), _rows(TB, D_MODEL)] + [_whole()] * 10,
        [_rows(TB, N_STATE), _rows(TB, N_STATE), _rows(TB, SSM_W), _rows(TB, rest),
         _rows(TB, D_MODEL), _rows(TB, D_MODEL), _rows(TB, D_MODEL), _rows(TB, SSM_W),
         _rows(TB, SSM_W), _rows(TB, SSM_W), _rows(TB, SGU_W), _rows(TB, SSM_W),
         _acc(SUBLANE, SSM_W), _acc(CHUNK, SGU_W),
         pl.BlockSpec((SGU_G, CHUNK, CHUNK), lambda i: (0, 0, 0))],
        [jax.ShapeDtypeStruct((t_len, N_STATE), F32), jax.ShapeDtypeStruct((t_len, N_STATE), F32),
         jax.ShapeDtypeStruct((t_len, SSM_W), F32), jax.ShapeDtypeStruct((t_len, rest), BF16),
         bf_d, bf_d, bf_d, bf_s, bf_s, bf_s, bf_s, bf_s,
         jax.ShapeDtypeStruct((SUBLANE, SSM_W), F32), jax.ShapeDtypeStruct((CHUNK, SGU_W), F32),
         jax.ShapeDtypeStruct((SGU_G, CHUNK, CHUNK), F32)],
        [dx2, p, y0, z, mixed, ya, yb, w_out, w_pa, w_pb, w_glu, cre, cim, ws_st, wst_st, d_skip,
         g_sgu], jobs)


def _scan_bwd(dsr, dsi, str_, sti, tab_rev, jobs=()):
    t_len = dsr.shape[0]
    nblk = t_len // SUBLANE
    lb = SCAN_LANES

    def body(dr_ref, di_ref, sr_ref, si_ref, tab_ref, lr_ref, li_ref, dar_ref, dai_ref):
        tab_v = [tab_ref[q] for q in range(8)]
        row0 = lax.broadcasted_iota(jnp.int32, (SUBLANE, lb), 0) == 0

        def step(k, carry):
            cr, ci, acr, aci = carry
            blocks = [nblk - 1 - (k * SCAN_UNROLL + u) for u in range(SCAN_UNROLL)]
            rows = [pl.multiple_of(kk * SUBLANE, SUBLANE) for kk in blocks]
            local = [_scan_local(dr_ref[pl.ds(r0, SUBLANE), :], di_ref[pl.ds(r0, SUBLANE), :],
                                 tab_v, (7, 6, 4)) for r0 in rows]
            for kk, r0, (xr, xi) in zip(blocks, rows, local):
                xr, xi = _scan_carry(xr, xi, tab_v, cr, ci)
                lr_ref[pl.ds(r0, SUBLANE), :] = xr
                li_ref[pl.ds(r0, SUBLANE), :] = xi
                cr, ci = xr[0:1, :], xi[0:1, :]
                rp = pl.multiple_of(jnp.maximum(kk - 1, 0) * SUBLANE, SUBLANE)
                has_prev = jnp.where(kk > 0, 1.0, 0.0)
                pr = sr_ref[pl.ds(rp, SUBLANE), :][SUBLANE - 1:SUBLANE, :] * has_prev
                pi = si_ref[pl.ds(rp, SUBLANE), :][SUBLANE - 1:SUBLANE, :] * has_prev
                s_r = jnp.where(row0, pr, pltpu.roll(sr_ref[pl.ds(r0, SUBLANE), :], 1, 0))
                s_i = jnp.where(row0, pi, pltpu.roll(si_ref[pl.ds(r0, SUBLANE), :], 1, 0))
                acr = acr + xr * s_r + xi * s_i
                aci = aci + xi * s_r - xr * s_i
            return cr, ci, acr, aci

        zero = jnp.zeros((1, lb), F32)
        zacc = jnp.zeros((SUBLANE, lb), F32)
        _, _, acr, aci = lax.fori_loop(0, nblk // SCAN_UNROLL, step, (zero, zero, zacc, zacc))
        dar_ref[...] = acr
        dai_ref[...] = aci

    col = pl.BlockSpec((t_len, lb), lambda j: (0, j))
    small = pl.BlockSpec((SUBLANE, lb), lambda j: (0, j))
    return _call(
        body, "scan_bwd", (N_STATE // lb,),
        [col, col, col, col, pl.BlockSpec((8, SUBLANE, lb), lambda j: (0, 0, j))],
        [col, col, small, small],
        [jax.ShapeDtypeStruct((t_len, N_STATE), F32)] * 2
        + [jax.ShapeDtypeStruct((SUBLANE, N_STATE), F32)] * 2,
        [dsr, dsi, str_, sti, tab_rev], jobs)


def _bwd_in(lam_r, lam_i, du_part, drest, x, dx2, g_mix, w_in, bre, bim, jobs=()):
    t_len = x.shape[0]
    cs = IN_COLS // N_CHIP

    def body(lr_ref, li_ref, du_ref, dr_ref, x_ref, dx2_ref, g_ref, w_ref, bre_ref, bim_ref,
             gx_ref, dp_ref, sm_ref):
        i = pl.program_id(0)
        du = du_ref[...] + _dot_nt(lr_ref[...], bre_ref[...]) + _dot_nt(li_ref[...], bim_ref[...])
        dp_ref[:, 0:SSM_W] = du.astype(BF16)
        dp_ref[:, SSM_W:] = dr_ref[...]
        dh = jnp.zeros((TB, D_MODEL), F32)
        for k in range(N_CHIP):
            dh = dh + _dot_nt(dp_ref[:, k * cs:(k + 1) * cs], w_ref[k])
        r, xh = _rms_stats(x_ref[...])
        gx_ref[...] = dx2_ref[...] + _rms_bwd(dh * g_ref[...], xh, r)
        upd = jnp.concatenate([jnp.sum(dh * xh, axis=0, keepdims=True),
                               jnp.zeros((SUBLANE - 1, D_MODEL), F32)], axis=0)

        @pl.when(i == 0)
        def _():
            sm_ref[...] = upd

        @pl.when(i > 0)
        def _():
            sm_ref[...] += upd

    return _call(
        body, "bwd_in", (t_len // TB,),
        [_rows(TB, N_STATE), _rows(TB, N_STATE), _rows(TB, SSM_W), _rows(TB, IN_COLS - SSM_W),
         _rows(TB, D_MODEL), _rows(TB, D_MODEL)] + [_whole()] * 4,
        [_rows(TB, D_MODEL), _rows(TB, IN_COLS), _acc(SUBLANE, D_MODEL)],
        [jax.ShapeDtypeStruct((t_len, D_MODEL), F32), jax.ShapeDtypeStruct((t_len, IN_COLS), BF16),
         jax.ShapeDtypeStruct((SUBLANE, D_MODEL), F32)],
        [lam_r, lam_i, du_part, drest, x, dx2, g_mix, w_in, bre, bim], jobs)


def _matmul_tn(a, b, name, out_shape, grid_ij, a_blk, a_map, b_blk, b_map, o_blk, o_map, jobs=()):
    tk = a_blk[0]
    nk = a.shape[0] // tk
    assert nk * tk == a.shape[0] and nk > 0

    def body(a_ref, b_ref, o_ref, acc_ref):
        k = pl.program_id(2)

        @pl.when(k == 0)
        def _():
            acc_ref[...] = jnp.zeros_like(acc_ref)

        acc_ref[...] += lax.dot_general(a_ref[...].astype(BF16), b_ref[...].astype(BF16),
                                        (((0,), (0,)), ((), ())), preferred_element_type=F32)

        @pl.when(k == nk - 1)
        def _():
            o_ref[...] = acc_ref[...]

    outs, per_job = _call(
        body, name, (grid_ij[0], grid_ij[1], nk),
        [pl.BlockSpec(a_blk, a_map), pl.BlockSpec(b_blk, b_map)], [pl.BlockSpec(o_blk, o_map)],
        [jax.ShapeDtypeStruct(out_shape, F32)], [a, b], jobs,
        scratch=[pltpu.VMEM((a_blk[1], b_blk[1]), F32)])
    return outs[0], per_job


def _dw_shards(a, b, name, tk, jobs=()):
    m, n = a.shape[1], b.shape[1]
    tn = n // N_CHIP
    tk = min(tk, a.shape[0])
    return _matmul_tn(a, b, name, (N_CHIP, m, tn), (1, N_CHIP),
                      (tk, m), lambda i, j, k: (k, 0), (tk, tn), lambda i, j, k: (k, j),
                      (None, m, tn), lambda i, j, k: (j, 0, 0), jobs)


def _dw_rows(a, b, name, tm, tk):
    m, n = a.shape[1], b.shape[1]
    tk = min(tk, a.shape[0])
    return _matmul_tn(a, b, name, (m, n), (m // tm, 1),
                      (tk, tm), lambda i, j, k: (k, i), (tk, n), lambda i, j, k: (k, 0),
                      (tm, n), lambda i, j, k: (i, 0))[0]


def _dw_cols(a, b, name, tn, sharded, jobs=()):
    t_len, m = a.shape
    n = b.shape[1]

    def body(a_ref, b_ref, o_ref):
        o_ref[...] = lax.dot_general(a_ref[...].astype(BF16), b_ref[...].astype(BF16),
                                     (((0,), (0,)), ((), ())), preferred_element_type=F32)

    if sharded:
        o_spec, o_shape = pl.BlockSpec((None, m, tn), lambda j: (j, 0, 0)), (n // tn, m, tn)
    else:
        o_spec, o_shape = pl.BlockSpec((m, tn), lambda j: (0, j)), (m, n)
    outs, per_job = _call(body, name, (n // tn,),
                          [_whole(), pl.BlockSpec((t_len, tn), lambda j: (0, j))], [o_spec],
                          [jax.ShapeDtypeStruct(o_shape, F32)], [a, b], jobs)
    return outs[0], per_job


def _dw_pair(a, m, b1, b2, name, jobs=()):
    n = b1.shape[1]
    tk = min(TK, a.shape[0])
    nk = a.shape[0] // tk
    rows_per_slab = m * LANE // n

    def body(a_ref, b1_ref, b2_ref, o1_ref, o2_ref, acc1_ref, acc2_ref):
        k = pl.program_id(0)
        a_t = a_ref[...].astype(BF16).T
        for b_ref, o_ref, acc_ref in ((b1_ref, o1_ref, acc1_ref), (b2_ref, o2_ref, acc2_ref)):
            piece = jnp.dot(a_t, b_ref[...].astype(BF16), preferred_element_type=F32)

            @pl.when(k == 0)
            def _(piece=piece, acc_ref=acc_ref):
                acc_ref[...] = piece

            @pl.when(k > 0)
            def _(piece=piece, acc_ref=acc_ref):
                acc_ref[...] += piece

            @pl.when(k == nk - 1)
            def _(o_ref=o_ref, acc_ref=acc_ref):
                for j in range(n // LANE):
                    rows = slice(j * rows_per_slab, (j + 1) * rows_per_slab)
                    o_ref[rows, :] = acc_ref[rows, j * LANE:(j + 1) * LANE]

    tok = pl.BlockSpec((tk, n), lambda k: (k, 0))
    out = pl.BlockSpec((m, LANE), lambda k: (0, 0))
    return _call(body, name, (nk,),
                 [pl.BlockSpec((tk, m), lambda k: (k, 0)), tok, tok], [out, out],
                 [jax.ShapeDtypeStruct((m, LANE), F32)] * 2, [a, b1, b2], jobs,
                 scratch=[pltpu.VMEM((m, n), F32)] * 2)


def _prefetch_call(body, name, grid, scalars, in_specs, out_specs, out_shape, args):
    return pl.pallas_call(
        body, name=name,
        grid_spec=pltpu.PrefetchScalarGridSpec(num_scalar_prefetch=1, grid=grid, in_specs=in_specs,
                                               out_specs=out_specs),
        out_shape=out_shape, compiler_params=_params(len(grid)),
    )(scalars, *args)


def _place_shard(w, where, name, dtype, tr):
    rows, cols = w.shape

    def body(s_ref, w_ref, o_ref):
        o_ref[...] = w_ref[...].astype(dtype)

    return _prefetch_call(
        body, name, (rows // tr,), where,
        [pl.BlockSpec((tr, cols), lambda i, s: (i, 0))],
        pl.BlockSpec((None, tr, cols), lambda i, s: (s[0], i, 0)),
        jax.ShapeDtypeStruct((N_CHIP, rows, cols), dtype), [w])


def _add_sibling(g, got, where, name):
    _, rs, cs = g.shape
    hr = rs // 2

    def body(s_ref, g_ref, got_ref, o_ref):
        o_ref[...] = (g_ref[...] + got_ref[...]).astype(BF16)

    return _prefetch_call(
        body, name, (N_CHIP,), where,
        [pl.BlockSpec((None, hr, cs), lambda k, s: (k, s[1], 0)),
         pl.BlockSpec((None, hr, cs), lambda k, s: (k, 0, 0))],
        pl.BlockSpec((None, hr, cs), lambda k, s: (k, 0, 0)),
        jax.ShapeDtypeStruct((N_CHIP, hr, cs), BF16), [g, got])


def _add_chips(sums, got, where, name):
    _, hr, cs = sums.shape

    def body(s_ref, own_ref, got_ref, o_ref):
        o_ref[...] = ((own_ref[...].astype(F32) + got_ref[0].astype(F32))
                      + got_ref[1].astype(F32)) + got_ref[2].astype(F32)

    return _prefetch_call(
        body, name, (1,), where,
        [pl.BlockSpec((None, hr, cs), lambda i, s: (s[0], 0, 0)),
         pl.BlockSpec((3, hr, cs), lambda i, s: (0, 0, 0))],
        pl.BlockSpec((hr, cs), lambda i, s: (s[1], 0)),
        jax.ShapeDtypeStruct((2 * hr, cs), F32), [sums, got])


def _small_allreduce(pack):
    rows = pack.shape[0]
    half = rows // 2

    def body(in_ref, out_ref, sib_ref, slots_ref, s_a, r_a, s_b, r_b, s_c, r_c):
        x, y, c, chips = _place()
        k_me = 2 * x + y
        sib = (x, y, 1 - c)
        first = _remote(in_ref, sib_ref, s_a, r_a, sib)
        first.start()
        first.wait()
        mine = _half(rows, c)
        slots_ref[k_me] = in_ref[mine, :] + sib_ref[mine, :]
        cps = [_remote(slots_ref.at[k_me], slots_ref.at[k_me], s_b.at[j], r_b.at[j], (*ch, c))
               for j, ch in enumerate(chips)]
        for cp in cps:
            cp.start()
        for j, ch in enumerate(chips):
            slot = slots_ref.at[_chip_index(ch)]
            _remote(slot, slot, s_b.at[j], r_b.at[j], (*ch, c)).wait_recv()
        for cp in cps:
            cp.wait_send()
        out_ref[mine, :] = ((slots_ref[0] + slots_ref[1]) + slots_ref[2]) + slots_ref[3]
        last = _remote(out_ref.at[mine, :], out_ref.at[mine, :], s_c, r_c, sib)
        last.start()
        theirs = out_ref.at[_half(rows, 1 - c), :]
        _remote(theirs, theirs, s_c, r_c, sib).wait_recv()
        last.wait_send()

    return pl.pallas_call(
        body, name="small_allreduce", in_specs=[_whole()], out_specs=_whole(),
        out_shape=jax.ShapeDtypeStruct(pack.shape, F32),
        scratch_shapes=[pltpu.VMEM(pack.shape, F32), pltpu.VMEM((N_CHIP, half, LANE), F32),
                        pltpu.SemaphoreType.DMA, pltpu.SemaphoreType.DMA,
                        pltpu.SemaphoreType.DMA((3,)), pltpu.SemaphoreType.DMA((3,)),
                        pltpu.SemaphoreType.DMA, pltpu.SemaphoreType.DMA],
        compiler_params=_params(0),
    )(pack)


def _adamw_update(w_ref, g_ref, m_ref, v_ref, d_ref, mo_ref, vo_ref):
    gv = g_ref[...]
    mn = ADAM_B1 * m_ref[...] + (1.0 - ADAM_B1) * gv
    vn = ADAM_B2 * v_ref[...] + (1.0 - ADAM_B2) * (gv * gv)
    mo_ref[...] = mn
    vo_ref[...] = vn
    m_hat = mn / (1.0 - ADAM_B1 ** ADAM_STEP)
    v_hat = vn / (1.0 - ADAM_B2 ** ADAM_STEP)
    d_ref[...] = -ADAM_LR * (m_hat / (jnp.sqrt(v_hat) + ADAM_EPS) + ADAM_WD * w_ref[...])


def _adamw(w, g, m, v, name, tr):
    rows, cols = w.shape
    blk = _rows(tr, cols)
    return _call(_adamw_update, name, (rows // tr,), [blk] * 4, [blk] * 3,
                 [jax.ShapeDtypeStruct(w.shape, F32)] * 3, [w, g, m, v])[0]


def _adamw_many(ws, gs, ms, vs, name):
    n = len(ws)

    def body(*refs):
        for t in range(n):
            _adamw_update(*[refs[q * n + t] for q in range(7)])

    outs = pl.pallas_call(
        body, name=name, in_specs=[_whole()] * (4 * n), out_specs=[_whole()] * (3 * n),
        out_shape=[jax.ShapeDtypeStruct(a.shape, F32) for _ in range(3) for a in ws],
        compiler_params=_params(0),
    )(*ws, *gs, *ms, *vs)
    return outs[:n], outs[n:2 * n], outs[2 * n:]


def _ssm_discretize(a_re, a_im, log_dt, b_re, b_im):
    dt = jnp.exp(log_dt)[:, None]
    mag = jnp.exp(dt * a_re)
    abr = mag * jnp.cos(dt * a_im)
    abi = mag * jnp.sin(dt * a_im)
    den = a_re * a_re + a_im * a_im
    nr = abr - 1.0
    ni = abi
    f_re = (nr * a_re + ni * a_im) / den
    f_im = (ni * a_re - nr * a_im) / den
    bbr = f_re[..., None] * b_re - f_im[..., None] * b_im
    bbi = f_re[..., None] * b_im + f_im[..., None] * b_re
    return abr, abi, bbr, bbi


def _scan_tables(abr, abi):
    ar = abr.reshape(1, N_STATE)
    ai = abi.reshape(1, N_STATE)
    pr, pi = [ar], [ai]
    for _ in range(SUBLANE - 1):
        pr, pi = pr + [pr[-1] * ar - pi[-1] * ai], pi + [pr[-1] * ai + pi[-1] * ar]
    row = jnp.arange(SUBLANE)[:, None]
    tabs = []
    for d in (1, 2, 4):
        tabs.append(jnp.where(row >= d, pr[d - 1], 0.0))
        tabs.append(jnp.where(row >= d, pi[d - 1], 0.0))
    tabs.append(jnp.concatenate(pr, axis=0))
    tabs.append(jnp.concatenate(pi, axis=0))
    fwd = jnp.stack(tabs)
    sign = jnp.array([1.0, -1.0] * 4, F32)[:, None, None]
    return fwd, fwd[:, ::-1, :] * sign


def _block_diag_b(bb):
    strip = bb.transpose(2, 0, 1).reshape(SSM_H, N_STATE)
    rows = lax.broadcasted_iota(jnp.int32, (SSM_W, N_STATE), 0) // SSM_H
    cols = lax.broadcasted_iota(jnp.int32, (SSM_W, N_STATE), 1) // SSM_P
    return jnp.where(rows == cols, jnp.tile(strip, (SSM_G, 1)), 0.0).astype(BF16)


def _block_diag_c(cc):
    strip = cc.transpose(0, 2, 1).reshape(N_STATE, SSM_H)
    rows = lax.broadcasted_iota(jnp.int32, (N_STATE, SSM_W), 0) // SSM_P
    cols = lax.broadcasted_iota(jnp.int32, (N_STATE, SSM_W), 1) // SSM_H
    return jnp.where(rows == cols, jnp.tile(strip, (1, SSM_G)), 0.0).astype(BF16)


SMALL_SHAPES = {
    "g_mix": (D_MODEL,), "a_re": (SSM_G, SSM_P), "a_im": (SSM_G, SSM_P), "log_dt": (SSM_G,),
    "b_re": (SSM_G, SSM_P, SSM_H), "b_im": (SSM_G, SSM_P, SSM_H),
    "c_re": (SSM_G, SSM_H, SSM_P), "c_im": (SSM_G, SSM_H, SSM_P),
    "d_skip": (SSM_W,), "b_glu": (SSM_W,), "g_sgu": (SGU_W,), "w_s": (SGU_G, CHUNK, CHUNK),
    "b_s": (SGU_G, CHUNK), "g_ffn": (D_MODEL,), "conv_b": (2 * D_FF,), "g_final": (D_MODEL,),
}
PACK_ITEMS = [("loss", (1,))] + [(n, SMALL_SHAPES[n]) for n in SMALL] + [("conv_w", (3, 2 * D_FF))]
TILE = SUBLANE * LANE


def _item_rows(shape):
    return -(-math.prod(shape) // TILE) * SUBLANE


PACK_ROWS = -(-sum(_item_rows(s) for _, s in PACK_ITEMS) // (2 * SUBLANE)) * (2 * SUBLANE)


def _pack(values):
    parts, used = [], 0
    for name, shape in PACK_ITEMS:
        size, rows = math.prod(shape), _item_rows(shape)
        if name in values:
            flat = values[name].astype(F32).reshape(size)
            if rows * LANE > size:
                flat = jnp.pad(flat, (0, rows * LANE - size))
            parts.append(flat.reshape(rows, LANE))
        else:
            parts.append(jnp.zeros((rows, LANE), F32))
        used += rows
    if PACK_ROWS > used:
        parts.append(jnp.zeros((PACK_ROWS - used, LANE), F32))
    return jnp.concatenate(parts, axis=0)


def _unpack(pack):
    out, off = {}, 0
    for name, shape in PACK_ITEMS:
        rows = _item_rows(shape)
        out[name] = pack[off:off + rows].reshape(rows * LANE)[:math.prod(shape)].reshape(shape)
        off += rows
    return out


PLACE_ROWS = {"w_in": 256, "w_up": 256, "w_down": 352, "w_out": 256, "w_proj_a": 256,
              "w_proj_b": 256, "w_glu": 128}


def kernel(x, g_mix, w_in, a_re, a_im, log_dt, b_re, b_im, c_re, c_im, d_skip, w_glu, b_glu, w_proj_a, g_sgu, w_s, b_s, w_proj_b, w_out, g_ffn, w_up, conv_w, conv_b, w_down, g_final, loss_target, m_g_mix, m_w_in, m_a_re, m_a_im, m_log_dt, m_b_re, m_b_im, m_c_re, m_c_im, m_d_skip, m_w_glu, m_b_glu, m_w_proj_a, m_g_sgu, m_w_s, m_b_s, m_w_proj_b, m_w_out, m_g_ffn, m_w_up, m_conv_w, m_conv_b, m_w_down, m_g_final, v_g_mix, v_w_in, v_a_re, v_a_im, v_log_dt, v_b_re, v_b_im, v_c_re, v_c_im, v_d_skip, v_w_glu, v_b_glu, v_w_proj_a, v_g_sgu, v_w_s, v_b_s, v_w_proj_b, v_w_out, v_g_ffn, v_w_up, v_conv_w, v_conv_b, v_w_down, v_g_final):
    given = dict(locals())
    w = {n: given[n] for n in WEIGHTS}
    m = {n: given["m_" + n] for n in WEIGHTS}
    v = {n: given["v_" + n] for n in WEIGHTS}

    def shard2d(a):
        return a.reshape(a.shape[-2], a.shape[-1])

    chip = 2 * lax.axis_index("x") + lax.axis_index("y")
    where = jnp.stack([chip, lax.axis_index("c")]).astype(jnp.int32)
    xs, target = x[0], loss_target[0]
    small = {n: w[n].reshape(SMALL_SHAPES[n]) for n in SMALL}

    (abr, abi, bbr, bbi), disc_vjp = jax.vjp(_ssm_discretize, small["a_re"], small["a_im"],
                                             small["log_dt"], small["b_re"], small["b_im"])
    tab_f, tab_r = _scan_tables(abr, abi)
    bre = _block_diag_b(bbr)
    bim = _block_diag_b(bbi)
    cre = _block_diag_c(small["c_re"])
    cim = _block_diag_c(small["c_im"])
    tril = jnp.tril(jnp.ones((CHUNK, CHUNK), dtype=bool))
    ws = jnp.where(tril[None], small["w_s"], 0.0)
    ws_st = ws.reshape(SGU_G // 2, 2 * CHUNK, CHUNK).astype(BF16)
    wst_st = ws.transpose(0, 2, 1).reshape(SGU_G // 2, 2 * CHUNK, CHUNK).astype(BF16)
    bmat = jnp.repeat(small["b_s"].T, SGU_D, axis=1)
    g_mix2 = small["g_mix"].reshape(1, D_MODEL)
    g_ffn2 = small["g_ffn"].reshape(1, D_MODEL)
    g_final2 = small["g_final"].reshape(1, D_MODEL)
    g_sgu2 = small["g_sgu"].reshape(1, SGU_W)
    d_skip2 = small["d_skip"].reshape(1, SSM_W)
    b_glu2 = small["b_glu"].reshape(1, SSM_W)
    conv_b2 = small["conv_b"].reshape(1, 2 * D_FF)

    gat = {n: _place_shard(shard2d(w[n]), where, "place_" + n, BF16, PLACE_ROWS[n]) for n in BIG}
    gat["conv_w"] = _place_shard(shard2d(w["conv_w"]), where, "place_conv_w", F32, 3)
    (gat["w_in"],), = _comm("gather_in_ici", [_job_gather_ici([gat["w_in"]])])
    (gat["w_in"],), = _comm("gather_in_sibling", [_job_gather_sibling([gat["w_in"]])])
    early = ["w_glu", "w_proj_a", "w_proj_b", "w_out", "w_down", "conv_w"]

    (p, h1, bur, bui), (got,) = _fwd_in(
        xs, g_mix2, gat["w_in"], bre, bim,
        [_job_gather_ici([gat[n] for n in early], whole=(5,))])
    gat.update(zip(early, got))
    (str_, sti), (got_e, (gat["w_up"],)) = _scan_fwd(
        bur, bui, tab_f,
        [_job_gather_sibling([gat[n] for n in early[:5]]), _job_gather_ici([gat["w_up"]])])
    gat.update(zip(early[:5], got_e))
    w_glu_f = gat["w_glu"].reshape(SSM_W, SSM_W)
    w_out_f = gat["w_out"].reshape(D_MODEL, D_MODEL)
    conv_w_f = gat["conv_w"].transpose(1, 0, 2).reshape(3, 2 * D_FF)
    (x2, y0, z, mixed, ya, yb), ((gat["w_up"],),) = _fwd_mix(
        xs, p, str_, sti, cre, cim, d_skip2, w_glu_f, b_glu2, gat["w_proj_a"], g_sgu2, ws_st, bmat,
        gat["w_proj_b"], w_out_f, [_job_gather_sibling([gat["w_up"]])])
    w_down_f = gat["w_down"].reshape(D_FF, D_MODEL)
    up, act, f, h2, dx3, sm_ffn = _fwd_ffn(x2, target, g_ffn2, gat["w_up"], conv_w_f, conv_b2,
                                           w_down_f, g_final2)

    def leg1_done(names, got):
        return [_add_sibling(part[n], s, where, "add_sibling_" + n) for n, s in zip(names, got)]

    def leg2_done(names, sums, got):
        return [_add_chips(s, o, where, "add_chips_" + n) for n, s, o in zip(names, sums, got)]

    part, red = {}, {}
    part["w_down"] = _dw_rows(f, dx3, "dw_down", D_FF // 2, 2 * TK).reshape(
        N_CHIP, D_FF // N_CHIP, D_MODEL)
    (dx2, dup, sm_conv, sm_gffn), (got,) = _bwd_ffn(
        dx3, up, act, x2, g_ffn2, gat["w_up"], conv_w_f, w_down_f,
        [_job_sibling_halves([part["w_down"]])])
    sum_down = leg1_done(["w_down"], got)
    part["w_up"], (got,) = _dw_shards(h2, dup, "dw_up", 4 * TK, [_job_to_owner(sum_down)])
    red_down = leg2_done(["w_down"], sum_down, got)
    ((dsr, dsi, du_part, drest, mrg, dya, dyb, yap, dz, y1, sgu, dy0, sm_mix, dbm, dws),
     (got, (red["w_down"],))) = _bwd_mix(
        dx2, p, y0, z, mixed, ya, yb, w_out_f, gat["w_proj_a"], gat["w_proj_b"], w_glu_f, cre, cim,
        ws_st, wst_st, d_skip2, g_sgu2,
        [_job_sibling_halves([part["w_up"]]), _job_swap_halves(red_down)])
    sum_up = leg1_done(["w_up"], got)
    (lam_r, lam_i, dar8, dai8), (got,) = _scan_bwd(dsr, dsi, str_, sti, tab_r, [_job_to_owner(sum_up)])
    red_up = leg2_done(["w_up"], sum_up, got)
    mix4 = ["w_out", "w_proj_a", "w_proj_b", "w_glu"]
    part["w_out"] = _dw_cols(mrg, dx2, "dw_out", D_MODEL // 2, False)[0].reshape(
        N_CHIP, D_MODEL // N_CHIP, D_MODEL)
    part["w_proj_a"] = _dw_cols(yap, dya, "dw_proj_a", D_MODEL // N_CHIP, True)[0]
    part["w_proj_b"] = _dw_cols(sgu, dyb, "dw_proj_b", D_MODEL // N_CHIP, True)[0]
    part["w_glu"] = _dw_cols(y1, dz, "dw_glu", SSM_W, False)[0].reshape(
        N_CHIP, SSM_W // N_CHIP, SSM_W)
    (grad_x, dp, sm_gmix), (got, (red["w_up"],)) = _bwd_in(
        lam_r, lam_i, du_part, drest, xs, dx2, g_mix2, gat["w_in"], bre, bim,
        [_job_sibling_halves([part[n] for n in mix4]), _job_swap_halves(red_up)])
    sums_m = leg1_done(mix4, got)
    part["w_in"], (got,) = _dw_cols(h1, dp, "dw_in", IN_COLS // N_CHIP, True, [_job_to_owner(sums_m)])
    red_m = leg2_done(mix4, sums_m, got)
    (dbd_r, dbd_i), (got, done_m) = _dw_pair(
        p, SSM_W, lam_r, lam_i, "db_bar",
        [_job_sibling_halves([part["w_in"]]), _job_swap_halves(red_m)])
    red.update(zip(mix4, done_m))
    sum_in = leg1_done(["w_in"], got)
    (dcd_r, dcd_i), (got,) = _dw_pair(dy0, SSM_W, str_, sti, "dc", [_job_to_owner(sum_in)])
    red_in = leg2_done(["w_in"], sum_in, got)
    (red["w_in"],), = _comm("swap_w_in", [_job_swap_halves(red_in)])

    def pick_c(slabs):
        two = LANE // SSM_P
        return jnp.einsum("jshsp->jshp", slabs.reshape(SSM_G // two, two, SSM_H, two, SSM_P)
                          ).reshape(SSM_G, SSM_H, SSM_P)

    def pick_b(slabs):
        return pick_c(slabs).transpose(0, 2, 1)

    dabr = jnp.sum(dar8, axis=0).reshape(SSM_G, SSM_P)
    dabi = jnp.sum(dai8, axis=0).reshape(SSM_G, SSM_P)
    d_a_re, d_a_im, d_log_dt, d_b_re, d_b_im = disc_vjp((dabr, dabi, pick_b(dbd_r), pick_b(dbd_i)))
    gsmall = {
        "g_mix": sm_gmix[0], "a_re": d_a_re, "a_im": d_a_im, "log_dt": d_log_dt,
        "b_re": d_b_re, "b_im": d_b_im, "c_re": pick_c(dcd_r), "c_im": -pick_c(dcd_i),
        "d_skip": sm_mix[0], "b_glu": sm_mix[1], "g_sgu": sm_mix[2],
        "w_s": jnp.where(tril[None], dws, 0.0),
        "b_s": dbm.reshape(CHUNK, SGU_G, SGU_D).sum(-1).T,
        "g_ffn": sm_gffn[0], "conv_b": sm_conv[3], "g_final": sm_ffn[0],
        "conv_w": sm_conv[0:3], "loss": sm_ffn[1, 0:1],
    }

    total_pack = _small_allreduce(_pack(gsmall))
    total = _unpack(total_pack)
    grads = dict(red)
    cs = 2 * D_FF // N_CHIP
    grads["conv_w"] = lax.dynamic_slice(total["conv_w"], (0, chip * cs), (3, cs))
    delta, new_m, new_v = {}, {}, {}
    for n in BIG + ("conv_w",):
        delta[n], new_m[n], new_v[n] = _adamw(shard2d(w[n]), grads[n], shard2d(m[n]), shard2d(v[n]),
                                              "adamw_" + n, PLACE_ROWS.get(n, 3))
    for n in SMALL:
        grads[n] = total[n].reshape(w[n].shape)
    ud, um, uv = _adamw_many(*[[d[n] for n in SMALL] for d in (w, grads, m, v)], "adamw_small")
    for i, n in enumerate(SMALL):
        delta[n], new_m[n], new_v[n] = ud[i], um[i], uv[i]

    def like(d):
        return [d[n].reshape(w[n].shape) for n in WEIGHTS]

    return (total["loss"].reshape(()), grad_x.reshape(x.shape), *like(grads), *like(delta),
            *like(new_m), *like(new_v))
```

```python
import math

import jax
import jax.numpy as jnp
from jax import lax
from jax.experimental import pallas as pl
from jax.experimental.pallas import tpu as pltpu

F32 = jnp.float32
BF16 = jnp.bfloat16
MESH = pl.DeviceIdType.MESH

D_MODEL = 1024
SSM_W = 512
SSM_G = 32
SSM_H = 16
SSM_P = 64
N_STATE = SSM_G * SSM_P
SGU_W = 512
SGU_G = 8
SGU_D = 64
CHUNK = 128
D_FF = 2816
IN_COLS = 3584
EPS = 1e-6
N_CHIP = 4

ADAM_LR = 0.001
ADAM_B1 = 0.9
ADAM_B2 = 0.999
ADAM_EPS = 1e-08
ADAM_WD = 0.01
ADAM_STEP = 10

SUBLANE = 8
LANE = 128
VMEM_LIMIT = 56 * 1024 * 1024
TB = 256
TK = 512
SCAN_LANES = 256
SCAN_UNROLL = 4
HALO = SUBLANE

BIG = ("w_in", "w_up", "w_down", "w_out", "w_proj_a", "w_proj_b", "w_glu")
SMALL = ("g_mix", "a_re", "a_im", "log_dt", "b_re", "b_im", "c_re", "c_im", "d_skip", "b_glu",
         "g_sgu", "w_s", "b_s", "g_ffn", "conv_b", "g_final")
WEIGHTS = ("g_mix", "w_in", "a_re", "a_im", "log_dt", "b_re", "b_im", "c_re", "c_im", "d_skip",
           "w_glu", "b_glu", "w_proj_a", "g_sgu", "w_s", "b_s", "w_proj_b", "w_out", "g_ffn",
           "w_up", "conv_w", "conv_b", "w_down", "g_final")

ANY = pl.BlockSpec(memory_space=pl.ANY)


def _params(n_grid):
    return pltpu.CompilerParams(dimension_semantics=("arbitrary",) * n_grid if n_grid else None,
                                vmem_limit_bytes=VMEM_LIMIT)


def _whole():
    return pl.BlockSpec(memory_space=pltpu.VMEM)


def _rows(tb, ncol):
    return pl.BlockSpec((tb, ncol), lambda i: (i, 0))


def _acc(nrow, ncol):
    return pl.BlockSpec((nrow, ncol), lambda i: (0, 0))


def _dot(a, b):
    return jnp.dot(a.astype(BF16), b.astype(BF16), preferred_element_type=F32)


def _dot_nt(a, b):
    return lax.dot_general(a.astype(BF16), b.astype(BF16), (((1,), (1,)), ((), ())),
                           preferred_element_type=F32)


def _sigmoid(v):
    return 0.5 * jnp.tanh(0.5 * v) + 0.5


_GELU_C = math.sqrt(2.0 / math.pi)


def _gelu(v):
    return 0.5 * v * (1.0 + jnp.tanh(_GELU_C * (v + 0.044715 * v * v * v)))


def _gelu_and_grad(v):
    v2 = v * v
    t = jnp.tanh(_GELU_C * v * (1.0 + 0.044715 * v2))
    half = 0.5 * (1.0 + t)
    return v * half, half + 0.5 * v * (1.0 - t * t) * _GELU_C * (1.0 + 3.0 * 0.044715 * v2)


def _rms_stats(v):
    r = lax.rsqrt(jnp.mean(v * v, axis=-1, keepdims=True) + EPS)
    return r, v * r


def _rms_bwd(dxh, xh, r):
    return r * (dxh - xh * jnp.mean(dxh * xh, axis=-1, keepdims=True))


def _place():
    x, y, c = lax.axis_index("x"), lax.axis_index("y"), lax.axis_index("c")
    chips = [(1 - x, y), (x, 1 - y), (1 - x, 1 - y)]
    return x, y, c, chips


def _chip_index(chip):
    return 2 * chip[0] + chip[1]


def _remote(src, dst, send_sem, recv_sem, device):
    return pltpu.make_async_remote_copy(src_ref=src, dst_ref=dst, send_sem=send_sem,
                                        recv_sem=recv_sem, device_id=device, device_id_type=MESH)


def _half(ref_rows, c):
    hr = ref_rows // 2
    return pl.ds(pl.multiple_of(c * hr, SUBLANE), hr)


class _Job:
    def __init__(self, start, finish, n_sem, ins=(), inouts=(), outs=()):
        self.start, self.finish, self.n_sem = start, finish, n_sem
        self.ins, self.inouts, self.outs = list(ins), list(inouts), list(outs)


def _job_gather_ici(bufs, whole=()):
    n = len(bufs)

    def copies(io):
        x, y, c, chips = _place()
        k_me = 2 * x + y
        out = []
        for t in range(n):
            for j, ch in enumerate(chips):
                if t in whole:
                    src, land = io[t].at[k_me], io[t].at[_chip_index(ch)]
                else:
                    mine = _half(bufs[t].shape[1], c)
                    src, land = io[t].at[k_me, mine, :], io[t].at[_chip_index(ch), mine, :]
                out.append((src, land, 3 * t + j, (*ch, c)))
        return out

    def start(ins, io, outs, ssem, rsem):
        for src, _, i, dev in copies(io):
            _remote(src, src, ssem(i), rsem(i), dev).start()

    def finish(ins, io, outs, ssem, rsem):
        cps = copies(io)
        for _, land, i, dev in cps:
            _remote(land, land, ssem(i), rsem(i), dev).wait_recv()
        for src, _, i, dev in cps:
            _remote(src, src, ssem(i), rsem(i), dev).wait_send()

    return _Job(start, finish, 3 * n, inouts=bufs)


def _job_gather_sibling(bufs):
    n = len(bufs)

    def copies(io):
        x, y, c, chips = _place()
        out = []
        for t in range(n):
            rows = bufs[t].shape[1]
            for j, ch in enumerate(chips):
                k = _chip_index(ch)
                out.append((io[t].at[k, _half(rows, c), :], io[t].at[k, _half(rows, 1 - c), :],
                            3 * t + j, (x, y, 1 - c)))
        return out

    def start(ins, io, outs, ssem, rsem):
        for src, _, i, dev in copies(io):
            _remote(src, src, ssem(i), rsem(i), dev).start()

    def finish(ins, io, outs, ssem, rsem):
        cps = copies(io)
        for _, land, i, dev in cps:
            _remote(land, land, ssem(i), rsem(i), dev).wait_recv()
        for src, _, i, dev in cps:
            _remote(src, src, ssem(i), rsem(i), dev).wait_send()

    return _Job(start, finish, 3 * n, inouts=bufs)


def _job_sibling_halves(grads):
    n = len(grads)

    def build(ins, outs, ssem, rsem):
        x, y, c, _ = _place()
        return [_remote(ins[t].at[:, _half(grads[t].shape[1], 1 - c), :], outs[t], ssem(t), rsem(t),
                        (x, y, 1 - c)) for t in range(n)]

    def start(ins, io, outs, ssem, rsem):
        for cp in build(ins, outs, ssem, rsem):
            cp.start()

    def finish(ins, io, outs, ssem, rsem):
        for cp in build(ins, outs, ssem, rsem):
            cp.wait()

    return _Job(start, finish, n, ins=grads,
                outs=[jax.ShapeDtypeStruct((N_CHIP, g.shape[1] // 2, g.shape[2]), F32) for g in grads])


def _job_to_owner(sums):
    n = len(sums)

    def build(ins, outs, ssem, rsem):
        x, y, c, chips = _place()
        return [_remote(ins[t].at[_chip_index(ch)], outs[t].at[j], ssem(3 * t + j), rsem(3 * t + j),
                        (*ch, c)) for t in range(n) for j, ch in enumerate(chips)]

    def start(ins, io, outs, ssem, rsem):
        for cp in build(ins, outs, ssem, rsem):
            cp.start()

    def finish(ins, io, outs, ssem, rsem):
        for cp in build(ins, outs, ssem, rsem):
            cp.wait()

    return _Job(start, finish, 3 * n, ins=sums,
                outs=[jax.ShapeDtypeStruct((3,) + s.shape[1:], s.dtype) for s in sums])


def _job_swap_halves(bufs):
    n = len(bufs)

    def start(ins, io, outs, ssem, rsem):
        x, y, c, _ = _place()
        for t in range(n):
            mine = io[t].at[_half(bufs[t].shape[0], c), :]
            _remote(mine, mine, ssem(t), rsem(t), (x, y, 1 - c)).start()

    def finish(ins, io, outs, ssem, rsem):
        x, y, c, _ = _place()
        for t in range(n):
            theirs = io[t].at[_half(bufs[t].shape[0], 1 - c), :]
            _remote(theirs, theirs, ssem(t), rsem(t), (x, y, 1 - c)).wait_recv()
        for t in range(n):
            mine = io[t].at[_half(bufs[t].shape[0], c), :]
            _remote(mine, mine, ssem(t), rsem(t), (x, y, 1 - c)).wait_send()

    return _Job(start, finish, n, inouts=bufs)


def _call(body, name, grid, in_specs, out_specs, out_shape, args, jobs=(), scratch=()):
    n_in, n_out, n_scr = len(args), len(out_shape), len(scratch)
    job_in = [a for jb in jobs for a in jb.ins + jb.inouts]
    job_out = [s for jb in jobs
               for s in [jax.ShapeDtypeStruct(a.shape, a.dtype) for a in jb.inouts] + jb.outs]
    aliases, pos_in, pos_out = {}, n_in, n_out
    for jb in jobs:
        pos_in += len(jb.ins)
        for _ in jb.inouts:
            aliases[pos_in] = pos_out
            pos_in += 1
            pos_out += 1
        pos_out += len(jb.outs)
    n_sem = sum(jb.n_sem for jb in jobs)

    def wrapped(*refs):
        c_in = refs[:n_in]
        j_in = refs[n_in:n_in + len(job_in)]
        c_out = refs[n_in + len(job_in):n_in + len(job_in) + n_out]
        j_out = refs[n_in + len(job_in) + n_out:n_in + len(job_in) + n_out + len(job_out)]
        rest = refs[n_in + len(job_in) + n_out + len(job_out):]
        c_scr = rest[:n_scr]
        views, pi, po, ps = [], 0, 0, 0
        for jb in jobs:
            ins = j_in[pi:pi + len(jb.ins)]
            pi += len(jb.ins) + len(jb.inouts)
            io = j_out[po:po + len(jb.inouts)]
            new = j_out[po + len(jb.inouts):po + len(jb.inouts) + len(jb.outs)]
            po += len(jb.inouts) + len(jb.outs)
            send = (lambda i, o=ps: rest[n_scr].at[o + i])
            recv = (lambda i, o=ps: rest[n_scr + 1].at[o + i])
            ps += jb.n_sem
            views.append((ins, io, new, send, recv))

        def run(which):
            for jb, vw in zip(jobs, views):
                (jb.start if which == 0 else jb.finish)(*vw)

        if not grid:
            run(0)
            run(1)
            return
        if jobs:
            first = pl.program_id(0) == 0
            last = pl.program_id(0) == grid[0] - 1
            for d in range(1, len(grid)):
                first = jnp.logical_and(first, pl.program_id(d) == 0)
                last = jnp.logical_and(last, pl.program_id(d) == grid[d] - 1)
            pl.when(first)(lambda: run(0))
        body(*c_in, *c_out, *c_scr)
        if jobs:
            pl.when(last)(lambda: run(1))

    sems = [pltpu.SemaphoreType.DMA((n_sem,)), pltpu.SemaphoreType.DMA((n_sem,))] if jobs else []
    kwargs = dict(grid=grid) if grid else {}
    res = pl.pallas_call(
        wrapped, name=name, in_specs=list(in_specs) + [ANY] * len(job_in),
        out_specs=list(out_specs) + [ANY] * len(job_out),
        out_shape=list(out_shape) + job_out, scratch_shapes=list(scratch) + sems,
        input_output_aliases=aliases, compiler_params=_params(len(grid)), **kwargs,
    )(*args, *job_in)
    outs, pos, per_job = list(res[:n_out]), n_out, []
    for jb in jobs:
        k = len(jb.inouts) + len(jb.outs)
        per_job.append(list(res[pos:pos + k]))
        pos += k
    return outs, per_job


def _comm(name, jobs):
    return _call(None, name, (), [], [], [], [], jobs)[1]


def _fwd_in(x, g_mix, w_in, bre, bim, jobs=()):
    t_len = x.shape[0]
    cs = IN_COLS // N_CHIP

    def body(x_ref, g_ref, w_ref, bre_ref, bim_ref, p_ref, h_ref, bur_ref, bui_ref):
        xv = x_ref[...]
        r, xh = _rms_stats(xv)
        h = (xh * g_ref[...]).astype(BF16)
        h_ref[...] = h
        for k in range(N_CHIP):
            p_ref[:, k * cs:(k + 1) * cs] = jnp.dot(h, w_ref[k],
                                                    preferred_element_type=F32).astype(BF16)
        u = p_ref[:, 0:SSM_W]
        bur_ref[...] = jnp.dot(u, bre_ref[...], preferred_element_type=F32).astype(BF16)
        bui_ref[...] = jnp.dot(u, bim_ref[...], preferred_element_type=F32).astype(BF16)

    return _call(
        body, "fwd_in", (t_len // TB,),
        [_rows(TB, D_MODEL), _whole(), _whole(), _whole(), _whole()],
        [_rows(TB, IN_COLS), _rows(TB, D_MODEL), _rows(TB, N_STATE), _rows(TB, N_STATE)],
        [jax.ShapeDtypeStruct((t_len, IN_COLS), BF16), jax.ShapeDtypeStruct((t_len, D_MODEL), BF16),
         jax.ShapeDtypeStruct((t_len, N_STATE), BF16), jax.ShapeDtypeStruct((t_len, N_STATE), BF16)],
        [x, g_mix, w_in, bre, bim], jobs)


def _scan_local(xr, xi, tab, shifts):
    for q, s in enumerate(shifts):
        ar, ai = tab[2 * q], tab[2 * q + 1]
        rr = pltpu.roll(xr, s, 0)
        ri = pltpu.roll(xi, s, 0)
        xr, xi = xr + ar * rr - ai * ri, xi + ar * ri + ai * rr
    return xr, xi


def _scan_carry(xr, xi, tab, cr, ci):
    pr, pi = tab[6], tab[7]
    return xr + pr * cr - pi * ci, xi + pr * ci + pi * cr


BF16_TILE = 2 * SUBLANE


def _load_blocks(r_ref, i_ref, base):
    out = []
    for q in range(SCAN_UNROLL // 2):
        rows = pl.ds(pl.multiple_of(base + q * BF16_TILE, BF16_TILE), BF16_TILE)
        vr, vi = r_ref[rows, :].astype(F32), i_ref[rows, :].astype(F32)
        out += [(vr[:SUBLANE], vi[:SUBLANE]), (vr[SUBLANE:], vi[SUBLANE:])]
    return out


def _store_blocks(r_ref, i_ref, base, blocks):
    for q in range(SCAN_UNROLL // 2):
        rows = pl.ds(pl.multiple_of(base + q * BF16_TILE, BF16_TILE), BF16_TILE)
        r_ref[rows, :] = jnp.concatenate([blocks[2 * q][0], blocks[2 * q + 1][0]], 0).astype(r_ref.dtype)
        i_ref[rows, :] = jnp.concatenate([blocks[2 * q][1], blocks[2 * q + 1][1]], 0).astype(i_ref.dtype)


def _scan_fwd(bur, bui, tab, jobs=()):
    t_len = bur.shape[0]
    nblk = t_len // SUBLANE
    lb = SCAN_LANES

    def body(br_ref, bi_ref, tab_ref, sr_ref, si_ref):
        tab_v = [tab_ref[q] for q in range(8)]

        def step(k, carry):
            cr, ci = carry
            base = pl.multiple_of(k * SCAN_UNROLL * SUBLANE, SCAN_UNROLL * SUBLANE)
            local = [_scan_local(xr, xi, tab_v, (1, 2, 4))
                     for xr, xi in _load_blocks(br_ref, bi_ref, base)]
            done = []
            for xr, xi in local:
                xr, xi = _scan_carry(xr, xi, tab_v, cr, ci)
                done.append((xr, xi))
                cr, ci = xr[SUBLANE - 1:SUBLANE, :], xi[SUBLANE - 1:SUBLANE, :]
            _store_blocks(sr_ref, si_ref, base, done)
            return cr, ci

        zero = jnp.zeros((1, lb), F32)
        lax.fori_loop(0, nblk // SCAN_UNROLL, step, (zero, zero))

    col = pl.BlockSpec((t_len, lb), lambda j: (0, j))
    return _call(
        body, "scan_fwd", (N_STATE // lb,),
        [col, col, pl.BlockSpec((8, SUBLANE, lb), lambda j: (0, 0, j))], [col, col],
        [jax.ShapeDtypeStruct((t_len, N_STATE), BF16)] * 2, [bur, bui, tab], jobs)


def _sgu_mix(v, ws_ref, lane_lo):
    rows = []
    for c0 in range(0, v.shape[0], CHUNK):
        slabs = []
        for j in range(SGU_W // LANE):
            prod = jnp.dot(ws_ref[j], v[c0:c0 + CHUNK, j * LANE:(j + 1) * LANE].astype(BF16),
                           preferred_element_type=F32)
            slabs.append(jnp.where(lane_lo, prod[:CHUNK], prod[CHUNK:]))
        rows.append(jnp.concatenate(slabs, axis=1))
    return jnp.concatenate(rows, axis=0) if len(rows) > 1 else rows[0]


def _fwd_mix(x, p, str_, sti, cre, cim, d_skip, w_glu, b_glu, w_pa, g_sgu, ws_st, bmat, w_pb, w_out,
             jobs=()):
    t_len = x.shape[0]

    def body(x_ref, p_ref, sr_ref, si_ref, cre_ref, cim_ref, dsk_ref, wg_ref, bg_ref, wpa_ref,
             gs_ref, ws_ref, bm_ref, wpb_ref, wo_ref,
             x2_ref, y0_ref, z_ref, mx_ref, ya_ref, yb_ref):
        u = p_ref[:, 0:SSM_W].astype(F32)
        y0 =_dot(sr_ref[...], cre_ref[...]) - _dot(si_ref[...], cim_ref[...]) + dsk_ref[...] * u
        y0_ref[...] = y0.astype(BF16)
        y1 = _gelu(y0)
        z = _dot(y1, wg_ref[...]) + bg_ref[...]
        z_ref[...] = z.astype(BF16)
        ya_pre = (y1 * _sigmoid(z)).astype(BF16)
        ya = jnp.concatenate([jnp.dot(ya_pre, wpa_ref[k], preferred_element_type=F32)
                              for k in range(N_CHIP)], axis=1)
        ya_ref[...] = ya.astype(BF16)

        uvg = _gelu(p_ref[:, SSM_W:SSM_W + 2 * SGU_W].astype(F32))
        u2 = uvg[:, :SGU_W]
        _, vh = _rms_stats(uvg[:, SGU_W:])
        v3 = vh * gs_ref[...]
        lane_lo = lax.broadcasted_iota(jnp.int32, (CHUNK, LANE), 1) < SGU_D
        bias = jnp.concatenate([bm_ref[...]] * (TB // CHUNK), axis=0)
        mixed = _sgu_mix(v3, ws_ref, lane_lo) + bias
        mx_ref[...] = mixed.astype(BF16)
        sgu = (u2 * mixed).astype(BF16)
        yb = jnp.concatenate([jnp.dot(sgu, wpb_ref[k], preferred_element_type=F32)
                              for k in range(N_CHIP)], axis=1)
        yb_ref[...] = yb.astype(BF16)

        lg0 = SSM_W + 2 * SGU_W
        ga = _sigmoid(p_ref[:, lg0:lg0 + D_MODEL].astype(F32))
        gb = _sigmoid(p_ref[:, lg0 + D_MODEL:lg0 + 2 * D_MODEL].astype(F32))
        mrg = ga * ya + gb * yb
        x2_ref[...] = x_ref[...] + _dot(mrg, wo_ref[...])

    return _call(
        body, "fwd_mix", (t_len // TB,),
        [_rows(TB, D_MODEL), _rows(TB, IN_COLS), _rows(TB, N_STATE), _rows(TB, N_STATE)]
        + [_whole()] * 11,
        [_rows(TB, D_MODEL), _rows(TB, SSM_W), _rows(TB, SSM_W), _rows(TB, SGU_W),
         _rows(TB, D_MODEL), _rows(TB, D_MODEL)],
        [jax.ShapeDtypeStruct((t_len, D_MODEL), F32), jax.ShapeDtypeStruct((t_len, SSM_W), BF16),
         jax.ShapeDtypeStruct((t_len, SSM_W), BF16), jax.ShapeDtypeStruct((t_len, SGU_W), BF16),
         jax.ShapeDtypeStruct((t_len, D_MODEL), BF16), jax.ShapeDtypeStruct((t_len, D_MODEL), BF16)],
        [x, p, str_, sti, cre, cim, d_skip, w_glu, b_glu, w_pa, g_sgu, ws_st, bmat, w_pb, w_out], jobs)


def _conv_taps(v, cw_ref, c0, width):
    w0 = cw_ref[0:1, c0:c0 + width]
    w1 = cw_ref[1:2, c0:c0 + width]
    w2 = cw_ref[2:3, c0:c0 + width]
    return w0 * pltpu.roll(v, 2, 0) + w1 * pltpu.roll(v, 1, 0) + w2 * v


def _fwd_ffn(x2, target, g_ffn, w_up, conv_w, conv_b, w_down, g_final):
    t_len = x2.shape[0]
    half = D_FF // 2
    blocks_per_halo = TB // HALO

    def body(x2_ref, xp_ref, tg_ref, gf_ref, wu_ref, cw_ref, cb_ref, wd_ref, gl_ref,
             up_ref, act_ref, f_ref, h2_ref, dx3_ref, sm_ref):
        i = pl.program_id(0)
        xe = jnp.concatenate([xp_ref[...] * jnp.where(i == 0, 0.0, 1.0), x2_ref[...]], axis=0)
        _, xh = _rms_stats(xe)
        h2 = (xh * gf_ref[...]).astype(BF16)
        h2_ref[...] = h2[HALO:]
        acc = jnp.zeros((TB, D_MODEL), F32)
        for hc in range(2):
            ca = hc * half
            cb = D_FF + hc * half
            ua = jnp.dot(h2, wu_ref[hc], preferred_element_type=F32)
            ub = jnp.dot(h2, wu_ref[2 + hc], preferred_element_type=F32)
            up_ref[:, ca:ca + half] = ua[HALO:].astype(BF16)
            up_ref[:, cb:cb + half] = ub[HALO:].astype(BF16)
            ac = _conv_taps(ua, cw_ref, ca, half)[HALO:] + cb_ref[:, ca:ca + half]
            bc = _conv_taps(ub, cw_ref, cb, half)[HALO:] + cb_ref[:, cb:cb + half]
            act_ref[:, ca:ca + half] = ac.astype(BF16)
            act_ref[:, cb:cb + half] = bc.astype(BF16)
            f = (ac * _sigmoid(ac) * bc).astype(BF16)
            f_ref[:, ca:ca + half] = f
            acc = acc + jnp.dot(f, wd_ref[ca:ca + half, :], preferred_element_type=F32)
        x3 = x2_ref[...] + acc
        r3, xh3 = _rms_stats(x3)
        err = xh3 * gl_ref[...] - tg_ref[...]
        dout = err * (1.0 / D_MODEL)
        dx3_ref[...] = _rms_bwd(dout * gl_ref[...], xh3, r3)
        dgl = jnp.sum(dout * xh3, axis=0, keepdims=True)
        loss = 0.5 * jnp.sum(jnp.mean(err * err, axis=-1, keepdims=True), axis=0, keepdims=True)
        upd = jnp.concatenate([dgl, jnp.broadcast_to(loss, (1, D_MODEL)),
                               jnp.zeros((SUBLANE - 2, D_MODEL), F32)], axis=0)

        @pl.when(i == 0)
        def _():
            sm_ref[...] = upd

        @pl.when(i > 0)
        def _():
            sm_ref[...] += upd

    prev = pl.BlockSpec((HALO, D_MODEL), lambda i: (jnp.maximum(i * blocks_per_halo - 1, 0), 0))
    return _call(
        body, "fwd_ffn", (t_len // TB,),
        [_rows(TB, D_MODEL), prev, _rows(TB, D_MODEL)] + [_whole()] * 6,
        [_rows(TB, 2 * D_FF), _rows(TB, 2 * D_FF), _rows(TB, D_FF), _rows(TB, D_MODEL),
         _rows(TB, D_MODEL), _acc(SUBLANE, D_MODEL)],
        [jax.ShapeDtypeStruct((t_len, 2 * D_FF), BF16), jax.ShapeDtypeStruct((t_len, 2 * D_FF), BF16),
         jax.ShapeDtypeStruct((t_len, D_FF), BF16), jax.ShapeDtypeStruct((t_len, D_MODEL), BF16),
         jax.ShapeDtypeStruct((t_len, D_MODEL), F32), jax.ShapeDtypeStruct((SUBLANE, D_MODEL), F32)],
        [x2, x2, target, g_ffn, w_up, conv_w, conv_b, w_down, g_final])[0]


def _bwd_ffn(dx3, up, act, x2, g_ffn, w_up, conv_w, w_down, jobs=()):
    t_len = x2.shape[0]
    half = D_FF // 2
    nblk = t_len // TB
    halo_b = 2 * HALO
    n_e = TB + HALO

    def body(dx_ref, dxn_ref, up_ref, act_ref, actn_ref, x2_ref, gf_ref, wu_ref, cw_ref,
             wd_ref, dx2_ref, dup_ref, smw_ref, smg_ref):
        i = pl.program_id(0)
        keep_last = jnp.where(i == nblk - 1, 0.0, 1.0)
        dxe = jnp.concatenate([dx_ref[...], dxn_ref[...] * keep_last], axis=0).astype(BF16)
        dh2 = jnp.zeros((TB, D_MODEL), F32)
        zpad = jnp.zeros((1, half), F32)
        for hc in range(2):
            ca = hc * half
            cb = D_FF + hc * half
            ac = jnp.concatenate([act_ref[:, ca:ca + half].astype(F32),
                                  actn_ref[:, ca:ca + half].astype(F32)[:HALO]], axis=0)
            bc = jnp.concatenate([act_ref[:, cb:cb + half].astype(F32),
                                  actn_ref[:, cb:cb + half].astype(F32)[:HALO]], axis=0)
            wa = [cw_ref[k:k + 1, ca:ca + half] for k in range(3)]
            wb = [cw_ref[k:k + 1, cb:cb + half] for k in range(3)]
            df = lax.dot_general(dxe, wd_ref[ca:ca + half, :], (((1,), (1,)), ((), ())),
                                 preferred_element_type=F32)
            sg = _sigmoid(ac)
            da = df * bc * sg * (1.0 + ac * (1.0 - sg))
            db = df * ac * sg
            da1, da2 = pltpu.roll(da, n_e - 1, 0), pltpu.roll(da, n_e - 2, 0)
            db1, db2 = pltpu.roll(db, n_e - 1, 0), pltpu.roll(db, n_e - 2, 0)
            dua = (wa[2] * da + wa[1] * da1 + wa[0] * da2)[:TB]
            dub = (wb[2] * db + wb[1] * db1 + wb[0] * db2)[:TB]
            dup_ref[:, ca:ca + half] = dua.astype(BF16)
            dup_ref[:, cb:cb + half] = dub.astype(BF16)
            dh2 = dh2 + _dot_nt(dua, wu_ref[hc]) + _dot_nt(dub, wu_ref[2 + hc])
            rows = []
            for u_, d0, d1, d2 in ((up_ref[:, ca:ca + half].astype(F32), da, da1, da2),
                                   (up_ref[:, cb:cb + half].astype(F32), db, db1, db2)):
                rows.append([jnp.sum(u_ * d2[:TB], axis=0, keepdims=True),
                             jnp.sum(u_ * d1[:TB], axis=0, keepdims=True),
                             jnp.sum(u_ * d0[:TB], axis=0, keepdims=True),
                             jnp.sum(d0[:TB], axis=0, keepdims=True)])
            for c0, rws in ((ca, rows[0]), (cb, rows[1])):
                upd = jnp.concatenate(rws + [zpad] * (SUBLANE - 4), axis=0)

                @pl.when(i == 0)
                def _(upd=upd, c0=c0):
                    smw_ref[:, c0:c0 + half] = upd

                @pl.when(i > 0)
                def _(upd=upd, c0=c0):
                    smw_ref[:, c0:c0 + half] += upd

        r2, xh2 = _rms_stats(x2_ref[...])
        dx2_ref[...] = dx_ref[...] + _rms_bwd(dh2 * gf_ref[...], xh2, r2)
        updg = jnp.concatenate([jnp.sum(dh2 * xh2, axis=0, keepdims=True),
                                jnp.zeros((SUBLANE - 1, D_MODEL), F32)], axis=0)

        @pl.when(i == 0)
        def _():
            smg_ref[...] = updg

        @pl.when(i > 0)
        def _():
            smg_ref[...] += updg

    nxt_d = pl.BlockSpec((HALO, D_MODEL),
                         lambda i: (jnp.minimum((i + 1) * (TB // HALO), t_len // HALO - 1), 0))
    nxt_a = pl.BlockSpec((halo_b, 2 * D_FF),
                         lambda i: (jnp.minimum((i + 1) * (TB // halo_b), t_len // halo_b - 1), 0))
    return _call(
        body, "bwd_ffn", (nblk,),
        [_rows(TB, D_MODEL), nxt_d, _rows(TB, 2 * D_FF), _rows(TB, 2 * D_FF), nxt_a,
         _rows(TB, D_MODEL)] + [_whole()] * 4,
        [_rows(TB, D_MODEL), _rows(TB, 2 * D_FF), _acc(SUBLANE, 2 * D_FF), _acc(SUBLANE, D_MODEL)],
        [jax.ShapeDtypeStruct((t_len, D_MODEL), F32), jax.ShapeDtypeStruct((t_len, 2 * D_FF), BF16),
         jax.ShapeDtypeStruct((SUBLANE, 2 * D_FF), F32), jax.ShapeDtypeStruct((SUBLANE, D_MODEL), F32)],
        [dx3, dx3, up, act, act, x2, g_ffn, w_up, conv_w, w_down], jobs)


def _bwd_mix(dx2, p, y0, z, mixed, ya, yb, w_out, w_pa, w_pb, w_glu, cre, cim, ws_st, wst_st,
             d_skip, g_sgu, jobs=()):
    t_len = dx2.shape[0]
    pc = D_MODEL // N_CHIP
    n_slab = SGU_W // LANE

    def body(dx_ref, p_ref, y0_ref, z_ref, mx_ref, ya_ref, yb_ref, wo_ref, wpa_ref, wpb_ref,
             wg_ref, cre_ref, cim_ref, ws_ref, wst_ref, dsk_ref, gs_ref,
             dsr_ref, dsi_ref, du_ref, drest_ref, mrg_ref, dya_ref, dyb_ref, yap_ref, dz_ref,
             y1_ref, sgu_ref, dy0_ref, sm_ref, dbm_ref, dws_ref):
        i = pl.program_id(0)
        first = i == 0
        lg0 = SSM_W + 2 * SGU_W
        dmrg = _dot_nt(dx_ref[...], wo_ref[...])
        ga = _sigmoid(p_ref[:, lg0:lg0 + D_MODEL].astype(F32))
        gb = _sigmoid(p_ref[:, lg0 + D_MODEL:lg0 + 2 * D_MODEL].astype(F32))
        yav = ya_ref[...].astype(F32)
        ybv = yb_ref[...].astype(F32)
        mrg_ref[...] = (ga * yav + gb * ybv).astype(BF16)
        drest_ref[:, 2 * SGU_W:2 * SGU_W + D_MODEL] = (dmrg * yav * ga * (1.0 - ga)).astype(BF16)
        drest_ref[:, 2 * SGU_W + D_MODEL:] = (dmrg * ybv * gb * (1.0 - gb)).astype(BF16)
        dya = (dmrg * ga).astype(BF16)
        dyb = (dmrg * gb).astype(BF16)
        dya_ref[...] = dya
        dyb_ref[...] = dyb

        y0v = y0_ref[...].astype(F32)
        y1, y1_grad = _gelu_and_grad(y0v)
        sz = _sigmoid(z_ref[...].astype(F32))
        y1_ref[...] = y1.astype(BF16)
        yap_ref[...] = (y1 * sz).astype(BF16)
        dyap = jnp.zeros((TB, SSM_W), F32)
        for k in range(N_CHIP):
            dyap = dyap + _dot_nt(dya[:, k * pc:(k + 1) * pc], wpa_ref[k])
        dz = dyap * y1 * sz * (1.0 - sz)
        dz_ref[...] = dz.astype(BF16)
        dy0 = (dyap * sz + _dot_nt(dz, wg_ref[...])) * y1_grad
        dy0_ref[...] = dy0.astype(BF16)
        u = p_ref[:, 0:SSM_W].astype(F32)
        du_ref[...] = dy0 * dsk_ref[...]
        dsr_ref[...] = _dot_nt(dy0, cre_ref[...]).astype(BF16)
        dsi_ref[...] = (-_dot_nt(dy0, cim_ref[...])).astype(BF16)

        uv = p_ref[:, SSM_W:lg0].astype(F32)
        uvg, gg = _gelu_and_grad(uv)
        u2 = uvg[:, :SGU_W]
        rv, vh = _rms_stats(uvg[:, SGU_W:])
        v3 = vh * gs_ref[...]
        mixed = mx_ref[...].astype(F32)
        dsgu = jnp.zeros((TB, SGU_W), F32)
        for k in range(N_CHIP):
            dsgu = dsgu + _dot_nt(dyb[:, k * pc:(k + 1) * pc], wpb_ref[k])
        sgu_ref[...] = (u2 * mixed).astype(BF16)
        du2 = dsgu * mixed
        dmix = dsgu * u2
        lane_lo = lax.broadcasted_iota(jnp.int32, (CHUNK, LANE), 1) < SGU_D
        dv3 = _sgu_mix(dmix, wst_ref, lane_lo)
        dbm = jnp.zeros((CHUNK, SGU_W), F32)
        for c0 in range(0, TB, CHUNK):
            dbm = dbm + dmix[c0:c0 + CHUNK]
        for j in range(n_slab):
            lo = jnp.zeros((CHUNK, CHUNK), F32)
            hi = jnp.zeros((CHUNK, CHUNK), F32)
            for c0 in range(0, TB, CHUNK):
                dsl = dmix[c0:c0 + CHUNK, j * LANE:(j + 1) * LANE]
                vsl = v3[c0:c0 + CHUNK, j * LANE:(j + 1) * LANE]
                lo = lo + _dot_nt(jnp.where(lane_lo, dsl, 0.0), vsl)
                hi = hi + _dot_nt(jnp.where(lane_lo, 0.0, dsl), vsl)

            @pl.when(first)
            def _(lo=lo, hi=hi, j=j):
                dws_ref[2 * j] = lo
                dws_ref[2 * j + 1] = hi

            @pl.when(jnp.logical_not(first))
            def _(lo=lo, hi=hi, j=j):
                dws_ref[2 * j] += lo
                dws_ref[2 * j + 1] += hi

        dv2 = _rms_bwd(dv3 * gs_ref[...], vh, rv)
        drest_ref[:, 0:SGU_W] = (du2 * gg[:, :SGU_W]).astype(BF16)
        drest_ref[:, SGU_W:2 * SGU_W] = (dv2 * gg[:, SGU_W:]).astype(BF16)

        upd = jnp.concatenate([jnp.sum(dy0 * u, axis=0, keepdims=True),
                               jnp.sum(dz, axis=0, keepdims=True),
                               jnp.sum(dv3 * vh, axis=0, keepdims=True),
                               jnp.zeros((SUBLANE - 3, SSM_W), F32)], axis=0)

        @pl.when(first)
        def _():
            sm_ref[...] = upd
            dbm_ref[...] = dbm

        @pl.when(jnp.logical_not(first))
        def _():
            sm_ref[...] += upd
            dbm_ref[...] += dbm

    rest = 2 * SGU_W + 2 * D_MODEL
    bf_d, bf_s = jax.ShapeDtypeStruct((t_len, D_MODEL), BF16), jax.ShapeDtypeStruct((t_len, SSM_W), BF16)
    return _call(
        body, "bwd_mix", (t_len // TB,),
        [_rows(TB, D_MODEL), _rows(TB, IN_COLS), _rows(TB, SSM_W), _rows(TB, SSM_W),
         _rows(TB, SGU_W), _rows(TB, D_MODEL), _rows(TB, D_MODEL)] + [_whole()] * 10,
        [_rows(TB, N_STATE), _rows(TB, N_STATE), _rows(TB, SSM_W), _rows(TB, rest),
         _rows(TB, D_MODEL), _rows(TB, D_MODEL), _rows(TB, D_MODEL), _rows(TB, SSM_W),
         _rows(TB, SSM_W), _rows(TB, SSM_W), _rows(TB, SGU_W), _rows(TB, SSM_W),
         _acc(SUBLANE, SSM_W), _acc(CHUNK, SGU_W),
         pl.BlockSpec((SGU_G, CHUNK, CHUNK), lambda i: (0, 0, 0))],
        [jax.ShapeDtypeStruct((t_len, N_STATE), BF16), jax.ShapeDtypeStruct((t_len, N_STATE), BF16),
         jax.ShapeDtypeStruct((t_len, SSM_W), F32), jax.ShapeDtypeStruct((t_len, rest), BF16),
         bf_d, bf_d, bf_d, bf_s, bf_s, bf_s, bf_s, bf_s,
         jax.ShapeDtypeStruct((SUBLANE, SSM_W), F32), jax.ShapeDtypeStruct((CHUNK, SGU_W), F32),
         jax.ShapeDtypeStruct((SGU_G, CHUNK, CHUNK), F32)],
        [dx2, p, y0, z, mixed, ya, yb, w_out, w_pa, w_pb, w_glu, cre, cim, ws_st, wst_st, d_skip,
         g_sgu], jobs)


def _scan_bwd(dsr, dsi, str_, sti, tab_rev, jobs=()):
    t_len = dsr.shape[0]
    nblk = t_len // SUBLANE
    lb = SCAN_LANES

    def body(dr_ref, di_ref, sr_ref, si_ref, tab_ref, lr_ref, li_ref, dar_ref, dai_ref):
        tab_v = [tab_ref[q] for q in range(8)]
        row0 = lax.broadcasted_iota(jnp.int32, (SUBLANE, lb), 0) == 0
        tile = BF16_TILE

        def step(k, carry):
            cr, ci, acr, aci = carry
            base = pl.multiple_of((nblk - (k + 1) * SCAN_UNROLL) * SUBLANE, SCAN_UNROLL * SUBLANE)
            state = _load_blocks(sr_ref, si_ref, base)
            before = pl.ds(pl.multiple_of(jnp.maximum(base - tile, 0), tile), tile)
            has_before = jnp.where(base > 0, 1.0, 0.0)
            prev = (sr_ref[before, :].astype(F32)[tile - 1:tile] * has_before,
                    si_ref[before, :].astype(F32)[tile - 1:tile] * has_before)
            local = [_scan_local(xr, xi, tab_v, (7, 6, 4))
                     for xr, xi in _load_blocks(dr_ref, di_ref, base)]
            lam = [None] * SCAN_UNROLL
            for b in reversed(range(SCAN_UNROLL)):
                xr, xi = _scan_carry(*local[b], tab_v, cr, ci)
                lam[b] = (xr, xi)
                cr, ci = xr[0:1, :], xi[0:1, :]
                pr, pi = prev if b == 0 else (state[b - 1][0][SUBLANE - 1:], state[b - 1][1][SUBLANE - 1:])
                s_r = jnp.where(row0, pr, pltpu.roll(state[b][0], 1, 0))
                s_i = jnp.where(row0, pi, pltpu.roll(state[b][1], 1, 0))
                acr = acr + xr * s_r + xi * s_i
                aci = aci + xi * s_r - xr * s_i
            _store_blocks(lr_ref, li_ref, base, lam)
            return cr, ci, acr, aci

        zero = jnp.zeros((1, lb), F32)
        zacc = jnp.zeros((SUBLANE, lb), F32)
        _, _, acr, aci = lax.fori_loop(0, nblk // SCAN_UNROLL, step, (zero, zero, zacc, zacc))
        dar_ref[...] = acr
        dai_ref[...] = aci

    col = pl.BlockSpec((t_len, lb), lambda j: (0, j))
    small = pl.BlockSpec((SUBLANE, lb), lambda j: (0, j))
    return _call(
        body, "scan_bwd", (N_STATE // lb,),
        [col, col, col, col, pl.BlockSpec((8, SUBLANE, lb), lambda j: (0, 0, j))],
        [col, col, small, small],
        [jax.ShapeDtypeStruct((t_len, N_STATE), BF16)] * 2
        + [jax.ShapeDtypeStruct((SUBLANE, N_STATE), F32)] * 2,
        [dsr, dsi, str_, sti, tab_rev], jobs)


def _bwd_in(lam_r, lam_i, du_part, drest, x, dx2, g_mix, w_in, bre, bim, jobs=()):
    t_len = x.shape[0]
    cs = IN_COLS // N_CHIP

    def body(lr_ref, li_ref, du_ref, dr_ref, x_ref, dx2_ref, g_ref, w_ref, bre_ref, bim_ref,
             gx_ref, dp_ref, sm_ref):
        i = pl.program_id(0)
        du = du_ref[...] + _dot_nt(lr_ref[...], bre_ref[...]) + _dot_nt(li_ref[...], bim_ref[...])
        dp_ref[:, 0:SSM_W] = du.astype(BF16)
        dp_ref[:, SSM_W:] = dr_ref[...]
        dh = jnp.zeros((TB, D_MODEL), F32)
        for k in range(N_CHIP):
            dh = dh + _dot_nt(dp_ref[:, k * cs:(k + 1) * cs], w_ref[k])
        r, xh = _rms_stats(x_ref[...])
        gx_ref[...] = dx2_ref[...] + _rms_bwd(dh * g_ref[...], xh, r)
        upd = jnp.concatenate([jnp.sum(dh * xh, axis=0, keepdims=True),
                               jnp.zeros((SUBLANE - 1, D_MODEL), F32)], axis=0)

        @pl.when(i == 0)
        def _():
            sm_ref[...] = upd

        @pl.when(i > 0)
        def _():
            sm_ref[...] += upd

    return _call(
        body, "bwd_in", (t_len // TB,),
        [_rows(TB, N_STATE), _rows(TB, N_STATE), _rows(TB, SSM_W), _rows(TB, IN_COLS - SSM_W),
         _rows(TB, D_MODEL), _rows(TB, D_MODEL)] + [_whole()] * 4,
        [_rows(TB, D_MODEL), _rows(TB, IN_COLS), _acc(SUBLANE, D_MODEL)],
        [jax.ShapeDtypeStruct((t_len, D_MODEL), F32), jax.ShapeDtypeStruct((t_len, IN_COLS), BF16),
         jax.ShapeDtypeStruct((SUBLANE, D_MODEL), F32)],
        [lam_r, lam_i, du_part, drest, x, dx2, g_mix, w_in, bre, bim], jobs)


def _matmul_tn(a, b, name, out_shape, grid_ij, a_blk, a_map, b_blk, b_map, o_blk, o_map, jobs=()):
    tk = a_blk[0]
    nk = a.shape[0] // tk
    assert nk * tk == a.shape[0] and nk > 0

    def body(a_ref, b_ref, o_ref, acc_ref):
        k = pl.program_id(2)

        @pl.when(k == 0)
        def _():
            acc_ref[...] = jnp.zeros_like(acc_ref)

        acc_ref[...] += lax.dot_general(a_ref[...].astype(BF16), b_ref[...].astype(BF16),
                                        (((0,), (0,)), ((), ())), preferred_element_type=F32)

        @pl.when(k == nk - 1)
        def _():
            o_ref[...] = acc_ref[...]

    outs, per_job = _call(
        body, name, (grid_ij[0], grid_ij[1], nk),
        [pl.BlockSpec(a_blk, a_map), pl.BlockSpec(b_blk, b_map)], [pl.BlockSpec(o_blk, o_map)],
        [jax.ShapeDtypeStruct(out_shape, F32)], [a, b], jobs,
        scratch=[pltpu.VMEM((a_blk[1], b_blk[1]), F32)])
    return outs[0], per_job


def _dw_shards(a, b, name, tk, jobs=()):
    m, n = a.shape[1], b.shape[1]
    tn = n // N_CHIP
    tk = min(tk, a.shape[0])
    return _matmul_tn(a, b, name, (N_CHIP, m, tn), (1, N_CHIP),
                      (tk, m), lambda i, j, k: (k, 0), (tk, tn), lambda i, j, k: (k, j),
                      (None, m, tn), lambda i, j, k: (j, 0, 0), jobs)


def _dw_rows(a, b, name, tm, tk):
    m, n = a.shape[1], b.shape[1]
    tk = min(tk, a.shape[0])
    return _matmul_tn(a, b, name, (m, n), (m // tm, 1),
                      (tk, tm), lambda i, j, k: (k, i), (tk, n), lambda i, j, k: (k, 0),
                      (tm, n), lambda i, j, k: (i, 0))[0]


def _dw_cols(a, b, name, tn, sharded, jobs=()):
    t_len, m = a.shape
    n = b.shape[1]

    def body(a_ref, b_ref, o_ref):
        o_ref[...] = lax.dot_general(a_ref[...].astype(BF16), b_ref[...].astype(BF16),
                                     (((0,), (0,)), ((), ())), preferred_element_type=F32)

    if sharded:
        o_spec, o_shape = pl.BlockSpec((None, m, tn), lambda j: (j, 0, 0)), (n // tn, m, tn)
    else:
        o_spec, o_shape = pl.BlockSpec((m, tn), lambda j: (0, j)), (m, n)
    outs, per_job = _call(body, name, (n // tn,),
                          [_whole(), pl.BlockSpec((t_len, tn), lambda j: (0, j))], [o_spec],
                          [jax.ShapeDtypeStruct(o_shape, F32)], [a, b], jobs)
    return outs[0], per_job


def _dw_pair(a, m, b1, b2, name, jobs=()):
    n = b1.shape[1]
    tk = min(TK, a.shape[0])
    nk = a.shape[0] // tk
    rows_per_slab = m * LANE // n

    def body(a_ref, b1_ref, b2_ref, o1_ref, o2_ref, acc1_ref, acc2_ref):
        k = pl.program_id(0)
        a_t = a_ref[...].astype(BF16).T
        for b_ref, o_ref, acc_ref in ((b1_ref, o1_ref, acc1_ref), (b2_ref, o2_ref, acc2_ref)):
            piece = jnp.dot(a_t, b_ref[...].astype(BF16), preferred_element_type=F32)

            @pl.when(k == 0)
            def _(piece=piece, acc_ref=acc_ref):
                acc_ref[...] = piece

            @pl.when(k > 0)
            def _(piece=piece, acc_ref=acc_ref):
                acc_ref[...] += piece

            @pl.when(k == nk - 1)
            def _(o_ref=o_ref, acc_ref=acc_ref):
                for j in range(n // LANE):
                    rows = slice(j * rows_per_slab, (j + 1) * rows_per_slab)
                    o_ref[rows, :] = acc_ref[rows, j * LANE:(j + 1) * LANE]

    tok = pl.BlockSpec((tk, n), lambda k: (k, 0))
    out = pl.BlockSpec((m, LANE), lambda k: (0, 0))
    return _call(body, name, (nk,),
                 [pl.BlockSpec((tk, m), lambda k: (k, 0)), tok, tok], [out, out],
                 [jax.ShapeDtypeStruct((m, LANE), F32)] * 2, [a, b1, b2], jobs,
                 scratch=[pltpu.VMEM((m, n), F32)] * 2)


def _prefetch_call(body, name, grid, scalars, in_specs, out_specs, out_shape, args):
    return pl.pallas_call(
        body, name=name,
        grid_spec=pltpu.PrefetchScalarGridSpec(num_scalar_prefetch=1, grid=grid, in_specs=in_specs,
                                               out_specs=out_specs),
        out_shape=out_shape, compiler_params=_params(len(grid)),
    )(scalars, *args)


def _place_shard(w, where, name, dtype, tr):
    rows, cols = w.shape

    def body(s_ref, w_ref, o_ref):
        o_ref[...] = w_ref[...].astype(dtype)

    return _prefetch_call(
        body, name, (rows // tr,), where,
        [pl.BlockSpec((tr, cols), lambda i, s: (i, 0))],
        pl.BlockSpec((None, tr, cols), lambda i, s: (s[0], i, 0)),
        jax.ShapeDtypeStruct((N_CHIP, rows, cols), dtype), [w])


def _add_sibling(g, got, where, name):
    _, rs, cs = g.shape
    hr = rs // 2

    def body(s_ref, g_ref, got_ref, o_ref):
        o_ref[...] = (g_ref[...] + got_ref[...]).astype(BF16)

    return _prefetch_call(
        body, name, (N_CHIP,), where,
        [pl.BlockSpec((None, hr, cs), lambda k, s: (k, s[1], 0)),
         pl.BlockSpec((None, hr, cs), lambda k, s: (k, 0, 0))],
        pl.BlockSpec((None, hr, cs), lambda k, s: (k, 0, 0)),
        jax.ShapeDtypeStruct((N_CHIP, hr, cs), BF16), [g, got])


def _add_chips(sums, got, where, name):
    _, hr, cs = sums.shape

    def body(s_ref, own_ref, got_ref, o_ref):
        o_ref[...] = ((own_ref[...].astype(F32) + got_ref[0].astype(F32))
                      + got_ref[1].astype(F32)) + got_ref[2].astype(F32)

    return _prefetch_call(
        body, name, (1,), where,
        [pl.BlockSpec((None, hr, cs), lambda i, s: (s[0], 0, 0)),
         pl.BlockSpec((3, hr, cs), lambda i, s: (0, 0, 0))],
        pl.BlockSpec((hr, cs), lambda i, s: (s[1], 0)),
        jax.ShapeDtypeStruct((2 * hr, cs), F32), [sums, got])


def _small_allreduce(pack):
    rows = pack.shape[0]
    half = rows // 2

    def body(in_ref, out_ref, sib_ref, slots_ref, s_a, r_a, s_b, r_b, s_c, r_c):
        x, y, c, chips = _place()
        k_me = 2 * x + y
        sib = (x, y, 1 - c)
        first = _remote(in_ref, sib_ref, s_a, r_a, sib)
        first.start()
        first.wait()
        mine = _half(rows, c)
        slots_ref[k_me] = in_ref[mine, :] + sib_ref[mine, :]
        cps = [_remote(slots_ref.at[k_me], slots_ref.at[k_me], s_b.at[j], r_b.at[j], (*ch, c))
               for j, ch in enumerate(chips)]
        for cp in cps:
            cp.start()
        for j, ch in enumerate(chips):
            slot = slots_ref.at[_chip_index(ch)]
            _remote(slot, slot, s_b.at[j], r_b.at[j], (*ch, c)).wait_recv()
        for cp in cps:
            cp.wait_send()
        out_ref[mine, :] = ((slots_ref[0] + slots_ref[1]) + slots_ref[2]) + slots_ref[3]
        last = _remote(out_ref.at[mine, :], out_ref.at[mine, :], s_c, r_c, sib)
        last.start()
        theirs = out_ref.at[_half(rows, 1 - c), :]
        _remote(theirs, theirs, s_c, r_c, sib).wait_recv()
        last.wait_send()

    return pl.pallas_call(
        body, name="small_allreduce", in_specs=[_whole()], out_specs=_whole(),
        out_shape=jax.ShapeDtypeStruct(pack.shape, F32),
        scratch_shapes=[pltpu.VMEM(pack.shape, F32), pltpu.VMEM((N_CHIP, half, LANE), F32),
                        pltpu.SemaphoreType.DMA, pltpu.SemaphoreType.DMA,
                        pltpu.SemaphoreType.DMA((3,)), pltpu.SemaphoreType.DMA((3,)),
                        pltpu.SemaphoreType.DMA, pltpu.SemaphoreType.DMA],
        compiler_params=_params(0),
    )(pack)


def _adamw_update(w_ref, g_ref, m_ref, v_ref, d_ref, mo_ref, vo_ref):
    gv = g_ref[...]
    mn = ADAM_B1 * m_ref[...] + (1.0 - ADAM_B1) * gv
    vn = ADAM_B2 * v_ref[...] + (1.0 - ADAM_B2) * (gv * gv)
    mo_ref[...] = mn
    vo_ref[...] = vn
    m_hat = mn / (1.0 - ADAM_B1 ** ADAM_STEP)
    v_hat = vn / (1.0 - ADAM_B2 ** ADAM_STEP)
    d_ref[...] = -ADAM_LR * (m_hat / (jnp.sqrt(v_hat) + ADAM_EPS) + ADAM_WD * w_ref[...])


def _adamw(w, g, m, v, name, tr):
    rows, cols = w.shape
    blk = _rows(tr, cols)
    return _call(_adamw_update, name, (rows // tr,), [blk] * 4, [blk] * 3,
                 [jax.ShapeDtypeStruct(w.shape, F32)] * 3, [w, g, m, v])[0]


def _adamw_many(ws, gs, ms, vs, name):
    n = len(ws)

    def body(*refs):
        for t in range(n):
            _adamw_update(*[refs[q * n + t] for q in range(7)])

    specs = [pl.BlockSpec(a.shape, lambda i, nd=a.ndim: (0,) * nd) for a in ws]
    outs = pl.pallas_call(
        body, name=name, grid=(1,), in_specs=specs * 4, out_specs=specs * 3,
        out_shape=[jax.ShapeDtypeStruct(a.shape, F32) for _ in range(3) for a in ws],
        compiler_params=_params(1),
    )(*ws, *gs, *ms, *vs)
    return outs[:n], outs[n:2 * n], outs[2 * n:]


def _ssm_discretize(a_re, a_im, log_dt, b_re, b_im):
    dt = jnp.exp(log_dt)[:, None]
    mag = jnp.exp(dt * a_re)
    abr = mag * jnp.cos(dt * a_im)
    abi = mag * jnp.sin(dt * a_im)
    den = a_re * a_re + a_im * a_im
    nr = abr - 1.0
    ni = abi
    f_re = (nr * a_re + ni * a_im) / den
    f_im = (ni * a_re - nr * a_im) / den
    bbr = f_re[..., None] * b_re - f_im[..., None] * b_im
    bbi = f_re[..., None] * b_im + f_im[..., None] * b_re
    return abr, abi, bbr, bbi


def _scan_tables(abr, abi):
    ar = abr.reshape(1, N_STATE)
    ai = abi.reshape(1, N_STATE)
    pr, pi = [ar], [ai]
    for _ in range(SUBLANE - 1):
        pr, pi = pr + [pr[-1] * ar - pi[-1] * ai], pi + [pr[-1] * ai + pi[-1] * ar]
    row = jnp.arange(SUBLANE)[:, None]
    tabs = []
    for d in (1, 2, 4):
        tabs.append(jnp.where(row >= d, pr[d - 1], 0.0))
        tabs.append(jnp.where(row >= d, pi[d - 1], 0.0))
    tabs.append(jnp.concatenate(pr, axis=0))
    tabs.append(jnp.concatenate(pi, axis=0))
    fwd = jnp.stack(tabs)
    sign = jnp.array([1.0, -1.0] * 4, F32)[:, None, None]
    return fwd, fwd[:, ::-1, :] * sign


def _block_diag_b(bb):
    strip = bb.transpose(2, 0, 1).reshape(SSM_H, N_STATE)
    rows = lax.broadcasted_iota(jnp.int32, (SSM_W, N_STATE), 0) // SSM_H
    cols = lax.broadcasted_iota(jnp.int32, (SSM_W, N_STATE), 1) // SSM_P
    return jnp.where(rows == cols, jnp.tile(strip, (SSM_G, 1)), 0.0).astype(BF16)


def _block_diag_c(cc):
    strip = cc.transpose(0, 2, 1).reshape(N_STATE, SSM_H)
    rows = lax.broadcasted_iota(jnp.int32, (N_STATE, SSM_W), 0) // SSM_P
    cols = lax.broadcasted_iota(jnp.int32, (N_STATE, SSM_W), 1) // SSM_H
    return jnp.where(rows == cols, jnp.tile(strip, (1, SSM_G)), 0.0).astype(BF16)


SMALL_SHAPES = {
    "g_mix": (D_MODEL,), "a_re": (SSM_G, SSM_P), "a_im": (SSM_G, SSM_P), "log_dt": (SSM_G,),
    "b_re": (SSM_G, SSM_P, SSM_H), "b_im": (SSM_G, SSM_P, SSM_H),
    "c_re": (SSM_G, SSM_H, SSM_P), "c_im": (SSM_G, SSM_H, SSM_P),
    "d_skip": (SSM_W,), "b_glu": (SSM_W,), "g_sgu": (SGU_W,), "w_s": (SGU_G, CHUNK, CHUNK),
    "b_s": (SGU_G, CHUNK), "g_ffn": (D_MODEL,), "conv_b": (2 * D_FF,), "g_final": (D_MODEL,),
}
PACK_ITEMS = [("loss", (1,))] + [(n, SMALL_SHAPES[n]) for n in SMALL] + [("conv_w", (3, 2 * D_FF))]
TILE = SUBLANE * LANE


def _item_rows(shape):
    return -(-math.prod(shape) // TILE) * SUBLANE


PACK_ROWS = -(-sum(_item_rows(s) for _, s in PACK_ITEMS) // (2 * SUBLANE)) * (2 * SUBLANE)


def _pack(values):
    parts, used = [], 0
    for name, shape in PACK_ITEMS:
        size, rows = math.prod(shape), _item_rows(shape)
        if name in values:
            flat = values[name].astype(F32).reshape(size)
            if rows * LANE > size:
                flat = jnp.pad(flat, (0, rows * LANE - size))
            parts.append(flat.reshape(rows, LANE))
        else:
            parts.append(jnp.zeros((rows, LANE), F32))
        used += rows
    if PACK_ROWS > used:
        parts.append(jnp.zeros((PACK_ROWS - used, LANE), F32))
    return jnp.concatenate(parts, axis=0)


def _unpack(pack):
    out, off = {}, 0
    for name, shape in PACK_ITEMS:
        rows = _item_rows(shape)
        out[name] = pack[off:off + rows].reshape(rows * LANE)[:math.prod(shape)].reshape(shape)
        off += rows
    return out


PLACE_ROWS = {"w_in": 256, "w_up": 256, "w_down": 352, "w_out": 256, "w_proj_a": 256,
              "w_proj_b": 256, "w_glu": 128}


def kernel(x, g_mix, w_in, a_re, a_im, log_dt, b_re, b_im, c_re, c_im, d_skip, w_glu, b_glu, w_proj_a, g_sgu, w_s, b_s, w_proj_b, w_out, g_ffn, w_up, conv_w, conv_b, w_down, g_final, loss_target, m_g_mix, m_w_in, m_a_re, m_a_im, m_log_dt, m_b_re, m_b_im, m_c_re, m_c_im, m_d_skip, m_w_glu, m_b_glu, m_w_proj_a, m_g_sgu, m_w_s, m_b_s, m_w_proj_b, m_w_out, m_g_ffn, m_w_up, m_conv_w, m_conv_b, m_w_down, m_g_final, v_g_mix, v_w_in, v_a_re, v_a_im, v_log_dt, v_b_re, v_b_im, v_c_re, v_c_im, v_d_skip, v_w_glu, v_b_glu, v_w_proj_a, v_g_sgu, v_w_s, v_b_s, v_w_proj_b, v_w_out, v_g_ffn, v_w_up, v_conv_w, v_conv_b, v_w_down, v_g_final):
    given = dict(locals())
    w = {n: given[n] for n in WEIGHTS}
    m = {n: given["m_" + n] for n in WEIGHTS}
    v = {n: given["v_" + n] for n in WEIGHTS}

    def shard2d(a):
        return a.reshape(a.shape[-2], a.shape[-1])

    chip = 2 * lax.axis_index("x") + lax.axis_index("y")
    where = jnp.stack([chip, lax.axis_index("c")]).astype(jnp.int32)
    xs, target = x[0], loss_target[0]
    small = {n: w[n].reshape(SMALL_SHAPES[n]) for n in SMALL}

    (abr, abi, bbr, bbi), disc_vjp = jax.vjp(_ssm_discretize, small["a_re"], small["a_im"],
                                             small["log_dt"], small["b_re"], small["b_im"])
    tab_f, tab_r = _scan_tables(abr, abi)
    bre = _block_diag_b(bbr)
    bim = _block_diag_b(bbi)
    cre = _block_diag_c(small["c_re"])
    cim = _block_diag_c(small["c_im"])
    tril = jnp.tril(jnp.ones((CHUNK, CHUNK), dtype=bool))
    ws = jnp.where(tril[None], small["w_s"], 0.0)
    ws_st = ws.reshape(SGU_G // 2, 2 * CHUNK, CHUNK).astype(BF16)
    wst_st = ws.transpose(0, 2, 1).reshape(SGU_G // 2, 2 * CHUNK, CHUNK).astype(BF16)
    bmat = jnp.repeat(small["b_s"].T, SGU_D, axis=1)
    g_mix2 = small["g_mix"].reshape(1, D_MODEL)
    g_ffn2 = small["g_ffn"].reshape(1, D_MODEL)
    g_final2 = small["g_final"].reshape(1, D_MODEL)
    g_sgu2 = small["g_sgu"].reshape(1, SGU_W)
    d_skip2 = small["d_skip"].reshape(1, SSM_W)
    b_glu2 = small["b_glu"].reshape(1, SSM_W)
    conv_b2 = small["conv_b"].reshape(1, 2 * D_FF)

    gat = {n: _place_shard(shard2d(w[n]), where, "place_" + n, BF16, PLACE_ROWS[n]) for n in BIG}
    gat["conv_w"] = _place_shard(shard2d(w["conv_w"]), where, "place_conv_w", F32, 3)
    (gat["w_in"],), = _comm("gather_in_ici", [_job_gather_ici([gat["w_in"]])])
    (gat["w_in"],), = _comm("gather_in_sibling", [_job_gather_sibling([gat["w_in"]])])
    early = ["w_glu", "w_proj_a", "w_proj_b", "w_out", "w_down", "conv_w"]

    (p, h1, bur, bui), (got,) = _fwd_in(
        xs, g_mix2, gat["w_in"], bre, bim,
        [_job_gather_ici([gat[n] for n in early], whole=(5,))])
    gat.update(zip(early, got))
    (str_, sti), (got_e, (gat["w_up"],)) = _scan_fwd(
        bur, bui, tab_f,
        [_job_gather_sibling([gat[n] for n in early[:5]]), _job_gather_ici([gat["w_up"]])])
    gat.update(zip(early[:5], got_e))
    w_glu_f = gat["w_glu"].reshape(SSM_W, SSM_W)
    w_out_f = gat["w_out"].reshape(D_MODEL, D_MODEL)
    conv_w_f = gat["conv_w"].transpose(1, 0, 2).reshape(3, 2 * D_FF)
    (x2, y0, z, mixed, ya, yb), ((gat["w_up"],),) = _fwd_mix(
        xs, p, str_, sti, cre, cim, d_skip2, w_glu_f, b_glu2, gat["w_proj_a"], g_sgu2, ws_st, bmat,
        gat["w_proj_b"], w_out_f, [_job_gather_sibling([gat["w_up"]])])
    w_down_f = gat["w_down"].reshape(D_FF, D_MODEL)
    up, act, f, h2, dx3, sm_ffn = _fwd_ffn(x2, target, g_ffn2, gat["w_up"], conv_w_f, conv_b2,
                                           w_down_f, g_final2)

    def leg1_done(names, got):
        return [_add_sibling(part[n], s, where, "add_sibling_" + n) for n, s in zip(names, got)]

    def leg2_done(names, sums, got):
        return [_add_chips(s, o, where, "add_chips_" + n) for n, s, o in zip(names, sums, got)]

    part, red = {}, {}
    part["w_down"] = _dw_rows(f, dx3, "dw_down", D_FF // 2, 2 * TK).reshape(
        N_CHIP, D_FF // N_CHIP, D_MODEL)
    (dx2, dup, sm_conv, sm_gffn), (got,) = _bwd_ffn(
        dx3, up, act, x2, g_ffn2, gat["w_up"], conv_w_f, w_down_f,
        [_job_sibling_halves([part["w_down"]])])
    sum_down = leg1_done(["w_down"], got)
    part["w_up"], (got,) = _dw_shards(h2, dup, "dw_up", 4 * TK, [_job_to_owner(sum_down)])
    red_down = leg2_done(["w_down"], sum_down, got)
    ((dsr, dsi, du_part, drest, mrg, dya, dyb, yap, dz, y1, sgu, dy0, sm_mix, dbm, dws),
     (got, (red["w_down"],))) = _bwd_mix(
        dx2, p, y0, z, mixed, ya, yb, w_out_f, gat["w_proj_a"], gat["w_proj_b"], w_glu_f, cre, cim,
        ws_st, wst_st, d_skip2, g_sgu2,
        [_job_sibling_halves([part["w_up"]]), _job_swap_halves(red_down)])
    sum_up = leg1_done(["w_up"], got)
    (lam_r, lam_i, dar8, dai8), (got,) = _scan_bwd(dsr, dsi, str_, sti, tab_r, [_job_to_owner(sum_up)])
    red_up = leg2_done(["w_up"], sum_up, got)
    mix4 = ["w_out", "w_proj_a", "w_proj_b", "w_glu"]
    part["w_out"] = _dw_cols(mrg, dx2, "dw_out", D_MODEL // 2, False)[0].reshape(
        N_CHIP, D_MODEL // N_CHIP, D_MODEL)
    part["w_proj_a"] = _dw_cols(yap, dya, "dw_proj_a", D_MODEL // N_CHIP, True)[0]
    part["w_proj_b"] = _dw_cols(sgu, dyb, "dw_proj_b", D_MODEL // N_CHIP, True)[0]
    part["w_glu"] = _dw_cols(y1, dz, "dw_glu", SSM_W, False)[0].reshape(
        N_CHIP, SSM_W // N_CHIP, SSM_W)
    (grad_x, dp, sm_gmix), (got, (red["w_up"],)) = _bwd_in(
        lam_r, lam_i, du_part, drest, xs, dx2, g_mix2, gat["w_in"], bre, bim,
        [_job_sibling_halves([part[n] for n in mix4]), _job_swap_halves(red_up)])
    sums_m = leg1_done(mix4, got)
    part["w_in"], (got,) = _dw_cols(h1, dp, "dw_in", IN_COLS // N_CHIP, True, [_job_to_owner(sums_m)])
    red_m = leg2_done(mix4, sums_m, got)
    (dbd_r, dbd_i), (got, done_m) = _dw_pair(
        p, SSM_W, lam_r, lam_i, "db_bar",
        [_job_sibling_halves([part["w_in"]]), _job_swap_halves(red_m)])
    red.update(zip(mix4, done_m))
    sum_in = leg1_done(["w_in"], got)
    (dcd_r, dcd_i), (got,) = _dw_pair(dy0, SSM_W, str_, sti, "dc", [_job_to_owner(sum_in)])
    red_in = leg2_done(["w_in"], sum_in, got)
    (red["w_in"],), = _comm("swap_w_in", [_job_swap_halves(red_in)])

    def pick_c(slabs):
        two = LANE // SSM_P
        return jnp.einsum("jshsp->jshp", slabs.reshape(SSM_G // two, two, SSM_H, two, SSM_P)
                          ).reshape(SSM_G, SSM_H, SSM_P)

    def pick_b(slabs):
        return pick_c(slabs).transpose(0, 2, 1)

    dabr = jnp.sum(dar8, axis=0).reshape(SSM_G, SSM_P)
    dabi = jnp.sum(dai8, axis=0).reshape(SSM_G, SSM_P)
    d_a_re, d_a_im, d_log_dt, d_b_re, d_b_im = disc_vjp((dabr, dabi, pick_b(dbd_r), pick_b(dbd_i)))
    gsmall = {
        "g_mix": sm_gmix[0], "a_re": d_a_re, "a_im": d_a_im, "log_dt": d_log_dt,
        "b_re": d_b_re, "b_im": d_b_im, "c_re": pick_c(dcd_r), "c_im": -pick_c(dcd_i),
        "d_skip": sm_mix[0], "b_glu": sm_mix[1], "g_sgu": sm_mix[2],
        "w_s": jnp.where(tril[None], dws, 0.0),
        "b_s": dbm.reshape(CHUNK, SGU_G, SGU_D).sum(-1).T,
        "g_ffn": sm_gffn[0], "conv_b": sm_conv[3], "g_final": sm_ffn[0],
        "conv_w": sm_conv[0:3], "loss": sm_ffn[1, 0:1],
    }

    total_pack = _small_allreduce(_pack(gsmall))
    total = _unpack(total_pack)
    grads = dict(red)
    cs = 2 * D_FF // N_CHIP
    grads["conv_w"] = lax.dynamic_slice(total["conv_w"], (0, chip * cs), (3, cs))
    delta, new_m, new_v = {}, {}, {}
    for n in BIG + ("conv_w",):
        delta[n], new_m[n], new_v[n] = _adamw(shard2d(w[n]), grads[n], shard2d(m[n]), shard2d(v[n]),
                                              "adamw_" + n, PLACE_ROWS.get(n, 3))
    for n in SMALL:
        grads[n] = total[n].reshape(w[n].shape)
    ud, um, uv = _adamw_many(*[[d[n] for n in SMALL] for d in (w, grads, m, v)], "adamw_small")
    for i, n in enumerate(SMALL):
        delta[n], new_m[n], new_v[n] = ud[i], um[i], uv[i]

    def like(d):
        return [d[n].reshape(w[n].shape) for n in WEIGHTS]

    return (total["loss"].reshape(()), grad_x.reshape(x.shape), *like(grads), *like(delta),
            *like(new_m), *like(new_v))
```

```python
import math

import jax
import jax.numpy as jnp
from jax import lax
from jax.experimental import pallas as pl
from jax.experimental.pallas import tpu as pltpu

F32 = jnp.float32
BF16 = jnp.bfloat16
MESH = pl.DeviceIdType.MESH

D_MODEL = 1024
SSM_W = 512
SSM_G = 32
SSM_H = 16
SSM_P = 64
N_STATE = SSM_G * SSM_P
DIAG_N = 128 * SSM_P // SSM_H
SGU_W = 512
SGU_G = 8
SGU_D = 64
CHUNK = 128
D_FF = 2816
IN_COLS = 3584
EPS = 1e-6
N_CHIP = 4

ADAM_LR = 0.001
ADAM_B1 = 0.9
ADAM_B2 = 0.999
ADAM_EPS = 1e-08
ADAM_WD = 0.01
ADAM_STEP = 10

SUBLANE = 8
LANE = 128
VMEM_LIMIT = 56 * 1024 * 1024
TB = 256
TK = 512
SCAN_LANES = 256
SCAN_UNROLL = 4
HALO = SUBLANE

BIG = ("w_in", "w_up", "w_down", "w_out", "w_proj_a", "w_proj_b", "w_glu")
SMALL = ("g_mix", "a_re", "a_im", "log_dt", "b_re", "b_im", "c_re", "c_im", "d_skip", "b_glu",
         "g_sgu", "w_s", "b_s", "g_ffn", "conv_b", "g_final")
WEIGHTS = ("g_mix", "w_in", "a_re", "a_im", "log_dt", "b_re", "b_im", "c_re", "c_im", "d_skip",
           "w_glu", "b_glu", "w_proj_a", "g_sgu", "w_s", "b_s", "w_proj_b", "w_out", "g_ffn",
           "w_up", "conv_w", "conv_b", "w_down", "g_final")

ANY = pl.BlockSpec(memory_space=pl.ANY)


def _params(n_grid):
    return pltpu.CompilerParams(dimension_semantics=("arbitrary",) * n_grid if n_grid else None,
                                vmem_limit_bytes=VMEM_LIMIT)


def _whole():
    return pl.BlockSpec(memory_space=pltpu.VMEM)


def _rows(tb, ncol):
    return pl.BlockSpec((tb, ncol), lambda i: (i, 0))


def _acc(nrow, ncol):
    return pl.BlockSpec((nrow, ncol), lambda i: (0, 0))


def _dot(a, b):
    return jnp.dot(a.astype(BF16), b.astype(BF16), preferred_element_type=F32)


def _dot_nt(a, b):
    return lax.dot_general(a.astype(BF16), b.astype(BF16), (((1,), (1,)), ((), ())),
                           preferred_element_type=F32)


def _sigmoid(v):
    return 0.5 * jnp.tanh(0.5 * v) + 0.5


_GELU_C = math.sqrt(2.0 / math.pi)


def _gelu(v):
    return 0.5 * v * (1.0 + jnp.tanh(_GELU_C * (v + 0.044715 * v * v * v)))


def _gelu_and_grad(v):
    v2 = v * v
    t = jnp.tanh(_GELU_C * v * (1.0 + 0.044715 * v2))
    half = 0.5 * (1.0 + t)
    return v * half, half + 0.5 * v * (1.0 - t * t) * _GELU_C * (1.0 + 3.0 * 0.044715 * v2)


def _rms_stats(v):
    r = lax.rsqrt(jnp.mean(v * v, axis=-1, keepdims=True) + EPS)
    return r, v * r


def _rms_bwd(dxh, xh, r):
    return r * (dxh - xh * jnp.mean(dxh * xh, axis=-1, keepdims=True))


def _place():
    x, y, c = lax.axis_index("x"), lax.axis_index("y"), lax.axis_index("c")
    chips = [(1 - x, y), (x, 1 - y), (1 - x, 1 - y)]
    return x, y, c, chips


def _chip_index(chip):
    return 2 * chip[0] + chip[1]


def _remote(src, dst, send_sem, recv_sem, device):
    return pltpu.make_async_remote_copy(src_ref=src, dst_ref=dst, send_sem=send_sem,
                                        recv_sem=recv_sem, device_id=device, device_id_type=MESH)


def _half(ref_rows, c):
    hr = ref_rows // 2
    return pl.ds(pl.multiple_of(c * hr, SUBLANE), hr)


class _Job:
    def __init__(self, start, finish, n_sem, ins=(), inouts=(), outs=()):
        self.start, self.finish, self.n_sem = start, finish, n_sem
        self.ins, self.inouts, self.outs = list(ins), list(inouts), list(outs)


def _job_gather_ici(bufs, whole=()):
    n = len(bufs)

    def copies(io):
        x, y, c, chips = _place()
        k_me = 2 * x + y
        out = []
        for t in range(n):
            for j, ch in enumerate(chips):
                if t in whole:
                    src, land = io[t].at[k_me], io[t].at[_chip_index(ch)]
                else:
                    mine = _half(bufs[t].shape[1], c)
                    src, land = io[t].at[k_me, mine, :], io[t].at[_chip_index(ch), mine, :]
                out.append((src, land, 3 * t + j, (*ch, c)))
        return out

    def start(ins, io, outs, ssem, rsem):
        for src, _, i, dev in copies(io):
            _remote(src, src, ssem(i), rsem(i), dev).start()

    def finish(ins, io, outs, ssem, rsem):
        cps = copies(io)
        for _, land, i, dev in cps:
            _remote(land, land, ssem(i), rsem(i), dev).wait_recv()
        for src, _, i, dev in cps:
            _remote(src, src, ssem(i), rsem(i), dev).wait_send()

    return _Job(start, finish, 3 * n, inouts=bufs)


def _job_gather_sibling(bufs):
    n = len(bufs)

    def copies(io):
        x, y, c, chips = _place()
        out = []
        for t in range(n):
            rows = bufs[t].shape[1]
            for j, ch in enumerate(chips):
                k = _chip_index(ch)
                out.append((io[t].at[k, _half(rows, c), :], io[t].at[k, _half(rows, 1 - c), :],
                            3 * t + j, (x, y, 1 - c)))
        return out

    def start(ins, io, outs, ssem, rsem):
        for src, _, i, dev in copies(io):
            _remote(src, src, ssem(i), rsem(i), dev).start()

    def finish(ins, io, outs, ssem, rsem):
        cps = copies(io)
        for _, land, i, dev in cps:
            _remote(land, land, ssem(i), rsem(i), dev).wait_recv()
        for src, _, i, dev in cps:
            _remote(src, src, ssem(i), rsem(i), dev).wait_send()

    return _Job(start, finish, 3 * n, inouts=bufs)


def _job_sibling_halves(grads):
    n = len(grads)

    def build(ins, outs, ssem, rsem):
        x, y, c, _ = _place()
        return [_remote(ins[t].at[:, _half(grads[t].shape[1], 1 - c), :], outs[t], ssem(t), rsem(t),
                        (x, y, 1 - c)) for t in range(n)]

    def start(ins, io, outs, ssem, rsem):
        for cp in build(ins, outs, ssem, rsem):
            cp.start()

    def finish(ins, io, outs, ssem, rsem):
        for cp in build(ins, outs, ssem, rsem):
            cp.wait()

    return _Job(start, finish, n, ins=grads,
                outs=[jax.ShapeDtypeStruct((N_CHIP, g.shape[1] // 2, g.shape[2]), F32) for g in grads])


def _job_to_owner(sums):
    n = len(sums)

    def build(ins, outs, ssem, rsem):
        x, y, c, chips = _place()
        return [_remote(ins[t].at[_chip_index(ch)], outs[t].at[j], ssem(3 * t + j), rsem(3 * t + j),
                        (*ch, c)) for t in range(n) for j, ch in enumerate(chips)]

    def start(ins, io, outs, ssem, rsem):
        for cp in build(ins, outs, ssem, rsem):
            cp.start()

    def finish(ins, io, outs, ssem, rsem):
        for cp in build(ins, outs, ssem, rsem):
            cp.wait()

    return _Job(start, finish, 3 * n, ins=sums,
                outs=[jax.ShapeDtypeStruct((3,) + s.shape[1:], s.dtype) for s in sums])


def _job_swap_halves(bufs):
    n = len(bufs)

    def start(ins, io, outs, ssem, rsem):
        x, y, c, _ = _place()
        for t in range(n):
            mine = io[t].at[_half(bufs[t].shape[0], c), :]
            _remote(mine, mine, ssem(t), rsem(t), (x, y, 1 - c)).start()

    def finish(ins, io, outs, ssem, rsem):
        x, y, c, _ = _place()
        for t in range(n):
            theirs = io[t].at[_half(bufs[t].shape[0], 1 - c), :]
            _remote(theirs, theirs, ssem(t), rsem(t), (x, y, 1 - c)).wait_recv()
        for t in range(n):
            mine = io[t].at[_half(bufs[t].shape[0], c), :]
            _remote(mine, mine, ssem(t), rsem(t), (x, y, 1 - c)).wait_send()

    return _Job(start, finish, n, inouts=bufs)


def _call(body, name, grid, in_specs, out_specs, out_shape, args, jobs=(), scratch=()):
    n_in, n_out, n_scr = len(args), len(out_shape), len(scratch)
    job_in = [a for jb in jobs for a in jb.ins + jb.inouts]
    job_out = [s for jb in jobs
               for s in [jax.ShapeDtypeStruct(a.shape, a.dtype) for a in jb.inouts] + jb.outs]
    aliases, pos_in, pos_out = {}, n_in, n_out
    for jb in jobs:
        pos_in += len(jb.ins)
        for _ in jb.inouts:
            aliases[pos_in] = pos_out
            pos_in += 1
            pos_out += 1
        pos_out += len(jb.outs)
    n_sem = sum(jb.n_sem for jb in jobs)

    def wrapped(*refs):
        c_in = refs[:n_in]
        j_in = refs[n_in:n_in + len(job_in)]
        c_out = refs[n_in + len(job_in):n_in + len(job_in) + n_out]
        j_out = refs[n_in + len(job_in) + n_out:n_in + len(job_in) + n_out + len(job_out)]
        rest = refs[n_in + len(job_in) + n_out + len(job_out):]
        c_scr = rest[:n_scr]
        views, pi, po, ps = [], 0, 0, 0
        for jb in jobs:
            ins = j_in[pi:pi + len(jb.ins)]
            pi += len(jb.ins) + len(jb.inouts)
            io = j_out[po:po + len(jb.inouts)]
            new = j_out[po + len(jb.inouts):po + len(jb.inouts) + len(jb.outs)]
            po += len(jb.inouts) + len(jb.outs)
            send = (lambda i, o=ps: rest[n_scr].at[o + i])
            recv = (lambda i, o=ps: rest[n_scr + 1].at[o + i])
            ps += jb.n_sem
            views.append((ins, io, new, send, recv))

        def run(which):
            for jb, vw in zip(jobs, views):
                (jb.start if which == 0 else jb.finish)(*vw)

        if not grid:
            run(0)
            run(1)
            return
        if jobs:
            first = pl.program_id(0) == 0
            last = pl.program_id(0) == grid[0] - 1
            for d in range(1, len(grid)):
                first = jnp.logical_and(first, pl.program_id(d) == 0)
                last = jnp.logical_and(last, pl.program_id(d) == grid[d] - 1)
            pl.when(first)(lambda: run(0))
        body(*c_in, *c_out, *c_scr)
        if jobs:
            pl.when(last)(lambda: run(1))

    sems = [pltpu.SemaphoreType.DMA((n_sem,)), pltpu.SemaphoreType.DMA((n_sem,))] if jobs else []
    kwargs = dict(grid=grid) if grid else {}
    res = pl.pallas_call(
        wrapped, name=name, in_specs=list(in_specs) + [ANY] * len(job_in),
        out_specs=list(out_specs) + [ANY] * len(job_out),
        out_shape=list(out_shape) + job_out, scratch_shapes=list(scratch) + sems,
        input_output_aliases=aliases, compiler_params=_params(len(grid)), **kwargs,
    )(*args, *job_in)
    outs, pos, per_job = list(res[:n_out]), n_out, []
    for jb in jobs:
        k = len(jb.inouts) + len(jb.outs)
        per_job.append(list(res[pos:pos + k]))
        pos += k
    return outs, per_job


def _comm(name, jobs):
    return _call(None, name, (), [], [], [], [], jobs)[1]


def _fwd_in(x, g_mix, w_in, bre, bim, jobs=()):
    t_len = x.shape[0]
    cs = IN_COLS // N_CHIP

    def body(x_ref, g_ref, w_ref, bre_ref, bim_ref, p_ref, h_ref, bur_ref, bui_ref):
        xv = x_ref[...]
        r, xh = _rms_stats(xv)
        h = (xh * g_ref[...]).astype(BF16)
        h_ref[...] = h
        for k in range(N_CHIP):
            p_ref[:, k * cs:(k + 1) * cs] = jnp.dot(h, w_ref[k],
                                                    preferred_element_type=F32).astype(BF16)
        u = p_ref[:, 0:SSM_W]
        for i in range(SSM_W // LANE):
            rows, cols = slice(i * LANE, (i + 1) * LANE), slice(i * DIAG_N, (i + 1) * DIAG_N)
            bur_ref[:, cols] = jnp.dot(u[:, rows], bre_ref[rows, cols],
                                       preferred_element_type=F32).astype(BF16)
            bui_ref[:, cols] = jnp.dot(u[:, rows], bim_ref[rows, cols],
                                       preferred_element_type=F32).astype(BF16)

    return _call(
        body, "fwd_in", (t_len // TB,),
        [_rows(TB, D_MODEL), _whole(), _whole(), _whole(), _whole()],
        [_rows(TB, IN_COLS), _rows(TB, D_MODEL), _rows(TB, N_STATE), _rows(TB, N_STATE)],
        [jax.ShapeDtypeStruct((t_len, IN_COLS), BF16), jax.ShapeDtypeStruct((t_len, D_MODEL), BF16),
         jax.ShapeDtypeStruct((t_len, N_STATE), BF16), jax.ShapeDtypeStruct((t_len, N_STATE), BF16)],
        [x, g_mix, w_in, bre, bim], jobs)


def _scan_local(xr, xi, tab, shifts):
    for q, s in enumerate(shifts):
        ar, ai = tab[2 * q], tab[2 * q + 1]
        rr = pltpu.roll(xr, s, 0)
        ri = pltpu.roll(xi, s, 0)
        xr, xi = xr + ar * rr - ai * ri, xi + ar * ri + ai * rr
    return xr, xi


def _scan_carry(xr, xi, tab, cr, ci):
    pr, pi = tab[6], tab[7]
    return xr + pr * cr - pi * ci, xi + pr * ci + pi * cr


BF16_TILE = 2 * SUBLANE


def _load_blocks(r_ref, i_ref, base):
    out = []
    for q in range(SCAN_UNROLL // 2):
        rows = pl.ds(pl.multiple_of(base + q * BF16_TILE, BF16_TILE), BF16_TILE)
        vr, vi = r_ref[rows, :].astype(F32), i_ref[rows, :].astype(F32)
        out += [(vr[:SUBLANE], vi[:SUBLANE]), (vr[SUBLANE:], vi[SUBLANE:])]
    return out


def _store_blocks(r_ref, i_ref, base, blocks):
    for q in range(SCAN_UNROLL // 2):
        rows = pl.ds(pl.multiple_of(base + q * BF16_TILE, BF16_TILE), BF16_TILE)
        r_ref[rows, :] = jnp.concatenate([blocks[2 * q][0], blocks[2 * q + 1][0]], 0).astype(r_ref.dtype)
        i_ref[rows, :] = jnp.concatenate([blocks[2 * q][1], blocks[2 * q + 1][1]], 0).astype(i_ref.dtype)


def _scan_fwd(bur, bui, tab, jobs=()):
    t_len = bur.shape[0]
    nblk = t_len // SUBLANE
    lb = SCAN_LANES

    def body(br_ref, bi_ref, tab_ref, sr_ref, si_ref):
        tab_v = [tab_ref[q] for q in range(8)]

        def step(k, carry):
            cr, ci = carry
            base = pl.multiple_of(k * SCAN_UNROLL * SUBLANE, SCAN_UNROLL * SUBLANE)
            local = [_scan_local(xr, xi, tab_v, (1, 2, 4))
                     for xr, xi in _load_blocks(br_ref, bi_ref, base)]
            done = []
            for xr, xi in local:
                xr, xi = _scan_carry(xr, xi, tab_v, cr, ci)
                done.append((xr, xi))
                cr, ci = xr[SUBLANE - 1:SUBLANE, :], xi[SUBLANE - 1:SUBLANE, :]
            _store_blocks(sr_ref, si_ref, base, done)
            return cr, ci

        zero = jnp.zeros((1, lb), F32)
        lax.fori_loop(0, nblk // SCAN_UNROLL, step, (zero, zero))

    col = pl.BlockSpec((t_len, lb), lambda j: (0, j))
    return _call(
        body, "scan_fwd", (N_STATE // lb,),
        [col, col, pl.BlockSpec((8, SUBLANE, lb), lambda j: (0, 0, j))], [col, col],
        [jax.ShapeDtypeStruct((t_len, N_STATE), BF16)] * 2, [bur, bui, tab], jobs)


def _sgu_mix(v, ws_ref, lane_lo):
    rows = []
    for c0 in range(0, v.shape[0], CHUNK):
        slabs = []
        for j in range(SGU_W // LANE):
            prod = jnp.dot(ws_ref[j], v[c0:c0 + CHUNK, j * LANE:(j + 1) * LANE].astype(BF16),
                           preferred_element_type=F32)
            slabs.append(jnp.where(lane_lo, prod[:CHUNK], prod[CHUNK:]))
        rows.append(jnp.concatenate(slabs, axis=1))
    return jnp.concatenate(rows, axis=0) if len(rows) > 1 else rows[0]


def _fwd_mix(x, p, str_, sti, cre, cim, d_skip, w_glu, b_glu, w_pa, g_sgu, ws_st, bmat, w_pb, w_out,
             jobs=()):
    t_len = x.shape[0]

    def body(x_ref, p_ref, sr_ref, si_ref, cre_ref, cim_ref, dsk_ref, wg_ref, bg_ref, wpa_ref,
             gs_ref, ws_ref, bm_ref, wpb_ref, wo_ref,
             x2_ref, y0_ref, z_ref, mx_ref, ya_ref, yb_ref):
        u = p_ref[:, 0:SSM_W].astype(F32)
        y0 = jnp.concatenate(
            [_dot(sr_ref[:, i * DIAG_N:(i + 1) * DIAG_N],
                  cre_ref[i * DIAG_N:(i + 1) * DIAG_N, i * LANE:(i + 1) * LANE])
             - _dot(si_ref[:, i * DIAG_N:(i + 1) * DIAG_N],
                    cim_ref[i * DIAG_N:(i + 1) * DIAG_N, i * LANE:(i + 1) * LANE])
             for i in range(SSM_W // LANE)], axis=1) + dsk_ref[...] * u
        y0_ref[...] = y0.astype(BF16)
        y1 = _gelu(y0)
        z = _dot(y1, wg_ref[...]) + bg_ref[...]
        z_ref[...] = z.astype(BF16)
        ya_pre = (y1 * _sigmoid(z)).astype(BF16)
        ya = jnp.concatenate([jnp.dot(ya_pre, wpa_ref[k], preferred_element_type=F32)
                              for k in range(N_CHIP)], axis=1)
        ya_ref[...] = ya.astype(BF16)

        uvg = _gelu(p_ref[:, SSM_W:SSM_W + 2 * SGU_W].astype(F32))
        u2 = uvg[:, :SGU_W]
        _, vh = _rms_stats(uvg[:, SGU_W:])
        v3 = vh * gs_ref[...]
        lane_lo = lax.broadcasted_iota(jnp.int32, (CHUNK, LANE), 1) < SGU_D
        bias = jnp.concatenate([bm_ref[...]] * (TB // CHUNK), axis=0)
        mixed = _sgu_mix(v3, ws_ref, lane_lo) + bias
        mx_ref[...] = mixed.astype(BF16)
        sgu = (u2 * mixed).astype(BF16)
        yb = jnp.concatenate([jnp.dot(sgu, wpb_ref[k], preferred_element_type=F32)
                              for k in range(N_CHIP)], axis=1)
        yb_ref[...] = yb.astype(BF16)

        lg0 = SSM_W + 2 * SGU_W
        ga = _sigmoid(p_ref[:, lg0:lg0 + D_MODEL].astype(F32))
        gb = _sigmoid(p_ref[:, lg0 + D_MODEL:lg0 + 2 * D_MODEL].astype(F32))
        mrg = ga * ya + gb * yb
        x2_ref[...] = x_ref[...] + _dot(mrg, wo_ref[...])

    return _call(
        body, "fwd_mix", (t_len // TB,),
        [_rows(TB, D_MODEL), _rows(TB, IN_COLS), _rows(TB, N_STATE), _rows(TB, N_STATE)]
        + [_whole()] * 11,
        [_rows(TB, D_MODEL), _rows(TB, SSM_W), _rows(TB, SSM_W), _rows(TB, SGU_W),
         _rows(TB, D_MODEL), _rows(TB, D_MODEL)],
        [jax.ShapeDtypeStruct((t_len, D_MODEL), F32), jax.ShapeDtypeStruct((t_len, SSM_W), BF16),
         jax.ShapeDtypeStruct((t_len, SSM_W), BF16), jax.ShapeDtypeStruct((t_len, SGU_W), BF16),
         jax.ShapeDtypeStruct((t_len, D_MODEL), BF16), jax.ShapeDtypeStruct((t_len, D_MODEL), BF16)],
        [x, p, str_, sti, cre, cim, d_skip, w_glu, b_glu, w_pa, g_sgu, ws_st, bmat, w_pb, w_out], jobs)


def _conv_taps(v, cw_ref, c0, width):
    w0 = cw_ref[0:1, c0:c0 + width]
    w1 = cw_ref[1:2, c0:c0 + width]
    w2 = cw_ref[2:3, c0:c0 + width]
    return w0 * pltpu.roll(v, 2, 0) + w1 * pltpu.roll(v, 1, 0) + w2 * v


def _fwd_ffn(x2, target, g_ffn, w_up, conv_w, conv_b, w_down, g_final):
    t_len = x2.shape[0]
    half = D_FF // 2
    blocks_per_halo = TB // HALO

    def body(x2_ref, xp_ref, tg_ref, gf_ref, wu_ref, cw_ref, cb_ref, wd_ref, gl_ref,
             up_ref, act_ref, f_ref, h2_ref, dx3_ref, sm_ref):
        i = pl.program_id(0)
        xe = jnp.concatenate([xp_ref[...] * jnp.where(i == 0, 0.0, 1.0), x2_ref[...]], axis=0)
        _, xh = _rms_stats(xe)
        h2 = (xh * gf_ref[...]).astype(BF16)
        h2_ref[...] = h2[HALO:]
        acc = jnp.zeros((TB, D_MODEL), F32)
        for hc in range(2):
            ca = hc * half
            cb = D_FF + hc * half
            ua = jnp.dot(h2, wu_ref[hc], preferred_element_type=F32)
            ub = jnp.dot(h2, wu_ref[2 + hc], preferred_element_type=F32)
            up_ref[:, ca:ca + half] = ua[HALO:].astype(BF16)
            up_ref[:, cb:cb + half] = ub[HALO:].astype(BF16)
            ac = _conv_taps(ua, cw_ref, ca, half)[HALO:] + cb_ref[:, ca:ca + half]
            bc = _conv_taps(ub, cw_ref, cb, half)[HALO:] + cb_ref[:, cb:cb + half]
            act_ref[:, ca:ca + half] = ac.astype(BF16)
            act_ref[:, cb:cb + half] = bc.astype(BF16)
            f = (ac * _sigmoid(ac) * bc).astype(BF16)
            f_ref[:, ca:ca + half] = f
            acc = acc + jnp.dot(f, wd_ref[ca:ca + half, :], preferred_element_type=F32)
        x3 = x2_ref[...] + acc
        r3, xh3 = _rms_stats(x3)
        err = xh3 * gl_ref[...] - tg_ref[...]
        dout = err * (1.0 / D_MODEL)
        dx3_ref[...] = _rms_bwd(dout * gl_ref[...], xh3, r3)
        dgl = jnp.sum(dout * xh3, axis=0, keepdims=True)
        loss = 0.5 * jnp.sum(jnp.mean(err * err, axis=-1, keepdims=True), axis=0, keepdims=True)
        upd = jnp.concatenate([dgl, jnp.broadcast_to(loss, (1, D_MODEL)),
                               jnp.zeros((SUBLANE - 2, D_MODEL), F32)], axis=0)

        @pl.when(i == 0)
        def _():
            sm_ref[...] = upd

        @pl.when(i > 0)
        def _():
            sm_ref[...] += upd

    prev = pl.BlockSpec((HALO, D_MODEL), lambda i: (jnp.maximum(i * blocks_per_halo - 1, 0), 0))
    return _call(
        body, "fwd_ffn", (t_len // TB,),
        [_rows(TB, D_MODEL), prev, _rows(TB, D_MODEL)] + [_whole()] * 6,
        [_rows(TB, 2 * D_FF), _rows(TB, 2 * D_FF), _rows(TB, D_FF), _rows(TB, D_MODEL),
         _rows(TB, D_MODEL), _acc(SUBLANE, D_MODEL)],
        [jax.ShapeDtypeStruct((t_len, 2 * D_FF), BF16), jax.ShapeDtypeStruct((t_len, 2 * D_FF), BF16),
         jax.ShapeDtypeStruct((t_len, D_FF), BF16), jax.ShapeDtypeStruct((t_len, D_MODEL), BF16),
         jax.ShapeDtypeStruct((t_len, D_MODEL), F32), jax.ShapeDtypeStruct((SUBLANE, D_MODEL), F32)],
        [x2, x2, target, g_ffn, w_up, conv_w, conv_b, w_down, g_final])[0]


def _bwd_ffn(dx3, up, act, x2, g_ffn, w_up, conv_w, w_down, jobs=()):
    t_len = x2.shape[0]
    half = D_FF // 2
    nblk = t_len // TB
    halo_b = 2 * HALO
    n_e = TB + HALO

    def body(dx_ref, dxn_ref, up_ref, act_ref, actn_ref, x2_ref, gf_ref, wu_ref, cw_ref,
             wd_ref, dx2_ref, dup_ref, smw_ref, smg_ref):
        i = pl.program_id(0)
        keep_last = jnp.where(i == nblk - 1, 0.0, 1.0)
        dxe = jnp.concatenate([dx_ref[...], dxn_ref[...] * keep_last], axis=0).astype(BF16)
        dh2 = jnp.zeros((TB, D_MODEL), F32)
        zpad = jnp.zeros((1, half), F32)
        for hc in range(2):
            ca = hc * half
            cb = D_FF + hc * half
            ac = jnp.concatenate([act_ref[:, ca:ca + half].astype(F32),
                                  actn_ref[:, ca:ca + half].astype(F32)[:HALO]], axis=0)
            bc = jnp.concatenate([act_ref[:, cb:cb + half].astype(F32),
                                  actn_ref[:, cb:cb + half].astype(F32)[:HALO]], axis=0)
            wa = [cw_ref[k:k + 1, ca:ca + half] for k in range(3)]
            wb = [cw_ref[k:k + 1, cb:cb + half] for k in range(3)]
            df = lax.dot_general(dxe, wd_ref[ca:ca + half, :], (((1,), (1,)), ((), ())),
                                 preferred_element_type=F32)
            sg = _sigmoid(ac)
            da = df * bc * sg * (1.0 + ac * (1.0 - sg))
            db = df * ac * sg
            da1, da2 = pltpu.roll(da, n_e - 1, 0), pltpu.roll(da, n_e - 2, 0)
            db1, db2 = pltpu.roll(db, n_e - 1, 0), pltpu.roll(db, n_e - 2, 0)
            dua = (wa[2] * da + wa[1] * da1 + wa[0] * da2)[:TB]
            dub = (wb[2] * db + wb[1] * db1 + wb[0] * db2)[:TB]
            dup_ref[:, ca:ca + half] = dua.astype(BF16)
            dup_ref[:, cb:cb + half] = dub.astype(BF16)
            dh2 = dh2 + _dot_nt(dua, wu_ref[hc]) + _dot_nt(dub, wu_ref[2 + hc])
            rows = []
            for u_, d0, d1, d2 in ((up_ref[:, ca:ca + half].astype(F32), da, da1, da2),
                                   (up_ref[:, cb:cb + half].astype(F32), db, db1, db2)):
                rows.append([jnp.sum(u_ * d2[:TB], axis=0, keepdims=True),
                             jnp.sum(u_ * d1[:TB], axis=0, keepdims=True),
                             jnp.sum(u_ * d0[:TB], axis=0, keepdims=True),
                             jnp.sum(d0[:TB], axis=0, keepdims=True)])
            for c0, rws in ((ca, rows[0]), (cb, rows[1])):
                upd = jnp.concatenate(rws + [zpad] * (SUBLANE - 4), axis=0)

                @pl.when(i == 0)
                def _(upd=upd, c0=c0):
                    smw_ref[:, c0:c0 + half] = upd

                @pl.when(i > 0)
                def _(upd=upd, c0=c0):
                    smw_ref[:, c0:c0 + half] += upd

        r2, xh2 = _rms_stats(x2_ref[...])
        dx2_ref[...] = dx_ref[...] + _rms_bwd(dh2 * gf_ref[...], xh2, r2)
        updg = jnp.concatenate([jnp.sum(dh2 * xh2, axis=0, keepdims=True),
                                jnp.zeros((SUBLANE - 1, D_MODEL), F32)], axis=0)

        @pl.when(i == 0)
        def _():
            smg_ref[...] = updg

        @pl.when(i > 0)
        def _():
            smg_ref[...] += updg

    nxt_d = pl.BlockSpec((HALO, D_MODEL),
                         lambda i: (jnp.minimum((i + 1) * (TB // HALO), t_len // HALO - 1), 0))
    nxt_a = pl.BlockSpec((halo_b, 2 * D_FF),
                         lambda i: (jnp.minimum((i + 1) * (TB // halo_b), t_len // halo_b - 1), 0))
    return _call(
        body, "bwd_ffn", (nblk,),
        [_rows(TB, D_MODEL), nxt_d, _rows(TB, 2 * D_FF), _rows(TB, 2 * D_FF), nxt_a,
         _rows(TB, D_MODEL)] + [_whole()] * 4,
        [_rows(TB, D_MODEL), _rows(TB, 2 * D_FF), _acc(SUBLANE, 2 * D_FF), _acc(SUBLANE, D_MODEL)],
        [jax.ShapeDtypeStruct((t_len, D_MODEL), F32), jax.ShapeDtypeStruct((t_len, 2 * D_FF), BF16),
         jax.ShapeDtypeStruct((SUBLANE, 2 * D_FF), F32), jax.ShapeDtypeStruct((SUBLANE, D_MODEL), F32)],
        [dx3, dx3, up, act, act, x2, g_ffn, w_up, conv_w, w_down], jobs)


def _bwd_mix(dx2, p, y0, z, mixed, ya, yb, w_out, w_pa, w_pb, w_glu, cre, cim, ws_st, wst_st,
             d_skip, g_sgu, jobs=()):
    t_len = dx2.shape[0]
    pc = D_MODEL // N_CHIP
    n_slab = SGU_W // LANE

    def body(dx_ref, p_ref, y0_ref, z_ref, mx_ref, ya_ref, yb_ref, wo_ref, wpa_ref, wpb_ref,
             wg_ref, cre_ref, cim_ref, ws_ref, wst_ref, dsk_ref, gs_ref,
             dsr_ref, dsi_ref, du_ref, drest_ref, mrg_ref, dya_ref, dyb_ref, yap_ref, dz_ref,
             y1_ref, sgu_ref, dy0_ref, sm_ref, dbm_ref, dws_ref):
        i = pl.program_id(0)
        first = i == 0
        lg0 = SSM_W + 2 * SGU_W
        dmrg = _dot_nt(dx_ref[...], wo_ref[...])
        ga = _sigmoid(p_ref[:, lg0:lg0 + D_MODEL].astype(F32))
        gb = _sigmoid(p_ref[:, lg0 + D_MODEL:lg0 + 2 * D_MODEL].astype(F32))
        yav = ya_ref[...].astype(F32)
        ybv = yb_ref[...].astype(F32)
        mrg_ref[...] = (ga * yav + gb * ybv).astype(BF16)
        drest_ref[:, 2 * SGU_W:2 * SGU_W + D_MODEL] = (dmrg * yav * ga * (1.0 - ga)).astype(BF16)
        drest_ref[:, 2 * SGU_W + D_MODEL:] = (dmrg * ybv * gb * (1.0 - gb)).astype(BF16)
        dya = (dmrg * ga).astype(BF16)
        dyb = (dmrg * gb).astype(BF16)
        dya_ref[...] = dya
        dyb_ref[...] = dyb

        y0v = y0_ref[...].astype(F32)
        y1, y1_grad = _gelu_and_grad(y0v)
        sz = _sigmoid(z_ref[...].astype(F32))
        y1_ref[...] = y1.astype(BF16)
        yap_ref[...] = (y1 * sz).astype(BF16)
        dyap = jnp.zeros((TB, SSM_W), F32)
        for k in range(N_CHIP):
            dyap = dyap + _dot_nt(dya[:, k * pc:(k + 1) * pc], wpa_ref[k])
        dz = dyap * y1 * sz * (1.0 - sz)
        dz_ref[...] = dz.astype(BF16)
        dy0 = (dyap * sz + _dot_nt(dz, wg_ref[...])) * y1_grad
        dy0_ref[...] = dy0.astype(BF16)
        u = p_ref[:, 0:SSM_W].astype(F32)
        du_ref[...] = dy0 * dsk_ref[...]
        for i in range(SSM_W // LANE):
            rows, cols = slice(i * DIAG_N, (i + 1) * DIAG_N), slice(i * LANE, (i + 1) * LANE)
            dsr_ref[:, rows] = _dot_nt(dy0[:, cols], cre_ref[rows, cols]).astype(BF16)
            dsi_ref[:, rows] = (-_dot_nt(dy0[:, cols], cim_ref[rows, cols])).astype(BF16)

        uv = p_ref[:, SSM_W:lg0].astype(F32)
        uvg, gg = _gelu_and_grad(uv)
        u2 = uvg[:, :SGU_W]
        rv, vh = _rms_stats(uvg[:, SGU_W:])
        v3 = vh * gs_ref[...]
        mixed = mx_ref[...].astype(F32)
        dsgu = jnp.zeros((TB, SGU_W), F32)
        for k in range(N_CHIP):
            dsgu = dsgu + _dot_nt(dyb[:, k * pc:(k + 1) * pc], wpb_ref[k])
        sgu_ref[...] = (u2 * mixed).astype(BF16)
        du2 = dsgu * mixed
        dmix = dsgu * u2
        lane_lo = lax.broadcasted_iota(jnp.int32, (CHUNK, LANE), 1) < SGU_D
        dv3 = _sgu_mix(dmix, wst_ref, lane_lo)
        dbm = jnp.zeros((CHUNK, SGU_W), F32)
        for c0 in range(0, TB, CHUNK):
            dbm = dbm + dmix[c0:c0 + CHUNK]
        for j in range(n_slab):
            lo = jnp.zeros((CHUNK, CHUNK), F32)
            hi = jnp.zeros((CHUNK, CHUNK), F32)
            for c0 in range(0, TB, CHUNK):
                dsl = dmix[c0:c0 + CHUNK, j * LANE:(j + 1) * LANE]
                vsl = v3[c0:c0 + CHUNK, j * LANE:(j + 1) * LANE]
                lo = lo + _dot_nt(jnp.where(lane_lo, dsl, 0.0), vsl)
                hi = hi + _dot_nt(jnp.where(lane_lo, 0.0, dsl), vsl)

            @pl.when(first)
            def _(lo=lo, hi=hi, j=j):
                dws_ref[2 * j] = lo
                dws_ref[2 * j + 1] = hi

            @pl.when(jnp.logical_not(first))
            def _(lo=lo, hi=hi, j=j):
                dws_ref[2 * j] += lo
                dws_ref[2 * j + 1] += hi

        dv2 = _rms_bwd(dv3 * gs_ref[...], vh, rv)
        drest_ref[:, 0:SGU_W] = (du2 * gg[:, :SGU_W]).astype(BF16)
        drest_ref[:, SGU_W:2 * SGU_W] = (dv2 * gg[:, SGU_W:]).astype(BF16)

        upd = jnp.concatenate([jnp.sum(dy0 * u, axis=0, keepdims=True),
                               jnp.sum(dz, axis=0, keepdims=True),
                               jnp.sum(dv3 * vh, axis=0, keepdims=True),
                               jnp.zeros((SUBLANE - 3, SSM_W), F32)], axis=0)

        @pl.when(first)
        def _():
            sm_ref[...] = upd
            dbm_ref[...] = dbm

        @pl.when(jnp.logical_not(first))
        def _():
            sm_ref[...] += upd
            dbm_ref[...] += dbm

    rest = 2 * SGU_W + 2 * D_MODEL
    bf_d, bf_s = jax.ShapeDtypeStruct((t_len, D_MODEL), BF16), jax.ShapeDtypeStruct((t_len, SSM_W), BF16)
    return _call(
        body, "bwd_mix", (t_len // TB,),
        [_rows(TB, D_MODEL), _rows(TB, IN_COLS), _rows(TB, SSM_W), _rows(TB, SSM_W),
         _rows(TB, SGU_W), _rows(TB, D_MODEL), _rows(TB, D_MODEL)] + [_whole()] * 10,
        [_rows(TB, N_STATE), _rows(TB, N_STATE), _rows(TB, SSM_W), _rows(TB, rest),
         _rows(TB, D_MODEL), _rows(TB, D_MODEL), _rows(TB, D_MODEL), _rows(TB, SSM_W),
         _rows(TB, SSM_W), _rows(TB, SSM_W), _rows(TB, SGU_W), _rows(TB, SSM_W),
         _acc(SUBLANE, SSM_W), _acc(CHUNK, SGU_W),
         pl.BlockSpec((SGU_G, CHUNK, CHUNK), lambda i: (0, 0, 0))],
        [jax.ShapeDtypeStruct((t_len, N_STATE), BF16), jax.ShapeDtypeStruct((t_len, N_STATE), BF16),
         jax.ShapeDtypeStruct((t_len, SSM_W), F32), jax.ShapeDtypeStruct((t_len, rest), BF16),
         bf_d, bf_d, bf_d, bf_s, bf_s, bf_s, bf_s, bf_s,
         jax.ShapeDtypeStruct((SUBLANE, SSM_W), F32), jax.ShapeDtypeStruct((CHUNK, SGU_W), F32),
         jax.ShapeDtypeStruct((SGU_G, CHUNK, CHUNK), F32)],
        [dx2, p, y0, z, mixed, ya, yb, w_out, w_pa, w_pb, w_glu, cre, cim, ws_st, wst_st, d_skip,
         g_sgu], jobs)


def _scan_bwd(dsr, dsi, str_, sti, tab_rev, jobs=()):
    t_len = dsr.shape[0]
    nblk = t_len // SUBLANE
    lb = SCAN_LANES

    def body(dr_ref, di_ref, sr_ref, si_ref, tab_ref, lr_ref, li_ref, dar_ref, dai_ref):
        tab_v = [tab_ref[q] for q in range(8)]
        row0 = lax.broadcasted_iota(jnp.int32, (SUBLANE, lb), 0) == 0
        tile = BF16_TILE

        def step(k, carry):
            cr, ci, acr, aci = carry
            base = pl.multiple_of((nblk - (k + 1) * SCAN_UNROLL) * SUBLANE, SCAN_UNROLL * SUBLANE)
            state = _load_blocks(sr_ref, si_ref, base)
            before = pl.ds(pl.multiple_of(jnp.maximum(base - tile, 0), tile), tile)
            has_before = jnp.where(base > 0, 1.0, 0.0)
            prev = (sr_ref[before, :].astype(F32)[tile - 1:tile] * has_before,
                    si_ref[before, :].astype(F32)[tile - 1:tile] * has_before)
            local = [_scan_local(xr, xi, tab_v, (7, 6, 4))
                     for xr, xi in _load_blocks(dr_ref, di_ref, base)]
            lam = [None] * SCAN_UNROLL
            for b in reversed(range(SCAN_UNROLL)):
                xr, xi = _scan_carry(*local[b], tab_v, cr, ci)
                lam[b] = (xr, xi)
                cr, ci = xr[0:1, :], xi[0:1, :]
                pr, pi = prev if b == 0 else (state[b - 1][0][SUBLANE - 1:], state[b - 1][1][SUBLANE - 1:])
                s_r = jnp.where(row0, pr, pltpu.roll(state[b][0], 1, 0))
                s_i = jnp.where(row0, pi, pltpu.roll(state[b][1], 1, 0))
                acr = acr + xr * s_r + xi * s_i
                aci = aci + xi * s_r - xr * s_i
            _store_blocks(lr_ref, li_ref, base, lam)
            return cr, ci, acr, aci

        zero = jnp.zeros((1, lb), F32)
        zacc = jnp.zeros((SUBLANE, lb), F32)
        _, _, acr, aci = lax.fori_loop(0, nblk // SCAN_UNROLL, step, (zero, zero, zacc, zacc))
        dar_ref[...] = acr
        dai_ref[...] = aci

    col = pl.BlockSpec((t_len, lb), lambda j: (0, j))
    small = pl.BlockSpec((SUBLANE, lb), lambda j: (0, j))
    return _call(
        body, "scan_bwd", (N_STATE // lb,),
        [col, col, col, col, pl.BlockSpec((8, SUBLANE, lb), lambda j: (0, 0, j))],
        [col, col, small, small],
        [jax.ShapeDtypeStruct((t_len, N_STATE), BF16)] * 2
        + [jax.ShapeDtypeStruct((SUBLANE, N_STATE), F32)] * 2,
        [dsr, dsi, str_, sti, tab_rev], jobs)


def _bwd_in(lam_r, lam_i, du_part, drest, x, dx2, g_mix, w_in, bre, bim, jobs=()):
    t_len = x.shape[0]
    cs = IN_COLS // N_CHIP

    def body(lr_ref, li_ref, du_ref, dr_ref, x_ref, dx2_ref, g_ref, w_ref, bre_ref, bim_ref,
             gx_ref, dp_ref, sm_ref):
        i = pl.program_id(0)
        du = du_ref[...] + jnp.concatenate(
            [_dot_nt(lr_ref[:, i * DIAG_N:(i + 1) * DIAG_N],
                     bre_ref[i * LANE:(i + 1) * LANE, i * DIAG_N:(i + 1) * DIAG_N])
             + _dot_nt(li_ref[:, i * DIAG_N:(i + 1) * DIAG_N],
                       bim_ref[i * LANE:(i + 1) * LANE, i * DIAG_N:(i + 1) * DIAG_N])
             for i in range(SSM_W // LANE)], axis=1)
        dp_ref[:, 0:SSM_W] = du.astype(BF16)
        dp_ref[:, SSM_W:] = dr_ref[...]
        dh = jnp.zeros((TB, D_MODEL), F32)
        for k in range(N_CHIP):
            dh = dh + _dot_nt(dp_ref[:, k * cs:(k + 1) * cs], w_ref[k])
        r, xh = _rms_stats(x_ref[...])
        gx_ref[...] = dx2_ref[...] + _rms_bwd(dh * g_ref[...], xh, r)
        upd = jnp.concatenate([jnp.sum(dh * xh, axis=0, keepdims=True),
                               jnp.zeros((SUBLANE - 1, D_MODEL), F32)], axis=0)

        @pl.when(i == 0)
        def _():
            sm_ref[...] = upd

        @pl.when(i > 0)
        def _():
            sm_ref[...] += upd

    return _call(
        body, "bwd_in", (t_len // TB,),
        [_rows(TB, N_STATE), _rows(TB, N_STATE), _rows(TB, SSM_W), _rows(TB, IN_COLS - SSM_W),
         _rows(TB, D_MODEL), _rows(TB, D_MODEL)] + [_whole()] * 4,
        [_rows(TB, D_MODEL), _rows(TB, IN_COLS), _acc(SUBLANE, D_MODEL)],
        [jax.ShapeDtypeStruct((t_len, D_MODEL), F32), jax.ShapeDtypeStruct((t_len, IN_COLS), BF16),
         jax.ShapeDtypeStruct((SUBLANE, D_MODEL), F32)],
        [lam_r, lam_i, du_part, drest, x, dx2, g_mix, w_in, bre, bim], jobs)


def _matmul_tn(a, b, name, out_shape, grid_ij, a_blk, a_map, b_blk, b_map, o_blk, o_map, jobs=()):
    tk = a_blk[0]
    nk = a.shape[0] // tk
    assert nk * tk == a.shape[0] and nk > 0

    def body(a_ref, b_ref, o_ref, acc_ref):
        k = pl.program_id(2)

        @pl.when(k == 0)
        def _():
            acc_ref[...] = jnp.zeros_like(acc_ref)

        acc_ref[...] += lax.dot_general(a_ref[...].astype(BF16), b_ref[...].astype(BF16),
                                        (((0,), (0,)), ((), ())), preferred_element_type=F32)

        @pl.when(k == nk - 1)
        def _():
            o_ref[...] = acc_ref[...]

    outs, per_job = _call(
        body, name, (grid_ij[0], grid_ij[1], nk),
        [pl.BlockSpec(a_blk, a_map), pl.BlockSpec(b_blk, b_map)], [pl.BlockSpec(o_blk, o_map)],
        [jax.ShapeDtypeStruct(out_shape, F32)], [a, b], jobs,
        scratch=[pltpu.VMEM((a_blk[1], b_blk[1]), F32)])
    return outs[0], per_job


def _dw_shards(a, b, name, tk, jobs=()):
    m, n = a.shape[1], b.shape[1]
    tn = n // N_CHIP
    tk = min(tk, a.shape[0])
    return _matmul_tn(a, b, name, (N_CHIP, m, tn), (1, N_CHIP),
                      (tk, m), lambda i, j, k: (k, 0), (tk, tn), lambda i, j, k: (k, j),
                      (None, m, tn), lambda i, j, k: (j, 0, 0), jobs)


def _dw_rows(a, b, name, tm, tk):
    m, n = a.shape[1], b.shape[1]
    tk = min(tk, a.shape[0])
    return _matmul_tn(a, b, name, (m, n), (m // tm, 1),
                      (tk, tm), lambda i, j, k: (k, i), (tk, n), lambda i, j, k: (k, 0),
                      (tm, n), lambda i, j, k: (i, 0))[0]


def _dw_cols(a, b, name, tn, sharded, jobs=()):
    t_len, m = a.shape
    n = b.shape[1]

    def body(a_ref, b_ref, o_ref):
        o_ref[...] = lax.dot_general(a_ref[...].astype(BF16), b_ref[...].astype(BF16),
                                     (((0,), (0,)), ((), ())), preferred_element_type=F32)

    if sharded:
        o_spec, o_shape = pl.BlockSpec((None, m, tn), lambda j: (j, 0, 0)), (n // tn, m, tn)
    else:
        o_spec, o_shape = pl.BlockSpec((m, tn), lambda j: (0, j)), (m, n)
    outs, per_job = _call(body, name, (n // tn,),
                          [_whole(), pl.BlockSpec((t_len, tn), lambda j: (0, j))], [o_spec],
                          [jax.ShapeDtypeStruct(o_shape, F32)], [a, b], jobs)
    return outs[0], per_job


def _dw_pair(a, m, b1, b2, name, jobs=()):
    t_len = a.shape[0]
    n_slab = DIAG_N // LANE
    rows_per_slab = LANE // n_slab

    def body(a_ref, b1_ref, b2_ref, o1_ref, o2_ref):
        for b_ref, o_ref in ((b1_ref, o1_ref), (b2_ref, o2_ref)):
            prod = lax.dot_general(a_ref[...].astype(BF16), b_ref[...].astype(BF16),
                                   (((0,), (0,)), ((), ())), preferred_element_type=F32)
            for j in range(n_slab):
                rows = slice(j * rows_per_slab, (j + 1) * rows_per_slab)
                o_ref[rows, :] = prod[rows, j * LANE:(j + 1) * LANE]

    tok = pl.BlockSpec((t_len, DIAG_N), lambda i: (0, i))
    out = pl.BlockSpec((LANE, LANE), lambda i: (i, 0))
    return _call(body, name, (m // LANE,),
                 [pl.BlockSpec((t_len, LANE), lambda i: (0, i)), tok, tok], [out, out],
                 [jax.ShapeDtypeStruct((m, LANE), F32)] * 2, [a, b1, b2], jobs)


def _prefetch_call(body, name, grid, scalars, in_specs, out_specs, out_shape, args):
    return pl.pallas_call(
        body, name=name,
        grid_spec=pltpu.PrefetchScalarGridSpec(num_scalar_prefetch=1, grid=grid, in_specs=in_specs,
                                               out_specs=out_specs),
        out_shape=out_shape, compiler_params=_params(len(grid)),
    )(scalars, *args)


def _place_shard(w, where, name, dtype, tr):
    rows, cols = w.shape

    def body(s_ref, w_ref, o_ref):
        o_ref[...] = w_ref[...].astype(dtype)

    return _prefetch_call(
        body, name, (rows // tr,), where,
        [pl.BlockSpec((tr, cols), lambda i, s: (i, 0))],
        pl.BlockSpec((None, tr, cols), lambda i, s: (s[0], i, 0)),
        jax.ShapeDtypeStruct((N_CHIP, rows, cols), dtype), [w])


def _add_sibling(g, got, where, name):
    _, rs, cs = g.shape
    hr = rs // 2

    def body(s_ref, g_ref, got_ref, o_ref):
        o_ref[...] = (g_ref[...] + got_ref[...]).astype(BF16)

    return _prefetch_call(
        body, name, (N_CHIP,), where,
        [pl.BlockSpec((None, hr, cs), lambda k, s: (k, s[1], 0)),
         pl.BlockSpec((None, hr, cs), lambda k, s: (k, 0, 0))],
        pl.BlockSpec((None, hr, cs), lambda k, s: (k, 0, 0)),
        jax.ShapeDtypeStruct((N_CHIP, hr, cs), BF16), [g, got])


def _add_chips(sums, got, where, name):
    _, hr, cs = sums.shape

    def body(s_ref, own_ref, got_ref, o_ref):
        o_ref[...] = ((own_ref[...].astype(F32) + got_ref[0].astype(F32))
                      + got_ref[1].astype(F32)) + got_ref[2].astype(F32)

    return _prefetch_call(
        body, name, (1,), where,
        [pl.BlockSpec((None, hr, cs), lambda i, s: (s[0], 0, 0)),
         pl.BlockSpec((3, hr, cs), lambda i, s: (0, 0, 0))],
        pl.BlockSpec((hr, cs), lambda i, s: (s[1], 0)),
        jax.ShapeDtypeStruct((2 * hr, cs), F32), [sums, got])


def _small_allreduce(pack):
    rows = pack.shape[0]
    half = rows // 2

    def body(in_ref, out_ref, sib_ref, slots_ref, s_a, r_a, s_b, r_b, s_c, r_c):
        x, y, c, chips = _place()
        k_me = 2 * x + y
        sib = (x, y, 1 - c)
        first = _remote(in_ref, sib_ref, s_a, r_a, sib)
        first.start()
        first.wait()
        mine = _half(rows, c)
        slots_ref[k_me] = in_ref[mine, :] + sib_ref[mine, :]
        cps = [_remote(slots_ref.at[k_me], slots_ref.at[k_me], s_b.at[j], r_b.at[j], (*ch, c))
               for j, ch in enumerate(chips)]
        for cp in cps:
            cp.start()
        for j, ch in enumerate(chips):
            slot = slots_ref.at[_chip_index(ch)]
            _remote(slot, slot, s_b.at[j], r_b.at[j], (*ch, c)).wait_recv()
        for cp in cps:
            cp.wait_send()
        out_ref[mine, :] = ((slots_ref[0] + slots_ref[1]) + slots_ref[2]) + slots_ref[3]
        last = _remote(out_ref.at[mine, :], out_ref.at[mine, :], s_c, r_c, sib)
        last.start()
        theirs = out_ref.at[_half(rows, 1 - c), :]
        _remote(theirs, theirs, s_c, r_c, sib).wait_recv()
        last.wait_send()

    return pl.pallas_call(
        body, name="small_allreduce", in_specs=[_whole()], out_specs=_whole(),
        out_shape=jax.ShapeDtypeStruct(pack.shape, F32),
        scratch_shapes=[pltpu.VMEM(pack.shape, F32), pltpu.VMEM((N_CHIP, half, LANE), F32),
                        pltpu.SemaphoreType.DMA, pltpu.SemaphoreType.DMA,
                        pltpu.SemaphoreType.DMA((3,)), pltpu.SemaphoreType.DMA((3,)),
                        pltpu.SemaphoreType.DMA, pltpu.SemaphoreType.DMA],
        compiler_params=_params(0),
    )(pack)


def _adamw_update(w_ref, g_ref, m_ref, v_ref, d_ref, mo_ref, vo_ref):
    gv = g_ref[...]
    mn = ADAM_B1 * m_ref[...] + (1.0 - ADAM_B1) * gv
    vn = ADAM_B2 * v_ref[...] + (1.0 - ADAM_B2) * (gv * gv)
    mo_ref[...] = mn
    vo_ref[...] = vn
    m_hat = mn / (1.0 - ADAM_B1 ** ADAM_STEP)
    v_hat = vn / (1.0 - ADAM_B2 ** ADAM_STEP)
    d_ref[...] = -ADAM_LR * (m_hat / (jnp.sqrt(v_hat) + ADAM_EPS) + ADAM_WD * w_ref[...])


def _adamw(w, g, m, v, name, tr):
    rows, cols = w.shape
    blk = _rows(tr, cols)
    return _call(_adamw_update, name, (rows // tr,), [blk] * 4, [blk] * 3,
                 [jax.ShapeDtypeStruct(w.shape, F32)] * 3, [w, g, m, v])[0]


def _adamw_many(ws, gs, ms, vs, name):
    n = len(ws)

    def body(*refs):
        for t in range(n):
            _adamw_update(*[refs[q * n + t] for q in range(7)])

    specs = [pl.BlockSpec(a.shape, lambda i, nd=a.ndim: (0,) * nd) for a in ws]
    outs = pl.pallas_call(
        body, name=name, grid=(1,), in_specs=specs * 4, out_specs=specs * 3,
        out_shape=[jax.ShapeDtypeStruct(a.shape, F32) for _ in range(3) for a in ws],
        compiler_params=_params(1),
    )(*ws, *gs, *ms, *vs)
    return outs[:n], outs[n:2 * n], outs[2 * n:]


def _ssm_discretize(a_re, a_im, log_dt, b_re, b_im):
    dt = jnp.exp(log_dt)[:, None]
    mag = jnp.exp(dt * a_re)
    abr = mag * jnp.cos(dt * a_im)
    abi = mag * jnp.sin(dt * a_im)
    den = a_re * a_re + a_im * a_im
    nr = abr - 1.0
    ni = abi
    f_re = (nr * a_re + ni * a_im) / den
    f_im = (ni * a_re - nr * a_im) / den
    bbr = f_re[..., None] * b_re - f_im[..., None] * b_im
    bbi = f_re[..., None] * b_im + f_im[..., None] * b_re
    return abr, abi, bbr, bbi


def _scan_tables(abr, abi):
    ar = abr.reshape(1, N_STATE)
    ai = abi.reshape(1, N_STATE)
    pr, pi = [ar], [ai]
    for _ in range(SUBLANE - 1):
        pr, pi = pr + [pr[-1] * ar - pi[-1] * ai], pi + [pr[-1] * ai + pi[-1] * ar]
    row = jnp.arange(SUBLANE)[:, None]
    tabs = []
    for d in (1, 2, 4):
        tabs.append(jnp.where(row >= d, pr[d - 1], 0.0))
        tabs.append(jnp.where(row >= d, pi[d - 1], 0.0))
    tabs.append(jnp.concatenate(pr, axis=0))
    tabs.append(jnp.concatenate(pi, axis=0))
    fwd = jnp.stack(tabs)
    sign = jnp.array([1.0, -1.0] * 4, F32)[:, None, None]
    return fwd, fwd[:, ::-1, :] * sign


def _block_diag_b(bb):
    strip = bb.transpose(2, 0, 1).reshape(SSM_H, N_STATE)
    rows = lax.broadcasted_iota(jnp.int32, (SSM_W, N_STATE), 0) // SSM_H
    cols = lax.broadcasted_iota(jnp.int32, (SSM_W, N_STATE), 1) // SSM_P
    return jnp.where(rows == cols, jnp.tile(strip, (SSM_G, 1)), 0.0).astype(BF16)


def _block_diag_c(cc):
    strip = cc.transpose(0, 2, 1).reshape(N_STATE, SSM_H)
    rows = lax.broadcasted_iota(jnp.int32, (N_STATE, SSM_W), 0) // SSM_P
    cols = lax.broadcasted_iota(jnp.int32, (N_STATE, SSM_W), 1) // SSM_H
    return jnp.where(rows == cols, jnp.tile(strip, (1, SSM_G)), 0.0).astype(BF16)


SMALL_SHAPES = {
    "g_mix": (D_MODEL,), "a_re": (SSM_G, SSM_P), "a_im": (SSM_G, SSM_P), "log_dt": (SSM_G,),
    "b_re": (SSM_G, SSM_P, SSM_H), "b_im": (SSM_G, SSM_P, SSM_H),
    "c_re": (SSM_G, SSM_H, SSM_P), "c_im": (SSM_G, SSM_H, SSM_P),
    "d_skip": (SSM_W,), "b_glu": (SSM_W,), "g_sgu": (SGU_W,), "w_s": (SGU_G, CHUNK, CHUNK),
    "b_s": (SGU_G, CHUNK), "g_ffn": (D_MODEL,), "conv_b": (2 * D_FF,), "g_final": (D_MODEL,),
}
PACK_ITEMS = [("loss", (1,))] + [(n, SMALL_SHAPES[n]) for n in SMALL] + [("conv_w", (3, 2 * D_FF))]
TILE = SUBLANE * LANE


def _item_rows(shape):
    return -(-math.prod(shape) // TILE) * SUBLANE


PACK_ROWS = -(-sum(_item_rows(s) for _, s in PACK_ITEMS) // (2 * SUBLANE)) * (2 * SUBLANE)


def _pack(values):
    parts, used = [], 0
    for name, shape in PACK_ITEMS:
        size, rows = math.prod(shape), _item_rows(shape)
        if name in values:
            flat = values[name].astype(F32).reshape(size)
            if rows * LANE > size:
                flat = jnp.pad(flat, (0, rows * LANE - size))
            parts.append(flat.reshape(rows, LANE))
        else:
            parts.append(jnp.zeros((rows, LANE), F32))
        used += rows
    if PACK_ROWS > used:
        parts.append(jnp.zeros((PACK_ROWS - used, LANE), F32))
    return jnp.concatenate(parts, axis=0)


def _unpack(pack):
    out, off = {}, 0
    for name, shape in PACK_ITEMS:
        rows = _item_rows(shape)
        out[name] = pack[off:off + rows].reshape(rows * LANE)[:math.prod(shape)].reshape(shape)
        off += rows
    return out


PLACE_ROWS = {"w_in": 256, "w_up": 256, "w_down": 352, "w_out": 256, "w_proj_a": 256,
              "w_proj_b": 256, "w_glu": 128}


def kernel(x, g_mix, w_in, a_re, a_im, log_dt, b_re, b_im, c_re, c_im, d_skip, w_glu, b_glu, w_proj_a, g_sgu, w_s, b_s, w_proj_b, w_out, g_ffn, w_up, conv_w, conv_b, w_down, g_final, loss_target, m_g_mix, m_w_in, m_a_re, m_a_im, m_log_dt, m_b_re, m_b_im, m_c_re, m_c_im, m_d_skip, m_w_glu, m_b_glu, m_w_proj_a, m_g_sgu, m_w_s, m_b_s, m_w_proj_b, m_w_out, m_g_ffn, m_w_up, m_conv_w, m_conv_b, m_w_down, m_g_final, v_g_mix, v_w_in, v_a_re, v_a_im, v_log_dt, v_b_re, v_b_im, v_c_re, v_c_im, v_d_skip, v_w_glu, v_b_glu, v_w_proj_a, v_g_sgu, v_w_s, v_b_s, v_w_proj_b, v_w_out, v_g_ffn, v_w_up, v_conv_w, v_conv_b, v_w_down, v_g_final):
    given = dict(locals())
    w = {n: given[n] for n in WEIGHTS}
    m = {n: given["m_" + n] for n in WEIGHTS}
    v = {n: given["v_" + n] for n in WEIGHTS}

    def shard2d(a):
        return a.reshape(a.shape[-2], a.shape[-1])

    chip = 2 * lax.axis_index("x") + lax.axis_index("y")
    where = jnp.stack([chip, lax.axis_index("c")]).astype(jnp.int32)
    xs, target = x[0], loss_target[0]
    small = {n: w[n].reshape(SMALL_SHAPES[n]) for n in SMALL}

    (abr, abi, bbr, bbi), disc_vjp = jax.vjp(_ssm_discretize, small["a_re"], small["a_im"],
                                             small["log_dt"], small["b_re"], small["b_im"])
    tab_f, tab_r = _scan_tables(abr, abi)
    bre = _block_diag_b(bbr)
    bim = _block_diag_b(bbi)
    cre = _block_diag_c(small["c_re"])
    cim = _block_diag_c(small["c_im"])
    tril = jnp.tril(jnp.ones((CHUNK, CHUNK), dtype=bool))
    ws = jnp.where(tril[None], small["w_s"], 0.0)
    ws_st = ws.reshape(SGU_G // 2, 2 * CHUNK, CHUNK).astype(BF16)
    wst_st = ws.transpose(0, 2, 1).reshape(SGU_G // 2, 2 * CHUNK, CHUNK).astype(BF16)
    bmat = jnp.repeat(small["b_s"].T, SGU_D, axis=1)
    g_mix2 = small["g_mix"].reshape(1, D_MODEL)
    g_ffn2 = small["g_ffn"].reshape(1, D_MODEL)
    g_final2 = small["g_final"].reshape(1, D_MODEL)
    g_sgu2 = small["g_sgu"].reshape(1, SGU_W)
    d_skip2 = small["d_skip"].reshape(1, SSM_W)
    b_glu2 = small["b_glu"].reshape(1, SSM_W)
    conv_b2 = small["conv_b"].reshape(1, 2 * D_FF)

    gat = {n: _place_shard(shard2d(w[n]), where, "place_" + n, BF16, PLACE_ROWS[n]) for n in BIG}
    gat["conv_w"] = _place_shard(shard2d(w["conv_w"]), where, "place_conv_w", F32, 3)
    (gat["w_in"],), = _comm("gather_in_ici", [_job_gather_ici([gat["w_in"]])])
    (gat["w_in"],), = _comm("gather_in_sibling", [_job_gather_sibling([gat["w_in"]])])
    early = ["w_glu", "w_proj_a", "w_proj_b", "w_out", "w_down", "conv_w"]

    (p, h1, bur, bui), (got,) = _fwd_in(
        xs, g_mix2, gat["w_in"], bre, bim,
        [_job_gather_ici([gat[n] for n in early], whole=(5,))])
    gat.update(zip(early, got))
    (str_, sti), (got_e, (gat["w_up"],)) = _scan_fwd(
        bur, bui, tab_f,
        [_job_gather_sibling([gat[n] for n in early[:5]]), _job_gather_ici([gat["w_up"]])])
    gat.update(zip(early[:5], got_e))
    w_glu_f = gat["w_glu"].reshape(SSM_W, SSM_W)
    w_out_f = gat["w_out"].reshape(D_MODEL, D_MODEL)
    conv_w_f = gat["conv_w"].transpose(1, 0, 2).reshape(3, 2 * D_FF)
    (x2, y0, z, mixed, ya, yb), ((gat["w_up"],),) = _fwd_mix(
        xs, p, str_, sti, cre, cim, d_skip2, w_glu_f, b_glu2, gat["w_proj_a"], g_sgu2, ws_st, bmat,
        gat["w_proj_b"], w_out_f, [_job_gather_sibling([gat["w_up"]])])
    w_down_f = gat["w_down"].reshape(D_FF, D_MODEL)
    up, act, f, h2, dx3, sm_ffn = _fwd_ffn(x2, target, g_ffn2, gat["w_up"], conv_w_f, conv_b2,
                                           w_down_f, g_final2)

    def leg1_done(names, got):
        return [_add_sibling(part[n], s, where, "add_sibling_" + n) for n, s in zip(names, got)]

    def leg2_done(names, sums, got):
        return [_add_chips(s, o, where, "add_chips_" + n) for n, s, o in zip(names, sums, got)]

    part, red = {}, {}
    part["w_down"] = _dw_rows(f, dx3, "dw_down", D_FF // 2, 2 * TK).reshape(
        N_CHIP, D_FF // N_CHIP, D_MODEL)
    (dx2, dup, sm_conv, sm_gffn), (got,) = _bwd_ffn(
        dx3, up, act, x2, g_ffn2, gat["w_up"], conv_w_f, w_down_f,
        [_job_sibling_halves([part["w_down"]])])
    sum_down = leg1_done(["w_down"], got)
    part["w_up"], (got,) = _dw_shards(h2, dup, "dw_up", 4 * TK, [_job_to_owner(sum_down)])
    red_down = leg2_done(["w_down"], sum_down, got)
    ((dsr, dsi, du_part, drest, mrg, dya, dyb, yap, dz, y1, sgu, dy0, sm_mix, dbm, dws),
     (got, (red["w_down"],))) = _bwd_mix(
        dx2, p, y0, z, mixed, ya, yb, w_out_f, gat["w_proj_a"], gat["w_proj_b"], w_glu_f, cre, cim,
        ws_st, wst_st, d_skip2, g_sgu2,
        [_job_sibling_halves([part["w_up"]]), _job_swap_halves(red_down)])
    sum_up = leg1_done(["w_up"], got)
    (lam_r, lam_i, dar8, dai8), (got,) = _scan_bwd(dsr, dsi, str_, sti, tab_r, [_job_to_owner(sum_up)])
    red_up = leg2_done(["w_up"], sum_up, got)
    mix4 = ["w_out", "w_proj_a", "w_proj_b", "w_glu"]
    part["w_out"] = _dw_cols(mrg, dx2, "dw_out", D_MODEL // 2, False)[0].reshape(
        N_CHIP, D_MODEL // N_CHIP, D_MODEL)
    part["w_proj_a"] = _dw_cols(yap, dya, "dw_proj_a", D_MODEL // N_CHIP, True)[0]
    part["w_proj_b"] = _dw_cols(sgu, dyb, "dw_proj_b", D_MODEL // N_CHIP, True)[0]
    part["w_glu"] = _dw_cols(y1, dz, "dw_glu", SSM_W, False)[0].reshape(
        N_CHIP, SSM_W // N_CHIP, SSM_W)
    (grad_x, dp, sm_gmix), (got, (red["w_up"],)) = _bwd_in(
        lam_r, lam_i, du_part, drest, xs, dx2, g_mix2, gat["w_in"], bre, bim,
        [_job_sibling_halves([part[n] for n in mix4]), _job_swap_halves(red_up)])
    sums_m = leg1_done(mix4, got)
    part["w_in"], (got,) = _dw_cols(h1, dp, "dw_in", IN_COLS // N_CHIP, True, [_job_to_owner(sums_m)])
    red_m = leg2_done(mix4, sums_m, got)
    (dbd_r, dbd_i), (got, done_m) = _dw_pair(
        p, SSM_W, lam_r, lam_i, "db_bar",
        [_job_sibling_halves([part["w_in"]]), _job_swap_halves(red_m)])
    red.update(zip(mix4, done_m))
    sum_in = leg1_done(["w_in"], got)
    (dcd_r, dcd_i), (got,) = _dw_pair(dy0, SSM_W, str_, sti, "dc", [_job_to_owner(sum_in)])
    red_in = leg2_done(["w_in"], sum_in, got)
    (red["w_in"],), = _comm("swap_w_in", [_job_swap_halves(red_in)])

    def pick_c(slabs):
        two = LANE // SSM_P
        return jnp.einsum("jshsp->jshp", slabs.reshape(SSM_G // two, two, SSM_H, two, SSM_P)
                          ).reshape(SSM_G, SSM_H, SSM_P)

    def pick_b(slabs):
        return pick_c(slabs).transpose(0, 2, 1)

    dabr = jnp.sum(dar8, axis=0).reshape(SSM_G, SSM_P)
    dabi = jnp.sum(dai8, axis=0).reshape(SSM_G, SSM_P)
    d_a_re, d_a_im, d_log_dt, d_b_re, d_b_im = disc_vjp((dabr, dabi, pick_b(dbd_r), pick_b(dbd_i)))
    gsmall = {
        "g_mix": sm_gmix[0], "a_re": d_a_re, "a_im": d_a_im, "log_dt": d_log_dt,
        "b_re": d_b_re, "b_im": d_b_im, "c_re": pick_c(dcd_r), "c_im": -pick_c(dcd_i),
        "d_skip": sm_mix[0], "b_glu": sm_mix[1], "g_sgu": sm_mix[2],
        "w_s": jnp.where(tril[None], dws, 0.0),
        "b_s": dbm.reshape(CHUNK, SGU_G, SGU_D).sum(-1).T,
        "g_ffn": sm_gffn[0], "conv_b": sm_conv[3], "g_final": sm_ffn[0],
        "conv_w": sm_conv[0:3], "loss": sm_ffn[1, 0:1],
    }

    total_pack = _small_allreduce(_pack(gsmall))
    total = _unpack(total_pack)
    grads = dict(red)
    cs = 2 * D_FF // N_CHIP
    grads["conv_w"] = lax.dynamic_slice(total["conv_w"], (0, chip * cs), (3, cs))
    delta, new_m, new_v = {}, {}, {}
    for n in BIG + ("conv_w",):
        delta[n], new_m[n], new_v[n] = _adamw(shard2d(w[n]), grads[n], shard2d(m[n]), shard2d(v[n]),
                                              "adamw_" + n, PLACE_ROWS.get(n, 3))
    for n in SMALL:
        grads[n] = total[n].reshape(w[n].shape)
    ud, um, uv = _adamw_many(*[[d[n] for n in SMALL] for d in (w, grads, m, v)], "adamw_small")
    for i, n in enumerate(SMALL):
        delta[n], new_m[n], new_v[n] = ud[i], um[i], uv[i]

    def like(d):
        return [d[n].reshape(w[n].shape) for n in WEIGHTS]

    return (total["loss"].reshape(()), grad_x.reshape(x.shape), *like(grads), *like(delta),
            *like(new_m), *like(new_v))
```

```python
import math

import jax
import jax.numpy as jnp
from jax import lax
from jax.experimental import pallas as pl
from jax.experimental.pallas import tpu as pltpu

F32 = jnp.float32
BF16 = jnp.bfloat16
MESH = pl.DeviceIdType.MESH

D_MODEL = 1024
SSM_W = 512
SSM_G = 32
SSM_H = 16
SSM_P = 64
N_STATE = SSM_G * SSM_P
DIAG_N = 128 * SSM_P // SSM_H
SGU_W = 512
SGU_G = 8
SGU_D = 64
CHUNK = 128
D_FF = 2816
IN_COLS = 3584
EPS = 1e-6
N_CHIP = 4

ADAM_LR = 0.001
ADAM_B1 = 0.9
ADAM_B2 = 0.999
ADAM_EPS = 1e-08
ADAM_WD = 0.01
ADAM_STEP = 10

SUBLANE = 8
LANE = 128
VMEM_LIMIT = 56 * 1024 * 1024
TB = 256
TK = 512
SCAN_LANES = 256
SCAN_UNROLL = 4
HALO = SUBLANE

BIG = ("w_in", "w_up", "w_down", "w_out", "w_proj_a", "w_proj_b", "w_glu")
SMALL = ("g_mix", "a_re", "a_im", "log_dt", "b_re", "b_im", "c_re", "c_im", "d_skip", "b_glu",
         "g_sgu", "w_s", "b_s", "g_ffn", "conv_b", "g_final")
WEIGHTS = ("g_mix", "w_in", "a_re", "a_im", "log_dt", "b_re", "b_im", "c_re", "c_im", "d_skip",
           "w_glu", "b_glu", "w_proj_a", "g_sgu", "w_s", "b_s", "w_proj_b", "w_out", "g_ffn",
           "w_up", "conv_w", "conv_b", "w_down", "g_final")

ANY = pl.BlockSpec(memory_space=pl.ANY)


def _params(n_grid):
    return pltpu.CompilerParams(dimension_semantics=("arbitrary",) * n_grid if n_grid else None,
                                vmem_limit_bytes=VMEM_LIMIT)


def _whole():
    return pl.BlockSpec(memory_space=pltpu.VMEM)


def _rows(tb, ncol):
    return pl.BlockSpec((tb, ncol), lambda i: (i, 0))


def _acc(nrow, ncol):
    return pl.BlockSpec((nrow, ncol), lambda i: (0, 0))


def _dot(a, b):
    return jnp.dot(a.astype(BF16), b.astype(BF16), preferred_element_type=F32)


def _dot_nt(a, b):
    return lax.dot_general(a.astype(BF16), b.astype(BF16), (((1,), (1,)), ((), ())),
                           preferred_element_type=F32)


def _sigmoid(v):
    return 0.5 * jnp.tanh(0.5 * v) + 0.5


_GELU_C = math.sqrt(2.0 / math.pi)


def _gelu(v):
    return 0.5 * v * (1.0 + jnp.tanh(_GELU_C * (v + 0.044715 * v * v * v)))


def _gelu_and_grad(v):
    v2 = v * v
    t = jnp.tanh(_GELU_C * v * (1.0 + 0.044715 * v2))
    half = 0.5 * (1.0 + t)
    return v * half, half + 0.5 * v * (1.0 - t * t) * _GELU_C * (1.0 + 3.0 * 0.044715 * v2)


def _rms_stats(v):
    r = lax.rsqrt(jnp.mean(v * v, axis=-1, keepdims=True) + EPS)
    return r, v * r


def _rms_bwd(dxh, xh, r):
    return r * (dxh - xh * jnp.mean(dxh * xh, axis=-1, keepdims=True))


def _place():
    x, y, c = lax.axis_index("x"), lax.axis_index("y"), lax.axis_index("c")
    chips = [(1 - x, y), (x, 1 - y), (1 - x, 1 - y)]
    return x, y, c, chips


def _chip_index(chip):
    return 2 * chip[0] + chip[1]


def _remote(src, dst, send_sem, recv_sem, device):
    return pltpu.make_async_remote_copy(src_ref=src, dst_ref=dst, send_sem=send_sem,
                                        recv_sem=recv_sem, device_id=device, device_id_type=MESH)


def _half(ref_rows, c):
    hr = ref_rows // 2
    return pl.ds(pl.multiple_of(c * hr, SUBLANE), hr)


class _Job:
    def __init__(self, hooks, n_sem, ins=(), inouts=(), outs=()):
        self.hooks, self.n_sem = list(hooks), n_sem
        self.ins, self.inouts, self.outs = list(ins), list(inouts), list(outs)


def _whole_span(start, finish):
    return [(0.0, "start", start), (1.0, "finish", finish)]


ICI, SIBLING = "ici", "sibling"


def _job_gather(bufs, legs):
    def copies(io, leg, first):
        b, window, kind, _ = legs[leg]
        x, y, c, chips = _place()
        k_me = 2 * x + y
        out = []
        for j, ch in enumerate(chips):
            k = _chip_index(ch)
            if window is None:
                src, land, dev = io[b].at[k_me], io[b].at[k], (*ch, c)
            else:
                r0, rows = window
                mine = pl.ds(pl.multiple_of(r0 + c * (rows // 2), SUBLANE), rows // 2)
                theirs = pl.ds(pl.multiple_of(r0 + (1 - c) * (rows // 2), SUBLANE), rows // 2)
                if kind == ICI:
                    src, land, dev = io[b].at[k_me, mine, :], io[b].at[k, mine, :], (*ch, c)
                else:
                    src, land, dev = io[b].at[k, mine, :], io[b].at[k, theirs, :], (x, y, 1 - c)
            out.append((src, land, first + j, dev))
        return out

    def starter(leg):
        def start(ins, io, outs, ssem, rsem):
            for src, _, i, dev in copies(io, leg, 3 * leg):
                _remote(src, src, ssem(i), rsem(i), dev).start()
        return start

    def finisher(leg):
        def finish(ins, io, outs, ssem, rsem):
            cps = copies(io, leg, 3 * leg)
            for _, land, i, dev in cps:
                _remote(land, land, ssem(i), rsem(i), dev).wait_recv()
            for src, _, i, dev in cps:
                _remote(src, src, ssem(i), rsem(i), dev).wait_send()
        return finish

    hooks = []
    for leg, (_, _, _, (begin, end)) in enumerate(legs):
        hooks += [(begin, "start", starter(leg)), (end, "finish", finisher(leg))]
    return _Job(hooks, 3 * len(legs), inouts=bufs)


def _job_sibling_halves(grads):
    n = len(grads)

    def build(ins, outs, ssem, rsem):
        x, y, c, _ = _place()
        return [_remote(ins[t].at[:, _half(grads[t].shape[1], 1 - c), :], outs[t], ssem(t), rsem(t),
                        (x, y, 1 - c)) for t in range(n)]

    def start(ins, io, outs, ssem, rsem):
        for cp in build(ins, outs, ssem, rsem):
            cp.start()

    def finish(ins, io, outs, ssem, rsem):
        for cp in build(ins, outs, ssem, rsem):
            cp.wait()

    return _Job(_whole_span(start, finish), n, ins=grads,
                outs=[jax.ShapeDtypeStruct((N_CHIP, g.shape[1] // 2, g.shape[2]), F32) for g in grads])


def _job_to_owner(sums):
    n = len(sums)

    def build(ins, outs, ssem, rsem):
        x, y, c, chips = _place()
        return [_remote(ins[t].at[_chip_index(ch)], outs[t].at[j], ssem(3 * t + j), rsem(3 * t + j),
                        (*ch, c)) for t in range(n) for j, ch in enumerate(chips)]

    def start(ins, io, outs, ssem, rsem):
        for cp in build(ins, outs, ssem, rsem):
            cp.start()

    def finish(ins, io, outs, ssem, rsem):
        for cp in build(ins, outs, ssem, rsem):
            cp.wait()

    return _Job(_whole_span(start, finish), 3 * n, ins=sums,
                outs=[jax.ShapeDtypeStruct((3,) + s.shape[1:], s.dtype) for s in sums])


def _job_swap_halves(bufs):
    n = len(bufs)

    def start(ins, io, outs, ssem, rsem):
        x, y, c, _ = _place()
        for t in range(n):
            mine = io[t].at[_half(bufs[t].shape[0], c), :]
            _remote(mine, mine, ssem(t), rsem(t), (x, y, 1 - c)).start()

    def finish(ins, io, outs, ssem, rsem):
        x, y, c, _ = _place()
        for t in range(n):
            theirs = io[t].at[_half(bufs[t].shape[0], 1 - c), :]
            _remote(theirs, theirs, ssem(t), rsem(t), (x, y, 1 - c)).wait_recv()
        for t in range(n):
            mine = io[t].at[_half(bufs[t].shape[0], c), :]
            _remote(mine, mine, ssem(t), rsem(t), (x, y, 1 - c)).wait_send()

    return _Job(_whole_span(start, finish), n, inouts=bufs)


def _call(body, name, grid, in_specs, out_specs, out_shape, args, jobs=(), scratch=()):
    n_in, n_out, n_scr = len(args), len(out_shape), len(scratch)
    job_in = [a for jb in jobs for a in jb.ins + jb.inouts]
    job_out = [s for jb in jobs
               for s in [jax.ShapeDtypeStruct(a.shape, a.dtype) for a in jb.inouts] + jb.outs]
    aliases, pos_in, pos_out = {}, n_in, n_out
    for jb in jobs:
        pos_in += len(jb.ins)
        for _ in jb.inouts:
            aliases[pos_in] = pos_out
            pos_in += 1
            pos_out += 1
        pos_out += len(jb.outs)
    n_sem = sum(jb.n_sem for jb in jobs)

    def wrapped(*refs):
        c_in = refs[:n_in]
        j_in = refs[n_in:n_in + len(job_in)]
        c_out = refs[n_in + len(job_in):n_in + len(job_in) + n_out]
        j_out = refs[n_in + len(job_in) + n_out:n_in + len(job_in) + n_out + len(job_out)]
        rest = refs[n_in + len(job_in) + n_out + len(job_out):]
        c_scr = rest[:n_scr]
        views, pi, po, ps = [], 0, 0, 0
        for jb in jobs:
            ins = j_in[pi:pi + len(jb.ins)]
            pi += len(jb.ins) + len(jb.inouts)
            io = j_out[po:po + len(jb.inouts)]
            new = j_out[po + len(jb.inouts):po + len(jb.inouts) + len(jb.outs)]
            po += len(jb.inouts) + len(jb.outs)
            send = (lambda i, o=ps: rest[n_scr].at[o + i])
            recv = (lambda i, o=ps: rest[n_scr + 1].at[o + i])
            ps += jb.n_sem
            views.append((ins, io, new, send, recv))

        def run(frac):
            for kind in ("finish", "start"):
                for jb, vw in zip(jobs, views):
                    for at, what, fn in jb.hooks:
                        if at == frac and what == kind:
                            fn(*vw)

        fracs = sorted({at for jb in jobs for at, _, _ in jb.hooks})
        if not grid:
            for frac in fracs:
                run(frac)
            return
        if jobs:
            assert len(grid) == 1 or set(fracs) <= {0.0, 1.0}
            first = pl.program_id(0) == 0
            last = pl.program_id(0) == grid[0] - 1
            for d in range(1, len(grid)):
                first = jnp.logical_and(first, pl.program_id(d) == 0)
                last = jnp.logical_and(last, pl.program_id(d) == grid[d] - 1)
            for frac in fracs:
                if frac < 1.0:
                    at_step = first if frac == 0.0 else pl.program_id(0) == int(frac * grid[0])
                    pl.when(at_step)(lambda frac=frac: run(frac))
        body(*c_in, *c_out, *c_scr)
        if jobs and 1.0 in fracs:
            pl.when(last)(lambda: run(1.0))

    sems = [pltpu.SemaphoreType.DMA((n_sem,)), pltpu.SemaphoreType.DMA((n_sem,))] if jobs else []
    kwargs = dict(grid=grid) if grid else {}
    res = pl.pallas_call(
        wrapped, name=name, in_specs=list(in_specs) + [ANY] * len(job_in),
        out_specs=list(out_specs) + [ANY] * len(job_out),
        out_shape=list(out_shape) + job_out, scratch_shapes=list(scratch) + sems,
        input_output_aliases=aliases, compiler_params=_params(len(grid)), **kwargs,
    )(*args, *job_in)
    outs, pos, per_job = list(res[:n_out]), n_out, []
    for jb in jobs:
        k = len(jb.inouts) + len(jb.outs)
        per_job.append(list(res[pos:pos + k]))
        pos += k
    return outs, per_job


def _comm(name, jobs):
    return _call(None, name, (), [], [], [], [], jobs)[1]


def _fwd_in(x, g_mix, w_in, bre, bim, jobs=()):
    t_len = x.shape[0]
    cs = IN_COLS // N_CHIP

    def body(x_ref, g_ref, w_ref, bre_ref, bim_ref, p_ref, h_ref, bur_ref, bui_ref):
        xv = x_ref[...]
        r, xh = _rms_stats(xv)
        h = (xh * g_ref[...]).astype(BF16)
        h_ref[...] = h
        for k in range(N_CHIP):
            p_ref[:, k * cs:(k + 1) * cs] = jnp.dot(h, w_ref[k],
                                                    preferred_element_type=F32).astype(BF16)
        u = p_ref[:, 0:SSM_W]
        for i in range(SSM_W // LANE):
            rows, cols = slice(i * LANE, (i + 1) * LANE), slice(i * DIAG_N, (i + 1) * DIAG_N)
            bur_ref[:, cols] = jnp.dot(u[:, rows], bre_ref[rows, cols],
                                       preferred_element_type=F32).astype(BF16)
            bui_ref[:, cols] = jnp.dot(u[:, rows], bim_ref[rows, cols],
                                       preferred_element_type=F32).astype(BF16)

    return _call(
        body, "fwd_in", (t_len // TB,),
        [_rows(TB, D_MODEL), _whole(), _whole(), _whole(), _whole()],
        [_rows(TB, IN_COLS), _rows(TB, D_MODEL), _rows(TB, N_STATE), _rows(TB, N_STATE)],
        [jax.ShapeDtypeStruct((t_len, IN_COLS), BF16), jax.ShapeDtypeStruct((t_len, D_MODEL), BF16),
         jax.ShapeDtypeStruct((t_len, N_STATE), BF16), jax.ShapeDtypeStruct((t_len, N_STATE), BF16)],
        [x, g_mix, w_in, bre, bim], jobs)


def _scan_local(xr, xi, tab, shifts):
    for q, s in enumerate(shifts):
        ar, ai = tab[2 * q], tab[2 * q + 1]
        rr = pltpu.roll(xr, s, 0)
        ri = pltpu.roll(xi, s, 0)
        xr, xi = xr + ar * rr - ai * ri, xi + ar * ri + ai * rr
    return xr, xi


def _scan_carry(xr, xi, tab, cr, ci):
    pr, pi = tab[6], tab[7]
    return xr + pr * cr - pi * ci, xi + pr * ci + pi * cr


BF16_TILE = 2 * SUBLANE


def _load_blocks(r_ref, i_ref, base):
    out = []
    for q in range(SCAN_UNROLL // 2):
        rows = pl.ds(pl.multiple_of(base + q * BF16_TILE, BF16_TILE), BF16_TILE)
        vr, vi = r_ref[rows, :].astype(F32), i_ref[rows, :].astype(F32)
        out += [(vr[:SUBLANE], vi[:SUBLANE]), (vr[SUBLANE:], vi[SUBLANE:])]
    return out


def _store_blocks(r_ref, i_ref, base, blocks):
    for q in range(SCAN_UNROLL // 2):
        rows = pl.ds(pl.multiple_of(base + q * BF16_TILE, BF16_TILE), BF16_TILE)
        r_ref[rows, :] = jnp.concatenate([blocks[2 * q][0], blocks[2 * q + 1][0]], 0).astype(r_ref.dtype)
        i_ref[rows, :] = jnp.concatenate([blocks[2 * q][1], blocks[2 * q + 1][1]], 0).astype(i_ref.dtype)


def _scan_fwd(bur, bui, tab, jobs=()):
    t_len = bur.shape[0]
    nblk = t_len // SUBLANE
    lb = SCAN_LANES

    def body(br_ref, bi_ref, tab_ref, sr_ref, si_ref):
        tab_v = [tab_ref[q] for q in range(8)]

        def step(k, carry):
            cr, ci = carry
            base = pl.multiple_of(k * SCAN_UNROLL * SUBLANE, SCAN_UNROLL * SUBLANE)
            local = [_scan_local(xr, xi, tab_v, (1, 2, 4))
                     for xr, xi in _load_blocks(br_ref, bi_ref, base)]
            done = []
            for xr, xi in local:
                xr, xi = _scan_carry(xr, xi, tab_v, cr, ci)
                done.append((xr, xi))
                cr, ci = xr[SUBLANE - 1:SUBLANE, :], xi[SUBLANE - 1:SUBLANE, :]
            _store_blocks(sr_ref, si_ref, base, done)
            return cr, ci

        zero = jnp.zeros((1, lb), F32)
        lax.fori_loop(0, nblk // SCAN_UNROLL, step, (zero, zero))

    col = pl.BlockSpec((t_len, lb), lambda j: (0, j))
    return _call(
        body, "scan_fwd", (N_STATE // lb,),
        [col, col, pl.BlockSpec((8, SUBLANE, lb), lambda j: (0, 0, j))], [col, col],
        [jax.ShapeDtypeStruct((t_len, N_STATE), BF16)] * 2, [bur, bui, tab], jobs)


def _sgu_mix(v, ws_ref, lane_lo):
    rows = []
    for c0 in range(0, v.shape[0], CHUNK):
        slabs = []
        for j in range(SGU_W // LANE):
            prod = jnp.dot(ws_ref[j], v[c0:c0 + CHUNK, j * LANE:(j + 1) * LANE].astype(BF16),
                           preferred_element_type=F32)
            slabs.append(jnp.where(lane_lo, prod[:CHUNK], prod[CHUNK:]))
        rows.append(jnp.concatenate(slabs, axis=1))
    return jnp.concatenate(rows, axis=0) if len(rows) > 1 else rows[0]


def _fwd_mix(x, p, str_, sti, cre, cim, d_skip, w_glu, b_glu, w_pa, g_sgu, ws_st, bmat, w_pb, w_out,
             jobs=()):
    t_len = x.shape[0]

    def body(x_ref, p_ref, sr_ref, si_ref, cre_ref, cim_ref, dsk_ref, wg_ref, bg_ref, wpa_ref,
             gs_ref, ws_ref, bm_ref, wpb_ref, wo_ref,
             x2_ref, y0_ref, z_ref, mx_ref, ya_ref, yb_ref):
        u = p_ref[:, 0:SSM_W].astype(F32)
        y0 = jnp.concatenate(
            [_dot(sr_ref[:, i * DIAG_N:(i + 1) * DIAG_N],
                  cre_ref[i * DIAG_N:(i + 1) * DIAG_N, i * LANE:(i + 1) * LANE])
             - _dot(si_ref[:, i * DIAG_N:(i + 1) * DIAG_N],
                    cim_ref[i * DIAG_N:(i + 1) * DIAG_N, i * LANE:(i + 1) * LANE])
             for i in range(SSM_W // LANE)], axis=1) + dsk_ref[...] * u
        y0_ref[...] = y0.astype(BF16)
        y1 = _gelu(y0)
        z = _dot(y1, wg_ref[...]) + bg_ref[...]
        z_ref[...] = z.astype(BF16)
        ya_pre = (y1 * _sigmoid(z)).astype(BF16)
        ya = jnp.concatenate([jnp.dot(ya_pre, wpa_ref[k], preferred_element_type=F32)
                              for k in range(N_CHIP)], axis=1)
        ya_ref[...] = ya.astype(BF16)

        uvg = _gelu(p_ref[:, SSM_W:SSM_W + 2 * SGU_W].astype(F32))
        u2 = uvg[:, :SGU_W]
        _, vh = _rms_stats(uvg[:, SGU_W:])
        v3 = vh * gs_ref[...]
        lane_lo = lax.broadcasted_iota(jnp.int32, (CHUNK, LANE), 1) < SGU_D
        bias = jnp.concatenate([bm_ref[...]] * (TB // CHUNK), axis=0)
        mixed = _sgu_mix(v3, ws_ref, lane_lo) + bias
        mx_ref[...] = mixed.astype(BF16)
        sgu = (u2 * mixed).astype(BF16)
        yb = jnp.concatenate([jnp.dot(sgu, wpb_ref[k], preferred_element_type=F32)
                              for k in range(N_CHIP)], axis=1)
        yb_ref[...] = yb.astype(BF16)

        lg0 = SSM_W + 2 * SGU_W
        ga = _sigmoid(p_ref[:, lg0:lg0 + D_MODEL].astype(F32))
        gb = _sigmoid(p_ref[:, lg0 + D_MODEL:lg0 + 2 * D_MODEL].astype(F32))
        mrg = ga * ya + gb * yb
        x2_ref[...] = x_ref[...] + _dot(mrg, wo_ref[...])

    return _call(
        body, "fwd_mix", (t_len // TB,),
        [_rows(TB, D_MODEL), _rows(TB, IN_COLS), _rows(TB, N_STATE), _rows(TB, N_STATE)]
        + [_whole()] * 11,
        [_rows(TB, D_MODEL), _rows(TB, SSM_W), _rows(TB, SSM_W), _rows(TB, SGU_W),
         _rows(TB, D_MODEL), _rows(TB, D_MODEL)],
        [jax.ShapeDtypeStruct((t_len, D_MODEL), F32), jax.ShapeDtypeStruct((t_len, SSM_W), BF16),
         jax.ShapeDtypeStruct((t_len, SSM_W), BF16), jax.ShapeDtypeStruct((t_len, SGU_W), BF16),
         jax.ShapeDtypeStruct((t_len, D_MODEL), BF16), jax.ShapeDtypeStruct((t_len, D_MODEL), BF16)],
        [x, p, str_, sti, cre, cim, d_skip, w_glu, b_glu, w_pa, g_sgu, ws_st, bmat, w_pb, w_out], jobs)


def _conv_taps(v, cw_ref, c0, width):
    w0 = cw_ref[0:1, c0:c0 + width]
    w1 = cw_ref[1:2, c0:c0 + width]
    w2 = cw_ref[2:3, c0:c0 + width]
    return w0 * pltpu.roll(v, 2, 0) + w1 * pltpu.roll(v, 1, 0) + w2 * v


def _fwd_ffn(x2, target, g_ffn, w_up, conv_w, conv_b, w_down, g_final):
    t_len = x2.shape[0]
    half = D_FF // 2
    blocks_per_halo = TB // HALO

    def body(x2_ref, xp_ref, tg_ref, gf_ref, wu_ref, cw_ref, cb_ref, wd_ref, gl_ref,
             up_ref, act_ref, f_ref, h2_ref, dx3_ref, sm_ref):
        i = pl.program_id(0)
        xe = jnp.concatenate([xp_ref[...] * jnp.where(i == 0, 0.0, 1.0), x2_ref[...]], axis=0)
        _, xh = _rms_stats(xe)
        h2 = (xh * gf_ref[...]).astype(BF16)
        h2_ref[...] = h2[HALO:]
        acc = jnp.zeros((TB, D_MODEL), F32)
        for hc in range(2):
            ca = hc * half
            cb = D_FF + hc * half
            ua = jnp.dot(h2, wu_ref[hc], preferred_element_type=F32)
            ub = jnp.dot(h2, wu_ref[2 + hc], preferred_element_type=F32)
            up_ref[:, ca:ca + half] = ua[HALO:].astype(BF16)
            up_ref[:, cb:cb + half] = ub[HALO:].astype(BF16)
            ac = _conv_taps(ua, cw_ref, ca, half)[HALO:] + cb_ref[:, ca:ca + half]
            bc = _conv_taps(ub, cw_ref, cb, half)[HALO:] + cb_ref[:, cb:cb + half]
            act_ref[:, ca:ca + half] = ac.astype(BF16)
            act_ref[:, cb:cb + half] = bc.astype(BF16)
            f = (ac * _sigmoid(ac) * bc).astype(BF16)
            f_ref[:, ca:ca + half] = f
            acc = acc + jnp.dot(f, wd_ref[ca:ca + half, :], preferred_element_type=F32)
        x3 = x2_ref[...] + acc
        r3, xh3 = _rms_stats(x3)
        err = xh3 * gl_ref[...] - tg_ref[...]
        dout = err * (1.0 / D_MODEL)
        dx3_ref[...] = _rms_bwd(dout * gl_ref[...], xh3, r3)
        dgl = jnp.sum(dout * xh3, axis=0, keepdims=True)
        loss = 0.5 * jnp.sum(jnp.mean(err * err, axis=-1, keepdims=True), axis=0, keepdims=True)
        upd = jnp.concatenate([dgl, jnp.broadcast_to(loss, (1, D_MODEL)),
                               jnp.zeros((SUBLANE - 2, D_MODEL), F32)], axis=0)

        @pl.when(i == 0)
        def _():
            sm_ref[...] = upd

        @pl.when(i > 0)
        def _():
            sm_ref[...] += upd

    prev = pl.BlockSpec((HALO, D_MODEL), lambda i: (jnp.maximum(i * blocks_per_halo - 1, 0), 0))
    return _call(
        body, "fwd_ffn", (t_len // TB,),
        [_rows(TB, D_MODEL), prev, _rows(TB, D_MODEL)] + [_whole()] * 6,
        [_rows(TB, 2 * D_FF), _rows(TB, 2 * D_FF), _rows(TB, D_FF), _rows(TB, D_MODEL),
         _rows(TB, D_MODEL), _acc(SUBLANE, D_MODEL)],
        [jax.ShapeDtypeStruct((t_len, 2 * D_FF), BF16), jax.ShapeDtypeStruct((t_len, 2 * D_FF), BF16),
         jax.ShapeDtypeStruct((t_len, D_FF), BF16), jax.ShapeDtypeStruct((t_len, D_MODEL), BF16),
         jax.ShapeDtypeStruct((t_len, D_MODEL), F32), jax.ShapeDtypeStruct((SUBLANE, D_MODEL), F32)],
        [x2, x2, target, g_ffn, w_up, conv_w, conv_b, w_down, g_final])[0]


def _bwd_ffn(dx3, up, act, x2, g_ffn, w_up, conv_w, w_down, jobs=()):
    t_len = x2.shape[0]
    half = D_FF // 2
    nblk = t_len // TB
    halo_b = 2 * HALO
    n_e = TB + HALO

    def body(dx_ref, dxn_ref, up_ref, act_ref, actn_ref, x2_ref, gf_ref, wu_ref, cw_ref,
             wd_ref, dx2_ref, dup_ref, smw_ref, smg_ref):
        i = pl.program_id(0)
        keep_last = jnp.where(i == nblk - 1, 0.0, 1.0)
        dxe = jnp.concatenate([dx_ref[...], dxn_ref[...] * keep_last], axis=0).astype(BF16)
        dh2 = jnp.zeros((TB, D_MODEL), F32)
        zpad = jnp.zeros((1, half), F32)
        for hc in range(2):
            ca = hc * half
            cb = D_FF + hc * half
            ac = jnp.concatenate([act_ref[:, ca:ca + half].astype(F32),
                                  actn_ref[:, ca:ca + half].astype(F32)[:HALO]], axis=0)
            bc = jnp.concatenate([act_ref[:, cb:cb + half].astype(F32),
                                  actn_ref[:, cb:cb + half].astype(F32)[:HALO]], axis=0)
            wa = [cw_ref[k:k + 1, ca:ca + half] for k in range(3)]
            wb = [cw_ref[k:k + 1, cb:cb + half] for k in range(3)]
            df = lax.dot_general(dxe, wd_ref[ca:ca + half, :], (((1,), (1,)), ((), ())),
                                 preferred_element_type=F32)
            sg = _sigmoid(ac)
            da = df * bc * sg * (1.0 + ac * (1.0 - sg))
            db = df * ac * sg
            da1, da2 = pltpu.roll(da, n_e - 1, 0), pltpu.roll(da, n_e - 2, 0)
            db1, db2 = pltpu.roll(db, n_e - 1, 0), pltpu.roll(db, n_e - 2, 0)
            dua = (wa[2] * da + wa[1] * da1 + wa[0] * da2)[:TB]
            dub = (wb[2] * db + wb[1] * db1 + wb[0] * db2)[:TB]
            dup_ref[:, ca:ca + half] = dua.astype(BF16)
            dup_ref[:, cb:cb + half] = dub.astype(BF16)
            dh2 = dh2 + _dot_nt(dua, wu_ref[hc]) + _dot_nt(dub, wu_ref[2 + hc])
            rows = []
            for u_, d0, d1, d2 in ((up_ref[:, ca:ca + half].astype(F32), da, da1, da2),
                                   (up_ref[:, cb:cb + half].astype(F32), db, db1, db2)):
                rows.append([jnp.sum(u_ * d2[:TB], axis=0, keepdims=True),
                             jnp.sum(u_ * d1[:TB], axis=0, keepdims=True),
                             jnp.sum(u_ * d0[:TB], axis=0, keepdims=True),
                             jnp.sum(d0[:TB], axis=0, keepdims=True)])
            for c0, rws in ((ca, rows[0]), (cb, rows[1])):
                upd = jnp.concatenate(rws + [zpad] * (SUBLANE - 4), axis=0)

                @pl.when(i == 0)
                def _(upd=upd, c0=c0):
                    smw_ref[:, c0:c0 + half] = upd

                @pl.when(i > 0)
                def _(upd=upd, c0=c0):
                    smw_ref[:, c0:c0 + half] += upd

        r2, xh2 = _rms_stats(x2_ref[...])
        dx2_ref[...] = dx_ref[...] + _rms_bwd(dh2 * gf_ref[...], xh2, r2)
        updg = jnp.concatenate([jnp.sum(dh2 * xh2, axis=0, keepdims=True),
                                jnp.zeros((SUBLANE - 1, D_MODEL), F32)], axis=0)

        @pl.when(i == 0)
        def _():
            smg_ref[...] = updg

        @pl.when(i > 0)
        def _():
            smg_ref[...] += updg

    nxt_d = pl.BlockSpec((HALO, D_MODEL),
                         lambda i: (jnp.minimum((i + 1) * (TB // HALO), t_len // HALO - 1), 0))
    nxt_a = pl.BlockSpec((halo_b, 2 * D_FF),
                         lambda i: (jnp.minimum((i + 1) * (TB // halo_b), t_len // halo_b - 1), 0))
    return _call(
        body, "bwd_ffn", (nblk,),
        [_rows(TB, D_MODEL), nxt_d, _rows(TB, 2 * D_FF), _rows(TB, 2 * D_FF), nxt_a,
         _rows(TB, D_MODEL)] + [_whole()] * 4,
        [_rows(TB, D_MODEL), _rows(TB, 2 * D_FF), _acc(SUBLANE, 2 * D_FF), _acc(SUBLANE, D_MODEL)],
        [jax.ShapeDtypeStruct((t_len, D_MODEL), F32), jax.ShapeDtypeStruct((t_len, 2 * D_FF), BF16),
         jax.ShapeDtypeStruct((SUBLANE, 2 * D_FF), F32), jax.ShapeDtypeStruct((SUBLANE, D_MODEL), F32)],
        [dx3, dx3, up, act, act, x2, g_ffn, w_up, conv_w, w_down], jobs)


def _bwd_mix(dx2, p, y0, z, mixed, ya, yb, w_out, w_pa, w_pb, w_glu, cre, cim, ws_st, wst_st,
             d_skip, g_sgu, jobs=()):
    t_len = dx2.shape[0]
    pc = D_MODEL // N_CHIP
    n_slab = SGU_W // LANE

    def body(dx_ref, p_ref, y0_ref, z_ref, mx_ref, ya_ref, yb_ref, wo_ref, wpa_ref, wpb_ref,
             wg_ref, cre_ref, cim_ref, ws_ref, wst_ref, dsk_ref, gs_ref,
             dsr_ref, dsi_ref, du_ref, drest_ref, mrg_ref, dya_ref, dyb_ref, yap_ref, dz_ref,
             y1_ref, sgu_ref, dy0_ref, sm_ref, dbm_ref, dws_ref):
        i = pl.program_id(0)
        first = i == 0
        lg0 = SSM_W + 2 * SGU_W
        dmrg = _dot_nt(dx_ref[...], wo_ref[...])
        ga = _sigmoid(p_ref[:, lg0:lg0 + D_MODEL].astype(F32))
        gb = _sigmoid(p_ref[:, lg0 + D_MODEL:lg0 + 2 * D_MODEL].astype(F32))
        yav = ya_ref[...].astype(F32)
        ybv = yb_ref[...].astype(F32)
        mrg_ref[...] = (ga * yav + gb * ybv).astype(BF16)
        drest_ref[:, 2 * SGU_W:2 * SGU_W + D_MODEL] = (dmrg * yav * ga * (1.0 - ga)).astype(BF16)
        drest_ref[:, 2 * SGU_W + D_MODEL:] = (dmrg * ybv * gb * (1.0 - gb)).astype(BF16)
        dya = (dmrg * ga).astype(BF16)
        dyb = (dmrg * gb).astype(BF16)
        dya_ref[...] = dya
        dyb_ref[...] = dyb

        y0v = y0_ref[...].astype(F32)
        y1, y1_grad = _gelu_and_grad(y0v)
        sz = _sigmoid(z_ref[...].astype(F32))
        y1_ref[...] = y1.astype(BF16)
        yap_ref[...] = (y1 * sz).astype(BF16)
        dyap = jnp.zeros((TB, SSM_W), F32)
        for k in range(N_CHIP):
            dyap = dyap + _dot_nt(dya[:, k * pc:(k + 1) * pc], wpa_ref[k])
        dz = dyap * y1 * sz * (1.0 - sz)
        dz_ref[...] = dz.astype(BF16)
        dy0 = (dyap * sz + _dot_nt(dz, wg_ref[...])) * y1_grad
        dy0_ref[...] = dy0.astype(BF16)
        u = p_ref[:, 0:SSM_W].astype(F32)
        du_ref[...] = dy0 * dsk_ref[...]
        for i in range(SSM_W // LANE):
            rows, cols = slice(i * DIAG_N, (i + 1) * DIAG_N), slice(i * LANE, (i + 1) * LANE)
            dsr_ref[:, rows] = _dot_nt(dy0[:, cols], cre_ref[rows, cols]).astype(BF16)
            dsi_ref[:, rows] = (-_dot_nt(dy0[:, cols], cim_ref[rows, cols])).astype(BF16)

        uv = p_ref[:, SSM_W:lg0].astype(F32)
        uvg, gg = _gelu_and_grad(uv)
        u2 = uvg[:, :SGU_W]
        rv, vh = _rms_stats(uvg[:, SGU_W:])
        v3 = vh * gs_ref[...]
        mixed = mx_ref[...].astype(F32)
        dsgu = jnp.zeros((TB, SGU_W), F32)
        for k in range(N_CHIP):
            dsgu = dsgu + _dot_nt(dyb[:, k * pc:(k + 1) * pc], wpb_ref[k])
        sgu_ref[...] = (u2 * mixed).astype(BF16)
        du2 = dsgu * mixed
        dmix = dsgu * u2
        lane_lo = lax.broadcasted_iota(jnp.int32, (CHUNK, LANE), 1) < SGU_D
        dv3 = _sgu_mix(dmix, wst_ref, lane_lo)
        dbm = jnp.zeros((CHUNK, SGU_W), F32)
        for c0 in range(0, TB, CHUNK):
            dbm = dbm + dmix[c0:c0 + CHUNK]
        for j in range(n_slab):
            lo = jnp.zeros((CHUNK, CHUNK), F32)
            hi = jnp.zeros((CHUNK, CHUNK), F32)
            for c0 in range(0, TB, CHUNK):
                dsl = dmix[c0:c0 + CHUNK, j * LANE:(j + 1) * LANE]
                vsl = v3[c0:c0 + CHUNK, j * LANE:(j + 1) * LANE]
                lo = lo + _dot_nt(jnp.where(lane_lo, dsl, 0.0), vsl)
                hi = hi + _dot_nt(jnp.where(lane_lo, 0.0, dsl), vsl)

            @pl.when(first)
            def _(lo=lo, hi=hi, j=j):
                dws_ref[2 * j] = lo
                dws_ref[2 * j + 1] = hi

            @pl.when(jnp.logical_not(first))
            def _(lo=lo, hi=hi, j=j):
                dws_ref[2 * j] += lo
                dws_ref[2 * j + 1] += hi

        dv2 = _rms_bwd(dv3 * gs_ref[...], vh, rv)
        drest_ref[:, 0:SGU_W] = (du2 * gg[:, :SGU_W]).astype(BF16)
        drest_ref[:, SGU_W:2 * SGU_W] = (dv2 * gg[:, SGU_W:]).astype(BF16)

        upd = jnp.concatenate([jnp.sum(dy0 * u, axis=0, keepdims=True),
                               jnp.sum(dz, axis=0, keepdims=True),
                               jnp.sum(dv3 * vh, axis=0, keepdims=True),
                               jnp.zeros((SUBLANE - 3, SSM_W), F32)], axis=0)

        @pl.when(first)
        def _():
            sm_ref[...] = upd
            dbm_ref[...] = dbm

        @pl.when(jnp.logical_not(first))
        def _():
            sm_ref[...] += upd
            dbm_ref[...] += dbm

    rest = 2 * SGU_W + 2 * D_MODEL
    bf_d, bf_s = jax.ShapeDtypeStruct((t_len, D_MODEL), BF16), jax.ShapeDtypeStruct((t_len, SSM_W), BF16)
    return _call(
        body, "bwd_mix", (t_len // TB,),
        [_rows(TB, D_MODEL), _rows(TB, IN_COLS), _rows(TB, SSM_W), _rows(TB, SSM_W),
         _rows(TB, SGU_W), _rows(TB, D_MODEL), _rows(TB, D_MODEL)] + [_whole()] * 10,
        [_rows(TB, N_STATE), _rows(TB, N_STATE), _rows(TB, SSM_W), _rows(TB, rest),
         _rows(TB, D_MODEL), _rows(TB, D_MODEL), _rows(TB, D_MODEL), _rows(TB, SSM_W),
         _rows(TB, SSM_W), _rows(TB, SSM_W), _rows(TB, SGU_W), _rows(TB, SSM_W),
         _acc(SUBLANE, SSM_W), _acc(CHUNK, SGU_W),
         pl.BlockSpec((SGU_G, CHUNK, CHUNK), lambda i: (0, 0, 0))],
        [jax.ShapeDtypeStruct((t_len, N_STATE), BF16), jax.ShapeDtypeStruct((t_len, N_STATE), BF16),
         jax.ShapeDtypeStruct((t_len, SSM_W), F32), jax.ShapeDtypeStruct((t_len, rest), BF16),
         bf_d, bf_d, bf_d, bf_s, bf_s, bf_s, bf_s, bf_s,
         jax.ShapeDtypeStruct((SUBLANE, SSM_W), F32), jax.ShapeDtypeStruct((CHUNK, SGU_W), F32),
         jax.ShapeDtypeStruct((SGU_G, CHUNK, CHUNK), F32)],
        [dx2, p, y0, z, mixed, ya, yb, w_out, w_pa, w_pb, w_glu, cre, cim, ws_st, wst_st, d_skip,
         g_sgu], jobs)


def _scan_bwd(dsr, dsi, str_, sti, tab_rev, jobs=()):
    t_len = dsr.shape[0]
    nblk = t_len // SUBLANE
    lb = SCAN_LANES

    def body(dr_ref, di_ref, sr_ref, si_ref, tab_ref, lr_ref, li_ref, dar_ref, dai_ref):
        tab_v = [tab_ref[q] for q in range(8)]
        row0 = lax.broadcasted_iota(jnp.int32, (SUBLANE, lb), 0) == 0
        tile = BF16_TILE

        def step(k, carry):
            cr, ci, acr, aci = carry
            base = pl.multiple_of((nblk - (k + 1) * SCAN_UNROLL) * SUBLANE, SCAN_UNROLL * SUBLANE)
            state = _load_blocks(sr_ref, si_ref, base)
            before = pl.ds(pl.multiple_of(jnp.maximum(base - tile, 0), tile), tile)
            has_before = jnp.where(base > 0, 1.0, 0.0)
            prev = (sr_ref[before, :].astype(F32)[tile - 1:tile] * has_before,
                    si_ref[before, :].astype(F32)[tile - 1:tile] * has_before)
            local = [_scan_local(xr, xi, tab_v, (7, 6, 4))
                     for xr, xi in _load_blocks(dr_ref, di_ref, base)]
            lam = [None] * SCAN_UNROLL
            for b in reversed(range(SCAN_UNROLL)):
                xr, xi = _scan_carry(*local[b], tab_v, cr, ci)
                lam[b] = (xr, xi)
                cr, ci = xr[0:1, :], xi[0:1, :]
                pr, pi = prev if b == 0 else (state[b - 1][0][SUBLANE - 1:], state[b - 1][1][SUBLANE - 1:])
                s_r = jnp.where(row0, pr, pltpu.roll(state[b][0], 1, 0))
                s_i = jnp.where(row0, pi, pltpu.roll(state[b][1], 1, 0))
                acr = acr + xr * s_r + xi * s_i
                aci = aci + xi * s_r - xr * s_i
            _store_blocks(lr_ref, li_ref, base, lam)
            return cr, ci, acr, aci

        zero = jnp.zeros((1, lb), F32)
        zacc = jnp.zeros((SUBLANE, lb), F32)
        _, _, acr, aci = lax.fori_loop(0, nblk // SCAN_UNROLL, step, (zero, zero, zacc, zacc))
        dar_ref[...] = acr
        dai_ref[...] = aci

    col = pl.BlockSpec((t_len, lb), lambda j: (0, j))
    small = pl.BlockSpec((SUBLANE, lb), lambda j: (0, j))
    return _call(
        body, "scan_bwd", (N_STATE // lb,),
        [col, col, col, col, pl.BlockSpec((8, SUBLANE, lb), lambda j: (0, 0, j))],
        [col, col, small, small],
        [jax.ShapeDtypeStruct((t_len, N_STATE), BF16)] * 2
        + [jax.ShapeDtypeStruct((SUBLANE, N_STATE), F32)] * 2,
        [dsr, dsi, str_, sti, tab_rev], jobs)


def _bwd_in(lam_r, lam_i, du_part, drest, x, dx2, g_mix, w_in, bre, bim, jobs=()):
    t_len = x.shape[0]
    cs = IN_COLS // N_CHIP

    def body(lr_ref, li_ref, du_ref, dr_ref, x_ref, dx2_ref, g_ref, w_ref, bre_ref, bim_ref,
             gx_ref, dp_ref, sm_ref):
        i = pl.program_id(0)
        du = du_ref[...] + jnp.concatenate(
            [_dot_nt(lr_ref[:, i * DIAG_N:(i + 1) * DIAG_N],
                     bre_ref[i * LANE:(i + 1) * LANE, i * DIAG_N:(i + 1) * DIAG_N])
             + _dot_nt(li_ref[:, i * DIAG_N:(i + 1) * DIAG_N],
                       bim_ref[i * LANE:(i + 1) * LANE, i * DIAG_N:(i + 1) * DIAG_N])
             for i in range(SSM_W // LANE)], axis=1)
        dp_ref[:, 0:SSM_W] = du.astype(BF16)
        dp_ref[:, SSM_W:] = dr_ref[...]
        dh = jnp.zeros((TB, D_MODEL), F32)
        for k in range(N_CHIP):
            dh = dh + _dot_nt(dp_ref[:, k * cs:(k + 1) * cs], w_ref[k])
        r, xh = _rms_stats(x_ref[...])
        gx_ref[...] = dx2_ref[...] + _rms_bwd(dh * g_ref[...], xh, r)
        upd = jnp.concatenate([jnp.sum(dh * xh, axis=0, keepdims=True),
                               jnp.zeros((SUBLANE - 1, D_MODEL), F32)], axis=0)

        @pl.when(i == 0)
        def _():
            sm_ref[...] = upd

        @pl.when(i > 0)
        def _():
            sm_ref[...] += upd

    return _call(
        body, "bwd_in", (t_len // TB,),
        [_rows(TB, N_STATE), _rows(TB, N_STATE), _rows(TB, SSM_W), _rows(TB, IN_COLS - SSM_W),
         _rows(TB, D_MODEL), _rows(TB, D_MODEL)] + [_whole()] * 4,
        [_rows(TB, D_MODEL), _rows(TB, IN_COLS), _acc(SUBLANE, D_MODEL)],
        [jax.ShapeDtypeStruct((t_len, D_MODEL), F32), jax.ShapeDtypeStruct((t_len, IN_COLS), BF16),
         jax.ShapeDtypeStruct((SUBLANE, D_MODEL), F32)],
        [lam_r, lam_i, du_part, drest, x, dx2, g_mix, w_in, bre, bim], jobs)


def _matmul_tn(a, b, name, out_shape, grid_ij, a_blk, a_map, b_blk, b_map, o_blk, o_map, jobs=()):
    tk = a_blk[0]
    nk = a.shape[0] // tk
    assert nk * tk == a.shape[0] and nk > 0

    def body(a_ref, b_ref, o_ref, acc_ref):
        k = pl.program_id(2)

        @pl.when(k == 0)
        def _():
            acc_ref[...] = jnp.zeros_like(acc_ref)

        acc_ref[...] += lax.dot_general(a_ref[...].astype(BF16), b_ref[...].astype(BF16),
                                        (((0,), (0,)), ((), ())), preferred_element_type=F32)

        @pl.when(k == nk - 1)
        def _():
            o_ref[...] = acc_ref[...]

    outs, per_job = _call(
        body, name, (grid_ij[0], grid_ij[1], nk),
        [pl.BlockSpec(a_blk, a_map), pl.BlockSpec(b_blk, b_map)], [pl.BlockSpec(o_blk, o_map)],
        [jax.ShapeDtypeStruct(out_shape, F32)], [a, b], jobs,
        scratch=[pltpu.VMEM((a_blk[1], b_blk[1]), F32)])
    return outs[0], per_job


def _dw_shards(a, b, name, tk, jobs=()):
    m, n = a.shape[1], b.shape[1]
    tn = n // N_CHIP
    tk = min(tk, a.shape[0])
    return _matmul_tn(a, b, name, (N_CHIP, m, tn), (1, N_CHIP),
                      (tk, m), lambda i, j, k: (k, 0), (tk, tn), lambda i, j, k: (k, j),
                      (None, m, tn), lambda i, j, k: (j, 0, 0), jobs)


def _dw_rows(a, b, name, tm, tk):
    m, n = a.shape[1], b.shape[1]
    tk = min(tk, a.shape[0])
    return _matmul_tn(a, b, name, (m, n), (m // tm, 1),
                      (tk, tm), lambda i, j, k: (k, i), (tk, n), lambda i, j, k: (k, 0),
                      (tm, n), lambda i, j, k: (i, 0))[0]


def _dw_cols(a, b, name, tn, sharded, jobs=()):
    t_len, m = a.shape
    n = b.shape[1]

    def body(a_ref, b_ref, o_ref):
        o_ref[...] = lax.dot_general(a_ref[...].astype(BF16), b_ref[...].astype(BF16),
                                     (((0,), (0,)), ((), ())), preferred_element_type=F32)

    if sharded:
        o_spec, o_shape = pl.BlockSpec((None, m, tn), lambda j: (j, 0, 0)), (n // tn, m, tn)
    else:
        o_spec, o_shape = pl.BlockSpec((m, tn), lambda j: (0, j)), (m, n)
    outs, per_job = _call(body, name, (n // tn,),
                          [_whole(), pl.BlockSpec((t_len, tn), lambda j: (0, j))], [o_spec],
                          [jax.ShapeDtypeStruct(o_shape, F32)], [a, b], jobs)
    return outs[0], per_job


def _dw_pair(a, m, b1, b2, name, jobs=()):
    t_len = a.shape[0]
    n_slab = DIAG_N // LANE
    rows_per_slab = LANE // n_slab

    def body(a_ref, b1_ref, b2_ref, o1_ref, o2_ref):
        for b_ref, o_ref in ((b1_ref, o1_ref), (b2_ref, o2_ref)):
            prod = lax.dot_general(a_ref[...].astype(BF16), b_ref[...].astype(BF16),
                                   (((0,), (0,)), ((), ())), preferred_element_type=F32)
            for j in range(n_slab):
                rows = slice(j * rows_per_slab, (j + 1) * rows_per_slab)
                o_ref[rows, :] = prod[rows, j * LANE:(j + 1) * LANE]

    tok = pl.BlockSpec((t_len, DIAG_N), lambda i: (0, i))
    out = pl.BlockSpec((LANE, LANE), lambda i: (i, 0))
    return _call(body, name, (m // LANE,),
                 [pl.BlockSpec((t_len, LANE), lambda i: (0, i)), tok, tok], [out, out],
                 [jax.ShapeDtypeStruct((m, LANE), F32)] * 2, [a, b1, b2], jobs)


def _prefetch_call(body, name, grid, scalars, in_specs, out_specs, out_shape, args):
    return pl.pallas_call(
        body, name=name,
        grid_spec=pltpu.PrefetchScalarGridSpec(num_scalar_prefetch=1, grid=grid, in_specs=in_specs,
                                               out_specs=out_specs),
        out_shape=out_shape, compiler_params=_params(len(grid)),
    )(scalars, *args)


def _place_shard(w, where, name, dtype, tr):
    rows, cols = w.shape

    def body(s_ref, w_ref, o_ref):
        o_ref[...] = w_ref[...].astype(dtype)

    return _prefetch_call(
        body, name, (rows // tr,), where,
        [pl.BlockSpec((tr, cols), lambda i, s: (i, 0))],
        pl.BlockSpec((None, tr, cols), lambda i, s: (s[0], i, 0)),
        jax.ShapeDtypeStruct((N_CHIP, rows, cols), dtype), [w])


def _add_sibling(g, got, where, name):
    _, rs, cs = g.shape
    hr = rs // 2

    def body(s_ref, g_ref, got_ref, o_ref):
        o_ref[...] = (g_ref[...] + got_ref[...]).astype(BF16)

    return _prefetch_call(
        body, name, (N_CHIP,), where,
        [pl.BlockSpec((None, hr, cs), lambda k, s: (k, s[1], 0)),
         pl.BlockSpec((None, hr, cs), lambda k, s: (k, 0, 0))],
        pl.BlockSpec((None, hr, cs), lambda k, s: (k, 0, 0)),
        jax.ShapeDtypeStruct((N_CHIP, hr, cs), BF16), [g, got])


def _add_chips(sums, got, where, name):
    _, hr, cs = sums.shape

    def body(s_ref, own_ref, got_ref, o_ref):
        o_ref[...] = ((own_ref[...].astype(F32) + got_ref[0].astype(F32))
                      + got_ref[1].astype(F32)) + got_ref[2].astype(F32)

    return _prefetch_call(
        body, name, (1,), where,
        [pl.BlockSpec((None, hr, cs), lambda i, s: (s[0], 0, 0)),
         pl.BlockSpec((3, hr, cs), lambda i, s: (0, 0, 0))],
        pl.BlockSpec((hr, cs), lambda i, s: (s[1], 0)),
        jax.ShapeDtypeStruct((2 * hr, cs), F32), [sums, got])


def _small_allreduce(pack):
    rows = pack.shape[0]
    half = rows // 2

    def body(in_ref, out_ref, sib_ref, slots_ref, s_a, r_a, s_b, r_b, s_c, r_c):
        x, y, c, chips = _place()
        k_me = 2 * x + y
        sib = (x, y, 1 - c)
        first = _remote(in_ref, sib_ref, s_a, r_a, sib)
        first.start()
        first.wait()
        mine = _half(rows, c)
        slots_ref[k_me] = in_ref[mine, :] + sib_ref[mine, :]
        cps = [_remote(slots_ref.at[k_me], slots_ref.at[k_me], s_b.at[j], r_b.at[j], (*ch, c))
               for j, ch in enumerate(chips)]
        for cp in cps:
            cp.start()
        for j, ch in enumerate(chips):
            slot = slots_ref.at[_chip_index(ch)]
            _remote(slot, slot, s_b.at[j], r_b.at[j], (*ch, c)).wait_recv()
        for cp in cps:
            cp.wait_send()
        out_ref[mine, :] = ((slots_ref[0] + slots_ref[1]) + slots_ref[2]) + slots_ref[3]
        last = _remote(out_ref.at[mine, :], out_ref.at[mine, :], s_c, r_c, sib)
        last.start()
        theirs = out_ref.at[_half(rows, 1 - c), :]
        _remote(theirs, theirs, s_c, r_c, sib).wait_recv()
        last.wait_send()

    return pl.pallas_call(
        body, name="small_allreduce", in_specs=[_whole()], out_specs=_whole(),
        out_shape=jax.ShapeDtypeStruct(pack.shape, F32),
        scratch_shapes=[pltpu.VMEM(pack.shape, F32), pltpu.VMEM((N_CHIP, half, LANE), F32),
                        pltpu.SemaphoreType.DMA, pltpu.SemaphoreType.DMA,
                        pltpu.SemaphoreType.DMA((3,)), pltpu.SemaphoreType.DMA((3,)),
                        pltpu.SemaphoreType.DMA, pltpu.SemaphoreType.DMA],
        compiler_params=_params(0),
    )(pack)


def _adamw_update(w_ref, g_ref, m_ref, v_ref, d_ref, mo_ref, vo_ref):
    gv = g_ref[...]
    mn = ADAM_B1 * m_ref[...] + (1.0 - ADAM_B1) * gv
    vn = ADAM_B2 * v_ref[...] + (1.0 - ADAM_B2) * (gv * gv)
    mo_ref[...] = mn
    vo_ref[...] = vn
    m_hat = mn / (1.0 - ADAM_B1 ** ADAM_STEP)
    v_hat = vn / (1.0 - ADAM_B2 ** ADAM_STEP)
    d_ref[...] = -ADAM_LR * (m_hat / (jnp.sqrt(v_hat) + ADAM_EPS) + ADAM_WD * w_ref[...])


def _adamw(w, g, m, v, name, tr):
    rows, cols = w.shape
    blk = _rows(tr, cols)
    return _call(_adamw_update, name, (rows // tr,), [blk] * 4, [blk] * 3,
                 [jax.ShapeDtypeStruct(w.shape, F32)] * 3, [w, g, m, v])[0]


def _adamw_many(ws, gs, ms, vs, name):
    n = len(ws)

    def body(*refs):
        for t in range(n):
            _adamw_update(*[refs[q * n + t] for q in range(7)])

    specs = [pl.BlockSpec(a.shape, lambda i, nd=a.ndim: (0,) * nd) for a in ws]
    outs = pl.pallas_call(
        body, name=name, grid=(1,), in_specs=specs * 4, out_specs=specs * 3,
        out_shape=[jax.ShapeDtypeStruct(a.shape, F32) for _ in range(3) for a in ws],
        compiler_params=_params(1),
    )(*ws, *gs, *ms, *vs)
    return outs[:n], outs[n:2 * n], outs[2 * n:]


def _ssm_discretize(a_re, a_im, log_dt, b_re, b_im):
    dt = jnp.exp(log_dt)[:, None]
    mag = jnp.exp(dt * a_re)
    abr = mag * jnp.cos(dt * a_im)
    abi = mag * jnp.sin(dt * a_im)
    den = a_re * a_re + a_im * a_im
    nr = abr - 1.0
    ni = abi
    f_re = (nr * a_re + ni * a_im) / den
    f_im = (ni * a_re - nr * a_im) / den
    bbr = f_re[..., None] * b_re - f_im[..., None] * b_im
    bbi = f_re[..., None] * b_im + f_im[..., None] * b_re
    return abr, abi, bbr, bbi


def _scan_tables(abr, abi):
    ar = abr.reshape(1, N_STATE)
    ai = abi.reshape(1, N_STATE)
    pr, pi = [ar], [ai]
    for _ in range(SUBLANE - 1):
        pr, pi = pr + [pr[-1] * ar - pi[-1] * ai], pi + [pr[-1] * ai + pi[-1] * ar]
    row = jnp.arange(SUBLANE)[:, None]
    tabs = []
    for d in (1, 2, 4):
        tabs.append(jnp.where(row >= d, pr[d - 1], 0.0))
        tabs.append(jnp.where(row >= d, pi[d - 1], 0.0))
    tabs.append(jnp.concatenate(pr, axis=0))
    tabs.append(jnp.concatenate(pi, axis=0))
    fwd = jnp.stack(tabs)
    sign = jnp.array([1.0, -1.0] * 4, F32)[:, None, None]
    return fwd, fwd[:, ::-1, :] * sign


def _block_diag_b(bb):
    strip = bb.transpose(2, 0, 1).reshape(SSM_H, N_STATE)
    rows = lax.broadcasted_iota(jnp.int32, (SSM_W, N_STATE), 0) // SSM_H
    cols = lax.broadcasted_iota(jnp.int32, (SSM_W, N_STATE), 1) // SSM_P
    return jnp.where(rows == cols, jnp.tile(strip, (SSM_G, 1)), 0.0).astype(BF16)


def _block_diag_c(cc):
    strip = cc.transpose(0, 2, 1).reshape(N_STATE, SSM_H)
    rows = lax.broadcasted_iota(jnp.int32, (N_STATE, SSM_W), 0) // SSM_P
    cols = lax.broadcasted_iota(jnp.int32, (N_STATE, SSM_W), 1) // SSM_H
    return jnp.where(rows == cols, jnp.tile(strip, (1, SSM_G)), 0.0).astype(BF16)


SMALL_SHAPES = {
    "g_mix": (D_MODEL,), "a_re": (SSM_G, SSM_P), "a_im": (SSM_G, SSM_P), "log_dt": (SSM_G,),
    "b_re": (SSM_G, SSM_P, SSM_H), "b_im": (SSM_G, SSM_P, SSM_H),
    "c_re": (SSM_G, SSM_H, SSM_P), "c_im": (SSM_G, SSM_H, SSM_P),
    "d_skip": (SSM_W,), "b_glu": (SSM_W,), "g_sgu": (SGU_W,), "w_s": (SGU_G, CHUNK, CHUNK),
    "b_s": (SGU_G, CHUNK), "g_ffn": (D_MODEL,), "conv_b": (2 * D_FF,), "g_final": (D_MODEL,),
}
PACK_ITEMS = [("loss", (1,))] + [(n, SMALL_SHAPES[n]) for n in SMALL] + [("conv_w", (3, 2 * D_FF))]
TILE = SUBLANE * LANE


def _item_rows(shape):
    return -(-math.prod(shape) // TILE) * SUBLANE


PACK_ROWS = -(-sum(_item_rows(s) for _, s in PACK_ITEMS) // (2 * SUBLANE)) * (2 * SUBLANE)


def _pack(values):
    parts, used = [], 0
    for name, shape in PACK_ITEMS:
        size, rows = math.prod(shape), _item_rows(shape)
        if name in values:
            flat = values[name].astype(F32).reshape(size)
            if rows * LANE > size:
                flat = jnp.pad(flat, (0, rows * LANE - size))
            parts.append(flat.reshape(rows, LANE))
        else:
            parts.append(jnp.zeros((rows, LANE), F32))
        used += rows
    if PACK_ROWS > used:
        parts.append(jnp.zeros((PACK_ROWS - used, LANE), F32))
    return jnp.concatenate(parts, axis=0)


def _unpack(pack):
    out, off = {}, 0
    for name, shape in PACK_ITEMS:
        rows = _item_rows(shape)
        out[name] = pack[off:off + rows].reshape(rows * LANE)[:math.prod(shape)].reshape(shape)
        off += rows
    return out


PLACE_ROWS = {"w_in": 256, "w_up": 256, "w_down": 352, "w_out": 256, "w_proj_a": 256,
              "w_proj_b": 256, "w_glu": 128}


def kernel(x, g_mix, w_in, a_re, a_im, log_dt, b_re, b_im, c_re, c_im, d_skip, w_glu, b_glu, w_proj_a, g_sgu, w_s, b_s, w_proj_b, w_out, g_ffn, w_up, conv_w, conv_b, w_down, g_final, loss_target, m_g_mix, m_w_in, m_a_re, m_a_im, m_log_dt, m_b_re, m_b_im, m_c_re, m_c_im, m_d_skip, m_w_glu, m_b_glu, m_w_proj_a, m_g_sgu, m_w_s, m_b_s, m_w_proj_b, m_w_out, m_g_ffn, m_w_up, m_conv_w, m_conv_b, m_w_down, m_g_final, v_g_mix, v_w_in, v_a_re, v_a_im, v_log_dt, v_b_re, v_b_im, v_c_re, v_c_im, v_d_skip, v_w_glu, v_b_glu, v_w_proj_a, v_g_sgu, v_w_s, v_b_s, v_w_proj_b, v_w_out, v_g_ffn, v_w_up, v_conv_w, v_conv_b, v_w_down, v_g_final):
    given = dict(locals())
    w = {n: given[n] for n in WEIGHTS}
    m = {n: given["m_" + n] for n in WEIGHTS}
    v = {n: given["v_" + n] for n in WEIGHTS}

    def shard2d(a):
        return a.reshape(a.shape[-2], a.shape[-1])

    chip = 2 * lax.axis_index("x") + lax.axis_index("y")
    where = jnp.stack([chip, lax.axis_index("c")]).astype(jnp.int32)
    xs, target = x[0], loss_target[0]
    small = {n: w[n].reshape(SMALL_SHAPES[n]) for n in SMALL}

    (abr, abi, bbr, bbi), disc_vjp = jax.vjp(_ssm_discretize, small["a_re"], small["a_im"],
                                             small["log_dt"], small["b_re"], small["b_im"])
    tab_f, tab_r = _scan_tables(abr, abi)
    bre = _block_diag_b(bbr)
    bim = _block_diag_b(bbi)
    cre = _block_diag_c(small["c_re"])
    cim = _block_diag_c(small["c_im"])
    tril = jnp.tril(jnp.ones((CHUNK, CHUNK), dtype=bool))
    ws = jnp.where(tril[None], small["w_s"], 0.0)
    ws_st = ws.reshape(SGU_G // 2, 2 * CHUNK, CHUNK).astype(BF16)
    wst_st = ws.transpose(0, 2, 1).reshape(SGU_G // 2, 2 * CHUNK, CHUNK).astype(BF16)
    bmat = jnp.repeat(small["b_s"].T, SGU_D, axis=1)
    g_mix2 = small["g_mix"].reshape(1, D_MODEL)
    g_ffn2 = small["g_ffn"].reshape(1, D_MODEL)
    g_final2 = small["g_final"].reshape(1, D_MODEL)
    g_sgu2 = small["g_sgu"].reshape(1, SGU_W)
    d_skip2 = small["d_skip"].reshape(1, SSM_W)
    b_glu2 = small["b_glu"].reshape(1, SSM_W)
    conv_b2 = small["conv_b"].reshape(1, 2 * D_FF)

    gat = {n: _place_shard(shard2d(w[n]), where, "place_" + n, BF16, PLACE_ROWS[n]) for n in BIG}
    gat["conv_w"] = _place_shard(shard2d(w["conv_w"]), where, "place_conv_w", F32, 3)
    all_rows = (0, D_MODEL)
    (gat["w_in"],), = _comm("gather_in", [_job_gather(
        [gat["w_in"]], [(0, all_rows, ICI, (0.0, 0.5)), (0, all_rows, SIBLING, (0.5, 1.0))])])
    mixers = ["w_glu", "w_proj_a", "w_proj_b", "w_out"]
    rows = {n: (0, gat[n].shape[1]) for n in mixers}
    down_a, down_b = (0, D_FF // 8), (D_FF // 8, D_FF // 8)
    up_a, up_b = (0, 3 * D_MODEL // 8), (3 * D_MODEL // 8, 5 * D_MODEL // 8)
    span = (0.0, 1.0)

    names = mixers + ["conv_w", "w_down"]
    (p, h1, bur, bui), (got,) = _fwd_in(
        xs, g_mix2, gat["w_in"], bre, bim,
        [_job_gather([gat[n] for n in names],
                     [(i, rows[n], ICI, span) for i, n in enumerate(mixers)]
                     + [(4, None, ICI, span), (5, down_a, ICI, span)])])
    gat.update(zip(names, got))
    names = mixers + ["w_down", "w_up"]
    (str_, sti), (got,) = _scan_fwd(
        bur, bui, tab_f,
        [_job_gather([gat[n] for n in names],
                     [(i, rows[n], SIBLING, span) for i, n in enumerate(mixers)]
                     + [(4, down_a, SIBLING, span), (4, down_b, ICI, span), (5, up_a, ICI, span)])])
    gat.update(zip(names, got))
    w_glu_f = gat["w_glu"].reshape(SSM_W, SSM_W)
    w_out_f = gat["w_out"].reshape(D_MODEL, D_MODEL)
    conv_w_f = gat["conv_w"].transpose(1, 0, 2).reshape(3, 2 * D_FF)
    (x2, y0, z, mixed, ya, yb), ((gat["w_down"], gat["w_up"]),) = _fwd_mix(
        xs, p, str_, sti, cre, cim, d_skip2, w_glu_f, b_glu2, gat["w_proj_a"], g_sgu2, ws_st, bmat,
        gat["w_proj_b"], w_out_f,
        [_job_gather([gat["w_down"], gat["w_up"]],
                     [(0, down_b, SIBLING, span), (1, up_a, SIBLING, span),
                      (1, up_b, ICI, (0.0, 0.75)), (1, up_b, SIBLING, (0.75, 1.0))])])
    w_down_f = gat["w_down"].reshape(D_FF, D_MODEL)
    up, act, f, h2, dx3, sm_ffn = _fwd_ffn(x2, target, g_ffn2, gat["w_up"], conv_w_f, conv_b2,
                                           w_down_f, g_final2)

    def leg1_done(names, got):
        return [_add_sibling(part[n], s, where, "add_sibling_" + n) for n, s in zip(names, got)]

    def leg2_done(names, sums, got):
        return [_add_chips(s, o, where, "add_chips_" + n) for n, s, o in zip(names, sums, got)]

    part, red = {}, {}
    part["w_down"] = _dw_rows(f, dx3, "dw_down", D_FF // 2, 2 * TK).reshape(
        N_CHIP, D_FF // N_CHIP, D_MODEL)
    (dx2, dup, sm_conv, sm_gffn), (got,) = _bwd_ffn(
        dx3, up, act, x2, g_ffn2, gat["w_up"], conv_w_f, w_down_f,
        [_job_sibling_halves([part["w_down"]])])
    sum_down = leg1_done(["w_down"], got)
    part["w_up"], (got,) = _dw_shards(h2, dup, "dw_up", 4 * TK, [_job_to_owner(sum_down)])
    red_down = leg2_done(["w_down"], sum_down, got)
    ((dsr, dsi, du_part, drest, mrg, dya, dyb, yap, dz, y1, sgu, dy0, sm_mix, dbm, dws),
     (got, (red["w_down"],))) = _bwd_mix(
        dx2, p, y0, z, mixed, ya, yb, w_out_f, gat["w_proj_a"], gat["w_proj_b"], w_glu_f, cre, cim,
        ws_st, wst_st, d_skip2, g_sgu2,
        [_job_sibling_halves([part["w_up"]]), _job_swap_halves(red_down)])
    sum_up = leg1_done(["w_up"], got)
    (lam_r, lam_i, dar8, dai8), (got,) = _scan_bwd(dsr, dsi, str_, sti, tab_r, [_job_to_owner(sum_up)])
    red_up = leg2_done(["w_up"], sum_up, got)
    mix4 = ["w_out", "w_proj_a", "w_proj_b", "w_glu"]
    part["w_out"] = _dw_cols(mrg, dx2, "dw_out", D_MODEL // 2, False)[0].reshape(
        N_CHIP, D_MODEL // N_CHIP, D_MODEL)
    part["w_proj_a"] = _dw_cols(yap, dya, "dw_proj_a", D_MODEL // N_CHIP, True)[0]
    part["w_proj_b"] = _dw_cols(sgu, dyb, "dw_proj_b", D_MODEL // N_CHIP, True)[0]
    part["w_glu"] = _dw_cols(y1, dz, "dw_glu", SSM_W, False)[0].reshape(
        N_CHIP, SSM_W // N_CHIP, SSM_W)
    (grad_x, dp, sm_gmix), (got, (red["w_up"],)) = _bwd_in(
        lam_r, lam_i, du_part, drest, xs, dx2, g_mix2, gat["w_in"], bre, bim,
        [_job_sibling_halves([part[n] for n in mix4]), _job_swap_halves(red_up)])
    sums_m = leg1_done(mix4, got)
    part["w_in"], (got,) = _dw_cols(h1, dp, "dw_in", IN_COLS // N_CHIP, True, [_job_to_owner(sums_m)])
    red_m = leg2_done(mix4, sums_m, got)
    (dbd_r, dbd_i), (got, done_m) = _dw_pair(
        p, SSM_W, lam_r, lam_i, "db_bar",
        [_job_sibling_halves([part["w_in"]]), _job_swap_halves(red_m)])
    red.update(zip(mix4, done_m))
    sum_in = leg1_done(["w_in"], got)
    (dcd_r, dcd_i), (got,) = _dw_pair(dy0, SSM_W, str_, sti, "dc", [_job_to_owner(sum_in)])
    red_in = leg2_done(["w_in"], sum_in, got)
    (red["w_in"],), = _comm("swap_w_in", [_job_swap_halves(red_in)])

    def pick_c(slabs):
        two = LANE // SSM_P
        return jnp.einsum("jshsp->jshp", slabs.reshape(SSM_G // two, two, SSM_H, two, SSM_P)
                          ).reshape(SSM_G, SSM_H, SSM_P)

    def pick_b(slabs):
        return pick_c(slabs).transpose(0, 2, 1)

    dabr = jnp.sum(dar8, axis=0).reshape(SSM_G, SSM_P)
    dabi = jnp.sum(dai8, axis=0).reshape(SSM_G, SSM_P)
    d_a_re, d_a_im, d_log_dt, d_b_re, d_b_im = disc_vjp((dabr, dabi, pick_b(dbd_r), pick_b(dbd_i)))
    gsmall = {
        "g_mix": sm_gmix[0], "a_re": d_a_re, "a_im": d_a_im, "log_dt": d_log_dt,
        "b_re": d_b_re, "b_im": d_b_im, "c_re": pick_c(dcd_r), "c_im": -pick_c(dcd_i),
        "d_skip": sm_mix[0], "b_glu": sm_mix[1], "g_sgu": sm_mix[2],
        "w_s": jnp.where(tril[None], dws, 0.0),
        "b_s": dbm.reshape(CHUNK, SGU_G, SGU_D).sum(-1).T,
        "g_ffn": sm_gffn[0], "conv_b": sm_conv[3], "g_final": sm_ffn[0],
        "conv_w": sm_conv[0:3], "loss": sm_ffn[1, 0:1],
    }

    total_pack = _small_allreduce(_pack(gsmall))
    total = _unpack(total_pack)
    grads = dict(red)
    cs = 2 * D_FF // N_CHIP
    grads["conv_w"] = lax.dynamic_slice(total["conv_w"], (0, chip * cs), (3, cs))
    delta, new_m, new_v = {}, {}, {}
    for n in BIG + ("conv_w",):
        delta[n], new_m[n], new_v[n] = _adamw(shard2d(w[n]), grads[n], shard2d(m[n]), shard2d(v[n]),
                                              "adamw_" + n, PLACE_ROWS.get(n, 3))
    for n in SMALL:
        grads[n] = total[n].reshape(w[n].shape)
    ud, um, uv = _adamw_many(*[[d[n] for n in SMALL] for d in (w, grads, m, v)], "adamw_small")
    for i, n in enumerate(SMALL):
        delta[n], new_m[n], new_v[n] = ud[i], um[i], uv[i]

    def like(d):
        return [d[n].reshape(w[n].shape) for n in WEIGHTS]

    return (total["loss"].reshape(()), grad_x.reshape(x.shape), *like(grads), *like(delta),
            *like(new_m), *like(new_v))
```

```python
import math

import jax
import jax.numpy as jnp
from jax import lax
from jax.experimental import pallas as pl
from jax.experimental.pallas import tpu as pltpu

F32 = jnp.float32
BF16 = jnp.bfloat16
MESH = pl.DeviceIdType.MESH

D_MODEL = 1024
SSM_W = 512
SSM_G = 32
SSM_H = 16
SSM_P = 64
N_STATE = SSM_G * SSM_P
DIAG_N = 128 * SSM_P // SSM_H
SGU_W = 512
SGU_G = 8
SGU_D = 64
CHUNK = 128
D_FF = 2816
IN_COLS = 3584
EPS = 1e-6
N_CHIP = 4

ADAM_LR = 0.001
ADAM_B1 = 0.9
ADAM_B2 = 0.999
ADAM_EPS = 1e-08
ADAM_WD = 0.01
ADAM_STEP = 10

SUBLANE = 8
LANE = 128
VMEM_LIMIT = 56 * 1024 * 1024
TB = 256
TK = 512
SCAN_LANES = 256
SCAN_UNROLL = 4
HALO = SUBLANE

BIG = ("w_in", "w_up", "w_down", "w_out", "w_proj_a", "w_proj_b", "w_glu")
SMALL = ("g_mix", "a_re", "a_im", "log_dt", "b_re", "b_im", "c_re", "c_im", "d_skip", "b_glu",
         "g_sgu", "w_s", "b_s", "g_ffn", "conv_b", "g_final")
WEIGHTS = ("g_mix", "w_in", "a_re", "a_im", "log_dt", "b_re", "b_im", "c_re", "c_im", "d_skip",
           "w_glu", "b_glu", "w_proj_a", "g_sgu", "w_s", "b_s", "w_proj_b", "w_out", "g_ffn",
           "w_up", "conv_w", "conv_b", "w_down", "g_final")

ANY = pl.BlockSpec(memory_space=pl.ANY)


def _params(n_grid):
    return pltpu.CompilerParams(dimension_semantics=("arbitrary",) * n_grid if n_grid else None,
                                vmem_limit_bytes=VMEM_LIMIT)


def _whole():
    return pl.BlockSpec(memory_space=pltpu.VMEM)


def _rows(tb, ncol):
    return pl.BlockSpec((tb, ncol), lambda i: (i, 0))


def _acc(nrow, ncol):
    return pl.BlockSpec((nrow, ncol), lambda i: (0, 0))


def _dot(a, b):
    return jnp.dot(a.astype(BF16), b.astype(BF16), preferred_element_type=F32)


def _dot_nt(a, b):
    return lax.dot_general(a.astype(BF16), b.astype(BF16), (((1,), (1,)), ((), ())),
                           preferred_element_type=F32)


def _sigmoid(v):
    return 0.5 * jnp.tanh(0.5 * v) + 0.5


_GELU_C = math.sqrt(2.0 / math.pi)


def _gelu(v):
    return 0.5 * v * (1.0 + jnp.tanh(_GELU_C * (v + 0.044715 * v * v * v)))


def _gelu_and_grad(v):
    v2 = v * v
    t = jnp.tanh(_GELU_C * v * (1.0 + 0.044715 * v2))
    half = 0.5 * (1.0 + t)
    return v * half, half + 0.5 * v * (1.0 - t * t) * _GELU_C * (1.0 + 3.0 * 0.044715 * v2)


def _rms_stats(v):
    r = lax.rsqrt(jnp.mean(v * v, axis=-1, keepdims=True) + EPS)
    return r, v * r


def _rms_bwd(dxh, xh, r):
    return r * (dxh - xh * jnp.mean(dxh * xh, axis=-1, keepdims=True))


def _place():
    x, y, c = lax.axis_index("x"), lax.axis_index("y"), lax.axis_index("c")
    chips = [(1 - x, y), (x, 1 - y), (1 - x, 1 - y)]
    return x, y, c, chips


def _chip_index(chip):
    return 2 * chip[0] + chip[1]


def _remote(src, dst, send_sem, recv_sem, device):
    return pltpu.make_async_remote_copy(src_ref=src, dst_ref=dst, send_sem=send_sem,
                                        recv_sem=recv_sem, device_id=device, device_id_type=MESH)


def _half(ref_rows, c):
    hr = ref_rows // 2
    return pl.ds(pl.multiple_of(c * hr, SUBLANE), hr)


class _Job:
    def __init__(self, hooks, n_sem, ins=(), inouts=(), outs=()):
        self.hooks, self.n_sem = list(hooks), n_sem
        self.ins, self.inouts, self.outs = list(ins), list(inouts), list(outs)


def _whole_span(start, finish):
    return [(0.0, "start", start), (1.0, "finish", finish)]


ICI, SIBLING = "ici", "sibling"


def _job_gather(bufs, legs):
    def copies(io, leg, first):
        b, window, kind, _ = legs[leg]
        x, y, c, chips = _place()
        k_me = 2 * x + y
        out = []
        for j, ch in enumerate(chips):
            k = _chip_index(ch)
            if window is None:
                src, land, dev = io[b].at[k_me], io[b].at[k], (*ch, c)
            else:
                r0, rows = window
                mine = pl.ds(pl.multiple_of(r0 + c * (rows // 2), SUBLANE), rows // 2)
                theirs = pl.ds(pl.multiple_of(r0 + (1 - c) * (rows // 2), SUBLANE), rows // 2)
                if kind == ICI:
                    src, land, dev = io[b].at[k_me, mine, :], io[b].at[k, mine, :], (*ch, c)
                else:
                    src, land, dev = io[b].at[k, mine, :], io[b].at[k, theirs, :], (x, y, 1 - c)
            out.append((src, land, first + j, dev))
        return out

    def starter(leg):
        def start(ins, io, outs, ssem, rsem):
            for src, _, i, dev in copies(io, leg, 3 * leg):
                _remote(src, src, ssem(i), rsem(i), dev).start()
        return start

    def finisher(leg):
        def finish(ins, io, outs, ssem, rsem):
            cps = copies(io, leg, 3 * leg)
            for _, land, i, dev in cps:
                _remote(land, land, ssem(i), rsem(i), dev).wait_recv()
            for src, _, i, dev in cps:
                _remote(src, src, ssem(i), rsem(i), dev).wait_send()
        return finish

    hooks = []
    for leg, (_, _, _, (begin, end)) in enumerate(legs):
        hooks += [(begin, "start", starter(leg)), (end, "finish", finisher(leg))]
    return _Job(hooks, 3 * len(legs), inouts=bufs)


def _job_sibling_halves(grads):
    n = len(grads)

    def build(ins, outs, ssem, rsem):
        x, y, c, _ = _place()
        return [_remote(ins[t].at[:, _half(grads[t].shape[1], 1 - c), :], outs[t], ssem(t), rsem(t),
                        (x, y, 1 - c)) for t in range(n)]

    def start(ins, io, outs, ssem, rsem):
        for cp in build(ins, outs, ssem, rsem):
            cp.start()

    def finish(ins, io, outs, ssem, rsem):
        for cp in build(ins, outs, ssem, rsem):
            cp.wait()

    return _Job(_whole_span(start, finish), n, ins=grads,
                outs=[jax.ShapeDtypeStruct((N_CHIP, g.shape[1] // 2, g.shape[2]), F32) for g in grads])


def _job_to_owner(sums):
    n = len(sums)

    def build(ins, outs, ssem, rsem):
        x, y, c, chips = _place()
        return [_remote(ins[t].at[_chip_index(ch)], outs[t].at[j], ssem(3 * t + j), rsem(3 * t + j),
                        (*ch, c)) for t in range(n) for j, ch in enumerate(chips)]

    def start(ins, io, outs, ssem, rsem):
        for cp in build(ins, outs, ssem, rsem):
            cp.start()

    def finish(ins, io, outs, ssem, rsem):
        for cp in build(ins, outs, ssem, rsem):
            cp.wait()

    return _Job(_whole_span(start, finish), 3 * n, ins=sums,
                outs=[jax.ShapeDtypeStruct((3,) + s.shape[1:], s.dtype) for s in sums])


def _job_swap_halves(bufs):
    n = len(bufs)

    def start(ins, io, outs, ssem, rsem):
        x, y, c, _ = _place()
        for t in range(n):
            mine = io[t].at[_half(bufs[t].shape[0], c), :]
            _remote(mine, mine, ssem(t), rsem(t), (x, y, 1 - c)).start()

    def finish(ins, io, outs, ssem, rsem):
        x, y, c, _ = _place()
        for t in range(n):
            theirs = io[t].at[_half(bufs[t].shape[0], 1 - c), :]
            _remote(theirs, theirs, ssem(t), rsem(t), (x, y, 1 - c)).wait_recv()
        for t in range(n):
            mine = io[t].at[_half(bufs[t].shape[0], c), :]
            _remote(mine, mine, ssem(t), rsem(t), (x, y, 1 - c)).wait_send()

    return _Job(_whole_span(start, finish), n, inouts=bufs)


def _call(body, name, grid, in_specs, out_specs, out_shape, args, jobs=(), scratch=()):
    n_in, n_out, n_scr = len(args), len(out_shape), len(scratch)
    job_in = [a for jb in jobs for a in jb.ins + jb.inouts]
    job_out = [s for jb in jobs
               for s in [jax.ShapeDtypeStruct(a.shape, a.dtype) for a in jb.inouts] + jb.outs]
    aliases, pos_in, pos_out = {}, n_in, n_out
    for jb in jobs:
        pos_in += len(jb.ins)
        for _ in jb.inouts:
            aliases[pos_in] = pos_out
            pos_in += 1
            pos_out += 1
        pos_out += len(jb.outs)
    n_sem = sum(jb.n_sem for jb in jobs)

    def wrapped(*refs):
        c_in = refs[:n_in]
        j_in = refs[n_in:n_in + len(job_in)]
        c_out = refs[n_in + len(job_in):n_in + len(job_in) + n_out]
        j_out = refs[n_in + len(job_in) + n_out:n_in + len(job_in) + n_out + len(job_out)]
        rest = refs[n_in + len(job_in) + n_out + len(job_out):]
        c_scr = rest[:n_scr]
        views, pi, po, ps = [], 0, 0, 0
        for jb in jobs:
            ins = j_in[pi:pi + len(jb.ins)]
            pi += len(jb.ins) + len(jb.inouts)
            io = j_out[po:po + len(jb.inouts)]
            new = j_out[po + len(jb.inouts):po + len(jb.inouts) + len(jb.outs)]
            po += len(jb.inouts) + len(jb.outs)
            send = (lambda i, o=ps: rest[n_scr].at[o + i])
            recv = (lambda i, o=ps: rest[n_scr + 1].at[o + i])
            ps += jb.n_sem
            views.append((ins, io, new, send, recv))

        def run(frac):
            for kind in ("finish", "start"):
                for jb, vw in zip(jobs, views):
                    for at, what, fn in jb.hooks:
                        if at == frac and what == kind:
                            fn(*vw)

        fracs = sorted({at for jb in jobs for at, _, _ in jb.hooks})
        if not grid:
            for frac in fracs:
                run(frac)
            return
        if jobs:
            assert len(grid) == 1 or set(fracs) <= {0.0, 1.0}
            first = pl.program_id(0) == 0
            last = pl.program_id(0) == grid[0] - 1
            for d in range(1, len(grid)):
                first = jnp.logical_and(first, pl.program_id(d) == 0)
                last = jnp.logical_and(last, pl.program_id(d) == grid[d] - 1)
            for frac in fracs:
                if frac < 1.0:
                    at_step = first if frac == 0.0 else pl.program_id(0) == int(frac * grid[0])
                    pl.when(at_step)(lambda frac=frac: run(frac))
        body(*c_in, *c_out, *c_scr)
        if jobs and 1.0 in fracs:
            pl.when(last)(lambda: run(1.0))

    sems = [pltpu.SemaphoreType.DMA((n_sem,)), pltpu.SemaphoreType.DMA((n_sem,))] if jobs else []
    kwargs = dict(grid=grid) if grid else {}
    res = pl.pallas_call(
        wrapped, name=name, in_specs=list(in_specs) + [ANY] * len(job_in),
        out_specs=list(out_specs) + [ANY] * len(job_out),
        out_shape=list(out_shape) + job_out, scratch_shapes=list(scratch) + sems,
        input_output_aliases=aliases, compiler_params=_params(len(grid)), **kwargs,
    )(*args, *job_in)
    outs, pos, per_job = list(res[:n_out]), n_out, []
    for jb in jobs:
        k = len(jb.inouts) + len(jb.outs)
        per_job.append(list(res[pos:pos + k]))
        pos += k
    return outs, per_job


def _comm(name, jobs):
    return _call(None, name, (), [], [], [], [], jobs)[1]


def _fwd_in(x, g_mix, w_in, bre, bim, jobs=()):
    t_len = x.shape[0]
    cs = IN_COLS // N_CHIP

    def body(x_ref, g_ref, w_ref, bre_ref, bim_ref, p_ref, h_ref, bur_ref, bui_ref):
        xv = x_ref[...]
        r, xh = _rms_stats(xv)
        h = (xh * g_ref[...]).astype(BF16)
        h_ref[...] = h
        for k in range(N_CHIP):
            p_ref[:, k * cs:(k + 1) * cs] = jnp.dot(h, w_ref[k],
                                                    preferred_element_type=F32).astype(BF16)
        u = p_ref[:, 0:SSM_W]
        for i in range(SSM_W // LANE):
            rows, cols = slice(i * LANE, (i + 1) * LANE), slice(i * DIAG_N, (i + 1) * DIAG_N)
            bur_ref[:, cols] = jnp.dot(u[:, rows], bre_ref[rows, cols],
                                       preferred_element_type=F32).astype(BF16)
            bui_ref[:, cols] = jnp.dot(u[:, rows], bim_ref[rows, cols],
                                       preferred_element_type=F32).astype(BF16)

    return _call(
        body, "fwd_in", (t_len // TB,),
        [_rows(TB, D_MODEL), _whole(), _whole(), _whole(), _whole()],
        [_rows(TB, IN_COLS), _rows(TB, D_MODEL), _rows(TB, N_STATE), _rows(TB, N_STATE)],
        [jax.ShapeDtypeStruct((t_len, IN_COLS), BF16), jax.ShapeDtypeStruct((t_len, D_MODEL), BF16),
         jax.ShapeDtypeStruct((t_len, N_STATE), BF16), jax.ShapeDtypeStruct((t_len, N_STATE), BF16)],
        [x, g_mix, w_in, bre, bim], jobs)


def _scan_local(xr, xi, tab, shifts):
    for q, s in enumerate(shifts):
        ar, ai = tab[2 * q], tab[2 * q + 1]
        rr = pltpu.roll(xr, s, 0)
        ri = pltpu.roll(xi, s, 0)
        xr, xi = xr + ar * rr - ai * ri, xi + ar * ri + ai * rr
    return xr, xi


def _scan_carry(xr, xi, tab, cr, ci):
    pr, pi = tab[6], tab[7]
    return xr + pr * cr - pi * ci, xi + pr * ci + pi * cr


BF16_TILE = 2 * SUBLANE


def _load_blocks(r_ref, i_ref, base):
    out = []
    for q in range(SCAN_UNROLL // 2):
        rows = pl.ds(pl.multiple_of(base + q * BF16_TILE, BF16_TILE), BF16_TILE)
        vr, vi = r_ref[rows, :].astype(F32), i_ref[rows, :].astype(F32)
        out += [(vr[:SUBLANE], vi[:SUBLANE]), (vr[SUBLANE:], vi[SUBLANE:])]
    return out


def _store_blocks(r_ref, i_ref, base, blocks):
    for q in range(SCAN_UNROLL // 2):
        rows = pl.ds(pl.multiple_of(base + q * BF16_TILE, BF16_TILE), BF16_TILE)
        r_ref[rows, :] = jnp.concatenate([blocks[2 * q][0], blocks[2 * q + 1][0]], 0).astype(r_ref.dtype)
        i_ref[rows, :] = jnp.concatenate([blocks[2 * q][1], blocks[2 * q + 1][1]], 0).astype(i_ref.dtype)


def _scan_fwd(bur, bui, tab, jobs=()):
    t_len = bur.shape[0]
    nblk = t_len // SUBLANE
    lb = SCAN_LANES

    def body(br_ref, bi_ref, tab_ref, sr_ref, si_ref):
        tab_v = [tab_ref[q] for q in range(8)]

        def step(k, carry):
            cr, ci = carry
            base = pl.multiple_of(k * SCAN_UNROLL * SUBLANE, SCAN_UNROLL * SUBLANE)
            local = [_scan_local(xr, xi, tab_v, (1, 2, 4))
                     for xr, xi in _load_blocks(br_ref, bi_ref, base)]
            done = []
            for xr, xi in local:
                xr, xi = _scan_carry(xr, xi, tab_v, cr, ci)
                done.append((xr, xi))
                cr, ci = xr[SUBLANE - 1:SUBLANE, :], xi[SUBLANE - 1:SUBLANE, :]
            _store_blocks(sr_ref, si_ref, base, done)
            return cr, ci

        zero = jnp.zeros((1, lb), F32)
        lax.fori_loop(0, nblk // SCAN_UNROLL, step, (zero, zero))

    col = pl.BlockSpec((t_len, lb), lambda j: (0, j))
    return _call(
        body, "scan_fwd", (N_STATE // lb,),
        [col, col, pl.BlockSpec((8, SUBLANE, lb), lambda j: (0, 0, j))], [col, col],
        [jax.ShapeDtypeStruct((t_len, N_STATE), BF16)] * 2, [bur, bui, tab], jobs)


def _sgu_mix(v, ws_ref, lane_lo):
    rows = []
    for c0 in range(0, v.shape[0], CHUNK):
        slabs = []
        for j in range(SGU_W // LANE):
            prod = jnp.dot(ws_ref[j], v[c0:c0 + CHUNK, j * LANE:(j + 1) * LANE].astype(BF16),
                           preferred_element_type=F32)
            slabs.append(jnp.where(lane_lo, prod[:CHUNK], prod[CHUNK:]))
        rows.append(jnp.concatenate(slabs, axis=1))
    return jnp.concatenate(rows, axis=0) if len(rows) > 1 else rows[0]


def _fwd_mix(x, p, str_, sti, cre, cim, d_skip, w_glu, b_glu, w_pa, g_sgu, ws_st, bmat, w_pb, w_out,
             jobs=()):
    t_len = x.shape[0]

    def body(x_ref, p_ref, sr_ref, si_ref, cre_ref, cim_ref, dsk_ref, wg_ref, bg_ref, wpa_ref,
             gs_ref, ws_ref, bm_ref, wpb_ref, wo_ref,
             x2_ref, y0_ref, z_ref, mx_ref, ya_ref, yb_ref):
        u = p_ref[:, 0:SSM_W].astype(F32)
        y0 = jnp.concatenate(
            [_dot(sr_ref[:, i * DIAG_N:(i + 1) * DIAG_N],
                  cre_ref[i * DIAG_N:(i + 1) * DIAG_N, i * LANE:(i + 1) * LANE])
             - _dot(si_ref[:, i * DIAG_N:(i + 1) * DIAG_N],
                    cim_ref[i * DIAG_N:(i + 1) * DIAG_N, i * LANE:(i + 1) * LANE])
             for i in range(SSM_W // LANE)], axis=1) + dsk_ref[...] * u
        y0_ref[...] = y0.astype(BF16)
        y1 = _gelu(y0)
        z = _dot(y1, wg_ref[...]) + bg_ref[...]
        z_ref[...] = z.astype(BF16)
        ya_pre = (y1 * _sigmoid(z)).astype(BF16)
        ya = jnp.concatenate([jnp.dot(ya_pre, wpa_ref[k], preferred_element_type=F32)
                              for k in range(N_CHIP)], axis=1)
        ya_ref[...] = ya.astype(BF16)

        uvg = _gelu(p_ref[:, SSM_W:SSM_W + 2 * SGU_W].astype(F32))
        u2 = uvg[:, :SGU_W]
        _, vh = _rms_stats(uvg[:, SGU_W:])
        v3 = vh * gs_ref[...]
        lane_lo = lax.broadcasted_iota(jnp.int32, (CHUNK, LANE), 1) < SGU_D
        bias = jnp.concatenate([bm_ref[...]] * (TB // CHUNK), axis=0)
        mixed = _sgu_mix(v3, ws_ref, lane_lo) + bias
        mx_ref[...] = mixed.astype(BF16)
        sgu = (u2 * mixed).astype(BF16)
        yb = jnp.concatenate([jnp.dot(sgu, wpb_ref[k], preferred_element_type=F32)
                              for k in range(N_CHIP)], axis=1)
        yb_ref[...] = yb.astype(BF16)

        lg0 = SSM_W + 2 * SGU_W
        ga = _sigmoid(p_ref[:, lg0:lg0 + D_MODEL].astype(F32))
        gb = _sigmoid(p_ref[:, lg0 + D_MODEL:lg0 + 2 * D_MODEL].astype(F32))
        mrg = ga * ya + gb * yb
        x2_ref[...] = x_ref[...] + _dot(mrg, wo_ref[...])

    return _call(
        body, "fwd_mix", (t_len // TB,),
        [_rows(TB, D_MODEL), _rows(TB, IN_COLS), _rows(TB, N_STATE), _rows(TB, N_STATE)]
        + [_whole()] * 11,
        [_rows(TB, D_MODEL), _rows(TB, SSM_W), _rows(TB, SSM_W), _rows(TB, SGU_W),
         _rows(TB, D_MODEL), _rows(TB, D_MODEL)],
        [jax.ShapeDtypeStruct((t_len, D_MODEL), F32), jax.ShapeDtypeStruct((t_len, SSM_W), BF16),
         jax.ShapeDtypeStruct((t_len, SSM_W), BF16), jax.ShapeDtypeStruct((t_len, SGU_W), BF16),
         jax.ShapeDtypeStruct((t_len, D_MODEL), BF16), jax.ShapeDtypeStruct((t_len, D_MODEL), BF16)],
        [x, p, str_, sti, cre, cim, d_skip, w_glu, b_glu, w_pa, g_sgu, ws_st, bmat, w_pb, w_out], jobs)


def _conv_taps(v, cw_ref, c0, width):
    w0 = cw_ref[0:1, c0:c0 + width]
    w1 = cw_ref[1:2, c0:c0 + width]
    w2 = cw_ref[2:3, c0:c0 + width]
    return w0 * pltpu.roll(v, 2, 0) + w1 * pltpu.roll(v, 1, 0) + w2 * v


def _fwd_ffn(x2, target, g_ffn, w_up, conv_w, conv_b, w_down, g_final):
    t_len = x2.shape[0]
    half = D_FF // 2
    blocks_per_halo = TB // HALO

    def body(x2_ref, xp_ref, tg_ref, gf_ref, wu_ref, cw_ref, cb_ref, wd_ref, gl_ref,
             up_ref, act_ref, f_ref, h2_ref, dx3_ref, sm_ref):
        i = pl.program_id(0)
        xe = jnp.concatenate([xp_ref[...] * jnp.where(i == 0, 0.0, 1.0), x2_ref[...]], axis=0)
        _, xh = _rms_stats(xe)
        h2 = (xh * gf_ref[...]).astype(BF16)
        h2_ref[...] = h2[HALO:]
        acc = jnp.zeros((TB, D_MODEL), F32)
        for hc in range(2):
            ca = hc * half
            cb = D_FF + hc * half
            ua = jnp.dot(h2, wu_ref[hc], preferred_element_type=F32)
            ub = jnp.dot(h2, wu_ref[2 + hc], preferred_element_type=F32)
            up_ref[:, ca:ca + half] = ua[HALO:].astype(BF16)
            up_ref[:, cb:cb + half] = ub[HALO:].astype(BF16)
            ac = _conv_taps(ua, cw_ref, ca, half)[HALO:] + cb_ref[:, ca:ca + half]
            bc = _conv_taps(ub, cw_ref, cb, half)[HALO:] + cb_ref[:, cb:cb + half]
            act_ref[:, ca:ca + half] = ac.astype(BF16)
            act_ref[:, cb:cb + half] = bc.astype(BF16)
            f = (ac * _sigmoid(ac) * bc).astype(BF16)
            f_ref[:, ca:ca + half] = f
            acc = acc + jnp.dot(f, wd_ref[ca:ca + half, :], preferred_element_type=F32)
        x3 = x2_ref[...] + acc
        r3, xh3 = _rms_stats(x3)
        err = xh3 * gl_ref[...] - tg_ref[...]
        dout = err * (1.0 / D_MODEL)
        dx3_ref[...] = _rms_bwd(dout * gl_ref[...], xh3, r3)
        dgl = jnp.sum(dout * xh3, axis=0, keepdims=True)
        loss = 0.5 * jnp.sum(jnp.mean(err * err, axis=-1, keepdims=True), axis=0, keepdims=True)
        upd = jnp.concatenate([dgl, jnp.broadcast_to(loss, (1, D_MODEL)),
                               jnp.zeros((SUBLANE - 2, D_MODEL), F32)], axis=0)

        @pl.when(i == 0)
        def _():
            sm_ref[...] = upd

        @pl.when(i > 0)
        def _():
            sm_ref[...] += upd

    prev = pl.BlockSpec((HALO, D_MODEL), lambda i: (jnp.maximum(i * blocks_per_halo - 1, 0), 0))
    return _call(
        body, "fwd_ffn", (t_len // TB,),
        [_rows(TB, D_MODEL), prev, _rows(TB, D_MODEL)] + [_whole()] * 6,
        [_rows(TB, 2 * D_FF), _rows(TB, 2 * D_FF), _rows(TB, D_FF), _rows(TB, D_MODEL),
         _rows(TB, D_MODEL), _acc(SUBLANE, D_MODEL)],
        [jax.ShapeDtypeStruct((t_len, 2 * D_FF), BF16), jax.ShapeDtypeStruct((t_len, 2 * D_FF), BF16),
         jax.ShapeDtypeStruct((t_len, D_FF), BF16), jax.ShapeDtypeStruct((t_len, D_MODEL), BF16),
         jax.ShapeDtypeStruct((t_len, D_MODEL), F32), jax.ShapeDtypeStruct((SUBLANE, D_MODEL), F32)],
        [x2, x2, target, g_ffn, w_up, conv_w, conv_b, w_down, g_final])[0]


def _bwd_ffn(dx3, up, act, x2, g_ffn, w_up, conv_w, w_down, jobs=()):
    t_len = x2.shape[0]
    half = D_FF // 2
    nblk = t_len // TB
    halo_b = 2 * HALO
    n_e = TB + HALO

    def body(dx_ref, dxn_ref, up_ref, act_ref, actn_ref, x2_ref, gf_ref, wu_ref, cw_ref,
             wd_ref, dx2_ref, dup_ref, smw_ref, smg_ref):
        i = pl.program_id(0)
        keep_last = jnp.where(i == nblk - 1, 0.0, 1.0)
        dxe = jnp.concatenate([dx_ref[...], dxn_ref[...] * keep_last], axis=0).astype(BF16)
        dh2 = jnp.zeros((TB, D_MODEL), F32)
        zpad = jnp.zeros((1, half), F32)
        for hc in range(2):
            ca = hc * half
            cb = D_FF + hc * half
            ac = jnp.concatenate([act_ref[:, ca:ca + half].astype(F32),
                                  actn_ref[:, ca:ca + half].astype(F32)[:HALO]], axis=0)
            bc = jnp.concatenate([act_ref[:, cb:cb + half].astype(F32),
                                  actn_ref[:, cb:cb + half].astype(F32)[:HALO]], axis=0)
            wa = [cw_ref[k:k + 1, ca:ca + half] for k in range(3)]
            wb = [cw_ref[k:k + 1, cb:cb + half] for k in range(3)]
            df = lax.dot_general(dxe, wd_ref[ca:ca + half, :], (((1,), (1,)), ((), ())),
                                 preferred_element_type=F32)
            sg = _sigmoid(ac)
            da = df * bc * sg * (1.0 + ac * (1.0 - sg))
            db = df * ac * sg
            da1, da2 = pltpu.roll(da, n_e - 1, 0), pltpu.roll(da, n_e - 2, 0)
            db1, db2 = pltpu.roll(db, n_e - 1, 0), pltpu.roll(db, n_e - 2, 0)
            dua = (wa[2] * da + wa[1] * da1 + wa[0] * da2)[:TB]
            dub = (wb[2] * db + wb[1] * db1 + wb[0] * db2)[:TB]
            dup_ref[:, ca:ca + half] = dua.astype(BF16)
            dup_ref[:, cb:cb + half] = dub.astype(BF16)
            dh2 = dh2 + _dot_nt(dua, wu_ref[hc]) + _dot_nt(dub, wu_ref[2 + hc])
            rows = []
            for u_, d0, d1, d2 in ((up_ref[:, ca:ca + half].astype(F32), da, da1, da2),
                                   (up_ref[:, cb:cb + half].astype(F32), db, db1, db2)):
                rows.append([jnp.sum(u_ * d2[:TB], axis=0, keepdims=True),
                             jnp.sum(u_ * d1[:TB], axis=0, keepdims=True),
                             jnp.sum(u_ * d0[:TB], axis=0, keepdims=True),
                             jnp.sum(d0[:TB], axis=0, keepdims=True)])
            for c0, rws in ((ca, rows[0]), (cb, rows[1])):
                upd = jnp.concatenate(rws + [zpad] * (SUBLANE - 4), axis=0)

                @pl.when(i == 0)
                def _(upd=upd, c0=c0):
                    smw_ref[:, c0:c0 + half] = upd

                @pl.when(i > 0)
                def _(upd=upd, c0=c0):
                    smw_ref[:, c0:c0 + half] += upd

        r2, xh2 = _rms_stats(x2_ref[...])
        dx2_ref[...] = dx_ref[...] + _rms_bwd(dh2 * gf_ref[...], xh2, r2)
        updg = jnp.concatenate([jnp.sum(dh2 * xh2, axis=0, keepdims=True),
                                jnp.zeros((SUBLANE - 1, D_MODEL), F32)], axis=0)

        @pl.when(i == 0)
        def _():
            smg_ref[...] = updg

        @pl.when(i > 0)
        def _():
            smg_ref[...] += updg

    nxt_d = pl.BlockSpec((HALO, D_MODEL),
                         lambda i: (jnp.minimum((i + 1) * (TB // HALO), t_len // HALO - 1), 0))
    nxt_a = pl.BlockSpec((halo_b, 2 * D_FF),
                         lambda i: (jnp.minimum((i + 1) * (TB // halo_b), t_len // halo_b - 1), 0))
    return _call(
        body, "bwd_ffn", (nblk,),
        [_rows(TB, D_MODEL), nxt_d, _rows(TB, 2 * D_FF), _rows(TB, 2 * D_FF), nxt_a,
         _rows(TB, D_MODEL)] + [_whole()] * 4,
        [_rows(TB, D_MODEL), _rows(TB, 2 * D_FF), _acc(SUBLANE, 2 * D_FF), _acc(SUBLANE, D_MODEL)],
        [jax.ShapeDtypeStruct((t_len, D_MODEL), F32), jax.ShapeDtypeStruct((t_len, 2 * D_FF), BF16),
         jax.ShapeDtypeStruct((SUBLANE, 2 * D_FF), F32), jax.ShapeDtypeStruct((SUBLANE, D_MODEL), F32)],
        [dx3, dx3, up, act, act, x2, g_ffn, w_up, conv_w, w_down], jobs)


def _bwd_mix(dx2, p, y0, z, mixed, ya, yb, w_out, w_pa, w_pb, w_glu, cre, cim, ws_st, wst_st,
             d_skip, g_sgu, jobs=()):
    t_len = dx2.shape[0]
    pc = D_MODEL // N_CHIP
    n_slab = SGU_W // LANE

    def body(dx_ref, p_ref, y0_ref, z_ref, mx_ref, ya_ref, yb_ref, wo_ref, wpa_ref, wpb_ref,
             wg_ref, cre_ref, cim_ref, ws_ref, wst_ref, dsk_ref, gs_ref,
             dsr_ref, dsi_ref, du_ref, drest_ref, mrg_ref, dya_ref, dyb_ref, yap_ref, dz_ref,
             y1_ref, sgu_ref, dy0_ref, sm_ref, dbm_ref, dws_ref):
        i = pl.program_id(0)
        first = i == 0
        lg0 = SSM_W + 2 * SGU_W
        dmrg = _dot_nt(dx_ref[...], wo_ref[...])
        ga = _sigmoid(p_ref[:, lg0:lg0 + D_MODEL].astype(F32))
        gb = _sigmoid(p_ref[:, lg0 + D_MODEL:lg0 + 2 * D_MODEL].astype(F32))
        yav = ya_ref[...].astype(F32)
        ybv = yb_ref[...].astype(F32)
        mrg_ref[...] = (ga * yav + gb * ybv).astype(BF16)
        drest_ref[:, 2 * SGU_W:2 * SGU_W + D_MODEL] = (dmrg * yav * ga * (1.0 - ga)).astype(BF16)
        drest_ref[:, 2 * SGU_W + D_MODEL:] = (dmrg * ybv * gb * (1.0 - gb)).astype(BF16)
        dya = (dmrg * ga).astype(BF16)
        dyb = (dmrg * gb).astype(BF16)
        dya_ref[...] = dya
        dyb_ref[...] = dyb

        y0v = y0_ref[...].astype(F32)
        y1, y1_grad = _gelu_and_grad(y0v)
        sz = _sigmoid(z_ref[...].astype(F32))
        y1_ref[...] = y1.astype(BF16)
        yap_ref[...] = (y1 * sz).astype(BF16)
        dyap = jnp.zeros((TB, SSM_W), F32)
        for k in range(N_CHIP):
            dyap = dyap + _dot_nt(dya[:, k * pc:(k + 1) * pc], wpa_ref[k])
        dz = dyap * y1 * sz * (1.0 - sz)
        dz_ref[...] = dz.astype(BF16)
        dy0 = (dyap * sz + _dot_nt(dz, wg_ref[...])) * y1_grad
        dy0_ref[...] = dy0.astype(BF16)
        u = p_ref[:, 0:SSM_W].astype(F32)
        du_ref[...] = dy0 * dsk_ref[...]
        for i in range(SSM_W // LANE):
            rows, cols = slice(i * DIAG_N, (i + 1) * DIAG_N), slice(i * LANE, (i + 1) * LANE)
            dsr_ref[:, rows] = _dot_nt(dy0[:, cols], cre_ref[rows, cols]).astype(BF16)
            dsi_ref[:, rows] = (-_dot_nt(dy0[:, cols], cim_ref[rows, cols])).astype(BF16)

        uv = p_ref[:, SSM_W:lg0].astype(F32)
        uvg, gg = _gelu_and_grad(uv)
        u2 = uvg[:, :SGU_W]
        rv, vh = _rms_stats(uvg[:, SGU_W:])
        v3 = vh * gs_ref[...]
        mixed = mx_ref[...].astype(F32)
        dsgu = jnp.zeros((TB, SGU_W), F32)
        for k in range(N_CHIP):
            dsgu = dsgu + _dot_nt(dyb[:, k * pc:(k + 1) * pc], wpb_ref[k])
        sgu_ref[...] = (u2 * mixed).astype(BF16)
        du2 = dsgu * mixed
        dmix = dsgu * u2
        lane_lo = lax.broadcasted_iota(jnp.int32, (CHUNK, LANE), 1) < SGU_D
        dv3 = _sgu_mix(dmix, wst_ref, lane_lo)
        dbm = jnp.zeros((CHUNK, SGU_W), F32)
        for c0 in range(0, TB, CHUNK):
            dbm = dbm + dmix[c0:c0 + CHUNK]
        for j in range(n_slab):
            lo = jnp.zeros((CHUNK, CHUNK), F32)
            hi = jnp.zeros((CHUNK, CHUNK), F32)
            for c0 in range(0, TB, CHUNK):
                dsl = dmix[c0:c0 + CHUNK, j * LANE:(j + 1) * LANE]
                vsl = v3[c0:c0 + CHUNK, j * LANE:(j + 1) * LANE]
                lo = lo + _dot_nt(jnp.where(lane_lo, dsl, 0.0), vsl)
                hi = hi + _dot_nt(jnp.where(lane_lo, 0.0, dsl), vsl)

            @pl.when(first)
            def _(lo=lo, hi=hi, j=j):
                dws_ref[2 * j] = lo
                dws_ref[2 * j + 1] = hi

            @pl.when(jnp.logical_not(first))
            def _(lo=lo, hi=hi, j=j):
                dws_ref[2 * j] += lo
                dws_ref[2 * j + 1] += hi

        dv2 = _rms_bwd(dv3 * gs_ref[...], vh, rv)
        drest_ref[:, 0:SGU_W] = (du2 * gg[:, :SGU_W]).astype(BF16)
        drest_ref[:, SGU_W:2 * SGU_W] = (dv2 * gg[:, SGU_W:]).astype(BF16)

        upd = jnp.concatenate([jnp.sum(dy0 * u, axis=0, keepdims=True),
                               jnp.sum(dz, axis=0, keepdims=True),
                               jnp.sum(dv3 * vh, axis=0, keepdims=True),
                               jnp.zeros((SUBLANE - 3, SSM_W), F32)], axis=0)

        @pl.when(first)
        def _():
            sm_ref[...] = upd
            dbm_ref[...] = dbm

        @pl.when(jnp.logical_not(first))
        def _():
            sm_ref[...] += upd
            dbm_ref[...] += dbm

    rest = 2 * SGU_W + 2 * D_MODEL
    bf_d, bf_s = jax.ShapeDtypeStruct((t_len, D_MODEL), BF16), jax.ShapeDtypeStruct((t_len, SSM_W), BF16)
    return _call(
        body, "bwd_mix", (t_len // TB,),
        [_rows(TB, D_MODEL), _rows(TB, IN_COLS), _rows(TB, SSM_W), _rows(TB, SSM_W),
         _rows(TB, SGU_W), _rows(TB, D_MODEL), _rows(TB, D_MODEL)] + [_whole()] * 10,
        [_rows(TB, N_STATE), _rows(TB, N_STATE), _rows(TB, SSM_W), _rows(TB, rest),
         _rows(TB, D_MODEL), _rows(TB, D_MODEL), _rows(TB, D_MODEL), _rows(TB, SSM_W),
         _rows(TB, SSM_W), _rows(TB, SSM_W), _rows(TB, SGU_W), _rows(TB, SSM_W),
         _acc(SUBLANE, SSM_W), _acc(CHUNK, SGU_W),
         pl.BlockSpec((SGU_G, CHUNK, CHUNK), lambda i: (0, 0, 0))],
        [jax.ShapeDtypeStruct((t_len, N_STATE), BF16), jax.ShapeDtypeStruct((t_len, N_STATE), BF16),
         jax.ShapeDtypeStruct((t_len, SSM_W), F32), jax.ShapeDtypeStruct((t_len, rest), BF16),
         bf_d, bf_d, bf_d, bf_s, bf_s, bf_s, bf_s, bf_s,
         jax.ShapeDtypeStruct((SUBLANE, SSM_W), F32), jax.ShapeDtypeStruct((CHUNK, SGU_W), F32),
         jax.ShapeDtypeStruct((SGU_G, CHUNK, CHUNK), F32)],
        [dx2, p, y0, z, mixed, ya, yb, w_out, w_pa, w_pb, w_glu, cre, cim, ws_st, wst_st, d_skip,
         g_sgu], jobs)


def _scan_bwd(dsr, dsi, str_, sti, tab_rev, jobs=()):
    t_len = dsr.shape[0]
    nblk = t_len // SUBLANE
    lb = SCAN_LANES

    def body(dr_ref, di_ref, sr_ref, si_ref, tab_ref, lr_ref, li_ref, dar_ref, dai_ref):
        tab_v = [tab_ref[q] for q in range(8)]
        row0 = lax.broadcasted_iota(jnp.int32, (SUBLANE, lb), 0) == 0
        tile = BF16_TILE

        def step(k, carry):
            cr, ci, acr, aci = carry
            base = pl.multiple_of((nblk - (k + 1) * SCAN_UNROLL) * SUBLANE, SCAN_UNROLL * SUBLANE)
            state = _load_blocks(sr_ref, si_ref, base)
            before = pl.ds(pl.multiple_of(jnp.maximum(base - tile, 0), tile), tile)
            has_before = jnp.where(base > 0, 1.0, 0.0)
            prev = (sr_ref[before, :].astype(F32)[tile - 1:tile] * has_before,
                    si_ref[before, :].astype(F32)[tile - 1:tile] * has_before)
            local = [_scan_local(xr, xi, tab_v, (7, 6, 4))
                     for xr, xi in _load_blocks(dr_ref, di_ref, base)]
            lam = [None] * SCAN_UNROLL
            for b in reversed(range(SCAN_UNROLL)):
                xr, xi = _scan_carry(*local[b], tab_v, cr, ci)
                lam[b] = (xr, xi)
                cr, ci = xr[0:1, :], xi[0:1, :]
                pr, pi = prev if b == 0 else (state[b - 1][0][SUBLANE - 1:], state[b - 1][1][SUBLANE - 1:])
                s_r = jnp.where(row0, pr, pltpu.roll(state[b][0], 1, 0))
                s_i = jnp.where(row0, pi, pltpu.roll(state[b][1], 1, 0))
                acr = acr + xr * s_r + xi * s_i
                aci = aci + xi * s_r - xr * s_i
            _store_blocks(lr_ref, li_ref, base, lam)
            return cr, ci, acr, aci

        zero = jnp.zeros((1, lb), F32)
        zacc = jnp.zeros((SUBLANE, lb), F32)
        _, _, acr, aci = lax.fori_loop(0, nblk // SCAN_UNROLL, step, (zero, zero, zacc, zacc))
        dar_ref[...] = acr
        dai_ref[...] = aci

    col = pl.BlockSpec((t_len, lb), lambda j: (0, j))
    small = pl.BlockSpec((SUBLANE, lb), lambda j: (0, j))
    return _call(
        body, "scan_bwd", (N_STATE // lb,),
        [col, col, col, col, pl.BlockSpec((8, SUBLANE, lb), lambda j: (0, 0, j))],
        [col, col, small, small],
        [jax.ShapeDtypeStruct((t_len, N_STATE), BF16)] * 2
        + [jax.ShapeDtypeStruct((SUBLANE, N_STATE), F32)] * 2,
        [dsr, dsi, str_, sti, tab_rev], jobs)


def _bwd_in(lam_r, lam_i, du_part, drest, x, dx2, g_mix, w_in, bre, bim, jobs=()):
    t_len = x.shape[0]
    cs = IN_COLS // N_CHIP

    def body(lr_ref, li_ref, du_ref, dr_ref, x_ref, dx2_ref, g_ref, w_ref, bre_ref, bim_ref,
             gx_ref, dp_ref, sm_ref):
        i = pl.program_id(0)
        du = du_ref[...] + jnp.concatenate(
            [_dot_nt(lr_ref[:, i * DIAG_N:(i + 1) * DIAG_N],
                     bre_ref[i * LANE:(i + 1) * LANE, i * DIAG_N:(i + 1) * DIAG_N])
             + _dot_nt(li_ref[:, i * DIAG_N:(i + 1) * DIAG_N],
                       bim_ref[i * LANE:(i + 1) * LANE, i * DIAG_N:(i + 1) * DIAG_N])
             for i in range(SSM_W // LANE)], axis=1)
        dp_ref[:, 0:SSM_W] = du.astype(BF16)
        dp_ref[:, SSM_W:] = dr_ref[...]
        dh = jnp.zeros((TB, D_MODEL), F32)
        for k in range(N_CHIP):
            dh = dh + _dot_nt(dp_ref[:, k * cs:(k + 1) * cs], w_ref[k])
        r, xh = _rms_stats(x_ref[...])
        gx_ref[...] = dx2_ref[...] + _rms_bwd(dh * g_ref[...], xh, r)
        upd = jnp.concatenate([jnp.sum(dh * xh, axis=0, keepdims=True),
                               jnp.zeros((SUBLANE - 1, D_MODEL), F32)], axis=0)

        @pl.when(i == 0)
        def _():
            sm_ref[...] = upd

        @pl.when(i > 0)
        def _():
            sm_ref[...] += upd

    return _call(
        body, "bwd_in", (t_len // TB,),
        [_rows(TB, N_STATE), _rows(TB, N_STATE), _rows(TB, SSM_W), _rows(TB, IN_COLS - SSM_W),
         _rows(TB, D_MODEL), _rows(TB, D_MODEL)] + [_whole()] * 4,
        [_rows(TB, D_MODEL), _rows(TB, IN_COLS), _acc(SUBLANE, D_MODEL)],
        [jax.ShapeDtypeStruct((t_len, D_MODEL), F32), jax.ShapeDtypeStruct((t_len, IN_COLS), BF16),
         jax.ShapeDtypeStruct((SUBLANE, D_MODEL), F32)],
        [lam_r, lam_i, du_part, drest, x, dx2, g_mix, w_in, bre, bim], jobs)


def _matmul_tn(a, b, name, out_shape, grid_ij, a_blk, a_map, b_blk, b_map, o_blk, o_map, jobs=()):
    tk = a_blk[0]
    nk = a.shape[0] // tk
    assert nk * tk == a.shape[0] and nk > 0

    def body(a_ref, b_ref, o_ref, acc_ref):
        k = pl.program_id(2)

        @pl.when(k == 0)
        def _():
            acc_ref[...] = jnp.zeros_like(acc_ref)

        acc_ref[...] += lax.dot_general(a_ref[...].astype(BF16), b_ref[...].astype(BF16),
                                        (((0,), (0,)), ((), ())), preferred_element_type=F32)

        @pl.when(k == nk - 1)
        def _():
            o_ref[...] = acc_ref[...]

    outs, per_job = _call(
        body, name, (grid_ij[0], grid_ij[1], nk),
        [pl.BlockSpec(a_blk, a_map), pl.BlockSpec(b_blk, b_map)], [pl.BlockSpec(o_blk, o_map)],
        [jax.ShapeDtypeStruct(out_shape, F32)], [a, b], jobs,
        scratch=[pltpu.VMEM((a_blk[1], b_blk[1]), F32)])
    return outs[0], per_job


def _dw_shards(a, b, name, tk, jobs=()):
    m, n = a.shape[1], b.shape[1]
    tn = n // N_CHIP
    tk = min(tk, a.shape[0])
    return _matmul_tn(a, b, name, (N_CHIP, m, tn), (1, N_CHIP),
                      (tk, m), lambda i, j, k: (k, 0), (tk, tn), lambda i, j, k: (k, j),
                      (None, m, tn), lambda i, j, k: (j, 0, 0), jobs)


def _dw_rows(a, b, name, tm, tk):
    m, n = a.shape[1], b.shape[1]
    tk = min(tk, a.shape[0])
    return _matmul_tn(a, b, name, (m, n), (m // tm, 1),
                      (tk, tm), lambda i, j, k: (k, i), (tk, n), lambda i, j, k: (k, 0),
                      (tm, n), lambda i, j, k: (i, 0))[0]


def _dw_cols(a, b, name, tn, sharded, jobs=()):
    t_len, m = a.shape
    n = b.shape[1]

    def body(a_ref, b_ref, o_ref):
        o_ref[...] = lax.dot_general(a_ref[...].astype(BF16), b_ref[...].astype(BF16),
                                     (((0,), (0,)), ((), ())), preferred_element_type=F32)

    if sharded:
        o_spec, o_shape = pl.BlockSpec((None, m, tn), lambda j: (j, 0, 0)), (n // tn, m, tn)
    else:
        o_spec, o_shape = pl.BlockSpec((m, tn), lambda j: (0, j)), (m, n)
    outs, per_job = _call(body, name, (n // tn,),
                          [_whole(), pl.BlockSpec((t_len, tn), lambda j: (0, j))], [o_spec],
                          [jax.ShapeDtypeStruct(o_shape, F32)], [a, b], jobs)
    return outs[0], per_job


def _dw_pair(a, m, b1, b2, name, jobs=()):
    t_len = a.shape[0]
    n_slab = DIAG_N // LANE
    rows_per_slab = LANE // n_slab

    def body(a_ref, b1_ref, b2_ref, o1_ref, o2_ref):
        for b_ref, o_ref in ((b1_ref, o1_ref), (b2_ref, o2_ref)):
            prod = lax.dot_general(a_ref[...].astype(BF16), b_ref[...].astype(BF16),
                                   (((0,), (0,)), ((), ())), preferred_element_type=F32)
            for j in range(n_slab):
                rows = slice(j * rows_per_slab, (j + 1) * rows_per_slab)
                o_ref[rows, :] = prod[rows, j * LANE:(j + 1) * LANE]

    tok = pl.BlockSpec((t_len, DIAG_N), lambda i: (0, i))
    out = pl.BlockSpec((LANE, LANE), lambda i: (i, 0))
    return _call(body, name, (m // LANE,),
                 [pl.BlockSpec((t_len, LANE), lambda i: (0, i)), tok, tok], [out, out],
                 [jax.ShapeDtypeStruct((m, LANE), F32)] * 2, [a, b1, b2], jobs)


def _prefetch_call(body, name, grid, scalars, in_specs, out_specs, out_shape, args):
    return pl.pallas_call(
        body, name=name,
        grid_spec=pltpu.PrefetchScalarGridSpec(num_scalar_prefetch=1, grid=grid, in_specs=in_specs,
                                               out_specs=out_specs),
        out_shape=out_shape, compiler_params=_params(len(grid)),
    )(scalars, *args)


def _place_shard(w, where, name, dtype, tr):
    rows, cols = w.shape

    def body(s_ref, w_ref, o_ref):
        o_ref[...] = w_ref[...].astype(dtype)

    return _prefetch_call(
        body, name, (rows // tr,), where,
        [pl.BlockSpec((tr, cols), lambda i, s: (i, 0))],
        pl.BlockSpec((None, tr, cols), lambda i, s: (s[0], i, 0)),
        jax.ShapeDtypeStruct((N_CHIP, rows, cols), dtype), [w])


def _place_shards(ws, where, name, dtype):
    n = len(ws)

    def body(s_ref, *refs):
        for t in range(n):
            refs[n + t][...] = refs[t][...].astype(dtype)

    return _prefetch_call(
        body, name, (1,), where,
        [pl.BlockSpec(w.shape, lambda i, s: (0, 0)) for w in ws],
        [pl.BlockSpec((None,) + w.shape, lambda i, s: (s[0], 0, 0)) for w in ws],
        [jax.ShapeDtypeStruct((N_CHIP,) + w.shape, dtype) for w in ws], ws)


def _add_sibling(gs, gots, where, name):
    n = len(gs)
    halves = [(g.shape[1] // 2, g.shape[2]) for g in gs]

    def body(s_ref, *refs):
        for t in range(n):
            refs[2 * n + t][...] = (refs[t][...] + refs[n + t][...]).astype(BF16)

    return _prefetch_call(
        body, name, (N_CHIP,), where,
        [pl.BlockSpec((None, hr, cs), lambda k, s: (k, s[1], 0)) for hr, cs in halves]
        + [pl.BlockSpec((None, hr, cs), lambda k, s: (k, 0, 0)) for hr, cs in halves],
        [pl.BlockSpec((None, hr, cs), lambda k, s: (k, 0, 0)) for hr, cs in halves],
        [jax.ShapeDtypeStruct((N_CHIP, hr, cs), BF16) for hr, cs in halves], list(gs) + list(gots))


def _add_chips(sums, gots, where, name):
    n = len(sums)
    halves = [s.shape[1:] for s in sums]

    def body(s_ref, *refs):
        for t in range(n):
            own_ref, got_ref = refs[t], refs[n + t]
            refs[2 * n + t][...] = ((own_ref[...].astype(F32) + got_ref[0].astype(F32))
                                    + got_ref[1].astype(F32)) + got_ref[2].astype(F32)

    return _prefetch_call(
        body, name, (1,), where,
        [pl.BlockSpec((None, hr, cs), lambda i, s: (s[0], 0, 0)) for hr, cs in halves]
        + [pl.BlockSpec((3, hr, cs), lambda i, s: (0, 0, 0)) for hr, cs in halves],
        [pl.BlockSpec((hr, cs), lambda i, s: (s[1], 0)) for hr, cs in halves],
        [jax.ShapeDtypeStruct((2 * hr, cs), F32) for hr, cs in halves], list(sums) + list(gots))


def _small_allreduce(pack):
    rows = pack.shape[0]
    half = rows // 2

    def body(in_ref, out_ref, sib_ref, slots_ref, s_a, r_a, s_b, r_b, s_c, r_c):
        x, y, c, chips = _place()
        k_me = 2 * x + y
        sib = (x, y, 1 - c)
        first = _remote(in_ref, sib_ref, s_a, r_a, sib)
        first.start()
        first.wait()
        mine = _half(rows, c)
        slots_ref[k_me] = in_ref[mine, :] + sib_ref[mine, :]
        cps = [_remote(slots_ref.at[k_me], slots_ref.at[k_me], s_b.at[j], r_b.at[j], (*ch, c))
               for j, ch in enumerate(chips)]
        for cp in cps:
            cp.start()
        for j, ch in enumerate(chips):
            slot = slots_ref.at[_chip_index(ch)]
            _remote(slot, slot, s_b.at[j], r_b.at[j], (*ch, c)).wait_recv()
        for cp in cps:
            cp.wait_send()
        out_ref[mine, :] = ((slots_ref[0] + slots_ref[1]) + slots_ref[2]) + slots_ref[3]
        last = _remote(out_ref.at[mine, :], out_ref.at[mine, :], s_c, r_c, sib)
        last.start()
        theirs = out_ref.at[_half(rows, 1 - c), :]
        _remote(theirs, theirs, s_c, r_c, sib).wait_recv()
        last.wait_send()

    return pl.pallas_call(
        body, name="small_allreduce", in_specs=[_whole()], out_specs=_whole(),
        out_shape=jax.ShapeDtypeStruct(pack.shape, F32),
        scratch_shapes=[pltpu.VMEM(pack.shape, F32), pltpu.VMEM((N_CHIP, half, LANE), F32),
                        pltpu.SemaphoreType.DMA, pltpu.SemaphoreType.DMA,
                        pltpu.SemaphoreType.DMA((3,)), pltpu.SemaphoreType.DMA((3,)),
                        pltpu.SemaphoreType.DMA, pltpu.SemaphoreType.DMA],
        compiler_params=_params(0),
    )(pack)


def _adamw_update(w_ref, g_ref, m_ref, v_ref, d_ref, mo_ref, vo_ref):
    gv = g_ref[...]
    mn = ADAM_B1 * m_ref[...] + (1.0 - ADAM_B1) * gv
    vn = ADAM_B2 * v_ref[...] + (1.0 - ADAM_B2) * (gv * gv)
    mo_ref[...] = mn
    vo_ref[...] = vn
    m_hat = mn / (1.0 - ADAM_B1 ** ADAM_STEP)
    v_hat = vn / (1.0 - ADAM_B2 ** ADAM_STEP)
    d_ref[...] = -ADAM_LR * (m_hat / (jnp.sqrt(v_hat) + ADAM_EPS) + ADAM_WD * w_ref[...])


def _adamw(w, g, m, v, name, tr):
    rows, cols = w.shape
    blk = _rows(tr, cols)

    def body(w_ref, g_ref, m_ref, v_ref, go_ref, d_ref, mo_ref, vo_ref):
        go_ref[...] = g_ref[...]
        _adamw_update(w_ref, g_ref, m_ref, v_ref, d_ref, mo_ref, vo_ref)

    return _call(body, name, (rows // tr,), [blk] * 4, [blk] * 4,
                 [jax.ShapeDtypeStruct(w.shape, F32)] * 4, [w, g, m, v])[0]


def _adamw_many(ws, gs, ms, vs, name):
    n = len(ws)

    def body(*refs):
        for t in range(n):
            _adamw_update(*[refs[q * n + t] for q in range(7)])

    specs = [pl.BlockSpec(a.shape, lambda i, nd=a.ndim: (0,) * nd) for a in ws]
    outs = pl.pallas_call(
        body, name=name, grid=(1,), in_specs=specs * 4, out_specs=specs * 3,
        out_shape=[jax.ShapeDtypeStruct(a.shape, F32) for _ in range(3) for a in ws],
        compiler_params=_params(1),
    )(*ws, *gs, *ms, *vs)
    return outs[:n], outs[n:2 * n], outs[2 * n:]


def _ssm_discretize(a_re, a_im, log_dt, b_re, b_im):
    dt = jnp.exp(log_dt)[:, None]
    mag = jnp.exp(dt * a_re)
    abr = mag * jnp.cos(dt * a_im)
    abi = mag * jnp.sin(dt * a_im)
    den = a_re * a_re + a_im * a_im
    nr = abr - 1.0
    ni = abi
    f_re = (nr * a_re + ni * a_im) / den
    f_im = (ni * a_re - nr * a_im) / den
    bbr = f_re[..., None] * b_re - f_im[..., None] * b_im
    bbi = f_re[..., None] * b_im + f_im[..., None] * b_re
    return abr, abi, bbr, bbi


def _scan_tables(abr, abi):
    ar = abr.reshape(1, N_STATE)
    ai = abi.reshape(1, N_STATE)
    pr, pi = [ar], [ai]
    for _ in range(SUBLANE - 1):
        pr, pi = pr + [pr[-1] * ar - pi[-1] * ai], pi + [pr[-1] * ai + pi[-1] * ar]
    row = jnp.arange(SUBLANE)[:, None]
    tabs = []
    for d in (1, 2, 4):
        tabs.append(jnp.where(row >= d, pr[d - 1], 0.0))
        tabs.append(jnp.where(row >= d, pi[d - 1], 0.0))
    tabs.append(jnp.concatenate(pr, axis=0))
    tabs.append(jnp.concatenate(pi, axis=0))
    fwd = jnp.stack(tabs)
    sign = jnp.array([1.0, -1.0] * 4, F32)[:, None, None]
    return fwd, fwd[:, ::-1, :] * sign


def _block_diag_b(bb):
    strip = bb.transpose(2, 0, 1).reshape(SSM_H, N_STATE)
    rows = lax.broadcasted_iota(jnp.int32, (SSM_W, N_STATE), 0) // SSM_H
    cols = lax.broadcasted_iota(jnp.int32, (SSM_W, N_STATE), 1) // SSM_P
    return jnp.where(rows == cols, jnp.tile(strip, (SSM_G, 1)), 0.0).astype(BF16)


def _block_diag_c(cc):
    strip = cc.transpose(0, 2, 1).reshape(N_STATE, SSM_H)
    rows = lax.broadcasted_iota(jnp.int32, (N_STATE, SSM_W), 0) // SSM_P
    cols = lax.broadcasted_iota(jnp.int32, (N_STATE, SSM_W), 1) // SSM_H
    return jnp.where(rows == cols, jnp.tile(strip, (1, SSM_G)), 0.0).astype(BF16)


SMALL_SHAPES = {
    "g_mix": (D_MODEL,), "a_re": (SSM_G, SSM_P), "a_im": (SSM_G, SSM_P), "log_dt": (SSM_G,),
    "b_re": (SSM_G, SSM_P, SSM_H), "b_im": (SSM_G, SSM_P, SSM_H),
    "c_re": (SSM_G, SSM_H, SSM_P), "c_im": (SSM_G, SSM_H, SSM_P),
    "d_skip": (SSM_W,), "b_glu": (SSM_W,), "g_sgu": (SGU_W,), "w_s": (SGU_G, CHUNK, CHUNK),
    "b_s": (SGU_G, CHUNK), "g_ffn": (D_MODEL,), "conv_b": (2 * D_FF,), "g_final": (D_MODEL,),
}
PACK_ITEMS = [("loss", (1,))] + [(n, SMALL_SHAPES[n]) for n in SMALL] + [("conv_w", (3, 2 * D_FF))]
TILE = SUBLANE * LANE


def _item_rows(shape):
    return -(-math.prod(shape) // TILE) * SUBLANE


PACK_ROWS = -(-sum(_item_rows(s) for _, s in PACK_ITEMS) // (2 * SUBLANE)) * (2 * SUBLANE)


def _pack(values):
    parts, used = [], 0
    for name, shape in PACK_ITEMS:
        size, rows = math.prod(shape), _item_rows(shape)
        if name in values:
            flat = values[name].astype(F32).reshape(size)
            if rows * LANE > size:
                flat = jnp.pad(flat, (0, rows * LANE - size))
            parts.append(flat.reshape(rows, LANE))
        else:
            parts.append(jnp.zeros((rows, LANE), F32))
        used += rows
    if PACK_ROWS > used:
        parts.append(jnp.zeros((PACK_ROWS - used, LANE), F32))
    return jnp.concatenate(parts, axis=0)


def _unpack(pack):
    out, off = {}, 0
    for name, shape in PACK_ITEMS:
        rows = _item_rows(shape)
        out[name] = pack[off:off + rows].reshape(rows * LANE)[:math.prod(shape)].reshape(shape)
        off += rows
    return out


PLACE_ROWS = {"w_in": 256, "w_up": 256, "w_down": 352, "w_out": 256, "w_proj_a": 256,
              "w_proj_b": 256, "w_glu": 128}


def kernel(x, g_mix, w_in, a_re, a_im, log_dt, b_re, b_im, c_re, c_im, d_skip, w_glu, b_glu, w_proj_a, g_sgu, w_s, b_s, w_proj_b, w_out, g_ffn, w_up, conv_w, conv_b, w_down, g_final, loss_target, m_g_mix, m_w_in, m_a_re, m_a_im, m_log_dt, m_b_re, m_b_im, m_c_re, m_c_im, m_d_skip, m_w_glu, m_b_glu, m_w_proj_a, m_g_sgu, m_w_s, m_b_s, m_w_proj_b, m_w_out, m_g_ffn, m_w_up, m_conv_w, m_conv_b, m_w_down, m_g_final, v_g_mix, v_w_in, v_a_re, v_a_im, v_log_dt, v_b_re, v_b_im, v_c_re, v_c_im, v_d_skip, v_w_glu, v_b_glu, v_w_proj_a, v_g_sgu, v_w_s, v_b_s, v_w_proj_b, v_w_out, v_g_ffn, v_w_up, v_conv_w, v_conv_b, v_w_down, v_g_final):
    given = dict(locals())
    w = {n: given[n] for n in WEIGHTS}
    m = {n: given["m_" + n] for n in WEIGHTS}
    v = {n: given["v_" + n] for n in WEIGHTS}

    def shard2d(a):
        return a.reshape(a.shape[-2], a.shape[-1])

    chip = 2 * lax.axis_index("x") + lax.axis_index("y")
    where = jnp.stack([chip, lax.axis_index("c")]).astype(jnp.int32)
    xs, target = x[0], loss_target[0]
    small = {n: w[n].reshape(SMALL_SHAPES[n]) for n in SMALL}

    (abr, abi, bbr, bbi), disc_vjp = jax.vjp(_ssm_discretize, small["a_re"], small["a_im"],
                                             small["log_dt"], small["b_re"], small["b_im"])
    tab_f, tab_r = _scan_tables(abr, abi)
    bre = _block_diag_b(bbr)
    bim = _block_diag_b(bbi)
    cre = _block_diag_c(small["c_re"])
    cim = _block_diag_c(small["c_im"])
    tril = jnp.tril(jnp.ones((CHUNK, CHUNK), dtype=bool))
    ws = jnp.where(tril[None], small["w_s"], 0.0)
    ws_st = ws.reshape(SGU_G // 2, 2 * CHUNK, CHUNK).astype(BF16)
    wst_st = ws.transpose(0, 2, 1).reshape(SGU_G // 2, 2 * CHUNK, CHUNK).astype(BF16)
    bmat = jnp.repeat(small["b_s"].T, SGU_D, axis=1)
    g_mix2 = small["g_mix"].reshape(1, D_MODEL)
    g_ffn2 = small["g_ffn"].reshape(1, D_MODEL)
    g_final2 = small["g_final"].reshape(1, D_MODEL)
    g_sgu2 = small["g_sgu"].reshape(1, SGU_W)
    d_skip2 = small["d_skip"].reshape(1, SSM_W)
    b_glu2 = small["b_glu"].reshape(1, SSM_W)
    conv_b2 = small["conv_b"].reshape(1, 2 * D_FF)

    gat = {"w_in": _place_shard(shard2d(w["w_in"]), where, "place_w_in", BF16, PLACE_ROWS["w_in"])}
    gat.update(zip(BIG[1:], _place_shards([shard2d(w[n]) for n in BIG[1:]], where, "place_rest", BF16)))
    gat["conv_w"] = _place_shard(shard2d(w["conv_w"]), where, "place_conv_w", F32, 3)
    all_rows = (0, D_MODEL)
    (gat["w_in"],), = _comm("gather_in", [_job_gather(
        [gat["w_in"]], [(0, all_rows, ICI, (0.0, 0.5)), (0, all_rows, SIBLING, (0.5, 1.0))])])
    mixers = ["w_glu", "w_proj_a", "w_proj_b", "w_out"]
    rows = {n: (0, gat[n].shape[1]) for n in mixers}
    down_a, down_b = (0, D_FF // 8), (D_FF // 8, D_FF // 8)
    up_a, up_b = (0, 3 * D_MODEL // 8), (3 * D_MODEL // 8, 5 * D_MODEL // 8)
    span = (0.0, 1.0)

    names = mixers + ["conv_w", "w_down"]
    (p, h1, bur, bui), (got,) = _fwd_in(
        xs, g_mix2, gat["w_in"], bre, bim,
        [_job_gather([gat[n] for n in names],
                     [(i, rows[n], ICI, span) for i, n in enumerate(mixers)]
                     + [(4, None, ICI, span), (5, down_a, ICI, span)])])
    gat.update(zip(names, got))
    names = mixers + ["w_down", "w_up"]
    (str_, sti), (got,) = _scan_fwd(
        bur, bui, tab_f,
        [_job_gather([gat[n] for n in names],
                     [(i, rows[n], SIBLING, span) for i, n in enumerate(mixers)]
                     + [(4, down_a, SIBLING, span), (4, down_b, ICI, span), (5, up_a, ICI, span)])])
    gat.update(zip(names, got))
    w_glu_f = gat["w_glu"].reshape(SSM_W, SSM_W)
    w_out_f = gat["w_out"].reshape(D_MODEL, D_MODEL)
    conv_w_f = gat["conv_w"].transpose(1, 0, 2).reshape(3, 2 * D_FF)
    (x2, y0, z, mixed, ya, yb), ((gat["w_down"], gat["w_up"]),) = _fwd_mix(
        xs, p, str_, sti, cre, cim, d_skip2, w_glu_f, b_glu2, gat["w_proj_a"], g_sgu2, ws_st, bmat,
        gat["w_proj_b"], w_out_f,
        [_job_gather([gat["w_down"], gat["w_up"]],
                     [(0, down_b, SIBLING, span), (1, up_a, SIBLING, span),
                      (1, up_b, ICI, (0.0, 0.75)), (1, up_b, SIBLING, (0.75, 1.0))])])
    w_down_f = gat["w_down"].reshape(D_FF, D_MODEL)
    up, act, f, h2, dx3, sm_ffn = _fwd_ffn(x2, target, g_ffn2, gat["w_up"], conv_w_f, conv_b2,
                                           w_down_f, g_final2)

    def leg1_done(names, got):
        return _add_sibling([part[n] for n in names], got, where, "add_sibling_" + names[0])

    def leg2_done(names, sums, got):
        return _add_chips(sums, got, where, "add_chips_" + names[0])

    part, red = {}, {}
    part["w_down"] = _dw_rows(f, dx3, "dw_down", D_FF // 2, 2 * TK).reshape(
        N_CHIP, D_FF // N_CHIP, D_MODEL)
    (dx2, dup, sm_conv, sm_gffn), (got,) = _bwd_ffn(
        dx3, up, act, x2, g_ffn2, gat["w_up"], conv_w_f, w_down_f,
        [_job_sibling_halves([part["w_down"]])])
    sum_down = leg1_done(["w_down"], got)
    part["w_up"], (got,) = _dw_shards(h2, dup, "dw_up", 4 * TK, [_job_to_owner(sum_down)])
    red_down = leg2_done(["w_down"], sum_down, got)
    ((dsr, dsi, du_part, drest, mrg, dya, dyb, yap, dz, y1, sgu, dy0, sm_mix, dbm, dws),
     (got, (red["w_down"],))) = _bwd_mix(
        dx2, p, y0, z, mixed, ya, yb, w_out_f, gat["w_proj_a"], gat["w_proj_b"], w_glu_f, cre, cim,
        ws_st, wst_st, d_skip2, g_sgu2,
        [_job_sibling_halves([part["w_up"]]), _job_swap_halves(red_down)])
    sum_up = leg1_done(["w_up"], got)
    (lam_r, lam_i, dar8, dai8), (got,) = _scan_bwd(dsr, dsi, str_, sti, tab_r, [_job_to_owner(sum_up)])
    red_up = leg2_done(["w_up"], sum_up, got)
    mix4 = ["w_out", "w_proj_a", "w_proj_b", "w_glu"]
    part["w_out"] = _dw_cols(mrg, dx2, "dw_out", D_MODEL // 2, False)[0].reshape(
        N_CHIP, D_MODEL // N_CHIP, D_MODEL)
    part["w_proj_a"] = _dw_cols(yap, dya, "dw_proj_a", D_MODEL // N_CHIP, True)[0]
    part["w_proj_b"] = _dw_cols(sgu, dyb, "dw_proj_b", D_MODEL // N_CHIP, True)[0]
    part["w_glu"] = _dw_cols(y1, dz, "dw_glu", SSM_W, False)[0].reshape(
        N_CHIP, SSM_W // N_CHIP, SSM_W)
    (grad_x, dp, sm_gmix), (got, (red["w_up"],)) = _bwd_in(
        lam_r, lam_i, du_part, drest, xs, dx2, g_mix2, gat["w_in"], bre, bim,
        [_job_sibling_halves([part[n] for n in mix4]), _job_swap_halves(red_up)])
    sums_m = leg1_done(mix4, got)
    part["w_in"], (got,) = _dw_cols(h1, dp, "dw_in", IN_COLS // N_CHIP, True, [_job_to_owner(sums_m)])
    red_m = leg2_done(mix4, sums_m, got)
    (dbd_r, dbd_i), (got, done_m) = _dw_pair(
        p, SSM_W, lam_r, lam_i, "db_bar",
        [_job_sibling_halves([part["w_in"]]), _job_swap_halves(red_m)])
    red.update(zip(mix4, done_m))
    sum_in = leg1_done(["w_in"], got)
    (dcd_r, dcd_i), (got,) = _dw_pair(dy0, SSM_W, str_, sti, "dc", [_job_to_owner(sum_in)])
    red_in = leg2_done(["w_in"], sum_in, got)
    (red["w_in"],), = _comm("swap_w_in", [_job_swap_halves(red_in)])

    def pick_c(slabs):
        two = LANE // SSM_P
        return jnp.einsum("jshsp->jshp", slabs.reshape(SSM_G // two, two, SSM_H, two, SSM_P)
                          ).reshape(SSM_G, SSM_H, SSM_P)

    def pick_b(slabs):
        return pick_c(slabs).transpose(0, 2, 1)

    dabr = jnp.sum(dar8, axis=0).reshape(SSM_G, SSM_P)
    dabi = jnp.sum(dai8, axis=0).reshape(SSM_G, SSM_P)
    d_a_re, d_a_im, d_log_dt, d_b_re, d_b_im = disc_vjp((dabr, dabi, pick_b(dbd_r), pick_b(dbd_i)))
    gsmall = {
        "g_mix": sm_gmix[0], "a_re": d_a_re, "a_im": d_a_im, "log_dt": d_log_dt,
        "b_re": d_b_re, "b_im": d_b_im, "c_re": pick_c(dcd_r), "c_im": -pick_c(dcd_i),
        "d_skip": sm_mix[0], "b_glu": sm_mix[1], "g_sgu": sm_mix[2],
        "w_s": jnp.where(tril[None], dws, 0.0),
        "b_s": dbm.reshape(CHUNK, SGU_G, SGU_D).sum(-1).T,
        "g_ffn": sm_gffn[0], "conv_b": sm_conv[3], "g_final": sm_ffn[0],
        "conv_w": sm_conv[0:3], "loss": sm_ffn[1, 0:1],
    }

    total_pack = _small_allreduce(_pack(gsmall))
    total = _unpack(total_pack)
    grads = dict(red)
    cs = 2 * D_FF // N_CHIP
    grads["conv_w"] = lax.dynamic_slice(total["conv_w"], (0, chip * cs), (3, cs))
    delta, new_m, new_v = {}, {}, {}
    for n in BIG + ("conv_w",):
        grads[n], delta[n], new_m[n], new_v[n] = _adamw(
            shard2d(w[n]), grads[n], shard2d(m[n]), shard2d(v[n]), "adamw_" + n, PLACE_ROWS.get(n, 3))
    for n in SMALL:
        grads[n] = total[n].reshape(w[n].shape)
    ud, um, uv = _adamw_many(*[[d[n] for n in SMALL] for d in (w, grads, m, v)], "adamw_small")
    for i, n in enumerate(SMALL):
        delta[n], new_m[n], new_v[n] = ud[i], um[i], uv[i]

    def like(d):
        return [d[n].reshape(w[n].shape) for n in WEIGHTS]

    return (total["loss"].reshape(()), grad_x.reshape(x.shape), *like(grads), *like(delta),
            *like(new_m), *like(new_v))
```

```python
import math

import jax
import jax.numpy as jnp
from jax import lax
from jax.experimental import pallas as pl
from jax.experimental.pallas import tpu as pltpu

F32 = jnp.float32
BF16 = jnp.bfloat16
MESH = pl.DeviceIdType.MESH

D_MODEL = 1024
SSM_W = 512
SSM_G = 32
SSM_H = 16
SSM_P = 64
N_STATE = SSM_G * SSM_P
DIAG_N = 128 * SSM_P // SSM_H
SGU_W = 512
SGU_G = 8
SGU_D = 64
CHUNK = 128
D_FF = 2816
IN_COLS = 3584
EPS = 1e-6
N_CHIP = 4

ADAM_LR = 0.001
ADAM_B1 = 0.9
ADAM_B2 = 0.999
ADAM_EPS = 1e-08
ADAM_WD = 0.01
ADAM_STEP = 10

SUBLANE = 8
LANE = 128
VMEM_LIMIT = 56 * 1024 * 1024
TB = 256
TK = 512
SCAN_LANES = 256
SCAN_UNROLL = 4
HALO = SUBLANE

BIG = ("w_in", "w_up", "w_down", "w_out", "w_proj_a", "w_proj_b", "w_glu")
SMALL = ("g_mix", "a_re", "a_im", "log_dt", "b_re", "b_im", "c_re", "c_im", "d_skip", "b_glu",
         "g_sgu", "w_s", "b_s", "g_ffn", "conv_b", "g_final")
WEIGHTS = ("g_mix", "w_in", "a_re", "a_im", "log_dt", "b_re", "b_im", "c_re", "c_im", "d_skip",
           "w_glu", "b_glu", "w_proj_a", "g_sgu", "w_s", "b_s", "w_proj_b", "w_out", "g_ffn",
           "w_up", "conv_w", "conv_b", "w_down", "g_final")

ANY = pl.BlockSpec(memory_space=pl.ANY)


def _params(n_grid):
    return pltpu.CompilerParams(dimension_semantics=("arbitrary",) * n_grid if n_grid else None,
                                vmem_limit_bytes=VMEM_LIMIT)


def _whole():
    return pl.BlockSpec(memory_space=pltpu.VMEM)


def _rows(tb, ncol):
    return pl.BlockSpec((tb, ncol), lambda i: (i, 0))


def _acc(nrow, ncol):
    return pl.BlockSpec((nrow, ncol), lambda i: (0, 0))


def _dot(a, b):
    return jnp.dot(a.astype(BF16), b.astype(BF16), preferred_element_type=F32)


def _dot_nt(a, b):
    return lax.dot_general(a.astype(BF16), b.astype(BF16), (((1,), (1,)), ((), ())),
                           preferred_element_type=F32)


def _sigmoid(v):
    return 0.5 * jnp.tanh(0.5 * v) + 0.5


_GELU_C = math.sqrt(2.0 / math.pi)


def _gelu(v):
    return 0.5 * v * (1.0 + jnp.tanh(_GELU_C * (v + 0.044715 * v * v * v)))


def _gelu_and_grad(v):
    v2 = v * v
    t = jnp.tanh(_GELU_C * v * (1.0 + 0.044715 * v2))
    half = 0.5 * (1.0 + t)
    return v * half, half + 0.5 * v * (1.0 - t * t) * _GELU_C * (1.0 + 3.0 * 0.044715 * v2)


def _rms_stats(v):
    r = lax.rsqrt(jnp.mean(v * v, axis=-1, keepdims=True) + EPS)
    return r, v * r


def _rms_bwd(dxh, xh, r):
    return r * (dxh - xh * jnp.mean(dxh * xh, axis=-1, keepdims=True))


def _place():
    x, y, c = lax.axis_index("x"), lax.axis_index("y"), lax.axis_index("c")
    chips = [(1 - x, y), (x, 1 - y), (1 - x, 1 - y)]
    return x, y, c, chips


def _chip_index(chip):
    return 2 * chip[0] + chip[1]


def _remote(src, dst, send_sem, recv_sem, device):
    return pltpu.make_async_remote_copy(src_ref=src, dst_ref=dst, send_sem=send_sem,
                                        recv_sem=recv_sem, device_id=device, device_id_type=MESH)


def _half(ref_rows, c):
    hr = ref_rows // 2
    return pl.ds(pl.multiple_of(c * hr, SUBLANE), hr)


class _Job:
    def __init__(self, hooks, n_sem, ins=(), inouts=(), outs=()):
        self.hooks, self.n_sem = list(hooks), n_sem
        self.ins, self.inouts, self.outs = list(ins), list(inouts), list(outs)


def _whole_span(start, finish):
    return [(0.0, "start", start), (1.0, "finish", finish)]


ICI, SIBLING = "ici", "sibling"


def _job_gather(bufs, legs):
    def copies(io, leg, first):
        b, window, kind, _ = legs[leg]
        x, y, c, chips = _place()
        k_me = 2 * x + y
        out = []
        for j, ch in enumerate(chips):
            k = _chip_index(ch)
            if window is None:
                src, land, dev = io[b].at[k_me], io[b].at[k], (*ch, c)
            else:
                r0, rows = window
                mine = pl.ds(pl.multiple_of(r0 + c * (rows // 2), SUBLANE), rows // 2)
                theirs = pl.ds(pl.multiple_of(r0 + (1 - c) * (rows // 2), SUBLANE), rows // 2)
                if kind == ICI:
                    src, land, dev = io[b].at[k_me, mine, :], io[b].at[k, mine, :], (*ch, c)
                else:
                    src, land, dev = io[b].at[k, mine, :], io[b].at[k, theirs, :], (x, y, 1 - c)
            out.append((src, land, first + j, dev))
        return out

    def starter(leg):
        def start(ins, io, outs, ssem, rsem):
            for src, _, i, dev in copies(io, leg, 3 * leg):
                _remote(src, src, ssem(i), rsem(i), dev).start()
        return start

    def finisher(leg):
        def finish(ins, io, outs, ssem, rsem):
            cps = copies(io, leg, 3 * leg)
            for _, land, i, dev in cps:
                _remote(land, land, ssem(i), rsem(i), dev).wait_recv()
            for src, _, i, dev in cps:
                _remote(src, src, ssem(i), rsem(i), dev).wait_send()
        return finish

    hooks = []
    for leg, (_, _, _, (begin, end)) in enumerate(legs):
        hooks += [(begin, "start", starter(leg)), (end, "finish", finisher(leg))]
    return _Job(hooks, 3 * len(legs), inouts=bufs)


def _job_sibling_halves(grads):
    n = len(grads)

    def build(ins, outs, ssem, rsem):
        x, y, c, _ = _place()
        return [_remote(ins[t].at[:, _half(grads[t].shape[1], 1 - c), :], outs[t], ssem(t), rsem(t),
                        (x, y, 1 - c)) for t in range(n)]

    def start(ins, io, outs, ssem, rsem):
        for cp in build(ins, outs, ssem, rsem):
            cp.start()

    def finish(ins, io, outs, ssem, rsem):
        for cp in build(ins, outs, ssem, rsem):
            cp.wait()

    return _Job(_whole_span(start, finish), n, ins=grads,
                outs=[jax.ShapeDtypeStruct((N_CHIP, g.shape[1] // 2, g.shape[2]), F32) for g in grads])


def _job_to_owner(sums):
    n = len(sums)

    def build(ins, outs, ssem, rsem):
        x, y, c, chips = _place()
        return [_remote(ins[t].at[_chip_index(ch)], outs[t].at[j], ssem(3 * t + j), rsem(3 * t + j),
                        (*ch, c)) for t in range(n) for j, ch in enumerate(chips)]

    def start(ins, io, outs, ssem, rsem):
        for cp in build(ins, outs, ssem, rsem):
            cp.start()

    def finish(ins, io, outs, ssem, rsem):
        for cp in build(ins, outs, ssem, rsem):
            cp.wait()

    return _Job(_whole_span(start, finish), 3 * n, ins=sums,
                outs=[jax.ShapeDtypeStruct((3,) + s.shape[1:], s.dtype) for s in sums])


def _job_swap_halves(bufs):
    n = len(bufs)

    def start(ins, io, outs, ssem, rsem):
        x, y, c, _ = _place()
        for t in range(n):
            mine = io[t].at[_half(bufs[t].shape[0], c), :]
            _remote(mine, mine, ssem(t), rsem(t), (x, y, 1 - c)).start()

    def finish(ins, io, outs, ssem, rsem):
        x, y, c, _ = _place()
        for t in range(n):
            theirs = io[t].at[_half(bufs[t].shape[0], 1 - c), :]
            _remote(theirs, theirs, ssem(t), rsem(t), (x, y, 1 - c)).wait_recv()
        for t in range(n):
            mine = io[t].at[_half(bufs[t].shape[0], c), :]
            _remote(mine, mine, ssem(t), rsem(t), (x, y, 1 - c)).wait_send()

    return _Job(_whole_span(start, finish), n, inouts=bufs)


def _call(body, name, grid, in_specs, out_specs, out_shape, args, jobs=(), scratch=()):
    n_in, n_out, n_scr = len(args), len(out_shape), len(scratch)
    job_in = [a for jb in jobs for a in jb.ins + jb.inouts]
    job_out = [s for jb in jobs
               for s in [jax.ShapeDtypeStruct(a.shape, a.dtype) for a in jb.inouts] + jb.outs]
    aliases, pos_in, pos_out = {}, n_in, n_out
    for jb in jobs:
        pos_in += len(jb.ins)
        for _ in jb.inouts:
            aliases[pos_in] = pos_out
            pos_in += 1
            pos_out += 1
        pos_out += len(jb.outs)
    n_sem = sum(jb.n_sem for jb in jobs)

    def wrapped(*refs):
        c_in = refs[:n_in]
        j_in = refs[n_in:n_in + len(job_in)]
        c_out = refs[n_in + len(job_in):n_in + len(job_in) + n_out]
        j_out = refs[n_in + len(job_in) + n_out:n_in + len(job_in) + n_out + len(job_out)]
        rest = refs[n_in + len(job_in) + n_out + len(job_out):]
        c_scr = rest[:n_scr]
        views, pi, po, ps = [], 0, 0, 0
        for jb in jobs:
            ins = j_in[pi:pi + len(jb.ins)]
            pi += len(jb.ins) + len(jb.inouts)
            io = j_out[po:po + len(jb.inouts)]
            new = j_out[po + len(jb.inouts):po + len(jb.inouts) + len(jb.outs)]
            po += len(jb.inouts) + len(jb.outs)
            send = (lambda i, o=ps: rest[n_scr].at[o + i])
            recv = (lambda i, o=ps: rest[n_scr + 1].at[o + i])
            ps += jb.n_sem
            views.append((ins, io, new, send, recv))

        def run(frac):
            for kind in ("finish", "start"):
                for jb, vw in zip(jobs, views):
                    for at, what, fn in jb.hooks:
                        if at == frac and what == kind:
                            fn(*vw)

        fracs = sorted({at for jb in jobs for at, _, _ in jb.hooks})
        if not grid:
            for frac in fracs:
                run(frac)
            return
        if jobs:
            assert len(grid) == 1 or set(fracs) <= {0.0, 1.0}
            first = pl.program_id(0) == 0
            last = pl.program_id(0) == grid[0] - 1
            for d in range(1, len(grid)):
                first = jnp.logical_and(first, pl.program_id(d) == 0)
                last = jnp.logical_and(last, pl.program_id(d) == grid[d] - 1)
            for frac in fracs:
                if frac < 1.0:
                    at_step = first if frac == 0.0 else pl.program_id(0) == int(frac * grid[0])
                    pl.when(at_step)(lambda frac=frac: run(frac))
        body(*c_in, *c_out, *c_scr)
        if jobs and 1.0 in fracs:
            pl.when(last)(lambda: run(1.0))

    sems = [pltpu.SemaphoreType.DMA((n_sem,)), pltpu.SemaphoreType.DMA((n_sem,))] if jobs else []
    kwargs = dict(grid=grid) if grid else {}
    res = pl.pallas_call(
        wrapped, name=name, in_specs=list(in_specs) + [ANY] * len(job_in),
        out_specs=list(out_specs) + [ANY] * len(job_out),
        out_shape=list(out_shape) + job_out, scratch_shapes=list(scratch) + sems,
        input_output_aliases=aliases, compiler_params=_params(len(grid)), **kwargs,
    )(*args, *job_in)
    outs, pos, per_job = list(res[:n_out]), n_out, []
    for jb in jobs:
        k = len(jb.inouts) + len(jb.outs)
        per_job.append(list(res[pos:pos + k]))
        pos += k
    return outs, per_job


def _comm(name, jobs):
    return _call(None, name, (), [], [], [], [], jobs)[1]


def _fwd_in(x, g_mix, w_in, bre, bim, jobs=()):
    t_len = x.shape[0]
    cs = IN_COLS // N_CHIP

    def body(x_ref, g_ref, w_ref, bre_ref, bim_ref, p_ref, h_ref, bur_ref, bui_ref):
        xv = x_ref[...]
        r, xh = _rms_stats(xv)
        h = (xh * g_ref[...]).astype(BF16)
        h_ref[...] = h
        for k in range(N_CHIP):
            p_ref[:, k * cs:(k + 1) * cs] = jnp.dot(h, w_ref[k],
                                                    preferred_element_type=F32).astype(BF16)
        u = p_ref[:, 0:SSM_W]
        for i in range(SSM_W // LANE):
            rows, cols = slice(i * LANE, (i + 1) * LANE), slice(i * DIAG_N, (i + 1) * DIAG_N)
            bur_ref[:, cols] = jnp.dot(u[:, rows], bre_ref[rows, cols],
                                       preferred_element_type=F32).astype(BF16)
            bui_ref[:, cols] = jnp.dot(u[:, rows], bim_ref[rows, cols],
                                       preferred_element_type=F32).astype(BF16)

    return _call(
        body, "fwd_in", (t_len // TB,),
        [_rows(TB, D_MODEL), _whole(), _whole(), _whole(), _whole()],
        [_rows(TB, IN_COLS), _rows(TB, D_MODEL), _rows(TB, N_STATE), _rows(TB, N_STATE)],
        [jax.ShapeDtypeStruct((t_len, IN_COLS), BF16), jax.ShapeDtypeStruct((t_len, D_MODEL), BF16),
         jax.ShapeDtypeStruct((t_len, N_STATE), BF16), jax.ShapeDtypeStruct((t_len, N_STATE), BF16)],
        [x, g_mix, w_in, bre, bim], jobs)


def _scan_local(xr, xi, tab, shifts):
    for q, s in enumerate(shifts):
        ar, ai = tab[2 * q], tab[2 * q + 1]
        rr = pltpu.roll(xr, s, 0)
        ri = pltpu.roll(xi, s, 0)
        xr, xi = xr + ar * rr - ai * ri, xi + ar * ri + ai * rr
    return xr, xi


def _scan_carry(xr, xi, tab, cr, ci):
    pr, pi = tab[6], tab[7]
    return xr + pr * cr - pi * ci, xi + pr * ci + pi * cr


BF16_TILE = 2 * SUBLANE


def _load_blocks(r_ref, i_ref, base):
    out = []
    for q in range(SCAN_UNROLL // 2):
        rows = pl.ds(pl.multiple_of(base + q * BF16_TILE, BF16_TILE), BF16_TILE)
        vr, vi = r_ref[rows, :].astype(F32), i_ref[rows, :].astype(F32)
        out += [(vr[:SUBLANE], vi[:SUBLANE]), (vr[SUBLANE:], vi[SUBLANE:])]
    return out


def _store_blocks(r_ref, i_ref, base, blocks):
    for q in range(SCAN_UNROLL // 2):
        rows = pl.ds(pl.multiple_of(base + q * BF16_TILE, BF16_TILE), BF16_TILE)
        r_ref[rows, :] = jnp.concatenate([blocks[2 * q][0], blocks[2 * q + 1][0]], 0).astype(r_ref.dtype)
        i_ref[rows, :] = jnp.concatenate([blocks[2 * q][1], blocks[2 * q + 1][1]], 0).astype(i_ref.dtype)


def _scan_fwd(bur, bui, tab, jobs=()):
    t_len = bur.shape[0]
    nblk = t_len // SUBLANE
    lb = SCAN_LANES

    def body(br_ref, bi_ref, tab_ref, sr_ref, si_ref):
        tab_v = [tab_ref[q] for q in range(8)]

        def step(k, carry):
            cr, ci = carry
            base = pl.multiple_of(k * SCAN_UNROLL * SUBLANE, SCAN_UNROLL * SUBLANE)
            local = [_scan_local(xr, xi, tab_v, (1, 2, 4))
                     for xr, xi in _load_blocks(br_ref, bi_ref, base)]
            done = []
            for xr, xi in local:
                xr, xi = _scan_carry(xr, xi, tab_v, cr, ci)
                done.append((xr, xi))
                cr, ci = xr[SUBLANE - 1:SUBLANE, :], xi[SUBLANE - 1:SUBLANE, :]
            _store_blocks(sr_ref, si_ref, base, done)
            return cr, ci

        zero = jnp.zeros((1, lb), F32)
        lax.fori_loop(0, nblk // SCAN_UNROLL, step, (zero, zero))

    col = pl.BlockSpec((t_len, lb), lambda j: (0, j))
    return _call(
        body, "scan_fwd", (N_STATE // lb,),
        [col, col, pl.BlockSpec((8, SUBLANE, lb), lambda j: (0, 0, j))], [col, col],
        [jax.ShapeDtypeStruct((t_len, N_STATE), BF16)] * 2, [bur, bui, tab], jobs)


def _sgu_mix(v, ws_ref, lane_lo):
    rows = []
    for c0 in range(0, v.shape[0], CHUNK):
        slabs = []
        for j in range(SGU_W // LANE):
            prod = jnp.dot(ws_ref[j], v[c0:c0 + CHUNK, j * LANE:(j + 1) * LANE].astype(BF16),
                           preferred_element_type=F32)
            slabs.append(jnp.where(lane_lo, prod[:CHUNK], prod[CHUNK:]))
        rows.append(jnp.concatenate(slabs, axis=1))
    return jnp.concatenate(rows, axis=0) if len(rows) > 1 else rows[0]


def _fwd_mix(x, p, str_, sti, cre, cim, d_skip, w_glu, b_glu, w_pa, g_sgu, ws_st, bmat, w_pb, w_out,
             jobs=()):
    t_len = x.shape[0]

    def body(x_ref, p_ref, sr_ref, si_ref, cre_ref, cim_ref, dsk_ref, wg_ref, bg_ref, wpa_ref,
             gs_ref, ws_ref, bm_ref, wpb_ref, wo_ref,
             x2_ref, y0_ref, z_ref, mx_ref, ya_ref, yb_ref):
        u = p_ref[:, 0:SSM_W].astype(F32)
        y0 = jnp.concatenate(
            [_dot(sr_ref[:, i * DIAG_N:(i + 1) * DIAG_N],
                  cre_ref[i * DIAG_N:(i + 1) * DIAG_N, i * LANE:(i + 1) * LANE])
             - _dot(si_ref[:, i * DIAG_N:(i + 1) * DIAG_N],
                    cim_ref[i * DIAG_N:(i + 1) * DIAG_N, i * LANE:(i + 1) * LANE])
             for i in range(SSM_W // LANE)], axis=1) + dsk_ref[...] * u
        y0_ref[...] = y0.astype(BF16)
        y1 = _gelu(y0)
        z = _dot(y1, wg_ref[...]) + bg_ref[...]
        z_ref[...] = z.astype(BF16)
        ya_pre = (y1 * _sigmoid(z)).astype(BF16)
        ya = jnp.concatenate([jnp.dot(ya_pre, wpa_ref[k], preferred_element_type=F32)
                              for k in range(N_CHIP)], axis=1)
        ya_ref[...] = ya.astype(BF16)

        uvg = _gelu(p_ref[:, SSM_W:SSM_W + 2 * SGU_W].astype(F32))
        u2 = uvg[:, :SGU_W]
        _, vh = _rms_stats(uvg[:, SGU_W:])
        v3 = vh * gs_ref[...]
        lane_lo = lax.broadcasted_iota(jnp.int32, (CHUNK, LANE), 1) < SGU_D
        bias = jnp.concatenate([bm_ref[...]] * (TB // CHUNK), axis=0)
        mixed = _sgu_mix(v3, ws_ref, lane_lo) + bias
        mx_ref[...] = mixed.astype(BF16)
        sgu = (u2 * mixed).astype(BF16)
        yb = jnp.concatenate([jnp.dot(sgu, wpb_ref[k], preferred_element_type=F32)
                              for k in range(N_CHIP)], axis=1)
        yb_ref[...] = yb.astype(BF16)

        lg0 = SSM_W + 2 * SGU_W
        ga = _sigmoid(p_ref[:, lg0:lg0 + D_MODEL].astype(F32))
        gb = _sigmoid(p_ref[:, lg0 + D_MODEL:lg0 + 2 * D_MODEL].astype(F32))
        mrg = ga * ya + gb * yb
        x2_ref[...] = x_ref[...] + _dot(mrg, wo_ref[...])

    return _call(
        body, "fwd_mix", (t_len // TB,),
        [_rows(TB, D_MODEL), _rows(TB, IN_COLS), _rows(TB, N_STATE), _rows(TB, N_STATE)]
        + [_whole()] * 11,
        [_rows(TB, D_MODEL), _rows(TB, SSM_W), _rows(TB, SSM_W), _rows(TB, SGU_W),
         _rows(TB, D_MODEL), _rows(TB, D_MODEL)],
        [jax.ShapeDtypeStruct((t_len, D_MODEL), F32), jax.ShapeDtypeStruct((t_len, SSM_W), BF16),
         jax.ShapeDtypeStruct((t_len, SSM_W), BF16), jax.ShapeDtypeStruct((t_len, SGU_W), BF16),
         jax.ShapeDtypeStruct((t_len, D_MODEL), BF16), jax.ShapeDtypeStruct((t_len, D_MODEL), BF16)],
        [x, p, str_, sti, cre, cim, d_skip, w_glu, b_glu, w_pa, g_sgu, ws_st, bmat, w_pb, w_out], jobs)


def _conv_taps(v, cw_ref, c0, width):
    w0 = cw_ref[0:1, c0:c0 + width]
    w1 = cw_ref[1:2, c0:c0 + width]
    w2 = cw_ref[2:3, c0:c0 + width]
    return w0 * pltpu.roll(v, 2, 0) + w1 * pltpu.roll(v, 1, 0) + w2 * v


def _fwd_ffn(x2, target, g_ffn, w_up, conv_w, conv_b, w_down, g_final):
    t_len = x2.shape[0]
    half = D_FF // 2
    blocks_per_halo = TB // HALO

    def body(x2_ref, xp_ref, tg_ref, gf_ref, wu_ref, cw_ref, cb_ref, wd_ref, gl_ref,
             up_ref, act_ref, f_ref, h2_ref, dx3_ref, sm_ref):
        i = pl.program_id(0)
        xe = jnp.concatenate([xp_ref[...] * jnp.where(i == 0, 0.0, 1.0), x2_ref[...]], axis=0)
        _, xh = _rms_stats(xe)
        h2 = (xh * gf_ref[...]).astype(BF16)
        h2_ref[...] = h2[HALO:]
        acc = jnp.zeros((TB, D_MODEL), F32)
        for hc in range(2):
            ca = hc * half
            cb = D_FF + hc * half
            ua = jnp.dot(h2, wu_ref[hc], preferred_element_type=F32)
            ub = jnp.dot(h2, wu_ref[2 + hc], preferred_element_type=F32)
            up_ref[:, ca:ca + half] = ua[HALO:].astype(BF16)
            up_ref[:, cb:cb + half] = ub[HALO:].astype(BF16)
            ac = _conv_taps(ua, cw_ref, ca, half)[HALO:] + cb_ref[:, ca:ca + half]
            bc = _conv_taps(ub, cw_ref, cb, half)[HALO:] + cb_ref[:, cb:cb + half]
            act_ref[:, ca:ca + half] = ac.astype(BF16)
            act_ref[:, cb:cb + half] = bc.astype(BF16)
            f = (ac * _sigmoid(ac) * bc).astype(BF16)
            f_ref[:, ca:ca + half] = f
            acc = acc + jnp.dot(f, wd_ref[ca:ca + half, :], preferred_element_type=F32)
        x3 = x2_ref[...] + acc
        r3, xh3 = _rms_stats(x3)
        err = xh3 * gl_ref[...] - tg_ref[...]
        dout = err * (1.0 / D_MODEL)
        dx3_ref[...] = _rms_bwd(dout * gl_ref[...], xh3, r3)
        dgl = jnp.sum(dout * xh3, axis=0, keepdims=True)
        loss = 0.5 * jnp.sum(jnp.mean(err * err, axis=-1, keepdims=True), axis=0, keepdims=True)
        upd = jnp.concatenate([dgl, jnp.broadcast_to(loss, (1, D_MODEL)),
                               jnp.zeros((SUBLANE - 2, D_MODEL), F32)], axis=0)

        @pl.when(i == 0)
        def _():
            sm_ref[...] = upd

        @pl.when(i > 0)
        def _():
            sm_ref[...] += upd

    prev = pl.BlockSpec((HALO, D_MODEL), lambda i: (jnp.maximum(i * blocks_per_halo - 1, 0), 0))
    return _call(
        body, "fwd_ffn", (t_len // TB,),
        [_rows(TB, D_MODEL), prev, _rows(TB, D_MODEL)] + [_whole()] * 6,
        [_rows(TB, 2 * D_FF), _rows(TB, 2 * D_FF), _rows(TB, D_FF), _rows(TB, D_MODEL),
         _rows(TB, D_MODEL), _acc(SUBLANE, D_MODEL)],
        [jax.ShapeDtypeStruct((t_len, 2 * D_FF), BF16), jax.ShapeDtypeStruct((t_len, 2 * D_FF), BF16),
         jax.ShapeDtypeStruct((t_len, D_FF), BF16), jax.ShapeDtypeStruct((t_len, D_MODEL), BF16),
         jax.ShapeDtypeStruct((t_len, D_MODEL), F32), jax.ShapeDtypeStruct((SUBLANE, D_MODEL), F32)],
        [x2, x2, target, g_ffn, w_up, conv_w, conv_b, w_down, g_final])[0]


def _bwd_ffn(dx3, up, act, x2, g_ffn, w_up, conv_w, w_down, jobs=()):
    t_len = x2.shape[0]
    half = D_FF // 2
    nblk = t_len // TB
    halo_b = 2 * HALO
    n_e = TB + HALO

    def body(dx_ref, dxn_ref, up_ref, act_ref, actn_ref, x2_ref, gf_ref, wu_ref, cw_ref,
             wd_ref, dx2_ref, dup_ref, smw_ref, smg_ref):
        i = pl.program_id(0)
        keep_last = jnp.where(i == nblk - 1, 0.0, 1.0)
        dxe = jnp.concatenate([dx_ref[...], dxn_ref[...] * keep_last], axis=0).astype(BF16)
        dh2 = jnp.zeros((TB, D_MODEL), F32)
        zpad = jnp.zeros((1, half), F32)
        for hc in range(2):
            ca = hc * half
            cb = D_FF + hc * half
            ac = jnp.concatenate([act_ref[:, ca:ca + half].astype(F32),
                                  actn_ref[:, ca:ca + half].astype(F32)[:HALO]], axis=0)
            bc = jnp.concatenate([act_ref[:, cb:cb + half].astype(F32),
                                  actn_ref[:, cb:cb + half].astype(F32)[:HALO]], axis=0)
            wa = [cw_ref[k:k + 1, ca:ca + half] for k in range(3)]
            wb = [cw_ref[k:k + 1, cb:cb + half] for k in range(3)]
            df = lax.dot_general(dxe, wd_ref[ca:ca + half, :], (((1,), (1,)), ((), ())),
                                 preferred_element_type=F32)
            sg = _sigmoid(ac)
            da = df * bc * sg * (1.0 + ac * (1.0 - sg))
            db = df * ac * sg
            da1, da2 = pltpu.roll(da, n_e - 1, 0), pltpu.roll(da, n_e - 2, 0)
            db1, db2 = pltpu.roll(db, n_e - 1, 0), pltpu.roll(db, n_e - 2, 0)
            dua = (wa[2] * da + wa[1] * da1 + wa[0] * da2)[:TB]
            dub = (wb[2] * db + wb[1] * db1 + wb[0] * db2)[:TB]
            dup_ref[:, ca:ca + half] = dua.astype(BF16)
            dup_ref[:, cb:cb + half] = dub.astype(BF16)
            dh2 = dh2 + _dot_nt(dua, wu_ref[hc]) + _dot_nt(dub, wu_ref[2 + hc])
            rows = []
            for u_, d0, d1, d2 in ((up_ref[:, ca:ca + half].astype(F32), da, da1, da2),
                                   (up_ref[:, cb:cb + half].astype(F32), db, db1, db2)):
                rows.append([jnp.sum(u_ * d2[:TB], axis=0, keepdims=True),
                             jnp.sum(u_ * d1[:TB], axis=0, keepdims=True),
                             jnp.sum(u_ * d0[:TB], axis=0, keepdims=True),
                             jnp.sum(d0[:TB], axis=0, keepdims=True)])
            for c0, rws in ((ca, rows[0]), (cb, rows[1])):
                upd = jnp.concatenate(rws + [zpad] * (SUBLANE - 4), axis=0)

                @pl.when(i == 0)
                def _(upd=upd, c0=c0):
                    smw_ref[:, c0:c0 + half] = upd

                @pl.when(i > 0)
                def _(upd=upd, c0=c0):
                    smw_ref[:, c0:c0 + half] += upd

        r2, xh2 = _rms_stats(x2_ref[...])
        dx2_ref[...] = dx_ref[...] + _rms_bwd(dh2 * gf_ref[...], xh2, r2)
        updg = jnp.concatenate([jnp.sum(dh2 * xh2, axis=0, keepdims=True),
                                jnp.zeros((SUBLANE - 1, D_MODEL), F32)], axis=0)

        @pl.when(i == 0)
        def _():
            smg_ref[...] = updg

        @pl.when(i > 0)
        def _():
            smg_ref[...] += updg

    nxt_d = pl.BlockSpec((HALO, D_MODEL),
                         lambda i: (jnp.minimum((i + 1) * (TB // HALO), t_len // HALO - 1), 0))
    nxt_a = pl.BlockSpec((halo_b, 2 * D_FF),
                         lambda i: (jnp.minimum((i + 1) * (TB // halo_b), t_len // halo_b - 1), 0))
    return _call(
        body, "bwd_ffn", (nblk,),
        [_rows(TB, D_MODEL), nxt_d, _rows(TB, 2 * D_FF), _rows(TB, 2 * D_FF), nxt_a,
         _rows(TB, D_MODEL)] + [_whole()] * 4,
        [_rows(TB, D_MODEL), _rows(TB, 2 * D_FF), _acc(SUBLANE, 2 * D_FF), _acc(SUBLANE, D_MODEL)],
        [jax.ShapeDtypeStruct((t_len, D_MODEL), F32), jax.ShapeDtypeStruct((t_len, 2 * D_FF), BF16),
         jax.ShapeDtypeStruct((SUBLANE, 2 * D_FF), F32), jax.ShapeDtypeStruct((SUBLANE, D_MODEL), F32)],
        [dx3, dx3, up, act, act, x2, g_ffn, w_up, conv_w, w_down], jobs)


def _bwd_mix(dx2, p, y0, z, mixed, ya, yb, w_out, w_pa, w_pb, w_glu, cre, cim, ws_st, wst_st,
             d_skip, g_sgu, jobs=()):
    t_len = dx2.shape[0]
    pc = D_MODEL // N_CHIP
    n_slab = SGU_W // LANE

    def body(dx_ref, p_ref, y0_ref, z_ref, mx_ref, ya_ref, yb_ref, wo_ref, wpa_ref, wpb_ref,
             wg_ref, cre_ref, cim_ref, ws_ref, wst_ref, dsk_ref, gs_ref,
             dsr_ref, dsi_ref, du_ref, drest_ref, mrg_ref, dya_ref, dyb_ref, yap_ref, dz_ref,
             y1_ref, sgu_ref, dy0_ref, sm_ref, dbm_ref, dws_ref):
        i = pl.program_id(0)
        first = i == 0
        lg0 = SSM_W + 2 * SGU_W
        dmrg = _dot_nt(dx_ref[...], wo_ref[...])
        ga = _sigmoid(p_ref[:, lg0:lg0 + D_MODEL].astype(F32))
        gb = _sigmoid(p_ref[:, lg0 + D_MODEL:lg0 + 2 * D_MODEL].astype(F32))
        yav = ya_ref[...].astype(F32)
        ybv = yb_ref[...].astype(F32)
        mrg_ref[...] = (ga * yav + gb * ybv).astype(BF16)
        drest_ref[:, 2 * SGU_W:2 * SGU_W + D_MODEL] = (dmrg * yav * ga * (1.0 - ga)).astype(BF16)
        drest_ref[:, 2 * SGU_W + D_MODEL:] = (dmrg * ybv * gb * (1.0 - gb)).astype(BF16)
        dya = (dmrg * ga).astype(BF16)
        dyb = (dmrg * gb).astype(BF16)
        dya_ref[...] = dya
        dyb_ref[...] = dyb

        y0v = y0_ref[...].astype(F32)
        y1, y1_grad = _gelu_and_grad(y0v)
        sz = _sigmoid(z_ref[...].astype(F32))
        y1_ref[...] = y1.astype(BF16)
        yap_ref[...] = (y1 * sz).astype(BF16)
        dyap = jnp.zeros((TB, SSM_W), F32)
        for k in range(N_CHIP):
            dyap = dyap + _dot_nt(dya[:, k * pc:(k + 1) * pc], wpa_ref[k])
        dz = dyap * y1 * sz * (1.0 - sz)
        dz_ref[...] = dz.astype(BF16)
        dy0 = (dyap * sz + _dot_nt(dz, wg_ref[...])) * y1_grad
        dy0_ref[...] = dy0.astype(BF16)
        u = p_ref[:, 0:SSM_W].astype(F32)
        du_ref[...] = dy0 * dsk_ref[...]
        for i in range(SSM_W // LANE):
            rows, cols = slice(i * DIAG_N, (i + 1) * DIAG_N), slice(i * LANE, (i + 1) * LANE)
            dsr_ref[:, rows] = _dot_nt(dy0[:, cols], cre_ref[rows, cols]).astype(BF16)
            dsi_ref[:, rows] = (-_dot_nt(dy0[:, cols], cim_ref[rows, cols])).astype(BF16)

        uv = p_ref[:, SSM_W:lg0].astype(F32)
        uvg, gg = _gelu_and_grad(uv)
        u2 = uvg[:, :SGU_W]
        rv, vh = _rms_stats(uvg[:, SGU_W:])
        v3 = vh * gs_ref[...]
        mixed = mx_ref[...].astype(F32)
        dsgu = jnp.zeros((TB, SGU_W), F32)
        for k in range(N_CHIP):
            dsgu = dsgu + _dot_nt(dyb[:, k * pc:(k + 1) * pc], wpb_ref[k])
        sgu_ref[...] = (u2 * mixed).astype(BF16)
        du2 = dsgu * mixed
        dmix = dsgu * u2
        lane_lo = lax.broadcasted_iota(jnp.int32, (CHUNK, LANE), 1) < SGU_D
        dv3 = _sgu_mix(dmix, wst_ref, lane_lo)
        dbm = jnp.zeros((CHUNK, SGU_W), F32)
        for c0 in range(0, TB, CHUNK):
            dbm = dbm + dmix[c0:c0 + CHUNK]
        for j in range(n_slab):
            lo = jnp.zeros((CHUNK, CHUNK), F32)
            hi = jnp.zeros((CHUNK, CHUNK), F32)
            for c0 in range(0, TB, CHUNK):
                dsl = dmix[c0:c0 + CHUNK, j * LANE:(j + 1) * LANE]
                vsl = v3[c0:c0 + CHUNK, j * LANE:(j + 1) * LANE]
                lo = lo + _dot_nt(jnp.where(lane_lo, dsl, 0.0), vsl)
                hi = hi + _dot_nt(jnp.where(lane_lo, 0.0, dsl), vsl)

            @pl.when(first)
            def _(lo=lo, hi=hi, j=j):
                dws_ref[2 * j] = lo
                dws_ref[2 * j + 1] = hi

            @pl.when(jnp.logical_not(first))
            def _(lo=lo, hi=hi, j=j):
                dws_ref[2 * j] += lo
                dws_ref[2 * j + 1] += hi

        dv2 = _rms_bwd(dv3 * gs_ref[...], vh, rv)
        drest_ref[:, 0:SGU_W] = (du2 * gg[:, :SGU_W]).astype(BF16)
        drest_ref[:, SGU_W:2 * SGU_W] = (dv2 * gg[:, SGU_W:]).astype(BF16)

        upd = jnp.concatenate([jnp.sum(dy0 * u, axis=0, keepdims=True),
                               jnp.sum(dz, axis=0, keepdims=True),
                               jnp.sum(dv3 * vh, axis=0, keepdims=True),
                               jnp.zeros((SUBLANE - 3, SSM_W), F32)], axis=0)

        @pl.when(first)
        def _():
            sm_ref[...] = upd
            dbm_ref[...] = dbm

        @pl.when(jnp.logical_not(first))
        def _():
            sm_ref[...] += upd
            dbm_ref[...] += dbm

    rest = 2 * SGU_W + 2 * D_MODEL
    bf_d, bf_s = jax.ShapeDtypeStruct((t_len, D_MODEL), BF16), jax.ShapeDtypeStruct((t_len, SSM_W), BF16)
    return _call(
        body, "bwd_mix", (t_len // TB,),
        [_rows(TB, D_MODEL), _rows(TB, IN_COLS), _rows(TB, SSM_W), _rows(TB, SSM_W),
         _rows(TB, SGU_W), _rows(TB, D_MODEL), _rows(TB, D_MODEL)] + [_whole()] * 10,
        [_rows(TB, N_STATE), _rows(TB, N_STATE), _rows(TB, SSM_W), _rows(TB, rest),
         _rows(TB, D_MODEL), _rows(TB, D_MODEL), _rows(TB, D_MODEL), _rows(TB, SSM_W),
         _rows(TB, SSM_W), _rows(TB, SSM_W), _rows(TB, SGU_W), _rows(TB, SSM_W),
         _acc(SUBLANE, SSM_W), _acc(CHUNK, SGU_W),
         pl.BlockSpec((SGU_G, CHUNK, CHUNK), lambda i: (0, 0, 0))],
        [jax.ShapeDtypeStruct((t_len, N_STATE), BF16), jax.ShapeDtypeStruct((t_len, N_STATE), BF16),
         jax.ShapeDtypeStruct((t_len, SSM_W), F32), jax.ShapeDtypeStruct((t_len, rest), BF16),
         bf_d, bf_d, bf_d, bf_s, bf_s, bf_s, bf_s, bf_s,
         jax.ShapeDtypeStruct((SUBLANE, SSM_W), F32), jax.ShapeDtypeStruct((CHUNK, SGU_W), F32),
         jax.ShapeDtypeStruct((SGU_G, CHUNK, CHUNK), F32)],
        [dx2, p, y0, z, mixed, ya, yb, w_out, w_pa, w_pb, w_glu, cre, cim, ws_st, wst_st, d_skip,
         g_sgu], jobs)


def _scan_bwd(dsr, dsi, str_, sti, tab_rev, jobs=()):
    t_len = dsr.shape[0]
    nblk = t_len // SUBLANE
    lb = SCAN_LANES

    def body(dr_ref, di_ref, sr_ref, si_ref, tab_ref, lr_ref, li_ref, dar_ref, dai_ref):
        tab_v = [tab_ref[q] for q in range(8)]
        row0 = lax.broadcasted_iota(jnp.int32, (SUBLANE, lb), 0) == 0
        tile = BF16_TILE

        def step(k, carry):
            cr, ci, acr, aci = carry
            base = pl.multiple_of((nblk - (k + 1) * SCAN_UNROLL) * SUBLANE, SCAN_UNROLL * SUBLANE)
            state = _load_blocks(sr_ref, si_ref, base)
            before = pl.ds(pl.multiple_of(jnp.maximum(base - tile, 0), tile), tile)
            has_before = jnp.where(base > 0, 1.0, 0.0)
            prev = (sr_ref[before, :].astype(F32)[tile - 1:tile] * has_before,
                    si_ref[before, :].astype(F32)[tile - 1:tile] * has_before)
            local = [_scan_local(xr, xi, tab_v, (7, 6, 4))
                     for xr, xi in _load_blocks(dr_ref, di_ref, base)]
            lam = [None] * SCAN_UNROLL
            for b in reversed(range(SCAN_UNROLL)):
                xr, xi = _scan_carry(*local[b], tab_v, cr, ci)
                lam[b] = (xr, xi)
                cr, ci = xr[0:1, :], xi[0:1, :]
                pr, pi = prev if b == 0 else (state[b - 1][0][SUBLANE - 1:], state[b - 1][1][SUBLANE - 1:])
                s_r = jnp.where(row0, pr, pltpu.roll(state[b][0], 1, 0))
                s_i = jnp.where(row0, pi, pltpu.roll(state[b][1], 1, 0))
                acr = acr + xr * s_r + xi * s_i
                aci = aci + xi * s_r - xr * s_i
            _store_blocks(lr_ref, li_ref, base, lam)
            return cr, ci, acr, aci

        zero = jnp.zeros((1, lb), F32)
        zacc = jnp.zeros((SUBLANE, lb), F32)
        _, _, acr, aci = lax.fori_loop(0, nblk // SCAN_UNROLL, step, (zero, zero, zacc, zacc))
        dar_ref[...] = acr
        dai_ref[...] = aci

    col = pl.BlockSpec((t_len, lb), lambda j: (0, j))
    small = pl.BlockSpec((SUBLANE, lb), lambda j: (0, j))
    return _call(
        body, "scan_bwd", (N_STATE // lb,),
        [col, col, col, col, pl.BlockSpec((8, SUBLANE, lb), lambda j: (0, 0, j))],
        [col, col, small, small],
        [jax.ShapeDtypeStruct((t_len, N_STATE), BF16)] * 2
        + [jax.ShapeDtypeStruct((SUBLANE, N_STATE), F32)] * 2,
        [dsr, dsi, str_, sti, tab_rev], jobs)


def _bwd_in(lam_r, lam_i, du_part, drest, x, dx2, g_mix, w_in, bre, bim, jobs=()):
    t_len = x.shape[0]
    cs = IN_COLS // N_CHIP

    def body(lr_ref, li_ref, du_ref, dr_ref, x_ref, dx2_ref, g_ref, w_ref, bre_ref, bim_ref,
             gx_ref, dp_ref, sm_ref):
        i = pl.program_id(0)
        du = du_ref[...] + jnp.concatenate(
            [_dot_nt(lr_ref[:, i * DIAG_N:(i + 1) * DIAG_N],
                     bre_ref[i * LANE:(i + 1) * LANE, i * DIAG_N:(i + 1) * DIAG_N])
             + _dot_nt(li_ref[:, i * DIAG_N:(i + 1) * DIAG_N],
                       bim_ref[i * LANE:(i + 1) * LANE, i * DIAG_N:(i + 1) * DIAG_N])
             for i in range(SSM_W // LANE)], axis=1)
        dp_ref[:, 0:SSM_W] = du.astype(BF16)
        dp_ref[:, SSM_W:] = dr_ref[...]
        dh = jnp.zeros((TB, D_MODEL), F32)
        for k in range(N_CHIP):
            dh = dh + _dot_nt(dp_ref[:, k * cs:(k + 1) * cs], w_ref[k])
        r, xh = _rms_stats(x_ref[...])
        gx_ref[...] = dx2_ref[...] + _rms_bwd(dh * g_ref[...], xh, r)
        upd = jnp.concatenate([jnp.sum(dh * xh, axis=0, keepdims=True),
                               jnp.zeros((SUBLANE - 1, D_MODEL), F32)], axis=0)

        @pl.when(i == 0)
        def _():
            sm_ref[...] = upd

        @pl.when(i > 0)
        def _():
            sm_ref[...] += upd

    return _call(
        body, "bwd_in", (t_len // TB,),
        [_rows(TB, N_STATE), _rows(TB, N_STATE), _rows(TB, SSM_W), _rows(TB, IN_COLS - SSM_W),
         _rows(TB, D_MODEL), _rows(TB, D_MODEL)] + [_whole()] * 4,
        [_rows(TB, D_MODEL), _rows(TB, IN_COLS), _acc(SUBLANE, D_MODEL)],
        [jax.ShapeDtypeStruct((t_len, D_MODEL), F32), jax.ShapeDtypeStruct((t_len, IN_COLS), BF16),
         jax.ShapeDtypeStruct((SUBLANE, D_MODEL), F32)],
        [lam_r, lam_i, du_part, drest, x, dx2, g_mix, w_in, bre, bim], jobs)


def _matmul_tn(a, b, name, out_shape, grid_ij, a_blk, a_map, b_blk, b_map, o_blk, o_map, jobs=()):
    tk = a_blk[0]
    nk = a.shape[0] // tk
    assert nk * tk == a.shape[0] and nk > 0

    def body(a_ref, b_ref, o_ref, acc_ref):
        k = pl.program_id(2)

        @pl.when(k == 0)
        def _():
            acc_ref[...] = jnp.zeros_like(acc_ref)

        acc_ref[...] += lax.dot_general(a_ref[...].astype(BF16), b_ref[...].astype(BF16),
                                        (((0,), (0,)), ((), ())), preferred_element_type=F32)

        @pl.when(k == nk - 1)
        def _():
            o_ref[...] = acc_ref[...]

    outs, per_job = _call(
        body, name, (grid_ij[0], grid_ij[1], nk),
        [pl.BlockSpec(a_blk, a_map), pl.BlockSpec(b_blk, b_map)], [pl.BlockSpec(o_blk, o_map)],
        [jax.ShapeDtypeStruct(out_shape, F32)], [a, b], jobs,
        scratch=[pltpu.VMEM((a_blk[1], b_blk[1]), F32)])
    return outs[0], per_job


def _dw_shards(a, b, name, tk, jobs=()):
    m, n = a.shape[1], b.shape[1]
    tn = n // N_CHIP
    tk = min(tk, a.shape[0])
    return _matmul_tn(a, b, name, (N_CHIP, m, tn), (1, N_CHIP),
                      (tk, m), lambda i, j, k: (k, 0), (tk, tn), lambda i, j, k: (k, j),
                      (None, m, tn), lambda i, j, k: (j, 0, 0), jobs)


def _dw_rows(a, b, name, tm, tk):
    m, n = a.shape[1], b.shape[1]
    tk = min(tk, a.shape[0])
    return _matmul_tn(a, b, name, (m, n), (m // tm, 1),
                      (tk, tm), lambda i, j, k: (k, i), (tk, n), lambda i, j, k: (k, 0),
                      (tm, n), lambda i, j, k: (i, 0))[0]


def _dw_cols(a, b, name, tn, sharded, jobs=(), a_cols=None):
    t_len, m = a.shape
    n = b.shape[1]
    a_spec = _whole()
    if a_cols is not None:
        tile, m = a_cols
        a_spec = pl.BlockSpec((t_len, m), lambda j: (0, tile))

    def body(a_ref, b_ref, o_ref):
        o_ref[...] = lax.dot_general(a_ref[...].astype(BF16), b_ref[...].astype(BF16),
                                     (((0,), (0,)), ((), ())), preferred_element_type=F32)

    if sharded:
        o_spec, o_shape = pl.BlockSpec((None, m, tn), lambda j: (j, 0, 0)), (n // tn, m, tn)
    else:
        o_spec, o_shape = pl.BlockSpec((m, tn), lambda j: (0, j)), (m, n)
    outs, per_job = _call(body, name, (n // tn,),
                          [a_spec, pl.BlockSpec((t_len, tn), lambda j: (0, j))], [o_spec],
                          [jax.ShapeDtypeStruct(o_shape, F32)], [a, b], jobs)
    return outs[0], per_job


def _dw_pair(a, m, b1, b2, name, jobs=()):
    t_len = a.shape[0]
    n_slab = DIAG_N // LANE
    rows_per_slab = LANE // n_slab

    def body(a_ref, b1_ref, b2_ref, o1_ref, o2_ref):
        for b_ref, o_ref in ((b1_ref, o1_ref), (b2_ref, o2_ref)):
            prod = lax.dot_general(a_ref[...].astype(BF16), b_ref[...].astype(BF16),
                                   (((0,), (0,)), ((), ())), preferred_element_type=F32)
            for j in range(n_slab):
                rows = slice(j * rows_per_slab, (j + 1) * rows_per_slab)
                o_ref[rows, :] = prod[rows, j * LANE:(j + 1) * LANE]

    tok = pl.BlockSpec((t_len, DIAG_N), lambda i: (0, i))
    out = pl.BlockSpec((LANE, LANE), lambda i: (i, 0))
    return _call(body, name, (m // LANE,),
                 [pl.BlockSpec((t_len, LANE), lambda i: (0, i)), tok, tok], [out, out],
                 [jax.ShapeDtypeStruct((m, LANE), F32)] * 2, [a, b1, b2], jobs)


def _prefetch_call(body, name, grid, scalars, in_specs, out_specs, out_shape, args):
    return pl.pallas_call(
        body, name=name,
        grid_spec=pltpu.PrefetchScalarGridSpec(num_scalar_prefetch=1, grid=grid, in_specs=in_specs,
                                               out_specs=out_specs),
        out_shape=out_shape, compiler_params=_params(len(grid)),
    )(scalars, *args)


def _place_shard(w, where, name, dtype, tr):
    rows, cols = w.shape

    def body(s_ref, w_ref, o_ref):
        o_ref[...] = w_ref[...].astype(dtype)

    return _prefetch_call(
        body, name, (rows // tr,), where,
        [pl.BlockSpec((tr, cols), lambda i, s: (i, 0))],
        pl.BlockSpec((None, tr, cols), lambda i, s: (s[0], i, 0)),
        jax.ShapeDtypeStruct((N_CHIP, rows, cols), dtype), [w])


def _place_shards(ws, where, name, dtype):
    n = len(ws)

    def body(s_ref, *refs):
        for t in range(n):
            refs[n + t][...] = refs[t][...].astype(dtype)

    return _prefetch_call(
        body, name, (1,), where,
        [pl.BlockSpec(w.shape, lambda i, s: (0, 0)) for w in ws],
        [pl.BlockSpec((None,) + w.shape, lambda i, s: (s[0], 0, 0)) for w in ws],
        [jax.ShapeDtypeStruct((N_CHIP,) + w.shape, dtype) for w in ws], ws)


def _add_sibling(gs, gots, where, name):
    n = len(gs)
    halves = [(g.shape[1] // 2, g.shape[2]) for g in gs]

    def body(s_ref, *refs):
        for t in range(n):
            refs[2 * n + t][...] = (refs[t][...] + refs[n + t][...]).astype(BF16)

    return _prefetch_call(
        body, name, (N_CHIP,), where,
        [pl.BlockSpec((None, hr, cs), lambda k, s: (k, s[1], 0)) for hr, cs in halves]
        + [pl.BlockSpec((None, hr, cs), lambda k, s: (k, 0, 0)) for hr, cs in halves],
        [pl.BlockSpec((None, hr, cs), lambda k, s: (k, 0, 0)) for hr, cs in halves],
        [jax.ShapeDtypeStruct((N_CHIP, hr, cs), BF16) for hr, cs in halves], list(gs) + list(gots))


def _add_chips(sums, gots, where, name):
    n = len(sums)
    halves = [s.shape[1:] for s in sums]

    def body(s_ref, *refs):
        for t in range(n):
            own_ref, got_ref = refs[t], refs[n + t]
            refs[2 * n + t][...] = ((own_ref[...].astype(F32) + got_ref[0].astype(F32))
                                    + got_ref[1].astype(F32)) + got_ref[2].astype(F32)

    return _prefetch_call(
        body, name, (1,), where,
        [pl.BlockSpec((None, hr, cs), lambda i, s: (s[0], 0, 0)) for hr, cs in halves]
        + [pl.BlockSpec((3, hr, cs), lambda i, s: (0, 0, 0)) for hr, cs in halves],
        [pl.BlockSpec((hr, cs), lambda i, s: (s[1], 0)) for hr, cs in halves],
        [jax.ShapeDtypeStruct((2 * hr, cs), F32) for hr, cs in halves], list(sums) + list(gots))


def _small_allreduce(pack):
    rows = pack.shape[0]
    half = rows // 2

    def body(in_ref, out_ref, sib_ref, slots_ref, s_a, r_a, s_b, r_b, s_c, r_c):
        x, y, c, chips = _place()
        k_me = 2 * x + y
        sib = (x, y, 1 - c)
        first = _remote(in_ref, sib_ref, s_a, r_a, sib)
        first.start()
        first.wait()
        mine = _half(rows, c)
        slots_ref[k_me] = in_ref[mine, :] + sib_ref[mine, :]
        cps = [_remote(slots_ref.at[k_me], slots_ref.at[k_me], s_b.at[j], r_b.at[j], (*ch, c))
               for j, ch in enumerate(chips)]
        for cp in cps:
            cp.start()
        for j, ch in enumerate(chips):
            slot = slots_ref.at[_chip_index(ch)]
            _remote(slot, slot, s_b.at[j], r_b.at[j], (*ch, c)).wait_recv()
        for cp in cps:
            cp.wait_send()
        out_ref[mine, :] = ((slots_ref[0] + slots_ref[1]) + slots_ref[2]) + slots_ref[3]
        last = _remote(out_ref.at[mine, :], out_ref.at[mine, :], s_c, r_c, sib)
        last.start()
        theirs = out_ref.at[_half(rows, 1 - c), :]
        _remote(theirs, theirs, s_c, r_c, sib).wait_recv()
        last.wait_send()

    return pl.pallas_call(
        body, name="small_allreduce", in_specs=[_whole()], out_specs=_whole(),
        out_shape=jax.ShapeDtypeStruct(pack.shape, F32),
        scratch_shapes=[pltpu.VMEM(pack.shape, F32), pltpu.VMEM((N_CHIP, half, LANE), F32),
                        pltpu.SemaphoreType.DMA, pltpu.SemaphoreType.DMA,
                        pltpu.SemaphoreType.DMA((3,)), pltpu.SemaphoreType.DMA((3,)),
                        pltpu.SemaphoreType.DMA, pltpu.SemaphoreType.DMA],
        compiler_params=_params(0),
    )(pack)


def _adamw_update(w_ref, g_ref, m_ref, v_ref, d_ref, mo_ref, vo_ref):
    gv = g_ref[...]
    mn = ADAM_B1 * m_ref[...] + (1.0 - ADAM_B1) * gv
    vn = ADAM_B2 * v_ref[...] + (1.0 - ADAM_B2) * (gv * gv)
    mo_ref[...] = mn
    vo_ref[...] = vn
    m_hat = mn / (1.0 - ADAM_B1 ** ADAM_STEP)
    v_hat = vn / (1.0 - ADAM_B2 ** ADAM_STEP)
    d_ref[...] = -ADAM_LR * (m_hat / (jnp.sqrt(v_hat) + ADAM_EPS) + ADAM_WD * w_ref[...])


def _adamw(w, g, m, v, name, tr):
    rows, cols = w.shape
    blk = _rows(tr, cols)
    parts = list(g) if isinstance(g, (list, tuple)) else [g]
    first = [0]
    for a in parts[:-1]:
        first.append(first[-1] + a.shape[0] // tr)

    def part_spec(a, f0):
        last = a.shape[0] // tr - 1
        return pl.BlockSpec((tr, cols), lambda i: (jnp.clip(i - f0, 0, last), 0))

    def body(w_ref, *refs):
        g_refs, (m_ref, v_ref, go_ref, d_ref, mo_ref, vo_ref) = refs[:len(parts)], refs[len(parts):]
        i = pl.program_id(0)
        go_ref[...] = g_refs[0][...]
        for q in range(1, len(parts)):
            @pl.when(i >= first[q])
            def _(q=q):
                go_ref[...] = g_refs[q][...]
        _adamw_update(w_ref, go_ref, m_ref, v_ref, d_ref, mo_ref, vo_ref)

    return _call(body, name, (rows // tr,),
                 [blk] + [part_spec(a, f0) for a, f0 in zip(parts, first)] + [blk] * 2, [blk] * 4,
                 [jax.ShapeDtypeStruct(w.shape, F32)] * 4, [w, *parts, m, v])[0]


def _adamw_many(ws, gs, ms, vs, name):
    n = len(ws)

    def body(*refs):
        for t in range(n):
            _adamw_update(*[refs[q * n + t] for q in range(7)])

    specs = [pl.BlockSpec(a.shape, lambda i, nd=a.ndim: (0,) * nd) for a in ws]
    outs = pl.pallas_call(
        body, name=name, grid=(1,), in_specs=specs * 4, out_specs=specs * 3,
        out_shape=[jax.ShapeDtypeStruct(a.shape, F32) for _ in range(3) for a in ws],
        compiler_params=_params(1),
    )(*ws, *gs, *ms, *vs)
    return outs[:n], outs[n:2 * n], outs[2 * n:]


def _ssm_discretize(a_re, a_im, log_dt, b_re, b_im):
    dt = jnp.exp(log_dt)[:, None]
    mag = jnp.exp(dt * a_re)
    abr = mag * jnp.cos(dt * a_im)
    abi = mag * jnp.sin(dt * a_im)
    den = a_re * a_re + a_im * a_im
    nr = abr - 1.0
    ni = abi
    f_re = (nr * a_re + ni * a_im) / den
    f_im = (ni * a_re - nr * a_im) / den
    bbr = f_re[..., None] * b_re - f_im[..., None] * b_im
    bbi = f_re[..., None] * b_im + f_im[..., None] * b_re
    return abr, abi, bbr, bbi


def _scan_tables(abr, abi):
    ar = abr.reshape(1, N_STATE)
    ai = abi.reshape(1, N_STATE)
    pr, pi = [ar], [ai]
    for _ in range(SUBLANE - 1):
        pr, pi = pr + [pr[-1] * ar - pi[-1] * ai], pi + [pr[-1] * ai + pi[-1] * ar]
    row = jnp.arange(SUBLANE)[:, None]
    tabs = []
    for d in (1, 2, 4):
        tabs.append(jnp.where(row >= d, pr[d - 1], 0.0))
        tabs.append(jnp.where(row >= d, pi[d - 1], 0.0))
    tabs.append(jnp.concatenate(pr, axis=0))
    tabs.append(jnp.concatenate(pi, axis=0))
    fwd = jnp.stack(tabs)
    sign = jnp.array([1.0, -1.0] * 4, F32)[:, None, None]
    return fwd, fwd[:, ::-1, :] * sign


def _block_diag_b(bb):
    strip = bb.transpose(2, 0, 1).reshape(SSM_H, N_STATE)
    rows = lax.broadcasted_iota(jnp.int32, (SSM_W, N_STATE), 0) // SSM_H
    cols = lax.broadcasted_iota(jnp.int32, (SSM_W, N_STATE), 1) // SSM_P
    return jnp.where(rows == cols, jnp.tile(strip, (SSM_G, 1)), 0.0).astype(BF16)


def _block_diag_c(cc):
    strip = cc.transpose(0, 2, 1).reshape(N_STATE, SSM_H)
    rows = lax.broadcasted_iota(jnp.int32, (N_STATE, SSM_W), 0) // SSM_P
    cols = lax.broadcasted_iota(jnp.int32, (N_STATE, SSM_W), 1) // SSM_H
    return jnp.where(rows == cols, jnp.tile(strip, (1, SSM_G)), 0.0).astype(BF16)


SMALL_SHAPES = {
    "g_mix": (D_MODEL,), "a_re": (SSM_G, SSM_P), "a_im": (SSM_G, SSM_P), "log_dt": (SSM_G,),
    "b_re": (SSM_G, SSM_P, SSM_H), "b_im": (SSM_G, SSM_P, SSM_H),
    "c_re": (SSM_G, SSM_H, SSM_P), "c_im": (SSM_G, SSM_H, SSM_P),
    "d_skip": (SSM_W,), "b_glu": (SSM_W,), "g_sgu": (SGU_W,), "w_s": (SGU_G, CHUNK, CHUNK),
    "b_s": (SGU_G, CHUNK), "g_ffn": (D_MODEL,), "conv_b": (2 * D_FF,), "g_final": (D_MODEL,),
}
PACK_ITEMS = [("loss", (1,))] + [(n, SMALL_SHAPES[n]) for n in SMALL] + [("conv_w", (3, 2 * D_FF))]
TILE = SUBLANE * LANE


def _item_rows(shape):
    return -(-math.prod(shape) // TILE) * SUBLANE


PACK_ROWS = -(-sum(_item_rows(s) for _, s in PACK_ITEMS) // (2 * SUBLANE)) * (2 * SUBLANE)


def _pack(values):
    parts, used = [], 0
    for name, shape in PACK_ITEMS:
        size, rows = math.prod(shape), _item_rows(shape)
        if name in values:
            flat = values[name].astype(F32).reshape(size)
            if rows * LANE > size:
                flat = jnp.pad(flat, (0, rows * LANE - size))
            parts.append(flat.reshape(rows, LANE))
        else:
            parts.append(jnp.zeros((rows, LANE), F32))
        used += rows
    if PACK_ROWS > used:
        parts.append(jnp.zeros((PACK_ROWS - used, LANE), F32))
    return jnp.concatenate(parts, axis=0)


def _unpack(pack):
    out, off = {}, 0
    for name, shape in PACK_ITEMS:
        rows = _item_rows(shape)
        out[name] = pack[off:off + rows].reshape(rows * LANE)[:math.prod(shape)].reshape(shape)
        off += rows
    return out


PLACE_ROWS = {"w_in": 256, "w_up": 256, "w_down": 352, "w_out": 256, "w_proj_a": 256,
              "w_proj_b": 256, "w_glu": 128}


def kernel(x, g_mix, w_in, a_re, a_im, log_dt, b_re, b_im, c_re, c_im, d_skip, w_glu, b_glu, w_proj_a, g_sgu, w_s, b_s, w_proj_b, w_out, g_ffn, w_up, conv_w, conv_b, w_down, g_final, loss_target, m_g_mix, m_w_in, m_a_re, m_a_im, m_log_dt, m_b_re, m_b_im, m_c_re, m_c_im, m_d_skip, m_w_glu, m_b_glu, m_w_proj_a, m_g_sgu, m_w_s, m_b_s, m_w_proj_b, m_w_out, m_g_ffn, m_w_up, m_conv_w, m_conv_b, m_w_down, m_g_final, v_g_mix, v_w_in, v_a_re, v_a_im, v_log_dt, v_b_re, v_b_im, v_c_re, v_c_im, v_d_skip, v_w_glu, v_b_glu, v_w_proj_a, v_g_sgu, v_w_s, v_b_s, v_w_proj_b, v_w_out, v_g_ffn, v_w_up, v_conv_w, v_conv_b, v_w_down, v_g_final):
    given = dict(locals())
    w = {n: given[n] for n in WEIGHTS}
    m = {n: given["m_" + n] for n in WEIGHTS}
    v = {n: given["v_" + n] for n in WEIGHTS}

    def shard2d(a):
        return a.reshape(a.shape[-2], a.shape[-1])

    chip = 2 * lax.axis_index("x") + lax.axis_index("y")
    where = jnp.stack([chip, lax.axis_index("c")]).astype(jnp.int32)
    xs, target = x[0], loss_target[0]
    small = {n: w[n].reshape(SMALL_SHAPES[n]) for n in SMALL}

    (abr, abi, bbr, bbi), disc_vjp = jax.vjp(_ssm_discretize, small["a_re"], small["a_im"],
                                             small["log_dt"], small["b_re"], small["b_im"])
    tab_f, tab_r = _scan_tables(abr, abi)
    bre = _block_diag_b(bbr)
    bim = _block_diag_b(bbi)
    cre = _block_diag_c(small["c_re"])
    cim = _block_diag_c(small["c_im"])
    tril = jnp.tril(jnp.ones((CHUNK, CHUNK), dtype=bool))
    ws = jnp.where(tril[None], small["w_s"], 0.0)
    ws_st = ws.reshape(SGU_G // 2, 2 * CHUNK, CHUNK).astype(BF16)
    wst_st = ws.transpose(0, 2, 1).reshape(SGU_G // 2, 2 * CHUNK, CHUNK).astype(BF16)
    bmat = jnp.repeat(small["b_s"].T, SGU_D, axis=1)
    g_mix2 = small["g_mix"].reshape(1, D_MODEL)
    g_ffn2 = small["g_ffn"].reshape(1, D_MODEL)
    g_final2 = small["g_final"].reshape(1, D_MODEL)
    g_sgu2 = small["g_sgu"].reshape(1, SGU_W)
    d_skip2 = small["d_skip"].reshape(1, SSM_W)
    b_glu2 = small["b_glu"].reshape(1, SSM_W)
    conv_b2 = small["conv_b"].reshape(1, 2 * D_FF)

    gat = {"w_in": _place_shard(shard2d(w["w_in"]), where, "place_w_in", BF16, PLACE_ROWS["w_in"])}
    gat.update(zip(BIG[1:], _place_shards([shard2d(w[n]) for n in BIG[1:]], where, "place_rest", BF16)))
    gat["conv_w"] = _place_shard(shard2d(w["conv_w"]), where, "place_conv_w", F32, 3)
    all_rows = (0, D_MODEL)
    (gat["w_in"],), = _comm("gather_in", [_job_gather(
        [gat["w_in"]], [(0, all_rows, ICI, (0.0, 0.5)), (0, all_rows, SIBLING, (0.5, 1.0))])])
    mixers = ["w_glu", "w_proj_a", "w_proj_b", "w_out"]
    rows = {n: (0, gat[n].shape[1]) for n in mixers}
    down_a, down_b = (0, D_FF // 8), (D_FF // 8, D_FF // 8)
    up_a, up_b = (0, 3 * D_MODEL // 8), (3 * D_MODEL // 8, 5 * D_MODEL // 8)
    span = (0.0, 1.0)

    names = mixers + ["conv_w", "w_down"]
    (p, h1, bur, bui), (got,) = _fwd_in(
        xs, g_mix2, gat["w_in"], bre, bim,
        [_job_gather([gat[n] for n in names],
                     [(i, rows[n], ICI, span) for i, n in enumerate(mixers)]
                     + [(4, None, ICI, span), (5, down_a, ICI, span)])])
    gat.update(zip(names, got))
    names = mixers + ["w_down", "w_up"]
    (str_, sti), (got,) = _scan_fwd(
        bur, bui, tab_f,
        [_job_gather([gat[n] for n in names],
                     [(i, rows[n], SIBLING, span) for i, n in enumerate(mixers)]
                     + [(4, down_a, SIBLING, span), (4, down_b, ICI, span), (5, up_a, ICI, span)])])
    gat.update(zip(names, got))
    w_glu_f = gat["w_glu"].reshape(SSM_W, SSM_W)
    w_out_f = gat["w_out"].reshape(D_MODEL, D_MODEL)
    conv_w_f = gat["conv_w"].transpose(1, 0, 2).reshape(3, 2 * D_FF)
    (x2, y0, z, mixed, ya, yb), ((gat["w_down"], gat["w_up"]),) = _fwd_mix(
        xs, p, str_, sti, cre, cim, d_skip2, w_glu_f, b_glu2, gat["w_proj_a"], g_sgu2, ws_st, bmat,
        gat["w_proj_b"], w_out_f,
        [_job_gather([gat["w_down"], gat["w_up"]],
                     [(0, down_b, SIBLING, span), (1, up_a, SIBLING, span),
                      (1, up_b, ICI, (0.0, 0.75)), (1, up_b, SIBLING, (0.75, 1.0))])])
    w_down_f = gat["w_down"].reshape(D_FF, D_MODEL)
    up, act, f, h2, dx3, sm_ffn = _fwd_ffn(x2, target, g_ffn2, gat["w_up"], conv_w_f, conv_b2,
                                           w_down_f, g_final2)

    def leg1_done(names, got):
        return _add_sibling([part[n] for n in names], got, where, "add_sibling_" + names[0])

    def leg2_done(names, sums, got):
        return _add_chips(sums, got, where, "add_chips_" + names[0])

    part, red = {}, {}
    part["w_down"] = _dw_rows(f, dx3, "dw_down", D_FF // 2, 2 * TK).reshape(
        N_CHIP, D_FF // N_CHIP, D_MODEL)
    (dx2, dup, sm_conv, sm_gffn), (got,) = _bwd_ffn(
        dx3, up, act, x2, g_ffn2, gat["w_up"], conv_w_f, w_down_f,
        [_job_sibling_halves([part["w_down"]])])
    sum_down = leg1_done(["w_down"], got)
    part["w_up"], (got,) = _dw_shards(h2, dup, "dw_up", 4 * TK, [_job_to_owner(sum_down)])
    red_down = leg2_done(["w_down"], sum_down, got)
    ((dsr, dsi, du_part, drest, mrg, dya, dyb, yap, dz, y1, sgu, dy0, sm_mix, dbm, dws),
     (got, (red["w_down"],))) = _bwd_mix(
        dx2, p, y0, z, mixed, ya, yb, w_out_f, gat["w_proj_a"], gat["w_proj_b"], w_glu_f, cre, cim,
        ws_st, wst_st, d_skip2, g_sgu2,
        [_job_sibling_halves([part["w_up"]]), _job_swap_halves(red_down)])
    sum_up = leg1_done(["w_up"], got)
    (lam_r, lam_i, dar8, dai8), (got,) = _scan_bwd(dsr, dsi, str_, sti, tab_r, [_job_to_owner(sum_up)])
    red_up = leg2_done(["w_up"], sum_up, got)
    mix4 = ["w_out", "w_proj_a", "w_proj_b", "w_glu"]
    part["w_out"] = _dw_cols(mrg, dx2, "dw_out", D_MODEL // 2, False)[0].reshape(
        N_CHIP, D_MODEL // N_CHIP, D_MODEL)
    part["w_proj_a"] = _dw_cols(yap, dya, "dw_proj_a", D_MODEL // N_CHIP, True)[0]
    part["w_proj_b"] = _dw_cols(sgu, dyb, "dw_proj_b", D_MODEL // N_CHIP, True)[0]
    part["w_glu"] = _dw_cols(y1, dz, "dw_glu", SSM_W, False)[0].reshape(
        N_CHIP, SSM_W // N_CHIP, SSM_W)
    (grad_x, dp, sm_gmix), (got, (red["w_up"],)) = _bwd_in(
        lam_r, lam_i, du_part, drest, xs, dx2, g_mix2, gat["w_in"], bre, bim,
        [_job_sibling_halves([part[n] for n in mix4]), _job_swap_halves(red_up)])
    sums_m = leg1_done(mix4, got)
    cs_in, half_in = IN_COLS // N_CHIP, D_MODEL // 2
    part["w_in_a"], (got,) = _dw_cols(h1, dp, "dw_in_a", cs_in, True, [_job_to_owner(sums_m)],
                                      a_cols=(0, half_in))
    red_m = leg2_done(mix4, sums_m, got)
    part["w_in_b"], (got, done_m) = _dw_cols(
        h1, dp, "dw_in_b", cs_in, True,
        [_job_sibling_halves([part["w_in_a"]]), _job_swap_halves(red_m)], a_cols=(1, half_in))
    red.update(zip(mix4, done_m))
    sum_a = leg1_done(["w_in_a"], got)
    (dbd_r, dbd_i), (got_a, got_b) = _dw_pair(
        p, SSM_W, lam_r, lam_i, "db_bar",
        [_job_to_owner(sum_a), _job_sibling_halves([part["w_in_b"]])])
    red_a = leg2_done(["w_in_a"], sum_a, got_a)
    sum_b = leg1_done(["w_in_b"], got_b)
    (dcd_r, dcd_i), (got, (red_a,)) = _dw_pair(
        dy0, SSM_W, str_, sti, "dc", [_job_to_owner(sum_b), _job_swap_halves(red_a)])
    red_b = leg2_done(["w_in_b"], sum_b, got)
    (red_b,), = _comm("swap_w_in", [_job_swap_halves(red_b)])

    def pick_c(slabs):
        two = LANE // SSM_P
        return jnp.einsum("jshsp->jshp", slabs.reshape(SSM_G // two, two, SSM_H, two, SSM_P)
                          ).reshape(SSM_G, SSM_H, SSM_P)

    def pick_b(slabs):
        return pick_c(slabs).transpose(0, 2, 1)

    dabr = jnp.sum(dar8, axis=0).reshape(SSM_G, SSM_P)
    dabi = jnp.sum(dai8, axis=0).reshape(SSM_G, SSM_P)
    d_a_re, d_a_im, d_log_dt, d_b_re, d_b_im = disc_vjp((dabr, dabi, pick_b(dbd_r), pick_b(dbd_i)))
    gsmall = {
        "g_mix": sm_gmix[0], "a_re": d_a_re, "a_im": d_a_im, "log_dt": d_log_dt,
        "b_re": d_b_re, "b_im": d_b_im, "c_re": pick_c(dcd_r), "c_im": -pick_c(dcd_i),
        "d_skip": sm_mix[0], "b_glu": sm_mix[1], "g_sgu": sm_mix[2],
        "w_s": jnp.where(tril[None], dws, 0.0),
        "b_s": dbm.reshape(CHUNK, SGU_G, SGU_D).sum(-1).T,
        "g_ffn": sm_gffn[0], "conv_b": sm_conv[3], "g_final": sm_ffn[0],
        "conv_w": sm_conv[0:3], "loss": sm_ffn[1, 0:1],
    }

    total_pack = _small_allreduce(_pack(gsmall))
    total = _unpack(total_pack)
    grads = dict(red)
    grads["w_in"] = [red_a, red_b]
    cs = 2 * D_FF // N_CHIP
    grads["conv_w"] = lax.dynamic_slice(total["conv_w"], (0, chip * cs), (3, cs))
    delta, new_m, new_v = {}, {}, {}
    for n in BIG + ("conv_w",):
        grads[n], delta[n], new_m[n], new_v[n] = _adamw(
            shard2d(w[n]), grads[n], shard2d(m[n]), shard2d(v[n]), "adamw_" + n, PLACE_ROWS.get(n, 3))
    for n in SMALL:
        grads[n] = total[n].reshape(w[n].shape)
    ud, um, uv = _adamw_many(*[[d[n] for n in SMALL] for d in (w, grads, m, v)], "adamw_small")
    for i, n in enumerate(SMALL):
        delta[n], new_m[n], new_v[n] = ud[i], um[i], uv[i]

    def like(d):
        return [d[n].reshape(w[n].shape) for n in WEIGHTS]

    return (total["loss"].reshape(()), grad_x.reshape(x.shape), *like(grads), *like(delta),
            *like(new_m), *like(new_v))
```

```python
import math

import jax
import jax.numpy as jnp
from jax import lax
from jax.experimental import pallas as pl
from jax.experimental.pallas import tpu as pltpu

F32 = jnp.float32
BF16 = jnp.bfloat16
MESH = pl.DeviceIdType.MESH

D_MODEL = 1024
SSM_W = 512
SSM_G = 32
SSM_H = 16
SSM_P = 64
N_STATE = SSM_G * SSM_P
DIAG_N = 128 * SSM_P // SSM_H
SGU_W = 512
SGU_G = 8
SGU_D = 64
CHUNK = 128
D_FF = 2816
IN_COLS = 3584
EPS = 1e-6
N_CHIP = 4

ADAM_LR = 0.001
ADAM_B1 = 0.9
ADAM_B2 = 0.999
ADAM_EPS = 1e-08
ADAM_WD = 0.01
ADAM_STEP = 10

SUBLANE = 8
LANE = 128
VMEM_LIMIT = 56 * 1024 * 1024
TB = 256
TK = 512
SCAN_LANES = 256
SCAN_UNROLL = 4
HALO = SUBLANE

BIG = ("w_in", "w_up", "w_down", "w_out", "w_proj_a", "w_proj_b", "w_glu")
SMALL = ("g_mix", "a_re", "a_im", "log_dt", "b_re", "b_im", "c_re", "c_im", "d_skip", "b_glu",
         "g_sgu", "w_s", "b_s", "g_ffn", "conv_b", "g_final")
WEIGHTS = ("g_mix", "w_in", "a_re", "a_im", "log_dt", "b_re", "b_im", "c_re", "c_im", "d_skip",
           "w_glu", "b_glu", "w_proj_a", "g_sgu", "w_s", "b_s", "w_proj_b", "w_out", "g_ffn",
           "w_up", "conv_w", "conv_b", "w_down", "g_final")

ANY = pl.BlockSpec(memory_space=pl.ANY)


def _params(n_grid):
    return pltpu.CompilerParams(dimension_semantics=("arbitrary",) * n_grid if n_grid else None,
                                vmem_limit_bytes=VMEM_LIMIT)


def _whole():
    return pl.BlockSpec(memory_space=pltpu.VMEM)


def _rows(tb, ncol):
    return pl.BlockSpec((tb, ncol), lambda i: (i, 0))


def _acc(nrow, ncol):
    return pl.BlockSpec((nrow, ncol), lambda i: (0, 0))


def _dot(a, b):
    return jnp.dot(a.astype(BF16), b.astype(BF16), preferred_element_type=F32)


def _dot_nt(a, b):
    return lax.dot_general(a.astype(BF16), b.astype(BF16), (((1,), (1,)), ((), ())),
                           preferred_element_type=F32)


def _sigmoid(v):
    return 0.5 * jnp.tanh(0.5 * v) + 0.5


_GELU_C = math.sqrt(2.0 / math.pi)


def _gelu(v):
    return 0.5 * v * (1.0 + jnp.tanh(_GELU_C * (v + 0.044715 * v * v * v)))


def _gelu_and_grad(v):
    v2 = v * v
    t = jnp.tanh(_GELU_C * v * (1.0 + 0.044715 * v2))
    half = 0.5 * (1.0 + t)
    return v * half, half + 0.5 * v * (1.0 - t * t) * _GELU_C * (1.0 + 3.0 * 0.044715 * v2)


def _rms_stats(v):
    r = lax.rsqrt(jnp.mean(v * v, axis=-1, keepdims=True) + EPS)
    return r, v * r


def _rms_bwd(dxh, xh, r):
    return r * (dxh - xh * jnp.mean(dxh * xh, axis=-1, keepdims=True))


def _place():
    x, y, c = lax.axis_index("x"), lax.axis_index("y"), lax.axis_index("c")
    chips = [(1 - x, y), (x, 1 - y), (1 - x, 1 - y)]
    return x, y, c, chips


def _chip_index(chip):
    return 2 * chip[0] + chip[1]


def _remote(src, dst, send_sem, recv_sem, device):
    return pltpu.make_async_remote_copy(src_ref=src, dst_ref=dst, send_sem=send_sem,
                                        recv_sem=recv_sem, device_id=device, device_id_type=MESH)


def _half(ref_rows, c):
    hr = ref_rows // 2
    return pl.ds(pl.multiple_of(c * hr, SUBLANE), hr)


class _Job:
    def __init__(self, hooks, n_sem, ins=(), inouts=(), outs=()):
        self.hooks, self.n_sem = list(hooks), n_sem
        self.ins, self.inouts, self.outs = list(ins), list(inouts), list(outs)


def _whole_span(start, finish):
    return [(0.0, "start", start), (1.0, "finish", finish)]


ICI, SIBLING = "ici", "sibling"


def _job_gather(bufs, legs):
    def copies(io, leg, first):
        b, window, kind, _ = legs[leg]
        x, y, c, chips = _place()
        k_me = 2 * x + y
        out = []
        for j, ch in enumerate(chips):
            k = _chip_index(ch)
            if window is None:
                src, land, dev = io[b].at[k_me], io[b].at[k], (*ch, c)
            else:
                r0, rows = window
                mine = pl.ds(pl.multiple_of(r0 + c * (rows // 2), SUBLANE), rows // 2)
                theirs = pl.ds(pl.multiple_of(r0 + (1 - c) * (rows // 2), SUBLANE), rows // 2)
                if kind == ICI:
                    src, land, dev = io[b].at[k_me, mine, :], io[b].at[k, mine, :], (*ch, c)
                else:
                    src, land, dev = io[b].at[k, mine, :], io[b].at[k, theirs, :], (x, y, 1 - c)
            out.append((src, land, first + j, dev))
        return out

    def starter(leg):
        def start(ins, io, outs, ssem, rsem):
            for src, _, i, dev in copies(io, leg, 3 * leg):
                _remote(src, src, ssem(i), rsem(i), dev).start()
        return start

    def finisher(leg):
        def finish(ins, io, outs, ssem, rsem):
            cps = copies(io, leg, 3 * leg)
            for _, land, i, dev in cps:
                _remote(land, land, ssem(i), rsem(i), dev).wait_recv()
            for src, _, i, dev in cps:
                _remote(src, src, ssem(i), rsem(i), dev).wait_send()
        return finish

    hooks = []
    for leg, (_, _, _, (begin, end)) in enumerate(legs):
        hooks += [(begin, "start", starter(leg)), (end, "finish", finisher(leg))]
    return _Job(hooks, 3 * len(legs), inouts=bufs)


def _job_sibling_halves(grads):
    n = len(grads)

    def build(ins, outs, ssem, rsem):
        x, y, c, _ = _place()
        return [_remote(ins[t].at[:, _half(grads[t].shape[1], 1 - c), :], outs[t], ssem(t), rsem(t),
                        (x, y, 1 - c)) for t in range(n)]

    def start(ins, io, outs, ssem, rsem):
        for cp in build(ins, outs, ssem, rsem):
            cp.start()

    def finish(ins, io, outs, ssem, rsem):
        for cp in build(ins, outs, ssem, rsem):
            cp.wait()

    return _Job(_whole_span(start, finish), n, ins=grads,
                outs=[jax.ShapeDtypeStruct((N_CHIP, g.shape[1] // 2, g.shape[2]), F32) for g in grads])


def _job_to_owner(sums):
    n = len(sums)

    def build(ins, outs, ssem, rsem):
        x, y, c, chips = _place()
        return [_remote(ins[t].at[_chip_index(ch)], outs[t].at[j], ssem(3 * t + j), rsem(3 * t + j),
                        (*ch, c)) for t in range(n) for j, ch in enumerate(chips)]

    def start(ins, io, outs, ssem, rsem):
        for cp in build(ins, outs, ssem, rsem):
            cp.start()

    def finish(ins, io, outs, ssem, rsem):
        for cp in build(ins, outs, ssem, rsem):
            cp.wait()

    return _Job(_whole_span(start, finish), 3 * n, ins=sums,
                outs=[jax.ShapeDtypeStruct((3,) + s.shape[1:], s.dtype) for s in sums])


def _job_swap_halves(bufs):
    n = len(bufs)

    def start(ins, io, outs, ssem, rsem):
        x, y, c, _ = _place()
        for t in range(n):
            mine = io[t].at[_half(bufs[t].shape[0], c), :]
            _remote(mine, mine, ssem(t), rsem(t), (x, y, 1 - c)).start()

    def finish(ins, io, outs, ssem, rsem):
        x, y, c, _ = _place()
        for t in range(n):
            theirs = io[t].at[_half(bufs[t].shape[0], 1 - c), :]
            _remote(theirs, theirs, ssem(t), rsem(t), (x, y, 1 - c)).wait_recv()
        for t in range(n):
            mine = io[t].at[_half(bufs[t].shape[0], c), :]
            _remote(mine, mine, ssem(t), rsem(t), (x, y, 1 - c)).wait_send()

    return _Job(_whole_span(start, finish), n, inouts=bufs)


def _call(body, name, grid, in_specs, out_specs, out_shape, args, jobs=(), scratch=()):
    n_in, n_out, n_scr = len(args), len(out_shape), len(scratch)
    job_in = [a for jb in jobs for a in jb.ins + jb.inouts]
    job_out = [s for jb in jobs
               for s in [jax.ShapeDtypeStruct(a.shape, a.dtype) for a in jb.inouts] + jb.outs]
    aliases, pos_in, pos_out = {}, n_in, n_out
    for jb in jobs:
        pos_in += len(jb.ins)
        for _ in jb.inouts:
            aliases[pos_in] = pos_out
            pos_in += 1
            pos_out += 1
        pos_out += len(jb.outs)
    n_sem = sum(jb.n_sem for jb in jobs)

    def wrapped(*refs):
        c_in = refs[:n_in]
        j_in = refs[n_in:n_in + len(job_in)]
        c_out = refs[n_in + len(job_in):n_in + len(job_in) + n_out]
        j_out = refs[n_in + len(job_in) + n_out:n_in + len(job_in) + n_out + len(job_out)]
        rest = refs[n_in + len(job_in) + n_out + len(job_out):]
        c_scr = rest[:n_scr]
        views, pi, po, ps = [], 0, 0, 0
        for jb in jobs:
            ins = j_in[pi:pi + len(jb.ins)]
            pi += len(jb.ins) + len(jb.inouts)
            io = j_out[po:po + len(jb.inouts)]
            new = j_out[po + len(jb.inouts):po + len(jb.inouts) + len(jb.outs)]
            po += len(jb.inouts) + len(jb.outs)
            send = (lambda i, o=ps: rest[n_scr].at[o + i])
            recv = (lambda i, o=ps: rest[n_scr + 1].at[o + i])
            ps += jb.n_sem
            views.append((ins, io, new, send, recv))

        def run(frac):
            for kind in ("finish", "start"):
                for jb, vw in zip(jobs, views):
                    for at, what, fn in jb.hooks:
                        if at == frac and what == kind:
                            fn(*vw)

        fracs = sorted({at for jb in jobs for at, _, _ in jb.hooks})
        if not grid:
            for frac in fracs:
                run(frac)
            return
        if jobs:
            assert len(grid) == 1 or set(fracs) <= {0.0, 1.0}
            first = pl.program_id(0) == 0
            last = pl.program_id(0) == grid[0] - 1
            for d in range(1, len(grid)):
                first = jnp.logical_and(first, pl.program_id(d) == 0)
                last = jnp.logical_and(last, pl.program_id(d) == grid[d] - 1)
            for frac in fracs:
                if frac < 1.0:
                    at_step = first if frac == 0.0 else pl.program_id(0) == int(frac * grid[0])
                    pl.when(at_step)(lambda frac=frac: run(frac))
        body(*c_in, *c_out, *c_scr)
        if jobs and 1.0 in fracs:
            pl.when(last)(lambda: run(1.0))

    sems = [pltpu.SemaphoreType.DMA((n_sem,)), pltpu.SemaphoreType.DMA((n_sem,))] if jobs else []
    kwargs = dict(grid=grid) if grid else {}
    res = pl.pallas_call(
        wrapped, name=name, in_specs=list(in_specs) + [ANY] * len(job_in),
        out_specs=list(out_specs) + [ANY] * len(job_out),
        out_shape=list(out_shape) + job_out, scratch_shapes=list(scratch) + sems,
        input_output_aliases=aliases, compiler_params=_params(len(grid)), **kwargs,
    )(*args, *job_in)
    outs, pos, per_job = list(res[:n_out]), n_out, []
    for jb in jobs:
        k = len(jb.inouts) + len(jb.outs)
        per_job.append(list(res[pos:pos + k]))
        pos += k
    return outs, per_job


def _comm(name, jobs):
    return _call(None, name, (), [], [], [], [], jobs)[1]


def _fwd_in(x, g_mix, w_in, bre, bim, jobs=()):
    t_len = x.shape[0]
    cs = IN_COLS // N_CHIP

    def body(x_ref, g_ref, w_ref, bre_ref, bim_ref, p_ref, h_ref, bur_ref, bui_ref):
        xv = x_ref[...]
        r, xh = _rms_stats(xv)
        h = (xh * g_ref[...]).astype(BF16)
        h_ref[...] = h
        for k in range(N_CHIP):
            p_ref[:, k * cs:(k + 1) * cs] = jnp.dot(h, w_ref[k],
                                                    preferred_element_type=F32).astype(BF16)
        u = p_ref[:, 0:SSM_W]
        for i in range(SSM_W // LANE):
            rows, cols = slice(i * LANE, (i + 1) * LANE), slice(i * DIAG_N, (i + 1) * DIAG_N)
            bur_ref[:, cols] = jnp.dot(u[:, rows], bre_ref[rows, cols],
                                       preferred_element_type=F32).astype(BF16)
            bui_ref[:, cols] = jnp.dot(u[:, rows], bim_ref[rows, cols],
                                       preferred_element_type=F32).astype(BF16)

    return _call(
        body, "fwd_in", (t_len // TB,),
        [_rows(TB, D_MODEL), _whole(), _whole(), _whole(), _whole()],
        [_rows(TB, IN_COLS), _rows(TB, D_MODEL), _rows(TB, N_STATE), _rows(TB, N_STATE)],
        [jax.ShapeDtypeStruct((t_len, IN_COLS), BF16), jax.ShapeDtypeStruct((t_len, D_MODEL), BF16),
         jax.ShapeDtypeStruct((t_len, N_STATE), BF16), jax.ShapeDtypeStruct((t_len, N_STATE), BF16)],
        [x, g_mix, w_in, bre, bim], jobs)


def _scan_local(xr, xi, tab, shifts):
    for q, s in enumerate(shifts):
        ar, ai = tab[2 * q], tab[2 * q + 1]
        rr = pltpu.roll(xr, s, 0)
        ri = pltpu.roll(xi, s, 0)
        xr, xi = xr + ar * rr - ai * ri, xi + ar * ri + ai * rr
    return xr, xi


def _scan_carry(xr, xi, tab, cr, ci):
    pr, pi = tab[6], tab[7]
    return xr + pr * cr - pi * ci, xi + pr * ci + pi * cr


BF16_TILE = 2 * SUBLANE


def _load_blocks(r_ref, i_ref, base):
    out = []
    for q in range(SCAN_UNROLL // 2):
        rows = pl.ds(pl.multiple_of(base + q * BF16_TILE, BF16_TILE), BF16_TILE)
        vr, vi = r_ref[rows, :].astype(F32), i_ref[rows, :].astype(F32)
        out += [(vr[:SUBLANE], vi[:SUBLANE]), (vr[SUBLANE:], vi[SUBLANE:])]
    return out


def _store_blocks(r_ref, i_ref, base, blocks):
    for q in range(SCAN_UNROLL // 2):
        rows = pl.ds(pl.multiple_of(base + q * BF16_TILE, BF16_TILE), BF16_TILE)
        r_ref[rows, :] = jnp.concatenate([blocks[2 * q][0], blocks[2 * q + 1][0]], 0).astype(r_ref.dtype)
        i_ref[rows, :] = jnp.concatenate([blocks[2 * q][1], blocks[2 * q + 1][1]], 0).astype(i_ref.dtype)


def _scan_fwd(bur, bui, tab, jobs=()):
    t_len = bur.shape[0]
    nblk = t_len // SUBLANE
    lb = SCAN_LANES

    def body(br_ref, bi_ref, tab_ref, sr_ref, si_ref):
        tab_v = [tab_ref[q] for q in range(8)]

        def step(k, carry):
            cr, ci = carry
            base = pl.multiple_of(k * SCAN_UNROLL * SUBLANE, SCAN_UNROLL * SUBLANE)
            local = [_scan_local(xr, xi, tab_v, (1, 2, 4))
                     for xr, xi in _load_blocks(br_ref, bi_ref, base)]
            done = []
            for xr, xi in local:
                xr, xi = _scan_carry(xr, xi, tab_v, cr, ci)
                done.append((xr, xi))
                cr, ci = xr[SUBLANE - 1:SUBLANE, :], xi[SUBLANE - 1:SUBLANE, :]
            _store_blocks(sr_ref, si_ref, base, done)
            return cr, ci

        zero = jnp.zeros((1, lb), F32)
        lax.fori_loop(0, nblk // SCAN_UNROLL, step, (zero, zero))

    col = pl.BlockSpec((t_len, lb), lambda j: (0, j))
    return _call(
        body, "scan_fwd", (N_STATE // lb,),
        [col, col, pl.BlockSpec((8, SUBLANE, lb), lambda j: (0, 0, j))], [col, col],
        [jax.ShapeDtypeStruct((t_len, N_STATE), BF16)] * 2, [bur, bui, tab], jobs)


def _sgu_mix(v, ws_ref, lane_lo):
    rows = []
    for c0 in range(0, v.shape[0], CHUNK):
        slabs = []
        for j in range(SGU_W // LANE):
            prod = jnp.dot(ws_ref[j], v[c0:c0 + CHUNK, j * LANE:(j + 1) * LANE].astype(BF16),
                           preferred_element_type=F32)
            slabs.append(jnp.where(lane_lo, prod[:CHUNK], prod[CHUNK:]))
        rows.append(jnp.concatenate(slabs, axis=1))
    return jnp.concatenate(rows, axis=0) if len(rows) > 1 else rows[0]


def _fwd_mix(x, p, str_, sti, cre, cim, d_skip, w_glu, b_glu, w_pa, g_sgu, ws_st, bmat, w_pb, w_out,
             jobs=()):
    t_len = x.shape[0]

    def body(x_ref, p_ref, sr_ref, si_ref, cre_ref, cim_ref, dsk_ref, wg_ref, bg_ref, wpa_ref,
             gs_ref, ws_ref, bm_ref, wpb_ref, wo_ref,
             x2_ref, y0_ref, z_ref, mx_ref, ya_ref, yb_ref):
        u = p_ref[:, 0:SSM_W].astype(F32)
        y0 = jnp.concatenate(
            [_dot(sr_ref[:, i * DIAG_N:(i + 1) * DIAG_N],
                  cre_ref[i * DIAG_N:(i + 1) * DIAG_N, i * LANE:(i + 1) * LANE])
             - _dot(si_ref[:, i * DIAG_N:(i + 1) * DIAG_N],
                    cim_ref[i * DIAG_N:(i + 1) * DIAG_N, i * LANE:(i + 1) * LANE])
             for i in range(SSM_W // LANE)], axis=1) + dsk_ref[...] * u
        y0_ref[...] = y0.astype(BF16)
        y1 = _gelu(y0)
        z = _dot(y1, wg_ref[...]) + bg_ref[...]
        z_ref[...] = z.astype(BF16)
        ya_pre = (y1 * _sigmoid(z)).astype(BF16)
        ya = jnp.concatenate([jnp.dot(ya_pre, wpa_ref[k], preferred_element_type=F32)
                              for k in range(N_CHIP)], axis=1)
        ya_ref[...] = ya.astype(BF16)

        uvg = _gelu(p_ref[:, SSM_W:SSM_W + 2 * SGU_W].astype(F32))
        u2 = uvg[:, :SGU_W]
        _, vh = _rms_stats(uvg[:, SGU_W:])
        v3 = vh * gs_ref[...]
        lane_lo = lax.broadcasted_iota(jnp.int32, (CHUNK, LANE), 1) < SGU_D
        bias = jnp.concatenate([bm_ref[...]] * (TB // CHUNK), axis=0)
        mixed = _sgu_mix(v3, ws_ref, lane_lo) + bias
        mx_ref[...] = mixed.astype(BF16)
        sgu = (u2 * mixed).astype(BF16)
        yb = jnp.concatenate([jnp.dot(sgu, wpb_ref[k], preferred_element_type=F32)
                              for k in range(N_CHIP)], axis=1)
        yb_ref[...] = yb.astype(BF16)

        lg0 = SSM_W + 2 * SGU_W
        ga = _sigmoid(p_ref[:, lg0:lg0 + D_MODEL].astype(F32))
        gb = _sigmoid(p_ref[:, lg0 + D_MODEL:lg0 + 2 * D_MODEL].astype(F32))
        mrg = ga * ya + gb * yb
        x2_ref[...] = x_ref[...] + _dot(mrg, wo_ref[...])

    return _call(
        body, "fwd_mix", (t_len // TB,),
        [_rows(TB, D_MODEL), _rows(TB, IN_COLS), _rows(TB, N_STATE), _rows(TB, N_STATE)]
        + [_whole()] * 11,
        [_rows(TB, D_MODEL), _rows(TB, SSM_W), _rows(TB, SSM_W), _rows(TB, SGU_W),
         _rows(TB, D_MODEL), _rows(TB, D_MODEL)],
        [jax.ShapeDtypeStruct((t_len, D_MODEL), F32), jax.ShapeDtypeStruct((t_len, SSM_W), BF16),
         jax.ShapeDtypeStruct((t_len, SSM_W), BF16), jax.ShapeDtypeStruct((t_len, SGU_W), BF16),
         jax.ShapeDtypeStruct((t_len, D_MODEL), BF16), jax.ShapeDtypeStruct((t_len, D_MODEL), BF16)],
        [x, p, str_, sti, cre, cim, d_skip, w_glu, b_glu, w_pa, g_sgu, ws_st, bmat, w_pb, w_out], jobs)


def _conv_taps(v, cw_ref, c0, width):
    w0 = cw_ref[0:1, c0:c0 + width]
    w1 = cw_ref[1:2, c0:c0 + width]
    w2 = cw_ref[2:3, c0:c0 + width]
    return w0 * pltpu.roll(v, 2, 0) + w1 * pltpu.roll(v, 1, 0) + w2 * v


def _fwd_ffn(x2, target, g_ffn, w_up, conv_w, conv_b, w_down, g_final):
    t_len = x2.shape[0]
    half = D_FF // 2
    blocks_per_halo = TB // HALO

    def body(x2_ref, xp_ref, tg_ref, gf_ref, wu_ref, cw_ref, cb_ref, wd_ref, gl_ref,
             up_ref, act_ref, f_ref, h2_ref, dx3_ref, sm_ref):
        i = pl.program_id(0)
        xe = jnp.concatenate([xp_ref[...] * jnp.where(i == 0, 0.0, 1.0), x2_ref[...]], axis=0)
        _, xh = _rms_stats(xe)
        h2 = (xh * gf_ref[...]).astype(BF16)
        h2_ref[...] = h2[HALO:]
        acc = jnp.zeros((TB, D_MODEL), F32)
        for hc in range(2):
            ca = hc * half
            cb = D_FF + hc * half
            ua = jnp.dot(h2, wu_ref[hc], preferred_element_type=F32)
            ub = jnp.dot(h2, wu_ref[2 + hc], preferred_element_type=F32)
            up_ref[:, ca:ca + half] = ua[HALO:].astype(BF16)
            up_ref[:, cb:cb + half] = ub[HALO:].astype(BF16)
            ac = _conv_taps(ua, cw_ref, ca, half)[HALO:] + cb_ref[:, ca:ca + half]
            bc = _conv_taps(ub, cw_ref, cb, half)[HALO:] + cb_ref[:, cb:cb + half]
            act_ref[:, ca:ca + half] = ac.astype(BF16)
            act_ref[:, cb:cb + half] = bc.astype(BF16)
            f = (ac * _sigmoid(ac) * bc).astype(BF16)
            f_ref[:, ca:ca + half] = f
            acc = acc + jnp.dot(f, wd_ref[ca:ca + half, :], preferred_element_type=F32)
        x3 = x2_ref[...] + acc
        r3, xh3 = _rms_stats(x3)
        err = xh3 * gl_ref[...] - tg_ref[...]
        dout = err * (1.0 / D_MODEL)
        dx3_ref[...] = _rms_bwd(dout * gl_ref[...], xh3, r3)
        dgl = jnp.sum(dout * xh3, axis=0, keepdims=True)
        loss = 0.5 * jnp.sum(jnp.mean(err * err, axis=-1, keepdims=True), axis=0, keepdims=True)
        upd = jnp.concatenate([dgl, jnp.broadcast_to(loss, (1, D_MODEL)),
                               jnp.zeros((SUBLANE - 2, D_MODEL), F32)], axis=0)

        @pl.when(i == 0)
        def _():
            sm_ref[...] = upd

        @pl.when(i > 0)
        def _():
            sm_ref[...] += upd

    prev = pl.BlockSpec((HALO, D_MODEL), lambda i: (jnp.maximum(i * blocks_per_halo - 1, 0), 0))
    return _call(
        body, "fwd_ffn", (t_len // TB,),
        [_rows(TB, D_MODEL), prev, _rows(TB, D_MODEL)] + [_whole()] * 6,
        [_rows(TB, 2 * D_FF), _rows(TB, 2 * D_FF), _rows(TB, D_FF), _rows(TB, D_MODEL),
         _rows(TB, D_MODEL), _acc(SUBLANE, D_MODEL)],
        [jax.ShapeDtypeStruct((t_len, 2 * D_FF), BF16), jax.ShapeDtypeStruct((t_len, 2 * D_FF), BF16),
         jax.ShapeDtypeStruct((t_len, D_FF), BF16), jax.ShapeDtypeStruct((t_len, D_MODEL), BF16),
         jax.ShapeDtypeStruct((t_len, D_MODEL), F32), jax.ShapeDtypeStruct((SUBLANE, D_MODEL), F32)],
        [x2, x2, target, g_ffn, w_up, conv_w, conv_b, w_down, g_final])[0]


def _bwd_ffn(dx3, up, act, x2, g_ffn, w_up, conv_w, w_down, jobs=()):
    t_len = x2.shape[0]
    half = D_FF // 2
    nblk = t_len // TB
    halo_b = 2 * HALO
    n_e = TB + HALO

    def body(dx_ref, dxn_ref, up_ref, act_ref, actn_ref, x2_ref, gf_ref, wu_ref, cw_ref,
             wd_ref, dx2_ref, dup_ref, smw_ref, smg_ref):
        i = pl.program_id(0)
        keep_last = jnp.where(i == nblk - 1, 0.0, 1.0)
        dxe = jnp.concatenate([dx_ref[...], dxn_ref[...] * keep_last], axis=0).astype(BF16)
        dh2 = jnp.zeros((TB, D_MODEL), F32)
        zpad = jnp.zeros((1, half), F32)
        for hc in range(2):
            ca = hc * half
            cb = D_FF + hc * half
            ac = jnp.concatenate([act_ref[:, ca:ca + half].astype(F32),
                                  actn_ref[:, ca:ca + half].astype(F32)[:HALO]], axis=0)
            bc = jnp.concatenate([act_ref[:, cb:cb + half].astype(F32),
                                  actn_ref[:, cb:cb + half].astype(F32)[:HALO]], axis=0)
            wa = [cw_ref[k:k + 1, ca:ca + half] for k in range(3)]
            wb = [cw_ref[k:k + 1, cb:cb + half] for k in range(3)]
            df = lax.dot_general(dxe, wd_ref[ca:ca + half, :], (((1,), (1,)), ((), ())),
                                 preferred_element_type=F32)
            sg = _sigmoid(ac)
            da = df * bc * sg * (1.0 + ac * (1.0 - sg))
            db = df * ac * sg
            da1, da2 = pltpu.roll(da, n_e - 1, 0), pltpu.roll(da, n_e - 2, 0)
            db1, db2 = pltpu.roll(db, n_e - 1, 0), pltpu.roll(db, n_e - 2, 0)
            dua = (wa[2] * da + wa[1] * da1 + wa[0] * da2)[:TB]
            dub = (wb[2] * db + wb[1] * db1 + wb[0] * db2)[:TB]
            dup_ref[:, ca:ca + half] = dua.astype(BF16)
            dup_ref[:, cb:cb + half] = dub.astype(BF16)
            dh2 = dh2 + _dot_nt(dua, wu_ref[hc]) + _dot_nt(dub, wu_ref[2 + hc])
            rows = []
            for u_, d0, d1, d2 in ((up_ref[:, ca:ca + half].astype(F32), da, da1, da2),
                                   (up_ref[:, cb:cb + half].astype(F32), db, db1, db2)):
                rows.append([jnp.sum(u_ * d2[:TB], axis=0, keepdims=True),
                             jnp.sum(u_ * d1[:TB], axis=0, keepdims=True),
                             jnp.sum(u_ * d0[:TB], axis=0, keepdims=True),
                             jnp.sum(d0[:TB], axis=0, keepdims=True)])
            for c0, rws in ((ca, rows[0]), (cb, rows[1])):
                upd = jnp.concatenate(rws + [zpad] * (SUBLANE - 4), axis=0)

                @pl.when(i == 0)
                def _(upd=upd, c0=c0):
                    smw_ref[:, c0:c0 + half] = upd

                @pl.when(i > 0)
                def _(upd=upd, c0=c0):
                    smw_ref[:, c0:c0 + half] += upd

        r2, xh2 = _rms_stats(x2_ref[...])
        dx2_ref[...] = dx_ref[...] + _rms_bwd(dh2 * gf_ref[...], xh2, r2)
        updg = jnp.concatenate([jnp.sum(dh2 * xh2, axis=0, keepdims=True),
                                jnp.zeros((SUBLANE - 1, D_MODEL), F32)], axis=0)

        @pl.when(i == 0)
        def _():
            smg_ref[...] = updg

        @pl.when(i > 0)
        def _():
            smg_ref[...] += updg

    nxt_d = pl.BlockSpec((HALO, D_MODEL),
                         lambda i: (jnp.minimum((i + 1) * (TB // HALO), t_len // HALO - 1), 0))
    nxt_a = pl.BlockSpec((halo_b, 2 * D_FF),
                         lambda i: (jnp.minimum((i + 1) * (TB // halo_b), t_len // halo_b - 1), 0))
    return _call(
        body, "bwd_ffn", (nblk,),
        [_rows(TB, D_MODEL), nxt_d, _rows(TB, 2 * D_FF), _rows(TB, 2 * D_FF), nxt_a,
         _rows(TB, D_MODEL)] + [_whole()] * 4,
        [_rows(TB, D_MODEL), _rows(TB, 2 * D_FF), _acc(SUBLANE, 2 * D_FF), _acc(SUBLANE, D_MODEL)],
        [jax.ShapeDtypeStruct((t_len, D_MODEL), F32), jax.ShapeDtypeStruct((t_len, 2 * D_FF), BF16),
         jax.ShapeDtypeStruct((SUBLANE, 2 * D_FF), F32), jax.ShapeDtypeStruct((SUBLANE, D_MODEL), F32)],
        [dx3, dx3, up, act, act, x2, g_ffn, w_up, conv_w, w_down], jobs)


def _bwd_mix(dx2, p, y0, z, mixed, ya, yb, w_out, w_pa, w_pb, w_glu, cre, cim, ws_st, wst_st,
             d_skip, g_sgu, jobs=()):
    t_len = dx2.shape[0]
    pc = D_MODEL // N_CHIP
    n_slab = SGU_W // LANE

    def body(dx_ref, p_ref, y0_ref, z_ref, mx_ref, ya_ref, yb_ref, wo_ref, wpa_ref, wpb_ref,
             wg_ref, cre_ref, cim_ref, ws_ref, wst_ref, dsk_ref, gs_ref,
             dsr_ref, dsi_ref, du_ref, drest_ref, mrg_ref, dya_ref, dyb_ref, yap_ref, dz_ref,
             y1_ref, sgu_ref, dy0_ref, sm_ref, dbm_ref, dws_ref):
        i = pl.program_id(0)
        first = i == 0
        lg0 = SSM_W + 2 * SGU_W
        dmrg = _dot_nt(dx_ref[...], wo_ref[...])
        ga = _sigmoid(p_ref[:, lg0:lg0 + D_MODEL].astype(F32))
        gb = _sigmoid(p_ref[:, lg0 + D_MODEL:lg0 + 2 * D_MODEL].astype(F32))
        yav = ya_ref[...].astype(F32)
        ybv = yb_ref[...].astype(F32)
        mrg_ref[...] = (ga * yav + gb * ybv).astype(BF16)
        drest_ref[:, 2 * SGU_W:2 * SGU_W + D_MODEL] = (dmrg * yav * ga * (1.0 - ga)).astype(BF16)
        drest_ref[:, 2 * SGU_W + D_MODEL:] = (dmrg * ybv * gb * (1.0 - gb)).astype(BF16)
        dya = (dmrg * ga).astype(BF16)
        dyb = (dmrg * gb).astype(BF16)
        dya_ref[...] = dya
        dyb_ref[...] = dyb

        y0v = y0_ref[...].astype(F32)
        y1, y1_grad = _gelu_and_grad(y0v)
        sz = _sigmoid(z_ref[...].astype(F32))
        y1_ref[...] = y1.astype(BF16)
        yap_ref[...] = (y1 * sz).astype(BF16)
        dyap = jnp.zeros((TB, SSM_W), F32)
        for k in range(N_CHIP):
            dyap = dyap + _dot_nt(dya[:, k * pc:(k + 1) * pc], wpa_ref[k])
        dz = dyap * y1 * sz * (1.0 - sz)
        dz_ref[...] = dz.astype(BF16)
        dy0 = (dyap * sz + _dot_nt(dz, wg_ref[...])) * y1_grad
        dy0_ref[...] = dy0.astype(BF16)
        u = p_ref[:, 0:SSM_W].astype(F32)
        du_ref[...] = dy0 * dsk_ref[...]
        for i in range(SSM_W // LANE):
            rows, cols = slice(i * DIAG_N, (i + 1) * DIAG_N), slice(i * LANE, (i + 1) * LANE)
            dsr_ref[:, rows] = _dot_nt(dy0[:, cols], cre_ref[rows, cols]).astype(BF16)
            dsi_ref[:, rows] = (-_dot_nt(dy0[:, cols], cim_ref[rows, cols])).astype(BF16)

        uv = p_ref[:, SSM_W:lg0].astype(F32)
        uvg, gg = _gelu_and_grad(uv)
        u2 = uvg[:, :SGU_W]
        rv, vh = _rms_stats(uvg[:, SGU_W:])
        v3 = vh * gs_ref[...]
        mixed = mx_ref[...].astype(F32)
        dsgu = jnp.zeros((TB, SGU_W), F32)
        for k in range(N_CHIP):
            dsgu = dsgu + _dot_nt(dyb[:, k * pc:(k + 1) * pc], wpb_ref[k])
        sgu_ref[...] = (u2 * mixed).astype(BF16)
        du2 = dsgu * mixed
        dmix = dsgu * u2
        lane_lo = lax.broadcasted_iota(jnp.int32, (CHUNK, LANE), 1) < SGU_D
        dv3 = _sgu_mix(dmix, wst_ref, lane_lo)
        dbm = jnp.zeros((CHUNK, SGU_W), F32)
        for c0 in range(0, TB, CHUNK):
            dbm = dbm + dmix[c0:c0 + CHUNK]
        for j in range(n_slab):
            lo = jnp.zeros((CHUNK, CHUNK), F32)
            hi = jnp.zeros((CHUNK, CHUNK), F32)
            for c0 in range(0, TB, CHUNK):
                dsl = dmix[c0:c0 + CHUNK, j * LANE:(j + 1) * LANE]
                vsl = v3[c0:c0 + CHUNK, j * LANE:(j + 1) * LANE]
                lo = lo + _dot_nt(jnp.where(lane_lo, dsl, 0.0), vsl)
                hi = hi + _dot_nt(jnp.where(lane_lo, 0.0, dsl), vsl)

            @pl.when(first)
            def _(lo=lo, hi=hi, j=j):
                dws_ref[2 * j] = lo
                dws_ref[2 * j + 1] = hi

            @pl.when(jnp.logical_not(first))
            def _(lo=lo, hi=hi, j=j):
                dws_ref[2 * j] += lo
                dws_ref[2 * j + 1] += hi

        dv2 = _rms_bwd(dv3 * gs_ref[...], vh, rv)
        drest_ref[:, 0:SGU_W] = (du2 * gg[:, :SGU_W]).astype(BF16)
        drest_ref[:, SGU_W:2 * SGU_W] = (dv2 * gg[:, SGU_W:]).astype(BF16)

        upd = jnp.concatenate([jnp.sum(dy0 * u, axis=0, keepdims=True),
                               jnp.sum(dz, axis=0, keepdims=True),
                               jnp.sum(dv3 * vh, axis=0, keepdims=True),
                               jnp.zeros((SUBLANE - 3, SSM_W), F32)], axis=0)

        @pl.when(first)
        def _():
            sm_ref[...] = upd
            dbm_ref[...] = dbm

        @pl.when(jnp.logical_not(first))
        def _():
            sm_ref[...] += upd
            dbm_ref[...] += dbm

    rest = 2 * SGU_W + 2 * D_MODEL
    bf_d, bf_s = jax.ShapeDtypeStruct((t_len, D_MODEL), BF16), jax.ShapeDtypeStruct((t_len, SSM_W), BF16)
    return _call(
        body, "bwd_mix", (t_len // TB,),
        [_rows(TB, D_MODEL), _rows(TB, IN_COLS), _rows(TB, SSM_W), _rows(TB, SSM_W),
         _rows(TB, SGU_W), _rows(TB, D_MODEL), _rows(TB, D_MODEL)] + [_whole()] * 10,
        [_rows(TB, N_STATE), _rows(TB, N_STATE), _rows(TB, SSM_W), _rows(TB, rest),
         _rows(TB, D_MODEL), _rows(TB, D_MODEL), _rows(TB, D_MODEL), _rows(TB, SSM_W),
         _rows(TB, SSM_W), _rows(TB, SSM_W), _rows(TB, SGU_W), _rows(TB, SSM_W),
         _acc(SUBLANE, SSM_W), _acc(CHUNK, SGU_W),
         pl.BlockSpec((SGU_G, CHUNK, CHUNK), lambda i: (0, 0, 0))],
        [jax.ShapeDtypeStruct((t_len, N_STATE), BF16), jax.ShapeDtypeStruct((t_len, N_STATE), BF16),
         jax.ShapeDtypeStruct((t_len, SSM_W), F32), jax.ShapeDtypeStruct((t_len, rest), BF16),
         bf_d, bf_d, bf_d, bf_s, bf_s, bf_s, bf_s, bf_s,
         jax.ShapeDtypeStruct((SUBLANE, SSM_W), F32), jax.ShapeDtypeStruct((CHUNK, SGU_W), F32),
         jax.ShapeDtypeStruct((SGU_G, CHUNK, CHUNK), F32)],
        [dx2, p, y0, z, mixed, ya, yb, w_out, w_pa, w_pb, w_glu, cre, cim, ws_st, wst_st, d_skip,
         g_sgu], jobs)


def _scan_bwd(dsr, dsi, str_, sti, tab_rev, jobs=()):
    t_len = dsr.shape[0]
    nblk = t_len // SUBLANE
    lb = SCAN_LANES

    def body(dr_ref, di_ref, sr_ref, si_ref, tab_ref, lr_ref, li_ref, dar_ref, dai_ref):
        tab_v = [tab_ref[q] for q in range(8)]
        row0 = lax.broadcasted_iota(jnp.int32, (SUBLANE, lb), 0) == 0
        tile = BF16_TILE

        def step(k, carry):
            cr, ci, acr, aci = carry
            base = pl.multiple_of((nblk - (k + 1) * SCAN_UNROLL) * SUBLANE, SCAN_UNROLL * SUBLANE)
            state = _load_blocks(sr_ref, si_ref, base)
            before = pl.ds(pl.multiple_of(jnp.maximum(base - tile, 0), tile), tile)
            has_before = jnp.where(base > 0, 1.0, 0.0)
            prev = (sr_ref[before, :].astype(F32)[tile - 1:tile] * has_before,
                    si_ref[before, :].astype(F32)[tile - 1:tile] * has_before)
            local = [_scan_local(xr, xi, tab_v, (7, 6, 4))
                     for xr, xi in _load_blocks(dr_ref, di_ref, base)]
            lam = [None] * SCAN_UNROLL
            for b in reversed(range(SCAN_UNROLL)):
                xr, xi = _scan_carry(*local[b], tab_v, cr, ci)
                lam[b] = (xr, xi)
                cr, ci = xr[0:1, :], xi[0:1, :]
                pr, pi = prev if b == 0 else (state[b - 1][0][SUBLANE - 1:], state[b - 1][1][SUBLANE - 1:])
                s_r = jnp.where(row0, pr, pltpu.roll(state[b][0], 1, 0))
                s_i = jnp.where(row0, pi, pltpu.roll(state[b][1], 1, 0))
                acr = acr + xr * s_r + xi * s_i
                aci = aci + xi * s_r - xr * s_i
            _store_blocks(lr_ref, li_ref, base, lam)
            return cr, ci, acr, aci

        zero = jnp.zeros((1, lb), F32)
        zacc = jnp.zeros((SUBLANE, lb), F32)
        _, _, acr, aci = lax.fori_loop(0, nblk // SCAN_UNROLL, step, (zero, zero, zacc, zacc))
        dar_ref[...] = acr
        dai_ref[...] = aci

    col = pl.BlockSpec((t_len, lb), lambda j: (0, j))
    small = pl.BlockSpec((SUBLANE, lb), lambda j: (0, j))
    return _call(
        body, "scan_bwd", (N_STATE // lb,),
        [col, col, col, col, pl.BlockSpec((8, SUBLANE, lb), lambda j: (0, 0, j))],
        [col, col, small, small],
        [jax.ShapeDtypeStruct((t_len, N_STATE), BF16)] * 2
        + [jax.ShapeDtypeStruct((SUBLANE, N_STATE), F32)] * 2,
        [dsr, dsi, str_, sti, tab_rev], jobs)


def _bwd_in(lam_r, lam_i, du_part, drest, x, dx2, g_mix, w_in, bre, bim, jobs=()):
    t_len = x.shape[0]
    cs = IN_COLS // N_CHIP

    def body(lr_ref, li_ref, du_ref, dr_ref, x_ref, dx2_ref, g_ref, w_ref, bre_ref, bim_ref,
             gx_ref, dp_ref, sm_ref):
        i = pl.program_id(0)
        du = du_ref[...] + jnp.concatenate(
            [_dot_nt(lr_ref[:, i * DIAG_N:(i + 1) * DIAG_N],
                     bre_ref[i * LANE:(i + 1) * LANE, i * DIAG_N:(i + 1) * DIAG_N])
             + _dot_nt(li_ref[:, i * DIAG_N:(i + 1) * DIAG_N],
                       bim_ref[i * LANE:(i + 1) * LANE, i * DIAG_N:(i + 1) * DIAG_N])
             for i in range(SSM_W // LANE)], axis=1)
        dp_ref[:, 0:SSM_W] = du.astype(BF16)
        dp_ref[:, SSM_W:] = dr_ref[...]
        dh = jnp.zeros((TB, D_MODEL), F32)
        for k in range(N_CHIP):
            dh = dh + _dot_nt(dp_ref[:, k * cs:(k + 1) * cs], w_ref[k])
        r, xh = _rms_stats(x_ref[...])
        gx_ref[...] = dx2_ref[...] + _rms_bwd(dh * g_ref[...], xh, r)
        upd = jnp.concatenate([jnp.sum(dh * xh, axis=0, keepdims=True),
                               jnp.zeros((SUBLANE - 1, D_MODEL), F32)], axis=0)

        @pl.when(i == 0)
        def _():
            sm_ref[...] = upd

        @pl.when(i > 0)
        def _():
            sm_ref[...] += upd

    return _call(
        body, "bwd_in", (t_len // TB,),
        [_rows(TB, N_STATE), _rows(TB, N_STATE), _rows(TB, SSM_W), _rows(TB, IN_COLS - SSM_W),
         _rows(TB, D_MODEL), _rows(TB, D_MODEL)] + [_whole()] * 4,
        [_rows(TB, D_MODEL), _rows(TB, IN_COLS), _acc(SUBLANE, D_MODEL)],
        [jax.ShapeDtypeStruct((t_len, D_MODEL), F32), jax.ShapeDtypeStruct((t_len, IN_COLS), BF16),
         jax.ShapeDtypeStruct((SUBLANE, D_MODEL), F32)],
        [lam_r, lam_i, du_part, drest, x, dx2, g_mix, w_in, bre, bim], jobs)


def _matmul_tn(a, b, name, out_shape, grid_ij, a_blk, a_map, b_blk, b_map, o_blk, o_map, jobs=()):
    tk = a_blk[0]
    nk = a.shape[0] // tk
    assert nk * tk == a.shape[0] and nk > 0

    def body(a_ref, b_ref, o_ref, acc_ref):
        k = pl.program_id(2)

        @pl.when(k == 0)
        def _():
            acc_ref[...] = jnp.zeros_like(acc_ref)

        acc_ref[...] += lax.dot_general(a_ref[...].astype(BF16), b_ref[...].astype(BF16),
                                        (((0,), (0,)), ((), ())), preferred_element_type=F32)

        @pl.when(k == nk - 1)
        def _():
            o_ref[...] = acc_ref[...]

    outs, per_job = _call(
        body, name, (grid_ij[0], grid_ij[1], nk),
        [pl.BlockSpec(a_blk, a_map), pl.BlockSpec(b_blk, b_map)], [pl.BlockSpec(o_blk, o_map)],
        [jax.ShapeDtypeStruct(out_shape, F32)], [a, b], jobs,
        scratch=[pltpu.VMEM((a_blk[1], b_blk[1]), F32)])
    return outs[0], per_job


def _dw_shards(a, b, name, tk, jobs=()):
    m, n = a.shape[1], b.shape[1]
    tn = n // N_CHIP
    tk = min(tk, a.shape[0])
    return _matmul_tn(a, b, name, (N_CHIP, m, tn), (1, N_CHIP),
                      (tk, m), lambda i, j, k: (k, 0), (tk, tn), lambda i, j, k: (k, j),
                      (None, m, tn), lambda i, j, k: (j, 0, 0), jobs)


def _dw_rows(a, b, name, tm, tk):
    m, n = a.shape[1], b.shape[1]
    tk = min(tk, a.shape[0])
    return _matmul_tn(a, b, name, (m, n), (m // tm, 1),
                      (tk, tm), lambda i, j, k: (k, i), (tk, n), lambda i, j, k: (k, 0),
                      (tm, n), lambda i, j, k: (i, 0))[0]


def _dw_cols(a, b, name, tn, sharded, jobs=()):
    t_len, m = a.shape
    n = b.shape[1]

    def body(a_ref, b_ref, o_ref):
        o_ref[...] = lax.dot_general(a_ref[...].astype(BF16), b_ref[...].astype(BF16),
                                     (((0,), (0,)), ((), ())), preferred_element_type=F32)

    if sharded:
        o_spec, o_shape = pl.BlockSpec((None, m, tn), lambda j: (j, 0, 0)), (n // tn, m, tn)
    else:
        o_spec, o_shape = pl.BlockSpec((m, tn), lambda j: (0, j)), (m, n)
    outs, per_job = _call(body, name, (n // tn,),
                          [_whole(), pl.BlockSpec((t_len, tn), lambda j: (0, j))], [o_spec],
                          [jax.ShapeDtypeStruct(o_shape, F32)], [a, b], jobs)
    return outs[0], per_job


def _dw_pair(a, m, b1, b2, name, jobs=()):
    t_len = a.shape[0]
    n_slab = DIAG_N // LANE
    rows_per_slab = LANE // n_slab

    def body(a_ref, b1_ref, b2_ref, o1_ref, o2_ref):
        for b_ref, o_ref in ((b1_ref, o1_ref), (b2_ref, o2_ref)):
            prod = lax.dot_general(a_ref[...].astype(BF16), b_ref[...].astype(BF16),
                                   (((0,), (0,)), ((), ())), preferred_element_type=F32)
            for j in range(n_slab):
                rows = slice(j * rows_per_slab, (j + 1) * rows_per_slab)
                o_ref[rows, :] = prod[rows, j * LANE:(j + 1) * LANE]

    tok = pl.BlockSpec((t_len, DIAG_N), lambda i: (0, i))
    out = pl.BlockSpec((LANE, LANE), lambda i: (i, 0))
    return _call(body, name, (m // LANE,),
                 [pl.BlockSpec((t_len, LANE), lambda i: (0, i)), tok, tok], [out, out],
                 [jax.ShapeDtypeStruct((m, LANE), F32)] * 2, [a, b1, b2], jobs)


def _prefetch_call(body, name, grid, scalars, in_specs, out_specs, out_shape, args):
    return pl.pallas_call(
        body, name=name,
        grid_spec=pltpu.PrefetchScalarGridSpec(num_scalar_prefetch=1, grid=grid, in_specs=in_specs,
                                               out_specs=out_specs),
        out_shape=out_shape, compiler_params=_params(len(grid)),
    )(scalars, *args)


def _place_shard(w, where, name, dtype, tr):
    rows, cols = w.shape

    def body(s_ref, w_ref, o_ref):
        o_ref[...] = w_ref[...].astype(dtype)

    return _prefetch_call(
        body, name, (rows // tr,), where,
        [pl.BlockSpec((tr, cols), lambda i, s: (i, 0))],
        pl.BlockSpec((None, tr, cols), lambda i, s: (s[0], i, 0)),
        jax.ShapeDtypeStruct((N_CHIP, rows, cols), dtype), [w])


def _place_shards(ws, where, name, dtype):
    n = len(ws)

    def body(s_ref, *refs):
        for t in range(n):
            refs[n + t][...] = refs[t][...].astype(dtype)

    return _prefetch_call(
        body, name, (1,), where,
        [pl.BlockSpec(w.shape, lambda i, s: (0, 0)) for w in ws],
        [pl.BlockSpec((None,) + w.shape, lambda i, s: (s[0], 0, 0)) for w in ws],
        [jax.ShapeDtypeStruct((N_CHIP,) + w.shape, dtype) for w in ws], ws)


def _add_sibling(gs, gots, where, name):
    n = len(gs)
    halves = [(g.shape[1] // 2, g.shape[2]) for g in gs]

    def body(s_ref, *refs):
        for t in range(n):
            refs[2 * n + t][...] = (refs[t][...] + refs[n + t][...]).astype(BF16)

    return _prefetch_call(
        body, name, (N_CHIP,), where,
        [pl.BlockSpec((None, hr, cs), lambda k, s: (k, s[1], 0)) for hr, cs in halves]
        + [pl.BlockSpec((None, hr, cs), lambda k, s: (k, 0, 0)) for hr, cs in halves],
        [pl.BlockSpec((None, hr, cs), lambda k, s: (k, 0, 0)) for hr, cs in halves],
        [jax.ShapeDtypeStruct((N_CHIP, hr, cs), BF16) for hr, cs in halves], list(gs) + list(gots))


def _add_chips(sums, gots, where, name):
    n = len(sums)
    halves = [s.shape[1:] for s in sums]

    def body(s_ref, *refs):
        for t in range(n):
            own_ref, got_ref = refs[t], refs[n + t]
            refs[2 * n + t][...] = ((own_ref[...].astype(F32) + got_ref[0].astype(F32))
                                    + got_ref[1].astype(F32)) + got_ref[2].astype(F32)

    return _prefetch_call(
        body, name, (1,), where,
        [pl.BlockSpec((None, hr, cs), lambda i, s: (s[0], 0, 0)) for hr, cs in halves]
        + [pl.BlockSpec((3, hr, cs), lambda i, s: (0, 0, 0)) for hr, cs in halves],
        [pl.BlockSpec((hr, cs), lambda i, s: (s[1], 0)) for hr, cs in halves],
        [jax.ShapeDtypeStruct((2 * hr, cs), F32) for hr, cs in halves], list(sums) + list(gots))


def _small_allreduce(pack):
    rows = pack.shape[0]
    half = rows // 2

    def body(in_ref, out_ref, sib_ref, slots_ref, s_a, r_a, s_b, r_b, s_c, r_c):
        x, y, c, chips = _place()
        k_me = 2 * x + y
        sib = (x, y, 1 - c)
        first = _remote(in_ref, sib_ref, s_a, r_a, sib)
        first.start()
        first.wait()
        mine = _half(rows, c)
        slots_ref[k_me] = in_ref[mine, :] + sib_ref[mine, :]
        cps = [_remote(slots_ref.at[k_me], slots_ref.at[k_me], s_b.at[j], r_b.at[j], (*ch, c))
               for j, ch in enumerate(chips)]
        for cp in cps:
            cp.start()
        for j, ch in enumerate(chips):
            slot = slots_ref.at[_chip_index(ch)]
            _remote(slot, slot, s_b.at[j], r_b.at[j], (*ch, c)).wait_recv()
        for cp in cps:
            cp.wait_send()
        out_ref[mine, :] = ((slots_ref[0] + slots_ref[1]) + slots_ref[2]) + slots_ref[3]
        last = _remote(out_ref.at[mine, :], out_ref.at[mine, :], s_c, r_c, sib)
        last.start()
        theirs = out_ref.at[_half(rows, 1 - c), :]
        _remote(theirs, theirs, s_c, r_c, sib).wait_recv()
        last.wait_send()

    return pl.pallas_call(
        body, name="small_allreduce", in_specs=[_whole()], out_specs=_whole(),
        out_shape=jax.ShapeDtypeStruct(pack.shape, F32),
        scratch_shapes=[pltpu.VMEM(pack.shape, F32), pltpu.VMEM((N_CHIP, half, LANE), F32),
                        pltpu.SemaphoreType.DMA, pltpu.SemaphoreType.DMA,
                        pltpu.SemaphoreType.DMA((3,)), pltpu.SemaphoreType.DMA((3,)),
                        pltpu.SemaphoreType.DMA, pltpu.SemaphoreType.DMA],
        compiler_params=_params(0),
    )(pack)


def _adamw_update(w_ref, g_ref, m_ref, v_ref, d_ref, mo_ref, vo_ref):
    gv = g_ref[...]
    mn = ADAM_B1 * m_ref[...] + (1.0 - ADAM_B1) * gv
    vn = ADAM_B2 * v_ref[...] + (1.0 - ADAM_B2) * (gv * gv)
    mo_ref[...] = mn
    vo_ref[...] = vn
    m_hat = mn / (1.0 - ADAM_B1 ** ADAM_STEP)
    v_hat = vn / (1.0 - ADAM_B2 ** ADAM_STEP)
    d_ref[...] = -ADAM_LR * (m_hat / (jnp.sqrt(v_hat) + ADAM_EPS) + ADAM_WD * w_ref[...])


def _adamw(w, g, m, v, name, tr):
    rows, cols = w.shape
    blk = _rows(tr, cols)

    def body(w_ref, g_ref, m_ref, v_ref, go_ref, d_ref, mo_ref, vo_ref):
        go_ref[...] = g_ref[...]
        _adamw_update(w_ref, g_ref, m_ref, v_ref, d_ref, mo_ref, vo_ref)

    return _call(body, name, (rows // tr,), [blk] * 4, [blk] * 4,
                 [jax.ShapeDtypeStruct(w.shape, F32)] * 4, [w, g, m, v])[0]


def _adamw_many(ws, gs, ms, vs, name):
    n = len(ws)

    def body(*refs):
        for t in range(n):
            _adamw_update(*[refs[q * n + t] for q in range(7)])

    specs = [pl.BlockSpec(a.shape, lambda i, nd=a.ndim: (0,) * nd) for a in ws]
    outs = pl.pallas_call(
        body, name=name, grid=(1,), in_specs=specs * 4, out_specs=specs * 3,
        out_shape=[jax.ShapeDtypeStruct(a.shape, F32) for _ in range(3) for a in ws],
        compiler_params=_params(1),
    )(*ws, *gs, *ms, *vs)
    return outs[:n], outs[n:2 * n], outs[2 * n:]


def _ssm_discretize(a_re, a_im, log_dt, b_re, b_im):
    dt = jnp.exp(log_dt)[:, None]
    mag = jnp.exp(dt * a_re)
    abr = mag * jnp.cos(dt * a_im)
    abi = mag * jnp.sin(dt * a_im)
    den = a_re * a_re + a_im * a_im
    nr = abr - 1.0
    ni = abi
    f_re = (nr * a_re + ni * a_im) / den
    f_im = (ni * a_re - nr * a_im) / den
    bbr = f_re[..., None] * b_re - f_im[..., None] * b_im
    bbi = f_re[..., None] * b_im + f_im[..., None] * b_re
    return abr, abi, bbr, bbi


def _scan_tables(abr, abi):
    ar = abr.reshape(1, N_STATE)
    ai = abi.reshape(1, N_STATE)
    pr, pi = [ar], [ai]
    for _ in range(SUBLANE - 1):
        pr, pi = pr + [pr[-1] * ar - pi[-1] * ai], pi + [pr[-1] * ai + pi[-1] * ar]
    row = jnp.arange(SUBLANE)[:, None]
    tabs = []
    for d in (1, 2, 4):
        tabs.append(jnp.where(row >= d, pr[d - 1], 0.0))
        tabs.append(jnp.where(row >= d, pi[d - 1], 0.0))
    tabs.append(jnp.concatenate(pr, axis=0))
    tabs.append(jnp.concatenate(pi, axis=0))
    fwd = jnp.stack(tabs)
    sign = jnp.array([1.0, -1.0] * 4, F32)[:, None, None]
    return fwd, fwd[:, ::-1, :] * sign


def _block_diag_b(bb):
    strip = bb.transpose(2, 0, 1).reshape(SSM_H, N_STATE)
    rows = lax.broadcasted_iota(jnp.int32, (SSM_W, N_STATE), 0) // SSM_H
    cols = lax.broadcasted_iota(jnp.int32, (SSM_W, N_STATE), 1) // SSM_P
    return jnp.where(rows == cols, jnp.tile(strip, (SSM_G, 1)), 0.0).astype(BF16)


def _block_diag_c(cc):
    strip = cc.transpose(0, 2, 1).reshape(N_STATE, SSM_H)
    rows = lax.broadcasted_iota(jnp.int32, (N_STATE, SSM_W), 0) // SSM_P
    cols = lax.broadcasted_iota(jnp.int32, (N_STATE, SSM_W), 1) // SSM_H
    return jnp.where(rows == cols, jnp.tile(strip, (1, SSM_G)), 0.0).astype(BF16)


SMALL_SHAPES = {
    "g_mix": (D_MODEL,), "a_re": (SSM_G, SSM_P), "a_im": (SSM_G, SSM_P), "log_dt": (SSM_G,),
    "b_re": (SSM_G, SSM_P, SSM_H), "b_im": (SSM_G, SSM_P, SSM_H),
    "c_re": (SSM_G, SSM_H, SSM_P), "c_im": (SSM_G, SSM_H, SSM_P),
    "d_skip": (SSM_W,), "b_glu": (SSM_W,), "g_sgu": (SGU_W,), "w_s": (SGU_G, CHUNK, CHUNK),
    "b_s": (SGU_G, CHUNK), "g_ffn": (D_MODEL,), "conv_b": (2 * D_FF,), "g_final": (D_MODEL,),
}
PACK_ITEMS = [("loss", (1,))] + [(n, SMALL_SHAPES[n]) for n in SMALL] + [("conv_w", (3, 2 * D_FF))]
TILE = SUBLANE * LANE


def _item_rows(shape):
    return -(-math.prod(shape) // TILE) * SUBLANE


PACK_ROWS = -(-sum(_item_rows(s) for _, s in PACK_ITEMS) // (2 * SUBLANE)) * (2 * SUBLANE)


def _pack(values):
    parts, used = [], 0
    for name, shape in PACK_ITEMS:
        size, rows = math.prod(shape), _item_rows(shape)
        if name in values:
            flat = values[name].astype(F32).reshape(size)
            if rows * LANE > size:
                flat = jnp.pad(flat, (0, rows * LANE - size))
            parts.append(flat.reshape(rows, LANE))
        else:
            parts.append(jnp.zeros((rows, LANE), F32))
        used += rows
    if PACK_ROWS > used:
        parts.append(jnp.zeros((PACK_ROWS - used, LANE), F32))
    return jnp.concatenate(parts, axis=0)


def _unpack(pack):
    out, off = {}, 0
    for name, shape in PACK_ITEMS:
        rows = _item_rows(shape)
        out[name] = pack[off:off + rows].reshape(rows * LANE)[:math.prod(shape)].reshape(shape)
        off += rows
    return out


PLACE_ROWS = {"w_in": 256, "w_up": 256, "w_down": 352, "w_out": 256, "w_proj_a": 256,
              "w_proj_b": 256, "w_glu": 128}


def kernel(x, g_mix, w_in, a_re, a_im, log_dt, b_re, b_im, c_re, c_im, d_skip, w_glu, b_glu, w_proj_a, g_sgu, w_s, b_s, w_proj_b, w_out, g_ffn, w_up, conv_w, conv_b, w_down, g_final, loss_target, m_g_mix, m_w_in, m_a_re, m_a_im, m_log_dt, m_b_re, m_b_im, m_c_re, m_c_im, m_d_skip, m_w_glu, m_b_glu, m_w_proj_a, m_g_sgu, m_w_s, m_b_s, m_w_proj_b, m_w_out, m_g_ffn, m_w_up, m_conv_w, m_conv_b, m_w_down, m_g_final, v_g_mix, v_w_in, v_a_re, v_a_im, v_log_dt, v_b_re, v_b_im, v_c_re, v_c_im, v_d_skip, v_w_glu, v_b_glu, v_w_proj_a, v_g_sgu, v_w_s, v_b_s, v_w_proj_b, v_w_out, v_g_ffn, v_w_up, v_conv_w, v_conv_b, v_w_down, v_g_final):
    given = dict(locals())
    w = {n: given[n] for n in WEIGHTS}
    m = {n: given["m_" + n] for n in WEIGHTS}
    v = {n: given["v_" + n] for n in WEIGHTS}

    def shard2d(a):
        return a.reshape(a.shape[-2], a.shape[-1])

    chip = 2 * lax.axis_index("x") + lax.axis_index("y")
    where = jnp.stack([chip, lax.axis_index("c")]).astype(jnp.int32)
    xs, target = x[0], loss_target[0]
    small = {n: w[n].reshape(SMALL_SHAPES[n]) for n in SMALL}

    (abr, abi, bbr, bbi), disc_vjp = jax.vjp(_ssm_discretize, small["a_re"], small["a_im"],
                                             small["log_dt"], small["b_re"], small["b_im"])
    tab_f, tab_r = _scan_tables(abr, abi)
    bre = _block_diag_b(bbr)
    bim = _block_diag_b(bbi)
    cre = _block_diag_c(small["c_re"])
    cim = _block_diag_c(small["c_im"])
    tril = jnp.tril(jnp.ones((CHUNK, CHUNK), dtype=bool))
    ws = jnp.where(tril[None], small["w_s"], 0.0)
    ws_st = ws.reshape(SGU_G // 2, 2 * CHUNK, CHUNK).astype(BF16)
    wst_st = ws.transpose(0, 2, 1).reshape(SGU_G // 2, 2 * CHUNK, CHUNK).astype(BF16)
    bmat = jnp.repeat(small["b_s"].T, SGU_D, axis=1)
    g_mix2 = small["g_mix"].reshape(1, D_MODEL)
    g_ffn2 = small["g_ffn"].reshape(1, D_MODEL)
    g_final2 = small["g_final"].reshape(1, D_MODEL)
    g_sgu2 = small["g_sgu"].reshape(1, SGU_W)
    d_skip2 = small["d_skip"].reshape(1, SSM_W)
    b_glu2 = small["b_glu"].reshape(1, SSM_W)
    conv_b2 = small["conv_b"].reshape(1, 2 * D_FF)

    gat = {"w_in": _place_shard(shard2d(w["w_in"]), where, "place_w_in", BF16, PLACE_ROWS["w_in"])}
    gat.update(zip(BIG[1:], _place_shards([shard2d(w[n]) for n in BIG[1:]], where, "place_rest", BF16)))
    gat["conv_w"] = _place_shard(shard2d(w["conv_w"]), where, "place_conv_w", F32, 3)
    all_rows = (0, D_MODEL)
    (gat["w_in"],), = _comm("gather_in", [_job_gather(
        [gat["w_in"]], [(0, all_rows, ICI, (0.0, 0.5)), (0, all_rows, SIBLING, (0.5, 1.0))])])
    mixers = ["w_glu", "w_proj_a", "w_proj_b", "w_out"]
    rows = {n: (0, gat[n].shape[1]) for n in mixers}
    down_a, down_b = (0, D_FF // 8), (D_FF // 8, D_FF // 8)
    up_a, up_b = (0, 3 * D_MODEL // 8), (3 * D_MODEL // 8, 5 * D_MODEL // 8)
    span = (0.0, 1.0)

    names = mixers + ["conv_w", "w_down"]
    (p, h1, bur, bui), (got,) = _fwd_in(
        xs, g_mix2, gat["w_in"], bre, bim,
        [_job_gather([gat[n] for n in names],
                     [(i, rows[n], ICI, span) for i, n in enumerate(mixers)]
                     + [(4, None, ICI, span), (5, down_a, ICI, span)])])
    gat.update(zip(names, got))
    names = mixers + ["w_down", "w_up"]
    (str_, sti), (got,) = _scan_fwd(
        bur, bui, tab_f,
        [_job_gather([gat[n] for n in names],
                     [(i, rows[n], SIBLING, span) for i, n in enumerate(mixers)]
                     + [(4, down_a, SIBLING, span), (4, down_b, ICI, span), (5, up_a, ICI, span)])])
    gat.update(zip(names, got))
    w_glu_f = gat["w_glu"].reshape(SSM_W, SSM_W)
    w_out_f = gat["w_out"].reshape(D_MODEL, D_MODEL)
    conv_w_f = gat["conv_w"].transpose(1, 0, 2).reshape(3, 2 * D_FF)
    (x2, y0, z, mixed, ya, yb), ((gat["w_down"], gat["w_up"]),) = _fwd_mix(
        xs, p, str_, sti, cre, cim, d_skip2, w_glu_f, b_glu2, gat["w_proj_a"], g_sgu2, ws_st, bmat,
        gat["w_proj_b"], w_out_f,
        [_job_gather([gat["w_down"], gat["w_up"]],
                     [(0, down_b, SIBLING, span), (1, up_a, SIBLING, span),
                      (1, up_b, ICI, (0.0, 0.9375)), (1, up_b, SIBLING, (0.9375, 1.0))])])
    w_down_f = gat["w_down"].reshape(D_FF, D_MODEL)
    up, act, f, h2, dx3, sm_ffn = _fwd_ffn(x2, target, g_ffn2, gat["w_up"], conv_w_f, conv_b2,
                                           w_down_f, g_final2)

    def leg1_done(names, got):
        return _add_sibling([part[n] for n in names], got, where, "add_sibling_" + names[0])

    def leg2_done(names, sums, got):
        return _add_chips(sums, got, where, "add_chips_" + names[0])

    part, red = {}, {}
    part["w_down"] = _dw_rows(f, dx3, "dw_down", D_FF // 2, 2 * TK).reshape(
        N_CHIP, D_FF // N_CHIP, D_MODEL)
    (dx2, dup, sm_conv, sm_gffn), (got,) = _bwd_ffn(
        dx3, up, act, x2, g_ffn2, gat["w_up"], conv_w_f, w_down_f,
        [_job_sibling_halves([part["w_down"]])])
    sum_down = leg1_done(["w_down"], got)
    part["w_up"], (got,) = _dw_shards(h2, dup, "dw_up", 4 * TK, [_job_to_owner(sum_down)])
    red_down = leg2_done(["w_down"], sum_down, got)
    ((dsr, dsi, du_part, drest, mrg, dya, dyb, yap, dz, y1, sgu, dy0, sm_mix, dbm, dws),
     (got, (red["w_down"],))) = _bwd_mix(
        dx2, p, y0, z, mixed, ya, yb, w_out_f, gat["w_proj_a"], gat["w_proj_b"], w_glu_f, cre, cim,
        ws_st, wst_st, d_skip2, g_sgu2,
        [_job_sibling_halves([part["w_up"]]), _job_swap_halves(red_down)])
    sum_up = leg1_done(["w_up"], got)
    (lam_r, lam_i, dar8, dai8), (got,) = _scan_bwd(dsr, dsi, str_, sti, tab_r, [_job_to_owner(sum_up)])
    red_up = leg2_done(["w_up"], sum_up, got)
    mix4 = ["w_out", "w_proj_a", "w_proj_b", "w_glu"]
    part["w_out"] = _dw_cols(mrg, dx2, "dw_out", D_MODEL // 2, False)[0].reshape(
        N_CHIP, D_MODEL // N_CHIP, D_MODEL)
    part["w_proj_a"] = _dw_cols(yap, dya, "dw_proj_a", D_MODEL // N_CHIP, True)[0]
    part["w_proj_b"] = _dw_cols(sgu, dyb, "dw_proj_b", D_MODEL // N_CHIP, True)[0]
    part["w_glu"] = _dw_cols(y1, dz, "dw_glu", SSM_W, False)[0].reshape(
        N_CHIP, SSM_W // N_CHIP, SSM_W)
    (grad_x, dp, sm_gmix), (got, (red["w_up"],)) = _bwd_in(
        lam_r, lam_i, du_part, drest, xs, dx2, g_mix2, gat["w_in"], bre, bim,
        [_job_sibling_halves([part[n] for n in mix4]), _job_swap_halves(red_up)])
    sums_m = leg1_done(mix4, got)
    part["w_in"], (got,) = _dw_cols(h1, dp, "dw_in", IN_COLS // N_CHIP, True, [_job_to_owner(sums_m)])
    red_m = leg2_done(mix4, sums_m, got)
    (dbd_r, dbd_i), (got, done_m) = _dw_pair(
        p, SSM_W, lam_r, lam_i, "db_bar",
        [_job_sibling_halves([part["w_in"]]), _job_swap_halves(red_m)])
    red.update(zip(mix4, done_m))
    sum_in = leg1_done(["w_in"], got)
    (dcd_r, dcd_i), (got,) = _dw_pair(dy0, SSM_W, str_, sti, "dc", [_job_to_owner(sum_in)])
    red_in = leg2_done(["w_in"], sum_in, got)
    (red["w_in"],), = _comm("swap_w_in", [_job_swap_halves(red_in)])

    def pick_c(slabs):
        two = LANE // SSM_P
        return jnp.einsum("jshsp->jshp", slabs.reshape(SSM_G // two, two, SSM_H, two, SSM_P)
                          ).reshape(SSM_G, SSM_H, SSM_P)

    def pick_b(slabs):
        return pick_c(slabs).transpose(0, 2, 1)

    dabr = jnp.sum(dar8, axis=0).reshape(SSM_G, SSM_P)
    dabi = jnp.sum(dai8, axis=0).reshape(SSM_G, SSM_P)
    d_a_re, d_a_im, d_log_dt, d_b_re, d_b_im = disc_vjp((dabr, dabi, pick_b(dbd_r), pick_b(dbd_i)))
    gsmall = {
        "g_mix": sm_gmix[0], "a_re": d_a_re, "a_im": d_a_im, "log_dt": d_log_dt,
        "b_re": d_b_re, "b_im": d_b_im, "c_re": pick_c(dcd_r), "c_im": -pick_c(dcd_i),
        "d_skip": sm_mix[0], "b_glu": sm_mix[1], "g_sgu": sm_mix[2],
        "w_s": jnp.where(tril[None], dws, 0.0),
        "b_s": dbm.reshape(CHUNK, SGU_G, SGU_D).sum(-1).T,
        "g_ffn": sm_gffn[0], "conv_b": sm_conv[3], "g_final": sm_ffn[0],
        "conv_w": sm_conv[0:3], "loss": sm_ffn[1, 0:1],
    }

    total_pack = _small_allreduce(_pack(gsmall))
    total = _unpack(total_pack)
    grads = dict(red)
    cs = 2 * D_FF // N_CHIP
    grads["conv_w"] = lax.dynamic_slice(total["conv_w"], (0, chip * cs), (3, cs))
    delta, new_m, new_v = {}, {}, {}
    for n in BIG + ("conv_w",):
        grads[n], delta[n], new_m[n], new_v[n] = _adamw(
            shard2d(w[n]), grads[n], shard2d(m[n]), shard2d(v[n]), "adamw_" + n, PLACE_ROWS.get(n, 3))
    for n in SMALL:
        grads[n] = total[n].reshape(w[n].shape)
    ud, um, uv = _adamw_many(*[[d[n] for n in SMALL] for d in (w, grads, m, v)], "adamw_small")
    for i, n in enumerate(SMALL):
        delta[n], new_m[n], new_v[n] = ud[i], um[i], uv[i]

    def like(d):
        return [d[n].reshape(w[n].shape) for n in WEIGHTS]

    return (total["loss"].reshape(()), grad_x.reshape(x.shape), *like(grads), *like(delta),
            *like(new_m), *like(new_v))
```

```python
import math

import jax
import jax.numpy as jnp
from jax import lax
from jax.experimental import pallas as pl
from jax.experimental.pallas import tpu as pltpu

F32 = jnp.float32
BF16 = jnp.bfloat16
MESH = pl.DeviceIdType.MESH

D_MODEL = 1024
SSM_W = 512
SSM_G = 32
SSM_H = 16
SSM_P = 64
N_STATE = SSM_G * SSM_P
DIAG_N = 128 * SSM_P // SSM_H
SGU_W = 512
SGU_G = 8
SGU_D = 64
CHUNK = 128
D_FF = 2816
IN_COLS = 3584
EPS = 1e-6
N_CHIP = 4

ADAM_LR = 0.001
ADAM_B1 = 0.9
ADAM_B2 = 0.999
ADAM_EPS = 1e-08
ADAM_WD = 0.01
ADAM_STEP = 10

SUBLANE = 8
LANE = 128
VMEM_LIMIT = 56 * 1024 * 1024
TB = 256
TK = 512
SCAN_LANES = 256
SCAN_UNROLL = 4
HALO = SUBLANE

BIG = ("w_in", "w_up", "w_down", "w_out", "w_proj_a", "w_proj_b", "w_glu")
SMALL = ("g_mix", "a_re", "a_im", "log_dt", "b_re", "b_im", "c_re", "c_im", "d_skip", "b_glu",
         "g_sgu", "w_s", "b_s", "g_ffn", "conv_b", "g_final")
WEIGHTS = ("g_mix", "w_in", "a_re", "a_im", "log_dt", "b_re", "b_im", "c_re", "c_im", "d_skip",
           "w_glu", "b_glu", "w_proj_a", "g_sgu", "w_s", "b_s", "w_proj_b", "w_out", "g_ffn",
           "w_up", "conv_w", "conv_b", "w_down", "g_final")

ANY = pl.BlockSpec(memory_space=pl.ANY)


def _params(n_grid):
    return pltpu.CompilerParams(dimension_semantics=("arbitrary",) * n_grid if n_grid else None,
                                vmem_limit_bytes=VMEM_LIMIT)


def _whole():
    return pl.BlockSpec(memory_space=pltpu.VMEM)


def _rows(tb, ncol):
    return pl.BlockSpec((tb, ncol), lambda i: (i, 0))


def _acc(nrow, ncol):
    return pl.BlockSpec((nrow, ncol), lambda i: (0, 0))


def _dot(a, b):
    return jnp.dot(a.astype(BF16), b.astype(BF16), preferred_element_type=F32)


def _dot_nt(a, b):
    return lax.dot_general(a.astype(BF16), b.astype(BF16), (((1,), (1,)), ((), ())),
                           preferred_element_type=F32)


def _sigmoid(v):
    return 0.5 * jnp.tanh(0.5 * v) + 0.5


_GELU_C = math.sqrt(2.0 / math.pi)


def _gelu(v):
    return 0.5 * v * (1.0 + jnp.tanh(_GELU_C * (v + 0.044715 * v * v * v)))


def _gelu_and_grad(v):
    v2 = v * v
    t = jnp.tanh(_GELU_C * v * (1.0 + 0.044715 * v2))
    half = 0.5 * (1.0 + t)
    return v * half, half + 0.5 * v * (1.0 - t * t) * _GELU_C * (1.0 + 3.0 * 0.044715 * v2)


def _rms_stats(v):
    r = lax.rsqrt(jnp.mean(v * v, axis=-1, keepdims=True) + EPS)
    return r, v * r


def _rms_bwd(dxh, xh, r):
    return r * (dxh - xh * jnp.mean(dxh * xh, axis=-1, keepdims=True))


def _place():
    x, y, c = lax.axis_index("x"), lax.axis_index("y"), lax.axis_index("c")
    chips = [(1 - x, y), (x, 1 - y), (1 - x, 1 - y)]
    return x, y, c, chips


def _chip_index(chip):
    return 2 * chip[0] + chip[1]


def _remote(src, dst, send_sem, recv_sem, device):
    return pltpu.make_async_remote_copy(src_ref=src, dst_ref=dst, send_sem=send_sem,
                                        recv_sem=recv_sem, device_id=device, device_id_type=MESH)


def _half(ref_rows, c):
    hr = ref_rows // 2
    return pl.ds(pl.multiple_of(c * hr, SUBLANE), hr)


class _Job:
    def __init__(self, hooks, n_sem, ins=(), inouts=(), outs=()):
        self.hooks, self.n_sem = list(hooks), n_sem
        self.ins, self.inouts, self.outs = list(ins), list(inouts), list(outs)


def _whole_span(start, finish):
    return [(0.0, "start", start), (1.0, "finish", finish)]


ICI, SIBLING = "ici", "sibling"


def _job_gather(bufs, legs):
    def copies(io, leg, first):
        b, window, kind, _ = legs[leg]
        x, y, c, chips = _place()
        k_me = 2 * x + y
        out = []
        for j, ch in enumerate(chips):
            k = _chip_index(ch)
            if window is None:
                src, land, dev = io[b].at[k_me], io[b].at[k], (*ch, c)
            else:
                r0, rows = window
                mine = pl.ds(pl.multiple_of(r0 + c * (rows // 2), SUBLANE), rows // 2)
                theirs = pl.ds(pl.multiple_of(r0 + (1 - c) * (rows // 2), SUBLANE), rows // 2)
                if kind == ICI:
                    src, land, dev = io[b].at[k_me, mine, :], io[b].at[k, mine, :], (*ch, c)
                else:
                    src, land, dev = io[b].at[k, mine, :], io[b].at[k, theirs, :], (x, y, 1 - c)
            out.append((src, land, first + j, dev))
        return out

    def starter(leg):
        def start(ins, io, outs, ssem, rsem):
            for src, _, i, dev in copies(io, leg, 3 * leg):
                _remote(src, src, ssem(i), rsem(i), dev).start()
        return start

    def finisher(leg):
        def finish(ins, io, outs, ssem, rsem):
            cps = copies(io, leg, 3 * leg)
            for _, land, i, dev in cps:
                _remote(land, land, ssem(i), rsem(i), dev).wait_recv()
            for src, _, i, dev in cps:
                _remote(src, src, ssem(i), rsem(i), dev).wait_send()
        return finish

    hooks = []
    for leg, (_, _, _, (begin, end)) in enumerate(legs):
        hooks += [(begin, "start", starter(leg)), (end, "finish", finisher(leg))]
    return _Job(hooks, 3 * len(legs), inouts=bufs)


def _job_sibling_halves(grads):
    n = len(grads)

    def build(ins, outs, ssem, rsem):
        x, y, c, _ = _place()
        return [_remote(ins[t].at[:, _half(grads[t].shape[1], 1 - c), :], outs[t], ssem(t), rsem(t),
                        (x, y, 1 - c)) for t in range(n)]

    def start(ins, io, outs, ssem, rsem):
        for cp in build(ins, outs, ssem, rsem):
            cp.start()

    def finish(ins, io, outs, ssem, rsem):
        for cp in build(ins, outs, ssem, rsem):
            cp.wait()

    return _Job(_whole_span(start, finish), n, ins=grads,
                outs=[jax.ShapeDtypeStruct((N_CHIP, g.shape[1] // 2, g.shape[2]), F32) for g in grads])


def _job_to_owner(sums):
    n = len(sums)

    def build(ins, outs, ssem, rsem):
        x, y, c, chips = _place()
        return [_remote(ins[t].at[_chip_index(ch)], outs[t].at[j], ssem(3 * t + j), rsem(3 * t + j),
                        (*ch, c)) for t in range(n) for j, ch in enumerate(chips)]

    def start(ins, io, outs, ssem, rsem):
        for cp in build(ins, outs, ssem, rsem):
            cp.start()

    def finish(ins, io, outs, ssem, rsem):
        for cp in build(ins, outs, ssem, rsem):
            cp.wait()

    return _Job(_whole_span(start, finish), 3 * n, ins=sums,
                outs=[jax.ShapeDtypeStruct((3,) + s.shape[1:], s.dtype) for s in sums])


def _job_swap_halves(bufs):
    n = len(bufs)

    def start(ins, io, outs, ssem, rsem):
        x, y, c, _ = _place()
        for t in range(n):
            mine = io[t].at[_half(bufs[t].shape[0], c), :]
            _remote(mine, mine, ssem(t), rsem(t), (x, y, 1 - c)).start()

    def finish(ins, io, outs, ssem, rsem):
        x, y, c, _ = _place()
        for t in range(n):
            theirs = io[t].at[_half(bufs[t].shape[0], 1 - c), :]
            _remote(theirs, theirs, ssem(t), rsem(t), (x, y, 1 - c)).wait_recv()
        for t in range(n):
            mine = io[t].at[_half(bufs[t].shape[0], c), :]
            _remote(mine, mine, ssem(t), rsem(t), (x, y, 1 - c)).wait_send()

    return _Job(_whole_span(start, finish), n, inouts=bufs)


def _call(body, name, grid, in_specs, out_specs, out_shape, args, jobs=(), scratch=()):
    n_in, n_out, n_scr = len(args), len(out_shape), len(scratch)
    job_in = [a for jb in jobs for a in jb.ins + jb.inouts]
    job_out = [s for jb in jobs
               for s in [jax.ShapeDtypeStruct(a.shape, a.dtype) for a in jb.inouts] + jb.outs]
    aliases, pos_in, pos_out = {}, n_in, n_out
    for jb in jobs:
        pos_in += len(jb.ins)
        for _ in jb.inouts:
            aliases[pos_in] = pos_out
            pos_in += 1
            pos_out += 1
        pos_out += len(jb.outs)
    n_sem = sum(jb.n_sem for jb in jobs)

    def wrapped(*refs):
        c_in = refs[:n_in]
        j_in = refs[n_in:n_in + len(job_in)]
        c_out = refs[n_in + len(job_in):n_in + len(job_in) + n_out]
        j_out = refs[n_in + len(job_in) + n_out:n_in + len(job_in) + n_out + len(job_out)]
        rest = refs[n_in + len(job_in) + n_out + len(job_out):]
        c_scr = rest[:n_scr]
        views, pi, po, ps = [], 0, 0, 0
        for jb in jobs:
            ins = j_in[pi:pi + len(jb.ins)]
            pi += len(jb.ins) + len(jb.inouts)
            io = j_out[po:po + len(jb.inouts)]
            new = j_out[po + len(jb.inouts):po + len(jb.inouts) + len(jb.outs)]
            po += len(jb.inouts) + len(jb.outs)
            send = (lambda i, o=ps: rest[n_scr].at[o + i])
            recv = (lambda i, o=ps: rest[n_scr + 1].at[o + i])
            ps += jb.n_sem
            views.append((ins, io, new, send, recv))

        def run(frac):
            for kind in ("finish", "start"):
                for jb, vw in zip(jobs, views):
                    for at, what, fn in jb.hooks:
                        if at == frac and what == kind:
                            fn(*vw)

        fracs = sorted({at for jb in jobs for at, _, _ in jb.hooks})
        if not grid:
            for frac in fracs:
                run(frac)
            return
        if jobs:
            assert len(grid) == 1 or set(fracs) <= {0.0, 1.0}
            first = pl.program_id(0) == 0
            last = pl.program_id(0) == grid[0] - 1
            for d in range(1, len(grid)):
                first = jnp.logical_and(first, pl.program_id(d) == 0)
                last = jnp.logical_and(last, pl.program_id(d) == grid[d] - 1)
            for frac in fracs:
                if frac < 1.0:
                    at_step = first if frac == 0.0 else pl.program_id(0) == int(frac * grid[0])
                    pl.when(at_step)(lambda frac=frac: run(frac))
        body(*c_in, *c_out, *c_scr)
        if jobs and 1.0 in fracs:
            pl.when(last)(lambda: run(1.0))

    sems = [pltpu.SemaphoreType.DMA((n_sem,)), pltpu.SemaphoreType.DMA((n_sem,))] if jobs else []
    kwargs = dict(grid=grid) if grid else {}
    res = pl.pallas_call(
        wrapped, name=name, in_specs=list(in_specs) + [ANY] * len(job_in),
        out_specs=list(out_specs) + [ANY] * len(job_out),
        out_shape=list(out_shape) + job_out, scratch_shapes=list(scratch) + sems,
        input_output_aliases=aliases, compiler_params=_params(len(grid)), **kwargs,
    )(*args, *job_in)
    outs, pos, per_job = list(res[:n_out]), n_out, []
    for jb in jobs:
        k = len(jb.inouts) + len(jb.outs)
        per_job.append(list(res[pos:pos + k]))
        pos += k
    return outs, per_job


def _comm(name, jobs):
    return _call(None, name, (), [], [], [], [], jobs)[1]


def _fwd_in(x, g_mix, w_in, bre, bim, jobs=()):
    t_len = x.shape[0]
    cs = IN_COLS // N_CHIP

    def body(x_ref, g_ref, w_ref, bre_ref, bim_ref, p_ref, h_ref, bur_ref, bui_ref):
        xv = x_ref[...]
        r, xh = _rms_stats(xv)
        h = (xh * g_ref[...]).astype(BF16)
        h_ref[...] = h
        for k in range(N_CHIP):
            p_ref[:, k * cs:(k + 1) * cs] = jnp.dot(h, w_ref[k],
                                                    preferred_element_type=F32).astype(BF16)
        u = p_ref[:, 0:SSM_W]
        for i in range(SSM_W // LANE):
            rows, cols = slice(i * LANE, (i + 1) * LANE), slice(i * DIAG_N, (i + 1) * DIAG_N)
            bur_ref[:, cols] = jnp.dot(u[:, rows], bre_ref[rows, cols],
                                       preferred_element_type=F32).astype(BF16)
            bui_ref[:, cols] = jnp.dot(u[:, rows], bim_ref[rows, cols],
                                       preferred_element_type=F32).astype(BF16)

    return _call(
        body, "fwd_in", (t_len // TB,),
        [_rows(TB, D_MODEL), _whole(), _whole(), _whole(), _whole()],
        [_rows(TB, IN_COLS), _rows(TB, D_MODEL), _rows(TB, N_STATE), _rows(TB, N_STATE)],
        [jax.ShapeDtypeStruct((t_len, IN_COLS), BF16), jax.ShapeDtypeStruct((t_len, D_MODEL), BF16),
         jax.ShapeDtypeStruct((t_len, N_STATE), BF16), jax.ShapeDtypeStruct((t_len, N_STATE), BF16)],
        [x, g_mix, w_in, bre, bim], jobs)


def _scan_local(xr, xi, tab, shifts):
    for q, s in enumerate(shifts):
        ar, ai = tab[2 * q], tab[2 * q + 1]
        rr = pltpu.roll(xr, s, 0)
        ri = pltpu.roll(xi, s, 0)
        xr, xi = xr + ar * rr - ai * ri, xi + ar * ri + ai * rr
    return xr, xi


def _scan_carry(xr, xi, tab, cr, ci):
    pr, pi = tab[6], tab[7]
    return xr + pr * cr - pi * ci, xi + pr * ci + pi * cr


BF16_TILE = 2 * SUBLANE


def _load_blocks(r_ref, i_ref, base):
    out = []
    for q in range(SCAN_UNROLL // 2):
        rows = pl.ds(pl.multiple_of(base + q * BF16_TILE, BF16_TILE), BF16_TILE)
        vr, vi = r_ref[rows, :].astype(F32), i_ref[rows, :].astype(F32)
        out += [(vr[:SUBLANE], vi[:SUBLANE]), (vr[SUBLANE:], vi[SUBLANE:])]
    return out


def _store_blocks(r_ref, i_ref, base, blocks):
    for q in range(SCAN_UNROLL // 2):
        rows = pl.ds(pl.multiple_of(base + q * BF16_TILE, BF16_TILE), BF16_TILE)
        r_ref[rows, :] = jnp.concatenate([blocks[2 * q][0], blocks[2 * q + 1][0]], 0).astype(r_ref.dtype)
        i_ref[rows, :] = jnp.concatenate([blocks[2 * q][1], blocks[2 * q + 1][1]], 0).astype(i_ref.dtype)


def _scan_fwd(bur, bui, tab, jobs=()):
    t_len = bur.shape[0]
    nblk = t_len // SUBLANE
    lb = SCAN_LANES

    def body(br_ref, bi_ref, tab_ref, sr_ref, si_ref):
        tab_v = [tab_ref[q] for q in range(8)]

        def step(k, carry):
            cr, ci = carry
            base = pl.multiple_of(k * SCAN_UNROLL * SUBLANE, SCAN_UNROLL * SUBLANE)
            local = [_scan_local(xr, xi, tab_v, (1, 2, 4))
                     for xr, xi in _load_blocks(br_ref, bi_ref, base)]
            done = []
            for xr, xi in local:
                xr, xi = _scan_carry(xr, xi, tab_v, cr, ci)
                done.append((xr, xi))
                cr, ci = xr[SUBLANE - 1:SUBLANE, :], xi[SUBLANE - 1:SUBLANE, :]
            _store_blocks(sr_ref, si_ref, base, done)
            return cr, ci

        zero = jnp.zeros((1, lb), F32)
        lax.fori_loop(0, nblk // SCAN_UNROLL, step, (zero, zero))

    col = pl.BlockSpec((t_len, lb), lambda j: (0, j))
    return _call(
        body, "scan_fwd", (N_STATE // lb,),
        [col, col, pl.BlockSpec((8, SUBLANE, lb), lambda j: (0, 0, j))], [col, col],
        [jax.ShapeDtypeStruct((t_len, N_STATE), BF16)] * 2, [bur, bui, tab], jobs)


def _sgu_mix(v, ws_ref, lane_lo):
    rows = []
    for c0 in range(0, v.shape[0], CHUNK):
        slabs = []
        for j in range(SGU_W // LANE):
            prod = jnp.dot(ws_ref[j], v[c0:c0 + CHUNK, j * LANE:(j + 1) * LANE].astype(BF16),
                           preferred_element_type=F32)
            slabs.append(jnp.where(lane_lo, prod[:CHUNK], prod[CHUNK:]))
        rows.append(jnp.concatenate(slabs, axis=1))
    return jnp.concatenate(rows, axis=0) if len(rows) > 1 else rows[0]


def _fwd_mix(x, p, str_, sti, cre, cim, d_skip, w_glu, b_glu, w_pa, g_sgu, ws_st, bmat, w_pb, w_out,
             jobs=()):
    t_len = x.shape[0]

    def body(x_ref, p_ref, sr_ref, si_ref, cre_ref, cim_ref, dsk_ref, wg_ref, bg_ref, wpa_ref,
             gs_ref, ws_ref, bm_ref, wpb_ref, wo_ref,
             x2_ref, y0_ref, z_ref, mx_ref, ya_ref, yb_ref):
        u = p_ref[:, 0:SSM_W].astype(F32)
        y0 = jnp.concatenate(
            [_dot(sr_ref[:, i * DIAG_N:(i + 1) * DIAG_N],
                  cre_ref[i * DIAG_N:(i + 1) * DIAG_N, i * LANE:(i + 1) * LANE])
             - _dot(si_ref[:, i * DIAG_N:(i + 1) * DIAG_N],
                    cim_ref[i * DIAG_N:(i + 1) * DIAG_N, i * LANE:(i + 1) * LANE])
             for i in range(SSM_W // LANE)], axis=1) + dsk_ref[...] * u
        y0_ref[...] = y0.astype(BF16)
        y1 = _gelu(y0)
        z = _dot(y1, wg_ref[...]) + bg_ref[...]
        z_ref[...] = z.astype(BF16)
        ya_pre = (y1 * _sigmoid(z)).astype(BF16)
        ya = jnp.concatenate([jnp.dot(ya_pre, wpa_ref[k], preferred_element_type=F32)
                              for k in range(N_CHIP)], axis=1)
        ya_ref[...] = ya.astype(BF16)

        uvg = _gelu(p_ref[:, SSM_W:SSM_W + 2 * SGU_W].astype(F32))
        u2 = uvg[:, :SGU_W]
        _, vh = _rms_stats(uvg[:, SGU_W:])
        v3 = vh * gs_ref[...]
        lane_lo = lax.broadcasted_iota(jnp.int32, (CHUNK, LANE), 1) < SGU_D
        bias = jnp.concatenate([bm_ref[...]] * (TB // CHUNK), axis=0)
        mixed = _sgu_mix(v3, ws_ref, lane_lo) + bias
        mx_ref[...] = mixed.astype(BF16)
        sgu = (u2 * mixed).astype(BF16)
        yb = jnp.concatenate([jnp.dot(sgu, wpb_ref[k], preferred_element_type=F32)
                              for k in range(N_CHIP)], axis=1)
        yb_ref[...] = yb.astype(BF16)

        lg0 = SSM_W + 2 * SGU_W
        ga = _sigmoid(p_ref[:, lg0:lg0 + D_MODEL].astype(F32))
        gb = _sigmoid(p_ref[:, lg0 + D_MODEL:lg0 + 2 * D_MODEL].astype(F32))
        mrg = ga * ya + gb * yb
        x2_ref[...] = x_ref[...] + _dot(mrg, wo_ref[...])

    return _call(
        body, "fwd_mix", (t_len // TB,),
        [_rows(TB, D_MODEL), _rows(TB, IN_COLS), _rows(TB, N_STATE), _rows(TB, N_STATE)]
        + [_whole()] * 11,
        [_rows(TB, D_MODEL), _rows(TB, SSM_W), _rows(TB, SSM_W), _rows(TB, SGU_W),
         _rows(TB, D_MODEL), _rows(TB, D_MODEL)],
        [jax.ShapeDtypeStruct((t_len, D_MODEL), F32), jax.ShapeDtypeStruct((t_len, SSM_W), BF16),
         jax.ShapeDtypeStruct((t_len, SSM_W), BF16), jax.ShapeDtypeStruct((t_len, SGU_W), BF16),
         jax.ShapeDtypeStruct((t_len, D_MODEL), BF16), jax.ShapeDtypeStruct((t_len, D_MODEL), BF16)],
        [x, p, str_, sti, cre, cim, d_skip, w_glu, b_glu, w_pa, g_sgu, ws_st, bmat, w_pb, w_out], jobs)


def _conv_taps(v, cw_ref, c0, width):
    w0 = cw_ref[0:1, c0:c0 + width]
    w1 = cw_ref[1:2, c0:c0 + width]
    w2 = cw_ref[2:3, c0:c0 + width]
    return w0 * pltpu.roll(v, 2, 0) + w1 * pltpu.roll(v, 1, 0) + w2 * v


def _fwd_ffn(x2, target, g_ffn, w_up, conv_w, conv_b, w_down, g_final):
    t_len = x2.shape[0]
    half = D_FF // 2
    blocks_per_halo = TB // HALO

    def body(x2_ref, xp_ref, tg_ref, gf_ref, wu_ref, cw_ref, cb_ref, wd_ref, gl_ref,
             up_ref, act_ref, f_ref, h2_ref, dx3_ref, sm_ref):
        i = pl.program_id(0)
        xe = jnp.concatenate([xp_ref[...] * jnp.where(i == 0, 0.0, 1.0), x2_ref[...]], axis=0)
        _, xh = _rms_stats(xe)
        h2 = (xh * gf_ref[...]).astype(BF16)
        h2_ref[...] = h2[HALO:]
        acc = jnp.zeros((TB, D_MODEL), F32)
        ups = [jnp.dot(h2, wu_ref[k], preferred_element_type=F32) for k in range(N_CHIP)]
        for hc in range(2):
            ca = hc * half
            cb = D_FF + hc * half
            ua, ub = ups[hc], ups[2 + hc]
            up_ref[:, ca:ca + half] = ua[HALO:].astype(BF16)
            up_ref[:, cb:cb + half] = ub[HALO:].astype(BF16)
            ac = _conv_taps(ua, cw_ref, ca, half)[HALO:] + cb_ref[:, ca:ca + half]
            bc = _conv_taps(ub, cw_ref, cb, half)[HALO:] + cb_ref[:, cb:cb + half]
            act_ref[:, ca:ca + half] = ac.astype(BF16)
            act_ref[:, cb:cb + half] = bc.astype(BF16)
            f = (ac * _sigmoid(ac) * bc).astype(BF16)
            f_ref[:, ca:ca + half] = f
            acc = acc + jnp.dot(f, wd_ref[ca:ca + half, :], preferred_element_type=F32)
        x3 = x2_ref[...] + acc
        r3, xh3 = _rms_stats(x3)
        err = xh3 * gl_ref[...] - tg_ref[...]
        dout = err * (1.0 / D_MODEL)
        dx3_ref[...] = _rms_bwd(dout * gl_ref[...], xh3, r3)
        dgl = jnp.sum(dout * xh3, axis=0, keepdims=True)
        loss = 0.5 * jnp.sum(jnp.mean(err * err, axis=-1, keepdims=True), axis=0, keepdims=True)
        upd = jnp.concatenate([dgl, jnp.broadcast_to(loss, (1, D_MODEL)),
                               jnp.zeros((SUBLANE - 2, D_MODEL), F32)], axis=0)

        @pl.when(i == 0)
        def _():
            sm_ref[...] = upd

        @pl.when(i > 0)
        def _():
            sm_ref[...] += upd

    prev = pl.BlockSpec((HALO, D_MODEL), lambda i: (jnp.maximum(i * blocks_per_halo - 1, 0), 0))
    return _call(
        body, "fwd_ffn", (t_len // TB,),
        [_rows(TB, D_MODEL), prev, _rows(TB, D_MODEL)] + [_whole()] * 6,
        [_rows(TB, 2 * D_FF), _rows(TB, 2 * D_FF), _rows(TB, D_FF), _rows(TB, D_MODEL),
         _rows(TB, D_MODEL), _acc(SUBLANE, D_MODEL)],
        [jax.ShapeDtypeStruct((t_len, 2 * D_FF), BF16), jax.ShapeDtypeStruct((t_len, 2 * D_FF), BF16),
         jax.ShapeDtypeStruct((t_len, D_FF), BF16), jax.ShapeDtypeStruct((t_len, D_MODEL), BF16),
         jax.ShapeDtypeStruct((t_len, D_MODEL), F32), jax.ShapeDtypeStruct((SUBLANE, D_MODEL), F32)],
        [x2, x2, target, g_ffn, w_up, conv_w, conv_b, w_down, g_final])[0]


def _bwd_ffn(dx3, up, act, x2, g_ffn, w_up, conv_w, w_down, jobs=()):
    t_len = x2.shape[0]
    half = D_FF // 2
    nblk = t_len // TB
    halo_b = 2 * HALO
    n_e = TB + HALO

    def body(dx_ref, dxn_ref, up_ref, act_ref, actn_ref, x2_ref, gf_ref, wu_ref, cw_ref,
             wd_ref, dx2_ref, dup_ref, smw_ref, smg_ref):
        i = pl.program_id(0)
        keep_last = jnp.where(i == nblk - 1, 0.0, 1.0)
        dxe = jnp.concatenate([dx_ref[...], dxn_ref[...] * keep_last], axis=0).astype(BF16)
        dh2 = jnp.zeros((TB, D_MODEL), F32)
        zpad = jnp.zeros((1, half), F32)
        dfs = [lax.dot_general(dxe, wd_ref[hc * half:(hc + 1) * half, :], (((1,), (1,)), ((), ())),
                               preferred_element_type=F32) for hc in range(2)]
        for hc in range(2):
            ca = hc * half
            cb = D_FF + hc * half
            ac = jnp.concatenate([act_ref[:, ca:ca + half].astype(F32),
                                  actn_ref[:, ca:ca + half].astype(F32)[:HALO]], axis=0)
            bc = jnp.concatenate([act_ref[:, cb:cb + half].astype(F32),
                                  actn_ref[:, cb:cb + half].astype(F32)[:HALO]], axis=0)
            wa = [cw_ref[k:k + 1, ca:ca + half] for k in range(3)]
            wb = [cw_ref[k:k + 1, cb:cb + half] for k in range(3)]
            df = dfs[hc]
            sg = _sigmoid(ac)
            da = df * bc * sg * (1.0 + ac * (1.0 - sg))
            db = df * ac * sg
            da1, da2 = pltpu.roll(da, n_e - 1, 0), pltpu.roll(da, n_e - 2, 0)
            db1, db2 = pltpu.roll(db, n_e - 1, 0), pltpu.roll(db, n_e - 2, 0)
            dua = (wa[2] * da + wa[1] * da1 + wa[0] * da2)[:TB]
            dub = (wb[2] * db + wb[1] * db1 + wb[0] * db2)[:TB]
            dup_ref[:, ca:ca + half] = dua.astype(BF16)
            dup_ref[:, cb:cb + half] = dub.astype(BF16)
            dh2 = dh2 + _dot_nt(dua, wu_ref[hc]) + _dot_nt(dub, wu_ref[2 + hc])
            rows = []
            for u_, d0, d1, d2 in ((up_ref[:, ca:ca + half].astype(F32), da, da1, da2),
                                   (up_ref[:, cb:cb + half].astype(F32), db, db1, db2)):
                rows.append([jnp.sum(u_ * d2[:TB], axis=0, keepdims=True),
                             jnp.sum(u_ * d1[:TB], axis=0, keepdims=True),
                             jnp.sum(u_ * d0[:TB], axis=0, keepdims=True),
                             jnp.sum(d0[:TB], axis=0, keepdims=True)])
            for c0, rws in ((ca, rows[0]), (cb, rows[1])):
                upd = jnp.concatenate(rws + [zpad] * (SUBLANE - 4), axis=0)

                @pl.when(i == 0)
                def _(upd=upd, c0=c0):
                    smw_ref[:, c0:c0 + half] = upd

                @pl.when(i > 0)
                def _(upd=upd, c0=c0):
                    smw_ref[:, c0:c0 + half] += upd

        r2, xh2 = _rms_stats(x2_ref[...])
        dx2_ref[...] = dx_ref[...] + _rms_bwd(dh2 * gf_ref[...], xh2, r2)
        updg = jnp.concatenate([jnp.sum(dh2 * xh2, axis=0, keepdims=True),
                                jnp.zeros((SUBLANE - 1, D_MODEL), F32)], axis=0)

        @pl.when(i == 0)
        def _():
            smg_ref[...] = updg

        @pl.when(i > 0)
        def _():
            smg_ref[...] += updg

    nxt_d = pl.BlockSpec((HALO, D_MODEL),
                         lambda i: (jnp.minimum((i + 1) * (TB // HALO), t_len // HALO - 1), 0))
    nxt_a = pl.BlockSpec((halo_b, 2 * D_FF),
                         lambda i: (jnp.minimum((i + 1) * (TB // halo_b), t_len // halo_b - 1), 0))
    return _call(
        body, "bwd_ffn", (nblk,),
        [_rows(TB, D_MODEL), nxt_d, _rows(TB, 2 * D_FF), _rows(TB, 2 * D_FF), nxt_a,
         _rows(TB, D_MODEL)] + [_whole()] * 4,
        [_rows(TB, D_MODEL), _rows(TB, 2 * D_FF), _acc(SUBLANE, 2 * D_FF), _acc(SUBLANE, D_MODEL)],
        [jax.ShapeDtypeStruct((t_len, D_MODEL), F32), jax.ShapeDtypeStruct((t_len, 2 * D_FF), BF16),
         jax.ShapeDtypeStruct((SUBLANE, 2 * D_FF), F32), jax.ShapeDtypeStruct((SUBLANE, D_MODEL), F32)],
        [dx3, dx3, up, act, act, x2, g_ffn, w_up, conv_w, w_down], jobs)


def _bwd_mix(dx2, p, y0, z, mixed, ya, yb, w_out, w_pa, w_pb, w_glu, cre, cim, ws_st, wst_st,
             d_skip, g_sgu, jobs=()):
    t_len = dx2.shape[0]
    pc = D_MODEL // N_CHIP
    n_slab = SGU_W // LANE

    def body(dx_ref, p_ref, y0_ref, z_ref, mx_ref, ya_ref, yb_ref, wo_ref, wpa_ref, wpb_ref,
             wg_ref, cre_ref, cim_ref, ws_ref, wst_ref, dsk_ref, gs_ref,
             dsr_ref, dsi_ref, du_ref, drest_ref, mrg_ref, dya_ref, dyb_ref, yap_ref, dz_ref,
             y1_ref, sgu_ref, dy0_ref, sm_ref, dbm_ref, dws_ref):
        i = pl.program_id(0)
        first = i == 0
        lg0 = SSM_W + 2 * SGU_W
        ga = _sigmoid(p_ref[:, lg0:lg0 + D_MODEL].astype(F32))
        gb = _sigmoid(p_ref[:, lg0 + D_MODEL:lg0 + 2 * D_MODEL].astype(F32))
        yav = ya_ref[...].astype(F32)
        ybv = yb_ref[...].astype(F32)
        mrg_ref[...] = (ga * yav + gb * ybv).astype(BF16)
        y0v = y0_ref[...].astype(F32)
        y1, y1_grad = _gelu_and_grad(y0v)
        sz = _sigmoid(z_ref[...].astype(F32))
        y1_ref[...] = y1.astype(BF16)
        yap_ref[...] = (y1 * sz).astype(BF16)

        dmrg = _dot_nt(dx_ref[...], wo_ref[...])
        drest_ref[:, 2 * SGU_W:2 * SGU_W + D_MODEL] = (dmrg * yav * ga * (1.0 - ga)).astype(BF16)
        drest_ref[:, 2 * SGU_W + D_MODEL:] = (dmrg * ybv * gb * (1.0 - gb)).astype(BF16)
        dya = (dmrg * ga).astype(BF16)
        dyb = (dmrg * gb).astype(BF16)
        dya_ref[...] = dya
        dyb_ref[...] = dyb

        dyap = jnp.zeros((TB, SSM_W), F32)
        for k in range(N_CHIP):
            dyap = dyap + _dot_nt(dya[:, k * pc:(k + 1) * pc], wpa_ref[k])
        dz = dyap * y1 * sz * (1.0 - sz)
        dz_ref[...] = dz.astype(BF16)
        dy0 = (dyap * sz + _dot_nt(dz, wg_ref[...])) * y1_grad
        dy0_ref[...] = dy0.astype(BF16)
        u = p_ref[:, 0:SSM_W].astype(F32)
        du_ref[...] = dy0 * dsk_ref[...]
        for q in range(SSM_W // LANE):
            rows, cols = slice(q * DIAG_N, (q + 1) * DIAG_N), slice(q * LANE, (q + 1) * LANE)
            dsr_ref[:, rows] = _dot_nt(dy0[:, cols], cre_ref[rows, cols]).astype(BF16)
            dsi_ref[:, rows] = (-_dot_nt(dy0[:, cols], cim_ref[rows, cols])).astype(BF16)

        uv = p_ref[:, SSM_W:lg0].astype(F32)
        uvg, gg = _gelu_and_grad(uv)
        u2 = uvg[:, :SGU_W]
        rv, vh = _rms_stats(uvg[:, SGU_W:])
        v3 = vh * gs_ref[...]
        mixed = mx_ref[...].astype(F32)
        dsgu = jnp.zeros((TB, SGU_W), F32)
        for k in range(N_CHIP):
            dsgu = dsgu + _dot_nt(dyb[:, k * pc:(k + 1) * pc], wpb_ref[k])
        sgu_ref[...] = (u2 * mixed).astype(BF16)
        drest_ref[:, 0:SGU_W] = (dsgu * mixed * gg[:, :SGU_W]).astype(BF16)
        dmix = dsgu * u2
        lane_lo = lax.broadcasted_iota(jnp.int32, (CHUNK, LANE), 1) < SGU_D
        dv3 = _sgu_mix(dmix, wst_ref, lane_lo)
        dbm = jnp.zeros((CHUNK, SGU_W), F32)
        for c0 in range(0, TB, CHUNK):
            dbm = dbm + dmix[c0:c0 + CHUNK]
        for j in range(n_slab):
            lo = jnp.zeros((CHUNK, CHUNK), F32)
            hi = jnp.zeros((CHUNK, CHUNK), F32)
            for c0 in range(0, TB, CHUNK):
                dsl = dmix[c0:c0 + CHUNK, j * LANE:(j + 1) * LANE]
                vsl = v3[c0:c0 + CHUNK, j * LANE:(j + 1) * LANE]
                lo = lo + _dot_nt(jnp.where(lane_lo, dsl, 0.0), vsl)
                hi = hi + _dot_nt(jnp.where(lane_lo, 0.0, dsl), vsl)

            @pl.when(first)
            def _(lo=lo, hi=hi, j=j):
                dws_ref[2 * j] = lo
                dws_ref[2 * j + 1] = hi

            @pl.when(jnp.logical_not(first))
            def _(lo=lo, hi=hi, j=j):
                dws_ref[2 * j] += lo
                dws_ref[2 * j + 1] += hi

        dv2 = _rms_bwd(dv3 * gs_ref[...], vh, rv)
        drest_ref[:, SGU_W:2 * SGU_W] = (dv2 * gg[:, SGU_W:]).astype(BF16)

        upd = jnp.concatenate([jnp.sum(dy0 * u, axis=0, keepdims=True),
                               jnp.sum(dz, axis=0, keepdims=True),
                               jnp.sum(dv3 * vh, axis=0, keepdims=True),
                               jnp.zeros((SUBLANE - 3, SSM_W), F32)], axis=0)

        @pl.when(first)
        def _():
            sm_ref[...] = upd
            dbm_ref[...] = dbm

        @pl.when(jnp.logical_not(first))
        def _():
            sm_ref[...] += upd
            dbm_ref[...] += dbm

    rest = 2 * SGU_W + 2 * D_MODEL
    bf_d, bf_s = jax.ShapeDtypeStruct((t_len, D_MODEL), BF16), jax.ShapeDtypeStruct((t_len, SSM_W), BF16)
    return _call(
        body, "bwd_mix", (t_len // TB,),
        [_rows(TB, D_MODEL), _rows(TB, IN_COLS), _rows(TB, SSM_W), _rows(TB, SSM_W),
         _rows(TB, SGU_W), _rows(TB, D_MODEL), _rows(TB, D_MODEL)] + [_whole()] * 10,
        [_rows(TB, N_STATE), _rows(TB, N_STATE), _rows(TB, SSM_W), _rows(TB, rest),
         _rows(TB, D_MODEL), _rows(TB, D_MODEL), _rows(TB, D_MODEL), _rows(TB, SSM_W),
         _rows(TB, SSM_W), _rows(TB, SSM_W), _rows(TB, SGU_W), _rows(TB, SSM_W),
         _acc(SUBLANE, SSM_W), _acc(CHUNK, SGU_W),
         pl.BlockSpec((SGU_G, CHUNK, CHUNK), lambda i: (0, 0, 0))],
        [jax.ShapeDtypeStruct((t_len, N_STATE), BF16), jax.ShapeDtypeStruct((t_len, N_STATE), BF16),
         jax.ShapeDtypeStruct((t_len, SSM_W), F32), jax.ShapeDtypeStruct((t_len, rest), BF16),
         bf_d, bf_d, bf_d, bf_s, bf_s, bf_s, bf_s, bf_s,
         jax.ShapeDtypeStruct((SUBLANE, SSM_W), F32), jax.ShapeDtypeStruct((CHUNK, SGU_W), F32),
         jax.ShapeDtypeStruct((SGU_G, CHUNK, CHUNK), F32)],
        [dx2, p, y0, z, mixed, ya, yb, w_out, w_pa, w_pb, w_glu, cre, cim, ws_st, wst_st, d_skip,
         g_sgu], jobs)


def _scan_bwd(dsr, dsi, str_, sti, tab_rev, jobs=()):
    t_len = dsr.shape[0]
    nblk = t_len // SUBLANE
    lb = SCAN_LANES

    def body(dr_ref, di_ref, sr_ref, si_ref, tab_ref, lr_ref, li_ref, dar_ref, dai_ref):
        tab_v = [tab_ref[q] for q in range(8)]
        row0 = lax.broadcasted_iota(jnp.int32, (SUBLANE, lb), 0) == 0
        tile = BF16_TILE

        def step(k, carry):
            cr, ci, acr, aci = carry
            base = pl.multiple_of((nblk - (k + 1) * SCAN_UNROLL) * SUBLANE, SCAN_UNROLL * SUBLANE)
            state = _load_blocks(sr_ref, si_ref, base)
            before = pl.ds(pl.multiple_of(jnp.maximum(base - tile, 0), tile), tile)
            has_before = jnp.where(base > 0, 1.0, 0.0)
            prev = (sr_ref[before, :].astype(F32)[tile - 1:tile] * has_before,
                    si_ref[before, :].astype(F32)[tile - 1:tile] * has_before)
            local = [_scan_local(xr, xi, tab_v, (7, 6, 4))
                     for xr, xi in _load_blocks(dr_ref, di_ref, base)]
            lam = [None] * SCAN_UNROLL
            for b in reversed(range(SCAN_UNROLL)):
                xr, xi = _scan_carry(*local[b], tab_v, cr, ci)
                lam[b] = (xr, xi)
                cr, ci = xr[0:1, :], xi[0:1, :]
                pr, pi = prev if b == 0 else (state[b - 1][0][SUBLANE - 1:], state[b - 1][1][SUBLANE - 1:])
                s_r = jnp.where(row0, pr, pltpu.roll(state[b][0], 1, 0))
                s_i = jnp.where(row0, pi, pltpu.roll(state[b][1], 1, 0))
                acr = acr + xr * s_r + xi * s_i
                aci = aci + xi * s_r - xr * s_i
            _store_blocks(lr_ref, li_ref, base, lam)
            return cr, ci, acr, aci

        zero = jnp.zeros((1, lb), F32)
        zacc = jnp.zeros((SUBLANE, lb), F32)
        _, _, acr, aci = lax.fori_loop(0, nblk // SCAN_UNROLL, step, (zero, zero, zacc, zacc))
        dar_ref[...] = acr
        dai_ref[...] = aci

    col = pl.BlockSpec((t_len, lb), lambda j: (0, j))
    small = pl.BlockSpec((SUBLANE, lb), lambda j: (0, j))
    return _call(
        body, "scan_bwd", (N_STATE // lb,),
        [col, col, col, col, pl.BlockSpec((8, SUBLANE, lb), lambda j: (0, 0, j))],
        [col, col, small, small],
        [jax.ShapeDtypeStruct((t_len, N_STATE), BF16)] * 2
        + [jax.ShapeDtypeStruct((SUBLANE, N_STATE), F32)] * 2,
        [dsr, dsi, str_, sti, tab_rev], jobs)


def _bwd_in(lam_r, lam_i, du_part, drest, x, dx2, g_mix, w_in, bre, bim, jobs=()):
    t_len = x.shape[0]
    cs = IN_COLS // N_CHIP

    def body(lr_ref, li_ref, du_ref, dr_ref, x_ref, dx2_ref, g_ref, w_ref, bre_ref, bim_ref,
             gx_ref, dp_ref, sm_ref):
        i = pl.program_id(0)
        du = du_ref[...] + jnp.concatenate(
            [_dot_nt(lr_ref[:, i * DIAG_N:(i + 1) * DIAG_N],
                     bre_ref[i * LANE:(i + 1) * LANE, i * DIAG_N:(i + 1) * DIAG_N])
             + _dot_nt(li_ref[:, i * DIAG_N:(i + 1) * DIAG_N],
                       bim_ref[i * LANE:(i + 1) * LANE, i * DIAG_N:(i + 1) * DIAG_N])
             for i in range(SSM_W // LANE)], axis=1)
        dp_ref[:, 0:SSM_W] = du.astype(BF16)
        dp_ref[:, SSM_W:] = dr_ref[...]
        dh = jnp.zeros((TB, D_MODEL), F32)
        for k in range(N_CHIP):
            dh = dh + _dot_nt(dp_ref[:, k * cs:(k + 1) * cs], w_ref[k])
        r, xh = _rms_stats(x_ref[...])
        gx_ref[...] = dx2_ref[...] + _rms_bwd(dh * g_ref[...], xh, r)
        upd = jnp.concatenate([jnp.sum(dh * xh, axis=0, keepdims=True),
                               jnp.zeros((SUBLANE - 1, D_MODEL), F32)], axis=0)

        @pl.when(i == 0)
        def _():
            sm_ref[...] = upd

        @pl.when(i > 0)
        def _():
            sm_ref[...] += upd

    return _call(
        body, "bwd_in", (t_len // TB,),
        [_rows(TB, N_STATE), _rows(TB, N_STATE), _rows(TB, SSM_W), _rows(TB, IN_COLS - SSM_W),
         _rows(TB, D_MODEL), _rows(TB, D_MODEL)] + [_whole()] * 4,
        [_rows(TB, D_MODEL), _rows(TB, IN_COLS), _acc(SUBLANE, D_MODEL)],
        [jax.ShapeDtypeStruct((t_len, D_MODEL), F32), jax.ShapeDtypeStruct((t_len, IN_COLS), BF16),
         jax.ShapeDtypeStruct((SUBLANE, D_MODEL), F32)],
        [lam_r, lam_i, du_part, drest, x, dx2, g_mix, w_in, bre, bim], jobs)


def _matmul_tn(a, b, name, out_shape, grid_ij, a_blk, a_map, b_blk, b_map, o_blk, o_map, jobs=()):
    tk = a_blk[0]
    nk = a.shape[0] // tk
    assert nk * tk == a.shape[0] and nk > 0

    def body(a_ref, b_ref, o_ref, acc_ref):
        k = pl.program_id(2)

        @pl.when(k == 0)
        def _():
            acc_ref[...] = jnp.zeros_like(acc_ref)

        acc_ref[...] += lax.dot_general(a_ref[...].astype(BF16), b_ref[...].astype(BF16),
                                        (((0,), (0,)), ((), ())), preferred_element_type=F32)

        @pl.when(k == nk - 1)
        def _():
            o_ref[...] = acc_ref[...]

    outs, per_job = _call(
        body, name, (grid_ij[0], grid_ij[1], nk),
        [pl.BlockSpec(a_blk, a_map), pl.BlockSpec(b_blk, b_map)], [pl.BlockSpec(o_blk, o_map)],
        [jax.ShapeDtypeStruct(out_shape, F32)], [a, b], jobs,
        scratch=[pltpu.VMEM((a_blk[1], b_blk[1]), F32)])
    return outs[0], per_job


def _dw_shards(a, b, name, tk, jobs=()):
    m, n = a.shape[1], b.shape[1]
    tn = n // N_CHIP
    tk = min(tk, a.shape[0])
    return _matmul_tn(a, b, name, (N_CHIP, m, tn), (1, N_CHIP),
                      (tk, m), lambda i, j, k: (k, 0), (tk, tn), lambda i, j, k: (k, j),
                      (None, m, tn), lambda i, j, k: (j, 0, 0), jobs)


def _dw_rows(a, b, name, tm, tk):
    m, n = a.shape[1], b.shape[1]
    tk = min(tk, a.shape[0])
    return _matmul_tn(a, b, name, (m, n), (m // tm, 1),
                      (tk, tm), lambda i, j, k: (k, i), (tk, n), lambda i, j, k: (k, 0),
                      (tm, n), lambda i, j, k: (i, 0))[0]


def _dw_cols(a, b, name, tn, sharded, jobs=()):
    t_len, m = a.shape
    n = b.shape[1]

    def body(a_ref, b_ref, o_ref):
        o_ref[...] = lax.dot_general(a_ref[...].astype(BF16), b_ref[...].astype(BF16),
                                     (((0,), (0,)), ((), ())), preferred_element_type=F32)

    if sharded:
        o_spec, o_shape = pl.BlockSpec((None, m, tn), lambda j: (j, 0, 0)), (n // tn, m, tn)
    else:
        o_spec, o_shape = pl.BlockSpec((m, tn), lambda j: (0, j)), (m, n)
    outs, per_job = _call(body, name, (n // tn,),
                          [_whole(), pl.BlockSpec((t_len, tn), lambda j: (0, j))], [o_spec],
                          [jax.ShapeDtypeStruct(o_shape, F32)], [a, b], jobs)
    return outs[0], per_job


def _dw_pair(a, m, b1, b2, name, jobs=()):
    t_len = a.shape[0]
    n_slab = DIAG_N // LANE
    rows_per_slab = LANE // n_slab

    def body(a_ref, b1_ref, b2_ref, o1_ref, o2_ref):
        for b_ref, o_ref in ((b1_ref, o1_ref), (b2_ref, o2_ref)):
            prod = lax.dot_general(a_ref[...].astype(BF16), b_ref[...].astype(BF16),
                                   (((0,), (0,)), ((), ())), preferred_element_type=F32)
            for j in range(n_slab):
                rows = slice(j * rows_per_slab, (j + 1) * rows_per_slab)
                o_ref[rows, :] = prod[rows, j * LANE:(j + 1) * LANE]

    tok = pl.BlockSpec((t_len, DIAG_N), lambda i: (0, i))
    out = pl.BlockSpec((LANE, LANE), lambda i: (i, 0))
    return _call(body, name, (m // LANE,),
                 [pl.BlockSpec((t_len, LANE), lambda i: (0, i)), tok, tok], [out, out],
                 [jax.ShapeDtypeStruct((m, LANE), F32)] * 2, [a, b1, b2], jobs)


def _prefetch_call(body, name, grid, scalars, in_specs, out_specs, out_shape, args):
    return pl.pallas_call(
        body, name=name,
        grid_spec=pltpu.PrefetchScalarGridSpec(num_scalar_prefetch=1, grid=grid, in_specs=in_specs,
                                               out_specs=out_specs),
        out_shape=out_shape, compiler_params=_params(len(grid)),
    )(scalars, *args)


def _place_shard(w, where, name, dtype, tr):
    rows, cols = w.shape

    def body(s_ref, w_ref, o_ref):
        o_ref[...] = w_ref[...].astype(dtype)

    return _prefetch_call(
        body, name, (rows // tr,), where,
        [pl.BlockSpec((tr, cols), lambda i, s: (i, 0))],
        pl.BlockSpec((None, tr, cols), lambda i, s: (s[0], i, 0)),
        jax.ShapeDtypeStruct((N_CHIP, rows, cols), dtype), [w])


def _place_shards(ws, where, name, dtype):
    n = len(ws)

    def body(s_ref, *refs):
        for t in range(n):
            refs[n + t][...] = refs[t][...].astype(dtype)

    return _prefetch_call(
        body, name, (1,), where,
        [pl.BlockSpec(w.shape, lambda i, s: (0, 0)) for w in ws],
        [pl.BlockSpec((None,) + w.shape, lambda i, s: (s[0], 0, 0)) for w in ws],
        [jax.ShapeDtypeStruct((N_CHIP,) + w.shape, dtype) for w in ws], ws)


def _add_sibling(gs, gots, where, name):
    n = len(gs)
    halves = [(g.shape[1] // 2, g.shape[2]) for g in gs]

    def body(s_ref, *refs):
        for t in range(n):
            refs[2 * n + t][...] = (refs[t][...] + refs[n + t][...]).astype(BF16)

    return _prefetch_call(
        body, name, (N_CHIP,), where,
        [pl.BlockSpec((None, hr, cs), lambda k, s: (k, s[1], 0)) for hr, cs in halves]
        + [pl.BlockSpec((None, hr, cs), lambda k, s: (k, 0, 0)) for hr, cs in halves],
        [pl.BlockSpec((None, hr, cs), lambda k, s: (k, 0, 0)) for hr, cs in halves],
        [jax.ShapeDtypeStruct((N_CHIP, hr, cs), BF16) for hr, cs in halves], list(gs) + list(gots))


def _add_chips(sums, gots, where, name):
    n = len(sums)
    halves = [s.shape[1:] for s in sums]

    def body(s_ref, *refs):
        for t in range(n):
            own_ref, got_ref = refs[t], refs[n + t]
            refs[2 * n + t][...] = ((own_ref[...].astype(F32) + got_ref[0].astype(F32))
                                    + got_ref[1].astype(F32)) + got_ref[2].astype(F32)

    return _prefetch_call(
        body, name, (1,), where,
        [pl.BlockSpec((None, hr, cs), lambda i, s: (s[0], 0, 0)) for hr, cs in halves]
        + [pl.BlockSpec((3, hr, cs), lambda i, s: (0, 0, 0)) for hr, cs in halves],
        [pl.BlockSpec((hr, cs), lambda i, s: (s[1], 0)) for hr, cs in halves],
        [jax.ShapeDtypeStruct((2 * hr, cs), F32) for hr, cs in halves], list(sums) + list(gots))


def _small_allreduce(pack):
    rows = pack.shape[0]
    half = rows // 2

    def body(in_ref, out_ref, sib_ref, slots_ref, s_a, r_a, s_b, r_b, s_c, r_c):
        x, y, c, chips = _place()
        k_me = 2 * x + y
        sib = (x, y, 1 - c)
        first = _remote(in_ref, sib_ref, s_a, r_a, sib)
        first.start()
        first.wait()
        mine = _half(rows, c)
        slots_ref[k_me] = in_ref[mine, :] + sib_ref[mine, :]
        cps = [_remote(slots_ref.at[k_me], slots_ref.at[k_me], s_b.at[j], r_b.at[j], (*ch, c))
               for j, ch in enumerate(chips)]
        for cp in cps:
            cp.start()
        for j, ch in enumerate(chips):
            slot = slots_ref.at[_chip_index(ch)]
            _remote(slot, slot, s_b.at[j], r_b.at[j], (*ch, c)).wait_recv()
        for cp in cps:
            cp.wait_send()
        out_ref[mine, :] = ((slots_ref[0] + slots_ref[1]) + slots_ref[2]) + slots_ref[3]
        last = _remote(out_ref.at[mine, :], out_ref.at[mine, :], s_c, r_c, sib)
        last.start()
        theirs = out_ref.at[_half(rows, 1 - c), :]
        _remote(theirs, theirs, s_c, r_c, sib).wait_recv()
        last.wait_send()

    return pl.pallas_call(
        body, name="small_allreduce", in_specs=[_whole()], out_specs=_whole(),
        out_shape=jax.ShapeDtypeStruct(pack.shape, F32),
        scratch_shapes=[pltpu.VMEM(pack.shape, F32), pltpu.VMEM((N_CHIP, half, LANE), F32),
                        pltpu.SemaphoreType.DMA, pltpu.SemaphoreType.DMA,
                        pltpu.SemaphoreType.DMA((3,)), pltpu.SemaphoreType.DMA((3,)),
                        pltpu.SemaphoreType.DMA, pltpu.SemaphoreType.DMA],
        compiler_params=_params(0),
    )(pack)


def _adamw_update(w_ref, g_ref, m_ref, v_ref, d_ref, mo_ref, vo_ref):
    gv = g_ref[...]
    mn = ADAM_B1 * m_ref[...] + (1.0 - ADAM_B1) * gv
    vn = ADAM_B2 * v_ref[...] + (1.0 - ADAM_B2) * (gv * gv)
    mo_ref[...] = mn
    vo_ref[...] = vn
    m_hat = mn / (1.0 - ADAM_B1 ** ADAM_STEP)
    v_hat = vn / (1.0 - ADAM_B2 ** ADAM_STEP)
    d_ref[...] = -ADAM_LR * (m_hat / (jnp.sqrt(v_hat) + ADAM_EPS) + ADAM_WD * w_ref[...])


def _adamw(w, g, m, v, name, tr):
    rows, cols = w.shape
    blk = _rows(tr, cols)

    def body(w_ref, g_ref, m_ref, v_ref, go_ref, d_ref, mo_ref, vo_ref):
        go_ref[...] = g_ref[...]
        _adamw_update(w_ref, g_ref, m_ref, v_ref, d_ref, mo_ref, vo_ref)

    return _call(body, name, (rows // tr,), [blk] * 4, [blk] * 4,
                 [jax.ShapeDtypeStruct(w.shape, F32)] * 4, [w, g, m, v])[0]


def _adamw_many(ws, gs, ms, vs, name):
    n = len(ws)

    def body(*refs):
        for t in range(n):
            _adamw_update(*[refs[q * n + t] for q in range(7)])

    specs = [pl.BlockSpec(a.shape, lambda i, nd=a.ndim: (0,) * nd) for a in ws]
    outs = pl.pallas_call(
        body, name=name, grid=(1,), in_specs=specs * 4, out_specs=specs * 3,
        out_shape=[jax.ShapeDtypeStruct(a.shape, F32) for _ in range(3) for a in ws],
        compiler_params=_params(1),
    )(*ws, *gs, *ms, *vs)
    return outs[:n], outs[n:2 * n], outs[2 * n:]


def _ssm_discretize(a_re, a_im, log_dt, b_re, b_im):
    dt = jnp.exp(log_dt)[:, None]
    mag = jnp.exp(dt * a_re)
    abr = mag * jnp.cos(dt * a_im)
    abi = mag * jnp.sin(dt * a_im)
    den = a_re * a_re + a_im * a_im
    nr = abr - 1.0
    ni = abi
    f_re = (nr * a_re + ni * a_im) / den
    f_im = (ni * a_re - nr * a_im) / den
    bbr = f_re[..., None] * b_re - f_im[..., None] * b_im
    bbi = f_re[..., None] * b_im + f_im[..., None] * b_re
    return abr, abi, bbr, bbi


def _scan_tables(abr, abi):
    ar = abr.reshape(1, N_STATE)
    ai = abi.reshape(1, N_STATE)
    pr, pi = [ar], [ai]
    for _ in range(SUBLANE - 1):
        pr, pi = pr + [pr[-1] * ar - pi[-1] * ai], pi + [pr[-1] * ai + pi[-1] * ar]
    row = jnp.arange(SUBLANE)[:, None]
    tabs = []
    for d in (1, 2, 4):
        tabs.append(jnp.where(row >= d, pr[d - 1], 0.0))
        tabs.append(jnp.where(row >= d, pi[d - 1], 0.0))
    tabs.append(jnp.concatenate(pr, axis=0))
    tabs.append(jnp.concatenate(pi, axis=0))
    fwd = jnp.stack(tabs)
    sign = jnp.array([1.0, -1.0] * 4, F32)[:, None, None]
    return fwd, fwd[:, ::-1, :] * sign


def _block_diag_b(bb):
    strip = bb.transpose(2, 0, 1).reshape(SSM_H, N_STATE)
    rows = lax.broadcasted_iota(jnp.int32, (SSM_W, N_STATE), 0) // SSM_H
    cols = lax.broadcasted_iota(jnp.int32, (SSM_W, N_STATE), 1) // SSM_P
    return jnp.where(rows == cols, jnp.tile(strip, (SSM_G, 1)), 0.0).astype(BF16)


def _block_diag_c(cc):
    strip = cc.transpose(0, 2, 1).reshape(N_STATE, SSM_H)
    rows = lax.broadcasted_iota(jnp.int32, (N_STATE, SSM_W), 0) // SSM_P
    cols = lax.broadcasted_iota(jnp.int32, (N_STATE, SSM_W), 1) // SSM_H
    return jnp.where(rows == cols, jnp.tile(strip, (1, SSM_G)), 0.0).astype(BF16)


SMALL_SHAPES = {
    "g_mix": (D_MODEL,), "a_re": (SSM_G, SSM_P), "a_im": (SSM_G, SSM_P), "log_dt": (SSM_G,),
    "b_re": (SSM_G, SSM_P, SSM_H), "b_im": (SSM_G, SSM_P, SSM_H),
    "c_re": (SSM_G, SSM_H, SSM_P), "c_im": (SSM_G, SSM_H, SSM_P),
    "d_skip": (SSM_W,), "b_glu": (SSM_W,), "g_sgu": (SGU_W,), "w_s": (SGU_G, CHUNK, CHUNK),
    "b_s": (SGU_G, CHUNK), "g_ffn": (D_MODEL,), "conv_b": (2 * D_FF,), "g_final": (D_MODEL,),
}
PACK_ITEMS = [("loss", (1,))] + [(n, SMALL_SHAPES[n]) for n in SMALL] + [("conv_w", (3, 2 * D_FF))]
TILE = SUBLANE * LANE


def _item_rows(shape):
    return -(-math.prod(shape) // TILE) * SUBLANE


PACK_ROWS = -(-sum(_item_rows(s) for _, s in PACK_ITEMS) // (2 * SUBLANE)) * (2 * SUBLANE)


def _pack(values):
    parts, used = [], 0
    for name, shape in PACK_ITEMS:
        size, rows = math.prod(shape), _item_rows(shape)
        if name in values:
            flat = values[name].astype(F32).reshape(size)
            if rows * LANE > size:
                flat = jnp.pad(flat, (0, rows * LANE - size))
            parts.append(flat.reshape(rows, LANE))
        else:
            parts.append(jnp.zeros((rows, LANE), F32))
        used += rows
    if PACK_ROWS > used:
        parts.append(jnp.zeros((PACK_ROWS - used, LANE), F32))
    return jnp.concatenate(parts, axis=0)


def _unpack(pack):
    out, off = {}, 0
    for name, shape in PACK_ITEMS:
        rows = _item_rows(shape)
        out[name] = pack[off:off + rows].reshape(rows * LANE)[:math.prod(shape)].reshape(shape)
        off += rows
    return out


PLACE_ROWS = {"w_in": 256, "w_up": 256, "w_down": 352, "w_out": 256, "w_proj_a": 256,
              "w_proj_b": 256, "w_glu": 128}


def kernel(x, g_mix, w_in, a_re, a_im, log_dt, b_re, b_im, c_re, c_im, d_skip, w_glu, b_glu, w_proj_a, g_sgu, w_s, b_s, w_proj_b, w_out, g_ffn, w_up, conv_w, conv_b, w_down, g_final, loss_target, m_g_mix, m_w_in, m_a_re, m_a_im, m_log_dt, m_b_re, m_b_im, m_c_re, m_c_im, m_d_skip, m_w_glu, m_b_glu, m_w_proj_a, m_g_sgu, m_w_s, m_b_s, m_w_proj_b, m_w_out, m_g_ffn, m_w_up, m_conv_w, m_conv_b, m_w_down, m_g_final, v_g_mix, v_w_in, v_a_re, v_a_im, v_log_dt, v_b_re, v_b_im, v_c_re, v_c_im, v_d_skip, v_w_glu, v_b_glu, v_w_proj_a, v_g_sgu, v_w_s, v_b_s, v_w_proj_b, v_w_out, v_g_ffn, v_w_up, v_conv_w, v_conv_b, v_w_down, v_g_final):
    given = dict(locals())
    w = {n: given[n] for n in WEIGHTS}
    m = {n: given["m_" + n] for n in WEIGHTS}
    v = {n: given["v_" + n] for n in WEIGHTS}

    def shard2d(a):
        return a.reshape(a.shape[-2], a.shape[-1])

    chip = 2 * lax.axis_index("x") + lax.axis_index("y")
    where = jnp.stack([chip, lax.axis_index("c")]).astype(jnp.int32)
    xs, target = x[0], loss_target[0]
    small = {n: w[n].reshape(SMALL_SHAPES[n]) for n in SMALL}

    (abr, abi, bbr, bbi), disc_vjp = jax.vjp(_ssm_discretize, small["a_re"], small["a_im"],
                                             small["log_dt"], small["b_re"], small["b_im"])
    tab_f, tab_r = _scan_tables(abr, abi)
    bre = _block_diag_b(bbr)
    bim = _block_diag_b(bbi)
    cre = _block_diag_c(small["c_re"])
    cim = _block_diag_c(small["c_im"])
    tril = jnp.tril(jnp.ones((CHUNK, CHUNK), dtype=bool))
    ws = jnp.where(tril[None], small["w_s"], 0.0)
    ws_st = ws.reshape(SGU_G // 2, 2 * CHUNK, CHUNK).astype(BF16)
    wst_st = ws.transpose(0, 2, 1).reshape(SGU_G // 2, 2 * CHUNK, CHUNK).astype(BF16)
    bmat = jnp.repeat(small["b_s"].T, SGU_D, axis=1)
    g_mix2 = small["g_mix"].reshape(1, D_MODEL)
    g_ffn2 = small["g_ffn"].reshape(1, D_MODEL)
    g_final2 = small["g_final"].reshape(1, D_MODEL)
    g_sgu2 = small["g_sgu"].reshape(1, SGU_W)
    d_skip2 = small["d_skip"].reshape(1, SSM_W)
    b_glu2 = small["b_glu"].reshape(1, SSM_W)
    conv_b2 = small["conv_b"].reshape(1, 2 * D_FF)

    gat = {"w_in": _place_shard(shard2d(w["w_in"]), where, "place_w_in", BF16, PLACE_ROWS["w_in"])}
    gat.update(zip(BIG[1:], _place_shards([shard2d(w[n]) for n in BIG[1:]], where, "place_rest", BF16)))
    gat["conv_w"] = _place_shard(shard2d(w["conv_w"]), where, "place_conv_w", F32, 3)
    all_rows = (0, D_MODEL)
    (gat["w_in"],), = _comm("gather_in", [_job_gather(
        [gat["w_in"]], [(0, all_rows, ICI, (0.0, 0.5)), (0, all_rows, SIBLING, (0.5, 1.0))])])
    mixers = ["w_glu", "w_proj_a", "w_proj_b", "w_out"]
    rows = {n: (0, gat[n].shape[1]) for n in mixers}
    down_a, down_b = (0, D_FF // 8), (D_FF // 8, D_FF // 8)
    up_a, up_b = (0, 3 * D_MODEL // 8), (3 * D_MODEL // 8, 5 * D_MODEL // 8)
    span = (0.0, 1.0)

    names = mixers + ["conv_w", "w_down"]
    (p, h1, bur, bui), (got,) = _fwd_in(
        xs, g_mix2, gat["w_in"], bre, bim,
        [_job_gather([gat[n] for n in names],
                     [(i, rows[n], ICI, span) for i, n in enumerate(mixers)]
                     + [(4, None, ICI, span), (5, down_a, ICI, span)])])
    gat.update(zip(names, got))
    names = mixers + ["w_down", "w_up"]
    (str_, sti), (got,) = _scan_fwd(
        bur, bui, tab_f,
        [_job_gather([gat[n] for n in names],
                     [(i, rows[n], SIBLING, span) for i, n in enumerate(mixers)]
                     + [(4, down_a, SIBLING, span), (4, down_b, ICI, span), (5, up_a, ICI, span)])])
    gat.update(zip(names, got))
    w_glu_f = gat["w_glu"].reshape(SSM_W, SSM_W)
    w_out_f = gat["w_out"].reshape(D_MODEL, D_MODEL)
    conv_w_f = gat["conv_w"].transpose(1, 0, 2).reshape(3, 2 * D_FF)
    (x2, y0, z, mixed, ya, yb), ((gat["w_down"], gat["w_up"]),) = _fwd_mix(
        xs, p, str_, sti, cre, cim, d_skip2, w_glu_f, b_glu2, gat["w_proj_a"], g_sgu2, ws_st, bmat,
        gat["w_proj_b"], w_out_f,
        [_job_gather([gat["w_down"], gat["w_up"]],
                     [(0, down_b, SIBLING, span), (1, up_a, SIBLING, span),
                      (1, up_b, ICI, (0.0, 0.75)), (1, up_b, SIBLING, (0.75, 1.0))])])
    w_down_f = gat["w_down"].reshape(D_FF, D_MODEL)
    up, act, f, h2, dx3, sm_ffn = _fwd_ffn(x2, target, g_ffn2, gat["w_up"], conv_w_f, conv_b2,
                                           w_down_f, g_final2)

    def leg1_done(names, got):
        return _add_sibling([part[n] for n in names], got, where, "add_sibling_" + names[0])

    def leg2_done(names, sums, got):
        return _add_chips(sums, got, where, "add_chips_" + names[0])

    part, red = {}, {}
    part["w_down"] = _dw_rows(f, dx3, "dw_down", D_FF // 2, 4 * TK).reshape(
        N_CHIP, D_FF // N_CHIP, D_MODEL)
    (dx2, dup, sm_conv, sm_gffn), (got,) = _bwd_ffn(
        dx3, up, act, x2, g_ffn2, gat["w_up"], conv_w_f, w_down_f,
        [_job_sibling_halves([part["w_down"]])])
    sum_down = leg1_done(["w_down"], got)
    part["w_up"], (got,) = _dw_shards(h2, dup, "dw_up", 4 * TK, [_job_to_owner(sum_down)])
    red_down = leg2_done(["w_down"], sum_down, got)
    ((dsr, dsi, du_part, drest, mrg, dya, dyb, yap, dz, y1, sgu, dy0, sm_mix, dbm, dws),
     (got, (red["w_down"],))) = _bwd_mix(
        dx2, p, y0, z, mixed, ya, yb, w_out_f, gat["w_proj_a"], gat["w_proj_b"], w_glu_f, cre, cim,
        ws_st, wst_st, d_skip2, g_sgu2,
        [_job_sibling_halves([part["w_up"]]), _job_swap_halves(red_down)])
    sum_up = leg1_done(["w_up"], got)
    (lam_r, lam_i, dar8, dai8), (got,) = _scan_bwd(dsr, dsi, str_, sti, tab_r, [_job_to_owner(sum_up)])
    red_up = leg2_done(["w_up"], sum_up, got)
    mix4 = ["w_out", "w_proj_a", "w_proj_b", "w_glu"]
    part["w_out"] = _dw_cols(mrg, dx2, "dw_out", D_MODEL // 2, False)[0].reshape(
        N_CHIP, D_MODEL // N_CHIP, D_MODEL)
    part["w_proj_a"] = _dw_cols(yap, dya, "dw_proj_a", D_MODEL // N_CHIP, True)[0]
    part["w_proj_b"] = _dw_cols(sgu, dyb, "dw_proj_b", D_MODEL // N_CHIP, True)[0]
    part["w_glu"] = _dw_cols(y1, dz, "dw_glu", SSM_W, False)[0].reshape(
        N_CHIP, SSM_W // N_CHIP, SSM_W)
    (grad_x, dp, sm_gmix), (got, (red["w_up"],)) = _bwd_in(
        lam_r, lam_i, du_part, drest, xs, dx2, g_mix2, gat["w_in"], bre, bim,
        [_job_sibling_halves([part[n] for n in mix4]), _job_swap_halves(red_up)])
    sums_m = leg1_done(mix4, got)
    part["w_in"], (got,) = _dw_cols(h1, dp, "dw_in", IN_COLS // N_CHIP, True, [_job_to_owner(sums_m)])
    red_m = leg2_done(mix4, sums_m, got)
    (dbd_r, dbd_i), (got, done_m) = _dw_pair(
        p, SSM_W, lam_r, lam_i, "db_bar",
        [_job_sibling_halves([part["w_in"]]), _job_swap_halves(red_m)])
    red.update(zip(mix4, done_m))
    sum_in = leg1_done(["w_in"], got)
    (dcd_r, dcd_i), (got,) = _dw_pair(dy0, SSM_W, str_, sti, "dc", [_job_to_owner(sum_in)])
    red_in = leg2_done(["w_in"], sum_in, got)
    (red["w_in"],), = _comm("swap_w_in", [_job_swap_halves(red_in)])

    def pick_c(slabs):
        two = LANE // SSM_P
        return jnp.einsum("jshsp->jshp", slabs.reshape(SSM_G // two, two, SSM_H, two, SSM_P)
                          ).reshape(SSM_G, SSM_H, SSM_P)

    def pick_b(slabs):
        return pick_c(slabs).transpose(0, 2, 1)

    dabr = jnp.sum(dar8, axis=0).reshape(SSM_G, SSM_P)
    dabi = jnp.sum(dai8, axis=0).reshape(SSM_G, SSM_P)
    d_a_re, d_a_im, d_log_dt, d_b_re, d_b_im = disc_vjp((dabr, dabi, pick_b(dbd_r), pick_b(dbd_i)))
    gsmall = {
        "g_mix": sm_gmix[0], "a_re": d_a_re, "a_im": d_a_im, "log_dt": d_log_dt,
        "b_re": d_b_re, "b_im": d_b_im, "c_re": pick_c(dcd_r), "c_im": -pick_c(dcd_i),
        "d_skip": sm_mix[0], "b_glu": sm_mix[1], "g_sgu": sm_mix[2],
        "w_s": jnp.where(tril[None], dws, 0.0),
        "b_s": dbm.reshape(CHUNK, SGU_G, SGU_D).sum(-1).T,
        "g_ffn": sm_gffn[0], "conv_b": sm_conv[3], "g_final": sm_ffn[0],
        "conv_w": sm_conv[0:3], "loss": sm_ffn[1, 0:1],
    }

    total_pack = _small_allreduce(_pack(gsmall))
    total = _unpack(total_pack)
    grads = dict(red)
    cs = 2 * D_FF // N_CHIP
    grads["conv_w"] = lax.dynamic_slice(total["conv_w"], (0, chip * cs), (3, cs))
    delta, new_m, new_v = {}, {}, {}
    for n in BIG + ("conv_w",):
        grads[n], delta[n], new_m[n], new_v[n] = _adamw(
            shard2d(w[n]), grads[n], shard2d(m[n]), shard2d(v[n]), "adamw_" + n, PLACE_ROWS.get(n, 3))
    for n in SMALL:
        grads[n] = total[n].reshape(w[n].shape)
    ud, um, uv = _adamw_many(*[[d[n] for n in SMALL] for d in (w, grads, m, v)], "adamw_small")
    for i, n in enumerate(SMALL):
        delta[n], new_m[n], new_v[n] = ud[i], um[i], uv[i]

    def like(d):
        return [d[n].reshape(w[n].shape) for n in WEIGHTS]

    return (total["loss"].reshape(()), grad_x.reshape(x.shape), *like(grads), *like(delta),
            *like(new_m), *like(new_v))
```

```python
import math

import jax
import jax.numpy as jnp
from jax import lax
from jax.experimental import pallas as pl
from jax.experimental.pallas import tpu as pltpu

F32 = jnp.float32
BF16 = jnp.bfloat16
MESH = pl.DeviceIdType.MESH

D_MODEL = 1024
SSM_W = 512
SSM_G = 32
SSM_H = 16
SSM_P = 64
N_STATE = SSM_G * SSM_P
DIAG_N = 128 * SSM_P // SSM_H
SGU_W = 512
SGU_G = 8
SGU_D = 64
CHUNK = 128
D_FF = 2816
IN_COLS = 3584
EPS = 1e-6
N_CHIP = 4

ADAM_LR = 0.001
ADAM_B1 = 0.9
ADAM_B2 = 0.999
ADAM_EPS = 1e-08
ADAM_WD = 0.01
ADAM_STEP = 10

SUBLANE = 8
LANE = 128
VMEM_LIMIT = 56 * 1024 * 1024
TB = 256
TK = 512
SCAN_LANES = 256
SCAN_UNROLL = 4
HALO = SUBLANE

BIG = ("w_in", "w_up", "w_down", "w_out", "w_proj_a", "w_proj_b", "w_glu")
SMALL = ("g_mix", "a_re", "a_im", "log_dt", "b_re", "b_im", "c_re", "c_im", "d_skip", "b_glu",
         "g_sgu", "w_s", "b_s", "g_ffn", "conv_b", "g_final")
WEIGHTS = ("g_mix", "w_in", "a_re", "a_im", "log_dt", "b_re", "b_im", "c_re", "c_im", "d_skip",
           "w_glu", "b_glu", "w_proj_a", "g_sgu", "w_s", "b_s", "w_proj_b", "w_out", "g_ffn",
           "w_up", "conv_w", "conv_b", "w_down", "g_final")

ANY = pl.BlockSpec(memory_space=pl.ANY)


def _params(n_grid):
    return pltpu.CompilerParams(dimension_semantics=("arbitrary",) * n_grid if n_grid else None,
                                vmem_limit_bytes=VMEM_LIMIT)


def _whole():
    return pl.BlockSpec(memory_space=pltpu.VMEM)


def _rows(tb, ncol):
    return pl.BlockSpec((tb, ncol), lambda i: (i, 0))


def _acc(nrow, ncol):
    return pl.BlockSpec((nrow, ncol), lambda i: (0, 0))


def _dot(a, b):
    return jnp.dot(a.astype(BF16), b.astype(BF16), preferred_element_type=F32)


def _dot_nt(a, b):
    return lax.dot_general(a.astype(BF16), b.astype(BF16), (((1,), (1,)), ((), ())),
                           preferred_element_type=F32)


def _sigmoid(v):
    return 0.5 * jnp.tanh(0.5 * v) + 0.5


_GELU_C = math.sqrt(2.0 / math.pi)


def _gelu(v):
    return 0.5 * v * (1.0 + jnp.tanh(_GELU_C * (v + 0.044715 * v * v * v)))


def _gelu_and_grad(v):
    v2 = v * v
    t = jnp.tanh(_GELU_C * v * (1.0 + 0.044715 * v2))
    half = 0.5 * (1.0 + t)
    return v * half, half + 0.5 * v * (1.0 - t * t) * _GELU_C * (1.0 + 3.0 * 0.044715 * v2)


def _rms_stats(v):
    r = lax.rsqrt(jnp.mean(v * v, axis=-1, keepdims=True) + EPS)
    return r, v * r


def _rms_bwd(dxh, xh, r):
    return r * (dxh - xh * jnp.mean(dxh * xh, axis=-1, keepdims=True))


def _place():
    x, y, c = lax.axis_index("x"), lax.axis_index("y"), lax.axis_index("c")
    chips = [(1 - x, y), (x, 1 - y), (1 - x, 1 - y)]
    return x, y, c, chips


def _chip_index(chip):
    return 2 * chip[0] + chip[1]


def _remote(src, dst, send_sem, recv_sem, device):
    return pltpu.make_async_remote_copy(src_ref=src, dst_ref=dst, send_sem=send_sem,
                                        recv_sem=recv_sem, device_id=device, device_id_type=MESH)


def _half(ref_rows, c):
    hr = ref_rows // 2
    return pl.ds(pl.multiple_of(c * hr, SUBLANE), hr)


class _Job:
    def __init__(self, hooks, n_sem, ins=(), inouts=(), outs=()):
        self.hooks, self.n_sem = list(hooks), n_sem
        self.ins, self.inouts, self.outs = list(ins), list(inouts), list(outs)


def _whole_span(start, finish):
    return [(0.0, "start", start), (1.0, "finish", finish)]


ICI, SIBLING = "ici", "sibling"


def _job_gather(bufs, legs):
    def copies(io, leg, first):
        b, window, kind, _ = legs[leg]
        x, y, c, chips = _place()
        k_me = 2 * x + y
        out = []
        for j, ch in enumerate(chips):
            k = _chip_index(ch)
            if window is None:
                src, land, dev = io[b].at[k_me], io[b].at[k], (*ch, c)
            else:
                r0, rows = window
                mine = pl.ds(pl.multiple_of(r0 + c * (rows // 2), SUBLANE), rows // 2)
                theirs = pl.ds(pl.multiple_of(r0 + (1 - c) * (rows // 2), SUBLANE), rows // 2)
                if kind == ICI:
                    src, land, dev = io[b].at[k_me, mine, :], io[b].at[k, mine, :], (*ch, c)
                else:
                    src, land, dev = io[b].at[k, mine, :], io[b].at[k, theirs, :], (x, y, 1 - c)
            out.append((src, land, first + j, dev))
        return out

    def starter(leg):
        def start(ins, io, outs, ssem, rsem):
            for src, _, i, dev in copies(io, leg, 3 * leg):
                _remote(src, src, ssem(i), rsem(i), dev).start()
        return start

    def finisher(leg):
        def finish(ins, io, outs, ssem, rsem):
            cps = copies(io, leg, 3 * leg)
            for _, land, i, dev in cps:
                _remote(land, land, ssem(i), rsem(i), dev).wait_recv()
            for src, _, i, dev in cps:
                _remote(src, src, ssem(i), rsem(i), dev).wait_send()
        return finish

    hooks = []
    for leg, (_, _, _, (begin, end)) in enumerate(legs):
        hooks += [(begin, "start", starter(leg)), (end, "finish", finisher(leg))]
    return _Job(hooks, 3 * len(legs), inouts=bufs)


def _job_sibling_halves(grads):
    n = len(grads)

    def build(ins, outs, ssem, rsem):
        x, y, c, _ = _place()
        return [_remote(ins[t].at[:, _half(grads[t].shape[1], 1 - c), :], outs[t], ssem(t), rsem(t),
                        (x, y, 1 - c)) for t in range(n)]

    def start(ins, io, outs, ssem, rsem):
        for cp in build(ins, outs, ssem, rsem):
            cp.start()

    def finish(ins, io, outs, ssem, rsem):
        for cp in build(ins, outs, ssem, rsem):
            cp.wait()

    return _Job(_whole_span(start, finish), n, ins=grads,
                outs=[jax.ShapeDtypeStruct((N_CHIP, g.shape[1] // 2, g.shape[2]), F32) for g in grads])


def _job_to_owner(sums, window=None, into=None):
    n = len(sums)

    def build(ins, io, outs, ssem, rsem):
        got = outs if into is None else io
        x, y, c, chips = _place()
        cps = []
        for t in range(n):
            rows = pl.ds(*(window if window is not None else (0, sums[t].shape[1])))
            for j, ch in enumerate(chips):
                cps.append(_remote(ins[t].at[_chip_index(ch), rows, :], got[t].at[j, rows, :],
                                   ssem(3 * t + j), rsem(3 * t + j), (*ch, c)))
        return cps

    def start(ins, io, outs, ssem, rsem):
        for cp in build(ins, io, outs, ssem, rsem):
            cp.start()

    def finish(ins, io, outs, ssem, rsem):
        for cp in build(ins, io, outs, ssem, rsem):
            cp.wait()

    if into is not None:
        return _Job(_whole_span(start, finish), 3 * n, ins=sums, inouts=into)
    return _Job(_whole_span(start, finish), 3 * n, ins=sums,
                outs=[jax.ShapeDtypeStruct((3,) + s.shape[1:], s.dtype) for s in sums])


def _job_swap_halves(bufs):
    n = len(bufs)

    def start(ins, io, outs, ssem, rsem):
        x, y, c, _ = _place()
        for t in range(n):
            mine = io[t].at[_half(bufs[t].shape[0], c), :]
            _remote(mine, mine, ssem(t), rsem(t), (x, y, 1 - c)).start()

    def finish(ins, io, outs, ssem, rsem):
        x, y, c, _ = _place()
        for t in range(n):
            theirs = io[t].at[_half(bufs[t].shape[0], 1 - c), :]
            _remote(theirs, theirs, ssem(t), rsem(t), (x, y, 1 - c)).wait_recv()
        for t in range(n):
            mine = io[t].at[_half(bufs[t].shape[0], c), :]
            _remote(mine, mine, ssem(t), rsem(t), (x, y, 1 - c)).wait_send()

    return _Job(_whole_span(start, finish), n, inouts=bufs)


def _call(body, name, grid, in_specs, out_specs, out_shape, args, jobs=(), scratch=()):
    n_in, n_out, n_scr = len(args), len(out_shape), len(scratch)
    job_in = [a for jb in jobs for a in jb.ins + jb.inouts]
    job_out = [s for jb in jobs
               for s in [jax.ShapeDtypeStruct(a.shape, a.dtype) for a in jb.inouts] + jb.outs]
    aliases, pos_in, pos_out = {}, n_in, n_out
    for jb in jobs:
        pos_in += len(jb.ins)
        for _ in jb.inouts:
            aliases[pos_in] = pos_out
            pos_in += 1
            pos_out += 1
        pos_out += len(jb.outs)
    n_sem = sum(jb.n_sem for jb in jobs)

    def wrapped(*refs):
        c_in = refs[:n_in]
        j_in = refs[n_in:n_in + len(job_in)]
        c_out = refs[n_in + len(job_in):n_in + len(job_in) + n_out]
        j_out = refs[n_in + len(job_in) + n_out:n_in + len(job_in) + n_out + len(job_out)]
        rest = refs[n_in + len(job_in) + n_out + len(job_out):]
        c_scr = rest[:n_scr]
        views, pi, po, ps = [], 0, 0, 0
        for jb in jobs:
            ins = j_in[pi:pi + len(jb.ins)]
            pi += len(jb.ins) + len(jb.inouts)
            io = j_out[po:po + len(jb.inouts)]
            new = j_out[po + len(jb.inouts):po + len(jb.inouts) + len(jb.outs)]
            po += len(jb.inouts) + len(jb.outs)
            send = (lambda i, o=ps: rest[n_scr].at[o + i])
            recv = (lambda i, o=ps: rest[n_scr + 1].at[o + i])
            ps += jb.n_sem
            views.append((ins, io, new, send, recv))

        def run(frac):
            for kind in ("finish", "start"):
                for jb, vw in zip(jobs, views):
                    for at, what, fn in jb.hooks:
                        if at == frac and what == kind:
                            fn(*vw)

        fracs = sorted({at for jb in jobs for at, _, _ in jb.hooks})
        if not grid:
            for frac in fracs:
                run(frac)
            return
        if jobs:
            assert len(grid) == 1 or set(fracs) <= {0.0, 1.0}
            first = pl.program_id(0) == 0
            last = pl.program_id(0) == grid[0] - 1
            for d in range(1, len(grid)):
                first = jnp.logical_and(first, pl.program_id(d) == 0)
                last = jnp.logical_and(last, pl.program_id(d) == grid[d] - 1)
            for frac in fracs:
                if frac < 1.0:
                    at_step = first if frac == 0.0 else pl.program_id(0) == int(frac * grid[0])
                    pl.when(at_step)(lambda frac=frac: run(frac))
        body(*c_in, *c_out, *c_scr)
        if jobs and 1.0 in fracs:
            pl.when(last)(lambda: run(1.0))

    sems = [pltpu.SemaphoreType.DMA((n_sem,)), pltpu.SemaphoreType.DMA((n_sem,))] if jobs else []
    kwargs = dict(grid=grid) if grid else {}
    res = pl.pallas_call(
        wrapped, name=name, in_specs=list(in_specs) + [ANY] * len(job_in),
        out_specs=list(out_specs) + [ANY] * len(job_out),
        out_shape=list(out_shape) + job_out, scratch_shapes=list(scratch) + sems,
        input_output_aliases=aliases, compiler_params=_params(len(grid)), **kwargs,
    )(*args, *job_in)
    outs, pos, per_job = list(res[:n_out]), n_out, []
    for jb in jobs:
        k = len(jb.inouts) + len(jb.outs)
        per_job.append(list(res[pos:pos + k]))
        pos += k
    return outs, per_job


def _comm(name, jobs):
    return _call(None, name, (), [], [], [], [], jobs)[1]


def _fwd_in(x, g_mix, w_in, bre, bim, jobs=()):
    t_len = x.shape[0]
    cs = IN_COLS // N_CHIP

    def body(x_ref, g_ref, w_ref, bre_ref, bim_ref, p_ref, h_ref, bur_ref, bui_ref):
        xv = x_ref[...]
        r, xh = _rms_stats(xv)
        h = (xh * g_ref[...]).astype(BF16)
        h_ref[...] = h
        for k in range(N_CHIP):
            p_ref[:, k * cs:(k + 1) * cs] = jnp.dot(h, w_ref[k],
                                                    preferred_element_type=F32).astype(BF16)
        u = p_ref[:, 0:SSM_W]
        for i in range(SSM_W // LANE):
            rows, cols = slice(i * LANE, (i + 1) * LANE), slice(i * DIAG_N, (i + 1) * DIAG_N)
            bur_ref[:, cols] = jnp.dot(u[:, rows], bre_ref[rows, cols],
                                       preferred_element_type=F32).astype(BF16)
            bui_ref[:, cols] = jnp.dot(u[:, rows], bim_ref[rows, cols],
                                       preferred_element_type=F32).astype(BF16)

    return _call(
        body, "fwd_in", (t_len // TB,),
        [_rows(TB, D_MODEL), _whole(), _whole(), _whole(), _whole()],
        [_rows(TB, IN_COLS), _rows(TB, D_MODEL), _rows(TB, N_STATE), _rows(TB, N_STATE)],
        [jax.ShapeDtypeStruct((t_len, IN_COLS), BF16), jax.ShapeDtypeStruct((t_len, D_MODEL), BF16),
         jax.ShapeDtypeStruct((t_len, N_STATE), BF16), jax.ShapeDtypeStruct((t_len, N_STATE), BF16)],
        [x, g_mix, w_in, bre, bim], jobs)


def _scan_local(xr, xi, tab, shifts):
    for q, s in enumerate(shifts):
        ar, ai = tab[2 * q], tab[2 * q + 1]
        rr = pltpu.roll(xr, s, 0)
        ri = pltpu.roll(xi, s, 0)
        xr, xi = xr + ar * rr - ai * ri, xi + ar * ri + ai * rr
    return xr, xi


def _scan_carry(xr, xi, tab, cr, ci):
    pr, pi = tab[6], tab[7]
    return xr + pr * cr - pi * ci, xi + pr * ci + pi * cr


BF16_TILE = 2 * SUBLANE


def _load_blocks(r_ref, i_ref, base):
    out = []
    for q in range(SCAN_UNROLL // 2):
        rows = pl.ds(pl.multiple_of(base + q * BF16_TILE, BF16_TILE), BF16_TILE)
        vr, vi = r_ref[rows, :].astype(F32), i_ref[rows, :].astype(F32)
        out += [(vr[:SUBLANE], vi[:SUBLANE]), (vr[SUBLANE:], vi[SUBLANE:])]
    return out


def _store_blocks(r_ref, i_ref, base, blocks):
    for q in range(SCAN_UNROLL // 2):
        rows = pl.ds(pl.multiple_of(base + q * BF16_TILE, BF16_TILE), BF16_TILE)
        r_ref[rows, :] = jnp.concatenate([blocks[2 * q][0], blocks[2 * q + 1][0]], 0).astype(r_ref.dtype)
        i_ref[rows, :] = jnp.concatenate([blocks[2 * q][1], blocks[2 * q + 1][1]], 0).astype(i_ref.dtype)


def _scan_fwd(bur, bui, tab, jobs=()):
    t_len = bur.shape[0]
    nblk = t_len // SUBLANE
    lb = SCAN_LANES

    def body(br_ref, bi_ref, tab_ref, sr_ref, si_ref):
        tab_v = [tab_ref[q] for q in range(8)]

        def step(k, carry):
            cr, ci = carry
            base = pl.multiple_of(k * SCAN_UNROLL * SUBLANE, SCAN_UNROLL * SUBLANE)
            local = [_scan_local(xr, xi, tab_v, (1, 2, 4))
                     for xr, xi in _load_blocks(br_ref, bi_ref, base)]
            done = []
            for xr, xi in local:
                xr, xi = _scan_carry(xr, xi, tab_v, cr, ci)
                done.append((xr, xi))
                cr, ci = xr[SUBLANE - 1:SUBLANE, :], xi[SUBLANE - 1:SUBLANE, :]
            _store_blocks(sr_ref, si_ref, base, done)
            return cr, ci

        zero = jnp.zeros((1, lb), F32)
        lax.fori_loop(0, nblk // SCAN_UNROLL, step, (zero, zero))

    col = pl.BlockSpec((t_len, lb), lambda j: (0, j))
    return _call(
        body, "scan_fwd", (N_STATE // lb,),
        [col, col, pl.BlockSpec((8, SUBLANE, lb), lambda j: (0, 0, j))], [col, col],
        [jax.ShapeDtypeStruct((t_len, N_STATE), BF16)] * 2, [bur, bui, tab], jobs)


def _sgu_mix(v, ws_ref, lane_lo):
    rows = []
    for c0 in range(0, v.shape[0], CHUNK):
        slabs = []
        for j in range(SGU_W // LANE):
            prod = jnp.dot(ws_ref[j], v[c0:c0 + CHUNK, j * LANE:(j + 1) * LANE].astype(BF16),
                           preferred_element_type=F32)
            slabs.append(jnp.where(lane_lo, prod[:CHUNK], prod[CHUNK:]))
        rows.append(jnp.concatenate(slabs, axis=1))
    return jnp.concatenate(rows, axis=0) if len(rows) > 1 else rows[0]


def _fwd_mix(x, p, str_, sti, cre, cim, d_skip, w_glu, b_glu, w_pa, g_sgu, ws_st, bmat, w_pb, w_out,
             jobs=()):
    t_len = x.shape[0]

    def body(x_ref, p_ref, sr_ref, si_ref, cre_ref, cim_ref, dsk_ref, wg_ref, bg_ref, wpa_ref,
             gs_ref, ws_ref, bm_ref, wpb_ref, wo_ref,
             x2_ref, y0_ref, z_ref, mx_ref, ya_ref, yb_ref):
        u = p_ref[:, 0:SSM_W].astype(F32)
        y0 = jnp.concatenate(
            [_dot(sr_ref[:, i * DIAG_N:(i + 1) * DIAG_N],
                  cre_ref[i * DIAG_N:(i + 1) * DIAG_N, i * LANE:(i + 1) * LANE])
             - _dot(si_ref[:, i * DIAG_N:(i + 1) * DIAG_N],
                    cim_ref[i * DIAG_N:(i + 1) * DIAG_N, i * LANE:(i + 1) * LANE])
             for i in range(SSM_W // LANE)], axis=1) + dsk_ref[...] * u
        y0_ref[...] = y0.astype(BF16)
        y1 = _gelu(y0)
        z = _dot(y1, wg_ref[...]) + bg_ref[...]
        z_ref[...] = z.astype(BF16)
        ya_pre = (y1 * _sigmoid(z)).astype(BF16)
        ya = jnp.concatenate([jnp.dot(ya_pre, wpa_ref[k], preferred_element_type=F32)
                              for k in range(N_CHIP)], axis=1)
        ya_ref[...] = ya.astype(BF16)

        uvg = _gelu(p_ref[:, SSM_W:SSM_W + 2 * SGU_W].astype(F32))
        u2 = uvg[:, :SGU_W]
        _, vh = _rms_stats(uvg[:, SGU_W:])
        v3 = vh * gs_ref[...]
        lane_lo = lax.broadcasted_iota(jnp.int32, (CHUNK, LANE), 1) < SGU_D
        bias = jnp.concatenate([bm_ref[...]] * (TB // CHUNK), axis=0)
        mixed = _sgu_mix(v3, ws_ref, lane_lo) + bias
        mx_ref[...] = mixed.astype(BF16)
        sgu = (u2 * mixed).astype(BF16)
        yb = jnp.concatenate([jnp.dot(sgu, wpb_ref[k], preferred_element_type=F32)
                              for k in range(N_CHIP)], axis=1)
        yb_ref[...] = yb.astype(BF16)

        lg0 = SSM_W + 2 * SGU_W
        ga = _sigmoid(p_ref[:, lg0:lg0 + D_MODEL].astype(F32))
        gb = _sigmoid(p_ref[:, lg0 + D_MODEL:lg0 + 2 * D_MODEL].astype(F32))
        mrg = ga * ya + gb * yb
        x2_ref[...] = x_ref[...] + _dot(mrg, wo_ref[...])

    return _call(
        body, "fwd_mix", (t_len // TB,),
        [_rows(TB, D_MODEL), _rows(TB, IN_COLS), _rows(TB, N_STATE), _rows(TB, N_STATE)]
        + [_whole()] * 11,
        [_rows(TB, D_MODEL), _rows(TB, SSM_W), _rows(TB, SSM_W), _rows(TB, SGU_W),
         _rows(TB, D_MODEL), _rows(TB, D_MODEL)],
        [jax.ShapeDtypeStruct((t_len, D_MODEL), F32), jax.ShapeDtypeStruct((t_len, SSM_W), BF16),
         jax.ShapeDtypeStruct((t_len, SSM_W), BF16), jax.ShapeDtypeStruct((t_len, SGU_W), BF16),
         jax.ShapeDtypeStruct((t_len, D_MODEL), BF16), jax.ShapeDtypeStruct((t_len, D_MODEL), BF16)],
        [x, p, str_, sti, cre, cim, d_skip, w_glu, b_glu, w_pa, g_sgu, ws_st, bmat, w_pb, w_out], jobs)


def _conv_taps(v, cw_ref, c0, width):
    w0 = cw_ref[0:1, c0:c0 + width]
    w1 = cw_ref[1:2, c0:c0 + width]
    w2 = cw_ref[2:3, c0:c0 + width]
    return w0 * pltpu.roll(v, 2, 0) + w1 * pltpu.roll(v, 1, 0) + w2 * v


def _fwd_ffn(x2, target, g_ffn, w_up, conv_w, conv_b, w_down, g_final):
    t_len = x2.shape[0]
    half = D_FF // 2
    blocks_per_halo = TB // HALO

    def body(x2_ref, xp_ref, tg_ref, gf_ref, wu_ref, cw_ref, cb_ref, wd_ref, gl_ref,
             up_ref, act_ref, f_ref, h2_ref, dx3_ref, sm_ref):
        i = pl.program_id(0)
        xe = jnp.concatenate([xp_ref[...] * jnp.where(i == 0, 0.0, 1.0), x2_ref[...]], axis=0)
        _, xh = _rms_stats(xe)
        h2 = (xh * gf_ref[...]).astype(BF16)
        h2_ref[...] = h2[HALO:]
        acc = jnp.zeros((TB, D_MODEL), F32)
        ups = [jnp.dot(h2, wu_ref[k], preferred_element_type=F32) for k in range(N_CHIP)]
        for hc in range(2):
            ca = hc * half
            cb = D_FF + hc * half
            ua, ub = ups[hc], ups[2 + hc]
            up_ref[:, ca:ca + half] = ua[HALO:].astype(BF16)
            up_ref[:, cb:cb + half] = ub[HALO:].astype(BF16)
            ac = _conv_taps(ua, cw_ref, ca, half)[HALO:] + cb_ref[:, ca:ca + half]
            bc = _conv_taps(ub, cw_ref, cb, half)[HALO:] + cb_ref[:, cb:cb + half]
            act_ref[:, ca:ca + half] = ac.astype(BF16)
            act_ref[:, cb:cb + half] = bc.astype(BF16)
            f = (ac * _sigmoid(ac) * bc).astype(BF16)
            f_ref[:, ca:ca + half] = f
            acc = acc + jnp.dot(f, wd_ref[ca:ca + half, :], preferred_element_type=F32)
        x3 = x2_ref[...] + acc
        r3, xh3 = _rms_stats(x3)
        err = xh3 * gl_ref[...] - tg_ref[...]
        dout = err * (1.0 / D_MODEL)
        dx3_ref[...] = _rms_bwd(dout * gl_ref[...], xh3, r3)
        dgl = jnp.sum(dout * xh3, axis=0, keepdims=True)
        loss = 0.5 * jnp.sum(jnp.mean(err * err, axis=-1, keepdims=True), axis=0, keepdims=True)
        upd = jnp.concatenate([dgl, jnp.broadcast_to(loss, (1, D_MODEL)),
                               jnp.zeros((SUBLANE - 2, D_MODEL), F32)], axis=0)

        @pl.when(i == 0)
        def _():
            sm_ref[...] = upd

        @pl.when(i > 0)
        def _():
            sm_ref[...] += upd

    prev = pl.BlockSpec((HALO, D_MODEL), lambda i: (jnp.maximum(i * blocks_per_halo - 1, 0), 0))
    return _call(
        body, "fwd_ffn", (t_len // TB,),
        [_rows(TB, D_MODEL), prev, _rows(TB, D_MODEL)] + [_whole()] * 6,
        [_rows(TB, 2 * D_FF), _rows(TB, 2 * D_FF), _rows(TB, D_FF), _rows(TB, D_MODEL),
         _rows(TB, D_MODEL), _acc(SUBLANE, D_MODEL)],
        [jax.ShapeDtypeStruct((t_len, 2 * D_FF), BF16), jax.ShapeDtypeStruct((t_len, 2 * D_FF), BF16),
         jax.ShapeDtypeStruct((t_len, D_FF), BF16), jax.ShapeDtypeStruct((t_len, D_MODEL), BF16),
         jax.ShapeDtypeStruct((t_len, D_MODEL), F32), jax.ShapeDtypeStruct((SUBLANE, D_MODEL), F32)],
        [x2, x2, target, g_ffn, w_up, conv_w, conv_b, w_down, g_final])[0]


def _bwd_ffn(dx3, up, act, x2, g_ffn, w_up, conv_w, w_down, jobs=()):
    t_len = x2.shape[0]
    half = D_FF // 2
    nblk = t_len // TB
    halo_b = 2 * HALO
    n_e = TB + HALO

    def body(dx_ref, dxn_ref, up_ref, act_ref, actn_ref, x2_ref, gf_ref, wu_ref, cw_ref,
             wd_ref, dx2_ref, dup_ref, smw_ref, smg_ref):
        i = pl.program_id(0)
        keep_last = jnp.where(i == nblk - 1, 0.0, 1.0)
        dxe = jnp.concatenate([dx_ref[...], dxn_ref[...] * keep_last], axis=0).astype(BF16)
        dh2 = jnp.zeros((TB, D_MODEL), F32)
        zpad = jnp.zeros((1, half), F32)
        dfs = [lax.dot_general(dxe, wd_ref[hc * half:(hc + 1) * half, :], (((1,), (1,)), ((), ())),
                               preferred_element_type=F32) for hc in range(2)]
        for hc in range(2):
            ca = hc * half
            cb = D_FF + hc * half
            ac = jnp.concatenate([act_ref[:, ca:ca + half].astype(F32),
                                  actn_ref[:, ca:ca + half].astype(F32)[:HALO]], axis=0)
            bc = jnp.concatenate([act_ref[:, cb:cb + half].astype(F32),
                                  actn_ref[:, cb:cb + half].astype(F32)[:HALO]], axis=0)
            wa = [cw_ref[k:k + 1, ca:ca + half] for k in range(3)]
            wb = [cw_ref[k:k + 1, cb:cb + half] for k in range(3)]
            df = dfs[hc]
            sg = _sigmoid(ac)
            da = df * bc * sg * (1.0 + ac * (1.0 - sg))
            db = df * ac * sg
            da1, da2 = pltpu.roll(da, n_e - 1, 0), pltpu.roll(da, n_e - 2, 0)
            db1, db2 = pltpu.roll(db, n_e - 1, 0), pltpu.roll(db, n_e - 2, 0)
            dua = (wa[2] * da + wa[1] * da1 + wa[0] * da2)[:TB]
            dub = (wb[2] * db + wb[1] * db1 + wb[0] * db2)[:TB]
            dup_ref[:, ca:ca + half] = dua.astype(BF16)
            dup_ref[:, cb:cb + half] = dub.astype(BF16)
            dh2 = dh2 + _dot_nt(dua, wu_ref[hc]) + _dot_nt(dub, wu_ref[2 + hc])
            rows = []
            for u_, d0, d1, d2 in ((up_ref[:, ca:ca + half].astype(F32), da, da1, da2),
                                   (up_ref[:, cb:cb + half].astype(F32), db, db1, db2)):
                rows.append([jnp.sum(u_ * d2[:TB], axis=0, keepdims=True),
                             jnp.sum(u_ * d1[:TB], axis=0, keepdims=True),
                             jnp.sum(u_ * d0[:TB], axis=0, keepdims=True),
                             jnp.sum(d0[:TB], axis=0, keepdims=True)])
            for c0, rws in ((ca, rows[0]), (cb, rows[1])):
                upd = jnp.concatenate(rws + [zpad] * (SUBLANE - 4), axis=0)

                @pl.when(i == 0)
                def _(upd=upd, c0=c0):
                    smw_ref[:, c0:c0 + half] = upd

                @pl.when(i > 0)
                def _(upd=upd, c0=c0):
                    smw_ref[:, c0:c0 + half] += upd

        r2, xh2 = _rms_stats(x2_ref[...])
        dx2_ref[...] = dx_ref[...] + _rms_bwd(dh2 * gf_ref[...], xh2, r2)
        updg = jnp.concatenate([jnp.sum(dh2 * xh2, axis=0, keepdims=True),
                                jnp.zeros((SUBLANE - 1, D_MODEL), F32)], axis=0)

        @pl.when(i == 0)
        def _():
            smg_ref[...] = updg

        @pl.when(i > 0)
        def _():
            smg_ref[...] += updg

    nxt_d = pl.BlockSpec((HALO, D_MODEL),
                         lambda i: (jnp.minimum((i + 1) * (TB // HALO), t_len // HALO - 1), 0))
    nxt_a = pl.BlockSpec((halo_b, 2 * D_FF),
                         lambda i: (jnp.minimum((i + 1) * (TB // halo_b), t_len // halo_b - 1), 0))
    return _call(
        body, "bwd_ffn", (nblk,),
        [_rows(TB, D_MODEL), nxt_d, _rows(TB, 2 * D_FF), _rows(TB, 2 * D_FF), nxt_a,
         _rows(TB, D_MODEL)] + [_whole()] * 4,
        [_rows(TB, D_MODEL), _rows(TB, 2 * D_FF), _acc(SUBLANE, 2 * D_FF), _acc(SUBLANE, D_MODEL)],
        [jax.ShapeDtypeStruct((t_len, D_MODEL), F32), jax.ShapeDtypeStruct((t_len, 2 * D_FF), BF16),
         jax.ShapeDtypeStruct((SUBLANE, 2 * D_FF), F32), jax.ShapeDtypeStruct((SUBLANE, D_MODEL), F32)],
        [dx3, dx3, up, act, act, x2, g_ffn, w_up, conv_w, w_down], jobs)


def _bwd_mix(dx2, p, y0, z, mixed, ya, yb, w_out, w_pa, w_pb, w_glu, cre, cim, ws_st, wst_st,
             d_skip, g_sgu, jobs=()):
    t_len = dx2.shape[0]
    pc = D_MODEL // N_CHIP
    n_slab = SGU_W // LANE

    def body(dx_ref, p_ref, y0_ref, z_ref, mx_ref, ya_ref, yb_ref, wo_ref, wpa_ref, wpb_ref,
             wg_ref, cre_ref, cim_ref, ws_ref, wst_ref, dsk_ref, gs_ref,
             dsr_ref, dsi_ref, du_ref, drest_ref, mrg_ref, dya_ref, dyb_ref, yap_ref, dz_ref,
             y1_ref, sgu_ref, dy0_ref, sm_ref, dbm_ref, dws_ref):
        i = pl.program_id(0)
        first = i == 0
        lg0 = SSM_W + 2 * SGU_W
        ga = _sigmoid(p_ref[:, lg0:lg0 + D_MODEL].astype(F32))
        gb = _sigmoid(p_ref[:, lg0 + D_MODEL:lg0 + 2 * D_MODEL].astype(F32))
        yav = ya_ref[...].astype(F32)
        ybv = yb_ref[...].astype(F32)
        mrg_ref[...] = (ga * yav + gb * ybv).astype(BF16)
        y0v = y0_ref[...].astype(F32)
        y1, y1_grad = _gelu_and_grad(y0v)
        sz = _sigmoid(z_ref[...].astype(F32))
        y1_ref[...] = y1.astype(BF16)
        yap_ref[...] = (y1 * sz).astype(BF16)

        dmrg = _dot_nt(dx_ref[...], wo_ref[...])
        drest_ref[:, 2 * SGU_W:2 * SGU_W + D_MODEL] = (dmrg * yav * ga * (1.0 - ga)).astype(BF16)
        drest_ref[:, 2 * SGU_W + D_MODEL:] = (dmrg * ybv * gb * (1.0 - gb)).astype(BF16)
        dya = (dmrg * ga).astype(BF16)
        dyb = (dmrg * gb).astype(BF16)
        dya_ref[...] = dya
        dyb_ref[...] = dyb

        dyap = jnp.zeros((TB, SSM_W), F32)
        for k in range(N_CHIP):
            dyap = dyap + _dot_nt(dya[:, k * pc:(k + 1) * pc], wpa_ref[k])
        dz = dyap * y1 * sz * (1.0 - sz)
        dz_ref[...] = dz.astype(BF16)
        dy0 = (dyap * sz + _dot_nt(dz, wg_ref[...])) * y1_grad
        dy0_ref[...] = dy0.astype(BF16)
        u = p_ref[:, 0:SSM_W].astype(F32)
        du_ref[...] = dy0 * dsk_ref[...]
        for q in range(SSM_W // LANE):
            rows, cols = slice(q * DIAG_N, (q + 1) * DIAG_N), slice(q * LANE, (q + 1) * LANE)
            dsr_ref[:, rows] = _dot_nt(dy0[:, cols], cre_ref[rows, cols]).astype(BF16)
            dsi_ref[:, rows] = (-_dot_nt(dy0[:, cols], cim_ref[rows, cols])).astype(BF16)

        uv = p_ref[:, SSM_W:lg0].astype(F32)
        uvg, gg = _gelu_and_grad(uv)
        u2 = uvg[:, :SGU_W]
        rv, vh = _rms_stats(uvg[:, SGU_W:])
        v3 = vh * gs_ref[...]
        mixed = mx_ref[...].astype(F32)
        dsgu = jnp.zeros((TB, SGU_W), F32)
        for k in range(N_CHIP):
            dsgu = dsgu + _dot_nt(dyb[:, k * pc:(k + 1) * pc], wpb_ref[k])
        sgu_ref[...] = (u2 * mixed).astype(BF16)
        drest_ref[:, 0:SGU_W] = (dsgu * mixed * gg[:, :SGU_W]).astype(BF16)
        dmix = dsgu * u2
        lane_lo = lax.broadcasted_iota(jnp.int32, (CHUNK, LANE), 1) < SGU_D
        dv3 = _sgu_mix(dmix, wst_ref, lane_lo)
        dbm = jnp.zeros((CHUNK, SGU_W), F32)
        for c0 in range(0, TB, CHUNK):
            dbm = dbm + dmix[c0:c0 + CHUNK]
        for j in range(n_slab):
            lo = jnp.zeros((CHUNK, CHUNK), F32)
            hi = jnp.zeros((CHUNK, CHUNK), F32)
            for c0 in range(0, TB, CHUNK):
                dsl = dmix[c0:c0 + CHUNK, j * LANE:(j + 1) * LANE]
                vsl = v3[c0:c0 + CHUNK, j * LANE:(j + 1) * LANE]
                lo = lo + _dot_nt(jnp.where(lane_lo, dsl, 0.0), vsl)
                hi = hi + _dot_nt(jnp.where(lane_lo, 0.0, dsl), vsl)

            @pl.when(first)
            def _(lo=lo, hi=hi, j=j):
                dws_ref[2 * j] = lo
                dws_ref[2 * j + 1] = hi

            @pl.when(jnp.logical_not(first))
            def _(lo=lo, hi=hi, j=j):
                dws_ref[2 * j] += lo
                dws_ref[2 * j + 1] += hi

        dv2 = _rms_bwd(dv3 * gs_ref[...], vh, rv)
        drest_ref[:, SGU_W:2 * SGU_W] = (dv2 * gg[:, SGU_W:]).astype(BF16)

        upd = jnp.concatenate([jnp.sum(dy0 * u, axis=0, keepdims=True),
                               jnp.sum(dz, axis=0, keepdims=True),
                               jnp.sum(dv3 * vh, axis=0, keepdims=True),
                               jnp.zeros((SUBLANE - 3, SSM_W), F32)], axis=0)

        @pl.when(first)
        def _():
            sm_ref[...] = upd
            dbm_ref[...] = dbm

        @pl.when(jnp.logical_not(first))
        def _():
            sm_ref[...] += upd
            dbm_ref[...] += dbm

    rest = 2 * SGU_W + 2 * D_MODEL
    bf_d, bf_s = jax.ShapeDtypeStruct((t_len, D_MODEL), BF16), jax.ShapeDtypeStruct((t_len, SSM_W), BF16)
    return _call(
        body, "bwd_mix", (t_len // TB,),
        [_rows(TB, D_MODEL), _rows(TB, IN_COLS), _rows(TB, SSM_W), _rows(TB, SSM_W),
         _rows(TB, SGU_W), _rows(TB, D_MODEL), _rows(TB, D_MODEL)] + [_whole()] * 10,
        [_rows(TB, N_STATE), _rows(TB, N_STATE), _rows(TB, SSM_W), _rows(TB, rest),
         _rows(TB, D_MODEL), _rows(TB, D_MODEL), _rows(TB, D_MODEL), _rows(TB, SSM_W),
         _rows(TB, SSM_W), _rows(TB, SSM_W), _rows(TB, SGU_W), _rows(TB, SSM_W),
         _acc(SUBLANE, SSM_W), _acc(CHUNK, SGU_W),
         pl.BlockSpec((SGU_G, CHUNK, CHUNK), lambda i: (0, 0, 0))],
        [jax.ShapeDtypeStruct((t_len, N_STATE), BF16), jax.ShapeDtypeStruct((t_len, N_STATE), BF16),
         jax.ShapeDtypeStruct((t_len, SSM_W), F32), jax.ShapeDtypeStruct((t_len, rest), BF16),
         bf_d, bf_d, bf_d, bf_s, bf_s, bf_s, bf_s, bf_s,
         jax.ShapeDtypeStruct((SUBLANE, SSM_W), F32), jax.ShapeDtypeStruct((CHUNK, SGU_W), F32),
         jax.ShapeDtypeStruct((SGU_G, CHUNK, CHUNK), F32)],
        [dx2, p, y0, z, mixed, ya, yb, w_out, w_pa, w_pb, w_glu, cre, cim, ws_st, wst_st, d_skip,
         g_sgu], jobs)


def _scan_bwd(dsr, dsi, str_, sti, tab_rev, jobs=()):
    t_len = dsr.shape[0]
    nblk = t_len // SUBLANE
    lb = SCAN_LANES

    def body(dr_ref, di_ref, sr_ref, si_ref, tab_ref, lr_ref, li_ref, dar_ref, dai_ref):
        tab_v = [tab_ref[q] for q in range(8)]
        row0 = lax.broadcasted_iota(jnp.int32, (SUBLANE, lb), 0) == 0
        tile = BF16_TILE

        def step(k, carry):
            cr, ci, acr, aci = carry
            base = pl.multiple_of((nblk - (k + 1) * SCAN_UNROLL) * SUBLANE, SCAN_UNROLL * SUBLANE)
            state = _load_blocks(sr_ref, si_ref, base)
            before = pl.ds(pl.multiple_of(jnp.maximum(base - tile, 0), tile), tile)
            has_before = jnp.where(base > 0, 1.0, 0.0)
            prev = (sr_ref[before, :].astype(F32)[tile - 1:tile] * has_before,
                    si_ref[before, :].astype(F32)[tile - 1:tile] * has_before)
            local = [_scan_local(xr, xi, tab_v, (7, 6, 4))
                     for xr, xi in _load_blocks(dr_ref, di_ref, base)]
            lam = [None] * SCAN_UNROLL
            for b in reversed(range(SCAN_UNROLL)):
                xr, xi = _scan_carry(*local[b], tab_v, cr, ci)
                lam[b] = (xr, xi)
                cr, ci = xr[0:1, :], xi[0:1, :]
                pr, pi = prev if b == 0 else (state[b - 1][0][SUBLANE - 1:], state[b - 1][1][SUBLANE - 1:])
                s_r = jnp.where(row0, pr, pltpu.roll(state[b][0], 1, 0))
                s_i = jnp.where(row0, pi, pltpu.roll(state[b][1], 1, 0))
                acr = acr + xr * s_r + xi * s_i
                aci = aci + xi * s_r - xr * s_i
            _store_blocks(lr_ref, li_ref, base, lam)
            return cr, ci, acr, aci

        zero = jnp.zeros((1, lb), F32)
        zacc = jnp.zeros((SUBLANE, lb), F32)
        _, _, acr, aci = lax.fori_loop(0, nblk // SCAN_UNROLL, step, (zero, zero, zacc, zacc))
        dar_ref[...] = acr
        dai_ref[...] = aci

    col = pl.BlockSpec((t_len, lb), lambda j: (0, j))
    small = pl.BlockSpec((SUBLANE, lb), lambda j: (0, j))
    return _call(
        body, "scan_bwd", (N_STATE // lb,),
        [col, col, col, col, pl.BlockSpec((8, SUBLANE, lb), lambda j: (0, 0, j))],
        [col, col, small, small],
        [jax.ShapeDtypeStruct((t_len, N_STATE), BF16)] * 2
        + [jax.ShapeDtypeStruct((SUBLANE, N_STATE), F32)] * 2,
        [dsr, dsi, str_, sti, tab_rev], jobs)


def _bwd_in(lam_r, lam_i, du_part, drest, x, dx2, g_mix, w_in, bre, bim, jobs=()):
    t_len = x.shape[0]
    cs = IN_COLS // N_CHIP

    def body(lr_ref, li_ref, du_ref, dr_ref, x_ref, dx2_ref, g_ref, w_ref, bre_ref, bim_ref,
             gx_ref, dp_ref, sm_ref):
        i = pl.program_id(0)
        du = du_ref[...] + jnp.concatenate(
            [_dot_nt(lr_ref[:, i * DIAG_N:(i + 1) * DIAG_N],
                     bre_ref[i * LANE:(i + 1) * LANE, i * DIAG_N:(i + 1) * DIAG_N])
             + _dot_nt(li_ref[:, i * DIAG_N:(i + 1) * DIAG_N],
                       bim_ref[i * LANE:(i + 1) * LANE, i * DIAG_N:(i + 1) * DIAG_N])
             for i in range(SSM_W // LANE)], axis=1)
        dp_ref[:, 0:SSM_W] = du.astype(BF16)
        dp_ref[:, SSM_W:] = dr_ref[...]
        dh = jnp.zeros((TB, D_MODEL), F32)
        for k in range(N_CHIP):
            dh = dh + _dot_nt(dp_ref[:, k * cs:(k + 1) * cs], w_ref[k])
        r, xh = _rms_stats(x_ref[...])
        gx_ref[...] = dx2_ref[...] + _rms_bwd(dh * g_ref[...], xh, r)
        upd = jnp.concatenate([jnp.sum(dh * xh, axis=0, keepdims=True),
                               jnp.zeros((SUBLANE - 1, D_MODEL), F32)], axis=0)

        @pl.when(i == 0)
        def _():
            sm_ref[...] = upd

        @pl.when(i > 0)
        def _():
            sm_ref[...] += upd

    return _call(
        body, "bwd_in", (t_len // TB,),
        [_rows(TB, N_STATE), _rows(TB, N_STATE), _rows(TB, SSM_W), _rows(TB, IN_COLS - SSM_W),
         _rows(TB, D_MODEL), _rows(TB, D_MODEL)] + [_whole()] * 4,
        [_rows(TB, D_MODEL), _rows(TB, IN_COLS), _acc(SUBLANE, D_MODEL)],
        [jax.ShapeDtypeStruct((t_len, D_MODEL), F32), jax.ShapeDtypeStruct((t_len, IN_COLS), BF16),
         jax.ShapeDtypeStruct((SUBLANE, D_MODEL), F32)],
        [lam_r, lam_i, du_part, drest, x, dx2, g_mix, w_in, bre, bim], jobs)


def _matmul_tn(a, b, name, out_shape, grid_ij, a_blk, a_map, b_blk, b_map, o_blk, o_map, jobs=()):
    tk = a_blk[0]
    nk = a.shape[0] // tk
    assert nk * tk == a.shape[0] and nk > 0

    def body(a_ref, b_ref, o_ref, acc_ref):
        k = pl.program_id(2)

        @pl.when(k == 0)
        def _():
            acc_ref[...] = jnp.zeros_like(acc_ref)

        acc_ref[...] += lax.dot_general(a_ref[...].astype(BF16), b_ref[...].astype(BF16),
                                        (((0,), (0,)), ((), ())), preferred_element_type=F32)

        @pl.when(k == nk - 1)
        def _():
            o_ref[...] = acc_ref[...]

    outs, per_job = _call(
        body, name, (grid_ij[0], grid_ij[1], nk),
        [pl.BlockSpec(a_blk, a_map), pl.BlockSpec(b_blk, b_map)], [pl.BlockSpec(o_blk, o_map)],
        [jax.ShapeDtypeStruct(out_shape, F32)], [a, b], jobs,
        scratch=[pltpu.VMEM((a_blk[1], b_blk[1]), F32)])
    return outs[0], per_job


def _dw_shards(a, b, name, tk, jobs=()):
    m, n = a.shape[1], b.shape[1]
    tn = n // N_CHIP
    tk = min(tk, a.shape[0])
    return _matmul_tn(a, b, name, (N_CHIP, m, tn), (1, N_CHIP),
                      (tk, m), lambda i, j, k: (k, 0), (tk, tn), lambda i, j, k: (k, j),
                      (None, m, tn), lambda i, j, k: (j, 0, 0), jobs)


def _dw_rows(a, b, name, tm, tk):
    m, n = a.shape[1], b.shape[1]
    tk = min(tk, a.shape[0])
    return _matmul_tn(a, b, name, (m, n), (m // tm, 1),
                      (tk, tm), lambda i, j, k: (k, i), (tk, n), lambda i, j, k: (k, 0),
                      (tm, n), lambda i, j, k: (i, 0))[0]


def _dw_cols(a, b, name, tn, sharded, jobs=()):
    t_len, m = a.shape
    n = b.shape[1]

    def body(a_ref, b_ref, o_ref):
        o_ref[...] = lax.dot_general(a_ref[...].astype(BF16), b_ref[...].astype(BF16),
                                     (((0,), (0,)), ((), ())), preferred_element_type=F32)

    if sharded:
        o_spec, o_shape = pl.BlockSpec((None, m, tn), lambda j: (j, 0, 0)), (n // tn, m, tn)
    else:
        o_spec, o_shape = pl.BlockSpec((m, tn), lambda j: (0, j)), (m, n)
    outs, per_job = _call(body, name, (n // tn,),
                          [_whole(), pl.BlockSpec((t_len, tn), lambda j: (0, j))], [o_spec],
                          [jax.ShapeDtypeStruct(o_shape, F32)], [a, b], jobs)
    return outs[0], per_job


def _dw_pair(a, m, b1, b2, name, jobs=()):
    t_len = a.shape[0]
    n_slab = DIAG_N // LANE
    rows_per_slab = LANE // n_slab

    def body(a_ref, b1_ref, b2_ref, o1_ref, o2_ref):
        for b_ref, o_ref in ((b1_ref, o1_ref), (b2_ref, o2_ref)):
            prod = lax.dot_general(a_ref[...].astype(BF16), b_ref[...].astype(BF16),
                                   (((0,), (0,)), ((), ())), preferred_element_type=F32)
            for j in range(n_slab):
                rows = slice(j * rows_per_slab, (j + 1) * rows_per_slab)
                o_ref[rows, :] = prod[rows, j * LANE:(j + 1) * LANE]

    tok = pl.BlockSpec((t_len, DIAG_N), lambda i: (0, i))
    out = pl.BlockSpec((LANE, LANE), lambda i: (i, 0))
    return _call(body, name, (m // LANE,),
                 [pl.BlockSpec((t_len, LANE), lambda i: (0, i)), tok, tok], [out, out],
                 [jax.ShapeDtypeStruct((m, LANE), F32)] * 2, [a, b1, b2], jobs)


def _prefetch_call(body, name, grid, scalars, in_specs, out_specs, out_shape, args):
    return pl.pallas_call(
        body, name=name,
        grid_spec=pltpu.PrefetchScalarGridSpec(num_scalar_prefetch=1, grid=grid, in_specs=in_specs,
                                               out_specs=out_specs),
        out_shape=out_shape, compiler_params=_params(len(grid)),
    )(scalars, *args)


def _place_shard(w, where, name, dtype, tr):
    rows, cols = w.shape

    def body(s_ref, w_ref, o_ref):
        o_ref[...] = w_ref[...].astype(dtype)

    return _prefetch_call(
        body, name, (rows // tr,), where,
        [pl.BlockSpec((tr, cols), lambda i, s: (i, 0))],
        pl.BlockSpec((None, tr, cols), lambda i, s: (s[0], i, 0)),
        jax.ShapeDtypeStruct((N_CHIP, rows, cols), dtype), [w])


def _place_shards(ws, where, name, dtype):
    n = len(ws)

    def body(s_ref, *refs):
        for t in range(n):
            refs[n + t][...] = refs[t][...].astype(dtype)

    return _prefetch_call(
        body, name, (1,), where,
        [pl.BlockSpec(w.shape, lambda i, s: (0, 0)) for w in ws],
        [pl.BlockSpec((None,) + w.shape, lambda i, s: (s[0], 0, 0)) for w in ws],
        [jax.ShapeDtypeStruct((N_CHIP,) + w.shape, dtype) for w in ws], ws)


def _add_sibling(gs, gots, where, name):
    n = len(gs)
    halves = [(g.shape[1] // 2, g.shape[2]) for g in gs]

    def body(s_ref, *refs):
        for t in range(n):
            refs[2 * n + t][...] = (refs[t][...] + refs[n + t][...]).astype(BF16)

    return _prefetch_call(
        body, name, (N_CHIP,), where,
        [pl.BlockSpec((None, hr, cs), lambda k, s: (k, s[1], 0)) for hr, cs in halves]
        + [pl.BlockSpec((None, hr, cs), lambda k, s: (k, 0, 0)) for hr, cs in halves],
        [pl.BlockSpec((None, hr, cs), lambda k, s: (k, 0, 0)) for hr, cs in halves],
        [jax.ShapeDtypeStruct((N_CHIP, hr, cs), BF16) for hr, cs in halves], list(gs) + list(gots))


def _add_chips(sums, gots, where, name):
    n = len(sums)
    halves = [s.shape[1:] for s in sums]

    def body(s_ref, *refs):
        for t in range(n):
            own_ref, got_ref = refs[t], refs[n + t]
            refs[2 * n + t][...] = ((own_ref[...].astype(F32) + got_ref[0].astype(F32))
                                    + got_ref[1].astype(F32)) + got_ref[2].astype(F32)

    return _prefetch_call(
        body, name, (1,), where,
        [pl.BlockSpec((None, hr, cs), lambda i, s: (s[0], 0, 0)) for hr, cs in halves]
        + [pl.BlockSpec((3, hr, cs), lambda i, s: (0, 0, 0)) for hr, cs in halves],
        [pl.BlockSpec((hr, cs), lambda i, s: (s[1], 0)) for hr, cs in halves],
        [jax.ShapeDtypeStruct((2 * hr, cs), F32) for hr, cs in halves], list(sums) + list(gots))


def _small_allreduce(pack):
    rows = pack.shape[0]
    half = rows // 2

    def body(in_ref, out_ref, sib_ref, slots_ref, s_a, r_a, s_b, r_b, s_c, r_c):
        x, y, c, chips = _place()
        k_me = 2 * x + y
        sib = (x, y, 1 - c)
        first = _remote(in_ref, sib_ref, s_a, r_a, sib)
        first.start()
        first.wait()
        mine = _half(rows, c)
        slots_ref[k_me] = in_ref[mine, :] + sib_ref[mine, :]
        cps = [_remote(slots_ref.at[k_me], slots_ref.at[k_me], s_b.at[j], r_b.at[j], (*ch, c))
               for j, ch in enumerate(chips)]
        for cp in cps:
            cp.start()
        for j, ch in enumerate(chips):
            slot = slots_ref.at[_chip_index(ch)]
            _remote(slot, slot, s_b.at[j], r_b.at[j], (*ch, c)).wait_recv()
        for cp in cps:
            cp.wait_send()
        out_ref[mine, :] = ((slots_ref[0] + slots_ref[1]) + slots_ref[2]) + slots_ref[3]
        last = _remote(out_ref.at[mine, :], out_ref.at[mine, :], s_c, r_c, sib)
        last.start()
        theirs = out_ref.at[_half(rows, 1 - c), :]
        _remote(theirs, theirs, s_c, r_c, sib).wait_recv()
        last.wait_send()

    return pl.pallas_call(
        body, name="small_allreduce", in_specs=[_whole()], out_specs=_whole(),
        out_shape=jax.ShapeDtypeStruct(pack.shape, F32),
        scratch_shapes=[pltpu.VMEM(pack.shape, F32), pltpu.VMEM((N_CHIP, half, LANE), F32),
                        pltpu.SemaphoreType.DMA, pltpu.SemaphoreType.DMA,
                        pltpu.SemaphoreType.DMA((3,)), pltpu.SemaphoreType.DMA((3,)),
                        pltpu.SemaphoreType.DMA, pltpu.SemaphoreType.DMA],
        compiler_params=_params(0),
    )(pack)


def _adamw_update(w_ref, g_ref, m_ref, v_ref, d_ref, mo_ref, vo_ref):
    gv = g_ref[...]
    mn = ADAM_B1 * m_ref[...] + (1.0 - ADAM_B1) * gv
    vn = ADAM_B2 * v_ref[...] + (1.0 - ADAM_B2) * (gv * gv)
    mo_ref[...] = mn
    vo_ref[...] = vn
    m_hat = mn / (1.0 - ADAM_B1 ** ADAM_STEP)
    v_hat = vn / (1.0 - ADAM_B2 ** ADAM_STEP)
    d_ref[...] = -ADAM_LR * (m_hat / (jnp.sqrt(v_hat) + ADAM_EPS) + ADAM_WD * w_ref[...])


def _adamw(w, g, m, v, name, tr):
    rows, cols = w.shape
    blk = _rows(tr, cols)

    def body(w_ref, g_ref, m_ref, v_ref, go_ref, d_ref, mo_ref, vo_ref):
        go_ref[...] = g_ref[...]
        _adamw_update(w_ref, g_ref, m_ref, v_ref, d_ref, mo_ref, vo_ref)

    return _call(body, name, (rows // tr,), [blk] * 4, [blk] * 4,
                 [jax.ShapeDtypeStruct(w.shape, F32)] * 4, [w, g, m, v])[0]


def _adamw_many(ws, gs, ms, vs, name):
    n = len(ws)

    def body(*refs):
        for t in range(n):
            _adamw_update(*[refs[q * n + t] for q in range(7)])

    specs = [pl.BlockSpec(a.shape, lambda i, nd=a.ndim: (0,) * nd) for a in ws]
    outs = pl.pallas_call(
        body, name=name, grid=(1,), in_specs=specs * 4, out_specs=specs * 3,
        out_shape=[jax.ShapeDtypeStruct(a.shape, F32) for _ in range(3) for a in ws],
        compiler_params=_params(1),
    )(*ws, *gs, *ms, *vs)
    return outs[:n], outs[n:2 * n], outs[2 * n:]


def _ssm_discretize(a_re, a_im, log_dt, b_re, b_im):
    dt = jnp.exp(log_dt)[:, None]
    mag = jnp.exp(dt * a_re)
    abr = mag * jnp.cos(dt * a_im)
    abi = mag * jnp.sin(dt * a_im)
    den = a_re * a_re + a_im * a_im
    nr = abr - 1.0
    ni = abi
    f_re = (nr * a_re + ni * a_im) / den
    f_im = (ni * a_re - nr * a_im) / den
    bbr = f_re[..., None] * b_re - f_im[..., None] * b_im
    bbi = f_re[..., None] * b_im + f_im[..., None] * b_re
    return abr, abi, bbr, bbi


def _scan_tables(abr, abi):
    ar = abr.reshape(1, N_STATE)
    ai = abi.reshape(1, N_STATE)
    pr, pi = [ar], [ai]
    for _ in range(SUBLANE - 1):
        pr, pi = pr + [pr[-1] * ar - pi[-1] * ai], pi + [pr[-1] * ai + pi[-1] * ar]
    row = jnp.arange(SUBLANE)[:, None]
    tabs = []
    for d in (1, 2, 4):
        tabs.append(jnp.where(row >= d, pr[d - 1], 0.0))
        tabs.append(jnp.where(row >= d, pi[d - 1], 0.0))
    tabs.append(jnp.concatenate(pr, axis=0))
    tabs.append(jnp.concatenate(pi, axis=0))
    fwd = jnp.stack(tabs)
    sign = jnp.array([1.0, -1.0] * 4, F32)[:, None, None]
    return fwd, fwd[:, ::-1, :] * sign


def _block_diag_b(bb):
    strip = bb.transpose(2, 0, 1).reshape(SSM_H, N_STATE)
    rows = lax.broadcasted_iota(jnp.int32, (SSM_W, N_STATE), 0) // SSM_H
    cols = lax.broadcasted_iota(jnp.int32, (SSM_W, N_STATE), 1) // SSM_P
    return jnp.where(rows == cols, jnp.tile(strip, (SSM_G, 1)), 0.0).astype(BF16)


def _block_diag_c(cc):
    strip = cc.transpose(0, 2, 1).reshape(N_STATE, SSM_H)
    rows = lax.broadcasted_iota(jnp.int32, (N_STATE, SSM_W), 0) // SSM_P
    cols = lax.broadcasted_iota(jnp.int32, (N_STATE, SSM_W), 1) // SSM_H
    return jnp.where(rows == cols, jnp.tile(strip, (1, SSM_G)), 0.0).astype(BF16)


SMALL_SHAPES = {
    "g_mix": (D_MODEL,), "a_re": (SSM_G, SSM_P), "a_im": (SSM_G, SSM_P), "log_dt": (SSM_G,),
    "b_re": (SSM_G, SSM_P, SSM_H), "b_im": (SSM_G, SSM_P, SSM_H),
    "c_re": (SSM_G, SSM_H, SSM_P), "c_im": (SSM_G, SSM_H, SSM_P),
    "d_skip": (SSM_W,), "b_glu": (SSM_W,), "g_sgu": (SGU_W,), "w_s": (SGU_G, CHUNK, CHUNK),
    "b_s": (SGU_G, CHUNK), "g_ffn": (D_MODEL,), "conv_b": (2 * D_FF,), "g_final": (D_MODEL,),
}
PACK_ITEMS = [("loss", (1,))] + [(n, SMALL_SHAPES[n]) for n in SMALL] + [("conv_w", (3, 2 * D_FF))]
TILE = SUBLANE * LANE


def _item_rows(shape):
    return -(-math.prod(shape) // TILE) * SUBLANE


PACK_ROWS = -(-sum(_item_rows(s) for _, s in PACK_ITEMS) // (2 * SUBLANE)) * (2 * SUBLANE)


def _pack(values):
    parts, used = [], 0
    for name, shape in PACK_ITEMS:
        size, rows = math.prod(shape), _item_rows(shape)
        if name in values:
            flat = values[name].astype(F32).reshape(size)
            if rows * LANE > size:
                flat = jnp.pad(flat, (0, rows * LANE - size))
            parts.append(flat.reshape(rows, LANE))
        else:
            parts.append(jnp.zeros((rows, LANE), F32))
        used += rows
    if PACK_ROWS > used:
        parts.append(jnp.zeros((PACK_ROWS - used, LANE), F32))
    return jnp.concatenate(parts, axis=0)


def _unpack(pack):
    out, off = {}, 0
    for name, shape in PACK_ITEMS:
        rows = _item_rows(shape)
        out[name] = pack[off:off + rows].reshape(rows * LANE)[:math.prod(shape)].reshape(shape)
        off += rows
    return out


PLACE_ROWS = {"w_in": 256, "w_up": 256, "w_down": 352, "w_out": 256, "w_proj_a": 256,
              "w_proj_b": 256, "w_glu": 128}


def kernel(x, g_mix, w_in, a_re, a_im, log_dt, b_re, b_im, c_re, c_im, d_skip, w_glu, b_glu, w_proj_a, g_sgu, w_s, b_s, w_proj_b, w_out, g_ffn, w_up, conv_w, conv_b, w_down, g_final, loss_target, m_g_mix, m_w_in, m_a_re, m_a_im, m_log_dt, m_b_re, m_b_im, m_c_re, m_c_im, m_d_skip, m_w_glu, m_b_glu, m_w_proj_a, m_g_sgu, m_w_s, m_b_s, m_w_proj_b, m_w_out, m_g_ffn, m_w_up, m_conv_w, m_conv_b, m_w_down, m_g_final, v_g_mix, v_w_in, v_a_re, v_a_im, v_log_dt, v_b_re, v_b_im, v_c_re, v_c_im, v_d_skip, v_w_glu, v_b_glu, v_w_proj_a, v_g_sgu, v_w_s, v_b_s, v_w_proj_b, v_w_out, v_g_ffn, v_w_up, v_conv_w, v_conv_b, v_w_down, v_g_final):
    given = dict(locals())
    w = {n: given[n] for n in WEIGHTS}
    m = {n: given["m_" + n] for n in WEIGHTS}
    v = {n: given["v_" + n] for n in WEIGHTS}

    def shard2d(a):
        return a.reshape(a.shape[-2], a.shape[-1])

    chip = 2 * lax.axis_index("x") + lax.axis_index("y")
    where = jnp.stack([chip, lax.axis_index("c")]).astype(jnp.int32)
    xs, target = x[0], loss_target[0]
    small = {n: w[n].reshape(SMALL_SHAPES[n]) for n in SMALL}

    (abr, abi, bbr, bbi), disc_vjp = jax.vjp(_ssm_discretize, small["a_re"], small["a_im"],
                                             small["log_dt"], small["b_re"], small["b_im"])
    tab_f, tab_r = _scan_tables(abr, abi)
    bre = _block_diag_b(bbr)
    bim = _block_diag_b(bbi)
    cre = _block_diag_c(small["c_re"])
    cim = _block_diag_c(small["c_im"])
    tril = jnp.tril(jnp.ones((CHUNK, CHUNK), dtype=bool))
    ws = jnp.where(tril[None], small["w_s"], 0.0)
    ws_st = ws.reshape(SGU_G // 2, 2 * CHUNK, CHUNK).astype(BF16)
    wst_st = ws.transpose(0, 2, 1).reshape(SGU_G // 2, 2 * CHUNK, CHUNK).astype(BF16)
    bmat = jnp.repeat(small["b_s"].T, SGU_D, axis=1)
    g_mix2 = small["g_mix"].reshape(1, D_MODEL)
    g_ffn2 = small["g_ffn"].reshape(1, D_MODEL)
    g_final2 = small["g_final"].reshape(1, D_MODEL)
    g_sgu2 = small["g_sgu"].reshape(1, SGU_W)
    d_skip2 = small["d_skip"].reshape(1, SSM_W)
    b_glu2 = small["b_glu"].reshape(1, SSM_W)
    conv_b2 = small["conv_b"].reshape(1, 2 * D_FF)

    gat = {"w_in": _place_shard(shard2d(w["w_in"]), where, "place_w_in", BF16, PLACE_ROWS["w_in"])}
    gat.update(zip(BIG[1:], _place_shards([shard2d(w[n]) for n in BIG[1:]], where, "place_rest", BF16)))
    gat["conv_w"] = _place_shard(shard2d(w["conv_w"]), where, "place_conv_w", F32, 3)
    all_rows = (0, D_MODEL)
    (gat["w_in"],), = _comm("gather_in", [_job_gather(
        [gat["w_in"]], [(0, all_rows, ICI, (0.0, 0.5)), (0, all_rows, SIBLING, (0.5, 1.0))])])
    mixers = ["w_glu", "w_proj_a", "w_proj_b", "w_out"]
    rows = {n: (0, gat[n].shape[1]) for n in mixers}
    down_a, down_b = (0, D_FF // 8), (D_FF // 8, D_FF // 8)
    up_a, up_b = (0, 3 * D_MODEL // 8), (3 * D_MODEL // 8, 5 * D_MODEL // 8)
    span = (0.0, 1.0)

    names = mixers + ["conv_w", "w_down"]
    (p, h1, bur, bui), (got,) = _fwd_in(
        xs, g_mix2, gat["w_in"], bre, bim,
        [_job_gather([gat[n] for n in names],
                     [(i, rows[n], ICI, span) for i, n in enumerate(mixers)]
                     + [(4, None, ICI, span), (5, down_a, ICI, span)])])
    gat.update(zip(names, got))
    names = mixers + ["w_down", "w_up"]
    (str_, sti), (got,) = _scan_fwd(
        bur, bui, tab_f,
        [_job_gather([gat[n] for n in names],
                     [(i, rows[n], SIBLING, span) for i, n in enumerate(mixers)]
                     + [(4, down_a, SIBLING, span), (4, down_b, ICI, span), (5, up_a, ICI, span)])])
    gat.update(zip(names, got))
    w_glu_f = gat["w_glu"].reshape(SSM_W, SSM_W)
    w_out_f = gat["w_out"].reshape(D_MODEL, D_MODEL)
    conv_w_f = gat["conv_w"].transpose(1, 0, 2).reshape(3, 2 * D_FF)
    (x2, y0, z, mixed, ya, yb), ((gat["w_down"], gat["w_up"]),) = _fwd_mix(
        xs, p, str_, sti, cre, cim, d_skip2, w_glu_f, b_glu2, gat["w_proj_a"], g_sgu2, ws_st, bmat,
        gat["w_proj_b"], w_out_f,
        [_job_gather([gat["w_down"], gat["w_up"]],
                     [(0, down_b, SIBLING, span), (1, up_a, SIBLING, span),
                      (1, up_b, ICI, (0.0, 0.75)), (1, up_b, SIBLING, (0.75, 1.0))])])
    w_down_f = gat["w_down"].reshape(D_FF, D_MODEL)
    up, act, f, h2, dx3, sm_ffn = _fwd_ffn(x2, target, g_ffn2, gat["w_up"], conv_w_f, conv_b2,
                                           w_down_f, g_final2)

    def leg1_done(names, got):
        return _add_sibling([part[n] for n in names], got, where, "add_sibling_" + names[0])

    def leg2_done(names, sums, got):
        return _add_chips(sums, got, where, "add_chips_" + names[0])

    part, red = {}, {}
    part["w_down"] = _dw_rows(f, dx3, "dw_down", D_FF // 2, 4 * TK).reshape(
        N_CHIP, D_FF // N_CHIP, D_MODEL)
    (dx2, dup, sm_conv, sm_gffn), (got,) = _bwd_ffn(
        dx3, up, act, x2, g_ffn2, gat["w_up"], conv_w_f, w_down_f,
        [_job_sibling_halves([part["w_down"]])])
    sum_down = leg1_done(["w_down"], got)
    part["w_up"], (got,) = _dw_shards(h2, dup, "dw_up", 4 * TK, [_job_to_owner(sum_down)])
    red_down = leg2_done(["w_down"], sum_down, got)
    ((dsr, dsi, du_part, drest, mrg, dya, dyb, yap, dz, y1, sgu, dy0, sm_mix, dbm, dws),
     (got, (red["w_down"],))) = _bwd_mix(
        dx2, p, y0, z, mixed, ya, yb, w_out_f, gat["w_proj_a"], gat["w_proj_b"], w_glu_f, cre, cim,
        ws_st, wst_st, d_skip2, g_sgu2,
        [_job_sibling_halves([part["w_up"]]), _job_swap_halves(red_down)])
    sum_up = leg1_done(["w_up"], got)
    up_rows = sum_up[0].shape[1]
    up_first = (0, 5 * up_rows // 8)
    up_rest = (up_first[1], up_rows - up_first[1])
    (lam_r, lam_i, dar8, dai8), (got_up,) = _scan_bwd(
        dsr, dsi, str_, sti, tab_r, [_job_to_owner(sum_up, window=up_first)])
    mix4 = ["w_out", "w_proj_a", "w_proj_b", "w_glu"]
    part["w_out"] = _dw_cols(mrg, dx2, "dw_out", D_MODEL // 2, False)[0].reshape(
        N_CHIP, D_MODEL // N_CHIP, D_MODEL)
    part["w_proj_a"] = _dw_cols(yap, dya, "dw_proj_a", D_MODEL // N_CHIP, True)[0]
    part["w_proj_b"] = _dw_cols(sgu, dyb, "dw_proj_b", D_MODEL // N_CHIP, True)[0]
    part["w_glu"] = _dw_cols(y1, dz, "dw_glu", SSM_W, False)[0].reshape(
        N_CHIP, SSM_W // N_CHIP, SSM_W)
    (grad_x, dp, sm_gmix), (got, got_up) = _bwd_in(
        lam_r, lam_i, du_part, drest, xs, dx2, g_mix2, gat["w_in"], bre, bim,
        [_job_sibling_halves([part[n] for n in mix4]),
         _job_to_owner(sum_up, window=up_rest, into=got_up)])
    red_up = leg2_done(["w_up"], sum_up, got_up)
    sums_m = leg1_done(mix4, got)
    part["w_in"], (got, (red["w_up"],)) = _dw_cols(
        h1, dp, "dw_in", IN_COLS // N_CHIP, True, [_job_to_owner(sums_m), _job_swap_halves(red_up)])
    red_m = leg2_done(mix4, sums_m, got)
    (dbd_r, dbd_i), (got, done_m) = _dw_pair(
        p, SSM_W, lam_r, lam_i, "db_bar",
        [_job_sibling_halves([part["w_in"]]), _job_swap_halves(red_m)])
    red.update(zip(mix4, done_m))
    sum_in = leg1_done(["w_in"], got)
    (dcd_r, dcd_i), (got,) = _dw_pair(dy0, SSM_W, str_, sti, "dc", [_job_to_owner(sum_in)])
    red_in = leg2_done(["w_in"], sum_in, got)
    (red["w_in"],), = _comm("swap_w_in", [_job_swap_halves(red_in)])

    def pick_c(slabs):
        two = LANE // SSM_P
        return jnp.einsum("jshsp->jshp", slabs.reshape(SSM_G // two, two, SSM_H, two, SSM_P)
                          ).reshape(SSM_G, SSM_H, SSM_P)

    def pick_b(slabs):
        return pick_c(slabs).transpose(0, 2, 1)

    dabr = jnp.sum(dar8, axis=0).reshape(SSM_G, SSM_P)
    dabi = jnp.sum(dai8, axis=0).reshape(SSM_G, SSM_P)
    d_a_re, d_a_im, d_log_dt, d_b_re, d_b_im = disc_vjp((dabr, dabi, pick_b(dbd_r), pick_b(dbd_i)))
    gsmall = {
        "g_mix": sm_gmix[0], "a_re": d_a_re, "a_im": d_a_im, "log_dt": d_log_dt,
        "b_re": d_b_re, "b_im": d_b_im, "c_re": pick_c(dcd_r), "c_im": -pick_c(dcd_i),
        "d_skip": sm_mix[0], "b_glu": sm_mix[1], "g_sgu": sm_mix[2],
        "w_s": jnp.where(tril[None], dws, 0.0),
        "b_s": dbm.reshape(CHUNK, SGU_G, SGU_D).sum(-1).T,
        "g_ffn": sm_gffn[0], "conv_b": sm_conv[3], "g_final": sm_ffn[0],
        "conv_w": sm_conv[0:3], "loss": sm_ffn[1, 0:1],
    }

    total_pack = _small_allreduce(_pack(gsmall))
    total = _unpack(total_pack)
    grads = dict(red)
    cs = 2 * D_FF // N_CHIP
    grads["conv_w"] = lax.dynamic_slice(total["conv_w"], (0, chip * cs), (3, cs))
    delta, new_m, new_v = {}, {}, {}
    for n in BIG + ("conv_w",):
        grads[n], delta[n], new_m[n], new_v[n] = _adamw(
            shard2d(w[n]), grads[n], shard2d(m[n]), shard2d(v[n]), "adamw_" + n, PLACE_ROWS.get(n, 3))
    for n in SMALL:
        grads[n] = total[n].reshape(w[n].shape)
    ud, um, uv = _adamw_many(*[[d[n] for n in SMALL] for d in (w, grads, m, v)], "adamw_small")
    for i, n in enumerate(SMALL):
        delta[n], new_m[n], new_v[n] = ud[i], um[i], uv[i]

    def like(d):
        return [d[n].reshape(w[n].shape) for n in WEIGHTS]

    return (total["loss"].reshape(()), grad_x.reshape(x.shape), *like(grads), *like(delta),
            *like(new_m), *like(new_v))
```

```python
import math

import jax
import jax.numpy as jnp
from jax import lax
from jax.experimental import pallas as pl
from jax.experimental.pallas import tpu as pltpu

F32 = jnp.float32
BF16 = jnp.bfloat16
MESH = pl.DeviceIdType.MESH

D_MODEL = 1024
SSM_W = 512
SSM_G = 32
SSM_H = 16
SSM_P = 64
N_STATE = SSM_G * SSM_P
DIAG_N = 128 * SSM_P // SSM_H
SGU_W = 512
SGU_G = 8
SGU_D = 64
CHUNK = 128
D_FF = 2816
IN_COLS = 3584
EPS = 1e-6
N_CHIP = 4

ADAM_LR = 0.001
ADAM_B1 = 0.9
ADAM_B2 = 0.999
ADAM_EPS = 1e-08
ADAM_WD = 0.01
ADAM_STEP = 10

SUBLANE = 8
LANE = 128
VMEM_LIMIT = 56 * 1024 * 1024
TB = 256
TK = 512
SCAN_LANES = 256
SCAN_UNROLL = 4
HALO = SUBLANE

BIG = ("w_in", "w_up", "w_down", "w_out", "w_proj_a", "w_proj_b", "w_glu")
SMALL = ("g_mix", "a_re", "a_im", "log_dt", "b_re", "b_im", "c_re", "c_im", "d_skip", "b_glu",
         "g_sgu", "w_s", "b_s", "g_ffn", "conv_b", "g_final")
WEIGHTS = ("g_mix", "w_in", "a_re", "a_im", "log_dt", "b_re", "b_im", "c_re", "c_im", "d_skip",
           "w_glu", "b_glu", "w_proj_a", "g_sgu", "w_s", "b_s", "w_proj_b", "w_out", "g_ffn",
           "w_up", "conv_w", "conv_b", "w_down", "g_final")

ANY = pl.BlockSpec(memory_space=pl.ANY)


def _params(n_grid):
    return pltpu.CompilerParams(dimension_semantics=("arbitrary",) * n_grid if n_grid else None,
                                vmem_limit_bytes=VMEM_LIMIT)


def _whole():
    return pl.BlockSpec(memory_space=pltpu.VMEM)


def _rows(tb, ncol):
    return pl.BlockSpec((tb, ncol), lambda i: (i, 0))


def _acc(nrow, ncol):
    return pl.BlockSpec((nrow, ncol), lambda i: (0, 0))


def _dot(a, b):
    return jnp.dot(a.astype(BF16), b.astype(BF16), preferred_element_type=F32)


def _dot_nt(a, b):
    return lax.dot_general(a.astype(BF16), b.astype(BF16), (((1,), (1,)), ((), ())),
                           preferred_element_type=F32)


def _sigmoid(v):
    return 0.5 * jnp.tanh(0.5 * v) + 0.5


_GELU_C = math.sqrt(2.0 / math.pi)


def _gelu(v):
    return 0.5 * v * (1.0 + jnp.tanh(_GELU_C * (v + 0.044715 * v * v * v)))


def _gelu_and_grad(v):
    v2 = v * v
    t = jnp.tanh(_GELU_C * v * (1.0 + 0.044715 * v2))
    half = 0.5 * (1.0 + t)
    return v * half, half + 0.5 * v * (1.0 - t * t) * _GELU_C * (1.0 + 3.0 * 0.044715 * v2)


def _rms_stats(v):
    r = lax.rsqrt(jnp.mean(v * v, axis=-1, keepdims=True) + EPS)
    return r, v * r


def _rms_bwd(dxh, xh, r):
    return r * (dxh - xh * jnp.mean(dxh * xh, axis=-1, keepdims=True))


def _place():
    x, y, c = lax.axis_index("x"), lax.axis_index("y"), lax.axis_index("c")
    chips = [(1 - x, y), (x, 1 - y), (1 - x, 1 - y)]
    return x, y, c, chips


def _chip_index(chip):
    return 2 * chip[0] + chip[1]


def _remote(src, dst, send_sem, recv_sem, device):
    return pltpu.make_async_remote_copy(src_ref=src, dst_ref=dst, send_sem=send_sem,
                                        recv_sem=recv_sem, device_id=device, device_id_type=MESH)


def _half(ref_rows, c):
    hr = ref_rows // 2
    return pl.ds(pl.multiple_of(c * hr, SUBLANE), hr)


class _Job:
    def __init__(self, hooks, n_sem, ins=(), inouts=(), outs=()):
        self.hooks, self.n_sem = list(hooks), n_sem
        self.ins, self.inouts, self.outs = list(ins), list(inouts), list(outs)


def _whole_span(start, finish):
    return [(0.0, "start", start), (1.0, "finish", finish)]


ICI, SIBLING = "ici", "sibling"


def _job_gather(bufs, legs):
    def copies(io, leg, first):
        b, window, kind, _ = legs[leg]
        x, y, c, chips = _place()
        k_me = 2 * x + y
        out = []
        for j, ch in enumerate(chips):
            k = _chip_index(ch)
            if window is None:
                src, land, dev = io[b].at[k_me], io[b].at[k], (*ch, c)
            else:
                r0, rows = window
                mine = pl.ds(pl.multiple_of(r0 + c * (rows // 2), SUBLANE), rows // 2)
                theirs = pl.ds(pl.multiple_of(r0 + (1 - c) * (rows // 2), SUBLANE), rows // 2)
                if kind == ICI:
                    src, land, dev = io[b].at[k_me, mine, :], io[b].at[k, mine, :], (*ch, c)
                else:
                    src, land, dev = io[b].at[k, mine, :], io[b].at[k, theirs, :], (x, y, 1 - c)
            out.append((src, land, first + j, dev))
        return out

    def starter(leg):
        def start(ins, io, outs, ssem, rsem):
            for src, _, i, dev in copies(io, leg, 3 * leg):
                _remote(src, src, ssem(i), rsem(i), dev).start()
        return start

    def finisher(leg):
        def finish(ins, io, outs, ssem, rsem):
            cps = copies(io, leg, 3 * leg)
            for _, land, i, dev in cps:
                _remote(land, land, ssem(i), rsem(i), dev).wait_recv()
            for src, _, i, dev in cps:
                _remote(src, src, ssem(i), rsem(i), dev).wait_send()
        return finish

    hooks = []
    for leg, (_, _, _, (begin, end)) in enumerate(legs):
        hooks += [(begin, "start", starter(leg)), (end, "finish", finisher(leg))]
    return _Job(hooks, 3 * len(legs), inouts=bufs)


def _job_sibling_halves(grads):
    n = len(grads)

    def build(ins, outs, ssem, rsem):
        x, y, c, _ = _place()
        return [_remote(ins[t].at[:, _half(grads[t].shape[1], 1 - c), :], outs[t], ssem(t), rsem(t),
                        (x, y, 1 - c)) for t in range(n)]

    def start(ins, io, outs, ssem, rsem):
        for cp in build(ins, outs, ssem, rsem):
            cp.start()

    def finish(ins, io, outs, ssem, rsem):
        for cp in build(ins, outs, ssem, rsem):
            cp.wait()

    return _Job(_whole_span(start, finish), n, ins=grads,
                outs=[jax.ShapeDtypeStruct((N_CHIP, g.shape[1] // 2, g.shape[2]), F32) for g in grads])


def _job_to_owner(sums):
    n = len(sums)

    def build(ins, outs, ssem, rsem):
        x, y, c, chips = _place()
        return [_remote(ins[t].at[_chip_index(ch)], outs[t].at[j], ssem(3 * t + j), rsem(3 * t + j),
                        (*ch, c)) for t in range(n) for j, ch in enumerate(chips)]

    def start(ins, io, outs, ssem, rsem):
        for cp in build(ins, outs, ssem, rsem):
            cp.start()

    def finish(ins, io, outs, ssem, rsem):
        for cp in build(ins, outs, ssem, rsem):
            cp.wait()

    return _Job(_whole_span(start, finish), 3 * n, ins=sums,
                outs=[jax.ShapeDtypeStruct((3,) + s.shape[1:], s.dtype) for s in sums])


def _job_swap_halves(bufs):
    n = len(bufs)

    def start(ins, io, outs, ssem, rsem):
        x, y, c, _ = _place()
        for t in range(n):
            mine = io[t].at[_half(bufs[t].shape[0], c), :]
            _remote(mine, mine, ssem(t), rsem(t), (x, y, 1 - c)).start()

    def finish(ins, io, outs, ssem, rsem):
        x, y, c, _ = _place()
        for t in range(n):
            theirs = io[t].at[_half(bufs[t].shape[0], 1 - c), :]
            _remote(theirs, theirs, ssem(t), rsem(t), (x, y, 1 - c)).wait_recv()
        for t in range(n):
            mine = io[t].at[_half(bufs[t].shape[0], c), :]
            _remote(mine, mine, ssem(t), rsem(t), (x, y, 1 - c)).wait_send()

    return _Job(_whole_span(start, finish), n, inouts=bufs)


def _call(body, name, grid, in_specs, out_specs, out_shape, args, jobs=(), scratch=()):
    n_in, n_out, n_scr = len(args), len(out_shape), len(scratch)
    job_in = [a for jb in jobs for a in jb.ins + jb.inouts]
    job_out = [s for jb in jobs
               for s in [jax.ShapeDtypeStruct(a.shape, a.dtype) for a in jb.inouts] + jb.outs]
    aliases, pos_in, pos_out = {}, n_in, n_out
    for jb in jobs:
        pos_in += len(jb.ins)
        for _ in jb.inouts:
            aliases[pos_in] = pos_out
            pos_in += 1
            pos_out += 1
        pos_out += len(jb.outs)
    n_sem = sum(jb.n_sem for jb in jobs)

    def wrapped(*refs):
        c_in = refs[:n_in]
        j_in = refs[n_in:n_in + len(job_in)]
        c_out = refs[n_in + len(job_in):n_in + len(job_in) + n_out]
        j_out = refs[n_in + len(job_in) + n_out:n_in + len(job_in) + n_out + len(job_out)]
        rest = refs[n_in + len(job_in) + n_out + len(job_out):]
        c_scr = rest[:n_scr]
        views, pi, po, ps = [], 0, 0, 0
        for jb in jobs:
            ins = j_in[pi:pi + len(jb.ins)]
            pi += len(jb.ins) + len(jb.inouts)
            io = j_out[po:po + len(jb.inouts)]
            new = j_out[po + len(jb.inouts):po + len(jb.inouts) + len(jb.outs)]
            po += len(jb.inouts) + len(jb.outs)
            send = (lambda i, o=ps: rest[n_scr].at[o + i])
            recv = (lambda i, o=ps: rest[n_scr + 1].at[o + i])
            ps += jb.n_sem
            views.append((ins, io, new, send, recv))

        def run(frac):
            for kind in ("finish", "start"):
                for jb, vw in zip(jobs, views):
                    for at, what, fn in jb.hooks:
                        if at == frac and what == kind:
                            fn(*vw)

        fracs = sorted({at for jb in jobs for at, _, _ in jb.hooks})
        if not grid:
            for frac in fracs:
                run(frac)
            return
        if jobs:
            assert len(grid) == 1 or set(fracs) <= {0.0, 1.0}
            first = pl.program_id(0) == 0
            last = pl.program_id(0) == grid[0] - 1
            for d in range(1, len(grid)):
                first = jnp.logical_and(first, pl.program_id(d) == 0)
                last = jnp.logical_and(last, pl.program_id(d) == grid[d] - 1)
            for frac in fracs:
                if frac < 1.0:
                    at_step = first if frac == 0.0 else pl.program_id(0) == int(frac * grid[0])
                    pl.when(at_step)(lambda frac=frac: run(frac))
        body(*c_in, *c_out, *c_scr)
        if jobs and 1.0 in fracs:
            pl.when(last)(lambda: run(1.0))

    sems = [pltpu.SemaphoreType.DMA((n_sem,)), pltpu.SemaphoreType.DMA((n_sem,))] if jobs else []
    kwargs = dict(grid=grid) if grid else {}
    res = pl.pallas_call(
        wrapped, name=name, in_specs=list(in_specs) + [ANY] * len(job_in),
        out_specs=list(out_specs) + [ANY] * len(job_out),
        out_shape=list(out_shape) + job_out, scratch_shapes=list(scratch) + sems,
        input_output_aliases=aliases, compiler_params=_params(len(grid)), **kwargs,
    )(*args, *job_in)
    outs, pos, per_job = list(res[:n_out]), n_out, []
    for jb in jobs:
        k = len(jb.inouts) + len(jb.outs)
        per_job.append(list(res[pos:pos + k]))
        pos += k
    return outs, per_job


def _comm(name, jobs):
    return _call(None, name, (), [], [], [], [], jobs)[1]


def _fwd_in(x, g_mix, w_in, bre, bim, jobs=()):
    t_len = x.shape[0]
    cs = IN_COLS // N_CHIP

    def body(x_ref, g_ref, w_ref, bre_ref, bim_ref, p_ref, h_ref, bur_ref, bui_ref):
        xv = x_ref[...]
        r, xh = _rms_stats(xv)
        h = (xh * g_ref[...]).astype(BF16)
        h_ref[...] = h
        for k in range(N_CHIP):
            p_ref[:, k * cs:(k + 1) * cs] = jnp.dot(h, w_ref[k],
                                                    preferred_element_type=F32).astype(BF16)
        u = p_ref[:, 0:SSM_W]
        for i in range(SSM_W // LANE):
            rows, cols = slice(i * LANE, (i + 1) * LANE), slice(i * DIAG_N, (i + 1) * DIAG_N)
            bur_ref[:, cols] = jnp.dot(u[:, rows], bre_ref[rows, cols],
                                       preferred_element_type=F32).astype(BF16)
            bui_ref[:, cols] = jnp.dot(u[:, rows], bim_ref[rows, cols],
                                       preferred_element_type=F32).astype(BF16)

    return _call(
        body, "fwd_in", (t_len // TB,),
        [_rows(TB, D_MODEL), _whole(), _whole(), _whole(), _whole()],
        [_rows(TB, IN_COLS), _rows(TB, D_MODEL), _rows(TB, N_STATE), _rows(TB, N_STATE)],
        [jax.ShapeDtypeStruct((t_len, IN_COLS), BF16), jax.ShapeDtypeStruct((t_len, D_MODEL), BF16),
         jax.ShapeDtypeStruct((t_len, N_STATE), BF16), jax.ShapeDtypeStruct((t_len, N_STATE), BF16)],
        [x, g_mix, w_in, bre, bim], jobs)


def _scan_local(xr, xi, tab, shifts):
    for q, s in enumerate(shifts):
        ar, ai = tab[2 * q], tab[2 * q + 1]
        rr = pltpu.roll(xr, s, 0)
        ri = pltpu.roll(xi, s, 0)
        xr, xi = xr + ar * rr - ai * ri, xi + ar * ri + ai * rr
    return xr, xi


def _scan_carry(xr, xi, tab, cr, ci):
    pr, pi = tab[6], tab[7]
    return xr + pr * cr - pi * ci, xi + pr * ci + pi * cr


BF16_TILE = 2 * SUBLANE


def _load_blocks(r_ref, i_ref, base):
    out = []
    for q in range(SCAN_UNROLL // 2):
        rows = pl.ds(pl.multiple_of(base + q * BF16_TILE, BF16_TILE), BF16_TILE)
        vr, vi = r_ref[rows, :].astype(F32), i_ref[rows, :].astype(F32)
        out += [(vr[:SUBLANE], vi[:SUBLANE]), (vr[SUBLANE:], vi[SUBLANE:])]
    return out


def _store_blocks(r_ref, i_ref, base, blocks):
    for q in range(SCAN_UNROLL // 2):
        rows = pl.ds(pl.multiple_of(base + q * BF16_TILE, BF16_TILE), BF16_TILE)
        r_ref[rows, :] = jnp.concatenate([blocks[2 * q][0], blocks[2 * q + 1][0]], 0).astype(r_ref.dtype)
        i_ref[rows, :] = jnp.concatenate([blocks[2 * q][1], blocks[2 * q + 1][1]], 0).astype(i_ref.dtype)


def _scan_fwd(bur, bui, tab, jobs=()):
    t_len = bur.shape[0]
    nblk = t_len // SUBLANE
    lb = SCAN_LANES

    def body(br_ref, bi_ref, tab_ref, sr_ref, si_ref):
        tab_v = [tab_ref[q] for q in range(8)]

        def step(k, carry):
            cr, ci = carry
            base = pl.multiple_of(k * SCAN_UNROLL * SUBLANE, SCAN_UNROLL * SUBLANE)
            local = [_scan_local(xr, xi, tab_v, (1, 2, 4))
                     for xr, xi in _load_blocks(br_ref, bi_ref, base)]
            done = []
            for xr, xi in local:
                xr, xi = _scan_carry(xr, xi, tab_v, cr, ci)
                done.append((xr, xi))
                cr, ci = xr[SUBLANE - 1:SUBLANE, :], xi[SUBLANE - 1:SUBLANE, :]
            _store_blocks(sr_ref, si_ref, base, done)
            return cr, ci

        zero = jnp.zeros((1, lb), F32)
        lax.fori_loop(0, nblk // SCAN_UNROLL, step, (zero, zero))

    col = pl.BlockSpec((t_len, lb), lambda j: (0, j))
    return _call(
        body, "scan_fwd", (N_STATE // lb,),
        [col, col, pl.BlockSpec((8, SUBLANE, lb), lambda j: (0, 0, j))], [col, col],
        [jax.ShapeDtypeStruct((t_len, N_STATE), BF16)] * 2, [bur, bui, tab], jobs)


def _sgu_mix(v, ws_ref, lane_lo):
    rows = []
    for c0 in range(0, v.shape[0], CHUNK):
        slabs = []
        for j in range(SGU_W // LANE):
            prod = jnp.dot(ws_ref[j], v[c0:c0 + CHUNK, j * LANE:(j + 1) * LANE].astype(BF16),
                           preferred_element_type=F32)
            slabs.append(jnp.where(lane_lo, prod[:CHUNK], prod[CHUNK:]))
        rows.append(jnp.concatenate(slabs, axis=1))
    return jnp.concatenate(rows, axis=0) if len(rows) > 1 else rows[0]


def _fwd_mix(x, p, str_, sti, cre, cim, d_skip, w_glu, b_glu, w_pa, g_sgu, ws_st, bmat, w_pb, w_out,
             jobs=()):
    t_len = x.shape[0]

    def body(x_ref, p_ref, sr_ref, si_ref, cre_ref, cim_ref, dsk_ref, wg_ref, bg_ref, wpa_ref,
             gs_ref, ws_ref, bm_ref, wpb_ref, wo_ref,
             x2_ref, y0_ref, z_ref, mx_ref, ya_ref, yb_ref):
        u = p_ref[:, 0:SSM_W].astype(F32)
        y0 = jnp.concatenate(
            [_dot(sr_ref[:, i * DIAG_N:(i + 1) * DIAG_N],
                  cre_ref[i * DIAG_N:(i + 1) * DIAG_N, i * LANE:(i + 1) * LANE])
             - _dot(si_ref[:, i * DIAG_N:(i + 1) * DIAG_N],
                    cim_ref[i * DIAG_N:(i + 1) * DIAG_N, i * LANE:(i + 1) * LANE])
             for i in range(SSM_W // LANE)], axis=1) + dsk_ref[...] * u
        y0_ref[...] = y0.astype(BF16)
        y1 = _gelu(y0)
        z = _dot(y1, wg_ref[...]) + bg_ref[...]
        z_ref[...] = z.astype(BF16)
        ya_pre = (y1 * _sigmoid(z)).astype(BF16)
        ya = jnp.concatenate([jnp.dot(ya_pre, wpa_ref[k], preferred_element_type=F32)
                              for k in range(N_CHIP)], axis=1)
        ya_ref[...] = ya.astype(BF16)

        uvg = _gelu(p_ref[:, SSM_W:SSM_W + 2 * SGU_W].astype(F32))
        u2 = uvg[:, :SGU_W]
        _, vh = _rms_stats(uvg[:, SGU_W:])
        v3 = vh * gs_ref[...]
        lane_lo = lax.broadcasted_iota(jnp.int32, (CHUNK, LANE), 1) < SGU_D
        bias = jnp.concatenate([bm_ref[...]] * (TB // CHUNK), axis=0)
        mixed = _sgu_mix(v3, ws_ref, lane_lo) + bias
        mx_ref[...] = mixed.astype(BF16)
        sgu = (u2 * mixed).astype(BF16)
        yb = jnp.concatenate([jnp.dot(sgu, wpb_ref[k], preferred_element_type=F32)
                              for k in range(N_CHIP)], axis=1)
        yb_ref[...] = yb.astype(BF16)

        lg0 = SSM_W + 2 * SGU_W
        ga = _sigmoid(p_ref[:, lg0:lg0 + D_MODEL].astype(F32))
        gb = _sigmoid(p_ref[:, lg0 + D_MODEL:lg0 + 2 * D_MODEL].astype(F32))
        mrg = ga * ya + gb * yb
        x2_ref[...] = x_ref[...] + _dot(mrg, wo_ref[...])

    return _call(
        body, "fwd_mix", (t_len // TB,),
        [_rows(TB, D_MODEL), _rows(TB, IN_COLS), _rows(TB, N_STATE), _rows(TB, N_STATE)]
        + [_whole()] * 11,
        [_rows(TB, D_MODEL), _rows(TB, SSM_W), _rows(TB, SSM_W), _rows(TB, SGU_W),
         _rows(TB, D_MODEL), _rows(TB, D_MODEL)],
        [jax.ShapeDtypeStruct((t_len, D_MODEL), F32), jax.ShapeDtypeStruct((t_len, SSM_W), BF16),
         jax.ShapeDtypeStruct((t_len, SSM_W), BF16), jax.ShapeDtypeStruct((t_len, SGU_W), BF16),
         jax.ShapeDtypeStruct((t_len, D_MODEL), BF16), jax.ShapeDtypeStruct((t_len, D_MODEL), BF16)],
        [x, p, str_, sti, cre, cim, d_skip, w_glu, b_glu, w_pa, g_sgu, ws_st, bmat, w_pb, w_out], jobs)


def _conv_taps(v, cw_ref, c0, width):
    w0 = cw_ref[0:1, c0:c0 + width]
    w1 = cw_ref[1:2, c0:c0 + width]
    w2 = cw_ref[2:3, c0:c0 + width]
    return w0 * pltpu.roll(v, 2, 0) + w1 * pltpu.roll(v, 1, 0) + w2 * v


def _fwd_ffn(x2, target, g_ffn, w_up, conv_w, conv_b, w_down, g_final):
    t_len = x2.shape[0]
    half = D_FF // 2
    blocks_per_halo = TB // HALO

    def body(x2_ref, xp_ref, tg_ref, gf_ref, wu_ref, cw_ref, cb_ref, wd_ref, gl_ref,
             up_ref, act_ref, f_ref, h2_ref, dx3_ref, sm_ref):
        i = pl.program_id(0)
        xe = jnp.concatenate([xp_ref[...] * jnp.where(i == 0, 0.0, 1.0), x2_ref[...]], axis=0)
        _, xh = _rms_stats(xe)
        h2 = (xh * gf_ref[...]).astype(BF16)
        h2_ref[...] = h2[HALO:]
        acc = jnp.zeros((TB, D_MODEL), F32)
        ups = [jnp.dot(h2, wu_ref[k], preferred_element_type=F32) for k in range(N_CHIP)]
        for hc in range(2):
            ca = hc * half
            cb = D_FF + hc * half
            ua, ub = ups[hc], ups[2 + hc]
            up_ref[:, ca:ca + half] = ua[HALO:].astype(BF16)
            up_ref[:, cb:cb + half] = ub[HALO:].astype(BF16)
            ac = _conv_taps(ua, cw_ref, ca, half)[HALO:] + cb_ref[:, ca:ca + half]
            bc = _conv_taps(ub, cw_ref, cb, half)[HALO:] + cb_ref[:, cb:cb + half]
            act_ref[:, ca:ca + half] = ac.astype(BF16)
            act_ref[:, cb:cb + half] = bc.astype(BF16)
            f = (ac * _sigmoid(ac) * bc).astype(BF16)
            f_ref[:, ca:ca + half] = f
            acc = acc + jnp.dot(f, wd_ref[ca:ca + half, :], preferred_element_type=F32)
        x3 = x2_ref[...] + acc
        r3, xh3 = _rms_stats(x3)
        err = xh3 * gl_ref[...] - tg_ref[...]
        dout = err * (1.0 / D_MODEL)
        dx3_ref[...] = _rms_bwd(dout * gl_ref[...], xh3, r3)
        dgl = jnp.sum(dout * xh3, axis=0, keepdims=True)
        loss = 0.5 * jnp.sum(jnp.mean(err * err, axis=-1, keepdims=True), axis=0, keepdims=True)
        upd = jnp.concatenate([dgl, jnp.broadcast_to(loss, (1, D_MODEL)),
                               jnp.zeros((SUBLANE - 2, D_MODEL), F32)], axis=0)

        @pl.when(i == 0)
        def _():
            sm_ref[...] = upd

        @pl.when(i > 0)
        def _():
            sm_ref[...] += upd

    prev = pl.BlockSpec((HALO, D_MODEL), lambda i: (jnp.maximum(i * blocks_per_halo - 1, 0), 0))
    return _call(
        body, "fwd_ffn", (t_len // TB,),
        [_rows(TB, D_MODEL), prev, _rows(TB, D_MODEL)] + [_whole()] * 6,
        [_rows(TB, 2 * D_FF), _rows(TB, 2 * D_FF), _rows(TB, D_FF), _rows(TB, D_MODEL),
         _rows(TB, D_MODEL), _acc(SUBLANE, D_MODEL)],
        [jax.ShapeDtypeStruct((t_len, 2 * D_FF), BF16), jax.ShapeDtypeStruct((t_len, 2 * D_FF), BF16),
         jax.ShapeDtypeStruct((t_len, D_FF), BF16), jax.ShapeDtypeStruct((t_len, D_MODEL), BF16),
         jax.ShapeDtypeStruct((t_len, D_MODEL), F32), jax.ShapeDtypeStruct((SUBLANE, D_MODEL), F32)],
        [x2, x2, target, g_ffn, w_up, conv_w, conv_b, w_down, g_final])[0]


def _bwd_ffn(dx3, up, act, x2, g_ffn, w_up, conv_w, w_down, jobs=()):
    t_len = x2.shape[0]
    half = D_FF // 2
    nblk = t_len // TB
    halo_b = 2 * HALO
    n_e = TB + HALO

    def body(dx_ref, dxn_ref, up_ref, act_ref, actn_ref, x2_ref, gf_ref, wu_ref, cw_ref,
             wd_ref, dx2_ref, dup_ref, smw_ref, smg_ref):
        i = pl.program_id(0)
        keep_last = jnp.where(i == nblk - 1, 0.0, 1.0)
        dxe = jnp.concatenate([dx_ref[...], dxn_ref[...] * keep_last], axis=0).astype(BF16)
        dh2 = jnp.zeros((TB, D_MODEL), F32)
        zpad = jnp.zeros((1, half), F32)
        dfs = [lax.dot_general(dxe, wd_ref[hc * half:(hc + 1) * half, :], (((1,), (1,)), ((), ())),
                               preferred_element_type=F32) for hc in range(2)]
        for hc in range(2):
            ca = hc * half
            cb = D_FF + hc * half
            ac = jnp.concatenate([act_ref[:, ca:ca + half].astype(F32),
                                  actn_ref[:, ca:ca + half].astype(F32)[:HALO]], axis=0)
            bc = jnp.concatenate([act_ref[:, cb:cb + half].astype(F32),
                                  actn_ref[:, cb:cb + half].astype(F32)[:HALO]], axis=0)
            wa = [cw_ref[k:k + 1, ca:ca + half] for k in range(3)]
            wb = [cw_ref[k:k + 1, cb:cb + half] for k in range(3)]
            df = dfs[hc]
            sg = _sigmoid(ac)
            da = df * bc * sg * (1.0 + ac * (1.0 - sg))
            db = df * ac * sg
            da1, da2 = pltpu.roll(da, n_e - 1, 0), pltpu.roll(da, n_e - 2, 0)
            db1, db2 = pltpu.roll(db, n_e - 1, 0), pltpu.roll(db, n_e - 2, 0)
            dua = (wa[2] * da + wa[1] * da1 + wa[0] * da2)[:TB]
            dub = (wb[2] * db + wb[1] * db1 + wb[0] * db2)[:TB]
            dup_ref[:, ca:ca + half] = dua.astype(BF16)
            dup_ref[:, cb:cb + half] = dub.astype(BF16)
            dh2 = dh2 + _dot_nt(dua, wu_ref[hc]) + _dot_nt(dub, wu_ref[2 + hc])
            rows = []
            for u_, d0, d1, d2 in ((up_ref[:, ca:ca + half].astype(F32), da, da1, da2),
                                   (up_ref[:, cb:cb + half].astype(F32), db, db1, db2)):
                rows.append([jnp.sum(u_ * d2[:TB], axis=0, keepdims=True),
                             jnp.sum(u_ * d1[:TB], axis=0, keepdims=True),
                             jnp.sum(u_ * d0[:TB], axis=0, keepdims=True),
                             jnp.sum(d0[:TB], axis=0, keepdims=True)])
            for c0, rws in ((ca, rows[0]), (cb, rows[1])):
                upd = jnp.concatenate(rws + [zpad] * (SUBLANE - 4), axis=0)

                @pl.when(i == 0)
                def _(upd=upd, c0=c0):
                    smw_ref[:, c0:c0 + half] = upd

                @pl.when(i > 0)
                def _(upd=upd, c0=c0):
                    smw_ref[:, c0:c0 + half] += upd

        r2, xh2 = _rms_stats(x2_ref[...])
        dx2_ref[...] = dx_ref[...] + _rms_bwd(dh2 * gf_ref[...], xh2, r2)
        updg = jnp.concatenate([jnp.sum(dh2 * xh2, axis=0, keepdims=True),
                                jnp.zeros((SUBLANE - 1, D_MODEL), F32)], axis=0)

        @pl.when(i == 0)
        def _():
            smg_ref[...] = updg

        @pl.when(i > 0)
        def _():
            smg_ref[...] += updg

    nxt_d = pl.BlockSpec((HALO, D_MODEL),
                         lambda i: (jnp.minimum((i + 1) * (TB // HALO), t_len // HALO - 1), 0))
    nxt_a = pl.BlockSpec((halo_b, 2 * D_FF),
                         lambda i: (jnp.minimum((i + 1) * (TB // halo_b), t_len // halo_b - 1), 0))
    return _call(
        body, "bwd_ffn", (nblk,),
        [_rows(TB, D_MODEL), nxt_d, _rows(TB, 2 * D_FF), _rows(TB, 2 * D_FF), nxt_a,
         _rows(TB, D_MODEL)] + [_whole()] * 4,
        [_rows(TB, D_MODEL), _rows(TB, 2 * D_FF), _acc(SUBLANE, 2 * D_FF), _acc(SUBLANE, D_MODEL)],
        [jax.ShapeDtypeStruct((t_len, D_MODEL), F32), jax.ShapeDtypeStruct((t_len, 2 * D_FF), BF16),
         jax.ShapeDtypeStruct((SUBLANE, 2 * D_FF), F32), jax.ShapeDtypeStruct((SUBLANE, D_MODEL), F32)],
        [dx3, dx3, up, act, act, x2, g_ffn, w_up, conv_w, w_down], jobs)


def _bwd_mix(dx2, p, y0, z, mixed, ya, yb, w_out, w_pa, w_pb, w_glu, cre, cim, ws_st, wst_st,
             d_skip, g_sgu, jobs=()):
    t_len = dx2.shape[0]
    pc = D_MODEL // N_CHIP
    n_slab = SGU_W // LANE

    def body(dx_ref, p_ref, y0_ref, z_ref, mx_ref, ya_ref, yb_ref, wo_ref, wpa_ref, wpb_ref,
             wg_ref, cre_ref, cim_ref, ws_ref, wst_ref, dsk_ref, gs_ref,
             dsr_ref, dsi_ref, du_ref, drest_ref, mrg_ref, dya_ref, dyb_ref, yap_ref, dz_ref,
             y1_ref, sgu_ref, dy0_ref, sm_ref, dbm_ref, dws_ref):
        i = pl.program_id(0)
        first = i == 0
        lg0 = SSM_W + 2 * SGU_W
        ga = _sigmoid(p_ref[:, lg0:lg0 + D_MODEL].astype(F32))
        gb = _sigmoid(p_ref[:, lg0 + D_MODEL:lg0 + 2 * D_MODEL].astype(F32))
        yav = ya_ref[...].astype(F32)
        ybv = yb_ref[...].astype(F32)
        mrg_ref[...] = (ga * yav + gb * ybv).astype(BF16)
        y0v = y0_ref[...].astype(F32)
        y1, y1_grad = _gelu_and_grad(y0v)
        sz = _sigmoid(z_ref[...].astype(F32))
        y1_ref[...] = y1.astype(BF16)
        yap_ref[...] = (y1 * sz).astype(BF16)

        dmrg = _dot_nt(dx_ref[...], wo_ref[...])
        drest_ref[:, 2 * SGU_W:2 * SGU_W + D_MODEL] = (dmrg * yav * ga * (1.0 - ga)).astype(BF16)
        drest_ref[:, 2 * SGU_W + D_MODEL:] = (dmrg * ybv * gb * (1.0 - gb)).astype(BF16)
        dya = (dmrg * ga).astype(BF16)
        dyb = (dmrg * gb).astype(BF16)
        dya_ref[...] = dya
        dyb_ref[...] = dyb

        dyap = jnp.zeros((TB, SSM_W), F32)
        for k in range(N_CHIP):
            dyap = dyap + _dot_nt(dya[:, k * pc:(k + 1) * pc], wpa_ref[k])
        dz = dyap * y1 * sz * (1.0 - sz)
        dz_ref[...] = dz.astype(BF16)
        dy0 = (dyap * sz + _dot_nt(dz, wg_ref[...])) * y1_grad
        dy0_ref[...] = dy0.astype(BF16)
        u = p_ref[:, 0:SSM_W].astype(F32)
        du_ref[...] = dy0 * dsk_ref[...]
        for q in range(SSM_W // LANE):
            rows, cols = slice(q * DIAG_N, (q + 1) * DIAG_N), slice(q * LANE, (q + 1) * LANE)
            dsr_ref[:, rows] = _dot_nt(dy0[:, cols], cre_ref[rows, cols]).astype(BF16)
            dsi_ref[:, rows] = (-_dot_nt(dy0[:, cols], cim_ref[rows, cols])).astype(BF16)

        uv = p_ref[:, SSM_W:lg0].astype(F32)
        uvg, gg = _gelu_and_grad(uv)
        u2 = uvg[:, :SGU_W]
        rv, vh = _rms_stats(uvg[:, SGU_W:])
        v3 = vh * gs_ref[...]
        mixed = mx_ref[...].astype(F32)
        dsgu = jnp.zeros((TB, SGU_W), F32)
        for k in range(N_CHIP):
            dsgu = dsgu + _dot_nt(dyb[:, k * pc:(k + 1) * pc], wpb_ref[k])
        sgu_ref[...] = (u2 * mixed).astype(BF16)
        drest_ref[:, 0:SGU_W] = (dsgu * mixed * gg[:, :SGU_W]).astype(BF16)
        dmix = dsgu * u2
        lane_lo = lax.broadcasted_iota(jnp.int32, (CHUNK, LANE), 1) < SGU_D
        dv3 = _sgu_mix(dmix, wst_ref, lane_lo)
        dbm = jnp.zeros((CHUNK, SGU_W), F32)
        for c0 in range(0, TB, CHUNK):
            dbm = dbm + dmix[c0:c0 + CHUNK]
        for j in range(n_slab):
            lo = jnp.zeros((CHUNK, CHUNK), F32)
            hi = jnp.zeros((CHUNK, CHUNK), F32)
            for c0 in range(0, TB, CHUNK):
                dsl = dmix[c0:c0 + CHUNK, j * LANE:(j + 1) * LANE]
                vsl = v3[c0:c0 + CHUNK, j * LANE:(j + 1) * LANE]
                lo = lo + _dot_nt(jnp.where(lane_lo, dsl, 0.0), vsl)
                hi = hi + _dot_nt(jnp.where(lane_lo, 0.0, dsl), vsl)

            @pl.when(first)
            def _(lo=lo, hi=hi, j=j):
                dws_ref[2 * j] = lo
                dws_ref[2 * j + 1] = hi

            @pl.when(jnp.logical_not(first))
            def _(lo=lo, hi=hi, j=j):
                dws_ref[2 * j] += lo
                dws_ref[2 * j + 1] += hi

        dv2 = _rms_bwd(dv3 * gs_ref[...], vh, rv)
        drest_ref[:, SGU_W:2 * SGU_W] = (dv2 * gg[:, SGU_W:]).astype(BF16)

        upd = jnp.concatenate([jnp.sum(dy0 * u, axis=0, keepdims=True),
                               jnp.sum(dz, axis=0, keepdims=True),
                               jnp.sum(dv3 * vh, axis=0, keepdims=True),
                               jnp.zeros((SUBLANE - 3, SSM_W), F32)], axis=0)

        @pl.when(first)
        def _():
            sm_ref[...] = upd
            dbm_ref[...] = dbm

        @pl.when(jnp.logical_not(first))
        def _():
            sm_ref[...] += upd
            dbm_ref[...] += dbm

    rest = 2 * SGU_W + 2 * D_MODEL
    bf_d, bf_s = jax.ShapeDtypeStruct((t_len, D_MODEL), BF16), jax.ShapeDtypeStruct((t_len, SSM_W), BF16)
    return _call(
        body, "bwd_mix", (t_len // TB,),
        [_rows(TB, D_MODEL), _rows(TB, IN_COLS), _rows(TB, SSM_W), _rows(TB, SSM_W),
         _rows(TB, SGU_W), _rows(TB, D_MODEL), _rows(TB, D_MODEL)] + [_whole()] * 10,
        [_rows(TB, N_STATE), _rows(TB, N_STATE), _rows(TB, SSM_W), _rows(TB, rest),
         _rows(TB, D_MODEL), _rows(TB, D_MODEL), _rows(TB, D_MODEL), _rows(TB, SSM_W),
         _rows(TB, SSM_W), _rows(TB, SSM_W), _rows(TB, SGU_W), _rows(TB, SSM_W),
         _acc(SUBLANE, SSM_W), _acc(CHUNK, SGU_W),
         pl.BlockSpec((SGU_G, CHUNK, CHUNK), lambda i: (0, 0, 0))],
        [jax.ShapeDtypeStruct((t_len, N_STATE), BF16), jax.ShapeDtypeStruct((t_len, N_STATE), BF16),
         jax.ShapeDtypeStruct((t_len, SSM_W), F32), jax.ShapeDtypeStruct((t_len, rest), BF16),
         bf_d, bf_d, bf_d, bf_s, bf_s, bf_s, bf_s, bf_s,
         jax.ShapeDtypeStruct((SUBLANE, SSM_W), F32), jax.ShapeDtypeStruct((CHUNK, SGU_W), F32),
         jax.ShapeDtypeStruct((SGU_G, CHUNK, CHUNK), F32)],
        [dx2, p, y0, z, mixed, ya, yb, w_out, w_pa, w_pb, w_glu, cre, cim, ws_st, wst_st, d_skip,
         g_sgu], jobs)


def _scan_bwd(dsr, dsi, str_, sti, tab_rev, jobs=()):
    t_len = dsr.shape[0]
    nblk = t_len // SUBLANE
    lb = SCAN_LANES

    def body(dr_ref, di_ref, sr_ref, si_ref, tab_ref, lr_ref, li_ref, dar_ref, dai_ref):
        tab_v = [tab_ref[q] for q in range(8)]
        row0 = lax.broadcasted_iota(jnp.int32, (SUBLANE, lb), 0) == 0
        tile = BF16_TILE

        def step(k, carry):
            cr, ci, acr, aci = carry
            base = pl.multiple_of((nblk - (k + 1) * SCAN_UNROLL) * SUBLANE, SCAN_UNROLL * SUBLANE)
            state = _load_blocks(sr_ref, si_ref, base)
            before = pl.ds(pl.multiple_of(jnp.maximum(base - tile, 0), tile), tile)
            has_before = jnp.where(base > 0, 1.0, 0.0)
            prev = (sr_ref[before, :].astype(F32)[tile - 1:tile] * has_before,
                    si_ref[before, :].astype(F32)[tile - 1:tile] * has_before)
            local = [_scan_local(xr, xi, tab_v, (7, 6, 4))
                     for xr, xi in _load_blocks(dr_ref, di_ref, base)]
            lam = [None] * SCAN_UNROLL
            for b in reversed(range(SCAN_UNROLL)):
                xr, xi = _scan_carry(*local[b], tab_v, cr, ci)
                lam[b] = (xr, xi)
                cr, ci = xr[0:1, :], xi[0:1, :]
                pr, pi = prev if b == 0 else (state[b - 1][0][SUBLANE - 1:], state[b - 1][1][SUBLANE - 1:])
                s_r = jnp.where(row0, pr, pltpu.roll(state[b][0], 1, 0))
                s_i = jnp.where(row0, pi, pltpu.roll(state[b][1], 1, 0))
                acr = acr + xr * s_r + xi * s_i
                aci = aci + xi * s_r - xr * s_i
            _store_blocks(lr_ref, li_ref, base, lam)
            return cr, ci, acr, aci

        zero = jnp.zeros((1, lb), F32)
        zacc = jnp.zeros((SUBLANE, lb), F32)
        _, _, acr, aci = lax.fori_loop(0, nblk // SCAN_UNROLL, step, (zero, zero, zacc, zacc))
        dar_ref[...] = acr
        dai_ref[...] = aci

    col = pl.BlockSpec((t_len, lb), lambda j: (0, j))
    small = pl.BlockSpec((SUBLANE, lb), lambda j: (0, j))
    return _call(
        body, "scan_bwd", (N_STATE // lb,),
        [col, col, col, col, pl.BlockSpec((8, SUBLANE, lb), lambda j: (0, 0, j))],
        [col, col, small, small],
        [jax.ShapeDtypeStruct((t_len, N_STATE), BF16)] * 2
        + [jax.ShapeDtypeStruct((SUBLANE, N_STATE), F32)] * 2,
        [dsr, dsi, str_, sti, tab_rev], jobs)


def _bwd_in(lam_r, lam_i, du_part, drest, x, dx2, g_mix, w_in, bre, bim, jobs=()):
    t_len = x.shape[0]
    cs = IN_COLS // N_CHIP

    def body(lr_ref, li_ref, du_ref, dr_ref, x_ref, dx2_ref, g_ref, w_ref, bre_ref, bim_ref,
             gx_ref, dp_ref, sm_ref):
        i = pl.program_id(0)
        du = du_ref[...] + jnp.concatenate(
            [_dot_nt(lr_ref[:, i * DIAG_N:(i + 1) * DIAG_N],
                     bre_ref[i * LANE:(i + 1) * LANE, i * DIAG_N:(i + 1) * DIAG_N])
             + _dot_nt(li_ref[:, i * DIAG_N:(i + 1) * DIAG_N],
                       bim_ref[i * LANE:(i + 1) * LANE, i * DIAG_N:(i + 1) * DIAG_N])
             for i in range(SSM_W // LANE)], axis=1)
        dp_ref[:, 0:SSM_W] = du.astype(BF16)
        dp_ref[:, SSM_W:] = dr_ref[...]
        dh = jnp.zeros((TB, D_MODEL), F32)
        for k in range(N_CHIP):
            dh = dh + _dot_nt(dp_ref[:, k * cs:(k + 1) * cs], w_ref[k])
        r, xh = _rms_stats(x_ref[...])
        gx_ref[...] = dx2_ref[...] + _rms_bwd(dh * g_ref[...], xh, r)
        upd = jnp.concatenate([jnp.sum(dh * xh, axis=0, keepdims=True),
                               jnp.zeros((SUBLANE - 1, D_MODEL), F32)], axis=0)

        @pl.when(i == 0)
        def _():
            sm_ref[...] = upd

        @pl.when(i > 0)
        def _():
            sm_ref[...] += upd

    return _call(
        body, "bwd_in", (t_len // TB,),
        [_rows(TB, N_STATE), _rows(TB, N_STATE), _rows(TB, SSM_W), _rows(TB, IN_COLS - SSM_W),
         _rows(TB, D_MODEL), _rows(TB, D_MODEL)] + [_whole()] * 4,
        [_rows(TB, D_MODEL), _rows(TB, IN_COLS), _acc(SUBLANE, D_MODEL)],
        [jax.ShapeDtypeStruct((t_len, D_MODEL), F32), jax.ShapeDtypeStruct((t_len, IN_COLS), BF16),
         jax.ShapeDtypeStruct((SUBLANE, D_MODEL), F32)],
        [lam_r, lam_i, du_part, drest, x, dx2, g_mix, w_in, bre, bim], jobs)


def _matmul_tn(a, b, name, out_shape, grid_ij, a_blk, a_map, b_blk, b_map, o_blk, o_map, jobs=()):
    tk = a_blk[0]
    nk = a.shape[0] // tk
    assert nk * tk == a.shape[0] and nk > 0

    def body(a_ref, b_ref, o_ref, acc_ref):
        k = pl.program_id(2)

        @pl.when(k == 0)
        def _():
            acc_ref[...] = jnp.zeros_like(acc_ref)

        acc_ref[...] += lax.dot_general(a_ref[...].astype(BF16), b_ref[...].astype(BF16),
                                        (((0,), (0,)), ((), ())), preferred_element_type=F32)

        @pl.when(k == nk - 1)
        def _():
            o_ref[...] = acc_ref[...]

    outs, per_job = _call(
        body, name, (grid_ij[0], grid_ij[1], nk),
        [pl.BlockSpec(a_blk, a_map), pl.BlockSpec(b_blk, b_map)], [pl.BlockSpec(o_blk, o_map)],
        [jax.ShapeDtypeStruct(out_shape, F32)], [a, b], jobs,
        scratch=[pltpu.VMEM((a_blk[1], b_blk[1]), F32)])
    return outs[0], per_job


def _dw_rows(a, b, name, tm, tk):
    m, n = a.shape[1], b.shape[1]
    tk = min(tk, a.shape[0])
    return _matmul_tn(a, b, name, (m, n), (m // tm, 1),
                      (tk, tm), lambda i, j, k: (k, i), (tk, n), lambda i, j, k: (k, 0),
                      (tm, n), lambda i, j, k: (i, 0))[0]


def _dw_cols(a, b, name, tn, sharded, jobs=()):
    t_len, m = a.shape
    n = b.shape[1]

    def body(a_ref, b_ref, o_ref):
        o_ref[...] = lax.dot_general(a_ref[...].astype(BF16), b_ref[...].astype(BF16),
                                     (((0,), (0,)), ((), ())), preferred_element_type=F32)

    if sharded:
        o_spec, o_shape = pl.BlockSpec((None, m, tn), lambda j: (j, 0, 0)), (n // tn, m, tn)
    else:
        o_spec, o_shape = pl.BlockSpec((m, tn), lambda j: (0, j)), (m, n)
    outs, per_job = _call(body, name, (n // tn,),
                          [_whole(), pl.BlockSpec((t_len, tn), lambda j: (0, j))], [o_spec],
                          [jax.ShapeDtypeStruct(o_shape, F32)], [a, b], jobs)
    return outs[0], per_job


def _dw_tiles(a, b, name, tm, tn, jobs=()):
    t_len, m = a.shape
    n = b.shape[1]

    def body(a_ref, b_ref, o_ref):
        o_ref[...] = lax.dot_general(a_ref[...].astype(BF16), b_ref[...].astype(BF16),
                                     (((0,), (0,)), ((), ())), preferred_element_type=F32)

    outs, per_job = _call(body, name, (n // tn, m // tm),
                          [pl.BlockSpec((t_len, tm), lambda j, i: (0, i)),
                           pl.BlockSpec((t_len, tn), lambda j, i: (0, j))],
                          [pl.BlockSpec((None, tm, tn), lambda j, i: (j, i, 0))],
                          [jax.ShapeDtypeStruct((n // tn, m, tn), F32)], [a, b], jobs)
    return outs[0], per_job


def _dw_pair(a, m, b1, b2, name, jobs=()):
    t_len = a.shape[0]
    n_slab = DIAG_N // LANE
    rows_per_slab = LANE // n_slab

    def body(a_ref, b1_ref, b2_ref, o1_ref, o2_ref):
        for b_ref, o_ref in ((b1_ref, o1_ref), (b2_ref, o2_ref)):
            prod = lax.dot_general(a_ref[...].astype(BF16), b_ref[...].astype(BF16),
                                   (((0,), (0,)), ((), ())), preferred_element_type=F32)
            for j in range(n_slab):
                rows = slice(j * rows_per_slab, (j + 1) * rows_per_slab)
                o_ref[rows, :] = prod[rows, j * LANE:(j + 1) * LANE]

    tok = pl.BlockSpec((t_len, DIAG_N), lambda i: (0, i))
    out = pl.BlockSpec((LANE, LANE), lambda i: (i, 0))
    return _call(body, name, (m // LANE,),
                 [pl.BlockSpec((t_len, LANE), lambda i: (0, i)), tok, tok], [out, out],
                 [jax.ShapeDtypeStruct((m, LANE), F32)] * 2, [a, b1, b2], jobs)


def _prefetch_call(body, name, grid, scalars, in_specs, out_specs, out_shape, args):
    return pl.pallas_call(
        body, name=name,
        grid_spec=pltpu.PrefetchScalarGridSpec(num_scalar_prefetch=1, grid=grid, in_specs=in_specs,
                                               out_specs=out_specs),
        out_shape=out_shape, compiler_params=_params(len(grid)),
    )(scalars, *args)


def _place_shard(w, where, name, dtype, tr):
    rows, cols = w.shape

    def body(s_ref, w_ref, o_ref):
        o_ref[...] = w_ref[...].astype(dtype)

    return _prefetch_call(
        body, name, (rows // tr,), where,
        [pl.BlockSpec((tr, cols), lambda i, s: (i, 0))],
        pl.BlockSpec((None, tr, cols), lambda i, s: (s[0], i, 0)),
        jax.ShapeDtypeStruct((N_CHIP, rows, cols), dtype), [w])


def _place_shards(ws, where, name, dtype):
    n = len(ws)

    def body(s_ref, *refs):
        for t in range(n):
            refs[n + t][...] = refs[t][...].astype(dtype)

    return _prefetch_call(
        body, name, (1,), where,
        [pl.BlockSpec(w.shape, lambda i, s: (0, 0)) for w in ws],
        [pl.BlockSpec((None,) + w.shape, lambda i, s: (s[0], 0, 0)) for w in ws],
        [jax.ShapeDtypeStruct((N_CHIP,) + w.shape, dtype) for w in ws], ws)


def _add_sibling(gs, gots, where, name):
    n = len(gs)
    halves = [(g.shape[1] // 2, g.shape[2]) for g in gs]

    def body(s_ref, *refs):
        for t in range(n):
            refs[2 * n + t][...] = (refs[t][...] + refs[n + t][...]).astype(BF16)

    return _prefetch_call(
        body, name, (N_CHIP,), where,
        [pl.BlockSpec((None, hr, cs), lambda k, s: (k, s[1], 0)) for hr, cs in halves]
        + [pl.BlockSpec((None, hr, cs), lambda k, s: (k, 0, 0)) for hr, cs in halves],
        [pl.BlockSpec((None, hr, cs), lambda k, s: (k, 0, 0)) for hr, cs in halves],
        [jax.ShapeDtypeStruct((N_CHIP, hr, cs), BF16) for hr, cs in halves], list(gs) + list(gots))


def _add_chips(sums, gots, where, name):
    n = len(sums)
    halves = [s.shape[1:] for s in sums]

    def body(s_ref, *refs):
        for t in range(n):
            own_ref, got_ref = refs[t], refs[n + t]
            refs[2 * n + t][...] = ((own_ref[...].astype(F32) + got_ref[0].astype(F32))
                                    + got_ref[1].astype(F32)) + got_ref[2].astype(F32)

    return _prefetch_call(
        body, name, (1,), where,
        [pl.BlockSpec((None, hr, cs), lambda i, s: (s[0], 0, 0)) for hr, cs in halves]
        + [pl.BlockSpec((3, hr, cs), lambda i, s: (0, 0, 0)) for hr, cs in halves],
        [pl.BlockSpec((hr, cs), lambda i, s: (s[1], 0)) for hr, cs in halves],
        [jax.ShapeDtypeStruct((2 * hr, cs), F32) for hr, cs in halves], list(sums) + list(gots))


def _small_allreduce(pack):
    rows = pack.shape[0]
    half = rows // 2

    def body(in_ref, out_ref, sib_ref, slots_ref, s_a, r_a, s_b, r_b, s_c, r_c):
        x, y, c, chips = _place()
        k_me = 2 * x + y
        sib = (x, y, 1 - c)
        first = _remote(in_ref, sib_ref, s_a, r_a, sib)
        first.start()
        first.wait()
        mine = _half(rows, c)
        slots_ref[k_me] = in_ref[mine, :] + sib_ref[mine, :]
        cps = [_remote(slots_ref.at[k_me], slots_ref.at[k_me], s_b.at[j], r_b.at[j], (*ch, c))
               for j, ch in enumerate(chips)]
        for cp in cps:
            cp.start()
        for j, ch in enumerate(chips):
            slot = slots_ref.at[_chip_index(ch)]
            _remote(slot, slot, s_b.at[j], r_b.at[j], (*ch, c)).wait_recv()
        for cp in cps:
            cp.wait_send()
        out_ref[mine, :] = ((slots_ref[0] + slots_ref[1]) + slots_ref[2]) + slots_ref[3]
        last = _remote(out_ref.at[mine, :], out_ref.at[mine, :], s_c, r_c, sib)
        last.start()
        theirs = out_ref.at[_half(rows, 1 - c), :]
        _remote(theirs, theirs, s_c, r_c, sib).wait_recv()
        last.wait_send()

    return pl.pallas_call(
        body, name="small_allreduce", in_specs=[_whole()], out_specs=_whole(),
        out_shape=jax.ShapeDtypeStruct(pack.shape, F32),
        scratch_shapes=[pltpu.VMEM(pack.shape, F32), pltpu.VMEM((N_CHIP, half, LANE), F32),
                        pltpu.SemaphoreType.DMA, pltpu.SemaphoreType.DMA,
                        pltpu.SemaphoreType.DMA((3,)), pltpu.SemaphoreType.DMA((3,)),
                        pltpu.SemaphoreType.DMA, pltpu.SemaphoreType.DMA],
        compiler_params=_params(0),
    )(pack)


def _adamw_update(w_ref, g_ref, m_ref, v_ref, d_ref, mo_ref, vo_ref):
    gv = g_ref[...]
    mn = ADAM_B1 * m_ref[...] + (1.0 - ADAM_B1) * gv
    vn = ADAM_B2 * v_ref[...] + (1.0 - ADAM_B2) * (gv * gv)
    mo_ref[...] = mn
    vo_ref[...] = vn
    m_hat = mn / (1.0 - ADAM_B1 ** ADAM_STEP)
    v_hat = vn / (1.0 - ADAM_B2 ** ADAM_STEP)
    d_ref[...] = -ADAM_LR * (m_hat / (jnp.sqrt(v_hat) + ADAM_EPS) + ADAM_WD * w_ref[...])


def _adamw(w, g, m, v, name, tr):
    rows, cols = w.shape
    blk = _rows(tr, cols)

    def body(w_ref, g_ref, m_ref, v_ref, go_ref, d_ref, mo_ref, vo_ref):
        go_ref[...] = g_ref[...]
        _adamw_update(w_ref, g_ref, m_ref, v_ref, d_ref, mo_ref, vo_ref)

    return _call(body, name, (rows // tr,), [blk] * 4, [blk] * 4,
                 [jax.ShapeDtypeStruct(w.shape, F32)] * 4, [w, g, m, v])[0]


def _adamw_many(ws, gs, ms, vs, name):
    n = len(ws)

    def body(*refs):
        for t in range(n):
            _adamw_update(*[refs[q * n + t] for q in range(7)])

    specs = [pl.BlockSpec(a.shape, lambda i, nd=a.ndim: (0,) * nd) for a in ws]
    outs = pl.pallas_call(
        body, name=name, grid=(1,), in_specs=specs * 4, out_specs=specs * 3,
        out_shape=[jax.ShapeDtypeStruct(a.shape, F32) for _ in range(3) for a in ws],
        compiler_params=_params(1),
    )(*ws, *gs, *ms, *vs)
    return outs[:n], outs[n:2 * n], outs[2 * n:]


def _ssm_discretize(a_re, a_im, log_dt, b_re, b_im):
    dt = jnp.exp(log_dt)[:, None]
    mag = jnp.exp(dt * a_re)
    abr = mag * jnp.cos(dt * a_im)
    abi = mag * jnp.sin(dt * a_im)
    den = a_re * a_re + a_im * a_im
    nr = abr - 1.0
    ni = abi
    f_re = (nr * a_re + ni * a_im) / den
    f_im = (ni * a_re - nr * a_im) / den
    bbr = f_re[..., None] * b_re - f_im[..., None] * b_im
    bbi = f_re[..., None] * b_im + f_im[..., None] * b_re
    return abr, abi, bbr, bbi


def _scan_tables(abr, abi):
    ar = abr.reshape(1, N_STATE)
    ai = abi.reshape(1, N_STATE)
    pr, pi = [ar], [ai]
    for _ in range(SUBLANE - 1):
        pr, pi = pr + [pr[-1] * ar - pi[-1] * ai], pi + [pr[-1] * ai + pi[-1] * ar]
    row = jnp.arange(SUBLANE)[:, None]
    tabs = []
    for d in (1, 2, 4):
        tabs.append(jnp.where(row >= d, pr[d - 1], 0.0))
        tabs.append(jnp.where(row >= d, pi[d - 1], 0.0))
    tabs.append(jnp.concatenate(pr, axis=0))
    tabs.append(jnp.concatenate(pi, axis=0))
    fwd = jnp.stack(tabs)
    sign = jnp.array([1.0, -1.0] * 4, F32)[:, None, None]
    return fwd, fwd[:, ::-1, :] * sign


def _block_diag_b(bb):
    strip = bb.transpose(2, 0, 1).reshape(SSM_H, N_STATE)
    rows = lax.broadcasted_iota(jnp.int32, (SSM_W, N_STATE), 0) // SSM_H
    cols = lax.broadcasted_iota(jnp.int32, (SSM_W, N_STATE), 1) // SSM_P
    return jnp.where(rows == cols, jnp.tile(strip, (SSM_G, 1)), 0.0).astype(BF16)


def _block_diag_c(cc):
    strip = cc.transpose(0, 2, 1).reshape(N_STATE, SSM_H)
    rows = lax.broadcasted_iota(jnp.int32, (N_STATE, SSM_W), 0) // SSM_P
    cols = lax.broadcasted_iota(jnp.int32, (N_STATE, SSM_W), 1) // SSM_H
    return jnp.where(rows == cols, jnp.tile(strip, (1, SSM_G)), 0.0).astype(BF16)


SMALL_SHAPES = {
    "g_mix": (D_MODEL,), "a_re": (SSM_G, SSM_P), "a_im": (SSM_G, SSM_P), "log_dt": (SSM_G,),
    "b_re": (SSM_G, SSM_P, SSM_H), "b_im": (SSM_G, SSM_P, SSM_H),
    "c_re": (SSM_G, SSM_H, SSM_P), "c_im": (SSM_G, SSM_H, SSM_P),
    "d_skip": (SSM_W,), "b_glu": (SSM_W,), "g_sgu": (SGU_W,), "w_s": (SGU_G, CHUNK, CHUNK),
    "b_s": (SGU_G, CHUNK), "g_ffn": (D_MODEL,), "conv_b": (2 * D_FF,), "g_final": (D_MODEL,),
}
PACK_ITEMS = [("loss", (1,))] + [(n, SMALL_SHAPES[n]) for n in SMALL] + [("conv_w", (3, 2 * D_FF))]
TILE = SUBLANE * LANE


def _item_rows(shape):
    return -(-math.prod(shape) // TILE) * SUBLANE


PACK_ROWS = -(-sum(_item_rows(s) for _, s in PACK_ITEMS) // (2 * SUBLANE)) * (2 * SUBLANE)


def _pack(values):
    parts, used = [], 0
    for name, shape in PACK_ITEMS:
        size, rows = math.prod(shape), _item_rows(shape)
        if name in values:
            flat = values[name].astype(F32).reshape(size)
            if rows * LANE > size:
                flat = jnp.pad(flat, (0, rows * LANE - size))
            parts.append(flat.reshape(rows, LANE))
        else:
            parts.append(jnp.zeros((rows, LANE), F32))
        used += rows
    if PACK_ROWS > used:
        parts.append(jnp.zeros((PACK_ROWS - used, LANE), F32))
    return jnp.concatenate(parts, axis=0)


def _unpack(pack):
    out, off = {}, 0
    for name, shape in PACK_ITEMS:
        rows = _item_rows(shape)
        out[name] = pack[off:off + rows].reshape(rows * LANE)[:math.prod(shape)].reshape(shape)
        off += rows
    return out


PLACE_ROWS = {"w_in": 256, "w_up": 256, "w_down": 352, "w_out": 256, "w_proj_a": 256,
              "w_proj_b": 256, "w_glu": 128}


def kernel(x, g_mix, w_in, a_re, a_im, log_dt, b_re, b_im, c_re, c_im, d_skip, w_glu, b_glu, w_proj_a, g_sgu, w_s, b_s, w_proj_b, w_out, g_ffn, w_up, conv_w, conv_b, w_down, g_final, loss_target, m_g_mix, m_w_in, m_a_re, m_a_im, m_log_dt, m_b_re, m_b_im, m_c_re, m_c_im, m_d_skip, m_w_glu, m_b_glu, m_w_proj_a, m_g_sgu, m_w_s, m_b_s, m_w_proj_b, m_w_out, m_g_ffn, m_w_up, m_conv_w, m_conv_b, m_w_down, m_g_final, v_g_mix, v_w_in, v_a_re, v_a_im, v_log_dt, v_b_re, v_b_im, v_c_re, v_c_im, v_d_skip, v_w_glu, v_b_glu, v_w_proj_a, v_g_sgu, v_w_s, v_b_s, v_w_proj_b, v_w_out, v_g_ffn, v_w_up, v_conv_w, v_conv_b, v_w_down, v_g_final):
    given = dict(locals())
    w = {n: given[n] for n in WEIGHTS}
    m = {n: given["m_" + n] for n in WEIGHTS}
    v = {n: given["v_" + n] for n in WEIGHTS}

    def shard2d(a):
        return a.reshape(a.shape[-2], a.shape[-1])

    chip = 2 * lax.axis_index("x") + lax.axis_index("y")
    where = jnp.stack([chip, lax.axis_index("c")]).astype(jnp.int32)
    xs, target = x[0], loss_target[0]
    small = {n: w[n].reshape(SMALL_SHAPES[n]) for n in SMALL}

    (abr, abi, bbr, bbi), disc_vjp = jax.vjp(_ssm_discretize, small["a_re"], small["a_im"],
                                             small["log_dt"], small["b_re"], small["b_im"])
    tab_f, tab_r = _scan_tables(abr, abi)
    bre = _block_diag_b(bbr)
    bim = _block_diag_b(bbi)
    cre = _block_diag_c(small["c_re"])
    cim = _block_diag_c(small["c_im"])
    tril = jnp.tril(jnp.ones((CHUNK, CHUNK), dtype=bool))
    ws = jnp.where(tril[None], small["w_s"], 0.0)
    ws_st = ws.reshape(SGU_G // 2, 2 * CHUNK, CHUNK).astype(BF16)
    wst_st = ws.transpose(0, 2, 1).reshape(SGU_G // 2, 2 * CHUNK, CHUNK).astype(BF16)
    bmat = jnp.repeat(small["b_s"].T, SGU_D, axis=1)
    g_mix2 = small["g_mix"].reshape(1, D_MODEL)
    g_ffn2 = small["g_ffn"].reshape(1, D_MODEL)
    g_final2 = small["g_final"].reshape(1, D_MODEL)
    g_sgu2 = small["g_sgu"].reshape(1, SGU_W)
    d_skip2 = small["d_skip"].reshape(1, SSM_W)
    b_glu2 = small["b_glu"].reshape(1, SSM_W)
    conv_b2 = small["conv_b"].reshape(1, 2 * D_FF)

    gat = {"w_in": _place_shard(shard2d(w["w_in"]), where, "place_w_in", BF16, PLACE_ROWS["w_in"])}
    gat.update(zip(BIG[1:], _place_shards([shard2d(w[n]) for n in BIG[1:]], where, "place_rest", BF16)))
    gat["conv_w"] = _place_shard(shard2d(w["conv_w"]), where, "place_conv_w", F32, 3)
    all_rows = (0, D_MODEL)
    (gat["w_in"],), = _comm("gather_in", [_job_gather(
        [gat["w_in"]], [(0, all_rows, ICI, (0.0, 0.5)), (0, all_rows, SIBLING, (0.5, 1.0))])])
    mixers = ["w_glu", "w_proj_a", "w_proj_b", "w_out"]
    rows = {n: (0, gat[n].shape[1]) for n in mixers}
    down_a, down_b = (0, D_FF // 8), (D_FF // 8, D_FF // 8)
    up_a, up_b = (0, 3 * D_MODEL // 8), (3 * D_MODEL // 8, 5 * D_MODEL // 8)
    span = (0.0, 1.0)

    names = mixers + ["conv_w", "w_down"]
    (p, h1, bur, bui), (got,) = _fwd_in(
        xs, g_mix2, gat["w_in"], bre, bim,
        [_job_gather([gat[n] for n in names],
                     [(i, rows[n], ICI, span) for i, n in enumerate(mixers)]
                     + [(4, None, ICI, span), (5, down_a, ICI, span)])])
    gat.update(zip(names, got))
    names = mixers + ["w_down", "w_up"]
    (str_, sti), (got,) = _scan_fwd(
        bur, bui, tab_f,
        [_job_gather([gat[n] for n in names],
                     [(i, rows[n], SIBLING, span) for i, n in enumerate(mixers)]
                     + [(4, down_a, SIBLING, span), (4, down_b, ICI, span), (5, up_a, ICI, span)])])
    gat.update(zip(names, got))
    w_glu_f = gat["w_glu"].reshape(SSM_W, SSM_W)
    w_out_f = gat["w_out"].reshape(D_MODEL, D_MODEL)
    conv_w_f = gat["conv_w"].transpose(1, 0, 2).reshape(3, 2 * D_FF)
    (x2, y0, z, mixed, ya, yb), ((gat["w_down"], gat["w_up"]),) = _fwd_mix(
        xs, p, str_, sti, cre, cim, d_skip2, w_glu_f, b_glu2, gat["w_proj_a"], g_sgu2, ws_st, bmat,
        gat["w_proj_b"], w_out_f,
        [_job_gather([gat["w_down"], gat["w_up"]],
                     [(0, down_b, SIBLING, span), (1, up_a, SIBLING, span),
                      (1, up_b, ICI, (0.0, 0.75)), (1, up_b, SIBLING, (0.75, 1.0))])])
    w_down_f = gat["w_down"].reshape(D_FF, D_MODEL)
    up, act, f, h2, dx3, sm_ffn = _fwd_ffn(x2, target, g_ffn2, gat["w_up"], conv_w_f, conv_b2,
                                           w_down_f, g_final2)

    def leg1_done(names, got):
        return _add_sibling([part[n] for n in names], got, where, "add_sibling_" + names[0])

    def leg2_done(names, sums, got):
        return _add_chips(sums, got, where, "add_chips_" + names[0])

    part, red = {}, {}
    part["w_down"] = _dw_rows(f, dx3, "dw_down", D_FF // 2, 4 * TK).reshape(
        N_CHIP, D_FF // N_CHIP, D_MODEL)
    (dx2, dup, sm_conv, sm_gffn), (got,) = _bwd_ffn(
        dx3, up, act, x2, g_ffn2, gat["w_up"], conv_w_f, w_down_f,
        [_job_sibling_halves([part["w_down"]])])
    sum_down = leg1_done(["w_down"], got)
    part["w_up"], (got,) = _dw_tiles(h2, dup, "dw_up", D_MODEL // 2, 2 * D_FF // N_CHIP,
                                     [_job_to_owner(sum_down)])
    red_down = leg2_done(["w_down"], sum_down, got)
    ((dsr, dsi, du_part, drest, mrg, dya, dyb, yap, dz, y1, sgu, dy0, sm_mix, dbm, dws),
     (got, (red["w_down"],))) = _bwd_mix(
        dx2, p, y0, z, mixed, ya, yb, w_out_f, gat["w_proj_a"], gat["w_proj_b"], w_glu_f, cre, cim,
        ws_st, wst_st, d_skip2, g_sgu2,
        [_job_sibling_halves([part["w_up"]]), _job_swap_halves(red_down)])
    sum_up = leg1_done(["w_up"], got)
    (lam_r, lam_i, dar8, dai8), (got,) = _scan_bwd(dsr, dsi, str_, sti, tab_r, [_job_to_owner(sum_up)])
    red_up = leg2_done(["w_up"], sum_up, got)
    mix4 = ["w_out", "w_proj_a", "w_proj_b", "w_glu"]
    part["w_out"] = _dw_cols(mrg, dx2, "dw_out", D_MODEL // 2, False)[0].reshape(
        N_CHIP, D_MODEL // N_CHIP, D_MODEL)
    part["w_proj_a"] = _dw_cols(yap, dya, "dw_proj_a", D_MODEL // N_CHIP, True)[0]
    part["w_proj_b"] = _dw_cols(sgu, dyb, "dw_proj_b", D_MODEL // N_CHIP, True)[0]
    part["w_glu"] = _dw_cols(y1, dz, "dw_glu", SSM_W, False)[0].reshape(
        N_CHIP, SSM_W // N_CHIP, SSM_W)
    (grad_x, dp, sm_gmix), (got, (red["w_up"],)) = _bwd_in(
        lam_r, lam_i, du_part, drest, xs, dx2, g_mix2, gat["w_in"], bre, bim,
        [_job_sibling_halves([part[n] for n in mix4]), _job_swap_halves(red_up)])
    sums_m = leg1_done(mix4, got)
    part["w_in"], (got,) = _dw_cols(h1, dp, "dw_in", IN_COLS // N_CHIP, True, [_job_to_owner(sums_m)])
    red_m = leg2_done(mix4, sums_m, got)
    (dbd_r, dbd_i), (got, done_m) = _dw_pair(
        p, SSM_W, lam_r, lam_i, "db_bar",
        [_job_sibling_halves([part["w_in"]]), _job_swap_halves(red_m)])
    red.update(zip(mix4, done_m))
    sum_in = leg1_done(["w_in"], got)
    (dcd_r, dcd_i), (got,) = _dw_pair(dy0, SSM_W, str_, sti, "dc", [_job_to_owner(sum_in)])
    red_in = leg2_done(["w_in"], sum_in, got)
    (red["w_in"],), = _comm("swap_w_in", [_job_swap_halves(red_in)])

    def pick_c(slabs):
        two = LANE // SSM_P
        return jnp.einsum("jshsp->jshp", slabs.reshape(SSM_G // two, two, SSM_H, two, SSM_P)
                          ).reshape(SSM_G, SSM_H, SSM_P)

    def pick_b(slabs):
        return pick_c(slabs).transpose(0, 2, 1)

    dabr = jnp.sum(dar8, axis=0).reshape(SSM_G, SSM_P)
    dabi = jnp.sum(dai8, axis=0).reshape(SSM_G, SSM_P)
    d_a_re, d_a_im, d_log_dt, d_b_re, d_b_im = disc_vjp((dabr, dabi, pick_b(dbd_r), pick_b(dbd_i)))
    gsmall = {
        "g_mix": sm_gmix[0], "a_re": d_a_re, "a_im": d_a_im, "log_dt": d_log_dt,
        "b_re": d_b_re, "b_im": d_b_im, "c_re": pick_c(dcd_r), "c_im": -pick_c(dcd_i),
        "d_skip": sm_mix[0], "b_glu": sm_mix[1], "g_sgu": sm_mix[2],
        "w_s": jnp.where(tril[None], dws, 0.0),
        "b_s": dbm.reshape(CHUNK, SGU_G, SGU_D).sum(-1).T,
        "g_ffn": sm_gffn[0], "conv_b": sm_conv[3], "g_final": sm_ffn[0],
        "conv_w": sm_conv[0:3], "loss": sm_ffn[1, 0:1],
    }

    total_pack = _small_allreduce(_pack(gsmall))
    total = _unpack(total_pack)
    grads = dict(red)
    cs = 2 * D_FF // N_CHIP
    grads["conv_w"] = lax.dynamic_slice(total["conv_w"], (0, chip * cs), (3, cs))
    delta, new_m, new_v = {}, {}, {}
    for n in BIG + ("conv_w",):
        grads[n], delta[n], new_m[n], new_v[n] = _adamw(
            shard2d(w[n]), grads[n], shard2d(m[n]), shard2d(v[n]), "adamw_" + n, PLACE_ROWS.get(n, 3))
    for n in SMALL:
        grads[n] = total[n].reshape(w[n].shape)
    ud, um, uv = _adamw_many(*[[d[n] for n in SMALL] for d in (w, grads, m, v)], "adamw_small")
    for i, n in enumerate(SMALL):
        delta[n], new_m[n], new_v[n] = ud[i], um[i], uv[i]

    def like(d):
        return [d[n].reshape(w[n].shape) for n in WEIGHTS]

    return (total["loss"].reshape(()), grad_x.reshape(x.shape), *like(grads), *like(delta),
            *like(new_m), *like(new_v))
```

```python
import math

import jax
import jax.numpy as jnp
from jax import lax
from jax.experimental import pallas as pl
from jax.experimental.pallas import tpu as pltpu

F32 = jnp.float32
BF16 = jnp.bfloat16
MESH = pl.DeviceIdType.MESH

D_MODEL = 1024
SSM_W = 512
SSM_G = 32
SSM_H = 16
SSM_P = 64
N_STATE = SSM_G * SSM_P
DIAG_N = 128 * SSM_P // SSM_H
SGU_W = 512
SGU_G = 8
SGU_D = 64
CHUNK = 128
D_FF = 2816
IN_COLS = 3584
EPS = 1e-6
N_CHIP = 4

ADAM_LR = 0.001
ADAM_B1 = 0.9
ADAM_B2 = 0.999
ADAM_EPS = 1e-08
ADAM_WD = 0.01
ADAM_STEP = 10

SUBLANE = 8
LANE = 128
VMEM_LIMIT = 56 * 1024 * 1024
TB = 256
TK = 512
SCAN_LANES = 256
SCAN_UNROLL = 4
SCAN_ROWS = 512
HALO = SUBLANE

BIG = ("w_in", "w_up", "w_down", "w_out", "w_proj_a", "w_proj_b", "w_glu")
SMALL = ("g_mix", "a_re", "a_im", "log_dt", "b_re", "b_im", "c_re", "c_im", "d_skip", "b_glu",
         "g_sgu", "w_s", "b_s", "g_ffn", "conv_b", "g_final")
WEIGHTS = ("g_mix", "w_in", "a_re", "a_im", "log_dt", "b_re", "b_im", "c_re", "c_im", "d_skip",
           "w_glu", "b_glu", "w_proj_a", "g_sgu", "w_s", "b_s", "w_proj_b", "w_out", "g_ffn",
           "w_up", "conv_w", "conv_b", "w_down", "g_final")

ANY = pl.BlockSpec(memory_space=pl.ANY)


def _params(n_grid):
    return pltpu.CompilerParams(dimension_semantics=("arbitrary",) * n_grid if n_grid else None,
                                vmem_limit_bytes=VMEM_LIMIT)


def _whole():
    return pl.BlockSpec(memory_space=pltpu.VMEM)


def _rows(tb, ncol):
    return pl.BlockSpec((tb, ncol), lambda i: (i, 0))


def _acc(nrow, ncol):
    return pl.BlockSpec((nrow, ncol), lambda i: (0, 0))


def _dot(a, b):
    return jnp.dot(a.astype(BF16), b.astype(BF16), preferred_element_type=F32)


def _dot_nt(a, b):
    return lax.dot_general(a.astype(BF16), b.astype(BF16), (((1,), (1,)), ((), ())),
                           preferred_element_type=F32)


def _sigmoid(v):
    return 0.5 * jnp.tanh(0.5 * v) + 0.5


_GELU_C = math.sqrt(2.0 / math.pi)


def _gelu(v):
    return 0.5 * v * (1.0 + jnp.tanh(_GELU_C * (v + 0.044715 * v * v * v)))


def _gelu_and_grad(v):
    v2 = v * v
    t = jnp.tanh(_GELU_C * v * (1.0 + 0.044715 * v2))
    half = 0.5 * (1.0 + t)
    return v * half, half + 0.5 * v * (1.0 - t * t) * _GELU_C * (1.0 + 3.0 * 0.044715 * v2)


def _rms_stats(v):
    r = lax.rsqrt(jnp.mean(v * v, axis=-1, keepdims=True) + EPS)
    return r, v * r


def _rms_bwd(dxh, xh, r):
    return r * (dxh - xh * jnp.mean(dxh * xh, axis=-1, keepdims=True))


def _place():
    x, y, c = lax.axis_index("x"), lax.axis_index("y"), lax.axis_index("c")
    chips = [(1 - x, y), (x, 1 - y), (1 - x, 1 - y)]
    return x, y, c, chips


def _chip_index(chip):
    return 2 * chip[0] + chip[1]


def _remote(src, dst, send_sem, recv_sem, device):
    return pltpu.make_async_remote_copy(src_ref=src, dst_ref=dst, send_sem=send_sem,
                                        recv_sem=recv_sem, device_id=device, device_id_type=MESH)


def _half(ref_rows, c):
    hr = ref_rows // 2
    return pl.ds(pl.multiple_of(c * hr, SUBLANE), hr)


class _Job:
    def __init__(self, hooks, n_sem, ins=(), inouts=(), outs=()):
        self.hooks, self.n_sem = list(hooks), n_sem
        self.ins, self.inouts, self.outs = list(ins), list(inouts), list(outs)


def _whole_span(start, finish):
    return [(0.0, "start", start), (1.0, "finish", finish)]


ICI, SIBLING = "ici", "sibling"


def _job_gather(bufs, legs):
    def copies(io, leg, first):
        b, window, kind, _ = legs[leg]
        x, y, c, chips = _place()
        k_me = 2 * x + y
        out = []
        for j, ch in enumerate(chips):
            k = _chip_index(ch)
            if window is None:
                src, land, dev = io[b].at[k_me], io[b].at[k], (*ch, c)
            else:
                r0, rows = window
                mine = pl.ds(pl.multiple_of(r0 + c * (rows // 2), SUBLANE), rows // 2)
                theirs = pl.ds(pl.multiple_of(r0 + (1 - c) * (rows // 2), SUBLANE), rows // 2)
                if kind == ICI:
                    src, land, dev = io[b].at[k_me, mine, :], io[b].at[k, mine, :], (*ch, c)
                else:
                    src, land, dev = io[b].at[k, mine, :], io[b].at[k, theirs, :], (x, y, 1 - c)
            out.append((src, land, first + j, dev))
        return out

    def starter(leg):
        def start(ins, io, outs, ssem, rsem):
            for src, _, i, dev in copies(io, leg, 3 * leg):
                _remote(src, src, ssem(i), rsem(i), dev).start()
        return start

    def finisher(leg):
        def finish(ins, io, outs, ssem, rsem):
            cps = copies(io, leg, 3 * leg)
            for _, land, i, dev in cps:
                _remote(land, land, ssem(i), rsem(i), dev).wait_recv()
            for src, _, i, dev in cps:
                _remote(src, src, ssem(i), rsem(i), dev).wait_send()
        return finish

    hooks = []
    for leg, (_, _, _, (begin, end)) in enumerate(legs):
        hooks += [(begin, "start", starter(leg)), (end, "finish", finisher(leg))]
    return _Job(hooks, 3 * len(legs), inouts=bufs)


def _job_sibling_halves(grads):
    n = len(grads)

    def build(ins, outs, ssem, rsem):
        x, y, c, _ = _place()
        return [_remote(ins[t].at[:, _half(grads[t].shape[1], 1 - c), :], outs[t], ssem(t), rsem(t),
                        (x, y, 1 - c)) for t in range(n)]

    def start(ins, io, outs, ssem, rsem):
        for cp in build(ins, outs, ssem, rsem):
            cp.start()

    def finish(ins, io, outs, ssem, rsem):
        for cp in build(ins, outs, ssem, rsem):
            cp.wait()

    return _Job(_whole_span(start, finish), n, ins=grads,
                outs=[jax.ShapeDtypeStruct((N_CHIP, g.shape[1] // 2, g.shape[2]), F32) for g in grads])


def _job_to_owner(sums):
    n = len(sums)

    def build(ins, outs, ssem, rsem):
        x, y, c, chips = _place()
        return [_remote(ins[t].at[_chip_index(ch)], outs[t].at[j], ssem(3 * t + j), rsem(3 * t + j),
                        (*ch, c)) for t in range(n) for j, ch in enumerate(chips)]

    def start(ins, io, outs, ssem, rsem):
        for cp in build(ins, outs, ssem, rsem):
            cp.start()

    def finish(ins, io, outs, ssem, rsem):
        for cp in build(ins, outs, ssem, rsem):
            cp.wait()

    return _Job(_whole_span(start, finish), 3 * n, ins=sums,
                outs=[jax.ShapeDtypeStruct((3,) + s.shape[1:], s.dtype) for s in sums])


def _job_swap_halves(bufs):
    n = len(bufs)

    def start(ins, io, outs, ssem, rsem):
        x, y, c, _ = _place()
        for t in range(n):
            mine = io[t].at[_half(bufs[t].shape[0], c), :]
            _remote(mine, mine, ssem(t), rsem(t), (x, y, 1 - c)).start()

    def finish(ins, io, outs, ssem, rsem):
        x, y, c, _ = _place()
        for t in range(n):
            theirs = io[t].at[_half(bufs[t].shape[0], 1 - c), :]
            _remote(theirs, theirs, ssem(t), rsem(t), (x, y, 1 - c)).wait_recv()
        for t in range(n):
            mine = io[t].at[_half(bufs[t].shape[0], c), :]
            _remote(mine, mine, ssem(t), rsem(t), (x, y, 1 - c)).wait_send()

    return _Job(_whole_span(start, finish), n, inouts=bufs)


def _call(body, name, grid, in_specs, out_specs, out_shape, args, jobs=(), scratch=()):
    n_in, n_out, n_scr = len(args), len(out_shape), len(scratch)
    job_in = [a for jb in jobs for a in jb.ins + jb.inouts]
    job_out = [s for jb in jobs
               for s in [jax.ShapeDtypeStruct(a.shape, a.dtype) for a in jb.inouts] + jb.outs]
    aliases, pos_in, pos_out = {}, n_in, n_out
    for jb in jobs:
        pos_in += len(jb.ins)
        for _ in jb.inouts:
            aliases[pos_in] = pos_out
            pos_in += 1
            pos_out += 1
        pos_out += len(jb.outs)
    n_sem = sum(jb.n_sem for jb in jobs)

    def wrapped(*refs):
        c_in = refs[:n_in]
        j_in = refs[n_in:n_in + len(job_in)]
        c_out = refs[n_in + len(job_in):n_in + len(job_in) + n_out]
        j_out = refs[n_in + len(job_in) + n_out:n_in + len(job_in) + n_out + len(job_out)]
        rest = refs[n_in + len(job_in) + n_out + len(job_out):]
        c_scr = rest[:n_scr]
        views, pi, po, ps = [], 0, 0, 0
        for jb in jobs:
            ins = j_in[pi:pi + len(jb.ins)]
            pi += len(jb.ins) + len(jb.inouts)
            io = j_out[po:po + len(jb.inouts)]
            new = j_out[po + len(jb.inouts):po + len(jb.inouts) + len(jb.outs)]
            po += len(jb.inouts) + len(jb.outs)
            send = (lambda i, o=ps: rest[n_scr].at[o + i])
            recv = (lambda i, o=ps: rest[n_scr + 1].at[o + i])
            ps += jb.n_sem
            views.append((ins, io, new, send, recv))

        def run(frac):
            for kind in ("finish", "start"):
                for jb, vw in zip(jobs, views):
                    for at, what, fn in jb.hooks:
                        if at == frac and what == kind:
                            fn(*vw)

        fracs = sorted({at for jb in jobs for at, _, _ in jb.hooks})
        if not grid:
            for frac in fracs:
                run(frac)
            return
        if jobs:
            assert len(grid) == 1 or set(fracs) <= {0.0, 1.0}
            first = pl.program_id(0) == 0
            last = pl.program_id(0) == grid[0] - 1
            for d in range(1, len(grid)):
                first = jnp.logical_and(first, pl.program_id(d) == 0)
                last = jnp.logical_and(last, pl.program_id(d) == grid[d] - 1)
            for frac in fracs:
                if frac < 1.0:
                    at_step = first if frac == 0.0 else pl.program_id(0) == int(frac * grid[0])
                    pl.when(at_step)(lambda frac=frac: run(frac))
        body(*c_in, *c_out, *c_scr)
        if jobs and 1.0 in fracs:
            pl.when(last)(lambda: run(1.0))

    sems = [pltpu.SemaphoreType.DMA((n_sem,)), pltpu.SemaphoreType.DMA((n_sem,))] if jobs else []
    kwargs = dict(grid=grid) if grid else {}
    res = pl.pallas_call(
        wrapped, name=name, in_specs=list(in_specs) + [ANY] * len(job_in),
        out_specs=list(out_specs) + [ANY] * len(job_out),
        out_shape=list(out_shape) + job_out, scratch_shapes=list(scratch) + sems,
        input_output_aliases=aliases, compiler_params=_params(len(grid)), **kwargs,
    )(*args, *job_in)
    outs, pos, per_job = list(res[:n_out]), n_out, []
    for jb in jobs:
        k = len(jb.inouts) + len(jb.outs)
        per_job.append(list(res[pos:pos + k]))
        pos += k
    return outs, per_job


def _comm(name, jobs):
    return _call(None, name, (), [], [], [], [], jobs)[1]


def _fwd_in(x, g_mix, w_in, bre, bim, jobs=()):
    t_len = x.shape[0]
    cs = IN_COLS // N_CHIP

    def body(x_ref, g_ref, w_ref, bre_ref, bim_ref, p_ref, h_ref, bur_ref, bui_ref):
        xv = x_ref[...]
        r, xh = _rms_stats(xv)
        h = (xh * g_ref[...]).astype(BF16)
        h_ref[...] = h
        for k in range(N_CHIP):
            p_ref[:, k * cs:(k + 1) * cs] = jnp.dot(h, w_ref[k],
                                                    preferred_element_type=F32).astype(BF16)
        u = p_ref[:, 0:SSM_W]
        for i in range(SSM_W // LANE):
            rows, cols = slice(i * LANE, (i + 1) * LANE), slice(i * DIAG_N, (i + 1) * DIAG_N)
            bur_ref[:, cols] = jnp.dot(u[:, rows], bre_ref[rows, cols],
                                       preferred_element_type=F32).astype(BF16)
            bui_ref[:, cols] = jnp.dot(u[:, rows], bim_ref[rows, cols],
                                       preferred_element_type=F32).astype(BF16)

    return _call(
        body, "fwd_in", (t_len // TB,),
        [_rows(TB, D_MODEL), _whole(), _whole(), _whole(), _whole()],
        [_rows(TB, IN_COLS), _rows(TB, D_MODEL), _rows(TB, N_STATE), _rows(TB, N_STATE)],
        [jax.ShapeDtypeStruct((t_len, IN_COLS), BF16), jax.ShapeDtypeStruct((t_len, D_MODEL), BF16),
         jax.ShapeDtypeStruct((t_len, N_STATE), BF16), jax.ShapeDtypeStruct((t_len, N_STATE), BF16)],
        [x, g_mix, w_in, bre, bim], jobs)


def _scan_local(xr, xi, tab, shifts):
    for q, s in enumerate(shifts):
        ar, ai = tab[2 * q], tab[2 * q + 1]
        rr = pltpu.roll(xr, s, 0)
        ri = pltpu.roll(xi, s, 0)
        xr, xi = xr + ar * rr - ai * ri, xi + ar * ri + ai * rr
    return xr, xi


def _scan_carry(xr, xi, tab, cr, ci):
    pr, pi = tab[6], tab[7]
    return xr + pr * cr - pi * ci, xi + pr * ci + pi * cr


BF16_TILE = 2 * SUBLANE


def _load_blocks(r_ref, i_ref, base):
    out = []
    for q in range(SCAN_UNROLL // 2):
        rows = pl.ds(pl.multiple_of(base + q * BF16_TILE, BF16_TILE), BF16_TILE)
        vr, vi = r_ref[rows, :].astype(F32), i_ref[rows, :].astype(F32)
        out += [(vr[:SUBLANE], vi[:SUBLANE]), (vr[SUBLANE:], vi[SUBLANE:])]
    return out


def _store_blocks(r_ref, i_ref, base, blocks):
    for q in range(SCAN_UNROLL // 2):
        rows = pl.ds(pl.multiple_of(base + q * BF16_TILE, BF16_TILE), BF16_TILE)
        r_ref[rows, :] = jnp.concatenate([blocks[2 * q][0], blocks[2 * q + 1][0]], 0).astype(r_ref.dtype)
        i_ref[rows, :] = jnp.concatenate([blocks[2 * q][1], blocks[2 * q + 1][1]], 0).astype(i_ref.dtype)


def _scan_fwd(bur, bui, tab, jobs=()):
    t_len = bur.shape[0]
    tt = min(SCAN_ROWS, t_len)
    lb = SCAN_LANES

    def body(br_ref, bi_ref, tab_ref, sr_ref, si_ref, car_ref):
        tab_v = [tab_ref[q] for q in range(8)]

        @pl.when(pl.program_id(1) == 0)
        def _():
            car_ref[...] = jnp.zeros_like(car_ref)

        def step(k, carry):
            cr, ci = carry
            base = pl.multiple_of(k * SCAN_UNROLL * SUBLANE, SCAN_UNROLL * SUBLANE)
            local = [_scan_local(xr, xi, tab_v, (1, 2, 4))
                     for xr, xi in _load_blocks(br_ref, bi_ref, base)]
            done = []
            for xr, xi in local:
                xr, xi = _scan_carry(xr, xi, tab_v, cr, ci)
                done.append((xr, xi))
                cr, ci = xr[SUBLANE - 1:SUBLANE, :], xi[SUBLANE - 1:SUBLANE, :]
            _store_blocks(sr_ref, si_ref, base, done)
            return cr, ci

        cr, ci = lax.fori_loop(0, tt // (SCAN_UNROLL * SUBLANE), step,
                               (car_ref[0:1, :], car_ref[1:2, :]))
        car_ref[0:1, :] = cr
        car_ref[1:2, :] = ci

    col = pl.BlockSpec((tt, lb), lambda j, t: (t, j))
    return _call(
        body, "scan_fwd", (N_STATE // lb, t_len // tt),
        [col, col, pl.BlockSpec((8, SUBLANE, lb), lambda j, t: (0, 0, j))], [col, col],
        [jax.ShapeDtypeStruct((t_len, N_STATE), BF16)] * 2, [bur, bui, tab], jobs,
        scratch=[pltpu.VMEM((SUBLANE, lb), F32)])


def _sgu_mix(v, ws_ref, lane_lo):
    rows = []
    for c0 in range(0, v.shape[0], CHUNK):
        slabs = []
        for j in range(SGU_W // LANE):
            prod = jnp.dot(ws_ref[j], v[c0:c0 + CHUNK, j * LANE:(j + 1) * LANE].astype(BF16),
                           preferred_element_type=F32)
            slabs.append(jnp.where(lane_lo, prod[:CHUNK], prod[CHUNK:]))
        rows.append(jnp.concatenate(slabs, axis=1))
    return jnp.concatenate(rows, axis=0) if len(rows) > 1 else rows[0]


def _fwd_mix(x, p, str_, sti, cre, cim, d_skip, w_glu, b_glu, w_pa, g_sgu, ws_st, bmat, w_pb, w_out,
             jobs=()):
    t_len = x.shape[0]

    def body(x_ref, p_ref, sr_ref, si_ref, cre_ref, cim_ref, dsk_ref, wg_ref, bg_ref, wpa_ref,
             gs_ref, ws_ref, bm_ref, wpb_ref, wo_ref,
             x2_ref, y0_ref, z_ref, mx_ref, ya_ref, yb_ref):
        u = p_ref[:, 0:SSM_W].astype(F32)
        y0 = jnp.concatenate(
            [_dot(sr_ref[:, i * DIAG_N:(i + 1) * DIAG_N],
                  cre_ref[i * DIAG_N:(i + 1) * DIAG_N, i * LANE:(i + 1) * LANE])
             - _dot(si_ref[:, i * DIAG_N:(i + 1) * DIAG_N],
                    cim_ref[i * DIAG_N:(i + 1) * DIAG_N, i * LANE:(i + 1) * LANE])
             for i in range(SSM_W // LANE)], axis=1) + dsk_ref[...] * u
        y0_ref[...] = y0.astype(BF16)
        y1 = _gelu(y0)
        z = _dot(y1, wg_ref[...]) + bg_ref[...]
        z_ref[...] = z.astype(BF16)
        ya_pre = (y1 * _sigmoid(z)).astype(BF16)
        ya = jnp.concatenate([jnp.dot(ya_pre, wpa_ref[k], preferred_element_type=F32)
                              for k in range(N_CHIP)], axis=1)
        ya_ref[...] = ya.astype(BF16)

        uvg = _gelu(p_ref[:, SSM_W:SSM_W + 2 * SGU_W].astype(F32))
        u2 = uvg[:, :SGU_W]
        _, vh = _rms_stats(uvg[:, SGU_W:])
        v3 = vh * gs_ref[...]
        lane_lo = lax.broadcasted_iota(jnp.int32, (CHUNK, LANE), 1) < SGU_D
        bias = jnp.concatenate([bm_ref[...]] * (TB // CHUNK), axis=0)
        mixed = _sgu_mix(v3, ws_ref, lane_lo) + bias
        mx_ref[...] = mixed.astype(BF16)
        sgu = (u2 * mixed).astype(BF16)
        yb = jnp.concatenate([jnp.dot(sgu, wpb_ref[k], preferred_element_type=F32)
                              for k in range(N_CHIP)], axis=1)
        yb_ref[...] = yb.astype(BF16)

        lg0 = SSM_W + 2 * SGU_W
        ga = _sigmoid(p_ref[:, lg0:lg0 + D_MODEL].astype(F32))
        gb = _sigmoid(p_ref[:, lg0 + D_MODEL:lg0 + 2 * D_MODEL].astype(F32))
        mrg = ga * ya + gb * yb
        x2_ref[...] = x_ref[...] + _dot(mrg, wo_ref[...])

    return _call(
        body, "fwd_mix", (t_len // TB,),
        [_rows(TB, D_MODEL), _rows(TB, IN_COLS), _rows(TB, N_STATE), _rows(TB, N_STATE)]
        + [_whole()] * 11,
        [_rows(TB, D_MODEL), _rows(TB, SSM_W), _rows(TB, SSM_W), _rows(TB, SGU_W),
         _rows(TB, D_MODEL), _rows(TB, D_MODEL)],
        [jax.ShapeDtypeStruct((t_len, D_MODEL), F32), jax.ShapeDtypeStruct((t_len, SSM_W), BF16),
         jax.ShapeDtypeStruct((t_len, SSM_W), BF16), jax.ShapeDtypeStruct((t_len, SGU_W), BF16),
         jax.ShapeDtypeStruct((t_len, D_MODEL), BF16), jax.ShapeDtypeStruct((t_len, D_MODEL), BF16)],
        [x, p, str_, sti, cre, cim, d_skip, w_glu, b_glu, w_pa, g_sgu, ws_st, bmat, w_pb, w_out], jobs)


def _conv_taps(v, cw_ref, c0, width):
    w0 = cw_ref[0:1, c0:c0 + width]
    w1 = cw_ref[1:2, c0:c0 + width]
    w2 = cw_ref[2:3, c0:c0 + width]
    return w0 * pltpu.roll(v, 2, 0) + w1 * pltpu.roll(v, 1, 0) + w2 * v


def _fwd_ffn(x2, target, g_ffn, w_up, conv_w, conv_b, w_down, g_final):
    t_len = x2.shape[0]
    half = D_FF // 2
    blocks_per_halo = TB // HALO

    def body(x2_ref, xp_ref, tg_ref, gf_ref, wu_ref, cw_ref, cb_ref, wd_ref, gl_ref,
             up_ref, act_ref, f_ref, h2_ref, dx3_ref, sm_ref):
        i = pl.program_id(0)
        xe = jnp.concatenate([xp_ref[...] * jnp.where(i == 0, 0.0, 1.0), x2_ref[...]], axis=0)
        _, xh = _rms_stats(xe)
        h2 = (xh * gf_ref[...]).astype(BF16)
        h2_ref[...] = h2[HALO:]
        acc = jnp.zeros((TB, D_MODEL), F32)
        ups = [jnp.dot(h2, wu_ref[k], preferred_element_type=F32) for k in range(N_CHIP)]
        for hc in range(2):
            ca = hc * half
            cb = D_FF + hc * half
            ua, ub = ups[hc], ups[2 + hc]
            up_ref[:, ca:ca + half] = ua[HALO:].astype(BF16)
            up_ref[:, cb:cb + half] = ub[HALO:].astype(BF16)
            ac = _conv_taps(ua, cw_ref, ca, half)[HALO:] + cb_ref[:, ca:ca + half]
            bc = _conv_taps(ub, cw_ref, cb, half)[HALO:] + cb_ref[:, cb:cb + half]
            act_ref[:, ca:ca + half] = ac.astype(BF16)
            act_ref[:, cb:cb + half] = bc.astype(BF16)
            f = (ac * _sigmoid(ac) * bc).astype(BF16)
            f_ref[:, ca:ca + half] = f
            acc = acc + jnp.dot(f, wd_ref[ca:ca + half, :], preferred_element_type=F32)
        x3 = x2_ref[...] + acc
        r3, xh3 = _rms_stats(x3)
        err = xh3 * gl_ref[...] - tg_ref[...]
        dout = err * (1.0 / D_MODEL)
        dx3_ref[...] = _rms_bwd(dout * gl_ref[...], xh3, r3)
        dgl = jnp.sum(dout * xh3, axis=0, keepdims=True)
        loss = 0.5 * jnp.sum(jnp.mean(err * err, axis=-1, keepdims=True), axis=0, keepdims=True)
        upd = jnp.concatenate([dgl, jnp.broadcast_to(loss, (1, D_MODEL)),
                               jnp.zeros((SUBLANE - 2, D_MODEL), F32)], axis=0)

        @pl.when(i == 0)
        def _():
            sm_ref[...] = upd

        @pl.when(i > 0)
        def _():
            sm_ref[...] += upd

    prev = pl.BlockSpec((HALO, D_MODEL), lambda i: (jnp.maximum(i * blocks_per_halo - 1, 0), 0))
    return _call(
        body, "fwd_ffn", (t_len // TB,),
        [_rows(TB, D_MODEL), prev, _rows(TB, D_MODEL)] + [_whole()] * 6,
        [_rows(TB, 2 * D_FF), _rows(TB, 2 * D_FF), _rows(TB, D_FF), _rows(TB, D_MODEL),
         _rows(TB, D_MODEL), _acc(SUBLANE, D_MODEL)],
        [jax.ShapeDtypeStruct((t_len, 2 * D_FF), BF16), jax.ShapeDtypeStruct((t_len, 2 * D_FF), BF16),
         jax.ShapeDtypeStruct((t_len, D_FF), BF16), jax.ShapeDtypeStruct((t_len, D_MODEL), BF16),
         jax.ShapeDtypeStruct((t_len, D_MODEL), F32), jax.ShapeDtypeStruct((SUBLANE, D_MODEL), F32)],
        [x2, x2, target, g_ffn, w_up, conv_w, conv_b, w_down, g_final])[0]


def _bwd_ffn(dx3, up, act, x2, g_ffn, w_up, conv_w, w_down, jobs=()):
    t_len = x2.shape[0]
    half = D_FF // 2
    nblk = t_len // TB
    halo_b = 2 * HALO
    n_e = TB + HALO

    def body(dx_ref, dxn_ref, up_ref, act_ref, actn_ref, x2_ref, gf_ref, wu_ref, cw_ref,
             wd_ref, dx2_ref, dup_ref, smw_ref, smg_ref):
        i = pl.program_id(0)
        keep_last = jnp.where(i == nblk - 1, 0.0, 1.0)
        dxe = jnp.concatenate([dx_ref[...], dxn_ref[...] * keep_last], axis=0).astype(BF16)
        dh2 = jnp.zeros((TB, D_MODEL), F32)
        zpad = jnp.zeros((1, half), F32)
        dfs = [lax.dot_general(dxe, wd_ref[hc * half:(hc + 1) * half, :], (((1,), (1,)), ((), ())),
                               preferred_element_type=F32) for hc in range(2)]
        for hc in range(2):
            ca = hc * half
            cb = D_FF + hc * half
            ac = jnp.concatenate([act_ref[:, ca:ca + half].astype(F32),
                                  actn_ref[:, ca:ca + half].astype(F32)[:HALO]], axis=0)
            bc = jnp.concatenate([act_ref[:, cb:cb + half].astype(F32),
                                  actn_ref[:, cb:cb + half].astype(F32)[:HALO]], axis=0)
            wa = [cw_ref[k:k + 1, ca:ca + half] for k in range(3)]
            wb = [cw_ref[k:k + 1, cb:cb + half] for k in range(3)]
            df = dfs[hc]
            sg = _sigmoid(ac)
            da = df * bc * sg * (1.0 + ac * (1.0 - sg))
            db = df * ac * sg
            da1, da2 = pltpu.roll(da, n_e - 1, 0), pltpu.roll(da, n_e - 2, 0)
            db1, db2 = pltpu.roll(db, n_e - 1, 0), pltpu.roll(db, n_e - 2, 0)
            dua = (wa[2] * da + wa[1] * da1 + wa[0] * da2)[:TB]
            dub = (wb[2] * db + wb[1] * db1 + wb[0] * db2)[:TB]
            dup_ref[:, ca:ca + half] = dua.astype(BF16)
            dup_ref[:, cb:cb + half] = dub.astype(BF16)
            dh2 = dh2 + _dot_nt(dua, wu_ref[hc]) + _dot_nt(dub, wu_ref[2 + hc])
            rows = []
            for u_, d0, d1, d2 in ((up_ref[:, ca:ca + half].astype(F32), da, da1, da2),
                                   (up_ref[:, cb:cb + half].astype(F32), db, db1, db2)):
                rows.append([jnp.sum(u_ * d2[:TB], axis=0, keepdims=True),
                             jnp.sum(u_ * d1[:TB], axis=0, keepdims=True),
                             jnp.sum(u_ * d0[:TB], axis=0, keepdims=True),
                             jnp.sum(d0[:TB], axis=0, keepdims=True)])
            for c0, rws in ((ca, rows[0]), (cb, rows[1])):
                upd = jnp.concatenate(rws + [zpad] * (SUBLANE - 4), axis=0)

                @pl.when(i == 0)
                def _(upd=upd, c0=c0):
                    smw_ref[:, c0:c0 + half] = upd

                @pl.when(i > 0)
                def _(upd=upd, c0=c0):
                    smw_ref[:, c0:c0 + half] += upd

        r2, xh2 = _rms_stats(x2_ref[...])
        dx2_ref[...] = dx_ref[...] + _rms_bwd(dh2 * gf_ref[...], xh2, r2)
        updg = jnp.concatenate([jnp.sum(dh2 * xh2, axis=0, keepdims=True),
                                jnp.zeros((SUBLANE - 1, D_MODEL), F32)], axis=0)

        @pl.when(i == 0)
        def _():
            smg_ref[...] = updg

        @pl.when(i > 0)
        def _():
            smg_ref[...] += updg

    nxt_d = pl.BlockSpec((HALO, D_MODEL),
                         lambda i: (jnp.minimum((i + 1) * (TB // HALO), t_len // HALO - 1), 0))
    nxt_a = pl.BlockSpec((halo_b, 2 * D_FF),
                         lambda i: (jnp.minimum((i + 1) * (TB // halo_b), t_len // halo_b - 1), 0))
    return _call(
        body, "bwd_ffn", (nblk,),
        [_rows(TB, D_MODEL), nxt_d, _rows(TB, 2 * D_FF), _rows(TB, 2 * D_FF), nxt_a,
         _rows(TB, D_MODEL)] + [_whole()] * 4,
        [_rows(TB, D_MODEL), _rows(TB, 2 * D_FF), _acc(SUBLANE, 2 * D_FF), _acc(SUBLANE, D_MODEL)],
        [jax.ShapeDtypeStruct((t_len, D_MODEL), F32), jax.ShapeDtypeStruct((t_len, 2 * D_FF), BF16),
         jax.ShapeDtypeStruct((SUBLANE, 2 * D_FF), F32), jax.ShapeDtypeStruct((SUBLANE, D_MODEL), F32)],
        [dx3, dx3, up, act, act, x2, g_ffn, w_up, conv_w, w_down], jobs)


def _bwd_mix(dx2, p, y0, z, mixed, ya, yb, w_out, w_pa, w_pb, w_glu, cre, cim, ws_st, wst_st,
             d_skip, g_sgu, jobs=()):
    t_len = dx2.shape[0]
    pc = D_MODEL // N_CHIP
    n_slab = SGU_W // LANE

    def body(dx_ref, p_ref, y0_ref, z_ref, mx_ref, ya_ref, yb_ref, wo_ref, wpa_ref, wpb_ref,
             wg_ref, cre_ref, cim_ref, ws_ref, wst_ref, dsk_ref, gs_ref,
             dsr_ref, dsi_ref, du_ref, drest_ref, mrg_ref, dya_ref, dyb_ref, yap_ref, dz_ref,
             y1_ref, sgu_ref, dy0_ref, sm_ref, dbm_ref, dws_ref):
        i = pl.program_id(0)
        first = i == 0
        lg0 = SSM_W + 2 * SGU_W
        ga = _sigmoid(p_ref[:, lg0:lg0 + D_MODEL].astype(F32))
        gb = _sigmoid(p_ref[:, lg0 + D_MODEL:lg0 + 2 * D_MODEL].astype(F32))
        yav = ya_ref[...].astype(F32)
        ybv = yb_ref[...].astype(F32)
        mrg_ref[...] = (ga * yav + gb * ybv).astype(BF16)
        y0v = y0_ref[...].astype(F32)
        y1, y1_grad = _gelu_and_grad(y0v)
        sz = _sigmoid(z_ref[...].astype(F32))
        y1_ref[...] = y1.astype(BF16)
        yap_ref[...] = (y1 * sz).astype(BF16)

        dmrg = _dot_nt(dx_ref[...], wo_ref[...])
        drest_ref[:, 2 * SGU_W:2 * SGU_W + D_MODEL] = (dmrg * yav * ga * (1.0 - ga)).astype(BF16)
        drest_ref[:, 2 * SGU_W + D_MODEL:] = (dmrg * ybv * gb * (1.0 - gb)).astype(BF16)
        dya = (dmrg * ga).astype(BF16)
        dyb = (dmrg * gb).astype(BF16)
        dya_ref[...] = dya
        dyb_ref[...] = dyb

        dyap = jnp.zeros((TB, SSM_W), F32)
        for k in range(N_CHIP):
            dyap = dyap + _dot_nt(dya[:, k * pc:(k + 1) * pc], wpa_ref[k])
        dz = dyap * y1 * sz * (1.0 - sz)
        dz_ref[...] = dz.astype(BF16)
        dy0 = (dyap * sz + _dot_nt(dz, wg_ref[...])) * y1_grad
        dy0_ref[...] = dy0.astype(BF16)
        u = p_ref[:, 0:SSM_W].astype(F32)
        du_ref[...] = dy0 * dsk_ref[...]
        for q in range(SSM_W // LANE):
            rows, cols = slice(q * DIAG_N, (q + 1) * DIAG_N), slice(q * LANE, (q + 1) * LANE)
            dsr_ref[:, rows] = _dot_nt(dy0[:, cols], cre_ref[rows, cols]).astype(BF16)
            dsi_ref[:, rows] = (-_dot_nt(dy0[:, cols], cim_ref[rows, cols])).astype(BF16)

        uv = p_ref[:, SSM_W:lg0].astype(F32)
        uvg, gg = _gelu_and_grad(uv)
        u2 = uvg[:, :SGU_W]
        rv, vh = _rms_stats(uvg[:, SGU_W:])
        v3 = vh * gs_ref[...]
        mixed = mx_ref[...].astype(F32)
        dsgu = jnp.zeros((TB, SGU_W), F32)
        for k in range(N_CHIP):
            dsgu = dsgu + _dot_nt(dyb[:, k * pc:(k + 1) * pc], wpb_ref[k])
        sgu_ref[...] = (u2 * mixed).astype(BF16)
        drest_ref[:, 0:SGU_W] = (dsgu * mixed * gg[:, :SGU_W]).astype(BF16)
        dmix = dsgu * u2
        lane_lo = lax.broadcasted_iota(jnp.int32, (CHUNK, LANE), 1) < SGU_D
        dv3 = _sgu_mix(dmix, wst_ref, lane_lo)
        dbm = jnp.zeros((CHUNK, SGU_W), F32)
        for c0 in range(0, TB, CHUNK):
            dbm = dbm + dmix[c0:c0 + CHUNK]
        for j in range(n_slab):
            lo = jnp.zeros((CHUNK, CHUNK), F32)
            hi = jnp.zeros((CHUNK, CHUNK), F32)
            for c0 in range(0, TB, CHUNK):
                dsl = dmix[c0:c0 + CHUNK, j * LANE:(j + 1) * LANE]
                vsl = v3[c0:c0 + CHUNK, j * LANE:(j + 1) * LANE]
                lo = lo + _dot_nt(jnp.where(lane_lo, dsl, 0.0), vsl)
                hi = hi + _dot_nt(jnp.where(lane_lo, 0.0, dsl), vsl)

            @pl.when(first)
            def _(lo=lo, hi=hi, j=j):
                dws_ref[2 * j] = lo
                dws_ref[2 * j + 1] = hi

            @pl.when(jnp.logical_not(first))
            def _(lo=lo, hi=hi, j=j):
                dws_ref[2 * j] += lo
                dws_ref[2 * j + 1] += hi

        dv2 = _rms_bwd(dv3 * gs_ref[...], vh, rv)
        drest_ref[:, SGU_W:2 * SGU_W] = (dv2 * gg[:, SGU_W:]).astype(BF16)

        upd = jnp.concatenate([jnp.sum(dy0 * u, axis=0, keepdims=True),
                               jnp.sum(dz, axis=0, keepdims=True),
                               jnp.sum(dv3 * vh, axis=0, keepdims=True),
                               jnp.zeros((SUBLANE - 3, SSM_W), F32)], axis=0)

        @pl.when(first)
        def _():
            sm_ref[...] = upd
            dbm_ref[...] = dbm

        @pl.when(jnp.logical_not(first))
        def _():
            sm_ref[...] += upd
            dbm_ref[...] += dbm

    rest = 2 * SGU_W + 2 * D_MODEL
    bf_d, bf_s = jax.ShapeDtypeStruct((t_len, D_MODEL), BF16), jax.ShapeDtypeStruct((t_len, SSM_W), BF16)
    return _call(
        body, "bwd_mix", (t_len // TB,),
        [_rows(TB, D_MODEL), _rows(TB, IN_COLS), _rows(TB, SSM_W), _rows(TB, SSM_W),
         _rows(TB, SGU_W), _rows(TB, D_MODEL), _rows(TB, D_MODEL)] + [_whole()] * 10,
        [_rows(TB, N_STATE), _rows(TB, N_STATE), _rows(TB, SSM_W), _rows(TB, rest),
         _rows(TB, D_MODEL), _rows(TB, D_MODEL), _rows(TB, D_MODEL), _rows(TB, SSM_W),
         _rows(TB, SSM_W), _rows(TB, SSM_W), _rows(TB, SGU_W), _rows(TB, SSM_W),
         _acc(SUBLANE, SSM_W), _acc(CHUNK, SGU_W),
         pl.BlockSpec((SGU_G, CHUNK, CHUNK), lambda i: (0, 0, 0))],
        [jax.ShapeDtypeStruct((t_len, N_STATE), BF16), jax.ShapeDtypeStruct((t_len, N_STATE), BF16),
         jax.ShapeDtypeStruct((t_len, SSM_W), F32), jax.ShapeDtypeStruct((t_len, rest), BF16),
         bf_d, bf_d, bf_d, bf_s, bf_s, bf_s, bf_s, bf_s,
         jax.ShapeDtypeStruct((SUBLANE, SSM_W), F32), jax.ShapeDtypeStruct((CHUNK, SGU_W), F32),
         jax.ShapeDtypeStruct((SGU_G, CHUNK, CHUNK), F32)],
        [dx2, p, y0, z, mixed, ya, yb, w_out, w_pa, w_pb, w_glu, cre, cim, ws_st, wst_st, d_skip,
         g_sgu], jobs)


def _scan_bwd(dsr, dsi, str_, sti, tab_rev, jobs=()):
    t_len = dsr.shape[0]
    tt = min(SCAN_ROWS, t_len)
    nt = t_len // tt
    nblk = tt // SUBLANE
    lb = SCAN_LANES
    tile = BF16_TILE

    def body(dr_ref, di_ref, sr_ref, si_ref, pr_ref, pi_ref, tab_ref, lr_ref, li_ref, dar_ref,
             dai_ref, car_ref, acc_ref):
        tab_v = [tab_ref[q] for q in range(8)]
        row0 = lax.broadcasted_iota(jnp.int32, (SUBLANE, lb), 0) == 0
        t = pl.program_id(1)

        @pl.when(t == 0)
        def _():
            car_ref[...] = jnp.zeros_like(car_ref)
            acc_ref[...] = jnp.zeros_like(acc_ref)

        has_outer = jnp.where(t < nt - 1, 1.0, 0.0)
        outer = (pr_ref[...].astype(F32)[tile - 1:tile] * has_outer,
                 pi_ref[...].astype(F32)[tile - 1:tile] * has_outer)

        def step(k, carry):
            cr, ci, acr, aci = carry
            base = pl.multiple_of((nblk - (k + 1) * SCAN_UNROLL) * SUBLANE, SCAN_UNROLL * SUBLANE)
            state = _load_blocks(sr_ref, si_ref, base)
            before = pl.ds(pl.multiple_of(jnp.maximum(base - tile, 0), tile), tile)
            inner = base > 0
            prev = (jnp.where(inner, sr_ref[before, :].astype(F32)[tile - 1:tile], outer[0]),
                    jnp.where(inner, si_ref[before, :].astype(F32)[tile - 1:tile], outer[1]))
            local = [_scan_local(xr, xi, tab_v, (7, 6, 4))
                     for xr, xi in _load_blocks(dr_ref, di_ref, base)]
            lam = [None] * SCAN_UNROLL
            for b in reversed(range(SCAN_UNROLL)):
                xr, xi = _scan_carry(*local[b], tab_v, cr, ci)
                lam[b] = (xr, xi)
                cr, ci = xr[0:1, :], xi[0:1, :]
                pr, pi = prev if b == 0 else (state[b - 1][0][SUBLANE - 1:], state[b - 1][1][SUBLANE - 1:])
                s_r = jnp.where(row0, pr, pltpu.roll(state[b][0], 1, 0))
                s_i = jnp.where(row0, pi, pltpu.roll(state[b][1], 1, 0))
                acr = acr + xr * s_r + xi * s_i
                aci = aci + xi * s_r - xr * s_i
            _store_blocks(lr_ref, li_ref, base, lam)
            return cr, ci, acr, aci

        cr, ci, acr, aci = lax.fori_loop(
            0, nblk // SCAN_UNROLL, step,
            (car_ref[0:1, :], car_ref[1:2, :], acc_ref[0], acc_ref[1]))
        car_ref[0:1, :] = cr
        car_ref[1:2, :] = ci
        acc_ref[0] = acr
        acc_ref[1] = aci

        @pl.when(t == nt - 1)
        def _():
            dar_ref[...] = acr
            dai_ref[...] = aci

    col = pl.BlockSpec((tt, lb), lambda j, t: (nt - 1 - t, j))
    edge = pl.BlockSpec((tile, lb), lambda j, t: (jnp.maximum((nt - 1 - t) * (tt // tile) - 1, 0), j))
    small = pl.BlockSpec((SUBLANE, lb), lambda j, t: (0, j))
    return _call(
        body, "scan_bwd", (N_STATE // lb, nt),
        [col, col, col, col, edge, edge, pl.BlockSpec((8, SUBLANE, lb), lambda j, t: (0, 0, j))],
        [col, col, small, small],
        [jax.ShapeDtypeStruct((t_len, N_STATE), BF16)] * 2
        + [jax.ShapeDtypeStruct((SUBLANE, N_STATE), F32)] * 2,
        [dsr, dsi, str_, sti, str_, sti, tab_rev], jobs,
        scratch=[pltpu.VMEM((SUBLANE, lb), F32), pltpu.VMEM((2, SUBLANE, lb), F32)])


def _bwd_in(lam_r, lam_i, du_part, drest, x, dx2, g_mix, w_in, bre, bim, jobs=()):
    t_len = x.shape[0]
    cs = IN_COLS // N_CHIP

    def body(lr_ref, li_ref, du_ref, dr_ref, x_ref, dx2_ref, g_ref, w_ref, bre_ref, bim_ref,
             gx_ref, dp_ref, sm_ref):
        i = pl.program_id(0)
        du = du_ref[...] + jnp.concatenate(
            [_dot_nt(lr_ref[:, i * DIAG_N:(i + 1) * DIAG_N],
                     bre_ref[i * LANE:(i + 1) * LANE, i * DIAG_N:(i + 1) * DIAG_N])
             + _dot_nt(li_ref[:, i * DIAG_N:(i + 1) * DIAG_N],
                       bim_ref[i * LANE:(i + 1) * LANE, i * DIAG_N:(i + 1) * DIAG_N])
             for i in range(SSM_W // LANE)], axis=1)
        dp_ref[:, 0:SSM_W] = du.astype(BF16)
        dp_ref[:, SSM_W:] = dr_ref[...]
        dh = jnp.zeros((TB, D_MODEL), F32)
        for k in range(N_CHIP):
            dh = dh + _dot_nt(dp_ref[:, k * cs:(k + 1) * cs], w_ref[k])
        r, xh = _rms_stats(x_ref[...])
        gx_ref[...] = dx2_ref[...] + _rms_bwd(dh * g_ref[...], xh, r)
        upd = jnp.concatenate([jnp.sum(dh * xh, axis=0, keepdims=True),
                               jnp.zeros((SUBLANE - 1, D_MODEL), F32)], axis=0)

        @pl.when(i == 0)
        def _():
            sm_ref[...] = upd

        @pl.when(i > 0)
        def _():
            sm_ref[...] += upd

    return _call(
        body, "bwd_in", (t_len // TB,),
        [_rows(TB, N_STATE), _rows(TB, N_STATE), _rows(TB, SSM_W), _rows(TB, IN_COLS - SSM_W),
         _rows(TB, D_MODEL), _rows(TB, D_MODEL)] + [_whole()] * 4,
        [_rows(TB, D_MODEL), _rows(TB, IN_COLS), _acc(SUBLANE, D_MODEL)],
        [jax.ShapeDtypeStruct((t_len, D_MODEL), F32), jax.ShapeDtypeStruct((t_len, IN_COLS), BF16),
         jax.ShapeDtypeStruct((SUBLANE, D_MODEL), F32)],
        [lam_r, lam_i, du_part, drest, x, dx2, g_mix, w_in, bre, bim], jobs)


def _matmul_tn(a, b, name, out_shape, grid_ij, a_blk, a_map, b_blk, b_map, o_blk, o_map, jobs=()):
    tk = a_blk[0]
    nk = a.shape[0] // tk
    assert nk * tk == a.shape[0] and nk > 0

    def body(a_ref, b_ref, o_ref, acc_ref):
        k = pl.program_id(2)

        @pl.when(k == 0)
        def _():
            acc_ref[...] = jnp.zeros_like(acc_ref)

        acc_ref[...] += lax.dot_general(a_ref[...].astype(BF16), b_ref[...].astype(BF16),
                                        (((0,), (0,)), ((), ())), preferred_element_type=F32)

        @pl.when(k == nk - 1)
        def _():
            o_ref[...] = acc_ref[...]

    outs, per_job = _call(
        body, name, (grid_ij[0], grid_ij[1], nk),
        [pl.BlockSpec(a_blk, a_map), pl.BlockSpec(b_blk, b_map)], [pl.BlockSpec(o_blk, o_map)],
        [jax.ShapeDtypeStruct(out_shape, F32)], [a, b], jobs,
        scratch=[pltpu.VMEM((a_blk[1], b_blk[1]), F32)])
    return outs[0], per_job


def _dw_rows(a, b, name, tm, tk):
    m, n = a.shape[1], b.shape[1]
    tk = min(tk, a.shape[0])
    return _matmul_tn(a, b, name, (m, n), (m // tm, 1),
                      (tk, tm), lambda i, j, k: (k, i), (tk, n), lambda i, j, k: (k, 0),
                      (tm, n), lambda i, j, k: (i, 0))[0]


def _dw_cols(a, b, name, tn, sharded, jobs=()):
    t_len, m = a.shape
    n = b.shape[1]

    def body(a_ref, b_ref, o_ref):
        o_ref[...] = lax.dot_general(a_ref[...].astype(BF16), b_ref[...].astype(BF16),
                                     (((0,), (0,)), ((), ())), preferred_element_type=F32)

    if sharded:
        o_spec, o_shape = pl.BlockSpec((None, m, tn), lambda j: (j, 0, 0)), (n // tn, m, tn)
    else:
        o_spec, o_shape = pl.BlockSpec((m, tn), lambda j: (0, j)), (m, n)
    outs, per_job = _call(body, name, (n // tn,),
                          [_whole(), pl.BlockSpec((t_len, tn), lambda j: (0, j))], [o_spec],
                          [jax.ShapeDtypeStruct(o_shape, F32)], [a, b], jobs)
    return outs[0], per_job


def _dw_tiles(a, b, name, tm, tn, jobs=()):
    t_len, m = a.shape
    n = b.shape[1]

    def body(a_ref, b_ref, o_ref):
        o_ref[...] = lax.dot_general(a_ref[...].astype(BF16), b_ref[...].astype(BF16),
                                     (((0,), (0,)), ((), ())), preferred_element_type=F32)

    outs, per_job = _call(body, name, (n // tn, m // tm),
                          [pl.BlockSpec((t_len, tm), lambda j, i: (0, i)),
                           pl.BlockSpec((t_len, tn), lambda j, i: (0, j))],
                          [pl.BlockSpec((None, tm, tn), lambda j, i: (j, i, 0))],
                          [jax.ShapeDtypeStruct((n // tn, m, tn), F32)], [a, b], jobs)
    return outs[0], per_job


def _dw_pair(a, m, b1, b2, name, jobs=()):
    t_len = a.shape[0]
    n_slab = DIAG_N // LANE
    rows_per_slab = LANE // n_slab

    def body(a_ref, b1_ref, b2_ref, o1_ref, o2_ref):
        for b_ref, o_ref in ((b1_ref, o1_ref), (b2_ref, o2_ref)):
            prod = lax.dot_general(a_ref[...].astype(BF16), b_ref[...].astype(BF16),
                                   (((0,), (0,)), ((), ())), preferred_element_type=F32)
            for j in range(n_slab):
                rows = slice(j * rows_per_slab, (j + 1) * rows_per_slab)
                o_ref[rows, :] = prod[rows, j * LANE:(j + 1) * LANE]

    tok = pl.BlockSpec((t_len, DIAG_N), lambda i: (0, i))
    out = pl.BlockSpec((LANE, LANE), lambda i: (i, 0))
    return _call(body, name, (m // LANE,),
                 [pl.BlockSpec((t_len, LANE), lambda i: (0, i)), tok, tok], [out, out],
                 [jax.ShapeDtypeStruct((m, LANE), F32)] * 2, [a, b1, b2], jobs)


def _prefetch_call(body, name, grid, scalars, in_specs, out_specs, out_shape, args):
    return pl.pallas_call(
        body, name=name,
        grid_spec=pltpu.PrefetchScalarGridSpec(num_scalar_prefetch=1, grid=grid, in_specs=in_specs,
                                               out_specs=out_specs),
        out_shape=out_shape, compiler_params=_params(len(grid)),
    )(scalars, *args)


def _place_shard(w, where, name, dtype, tr):
    rows, cols = w.shape

    def body(s_ref, w_ref, o_ref):
        o_ref[...] = w_ref[...].astype(dtype)

    return _prefetch_call(
        body, name, (rows // tr,), where,
        [pl.BlockSpec((tr, cols), lambda i, s: (i, 0))],
        pl.BlockSpec((None, tr, cols), lambda i, s: (s[0], i, 0)),
        jax.ShapeDtypeStruct((N_CHIP, rows, cols), dtype), [w])


def _place_shards(ws, where, name, dtype):
    n = len(ws)

    def body(s_ref, *refs):
        for t in range(n):
            refs[n + t][...] = refs[t][...].astype(dtype)

    return _prefetch_call(
        body, name, (1,), where,
        [pl.BlockSpec(w.shape, lambda i, s: (0, 0)) for w in ws],
        [pl.BlockSpec((None,) + w.shape, lambda i, s: (s[0], 0, 0)) for w in ws],
        [jax.ShapeDtypeStruct((N_CHIP,) + w.shape, dtype) for w in ws], ws)


def _add_sibling(gs, gots, where, name):
    n = len(gs)
    halves = [(g.shape[1] // 2, g.shape[2]) for g in gs]

    def body(s_ref, *refs):
        for t in range(n):
            refs[2 * n + t][...] = (refs[t][...] + refs[n + t][...]).astype(BF16)

    return _prefetch_call(
        body, name, (N_CHIP,), where,
        [pl.BlockSpec((None, hr, cs), lambda k, s: (k, s[1], 0)) for hr, cs in halves]
        + [pl.BlockSpec((None, hr, cs), lambda k, s: (k, 0, 0)) for hr, cs in halves],
        [pl.BlockSpec((None, hr, cs), lambda k, s: (k, 0, 0)) for hr, cs in halves],
        [jax.ShapeDtypeStruct((N_CHIP, hr, cs), BF16) for hr, cs in halves], list(gs) + list(gots))


def _add_chips(sums, gots, where, name):
    n = len(sums)
    halves = [s.shape[1:] for s in sums]

    def body(s_ref, *refs):
        for t in range(n):
            own_ref, got_ref = refs[t], refs[n + t]
            refs[2 * n + t][...] = ((own_ref[...].astype(F32) + got_ref[0].astype(F32))
                                    + got_ref[1].astype(F32)) + got_ref[2].astype(F32)

    return _prefetch_call(
        body, name, (1,), where,
        [pl.BlockSpec((None, hr, cs), lambda i, s: (s[0], 0, 0)) for hr, cs in halves]
        + [pl.BlockSpec((3, hr, cs), lambda i, s: (0, 0, 0)) for hr, cs in halves],
        [pl.BlockSpec((hr, cs), lambda i, s: (s[1], 0)) for hr, cs in halves],
        [jax.ShapeDtypeStruct((2 * hr, cs), F32) for hr, cs in halves], list(sums) + list(gots))


def _small_allreduce(pack):
    rows = pack.shape[0]
    half = rows // 2

    def body(in_ref, out_ref, sib_ref, slots_ref, s_a, r_a, s_b, r_b, s_c, r_c):
        x, y, c, chips = _place()
        k_me = 2 * x + y
        sib = (x, y, 1 - c)
        first = _remote(in_ref, sib_ref, s_a, r_a, sib)
        first.start()
        first.wait()
        mine = _half(rows, c)
        slots_ref[k_me] = in_ref[mine, :] + sib_ref[mine, :]
        cps = [_remote(slots_ref.at[k_me], slots_ref.at[k_me], s_b.at[j], r_b.at[j], (*ch, c))
               for j, ch in enumerate(chips)]
        for cp in cps:
            cp.start()
        for j, ch in enumerate(chips):
            slot = slots_ref.at[_chip_index(ch)]
            _remote(slot, slot, s_b.at[j], r_b.at[j], (*ch, c)).wait_recv()
        for cp in cps:
            cp.wait_send()
        out_ref[mine, :] = ((slots_ref[0] + slots_ref[1]) + slots_ref[2]) + slots_ref[3]
        last = _remote(out_ref.at[mine, :], out_ref.at[mine, :], s_c, r_c, sib)
        last.start()
        theirs = out_ref.at[_half(rows, 1 - c), :]
        _remote(theirs, theirs, s_c, r_c, sib).wait_recv()
        last.wait_send()

    return pl.pallas_call(
        body, name="small_allreduce", in_specs=[_whole()], out_specs=_whole(),
        out_shape=jax.ShapeDtypeStruct(pack.shape, F32),
        scratch_shapes=[pltpu.VMEM(pack.shape, F32), pltpu.VMEM((N_CHIP, half, LANE), F32),
                        pltpu.SemaphoreType.DMA, pltpu.SemaphoreType.DMA,
                        pltpu.SemaphoreType.DMA((3,)), pltpu.SemaphoreType.DMA((3,)),
                        pltpu.SemaphoreType.DMA, pltpu.SemaphoreType.DMA],
        compiler_params=_params(0),
    )(pack)


def _adamw_update(w_ref, g_ref, m_ref, v_ref, d_ref, mo_ref, vo_ref):
    gv = g_ref[...]
    mn = ADAM_B1 * m_ref[...] + (1.0 - ADAM_B1) * gv
    vn = ADAM_B2 * v_ref[...] + (1.0 - ADAM_B2) * (gv * gv)
    mo_ref[...] = mn
    vo_ref[...] = vn
    m_hat = mn / (1.0 - ADAM_B1 ** ADAM_STEP)
    v_hat = vn / (1.0 - ADAM_B2 ** ADAM_STEP)
    d_ref[...] = -ADAM_LR * (m_hat / (jnp.sqrt(v_hat) + ADAM_EPS) + ADAM_WD * w_ref[...])


def _adamw(w, g, m, v, name, tr):
    rows, cols = w.shape
    blk = _rows(tr, cols)

    def body(w_ref, g_ref, m_ref, v_ref, go_ref, d_ref, mo_ref, vo_ref):
        go_ref[...] = g_ref[...]
        _adamw_update(w_ref, g_ref, m_ref, v_ref, d_ref, mo_ref, vo_ref)

    return _call(body, name, (rows // tr,), [blk] * 4, [blk] * 4,
                 [jax.ShapeDtypeStruct(w.shape, F32)] * 4, [w, g, m, v])[0]


def _adamw_many(ws, gs, ms, vs, name):
    n = len(ws)

    def body(*refs):
        for t in range(n):
            _adamw_update(*[refs[q * n + t] for q in range(7)])

    specs = [pl.BlockSpec(a.shape, lambda i, nd=a.ndim: (0,) * nd) for a in ws]
    outs = pl.pallas_call(
        body, name=name, grid=(1,), in_specs=specs * 4, out_specs=specs * 3,
        out_shape=[jax.ShapeDtypeStruct(a.shape, F32) for _ in range(3) for a in ws],
        compiler_params=_params(1),
    )(*ws, *gs, *ms, *vs)
    return outs[:n], outs[n:2 * n], outs[2 * n:]


def _ssm_discretize(a_re, a_im, log_dt, b_re, b_im):
    dt = jnp.exp(log_dt)[:, None]
    mag = jnp.exp(dt * a_re)
    abr = mag * jnp.cos(dt * a_im)
    abi = mag * jnp.sin(dt * a_im)
    den = a_re * a_re + a_im * a_im
    nr = abr - 1.0
    ni = abi
    f_re = (nr * a_re + ni * a_im) / den
    f_im = (ni * a_re - nr * a_im) / den
    bbr = f_re[..., None] * b_re - f_im[..., None] * b_im
    bbi = f_re[..., None] * b_im + f_im[..., None] * b_re
    return abr, abi, bbr, bbi


def _scan_tables(abr, abi):
    ar = abr.reshape(1, N_STATE)
    ai = abi.reshape(1, N_STATE)
    pr, pi = [ar], [ai]
    for _ in range(SUBLANE - 1):
        pr, pi = pr + [pr[-1] * ar - pi[-1] * ai], pi + [pr[-1] * ai + pi[-1] * ar]
    row = jnp.arange(SUBLANE)[:, None]
    tabs = []
    for d in (1, 2, 4):
        tabs.append(jnp.where(row >= d, pr[d - 1], 0.0))
        tabs.append(jnp.where(row >= d, pi[d - 1], 0.0))
    tabs.append(jnp.concatenate(pr, axis=0))
    tabs.append(jnp.concatenate(pi, axis=0))
    fwd = jnp.stack(tabs)
    sign = jnp.array([1.0, -1.0] * 4, F32)[:, None, None]
    return fwd, fwd[:, ::-1, :] * sign


def _block_diag_b(bb):
    strip = bb.transpose(2, 0, 1).reshape(SSM_H, N_STATE)
    rows = lax.broadcasted_iota(jnp.int32, (SSM_W, N_STATE), 0) // SSM_H
    cols = lax.broadcasted_iota(jnp.int32, (SSM_W, N_STATE), 1) // SSM_P
    return jnp.where(rows == cols, jnp.tile(strip, (SSM_G, 1)), 0.0).astype(BF16)


def _block_diag_c(cc):
    strip = cc.transpose(0, 2, 1).reshape(N_STATE, SSM_H)
    rows = lax.broadcasted_iota(jnp.int32, (N_STATE, SSM_W), 0) // SSM_P
    cols = lax.broadcasted_iota(jnp.int32, (N_STATE, SSM_W), 1) // SSM_H
    return jnp.where(rows == cols, jnp.tile(strip, (1, SSM_G)), 0.0).astype(BF16)


SMALL_SHAPES = {
    "g_mix": (D_MODEL,), "a_re": (SSM_G, SSM_P), "a_im": (SSM_G, SSM_P), "log_dt": (SSM_G,),
    "b_re": (SSM_G, SSM_P, SSM_H), "b_im": (SSM_G, SSM_P, SSM_H),
    "c_re": (SSM_G, SSM_H, SSM_P), "c_im": (SSM_G, SSM_H, SSM_P),
    "d_skip": (SSM_W,), "b_glu": (SSM_W,), "g_sgu": (SGU_W,), "w_s": (SGU_G, CHUNK, CHUNK),
    "b_s": (SGU_G, CHUNK), "g_ffn": (D_MODEL,), "conv_b": (2 * D_FF,), "g_final": (D_MODEL,),
}
PACK_ITEMS = [("loss", (1,))] + [(n, SMALL_SHAPES[n]) for n in SMALL] + [("conv_w", (3, 2 * D_FF))]
TILE = SUBLANE * LANE


def _item_rows(shape):
    return -(-math.prod(shape) // TILE) * SUBLANE


PACK_ROWS = -(-sum(_item_rows(s) for _, s in PACK_ITEMS) // (2 * SUBLANE)) * (2 * SUBLANE)


def _pack(values):
    parts, used = [], 0
    for name, shape in PACK_ITEMS:
        size, rows = math.prod(shape), _item_rows(shape)
        if name in values:
            flat = values[name].astype(F32).reshape(size)
            if rows * LANE > size:
                flat = jnp.pad(flat, (0, rows * LANE - size))
            parts.append(flat.reshape(rows, LANE))
        else:
            parts.append(jnp.zeros((rows, LANE), F32))
        used += rows
    if PACK_ROWS > used:
        parts.append(jnp.zeros((PACK_ROWS - used, LANE), F32))
    return jnp.concatenate(parts, axis=0)


def _unpack(pack):
    out, off = {}, 0
    for name, shape in PACK_ITEMS:
        rows = _item_rows(shape)
        out[name] = pack[off:off + rows].reshape(rows * LANE)[:math.prod(shape)].reshape(shape)
        off += rows
    return out


PLACE_ROWS = {"w_in": 256, "w_up": 256, "w_down": 352, "w_out": 256, "w_proj_a": 256,
              "w_proj_b": 256, "w_glu": 128}


def kernel(x, g_mix, w_in, a_re, a_im, log_dt, b_re, b_im, c_re, c_im, d_skip, w_glu, b_glu, w_proj_a, g_sgu, w_s, b_s, w_proj_b, w_out, g_ffn, w_up, conv_w, conv_b, w_down, g_final, loss_target, m_g_mix, m_w_in, m_a_re, m_a_im, m_log_dt, m_b_re, m_b_im, m_c_re, m_c_im, m_d_skip, m_w_glu, m_b_glu, m_w_proj_a, m_g_sgu, m_w_s, m_b_s, m_w_proj_b, m_w_out, m_g_ffn, m_w_up, m_conv_w, m_conv_b, m_w_down, m_g_final, v_g_mix, v_w_in, v_a_re, v_a_im, v_log_dt, v_b_re, v_b_im, v_c_re, v_c_im, v_d_skip, v_w_glu, v_b_glu, v_w_proj_a, v_g_sgu, v_w_s, v_b_s, v_w_proj_b, v_w_out, v_g_ffn, v_w_up, v_conv_w, v_conv_b, v_w_down, v_g_final):
    given = dict(locals())
    w = {n: given[n] for n in WEIGHTS}
    m = {n: given["m_" + n] for n in WEIGHTS}
    v = {n: given["v_" + n] for n in WEIGHTS}

    def shard2d(a):
        return a.reshape(a.shape[-2], a.shape[-1])

    chip = 2 * lax.axis_index("x") + lax.axis_index("y")
    where = jnp.stack([chip, lax.axis_index("c")]).astype(jnp.int32)
    xs, target = x[0], loss_target[0]
    small = {n: w[n].reshape(SMALL_SHAPES[n]) for n in SMALL}

    (abr, abi, bbr, bbi), disc_vjp = jax.vjp(_ssm_discretize, small["a_re"], small["a_im"],
                                             small["log_dt"], small["b_re"], small["b_im"])
    tab_f, tab_r = _scan_tables(abr, abi)
    bre = _block_diag_b(bbr)
    bim = _block_diag_b(bbi)
    cre = _block_diag_c(small["c_re"])
    cim = _block_diag_c(small["c_im"])
    tril = jnp.tril(jnp.ones((CHUNK, CHUNK), dtype=bool))
    ws = jnp.where(tril[None], small["w_s"], 0.0)
    ws_st = ws.reshape(SGU_G // 2, 2 * CHUNK, CHUNK).astype(BF16)
    wst_st = ws.transpose(0, 2, 1).reshape(SGU_G // 2, 2 * CHUNK, CHUNK).astype(BF16)
    bmat = jnp.repeat(small["b_s"].T, SGU_D, axis=1)
    g_mix2 = small["g_mix"].reshape(1, D_MODEL)
    g_ffn2 = small["g_ffn"].reshape(1, D_MODEL)
    g_final2 = small["g_final"].reshape(1, D_MODEL)
    g_sgu2 = small["g_sgu"].reshape(1, SGU_W)
    d_skip2 = small["d_skip"].reshape(1, SSM_W)
    b_glu2 = small["b_glu"].reshape(1, SSM_W)
    conv_b2 = small["conv_b"].reshape(1, 2 * D_FF)

    gat = {"w_in": _place_shard(shard2d(w["w_in"]), where, "place_w_in", BF16, PLACE_ROWS["w_in"])}
    gat.update(zip(BIG[1:], _place_shards([shard2d(w[n]) for n in BIG[1:]], where, "place_rest", BF16)))
    gat["conv_w"] = _place_shard(shard2d(w["conv_w"]), where, "place_conv_w", F32, 3)
    all_rows = (0, D_MODEL)
    (gat["w_in"],), = _comm("gather_in", [_job_gather(
        [gat["w_in"]], [(0, all_rows, ICI, (0.0, 0.5)), (0, all_rows, SIBLING, (0.5, 1.0))])])
    mixers = ["w_glu", "w_proj_a", "w_proj_b", "w_out"]
    rows = {n: (0, gat[n].shape[1]) for n in mixers}
    down_a, down_b = (0, D_FF // 8), (D_FF // 8, D_FF // 8)
    up_a, up_b = (0, 3 * D_MODEL // 8), (3 * D_MODEL // 8, 5 * D_MODEL // 8)
    span = (0.0, 1.0)

    names = mixers + ["conv_w", "w_down"]
    (p, h1, bur, bui), (got,) = _fwd_in(
        xs, g_mix2, gat["w_in"], bre, bim,
        [_job_gather([gat[n] for n in names],
                     [(i, rows[n], ICI, span) for i, n in enumerate(mixers)]
                     + [(4, None, ICI, span), (5, down_a, ICI, span)])])
    gat.update(zip(names, got))
    names = mixers + ["w_down", "w_up"]
    (str_, sti), (got,) = _scan_fwd(
        bur, bui, tab_f,
        [_job_gather([gat[n] for n in names],
                     [(i, rows[n], SIBLING, span) for i, n in enumerate(mixers)]
                     + [(4, down_a, SIBLING, span), (4, down_b, ICI, span), (5, up_a, ICI, span)])])
    gat.update(zip(names, got))
    w_glu_f = gat["w_glu"].reshape(SSM_W, SSM_W)
    w_out_f = gat["w_out"].reshape(D_MODEL, D_MODEL)
    conv_w_f = gat["conv_w"].transpose(1, 0, 2).reshape(3, 2 * D_FF)
    (x2, y0, z, mixed, ya, yb), ((gat["w_down"], gat["w_up"]),) = _fwd_mix(
        xs, p, str_, sti, cre, cim, d_skip2, w_glu_f, b_glu2, gat["w_proj_a"], g_sgu2, ws_st, bmat,
        gat["w_proj_b"], w_out_f,
        [_job_gather([gat["w_down"], gat["w_up"]],
                     [(0, down_b, SIBLING, span), (1, up_a, SIBLING, span),
                      (1, up_b, ICI, (0.0, 0.75)), (1, up_b, SIBLING, (0.75, 1.0))])])
    w_down_f = gat["w_down"].reshape(D_FF, D_MODEL)
    up, act, f, h2, dx3, sm_ffn = _fwd_ffn(x2, target, g_ffn2, gat["w_up"], conv_w_f, conv_b2,
                                           w_down_f, g_final2)

    def leg1_done(names, got):
        return _add_sibling([part[n] for n in names], got, where, "add_sibling_" + names[0])

    def leg2_done(names, sums, got):
        return _add_chips(sums, got, where, "add_chips_" + names[0])

    part, red = {}, {}
    part["w_down"] = _dw_rows(f, dx3, "dw_down", D_FF // 2, 4 * TK).reshape(
        N_CHIP, D_FF // N_CHIP, D_MODEL)
    (dx2, dup, sm_conv, sm_gffn), (got,) = _bwd_ffn(
        dx3, up, act, x2, g_ffn2, gat["w_up"], conv_w_f, w_down_f,
        [_job_sibling_halves([part["w_down"]])])
    sum_down = leg1_done(["w_down"], got)
    part["w_up"], (got,) = _dw_tiles(h2, dup, "dw_up", D_MODEL // 2, 2 * D_FF // N_CHIP,
                                     [_job_to_owner(sum_down)])
    red_down = leg2_done(["w_down"], sum_down, got)
    ((dsr, dsi, du_part, drest, mrg, dya, dyb, yap, dz, y1, sgu, dy0, sm_mix, dbm, dws),
     (got, (red["w_down"],))) = _bwd_mix(
        dx2, p, y0, z, mixed, ya, yb, w_out_f, gat["w_proj_a"], gat["w_proj_b"], w_glu_f, cre, cim,
        ws_st, wst_st, d_skip2, g_sgu2,
        [_job_sibling_halves([part["w_up"]]), _job_swap_halves(red_down)])
    sum_up = leg1_done(["w_up"], got)
    (lam_r, lam_i, dar8, dai8), (got,) = _scan_bwd(dsr, dsi, str_, sti, tab_r, [_job_to_owner(sum_up)])
    red_up = leg2_done(["w_up"], sum_up, got)
    mix4 = ["w_out", "w_proj_a", "w_proj_b", "w_glu"]
    part["w_out"] = _dw_cols(mrg, dx2, "dw_out", D_MODEL // 2, False)[0].reshape(
        N_CHIP, D_MODEL // N_CHIP, D_MODEL)
    part["w_proj_a"] = _dw_cols(yap, dya, "dw_proj_a", D_MODEL // N_CHIP, True)[0]
    part["w_proj_b"] = _dw_cols(sgu, dyb, "dw_proj_b", D_MODEL // N_CHIP, True)[0]
    part["w_glu"] = _dw_cols(y1, dz, "dw_glu", SSM_W, False)[0].reshape(
        N_CHIP, SSM_W // N_CHIP, SSM_W)
    (grad_x, dp, sm_gmix), (got, (red["w_up"],)) = _bwd_in(
        lam_r, lam_i, du_part, drest, xs, dx2, g_mix2, gat["w_in"], bre, bim,
        [_job_sibling_halves([part[n] for n in mix4]), _job_swap_halves(red_up)])
    sums_m = leg1_done(mix4, got)
    part["w_in"], (got,) = _dw_cols(h1, dp, "dw_in", IN_COLS // N_CHIP, True, [_job_to_owner(sums_m)])
    red_m = leg2_done(mix4, sums_m, got)
    (dbd_r, dbd_i), (got, done_m) = _dw_pair(
        p, SSM_W, lam_r, lam_i, "db_bar",
        [_job_sibling_halves([part["w_in"]]), _job_swap_halves(red_m)])
    red.update(zip(mix4, done_m))
    sum_in = leg1_done(["w_in"], got)
    (dcd_r, dcd_i), (got,) = _dw_pair(dy0, SSM_W, str_, sti, "dc", [_job_to_owner(sum_in)])
    red_in = leg2_done(["w_in"], sum_in, got)
    (red["w_in"],), = _comm("swap_w_in", [_job_swap_halves(red_in)])

    def pick_c(slabs):
        two = LANE // SSM_P
        return jnp.einsum("jshsp->jshp", slabs.reshape(SSM_G // two, two, SSM_H, two, SSM_P)
                          ).reshape(SSM_G, SSM_H, SSM_P)

    def pick_b(slabs):
        return pick_c(slabs).transpose(0, 2, 1)

    dabr = jnp.sum(dar8, axis=0).reshape(SSM_G, SSM_P)
    dabi = jnp.sum(dai8, axis=0).reshape(SSM_G, SSM_P)
    d_a_re, d_a_im, d_log_dt, d_b_re, d_b_im = disc_vjp((dabr, dabi, pick_b(dbd_r), pick_b(dbd_i)))
    gsmall = {
        "g_mix": sm_gmix[0], "a_re": d_a_re, "a_im": d_a_im, "log_dt": d_log_dt,
        "b_re": d_b_re, "b_im": d_b_im, "c_re": pick_c(dcd_r), "c_im": -pick_c(dcd_i),
        "d_skip": sm_mix[0], "b_glu": sm_mix[1], "g_sgu": sm_mix[2],
        "w_s": jnp.where(tril[None], dws, 0.0),
        "b_s": dbm.reshape(CHUNK, SGU_G, SGU_D).sum(-1).T,
        "g_ffn": sm_gffn[0], "conv_b": sm_conv[3], "g_final": sm_ffn[0],
        "conv_w": sm_conv[0:3], "loss": sm_ffn[1, 0:1],
    }

    total_pack = _small_allreduce(_pack(gsmall))
    total = _unpack(total_pack)
    grads = dict(red)
    cs = 2 * D_FF // N_CHIP
    grads["conv_w"] = lax.dynamic_slice(total["conv_w"], (0, chip * cs), (3, cs))
    delta, new_m, new_v = {}, {}, {}
    for n in BIG + ("conv_w",):
        grads[n], delta[n], new_m[n], new_v[n] = _adamw(
            shard2d(w[n]), grads[n], shard2d(m[n]), shard2d(v[n]), "adamw_" + n, PLACE_ROWS.get(n, 3))
    for n in SMALL:
        grads[n] = total[n].reshape(w[n].shape)
    ud, um, uv = _adamw_many(*[[d[n] for n in SMALL] for d in (w, grads, m, v)], "adamw_small")
    for i, n in enumerate(SMALL):
        delta[n], new_m[n], new_v[n] = ud[i], um[i], uv[i]

    def like(d):
        return [d[n].reshape(w[n].shape) for n in WEIGHTS]

    return (total["loss"].reshape(()), grad_x.reshape(x.shape), *like(grads), *like(delta),
            *like(new_m), *like(new_v))
```

```python
import math

import jax
import jax.numpy as jnp
from jax import lax
from jax.experimental import pallas as pl
from jax.experimental.pallas import tpu as pltpu

F32 = jnp.float32
BF16 = jnp.bfloat16
MESH = pl.DeviceIdType.MESH

D_MODEL = 1024
SSM_W = 512
SSM_G = 32
SSM_H = 16
SSM_P = 64
N_STATE = SSM_G * SSM_P
DIAG_N = 128 * SSM_P // SSM_H
SGU_W = 512
SGU_G = 8
SGU_D = 64
CHUNK = 128
D_FF = 2816
IN_COLS = 3584
EPS = 1e-6
N_CHIP = 4

ADAM_LR = 0.001
ADAM_B1 = 0.9
ADAM_B2 = 0.999
ADAM_EPS = 1e-08
ADAM_WD = 0.01
ADAM_STEP = 10

SUBLANE = 8
LANE = 128
VMEM_LIMIT = 56 * 1024 * 1024
TB = 256
TK = 512
SCAN_LANES = 256
SCAN_UNROLL = 4
HALO = SUBLANE

BIG = ("w_in", "w_up", "w_down", "w_out", "w_proj_a", "w_proj_b", "w_glu")
SMALL = ("g_mix", "a_re", "a_im", "log_dt", "b_re", "b_im", "c_re", "c_im", "d_skip", "b_glu",
         "g_sgu", "w_s", "b_s", "g_ffn", "conv_b", "g_final")
WEIGHTS = ("g_mix", "w_in", "a_re", "a_im", "log_dt", "b_re", "b_im", "c_re", "c_im", "d_skip",
           "w_glu", "b_glu", "w_proj_a", "g_sgu", "w_s", "b_s", "w_proj_b", "w_out", "g_ffn",
           "w_up", "conv_w", "conv_b", "w_down", "g_final")

ANY = pl.BlockSpec(memory_space=pl.ANY)


def _params(n_grid):
    return pltpu.CompilerParams(dimension_semantics=("arbitrary",) * n_grid if n_grid else None,
                                vmem_limit_bytes=VMEM_LIMIT)


def _whole():
    return pl.BlockSpec(memory_space=pltpu.VMEM)


def _rows(tb, ncol):
    return pl.BlockSpec((tb, ncol), lambda i: (i, 0))


def _acc(nrow, ncol):
    return pl.BlockSpec((nrow, ncol), lambda i: (0, 0))


def _dot(a, b):
    return jnp.dot(a.astype(BF16), b.astype(BF16), preferred_element_type=F32)


def _dot_nt(a, b):
    return lax.dot_general(a.astype(BF16), b.astype(BF16), (((1,), (1,)), ((), ())),
                           preferred_element_type=F32)


def _sigmoid(v):
    return 0.5 * jnp.tanh(0.5 * v) + 0.5


_GELU_C = math.sqrt(2.0 / math.pi)


def _gelu(v):
    return 0.5 * v * (1.0 + jnp.tanh(_GELU_C * (v + 0.044715 * v * v * v)))


def _gelu_and_grad(v):
    v2 = v * v
    t = jnp.tanh(_GELU_C * v * (1.0 + 0.044715 * v2))
    half = 0.5 * (1.0 + t)
    return v * half, half + 0.5 * v * (1.0 - t * t) * _GELU_C * (1.0 + 3.0 * 0.044715 * v2)


def _rms_stats(v):
    r = lax.rsqrt(jnp.mean(v * v, axis=-1, keepdims=True) + EPS)
    return r, v * r


def _rms_bwd(dxh, xh, r):
    return r * (dxh - xh * jnp.mean(dxh * xh, axis=-1, keepdims=True))


def _place():
    x, y, c = lax.axis_index("x"), lax.axis_index("y"), lax.axis_index("c")
    chips = [(1 - x, y), (x, 1 - y), (1 - x, 1 - y)]
    return x, y, c, chips


def _chip_index(chip):
    return 2 * chip[0] + chip[1]


def _remote(src, dst, send_sem, recv_sem, device):
    return pltpu.make_async_remote_copy(src_ref=src, dst_ref=dst, send_sem=send_sem,
                                        recv_sem=recv_sem, device_id=device, device_id_type=MESH)


def _half(ref_rows, c):
    hr = ref_rows // 2
    return pl.ds(pl.multiple_of(c * hr, SUBLANE), hr)


class _Job:
    def __init__(self, hooks, n_sem, ins=(), inouts=(), outs=()):
        self.hooks, self.n_sem = list(hooks), n_sem
        self.ins, self.inouts, self.outs = list(ins), list(inouts), list(outs)


def _whole_span(start, finish):
    return [(0.0, "start", start), (1.0, "finish", finish)]


ICI, SIBLING = "ici", "sibling"


def _job_gather(bufs, legs):
    def copies(io, leg, first):
        b, window, kind, _ = legs[leg]
        x, y, c, chips = _place()
        k_me = 2 * x + y
        out = []
        for j, ch in enumerate(chips):
            k = _chip_index(ch)
            if window is None:
                src, land, dev = io[b].at[k_me], io[b].at[k], (*ch, c)
            else:
                r0, rows = window
                mine = pl.ds(pl.multiple_of(r0 + c * (rows // 2), SUBLANE), rows // 2)
                theirs = pl.ds(pl.multiple_of(r0 + (1 - c) * (rows // 2), SUBLANE), rows // 2)
                if kind == ICI:
                    src, land, dev = io[b].at[k_me, mine, :], io[b].at[k, mine, :], (*ch, c)
                else:
                    src, land, dev = io[b].at[k, mine, :], io[b].at[k, theirs, :], (x, y, 1 - c)
            out.append((src, land, first + j, dev))
        return out

    def starter(leg):
        def start(ins, io, outs, ssem, rsem):
            for src, _, i, dev in copies(io, leg, 3 * leg):
                _remote(src, src, ssem(i), rsem(i), dev).start()
        return start

    def finisher(leg):
        def finish(ins, io, outs, ssem, rsem):
            cps = copies(io, leg, 3 * leg)
            for _, land, i, dev in cps:
                _remote(land, land, ssem(i), rsem(i), dev).wait_recv()
            for src, _, i, dev in cps:
                _remote(src, src, ssem(i), rsem(i), dev).wait_send()
        return finish

    hooks = []
    for leg, (_, _, _, (begin, end)) in enumerate(legs):
        hooks += [(begin, "start", starter(leg)), (end, "finish", finisher(leg))]
    return _Job(hooks, 3 * len(legs), inouts=bufs)


def _job_sibling_halves(grads):
    n = len(grads)

    def build(ins, outs, ssem, rsem):
        x, y, c, _ = _place()
        return [_remote(ins[t].at[:, _half(grads[t].shape[1], 1 - c), :], outs[t], ssem(t), rsem(t),
                        (x, y, 1 - c)) for t in range(n)]

    def start(ins, io, outs, ssem, rsem):
        for cp in build(ins, outs, ssem, rsem):
            cp.start()

    def finish(ins, io, outs, ssem, rsem):
        for cp in build(ins, outs, ssem, rsem):
            cp.wait()

    return _Job(_whole_span(start, finish), n, ins=grads,
                outs=[jax.ShapeDtypeStruct((N_CHIP, g.shape[1] // 2, g.shape[2]), F32) for g in grads])


def _job_to_owner(sums):
    n = len(sums)

    def build(ins, outs, ssem, rsem):
        x, y, c, chips = _place()
        return [_remote(ins[t].at[_chip_index(ch)], outs[t].at[j], ssem(3 * t + j), rsem(3 * t + j),
                        (*ch, c)) for t in range(n) for j, ch in enumerate(chips)]

    def start(ins, io, outs, ssem, rsem):
        for cp in build(ins, outs, ssem, rsem):
            cp.start()

    def finish(ins, io, outs, ssem, rsem):
        for cp in build(ins, outs, ssem, rsem):
            cp.wait()

    return _Job(_whole_span(start, finish), 3 * n, ins=sums,
                outs=[jax.ShapeDtypeStruct((3,) + s.shape[1:], s.dtype) for s in sums])


def _job_swap_halves(bufs):
    n = len(bufs)

    def start(ins, io, outs, ssem, rsem):
        x, y, c, _ = _place()
        for t in range(n):
            mine = io[t].at[_half(bufs[t].shape[0], c), :]
            _remote(mine, mine, ssem(t), rsem(t), (x, y, 1 - c)).start()

    def finish(ins, io, outs, ssem, rsem):
        x, y, c, _ = _place()
        for t in range(n):
            theirs = io[t].at[_half(bufs[t].shape[0], 1 - c), :]
            _remote(theirs, theirs, ssem(t), rsem(t), (x, y, 1 - c)).wait_recv()
        for t in range(n):
            mine = io[t].at[_half(bufs[t].shape[0], c), :]
            _remote(mine, mine, ssem(t), rsem(t), (x, y, 1 - c)).wait_send()

    return _Job(_whole_span(start, finish), n, inouts=bufs)


def _call(body, name, grid, in_specs, out_specs, out_shape, args, jobs=(), scratch=()):
    n_in, n_out, n_scr = len(args), len(out_shape), len(scratch)
    job_in = [a for jb in jobs for a in jb.ins + jb.inouts]
    job_out = [s for jb in jobs
               for s in [jax.ShapeDtypeStruct(a.shape, a.dtype) for a in jb.inouts] + jb.outs]
    aliases, pos_in, pos_out = {}, n_in, n_out
    for jb in jobs:
        pos_in += len(jb.ins)
        for _ in jb.inouts:
            aliases[pos_in] = pos_out
            pos_in += 1
            pos_out += 1
        pos_out += len(jb.outs)
    n_sem = sum(jb.n_sem for jb in jobs)

    def wrapped(*refs):
        c_in = refs[:n_in]
        j_in = refs[n_in:n_in + len(job_in)]
        c_out = refs[n_in + len(job_in):n_in + len(job_in) + n_out]
        j_out = refs[n_in + len(job_in) + n_out:n_in + len(job_in) + n_out + len(job_out)]
        rest = refs[n_in + len(job_in) + n_out + len(job_out):]
        c_scr = rest[:n_scr]
        views, pi, po, ps = [], 0, 0, 0
        for jb in jobs:
            ins = j_in[pi:pi + len(jb.ins)]
            pi += len(jb.ins) + len(jb.inouts)
            io = j_out[po:po + len(jb.inouts)]
            new = j_out[po + len(jb.inouts):po + len(jb.inouts) + len(jb.outs)]
            po += len(jb.inouts) + len(jb.outs)
            send = (lambda i, o=ps: rest[n_scr].at[o + i])
            recv = (lambda i, o=ps: rest[n_scr + 1].at[o + i])
            ps += jb.n_sem
            views.append((ins, io, new, send, recv))

        def run(frac):
            for kind in ("finish", "start"):
                for jb, vw in zip(jobs, views):
                    for at, what, fn in jb.hooks:
                        if at == frac and what == kind:
                            fn(*vw)

        fracs = sorted({at for jb in jobs for at, _, _ in jb.hooks})
        if not grid:
            for frac in fracs:
                run(frac)
            return
        if jobs:
            assert len(grid) == 1 or set(fracs) <= {0.0, 1.0}
            first = pl.program_id(0) == 0
            last = pl.program_id(0) == grid[0] - 1
            for d in range(1, len(grid)):
                first = jnp.logical_and(first, pl.program_id(d) == 0)
                last = jnp.logical_and(last, pl.program_id(d) == grid[d] - 1)
            for frac in fracs:
                if frac < 1.0:
                    at_step = first if frac == 0.0 else pl.program_id(0) == int(frac * grid[0])
                    pl.when(at_step)(lambda frac=frac: run(frac))
        body(*c_in, *c_out, *c_scr)
        if jobs and 1.0 in fracs:
            pl.when(last)(lambda: run(1.0))

    sems = [pltpu.SemaphoreType.DMA((n_sem,)), pltpu.SemaphoreType.DMA((n_sem,))] if jobs else []
    kwargs = dict(grid=grid) if grid else {}
    res = pl.pallas_call(
        wrapped, name=name, in_specs=list(in_specs) + [ANY] * len(job_in),
        out_specs=list(out_specs) + [ANY] * len(job_out),
        out_shape=list(out_shape) + job_out, scratch_shapes=list(scratch) + sems,
        input_output_aliases=aliases, compiler_params=_params(len(grid)), **kwargs,
    )(*args, *job_in)
    outs, pos, per_job = list(res[:n_out]), n_out, []
    for jb in jobs:
        k = len(jb.inouts) + len(jb.outs)
        per_job.append(list(res[pos:pos + k]))
        pos += k
    return outs, per_job


def _comm(name, jobs):
    return _call(None, name, (), [], [], [], [], jobs)[1]


def _fwd_in(x, g_mix, w_in, bre, bim, jobs=()):
    t_len = x.shape[0]
    cs = IN_COLS // N_CHIP

    def body(x_ref, g_ref, w_ref, bre_ref, bim_ref, p_ref, h_ref, bur_ref, bui_ref):
        xv = x_ref[...]
        r, xh = _rms_stats(xv)
        h = (xh * g_ref[...]).astype(BF16)
        h_ref[...] = h
        for k in range(N_CHIP):
            p_ref[:, k * cs:(k + 1) * cs] = jnp.dot(h, w_ref[k],
                                                    preferred_element_type=F32).astype(BF16)
        u = p_ref[:, 0:SSM_W]
        for i in range(SSM_W // LANE):
            rows, cols = slice(i * LANE, (i + 1) * LANE), slice(i * DIAG_N, (i + 1) * DIAG_N)
            bur_ref[:, cols] = jnp.dot(u[:, rows], bre_ref[rows, cols],
                                       preferred_element_type=F32).astype(BF16)
            bui_ref[:, cols] = jnp.dot(u[:, rows], bim_ref[rows, cols],
                                       preferred_element_type=F32).astype(BF16)

    return _call(
        body, "fwd_in", (t_len // TB,),
        [_rows(TB, D_MODEL), _whole(), _whole(), _whole(), _whole()],
        [_rows(TB, IN_COLS), _rows(TB, D_MODEL), _rows(TB, N_STATE), _rows(TB, N_STATE)],
        [jax.ShapeDtypeStruct((t_len, IN_COLS), BF16), jax.ShapeDtypeStruct((t_len, D_MODEL), BF16),
         jax.ShapeDtypeStruct((t_len, N_STATE), BF16), jax.ShapeDtypeStruct((t_len, N_STATE), BF16)],
        [x, g_mix, w_in, bre, bim], jobs)


def _scan_local(xr, xi, tab, shifts):
    for q, s in enumerate(shifts):
        ar, ai = tab[2 * q], tab[2 * q + 1]
        rr = pltpu.roll(xr, s, 0)
        ri = pltpu.roll(xi, s, 0)
        xr, xi = xr + ar * rr - ai * ri, xi + ar * ri + ai * rr
    return xr, xi


def _scan_carry(xr, xi, tab, cr, ci):
    pr, pi = tab[6], tab[7]
    return xr + pr * cr - pi * ci, xi + pr * ci + pi * cr


BF16_TILE = 2 * SUBLANE


def _load_blocks(r_ref, i_ref, base):
    out = []
    for q in range(SCAN_UNROLL // 2):
        rows = pl.ds(pl.multiple_of(base + q * BF16_TILE, BF16_TILE), BF16_TILE)
        vr, vi = r_ref[rows, :].astype(F32), i_ref[rows, :].astype(F32)
        out += [(vr[:SUBLANE], vi[:SUBLANE]), (vr[SUBLANE:], vi[SUBLANE:])]
    return out


def _store_blocks(r_ref, i_ref, base, blocks):
    for q in range(SCAN_UNROLL // 2):
        rows = pl.ds(pl.multiple_of(base + q * BF16_TILE, BF16_TILE), BF16_TILE)
        r_ref[rows, :] = jnp.concatenate([blocks[2 * q][0], blocks[2 * q + 1][0]], 0).astype(r_ref.dtype)
        i_ref[rows, :] = jnp.concatenate([blocks[2 * q][1], blocks[2 * q + 1][1]], 0).astype(i_ref.dtype)


def _scan_fwd(bur, bui, tab, jobs=()):
    t_len = bur.shape[0]
    nblk = t_len // SUBLANE
    lb = SCAN_LANES

    def body(br_ref, bi_ref, tab_ref, sr_ref, si_ref):
        tab_v = [tab_ref[q] for q in range(8)]

        def step(k, carry):
            cr, ci = carry
            base = pl.multiple_of(k * SCAN_UNROLL * SUBLANE, SCAN_UNROLL * SUBLANE)
            local = [_scan_local(xr, xi, tab_v, (1, 2, 4))
                     for xr, xi in _load_blocks(br_ref, bi_ref, base)]
            done = []
            for xr, xi in local:
                xr, xi = _scan_carry(xr, xi, tab_v, cr, ci)
                done.append((xr, xi))
                cr, ci = xr[SUBLANE - 1:SUBLANE, :], xi[SUBLANE - 1:SUBLANE, :]
            _store_blocks(sr_ref, si_ref, base, done)
            return cr, ci

        zero = jnp.zeros((1, lb), F32)
        lax.fori_loop(0, nblk // SCAN_UNROLL, step, (zero, zero))

    col = pl.BlockSpec((t_len, lb), lambda j: (0, j))
    return _call(
        body, "scan_fwd", (N_STATE // lb,),
        [col, col, pl.BlockSpec((8, SUBLANE, lb), lambda j: (0, 0, j))], [col, col],
        [jax.ShapeDtypeStruct((t_len, N_STATE), BF16)] * 2, [bur, bui, tab], jobs)


def _sgu_mix(v, ws_ref, lane_lo):
    rows = []
    for c0 in range(0, v.shape[0], CHUNK):
        slabs = []
        for j in range(SGU_W // LANE):
            prod = jnp.dot(ws_ref[j], v[c0:c0 + CHUNK, j * LANE:(j + 1) * LANE].astype(BF16),
                           preferred_element_type=F32)
            slabs.append(jnp.where(lane_lo, prod[:CHUNK], prod[CHUNK:]))
        rows.append(jnp.concatenate(slabs, axis=1))
    return jnp.concatenate(rows, axis=0) if len(rows) > 1 else rows[0]


def _fwd_mix(x, p, str_, sti, cre, cim, d_skip, w_glu, b_glu, w_pa, g_sgu, ws_st, bmat, w_pb, w_out,
             jobs=()):
    t_len = x.shape[0]

    def body(x_ref, p_ref, sr_ref, si_ref, cre_ref, cim_ref, dsk_ref, wg_ref, bg_ref, wpa_ref,
             gs_ref, ws_ref, bm_ref, wpb_ref, wo_ref,
             x2_ref, y0_ref, z_ref, mx_ref, ya_ref, yb_ref):
        u = p_ref[:, 0:SSM_W].astype(F32)
        y0 = jnp.concatenate(
            [_dot(sr_ref[:, i * DIAG_N:(i + 1) * DIAG_N],
                  cre_ref[i * DIAG_N:(i + 1) * DIAG_N, i * LANE:(i + 1) * LANE])
             - _dot(si_ref[:, i * DIAG_N:(i + 1) * DIAG_N],
                    cim_ref[i * DIAG_N:(i + 1) * DIAG_N, i * LANE:(i + 1) * LANE])
             for i in range(SSM_W // LANE)], axis=1) + dsk_ref[...] * u
        y0_ref[...] = y0.astype(BF16)
        y1 = _gelu(y0)
        z = _dot(y1, wg_ref[...]) + bg_ref[...]
        z_ref[...] = z.astype(BF16)
        ya_pre = (y1 * _sigmoid(z)).astype(BF16)
        ya = jnp.concatenate([jnp.dot(ya_pre, wpa_ref[k], preferred_element_type=F32)
                              for k in range(N_CHIP)], axis=1)
        ya_ref[...] = ya.astype(BF16)

        uvg = _gelu(p_ref[:, SSM_W:SSM_W + 2 * SGU_W].astype(F32))
        u2 = uvg[:, :SGU_W]
        _, vh = _rms_stats(uvg[:, SGU_W:])
        v3 = vh * gs_ref[...]
        lane_lo = lax.broadcasted_iota(jnp.int32, (CHUNK, LANE), 1) < SGU_D
        bias = jnp.concatenate([bm_ref[...]] * (TB // CHUNK), axis=0)
        mixed = _sgu_mix(v3, ws_ref, lane_lo) + bias
        mx_ref[...] = mixed.astype(BF16)
        sgu = (u2 * mixed).astype(BF16)
        yb = jnp.concatenate([jnp.dot(sgu, wpb_ref[k], preferred_element_type=F32)
                              for k in range(N_CHIP)], axis=1)
        yb_ref[...] = yb.astype(BF16)

        lg0 = SSM_W + 2 * SGU_W
        ga = _sigmoid(p_ref[:, lg0:lg0 + D_MODEL].astype(F32))
        gb = _sigmoid(p_ref[:, lg0 + D_MODEL:lg0 + 2 * D_MODEL].astype(F32))
        mrg = ga * ya + gb * yb
        x2_ref[...] = x_ref[...] + _dot(mrg, wo_ref[...])

    return _call(
        body, "fwd_mix", (t_len // TB,),
        [_rows(TB, D_MODEL), _rows(TB, IN_COLS), _rows(TB, N_STATE), _rows(TB, N_STATE)]
        + [_whole()] * 11,
        [_rows(TB, D_MODEL), _rows(TB, SSM_W), _rows(TB, SSM_W), _rows(TB, SGU_W),
         _rows(TB, D_MODEL), _rows(TB, D_MODEL)],
        [jax.ShapeDtypeStruct((t_len, D_MODEL), F32), jax.ShapeDtypeStruct((t_len, SSM_W), BF16),
         jax.ShapeDtypeStruct((t_len, SSM_W), BF16), jax.ShapeDtypeStruct((t_len, SGU_W), BF16),
         jax.ShapeDtypeStruct((t_len, D_MODEL), BF16), jax.ShapeDtypeStruct((t_len, D_MODEL), BF16)],
        [x, p, str_, sti, cre, cim, d_skip, w_glu, b_glu, w_pa, g_sgu, ws_st, bmat, w_pb, w_out], jobs)


def _conv_taps(v, cw_ref, c0, width):
    w0 = cw_ref[0:1, c0:c0 + width]
    w1 = cw_ref[1:2, c0:c0 + width]
    w2 = cw_ref[2:3, c0:c0 + width]
    return w0 * pltpu.roll(v, 2, 0) + w1 * pltpu.roll(v, 1, 0) + w2 * v


def _fwd_ffn(x2, target, g_ffn, w_up, conv_w, conv_b, w_down, g_final):
    t_len = x2.shape[0]
    half = D_FF // 2
    blocks_per_halo = TB // HALO

    def body(x2_ref, xp_ref, tg_ref, gf_ref, wu_ref, cw_ref, cb_ref, wd_ref, gl_ref,
             up_ref, act_ref, f_ref, h2_ref, dx3_ref, sm_ref):
        i = pl.program_id(0)
        xe = jnp.concatenate([xp_ref[...] * jnp.where(i == 0, 0.0, 1.0), x2_ref[...]], axis=0)
        _, xh = _rms_stats(xe)
        h2 = (xh * gf_ref[...]).astype(BF16)
        h2_ref[...] = h2[HALO:]
        acc = jnp.zeros((TB, D_MODEL), F32)
        ups = [jnp.dot(h2, wu_ref[k], preferred_element_type=F32) for k in range(N_CHIP)]
        for hc in range(2):
            ca = hc * half
            cb = D_FF + hc * half
            ua, ub = ups[hc], ups[2 + hc]
            up_ref[:, ca:ca + half] = ua[HALO:].astype(BF16)
            up_ref[:, cb:cb + half] = ub[HALO:].astype(BF16)
            ac = _conv_taps(ua, cw_ref, ca, half)[HALO:] + cb_ref[:, ca:ca + half]
            bc = _conv_taps(ub, cw_ref, cb, half)[HALO:] + cb_ref[:, cb:cb + half]
            act_ref[:, ca:ca + half] = ac.astype(BF16)
            act_ref[:, cb:cb + half] = bc.astype(BF16)
            f = (ac * _sigmoid(ac) * bc).astype(BF16)
            f_ref[:, ca:ca + half] = f
            acc = acc + jnp.dot(f, wd_ref[ca:ca + half, :], preferred_element_type=F32)
        x3 = x2_ref[...] + acc
        r3, xh3 = _rms_stats(x3)
        err = xh3 * gl_ref[...] - tg_ref[...]
        dout = err * (1.0 / D_MODEL)
        dx3_ref[...] = _rms_bwd(dout * gl_ref[...], xh3, r3)
        dgl = jnp.sum(dout * xh3, axis=0, keepdims=True)
        loss = 0.5 * jnp.sum(jnp.mean(err * err, axis=-1, keepdims=True), axis=0, keepdims=True)
        upd = jnp.concatenate([dgl, jnp.broadcast_to(loss, (1, D_MODEL)),
                               jnp.zeros((SUBLANE - 2, D_MODEL), F32)], axis=0)

        @pl.when(i == 0)
        def _():
            sm_ref[...] = upd

        @pl.when(i > 0)
        def _():
            sm_ref[...] += upd

    prev = pl.BlockSpec((HALO, D_MODEL), lambda i: (jnp.maximum(i * blocks_per_halo - 1, 0), 0))
    return _call(
        body, "fwd_ffn", (t_len // TB,),
        [_rows(TB, D_MODEL), prev, _rows(TB, D_MODEL)] + [_whole()] * 6,
        [_rows(TB, 2 * D_FF), _rows(TB, 2 * D_FF), _rows(TB, D_FF), _rows(TB, D_MODEL),
         _rows(TB, D_MODEL), _acc(SUBLANE, D_MODEL)],
        [jax.ShapeDtypeStruct((t_len, 2 * D_FF), BF16), jax.ShapeDtypeStruct((t_len, 2 * D_FF), BF16),
         jax.ShapeDtypeStruct((t_len, D_FF), BF16), jax.ShapeDtypeStruct((t_len, D_MODEL), BF16),
         jax.ShapeDtypeStruct((t_len, D_MODEL), F32), jax.ShapeDtypeStruct((SUBLANE, D_MODEL), F32)],
        [x2, x2, target, g_ffn, w_up, conv_w, conv_b, w_down, g_final])[0]


def _bwd_ffn(dx3, up, act, x2, g_ffn, w_up, conv_w, w_down, jobs=()):
    t_len = x2.shape[0]
    half = D_FF // 2
    nblk = t_len // TB
    halo_b = 2 * HALO
    n_e = TB + HALO

    def body(dx_ref, dxn_ref, up_ref, act_ref, actn_ref, x2_ref, gf_ref, wu_ref, cw_ref,
             wd_ref, dx2_ref, dup_ref, smw_ref, smg_ref):
        i = pl.program_id(0)
        keep_last = jnp.where(i == nblk - 1, 0.0, 1.0)
        dxe = jnp.concatenate([dx_ref[...], dxn_ref[...] * keep_last], axis=0).astype(BF16)
        dh2 = jnp.zeros((TB, D_MODEL), F32)
        zpad = jnp.zeros((1, half), F32)
        dfs = [lax.dot_general(dxe, wd_ref[hc * half:(hc + 1) * half, :], (((1,), (1,)), ((), ())),
                               preferred_element_type=F32) for hc in range(2)]
        for hc in range(2):
            ca = hc * half
            cb = D_FF + hc * half
            ac = jnp.concatenate([act_ref[:, ca:ca + half].astype(F32),
                                  actn_ref[:, ca:ca + half].astype(F32)[:HALO]], axis=0)
            bc = jnp.concatenate([act_ref[:, cb:cb + half].astype(F32),
                                  actn_ref[:, cb:cb + half].astype(F32)[:HALO]], axis=0)
            wa = [cw_ref[k:k + 1, ca:ca + half] for k in range(3)]
            wb = [cw_ref[k:k + 1, cb:cb + half] for k in range(3)]
            df = dfs[hc]
            sg = _sigmoid(ac)
            da = df * bc * sg * (1.0 + ac * (1.0 - sg))
            db = df * ac * sg
            da1, da2 = pltpu.roll(da, n_e - 1, 0), pltpu.roll(da, n_e - 2, 0)
            db1, db2 = pltpu.roll(db, n_e - 1, 0), pltpu.roll(db, n_e - 2, 0)
            dua = (wa[2] * da + wa[1] * da1 + wa[0] * da2)[:TB]
            dub = (wb[2] * db + wb[1] * db1 + wb[0] * db2)[:TB]
            dup_ref[:, ca:ca + half] = dua.astype(BF16)
            dup_ref[:, cb:cb + half] = dub.astype(BF16)
            dh2 = dh2 + _dot_nt(dua, wu_ref[hc]) + _dot_nt(dub, wu_ref[2 + hc])
            rows = []
            for u_, d0, d1, d2 in ((up_ref[:, ca:ca + half].astype(F32), da, da1, da2),
                                   (up_ref[:, cb:cb + half].astype(F32), db, db1, db2)):
                rows.append([jnp.sum(u_ * d2[:TB], axis=0, keepdims=True),
                             jnp.sum(u_ * d1[:TB], axis=0, keepdims=True),
                             jnp.sum(u_ * d0[:TB], axis=0, keepdims=True),
                             jnp.sum(d0[:TB], axis=0, keepdims=True)])
            for c0, rws in ((ca, rows[0]), (cb, rows[1])):
                upd = jnp.concatenate(rws + [zpad] * (SUBLANE - 4), axis=0)

                @pl.when(i == 0)
                def _(upd=upd, c0=c0):
                    smw_ref[:, c0:c0 + half] = upd

                @pl.when(i > 0)
                def _(upd=upd, c0=c0):
                    smw_ref[:, c0:c0 + half] += upd

        r2, xh2 = _rms_stats(x2_ref[...])
        dx2_ref[...] = dx_ref[...] + _rms_bwd(dh2 * gf_ref[...], xh2, r2)
        updg = jnp.concatenate([jnp.sum(dh2 * xh2, axis=0, keepdims=True),
                                jnp.zeros((SUBLANE - 1, D_MODEL), F32)], axis=0)

        @pl.when(i == 0)
        def _():
            smg_ref[...] = updg

        @pl.when(i > 0)
        def _():
            smg_ref[...] += updg

    nxt_d = pl.BlockSpec((HALO, D_MODEL),
                         lambda i: (jnp.minimum((i + 1) * (TB // HALO), t_len // HALO - 1), 0))
    nxt_a = pl.BlockSpec((halo_b, 2 * D_FF),
                         lambda i: (jnp.minimum((i + 1) * (TB // halo_b), t_len // halo_b - 1), 0))
    return _call(
        body, "bwd_ffn", (nblk,),
        [_rows(TB, D_MODEL), nxt_d, _rows(TB, 2 * D_FF), _rows(TB, 2 * D_FF), nxt_a,
         _rows(TB, D_MODEL)] + [_whole()] * 4,
        [_rows(TB, D_MODEL), _rows(TB, 2 * D_FF), _acc(SUBLANE, 2 * D_FF), _acc(SUBLANE, D_MODEL)],
        [jax.ShapeDtypeStruct((t_len, D_MODEL), F32), jax.ShapeDtypeStruct((t_len, 2 * D_FF), BF16),
         jax.ShapeDtypeStruct((SUBLANE, 2 * D_FF), F32), jax.ShapeDtypeStruct((SUBLANE, D_MODEL), F32)],
        [dx3, dx3, up, act, act, x2, g_ffn, w_up, conv_w, w_down], jobs)


def _bwd_mix(dx2, p, y0, z, mixed, ya, yb, w_out, w_pa, w_pb, w_glu, cre, cim, ws_st, wst_st,
             d_skip, g_sgu, jobs=()):
    t_len = dx2.shape[0]
    pc = D_MODEL // N_CHIP
    n_slab = SGU_W // LANE

    def body(dx_ref, p_ref, y0_ref, z_ref, mx_ref, ya_ref, yb_ref, wo_ref, wpa_ref, wpb_ref,
             wg_ref, cre_ref, cim_ref, ws_ref, wst_ref, dsk_ref, gs_ref,
             dsr_ref, dsi_ref, du_ref, drest_ref, mrg_ref, dya_ref, dyb_ref, yap_ref, dz_ref,
             y1_ref, sgu_ref, dy0_ref, sm_ref, dbm_ref, dws_ref):
        i = pl.program_id(0)
        first = i == 0
        lg0 = SSM_W + 2 * SGU_W
        ga = _sigmoid(p_ref[:, lg0:lg0 + D_MODEL].astype(F32))
        gb = _sigmoid(p_ref[:, lg0 + D_MODEL:lg0 + 2 * D_MODEL].astype(F32))
        yav = ya_ref[...].astype(F32)
        ybv = yb_ref[...].astype(F32)
        mrg_ref[...] = (ga * yav + gb * ybv).astype(BF16)
        y0v = y0_ref[...].astype(F32)
        y1, y1_grad = _gelu_and_grad(y0v)
        sz = _sigmoid(z_ref[...].astype(F32))
        y1_ref[...] = y1.astype(BF16)
        yap_ref[...] = (y1 * sz).astype(BF16)

        dmrg = _dot_nt(dx_ref[...], wo_ref[...])
        drest_ref[:, 2 * SGU_W:2 * SGU_W + D_MODEL] = (dmrg * yav * ga * (1.0 - ga)).astype(BF16)
        drest_ref[:, 2 * SGU_W + D_MODEL:] = (dmrg * ybv * gb * (1.0 - gb)).astype(BF16)
        dya = (dmrg * ga).astype(BF16)
        dyb = (dmrg * gb).astype(BF16)
        dya_ref[...] = dya
        dyb_ref[...] = dyb

        dyap = jnp.zeros((TB, SSM_W), F32)
        for k in range(N_CHIP):
            dyap = dyap + _dot_nt(dya[:, k * pc:(k + 1) * pc], wpa_ref[k])
        dz = dyap * y1 * sz * (1.0 - sz)
        dz_ref[...] = dz.astype(BF16)
        dy0 = (dyap * sz + _dot_nt(dz, wg_ref[...])) * y1_grad
        dy0_ref[...] = dy0.astype(BF16)
        u = p_ref[:, 0:SSM_W].astype(F32)
        du_ref[...] = dy0 * dsk_ref[...]
        for q in range(SSM_W // LANE):
            rows, cols = slice(q * DIAG_N, (q + 1) * DIAG_N), slice(q * LANE, (q + 1) * LANE)
            dsr_ref[:, rows] = _dot_nt(dy0[:, cols], cre_ref[rows, cols]).astype(BF16)
            dsi_ref[:, rows] = (-_dot_nt(dy0[:, cols], cim_ref[rows, cols])).astype(BF16)

        uv = p_ref[:, SSM_W:lg0].astype(F32)
        uvg, gg = _gelu_and_grad(uv)
        u2 = uvg[:, :SGU_W]
        rv, vh = _rms_stats(uvg[:, SGU_W:])
        v3 = vh * gs_ref[...]
        mixed = mx_ref[...].astype(F32)
        dsgu = jnp.zeros((TB, SGU_W), F32)
        for k in range(N_CHIP):
            dsgu = dsgu + _dot_nt(dyb[:, k * pc:(k + 1) * pc], wpb_ref[k])
        sgu_ref[...] = (u2 * mixed).astype(BF16)
        drest_ref[:, 0:SGU_W] = (dsgu * mixed * gg[:, :SGU_W]).astype(BF16)
        dmix = dsgu * u2
        lane_lo = lax.broadcasted_iota(jnp.int32, (CHUNK, LANE), 1) < SGU_D
        dv3 = _sgu_mix(dmix, wst_ref, lane_lo)
        dbm = jnp.zeros((CHUNK, SGU_W), F32)
        for c0 in range(0, TB, CHUNK):
            dbm = dbm + dmix[c0:c0 + CHUNK]
        for j in range(n_slab):
            lo = jnp.zeros((CHUNK, CHUNK), F32)
            hi = jnp.zeros((CHUNK, CHUNK), F32)
            for c0 in range(0, TB, CHUNK):
                dsl = dmix[c0:c0 + CHUNK, j * LANE:(j + 1) * LANE]
                vsl = v3[c0:c0 + CHUNK, j * LANE:(j + 1) * LANE]
                lo = lo + _dot_nt(jnp.where(lane_lo, dsl, 0.0), vsl)
                hi = hi + _dot_nt(jnp.where(lane_lo, 0.0, dsl), vsl)

            @pl.when(first)
            def _(lo=lo, hi=hi, j=j):
                dws_ref[2 * j] = lo
                dws_ref[2 * j + 1] = hi

            @pl.when(jnp.logical_not(first))
            def _(lo=lo, hi=hi, j=j):
                dws_ref[2 * j] += lo
                dws_ref[2 * j + 1] += hi

        dv2 = _rms_bwd(dv3 * gs_ref[...], vh, rv)
        drest_ref[:, SGU_W:2 * SGU_W] = (dv2 * gg[:, SGU_W:]).astype(BF16)

        upd = jnp.concatenate([jnp.sum(dy0 * u, axis=0, keepdims=True),
                               jnp.sum(dz, axis=0, keepdims=True),
                               jnp.sum(dv3 * vh, axis=0, keepdims=True),
                               jnp.zeros((SUBLANE - 3, SSM_W), F32)], axis=0)

        @pl.when(first)
        def _():
            sm_ref[...] = upd
            dbm_ref[...] = dbm

        @pl.when(jnp.logical_not(first))
        def _():
            sm_ref[...] += upd
            dbm_ref[...] += dbm

    rest = 2 * SGU_W + 2 * D_MODEL
    bf_d, bf_s = jax.ShapeDtypeStruct((t_len, D_MODEL), BF16), jax.ShapeDtypeStruct((t_len, SSM_W), BF16)
    return _call(
        body, "bwd_mix", (t_len // TB,),
        [_rows(TB, D_MODEL), _rows(TB, IN_COLS), _rows(TB, SSM_W), _rows(TB, SSM_W),
         _rows(TB, SGU_W), _rows(TB, D_MODEL), _rows(TB, D_MODEL)] + [_whole()] * 10,
        [_rows(TB, N_STATE), _rows(TB, N_STATE), _rows(TB, SSM_W), _rows(TB, rest),
         _rows(TB, D_MODEL), _rows(TB, D_MODEL), _rows(TB, D_MODEL), _rows(TB, SSM_W),
         _rows(TB, SSM_W), _rows(TB, SSM_W), _rows(TB, SGU_W), _rows(TB, SSM_W),
         _acc(SUBLANE, SSM_W), _acc(CHUNK, SGU_W),
         pl.BlockSpec((SGU_G, CHUNK, CHUNK), lambda i: (0, 0, 0))],
        [jax.ShapeDtypeStruct((t_len, N_STATE), BF16), jax.ShapeDtypeStruct((t_len, N_STATE), BF16),
         jax.ShapeDtypeStruct((t_len, SSM_W), F32), jax.ShapeDtypeStruct((t_len, rest), BF16),
         bf_d, bf_d, bf_d, bf_s, bf_s, bf_s, bf_s, bf_s,
         jax.ShapeDtypeStruct((SUBLANE, SSM_W), F32), jax.ShapeDtypeStruct((CHUNK, SGU_W), F32),
         jax.ShapeDtypeStruct((SGU_G, CHUNK, CHUNK), F32)],
        [dx2, p, y0, z, mixed, ya, yb, w_out, w_pa, w_pb, w_glu, cre, cim, ws_st, wst_st, d_skip,
         g_sgu], jobs)


def _scan_bwd(dsr, dsi, str_, sti, tab_rev, jobs=()):
    t_len = dsr.shape[0]
    nblk = t_len // SUBLANE
    lb = SCAN_LANES

    def body(dr_ref, di_ref, sr_ref, si_ref, tab_ref, lr_ref, li_ref, dar_ref, dai_ref):
        tab_v = [tab_ref[q] for q in range(8)]
        row0 = lax.broadcasted_iota(jnp.int32, (SUBLANE, lb), 0) == 0
        tile = BF16_TILE

        def step(k, carry):
            cr, ci, acr, aci = carry
            base = pl.multiple_of((nblk - (k + 1) * SCAN_UNROLL) * SUBLANE, SCAN_UNROLL * SUBLANE)
            state = _load_blocks(sr_ref, si_ref, base)
            before = pl.ds(pl.multiple_of(jnp.maximum(base - tile, 0), tile), tile)
            has_before = jnp.where(base > 0, 1.0, 0.0)
            prev = (sr_ref[before, :].astype(F32)[tile - 1:tile] * has_before,
                    si_ref[before, :].astype(F32)[tile - 1:tile] * has_before)
            local = [_scan_local(xr, xi, tab_v, (7, 6, 4))
                     for xr, xi in _load_blocks(dr_ref, di_ref, base)]
            lam = [None] * SCAN_UNROLL
            for b in reversed(range(SCAN_UNROLL)):
                xr, xi = _scan_carry(*local[b], tab_v, cr, ci)
                lam[b] = (xr, xi)
                cr, ci = xr[0:1, :], xi[0:1, :]
                pr, pi = prev if b == 0 else (state[b - 1][0][SUBLANE - 1:], state[b - 1][1][SUBLANE - 1:])
                s_r = jnp.where(row0, pr, pltpu.roll(state[b][0], 1, 0))
                s_i = jnp.where(row0, pi, pltpu.roll(state[b][1], 1, 0))
                acr = acr + xr * s_r + xi * s_i
                aci = aci + xi * s_r - xr * s_i
            _store_blocks(lr_ref, li_ref, base, lam)
            return cr, ci, acr, aci

        zero = jnp.zeros((1, lb), F32)
        zacc = jnp.zeros((SUBLANE, lb), F32)
        _, _, acr, aci = lax.fori_loop(0, nblk // SCAN_UNROLL, step, (zero, zero, zacc, zacc))
        dar_ref[...] = acr
        dai_ref[...] = aci

    col = pl.BlockSpec((t_len, lb), lambda j: (0, j))
    small = pl.BlockSpec((SUBLANE, lb), lambda j: (0, j))
    return _call(
        body, "scan_bwd", (N_STATE // lb,),
        [col, col, col, col, pl.BlockSpec((8, SUBLANE, lb), lambda j: (0, 0, j))],
        [col, col, small, small],
        [jax.ShapeDtypeStruct((t_len, N_STATE), BF16)] * 2
        + [jax.ShapeDtypeStruct((SUBLANE, N_STATE), F32)] * 2,
        [dsr, dsi, str_, sti, tab_rev], jobs)


def _bwd_in(lam_r, lam_i, du_part, drest, x, dx2, g_mix, w_in, bre, bim, jobs=()):
    t_len = x.shape[0]
    cs = IN_COLS // N_CHIP

    def body(lr_ref, li_ref, du_ref, dr_ref, x_ref, dx2_ref, g_ref, w_ref, bre_ref, bim_ref,
             gx_ref, dp_ref, sm_ref):
        i = pl.program_id(0)
        du = du_ref[...] + jnp.concatenate(
            [_dot_nt(lr_ref[:, i * DIAG_N:(i + 1) * DIAG_N],
                     bre_ref[i * LANE:(i + 1) * LANE, i * DIAG_N:(i + 1) * DIAG_N])
             + _dot_nt(li_ref[:, i * DIAG_N:(i + 1) * DIAG_N],
                       bim_ref[i * LANE:(i + 1) * LANE, i * DIAG_N:(i + 1) * DIAG_N])
             for i in range(SSM_W // LANE)], axis=1)
        dp_ref[:, 0:SSM_W] = du.astype(BF16)
        dp_ref[:, SSM_W:] = dr_ref[...]
        dh = jnp.zeros((TB, D_MODEL), F32)
        for k in range(N_CHIP):
            dh = dh + _dot_nt(dp_ref[:, k * cs:(k + 1) * cs], w_ref[k])
        r, xh = _rms_stats(x_ref[...])
        gx_ref[...] = dx2_ref[...] + _rms_bwd(dh * g_ref[...], xh, r)
        upd = jnp.concatenate([jnp.sum(dh * xh, axis=0, keepdims=True),
                               jnp.zeros((SUBLANE - 1, D_MODEL), F32)], axis=0)

        @pl.when(i == 0)
        def _():
            sm_ref[...] = upd

        @pl.when(i > 0)
        def _():
            sm_ref[...] += upd

    return _call(
        body, "bwd_in", (t_len // TB,),
        [_rows(TB, N_STATE), _rows(TB, N_STATE), _rows(TB, SSM_W), _rows(TB, IN_COLS - SSM_W),
         _rows(TB, D_MODEL), _rows(TB, D_MODEL)] + [_whole()] * 4,
        [_rows(TB, D_MODEL), _rows(TB, IN_COLS), _acc(SUBLANE, D_MODEL)],
        [jax.ShapeDtypeStruct((t_len, D_MODEL), F32), jax.ShapeDtypeStruct((t_len, IN_COLS), BF16),
         jax.ShapeDtypeStruct((SUBLANE, D_MODEL), F32)],
        [lam_r, lam_i, du_part, drest, x, dx2, g_mix, w_in, bre, bim], jobs)


def _matmul_tn(a, b, name, out_shape, grid_ij, a_blk, a_map, b_blk, b_map, o_blk, o_map, jobs=()):
    tk = a_blk[0]
    nk = a.shape[0] // tk
    assert nk * tk == a.shape[0] and nk > 0

    def body(a_ref, b_ref, o_ref, acc_ref):
        k = pl.program_id(2)

        @pl.when(k == 0)
        def _():
            acc_ref[...] = jnp.zeros_like(acc_ref)

        acc_ref[...] += lax.dot_general(a_ref[...].astype(BF16), b_ref[...].astype(BF16),
                                        (((0,), (0,)), ((), ())), preferred_element_type=F32)

        @pl.when(k == nk - 1)
        def _():
            o_ref[...] = acc_ref[...]

    outs, per_job = _call(
        body, name, (grid_ij[0], grid_ij[1], nk),
        [pl.BlockSpec(a_blk, a_map), pl.BlockSpec(b_blk, b_map)], [pl.BlockSpec(o_blk, o_map)],
        [jax.ShapeDtypeStruct(out_shape, F32)], [a, b], jobs,
        scratch=[pltpu.VMEM((a_blk[1], b_blk[1]), F32)])
    return outs[0], per_job


def _dw_rows(a, b, name, tm, tk):
    m, n = a.shape[1], b.shape[1]
    tk = min(tk, a.shape[0])
    return _matmul_tn(a, b, name, (m, n), (m // tm, 1),
                      (tk, tm), lambda i, j, k: (k, i), (tk, n), lambda i, j, k: (k, 0),
                      (tm, n), lambda i, j, k: (i, 0))[0]


def _dw_cols(a, b, name, tn, sharded, jobs=()):
    t_len, m = a.shape
    n = b.shape[1]

    def body(a_ref, b_ref, o_ref):
        o_ref[...] = lax.dot_general(a_ref[...].astype(BF16), b_ref[...].astype(BF16),
                                     (((0,), (0,)), ((), ())), preferred_element_type=F32)

    if sharded:
        o_spec, o_shape = pl.BlockSpec((None, m, tn), lambda j: (j, 0, 0)), (n // tn, m, tn)
    else:
        o_spec, o_shape = pl.BlockSpec((m, tn), lambda j: (0, j)), (m, n)
    outs, per_job = _call(body, name, (n // tn,),
                          [_whole(), pl.BlockSpec((t_len, tn), lambda j: (0, j))], [o_spec],
                          [jax.ShapeDtypeStruct(o_shape, F32)], [a, b], jobs)
    return outs[0], per_job


def _dw_tiles(a, b, name, tm, tn, jobs=()):
    t_len, m = a.shape
    n = b.shape[1]

    def body(a_ref, b_ref, o_ref):
        o_ref[...] = lax.dot_general(a_ref[...].astype(BF16), b_ref[...].astype(BF16),
                                     (((0,), (0,)), ((), ())), preferred_element_type=F32)

    outs, per_job = _call(body, name, (n // tn, m // tm),
                          [pl.BlockSpec((t_len, tm), lambda j, i: (0, i)),
                           pl.BlockSpec((t_len, tn), lambda j, i: (0, j))],
                          [pl.BlockSpec((None, tm, tn), lambda j, i: (j, i, 0))],
                          [jax.ShapeDtypeStruct((n // tn, m, tn), F32)], [a, b], jobs)
    return outs[0], per_job


def _dw_pair(a, m, b1, b2, name, jobs=()):
    t_len = a.shape[0]
    n_slab = DIAG_N // LANE
    rows_per_slab = LANE // n_slab

    def body(a_ref, b1_ref, b2_ref, o1_ref, o2_ref):
        for b_ref, o_ref in ((b1_ref, o1_ref), (b2_ref, o2_ref)):
            prod = lax.dot_general(a_ref[...].astype(BF16), b_ref[...].astype(BF16),
                                   (((0,), (0,)), ((), ())), preferred_element_type=F32)
            for j in range(n_slab):
                rows = slice(j * rows_per_slab, (j + 1) * rows_per_slab)
                o_ref[rows, :] = prod[rows, j * LANE:(j + 1) * LANE]

    tok = pl.BlockSpec((t_len, DIAG_N), lambda i: (0, i))
    out = pl.BlockSpec((LANE, LANE), lambda i: (i, 0))
    return _call(body, name, (m // LANE,),
                 [pl.BlockSpec((t_len, LANE), lambda i: (0, i)), tok, tok], [out, out],
                 [jax.ShapeDtypeStruct((m, LANE), F32)] * 2, [a, b1, b2], jobs)


def _prefetch_call(body, name, grid, scalars, in_specs, out_specs, out_shape, args):
    return pl.pallas_call(
        body, name=name,
        grid_spec=pltpu.PrefetchScalarGridSpec(num_scalar_prefetch=1, grid=grid, in_specs=in_specs,
                                               out_specs=out_specs),
        out_shape=out_shape, compiler_params=_params(len(grid)),
    )(scalars, *args)


def _place_shard(w, where, name, dtype, tr):
    rows, cols = w.shape

    def body(s_ref, w_ref, o_ref):
        o_ref[...] = w_ref[...].astype(dtype)

    return _prefetch_call(
        body, name, (rows // tr,), where,
        [pl.BlockSpec((tr, cols), lambda i, s: (i, 0))],
        pl.BlockSpec((None, tr, cols), lambda i, s: (s[0], i, 0)),
        jax.ShapeDtypeStruct((N_CHIP, rows, cols), dtype), [w])


def _place_shards(ws, where, name, dtype):
    n = len(ws)

    def body(s_ref, *refs):
        for t in range(n):
            refs[n + t][...] = refs[t][...].astype(dtype)

    return _prefetch_call(
        body, name, (1,), where,
        [pl.BlockSpec(w.shape, lambda i, s: (0, 0)) for w in ws],
        [pl.BlockSpec((None,) + w.shape, lambda i, s: (s[0], 0, 0)) for w in ws],
        [jax.ShapeDtypeStruct((N_CHIP,) + w.shape, dtype) for w in ws], ws)


def _add_sibling(gs, gots, where, name):
    n = len(gs)
    halves = [(g.shape[1] // 2, g.shape[2]) for g in gs]

    def body(s_ref, *refs):
        for t in range(n):
            refs[2 * n + t][...] = (refs[t][...] + refs[n + t][...]).astype(BF16)

    return _prefetch_call(
        body, name, (N_CHIP,), where,
        [pl.BlockSpec((None, hr, cs), lambda k, s: (k, s[1], 0)) for hr, cs in halves]
        + [pl.BlockSpec((None, hr, cs), lambda k, s: (k, 0, 0)) for hr, cs in halves],
        [pl.BlockSpec((None, hr, cs), lambda k, s: (k, 0, 0)) for hr, cs in halves],
        [jax.ShapeDtypeStruct((N_CHIP, hr, cs), BF16) for hr, cs in halves], list(gs) + list(gots))


def _add_chips(sums, gots, where, name):
    n = len(sums)
    halves = [s.shape[1:] for s in sums]

    def body(s_ref, *refs):
        for t in range(n):
            own_ref, got_ref = refs[t], refs[n + t]
            refs[2 * n + t][...] = ((own_ref[...].astype(F32) + got_ref[0].astype(F32))
                                    + got_ref[1].astype(F32)) + got_ref[2].astype(F32)

    return _prefetch_call(
        body, name, (1,), where,
        [pl.BlockSpec((None, hr, cs), lambda i, s: (s[0], 0, 0)) for hr, cs in halves]
        + [pl.BlockSpec((3, hr, cs), lambda i, s: (0, 0, 0)) for hr, cs in halves],
        [pl.BlockSpec((hr, cs), lambda i, s: (s[1], 0)) for hr, cs in halves],
        [jax.ShapeDtypeStruct((2 * hr, cs), F32) for hr, cs in halves], list(sums) + list(gots))


def _small_allreduce(pack):
    rows = pack.shape[0]
    half = rows // 2

    def body(in_ref, out_ref, sib_ref, slots_ref, s_a, r_a, s_b, r_b, s_c, r_c):
        x, y, c, chips = _place()
        k_me = 2 * x + y
        sib = (x, y, 1 - c)
        first = _remote(in_ref, sib_ref, s_a, r_a, sib)
        first.start()
        first.wait()
        mine = _half(rows, c)
        slots_ref[k_me] = in_ref[mine, :] + sib_ref[mine, :]
        cps = [_remote(slots_ref.at[k_me], slots_ref.at[k_me], s_b.at[j], r_b.at[j], (*ch, c))
               for j, ch in enumerate(chips)]
        for cp in cps:
            cp.start()
        for j, ch in enumerate(chips):
            slot = slots_ref.at[_chip_index(ch)]
            _remote(slot, slot, s_b.at[j], r_b.at[j], (*ch, c)).wait_recv()
        for cp in cps:
            cp.wait_send()
        out_ref[mine, :] = ((slots_ref[0] + slots_ref[1]) + slots_ref[2]) + slots_ref[3]
        last = _remote(out_ref.at[mine, :], out_ref.at[mine, :], s_c, r_c, sib)
        last.start()
        theirs = out_ref.at[_half(rows, 1 - c), :]
        _remote(theirs, theirs, s_c, r_c, sib).wait_recv()
        last.wait_send()

    return pl.pallas_call(
        body, name="small_allreduce", in_specs=[_whole()], out_specs=_whole(),
        out_shape=jax.ShapeDtypeStruct(pack.shape, F32),
        scratch_shapes=[pltpu.VMEM(pack.shape, F32), pltpu.VMEM((N_CHIP, half, LANE), F32),
                        pltpu.SemaphoreType.DMA, pltpu.SemaphoreType.DMA,
                        pltpu.SemaphoreType.DMA((3,)), pltpu.SemaphoreType.DMA((3,)),
                        pltpu.SemaphoreType.DMA, pltpu.SemaphoreType.DMA],
        compiler_params=_params(0),
    )(pack)


def _adamw_update(w_ref, g_ref, m_ref, v_ref, d_ref, mo_ref, vo_ref):
    gv = g_ref[...]
    mn = ADAM_B1 * m_ref[...] + (1.0 - ADAM_B1) * gv
    vn = ADAM_B2 * v_ref[...] + (1.0 - ADAM_B2) * (gv * gv)
    mo_ref[...] = mn
    vo_ref[...] = vn
    m_hat = mn / (1.0 - ADAM_B1 ** ADAM_STEP)
    v_hat = vn / (1.0 - ADAM_B2 ** ADAM_STEP)
    d_ref[...] = -ADAM_LR * (m_hat / (jnp.sqrt(v_hat) + ADAM_EPS) + ADAM_WD * w_ref[...])


def _adamw(w, g, m, v, name, tr):
    rows, cols = w.shape
    blk = _rows(tr, cols)

    def body(w_ref, g_ref, m_ref, v_ref, go_ref, d_ref, mo_ref, vo_ref):
        go_ref[...] = g_ref[...]
        _adamw_update(w_ref, g_ref, m_ref, v_ref, d_ref, mo_ref, vo_ref)

    return _call(body, name, (rows // tr,), [blk] * 4, [blk] * 4,
                 [jax.ShapeDtypeStruct(w.shape, F32)] * 4, [w, g, m, v])[0]


def _adamw_many(ws, gs, ms, vs, name):
    n = len(ws)

    def body(*refs):
        for t in range(n):
            _adamw_update(*[refs[q * n + t] for q in range(7)])

    specs = [pl.BlockSpec(a.shape, lambda i, nd=a.ndim: (0,) * nd) for a in ws]
    outs = pl.pallas_call(
        body, name=name, grid=(1,), in_specs=specs * 4, out_specs=specs * 3,
        out_shape=[jax.ShapeDtypeStruct(a.shape, F32) for _ in range(3) for a in ws],
        compiler_params=_params(1),
    )(*ws, *gs, *ms, *vs)
    return outs[:n], outs[n:2 * n], outs[2 * n:]


def _ssm_discretize(a_re, a_im, log_dt, b_re, b_im):
    dt = jnp.exp(log_dt)[:, None]
    mag = jnp.exp(dt * a_re)
    abr = mag * jnp.cos(dt * a_im)
    abi = mag * jnp.sin(dt * a_im)
    den = a_re * a_re + a_im * a_im
    nr = abr - 1.0
    ni = abi
    f_re = (nr * a_re + ni * a_im) / den
    f_im = (ni * a_re - nr * a_im) / den
    bbr = f_re[..., None] * b_re - f_im[..., None] * b_im
    bbi = f_re[..., None] * b_im + f_im[..., None] * b_re
    return abr, abi, bbr, bbi


def _scan_tables(abr, abi):
    ar = abr.reshape(1, N_STATE)
    ai = abi.reshape(1, N_STATE)
    pr, pi = [ar], [ai]
    for _ in range(SUBLANE - 1):
        pr, pi = pr + [pr[-1] * ar - pi[-1] * ai], pi + [pr[-1] * ai + pi[-1] * ar]
    row = jnp.arange(SUBLANE)[:, None]
    tabs = []
    for d in (1, 2, 4):
        tabs.append(jnp.where(row >= d, pr[d - 1], 0.0))
        tabs.append(jnp.where(row >= d, pi[d - 1], 0.0))
    tabs.append(jnp.concatenate(pr, axis=0))
    tabs.append(jnp.concatenate(pi, axis=0))
    fwd = jnp.stack(tabs)
    sign = jnp.array([1.0, -1.0] * 4, F32)[:, None, None]
    return fwd, fwd[:, ::-1, :] * sign


def _block_diag_b(bb):
    strip = bb.transpose(2, 0, 1).reshape(SSM_H, N_STATE)
    rows = lax.broadcasted_iota(jnp.int32, (SSM_W, N_STATE), 0) // SSM_H
    cols = lax.broadcasted_iota(jnp.int32, (SSM_W, N_STATE), 1) // SSM_P
    return jnp.where(rows == cols, jnp.tile(strip, (SSM_G, 1)), 0.0).astype(BF16)


def _block_diag_c(cc):
    strip = cc.transpose(0, 2, 1).reshape(N_STATE, SSM_H)
    rows = lax.broadcasted_iota(jnp.int32, (N_STATE, SSM_W), 0) // SSM_P
    cols = lax.broadcasted_iota(jnp.int32, (N_STATE, SSM_W), 1) // SSM_H
    return jnp.where(rows == cols, jnp.tile(strip, (1, SSM_G)), 0.0).astype(BF16)


SMALL_SHAPES = {
    "g_mix": (D_MODEL,), "a_re": (SSM_G, SSM_P), "a_im": (SSM_G, SSM_P), "log_dt": (SSM_G,),
    "b_re": (SSM_G, SSM_P, SSM_H), "b_im": (SSM_G, SSM_P, SSM_H),
    "c_re": (SSM_G, SSM_H, SSM_P), "c_im": (SSM_G, SSM_H, SSM_P),
    "d_skip": (SSM_W,), "b_glu": (SSM_W,), "g_sgu": (SGU_W,), "w_s": (SGU_G, CHUNK, CHUNK),
    "b_s": (SGU_G, CHUNK), "g_ffn": (D_MODEL,), "conv_b": (2 * D_FF,), "g_final": (D_MODEL,),
}
PACK_ITEMS = [("loss", (1,))] + [(n, SMALL_SHAPES[n]) for n in SMALL] + [("conv_w", (3, 2 * D_FF))]
TILE = SUBLANE * LANE


def _item_rows(shape):
    return -(-math.prod(shape) // TILE) * SUBLANE


PACK_ROWS = -(-sum(_item_rows(s) for _, s in PACK_ITEMS) // (2 * SUBLANE)) * (2 * SUBLANE)


def _pack(values):
    parts, used = [], 0
    for name, shape in PACK_ITEMS:
        size, rows = math.prod(shape), _item_rows(shape)
        if name in values:
            flat = values[name].astype(F32).reshape(size)
            if rows * LANE > size:
                flat = jnp.pad(flat, (0, rows * LANE - size))
            parts.append(flat.reshape(rows, LANE))
        else:
            parts.append(jnp.zeros((rows, LANE), F32))
        used += rows
    if PACK_ROWS > used:
        parts.append(jnp.zeros((PACK_ROWS - used, LANE), F32))
    return jnp.concatenate(parts, axis=0)


def _unpack(pack):
    out, off = {}, 0
    for name, shape in PACK_ITEMS:
        rows = _item_rows(shape)
        out[name] = pack[off:off + rows].reshape(rows * LANE)[:math.prod(shape)].reshape(shape)
        off += rows
    return out


PLACE_ROWS = {"w_in": 256, "w_up": 256, "w_down": 352, "w_out": 256, "w_proj_a": 256,
              "w_proj_b": 256, "w_glu": 128}


def kernel(x, g_mix, w_in, a_re, a_im, log_dt, b_re, b_im, c_re, c_im, d_skip, w_glu, b_glu, w_proj_a, g_sgu, w_s, b_s, w_proj_b, w_out, g_ffn, w_up, conv_w, conv_b, w_down, g_final, loss_target, m_g_mix, m_w_in, m_a_re, m_a_im, m_log_dt, m_b_re, m_b_im, m_c_re, m_c_im, m_d_skip, m_w_glu, m_b_glu, m_w_proj_a, m_g_sgu, m_w_s, m_b_s, m_w_proj_b, m_w_out, m_g_ffn, m_w_up, m_conv_w, m_conv_b, m_w_down, m_g_final, v_g_mix, v_w_in, v_a_re, v_a_im, v_log_dt, v_b_re, v_b_im, v_c_re, v_c_im, v_d_skip, v_w_glu, v_b_glu, v_w_proj_a, v_g_sgu, v_w_s, v_b_s, v_w_proj_b, v_w_out, v_g_ffn, v_w_up, v_conv_w, v_conv_b, v_w_down, v_g_final):
    given = dict(locals())
    w = {n: given[n] for n in WEIGHTS}
    m = {n: given["m_" + n] for n in WEIGHTS}
    v = {n: given["v_" + n] for n in WEIGHTS}

    def shard2d(a):
        return a.reshape(a.shape[-2], a.shape[-1])

    chip = 2 * lax.axis_index("x") + lax.axis_index("y")
    where = jnp.stack([chip, lax.axis_index("c")]).astype(jnp.int32)
    xs, target = x[0], loss_target[0]
    small = {n: w[n].reshape(SMALL_SHAPES[n]) for n in SMALL}

    (abr, abi, bbr, bbi), disc_vjp = jax.vjp(_ssm_discretize, small["a_re"], small["a_im"],
                                             small["log_dt"], small["b_re"], small["b_im"])
    tab_f, tab_r = _scan_tables(abr, abi)
    bre = _block_diag_b(bbr)
    bim = _block_diag_b(bbi)
    cre = _block_diag_c(small["c_re"])
    cim = _block_diag_c(small["c_im"])
    tril = jnp.tril(jnp.ones((CHUNK, CHUNK), dtype=bool))
    ws = jnp.where(tril[None], small["w_s"], 0.0)
    ws_st = ws.reshape(SGU_G // 2, 2 * CHUNK, CHUNK).astype(BF16)
    wst_st = ws.transpose(0, 2, 1).reshape(SGU_G // 2, 2 * CHUNK, CHUNK).astype(BF16)
    bmat = jnp.repeat(small["b_s"].T, SGU_D, axis=1)
    g_mix2 = small["g_mix"].reshape(1, D_MODEL)
    g_ffn2 = small["g_ffn"].reshape(1, D_MODEL)
    g_final2 = small["g_final"].reshape(1, D_MODEL)
    g_sgu2 = small["g_sgu"].reshape(1, SGU_W)
    d_skip2 = small["d_skip"].reshape(1, SSM_W)
    b_glu2 = small["b_glu"].reshape(1, SSM_W)
    conv_b2 = small["conv_b"].reshape(1, 2 * D_FF)

    gat = {"w_in": _place_shard(shard2d(w["w_in"]), where, "place_w_in", BF16, PLACE_ROWS["w_in"])}
    gat.update(zip(BIG[1:], _place_shards([shard2d(w[n]) for n in BIG[1:]], where, "place_rest", BF16)))
    gat["conv_w"] = _place_shard(shard2d(w["conv_w"]), where, "place_conv_w", F32, 3)
    all_rows = (0, D_MODEL)
    (gat["w_in"],), = _comm("gather_in", [_job_gather(
        [gat["w_in"]], [(0, all_rows, ICI, (0.0, 0.5)), (0, all_rows, SIBLING, (0.5, 1.0))])])
    mixers = ["w_glu", "w_proj_a", "w_proj_b", "w_out"]
    rows = {n: (0, gat[n].shape[1]) for n in mixers}
    down_a, down_b = (0, D_FF // 8), (D_FF // 8, D_FF // 8)
    up_a, up_b = (0, 3 * D_MODEL // 8), (3 * D_MODEL // 8, 5 * D_MODEL // 8)
    span = (0.0, 1.0)

    names = mixers + ["conv_w", "w_down"]
    (p, h1, bur, bui), (got,) = _fwd_in(
        xs, g_mix2, gat["w_in"], bre, bim,
        [_job_gather([gat[n] for n in names],
                     [(i, rows[n], ICI, span) for i, n in enumerate(mixers)]
                     + [(4, None, ICI, span), (5, down_a, ICI, span)])])
    gat.update(zip(names, got))
    names = mixers + ["w_down", "w_up"]
    (str_, sti), (got,) = _scan_fwd(
        bur, bui, tab_f,
        [_job_gather([gat[n] for n in names],
                     [(i, rows[n], SIBLING, span) for i, n in enumerate(mixers)]
                     + [(4, down_a, SIBLING, span), (4, down_b, ICI, span), (5, up_a, ICI, span)])])
    gat.update(zip(names, got))
    w_glu_f = gat["w_glu"].reshape(SSM_W, SSM_W)
    w_out_f = gat["w_out"].reshape(D_MODEL, D_MODEL)
    conv_w_f = gat["conv_w"].transpose(1, 0, 2).reshape(3, 2 * D_FF)
    (x2, y0, z, mixed, ya, yb), ((gat["w_down"], gat["w_up"]),) = _fwd_mix(
        xs, p, str_, sti, cre, cim, d_skip2, w_glu_f, b_glu2, gat["w_proj_a"], g_sgu2, ws_st, bmat,
        gat["w_proj_b"], w_out_f,
        [_job_gather([gat["w_down"], gat["w_up"]],
                     [(0, down_b, SIBLING, span), (1, up_a, SIBLING, span),
                      (1, up_b, ICI, (0.0, 0.75)), (1, up_b, SIBLING, (0.75, 1.0))])])
    w_down_f = gat["w_down"].reshape(D_FF, D_MODEL)
    up, act, f, h2, dx3, sm_ffn = _fwd_ffn(x2, target, g_ffn2, gat["w_up"], conv_w_f, conv_b2,
                                           w_down_f, g_final2)

    def leg1_done(names, got):
        return _add_sibling([part[n] for n in names], got, where, "add_sibling_" + names[0])

    def leg2_done(names, sums, got):
        return _add_chips(sums, got, where, "add_chips_" + names[0])

    part, red = {}, {}
    part["w_down"] = _dw_rows(f, dx3, "dw_down", D_FF // 2, 4 * TK).reshape(
        N_CHIP, D_FF // N_CHIP, D_MODEL)
    (dx2, dup, sm_conv, sm_gffn), (got,) = _bwd_ffn(
        dx3, up, act, x2, g_ffn2, gat["w_up"], conv_w_f, w_down_f,
        [_job_sibling_halves([part["w_down"]])])
    sum_down = leg1_done(["w_down"], got)
    part["w_up"], (got,) = _dw_tiles(h2, dup, "dw_up", D_MODEL // 2, 2 * D_FF // N_CHIP,
                                     [_job_to_owner(sum_down)])
    red_down = leg2_done(["w_down"], sum_down, got)
    ((dsr, dsi, du_part, drest, mrg, dya, dyb, yap, dz, y1, sgu, dy0, sm_mix, dbm, dws),
     (got, (red["w_down"],))) = _bwd_mix(
        dx2, p, y0, z, mixed, ya, yb, w_out_f, gat["w_proj_a"], gat["w_proj_b"], w_glu_f, cre, cim,
        ws_st, wst_st, d_skip2, g_sgu2,
        [_job_sibling_halves([part["w_up"]]), _job_swap_halves(red_down)])
    sum_up = leg1_done(["w_up"], got)
    (lam_r, lam_i, dar8, dai8), (got,) = _scan_bwd(dsr, dsi, str_, sti, tab_r, [_job_to_owner(sum_up)])
    red_up = leg2_done(["w_up"], sum_up, got)
    mix4 = ["w_out", "w_proj_a", "w_proj_b", "w_glu"]
    part["w_out"] = _dw_cols(mrg, dx2, "dw_out", D_MODEL // 2, False)[0].reshape(
        N_CHIP, D_MODEL // N_CHIP, D_MODEL)
    part["w_proj_a"] = _dw_cols(yap, dya, "dw_proj_a", D_MODEL // N_CHIP, True)[0]
    part["w_proj_b"] = _dw_cols(sgu, dyb, "dw_proj_b", D_MODEL // N_CHIP, True)[0]
    part["w_glu"] = _dw_cols(y1, dz, "dw_glu", SSM_W, False)[0].reshape(
        N_CHIP, SSM_W // N_CHIP, SSM_W)
    got, (red["w_up"],) = _comm(
        "mixer_sibling_halves", [_job_sibling_halves([part[n] for n in mix4]), _job_swap_halves(red_up)])
    (grad_x, dp, sm_gmix), _ = _bwd_in(
        lam_r, lam_i, du_part, drest, xs, dx2, g_mix2, gat["w_in"], bre, bim)
    sums_m = leg1_done(mix4, got)
    part["w_in"], (got,) = _dw_cols(h1, dp, "dw_in", IN_COLS // N_CHIP, True, [_job_to_owner(sums_m)])
    red_m = leg2_done(mix4, sums_m, got)
    (dbd_r, dbd_i), (got, done_m) = _dw_pair(
        p, SSM_W, lam_r, lam_i, "db_bar",
        [_job_sibling_halves([part["w_in"]]), _job_swap_halves(red_m)])
    red.update(zip(mix4, done_m))
    sum_in = leg1_done(["w_in"], got)
    (dcd_r, dcd_i), (got,) = _dw_pair(dy0, SSM_W, str_, sti, "dc", [_job_to_owner(sum_in)])
    red_in = leg2_done(["w_in"], sum_in, got)
    (red["w_in"],), = _comm("swap_w_in", [_job_swap_halves(red_in)])

    def pick_c(slabs):
        two = LANE // SSM_P
        return jnp.einsum("jshsp->jshp", slabs.reshape(SSM_G // two, two, SSM_H, two, SSM_P)
                          ).reshape(SSM_G, SSM_H, SSM_P)

    def pick_b(slabs):
        return pick_c(slabs).transpose(0, 2, 1)

    dabr = jnp.sum(dar8, axis=0).reshape(SSM_G, SSM_P)
    dabi = jnp.sum(dai8, axis=0).reshape(SSM_G, SSM_P)
    d_a_re, d_a_im, d_log_dt, d_b_re, d_b_im = disc_vjp((dabr, dabi, pick_b(dbd_r), pick_b(dbd_i)))
    gsmall = {
        "g_mix": sm_gmix[0], "a_re": d_a_re, "a_im": d_a_im, "log_dt": d_log_dt,
        "b_re": d_b_re, "b_im": d_b_im, "c_re": pick_c(dcd_r), "c_im": -pick_c(dcd_i),
        "d_skip": sm_mix[0], "b_glu": sm_mix[1], "g_sgu": sm_mix[2],
        "w_s": jnp.where(tril[None], dws, 0.0),
        "b_s": dbm.reshape(CHUNK, SGU_G, SGU_D).sum(-1).T,
        "g_ffn": sm_gffn[0], "conv_b": sm_conv[3], "g_final": sm_ffn[0],
        "conv_w": sm_conv[0:3], "loss": sm_ffn[1, 0:1],
    }

    total_pack = _small_allreduce(_pack(gsmall))
    total = _unpack(total_pack)
    grads = dict(red)
    cs = 2 * D_FF // N_CHIP
    grads["conv_w"] = lax.dynamic_slice(total["conv_w"], (0, chip * cs), (3, cs))
    delta, new_m, new_v = {}, {}, {}
    for n in BIG + ("conv_w",):
        grads[n], delta[n], new_m[n], new_v[n] = _adamw(
            shard2d(w[n]), grads[n], shard2d(m[n]), shard2d(v[n]), "adamw_" + n, PLACE_ROWS.get(n, 3))
    for n in SMALL:
        grads[n] = total[n].reshape(w[n].shape)
    ud, um, uv = _adamw_many(*[[d[n] for n in SMALL] for d in (w, grads, m, v)], "adamw_small")
    for i, n in enumerate(SMALL):
        delta[n], new_m[n], new_v[n] = ud[i], um[i], uv[i]

    def like(d):
        return [d[n].reshape(w[n].shape) for n in WEIGHTS]

    return (total["loss"].reshape(()), grad_x.reshape(x.shape), *like(grads), *like(delta),
            *like(new_m), *like(new_v))
```

```python
import math

import jax
import jax.numpy as jnp
from jax import lax
from jax.experimental import pallas as pl
from jax.experimental.pallas import tpu as pltpu

F32 = jnp.float32
BF16 = jnp.bfloat16
MESH = pl.DeviceIdType.MESH

D_MODEL = 1024
SSM_W = 512
SSM_G = 32
SSM_H = 16
SSM_P = 64
N_STATE = SSM_G * SSM_P
DIAG_N = 128 * SSM_P // SSM_H
SGU_W = 512
SGU_G = 8
SGU_D = 64
CHUNK = 128
D_FF = 2816
IN_COLS = 3584
EPS = 1e-6
N_CHIP = 4

ADAM_LR = 0.001
ADAM_B1 = 0.9
ADAM_B2 = 0.999
ADAM_EPS = 1e-08
ADAM_WD = 0.01
ADAM_STEP = 10

SUBLANE = 8
LANE = 128
VMEM_LIMIT = 56 * 1024 * 1024
TB = 256
TK = 512
SCAN_LANES = 256
SCAN_UNROLL = 4
HALO = SUBLANE

BIG = ("w_in", "w_up", "w_down", "w_out", "w_proj_a", "w_proj_b", "w_glu")
SMALL = ("g_mix", "a_re", "a_im", "log_dt", "b_re", "b_im", "c_re", "c_im", "d_skip", "b_glu",
         "g_sgu", "w_s", "b_s", "g_ffn", "conv_b", "g_final")
WEIGHTS = ("g_mix", "w_in", "a_re", "a_im", "log_dt", "b_re", "b_im", "c_re", "c_im", "d_skip",
           "w_glu", "b_glu", "w_proj_a", "g_sgu", "w_s", "b_s", "w_proj_b", "w_out", "g_ffn",
           "w_up", "conv_w", "conv_b", "w_down", "g_final")

ANY = pl.BlockSpec(memory_space=pl.ANY)


def _params(n_grid):
    return pltpu.CompilerParams(dimension_semantics=("arbitrary",) * n_grid if n_grid else None,
                                vmem_limit_bytes=VMEM_LIMIT)


def _whole():
    return pl.BlockSpec(memory_space=pltpu.VMEM)


def _rows(tb, ncol):
    return pl.BlockSpec((tb, ncol), lambda i: (i, 0))


def _acc(nrow, ncol):
    return pl.BlockSpec((nrow, ncol), lambda i: (0, 0))


def _dot(a, b):
    return jnp.dot(a.astype(BF16), b.astype(BF16), preferred_element_type=F32)


def _dot_nt(a, b):
    return lax.dot_general(a.astype(BF16), b.astype(BF16), (((1,), (1,)), ((), ())),
                           preferred_element_type=F32)


def _sigmoid(v):
    return 0.5 * jnp.tanh(0.5 * v) + 0.5


_GELU_C = math.sqrt(2.0 / math.pi)


def _gelu(v):
    return 0.5 * v * (1.0 + jnp.tanh(_GELU_C * (v + 0.044715 * v * v * v)))


def _gelu_and_grad(v):
    v2 = v * v
    t = jnp.tanh(_GELU_C * v * (1.0 + 0.044715 * v2))
    half = 0.5 * (1.0 + t)
    return v * half, half + 0.5 * v * (1.0 - t * t) * _GELU_C * (1.0 + 3.0 * 0.044715 * v2)


def _rms_stats(v):
    r = lax.rsqrt(jnp.mean(v * v, axis=-1, keepdims=True) + EPS)
    return r, v * r


def _rms_bwd(dxh, xh, r):
    return r * (dxh - xh * jnp.mean(dxh * xh, axis=-1, keepdims=True))


def _place():
    x, y, c = lax.axis_index("x"), lax.axis_index("y"), lax.axis_index("c")
    chips = [(1 - x, y), (x, 1 - y), (1 - x, 1 - y)]
    return x, y, c, chips


def _chip_index(chip):
    return 2 * chip[0] + chip[1]


def _remote(src, dst, send_sem, recv_sem, device):
    return pltpu.make_async_remote_copy(src_ref=src, dst_ref=dst, send_sem=send_sem,
                                        recv_sem=recv_sem, device_id=device, device_id_type=MESH)


def _half(ref_rows, c):
    hr = ref_rows // 2
    return pl.ds(pl.multiple_of(c * hr, SUBLANE), hr)


class _Job:
    def __init__(self, hooks, n_sem, ins=(), inouts=(), outs=()):
        self.hooks, self.n_sem = list(hooks), n_sem
        self.ins, self.inouts, self.outs = list(ins), list(inouts), list(outs)


def _whole_span(start, finish):
    return [(0.0, "start", start), (1.0, "finish", finish)]


ICI, SIBLING = "ici", "sibling"


def _job_gather(bufs, legs):
    def copies(io, leg, first):
        b, window, kind, _ = legs[leg]
        x, y, c, chips = _place()
        k_me = 2 * x + y
        out = []
        for j, ch in enumerate(chips):
            k = _chip_index(ch)
            if window is None:
                src, land, dev = io[b].at[k_me], io[b].at[k], (*ch, c)
            else:
                r0, rows = window
                mine = pl.ds(pl.multiple_of(r0 + c * (rows // 2), SUBLANE), rows // 2)
                theirs = pl.ds(pl.multiple_of(r0 + (1 - c) * (rows // 2), SUBLANE), rows // 2)
                if kind == ICI:
                    src, land, dev = io[b].at[k_me, mine, :], io[b].at[k, mine, :], (*ch, c)
                else:
                    src, land, dev = io[b].at[k, mine, :], io[b].at[k, theirs, :], (x, y, 1 - c)
            out.append((src, land, first + j, dev))
        return out

    def starter(leg):
        def start(ins, io, outs, ssem, rsem):
            for src, _, i, dev in copies(io, leg, 3 * leg):
                _remote(src, src, ssem(i), rsem(i), dev).start()
        return start

    def finisher(leg):
        def finish(ins, io, outs, ssem, rsem):
            cps = copies(io, leg, 3 * leg)
            for _, land, i, dev in cps:
                _remote(land, land, ssem(i), rsem(i), dev).wait_recv()
            for src, _, i, dev in cps:
                _remote(src, src, ssem(i), rsem(i), dev).wait_send()
        return finish

    hooks = []
    for leg, (_, _, _, (begin, end)) in enumerate(legs):
        hooks += [(begin, "start", starter(leg)), (end, "finish", finisher(leg))]
    return _Job(hooks, 3 * len(legs), inouts=bufs)


def _job_sibling_halves(grads):
    n = len(grads)

    def build(ins, outs, ssem, rsem):
        x, y, c, _ = _place()
        return [_remote(ins[t].at[:, _half(grads[t].shape[1], 1 - c), :], outs[t], ssem(t), rsem(t),
                        (x, y, 1 - c)) for t in range(n)]

    def start(ins, io, outs, ssem, rsem):
        for cp in build(ins, outs, ssem, rsem):
            cp.start()

    def finish(ins, io, outs, ssem, rsem):
        for cp in build(ins, outs, ssem, rsem):
            cp.wait()

    return _Job(_whole_span(start, finish), n, ins=grads,
                outs=[jax.ShapeDtypeStruct((N_CHIP, g.shape[1] // 2, g.shape[2]), F32) for g in grads])


def _job_to_owner(sums):
    n = len(sums)

    def build(ins, outs, ssem, rsem):
        x, y, c, chips = _place()
        return [_remote(ins[t].at[_chip_index(ch)], outs[t].at[j], ssem(3 * t + j), rsem(3 * t + j),
                        (*ch, c)) for t in range(n) for j, ch in enumerate(chips)]

    def start(ins, io, outs, ssem, rsem):
        for cp in build(ins, outs, ssem, rsem):
            cp.start()

    def finish(ins, io, outs, ssem, rsem):
        for cp in build(ins, outs, ssem, rsem):
            cp.wait()

    return _Job(_whole_span(start, finish), 3 * n, ins=sums,
                outs=[jax.ShapeDtypeStruct((3,) + s.shape[1:], s.dtype) for s in sums])


def _job_swap_halves(bufs):
    n = len(bufs)

    def start(ins, io, outs, ssem, rsem):
        x, y, c, _ = _place()
        for t in range(n):
            mine = io[t].at[_half(bufs[t].shape[0], c), :]
            _remote(mine, mine, ssem(t), rsem(t), (x, y, 1 - c)).start()

    def finish(ins, io, outs, ssem, rsem):
        x, y, c, _ = _place()
        for t in range(n):
            theirs = io[t].at[_half(bufs[t].shape[0], 1 - c), :]
            _remote(theirs, theirs, ssem(t), rsem(t), (x, y, 1 - c)).wait_recv()
        for t in range(n):
            mine = io[t].at[_half(bufs[t].shape[0], c), :]
            _remote(mine, mine, ssem(t), rsem(t), (x, y, 1 - c)).wait_send()

    return _Job(_whole_span(start, finish), n, inouts=bufs)


def _call(body, name, grid, in_specs, out_specs, out_shape, args, jobs=(), scratch=()):
    n_in, n_out, n_scr = len(args), len(out_shape), len(scratch)
    job_in = [a for jb in jobs for a in jb.ins + jb.inouts]
    job_out = [s for jb in jobs
               for s in [jax.ShapeDtypeStruct(a.shape, a.dtype) for a in jb.inouts] + jb.outs]
    aliases, pos_in, pos_out = {}, n_in, n_out
    for jb in jobs:
        pos_in += len(jb.ins)
        for _ in jb.inouts:
            aliases[pos_in] = pos_out
            pos_in += 1
            pos_out += 1
        pos_out += len(jb.outs)
    n_sem = sum(jb.n_sem for jb in jobs)

    def wrapped(*refs):
        c_in = refs[:n_in]
        j_in = refs[n_in:n_in + len(job_in)]
        c_out = refs[n_in + len(job_in):n_in + len(job_in) + n_out]
        j_out = refs[n_in + len(job_in) + n_out:n_in + len(job_in) + n_out + len(job_out)]
        rest = refs[n_in + len(job_in) + n_out + len(job_out):]
        c_scr = rest[:n_scr]
        views, pi, po, ps = [], 0, 0, 0
        for jb in jobs:
            ins = j_in[pi:pi + len(jb.ins)]
            pi += len(jb.ins) + len(jb.inouts)
            io = j_out[po:po + len(jb.inouts)]
            new = j_out[po + len(jb.inouts):po + len(jb.inouts) + len(jb.outs)]
            po += len(jb.inouts) + len(jb.outs)
            send = (lambda i, o=ps: rest[n_scr].at[o + i])
            recv = (lambda i, o=ps: rest[n_scr + 1].at[o + i])
            ps += jb.n_sem
            views.append((ins, io, new, send, recv))

        def run(frac):
            for kind in ("finish", "start"):
                for jb, vw in zip(jobs, views):
                    for at, what, fn in jb.hooks:
                        if at == frac and what == kind:
                            fn(*vw)

        fracs = sorted({at for jb in jobs for at, _, _ in jb.hooks})
        if not grid:
            for frac in fracs:
                run(frac)
            return
        if jobs:
            assert len(grid) == 1 or set(fracs) <= {0.0, 1.0}
            first = pl.program_id(0) == 0
            last = pl.program_id(0) == grid[0] - 1
            for d in range(1, len(grid)):
                first = jnp.logical_and(first, pl.program_id(d) == 0)
                last = jnp.logical_and(last, pl.program_id(d) == grid[d] - 1)
            for frac in fracs:
                if frac < 1.0:
                    at_step = first if frac == 0.0 else pl.program_id(0) == int(frac * grid[0])
                    pl.when(at_step)(lambda frac=frac: run(frac))
        body(*c_in, *c_out, *c_scr)
        if jobs and 1.0 in fracs:
            pl.when(last)(lambda: run(1.0))

    sems = [pltpu.SemaphoreType.DMA((n_sem,)), pltpu.SemaphoreType.DMA((n_sem,))] if jobs else []
    kwargs = dict(grid=grid) if grid else {}
    res = pl.pallas_call(
        wrapped, name=name, in_specs=list(in_specs) + [ANY] * len(job_in),
        out_specs=list(out_specs) + [ANY] * len(job_out),
        out_shape=list(out_shape) + job_out, scratch_shapes=list(scratch) + sems,
        input_output_aliases=aliases, compiler_params=_params(len(grid)), **kwargs,
    )(*args, *job_in)
    outs, pos, per_job = list(res[:n_out]), n_out, []
    for jb in jobs:
        k = len(jb.inouts) + len(jb.outs)
        per_job.append(list(res[pos:pos + k]))
        pos += k
    return outs, per_job


def _comm(name, jobs):
    return _call(None, name, (), [], [], [], [], jobs)[1]


def _fwd_in(x, g_mix, w_in, bre, bim, jobs=()):
    t_len = x.shape[0]
    cs = IN_COLS // N_CHIP

    def body(x_ref, g_ref, w_ref, bre_ref, bim_ref, p_ref, h_ref, bur_ref, bui_ref):
        xv = x_ref[...]
        r, xh = _rms_stats(xv)
        h = (xh * g_ref[...]).astype(BF16)
        h_ref[...] = h
        for k in range(N_CHIP):
            p_ref[:, k * cs:(k + 1) * cs] = jnp.dot(h, w_ref[k],
                                                    preferred_element_type=F32).astype(BF16)
        u = p_ref[:, 0:SSM_W]
        for i in range(SSM_W // LANE):
            rows, cols = slice(i * LANE, (i + 1) * LANE), slice(i * DIAG_N, (i + 1) * DIAG_N)
            bur_ref[:, cols] = jnp.dot(u[:, rows], bre_ref[rows, cols],
                                       preferred_element_type=F32).astype(BF16)
            bui_ref[:, cols] = jnp.dot(u[:, rows], bim_ref[rows, cols],
                                       preferred_element_type=F32).astype(BF16)

    return _call(
        body, "fwd_in", (t_len // TB,),
        [_rows(TB, D_MODEL), _whole(), _whole(), _whole(), _whole()],
        [_rows(TB, IN_COLS), _rows(TB, D_MODEL), _rows(TB, N_STATE), _rows(TB, N_STATE)],
        [jax.ShapeDtypeStruct((t_len, IN_COLS), BF16), jax.ShapeDtypeStruct((t_len, D_MODEL), BF16),
         jax.ShapeDtypeStruct((t_len, N_STATE), BF16), jax.ShapeDtypeStruct((t_len, N_STATE), BF16)],
        [x, g_mix, w_in, bre, bim], jobs)


def _scan_local(xr, xi, tab, shifts):
    for q, s in enumerate(shifts):
        ar, ai = tab[2 * q], tab[2 * q + 1]
        rr = pltpu.roll(xr, s, 0)
        ri = pltpu.roll(xi, s, 0)
        xr, xi = xr + ar * rr - ai * ri, xi + ar * ri + ai * rr
    return xr, xi


def _scan_carry(xr, xi, tab, cr, ci):
    pr, pi = tab[6], tab[7]
    return xr + pr * cr - pi * ci, xi + pr * ci + pi * cr


BF16_TILE = 2 * SUBLANE


def _load_blocks(r_ref, i_ref, base):
    out = []
    for q in range(SCAN_UNROLL // 2):
        rows = pl.ds(pl.multiple_of(base + q * BF16_TILE, BF16_TILE), BF16_TILE)
        vr, vi = r_ref[rows, :].astype(F32), i_ref[rows, :].astype(F32)
        out += [(vr[:SUBLANE], vi[:SUBLANE]), (vr[SUBLANE:], vi[SUBLANE:])]
    return out


def _store_blocks(r_ref, i_ref, base, blocks):
    for q in range(SCAN_UNROLL // 2):
        rows = pl.ds(pl.multiple_of(base + q * BF16_TILE, BF16_TILE), BF16_TILE)
        r_ref[rows, :] = jnp.concatenate([blocks[2 * q][0], blocks[2 * q + 1][0]], 0).astype(r_ref.dtype)
        i_ref[rows, :] = jnp.concatenate([blocks[2 * q][1], blocks[2 * q + 1][1]], 0).astype(i_ref.dtype)


def _scan_fwd(bur, bui, tab, jobs=()):
    t_len = bur.shape[0]
    nblk = t_len // SUBLANE
    lb = SCAN_LANES

    def body(br_ref, bi_ref, tab_ref, sr_ref, si_ref):
        tab_v = [tab_ref[q] for q in range(8)]

        def step(k, carry):
            cr, ci = carry
            base = pl.multiple_of(k * SCAN_UNROLL * SUBLANE, SCAN_UNROLL * SUBLANE)
            local = [_scan_local(xr, xi, tab_v, (1, 2, 4))
                     for xr, xi in _load_blocks(br_ref, bi_ref, base)]
            done = []
            for xr, xi in local:
                xr, xi = _scan_carry(xr, xi, tab_v, cr, ci)
                done.append((xr, xi))
                cr, ci = xr[SUBLANE - 1:SUBLANE, :], xi[SUBLANE - 1:SUBLANE, :]
            _store_blocks(sr_ref, si_ref, base, done)
            return cr, ci

        zero = jnp.zeros((1, lb), F32)
        lax.fori_loop(0, nblk // SCAN_UNROLL, step, (zero, zero))

    col = pl.BlockSpec((t_len, lb), lambda j: (0, j))
    return _call(
        body, "scan_fwd", (N_STATE // lb,),
        [col, col, pl.BlockSpec((8, SUBLANE, lb), lambda j: (0, 0, j))], [col, col],
        [jax.ShapeDtypeStruct((t_len, N_STATE), BF16)] * 2, [bur, bui, tab], jobs)


def _sgu_mix(v, ws_ref, lane_lo):
    rows = []
    for c0 in range(0, v.shape[0], CHUNK):
        slabs = []
        for j in range(SGU_W // LANE):
            prod = jnp.dot(ws_ref[j], v[c0:c0 + CHUNK, j * LANE:(j + 1) * LANE].astype(BF16),
                           preferred_element_type=F32)
            slabs.append(jnp.where(lane_lo, prod[:CHUNK], prod[CHUNK:]))
        rows.append(jnp.concatenate(slabs, axis=1))
    return jnp.concatenate(rows, axis=0) if len(rows) > 1 else rows[0]


def _fwd_mix(x, p, str_, sti, cre, cim, d_skip, w_glu, b_glu, w_pa, g_sgu, ws_st, bmat, w_pb, w_out,
             jobs=()):
    t_len = x.shape[0]

    def body(x_ref, p_ref, sr_ref, si_ref, cre_ref, cim_ref, dsk_ref, wg_ref, bg_ref, wpa_ref,
             gs_ref, ws_ref, bm_ref, wpb_ref, wo_ref,
             x2_ref, y0_ref, z_ref, mx_ref, ya_ref, yb_ref):
        u = p_ref[:, 0:SSM_W].astype(F32)
        y0 = jnp.concatenate(
            [_dot(sr_ref[:, i * DIAG_N:(i + 1) * DIAG_N],
                  cre_ref[i * DIAG_N:(i + 1) * DIAG_N, i * LANE:(i + 1) * LANE])
             - _dot(si_ref[:, i * DIAG_N:(i + 1) * DIAG_N],
                    cim_ref[i * DIAG_N:(i + 1) * DIAG_N, i * LANE:(i + 1) * LANE])
             for i in range(SSM_W // LANE)], axis=1) + dsk_ref[...] * u
        y0_ref[...] = y0.astype(BF16)
        y1 = _gelu(y0)
        z = _dot(y1, wg_ref[...]) + bg_ref[...]
        z_ref[...] = z.astype(BF16)
        ya_pre = (y1 * _sigmoid(z)).astype(BF16)
        ya = jnp.concatenate([jnp.dot(ya_pre, wpa_ref[k], preferred_element_type=F32)
                              for k in range(N_CHIP)], axis=1)
        ya_ref[...] = ya.astype(BF16)

        uvg = _gelu(p_ref[:, SSM_W:SSM_W + 2 * SGU_W].astype(F32))
        u2 = uvg[:, :SGU_W]
        _, vh = _rms_stats(uvg[:, SGU_W:])
        v3 = vh * gs_ref[...]
        lane_lo = lax.broadcasted_iota(jnp.int32, (CHUNK, LANE), 1) < SGU_D
        bias = jnp.concatenate([bm_ref[...]] * (TB // CHUNK), axis=0)
        mixed = _sgu_mix(v3, ws_ref, lane_lo) + bias
        mx_ref[...] = mixed.astype(BF16)
        sgu = (u2 * mixed).astype(BF16)
        yb = jnp.concatenate([jnp.dot(sgu, wpb_ref[k], preferred_element_type=F32)
                              for k in range(N_CHIP)], axis=1)
        yb_ref[...] = yb.astype(BF16)

        lg0 = SSM_W + 2 * SGU_W
        ga = _sigmoid(p_ref[:, lg0:lg0 + D_MODEL].astype(F32))
        gb = _sigmoid(p_ref[:, lg0 + D_MODEL:lg0 + 2 * D_MODEL].astype(F32))
        mrg = ga * ya + gb * yb
        x2_ref[...] = x_ref[...] + _dot(mrg, wo_ref[...])

    return _call(
        body, "fwd_mix", (t_len // TB,),
        [_rows(TB, D_MODEL), _rows(TB, IN_COLS), _rows(TB, N_STATE), _rows(TB, N_STATE)]
        + [_whole()] * 11,
        [_rows(TB, D_MODEL), _rows(TB, SSM_W), _rows(TB, SSM_W), _rows(TB, SGU_W),
         _rows(TB, D_MODEL), _rows(TB, D_MODEL)],
        [jax.ShapeDtypeStruct((t_len, D_MODEL), F32), jax.ShapeDtypeStruct((t_len, SSM_W), BF16),
         jax.ShapeDtypeStruct((t_len, SSM_W), BF16), jax.ShapeDtypeStruct((t_len, SGU_W), BF16),
         jax.ShapeDtypeStruct((t_len, D_MODEL), BF16), jax.ShapeDtypeStruct((t_len, D_MODEL), BF16)],
        [x, p, str_, sti, cre, cim, d_skip, w_glu, b_glu, w_pa, g_sgu, ws_st, bmat, w_pb, w_out], jobs)


def _conv_taps(v, cw_ref, c0, width):
    w0 = cw_ref[0:1, c0:c0 + width]
    w1 = cw_ref[1:2, c0:c0 + width]
    w2 = cw_ref[2:3, c0:c0 + width]
    return w0 * pltpu.roll(v, 2, 0) + w1 * pltpu.roll(v, 1, 0) + w2 * v


def _fwd_ffn(x2, target, g_ffn, w_up, conv_w, conv_b, w_down, g_final):
    t_len = x2.shape[0]
    half = D_FF // 2
    blocks_per_halo = TB // HALO

    def body(x2_ref, xp_ref, tg_ref, gf_ref, wu_ref, cw_ref, cb_ref, wd_ref, gl_ref,
             up_ref, act_ref, f_ref, h2_ref, dx3_ref, sm_ref):
        i = pl.program_id(0)
        xe = jnp.concatenate([xp_ref[...] * jnp.where(i == 0, 0.0, 1.0), x2_ref[...]], axis=0)
        _, xh = _rms_stats(xe)
        h2 = (xh * gf_ref[...]).astype(BF16)
        h2_ref[...] = h2[HALO:]
        acc = jnp.zeros((TB, D_MODEL), F32)
        ups = [jnp.dot(h2, wu_ref[k], preferred_element_type=F32) for k in range(N_CHIP)]
        for hc in range(2):
            ca = hc * half
            cb = D_FF + hc * half
            ua, ub = ups[hc], ups[2 + hc]
            up_ref[:, ca:ca + half] = ua[HALO:].astype(BF16)
            up_ref[:, cb:cb + half] = ub[HALO:].astype(BF16)
            ac = _conv_taps(ua, cw_ref, ca, half)[HALO:] + cb_ref[:, ca:ca + half]
            bc = _conv_taps(ub, cw_ref, cb, half)[HALO:] + cb_ref[:, cb:cb + half]
            act_ref[:, ca:ca + half] = ac.astype(BF16)
            act_ref[:, cb:cb + half] = bc.astype(BF16)
            f = (ac * _sigmoid(ac) * bc).astype(BF16)
            f_ref[:, ca:ca + half] = f
            acc = acc + jnp.dot(f, wd_ref[ca:ca + half, :], preferred_element_type=F32)
        x3 = x2_ref[...] + acc
        r3, xh3 = _rms_stats(x3)
        err = xh3 * gl_ref[...] - tg_ref[...]
        dout = err * (1.0 / D_MODEL)
        dx3_ref[...] = _rms_bwd(dout * gl_ref[...], xh3, r3)
        dgl = jnp.sum(dout * xh3, axis=0, keepdims=True)
        loss = 0.5 * jnp.sum(jnp.mean(err * err, axis=-1, keepdims=True), axis=0, keepdims=True)
        upd = jnp.concatenate([dgl, jnp.broadcast_to(loss, (1, D_MODEL)),
                               jnp.zeros((SUBLANE - 2, D_MODEL), F32)], axis=0)

        @pl.when(i == 0)
        def _():
            sm_ref[...] = upd

        @pl.when(i > 0)
        def _():
            sm_ref[...] += upd

    prev = pl.BlockSpec((HALO, D_MODEL), lambda i: (jnp.maximum(i * blocks_per_halo - 1, 0), 0))
    return _call(
        body, "fwd_ffn", (t_len // TB,),
        [_rows(TB, D_MODEL), prev, _rows(TB, D_MODEL)] + [_whole()] * 6,
        [_rows(TB, 2 * D_FF), _rows(TB, 2 * D_FF), _rows(TB, D_FF), _rows(TB, D_MODEL),
         _rows(TB, D_MODEL), _acc(SUBLANE, D_MODEL)],
        [jax.ShapeDtypeStruct((t_len, 2 * D_FF), BF16), jax.ShapeDtypeStruct((t_len, 2 * D_FF), BF16),
         jax.ShapeDtypeStruct((t_len, D_FF), BF16), jax.ShapeDtypeStruct((t_len, D_MODEL), BF16),
         jax.ShapeDtypeStruct((t_len, D_MODEL), F32), jax.ShapeDtypeStruct((SUBLANE, D_MODEL), F32)],
        [x2, x2, target, g_ffn, w_up, conv_w, conv_b, w_down, g_final])[0]


def _bwd_ffn(dx3, up, act, x2, g_ffn, w_up, conv_w, w_down, jobs=()):
    t_len = x2.shape[0]
    half = D_FF // 2
    nblk = t_len // TB
    halo_b = 2 * HALO
    n_e = TB + HALO

    def body(dx_ref, dxn_ref, up_ref, act_ref, actn_ref, x2_ref, gf_ref, wu_ref, cw_ref,
             wd_ref, dx2_ref, dup_ref, smw_ref, smg_ref):
        i = pl.program_id(0)
        keep_last = jnp.where(i == nblk - 1, 0.0, 1.0)
        dxe = jnp.concatenate([dx_ref[...], dxn_ref[...] * keep_last], axis=0).astype(BF16)
        dh2 = jnp.zeros((TB, D_MODEL), F32)
        zpad = jnp.zeros((1, half), F32)
        dfs = [lax.dot_general(dxe, wd_ref[hc * half:(hc + 1) * half, :], (((1,), (1,)), ((), ())),
                               preferred_element_type=F32) for hc in range(2)]
        for hc in range(2):
            ca = hc * half
            cb = D_FF + hc * half
            ac = jnp.concatenate([act_ref[:, ca:ca + half].astype(F32),
                                  actn_ref[:, ca:ca + half].astype(F32)[:HALO]], axis=0)
            bc = jnp.concatenate([act_ref[:, cb:cb + half].astype(F32),
                                  actn_ref[:, cb:cb + half].astype(F32)[:HALO]], axis=0)
            wa = [cw_ref[k:k + 1, ca:ca + half] for k in range(3)]
            wb = [cw_ref[k:k + 1, cb:cb + half] for k in range(3)]
            df = dfs[hc]
            sg = _sigmoid(ac)
            da = df * bc * sg * (1.0 + ac * (1.0 - sg))
            db = df * ac * sg
            da1, da2 = pltpu.roll(da, n_e - 1, 0), pltpu.roll(da, n_e - 2, 0)
            db1, db2 = pltpu.roll(db, n_e - 1, 0), pltpu.roll(db, n_e - 2, 0)
            dua = (wa[2] * da + wa[1] * da1 + wa[0] * da2)[:TB]
            dub = (wb[2] * db + wb[1] * db1 + wb[0] * db2)[:TB]
            dup_ref[:, ca:ca + half] = dua.astype(BF16)
            dup_ref[:, cb:cb + half] = dub.astype(BF16)
            dh2 = dh2 + _dot_nt(dua, wu_ref[hc]) + _dot_nt(dub, wu_ref[2 + hc])
            rows = []
            for u_, d0, d1, d2 in ((up_ref[:, ca:ca + half].astype(F32), da, da1, da2),
                                   (up_ref[:, cb:cb + half].astype(F32), db, db1, db2)):
                rows.append([jnp.sum(u_ * d2[:TB], axis=0, keepdims=True),
                             jnp.sum(u_ * d1[:TB], axis=0, keepdims=True),
                             jnp.sum(u_ * d0[:TB], axis=0, keepdims=True),
                             jnp.sum(d0[:TB], axis=0, keepdims=True)])
            for c0, rws in ((ca, rows[0]), (cb, rows[1])):
                upd = jnp.concatenate(rws + [zpad] * (SUBLANE - 4), axis=0)

                @pl.when(i == 0)
                def _(upd=upd, c0=c0):
                    smw_ref[:, c0:c0 + half] = upd

                @pl.when(i > 0)
                def _(upd=upd, c0=c0):
                    smw_ref[:, c0:c0 + half] += upd

        r2, xh2 = _rms_stats(x2_ref[...])
        dx2_ref[...] = dx_ref[...] + _rms_bwd(dh2 * gf_ref[...], xh2, r2)
        updg = jnp.concatenate([jnp.sum(dh2 * xh2, axis=0, keepdims=True),
                                jnp.zeros((SUBLANE - 1, D_MODEL), F32)], axis=0)

        @pl.when(i == 0)
        def _():
            smg_ref[...] = updg

        @pl.when(i > 0)
        def _():
            smg_ref[...] += updg

    nxt_d = pl.BlockSpec((HALO, D_MODEL),
                         lambda i: (jnp.minimum((i + 1) * (TB // HALO), t_len // HALO - 1), 0))
    nxt_a = pl.BlockSpec((halo_b, 2 * D_FF),
                         lambda i: (jnp.minimum((i + 1) * (TB // halo_b), t_len // halo_b - 1), 0))
    return _call(
        body, "bwd_ffn", (nblk,),
        [_rows(TB, D_MODEL), nxt_d, _rows(TB, 2 * D_FF), _rows(TB, 2 * D_FF), nxt_a,
         _rows(TB, D_MODEL)] + [_whole()] * 4,
        [_rows(TB, D_MODEL), _rows(TB, 2 * D_FF), _acc(SUBLANE, 2 * D_FF), _acc(SUBLANE, D_MODEL)],
        [jax.ShapeDtypeStruct((t_len, D_MODEL), F32), jax.ShapeDtypeStruct((t_len, 2 * D_FF), BF16),
         jax.ShapeDtypeStruct((SUBLANE, 2 * D_FF), F32), jax.ShapeDtypeStruct((SUBLANE, D_MODEL), F32)],
        [dx3, dx3, up, act, act, x2, g_ffn, w_up, conv_w, w_down], jobs)


def _bwd_mix(dx2, p, y0, z, mixed, ya, yb, w_out, w_pa, w_pb, w_glu, cre, cim, ws_st, wst_st,
             d_skip, g_sgu, jobs=()):
    t_len = dx2.shape[0]
    pc = D_MODEL // N_CHIP
    n_slab = SGU_W // LANE

    def body(dx_ref, p_ref, y0_ref, z_ref, mx_ref, ya_ref, yb_ref, wo_ref, wpa_ref, wpb_ref,
             wg_ref, cre_ref, cim_ref, ws_ref, wst_ref, dsk_ref, gs_ref,
             dsr_ref, dsi_ref, du_ref, drest_ref, mrg_ref, dya_ref, dyb_ref, yap_ref, dz_ref,
             y1_ref, sgu_ref, dy0_ref, sm_ref, dbm_ref, dws_ref):
        i = pl.program_id(0)
        first = i == 0
        lg0 = SSM_W + 2 * SGU_W
        ga = _sigmoid(p_ref[:, lg0:lg0 + D_MODEL].astype(F32))
        gb = _sigmoid(p_ref[:, lg0 + D_MODEL:lg0 + 2 * D_MODEL].astype(F32))
        yav = ya_ref[...].astype(F32)
        ybv = yb_ref[...].astype(F32)
        mrg_ref[...] = (ga * yav + gb * ybv).astype(BF16)
        y0v = y0_ref[...].astype(F32)
        y1, y1_grad = _gelu_and_grad(y0v)
        sz = _sigmoid(z_ref[...].astype(F32))
        y1_ref[...] = y1.astype(BF16)
        yap_ref[...] = (y1 * sz).astype(BF16)

        dmrg = _dot_nt(dx_ref[...], wo_ref[...])
        drest_ref[:, 2 * SGU_W:2 * SGU_W + D_MODEL] = (dmrg * yav * ga * (1.0 - ga)).astype(BF16)
        drest_ref[:, 2 * SGU_W + D_MODEL:] = (dmrg * ybv * gb * (1.0 - gb)).astype(BF16)
        dya = (dmrg * ga).astype(BF16)
        dyb = (dmrg * gb).astype(BF16)
        dya_ref[...] = dya
        dyb_ref[...] = dyb

        dyap = jnp.zeros((TB, SSM_W), F32)
        for k in range(N_CHIP):
            dyap = dyap + _dot_nt(dya[:, k * pc:(k + 1) * pc], wpa_ref[k])
        dz = dyap * y1 * sz * (1.0 - sz)
        dz_ref[...] = dz.astype(BF16)
        dy0 = (dyap * sz + _dot_nt(dz, wg_ref[...])) * y1_grad
        dy0_ref[...] = dy0.astype(BF16)
        u = p_ref[:, 0:SSM_W].astype(F32)
        du_ref[...] = dy0 * dsk_ref[...]
        for q in range(SSM_W // LANE):
            rows, cols = slice(q * DIAG_N, (q + 1) * DIAG_N), slice(q * LANE, (q + 1) * LANE)
            dsr_ref[:, rows] = _dot_nt(dy0[:, cols], cre_ref[rows, cols]).astype(BF16)
            dsi_ref[:, rows] = (-_dot_nt(dy0[:, cols], cim_ref[rows, cols])).astype(BF16)

        uv = p_ref[:, SSM_W:lg0].astype(F32)
        uvg, gg = _gelu_and_grad(uv)
        u2 = uvg[:, :SGU_W]
        rv, vh = _rms_stats(uvg[:, SGU_W:])
        v3 = vh * gs_ref[...]
        mixed = mx_ref[...].astype(F32)
        dsgu = jnp.zeros((TB, SGU_W), F32)
        for k in range(N_CHIP):
            dsgu = dsgu + _dot_nt(dyb[:, k * pc:(k + 1) * pc], wpb_ref[k])
        sgu_ref[...] = (u2 * mixed).astype(BF16)
        drest_ref[:, 0:SGU_W] = (dsgu * mixed * gg[:, :SGU_W]).astype(BF16)
        dmix = dsgu * u2
        lane_lo = lax.broadcasted_iota(jnp.int32, (CHUNK, LANE), 1) < SGU_D
        dv3 = _sgu_mix(dmix, wst_ref, lane_lo)
        dbm = jnp.zeros((CHUNK, SGU_W), F32)
        for c0 in range(0, TB, CHUNK):
            dbm = dbm + dmix[c0:c0 + CHUNK]
        for j in range(n_slab):
            lo = jnp.zeros((CHUNK, CHUNK), F32)
            hi = jnp.zeros((CHUNK, CHUNK), F32)
            for c0 in range(0, TB, CHUNK):
                dsl = dmix[c0:c0 + CHUNK, j * LANE:(j + 1) * LANE]
                vsl = v3[c0:c0 + CHUNK, j * LANE:(j + 1) * LANE]
                lo = lo + _dot_nt(jnp.where(lane_lo, dsl, 0.0), vsl)
                hi = hi + _dot_nt(jnp.where(lane_lo, 0.0, dsl), vsl)

            @pl.when(first)
            def _(lo=lo, hi=hi, j=j):
                dws_ref[2 * j] = lo
                dws_ref[2 * j + 1] = hi

            @pl.when(jnp.logical_not(first))
            def _(lo=lo, hi=hi, j=j):
                dws_ref[2 * j] += lo
                dws_ref[2 * j + 1] += hi

        dv2 = _rms_bwd(dv3 * gs_ref[...], vh, rv)
        drest_ref[:, SGU_W:2 * SGU_W] = (dv2 * gg[:, SGU_W:]).astype(BF16)

        upd = jnp.concatenate([jnp.sum(dy0 * u, axis=0, keepdims=True),
                               jnp.sum(dz, axis=0, keepdims=True),
                               jnp.sum(dv3 * vh, axis=0, keepdims=True),
                               jnp.zeros((SUBLANE - 3, SSM_W), F32)], axis=0)

        @pl.when(first)
        def _():
            sm_ref[...] = upd
            dbm_ref[...] = dbm

        @pl.when(jnp.logical_not(first))
        def _():
            sm_ref[...] += upd
            dbm_ref[...] += dbm

    rest = 2 * SGU_W + 2 * D_MODEL
    bf_d, bf_s = jax.ShapeDtypeStruct((t_len, D_MODEL), BF16), jax.ShapeDtypeStruct((t_len, SSM_W), BF16)
    return _call(
        body, "bwd_mix", (t_len // TB,),
        [_rows(TB, D_MODEL), _rows(TB, IN_COLS), _rows(TB, SSM_W), _rows(TB, SSM_W),
         _rows(TB, SGU_W), _rows(TB, D_MODEL), _rows(TB, D_MODEL)] + [_whole()] * 10,
        [_rows(TB, N_STATE), _rows(TB, N_STATE), _rows(TB, SSM_W), _rows(TB, rest),
         _rows(TB, D_MODEL), _rows(TB, D_MODEL), _rows(TB, D_MODEL), _rows(TB, SSM_W),
         _rows(TB, SSM_W), _rows(TB, SSM_W), _rows(TB, SGU_W), _rows(TB, SSM_W),
         _acc(SUBLANE, SSM_W), _acc(CHUNK, SGU_W),
         pl.BlockSpec((SGU_G, CHUNK, CHUNK), lambda i: (0, 0, 0))],
        [jax.ShapeDtypeStruct((t_len, N_STATE), BF16), jax.ShapeDtypeStruct((t_len, N_STATE), BF16),
         jax.ShapeDtypeStruct((t_len, SSM_W), F32), jax.ShapeDtypeStruct((t_len, rest), BF16),
         bf_d, bf_d, bf_d, bf_s, bf_s, bf_s, bf_s, bf_s,
         jax.ShapeDtypeStruct((SUBLANE, SSM_W), F32), jax.ShapeDtypeStruct((CHUNK, SGU_W), F32),
         jax.ShapeDtypeStruct((SGU_G, CHUNK, CHUNK), F32)],
        [dx2, p, y0, z, mixed, ya, yb, w_out, w_pa, w_pb, w_glu, cre, cim, ws_st, wst_st, d_skip,
         g_sgu], jobs)


def _scan_bwd(dsr, dsi, str_, sti, tab_rev, jobs=()):
    t_len = dsr.shape[0]
    nblk = t_len // SUBLANE
    lb = SCAN_LANES

    def body(dr_ref, di_ref, sr_ref, si_ref, tab_ref, lr_ref, li_ref, dar_ref, dai_ref):
        tab_v = [tab_ref[q] for q in range(8)]
        row0 = lax.broadcasted_iota(jnp.int32, (SUBLANE, lb), 0) == 0
        tile = BF16_TILE

        def step(k, carry):
            cr, ci, acr, aci = carry
            base = pl.multiple_of((nblk - (k + 1) * SCAN_UNROLL) * SUBLANE, SCAN_UNROLL * SUBLANE)
            state = _load_blocks(sr_ref, si_ref, base)
            before = pl.ds(pl.multiple_of(jnp.maximum(base - tile, 0), tile), tile)
            has_before = jnp.where(base > 0, 1.0, 0.0)
            prev = (sr_ref[before, :].astype(F32)[tile - 1:tile] * has_before,
                    si_ref[before, :].astype(F32)[tile - 1:tile] * has_before)
            local = [_scan_local(xr, xi, tab_v, (7, 6, 4))
                     for xr, xi in _load_blocks(dr_ref, di_ref, base)]
            lam = [None] * SCAN_UNROLL
            for b in reversed(range(SCAN_UNROLL)):
                xr, xi = _scan_carry(*local[b], tab_v, cr, ci)
                lam[b] = (xr, xi)
                cr, ci = xr[0:1, :], xi[0:1, :]
                pr, pi = prev if b == 0 else (state[b - 1][0][SUBLANE - 1:], state[b - 1][1][SUBLANE - 1:])
                s_r = jnp.where(row0, pr, pltpu.roll(state[b][0], 1, 0))
                s_i = jnp.where(row0, pi, pltpu.roll(state[b][1], 1, 0))
                acr = acr + xr * s_r + xi * s_i
                aci = aci + xi * s_r - xr * s_i
            _store_blocks(lr_ref, li_ref, base, lam)
            return cr, ci, acr, aci

        zero = jnp.zeros((1, lb), F32)
        zacc = jnp.zeros((SUBLANE, lb), F32)
        _, _, acr, aci = lax.fori_loop(0, nblk // SCAN_UNROLL, step, (zero, zero, zacc, zacc))
        dar_ref[...] = acr
        dai_ref[...] = aci

    col = pl.BlockSpec((t_len, lb), lambda j: (0, j))
    small = pl.BlockSpec((SUBLANE, lb), lambda j: (0, j))
    return _call(
        body, "scan_bwd", (N_STATE // lb,),
        [col, col, col, col, pl.BlockSpec((8, SUBLANE, lb), lambda j: (0, 0, j))],
        [col, col, small, small],
        [jax.ShapeDtypeStruct((t_len, N_STATE), BF16)] * 2
        + [jax.ShapeDtypeStruct((SUBLANE, N_STATE), F32)] * 2,
        [dsr, dsi, str_, sti, tab_rev], jobs)


def _bwd_in(lam_r, lam_i, du_part, drest, x, dx2, g_mix, w_in, bre, bim, jobs=()):
    t_len = x.shape[0]
    cs = IN_COLS // N_CHIP

    def body(lr_ref, li_ref, du_ref, dr_ref, x_ref, dx2_ref, g_ref, w_ref, bre_ref, bim_ref,
             gx_ref, dp_ref, sm_ref):
        i = pl.program_id(0)
        du = du_ref[...] + jnp.concatenate(
            [_dot_nt(lr_ref[:, i * DIAG_N:(i + 1) * DIAG_N],
                     bre_ref[i * LANE:(i + 1) * LANE, i * DIAG_N:(i + 1) * DIAG_N])
             + _dot_nt(li_ref[:, i * DIAG_N:(i + 1) * DIAG_N],
                       bim_ref[i * LANE:(i + 1) * LANE, i * DIAG_N:(i + 1) * DIAG_N])
             for i in range(SSM_W // LANE)], axis=1)
        dp_ref[:, 0:SSM_W] = du.astype(BF16)
        dp_ref[:, SSM_W:] = dr_ref[...]
        dh = jnp.zeros((TB, D_MODEL), F32)
        for k in range(N_CHIP):
            dh = dh + _dot_nt(dp_ref[:, k * cs:(k + 1) * cs], w_ref[k])
        r, xh = _rms_stats(x_ref[...])
        gx_ref[...] = dx2_ref[...] + _rms_bwd(dh * g_ref[...], xh, r)
        upd = jnp.concatenate([jnp.sum(dh * xh, axis=0, keepdims=True),
                               jnp.zeros((SUBLANE - 1, D_MODEL), F32)], axis=0)

        @pl.when(i == 0)
        def _():
            sm_ref[...] = upd

        @pl.when(i > 0)
        def _():
            sm_ref[...] += upd

    return _call(
        body, "bwd_in", (t_len // TB,),
        [_rows(TB, N_STATE), _rows(TB, N_STATE), _rows(TB, SSM_W), _rows(TB, IN_COLS - SSM_W),
         _rows(TB, D_MODEL), _rows(TB, D_MODEL)] + [_whole()] * 4,
        [_rows(TB, D_MODEL), _rows(TB, IN_COLS), _acc(SUBLANE, D_MODEL)],
        [jax.ShapeDtypeStruct((t_len, D_MODEL), F32), jax.ShapeDtypeStruct((t_len, IN_COLS), BF16),
         jax.ShapeDtypeStruct((SUBLANE, D_MODEL), F32)],
        [lam_r, lam_i, du_part, drest, x, dx2, g_mix, w_in, bre, bim], jobs)


def _matmul_tn(a, b, name, out_shape, grid_ij, a_blk, a_map, b_blk, b_map, o_blk, o_map, jobs=()):
    tk = a_blk[0]
    nk = a.shape[0] // tk
    assert nk * tk == a.shape[0] and nk > 0

    def body(a_ref, b_ref, o_ref, acc_ref):
        k = pl.program_id(2)

        @pl.when(k == 0)
        def _():
            acc_ref[...] = jnp.zeros_like(acc_ref)

        acc_ref[...] += lax.dot_general(a_ref[...].astype(BF16), b_ref[...].astype(BF16),
                                        (((0,), (0,)), ((), ())), preferred_element_type=F32)

        @pl.when(k == nk - 1)
        def _():
            o_ref[...] = acc_ref[...]

    outs, per_job = _call(
        body, name, (grid_ij[0], grid_ij[1], nk),
        [pl.BlockSpec(a_blk, a_map), pl.BlockSpec(b_blk, b_map)], [pl.BlockSpec(o_blk, o_map)],
        [jax.ShapeDtypeStruct(out_shape, F32)], [a, b], jobs,
        scratch=[pltpu.VMEM((a_blk[1], b_blk[1]), F32)])
    return outs[0], per_job


def _dw_rows(a, b, name, tm, tk):
    m, n = a.shape[1], b.shape[1]
    tk = min(tk, a.shape[0])
    return _matmul_tn(a, b, name, (m, n), (m // tm, 1),
                      (tk, tm), lambda i, j, k: (k, i), (tk, n), lambda i, j, k: (k, 0),
                      (tm, n), lambda i, j, k: (i, 0))[0]


def _dw_cols(a, b, name, tn, sharded, jobs=()):
    t_len, m = a.shape
    n = b.shape[1]

    def body(a_ref, b_ref, o_ref):
        o_ref[...] = lax.dot_general(a_ref[...].astype(BF16), b_ref[...].astype(BF16),
                                     (((0,), (0,)), ((), ())), preferred_element_type=F32)

    if sharded:
        o_spec, o_shape = pl.BlockSpec((None, m, tn), lambda j: (j, 0, 0)), (n // tn, m, tn)
    else:
        o_spec, o_shape = pl.BlockSpec((m, tn), lambda j: (0, j)), (m, n)
    outs, per_job = _call(body, name, (n // tn,),
                          [_whole(), pl.BlockSpec((t_len, tn), lambda j: (0, j))], [o_spec],
                          [jax.ShapeDtypeStruct(o_shape, F32)], [a, b], jobs)
    return outs[0], per_job


def _dw_tiles(a, b, name, tm, tn, jobs=()):
    t_len, m = a.shape
    n = b.shape[1]

    def body(a_ref, b_ref, o_ref):
        o_ref[...] = lax.dot_general(a_ref[...].astype(BF16), b_ref[...].astype(BF16),
                                     (((0,), (0,)), ((), ())), preferred_element_type=F32)

    outs, per_job = _call(body, name, (n // tn, m // tm),
                          [pl.BlockSpec((t_len, tm), lambda j, i: (0, i)),
                           pl.BlockSpec((t_len, tn), lambda j, i: (0, j))],
                          [pl.BlockSpec((None, tm, tn), lambda j, i: (j, i, 0))],
                          [jax.ShapeDtypeStruct((n // tn, m, tn), F32)], [a, b], jobs)
    return outs[0], per_job


def _dw_pair(a, m, b1, b2, name, jobs=()):
    t_len = a.shape[0]
    n_slab = DIAG_N // LANE
    rows_per_slab = LANE // n_slab

    def body(a_ref, b1_ref, b2_ref, o1_ref, o2_ref):
        for b_ref, o_ref in ((b1_ref, o1_ref), (b2_ref, o2_ref)):
            prod = lax.dot_general(a_ref[...].astype(BF16), b_ref[...].astype(BF16),
                                   (((0,), (0,)), ((), ())), preferred_element_type=F32)
            for j in range(n_slab):
                rows = slice(j * rows_per_slab, (j + 1) * rows_per_slab)
                o_ref[rows, :] = prod[rows, j * LANE:(j + 1) * LANE]

    tok = pl.BlockSpec((t_len, DIAG_N), lambda i: (0, i))
    out = pl.BlockSpec((LANE, LANE), lambda i: (i, 0))
    return _call(body, name, (m // LANE,),
                 [pl.BlockSpec((t_len, LANE), lambda i: (0, i)), tok, tok], [out, out],
                 [jax.ShapeDtypeStruct((m, LANE), F32)] * 2, [a, b1, b2], jobs)


def _prefetch_call(body, name, grid, scalars, in_specs, out_specs, out_shape, args):
    return pl.pallas_call(
        body, name=name,
        grid_spec=pltpu.PrefetchScalarGridSpec(num_scalar_prefetch=1, grid=grid, in_specs=in_specs,
                                               out_specs=out_specs),
        out_shape=out_shape, compiler_params=_params(len(grid)),
    )(scalars, *args)


def _place_shard(w, where, name, dtype, tr):
    rows, cols = w.shape

    def body(s_ref, w_ref, o_ref):
        o_ref[...] = w_ref[...].astype(dtype)

    return _prefetch_call(
        body, name, (rows // tr,), where,
        [pl.BlockSpec((tr, cols), lambda i, s: (i, 0))],
        pl.BlockSpec((None, tr, cols), lambda i, s: (s[0], i, 0)),
        jax.ShapeDtypeStruct((N_CHIP, rows, cols), dtype), [w])


def _place_shards(ws, where, name, dtype):
    n = len(ws)

    def body(s_ref, *refs):
        for t in range(n):
            refs[n + t][...] = refs[t][...].astype(dtype)

    return _prefetch_call(
        body, name, (1,), where,
        [pl.BlockSpec(w.shape, lambda i, s: (0, 0)) for w in ws],
        [pl.BlockSpec((None,) + w.shape, lambda i, s: (s[0], 0, 0)) for w in ws],
        [jax.ShapeDtypeStruct((N_CHIP,) + w.shape, dtype) for w in ws], ws)


def _add_sibling(gs, gots, where, name):
    n = len(gs)
    halves = [(g.shape[1] // 2, g.shape[2]) for g in gs]

    def body(s_ref, *refs):
        for t in range(n):
            refs[2 * n + t][...] = (refs[t][...] + refs[n + t][...]).astype(BF16)

    return _prefetch_call(
        body, name, (N_CHIP,), where,
        [pl.BlockSpec((None, hr, cs), lambda k, s: (k, s[1], 0)) for hr, cs in halves]
        + [pl.BlockSpec((None, hr, cs), lambda k, s: (k, 0, 0)) for hr, cs in halves],
        [pl.BlockSpec((None, hr, cs), lambda k, s: (k, 0, 0)) for hr, cs in halves],
        [jax.ShapeDtypeStruct((N_CHIP, hr, cs), BF16) for hr, cs in halves], list(gs) + list(gots))


def _add_chips(sums, gots, where, name):
    n = len(sums)
    halves = [s.shape[1:] for s in sums]

    def body(s_ref, *refs):
        for t in range(n):
            own_ref, got_ref = refs[t], refs[n + t]
            refs[2 * n + t][...] = ((own_ref[...].astype(F32) + got_ref[0].astype(F32))
                                    + got_ref[1].astype(F32)) + got_ref[2].astype(F32)

    return _prefetch_call(
        body, name, (1,), where,
        [pl.BlockSpec((None, hr, cs), lambda i, s: (s[0], 0, 0)) for hr, cs in halves]
        + [pl.BlockSpec((3, hr, cs), lambda i, s: (0, 0, 0)) for hr, cs in halves],
        [pl.BlockSpec((hr, cs), lambda i, s: (s[1], 0)) for hr, cs in halves],
        [jax.ShapeDtypeStruct((2 * hr, cs), F32) for hr, cs in halves], list(sums) + list(gots))


def _small_allreduce(pack):
    rows = pack.shape[0]
    half = rows // 2

    def body(in_ref, out_ref, sib_ref, slots_ref, s_a, r_a, s_b, r_b, s_c, r_c):
        x, y, c, chips = _place()
        k_me = 2 * x + y
        sib = (x, y, 1 - c)
        first = _remote(in_ref, sib_ref, s_a, r_a, sib)
        first.start()
        first.wait()
        mine = _half(rows, c)
        slots_ref[k_me] = in_ref[mine, :] + sib_ref[mine, :]
        cps = [_remote(slots_ref.at[k_me], slots_ref.at[k_me], s_b.at[j], r_b.at[j], (*ch, c))
               for j, ch in enumerate(chips)]
        for cp in cps:
            cp.start()
        for j, ch in enumerate(chips):
            slot = slots_ref.at[_chip_index(ch)]
            _remote(slot, slot, s_b.at[j], r_b.at[j], (*ch, c)).wait_recv()
        for cp in cps:
            cp.wait_send()
        out_ref[mine, :] = ((slots_ref[0] + slots_ref[1]) + slots_ref[2]) + slots_ref[3]
        last = _remote(out_ref.at[mine, :], out_ref.at[mine, :], s_c, r_c, sib)
        last.start()
        theirs = out_ref.at[_half(rows, 1 - c), :]
        _remote(theirs, theirs, s_c, r_c, sib).wait_recv()
        last.wait_send()

    return pl.pallas_call(
        body, name="small_allreduce", in_specs=[_whole()], out_specs=_whole(),
        out_shape=jax.ShapeDtypeStruct(pack.shape, F32),
        scratch_shapes=[pltpu.VMEM(pack.shape, F32), pltpu.VMEM((N_CHIP, half, LANE), F32),
                        pltpu.SemaphoreType.DMA, pltpu.SemaphoreType.DMA,
                        pltpu.SemaphoreType.DMA((3,)), pltpu.SemaphoreType.DMA((3,)),
                        pltpu.SemaphoreType.DMA, pltpu.SemaphoreType.DMA],
        compiler_params=_params(0),
    )(pack)


def _adamw_update(w_ref, g_ref, m_ref, v_ref, d_ref, mo_ref, vo_ref):
    gv = g_ref[...]
    mn = ADAM_B1 * m_ref[...] + (1.0 - ADAM_B1) * gv
    vn = ADAM_B2 * v_ref[...] + (1.0 - ADAM_B2) * (gv * gv)
    mo_ref[...] = mn
    vo_ref[...] = vn
    m_hat = mn / (1.0 - ADAM_B1 ** ADAM_STEP)
    v_hat = vn / (1.0 - ADAM_B2 ** ADAM_STEP)
    d_ref[...] = -ADAM_LR * (m_hat / (jnp.sqrt(v_hat) + ADAM_EPS) + ADAM_WD * w_ref[...])


def _adamw(w, g, m, v, name, tr):
    rows, cols = w.shape
    blk = _rows(tr, cols)

    def body(w_ref, g_ref, m_ref, v_ref, go_ref, d_ref, mo_ref, vo_ref):
        go_ref[...] = g_ref[...]
        _adamw_update(w_ref, g_ref, m_ref, v_ref, d_ref, mo_ref, vo_ref)

    return _call(body, name, (rows // tr,), [blk] * 4, [blk] * 4,
                 [jax.ShapeDtypeStruct(w.shape, F32)] * 4, [w, g, m, v])[0]


def _adamw_many(ws, gs, ms, vs, name):
    n = len(ws)

    def body(*refs):
        for t in range(n):
            _adamw_update(*[refs[q * n + t] for q in range(7)])

    specs = [pl.BlockSpec(a.shape, lambda i, nd=a.ndim: (0,) * nd) for a in ws]
    outs = pl.pallas_call(
        body, name=name, grid=(1,), in_specs=specs * 4, out_specs=specs * 3,
        out_shape=[jax.ShapeDtypeStruct(a.shape, F32) for _ in range(3) for a in ws],
        compiler_params=_params(1),
    )(*ws, *gs, *ms, *vs)
    return outs[:n], outs[n:2 * n], outs[2 * n:]


def _ssm_discretize(a_re, a_im, log_dt, b_re, b_im):
    dt = jnp.exp(log_dt)[:, None]
    mag = jnp.exp(dt * a_re)
    abr = mag * jnp.cos(dt * a_im)
    abi = mag * jnp.sin(dt * a_im)
    den = a_re * a_re + a_im * a_im
    nr = abr - 1.0
    ni = abi
    f_re = (nr * a_re + ni * a_im) / den
    f_im = (ni * a_re - nr * a_im) / den
    bbr = f_re[..., None] * b_re - f_im[..., None] * b_im
    bbi = f_re[..., None] * b_im + f_im[..., None] * b_re
    return abr, abi, bbr, bbi


def _scan_tables(abr, abi):
    ar = abr.reshape(1, N_STATE)
    ai = abi.reshape(1, N_STATE)
    pr, pi = [ar], [ai]
    for _ in range(SUBLANE - 1):
        pr, pi = pr + [pr[-1] * ar - pi[-1] * ai], pi + [pr[-1] * ai + pi[-1] * ar]
    row = jnp.arange(SUBLANE)[:, None]
    tabs = []
    for d in (1, 2, 4):
        tabs.append(jnp.where(row >= d, pr[d - 1], 0.0))
        tabs.append(jnp.where(row >= d, pi[d - 1], 0.0))
    tabs.append(jnp.concatenate(pr, axis=0))
    tabs.append(jnp.concatenate(pi, axis=0))
    fwd = jnp.stack(tabs)
    sign = jnp.array([1.0, -1.0] * 4, F32)[:, None, None]
    return fwd, fwd[:, ::-1, :] * sign


def _block_diag_b(bb):
    strip = bb.transpose(2, 0, 1).reshape(SSM_H, N_STATE)
    rows = lax.broadcasted_iota(jnp.int32, (SSM_W, N_STATE), 0) // SSM_H
    cols = lax.broadcasted_iota(jnp.int32, (SSM_W, N_STATE), 1) // SSM_P
    return jnp.where(rows == cols, jnp.tile(strip, (SSM_G, 1)), 0.0).astype(BF16)


def _block_diag_c(cc):
    strip = cc.transpose(0, 2, 1).reshape(N_STATE, SSM_H)
    rows = lax.broadcasted_iota(jnp.int32, (N_STATE, SSM_W), 0) // SSM_P
    cols = lax.broadcasted_iota(jnp.int32, (N_STATE, SSM_W), 1) // SSM_H
    return jnp.where(rows == cols, jnp.tile(strip, (1, SSM_G)), 0.0).astype(BF16)


SMALL_SHAPES = {
    "g_mix": (D_MODEL,), "a_re": (SSM_G, SSM_P), "a_im": (SSM_G, SSM_P), "log_dt": (SSM_G,),
    "b_re": (SSM_G, SSM_P, SSM_H), "b_im": (SSM_G, SSM_P, SSM_H),
    "c_re": (SSM_G, SSM_H, SSM_P), "c_im": (SSM_G, SSM_H, SSM_P),
    "d_skip": (SSM_W,), "b_glu": (SSM_W,), "g_sgu": (SGU_W,), "w_s": (SGU_G, CHUNK, CHUNK),
    "b_s": (SGU_G, CHUNK), "g_ffn": (D_MODEL,), "conv_b": (2 * D_FF,), "g_final": (D_MODEL,),
}
PACK_ITEMS = [("loss", (1,))] + [(n, SMALL_SHAPES[n]) for n in SMALL] + [("conv_w", (3, 2 * D_FF))]
TILE = SUBLANE * LANE


def _item_rows(shape):
    return -(-math.prod(shape) // TILE) * SUBLANE


PACK_ROWS = -(-sum(_item_rows(s) for _, s in PACK_ITEMS) // (2 * SUBLANE)) * (2 * SUBLANE)


def _pack(values):
    parts, used = [], 0
    for name, shape in PACK_ITEMS:
        size, rows = math.prod(shape), _item_rows(shape)
        if name in values:
            flat = values[name].astype(F32).reshape(size)
            if rows * LANE > size:
                flat = jnp.pad(flat, (0, rows * LANE - size))
            parts.append(flat.reshape(rows, LANE))
        else:
            parts.append(jnp.zeros((rows, LANE), F32))
        used += rows
    if PACK_ROWS > used:
        parts.append(jnp.zeros((PACK_ROWS - used, LANE), F32))
    return jnp.concatenate(parts, axis=0)


def _unpack(pack):
    out, off = {}, 0
    for name, shape in PACK_ITEMS:
        rows = _item_rows(shape)
        out[name] = pack[off:off + rows].reshape(rows * LANE)[:math.prod(shape)].reshape(shape)
        off += rows
    return out


PLACE_ROWS = {"w_in": 256, "w_up": 256, "w_down": 352, "w_out": 256, "w_proj_a": 256,
              "w_proj_b": 256, "w_glu": 128}


def kernel(x, g_mix, w_in, a_re, a_im, log_dt, b_re, b_im, c_re, c_im, d_skip, w_glu, b_glu, w_proj_a, g_sgu, w_s, b_s, w_proj_b, w_out, g_ffn, w_up, conv_w, conv_b, w_down, g_final, loss_target, m_g_mix, m_w_in, m_a_re, m_a_im, m_log_dt, m_b_re, m_b_im, m_c_re, m_c_im, m_d_skip, m_w_glu, m_b_glu, m_w_proj_a, m_g_sgu, m_w_s, m_b_s, m_w_proj_b, m_w_out, m_g_ffn, m_w_up, m_conv_w, m_conv_b, m_w_down, m_g_final, v_g_mix, v_w_in, v_a_re, v_a_im, v_log_dt, v_b_re, v_b_im, v_c_re, v_c_im, v_d_skip, v_w_glu, v_b_glu, v_w_proj_a, v_g_sgu, v_w_s, v_b_s, v_w_proj_b, v_w_out, v_g_ffn, v_w_up, v_conv_w, v_conv_b, v_w_down, v_g_final):
    given = dict(locals())
    w = {n: given[n] for n in WEIGHTS}
    m = {n: given["m_" + n] for n in WEIGHTS}
    v = {n: given["v_" + n] for n in WEIGHTS}

    def shard2d(a):
        return a.reshape(a.shape[-2], a.shape[-1])

    chip = 2 * lax.axis_index("x") + lax.axis_index("y")
    where = jnp.stack([chip, lax.axis_index("c")]).astype(jnp.int32)
    xs, target = x[0], loss_target[0]
    small = {n: w[n].reshape(SMALL_SHAPES[n]) for n in SMALL}

    (abr, abi, bbr, bbi), disc_vjp = jax.vjp(_ssm_discretize, small["a_re"], small["a_im"],
                                             small["log_dt"], small["b_re"], small["b_im"])
    tab_f, tab_r = _scan_tables(abr, abi)
    bre = _block_diag_b(bbr)
    bim = _block_diag_b(bbi)
    cre = _block_diag_c(small["c_re"])
    cim = _block_diag_c(small["c_im"])
    tril = jnp.tril(jnp.ones((CHUNK, CHUNK), dtype=bool))
    ws = jnp.where(tril[None], small["w_s"], 0.0)
    ws_st = ws.reshape(SGU_G // 2, 2 * CHUNK, CHUNK).astype(BF16)
    wst_st = ws.transpose(0, 2, 1).reshape(SGU_G // 2, 2 * CHUNK, CHUNK).astype(BF16)
    bmat = jnp.repeat(small["b_s"].T, SGU_D, axis=1)
    g_mix2 = small["g_mix"].reshape(1, D_MODEL)
    g_ffn2 = small["g_ffn"].reshape(1, D_MODEL)
    g_final2 = small["g_final"].reshape(1, D_MODEL)
    g_sgu2 = small["g_sgu"].reshape(1, SGU_W)
    d_skip2 = small["d_skip"].reshape(1, SSM_W)
    b_glu2 = small["b_glu"].reshape(1, SSM_W)
    conv_b2 = small["conv_b"].reshape(1, 2 * D_FF)

    gat = {"w_in": _place_shard(shard2d(w["w_in"]), where, "place_w_in", BF16, PLACE_ROWS["w_in"])}
    gat.update(zip(BIG[1:], _place_shards([shard2d(w[n]) for n in BIG[1:]], where, "place_rest", BF16)))
    gat["conv_w"] = _place_shard(shard2d(w["conv_w"]), where, "place_conv_w", F32, 3)
    all_rows = (0, D_MODEL)
    (gat["w_in"],), = _comm("gather_in", [_job_gather(
        [gat["w_in"]], [(0, all_rows, ICI, (0.0, 0.5)), (0, all_rows, SIBLING, (0.5, 1.0))])])
    mixers = ["w_glu", "w_proj_a", "w_proj_b", "w_out"]
    rows = {n: (0, gat[n].shape[1]) for n in mixers}
    down_a, down_b = (0, D_FF // 8), (D_FF // 8, D_FF // 8)
    up_a, up_b = (0, 3 * D_MODEL // 8), (3 * D_MODEL // 8, 5 * D_MODEL // 8)
    span = (0.0, 1.0)

    names = mixers + ["conv_w", "w_down"]
    (p, h1, bur, bui), (got,) = _fwd_in(
        xs, g_mix2, gat["w_in"], bre, bim,
        [_job_gather([gat[n] for n in names],
                     [(i, rows[n], ICI, span) for i, n in enumerate(mixers)]
                     + [(4, None, ICI, span), (5, down_a, ICI, span)])])
    gat.update(zip(names, got))
    names = mixers + ["w_down", "w_up"]
    (str_, sti), (got,) = _scan_fwd(
        bur, bui, tab_f,
        [_job_gather([gat[n] for n in names],
                     [(i, rows[n], SIBLING, span) for i, n in enumerate(mixers)]
                     + [(4, down_a, SIBLING, span), (4, down_b, ICI, span), (5, up_a, ICI, span)])])
    gat.update(zip(names, got))
    w_glu_f = gat["w_glu"].reshape(SSM_W, SSM_W)
    w_out_f = gat["w_out"].reshape(D_MODEL, D_MODEL)
    conv_w_f = gat["conv_w"].transpose(1, 0, 2).reshape(3, 2 * D_FF)
    (x2, y0, z, mixed, ya, yb), ((gat["w_down"], gat["w_up"]),) = _fwd_mix(
        xs, p, str_, sti, cre, cim, d_skip2, w_glu_f, b_glu2, gat["w_proj_a"], g_sgu2, ws_st, bmat,
        gat["w_proj_b"], w_out_f,
        [_job_gather([gat["w_down"], gat["w_up"]],
                     [(0, down_b, SIBLING, span), (1, up_a, SIBLING, span),
                      (1, up_b, ICI, (0.0, 0.75)), (1, up_b, SIBLING, (0.75, 1.0))])])
    w_down_f = gat["w_down"].reshape(D_FF, D_MODEL)
    up, act, f, h2, dx3, sm_ffn = _fwd_ffn(x2, target, g_ffn2, gat["w_up"], conv_w_f, conv_b2,
                                           w_down_f, g_final2)

    def leg1_done(names, got):
        return _add_sibling([part[n] for n in names], got, where, "add_sibling_" + names[0])

    def leg2_done(names, sums, got):
        return _add_chips(sums, got, where, "add_chips_" + names[0])

    part, red = {}, {}
    part["w_down"] = _dw_rows(f, dx3, "dw_down", D_FF // 2, 4 * TK).reshape(
        N_CHIP, D_FF // N_CHIP, D_MODEL)
    (dx2, dup, sm_conv, sm_gffn), (got,) = _bwd_ffn(
        dx3, up, act, x2, g_ffn2, gat["w_up"], conv_w_f, w_down_f,
        [_job_sibling_halves([part["w_down"]])])
    sum_down = leg1_done(["w_down"], got)
    part["w_up"], (got,) = _dw_tiles(h2, dup, "dw_up", D_MODEL // 2, 2 * D_FF // N_CHIP,
                                     [_job_to_owner(sum_down)])
    red_down = leg2_done(["w_down"], sum_down, got)
    ((dsr, dsi, du_part, drest, mrg, dya, dyb, yap, dz, y1, sgu, dy0, sm_mix, dbm, dws),
     (got, (red["w_down"],))) = _bwd_mix(
        dx2, p, y0, z, mixed, ya, yb, w_out_f, gat["w_proj_a"], gat["w_proj_b"], w_glu_f, cre, cim,
        ws_st, wst_st, d_skip2, g_sgu2,
        [_job_sibling_halves([part["w_up"]]), _job_swap_halves(red_down)])
    sum_up = leg1_done(["w_up"], got)
    (lam_r, lam_i, dar8, dai8), (got,) = _scan_bwd(dsr, dsi, str_, sti, tab_r, [_job_to_owner(sum_up)])
    red_up = leg2_done(["w_up"], sum_up, got)
    mix4 = ["w_out", "w_proj_a", "w_proj_b", "w_glu"]
    w_glu_g, ((red["w_up"],),) = _dw_cols(y1, dz, "dw_glu", SSM_W, False, [_job_swap_halves(red_up)])
    part["w_glu"] = w_glu_g.reshape(N_CHIP, SSM_W // N_CHIP, SSM_W)
    part["w_proj_b"], (sib_glu,) = _dw_cols(sgu, dyb, "dw_proj_b", D_MODEL // N_CHIP, True,
                                            [_job_sibling_halves([part["w_glu"]])])
    part["w_proj_a"], (sib_pb,) = _dw_cols(yap, dya, "dw_proj_a", D_MODEL // N_CHIP, True,
                                           [_job_sibling_halves([part["w_proj_b"]])])
    w_out_g, (sib_pa,) = _dw_cols(mrg, dx2, "dw_out", D_MODEL // 2, False,
                                  [_job_sibling_halves([part["w_proj_a"]])])
    part["w_out"] = w_out_g.reshape(N_CHIP, D_MODEL // N_CHIP, D_MODEL)
    sib_out, = _comm("sibling_halves_w_out", [_job_sibling_halves([part["w_out"]])])
    got = sib_out + sib_pa + sib_pb + sib_glu
    (grad_x, dp, sm_gmix), _ = _bwd_in(
        lam_r, lam_i, du_part, drest, xs, dx2, g_mix2, gat["w_in"], bre, bim)
    sums_m = leg1_done(mix4, got)
    part["w_in"], (got,) = _dw_cols(h1, dp, "dw_in", IN_COLS // N_CHIP, True, [_job_to_owner(sums_m)])
    red_m = leg2_done(mix4, sums_m, got)
    (dbd_r, dbd_i), (got, done_m) = _dw_pair(
        p, SSM_W, lam_r, lam_i, "db_bar",
        [_job_sibling_halves([part["w_in"]]), _job_swap_halves(red_m)])
    red.update(zip(mix4, done_m))
    sum_in = leg1_done(["w_in"], got)
    (dcd_r, dcd_i), (got,) = _dw_pair(dy0, SSM_W, str_, sti, "dc", [_job_to_owner(sum_in)])
    red_in = leg2_done(["w_in"], sum_in, got)
    (red["w_in"],), = _comm("swap_w_in", [_job_swap_halves(red_in)])

    def pick_c(slabs):
        two = LANE // SSM_P
        return jnp.einsum("jshsp->jshp", slabs.reshape(SSM_G // two, two, SSM_H, two, SSM_P)
                          ).reshape(SSM_G, SSM_H, SSM_P)

    def pick_b(slabs):
        return pick_c(slabs).transpose(0, 2, 1)

    dabr = jnp.sum(dar8, axis=0).reshape(SSM_G, SSM_P)
    dabi = jnp.sum(dai8, axis=0).reshape(SSM_G, SSM_P)
    d_a_re, d_a_im, d_log_dt, d_b_re, d_b_im = disc_vjp((dabr, dabi, pick_b(dbd_r), pick_b(dbd_i)))
    gsmall = {
        "g_mix": sm_gmix[0], "a_re": d_a_re, "a_im": d_a_im, "log_dt": d_log_dt,
        "b_re": d_b_re, "b_im": d_b_im, "c_re": pick_c(dcd_r), "c_im": -pick_c(dcd_i),
        "d_skip": sm_mix[0], "b_glu": sm_mix[1], "g_sgu": sm_mix[2],
        "w_s": jnp.where(tril[None], dws, 0.0),
        "b_s": dbm.reshape(CHUNK, SGU_G, SGU_D).sum(-1).T,
        "g_ffn": sm_gffn[0], "conv_b": sm_conv[3], "g_final": sm_ffn[0],
        "conv_w": sm_conv[0:3], "loss": sm_ffn[1, 0:1],
    }

    total_pack = _small_allreduce(_pack(gsmall))
    total = _unpack(total_pack)
    grads = dict(red)
    cs = 2 * D_FF // N_CHIP
    grads["conv_w"] = lax.dynamic_slice(total["conv_w"], (0, chip * cs), (3, cs))
    delta, new_m, new_v = {}, {}, {}
    for n in BIG + ("conv_w",):
        grads[n], delta[n], new_m[n], new_v[n] = _adamw(
            shard2d(w[n]), grads[n], shard2d(m[n]), shard2d(v[n]), "adamw_" + n, PLACE_ROWS.get(n, 3))
    for n in SMALL:
        grads[n] = total[n].reshape(w[n].shape)
    ud, um, uv = _adamw_many(*[[d[n] for n in SMALL] for d in (w, grads, m, v)], "adamw_small")
    for i, n in enumerate(SMALL):
        delta[n], new_m[n], new_v[n] = ud[i], um[i], uv[i]

    def like(d):
        return [d[n].reshape(w[n].shape) for n in WEIGHTS]

    return (total["loss"].reshape(()), grad_x.reshape(x.shape), *like(grads), *like(delta),
            *like(new_m), *like(new_v))
```

```python
import math

import jax
import jax.numpy as jnp
from jax import lax
from jax.experimental import pallas as pl
from jax.experimental.pallas import tpu as pltpu

F32 = jnp.float32
BF16 = jnp.bfloat16
MESH = pl.DeviceIdType.MESH

D_MODEL = 1024
SSM_W = 512
SSM_G = 32
SSM_H = 16
SSM_P = 64
N_STATE = SSM_G * SSM_P
DIAG_N = 128 * SSM_P // SSM_H
SGU_W = 512
SGU_G = 8
SGU_D = 64
CHUNK = 128
D_FF = 2816
IN_COLS = 3584
EPS = 1e-6
N_CHIP = 4

ADAM_LR = 0.001
ADAM_B1 = 0.9
ADAM_B2 = 0.999
ADAM_EPS = 1e-08
ADAM_WD = 0.01
ADAM_STEP = 10

SUBLANE = 8
LANE = 128
VMEM_LIMIT = 56 * 1024 * 1024
TB = 256
TB_WIDE = 512
TK = 512
SCAN_LANES = 256
SCAN_UNROLL = 4
HALO = SUBLANE

BIG = ("w_in", "w_up", "w_down", "w_out", "w_proj_a", "w_proj_b", "w_glu")
SMALL = ("g_mix", "a_re", "a_im", "log_dt", "b_re", "b_im", "c_re", "c_im", "d_skip", "b_glu",
         "g_sgu", "w_s", "b_s", "g_ffn", "conv_b", "g_final")
WEIGHTS = ("g_mix", "w_in", "a_re", "a_im", "log_dt", "b_re", "b_im", "c_re", "c_im", "d_skip",
           "w_glu", "b_glu", "w_proj_a", "g_sgu", "w_s", "b_s", "w_proj_b", "w_out", "g_ffn",
           "w_up", "conv_w", "conv_b", "w_down", "g_final")

ANY = pl.BlockSpec(memory_space=pl.ANY)


def _params(n_grid):
    return pltpu.CompilerParams(dimension_semantics=("arbitrary",) * n_grid if n_grid else None,
                                vmem_limit_bytes=VMEM_LIMIT)


def _whole():
    return pl.BlockSpec(memory_space=pltpu.VMEM)


def _rows(tb, ncol):
    return pl.BlockSpec((tb, ncol), lambda i: (i, 0))


def _acc(nrow, ncol):
    return pl.BlockSpec((nrow, ncol), lambda i: (0, 0))


def _dot(a, b):
    return jnp.dot(a.astype(BF16), b.astype(BF16), preferred_element_type=F32)


def _dot_nt(a, b):
    return lax.dot_general(a.astype(BF16), b.astype(BF16), (((1,), (1,)), ((), ())),
                           preferred_element_type=F32)


def _sigmoid(v):
    return 0.5 * jnp.tanh(0.5 * v) + 0.5


_GELU_C = math.sqrt(2.0 / math.pi)


def _gelu(v):
    return 0.5 * v * (1.0 + jnp.tanh(_GELU_C * (v + 0.044715 * v * v * v)))


def _gelu_and_grad(v):
    v2 = v * v
    t = jnp.tanh(_GELU_C * v * (1.0 + 0.044715 * v2))
    half = 0.5 * (1.0 + t)
    return v * half, half + 0.5 * v * (1.0 - t * t) * _GELU_C * (1.0 + 3.0 * 0.044715 * v2)


def _rms_stats(v):
    r = lax.rsqrt(jnp.mean(v * v, axis=-1, keepdims=True) + EPS)
    return r, v * r


def _rms_bwd(dxh, xh, r):
    return r * (dxh - xh * jnp.mean(dxh * xh, axis=-1, keepdims=True))


def _place():
    x, y, c = lax.axis_index("x"), lax.axis_index("y"), lax.axis_index("c")
    chips = [(1 - x, y), (x, 1 - y), (1 - x, 1 - y)]
    return x, y, c, chips


def _chip_index(chip):
    return 2 * chip[0] + chip[1]


def _remote(src, dst, send_sem, recv_sem, device):
    return pltpu.make_async_remote_copy(src_ref=src, dst_ref=dst, send_sem=send_sem,
                                        recv_sem=recv_sem, device_id=device, device_id_type=MESH)


def _half(ref_rows, c):
    hr = ref_rows // 2
    return pl.ds(pl.multiple_of(c * hr, SUBLANE), hr)


class _Job:
    def __init__(self, hooks, n_sem, ins=(), inouts=(), outs=()):
        self.hooks, self.n_sem = list(hooks), n_sem
        self.ins, self.inouts, self.outs = list(ins), list(inouts), list(outs)


def _whole_span(start, finish):
    return [(0.0, "start", start), (1.0, "finish", finish)]


ICI, SIBLING = "ici", "sibling"


def _job_gather(bufs, legs):
    def copies(io, leg, first):
        b, window, kind, _ = legs[leg]
        x, y, c, chips = _place()
        k_me = 2 * x + y
        out = []
        for j, ch in enumerate(chips):
            k = _chip_index(ch)
            if window is None:
                src, land, dev = io[b].at[k_me], io[b].at[k], (*ch, c)
            else:
                r0, rows = window
                mine = pl.ds(pl.multiple_of(r0 + c * (rows // 2), SUBLANE), rows // 2)
                theirs = pl.ds(pl.multiple_of(r0 + (1 - c) * (rows // 2), SUBLANE), rows // 2)
                if kind == ICI:
                    src, land, dev = io[b].at[k_me, mine, :], io[b].at[k, mine, :], (*ch, c)
                else:
                    src, land, dev = io[b].at[k, mine, :], io[b].at[k, theirs, :], (x, y, 1 - c)
            out.append((src, land, first + j, dev))
        return out

    def starter(leg):
        def start(ins, io, outs, ssem, rsem):
            for src, _, i, dev in copies(io, leg, 3 * leg):
                _remote(src, src, ssem(i), rsem(i), dev).start()
        return start

    def finisher(leg):
        def finish(ins, io, outs, ssem, rsem):
            cps = copies(io, leg, 3 * leg)
            for _, land, i, dev in cps:
                _remote(land, land, ssem(i), rsem(i), dev).wait_recv()
            for src, _, i, dev in cps:
                _remote(src, src, ssem(i), rsem(i), dev).wait_send()
        return finish

    hooks = []
    for leg, (_, _, _, (begin, end)) in enumerate(legs):
        hooks += [(begin, "start", starter(leg)), (end, "finish", finisher(leg))]
    return _Job(hooks, 3 * len(legs), inouts=bufs)


def _job_sibling_halves(grads):
    n = len(grads)

    def build(ins, outs, ssem, rsem):
        x, y, c, _ = _place()
        return [_remote(ins[t].at[:, _half(grads[t].shape[1], 1 - c), :], outs[t], ssem(t), rsem(t),
                        (x, y, 1 - c)) for t in range(n)]

    def start(ins, io, outs, ssem, rsem):
        for cp in build(ins, outs, ssem, rsem):
            cp.start()

    def finish(ins, io, outs, ssem, rsem):
        for cp in build(ins, outs, ssem, rsem):
            cp.wait()

    return _Job(_whole_span(start, finish), n, ins=grads,
                outs=[jax.ShapeDtypeStruct((N_CHIP, g.shape[1] // 2, g.shape[2]), F32) for g in grads])


def _job_to_owner(sums):
    n = len(sums)

    def build(ins, outs, ssem, rsem):
        x, y, c, chips = _place()
        return [_remote(ins[t].at[_chip_index(ch)], outs[t].at[j], ssem(3 * t + j), rsem(3 * t + j),
                        (*ch, c)) for t in range(n) for j, ch in enumerate(chips)]

    def start(ins, io, outs, ssem, rsem):
        for cp in build(ins, outs, ssem, rsem):
            cp.start()

    def finish(ins, io, outs, ssem, rsem):
        for cp in build(ins, outs, ssem, rsem):
            cp.wait()

    return _Job(_whole_span(start, finish), 3 * n, ins=sums,
                outs=[jax.ShapeDtypeStruct((3,) + s.shape[1:], s.dtype) for s in sums])


def _job_swap_halves(bufs):
    n = len(bufs)

    def start(ins, io, outs, ssem, rsem):
        x, y, c, _ = _place()
        for t in range(n):
            mine = io[t].at[_half(bufs[t].shape[0], c), :]
            _remote(mine, mine, ssem(t), rsem(t), (x, y, 1 - c)).start()

    def finish(ins, io, outs, ssem, rsem):
        x, y, c, _ = _place()
        for t in range(n):
            theirs = io[t].at[_half(bufs[t].shape[0], 1 - c), :]
            _remote(theirs, theirs, ssem(t), rsem(t), (x, y, 1 - c)).wait_recv()
        for t in range(n):
            mine = io[t].at[_half(bufs[t].shape[0], c), :]
            _remote(mine, mine, ssem(t), rsem(t), (x, y, 1 - c)).wait_send()

    return _Job(_whole_span(start, finish), n, inouts=bufs)


def _call(body, name, grid, in_specs, out_specs, out_shape, args, jobs=(), scratch=()):
    n_in, n_out, n_scr = len(args), len(out_shape), len(scratch)
    job_in = [a for jb in jobs for a in jb.ins + jb.inouts]
    job_out = [s for jb in jobs
               for s in [jax.ShapeDtypeStruct(a.shape, a.dtype) for a in jb.inouts] + jb.outs]
    aliases, pos_in, pos_out = {}, n_in, n_out
    for jb in jobs:
        pos_in += len(jb.ins)
        for _ in jb.inouts:
            aliases[pos_in] = pos_out
            pos_in += 1
            pos_out += 1
        pos_out += len(jb.outs)
    n_sem = sum(jb.n_sem for jb in jobs)

    def wrapped(*refs):
        c_in = refs[:n_in]
        j_in = refs[n_in:n_in + len(job_in)]
        c_out = refs[n_in + len(job_in):n_in + len(job_in) + n_out]
        j_out = refs[n_in + len(job_in) + n_out:n_in + len(job_in) + n_out + len(job_out)]
        rest = refs[n_in + len(job_in) + n_out + len(job_out):]
        c_scr = rest[:n_scr]
        views, pi, po, ps = [], 0, 0, 0
        for jb in jobs:
            ins = j_in[pi:pi + len(jb.ins)]
            pi += len(jb.ins) + len(jb.inouts)
            io = j_out[po:po + len(jb.inouts)]
            new = j_out[po + len(jb.inouts):po + len(jb.inouts) + len(jb.outs)]
            po += len(jb.inouts) + len(jb.outs)
            send = (lambda i, o=ps: rest[n_scr].at[o + i])
            recv = (lambda i, o=ps: rest[n_scr + 1].at[o + i])
            ps += jb.n_sem
            views.append((ins, io, new, send, recv))

        def run(frac):
            for kind in ("finish", "start"):
                for jb, vw in zip(jobs, views):
                    for at, what, fn in jb.hooks:
                        if at == frac and what == kind:
                            fn(*vw)

        fracs = sorted({at for jb in jobs for at, _, _ in jb.hooks})
        if not grid:
            for frac in fracs:
                run(frac)
            return
        if jobs:
            assert len(grid) == 1 or set(fracs) <= {0.0, 1.0}
            first = pl.program_id(0) == 0
            last = pl.program_id(0) == grid[0] - 1
            for d in range(1, len(grid)):
                first = jnp.logical_and(first, pl.program_id(d) == 0)
                last = jnp.logical_and(last, pl.program_id(d) == grid[d] - 1)
            for frac in fracs:
                if frac < 1.0:
                    at_step = first if frac == 0.0 else pl.program_id(0) == int(frac * grid[0])
                    pl.when(at_step)(lambda frac=frac: run(frac))
        body(*c_in, *c_out, *c_scr)
        if jobs and 1.0 in fracs:
            pl.when(last)(lambda: run(1.0))

    sems = [pltpu.SemaphoreType.DMA((n_sem,)), pltpu.SemaphoreType.DMA((n_sem,))] if jobs else []
    kwargs = dict(grid=grid) if grid else {}
    res = pl.pallas_call(
        wrapped, name=name, in_specs=list(in_specs) + [ANY] * len(job_in),
        out_specs=list(out_specs) + [ANY] * len(job_out),
        out_shape=list(out_shape) + job_out, scratch_shapes=list(scratch) + sems,
        input_output_aliases=aliases, compiler_params=_params(len(grid)), **kwargs,
    )(*args, *job_in)
    outs, pos, per_job = list(res[:n_out]), n_out, []
    for jb in jobs:
        k = len(jb.inouts) + len(jb.outs)
        per_job.append(list(res[pos:pos + k]))
        pos += k
    return outs, per_job


def _comm(name, jobs):
    return _call(None, name, (), [], [], [], [], jobs)[1]


def _fwd_in(x, g_mix, w_in, bre, bim, jobs=()):
    t_len = x.shape[0]
    cs = IN_COLS // N_CHIP

    def body(x_ref, g_ref, w_ref, bre_ref, bim_ref, p_ref, h_ref, bur_ref, bui_ref):
        xv = x_ref[...]
        r, xh = _rms_stats(xv)
        h = (xh * g_ref[...]).astype(BF16)
        h_ref[...] = h
        for k in range(N_CHIP):
            p_ref[:, k * cs:(k + 1) * cs] = jnp.dot(h, w_ref[k],
                                                    preferred_element_type=F32).astype(BF16)
        u = p_ref[:, 0:SSM_W]
        for i in range(SSM_W // LANE):
            rows, cols = slice(i * LANE, (i + 1) * LANE), slice(i * DIAG_N, (i + 1) * DIAG_N)
            bur_ref[:, cols] = jnp.dot(u[:, rows], bre_ref[rows, cols],
                                       preferred_element_type=F32).astype(BF16)
            bui_ref[:, cols] = jnp.dot(u[:, rows], bim_ref[rows, cols],
                                       preferred_element_type=F32).astype(BF16)

    tb = min(TB_WIDE, t_len)
    return _call(
        body, "fwd_in", (t_len // tb,),
        [_rows(tb, D_MODEL), _whole(), _whole(), _whole(), _whole()],
        [_rows(tb, IN_COLS), _rows(tb, D_MODEL), _rows(tb, N_STATE), _rows(tb, N_STATE)],
        [jax.ShapeDtypeStruct((t_len, IN_COLS), BF16), jax.ShapeDtypeStruct((t_len, D_MODEL), BF16),
         jax.ShapeDtypeStruct((t_len, N_STATE), BF16), jax.ShapeDtypeStruct((t_len, N_STATE), BF16)],
        [x, g_mix, w_in, bre, bim], jobs)


def _scan_local(xr, xi, tab, shifts):
    for q, s in enumerate(shifts):
        ar, ai = tab[2 * q], tab[2 * q + 1]
        rr = pltpu.roll(xr, s, 0)
        ri = pltpu.roll(xi, s, 0)
        xr, xi = xr + ar * rr - ai * ri, xi + ar * ri + ai * rr
    return xr, xi


def _scan_carry(xr, xi, tab, cr, ci):
    pr, pi = tab[6], tab[7]
    return xr + pr * cr - pi * ci, xi + pr * ci + pi * cr


BF16_TILE = 2 * SUBLANE


def _load_blocks(r_ref, i_ref, base):
    out = []
    for q in range(SCAN_UNROLL // 2):
        rows = pl.ds(pl.multiple_of(base + q * BF16_TILE, BF16_TILE), BF16_TILE)
        vr, vi = r_ref[rows, :].astype(F32), i_ref[rows, :].astype(F32)
        out += [(vr[:SUBLANE], vi[:SUBLANE]), (vr[SUBLANE:], vi[SUBLANE:])]
    return out


def _store_blocks(r_ref, i_ref, base, blocks):
    for q in range(SCAN_UNROLL // 2):
        rows = pl.ds(pl.multiple_of(base + q * BF16_TILE, BF16_TILE), BF16_TILE)
        r_ref[rows, :] = jnp.concatenate([blocks[2 * q][0], blocks[2 * q + 1][0]], 0).astype(r_ref.dtype)
        i_ref[rows, :] = jnp.concatenate([blocks[2 * q][1], blocks[2 * q + 1][1]], 0).astype(i_ref.dtype)


def _scan_fwd(bur, bui, tab, jobs=()):
    t_len = bur.shape[0]
    nblk = t_len // SUBLANE
    lb = SCAN_LANES

    def body(br_ref, bi_ref, tab_ref, sr_ref, si_ref):
        tab_v = [tab_ref[q] for q in range(8)]

        def step(k, carry):
            cr, ci = carry
            base = pl.multiple_of(k * SCAN_UNROLL * SUBLANE, SCAN_UNROLL * SUBLANE)
            local = [_scan_local(xr, xi, tab_v, (1, 2, 4))
                     for xr, xi in _load_blocks(br_ref, bi_ref, base)]
            done = []
            for xr, xi in local:
                xr, xi = _scan_carry(xr, xi, tab_v, cr, ci)
                done.append((xr, xi))
                cr, ci = xr[SUBLANE - 1:SUBLANE, :], xi[SUBLANE - 1:SUBLANE, :]
            _store_blocks(sr_ref, si_ref, base, done)
            return cr, ci

        zero = jnp.zeros((1, lb), F32)
        lax.fori_loop(0, nblk // SCAN_UNROLL, step, (zero, zero))

    col = pl.BlockSpec((t_len, lb), lambda j: (0, j))
    return _call(
        body, "scan_fwd", (N_STATE // lb,),
        [col, col, pl.BlockSpec((8, SUBLANE, lb), lambda j: (0, 0, j))], [col, col],
        [jax.ShapeDtypeStruct((t_len, N_STATE), BF16)] * 2, [bur, bui, tab], jobs)


def _sgu_mix(v, ws_ref, lane_lo):
    rows = []
    for c0 in range(0, v.shape[0], CHUNK):
        slabs = []
        for j in range(SGU_W // LANE):
            prod = jnp.dot(ws_ref[j], v[c0:c0 + CHUNK, j * LANE:(j + 1) * LANE].astype(BF16),
                           preferred_element_type=F32)
            slabs.append(jnp.where(lane_lo, prod[:CHUNK], prod[CHUNK:]))
        rows.append(jnp.concatenate(slabs, axis=1))
    return jnp.concatenate(rows, axis=0) if len(rows) > 1 else rows[0]


def _fwd_mix(x, p, str_, sti, cre, cim, d_skip, w_glu, b_glu, w_pa, g_sgu, ws_st, bmat, w_pb, w_out,
             jobs=()):
    t_len = x.shape[0]

    def body(x_ref, p_ref, sr_ref, si_ref, cre_ref, cim_ref, dsk_ref, wg_ref, bg_ref, wpa_ref,
             gs_ref, ws_ref, bm_ref, wpb_ref, wo_ref,
             x2_ref, y0_ref, z_ref, mx_ref, ya_ref, yb_ref):
        u = p_ref[:, 0:SSM_W].astype(F32)
        y0 = jnp.concatenate(
            [_dot(sr_ref[:, i * DIAG_N:(i + 1) * DIAG_N],
                  cre_ref[i * DIAG_N:(i + 1) * DIAG_N, i * LANE:(i + 1) * LANE])
             - _dot(si_ref[:, i * DIAG_N:(i + 1) * DIAG_N],
                    cim_ref[i * DIAG_N:(i + 1) * DIAG_N, i * LANE:(i + 1) * LANE])
             for i in range(SSM_W // LANE)], axis=1) + dsk_ref[...] * u
        y0_ref[...] = y0.astype(BF16)
        y1 = _gelu(y0)
        z = _dot(y1, wg_ref[...]) + bg_ref[...]
        z_ref[...] = z.astype(BF16)
        ya_pre = (y1 * _sigmoid(z)).astype(BF16)
        ya = jnp.concatenate([jnp.dot(ya_pre, wpa_ref[k], preferred_element_type=F32)
                              for k in range(N_CHIP)], axis=1)
        ya_ref[...] = ya.astype(BF16)

        uvg = _gelu(p_ref[:, SSM_W:SSM_W + 2 * SGU_W].astype(F32))
        u2 = uvg[:, :SGU_W]
        _, vh = _rms_stats(uvg[:, SGU_W:])
        v3 = vh * gs_ref[...]
        lane_lo = lax.broadcasted_iota(jnp.int32, (CHUNK, LANE), 1) < SGU_D
        bias = jnp.concatenate([bm_ref[...]] * (TB // CHUNK), axis=0)
        mixed = _sgu_mix(v3, ws_ref, lane_lo) + bias
        mx_ref[...] = mixed.astype(BF16)
        sgu = (u2 * mixed).astype(BF16)
        yb = jnp.concatenate([jnp.dot(sgu, wpb_ref[k], preferred_element_type=F32)
                              for k in range(N_CHIP)], axis=1)
        yb_ref[...] = yb.astype(BF16)

        lg0 = SSM_W + 2 * SGU_W
        ga = _sigmoid(p_ref[:, lg0:lg0 + D_MODEL].astype(F32))
        gb = _sigmoid(p_ref[:, lg0 + D_MODEL:lg0 + 2 * D_MODEL].astype(F32))
        mrg = ga * ya + gb * yb
        x2_ref[...] = x_ref[...] + _dot(mrg, wo_ref[...])

    return _call(
        body, "fwd_mix", (t_len // TB,),
        [_rows(TB, D_MODEL), _rows(TB, IN_COLS), _rows(TB, N_STATE), _rows(TB, N_STATE)]
        + [_whole()] * 11,
        [_rows(TB, D_MODEL), _rows(TB, SSM_W), _rows(TB, SSM_W), _rows(TB, SGU_W),
         _rows(TB, D_MODEL), _rows(TB, D_MODEL)],
        [jax.ShapeDtypeStruct((t_len, D_MODEL), F32), jax.ShapeDtypeStruct((t_len, SSM_W), BF16),
         jax.ShapeDtypeStruct((t_len, SSM_W), BF16), jax.ShapeDtypeStruct((t_len, SGU_W), BF16),
         jax.ShapeDtypeStruct((t_len, D_MODEL), BF16), jax.ShapeDtypeStruct((t_len, D_MODEL), BF16)],
        [x, p, str_, sti, cre, cim, d_skip, w_glu, b_glu, w_pa, g_sgu, ws_st, bmat, w_pb, w_out], jobs)


def _conv_taps(v, cw_ref, c0, width):
    w0 = cw_ref[0:1, c0:c0 + width]
    w1 = cw_ref[1:2, c0:c0 + width]
    w2 = cw_ref[2:3, c0:c0 + width]
    return w0 * pltpu.roll(v, 2, 0) + w1 * pltpu.roll(v, 1, 0) + w2 * v


def _fwd_ffn(x2, target, g_ffn, w_up, conv_w, conv_b, w_down, g_final):
    t_len = x2.shape[0]
    half = D_FF // 2
    blocks_per_halo = TB // HALO

    def body(x2_ref, xp_ref, tg_ref, gf_ref, wu_ref, cw_ref, cb_ref, wd_ref, gl_ref,
             up_ref, act_ref, f_ref, h2_ref, dx3_ref, sm_ref):
        i = pl.program_id(0)
        xe = jnp.concatenate([xp_ref[...] * jnp.where(i == 0, 0.0, 1.0), x2_ref[...]], axis=0)
        _, xh = _rms_stats(xe)
        h2 = (xh * gf_ref[...]).astype(BF16)
        h2_ref[...] = h2[HALO:]
        acc = jnp.zeros((TB, D_MODEL), F32)
        ups = [jnp.dot(h2, wu_ref[k], preferred_element_type=F32) for k in range(N_CHIP)]
        for hc in range(2):
            ca = hc * half
            cb = D_FF + hc * half
            ua, ub = ups[hc], ups[2 + hc]
            up_ref[:, ca:ca + half] = ua[HALO:].astype(BF16)
            up_ref[:, cb:cb + half] = ub[HALO:].astype(BF16)
            ac = _conv_taps(ua, cw_ref, ca, half)[HALO:] + cb_ref[:, ca:ca + half]
            bc = _conv_taps(ub, cw_ref, cb, half)[HALO:] + cb_ref[:, cb:cb + half]
            act_ref[:, ca:ca + half] = ac.astype(BF16)
            act_ref[:, cb:cb + half] = bc.astype(BF16)
            f = (ac * _sigmoid(ac) * bc).astype(BF16)
            f_ref[:, ca:ca + half] = f
            acc = acc + jnp.dot(f, wd_ref[ca:ca + half, :], preferred_element_type=F32)
        x3 = x2_ref[...] + acc
        r3, xh3 = _rms_stats(x3)
        err = xh3 * gl_ref[...] - tg_ref[...]
        dout = err * (1.0 / D_MODEL)
        dx3_ref[...] = _rms_bwd(dout * gl_ref[...], xh3, r3)
        dgl = jnp.sum(dout * xh3, axis=0, keepdims=True)
        loss = 0.5 * jnp.sum(jnp.mean(err * err, axis=-1, keepdims=True), axis=0, keepdims=True)
        upd = jnp.concatenate([dgl, jnp.broadcast_to(loss, (1, D_MODEL)),
                               jnp.zeros((SUBLANE - 2, D_MODEL), F32)], axis=0)

        @pl.when(i == 0)
        def _():
            sm_ref[...] = upd

        @pl.when(i > 0)
        def _():
            sm_ref[...] += upd

    prev = pl.BlockSpec((HALO, D_MODEL), lambda i: (jnp.maximum(i * blocks_per_halo - 1, 0), 0))
    return _call(
        body, "fwd_ffn", (t_len // TB,),
        [_rows(TB, D_MODEL), prev, _rows(TB, D_MODEL)] + [_whole()] * 6,
        [_rows(TB, 2 * D_FF), _rows(TB, 2 * D_FF), _rows(TB, D_FF), _rows(TB, D_MODEL),
         _rows(TB, D_MODEL), _acc(SUBLANE, D_MODEL)],
        [jax.ShapeDtypeStruct((t_len, 2 * D_FF), BF16), jax.ShapeDtypeStruct((t_len, 2 * D_FF), BF16),
         jax.ShapeDtypeStruct((t_len, D_FF), BF16), jax.ShapeDtypeStruct((t_len, D_MODEL), BF16),
         jax.ShapeDtypeStruct((t_len, D_MODEL), F32), jax.ShapeDtypeStruct((SUBLANE, D_MODEL), F32)],
        [x2, x2, target, g_ffn, w_up, conv_w, conv_b, w_down, g_final])[0]


def _bwd_ffn(dx3, up, act, x2, g_ffn, w_up, conv_w, w_down, jobs=()):
    t_len = x2.shape[0]
    half = D_FF // 2
    nblk = t_len // TB
    halo_b = 2 * HALO
    n_e = TB + HALO

    def body(dx_ref, dxn_ref, up_ref, act_ref, actn_ref, x2_ref, gf_ref, wu_ref, cw_ref,
             wd_ref, dx2_ref, dup_ref, smw_ref, smg_ref):
        i = pl.program_id(0)
        keep_last = jnp.where(i == nblk - 1, 0.0, 1.0)
        dxe = jnp.concatenate([dx_ref[...], dxn_ref[...] * keep_last], axis=0).astype(BF16)
        dh2 = jnp.zeros((TB, D_MODEL), F32)
        zpad = jnp.zeros((1, half), F32)
        dfs = [lax.dot_general(dxe, wd_ref[hc * half:(hc + 1) * half, :], (((1,), (1,)), ((), ())),
                               preferred_element_type=F32) for hc in range(2)]
        for hc in range(2):
            ca = hc * half
            cb = D_FF + hc * half
            ac = jnp.concatenate([act_ref[:, ca:ca + half].astype(F32),
                                  actn_ref[:, ca:ca + half].astype(F32)[:HALO]], axis=0)
            bc = jnp.concatenate([act_ref[:, cb:cb + half].astype(F32),
                                  actn_ref[:, cb:cb + half].astype(F32)[:HALO]], axis=0)
            wa = [cw_ref[k:k + 1, ca:ca + half] for k in range(3)]
            wb = [cw_ref[k:k + 1, cb:cb + half] for k in range(3)]
            df = dfs[hc]
            sg = _sigmoid(ac)
            da = df * bc * sg * (1.0 + ac * (1.0 - sg))
            db = df * ac * sg
            da1, da2 = pltpu.roll(da, n_e - 1, 0), pltpu.roll(da, n_e - 2, 0)
            db1, db2 = pltpu.roll(db, n_e - 1, 0), pltpu.roll(db, n_e - 2, 0)
            dua = (wa[2] * da + wa[1] * da1 + wa[0] * da2)[:TB]
            dub = (wb[2] * db + wb[1] * db1 + wb[0] * db2)[:TB]
            dup_ref[:, ca:ca + half] = dua.astype(BF16)
            dup_ref[:, cb:cb + half] = dub.astype(BF16)
            dh2 = dh2 + _dot_nt(dua, wu_ref[hc]) + _dot_nt(dub, wu_ref[2 + hc])
            rows = []
            for u_, d0, d1, d2 in ((up_ref[:, ca:ca + half].astype(F32), da, da1, da2),
                                   (up_ref[:, cb:cb + half].astype(F32), db, db1, db2)):
                rows.append([jnp.sum(u_ * d2[:TB], axis=0, keepdims=True),
                             jnp.sum(u_ * d1[:TB], axis=0, keepdims=True),
                             jnp.sum(u_ * d0[:TB], axis=0, keepdims=True),
                             jnp.sum(d0[:TB], axis=0, keepdims=True)])
            for c0, rws in ((ca, rows[0]), (cb, rows[1])):
                upd = jnp.concatenate(rws + [zpad] * (SUBLANE - 4), axis=0)

                @pl.when(i == 0)
                def _(upd=upd, c0=c0):
                    smw_ref[:, c0:c0 + half] = upd

                @pl.when(i > 0)
                def _(upd=upd, c0=c0):
                    smw_ref[:, c0:c0 + half] += upd

        r2, xh2 = _rms_stats(x2_ref[...])
        dx2_ref[...] = dx_ref[...] + _rms_bwd(dh2 * gf_ref[...], xh2, r2)
        updg = jnp.concatenate([jnp.sum(dh2 * xh2, axis=0, keepdims=True),
                                jnp.zeros((SUBLANE - 1, D_MODEL), F32)], axis=0)

        @pl.when(i == 0)
        def _():
            smg_ref[...] = updg

        @pl.when(i > 0)
        def _():
            smg_ref[...] += updg

    nxt_d = pl.BlockSpec((HALO, D_MODEL),
                         lambda i: (jnp.minimum((i + 1) * (TB // HALO), t_len // HALO - 1), 0))
    nxt_a = pl.BlockSpec((halo_b, 2 * D_FF),
                         lambda i: (jnp.minimum((i + 1) * (TB // halo_b), t_len // halo_b - 1), 0))
    return _call(
        body, "bwd_ffn", (nblk,),
        [_rows(TB, D_MODEL), nxt_d, _rows(TB, 2 * D_FF), _rows(TB, 2 * D_FF), nxt_a,
         _rows(TB, D_MODEL)] + [_whole()] * 4,
        [_rows(TB, D_MODEL), _rows(TB, 2 * D_FF), _acc(SUBLANE, 2 * D_FF), _acc(SUBLANE, D_MODEL)],
        [jax.ShapeDtypeStruct((t_len, D_MODEL), F32), jax.ShapeDtypeStruct((t_len, 2 * D_FF), BF16),
         jax.ShapeDtypeStruct((SUBLANE, 2 * D_FF), F32), jax.ShapeDtypeStruct((SUBLANE, D_MODEL), F32)],
        [dx3, dx3, up, act, act, x2, g_ffn, w_up, conv_w, w_down], jobs)


def _bwd_mix(dx2, p, y0, z, mixed, ya, yb, w_out, w_pa, w_pb, w_glu, cre, cim, ws_st, wst_st,
             d_skip, g_sgu, jobs=()):
    t_len = dx2.shape[0]
    pc = D_MODEL // N_CHIP
    n_slab = SGU_W // LANE

    def body(dx_ref, p_ref, y0_ref, z_ref, mx_ref, ya_ref, yb_ref, wo_ref, wpa_ref, wpb_ref,
             wg_ref, cre_ref, cim_ref, ws_ref, wst_ref, dsk_ref, gs_ref,
             dsr_ref, dsi_ref, du_ref, drest_ref, mrg_ref, dya_ref, dyb_ref, yap_ref, dz_ref,
             y1_ref, sgu_ref, dy0_ref, sm_ref, dbm_ref, dws_ref):
        i = pl.program_id(0)
        first = i == 0
        lg0 = SSM_W + 2 * SGU_W
        ga = _sigmoid(p_ref[:, lg0:lg0 + D_MODEL].astype(F32))
        gb = _sigmoid(p_ref[:, lg0 + D_MODEL:lg0 + 2 * D_MODEL].astype(F32))
        yav = ya_ref[...].astype(F32)
        ybv = yb_ref[...].astype(F32)
        mrg_ref[...] = (ga * yav + gb * ybv).astype(BF16)
        y0v = y0_ref[...].astype(F32)
        y1, y1_grad = _gelu_and_grad(y0v)
        sz = _sigmoid(z_ref[...].astype(F32))
        y1_ref[...] = y1.astype(BF16)
        yap_ref[...] = (y1 * sz).astype(BF16)

        dmrg = _dot_nt(dx_ref[...], wo_ref[...])
        drest_ref[:, 2 * SGU_W:2 * SGU_W + D_MODEL] = (dmrg * yav * ga * (1.0 - ga)).astype(BF16)
        drest_ref[:, 2 * SGU_W + D_MODEL:] = (dmrg * ybv * gb * (1.0 - gb)).astype(BF16)
        dya = (dmrg * ga).astype(BF16)
        dyb = (dmrg * gb).astype(BF16)
        dya_ref[...] = dya
        dyb_ref[...] = dyb

        dyap = jnp.zeros((TB, SSM_W), F32)
        for k in range(N_CHIP):
            dyap = dyap + _dot_nt(dya[:, k * pc:(k + 1) * pc], wpa_ref[k])
        dz = dyap * y1 * sz * (1.0 - sz)
        dz_ref[...] = dz.astype(BF16)
        dy0 = (dyap * sz + _dot_nt(dz, wg_ref[...])) * y1_grad
        dy0_ref[...] = dy0.astype(BF16)
        u = p_ref[:, 0:SSM_W].astype(F32)
        du_ref[...] = dy0 * dsk_ref[...]
        for q in range(SSM_W // LANE):
            rows, cols = slice(q * DIAG_N, (q + 1) * DIAG_N), slice(q * LANE, (q + 1) * LANE)
            dsr_ref[:, rows] = _dot_nt(dy0[:, cols], cre_ref[rows, cols]).astype(BF16)
            dsi_ref[:, rows] = (-_dot_nt(dy0[:, cols], cim_ref[rows, cols])).astype(BF16)

        uv = p_ref[:, SSM_W:lg0].astype(F32)
        uvg, gg = _gelu_and_grad(uv)
        u2 = uvg[:, :SGU_W]
        rv, vh = _rms_stats(uvg[:, SGU_W:])
        v3 = vh * gs_ref[...]
        mixed = mx_ref[...].astype(F32)
        dsgu = jnp.zeros((TB, SGU_W), F32)
        for k in range(N_CHIP):
            dsgu = dsgu + _dot_nt(dyb[:, k * pc:(k + 1) * pc], wpb_ref[k])
        sgu_ref[...] = (u2 * mixed).astype(BF16)
        drest_ref[:, 0:SGU_W] = (dsgu * mixed * gg[:, :SGU_W]).astype(BF16)
        dmix = dsgu * u2
        lane_lo = lax.broadcasted_iota(jnp.int32, (CHUNK, LANE), 1) < SGU_D
        dv3 = _sgu_mix(dmix, wst_ref, lane_lo)
        dbm = jnp.zeros((CHUNK, SGU_W), F32)
        for c0 in range(0, TB, CHUNK):
            dbm = dbm + dmix[c0:c0 + CHUNK]
        for j in range(n_slab):
            lo = jnp.zeros((CHUNK, CHUNK), F32)
            hi = jnp.zeros((CHUNK, CHUNK), F32)
            for c0 in range(0, TB, CHUNK):
                dsl = dmix[c0:c0 + CHUNK, j * LANE:(j + 1) * LANE]
                vsl = v3[c0:c0 + CHUNK, j * LANE:(j + 1) * LANE]
                lo = lo + _dot_nt(jnp.where(lane_lo, dsl, 0.0), vsl)
                hi = hi + _dot_nt(jnp.where(lane_lo, 0.0, dsl), vsl)

            @pl.when(first)
            def _(lo=lo, hi=hi, j=j):
                dws_ref[2 * j] = lo
                dws_ref[2 * j + 1] = hi

            @pl.when(jnp.logical_not(first))
            def _(lo=lo, hi=hi, j=j):
                dws_ref[2 * j] += lo
                dws_ref[2 * j + 1] += hi

        dv2 = _rms_bwd(dv3 * gs_ref[...], vh, rv)
        drest_ref[:, SGU_W:2 * SGU_W] = (dv2 * gg[:, SGU_W:]).astype(BF16)

        upd = jnp.concatenate([jnp.sum(dy0 * u, axis=0, keepdims=True),
                               jnp.sum(dz, axis=0, keepdims=True),
                               jnp.sum(dv3 * vh, axis=0, keepdims=True),
                               jnp.zeros((SUBLANE - 3, SSM_W), F32)], axis=0)

        @pl.when(first)
        def _():
            sm_ref[...] = upd
            dbm_ref[...] = dbm

        @pl.when(jnp.logical_not(first))
        def _():
            sm_ref[...] += upd
            dbm_ref[...] += dbm

    rest = 2 * SGU_W + 2 * D_MODEL
    bf_d, bf_s = jax.ShapeDtypeStruct((t_len, D_MODEL), BF16), jax.ShapeDtypeStruct((t_len, SSM_W), BF16)
    return _call(
        body, "bwd_mix", (t_len // TB,),
        [_rows(TB, D_MODEL), _rows(TB, IN_COLS), _rows(TB, SSM_W), _rows(TB, SSM_W),
         _rows(TB, SGU_W), _rows(TB, D_MODEL), _rows(TB, D_MODEL)] + [_whole()] * 10,
        [_rows(TB, N_STATE), _rows(TB, N_STATE), _rows(TB, SSM_W), _rows(TB, rest),
         _rows(TB, D_MODEL), _rows(TB, D_MODEL), _rows(TB, D_MODEL), _rows(TB, SSM_W),
         _rows(TB, SSM_W), _rows(TB, SSM_W), _rows(TB, SGU_W), _rows(TB, SSM_W),
         _acc(SUBLANE, SSM_W), _acc(CHUNK, SGU_W),
         pl.BlockSpec((SGU_G, CHUNK, CHUNK), lambda i: (0, 0, 0))],
        [jax.ShapeDtypeStruct((t_len, N_STATE), BF16), jax.ShapeDtypeStruct((t_len, N_STATE), BF16),
         jax.ShapeDtypeStruct((t_len, SSM_W), F32), jax.ShapeDtypeStruct((t_len, rest), BF16),
         bf_d, bf_d, bf_d, bf_s, bf_s, bf_s, bf_s, bf_s,
         jax.ShapeDtypeStruct((SUBLANE, SSM_W), F32), jax.ShapeDtypeStruct((CHUNK, SGU_W), F32),
         jax.ShapeDtypeStruct((SGU_G, CHUNK, CHUNK), F32)],
        [dx2, p, y0, z, mixed, ya, yb, w_out, w_pa, w_pb, w_glu, cre, cim, ws_st, wst_st, d_skip,
         g_sgu], jobs)


def _scan_bwd(dsr, dsi, str_, sti, tab_rev, jobs=()):
    t_len = dsr.shape[0]
    nblk = t_len // SUBLANE
    lb = SCAN_LANES

    def body(dr_ref, di_ref, sr_ref, si_ref, tab_ref, lr_ref, li_ref, dar_ref, dai_ref):
        tab_v = [tab_ref[q] for q in range(8)]
        row0 = lax.broadcasted_iota(jnp.int32, (SUBLANE, lb), 0) == 0
        tile = BF16_TILE

        def step(k, carry):
            cr, ci, acr, aci = carry
            base = pl.multiple_of((nblk - (k + 1) * SCAN_UNROLL) * SUBLANE, SCAN_UNROLL * SUBLANE)
            state = _load_blocks(sr_ref, si_ref, base)
            before = pl.ds(pl.multiple_of(jnp.maximum(base - tile, 0), tile), tile)
            has_before = jnp.where(base > 0, 1.0, 0.0)
            prev = (sr_ref[before, :].astype(F32)[tile - 1:tile] * has_before,
                    si_ref[before, :].astype(F32)[tile - 1:tile] * has_before)
            local = [_scan_local(xr, xi, tab_v, (7, 6, 4))
                     for xr, xi in _load_blocks(dr_ref, di_ref, base)]
            lam = [None] * SCAN_UNROLL
            for b in reversed(range(SCAN_UNROLL)):
                xr, xi = _scan_carry(*local[b], tab_v, cr, ci)
                lam[b] = (xr, xi)
                cr, ci = xr[0:1, :], xi[0:1, :]
                pr, pi = prev if b == 0 else (state[b - 1][0][SUBLANE - 1:], state[b - 1][1][SUBLANE - 1:])
                s_r = jnp.where(row0, pr, pltpu.roll(state[b][0], 1, 0))
                s_i = jnp.where(row0, pi, pltpu.roll(state[b][1], 1, 0))
                acr = acr + xr * s_r + xi * s_i
                aci = aci + xi * s_r - xr * s_i
            _store_blocks(lr_ref, li_ref, base, lam)
            return cr, ci, acr, aci

        zero = jnp.zeros((1, lb), F32)
        zacc = jnp.zeros((SUBLANE, lb), F32)
        _, _, acr, aci = lax.fori_loop(0, nblk // SCAN_UNROLL, step, (zero, zero, zacc, zacc))
        dar_ref[...] = acr
        dai_ref[...] = aci

    col = pl.BlockSpec((t_len, lb), lambda j: (0, j))
    small = pl.BlockSpec((SUBLANE, lb), lambda j: (0, j))
    return _call(
        body, "scan_bwd", (N_STATE // lb,),
        [col, col, col, col, pl.BlockSpec((8, SUBLANE, lb), lambda j: (0, 0, j))],
        [col, col, small, small],
        [jax.ShapeDtypeStruct((t_len, N_STATE), BF16)] * 2
        + [jax.ShapeDtypeStruct((SUBLANE, N_STATE), F32)] * 2,
        [dsr, dsi, str_, sti, tab_rev], jobs)


def _bwd_in(lam_r, lam_i, du_part, drest, x, dx2, g_mix, w_in, bre, bim, jobs=()):
    t_len = x.shape[0]
    cs = IN_COLS // N_CHIP

    def body(lr_ref, li_ref, du_ref, dr_ref, x_ref, dx2_ref, g_ref, w_ref, bre_ref, bim_ref,
             gx_ref, dp_ref, sm_ref):
        i = pl.program_id(0)
        du = du_ref[...] + jnp.concatenate(
            [_dot_nt(lr_ref[:, i * DIAG_N:(i + 1) * DIAG_N],
                     bre_ref[i * LANE:(i + 1) * LANE, i * DIAG_N:(i + 1) * DIAG_N])
             + _dot_nt(li_ref[:, i * DIAG_N:(i + 1) * DIAG_N],
                       bim_ref[i * LANE:(i + 1) * LANE, i * DIAG_N:(i + 1) * DIAG_N])
             for i in range(SSM_W // LANE)], axis=1)
        dp_ref[:, 0:SSM_W] = du.astype(BF16)
        dp_ref[:, SSM_W:] = dr_ref[...]
        dh = jnp.zeros(x_ref.shape, F32)
        for k in range(N_CHIP):
            dh = dh + _dot_nt(dp_ref[:, k * cs:(k + 1) * cs], w_ref[k])
        r, xh = _rms_stats(x_ref[...])
        gx_ref[...] = dx2_ref[...] + _rms_bwd(dh * g_ref[...], xh, r)
        upd = jnp.concatenate([jnp.sum(dh * xh, axis=0, keepdims=True),
                               jnp.zeros((SUBLANE - 1, D_MODEL), F32)], axis=0)

        @pl.when(i == 0)
        def _():
            sm_ref[...] = upd

        @pl.when(i > 0)
        def _():
            sm_ref[...] += upd

    tb = min(TB_WIDE, t_len)
    return _call(
        body, "bwd_in", (t_len // tb,),
        [_rows(tb, N_STATE), _rows(tb, N_STATE), _rows(tb, SSM_W), _rows(tb, IN_COLS - SSM_W),
         _rows(tb, D_MODEL), _rows(tb, D_MODEL)] + [_whole()] * 4,
        [_rows(tb, D_MODEL), _rows(tb, IN_COLS), _acc(SUBLANE, D_MODEL)],
        [jax.ShapeDtypeStruct((t_len, D_MODEL), F32), jax.ShapeDtypeStruct((t_len, IN_COLS), BF16),
         jax.ShapeDtypeStruct((SUBLANE, D_MODEL), F32)],
        [lam_r, lam_i, du_part, drest, x, dx2, g_mix, w_in, bre, bim], jobs)


def _matmul_tn(a, b, name, out_shape, grid_ij, a_blk, a_map, b_blk, b_map, o_blk, o_map, jobs=()):
    tk = a_blk[0]
    nk = a.shape[0] // tk
    assert nk * tk == a.shape[0] and nk > 0

    def body(a_ref, b_ref, o_ref, acc_ref):
        k = pl.program_id(2)

        @pl.when(k == 0)
        def _():
            acc_ref[...] = jnp.zeros_like(acc_ref)

        acc_ref[...] += lax.dot_general(a_ref[...].astype(BF16), b_ref[...].astype(BF16),
                                        (((0,), (0,)), ((), ())), preferred_element_type=F32)

        @pl.when(k == nk - 1)
        def _():
            o_ref[...] = acc_ref[...]

    outs, per_job = _call(
        body, name, (grid_ij[0], grid_ij[1], nk),
        [pl.BlockSpec(a_blk, a_map), pl.BlockSpec(b_blk, b_map)], [pl.BlockSpec(o_blk, o_map)],
        [jax.ShapeDtypeStruct(out_shape, F32)], [a, b], jobs,
        scratch=[pltpu.VMEM((a_blk[1], b_blk[1]), F32)])
    return outs[0], per_job


def _dw_rows(a, b, name, tm, tk):
    m, n = a.shape[1], b.shape[1]
    tk = min(tk, a.shape[0])
    return _matmul_tn(a, b, name, (m, n), (m // tm, 1),
                      (tk, tm), lambda i, j, k: (k, i), (tk, n), lambda i, j, k: (k, 0),
                      (tm, n), lambda i, j, k: (i, 0))[0]


def _dw_cols(a, b, name, tn, sharded, jobs=()):
    t_len, m = a.shape
    n = b.shape[1]

    def body(a_ref, b_ref, o_ref):
        o_ref[...] = lax.dot_general(a_ref[...].astype(BF16), b_ref[...].astype(BF16),
                                     (((0,), (0,)), ((), ())), preferred_element_type=F32)

    if sharded:
        o_spec, o_shape = pl.BlockSpec((None, m, tn), lambda j: (j, 0, 0)), (n // tn, m, tn)
    else:
        o_spec, o_shape = pl.BlockSpec((m, tn), lambda j: (0, j)), (m, n)
    outs, per_job = _call(body, name, (n // tn,),
                          [_whole(), pl.BlockSpec((t_len, tn), lambda j: (0, j))], [o_spec],
                          [jax.ShapeDtypeStruct(o_shape, F32)], [a, b], jobs)
    return outs[0], per_job


def _dw_tiles(a, b, name, tm, tn, jobs=()):
    t_len, m = a.shape
    n = b.shape[1]

    def body(a_ref, b_ref, o_ref):
        o_ref[...] = lax.dot_general(a_ref[...].astype(BF16), b_ref[...].astype(BF16),
                                     (((0,), (0,)), ((), ())), preferred_element_type=F32)

    outs, per_job = _call(body, name, (n // tn, m // tm),
                          [pl.BlockSpec((t_len, tm), lambda j, i: (0, i)),
                           pl.BlockSpec((t_len, tn), lambda j, i: (0, j))],
                          [pl.BlockSpec((None, tm, tn), lambda j, i: (j, i, 0))],
                          [jax.ShapeDtypeStruct((n // tn, m, tn), F32)], [a, b], jobs)
    return outs[0], per_job


def _dw_pair(a, m, b1, b2, name, jobs=()):
    t_len = a.shape[0]
    n_slab = DIAG_N // LANE
    rows_per_slab = LANE // n_slab

    def body(a_ref, b1_ref, b2_ref, o1_ref, o2_ref):
        for b_ref, o_ref in ((b1_ref, o1_ref), (b2_ref, o2_ref)):
            prod = lax.dot_general(a_ref[...].astype(BF16), b_ref[...].astype(BF16),
                                   (((0,), (0,)), ((), ())), preferred_element_type=F32)
            for j in range(n_slab):
                rows = slice(j * rows_per_slab, (j + 1) * rows_per_slab)
                o_ref[rows, :] = prod[rows, j * LANE:(j + 1) * LANE]

    tok = pl.BlockSpec((t_len, DIAG_N), lambda i: (0, i))
    out = pl.BlockSpec((LANE, LANE), lambda i: (i, 0))
    return _call(body, name, (m // LANE,),
                 [pl.BlockSpec((t_len, LANE), lambda i: (0, i)), tok, tok], [out, out],
                 [jax.ShapeDtypeStruct((m, LANE), F32)] * 2, [a, b1, b2], jobs)


def _prefetch_call(body, name, grid, scalars, in_specs, out_specs, out_shape, args):
    return pl.pallas_call(
        body, name=name,
        grid_spec=pltpu.PrefetchScalarGridSpec(num_scalar_prefetch=1, grid=grid, in_specs=in_specs,
                                               out_specs=out_specs),
        out_shape=out_shape, compiler_params=_params(len(grid)),
    )(scalars, *args)


def _place_shard(w, where, name, dtype, tr):
    rows, cols = w.shape

    def body(s_ref, w_ref, o_ref):
        o_ref[...] = w_ref[...].astype(dtype)

    return _prefetch_call(
        body, name, (rows // tr,), where,
        [pl.BlockSpec((tr, cols), lambda i, s: (i, 0))],
        pl.BlockSpec((None, tr, cols), lambda i, s: (s[0], i, 0)),
        jax.ShapeDtypeStruct((N_CHIP, rows, cols), dtype), [w])


def _place_shards(ws, where, name, dtype):
    n = len(ws)

    def body(s_ref, *refs):
        for t in range(n):
            refs[n + t][...] = refs[t][...].astype(dtype)

    return _prefetch_call(
        body, name, (1,), where,
        [pl.BlockSpec(w.shape, lambda i, s: (0, 0)) for w in ws],
        [pl.BlockSpec((None,) + w.shape, lambda i, s: (s[0], 0, 0)) for w in ws],
        [jax.ShapeDtypeStruct((N_CHIP,) + w.shape, dtype) for w in ws], ws)


def _add_sibling(gs, gots, where, name):
    n = len(gs)
    halves = [(g.shape[1] // 2, g.shape[2]) for g in gs]

    def body(s_ref, *refs):
        for t in range(n):
            refs[2 * n + t][...] = (refs[t][...] + refs[n + t][...]).astype(BF16)

    return _prefetch_call(
        body, name, (N_CHIP,), where,
        [pl.BlockSpec((None, hr, cs), lambda k, s: (k, s[1], 0)) for hr, cs in halves]
        + [pl.BlockSpec((None, hr, cs), lambda k, s: (k, 0, 0)) for hr, cs in halves],
        [pl.BlockSpec((None, hr, cs), lambda k, s: (k, 0, 0)) for hr, cs in halves],
        [jax.ShapeDtypeStruct((N_CHIP, hr, cs), BF16) for hr, cs in halves], list(gs) + list(gots))


def _add_chips(sums, gots, where, name):
    n = len(sums)
    halves = [s.shape[1:] for s in sums]

    def body(s_ref, *refs):
        for t in range(n):
            own_ref, got_ref = refs[t], refs[n + t]
            refs[2 * n + t][...] = ((own_ref[...].astype(F32) + got_ref[0].astype(F32))
                                    + got_ref[1].astype(F32)) + got_ref[2].astype(F32)

    return _prefetch_call(
        body, name, (1,), where,
        [pl.BlockSpec((None, hr, cs), lambda i, s: (s[0], 0, 0)) for hr, cs in halves]
        + [pl.BlockSpec((3, hr, cs), lambda i, s: (0, 0, 0)) for hr, cs in halves],
        [pl.BlockSpec((hr, cs), lambda i, s: (s[1], 0)) for hr, cs in halves],
        [jax.ShapeDtypeStruct((2 * hr, cs), F32) for hr, cs in halves], list(sums) + list(gots))


def _small_allreduce(pack):
    rows = pack.shape[0]
    half = rows // 2

    def body(in_ref, out_ref, sib_ref, slots_ref, s_a, r_a, s_b, r_b, s_c, r_c):
        x, y, c, chips = _place()
        k_me = 2 * x + y
        sib = (x, y, 1 - c)
        first = _remote(in_ref, sib_ref, s_a, r_a, sib)
        first.start()
        first.wait()
        mine = _half(rows, c)
        slots_ref[k_me] = in_ref[mine, :] + sib_ref[mine, :]
        cps = [_remote(slots_ref.at[k_me], slots_ref.at[k_me], s_b.at[j], r_b.at[j], (*ch, c))
               for j, ch in enumerate(chips)]
        for cp in cps:
            cp.start()
        for j, ch in enumerate(chips):
            slot = slots_ref.at[_chip_index(ch)]
            _remote(slot, slot, s_b.at[j], r_b.at[j], (*ch, c)).wait_recv()
        for cp in cps:
            cp.wait_send()
        out_ref[mine, :] = ((slots_ref[0] + slots_ref[1]) + slots_ref[2]) + slots_ref[3]
        last = _remote(out_ref.at[mine, :], out_ref.at[mine, :], s_c, r_c, sib)
        last.start()
        theirs = out_ref.at[_half(rows, 1 - c), :]
        _remote(theirs, theirs, s_c, r_c, sib).wait_recv()
        last.wait_send()

    return pl.pallas_call(
        body, name="small_allreduce", in_specs=[_whole()], out_specs=_whole(),
        out_shape=jax.ShapeDtypeStruct(pack.shape, F32),
        scratch_shapes=[pltpu.VMEM(pack.shape, F32), pltpu.VMEM((N_CHIP, half, LANE), F32),
                        pltpu.SemaphoreType.DMA, pltpu.SemaphoreType.DMA,
                        pltpu.SemaphoreType.DMA((3,)), pltpu.SemaphoreType.DMA((3,)),
                        pltpu.SemaphoreType.DMA, pltpu.SemaphoreType.DMA],
        compiler_params=_params(0),
    )(pack)


def _adamw_update(w_ref, g_ref, m_ref, v_ref, d_ref, mo_ref, vo_ref):
    gv = g_ref[...]
    mn = ADAM_B1 * m_ref[...] + (1.0 - ADAM_B1) * gv
    vn = ADAM_B2 * v_ref[...] + (1.0 - ADAM_B2) * (gv * gv)
    mo_ref[...] = mn
    vo_ref[...] = vn
    m_hat = mn / (1.0 - ADAM_B1 ** ADAM_STEP)
    v_hat = vn / (1.0 - ADAM_B2 ** ADAM_STEP)
    d_ref[...] = -ADAM_LR * (m_hat / (jnp.sqrt(v_hat) + ADAM_EPS) + ADAM_WD * w_ref[...])


def _adamw(w, g, m, v, name, tr):
    rows, cols = w.shape
    blk = _rows(tr, cols)

    def body(w_ref, g_ref, m_ref, v_ref, go_ref, d_ref, mo_ref, vo_ref):
        go_ref[...] = g_ref[...]
        _adamw_update(w_ref, g_ref, m_ref, v_ref, d_ref, mo_ref, vo_ref)

    return _call(body, name, (rows // tr,), [blk] * 4, [blk] * 4,
                 [jax.ShapeDtypeStruct(w.shape, F32)] * 4, [w, g, m, v])[0]


def _adamw_many(ws, gs, ms, vs, name):
    n = len(ws)

    def body(*refs):
        for t in range(n):
            _adamw_update(*[refs[q * n + t] for q in range(7)])

    specs = [pl.BlockSpec(a.shape, lambda i, nd=a.ndim: (0,) * nd) for a in ws]
    outs = pl.pallas_call(
        body, name=name, grid=(1,), in_specs=specs * 4, out_specs=specs * 3,
        out_shape=[jax.ShapeDtypeStruct(a.shape, F32) for _ in range(3) for a in ws],
        compiler_params=_params(1),
    )(*ws, *gs, *ms, *vs)
    return outs[:n], outs[n:2 * n], outs[2 * n:]


def _ssm_discretize(a_re, a_im, log_dt, b_re, b_im):
    dt = jnp.exp(log_dt)[:, None]
    mag = jnp.exp(dt * a_re)
    abr = mag * jnp.cos(dt * a_im)
    abi = mag * jnp.sin(dt * a_im)
    den = a_re * a_re + a_im * a_im
    nr = abr - 1.0
    ni = abi
    f_re = (nr * a_re + ni * a_im) / den
    f_im = (ni * a_re - nr * a_im) / den
    bbr = f_re[..., None] * b_re - f_im[..., None] * b_im
    bbi = f_re[..., None] * b_im + f_im[..., None] * b_re
    return abr, abi, bbr, bbi


def _scan_tables(abr, abi):
    ar = abr.reshape(1, N_STATE)
    ai = abi.reshape(1, N_STATE)
    pr, pi = [ar], [ai]
    for _ in range(SUBLANE - 1):
        pr, pi = pr + [pr[-1] * ar - pi[-1] * ai], pi + [pr[-1] * ai + pi[-1] * ar]
    row = jnp.arange(SUBLANE)[:, None]
    tabs = []
    for d in (1, 2, 4):
        tabs.append(jnp.where(row >= d, pr[d - 1], 0.0))
        tabs.append(jnp.where(row >= d, pi[d - 1], 0.0))
    tabs.append(jnp.concatenate(pr, axis=0))
    tabs.append(jnp.concatenate(pi, axis=0))
    fwd = jnp.stack(tabs)
    sign = jnp.array([1.0, -1.0] * 4, F32)[:, None, None]
    return fwd, fwd[:, ::-1, :] * sign


def _block_diag_b(bb):
    strip = bb.transpose(2, 0, 1).reshape(SSM_H, N_STATE)
    rows = lax.broadcasted_iota(jnp.int32, (SSM_W, N_STATE), 0) // SSM_H
    cols = lax.broadcasted_iota(jnp.int32, (SSM_W, N_STATE), 1) // SSM_P
    return jnp.where(rows == cols, jnp.tile(strip, (SSM_G, 1)), 0.0).astype(BF16)


def _block_diag_c(cc):
    strip = cc.transpose(0, 2, 1).reshape(N_STATE, SSM_H)
    rows = lax.broadcasted_iota(jnp.int32, (N_STATE, SSM_W), 0) // SSM_P
    cols = lax.broadcasted_iota(jnp.int32, (N_STATE, SSM_W), 1) // SSM_H
    return jnp.where(rows == cols, jnp.tile(strip, (1, SSM_G)), 0.0).astype(BF16)


SMALL_SHAPES = {
    "g_mix": (D_MODEL,), "a_re": (SSM_G, SSM_P), "a_im": (SSM_G, SSM_P), "log_dt": (SSM_G,),
    "b_re": (SSM_G, SSM_P, SSM_H), "b_im": (SSM_G, SSM_P, SSM_H),
    "c_re": (SSM_G, SSM_H, SSM_P), "c_im": (SSM_G, SSM_H, SSM_P),
    "d_skip": (SSM_W,), "b_glu": (SSM_W,), "g_sgu": (SGU_W,), "w_s": (SGU_G, CHUNK, CHUNK),
    "b_s": (SGU_G, CHUNK), "g_ffn": (D_MODEL,), "conv_b": (2 * D_FF,), "g_final": (D_MODEL,),
}
PACK_ITEMS = [("loss", (1,))] + [(n, SMALL_SHAPES[n]) for n in SMALL] + [("conv_w", (3, 2 * D_FF))]
TILE = SUBLANE * LANE


def _item_rows(shape):
    return -(-math.prod(shape) // TILE) * SUBLANE


PACK_ROWS = -(-sum(_item_rows(s) for _, s in PACK_ITEMS) // (2 * SUBLANE)) * (2 * SUBLANE)


def _pack(values):
    parts, used = [], 0
    for name, shape in PACK_ITEMS:
        size, rows = math.prod(shape), _item_rows(shape)
        if name in values:
            flat = values[name].astype(F32).reshape(size)
            if rows * LANE > size:
                flat = jnp.pad(flat, (0, rows * LANE - size))
            parts.append(flat.reshape(rows, LANE))
        else:
            parts.append(jnp.zeros((rows, LANE), F32))
        used += rows
    if PACK_ROWS > used:
        parts.append(jnp.zeros((PACK_ROWS - used, LANE), F32))
    return jnp.concatenate(parts, axis=0)


def _unpack(pack):
    out, off = {}, 0
    for name, shape in PACK_ITEMS:
        rows = _item_rows(shape)
        out[name] = pack[off:off + rows].reshape(rows * LANE)[:math.prod(shape)].reshape(shape)
        off += rows
    return out


PLACE_ROWS = {"w_in": 256, "w_up": 256, "w_down": 352, "w_out": 256, "w_proj_a": 256,
              "w_proj_b": 256, "w_glu": 128}


def kernel(x, g_mix, w_in, a_re, a_im, log_dt, b_re, b_im, c_re, c_im, d_skip, w_glu, b_glu, w_proj_a, g_sgu, w_s, b_s, w_proj_b, w_out, g_ffn, w_up, conv_w, conv_b, w_down, g_final, loss_target, m_g_mix, m_w_in, m_a_re, m_a_im, m_log_dt, m_b_re, m_b_im, m_c_re, m_c_im, m_d_skip, m_w_glu, m_b_glu, m_w_proj_a, m_g_sgu, m_w_s, m_b_s, m_w_proj_b, m_w_out, m_g_ffn, m_w_up, m_conv_w, m_conv_b, m_w_down, m_g_final, v_g_mix, v_w_in, v_a_re, v_a_im, v_log_dt, v_b_re, v_b_im, v_c_re, v_c_im, v_d_skip, v_w_glu, v_b_glu, v_w_proj_a, v_g_sgu, v_w_s, v_b_s, v_w_proj_b, v_w_out, v_g_ffn, v_w_up, v_conv_w, v_conv_b, v_w_down, v_g_final):
    given = dict(locals())
    w = {n: given[n] for n in WEIGHTS}
    m = {n: given["m_" + n] for n in WEIGHTS}
    v = {n: given["v_" + n] for n in WEIGHTS}

    def shard2d(a):
        return a.reshape(a.shape[-2], a.shape[-1])

    chip = 2 * lax.axis_index("x") + lax.axis_index("y")
    where = jnp.stack([chip, lax.axis_index("c")]).astype(jnp.int32)
    xs, target = x[0], loss_target[0]
    small = {n: w[n].reshape(SMALL_SHAPES[n]) for n in SMALL}

    (abr, abi, bbr, bbi), disc_vjp = jax.vjp(_ssm_discretize, small["a_re"], small["a_im"],
                                             small["log_dt"], small["b_re"], small["b_im"])
    tab_f, tab_r = _scan_tables(abr, abi)
    bre = _block_diag_b(bbr)
    bim = _block_diag_b(bbi)
    cre = _block_diag_c(small["c_re"])
    cim = _block_diag_c(small["c_im"])
    tril = jnp.tril(jnp.ones((CHUNK, CHUNK), dtype=bool))
    ws = jnp.where(tril[None], small["w_s"], 0.0)
    ws_st = ws.reshape(SGU_G // 2, 2 * CHUNK, CHUNK).astype(BF16)
    wst_st = ws.transpose(0, 2, 1).reshape(SGU_G // 2, 2 * CHUNK, CHUNK).astype(BF16)
    bmat = jnp.repeat(small["b_s"].T, SGU_D, axis=1)
    g_mix2 = small["g_mix"].reshape(1, D_MODEL)
    g_ffn2 = small["g_ffn"].reshape(1, D_MODEL)
    g_final2 = small["g_final"].reshape(1, D_MODEL)
    g_sgu2 = small["g_sgu"].reshape(1, SGU_W)
    d_skip2 = small["d_skip"].reshape(1, SSM_W)
    b_glu2 = small["b_glu"].reshape(1, SSM_W)
    conv_b2 = small["conv_b"].reshape(1, 2 * D_FF)

    gat = {"w_in": _place_shard(shard2d(w["w_in"]), where, "place_w_in", BF16, PLACE_ROWS["w_in"])}
    gat.update(zip(BIG[1:], _place_shards([shard2d(w[n]) for n in BIG[1:]], where, "place_rest", BF16)))
    gat["conv_w"] = _place_shard(shard2d(w["conv_w"]), where, "place_conv_w", F32, 3)
    all_rows = (0, D_MODEL)
    (gat["w_in"],), = _comm("gather_in", [_job_gather(
        [gat["w_in"]], [(0, all_rows, ICI, (0.0, 0.5)), (0, all_rows, SIBLING, (0.5, 1.0))])])
    mixers = ["w_glu", "w_proj_a", "w_proj_b", "w_out"]
    rows = {n: (0, gat[n].shape[1]) for n in mixers}
    down_a, down_b = (0, D_FF // 8), (D_FF // 8, D_FF // 8)
    up_a, up_b = (0, 3 * D_MODEL // 8), (3 * D_MODEL // 8, 5 * D_MODEL // 8)
    span = (0.0, 1.0)

    names = mixers + ["conv_w", "w_down"]
    (p, h1, bur, bui), (got,) = _fwd_in(
        xs, g_mix2, gat["w_in"], bre, bim,
        [_job_gather([gat[n] for n in names],
                     [(i, rows[n], ICI, span) for i, n in enumerate(mixers)]
                     + [(4, None, ICI, span), (5, down_a, ICI, span)])])
    gat.update(zip(names, got))
    names = mixers + ["w_down", "w_up"]
    (str_, sti), (got,) = _scan_fwd(
        bur, bui, tab_f,
        [_job_gather([gat[n] for n in names],
                     [(i, rows[n], SIBLING, span) for i, n in enumerate(mixers)]
                     + [(4, down_a, SIBLING, span), (4, down_b, ICI, span), (5, up_a, ICI, span)])])
    gat.update(zip(names, got))
    w_glu_f = gat["w_glu"].reshape(SSM_W, SSM_W)
    w_out_f = gat["w_out"].reshape(D_MODEL, D_MODEL)
    conv_w_f = gat["conv_w"].transpose(1, 0, 2).reshape(3, 2 * D_FF)
    (x2, y0, z, mixed, ya, yb), ((gat["w_down"], gat["w_up"]),) = _fwd_mix(
        xs, p, str_, sti, cre, cim, d_skip2, w_glu_f, b_glu2, gat["w_proj_a"], g_sgu2, ws_st, bmat,
        gat["w_proj_b"], w_out_f,
        [_job_gather([gat["w_down"], gat["w_up"]],
                     [(0, down_b, SIBLING, span), (1, up_a, SIBLING, span),
                      (1, up_b, ICI, (0.0, 0.75)), (1, up_b, SIBLING, (0.75, 1.0))])])
    w_down_f = gat["w_down"].reshape(D_FF, D_MODEL)
    up, act, f, h2, dx3, sm_ffn = _fwd_ffn(x2, target, g_ffn2, gat["w_up"], conv_w_f, conv_b2,
                                           w_down_f, g_final2)

    def leg1_done(names, got):
        return _add_sibling([part[n] for n in names], got, where, "add_sibling_" + names[0])

    def leg2_done(names, sums, got):
        return _add_chips(sums, got, where, "add_chips_" + names[0])

    part, red = {}, {}
    part["w_down"] = _dw_rows(f, dx3, "dw_down", D_FF // 2, 4 * TK).reshape(
        N_CHIP, D_FF // N_CHIP, D_MODEL)
    (dx2, dup, sm_conv, sm_gffn), (got,) = _bwd_ffn(
        dx3, up, act, x2, g_ffn2, gat["w_up"], conv_w_f, w_down_f,
        [_job_sibling_halves([part["w_down"]])])
    sum_down = leg1_done(["w_down"], got)
    part["w_up"], (got,) = _dw_tiles(h2, dup, "dw_up", D_MODEL // 2, 2 * D_FF // N_CHIP,
                                     [_job_to_owner(sum_down)])
    red_down = leg2_done(["w_down"], sum_down, got)
    ((dsr, dsi, du_part, drest, mrg, dya, dyb, yap, dz, y1, sgu, dy0, sm_mix, dbm, dws),
     (got, (red["w_down"],))) = _bwd_mix(
        dx2, p, y0, z, mixed, ya, yb, w_out_f, gat["w_proj_a"], gat["w_proj_b"], w_glu_f, cre, cim,
        ws_st, wst_st, d_skip2, g_sgu2,
        [_job_sibling_halves([part["w_up"]]), _job_swap_halves(red_down)])
    sum_up = leg1_done(["w_up"], got)
    (lam_r, lam_i, dar8, dai8), (got,) = _scan_bwd(dsr, dsi, str_, sti, tab_r, [_job_to_owner(sum_up)])
    red_up = leg2_done(["w_up"], sum_up, got)
    mix4 = ["w_out", "w_proj_a", "w_proj_b", "w_glu"]
    part["w_out"] = _dw_cols(mrg, dx2, "dw_out", D_MODEL // 2, False)[0].reshape(
        N_CHIP, D_MODEL // N_CHIP, D_MODEL)
    part["w_proj_a"] = _dw_cols(yap, dya, "dw_proj_a", D_MODEL // N_CHIP, True)[0]
    part["w_proj_b"] = _dw_cols(sgu, dyb, "dw_proj_b", D_MODEL // N_CHIP, True)[0]
    part["w_glu"] = _dw_cols(y1, dz, "dw_glu", SSM_W, False)[0].reshape(
        N_CHIP, SSM_W // N_CHIP, SSM_W)
    got, (red["w_up"],) = _comm(
        "mixer_sibling_halves", [_job_sibling_halves([part[n] for n in mix4]), _job_swap_halves(red_up)])
    (grad_x, dp, sm_gmix), _ = _bwd_in(
        lam_r, lam_i, du_part, drest, xs, dx2, g_mix2, gat["w_in"], bre, bim)
    sums_m = leg1_done(mix4, got)
    part["w_in"], (got,) = _dw_cols(h1, dp, "dw_in", IN_COLS // N_CHIP, True, [_job_to_owner(sums_m)])
    red_m = leg2_done(mix4, sums_m, got)
    (dbd_r, dbd_i), (got, done_m) = _dw_pair(
        p, SSM_W, lam_r, lam_i, "db_bar",
        [_job_sibling_halves([part["w_in"]]), _job_swap_halves(red_m)])
    red.update(zip(mix4, done_m))
    sum_in = leg1_done(["w_in"], got)
    (dcd_r, dcd_i), (got,) = _dw_pair(dy0, SSM_W, str_, sti, "dc", [_job_to_owner(sum_in)])
    red_in = leg2_done(["w_in"], sum_in, got)
    (red["w_in"],), = _comm("swap_w_in", [_job_swap_halves(red_in)])

    def pick_c(slabs):
        two = LANE // SSM_P
        return jnp.einsum("jshsp->jshp", slabs.reshape(SSM_G // two, two, SSM_H, two, SSM_P)
                          ).reshape(SSM_G, SSM_H, SSM_P)

    def pick_b(slabs):
        return pick_c(slabs).transpose(0, 2, 1)

    dabr = jnp.sum(dar8, axis=0).reshape(SSM_G, SSM_P)
    dabi = jnp.sum(dai8, axis=0).reshape(SSM_G, SSM_P)
    d_a_re, d_a_im, d_log_dt, d_b_re, d_b_im = disc_vjp((dabr, dabi, pick_b(dbd_r), pick_b(dbd_i)))
    gsmall = {
        "g_mix": sm_gmix[0], "a_re": d_a_re, "a_im": d_a_im, "log_dt": d_log_dt,
        "b_re": d_b_re, "b_im": d_b_im, "c_re": pick_c(dcd_r), "c_im": -pick_c(dcd_i),
        "d_skip": sm_mix[0], "b_glu": sm_mix[1], "g_sgu": sm_mix[2],
        "w_s": jnp.where(tril[None], dws, 0.0),
        "b_s": dbm.reshape(CHUNK, SGU_G, SGU_D).sum(-1).T,
        "g_ffn": sm_gffn[0], "conv_b": sm_conv[3], "g_final": sm_ffn[0],
        "conv_w": sm_conv[0:3], "loss": sm_ffn[1, 0:1],
    }

    total_pack = _small_allreduce(_pack(gsmall))
    total = _unpack(total_pack)
    grads = dict(red)
    cs = 2 * D_FF // N_CHIP
    grads["conv_w"] = lax.dynamic_slice(total["conv_w"], (0, chip * cs), (3, cs))
    delta, new_m, new_v = {}, {}, {}
    for n in BIG + ("conv_w",):
        grads[n], delta[n], new_m[n], new_v[n] = _adamw(
            shard2d(w[n]), grads[n], shard2d(m[n]), shard2d(v[n]), "adamw_" + n, PLACE_ROWS.get(n, 3))
    for n in SMALL:
        grads[n] = total[n].reshape(w[n].shape)
    ud, um, uv = _adamw_many(*[[d[n] for n in SMALL] for d in (w, grads, m, v)], "adamw_small")
    for i, n in enumerate(SMALL):
        delta[n], new_m[n], new_v[n] = ud[i], um[i], uv[i]

    def like(d):
        return [d[n].reshape(w[n].shape) for n in WEIGHTS]

    return (total["loss"].reshape(()), grad_x.reshape(x.shape), *like(grads), *like(delta),
            *like(new_m), *like(new_v))
```

```python
import math

import jax
import jax.numpy as jnp
from jax import lax
from jax.experimental import pallas as pl
from jax.experimental.pallas import tpu as pltpu

F32 = jnp.float32
BF16 = jnp.bfloat16
MESH = pl.DeviceIdType.MESH

D_MODEL = 1024
SSM_W = 512
SSM_G = 32
SSM_H = 16
SSM_P = 64
N_STATE = SSM_G * SSM_P
DIAG_N = 128 * SSM_P // SSM_H
SGU_W = 512
SGU_G = 8
SGU_D = 64
CHUNK = 128
D_FF = 2816
IN_COLS = 3584
EPS = 1e-6
N_CHIP = 4

ADAM_LR = 0.001
ADAM_B1 = 0.9
ADAM_B2 = 0.999
ADAM_EPS = 1e-08
ADAM_WD = 0.01
ADAM_STEP = 10

SUBLANE = 8
LANE = 128
VMEM_LIMIT = 56 * 1024 * 1024
TB = 256
TB_WIDE = 512
TK = 512
SCAN_LANES = 256
SCAN_UNROLL = 4
HALO = SUBLANE

BIG = ("w_in", "w_up", "w_down", "w_out", "w_proj_a", "w_proj_b", "w_glu")
SMALL = ("g_mix", "a_re", "a_im", "log_dt", "b_re", "b_im", "c_re", "c_im", "d_skip", "b_glu",
         "g_sgu", "w_s", "b_s", "g_ffn", "conv_b", "g_final")
WEIGHTS = ("g_mix", "w_in", "a_re", "a_im", "log_dt", "b_re", "b_im", "c_re", "c_im", "d_skip",
           "w_glu", "b_glu", "w_proj_a", "g_sgu", "w_s", "b_s", "w_proj_b", "w_out", "g_ffn",
           "w_up", "conv_w", "conv_b", "w_down", "g_final")

ANY = pl.BlockSpec(memory_space=pl.ANY)


def _params(n_grid):
    return pltpu.CompilerParams(dimension_semantics=("arbitrary",) * n_grid if n_grid else None,
                                vmem_limit_bytes=VMEM_LIMIT)


def _whole():
    return pl.BlockSpec(memory_space=pltpu.VMEM)


def _rows(tb, ncol):
    return pl.BlockSpec((tb, ncol), lambda i: (i, 0))


def _acc(nrow, ncol):
    return pl.BlockSpec((nrow, ncol), lambda i: (0, 0))


def _dot(a, b):
    return jnp.dot(a.astype(BF16), b.astype(BF16), preferred_element_type=F32)


def _dot_nt(a, b):
    return lax.dot_general(a.astype(BF16), b.astype(BF16), (((1,), (1,)), ((), ())),
                           preferred_element_type=F32)


def _sigmoid(v):
    return 0.5 * jnp.tanh(0.5 * v) + 0.5


_GELU_C = math.sqrt(2.0 / math.pi)


def _gelu(v):
    return 0.5 * v * (1.0 + jnp.tanh(_GELU_C * (v + 0.044715 * v * v * v)))


def _gelu_and_grad(v):
    v2 = v * v
    t = jnp.tanh(_GELU_C * v * (1.0 + 0.044715 * v2))
    half = 0.5 * (1.0 + t)
    return v * half, half + 0.5 * v * (1.0 - t * t) * _GELU_C * (1.0 + 3.0 * 0.044715 * v2)


def _rms_stats(v):
    r = lax.rsqrt(jnp.mean(v * v, axis=-1, keepdims=True) + EPS)
    return r, v * r


def _rms_bwd(dxh, xh, r):
    return r * (dxh - xh * jnp.mean(dxh * xh, axis=-1, keepdims=True))


def _place():
    x, y, c = lax.axis_index("x"), lax.axis_index("y"), lax.axis_index("c")
    chips = [(1 - x, y), (x, 1 - y), (1 - x, 1 - y)]
    return x, y, c, chips


def _chip_index(chip):
    return 2 * chip[0] + chip[1]


def _remote(src, dst, send_sem, recv_sem, device):
    return pltpu.make_async_remote_copy(src_ref=src, dst_ref=dst, send_sem=send_sem,
                                        recv_sem=recv_sem, device_id=device, device_id_type=MESH)


def _half(ref_rows, c):
    hr = ref_rows // 2
    return pl.ds(pl.multiple_of(c * hr, SUBLANE), hr)


class _Job:
    def __init__(self, hooks, n_sem, ins=(), inouts=(), outs=()):
        self.hooks, self.n_sem = list(hooks), n_sem
        self.ins, self.inouts, self.outs = list(ins), list(inouts), list(outs)


def _whole_span(start, finish):
    return [(0.0, "start", start), (1.0, "finish", finish)]


ICI, SIBLING = "ici", "sibling"


def _job_gather(bufs, legs):
    def copies(io, leg, first):
        b, window, kind, _ = legs[leg]
        x, y, c, chips = _place()
        k_me = 2 * x + y
        out = []
        for j, ch in enumerate(chips):
            k = _chip_index(ch)
            if window is None:
                src, land, dev = io[b].at[k_me], io[b].at[k], (*ch, c)
            else:
                r0, rows = window
                mine = pl.ds(pl.multiple_of(r0 + c * (rows // 2), SUBLANE), rows // 2)
                theirs = pl.ds(pl.multiple_of(r0 + (1 - c) * (rows // 2), SUBLANE), rows // 2)
                if kind == ICI:
                    src, land, dev = io[b].at[k_me, mine, :], io[b].at[k, mine, :], (*ch, c)
                else:
                    src, land, dev = io[b].at[k, mine, :], io[b].at[k, theirs, :], (x, y, 1 - c)
            out.append((src, land, first + j, dev))
        return out

    def starter(leg):
        def start(ins, io, outs, ssem, rsem):
            for src, _, i, dev in copies(io, leg, 3 * leg):
                _remote(src, src, ssem(i), rsem(i), dev).start()
        return start

    def finisher(leg):
        def finish(ins, io, outs, ssem, rsem):
            cps = copies(io, leg, 3 * leg)
            for _, land, i, dev in cps:
                _remote(land, land, ssem(i), rsem(i), dev).wait_recv()
            for src, _, i, dev in cps:
                _remote(src, src, ssem(i), rsem(i), dev).wait_send()
        return finish

    hooks = []
    for leg, (_, _, _, (begin, end)) in enumerate(legs):
        hooks += [(begin, "start", starter(leg)), (end, "finish", finisher(leg))]
    return _Job(hooks, 3 * len(legs), inouts=bufs)


def _job_sibling_halves(grads):
    n = len(grads)

    def build(ins, outs, ssem, rsem):
        x, y, c, _ = _place()
        return [_remote(ins[t].at[:, _half(grads[t].shape[1], 1 - c), :], outs[t], ssem(t), rsem(t),
                        (x, y, 1 - c)) for t in range(n)]

    def start(ins, io, outs, ssem, rsem):
        for cp in build(ins, outs, ssem, rsem):
            cp.start()

    def finish(ins, io, outs, ssem, rsem):
        for cp in build(ins, outs, ssem, rsem):
            cp.wait()

    return _Job(_whole_span(start, finish), n, ins=grads,
                outs=[jax.ShapeDtypeStruct((N_CHIP, g.shape[1] // 2, g.shape[2]), F32) for g in grads])


def _job_to_owner(sums):
    n = len(sums)

    def build(ins, outs, ssem, rsem):
        x, y, c, chips = _place()
        return [_remote(ins[t].at[_chip_index(ch)], outs[t].at[j], ssem(3 * t + j), rsem(3 * t + j),
                        (*ch, c)) for t in range(n) for j, ch in enumerate(chips)]

    def start(ins, io, outs, ssem, rsem):
        for cp in build(ins, outs, ssem, rsem):
            cp.start()

    def finish(ins, io, outs, ssem, rsem):
        for cp in build(ins, outs, ssem, rsem):
            cp.wait()

    return _Job(_whole_span(start, finish), 3 * n, ins=sums,
                outs=[jax.ShapeDtypeStruct((3,) + s.shape[1:], s.dtype) for s in sums])


def _job_swap_halves(bufs):
    n = len(bufs)

    def start(ins, io, outs, ssem, rsem):
        x, y, c, _ = _place()
        for t in range(n):
            mine = io[t].at[_half(bufs[t].shape[0], c), :]
            _remote(mine, mine, ssem(t), rsem(t), (x, y, 1 - c)).start()

    def finish(ins, io, outs, ssem, rsem):
        x, y, c, _ = _place()
        for t in range(n):
            theirs = io[t].at[_half(bufs[t].shape[0], 1 - c), :]
            _remote(theirs, theirs, ssem(t), rsem(t), (x, y, 1 - c)).wait_recv()
        for t in range(n):
            mine = io[t].at[_half(bufs[t].shape[0], c), :]
            _remote(mine, mine, ssem(t), rsem(t), (x, y, 1 - c)).wait_send()

    return _Job(_whole_span(start, finish), n, inouts=bufs)


def _call(body, name, grid, in_specs, out_specs, out_shape, args, jobs=(), scratch=()):
    n_in, n_out, n_scr = len(args), len(out_shape), len(scratch)
    job_in = [a for jb in jobs for a in jb.ins + jb.inouts]
    job_out = [s for jb in jobs
               for s in [jax.ShapeDtypeStruct(a.shape, a.dtype) for a in jb.inouts] + jb.outs]
    aliases, pos_in, pos_out = {}, n_in, n_out
    for jb in jobs:
        pos_in += len(jb.ins)
        for _ in jb.inouts:
            aliases[pos_in] = pos_out
            pos_in += 1
            pos_out += 1
        pos_out += len(jb.outs)
    n_sem = sum(jb.n_sem for jb in jobs)

    def wrapped(*refs):
        c_in = refs[:n_in]
        j_in = refs[n_in:n_in + len(job_in)]
        c_out = refs[n_in + len(job_in):n_in + len(job_in) + n_out]
        j_out = refs[n_in + len(job_in) + n_out:n_in + len(job_in) + n_out + len(job_out)]
        rest = refs[n_in + len(job_in) + n_out + len(job_out):]
        c_scr = rest[:n_scr]
        views, pi, po, ps = [], 0, 0, 0
        for jb in jobs:
            ins = j_in[pi:pi + len(jb.ins)]
            pi += len(jb.ins) + len(jb.inouts)
            io = j_out[po:po + len(jb.inouts)]
            new = j_out[po + len(jb.inouts):po + len(jb.inouts) + len(jb.outs)]
            po += len(jb.inouts) + len(jb.outs)
            send = (lambda i, o=ps: rest[n_scr].at[o + i])
            recv = (lambda i, o=ps: rest[n_scr + 1].at[o + i])
            ps += jb.n_sem
            views.append((ins, io, new, send, recv))

        def run(frac):
            for kind in ("finish", "start"):
                for jb, vw in zip(jobs, views):
                    for at, what, fn in jb.hooks:
                        if at == frac and what == kind:
                            fn(*vw)

        fracs = sorted({at for jb in jobs for at, _, _ in jb.hooks})
        if not grid:
            for frac in fracs:
                run(frac)
            return
        if jobs:
            assert len(grid) == 1 or set(fracs) <= {0.0, 1.0}
            first = pl.program_id(0) == 0
            last = pl.program_id(0) == grid[0] - 1
            for d in range(1, len(grid)):
                first = jnp.logical_and(first, pl.program_id(d) == 0)
                last = jnp.logical_and(last, pl.program_id(d) == grid[d] - 1)
            for frac in fracs:
                if frac < 1.0:
                    at_step = first if frac == 0.0 else pl.program_id(0) == int(frac * grid[0])
                    pl.when(at_step)(lambda frac=frac: run(frac))
        body(*c_in, *c_out, *c_scr)
        if jobs and 1.0 in fracs:
            pl.when(last)(lambda: run(1.0))

    sems = [pltpu.SemaphoreType.DMA((n_sem,)), pltpu.SemaphoreType.DMA((n_sem,))] if jobs else []
    kwargs = dict(grid=grid) if grid else {}
    res = pl.pallas_call(
        wrapped, name=name, in_specs=list(in_specs) + [ANY] * len(job_in),
        out_specs=list(out_specs) + [ANY] * len(job_out),
        out_shape=list(out_shape) + job_out, scratch_shapes=list(scratch) + sems,
        input_output_aliases=aliases, compiler_params=_params(len(grid)), **kwargs,
    )(*args, *job_in)
    outs, pos, per_job = list(res[:n_out]), n_out, []
    for jb in jobs:
        k = len(jb.inouts) + len(jb.outs)
        per_job.append(list(res[pos:pos + k]))
        pos += k
    return outs, per_job


def _comm(name, jobs):
    return _call(None, name, (), [], [], [], [], jobs)[1]


def _fwd_in(x, g_mix, w_in, bre, bim, jobs=()):
    t_len = x.shape[0]
    cs = IN_COLS // N_CHIP

    def body(x_ref, g_ref, w_ref, bre_ref, bim_ref, p_ref, h_ref, bur_ref, bui_ref):
        xv = x_ref[...]
        r, xh = _rms_stats(xv)
        h = (xh * g_ref[...]).astype(BF16)
        h_ref[...] = h
        for k in range(N_CHIP):
            p_ref[:, k * cs:(k + 1) * cs] = jnp.dot(h, w_ref[k],
                                                    preferred_element_type=F32).astype(BF16)
        u = p_ref[:, 0:SSM_W]
        for i in range(SSM_W // LANE):
            rows, cols = slice(i * LANE, (i + 1) * LANE), slice(i * DIAG_N, (i + 1) * DIAG_N)
            bur_ref[:, cols] = jnp.dot(u[:, rows], bre_ref[rows, cols],
                                       preferred_element_type=F32).astype(BF16)
            bui_ref[:, cols] = jnp.dot(u[:, rows], bim_ref[rows, cols],
                                       preferred_element_type=F32).astype(BF16)

    tb = min(TB_WIDE, t_len)
    return _call(
        body, "fwd_in", (t_len // tb,),
        [_rows(tb, D_MODEL), _whole(), _whole(), _whole(), _whole()],
        [_rows(tb, IN_COLS), _rows(tb, D_MODEL), _rows(tb, N_STATE), _rows(tb, N_STATE)],
        [jax.ShapeDtypeStruct((t_len, IN_COLS), BF16), jax.ShapeDtypeStruct((t_len, D_MODEL), BF16),
         jax.ShapeDtypeStruct((t_len, N_STATE), BF16), jax.ShapeDtypeStruct((t_len, N_STATE), BF16)],
        [x, g_mix, w_in, bre, bim], jobs)


def _scan_local(xr, xi, tab, shifts):
    for q, s in enumerate(shifts):
        ar, ai = tab[2 * q], tab[2 * q + 1]
        rr = pltpu.roll(xr, s, 0)
        ri = pltpu.roll(xi, s, 0)
        xr, xi = xr + ar * rr - ai * ri, xi + ar * ri + ai * rr
    return xr, xi


def _scan_carry(xr, xi, tab, cr, ci):
    pr, pi = tab[6], tab[7]
    return xr + pr * cr - pi * ci, xi + pr * ci + pi * cr


BF16_TILE = 2 * SUBLANE


def _load_blocks(r_ref, i_ref, base):
    out = []
    for q in range(SCAN_UNROLL // 2):
        rows = pl.ds(pl.multiple_of(base + q * BF16_TILE, BF16_TILE), BF16_TILE)
        vr, vi = r_ref[rows, :].astype(F32), i_ref[rows, :].astype(F32)
        out += [(vr[:SUBLANE], vi[:SUBLANE]), (vr[SUBLANE:], vi[SUBLANE:])]
    return out


def _store_blocks(r_ref, i_ref, base, blocks):
    for q in range(SCAN_UNROLL // 2):
        rows = pl.ds(pl.multiple_of(base + q * BF16_TILE, BF16_TILE), BF16_TILE)
        r_ref[rows, :] = jnp.concatenate([blocks[2 * q][0], blocks[2 * q + 1][0]], 0).astype(r_ref.dtype)
        i_ref[rows, :] = jnp.concatenate([blocks[2 * q][1], blocks[2 * q + 1][1]], 0).astype(i_ref.dtype)


def _scan_fwd(bur, bui, tab, jobs=()):
    t_len = bur.shape[0]
    nblk = t_len // SUBLANE
    lb = SCAN_LANES

    def body(br_ref, bi_ref, tab_ref, sr_ref, si_ref):
        tab_v = [tab_ref[q] for q in range(8)]

        def step(k, carry):
            cr, ci = carry
            base = pl.multiple_of(k * SCAN_UNROLL * SUBLANE, SCAN_UNROLL * SUBLANE)
            local = [_scan_local(xr, xi, tab_v, (1, 2, 4))
                     for xr, xi in _load_blocks(br_ref, bi_ref, base)]
            done = []
            for xr, xi in local:
                xr, xi = _scan_carry(xr, xi, tab_v, cr, ci)
                done.append((xr, xi))
                cr, ci = xr[SUBLANE - 1:SUBLANE, :], xi[SUBLANE - 1:SUBLANE, :]
            _store_blocks(sr_ref, si_ref, base, done)
            return cr, ci

        zero = jnp.zeros((1, lb), F32)
        lax.fori_loop(0, nblk // SCAN_UNROLL, step, (zero, zero))

    col = pl.BlockSpec((t_len, lb), lambda j: (0, j))
    return _call(
        body, "scan_fwd", (N_STATE // lb,),
        [col, col, pl.BlockSpec((8, SUBLANE, lb), lambda j: (0, 0, j))], [col, col],
        [jax.ShapeDtypeStruct((t_len, N_STATE), BF16)] * 2, [bur, bui, tab], jobs)


def _sgu_mix(v, ws_ref, lane_lo):
    rows = []
    for c0 in range(0, v.shape[0], CHUNK):
        slabs = []
        for j in range(SGU_W // LANE):
            prod = jnp.dot(ws_ref[j], v[c0:c0 + CHUNK, j * LANE:(j + 1) * LANE].astype(BF16),
                           preferred_element_type=F32)
            slabs.append(jnp.where(lane_lo, prod[:CHUNK], prod[CHUNK:]))
        rows.append(jnp.concatenate(slabs, axis=1))
    return jnp.concatenate(rows, axis=0) if len(rows) > 1 else rows[0]


def _fwd_mix(x, p, str_, sti, cre, cim, d_skip, w_glu, b_glu, w_pa, g_sgu, ws_st, bmat, w_pb, w_out,
             jobs=()):
    t_len = x.shape[0]

    def body(x_ref, p_ref, sr_ref, si_ref, cre_ref, cim_ref, dsk_ref, wg_ref, bg_ref, wpa_ref,
             gs_ref, ws_ref, bm_ref, wpb_ref, wo_ref,
             x2_ref, y0_ref, z_ref, mx_ref, ya_ref, yb_ref):
        u = p_ref[:, 0:SSM_W].astype(F32)
        y0 = jnp.concatenate(
            [_dot(sr_ref[:, i * DIAG_N:(i + 1) * DIAG_N],
                  cre_ref[i * DIAG_N:(i + 1) * DIAG_N, i * LANE:(i + 1) * LANE])
             - _dot(si_ref[:, i * DIAG_N:(i + 1) * DIAG_N],
                    cim_ref[i * DIAG_N:(i + 1) * DIAG_N, i * LANE:(i + 1) * LANE])
             for i in range(SSM_W // LANE)], axis=1) + dsk_ref[...] * u
        y0_ref[...] = y0.astype(BF16)
        y1 = _gelu(y0)
        z = _dot(y1, wg_ref[...]) + bg_ref[...]
        z_ref[...] = z.astype(BF16)
        ya_pre = (y1 * _sigmoid(z)).astype(BF16)
        ya = jnp.concatenate([jnp.dot(ya_pre, wpa_ref[k], preferred_element_type=F32)
                              for k in range(N_CHIP)], axis=1)
        ya_ref[...] = ya.astype(BF16)

        uvg = _gelu(p_ref[:, SSM_W:SSM_W + 2 * SGU_W].astype(F32))
        u2 = uvg[:, :SGU_W]
        _, vh = _rms_stats(uvg[:, SGU_W:])
        v3 = vh * gs_ref[...]
        lane_lo = lax.broadcasted_iota(jnp.int32, (CHUNK, LANE), 1) < SGU_D
        bias = jnp.concatenate([bm_ref[...]] * (x_ref.shape[0] // CHUNK), axis=0)
        mixed = _sgu_mix(v3, ws_ref, lane_lo) + bias
        mx_ref[...] = mixed.astype(BF16)
        sgu = (u2 * mixed).astype(BF16)
        yb = jnp.concatenate([jnp.dot(sgu, wpb_ref[k], preferred_element_type=F32)
                              for k in range(N_CHIP)], axis=1)
        yb_ref[...] = yb.astype(BF16)

        lg0 = SSM_W + 2 * SGU_W
        ga = _sigmoid(p_ref[:, lg0:lg0 + D_MODEL].astype(F32))
        gb = _sigmoid(p_ref[:, lg0 + D_MODEL:lg0 + 2 * D_MODEL].astype(F32))
        mrg = ga * ya + gb * yb
        x2_ref[...] = x_ref[...] + _dot(mrg, wo_ref[...])

    tb = min(TB_WIDE, t_len)
    return _call(
        body, "fwd_mix", (t_len // tb,),
        [_rows(tb, D_MODEL), _rows(tb, IN_COLS), _rows(tb, N_STATE), _rows(tb, N_STATE)]
        + [_whole()] * 11,
        [_rows(tb, D_MODEL), _rows(tb, SSM_W), _rows(tb, SSM_W), _rows(tb, SGU_W),
         _rows(tb, D_MODEL), _rows(tb, D_MODEL)],
        [jax.ShapeDtypeStruct((t_len, D_MODEL), F32), jax.ShapeDtypeStruct((t_len, SSM_W), BF16),
         jax.ShapeDtypeStruct((t_len, SSM_W), BF16), jax.ShapeDtypeStruct((t_len, SGU_W), BF16),
         jax.ShapeDtypeStruct((t_len, D_MODEL), BF16), jax.ShapeDtypeStruct((t_len, D_MODEL), BF16)],
        [x, p, str_, sti, cre, cim, d_skip, w_glu, b_glu, w_pa, g_sgu, ws_st, bmat, w_pb, w_out], jobs)


def _conv_taps(v, cw_ref, c0, width):
    w0 = cw_ref[0:1, c0:c0 + width]
    w1 = cw_ref[1:2, c0:c0 + width]
    w2 = cw_ref[2:3, c0:c0 + width]
    return w0 * pltpu.roll(v, 2, 0) + w1 * pltpu.roll(v, 1, 0) + w2 * v


def _fwd_ffn(x2, target, g_ffn, w_up, conv_w, conv_b, w_down, g_final):
    t_len = x2.shape[0]
    half = D_FF // 2
    blocks_per_halo = TB // HALO

    def body(x2_ref, xp_ref, tg_ref, gf_ref, wu_ref, cw_ref, cb_ref, wd_ref, gl_ref,
             up_ref, act_ref, f_ref, h2_ref, dx3_ref, sm_ref):
        i = pl.program_id(0)
        xe = jnp.concatenate([xp_ref[...] * jnp.where(i == 0, 0.0, 1.0), x2_ref[...]], axis=0)
        _, xh = _rms_stats(xe)
        h2 = (xh * gf_ref[...]).astype(BF16)
        h2_ref[...] = h2[HALO:]
        acc = jnp.zeros((TB, D_MODEL), F32)
        ups = [jnp.dot(h2, wu_ref[k], preferred_element_type=F32) for k in range(N_CHIP)]
        for hc in range(2):
            ca = hc * half
            cb = D_FF + hc * half
            ua, ub = ups[hc], ups[2 + hc]
            up_ref[:, ca:ca + half] = ua[HALO:].astype(BF16)
            up_ref[:, cb:cb + half] = ub[HALO:].astype(BF16)
            ac = _conv_taps(ua, cw_ref, ca, half)[HALO:] + cb_ref[:, ca:ca + half]
            bc = _conv_taps(ub, cw_ref, cb, half)[HALO:] + cb_ref[:, cb:cb + half]
            act_ref[:, ca:ca + half] = ac.astype(BF16)
            act_ref[:, cb:cb + half] = bc.astype(BF16)
            f = (ac * _sigmoid(ac) * bc).astype(BF16)
            f_ref[:, ca:ca + half] = f
            acc = acc + jnp.dot(f, wd_ref[ca:ca + half, :], preferred_element_type=F32)
        x3 = x2_ref[...] + acc
        r3, xh3 = _rms_stats(x3)
        err = xh3 * gl_ref[...] - tg_ref[...]
        dout = err * (1.0 / D_MODEL)
        dx3_ref[...] = _rms_bwd(dout * gl_ref[...], xh3, r3)
        dgl = jnp.sum(dout * xh3, axis=0, keepdims=True)
        loss = 0.5 * jnp.sum(jnp.mean(err * err, axis=-1, keepdims=True), axis=0, keepdims=True)
        upd = jnp.concatenate([dgl, jnp.broadcast_to(loss, (1, D_MODEL)),
                               jnp.zeros((SUBLANE - 2, D_MODEL), F32)], axis=0)

        @pl.when(i == 0)
        def _():
            sm_ref[...] = upd

        @pl.when(i > 0)
        def _():
            sm_ref[...] += upd

    prev = pl.BlockSpec((HALO, D_MODEL), lambda i: (jnp.maximum(i * blocks_per_halo - 1, 0), 0))
    return _call(
        body, "fwd_ffn", (t_len // TB,),
        [_rows(TB, D_MODEL), prev, _rows(TB, D_MODEL)] + [_whole()] * 6,
        [_rows(TB, 2 * D_FF), _rows(TB, 2 * D_FF), _rows(TB, D_FF), _rows(TB, D_MODEL),
         _rows(TB, D_MODEL), _acc(SUBLANE, D_MODEL)],
        [jax.ShapeDtypeStruct((t_len, 2 * D_FF), BF16), jax.ShapeDtypeStruct((t_len, 2 * D_FF), BF16),
         jax.ShapeDtypeStruct((t_len, D_FF), BF16), jax.ShapeDtypeStruct((t_len, D_MODEL), BF16),
         jax.ShapeDtypeStruct((t_len, D_MODEL), F32), jax.ShapeDtypeStruct((SUBLANE, D_MODEL), F32)],
        [x2, x2, target, g_ffn, w_up, conv_w, conv_b, w_down, g_final])[0]


def _bwd_ffn(dx3, up, act, x2, g_ffn, w_up, conv_w, w_down, jobs=()):
    t_len = x2.shape[0]
    half = D_FF // 2
    nblk = t_len // TB
    halo_b = 2 * HALO
    n_e = TB + HALO

    def body(dx_ref, dxn_ref, up_ref, act_ref, actn_ref, x2_ref, gf_ref, wu_ref, cw_ref,
             wd_ref, dx2_ref, dup_ref, smw_ref, smg_ref):
        i = pl.program_id(0)
        keep_last = jnp.where(i == nblk - 1, 0.0, 1.0)
        dxe = jnp.concatenate([dx_ref[...], dxn_ref[...] * keep_last], axis=0).astype(BF16)
        dh2 = jnp.zeros((TB, D_MODEL), F32)
        zpad = jnp.zeros((1, half), F32)
        dfs = [lax.dot_general(dxe, wd_ref[hc * half:(hc + 1) * half, :], (((1,), (1,)), ((), ())),
                               preferred_element_type=F32) for hc in range(2)]
        for hc in range(2):
            ca = hc * half
            cb = D_FF + hc * half
            ac = jnp.concatenate([act_ref[:, ca:ca + half].astype(F32),
                                  actn_ref[:, ca:ca + half].astype(F32)[:HALO]], axis=0)
            bc = jnp.concatenate([act_ref[:, cb:cb + half].astype(F32),
                                  actn_ref[:, cb:cb + half].astype(F32)[:HALO]], axis=0)
            wa = [cw_ref[k:k + 1, ca:ca + half] for k in range(3)]
            wb = [cw_ref[k:k + 1, cb:cb + half] for k in range(3)]
            df = dfs[hc]
            sg = _sigmoid(ac)
            da = df * bc * sg * (1.0 + ac * (1.0 - sg))
            db = df * ac * sg
            da1, da2 = pltpu.roll(da, n_e - 1, 0), pltpu.roll(da, n_e - 2, 0)
            db1, db2 = pltpu.roll(db, n_e - 1, 0), pltpu.roll(db, n_e - 2, 0)
            dua = (wa[2] * da + wa[1] * da1 + wa[0] * da2)[:TB]
            dub = (wb[2] * db + wb[1] * db1 + wb[0] * db2)[:TB]
            dup_ref[:, ca:ca + half] = dua.astype(BF16)
            dup_ref[:, cb:cb + half] = dub.astype(BF16)
            dh2 = dh2 + _dot_nt(dua, wu_ref[hc]) + _dot_nt(dub, wu_ref[2 + hc])
            rows = []
            for u_, d0, d1, d2 in ((up_ref[:, ca:ca + half].astype(F32), da, da1, da2),
                                   (up_ref[:, cb:cb + half].astype(F32), db, db1, db2)):
                rows.append([jnp.sum(u_ * d2[:TB], axis=0, keepdims=True),
                             jnp.sum(u_ * d1[:TB], axis=0, keepdims=True),
                             jnp.sum(u_ * d0[:TB], axis=0, keepdims=True),
                             jnp.sum(d0[:TB], axis=0, keepdims=True)])
            for c0, rws in ((ca, rows[0]), (cb, rows[1])):
                upd = jnp.concatenate(rws + [zpad] * (SUBLANE - 4), axis=0)

                @pl.when(i == 0)
                def _(upd=upd, c0=c0):
                    smw_ref[:, c0:c0 + half] = upd

                @pl.when(i > 0)
                def _(upd=upd, c0=c0):
                    smw_ref[:, c0:c0 + half] += upd

        r2, xh2 = _rms_stats(x2_ref[...])
        dx2_ref[...] = dx_ref[...] + _rms_bwd(dh2 * gf_ref[...], xh2, r2)
        updg = jnp.concatenate([jnp.sum(dh2 * xh2, axis=0, keepdims=True),
                                jnp.zeros((SUBLANE - 1, D_MODEL), F32)], axis=0)

        @pl.when(i == 0)
        def _():
            smg_ref[...] = updg

        @pl.when(i > 0)
        def _():
            smg_ref[...] += updg

    nxt_d = pl.BlockSpec((HALO, D_MODEL),
                         lambda i: (jnp.minimum((i + 1) * (TB // HALO), t_len // HALO - 1), 0))
    nxt_a = pl.BlockSpec((halo_b, 2 * D_FF),
                         lambda i: (jnp.minimum((i + 1) * (TB // halo_b), t_len // halo_b - 1), 0))
    return _call(
        body, "bwd_ffn", (nblk,),
        [_rows(TB, D_MODEL), nxt_d, _rows(TB, 2 * D_FF), _rows(TB, 2 * D_FF), nxt_a,
         _rows(TB, D_MODEL)] + [_whole()] * 4,
        [_rows(TB, D_MODEL), _rows(TB, 2 * D_FF), _acc(SUBLANE, 2 * D_FF), _acc(SUBLANE, D_MODEL)],
        [jax.ShapeDtypeStruct((t_len, D_MODEL), F32), jax.ShapeDtypeStruct((t_len, 2 * D_FF), BF16),
         jax.ShapeDtypeStruct((SUBLANE, 2 * D_FF), F32), jax.ShapeDtypeStruct((SUBLANE, D_MODEL), F32)],
        [dx3, dx3, up, act, act, x2, g_ffn, w_up, conv_w, w_down], jobs)


def _bwd_mix(dx2, p, y0, z, mixed, ya, yb, w_out, w_pa, w_pb, w_glu, cre, cim, ws_st, wst_st,
             d_skip, g_sgu, jobs=()):
    t_len = dx2.shape[0]
    pc = D_MODEL // N_CHIP
    n_slab = SGU_W // LANE

    def body(dx_ref, p_ref, y0_ref, z_ref, mx_ref, ya_ref, yb_ref, wo_ref, wpa_ref, wpb_ref,
             wg_ref, cre_ref, cim_ref, ws_ref, wst_ref, dsk_ref, gs_ref,
             dsr_ref, dsi_ref, du_ref, drest_ref, mrg_ref, dya_ref, dyb_ref, yap_ref, dz_ref,
             y1_ref, sgu_ref, dy0_ref, sm_ref, dbm_ref, dws_ref):
        i = pl.program_id(0)
        first = i == 0
        lg0 = SSM_W + 2 * SGU_W
        ga = _sigmoid(p_ref[:, lg0:lg0 + D_MODEL].astype(F32))
        gb = _sigmoid(p_ref[:, lg0 + D_MODEL:lg0 + 2 * D_MODEL].astype(F32))
        yav = ya_ref[...].astype(F32)
        ybv = yb_ref[...].astype(F32)
        mrg_ref[...] = (ga * yav + gb * ybv).astype(BF16)
        y0v = y0_ref[...].astype(F32)
        y1, y1_grad = _gelu_and_grad(y0v)
        sz = _sigmoid(z_ref[...].astype(F32))
        y1_ref[...] = y1.astype(BF16)
        yap_ref[...] = (y1 * sz).astype(BF16)

        dmrg = _dot_nt(dx_ref[...], wo_ref[...])
        drest_ref[:, 2 * SGU_W:2 * SGU_W + D_MODEL] = (dmrg * yav * ga * (1.0 - ga)).astype(BF16)
        drest_ref[:, 2 * SGU_W + D_MODEL:] = (dmrg * ybv * gb * (1.0 - gb)).astype(BF16)
        dya = (dmrg * ga).astype(BF16)
        dyb = (dmrg * gb).astype(BF16)
        dya_ref[...] = dya
        dyb_ref[...] = dyb

        dyap = jnp.zeros((TB, SSM_W), F32)
        for k in range(N_CHIP):
            dyap = dyap + _dot_nt(dya[:, k * pc:(k + 1) * pc], wpa_ref[k])
        dz = dyap * y1 * sz * (1.0 - sz)
        dz_ref[...] = dz.astype(BF16)
        dy0 = (dyap * sz + _dot_nt(dz, wg_ref[...])) * y1_grad
        dy0_ref[...] = dy0.astype(BF16)
        u = p_ref[:, 0:SSM_W].astype(F32)
        du_ref[...] = dy0 * dsk_ref[...]
        for q in range(SSM_W // LANE):
            rows, cols = slice(q * DIAG_N, (q + 1) * DIAG_N), slice(q * LANE, (q + 1) * LANE)
            dsr_ref[:, rows] = _dot_nt(dy0[:, cols], cre_ref[rows, cols]).astype(BF16)
            dsi_ref[:, rows] = (-_dot_nt(dy0[:, cols], cim_ref[rows, cols])).astype(BF16)

        uv = p_ref[:, SSM_W:lg0].astype(F32)
        uvg, gg = _gelu_and_grad(uv)
        u2 = uvg[:, :SGU_W]
        rv, vh = _rms_stats(uvg[:, SGU_W:])
        v3 = vh * gs_ref[...]
        mixed = mx_ref[...].astype(F32)
        dsgu = jnp.zeros((TB, SGU_W), F32)
        for k in range(N_CHIP):
            dsgu = dsgu + _dot_nt(dyb[:, k * pc:(k + 1) * pc], wpb_ref[k])
        sgu_ref[...] = (u2 * mixed).astype(BF16)
        drest_ref[:, 0:SGU_W] = (dsgu * mixed * gg[:, :SGU_W]).astype(BF16)
        dmix = dsgu * u2
        lane_lo = lax.broadcasted_iota(jnp.int32, (CHUNK, LANE), 1) < SGU_D
        dv3 = _sgu_mix(dmix, wst_ref, lane_lo)
        dbm = jnp.zeros((CHUNK, SGU_W), F32)
        for c0 in range(0, TB, CHUNK):
            dbm = dbm + dmix[c0:c0 + CHUNK]
        for j in range(n_slab):
            lo = jnp.zeros((CHUNK, CHUNK), F32)
            hi = jnp.zeros((CHUNK, CHUNK), F32)
            for c0 in range(0, TB, CHUNK):
                dsl = dmix[c0:c0 + CHUNK, j * LANE:(j + 1) * LANE]
                vsl = v3[c0:c0 + CHUNK, j * LANE:(j + 1) * LANE]
                lo = lo + _dot_nt(jnp.where(lane_lo, dsl, 0.0), vsl)
                hi = hi + _dot_nt(jnp.where(lane_lo, 0.0, dsl), vsl)

            @pl.when(first)
            def _(lo=lo, hi=hi, j=j):
                dws_ref[2 * j] = lo
                dws_ref[2 * j + 1] = hi

            @pl.when(jnp.logical_not(first))
            def _(lo=lo, hi=hi, j=j):
                dws_ref[2 * j] += lo
                dws_ref[2 * j + 1] += hi

        dv2 = _rms_bwd(dv3 * gs_ref[...], vh, rv)
        drest_ref[:, SGU_W:2 * SGU_W] = (dv2 * gg[:, SGU_W:]).astype(BF16)

        upd = jnp.concatenate([jnp.sum(dy0 * u, axis=0, keepdims=True),
                               jnp.sum(dz, axis=0, keepdims=True),
                               jnp.sum(dv3 * vh, axis=0, keepdims=True),
                               jnp.zeros((SUBLANE - 3, SSM_W), F32)], axis=0)

        @pl.when(first)
        def _():
            sm_ref[...] = upd
            dbm_ref[...] = dbm

        @pl.when(jnp.logical_not(first))
        def _():
            sm_ref[...] += upd
            dbm_ref[...] += dbm

    rest = 2 * SGU_W + 2 * D_MODEL
    bf_d, bf_s = jax.ShapeDtypeStruct((t_len, D_MODEL), BF16), jax.ShapeDtypeStruct((t_len, SSM_W), BF16)
    return _call(
        body, "bwd_mix", (t_len // TB,),
        [_rows(TB, D_MODEL), _rows(TB, IN_COLS), _rows(TB, SSM_W), _rows(TB, SSM_W),
         _rows(TB, SGU_W), _rows(TB, D_MODEL), _rows(TB, D_MODEL)] + [_whole()] * 10,
        [_rows(TB, N_STATE), _rows(TB, N_STATE), _rows(TB, SSM_W), _rows(TB, rest),
         _rows(TB, D_MODEL), _rows(TB, D_MODEL), _rows(TB, D_MODEL), _rows(TB, SSM_W),
         _rows(TB, SSM_W), _rows(TB, SSM_W), _rows(TB, SGU_W), _rows(TB, SSM_W),
         _acc(SUBLANE, SSM_W), _acc(CHUNK, SGU_W),
         pl.BlockSpec((SGU_G, CHUNK, CHUNK), lambda i: (0, 0, 0))],
        [jax.ShapeDtypeStruct((t_len, N_STATE), BF16), jax.ShapeDtypeStruct((t_len, N_STATE), BF16),
         jax.ShapeDtypeStruct((t_len, SSM_W), F32), jax.ShapeDtypeStruct((t_len, rest), BF16),
         bf_d, bf_d, bf_d, bf_s, bf_s, bf_s, bf_s, bf_s,
         jax.ShapeDtypeStruct((SUBLANE, SSM_W), F32), jax.ShapeDtypeStruct((CHUNK, SGU_W), F32),
         jax.ShapeDtypeStruct((SGU_G, CHUNK, CHUNK), F32)],
        [dx2, p, y0, z, mixed, ya, yb, w_out, w_pa, w_pb, w_glu, cre, cim, ws_st, wst_st, d_skip,
         g_sgu], jobs)


def _scan_bwd(dsr, dsi, str_, sti, tab_rev, jobs=()):
    t_len = dsr.shape[0]
    nblk = t_len // SUBLANE
    lb = SCAN_LANES

    def body(dr_ref, di_ref, sr_ref, si_ref, tab_ref, lr_ref, li_ref, dar_ref, dai_ref):
        tab_v = [tab_ref[q] for q in range(8)]
        row0 = lax.broadcasted_iota(jnp.int32, (SUBLANE, lb), 0) == 0
        tile = BF16_TILE

        def step(k, carry):
            cr, ci, acr, aci = carry
            base = pl.multiple_of((nblk - (k + 1) * SCAN_UNROLL) * SUBLANE, SCAN_UNROLL * SUBLANE)
            state = _load_blocks(sr_ref, si_ref, base)
            before = pl.ds(pl.multiple_of(jnp.maximum(base - tile, 0), tile), tile)
            has_before = jnp.where(base > 0, 1.0, 0.0)
            prev = (sr_ref[before, :].astype(F32)[tile - 1:tile] * has_before,
                    si_ref[before, :].astype(F32)[tile - 1:tile] * has_before)
            local = [_scan_local(xr, xi, tab_v, (7, 6, 4))
                     for xr, xi in _load_blocks(dr_ref, di_ref, base)]
            lam = [None] * SCAN_UNROLL
            for b in reversed(range(SCAN_UNROLL)):
                xr, xi = _scan_carry(*local[b], tab_v, cr, ci)
                lam[b] = (xr, xi)
                cr, ci = xr[0:1, :], xi[0:1, :]
                pr, pi = prev if b == 0 else (state[b - 1][0][SUBLANE - 1:], state[b - 1][1][SUBLANE - 1:])
                s_r = jnp.where(row0, pr, pltpu.roll(state[b][0], 1, 0))
                s_i = jnp.where(row0, pi, pltpu.roll(state[b][1], 1, 0))
                acr = acr + xr * s_r + xi * s_i
                aci = aci + xi * s_r - xr * s_i
            _store_blocks(lr_ref, li_ref, base, lam)
            return cr, ci, acr, aci

        zero = jnp.zeros((1, lb), F32)
        zacc = jnp.zeros((SUBLANE, lb), F32)
        _, _, acr, aci = lax.fori_loop(0, nblk // SCAN_UNROLL, step, (zero, zero, zacc, zacc))
        dar_ref[...] = acr
        dai_ref[...] = aci

    col = pl.BlockSpec((t_len, lb), lambda j: (0, j))
    small = pl.BlockSpec((SUBLANE, lb), lambda j: (0, j))
    return _call(
        body, "scan_bwd", (N_STATE // lb,),
        [col, col, col, col, pl.BlockSpec((8, SUBLANE, lb), lambda j: (0, 0, j))],
        [col, col, small, small],
        [jax.ShapeDtypeStruct((t_len, N_STATE), BF16)] * 2
        + [jax.ShapeDtypeStruct((SUBLANE, N_STATE), F32)] * 2,
        [dsr, dsi, str_, sti, tab_rev], jobs)


def _bwd_in(lam_r, lam_i, du_part, drest, x, dx2, g_mix, w_in, bre, bim, jobs=()):
    t_len = x.shape[0]
    cs = IN_COLS // N_CHIP

    def body(lr_ref, li_ref, du_ref, dr_ref, x_ref, dx2_ref, g_ref, w_ref, bre_ref, bim_ref,
             gx_ref, dp_ref, sm_ref):
        i = pl.program_id(0)
        du = du_ref[...] + jnp.concatenate(
            [_dot_nt(lr_ref[:, i * DIAG_N:(i + 1) * DIAG_N],
                     bre_ref[i * LANE:(i + 1) * LANE, i * DIAG_N:(i + 1) * DIAG_N])
             + _dot_nt(li_ref[:, i * DIAG_N:(i + 1) * DIAG_N],
                       bim_ref[i * LANE:(i + 1) * LANE, i * DIAG_N:(i + 1) * DIAG_N])
             for i in range(SSM_W // LANE)], axis=1)
        dp_ref[:, 0:SSM_W] = du.astype(BF16)
        dp_ref[:, SSM_W:] = dr_ref[...]
        dh = jnp.zeros(x_ref.shape, F32)
        for k in range(N_CHIP):
            dh = dh + _dot_nt(dp_ref[:, k * cs:(k + 1) * cs], w_ref[k])
        r, xh = _rms_stats(x_ref[...])
        gx_ref[...] = dx2_ref[...] + _rms_bwd(dh * g_ref[...], xh, r)
        upd = jnp.concatenate([jnp.sum(dh * xh, axis=0, keepdims=True),
                               jnp.zeros((SUBLANE - 1, D_MODEL), F32)], axis=0)

        @pl.when(i == 0)
        def _():
            sm_ref[...] = upd

        @pl.when(i > 0)
        def _():
            sm_ref[...] += upd

    tb = min(TB_WIDE, t_len)
    return _call(
        body, "bwd_in", (t_len // tb,),
        [_rows(tb, N_STATE), _rows(tb, N_STATE), _rows(tb, SSM_W), _rows(tb, IN_COLS - SSM_W),
         _rows(tb, D_MODEL), _rows(tb, D_MODEL)] + [_whole()] * 4,
        [_rows(tb, D_MODEL), _rows(tb, IN_COLS), _acc(SUBLANE, D_MODEL)],
        [jax.ShapeDtypeStruct((t_len, D_MODEL), F32), jax.ShapeDtypeStruct((t_len, IN_COLS), BF16),
         jax.ShapeDtypeStruct((SUBLANE, D_MODEL), F32)],
        [lam_r, lam_i, du_part, drest, x, dx2, g_mix, w_in, bre, bim], jobs)


def _matmul_tn(a, b, name, out_shape, grid_ij, a_blk, a_map, b_blk, b_map, o_blk, o_map, jobs=()):
    tk = a_blk[0]
    nk = a.shape[0] // tk
    assert nk * tk == a.shape[0] and nk > 0

    def body(a_ref, b_ref, o_ref, acc_ref):
        k = pl.program_id(2)

        @pl.when(k == 0)
        def _():
            acc_ref[...] = jnp.zeros_like(acc_ref)

        acc_ref[...] += lax.dot_general(a_ref[...].astype(BF16), b_ref[...].astype(BF16),
                                        (((0,), (0,)), ((), ())), preferred_element_type=F32)

        @pl.when(k == nk - 1)
        def _():
            o_ref[...] = acc_ref[...]

    outs, per_job = _call(
        body, name, (grid_ij[0], grid_ij[1], nk),
        [pl.BlockSpec(a_blk, a_map), pl.BlockSpec(b_blk, b_map)], [pl.BlockSpec(o_blk, o_map)],
        [jax.ShapeDtypeStruct(out_shape, F32)], [a, b], jobs,
        scratch=[pltpu.VMEM((a_blk[1], b_blk[1]), F32)])
    return outs[0], per_job


def _dw_rows(a, b, name, tm, tk):
    m, n = a.shape[1], b.shape[1]
    tk = min(tk, a.shape[0])
    return _matmul_tn(a, b, name, (m, n), (m // tm, 1),
                      (tk, tm), lambda i, j, k: (k, i), (tk, n), lambda i, j, k: (k, 0),
                      (tm, n), lambda i, j, k: (i, 0))[0]


def _dw_cols(a, b, name, tn, sharded, jobs=()):
    t_len, m = a.shape
    n = b.shape[1]

    def body(a_ref, b_ref, o_ref):
        o_ref[...] = lax.dot_general(a_ref[...].astype(BF16), b_ref[...].astype(BF16),
                                     (((0,), (0,)), ((), ())), preferred_element_type=F32)

    if sharded:
        o_spec, o_shape = pl.BlockSpec((None, m, tn), lambda j: (j, 0, 0)), (n // tn, m, tn)
    else:
        o_spec, o_shape = pl.BlockSpec((m, tn), lambda j: (0, j)), (m, n)
    outs, per_job = _call(body, name, (n // tn,),
                          [_whole(), pl.BlockSpec((t_len, tn), lambda j: (0, j))], [o_spec],
                          [jax.ShapeDtypeStruct(o_shape, F32)], [a, b], jobs)
    return outs[0], per_job


def _dw_tiles(a, b, name, tm, tn, jobs=()):
    t_len, m = a.shape
    n = b.shape[1]

    def body(a_ref, b_ref, o_ref):
        o_ref[...] = lax.dot_general(a_ref[...].astype(BF16), b_ref[...].astype(BF16),
                                     (((0,), (0,)), ((), ())), preferred_element_type=F32)

    outs, per_job = _call(body, name, (n // tn, m // tm),
                          [pl.BlockSpec((t_len, tm), lambda j, i: (0, i)),
                           pl.BlockSpec((t_len, tn), lambda j, i: (0, j))],
                          [pl.BlockSpec((None, tm, tn), lambda j, i: (j, i, 0))],
                          [jax.ShapeDtypeStruct((n // tn, m, tn), F32)], [a, b], jobs)
    return outs[0], per_job


def _dw_pair(a, m, b1, b2, name, jobs=()):
    t_len = a.shape[0]
    n_slab = DIAG_N // LANE
    rows_per_slab = LANE // n_slab

    def body(a_ref, b1_ref, b2_ref, o1_ref, o2_ref):
        for b_ref, o_ref in ((b1_ref, o1_ref), (b2_ref, o2_ref)):
            prod = lax.dot_general(a_ref[...].astype(BF16), b_ref[...].astype(BF16),
                                   (((0,), (0,)), ((), ())), preferred_element_type=F32)
            for j in range(n_slab):
                rows = slice(j * rows_per_slab, (j + 1) * rows_per_slab)
                o_ref[rows, :] = prod[rows, j * LANE:(j + 1) * LANE]

    tok = pl.BlockSpec((t_len, DIAG_N), lambda i: (0, i))
    out = pl.BlockSpec((LANE, LANE), lambda i: (i, 0))
    return _call(body, name, (m // LANE,),
                 [pl.BlockSpec((t_len, LANE), lambda i: (0, i)), tok, tok], [out, out],
                 [jax.ShapeDtypeStruct((m, LANE), F32)] * 2, [a, b1, b2], jobs)


def _prefetch_call(body, name, grid, scalars, in_specs, out_specs, out_shape, args):
    return pl.pallas_call(
        body, name=name,
        grid_spec=pltpu.PrefetchScalarGridSpec(num_scalar_prefetch=1, grid=grid, in_specs=in_specs,
                                               out_specs=out_specs),
        out_shape=out_shape, compiler_params=_params(len(grid)),
    )(scalars, *args)


def _place_shard(w, where, name, dtype, tr):
    rows, cols = w.shape

    def body(s_ref, w_ref, o_ref):
        o_ref[...] = w_ref[...].astype(dtype)

    return _prefetch_call(
        body, name, (rows // tr,), where,
        [pl.BlockSpec((tr, cols), lambda i, s: (i, 0))],
        pl.BlockSpec((None, tr, cols), lambda i, s: (s[0], i, 0)),
        jax.ShapeDtypeStruct((N_CHIP, rows, cols), dtype), [w])


def _place_shards(ws, where, name, dtype):
    n = len(ws)

    def body(s_ref, *refs):
        for t in range(n):
            refs[n + t][...] = refs[t][...].astype(dtype)

    return _prefetch_call(
        body, name, (1,), where,
        [pl.BlockSpec(w.shape, lambda i, s: (0, 0)) for w in ws],
        [pl.BlockSpec((None,) + w.shape, lambda i, s: (s[0], 0, 0)) for w in ws],
        [jax.ShapeDtypeStruct((N_CHIP,) + w.shape, dtype) for w in ws], ws)


def _add_sibling(gs, gots, where, name):
    n = len(gs)
    halves = [(g.shape[1] // 2, g.shape[2]) for g in gs]

    def body(s_ref, *refs):
        for t in range(n):
            refs[2 * n + t][...] = (refs[t][...] + refs[n + t][...]).astype(BF16)

    return _prefetch_call(
        body, name, (N_CHIP,), where,
        [pl.BlockSpec((None, hr, cs), lambda k, s: (k, s[1], 0)) for hr, cs in halves]
        + [pl.BlockSpec((None, hr, cs), lambda k, s: (k, 0, 0)) for hr, cs in halves],
        [pl.BlockSpec((None, hr, cs), lambda k, s: (k, 0, 0)) for hr, cs in halves],
        [jax.ShapeDtypeStruct((N_CHIP, hr, cs), BF16) for hr, cs in halves], list(gs) + list(gots))


def _add_chips(sums, gots, where, name):
    n = len(sums)
    halves = [s.shape[1:] for s in sums]

    def body(s_ref, *refs):
        for t in range(n):
            own_ref, got_ref = refs[t], refs[n + t]
            refs[2 * n + t][...] = ((own_ref[...].astype(F32) + got_ref[0].astype(F32))
                                    + got_ref[1].astype(F32)) + got_ref[2].astype(F32)

    return _prefetch_call(
        body, name, (1,), where,
        [pl.BlockSpec((None, hr, cs), lambda i, s: (s[0], 0, 0)) for hr, cs in halves]
        + [pl.BlockSpec((3, hr, cs), lambda i, s: (0, 0, 0)) for hr, cs in halves],
        [pl.BlockSpec((hr, cs), lambda i, s: (s[1], 0)) for hr, cs in halves],
        [jax.ShapeDtypeStruct((2 * hr, cs), F32) for hr, cs in halves], list(sums) + list(gots))


def _small_allreduce(pack):
    rows = pack.shape[0]
    half = rows // 2

    def body(in_ref, out_ref, sib_ref, slots_ref, s_a, r_a, s_b, r_b, s_c, r_c):
        x, y, c, chips = _place()
        k_me = 2 * x + y
        sib = (x, y, 1 - c)
        first = _remote(in_ref, sib_ref, s_a, r_a, sib)
        first.start()
        first.wait()
        mine = _half(rows, c)
        slots_ref[k_me] = in_ref[mine, :] + sib_ref[mine, :]
        cps = [_remote(slots_ref.at[k_me], slots_ref.at[k_me], s_b.at[j], r_b.at[j], (*ch, c))
               for j, ch in enumerate(chips)]
        for cp in cps:
            cp.start()
        for j, ch in enumerate(chips):
            slot = slots_ref.at[_chip_index(ch)]
            _remote(slot, slot, s_b.at[j], r_b.at[j], (*ch, c)).wait_recv()
        for cp in cps:
            cp.wait_send()
        out_ref[mine, :] = ((slots_ref[0] + slots_ref[1]) + slots_ref[2]) + slots_ref[3]
        last = _remote(out_ref.at[mine, :], out_ref.at[mine, :], s_c, r_c, sib)
        last.start()
        theirs = out_ref.at[_half(rows, 1 - c), :]
        _remote(theirs, theirs, s_c, r_c, sib).wait_recv()
        last.wait_send()

    return pl.pallas_call(
        body, name="small_allreduce", in_specs=[_whole()], out_specs=_whole(),
        out_shape=jax.ShapeDtypeStruct(pack.shape, F32),
        scratch_shapes=[pltpu.VMEM(pack.shape, F32), pltpu.VMEM((N_CHIP, half, LANE), F32),
                        pltpu.SemaphoreType.DMA, pltpu.SemaphoreType.DMA,
                        pltpu.SemaphoreType.DMA((3,)), pltpu.SemaphoreType.DMA((3,)),
                        pltpu.SemaphoreType.DMA, pltpu.SemaphoreType.DMA],
        compiler_params=_params(0),
    )(pack)


def _adamw_update(w_ref, g_ref, m_ref, v_ref, d_ref, mo_ref, vo_ref):
    gv = g_ref[...]
    mn = ADAM_B1 * m_ref[...] + (1.0 - ADAM_B1) * gv
    vn = ADAM_B2 * v_ref[...] + (1.0 - ADAM_B2) * (gv * gv)
    mo_ref[...] = mn
    vo_ref[...] = vn
    m_hat = mn / (1.0 - ADAM_B1 ** ADAM_STEP)
    v_hat = vn / (1.0 - ADAM_B2 ** ADAM_STEP)
    d_ref[...] = -ADAM_LR * (m_hat / (jnp.sqrt(v_hat) + ADAM_EPS) + ADAM_WD * w_ref[...])


def _adamw(w, g, m, v, name, tr):
    rows, cols = w.shape
    blk = _rows(tr, cols)

    def body(w_ref, g_ref, m_ref, v_ref, go_ref, d_ref, mo_ref, vo_ref):
        go_ref[...] = g_ref[...]
        _adamw_update(w_ref, g_ref, m_ref, v_ref, d_ref, mo_ref, vo_ref)

    return _call(body, name, (rows // tr,), [blk] * 4, [blk] * 4,
                 [jax.ShapeDtypeStruct(w.shape, F32)] * 4, [w, g, m, v])[0]


def _adamw_many(ws, gs, ms, vs, name):
    n = len(ws)

    def body(*refs):
        for t in range(n):
            _adamw_update(*[refs[q * n + t] for q in range(7)])

    specs = [pl.BlockSpec(a.shape, lambda i, nd=a.ndim: (0,) * nd) for a in ws]
    outs = pl.pallas_call(
        body, name=name, grid=(1,), in_specs=specs * 4, out_specs=specs * 3,
        out_shape=[jax.ShapeDtypeStruct(a.shape, F32) for _ in range(3) for a in ws],
        compiler_params=_params(1),
    )(*ws, *gs, *ms, *vs)
    return outs[:n], outs[n:2 * n], outs[2 * n:]


def _ssm_discretize(a_re, a_im, log_dt, b_re, b_im):
    dt = jnp.exp(log_dt)[:, None]
    mag = jnp.exp(dt * a_re)
    abr = mag * jnp.cos(dt * a_im)
    abi = mag * jnp.sin(dt * a_im)
    den = a_re * a_re + a_im * a_im
    nr = abr - 1.0
    ni = abi
    f_re = (nr * a_re + ni * a_im) / den
    f_im = (ni * a_re - nr * a_im) / den
    bbr = f_re[..., None] * b_re - f_im[..., None] * b_im
    bbi = f_re[..., None] * b_im + f_im[..., None] * b_re
    return abr, abi, bbr, bbi


def _scan_tables(abr, abi):
    ar = abr.reshape(1, N_STATE)
    ai = abi.reshape(1, N_STATE)
    pr, pi = [ar], [ai]
    for _ in range(SUBLANE - 1):
        pr, pi = pr + [pr[-1] * ar - pi[-1] * ai], pi + [pr[-1] * ai + pi[-1] * ar]
    row = jnp.arange(SUBLANE)[:, None]
    tabs = []
    for d in (1, 2, 4):
        tabs.append(jnp.where(row >= d, pr[d - 1], 0.0))
        tabs.append(jnp.where(row >= d, pi[d - 1], 0.0))
    tabs.append(jnp.concatenate(pr, axis=0))
    tabs.append(jnp.concatenate(pi, axis=0))
    fwd = jnp.stack(tabs)
    sign = jnp.array([1.0, -1.0] * 4, F32)[:, None, None]
    return fwd, fwd[:, ::-1, :] * sign


def _block_diag_b(bb):
    strip = bb.transpose(2, 0, 1).reshape(SSM_H, N_STATE)
    rows = lax.broadcasted_iota(jnp.int32, (SSM_W, N_STATE), 0) // SSM_H
    cols = lax.broadcasted_iota(jnp.int32, (SSM_W, N_STATE), 1) // SSM_P
    return jnp.where(rows == cols, jnp.tile(strip, (SSM_G, 1)), 0.0).astype(BF16)


def _block_diag_c(cc):
    strip = cc.transpose(0, 2, 1).reshape(N_STATE, SSM_H)
    rows = lax.broadcasted_iota(jnp.int32, (N_STATE, SSM_W), 0) // SSM_P
    cols = lax.broadcasted_iota(jnp.int32, (N_STATE, SSM_W), 1) // SSM_H
    return jnp.where(rows == cols, jnp.tile(strip, (1, SSM_G)), 0.0).astype(BF16)


SMALL_SHAPES = {
    "g_mix": (D_MODEL,), "a_re": (SSM_G, SSM_P), "a_im": (SSM_G, SSM_P), "log_dt": (SSM_G,),
    "b_re": (SSM_G, SSM_P, SSM_H), "b_im": (SSM_G, SSM_P, SSM_H),
    "c_re": (SSM_G, SSM_H, SSM_P), "c_im": (SSM_G, SSM_H, SSM_P),
    "d_skip": (SSM_W,), "b_glu": (SSM_W,), "g_sgu": (SGU_W,), "w_s": (SGU_G, CHUNK, CHUNK),
    "b_s": (SGU_G, CHUNK), "g_ffn": (D_MODEL,), "conv_b": (2 * D_FF,), "g_final": (D_MODEL,),
}
PACK_ITEMS = [("loss", (1,))] + [(n, SMALL_SHAPES[n]) for n in SMALL] + [("conv_w", (3, 2 * D_FF))]
TILE = SUBLANE * LANE


def _item_rows(shape):
    return -(-math.prod(shape) // TILE) * SUBLANE


PACK_ROWS = -(-sum(_item_rows(s) for _, s in PACK_ITEMS) // (2 * SUBLANE)) * (2 * SUBLANE)


def _pack(values):
    parts, used = [], 0
    for name, shape in PACK_ITEMS:
        size, rows = math.prod(shape), _item_rows(shape)
        if name in values:
            flat = values[name].astype(F32).reshape(size)
            if rows * LANE > size:
                flat = jnp.pad(flat, (0, rows * LANE - size))
            parts.append(flat.reshape(rows, LANE))
        else:
            parts.append(jnp.zeros((rows, LANE), F32))
        used += rows
    if PACK_ROWS > used:
        parts.append(jnp.zeros((PACK_ROWS - used, LANE), F32))
    return jnp.concatenate(parts, axis=0)


def _unpack(pack):
    out, off = {}, 0
    for name, shape in PACK_ITEMS:
        rows = _item_rows(shape)
        out[name] = pack[off:off + rows].reshape(rows * LANE)[:math.prod(shape)].reshape(shape)
        off += rows
    return out


PLACE_ROWS = {"w_in": 256, "w_up": 256, "w_down": 352, "w_out": 256, "w_proj_a": 256,
              "w_proj_b": 256, "w_glu": 128}


def kernel(x, g_mix, w_in, a_re, a_im, log_dt, b_re, b_im, c_re, c_im, d_skip, w_glu, b_glu, w_proj_a, g_sgu, w_s, b_s, w_proj_b, w_out, g_ffn, w_up, conv_w, conv_b, w_down, g_final, loss_target, m_g_mix, m_w_in, m_a_re, m_a_im, m_log_dt, m_b_re, m_b_im, m_c_re, m_c_im, m_d_skip, m_w_glu, m_b_glu, m_w_proj_a, m_g_sgu, m_w_s, m_b_s, m_w_proj_b, m_w_out, m_g_ffn, m_w_up, m_conv_w, m_conv_b, m_w_down, m_g_final, v_g_mix, v_w_in, v_a_re, v_a_im, v_log_dt, v_b_re, v_b_im, v_c_re, v_c_im, v_d_skip, v_w_glu, v_b_glu, v_w_proj_a, v_g_sgu, v_w_s, v_b_s, v_w_proj_b, v_w_out, v_g_ffn, v_w_up, v_conv_w, v_conv_b, v_w_down, v_g_final):
    given = dict(locals())
    w = {n: given[n] for n in WEIGHTS}
    m = {n: given["m_" + n] for n in WEIGHTS}
    v = {n: given["v_" + n] for n in WEIGHTS}

    def shard2d(a):
        return a.reshape(a.shape[-2], a.shape[-1])

    chip = 2 * lax.axis_index("x") + lax.axis_index("y")
    where = jnp.stack([chip, lax.axis_index("c")]).astype(jnp.int32)
    xs, target = x[0], loss_target[0]
    small = {n: w[n].reshape(SMALL_SHAPES[n]) for n in SMALL}

    (abr, abi, bbr, bbi), disc_vjp = jax.vjp(_ssm_discretize, small["a_re"], small["a_im"],
                                             small["log_dt"], small["b_re"], small["b_im"])
    tab_f, tab_r = _scan_tables(abr, abi)
    bre = _block_diag_b(bbr)
    bim = _block_diag_b(bbi)
    cre = _block_diag_c(small["c_re"])
    cim = _block_diag_c(small["c_im"])
    tril = jnp.tril(jnp.ones((CHUNK, CHUNK), dtype=bool))
    ws = jnp.where(tril[None], small["w_s"], 0.0)
    ws_st = ws.reshape(SGU_G // 2, 2 * CHUNK, CHUNK).astype(BF16)
    wst_st = ws.transpose(0, 2, 1).reshape(SGU_G // 2, 2 * CHUNK, CHUNK).astype(BF16)
    bmat = jnp.repeat(small["b_s"].T, SGU_D, axis=1)
    g_mix2 = small["g_mix"].reshape(1, D_MODEL)
    g_ffn2 = small["g_ffn"].reshape(1, D_MODEL)
    g_final2 = small["g_final"].reshape(1, D_MODEL)
    g_sgu2 = small["g_sgu"].reshape(1, SGU_W)
    d_skip2 = small["d_skip"].reshape(1, SSM_W)
    b_glu2 = small["b_glu"].reshape(1, SSM_W)
    conv_b2 = small["conv_b"].reshape(1, 2 * D_FF)

    gat = {"w_in": _place_shard(shard2d(w["w_in"]), where, "place_w_in", BF16, PLACE_ROWS["w_in"])}
    gat.update(zip(BIG[1:], _place_shards([shard2d(w[n]) for n in BIG[1:]], where, "place_rest", BF16)))
    gat["conv_w"] = _place_shard(shard2d(w["conv_w"]), where, "place_conv_w", F32, 3)
    all_rows = (0, D_MODEL)
    (gat["w_in"],), = _comm("gather_in", [_job_gather(
        [gat["w_in"]], [(0, all_rows, ICI, (0.0, 0.5)), (0, all_rows, SIBLING, (0.5, 1.0))])])
    mixers = ["w_glu", "w_proj_a", "w_proj_b", "w_out"]
    rows = {n: (0, gat[n].shape[1]) for n in mixers}
    down_a, down_b = (0, D_FF // 8), (D_FF // 8, D_FF // 8)
    up_a, up_b = (0, 3 * D_MODEL // 8), (3 * D_MODEL // 8, 5 * D_MODEL // 8)
    span = (0.0, 1.0)

    names = mixers + ["conv_w", "w_down"]
    (p, h1, bur, bui), (got,) = _fwd_in(
        xs, g_mix2, gat["w_in"], bre, bim,
        [_job_gather([gat[n] for n in names],
                     [(i, rows[n], ICI, span) for i, n in enumerate(mixers)]
                     + [(4, None, ICI, span), (5, down_a, ICI, span)])])
    gat.update(zip(names, got))
    names = mixers + ["w_down", "w_up"]
    (str_, sti), (got,) = _scan_fwd(
        bur, bui, tab_f,
        [_job_gather([gat[n] for n in names],
                     [(i, rows[n], SIBLING, span) for i, n in enumerate(mixers)]
                     + [(4, down_a, SIBLING, span), (4, down_b, ICI, span), (5, up_a, ICI, span)])])
    gat.update(zip(names, got))
    w_glu_f = gat["w_glu"].reshape(SSM_W, SSM_W)
    w_out_f = gat["w_out"].reshape(D_MODEL, D_MODEL)
    conv_w_f = gat["conv_w"].transpose(1, 0, 2).reshape(3, 2 * D_FF)
    (x2, y0, z, mixed, ya, yb), ((gat["w_down"], gat["w_up"]),) = _fwd_mix(
        xs, p, str_, sti, cre, cim, d_skip2, w_glu_f, b_glu2, gat["w_proj_a"], g_sgu2, ws_st, bmat,
        gat["w_proj_b"], w_out_f,
        [_job_gather([gat["w_down"], gat["w_up"]],
                     [(0, down_b, SIBLING, span), (1, up_a, SIBLING, span),
                      (1, up_b, ICI, (0.0, 0.75)), (1, up_b, SIBLING, (0.75, 1.0))])])
    w_down_f = gat["w_down"].reshape(D_FF, D_MODEL)
    up, act, f, h2, dx3, sm_ffn = _fwd_ffn(x2, target, g_ffn2, gat["w_up"], conv_w_f, conv_b2,
                                           w_down_f, g_final2)

    def leg1_done(names, got):
        return _add_sibling([part[n] for n in names], got, where, "add_sibling_" + names[0])

    def leg2_done(names, sums, got):
        return _add_chips(sums, got, where, "add_chips_" + names[0])

    part, red = {}, {}
    part["w_down"] = _dw_rows(f, dx3, "dw_down", D_FF // 2, 4 * TK).reshape(
        N_CHIP, D_FF // N_CHIP, D_MODEL)
    (dx2, dup, sm_conv, sm_gffn), (got,) = _bwd_ffn(
        dx3, up, act, x2, g_ffn2, gat["w_up"], conv_w_f, w_down_f,
        [_job_sibling_halves([part["w_down"]])])
    sum_down = leg1_done(["w_down"], got)
    part["w_up"], (got,) = _dw_tiles(h2, dup, "dw_up", D_MODEL // 2, 2 * D_FF // N_CHIP,
                                     [_job_to_owner(sum_down)])
    red_down = leg2_done(["w_down"], sum_down, got)
    ((dsr, dsi, du_part, drest, mrg, dya, dyb, yap, dz, y1, sgu, dy0, sm_mix, dbm, dws),
     (got, (red["w_down"],))) = _bwd_mix(
        dx2, p, y0, z, mixed, ya, yb, w_out_f, gat["w_proj_a"], gat["w_proj_b"], w_glu_f, cre, cim,
        ws_st, wst_st, d_skip2, g_sgu2,
        [_job_sibling_halves([part["w_up"]]), _job_swap_halves(red_down)])
    sum_up = leg1_done(["w_up"], got)
    (lam_r, lam_i, dar8, dai8), (got,) = _scan_bwd(dsr, dsi, str_, sti, tab_r, [_job_to_owner(sum_up)])
    red_up = leg2_done(["w_up"], sum_up, got)
    mix4 = ["w_out", "w_proj_a", "w_proj_b", "w_glu"]
    part["w_out"] = _dw_cols(mrg, dx2, "dw_out", D_MODEL // 2, False)[0].reshape(
        N_CHIP, D_MODEL // N_CHIP, D_MODEL)
    part["w_proj_a"] = _dw_cols(yap, dya, "dw_proj_a", D_MODEL // N_CHIP, True)[0]
    part["w_proj_b"] = _dw_cols(sgu, dyb, "dw_proj_b", D_MODEL // N_CHIP, True)[0]
    part["w_glu"] = _dw_cols(y1, dz, "dw_glu", SSM_W, False)[0].reshape(
        N_CHIP, SSM_W // N_CHIP, SSM_W)
    got, (red["w_up"],) = _comm(
        "mixer_sibling_halves", [_job_sibling_halves([part[n] for n in mix4]), _job_swap_halves(red_up)])
    (grad_x, dp, sm_gmix), _ = _bwd_in(
        lam_r, lam_i, du_part, drest, xs, dx2, g_mix2, gat["w_in"], bre, bim)
    sums_m = leg1_done(mix4, got)
    part["w_in"], (got,) = _dw_cols(h1, dp, "dw_in", IN_COLS // N_CHIP, True, [_job_to_owner(sums_m)])
    red_m = leg2_done(mix4, sums_m, got)
    (dbd_r, dbd_i), (got, done_m) = _dw_pair(
        p, SSM_W, lam_r, lam_i, "db_bar",
        [_job_sibling_halves([part["w_in"]]), _job_swap_halves(red_m)])
    red.update(zip(mix4, done_m))
    sum_in = leg1_done(["w_in"], got)
    (dcd_r, dcd_i), (got,) = _dw_pair(dy0, SSM_W, str_, sti, "dc", [_job_to_owner(sum_in)])
    red_in = leg2_done(["w_in"], sum_in, got)
    (red["w_in"],), = _comm("swap_w_in", [_job_swap_halves(red_in)])

    def pick_c(slabs):
        two = LANE // SSM_P
        return jnp.einsum("jshsp->jshp", slabs.reshape(SSM_G // two, two, SSM_H, two, SSM_P)
                          ).reshape(SSM_G, SSM_H, SSM_P)

    def pick_b(slabs):
        return pick_c(slabs).transpose(0, 2, 1)

    dabr = jnp.sum(dar8, axis=0).reshape(SSM_G, SSM_P)
    dabi = jnp.sum(dai8, axis=0).reshape(SSM_G, SSM_P)
    d_a_re, d_a_im, d_log_dt, d_b_re, d_b_im = disc_vjp((dabr, dabi, pick_b(dbd_r), pick_b(dbd_i)))
    gsmall = {
        "g_mix": sm_gmix[0], "a_re": d_a_re, "a_im": d_a_im, "log_dt": d_log_dt,
        "b_re": d_b_re, "b_im": d_b_im, "c_re": pick_c(dcd_r), "c_im": -pick_c(dcd_i),
        "d_skip": sm_mix[0], "b_glu": sm_mix[1], "g_sgu": sm_mix[2],
        "w_s": jnp.where(tril[None], dws, 0.0),
        "b_s": dbm.reshape(CHUNK, SGU_G, SGU_D).sum(-1).T,
        "g_ffn": sm_gffn[0], "conv_b": sm_conv[3], "g_final": sm_ffn[0],
        "conv_w": sm_conv[0:3], "loss": sm_ffn[1, 0:1],
    }

    total_pack = _small_allreduce(_pack(gsmall))
    total = _unpack(total_pack)
    grads = dict(red)
    cs = 2 * D_FF // N_CHIP
    grads["conv_w"] = lax.dynamic_slice(total["conv_w"], (0, chip * cs), (3, cs))
    delta, new_m, new_v = {}, {}, {}
    for n in BIG + ("conv_w",):
        grads[n], delta[n], new_m[n], new_v[n] = _adamw(
            shard2d(w[n]), grads[n], shard2d(m[n]), shard2d(v[n]), "adamw_" + n, PLACE_ROWS.get(n, 3))
    for n in SMALL:
        grads[n] = total[n].reshape(w[n].shape)
    ud, um, uv = _adamw_many(*[[d[n] for n in SMALL] for d in (w, grads, m, v)], "adamw_small")
    for i, n in enumerate(SMALL):
        delta[n], new_m[n], new_v[n] = ud[i], um[i], uv[i]

    def like(d):
        return [d[n].reshape(w[n].shape) for n in WEIGHTS]

    return (total["loss"].reshape(()), grad_x.reshape(x.shape), *like(grads), *like(delta),
            *like(new_m), *like(new_v))
```

```python
import math

import jax
import jax.numpy as jnp
from jax import lax
from jax.experimental import pallas as pl
from jax.experimental.pallas import tpu as pltpu

F32 = jnp.float32
BF16 = jnp.bfloat16
MESH = pl.DeviceIdType.MESH

D_MODEL = 1024
SSM_W = 512
SSM_G = 32
SSM_H = 16
SSM_P = 64
N_STATE = SSM_G * SSM_P
DIAG_N = 128 * SSM_P // SSM_H
SGU_W = 512
SGU_G = 8
SGU_D = 64
CHUNK = 128
D_FF = 2816
IN_COLS = 3584
EPS = 1e-6
N_CHIP = 4

ADAM_LR = 0.001
ADAM_B1 = 0.9
ADAM_B2 = 0.999
ADAM_EPS = 1e-08
ADAM_WD = 0.01
ADAM_STEP = 10

SUBLANE = 8
LANE = 128
VMEM_LIMIT = 56 * 1024 * 1024
TB = 256
TB_WIDE = 512
TK = 512
SCAN_LANES = 256
SCAN_UNROLL = 4
HALO = SUBLANE

BIG = ("w_in", "w_up", "w_down", "w_out", "w_proj_a", "w_proj_b", "w_glu")
SMALL = ("g_mix", "a_re", "a_im", "log_dt", "b_re", "b_im", "c_re", "c_im", "d_skip", "b_glu",
         "g_sgu", "w_s", "b_s", "g_ffn", "conv_b", "g_final")
WEIGHTS = ("g_mix", "w_in", "a_re", "a_im", "log_dt", "b_re", "b_im", "c_re", "c_im", "d_skip",
           "w_glu", "b_glu", "w_proj_a", "g_sgu", "w_s", "b_s", "w_proj_b", "w_out", "g_ffn",
           "w_up", "conv_w", "conv_b", "w_down", "g_final")

ANY = pl.BlockSpec(memory_space=pl.ANY)


def _params(n_grid):
    return pltpu.CompilerParams(dimension_semantics=("arbitrary",) * n_grid if n_grid else None,
                                vmem_limit_bytes=VMEM_LIMIT)


def _whole():
    return pl.BlockSpec(memory_space=pltpu.VMEM)


def _rows(tb, ncol):
    return pl.BlockSpec((tb, ncol), lambda i: (i, 0))


def _acc(nrow, ncol):
    return pl.BlockSpec((nrow, ncol), lambda i: (0, 0))


def _dot(a, b):
    return jnp.dot(a.astype(BF16), b.astype(BF16), preferred_element_type=F32)


def _dot_nt(a, b):
    return lax.dot_general(a.astype(BF16), b.astype(BF16), (((1,), (1,)), ((), ())),
                           preferred_element_type=F32)


def _sigmoid(v):
    return 0.5 * jnp.tanh(0.5 * v) + 0.5


_GELU_C = math.sqrt(2.0 / math.pi)


def _gelu(v):
    return 0.5 * v * (1.0 + jnp.tanh(_GELU_C * (v + 0.044715 * v * v * v)))


def _gelu_and_grad(v):
    v2 = v * v
    t = jnp.tanh(_GELU_C * v * (1.0 + 0.044715 * v2))
    half = 0.5 * (1.0 + t)
    return v * half, half + 0.5 * v * (1.0 - t * t) * _GELU_C * (1.0 + 3.0 * 0.044715 * v2)


def _rms_stats(v):
    r = lax.rsqrt(jnp.mean(v * v, axis=-1, keepdims=True) + EPS)
    return r, v * r


def _rms_bwd(dxh, xh, r):
    return r * (dxh - xh * jnp.mean(dxh * xh, axis=-1, keepdims=True))


def _place():
    x, y, c = lax.axis_index("x"), lax.axis_index("y"), lax.axis_index("c")
    chips = [(1 - x, y), (x, 1 - y), (1 - x, 1 - y)]
    return x, y, c, chips


def _chip_index(chip):
    return 2 * chip[0] + chip[1]


def _remote(src, dst, send_sem, recv_sem, device):
    return pltpu.make_async_remote_copy(src_ref=src, dst_ref=dst, send_sem=send_sem,
                                        recv_sem=recv_sem, device_id=device, device_id_type=MESH)


def _half(ref_rows, c):
    hr = ref_rows // 2
    return pl.ds(pl.multiple_of(c * hr, SUBLANE), hr)


class _Job:
    def __init__(self, hooks, n_sem, ins=(), inouts=(), outs=()):
        self.hooks, self.n_sem = list(hooks), n_sem
        self.ins, self.inouts, self.outs = list(ins), list(inouts), list(outs)


def _whole_span(start, finish):
    return [(0.0, "start", start), (1.0, "finish", finish)]


ICI, SIBLING = "ici", "sibling"


def _job_gather(bufs, legs):
    def copies(io, leg, first):
        b, window, kind, _ = legs[leg]
        x, y, c, chips = _place()
        k_me = 2 * x + y
        out = []
        for j, ch in enumerate(chips):
            k = _chip_index(ch)
            if window is None:
                src, land, dev = io[b].at[k_me], io[b].at[k], (*ch, c)
            else:
                r0, rows = window
                mine = pl.ds(pl.multiple_of(r0 + c * (rows // 2), SUBLANE), rows // 2)
                theirs = pl.ds(pl.multiple_of(r0 + (1 - c) * (rows // 2), SUBLANE), rows // 2)
                if kind == ICI:
                    src, land, dev = io[b].at[k_me, mine, :], io[b].at[k, mine, :], (*ch, c)
                else:
                    src, land, dev = io[b].at[k, mine, :], io[b].at[k, theirs, :], (x, y, 1 - c)
            out.append((src, land, first + j, dev))
        return out

    def starter(leg):
        def start(ins, io, outs, ssem, rsem):
            for src, _, i, dev in copies(io, leg, 3 * leg):
                _remote(src, src, ssem(i), rsem(i), dev).start()
        return start

    def finisher(leg):
        def finish(ins, io, outs, ssem, rsem):
            cps = copies(io, leg, 3 * leg)
            for _, land, i, dev in cps:
                _remote(land, land, ssem(i), rsem(i), dev).wait_recv()
            for src, _, i, dev in cps:
                _remote(src, src, ssem(i), rsem(i), dev).wait_send()
        return finish

    hooks = []
    for leg, (_, _, _, (begin, end)) in enumerate(legs):
        hooks += [(begin, "start", starter(leg)), (end, "finish", finisher(leg))]
    return _Job(hooks, 3 * len(legs), inouts=bufs)


def _job_gather_now(buf):
    rows = buf.shape[1]

    def run(ins, io, outs, ssem, rsem):
        x, y, c, chips = _place()
        k_me = 2 * x + y
        sib = (x, y, 1 - c)
        mine, theirs = _half(rows, c), _half(rows, 1 - c)
        own = io[0].at[k_me, mine, :]
        sends = [_remote(own, own, ssem(j), rsem(j), (*ch, c)) for j, ch in enumerate(chips)]
        for cp in sends:
            cp.start()
        passed = []
        for j, ch in enumerate(chips):
            landed = io[0].at[_chip_index(ch), mine, :]
            _remote(landed, landed, ssem(j), rsem(j), sib).wait_recv()
            cp = _remote(landed, landed, ssem(3 + j), rsem(3 + j), sib)
            cp.start()
            passed.append(cp)
        for j, ch in enumerate(chips):
            landed = io[0].at[_chip_index(ch), theirs, :]
            _remote(landed, landed, ssem(3 + j), rsem(3 + j), sib).wait_recv()
        for cp in sends + passed:
            cp.wait_send()

    return _Job([(0.0, "start", run)], 6, inouts=[buf])


def _job_sibling_halves(grads):
    n = len(grads)

    def build(ins, outs, ssem, rsem):
        x, y, c, _ = _place()
        return [_remote(ins[t].at[:, _half(grads[t].shape[1], 1 - c), :], outs[t], ssem(t), rsem(t),
                        (x, y, 1 - c)) for t in range(n)]

    def start(ins, io, outs, ssem, rsem):
        for cp in build(ins, outs, ssem, rsem):
            cp.start()

    def finish(ins, io, outs, ssem, rsem):
        for cp in build(ins, outs, ssem, rsem):
            cp.wait()

    return _Job(_whole_span(start, finish), n, ins=grads,
                outs=[jax.ShapeDtypeStruct((N_CHIP, g.shape[1] // 2, g.shape[2]), F32) for g in grads])


def _job_to_owner(sums):
    n = len(sums)

    def build(ins, outs, ssem, rsem):
        x, y, c, chips = _place()
        return [_remote(ins[t].at[_chip_index(ch)], outs[t].at[j], ssem(3 * t + j), rsem(3 * t + j),
                        (*ch, c)) for t in range(n) for j, ch in enumerate(chips)]

    def start(ins, io, outs, ssem, rsem):
        for cp in build(ins, outs, ssem, rsem):
            cp.start()

    def finish(ins, io, outs, ssem, rsem):
        for cp in build(ins, outs, ssem, rsem):
            cp.wait()

    return _Job(_whole_span(start, finish), 3 * n, ins=sums,
                outs=[jax.ShapeDtypeStruct((3,) + s.shape[1:], s.dtype) for s in sums])


def _job_swap_halves(bufs):
    n = len(bufs)

    def start(ins, io, outs, ssem, rsem):
        x, y, c, _ = _place()
        for t in range(n):
            mine = io[t].at[_half(bufs[t].shape[0], c), :]
            _remote(mine, mine, ssem(t), rsem(t), (x, y, 1 - c)).start()

    def finish(ins, io, outs, ssem, rsem):
        x, y, c, _ = _place()
        for t in range(n):
            theirs = io[t].at[_half(bufs[t].shape[0], 1 - c), :]
            _remote(theirs, theirs, ssem(t), rsem(t), (x, y, 1 - c)).wait_recv()
        for t in range(n):
            mine = io[t].at[_half(bufs[t].shape[0], c), :]
            _remote(mine, mine, ssem(t), rsem(t), (x, y, 1 - c)).wait_send()

    return _Job(_whole_span(start, finish), n, inouts=bufs)


def _call(body, name, grid, in_specs, out_specs, out_shape, args, jobs=(), scratch=()):
    n_in, n_out, n_scr = len(args), len(out_shape), len(scratch)
    job_in = [a for jb in jobs for a in jb.ins + jb.inouts]
    job_out = [s for jb in jobs
               for s in [jax.ShapeDtypeStruct(a.shape, a.dtype) for a in jb.inouts] + jb.outs]
    aliases, pos_in, pos_out = {}, n_in, n_out
    for jb in jobs:
        pos_in += len(jb.ins)
        for _ in jb.inouts:
            aliases[pos_in] = pos_out
            pos_in += 1
            pos_out += 1
        pos_out += len(jb.outs)
    n_sem = sum(jb.n_sem for jb in jobs)

    def wrapped(*refs):
        c_in = refs[:n_in]
        j_in = refs[n_in:n_in + len(job_in)]
        c_out = refs[n_in + len(job_in):n_in + len(job_in) + n_out]
        j_out = refs[n_in + len(job_in) + n_out:n_in + len(job_in) + n_out + len(job_out)]
        rest = refs[n_in + len(job_in) + n_out + len(job_out):]
        c_scr = rest[:n_scr]
        views, pi, po, ps = [], 0, 0, 0
        for jb in jobs:
            ins = j_in[pi:pi + len(jb.ins)]
            pi += len(jb.ins) + len(jb.inouts)
            io = j_out[po:po + len(jb.inouts)]
            new = j_out[po + len(jb.inouts):po + len(jb.inouts) + len(jb.outs)]
            po += len(jb.inouts) + len(jb.outs)
            send = (lambda i, o=ps: rest[n_scr].at[o + i])
            recv = (lambda i, o=ps: rest[n_scr + 1].at[o + i])
            ps += jb.n_sem
            views.append((ins, io, new, send, recv))

        def run(frac):
            for kind in ("finish", "start"):
                for jb, vw in zip(jobs, views):
                    for at, what, fn in jb.hooks:
                        if at == frac and what == kind:
                            fn(*vw)

        fracs = sorted({at for jb in jobs for at, _, _ in jb.hooks})
        if not grid:
            for frac in fracs:
                run(frac)
            return
        if jobs:
            assert len(grid) == 1 or set(fracs) <= {0.0, 1.0}
            first = pl.program_id(0) == 0
            last = pl.program_id(0) == grid[0] - 1
            for d in range(1, len(grid)):
                first = jnp.logical_and(first, pl.program_id(d) == 0)
                last = jnp.logical_and(last, pl.program_id(d) == grid[d] - 1)
            for frac in fracs:
                if frac < 1.0:
                    at_step = first if frac == 0.0 else pl.program_id(0) == int(frac * grid[0])
                    pl.when(at_step)(lambda frac=frac: run(frac))
        body(*c_in, *c_out, *c_scr)
        if jobs and 1.0 in fracs:
            pl.when(last)(lambda: run(1.0))

    sems = [pltpu.SemaphoreType.DMA((n_sem,)), pltpu.SemaphoreType.DMA((n_sem,))] if jobs else []
    kwargs = dict(grid=grid) if grid else {}
    res = pl.pallas_call(
        wrapped, name=name, in_specs=list(in_specs) + [ANY] * len(job_in),
        out_specs=list(out_specs) + [ANY] * len(job_out),
        out_shape=list(out_shape) + job_out, scratch_shapes=list(scratch) + sems,
        input_output_aliases=aliases, compiler_params=_params(len(grid)), **kwargs,
    )(*args, *job_in)
    outs, pos, per_job = list(res[:n_out]), n_out, []
    for jb in jobs:
        k = len(jb.inouts) + len(jb.outs)
        per_job.append(list(res[pos:pos + k]))
        pos += k
    return outs, per_job


def _comm(name, jobs):
    return _call(None, name, (), [], [], [], [], jobs)[1]


def _fwd_in(x, g_mix, w_in, bre, bim, jobs=()):
    t_len = x.shape[0]
    cs = IN_COLS // N_CHIP

    def body(x_ref, g_ref, w_ref, bre_ref, bim_ref, p_ref, h_ref, bur_ref, bui_ref):
        xv = x_ref[...]
        r, xh = _rms_stats(xv)
        h = (xh * g_ref[...]).astype(BF16)
        h_ref[...] = h
        for k in range(N_CHIP):
            p_ref[:, k * cs:(k + 1) * cs] = jnp.dot(h, w_ref[k],
                                                    preferred_element_type=F32).astype(BF16)
        u = p_ref[:, 0:SSM_W]
        for i in range(SSM_W // LANE):
            rows, cols = slice(i * LANE, (i + 1) * LANE), slice(i * DIAG_N, (i + 1) * DIAG_N)
            bur_ref[:, cols] = jnp.dot(u[:, rows], bre_ref[rows, cols],
                                       preferred_element_type=F32).astype(BF16)
            bui_ref[:, cols] = jnp.dot(u[:, rows], bim_ref[rows, cols],
                                       preferred_element_type=F32).astype(BF16)

    tb = min(TB_WIDE, t_len)
    return _call(
        body, "fwd_in", (t_len // tb,),
        [_rows(tb, D_MODEL), _whole(), _whole(), _whole(), _whole()],
        [_rows(tb, IN_COLS), _rows(tb, D_MODEL), _rows(tb, N_STATE), _rows(tb, N_STATE)],
        [jax.ShapeDtypeStruct((t_len, IN_COLS), BF16), jax.ShapeDtypeStruct((t_len, D_MODEL), BF16),
         jax.ShapeDtypeStruct((t_len, N_STATE), BF16), jax.ShapeDtypeStruct((t_len, N_STATE), BF16)],
        [x, g_mix, w_in, bre, bim], jobs)


def _scan_local(xr, xi, tab, shifts):
    for q, s in enumerate(shifts):
        ar, ai = tab[2 * q], tab[2 * q + 1]
        rr = pltpu.roll(xr, s, 0)
        ri = pltpu.roll(xi, s, 0)
        xr, xi = xr + ar * rr - ai * ri, xi + ar * ri + ai * rr
    return xr, xi


def _scan_carry(xr, xi, tab, cr, ci):
    pr, pi = tab[6], tab[7]
    return xr + pr * cr - pi * ci, xi + pr * ci + pi * cr


BF16_TILE = 2 * SUBLANE


def _load_blocks(r_ref, i_ref, base):
    out = []
    for q in range(SCAN_UNROLL // 2):
        rows = pl.ds(pl.multiple_of(base + q * BF16_TILE, BF16_TILE), BF16_TILE)
        vr, vi = r_ref[rows, :].astype(F32), i_ref[rows, :].astype(F32)
        out += [(vr[:SUBLANE], vi[:SUBLANE]), (vr[SUBLANE:], vi[SUBLANE:])]
    return out


def _store_blocks(r_ref, i_ref, base, blocks):
    for q in range(SCAN_UNROLL // 2):
        rows = pl.ds(pl.multiple_of(base + q * BF16_TILE, BF16_TILE), BF16_TILE)
        r_ref[rows, :] = jnp.concatenate([blocks[2 * q][0], blocks[2 * q + 1][0]], 0).astype(r_ref.dtype)
        i_ref[rows, :] = jnp.concatenate([blocks[2 * q][1], blocks[2 * q + 1][1]], 0).astype(i_ref.dtype)


def _scan_fwd(bur, bui, tab, jobs=()):
    t_len = bur.shape[0]
    nblk = t_len // SUBLANE
    lb = SCAN_LANES

    def body(br_ref, bi_ref, tab_ref, sr_ref, si_ref):
        tab_v = [tab_ref[q] for q in range(8)]

        def step(k, carry):
            cr, ci = carry
            base = pl.multiple_of(k * SCAN_UNROLL * SUBLANE, SCAN_UNROLL * SUBLANE)
            local = [_scan_local(xr, xi, tab_v, (1, 2, 4))
                     for xr, xi in _load_blocks(br_ref, bi_ref, base)]
            done = []
            for xr, xi in local:
                xr, xi = _scan_carry(xr, xi, tab_v, cr, ci)
                done.append((xr, xi))
                cr, ci = xr[SUBLANE - 1:SUBLANE, :], xi[SUBLANE - 1:SUBLANE, :]
            _store_blocks(sr_ref, si_ref, base, done)
            return cr, ci

        zero = jnp.zeros((1, lb), F32)
        lax.fori_loop(0, nblk // SCAN_UNROLL, step, (zero, zero))

    col = pl.BlockSpec((t_len, lb), lambda j: (0, j))
    return _call(
        body, "scan_fwd", (N_STATE // lb,),
        [col, col, pl.BlockSpec((8, SUBLANE, lb), lambda j: (0, 0, j))], [col, col],
        [jax.ShapeDtypeStruct((t_len, N_STATE), BF16)] * 2, [bur, bui, tab], jobs)


def _sgu_mix(v, ws_ref, lane_lo):
    rows = []
    for c0 in range(0, v.shape[0], CHUNK):
        slabs = []
        for j in range(SGU_W // LANE):
            prod = jnp.dot(ws_ref[j], v[c0:c0 + CHUNK, j * LANE:(j + 1) * LANE].astype(BF16),
                           preferred_element_type=F32)
            slabs.append(jnp.where(lane_lo, prod[:CHUNK], prod[CHUNK:]))
        rows.append(jnp.concatenate(slabs, axis=1))
    return jnp.concatenate(rows, axis=0) if len(rows) > 1 else rows[0]


def _fwd_mix(x, p, str_, sti, cre, cim, d_skip, w_glu, b_glu, w_pa, g_sgu, ws_st, bmat, w_pb, w_out,
             jobs=()):
    t_len = x.shape[0]

    def body(x_ref, p_ref, sr_ref, si_ref, cre_ref, cim_ref, dsk_ref, wg_ref, bg_ref, wpa_ref,
             gs_ref, ws_ref, bm_ref, wpb_ref, wo_ref,
             x2_ref, y0_ref, z_ref, mx_ref, ya_ref, yb_ref):
        u = p_ref[:, 0:SSM_W].astype(F32)
        y0 = jnp.concatenate(
            [_dot(sr_ref[:, i * DIAG_N:(i + 1) * DIAG_N],
                  cre_ref[i * DIAG_N:(i + 1) * DIAG_N, i * LANE:(i + 1) * LANE])
             - _dot(si_ref[:, i * DIAG_N:(i + 1) * DIAG_N],
                    cim_ref[i * DIAG_N:(i + 1) * DIAG_N, i * LANE:(i + 1) * LANE])
             for i in range(SSM_W // LANE)], axis=1) + dsk_ref[...] * u
        y0_ref[...] = y0.astype(BF16)
        y1 = _gelu(y0)
        z = _dot(y1, wg_ref[...]) + bg_ref[...]
        z_ref[...] = z.astype(BF16)
        ya_pre = (y1 * _sigmoid(z)).astype(BF16)
        ya = jnp.concatenate([jnp.dot(ya_pre, wpa_ref[k], preferred_element_type=F32)
                              for k in range(N_CHIP)], axis=1)
        ya_ref[...] = ya.astype(BF16)

        uvg = _gelu(p_ref[:, SSM_W:SSM_W + 2 * SGU_W].astype(F32))
        u2 = uvg[:, :SGU_W]
        _, vh = _rms_stats(uvg[:, SGU_W:])
        v3 = vh * gs_ref[...]
        lane_lo = lax.broadcasted_iota(jnp.int32, (CHUNK, LANE), 1) < SGU_D
        bias = jnp.concatenate([bm_ref[...]] * (TB // CHUNK), axis=0)
        mixed = _sgu_mix(v3, ws_ref, lane_lo) + bias
        mx_ref[...] = mixed.astype(BF16)
        sgu = (u2 * mixed).astype(BF16)
        yb = jnp.concatenate([jnp.dot(sgu, wpb_ref[k], preferred_element_type=F32)
                              for k in range(N_CHIP)], axis=1)
        yb_ref[...] = yb.astype(BF16)

        lg0 = SSM_W + 2 * SGU_W
        ga = _sigmoid(p_ref[:, lg0:lg0 + D_MODEL].astype(F32))
        gb = _sigmoid(p_ref[:, lg0 + D_MODEL:lg0 + 2 * D_MODEL].astype(F32))
        mrg = ga * ya + gb * yb
        x2_ref[...] = x_ref[...] + _dot(mrg, wo_ref[...])

    return _call(
        body, "fwd_mix", (t_len // TB,),
        [_rows(TB, D_MODEL), _rows(TB, IN_COLS), _rows(TB, N_STATE), _rows(TB, N_STATE)]
        + [_whole()] * 11,
        [_rows(TB, D_MODEL), _rows(TB, SSM_W), _rows(TB, SSM_W), _rows(TB, SGU_W),
         _rows(TB, D_MODEL), _rows(TB, D_MODEL)],
        [jax.ShapeDtypeStruct((t_len, D_MODEL), F32), jax.ShapeDtypeStruct((t_len, SSM_W), BF16),
         jax.ShapeDtypeStruct((t_len, SSM_W), BF16), jax.ShapeDtypeStruct((t_len, SGU_W), BF16),
         jax.ShapeDtypeStruct((t_len, D_MODEL), BF16), jax.ShapeDtypeStruct((t_len, D_MODEL), BF16)],
        [x, p, str_, sti, cre, cim, d_skip, w_glu, b_glu, w_pa, g_sgu, ws_st, bmat, w_pb, w_out], jobs)


def _conv_taps(v, cw_ref, c0, width):
    w0 = cw_ref[0:1, c0:c0 + width]
    w1 = cw_ref[1:2, c0:c0 + width]
    w2 = cw_ref[2:3, c0:c0 + width]
    return w0 * pltpu.roll(v, 2, 0) + w1 * pltpu.roll(v, 1, 0) + w2 * v


def _fwd_ffn(x2, target, g_ffn, w_up, conv_w, conv_b, w_down, g_final):
    t_len = x2.shape[0]
    half = D_FF // 2
    blocks_per_halo = TB // HALO

    def body(x2_ref, xp_ref, tg_ref, gf_ref, wu_ref, cw_ref, cb_ref, wd_ref, gl_ref,
             up_ref, act_ref, f_ref, h2_ref, dx3_ref, sm_ref):
        i = pl.program_id(0)
        xe = jnp.concatenate([xp_ref[...] * jnp.where(i == 0, 0.0, 1.0), x2_ref[...]], axis=0)
        _, xh = _rms_stats(xe)
        h2 = (xh * gf_ref[...]).astype(BF16)
        h2_ref[...] = h2[HALO:]
        acc = jnp.zeros((TB, D_MODEL), F32)
        ups = [jnp.dot(h2, wu_ref[k], preferred_element_type=F32) for k in range(N_CHIP)]
        for hc in range(2):
            ca = hc * half
            cb = D_FF + hc * half
            ua, ub = ups[hc], ups[2 + hc]
            up_ref[:, ca:ca + half] = ua[HALO:].astype(BF16)
            up_ref[:, cb:cb + half] = ub[HALO:].astype(BF16)
            ac = _conv_taps(ua, cw_ref, ca, half)[HALO:] + cb_ref[:, ca:ca + half]
            bc = _conv_taps(ub, cw_ref, cb, half)[HALO:] + cb_ref[:, cb:cb + half]
            act_ref[:, ca:ca + half] = ac.astype(BF16)
            act_ref[:, cb:cb + half] = bc.astype(BF16)
            f = (ac * _sigmoid(ac) * bc).astype(BF16)
            f_ref[:, ca:ca + half] = f
            acc = acc + jnp.dot(f, wd_ref[ca:ca + half, :], preferred_element_type=F32)
        x3 = x2_ref[...] + acc
        r3, xh3 = _rms_stats(x3)
        err = xh3 * gl_ref[...] - tg_ref[...]
        dout = err * (1.0 / D_MODEL)
        dx3_ref[...] = _rms_bwd(dout * gl_ref[...], xh3, r3)
        dgl = jnp.sum(dout * xh3, axis=0, keepdims=True)
        loss = 0.5 * jnp.sum(jnp.mean(err * err, axis=-1, keepdims=True), axis=0, keepdims=True)
        upd = jnp.concatenate([dgl, jnp.broadcast_to(loss, (1, D_MODEL)),
                               jnp.zeros((SUBLANE - 2, D_MODEL), F32)], axis=0)

        @pl.when(i == 0)
        def _():
            sm_ref[...] = upd

        @pl.when(i > 0)
        def _():
            sm_ref[...] += upd

    prev = pl.BlockSpec((HALO, D_MODEL), lambda i: (jnp.maximum(i * blocks_per_halo - 1, 0), 0))
    return _call(
        body, "fwd_ffn", (t_len // TB,),
        [_rows(TB, D_MODEL), prev, _rows(TB, D_MODEL)] + [_whole()] * 6,
        [_rows(TB, 2 * D_FF), _rows(TB, 2 * D_FF), _rows(TB, D_FF), _rows(TB, D_MODEL),
         _rows(TB, D_MODEL), _acc(SUBLANE, D_MODEL)],
        [jax.ShapeDtypeStruct((t_len, 2 * D_FF), BF16), jax.ShapeDtypeStruct((t_len, 2 * D_FF), BF16),
         jax.ShapeDtypeStruct((t_len, D_FF), BF16), jax.ShapeDtypeStruct((t_len, D_MODEL), BF16),
         jax.ShapeDtypeStruct((t_len, D_MODEL), F32), jax.ShapeDtypeStruct((SUBLANE, D_MODEL), F32)],
        [x2, x2, target, g_ffn, w_up, conv_w, conv_b, w_down, g_final])[0]


def _bwd_ffn(dx3, up, act, x2, g_ffn, w_up, conv_w, w_down, jobs=()):
    t_len = x2.shape[0]
    half = D_FF // 2
    nblk = t_len // TB
    halo_b = 2 * HALO
    n_e = TB + HALO

    def body(dx_ref, dxn_ref, up_ref, act_ref, actn_ref, x2_ref, gf_ref, wu_ref, cw_ref,
             wd_ref, dx2_ref, dup_ref, smw_ref, smg_ref):
        i = pl.program_id(0)
        keep_last = jnp.where(i == nblk - 1, 0.0, 1.0)
        dxe = jnp.concatenate([dx_ref[...], dxn_ref[...] * keep_last], axis=0).astype(BF16)
        dh2 = jnp.zeros((TB, D_MODEL), F32)
        zpad = jnp.zeros((1, half), F32)
        dfs = [lax.dot_general(dxe, wd_ref[hc * half:(hc + 1) * half, :], (((1,), (1,)), ((), ())),
                               preferred_element_type=F32) for hc in range(2)]
        for hc in range(2):
            ca = hc * half
            cb = D_FF + hc * half
            ac = jnp.concatenate([act_ref[:, ca:ca + half].astype(F32),
                                  actn_ref[:, ca:ca + half].astype(F32)[:HALO]], axis=0)
            bc = jnp.concatenate([act_ref[:, cb:cb + half].astype(F32),
                                  actn_ref[:, cb:cb + half].astype(F32)[:HALO]], axis=0)
            wa = [cw_ref[k:k + 1, ca:ca + half] for k in range(3)]
            wb = [cw_ref[k:k + 1, cb:cb + half] for k in range(3)]
            df = dfs[hc]
            sg = _sigmoid(ac)
            da = df * bc * sg * (1.0 + ac * (1.0 - sg))
            db = df * ac * sg
            da1, da2 = pltpu.roll(da, n_e - 1, 0), pltpu.roll(da, n_e - 2, 0)
            db1, db2 = pltpu.roll(db, n_e - 1, 0), pltpu.roll(db, n_e - 2, 0)
            dua = (wa[2] * da + wa[1] * da1 + wa[0] * da2)[:TB]
            dub = (wb[2] * db + wb[1] * db1 + wb[0] * db2)[:TB]
            dup_ref[:, ca:ca + half] = dua.astype(BF16)
            dup_ref[:, cb:cb + half] = dub.astype(BF16)
            dh2 = dh2 + _dot_nt(dua, wu_ref[hc]) + _dot_nt(dub, wu_ref[2 + hc])
            rows = []
            for u_, d0, d1, d2 in ((up_ref[:, ca:ca + half].astype(F32), da, da1, da2),
                                   (up_ref[:, cb:cb + half].astype(F32), db, db1, db2)):
                rows.append([jnp.sum(u_ * d2[:TB], axis=0, keepdims=True),
                             jnp.sum(u_ * d1[:TB], axis=0, keepdims=True),
                             jnp.sum(u_ * d0[:TB], axis=0, keepdims=True),
                             jnp.sum(d0[:TB], axis=0, keepdims=True)])
            for c0, rws in ((ca, rows[0]), (cb, rows[1])):
                upd = jnp.concatenate(rws + [zpad] * (SUBLANE - 4), axis=0)

                @pl.when(i == 0)
                def _(upd=upd, c0=c0):
                    smw_ref[:, c0:c0 + half] = upd

                @pl.when(i > 0)
                def _(upd=upd, c0=c0):
                    smw_ref[:, c0:c0 + half] += upd

        r2, xh2 = _rms_stats(x2_ref[...])
        dx2_ref[...] = dx_ref[...] + _rms_bwd(dh2 * gf_ref[...], xh2, r2)
        updg = jnp.concatenate([jnp.sum(dh2 * xh2, axis=0, keepdims=True),
                                jnp.zeros((SUBLANE - 1, D_MODEL), F32)], axis=0)

        @pl.when(i == 0)
        def _():
            smg_ref[...] = updg

        @pl.when(i > 0)
        def _():
            smg_ref[...] += updg

    nxt_d = pl.BlockSpec((HALO, D_MODEL),
                         lambda i: (jnp.minimum((i + 1) * (TB // HALO), t_len // HALO - 1), 0))
    nxt_a = pl.BlockSpec((halo_b, 2 * D_FF),
                         lambda i: (jnp.minimum((i + 1) * (TB // halo_b), t_len // halo_b - 1), 0))
    return _call(
        body, "bwd_ffn", (nblk,),
        [_rows(TB, D_MODEL), nxt_d, _rows(TB, 2 * D_FF), _rows(TB, 2 * D_FF), nxt_a,
         _rows(TB, D_MODEL)] + [_whole()] * 4,
        [_rows(TB, D_MODEL), _rows(TB, 2 * D_FF), _acc(SUBLANE, 2 * D_FF), _acc(SUBLANE, D_MODEL)],
        [jax.ShapeDtypeStruct((t_len, D_MODEL), F32), jax.ShapeDtypeStruct((t_len, 2 * D_FF), BF16),
         jax.ShapeDtypeStruct((SUBLANE, 2 * D_FF), F32), jax.ShapeDtypeStruct((SUBLANE, D_MODEL), F32)],
        [dx3, dx3, up, act, act, x2, g_ffn, w_up, conv_w, w_down], jobs)


def _bwd_mix(dx2, p, y0, z, mixed, ya, yb, w_out, w_pa, w_pb, w_glu, cre, cim, ws_st, wst_st,
             d_skip, g_sgu, jobs=()):
    t_len = dx2.shape[0]
    pc = D_MODEL // N_CHIP
    n_slab = SGU_W // LANE

    def body(dx_ref, p_ref, y0_ref, z_ref, mx_ref, ya_ref, yb_ref, wo_ref, wpa_ref, wpb_ref,
             wg_ref, cre_ref, cim_ref, ws_ref, wst_ref, dsk_ref, gs_ref,
             dsr_ref, dsi_ref, du_ref, drest_ref, mrg_ref, dya_ref, dyb_ref, yap_ref, dz_ref,
             y1_ref, sgu_ref, dy0_ref, sm_ref, dbm_ref, dws_ref):
        i = pl.program_id(0)
        first = i == 0
        lg0 = SSM_W + 2 * SGU_W
        ga = _sigmoid(p_ref[:, lg0:lg0 + D_MODEL].astype(F32))
        gb = _sigmoid(p_ref[:, lg0 + D_MODEL:lg0 + 2 * D_MODEL].astype(F32))
        yav = ya_ref[...].astype(F32)
        ybv = yb_ref[...].astype(F32)
        mrg_ref[...] = (ga * yav + gb * ybv).astype(BF16)
        y0v = y0_ref[...].astype(F32)
        y1, y1_grad = _gelu_and_grad(y0v)
        sz = _sigmoid(z_ref[...].astype(F32))
        y1_ref[...] = y1.astype(BF16)
        yap_ref[...] = (y1 * sz).astype(BF16)

        dmrg = _dot_nt(dx_ref[...], wo_ref[...])
        drest_ref[:, 2 * SGU_W:2 * SGU_W + D_MODEL] = (dmrg * yav * ga * (1.0 - ga)).astype(BF16)
        drest_ref[:, 2 * SGU_W + D_MODEL:] = (dmrg * ybv * gb * (1.0 - gb)).astype(BF16)
        dya = (dmrg * ga).astype(BF16)
        dyb = (dmrg * gb).astype(BF16)
        dya_ref[...] = dya
        dyb_ref[...] = dyb

        dyap = jnp.zeros((TB, SSM_W), F32)
        for k in range(N_CHIP):
            dyap = dyap + _dot_nt(dya[:, k * pc:(k + 1) * pc], wpa_ref[k])
        dz = dyap * y1 * sz * (1.0 - sz)
        dz_ref[...] = dz.astype(BF16)
        dy0 = (dyap * sz + _dot_nt(dz, wg_ref[...])) * y1_grad
        dy0_ref[...] = dy0.astype(BF16)
        u = p_ref[:, 0:SSM_W].astype(F32)
        du_ref[...] = dy0 * dsk_ref[...]
        for q in range(SSM_W // LANE):
            rows, cols = slice(q * DIAG_N, (q + 1) * DIAG_N), slice(q * LANE, (q + 1) * LANE)
            dsr_ref[:, rows] = _dot_nt(dy0[:, cols], cre_ref[rows, cols]).astype(BF16)
            dsi_ref[:, rows] = (-_dot_nt(dy0[:, cols], cim_ref[rows, cols])).astype(BF16)

        uv = p_ref[:, SSM_W:lg0].astype(F32)
        uvg, gg = _gelu_and_grad(uv)
        u2 = uvg[:, :SGU_W]
        rv, vh = _rms_stats(uvg[:, SGU_W:])
        v3 = vh * gs_ref[...]
        mixed = mx_ref[...].astype(F32)
        dsgu = jnp.zeros((TB, SGU_W), F32)
        for k in range(N_CHIP):
            dsgu = dsgu + _dot_nt(dyb[:, k * pc:(k + 1) * pc], wpb_ref[k])
        sgu_ref[...] = (u2 * mixed).astype(BF16)
        drest_ref[:, 0:SGU_W] = (dsgu * mixed * gg[:, :SGU_W]).astype(BF16)
        dmix = dsgu * u2
        lane_lo = lax.broadcasted_iota(jnp.int32, (CHUNK, LANE), 1) < SGU_D
        dv3 = _sgu_mix(dmix, wst_ref, lane_lo)
        dbm = jnp.zeros((CHUNK, SGU_W), F32)
        for c0 in range(0, TB, CHUNK):
            dbm = dbm + dmix[c0:c0 + CHUNK]
        for j in range(n_slab):
            lo = jnp.zeros((CHUNK, CHUNK), F32)
            hi = jnp.zeros((CHUNK, CHUNK), F32)
            for c0 in range(0, TB, CHUNK):
                dsl = dmix[c0:c0 + CHUNK, j * LANE:(j + 1) * LANE]
                vsl = v3[c0:c0 + CHUNK, j * LANE:(j + 1) * LANE]
                lo = lo + _dot_nt(jnp.where(lane_lo, dsl, 0.0), vsl)
                hi = hi + _dot_nt(jnp.where(lane_lo, 0.0, dsl), vsl)

            @pl.when(first)
            def _(lo=lo, hi=hi, j=j):
                dws_ref[2 * j] = lo
                dws_ref[2 * j + 1] = hi

            @pl.when(jnp.logical_not(first))
            def _(lo=lo, hi=hi, j=j):
                dws_ref[2 * j] += lo
                dws_ref[2 * j + 1] += hi

        dv2 = _rms_bwd(dv3 * gs_ref[...], vh, rv)
        drest_ref[:, SGU_W:2 * SGU_W] = (dv2 * gg[:, SGU_W:]).astype(BF16)

        upd = jnp.concatenate([jnp.sum(dy0 * u, axis=0, keepdims=True),
                               jnp.sum(dz, axis=0, keepdims=True),
                               jnp.sum(dv3 * vh, axis=0, keepdims=True),
                               jnp.zeros((SUBLANE - 3, SSM_W), F32)], axis=0)

        @pl.when(first)
        def _():
            sm_ref[...] = upd
            dbm_ref[...] = dbm

        @pl.when(jnp.logical_not(first))
        def _():
            sm_ref[...] += upd
            dbm_ref[...] += dbm

    rest = 2 * SGU_W + 2 * D_MODEL
    bf_d, bf_s = jax.ShapeDtypeStruct((t_len, D_MODEL), BF16), jax.ShapeDtypeStruct((t_len, SSM_W), BF16)
    return _call(
        body, "bwd_mix", (t_len // TB,),
        [_rows(TB, D_MODEL), _rows(TB, IN_COLS), _rows(TB, SSM_W), _rows(TB, SSM_W),
         _rows(TB, SGU_W), _rows(TB, D_MODEL), _rows(TB, D_MODEL)] + [_whole()] * 10,
        [_rows(TB, N_STATE), _rows(TB, N_STATE), _rows(TB, SSM_W), _rows(TB, rest),
         _rows(TB, D_MODEL), _rows(TB, D_MODEL), _rows(TB, D_MODEL), _rows(TB, SSM_W),
         _rows(TB, SSM_W), _rows(TB, SSM_W), _rows(TB, SGU_W), _rows(TB, SSM_W),
         _acc(SUBLANE, SSM_W), _acc(CHUNK, SGU_W),
         pl.BlockSpec((SGU_G, CHUNK, CHUNK), lambda i: (0, 0, 0))],
        [jax.ShapeDtypeStruct((t_len, N_STATE), BF16), jax.ShapeDtypeStruct((t_len, N_STATE), BF16),
         jax.ShapeDtypeStruct((t_len, SSM_W), F32), jax.ShapeDtypeStruct((t_len, rest), BF16),
         bf_d, bf_d, bf_d, bf_s, bf_s, bf_s, bf_s, bf_s,
         jax.ShapeDtypeStruct((SUBLANE, SSM_W), F32), jax.ShapeDtypeStruct((CHUNK, SGU_W), F32),
         jax.ShapeDtypeStruct((SGU_G, CHUNK, CHUNK), F32)],
        [dx2, p, y0, z, mixed, ya, yb, w_out, w_pa, w_pb, w_glu, cre, cim, ws_st, wst_st, d_skip,
         g_sgu], jobs)


def _scan_bwd(dsr, dsi, str_, sti, tab_rev, jobs=()):
    t_len = dsr.shape[0]
    nblk = t_len // SUBLANE
    lb = SCAN_LANES

    def body(dr_ref, di_ref, sr_ref, si_ref, tab_ref, lr_ref, li_ref, dar_ref, dai_ref):
        tab_v = [tab_ref[q] for q in range(8)]
        row0 = lax.broadcasted_iota(jnp.int32, (SUBLANE, lb), 0) == 0
        tile = BF16_TILE

        def step(k, carry):
            cr, ci, acr, aci = carry
            base = pl.multiple_of((nblk - (k + 1) * SCAN_UNROLL) * SUBLANE, SCAN_UNROLL * SUBLANE)
            state = _load_blocks(sr_ref, si_ref, base)
            before = pl.ds(pl.multiple_of(jnp.maximum(base - tile, 0), tile), tile)
            has_before = jnp.where(base > 0, 1.0, 0.0)
            prev = (sr_ref[before, :].astype(F32)[tile - 1:tile] * has_before,
                    si_ref[before, :].astype(F32)[tile - 1:tile] * has_before)
            local = [_scan_local(xr, xi, tab_v, (7, 6, 4))
                     for xr, xi in _load_blocks(dr_ref, di_ref, base)]
            lam = [None] * SCAN_UNROLL
            for b in reversed(range(SCAN_UNROLL)):
                xr, xi = _scan_carry(*local[b], tab_v, cr, ci)
                lam[b] = (xr, xi)
                cr, ci = xr[0:1, :], xi[0:1, :]
                pr, pi = prev if b == 0 else (state[b - 1][0][SUBLANE - 1:], state[b - 1][1][SUBLANE - 1:])
                s_r = jnp.where(row0, pr, pltpu.roll(state[b][0], 1, 0))
                s_i = jnp.where(row0, pi, pltpu.roll(state[b][1], 1, 0))
                acr = acr + xr * s_r + xi * s_i
                aci = aci + xi * s_r - xr * s_i
            _store_blocks(lr_ref, li_ref, base, lam)
            return cr, ci, acr, aci

        zero = jnp.zeros((1, lb), F32)
        zacc = jnp.zeros((SUBLANE, lb), F32)
        _, _, acr, aci = lax.fori_loop(0, nblk // SCAN_UNROLL, step, (zero, zero, zacc, zacc))
        dar_ref[...] = acr
        dai_ref[...] = aci

    col = pl.BlockSpec((t_len, lb), lambda j: (0, j))
    small = pl.BlockSpec((SUBLANE, lb), lambda j: (0, j))
    return _call(
        body, "scan_bwd", (N_STATE // lb,),
        [col, col, col, col, pl.BlockSpec((8, SUBLANE, lb), lambda j: (0, 0, j))],
        [col, col, small, small],
        [jax.ShapeDtypeStruct((t_len, N_STATE), BF16)] * 2
        + [jax.ShapeDtypeStruct((SUBLANE, N_STATE), F32)] * 2,
        [dsr, dsi, str_, sti, tab_rev], jobs)


def _bwd_in(lam_r, lam_i, du_part, drest, x, dx2, g_mix, w_in, bre, bim, jobs=()):
    t_len = x.shape[0]
    cs = IN_COLS // N_CHIP

    def body(lr_ref, li_ref, du_ref, dr_ref, x_ref, dx2_ref, g_ref, w_ref, bre_ref, bim_ref,
             gx_ref, dp_ref, sm_ref):
        i = pl.program_id(0)
        du = du_ref[...] + jnp.concatenate(
            [_dot_nt(lr_ref[:, i * DIAG_N:(i + 1) * DIAG_N],
                     bre_ref[i * LANE:(i + 1) * LANE, i * DIAG_N:(i + 1) * DIAG_N])
             + _dot_nt(li_ref[:, i * DIAG_N:(i + 1) * DIAG_N],
                       bim_ref[i * LANE:(i + 1) * LANE, i * DIAG_N:(i + 1) * DIAG_N])
             for i in range(SSM_W // LANE)], axis=1)
        dp_ref[:, 0:SSM_W] = du.astype(BF16)
        dp_ref[:, SSM_W:] = dr_ref[...]
        dh = jnp.zeros(x_ref.shape, F32)
        for k in range(N_CHIP):
            dh = dh + _dot_nt(dp_ref[:, k * cs:(k + 1) * cs], w_ref[k])
        r, xh = _rms_stats(x_ref[...])
        gx_ref[...] = dx2_ref[...] + _rms_bwd(dh * g_ref[...], xh, r)
        upd = jnp.concatenate([jnp.sum(dh * xh, axis=0, keepdims=True),
                               jnp.zeros((SUBLANE - 1, D_MODEL), F32)], axis=0)

        @pl.when(i == 0)
        def _():
            sm_ref[...] = upd

        @pl.when(i > 0)
        def _():
            sm_ref[...] += upd

    tb = min(TB_WIDE, t_len)
    return _call(
        body, "bwd_in", (t_len // tb,),
        [_rows(tb, N_STATE), _rows(tb, N_STATE), _rows(tb, SSM_W), _rows(tb, IN_COLS - SSM_W),
         _rows(tb, D_MODEL), _rows(tb, D_MODEL)] + [_whole()] * 4,
        [_rows(tb, D_MODEL), _rows(tb, IN_COLS), _acc(SUBLANE, D_MODEL)],
        [jax.ShapeDtypeStruct((t_len, D_MODEL), F32), jax.ShapeDtypeStruct((t_len, IN_COLS), BF16),
         jax.ShapeDtypeStruct((SUBLANE, D_MODEL), F32)],
        [lam_r, lam_i, du_part, drest, x, dx2, g_mix, w_in, bre, bim], jobs)


def _matmul_tn(a, b, name, out_shape, grid_ij, a_blk, a_map, b_blk, b_map, o_blk, o_map, jobs=()):
    tk = a_blk[0]
    nk = a.shape[0] // tk
    assert nk * tk == a.shape[0] and nk > 0

    def body(a_ref, b_ref, o_ref, acc_ref):
        k = pl.program_id(2)

        @pl.when(k == 0)
        def _():
            acc_ref[...] = jnp.zeros_like(acc_ref)

        acc_ref[...] += lax.dot_general(a_ref[...].astype(BF16), b_ref[...].astype(BF16),
                                        (((0,), (0,)), ((), ())), preferred_element_type=F32)

        @pl.when(k == nk - 1)
        def _():
            o_ref[...] = acc_ref[...]

    outs, per_job = _call(
        body, name, (grid_ij[0], grid_ij[1], nk),
        [pl.BlockSpec(a_blk, a_map), pl.BlockSpec(b_blk, b_map)], [pl.BlockSpec(o_blk, o_map)],
        [jax.ShapeDtypeStruct(out_shape, F32)], [a, b], jobs,
        scratch=[pltpu.VMEM((a_blk[1], b_blk[1]), F32)])
    return outs[0], per_job


def _dw_rows(a, b, name, tm, tk):
    m, n = a.shape[1], b.shape[1]
    tk = min(tk, a.shape[0])
    return _matmul_tn(a, b, name, (m, n), (m // tm, 1),
                      (tk, tm), lambda i, j, k: (k, i), (tk, n), lambda i, j, k: (k, 0),
                      (tm, n), lambda i, j, k: (i, 0))[0]


def _dw_cols(a, b, name, tn, sharded, jobs=()):
    t_len, m = a.shape
    n = b.shape[1]

    def body(a_ref, b_ref, o_ref):
        o_ref[...] = lax.dot_general(a_ref[...].astype(BF16), b_ref[...].astype(BF16),
                                     (((0,), (0,)), ((), ())), preferred_element_type=F32)

    if sharded:
        o_spec, o_shape = pl.BlockSpec((None, m, tn), lambda j: (j, 0, 0)), (n // tn, m, tn)
    else:
        o_spec, o_shape = pl.BlockSpec((m, tn), lambda j: (0, j)), (m, n)
    outs, per_job = _call(body, name, (n // tn,),
                          [_whole(), pl.BlockSpec((t_len, tn), lambda j: (0, j))], [o_spec],
                          [jax.ShapeDtypeStruct(o_shape, F32)], [a, b], jobs)
    return outs[0], per_job


def _dw_tiles(a, b, name, tm, tn, jobs=()):
    t_len, m = a.shape
    n = b.shape[1]

    def body(a_ref, b_ref, o_ref):
        o_ref[...] = lax.dot_general(a_ref[...].astype(BF16), b_ref[...].astype(BF16),
                                     (((0,), (0,)), ((), ())), preferred_element_type=F32)

    outs, per_job = _call(body, name, (n // tn, m // tm),
                          [pl.BlockSpec((t_len, tm), lambda j, i: (0, i)),
                           pl.BlockSpec((t_len, tn), lambda j, i: (0, j))],
                          [pl.BlockSpec((None, tm, tn), lambda j, i: (j, i, 0))],
                          [jax.ShapeDtypeStruct((n // tn, m, tn), F32)], [a, b], jobs)
    return outs[0], per_job


def _dw_pair(a, m, b1, b2, name, jobs=()):
    t_len = a.shape[0]
    n_slab = DIAG_N // LANE
    rows_per_slab = LANE // n_slab

    def body(a_ref, b1_ref, b2_ref, o1_ref, o2_ref):
        for b_ref, o_ref in ((b1_ref, o1_ref), (b2_ref, o2_ref)):
            prod = lax.dot_general(a_ref[...].astype(BF16), b_ref[...].astype(BF16),
                                   (((0,), (0,)), ((), ())), preferred_element_type=F32)
            for j in range(n_slab):
                rows = slice(j * rows_per_slab, (j + 1) * rows_per_slab)
                o_ref[rows, :] = prod[rows, j * LANE:(j + 1) * LANE]

    tok = pl.BlockSpec((t_len, DIAG_N), lambda i: (0, i))
    out = pl.BlockSpec((LANE, LANE), lambda i: (i, 0))
    return _call(body, name, (m // LANE,),
                 [pl.BlockSpec((t_len, LANE), lambda i: (0, i)), tok, tok], [out, out],
                 [jax.ShapeDtypeStruct((m, LANE), F32)] * 2, [a, b1, b2], jobs)


def _prefetch_call(body, name, grid, scalars, in_specs, out_specs, out_shape, args):
    return pl.pallas_call(
        body, name=name,
        grid_spec=pltpu.PrefetchScalarGridSpec(num_scalar_prefetch=1, grid=grid, in_specs=in_specs,
                                               out_specs=out_specs),
        out_shape=out_shape, compiler_params=_params(len(grid)),
    )(scalars, *args)


def _place_shard(w, where, name, dtype, tr):
    rows, cols = w.shape

    def body(s_ref, w_ref, o_ref):
        o_ref[...] = w_ref[...].astype(dtype)

    return _prefetch_call(
        body, name, (rows // tr,), where,
        [pl.BlockSpec((tr, cols), lambda i, s: (i, 0))],
        pl.BlockSpec((None, tr, cols), lambda i, s: (s[0], i, 0)),
        jax.ShapeDtypeStruct((N_CHIP, rows, cols), dtype), [w])


def _place_shards(ws, where, name, dtype):
    n = len(ws)

    def body(s_ref, *refs):
        for t in range(n):
            refs[n + t][...] = refs[t][...].astype(dtype)

    return _prefetch_call(
        body, name, (1,), where,
        [pl.BlockSpec(w.shape, lambda i, s: (0, 0)) for w in ws],
        [pl.BlockSpec((None,) + w.shape, lambda i, s: (s[0], 0, 0)) for w in ws],
        [jax.ShapeDtypeStruct((N_CHIP,) + w.shape, dtype) for w in ws], ws)


def _add_sibling(gs, gots, where, name):
    n = len(gs)
    halves = [(g.shape[1] // 2, g.shape[2]) for g in gs]

    def body(s_ref, *refs):
        for t in range(n):
            refs[2 * n + t][...] = (refs[t][...] + refs[n + t][...]).astype(BF16)

    return _prefetch_call(
        body, name, (N_CHIP,), where,
        [pl.BlockSpec((None, hr, cs), lambda k, s: (k, s[1], 0)) for hr, cs in halves]
        + [pl.BlockSpec((None, hr, cs), lambda k, s: (k, 0, 0)) for hr, cs in halves],
        [pl.BlockSpec((None, hr, cs), lambda k, s: (k, 0, 0)) for hr, cs in halves],
        [jax.ShapeDtypeStruct((N_CHIP, hr, cs), BF16) for hr, cs in halves], list(gs) + list(gots))


def _add_chips(sums, gots, where, name):
    n = len(sums)
    halves = [s.shape[1:] for s in sums]

    def body(s_ref, *refs):
        for t in range(n):
            own_ref, got_ref = refs[t], refs[n + t]
            refs[2 * n + t][...] = ((own_ref[...].astype(F32) + got_ref[0].astype(F32))
                                    + got_ref[1].astype(F32)) + got_ref[2].astype(F32)

    return _prefetch_call(
        body, name, (1,), where,
        [pl.BlockSpec((None, hr, cs), lambda i, s: (s[0], 0, 0)) for hr, cs in halves]
        + [pl.BlockSpec((3, hr, cs), lambda i, s: (0, 0, 0)) for hr, cs in halves],
        [pl.BlockSpec((hr, cs), lambda i, s: (s[1], 0)) for hr, cs in halves],
        [jax.ShapeDtypeStruct((2 * hr, cs), F32) for hr, cs in halves], list(sums) + list(gots))


def _small_allreduce(pack):
    rows = pack.shape[0]
    half = rows // 2

    def body(in_ref, out_ref, sib_ref, slots_ref, s_a, r_a, s_b, r_b, s_c, r_c):
        x, y, c, chips = _place()
        k_me = 2 * x + y
        sib = (x, y, 1 - c)
        first = _remote(in_ref, sib_ref, s_a, r_a, sib)
        first.start()
        first.wait()
        mine = _half(rows, c)
        slots_ref[k_me] = in_ref[mine, :] + sib_ref[mine, :]
        cps = [_remote(slots_ref.at[k_me], slots_ref.at[k_me], s_b.at[j], r_b.at[j], (*ch, c))
               for j, ch in enumerate(chips)]
        for cp in cps:
            cp.start()
        for j, ch in enumerate(chips):
            slot = slots_ref.at[_chip_index(ch)]
            _remote(slot, slot, s_b.at[j], r_b.at[j], (*ch, c)).wait_recv()
        for cp in cps:
            cp.wait_send()
        out_ref[mine, :] = ((slots_ref[0] + slots_ref[1]) + slots_ref[2]) + slots_ref[3]
        last = _remote(out_ref.at[mine, :], out_ref.at[mine, :], s_c, r_c, sib)
        last.start()
        theirs = out_ref.at[_half(rows, 1 - c), :]
        _remote(theirs, theirs, s_c, r_c, sib).wait_recv()
        last.wait_send()

    return pl.pallas_call(
        body, name="small_allreduce", in_specs=[_whole()], out_specs=_whole(),
        out_shape=jax.ShapeDtypeStruct(pack.shape, F32),
        scratch_shapes=[pltpu.VMEM(pack.shape, F32), pltpu.VMEM((N_CHIP, half, LANE), F32),
                        pltpu.SemaphoreType.DMA, pltpu.SemaphoreType.DMA,
                        pltpu.SemaphoreType.DMA((3,)), pltpu.SemaphoreType.DMA((3,)),
                        pltpu.SemaphoreType.DMA, pltpu.SemaphoreType.DMA],
        compiler_params=_params(0),
    )(pack)


def _adamw_update(w_ref, g_ref, m_ref, v_ref, d_ref, mo_ref, vo_ref):
    gv = g_ref[...]
    mn = ADAM_B1 * m_ref[...] + (1.0 - ADAM_B1) * gv
    vn = ADAM_B2 * v_ref[...] + (1.0 - ADAM_B2) * (gv * gv)
    mo_ref[...] = mn
    vo_ref[...] = vn
    m_hat = mn / (1.0 - ADAM_B1 ** ADAM_STEP)
    v_hat = vn / (1.0 - ADAM_B2 ** ADAM_STEP)
    d_ref[...] = -ADAM_LR * (m_hat / (jnp.sqrt(v_hat) + ADAM_EPS) + ADAM_WD * w_ref[...])


def _adamw(w, g, m, v, name, tr):
    rows, cols = w.shape
    blk = _rows(tr, cols)

    def body(w_ref, g_ref, m_ref, v_ref, go_ref, d_ref, mo_ref, vo_ref):
        go_ref[...] = g_ref[...]
        _adamw_update(w_ref, g_ref, m_ref, v_ref, d_ref, mo_ref, vo_ref)

    return _call(body, name, (rows // tr,), [blk] * 4, [blk] * 4,
                 [jax.ShapeDtypeStruct(w.shape, F32)] * 4, [w, g, m, v])[0]


def _adamw_many(ws, gs, ms, vs, name):
    n = len(ws)

    def body(*refs):
        for t in range(n):
            _adamw_update(*[refs[q * n + t] for q in range(7)])

    specs = [pl.BlockSpec(a.shape, lambda i, nd=a.ndim: (0,) * nd) for a in ws]
    outs = pl.pallas_call(
        body, name=name, grid=(1,), in_specs=specs * 4, out_specs=specs * 3,
        out_shape=[jax.ShapeDtypeStruct(a.shape, F32) for _ in range(3) for a in ws],
        compiler_params=_params(1),
    )(*ws, *gs, *ms, *vs)
    return outs[:n], outs[n:2 * n], outs[2 * n:]


def _ssm_discretize(a_re, a_im, log_dt, b_re, b_im):
    dt = jnp.exp(log_dt)[:, None]
    mag = jnp.exp(dt * a_re)
    abr = mag * jnp.cos(dt * a_im)
    abi = mag * jnp.sin(dt * a_im)
    den = a_re * a_re + a_im * a_im
    nr = abr - 1.0
    ni = abi
    f_re = (nr * a_re + ni * a_im) / den
    f_im = (ni * a_re - nr * a_im) / den
    bbr = f_re[..., None] * b_re - f_im[..., None] * b_im
    bbi = f_re[..., None] * b_im + f_im[..., None] * b_re
    return abr, abi, bbr, bbi


def _scan_tables(abr, abi):
    ar = abr.reshape(1, N_STATE)
    ai = abi.reshape(1, N_STATE)
    pr, pi = [ar], [ai]
    for _ in range(SUBLANE - 1):
        pr, pi = pr + [pr[-1] * ar - pi[-1] * ai], pi + [pr[-1] * ai + pi[-1] * ar]
    row = jnp.arange(SUBLANE)[:, None]
    tabs = []
    for d in (1, 2, 4):
        tabs.append(jnp.where(row >= d, pr[d - 1], 0.0))
        tabs.append(jnp.where(row >= d, pi[d - 1], 0.0))
    tabs.append(jnp.concatenate(pr, axis=0))
    tabs.append(jnp.concatenate(pi, axis=0))
    fwd = jnp.stack(tabs)
    sign = jnp.array([1.0, -1.0] * 4, F32)[:, None, None]
    return fwd, fwd[:, ::-1, :] * sign


def _block_diag_b(bb):
    strip = bb.transpose(2, 0, 1).reshape(SSM_H, N_STATE)
    rows = lax.broadcasted_iota(jnp.int32, (SSM_W, N_STATE), 0) // SSM_H
    cols = lax.broadcasted_iota(jnp.int32, (SSM_W, N_STATE), 1) // SSM_P
    return jnp.where(rows == cols, jnp.tile(strip, (SSM_G, 1)), 0.0).astype(BF16)


def _block_diag_c(cc):
    strip = cc.transpose(0, 2, 1).reshape(N_STATE, SSM_H)
    rows = lax.broadcasted_iota(jnp.int32, (N_STATE, SSM_W), 0) // SSM_P
    cols = lax.broadcasted_iota(jnp.int32, (N_STATE, SSM_W), 1) // SSM_H
    return jnp.where(rows == cols, jnp.tile(strip, (1, SSM_G)), 0.0).astype(BF16)


SMALL_SHAPES = {
    "g_mix": (D_MODEL,), "a_re": (SSM_G, SSM_P), "a_im": (SSM_G, SSM_P), "log_dt": (SSM_G,),
    "b_re": (SSM_G, SSM_P, SSM_H), "b_im": (SSM_G, SSM_P, SSM_H),
    "c_re": (SSM_G, SSM_H, SSM_P), "c_im": (SSM_G, SSM_H, SSM_P),
    "d_skip": (SSM_W,), "b_glu": (SSM_W,), "g_sgu": (SGU_W,), "w_s": (SGU_G, CHUNK, CHUNK),
    "b_s": (SGU_G, CHUNK), "g_ffn": (D_MODEL,), "conv_b": (2 * D_FF,), "g_final": (D_MODEL,),
}
PACK_ITEMS = [("loss", (1,))] + [(n, SMALL_SHAPES[n]) for n in SMALL] + [("conv_w", (3, 2 * D_FF))]
TILE = SUBLANE * LANE


def _item_rows(shape):
    return -(-math.prod(shape) // TILE) * SUBLANE


PACK_ROWS = -(-sum(_item_rows(s) for _, s in PACK_ITEMS) // (2 * SUBLANE)) * (2 * SUBLANE)


def _pack(values):
    parts, used = [], 0
    for name, shape in PACK_ITEMS:
        size, rows = math.prod(shape), _item_rows(shape)
        if name in values:
            flat = values[name].astype(F32).reshape(size)
            if rows * LANE > size:
                flat = jnp.pad(flat, (0, rows * LANE - size))
            parts.append(flat.reshape(rows, LANE))
        else:
            parts.append(jnp.zeros((rows, LANE), F32))
        used += rows
    if PACK_ROWS > used:
        parts.append(jnp.zeros((PACK_ROWS - used, LANE), F32))
    return jnp.concatenate(parts, axis=0)


def _unpack(pack):
    out, off = {}, 0
    for name, shape in PACK_ITEMS:
        rows = _item_rows(shape)
        out[name] = pack[off:off + rows].reshape(rows * LANE)[:math.prod(shape)].reshape(shape)
        off += rows
    return out


PLACE_ROWS = {"w_in": 256, "w_up": 256, "w_down": 352, "w_out": 256, "w_proj_a": 256,
              "w_proj_b": 256, "w_glu": 128}


def kernel(x, g_mix, w_in, a_re, a_im, log_dt, b_re, b_im, c_re, c_im, d_skip, w_glu, b_glu, w_proj_a, g_sgu, w_s, b_s, w_proj_b, w_out, g_ffn, w_up, conv_w, conv_b, w_down, g_final, loss_target, m_g_mix, m_w_in, m_a_re, m_a_im, m_log_dt, m_b_re, m_b_im, m_c_re, m_c_im, m_d_skip, m_w_glu, m_b_glu, m_w_proj_a, m_g_sgu, m_w_s, m_b_s, m_w_proj_b, m_w_out, m_g_ffn, m_w_up, m_conv_w, m_conv_b, m_w_down, m_g_final, v_g_mix, v_w_in, v_a_re, v_a_im, v_log_dt, v_b_re, v_b_im, v_c_re, v_c_im, v_d_skip, v_w_glu, v_b_glu, v_w_proj_a, v_g_sgu, v_w_s, v_b_s, v_w_proj_b, v_w_out, v_g_ffn, v_w_up, v_conv_w, v_conv_b, v_w_down, v_g_final):
    given = dict(locals())
    w = {n: given[n] for n in WEIGHTS}
    m = {n: given["m_" + n] for n in WEIGHTS}
    v = {n: given["v_" + n] for n in WEIGHTS}

    def shard2d(a):
        return a.reshape(a.shape[-2], a.shape[-1])

    chip = 2 * lax.axis_index("x") + lax.axis_index("y")
    where = jnp.stack([chip, lax.axis_index("c")]).astype(jnp.int32)
    xs, target = x[0], loss_target[0]
    small = {n: w[n].reshape(SMALL_SHAPES[n]) for n in SMALL}

    (abr, abi, bbr, bbi), disc_vjp = jax.vjp(_ssm_discretize, small["a_re"], small["a_im"],
                                             small["log_dt"], small["b_re"], small["b_im"])
    tab_f, tab_r = _scan_tables(abr, abi)
    bre = _block_diag_b(bbr)
    bim = _block_diag_b(bbi)
    cre = _block_diag_c(small["c_re"])
    cim = _block_diag_c(small["c_im"])
    tril = jnp.tril(jnp.ones((CHUNK, CHUNK), dtype=bool))
    ws = jnp.where(tril[None], small["w_s"], 0.0)
    ws_st = ws.reshape(SGU_G // 2, 2 * CHUNK, CHUNK).astype(BF16)
    wst_st = ws.transpose(0, 2, 1).reshape(SGU_G // 2, 2 * CHUNK, CHUNK).astype(BF16)
    bmat = jnp.repeat(small["b_s"].T, SGU_D, axis=1)
    g_mix2 = small["g_mix"].reshape(1, D_MODEL)
    g_ffn2 = small["g_ffn"].reshape(1, D_MODEL)
    g_final2 = small["g_final"].reshape(1, D_MODEL)
    g_sgu2 = small["g_sgu"].reshape(1, SGU_W)
    d_skip2 = small["d_skip"].reshape(1, SSM_W)
    b_glu2 = small["b_glu"].reshape(1, SSM_W)
    conv_b2 = small["conv_b"].reshape(1, 2 * D_FF)

    gat = {"w_in": _place_shard(shard2d(w["w_in"]), where, "place_w_in", BF16, PLACE_ROWS["w_in"])}
    gat.update(zip(BIG[1:], _place_shards([shard2d(w[n]) for n in BIG[1:]], where, "place_rest", BF16)))
    gat["conv_w"] = _place_shard(shard2d(w["conv_w"]), where, "place_conv_w", F32, 3)
    (gat["w_in"],), = _comm("gather_in", [_job_gather_now(gat["w_in"])])
    mixers = ["w_glu", "w_proj_a", "w_proj_b", "w_out"]
    rows = {n: (0, gat[n].shape[1]) for n in mixers}
    down_a, down_b = (0, D_FF // 8), (D_FF // 8, D_FF // 8)
    up_a, up_b = (0, 3 * D_MODEL // 8), (3 * D_MODEL // 8, 5 * D_MODEL // 8)
    span = (0.0, 1.0)

    names = mixers + ["conv_w", "w_down"]
    (p, h1, bur, bui), (got,) = _fwd_in(
        xs, g_mix2, gat["w_in"], bre, bim,
        [_job_gather([gat[n] for n in names],
                     [(i, rows[n], ICI, span) for i, n in enumerate(mixers)]
                     + [(4, None, ICI, span), (5, down_a, ICI, span)])])
    gat.update(zip(names, got))
    names = mixers + ["w_down", "w_up"]
    (str_, sti), (got,) = _scan_fwd(
        bur, bui, tab_f,
        [_job_gather([gat[n] for n in names],
                     [(i, rows[n], SIBLING, span) for i, n in enumerate(mixers)]
                     + [(4, down_a, SIBLING, span), (4, down_b, ICI, span), (5, up_a, ICI, span)])])
    gat.update(zip(names, got))
    w_glu_f = gat["w_glu"].reshape(SSM_W, SSM_W)
    w_out_f = gat["w_out"].reshape(D_MODEL, D_MODEL)
    conv_w_f = gat["conv_w"].transpose(1, 0, 2).reshape(3, 2 * D_FF)
    (x2, y0, z, mixed, ya, yb), ((gat["w_down"], gat["w_up"]),) = _fwd_mix(
        xs, p, str_, sti, cre, cim, d_skip2, w_glu_f, b_glu2, gat["w_proj_a"], g_sgu2, ws_st, bmat,
        gat["w_proj_b"], w_out_f,
        [_job_gather([gat["w_down"], gat["w_up"]],
                     [(0, down_b, SIBLING, span), (1, up_a, SIBLING, span),
                      (1, up_b, ICI, (0.0, 0.75)), (1, up_b, SIBLING, (0.75, 1.0))])])
    w_down_f = gat["w_down"].reshape(D_FF, D_MODEL)
    up, act, f, h2, dx3, sm_ffn = _fwd_ffn(x2, target, g_ffn2, gat["w_up"], conv_w_f, conv_b2,
                                           w_down_f, g_final2)

    def leg1_done(names, got):
        return _add_sibling([part[n] for n in names], got, where, "add_sibling_" + names[0])

    def leg2_done(names, sums, got):
        return _add_chips(sums, got, where, "add_chips_" + names[0])

    part, red = {}, {}
    part["w_down"] = _dw_rows(f, dx3, "dw_down", D_FF // 2, 4 * TK).reshape(
        N_CHIP, D_FF // N_CHIP, D_MODEL)
    (dx2, dup, sm_conv, sm_gffn), (got,) = _bwd_ffn(
        dx3, up, act, x2, g_ffn2, gat["w_up"], conv_w_f, w_down_f,
        [_job_sibling_halves([part["w_down"]])])
    sum_down = leg1_done(["w_down"], got)
    part["w_up"], (got,) = _dw_tiles(h2, dup, "dw_up", D_MODEL // 2, 2 * D_FF // N_CHIP,
                                     [_job_to_owner(sum_down)])
    red_down = leg2_done(["w_down"], sum_down, got)
    ((dsr, dsi, du_part, drest, mrg, dya, dyb, yap, dz, y1, sgu, dy0, sm_mix, dbm, dws),
     (got, (red["w_down"],))) = _bwd_mix(
        dx2, p, y0, z, mixed, ya, yb, w_out_f, gat["w_proj_a"], gat["w_proj_b"], w_glu_f, cre, cim,
        ws_st, wst_st, d_skip2, g_sgu2,
        [_job_sibling_halves([part["w_up"]]), _job_swap_halves(red_down)])
    sum_up = leg1_done(["w_up"], got)
    (lam_r, lam_i, dar8, dai8), (got,) = _scan_bwd(dsr, dsi, str_, sti, tab_r, [_job_to_owner(sum_up)])
    red_up = leg2_done(["w_up"], sum_up, got)
    mix4 = ["w_out", "w_proj_a", "w_proj_b", "w_glu"]
    part["w_out"] = _dw_cols(mrg, dx2, "dw_out", D_MODEL // 2, False)[0].reshape(
        N_CHIP, D_MODEL // N_CHIP, D_MODEL)
    part["w_proj_a"] = _dw_cols(yap, dya, "dw_proj_a", D_MODEL // N_CHIP, True)[0]
    part["w_proj_b"] = _dw_cols(sgu, dyb, "dw_proj_b", D_MODEL // N_CHIP, True)[0]
    part["w_glu"] = _dw_cols(y1, dz, "dw_glu", SSM_W, False)[0].reshape(
        N_CHIP, SSM_W // N_CHIP, SSM_W)
    got, (red["w_up"],) = _comm(
        "mixer_sibling_halves", [_job_sibling_halves([part[n] for n in mix4]), _job_swap_halves(red_up)])
    (grad_x, dp, sm_gmix), _ = _bwd_in(
        lam_r, lam_i, du_part, drest, xs, dx2, g_mix2, gat["w_in"], bre, bim)
    sums_m = leg1_done(mix4, got)
    part["w_in"], (got,) = _dw_cols(h1, dp, "dw_in", IN_COLS // N_CHIP, True, [_job_to_owner(sums_m)])
    red_m = leg2_done(mix4, sums_m, got)
    (dbd_r, dbd_i), (got, done_m) = _dw_pair(
        p, SSM_W, lam_r, lam_i, "db_bar",
        [_job_sibling_halves([part["w_in"]]), _job_swap_halves(red_m)])
    red.update(zip(mix4, done_m))
    sum_in = leg1_done(["w_in"], got)
    (dcd_r, dcd_i), (got,) = _dw_pair(dy0, SSM_W, str_, sti, "dc", [_job_to_owner(sum_in)])
    red_in = leg2_done(["w_in"], sum_in, got)
    (red["w_in"],), = _comm("swap_w_in", [_job_swap_halves(red_in)])

    def pick_c(slabs):
        two = LANE // SSM_P
        return jnp.einsum("jshsp->jshp", slabs.reshape(SSM_G // two, two, SSM_H, two, SSM_P)
                          ).reshape(SSM_G, SSM_H, SSM_P)

    def pick_b(slabs):
        return pick_c(slabs).transpose(0, 2, 1)

    dabr = jnp.sum(dar8, axis=0).reshape(SSM_G, SSM_P)
    dabi = jnp.sum(dai8, axis=0).reshape(SSM_G, SSM_P)
    d_a_re, d_a_im, d_log_dt, d_b_re, d_b_im = disc_vjp((dabr, dabi, pick_b(dbd_r), pick_b(dbd_i)))
    gsmall = {
        "g_mix": sm_gmix[0], "a_re": d_a_re, "a_im": d_a_im, "log_dt": d_log_dt,
        "b_re": d_b_re, "b_im": d_b_im, "c_re": pick_c(dcd_r), "c_im": -pick_c(dcd_i),
        "d_skip": sm_mix[0], "b_glu": sm_mix[1], "g_sgu": sm_mix[2],
        "w_s": jnp.where(tril[None], dws, 0.0),
        "b_s": dbm.reshape(CHUNK, SGU_G, SGU_D).sum(-1).T,
        "g_ffn": sm_gffn[0], "conv_b": sm_conv[3], "g_final": sm_ffn[0],
        "conv_w": sm_conv[0:3], "loss": sm_ffn[1, 0:1],
    }

    total_pack = _small_allreduce(_pack(gsmall))
    total = _unpack(total_pack)
    grads = dict(red)
    cs = 2 * D_FF // N_CHIP
    grads["conv_w"] = lax.dynamic_slice(total["conv_w"], (0, chip * cs), (3, cs))
    delta, new_m, new_v = {}, {}, {}
    for n in BIG + ("conv_w",):
        grads[n], delta[n], new_m[n], new_v[n] = _adamw(
            shard2d(w[n]), grads[n], shard2d(m[n]), shard2d(v[n]), "adamw_" + n, PLACE_ROWS.get(n, 3))
    for n in SMALL:
        grads[n] = total[n].reshape(w[n].shape)
    ud, um, uv = _adamw_many(*[[d[n] for n in SMALL] for d in (w, grads, m, v)], "adamw_small")
    for i, n in enumerate(SMALL):
        delta[n], new_m[n], new_v[n] = ud[i], um[i], uv[i]

    def like(d):
        return [d[n].reshape(w[n].shape) for n in WEIGHTS]

    return (total["loss"].reshape(()), grad_x.reshape(x.shape), *like(grads), *like(delta),
            *like(new_m), *like(new_v))
```

```python
import math

import jax
import jax.numpy as jnp
from jax import lax
from jax.experimental import pallas as pl
from jax.experimental.pallas import tpu as pltpu

F32 = jnp.float32
BF16 = jnp.bfloat16
MESH = pl.DeviceIdType.MESH

D_MODEL = 1024
SSM_W = 512
SSM_G = 32
SSM_H = 16
SSM_P = 64
N_STATE = SSM_G * SSM_P
DIAG_N = 128 * SSM_P // SSM_H
SGU_W = 512
SGU_G = 8
SGU_D = 64
CHUNK = 128
D_FF = 2816
IN_COLS = 3584
EPS = 1e-6
N_CHIP = 4

ADAM_LR = 0.001
ADAM_B1 = 0.9
ADAM_B2 = 0.999
ADAM_EPS = 1e-08
ADAM_WD = 0.01
ADAM_STEP = 10

SUBLANE = 8
LANE = 128
VMEM_LIMIT = 56 * 1024 * 1024
TB = 256
TB_WIDE = 512
TK = 512
SCAN_LANES = 256
SCAN_UNROLL = 4
HALO = SUBLANE

BIG = ("w_in", "w_up", "w_down", "w_out", "w_proj_a", "w_proj_b", "w_glu")
SMALL = ("g_mix", "a_re", "a_im", "log_dt", "b_re", "b_im", "c_re", "c_im", "d_skip", "b_glu",
         "g_sgu", "w_s", "b_s", "g_ffn", "conv_b", "g_final")
WEIGHTS = ("g_mix", "w_in", "a_re", "a_im", "log_dt", "b_re", "b_im", "c_re", "c_im", "d_skip",
           "w_glu", "b_glu", "w_proj_a", "g_sgu", "w_s", "b_s", "w_proj_b", "w_out", "g_ffn",
           "w_up", "conv_w", "conv_b", "w_down", "g_final")

ANY = pl.BlockSpec(memory_space=pl.ANY)


def _params(n_grid):
    return pltpu.CompilerParams(dimension_semantics=("arbitrary",) * n_grid if n_grid else None,
                                vmem_limit_bytes=VMEM_LIMIT)


def _whole():
    return pl.BlockSpec(memory_space=pltpu.VMEM)


def _rows(tb, ncol):
    return pl.BlockSpec((tb, ncol), lambda i: (i, 0))


def _acc(nrow, ncol):
    return pl.BlockSpec((nrow, ncol), lambda i: (0, 0))


def _dot(a, b):
    return jnp.dot(a.astype(BF16), b.astype(BF16), preferred_element_type=F32)


def _dot_nt(a, b):
    return lax.dot_general(a.astype(BF16), b.astype(BF16), (((1,), (1,)), ((), ())),
                           preferred_element_type=F32)


def _sigmoid(v):
    return 0.5 * jnp.tanh(0.5 * v) + 0.5


_GELU_C = math.sqrt(2.0 / math.pi)


def _gelu(v):
    return 0.5 * v * (1.0 + jnp.tanh(_GELU_C * (v + 0.044715 * v * v * v)))


def _gelu_and_grad(v):
    v2 = v * v
    t = jnp.tanh(_GELU_C * v * (1.0 + 0.044715 * v2))
    half = 0.5 * (1.0 + t)
    return v * half, half + 0.5 * v * (1.0 - t * t) * _GELU_C * (1.0 + 3.0 * 0.044715 * v2)


def _rms_stats(v):
    r = lax.rsqrt(jnp.mean(v * v, axis=-1, keepdims=True) + EPS)
    return r, v * r


def _rms_bwd(dxh, xh, r):
    return r * (dxh - xh * jnp.mean(dxh * xh, axis=-1, keepdims=True))


def _place():
    x, y, c = lax.axis_index("x"), lax.axis_index("y"), lax.axis_index("c")
    chips = [(1 - x, y), (x, 1 - y), (1 - x, 1 - y)]
    return x, y, c, chips


def _chip_index(chip):
    return 2 * chip[0] + chip[1]


def _remote(src, dst, send_sem, recv_sem, device):
    return pltpu.make_async_remote_copy(src_ref=src, dst_ref=dst, send_sem=send_sem,
                                        recv_sem=recv_sem, device_id=device, device_id_type=MESH)


def _half(ref_rows, c):
    hr = ref_rows // 2
    return pl.ds(pl.multiple_of(c * hr, SUBLANE), hr)


class _Job:
    def __init__(self, hooks, n_sem, ins=(), inouts=(), outs=()):
        self.hooks, self.n_sem = list(hooks), n_sem
        self.ins, self.inouts, self.outs = list(ins), list(inouts), list(outs)


def _whole_span(start, finish):
    return [(0.0, "start", start), (1.0, "finish", finish)]


ICI, SIBLING = "ici", "sibling"


def _job_gather(bufs, legs):
    def copies(io, leg, first):
        b, window, kind, _ = legs[leg]
        x, y, c, chips = _place()
        k_me = 2 * x + y
        out = []
        for j, ch in enumerate(chips):
            k = _chip_index(ch)
            if window is None:
                src, land, dev = io[b].at[k_me], io[b].at[k], (*ch, c)
            else:
                r0, rows = window
                mine = pl.ds(pl.multiple_of(r0 + c * (rows // 2), SUBLANE), rows // 2)
                theirs = pl.ds(pl.multiple_of(r0 + (1 - c) * (rows // 2), SUBLANE), rows // 2)
                if kind == ICI:
                    src, land, dev = io[b].at[k_me, mine, :], io[b].at[k, mine, :], (*ch, c)
                else:
                    src, land, dev = io[b].at[k, mine, :], io[b].at[k, theirs, :], (x, y, 1 - c)
            out.append((src, land, first + j, dev))
        return out

    def starter(leg):
        def start(ins, io, outs, ssem, rsem):
            for src, _, i, dev in copies(io, leg, 3 * leg):
                _remote(src, src, ssem(i), rsem(i), dev).start()
        return start

    def finisher(leg):
        def finish(ins, io, outs, ssem, rsem):
            cps = copies(io, leg, 3 * leg)
            for _, land, i, dev in cps:
                _remote(land, land, ssem(i), rsem(i), dev).wait_recv()
            for src, _, i, dev in cps:
                _remote(src, src, ssem(i), rsem(i), dev).wait_send()
        return finish

    hooks = []
    for leg, (_, _, _, (begin, end)) in enumerate(legs):
        hooks += [(begin, "start", starter(leg)), (end, "finish", finisher(leg))]
    return _Job(hooks, 3 * len(legs), inouts=bufs)


def _job_gather_now(buf):
    rows = buf.shape[1]

    def run(ins, io, outs, ssem, rsem):
        x, y, c, chips = _place()
        k_me = 2 * x + y
        sib = (x, y, 1 - c)
        mine, theirs = _half(rows, c), _half(rows, 1 - c)
        own = io[0].at[k_me, mine, :]
        sends = [_remote(own, own, ssem(j), rsem(j), (*ch, c)) for j, ch in enumerate(chips)]
        for cp in sends:
            cp.start()
        passed = []
        for j, ch in enumerate(chips):
            landed = io[0].at[_chip_index(ch), mine, :]
            _remote(landed, landed, ssem(j), rsem(j), sib).wait_recv()
            cp = _remote(landed, landed, ssem(3 + j), rsem(3 + j), sib)
            cp.start()
            passed.append(cp)
        for j, ch in enumerate(chips):
            landed = io[0].at[_chip_index(ch), theirs, :]
            _remote(landed, landed, ssem(3 + j), rsem(3 + j), sib).wait_recv()
        for cp in sends + passed:
            cp.wait_send()

    return _Job([(0.0, "start", run)], 6, inouts=[buf])


def _job_sibling_halves(grads):
    n = len(grads)

    def build(ins, outs, ssem, rsem):
        x, y, c, _ = _place()
        return [_remote(ins[t].at[:, _half(grads[t].shape[1], 1 - c), :], outs[t], ssem(t), rsem(t),
                        (x, y, 1 - c)) for t in range(n)]

    def start(ins, io, outs, ssem, rsem):
        for cp in build(ins, outs, ssem, rsem):
            cp.start()

    def finish(ins, io, outs, ssem, rsem):
        for cp in build(ins, outs, ssem, rsem):
            cp.wait()

    return _Job(_whole_span(start, finish), n, ins=grads,
                outs=[jax.ShapeDtypeStruct((N_CHIP, g.shape[1] // 2, g.shape[2]), F32) for g in grads])


def _job_to_owner(sums):
    n = len(sums)

    def build(ins, outs, ssem, rsem):
        x, y, c, chips = _place()
        return [_remote(ins[t].at[_chip_index(ch)], outs[t].at[j], ssem(3 * t + j), rsem(3 * t + j),
                        (*ch, c)) for t in range(n) for j, ch in enumerate(chips)]

    def start(ins, io, outs, ssem, rsem):
        for cp in build(ins, outs, ssem, rsem):
            cp.start()

    def finish(ins, io, outs, ssem, rsem):
        for cp in build(ins, outs, ssem, rsem):
            cp.wait()

    return _Job(_whole_span(start, finish), 3 * n, ins=sums,
                outs=[jax.ShapeDtypeStruct((3,) + s.shape[1:], s.dtype) for s in sums])


def _job_swap_halves(bufs):
    n = len(bufs)

    def start(ins, io, outs, ssem, rsem):
        x, y, c, _ = _place()
        for t in range(n):
            mine = io[t].at[_half(bufs[t].shape[0], c), :]
            _remote(mine, mine, ssem(t), rsem(t), (x, y, 1 - c)).start()

    def finish(ins, io, outs, ssem, rsem):
        x, y, c, _ = _place()
        for t in range(n):
            theirs = io[t].at[_half(bufs[t].shape[0], 1 - c), :]
            _remote(theirs, theirs, ssem(t), rsem(t), (x, y, 1 - c)).wait_recv()
        for t in range(n):
            mine = io[t].at[_half(bufs[t].shape[0], c), :]
            _remote(mine, mine, ssem(t), rsem(t), (x, y, 1 - c)).wait_send()

    return _Job(_whole_span(start, finish), n, inouts=bufs)


def _call(body, name, grid, in_specs, out_specs, out_shape, args, jobs=(), scratch=()):
    n_in, n_out, n_scr = len(args), len(out_shape), len(scratch)
    job_in = [a for jb in jobs for a in jb.ins + jb.inouts]
    job_out = [s for jb in jobs
               for s in [jax.ShapeDtypeStruct(a.shape, a.dtype) for a in jb.inouts] + jb.outs]
    aliases, pos_in, pos_out = {}, n_in, n_out
    for jb in jobs:
        pos_in += len(jb.ins)
        for _ in jb.inouts:
            aliases[pos_in] = pos_out
            pos_in += 1
            pos_out += 1
        pos_out += len(jb.outs)
    n_sem = sum(jb.n_sem for jb in jobs)

    def wrapped(*refs):
        c_in = refs[:n_in]
        j_in = refs[n_in:n_in + len(job_in)]
        c_out = refs[n_in + len(job_in):n_in + len(job_in) + n_out]
        j_out = refs[n_in + len(job_in) + n_out:n_in + len(job_in) + n_out + len(job_out)]
        rest = refs[n_in + len(job_in) + n_out + len(job_out):]
        c_scr = rest[:n_scr]
        views, pi, po, ps = [], 0, 0, 0
        for jb in jobs:
            ins = j_in[pi:pi + len(jb.ins)]
            pi += len(jb.ins) + len(jb.inouts)
            io = j_out[po:po + len(jb.inouts)]
            new = j_out[po + len(jb.inouts):po + len(jb.inouts) + len(jb.outs)]
            po += len(jb.inouts) + len(jb.outs)
            send = (lambda i, o=ps: rest[n_scr].at[o + i])
            recv = (lambda i, o=ps: rest[n_scr + 1].at[o + i])
            ps += jb.n_sem
            views.append((ins, io, new, send, recv))

        def run(frac):
            for kind in ("finish", "start"):
                for jb, vw in zip(jobs, views):
                    for at, what, fn in jb.hooks:
                        if at == frac and what == kind:
                            fn(*vw)

        fracs = sorted({at for jb in jobs for at, _, _ in jb.hooks})
        if not grid:
            for frac in fracs:
                run(frac)
            return
        if jobs:
            assert len(grid) == 1 or set(fracs) <= {0.0, 1.0}
            first = pl.program_id(0) == 0
            last = pl.program_id(0) == grid[0] - 1
            for d in range(1, len(grid)):
                first = jnp.logical_and(first, pl.program_id(d) == 0)
                last = jnp.logical_and(last, pl.program_id(d) == grid[d] - 1)
            for frac in fracs:
                if frac < 1.0:
                    at_step = first if frac == 0.0 else pl.program_id(0) == int(frac * grid[0])
                    pl.when(at_step)(lambda frac=frac: run(frac))
        body(*c_in, *c_out, *c_scr)
        if jobs and 1.0 in fracs:
            pl.when(last)(lambda: run(1.0))

    sems = [pltpu.SemaphoreType.DMA((n_sem,)), pltpu.SemaphoreType.DMA((n_sem,))] if jobs else []
    kwargs = dict(grid=grid) if grid else {}
    res = pl.pallas_call(
        wrapped, name=name, in_specs=list(in_specs) + [ANY] * len(job_in),
        out_specs=list(out_specs) + [ANY] * len(job_out),
        out_shape=list(out_shape) + job_out, scratch_shapes=list(scratch) + sems,
        input_output_aliases=aliases, compiler_params=_params(len(grid)), **kwargs,
    )(*args, *job_in)
    outs, pos, per_job = list(res[:n_out]), n_out, []
    for jb in jobs:
        k = len(jb.inouts) + len(jb.outs)
        per_job.append(list(res[pos:pos + k]))
        pos += k
    return outs, per_job


def _comm(name, jobs):
    return _call(None, name, (), [], [], [], [], jobs)[1]


def _fwd_in(x, g_mix, w_in, bre, bim, jobs=()):
    t_len = x.shape[0]
    cs = IN_COLS // N_CHIP

    def body(x_ref, g_ref, w_ref, bre_ref, bim_ref, p_ref, h_ref, bur_ref, bui_ref):
        xv = x_ref[...]
        r, xh = _rms_stats(xv)
        h = (xh * g_ref[...]).astype(BF16)
        h_ref[...] = h
        for k in range(N_CHIP):
            p_ref[:, k * cs:(k + 1) * cs] = jnp.dot(h, w_ref[k],
                                                    preferred_element_type=F32).astype(BF16)
        u = p_ref[:, 0:SSM_W]
        for i in range(SSM_W // LANE):
            rows, cols = slice(i * LANE, (i + 1) * LANE), slice(i * DIAG_N, (i + 1) * DIAG_N)
            bur_ref[:, cols] = jnp.dot(u[:, rows], bre_ref[rows, cols],
                                       preferred_element_type=F32).astype(BF16)
            bui_ref[:, cols] = jnp.dot(u[:, rows], bim_ref[rows, cols],
                                       preferred_element_type=F32).astype(BF16)

    tb = min(TB_WIDE, t_len)
    return _call(
        body, "fwd_in", (t_len // tb,),
        [_rows(tb, D_MODEL), _whole(), _whole(), _whole(), _whole()],
        [_rows(tb, IN_COLS), _rows(tb, D_MODEL), _rows(tb, N_STATE), _rows(tb, N_STATE)],
        [jax.ShapeDtypeStruct((t_len, IN_COLS), BF16), jax.ShapeDtypeStruct((t_len, D_MODEL), BF16),
         jax.ShapeDtypeStruct((t_len, N_STATE), BF16), jax.ShapeDtypeStruct((t_len, N_STATE), BF16)],
        [x, g_mix, w_in, bre, bim], jobs)


def _scan_local(xr, xi, tab, shifts):
    for q, s in enumerate(shifts):
        ar, ai = tab[2 * q], tab[2 * q + 1]
        rr = pltpu.roll(xr, s, 0)
        ri = pltpu.roll(xi, s, 0)
        xr, xi = xr + ar * rr - ai * ri, xi + ar * ri + ai * rr
    return xr, xi


def _scan_carry(xr, xi, tab, cr, ci):
    pr, pi = tab[6], tab[7]
    return xr + pr * cr - pi * ci, xi + pr * ci + pi * cr


BF16_TILE = 2 * SUBLANE


def _load_blocks(r_ref, i_ref, base):
    out = []
    for q in range(SCAN_UNROLL // 2):
        rows = pl.ds(pl.multiple_of(base + q * BF16_TILE, BF16_TILE), BF16_TILE)
        vr, vi = r_ref[rows, :].astype(F32), i_ref[rows, :].astype(F32)
        out += [(vr[:SUBLANE], vi[:SUBLANE]), (vr[SUBLANE:], vi[SUBLANE:])]
    return out


def _store_blocks(r_ref, i_ref, base, blocks):
    for q in range(SCAN_UNROLL // 2):
        rows = pl.ds(pl.multiple_of(base + q * BF16_TILE, BF16_TILE), BF16_TILE)
        r_ref[rows, :] = jnp.concatenate([blocks[2 * q][0], blocks[2 * q + 1][0]], 0).astype(r_ref.dtype)
        i_ref[rows, :] = jnp.concatenate([blocks[2 * q][1], blocks[2 * q + 1][1]], 0).astype(i_ref.dtype)


def _scan_fwd(bur, bui, tab, jobs=()):
    t_len = bur.shape[0]
    nblk = t_len // SUBLANE
    lb = SCAN_LANES

    def body(br_ref, bi_ref, tab_ref, sr_ref, si_ref):
        tab_v = [tab_ref[q] for q in range(8)]

        def step(k, carry):
            cr, ci = carry
            base = pl.multiple_of(k * SCAN_UNROLL * SUBLANE, SCAN_UNROLL * SUBLANE)
            local = [_scan_local(xr, xi, tab_v, (1, 2, 4))
                     for xr, xi in _load_blocks(br_ref, bi_ref, base)]
            done = []
            for xr, xi in local:
                xr, xi = _scan_carry(xr, xi, tab_v, cr, ci)
                done.append((xr, xi))
                cr, ci = xr[SUBLANE - 1:SUBLANE, :], xi[SUBLANE - 1:SUBLANE, :]
            _store_blocks(sr_ref, si_ref, base, done)
            return cr, ci

        zero = jnp.zeros((1, lb), F32)
        lax.fori_loop(0, nblk // SCAN_UNROLL, step, (zero, zero))

    col = pl.BlockSpec((t_len, lb), lambda j: (0, j))
    return _call(
        body, "scan_fwd", (N_STATE // lb,),
        [col, col, pl.BlockSpec((8, SUBLANE, lb), lambda j: (0, 0, j))], [col, col],
        [jax.ShapeDtypeStruct((t_len, N_STATE), BF16)] * 2, [bur, bui, tab], jobs)


def _sgu_mix(v, ws_ref, lane_lo):
    rows = []
    for c0 in range(0, v.shape[0], CHUNK):
        slabs = []
        for j in range(SGU_W // LANE):
            prod = jnp.dot(ws_ref[j], v[c0:c0 + CHUNK, j * LANE:(j + 1) * LANE].astype(BF16),
                           preferred_element_type=F32)
            slabs.append(jnp.where(lane_lo, prod[:CHUNK], prod[CHUNK:]))
        rows.append(jnp.concatenate(slabs, axis=1))
    return jnp.concatenate(rows, axis=0) if len(rows) > 1 else rows[0]


def _fwd_mix(x, p, str_, sti, cre, cim, d_skip, w_glu, b_glu, w_pa, g_sgu, ws_st, bmat, w_pb, w_out,
             jobs=()):
    t_len = x.shape[0]

    def body(x_ref, p_ref, sr_ref, si_ref, cre_ref, cim_ref, dsk_ref, wg_ref, bg_ref, wpa_ref,
             gs_ref, ws_ref, bm_ref, wpb_ref, wo_ref,
             x2_ref, y0_ref, z_ref, mx_ref, ya_ref, yb_ref):
        u = p_ref[:, 0:SSM_W].astype(F32)
        y0 = jnp.concatenate(
            [_dot(sr_ref[:, i * DIAG_N:(i + 1) * DIAG_N],
                  cre_ref[i * DIAG_N:(i + 1) * DIAG_N, i * LANE:(i + 1) * LANE])
             - _dot(si_ref[:, i * DIAG_N:(i + 1) * DIAG_N],
                    cim_ref[i * DIAG_N:(i + 1) * DIAG_N, i * LANE:(i + 1) * LANE])
             for i in range(SSM_W // LANE)], axis=1) + dsk_ref[...] * u
        y0_ref[...] = y0.astype(BF16)
        y1 = _gelu(y0)
        z = _dot(y1, wg_ref[...]) + bg_ref[...]
        z_ref[...] = z.astype(BF16)
        ya_pre = (y1 * _sigmoid(z)).astype(BF16)
        ya = jnp.concatenate([jnp.dot(ya_pre, wpa_ref[k], preferred_element_type=F32)
                              for k in range(N_CHIP)], axis=1)
        ya_ref[...] = ya.astype(BF16)

        uvg = _gelu(p_ref[:, SSM_W:SSM_W + 2 * SGU_W].astype(F32))
        u2 = uvg[:, :SGU_W]
        _, vh = _rms_stats(uvg[:, SGU_W:])
        v3 = vh * gs_ref[...]
        lane_lo = lax.broadcasted_iota(jnp.int32, (CHUNK, LANE), 1) < SGU_D
        bias = jnp.concatenate([bm_ref[...]] * (TB // CHUNK), axis=0)
        mixed = _sgu_mix(v3, ws_ref, lane_lo) + bias
        mx_ref[...] = mixed.astype(BF16)
        sgu = (u2 * mixed).astype(BF16)
        yb = jnp.concatenate([jnp.dot(sgu, wpb_ref[k], preferred_element_type=F32)
                              for k in range(N_CHIP)], axis=1)
        yb_ref[...] = yb.astype(BF16)

        lg0 = SSM_W + 2 * SGU_W
        ga = _sigmoid(p_ref[:, lg0:lg0 + D_MODEL].astype(F32))
        gb = _sigmoid(p_ref[:, lg0 + D_MODEL:lg0 + 2 * D_MODEL].astype(F32))
        mrg = ga * ya + gb * yb
        x2_ref[...] = x_ref[...] + _dot(mrg, wo_ref[...])

    return _call(
        body, "fwd_mix", (t_len // TB,),
        [_rows(TB, D_MODEL), _rows(TB, IN_COLS), _rows(TB, N_STATE), _rows(TB, N_STATE)]
        + [_whole()] * 11,
        [_rows(TB, D_MODEL), _rows(TB, SSM_W), _rows(TB, SSM_W), _rows(TB, SGU_W),
         _rows(TB, D_MODEL), _rows(TB, D_MODEL)],
        [jax.ShapeDtypeStruct((t_len, D_MODEL), F32), jax.ShapeDtypeStruct((t_len, SSM_W), BF16),
         jax.ShapeDtypeStruct((t_len, SSM_W), BF16), jax.ShapeDtypeStruct((t_len, SGU_W), BF16),
         jax.ShapeDtypeStruct((t_len, D_MODEL), BF16), jax.ShapeDtypeStruct((t_len, D_MODEL), BF16)],
        [x, p, str_, sti, cre, cim, d_skip, w_glu, b_glu, w_pa, g_sgu, ws_st, bmat, w_pb, w_out], jobs)


def _conv_taps(v, cw_ref, c0, width):
    w0 = cw_ref[0:1, c0:c0 + width]
    w1 = cw_ref[1:2, c0:c0 + width]
    w2 = cw_ref[2:3, c0:c0 + width]
    return w0 * pltpu.roll(v, 2, 0) + w1 * pltpu.roll(v, 1, 0) + w2 * v


def _fwd_ffn(x2, target, g_ffn, w_up, conv_w, conv_b, w_down, g_final):
    t_len = x2.shape[0]
    half = D_FF // 2
    blocks_per_halo = TB // HALO

    def body(x2_ref, xp_ref, tg_ref, gf_ref, wu_ref, cw_ref, cb_ref, wd_ref, gl_ref,
             up_ref, act_ref, f_ref, h2_ref, dx3_ref, sm_ref):
        i = pl.program_id(0)
        xe = jnp.concatenate([xp_ref[...] * jnp.where(i == 0, 0.0, 1.0), x2_ref[...]], axis=0)
        _, xh = _rms_stats(xe)
        h2 = (xh * gf_ref[...]).astype(BF16)
        h2_ref[...] = h2[HALO:]
        acc = jnp.zeros((TB, D_MODEL), F32)
        ups = [jnp.dot(h2, wu_ref[k], preferred_element_type=F32) for k in range(N_CHIP)]
        for hc in range(2):
            ca = hc * half
            cb = D_FF + hc * half
            ua, ub = ups[hc], ups[2 + hc]
            up_ref[:, ca:ca + half] = ua[HALO:].astype(BF16)
            up_ref[:, cb:cb + half] = ub[HALO:].astype(BF16)
            ac = _conv_taps(ua, cw_ref, ca, half)[HALO:] + cb_ref[:, ca:ca + half]
            bc = _conv_taps(ub, cw_ref, cb, half)[HALO:] + cb_ref[:, cb:cb + half]
            act_ref[:, ca:ca + half] = ac.astype(BF16)
            act_ref[:, cb:cb + half] = bc.astype(BF16)
            f = (ac * _sigmoid(ac) * bc).astype(BF16)
            f_ref[:, ca:ca + half] = f
            acc = acc + jnp.dot(f, wd_ref[ca:ca + half, :], preferred_element_type=F32)
        x3 = x2_ref[...] + acc
        r3, xh3 = _rms_stats(x3)
        err = xh3 * gl_ref[...] - tg_ref[...]
        dout = err * (1.0 / D_MODEL)
        dx3_ref[...] = _rms_bwd(dout * gl_ref[...], xh3, r3)
        dgl = jnp.sum(dout * xh3, axis=0, keepdims=True)
        loss = 0.5 * jnp.sum(jnp.mean(err * err, axis=-1, keepdims=True), axis=0, keepdims=True)
        upd = jnp.concatenate([dgl, jnp.broadcast_to(loss, (1, D_MODEL)),
                               jnp.zeros((SUBLANE - 2, D_MODEL), F32)], axis=0)

        @pl.when(i == 0)
        def _():
            sm_ref[...] = upd

        @pl.when(i > 0)
        def _():
            sm_ref[...] += upd

    prev = pl.BlockSpec((HALO, D_MODEL), lambda i: (jnp.maximum(i * blocks_per_halo - 1, 0), 0))
    return _call(
        body, "fwd_ffn", (t_len // TB,),
        [_rows(TB, D_MODEL), prev, _rows(TB, D_MODEL)] + [_whole()] * 6,
        [_rows(TB, 2 * D_FF), _rows(TB, 2 * D_FF), _rows(TB, D_FF), _rows(TB, D_MODEL),
         _rows(TB, D_MODEL), _acc(SUBLANE, D_MODEL)],
        [jax.ShapeDtypeStruct((t_len, 2 * D_FF), BF16), jax.ShapeDtypeStruct((t_len, 2 * D_FF), BF16),
         jax.ShapeDtypeStruct((t_len, D_FF), BF16), jax.ShapeDtypeStruct((t_len, D_MODEL), BF16),
         jax.ShapeDtypeStruct((t_len, D_MODEL), F32), jax.ShapeDtypeStruct((SUBLANE, D_MODEL), F32)],
        [x2, x2, target, g_ffn, w_up, conv_w, conv_b, w_down, g_final])[0]


def _bwd_ffn(dx3, up, act, x2, g_ffn, w_up, conv_w, w_down, jobs=()):
    t_len = x2.shape[0]
    half = D_FF // 2
    nblk = t_len // TB
    halo_b = 2 * HALO
    n_e = TB + HALO

    def body(dx_ref, dxn_ref, up_ref, act_ref, actn_ref, x2_ref, gf_ref, wu_ref, cw_ref,
             wd_ref, dx2_ref, dup_ref, smw_ref, smg_ref):
        i = pl.program_id(0)
        keep_last = jnp.where(i == nblk - 1, 0.0, 1.0)
        dxe = jnp.concatenate([dx_ref[...], dxn_ref[...] * keep_last], axis=0).astype(BF16)
        dh2 = jnp.zeros((TB, D_MODEL), F32)
        zpad = jnp.zeros((1, half), F32)
        dfs = [lax.dot_general(dxe, wd_ref[hc * half:(hc + 1) * half, :], (((1,), (1,)), ((), ())),
                               preferred_element_type=F32) for hc in range(2)]
        for hc in range(2):
            ca = hc * half
            cb = D_FF + hc * half
            ac = jnp.concatenate([act_ref[:, ca:ca + half].astype(F32),
                                  actn_ref[:, ca:ca + half].astype(F32)[:HALO]], axis=0)
            bc = jnp.concatenate([act_ref[:, cb:cb + half].astype(F32),
                                  actn_ref[:, cb:cb + half].astype(F32)[:HALO]], axis=0)
            wa = [cw_ref[k:k + 1, ca:ca + half] for k in range(3)]
            wb = [cw_ref[k:k + 1, cb:cb + half] for k in range(3)]
            df = dfs[hc]
            sg = _sigmoid(ac)
            da = df * bc * sg * (1.0 + ac * (1.0 - sg))
            db = df * ac * sg
            da1, da2 = pltpu.roll(da, n_e - 1, 0), pltpu.roll(da, n_e - 2, 0)
            db1, db2 = pltpu.roll(db, n_e - 1, 0), pltpu.roll(db, n_e - 2, 0)
            dua = (wa[2] * da + wa[1] * da1 + wa[0] * da2)[:TB]
            dub = (wb[2] * db + wb[1] * db1 + wb[0] * db2)[:TB]
            dup_ref[:, ca:ca + half] = dua.astype(BF16)
            dup_ref[:, cb:cb + half] = dub.astype(BF16)
            dh2 = dh2 + _dot_nt(dua, wu_ref[hc]) + _dot_nt(dub, wu_ref[2 + hc])
            rows = []
            for u_, d0, d1, d2 in ((up_ref[:, ca:ca + half].astype(F32), da, da1, da2),
                                   (up_ref[:, cb:cb + half].astype(F32), db, db1, db2)):
                rows.append([jnp.sum(u_ * d2[:TB], axis=0, keepdims=True),
                             jnp.sum(u_ * d1[:TB], axis=0, keepdims=True),
                             jnp.sum(u_ * d0[:TB], axis=0, keepdims=True),
                             jnp.sum(d0[:TB], axis=0, keepdims=True)])
            for c0, rws in ((ca, rows[0]), (cb, rows[1])):
                upd = jnp.concatenate(rws + [zpad] * (SUBLANE - 4), axis=0)

                @pl.when(i == 0)
                def _(upd=upd, c0=c0):
                    smw_ref[:, c0:c0 + half] = upd

                @pl.when(i > 0)
                def _(upd=upd, c0=c0):
                    smw_ref[:, c0:c0 + half] += upd

        r2, xh2 = _rms_stats(x2_ref[...])
        dx2_ref[...] = dx_ref[...] + _rms_bwd(dh2 * gf_ref[...], xh2, r2)
        updg = jnp.concatenate([jnp.sum(dh2 * xh2, axis=0, keepdims=True),
                                jnp.zeros((SUBLANE - 1, D_MODEL), F32)], axis=0)

        @pl.when(i == 0)
        def _():
            smg_ref[...] = updg

        @pl.when(i > 0)
        def _():
            smg_ref[...] += updg

    nxt_d = pl.BlockSpec((HALO, D_MODEL),
                         lambda i: (jnp.minimum((i + 1) * (TB // HALO), t_len // HALO - 1), 0))
    nxt_a = pl.BlockSpec((halo_b, 2 * D_FF),
                         lambda i: (jnp.minimum((i + 1) * (TB // halo_b), t_len // halo_b - 1), 0))
    return _call(
        body, "bwd_ffn", (nblk,),
        [_rows(TB, D_MODEL), nxt_d, _rows(TB, 2 * D_FF), _rows(TB, 2 * D_FF), nxt_a,
         _rows(TB, D_MODEL)] + [_whole()] * 4,
        [_rows(TB, D_MODEL), _rows(TB, 2 * D_FF), _acc(SUBLANE, 2 * D_FF), _acc(SUBLANE, D_MODEL)],
        [jax.ShapeDtypeStruct((t_len, D_MODEL), F32), jax.ShapeDtypeStruct((t_len, 2 * D_FF), BF16),
         jax.ShapeDtypeStruct((SUBLANE, 2 * D_FF), F32), jax.ShapeDtypeStruct((SUBLANE, D_MODEL), F32)],
        [dx3, dx3, up, act, act, x2, g_ffn, w_up, conv_w, w_down], jobs)


def _bwd_mix(dx2, p, y0, z, mixed, ya, yb, w_out, w_pa, w_pb, w_glu, cre, cim, ws_st, wst_st,
             d_skip, g_sgu, jobs=()):
    t_len = dx2.shape[0]
    pc = D_MODEL // N_CHIP
    n_slab = SGU_W // LANE

    def body(dx_ref, p_ref, y0_ref, z_ref, mx_ref, ya_ref, yb_ref, wo_ref, wpa_ref, wpb_ref,
             wg_ref, cre_ref, cim_ref, ws_ref, wst_ref, dsk_ref, gs_ref,
             dsr_ref, dsi_ref, du_ref, drest_ref, mrg_ref, dya_ref, dyb_ref, yap_ref, dz_ref,
             y1_ref, sgu_ref, dy0_ref, sm_ref, dbm_ref, dws_ref):
        i = pl.program_id(0)
        first = i == 0
        lg0 = SSM_W + 2 * SGU_W
        ga = _sigmoid(p_ref[:, lg0:lg0 + D_MODEL].astype(F32))
        gb = _sigmoid(p_ref[:, lg0 + D_MODEL:lg0 + 2 * D_MODEL].astype(F32))
        yav = ya_ref[...].astype(F32)
        ybv = yb_ref[...].astype(F32)
        mrg_ref[...] = (ga * yav + gb * ybv).astype(BF16)
        y0v = y0_ref[...].astype(F32)
        y1, y1_grad = _gelu_and_grad(y0v)
        sz = _sigmoid(z_ref[...].astype(F32))
        y1_ref[...] = y1.astype(BF16)
        yap_ref[...] = (y1 * sz).astype(BF16)

        dmrg = _dot_nt(dx_ref[...], wo_ref[...])
        drest_ref[:, 2 * SGU_W:2 * SGU_W + D_MODEL] = (dmrg * yav * ga * (1.0 - ga)).astype(BF16)
        drest_ref[:, 2 * SGU_W + D_MODEL:] = (dmrg * ybv * gb * (1.0 - gb)).astype(BF16)
        dya = (dmrg * ga).astype(BF16)
        dyb = (dmrg * gb).astype(BF16)
        dya_ref[...] = dya
        dyb_ref[...] = dyb

        dyap = jnp.zeros((TB, SSM_W), F32)
        for k in range(N_CHIP):
            dyap = dyap + _dot_nt(dya[:, k * pc:(k + 1) * pc], wpa_ref[k])
        dz = dyap * y1 * sz * (1.0 - sz)
        dz_ref[...] = dz.astype(BF16)
        dy0 = (dyap * sz + _dot_nt(dz, wg_ref[...])) * y1_grad
        dy0_ref[...] = dy0.astype(BF16)
        u = p_ref[:, 0:SSM_W].astype(F32)
        du_ref[...] = dy0 * dsk_ref[...]
        for q in range(SSM_W // LANE):
            rows, cols = slice(q * DIAG_N, (q + 1) * DIAG_N), slice(q * LANE, (q + 1) * LANE)
            dsr_ref[:, rows] = _dot_nt(dy0[:, cols], cre_ref[rows, cols]).astype(BF16)
            dsi_ref[:, rows] = (-_dot_nt(dy0[:, cols], cim_ref[rows, cols])).astype(BF16)

        uv = p_ref[:, SSM_W:lg0].astype(F32)
        uvg, gg = _gelu_and_grad(uv)
        u2 = uvg[:, :SGU_W]
        rv, vh = _rms_stats(uvg[:, SGU_W:])
        v3 = vh * gs_ref[...]
        mixed = mx_ref[...].astype(F32)
        dsgu = jnp.zeros((TB, SGU_W), F32)
        for k in range(N_CHIP):
            dsgu = dsgu + _dot_nt(dyb[:, k * pc:(k + 1) * pc], wpb_ref[k])
        sgu_ref[...] = (u2 * mixed).astype(BF16)
        drest_ref[:, 0:SGU_W] = (dsgu * mixed * gg[:, :SGU_W]).astype(BF16)
        dmix = dsgu * u2
        lane_lo = lax.broadcasted_iota(jnp.int32, (CHUNK, LANE), 1) < SGU_D
        dv3 = _sgu_mix(dmix, wst_ref, lane_lo)
        dbm = jnp.zeros((CHUNK, SGU_W), F32)
        for c0 in range(0, TB, CHUNK):
            dbm = dbm + dmix[c0:c0 + CHUNK]
        for j in range(n_slab):
            lo = jnp.zeros((CHUNK, CHUNK), F32)
            hi = jnp.zeros((CHUNK, CHUNK), F32)
            for c0 in range(0, TB, CHUNK):
                dsl = dmix[c0:c0 + CHUNK, j * LANE:(j + 1) * LANE]
                vsl = v3[c0:c0 + CHUNK, j * LANE:(j + 1) * LANE]
                lo = lo + _dot_nt(jnp.where(lane_lo, dsl, 0.0), vsl)
                hi = hi + _dot_nt(jnp.where(lane_lo, 0.0, dsl), vsl)

            @pl.when(first)
            def _(lo=lo, hi=hi, j=j):
                dws_ref[2 * j] = lo
                dws_ref[2 * j + 1] = hi

            @pl.when(jnp.logical_not(first))
            def _(lo=lo, hi=hi, j=j):
                dws_ref[2 * j] += lo
                dws_ref[2 * j + 1] += hi

        dv2 = _rms_bwd(dv3 * gs_ref[...], vh, rv)
        drest_ref[:, SGU_W:2 * SGU_W] = (dv2 * gg[:, SGU_W:]).astype(BF16)

        upd = jnp.concatenate([jnp.sum(dy0 * u, axis=0, keepdims=True),
                               jnp.sum(dz, axis=0, keepdims=True),
                               jnp.sum(dv3 * vh, axis=0, keepdims=True),
                               jnp.zeros((SUBLANE - 3, SSM_W), F32)], axis=0)

        @pl.when(first)
        def _():
            sm_ref[...] = upd
            dbm_ref[...] = dbm

        @pl.when(jnp.logical_not(first))
        def _():
            sm_ref[...] += upd
            dbm_ref[...] += dbm

    rest = 2 * SGU_W + 2 * D_MODEL
    bf_d, bf_s = jax.ShapeDtypeStruct((t_len, D_MODEL), BF16), jax.ShapeDtypeStruct((t_len, SSM_W), BF16)
    return _call(
        body, "bwd_mix", (t_len // TB,),
        [_rows(TB, D_MODEL), _rows(TB, IN_COLS), _rows(TB, SSM_W), _rows(TB, SSM_W),
         _rows(TB, SGU_W), _rows(TB, D_MODEL), _rows(TB, D_MODEL)] + [_whole()] * 10,
        [_rows(TB, N_STATE), _rows(TB, N_STATE), _rows(TB, SSM_W), _rows(TB, rest),
         _rows(TB, D_MODEL), _rows(TB, D_MODEL), _rows(TB, D_MODEL), _rows(TB, SSM_W),
         _rows(TB, SSM_W), _rows(TB, SSM_W), _rows(TB, SGU_W), _rows(TB, SSM_W),
         _acc(SUBLANE, SSM_W), _acc(CHUNK, SGU_W),
         pl.BlockSpec((SGU_G, CHUNK, CHUNK), lambda i: (0, 0, 0))],
        [jax.ShapeDtypeStruct((t_len, N_STATE), BF16), jax.ShapeDtypeStruct((t_len, N_STATE), BF16),
         jax.ShapeDtypeStruct((t_len, SSM_W), F32), jax.ShapeDtypeStruct((t_len, rest), BF16),
         bf_d, bf_d, bf_d, bf_s, bf_s, bf_s, bf_s, bf_s,
         jax.ShapeDtypeStruct((SUBLANE, SSM_W), F32), jax.ShapeDtypeStruct((CHUNK, SGU_W), F32),
         jax.ShapeDtypeStruct((SGU_G, CHUNK, CHUNK), F32)],
        [dx2, p, y0, z, mixed, ya, yb, w_out, w_pa, w_pb, w_glu, cre, cim, ws_st, wst_st, d_skip,
         g_sgu], jobs)


def _scan_bwd(dsr, dsi, str_, sti, tab_rev, jobs=()):
    t_len = dsr.shape[0]
    nblk = t_len // SUBLANE
    lb = SCAN_LANES

    def body(dr_ref, di_ref, sr_ref, si_ref, tab_ref, lr_ref, li_ref, dar_ref, dai_ref):
        tab_v = [tab_ref[q] for q in range(8)]
        row0 = lax.broadcasted_iota(jnp.int32, (SUBLANE, lb), 0) == 0
        tile = BF16_TILE

        def step(k, carry):
            cr, ci, acr, aci = carry
            base = pl.multiple_of((nblk - (k + 1) * SCAN_UNROLL) * SUBLANE, SCAN_UNROLL * SUBLANE)
            state = _load_blocks(sr_ref, si_ref, base)
            before = pl.ds(pl.multiple_of(jnp.maximum(base - tile, 0), tile), tile)
            has_before = jnp.where(base > 0, 1.0, 0.0)
            prev = (sr_ref[before, :].astype(F32)[tile - 1:tile] * has_before,
                    si_ref[before, :].astype(F32)[tile - 1:tile] * has_before)
            local = [_scan_local(xr, xi, tab_v, (7, 6, 4))
                     for xr, xi in _load_blocks(dr_ref, di_ref, base)]
            lam = [None] * SCAN_UNROLL
            for b in reversed(range(SCAN_UNROLL)):
                xr, xi = _scan_carry(*local[b], tab_v, cr, ci)
                lam[b] = (xr, xi)
                cr, ci = xr[0:1, :], xi[0:1, :]
                pr, pi = prev if b == 0 else (state[b - 1][0][SUBLANE - 1:], state[b - 1][1][SUBLANE - 1:])
                s_r = jnp.where(row0, pr, pltpu.roll(state[b][0], 1, 0))
                s_i = jnp.where(row0, pi, pltpu.roll(state[b][1], 1, 0))
                acr = acr + xr * s_r + xi * s_i
                aci = aci + xi * s_r - xr * s_i
            _store_blocks(lr_ref, li_ref, base, lam)
            return cr, ci, acr, aci

        zero = jnp.zeros((1, lb), F32)
        zacc = jnp.zeros((SUBLANE, lb), F32)
        _, _, acr, aci = lax.fori_loop(0, nblk // SCAN_UNROLL, step, (zero, zero, zacc, zacc))
        dar_ref[...] = acr
        dai_ref[...] = aci

    col = pl.BlockSpec((t_len, lb), lambda j: (0, j))
    small = pl.BlockSpec((SUBLANE, lb), lambda j: (0, j))
    return _call(
        body, "scan_bwd", (N_STATE // lb,),
        [col, col, col, col, pl.BlockSpec((8, SUBLANE, lb), lambda j: (0, 0, j))],
        [col, col, small, small],
        [jax.ShapeDtypeStruct((t_len, N_STATE), BF16)] * 2
        + [jax.ShapeDtypeStruct((SUBLANE, N_STATE), F32)] * 2,
        [dsr, dsi, str_, sti, tab_rev], jobs)


def _bwd_in(lam_r, lam_i, du_part, drest, x, dx2, g_mix, w_in, bre, bim, jobs=()):
    t_len = x.shape[0]
    cs = IN_COLS // N_CHIP

    def body(lr_ref, li_ref, du_ref, dr_ref, x_ref, dx2_ref, g_ref, w_ref, bre_ref, bim_ref,
             gx_ref, dp_ref, sm_ref):
        i = pl.program_id(0)
        du = du_ref[...] + jnp.concatenate(
            [_dot_nt(lr_ref[:, i * DIAG_N:(i + 1) * DIAG_N],
                     bre_ref[i * LANE:(i + 1) * LANE, i * DIAG_N:(i + 1) * DIAG_N])
             + _dot_nt(li_ref[:, i * DIAG_N:(i + 1) * DIAG_N],
                       bim_ref[i * LANE:(i + 1) * LANE, i * DIAG_N:(i + 1) * DIAG_N])
             for i in range(SSM_W // LANE)], axis=1)
        dp_ref[:, 0:SSM_W] = du.astype(BF16)
        dp_ref[:, SSM_W:] = dr_ref[...]
        dh = jnp.zeros(x_ref.shape, F32)
        for k in range(N_CHIP):
            dh = dh + _dot_nt(dp_ref[:, k * cs:(k + 1) * cs], w_ref[k])
        r, xh = _rms_stats(x_ref[...])
        gx_ref[...] = dx2_ref[...] + _rms_bwd(dh * g_ref[...], xh, r)
        upd = jnp.concatenate([jnp.sum(dh * xh, axis=0, keepdims=True),
                               jnp.zeros((SUBLANE - 1, D_MODEL), F32)], axis=0)

        @pl.when(i == 0)
        def _():
            sm_ref[...] = upd

        @pl.when(i > 0)
        def _():
            sm_ref[...] += upd

    tb = min(TB_WIDE, t_len)
    return _call(
        body, "bwd_in", (t_len // tb,),
        [_rows(tb, N_STATE), _rows(tb, N_STATE), _rows(tb, SSM_W), _rows(tb, IN_COLS - SSM_W),
         _rows(tb, D_MODEL), _rows(tb, D_MODEL)] + [_whole()] * 4,
        [_rows(tb, D_MODEL), _rows(tb, IN_COLS), _acc(SUBLANE, D_MODEL)],
        [jax.ShapeDtypeStruct((t_len, D_MODEL), F32), jax.ShapeDtypeStruct((t_len, IN_COLS), BF16),
         jax.ShapeDtypeStruct((SUBLANE, D_MODEL), F32)],
        [lam_r, lam_i, du_part, drest, x, dx2, g_mix, w_in, bre, bim], jobs)


def _matmul_tn(a, b, name, out_shape, grid_ij, a_blk, a_map, b_blk, b_map, o_blk, o_map, jobs=()):
    tk = a_blk[0]
    nk = a.shape[0] // tk
    assert nk * tk == a.shape[0] and nk > 0

    def body(a_ref, b_ref, o_ref, acc_ref):
        k = pl.program_id(2)

        @pl.when(k == 0)
        def _():
            acc_ref[...] = jnp.zeros_like(acc_ref)

        acc_ref[...] += lax.dot_general(a_ref[...].astype(BF16), b_ref[...].astype(BF16),
                                        (((0,), (0,)), ((), ())), preferred_element_type=F32)

        @pl.when(k == nk - 1)
        def _():
            o_ref[...] = acc_ref[...]

    outs, per_job = _call(
        body, name, (grid_ij[0], grid_ij[1], nk),
        [pl.BlockSpec(a_blk, a_map), pl.BlockSpec(b_blk, b_map)], [pl.BlockSpec(o_blk, o_map)],
        [jax.ShapeDtypeStruct(out_shape, F32)], [a, b], jobs,
        scratch=[pltpu.VMEM((a_blk[1], b_blk[1]), F32)])
    return outs[0], per_job


def _dw_rows(a, b, name, tm, tk):
    m, n = a.shape[1], b.shape[1]
    tk = min(tk, a.shape[0])
    return _matmul_tn(a, b, name, (m, n), (m // tm, 1),
                      (tk, tm), lambda i, j, k: (k, i), (tk, n), lambda i, j, k: (k, 0),
                      (tm, n), lambda i, j, k: (i, 0))[0]


def _dw_cols(a, b, name, tn, sharded, jobs=()):
    t_len, m = a.shape
    n = b.shape[1]

    def body(a_ref, b_ref, o_ref):
        o_ref[...] = lax.dot_general(a_ref[...].astype(BF16), b_ref[...].astype(BF16),
                                     (((0,), (0,)), ((), ())), preferred_element_type=F32)

    if sharded:
        o_spec, o_shape = pl.BlockSpec((None, m, tn), lambda j: (j, 0, 0)), (n // tn, m, tn)
    else:
        o_spec, o_shape = pl.BlockSpec((m, tn), lambda j: (0, j)), (m, n)
    outs, per_job = _call(body, name, (n // tn,),
                          [_whole(), pl.BlockSpec((t_len, tn), lambda j: (0, j))], [o_spec],
                          [jax.ShapeDtypeStruct(o_shape, F32)], [a, b], jobs)
    return outs[0], per_job


def _dw_tiles(a, b, name, tm, tn, jobs=()):
    t_len, m = a.shape
    n = b.shape[1]

    def body(a_ref, b_ref, o_ref):
        o_ref[...] = lax.dot_general(a_ref[...].astype(BF16), b_ref[...].astype(BF16),
                                     (((0,), (0,)), ((), ())), preferred_element_type=F32)

    outs, per_job = _call(body, name, (n // tn, m // tm),
                          [pl.BlockSpec((t_len, tm), lambda j, i: (0, i)),
                           pl.BlockSpec((t_len, tn), lambda j, i: (0, j))],
                          [pl.BlockSpec((None, tm, tn), lambda j, i: (j, i, 0))],
                          [jax.ShapeDtypeStruct((n // tn, m, tn), F32)], [a, b], jobs)
    return outs[0], per_job


def _dw_pair(a, m, b1, b2, name, jobs=()):
    t_len = a.shape[0]
    n_slab = DIAG_N // LANE
    rows_per_slab = LANE // n_slab

    def body(a_ref, b1_ref, b2_ref, o1_ref, o2_ref):
        for b_ref, o_ref in ((b1_ref, o1_ref), (b2_ref, o2_ref)):
            prod = lax.dot_general(a_ref[...].astype(BF16), b_ref[...].astype(BF16),
                                   (((0,), (0,)), ((), ())), preferred_element_type=F32)
            for j in range(n_slab):
                rows = slice(j * rows_per_slab, (j + 1) * rows_per_slab)
                o_ref[rows, :] = prod[rows, j * LANE:(j + 1) * LANE]

    tok = pl.BlockSpec((t_len, DIAG_N), lambda i: (0, i))
    out = pl.BlockSpec((LANE, LANE), lambda i: (i, 0))
    return _call(body, name, (m // LANE,),
                 [pl.BlockSpec((t_len, LANE), lambda i: (0, i)), tok, tok], [out, out],
                 [jax.ShapeDtypeStruct((m, LANE), F32)] * 2, [a, b1, b2], jobs)


def _prefetch_call(body, name, grid, scalars, in_specs, out_specs, out_shape, args):
    return pl.pallas_call(
        body, name=name,
        grid_spec=pltpu.PrefetchScalarGridSpec(num_scalar_prefetch=1, grid=grid, in_specs=in_specs,
                                               out_specs=out_specs),
        out_shape=out_shape, compiler_params=_params(len(grid)),
    )(scalars, *args)


def _place_shard(w, where, name, dtype, tr):
    rows, cols = w.shape

    def body(s_ref, w_ref, o_ref):
        o_ref[...] = w_ref[...].astype(dtype)

    return _prefetch_call(
        body, name, (rows // tr,), where,
        [pl.BlockSpec((tr, cols), lambda i, s: (i, 0))],
        pl.BlockSpec((None, tr, cols), lambda i, s: (s[0], i, 0)),
        jax.ShapeDtypeStruct((N_CHIP, rows, cols), dtype), [w])


def _place_shards(ws, where, name, dtype):
    n = len(ws)

    def body(s_ref, *refs):
        for t in range(n):
            refs[n + t][...] = refs[t][...].astype(dtype)

    return _prefetch_call(
        body, name, (1,), where,
        [pl.BlockSpec(w.shape, lambda i, s: (0, 0)) for w in ws],
        [pl.BlockSpec((None,) + w.shape, lambda i, s: (s[0], 0, 0)) for w in ws],
        [jax.ShapeDtypeStruct((N_CHIP,) + w.shape, dtype) for w in ws], ws)


def _add_sibling(gs, gots, where, name):
    n = len(gs)
    halves = [(g.shape[1] // 2, g.shape[2]) for g in gs]

    def body(s_ref, *refs):
        for t in range(n):
            refs[2 * n + t][...] = (refs[t][...] + refs[n + t][...]).astype(BF16)

    return _prefetch_call(
        body, name, (N_CHIP,), where,
        [pl.BlockSpec((None, hr, cs), lambda k, s: (k, s[1], 0)) for hr, cs in halves]
        + [pl.BlockSpec((None, hr, cs), lambda k, s: (k, 0, 0)) for hr, cs in halves],
        [pl.BlockSpec((None, hr, cs), lambda k, s: (k, 0, 0)) for hr, cs in halves],
        [jax.ShapeDtypeStruct((N_CHIP, hr, cs), BF16) for hr, cs in halves], list(gs) + list(gots))


def _add_chips(sums, gots, where, name):
    n = len(sums)
    halves = [s.shape[1:] for s in sums]

    def body(s_ref, *refs):
        for t in range(n):
            own_ref, got_ref = refs[t], refs[n + t]
            refs[2 * n + t][...] = ((own_ref[...].astype(F32) + got_ref[0].astype(F32))
                                    + got_ref[1].astype(F32)) + got_ref[2].astype(F32)

    return _prefetch_call(
        body, name, (1,), where,
        [pl.BlockSpec((None, hr, cs), lambda i, s: (s[0], 0, 0)) for hr, cs in halves]
        + [pl.BlockSpec((3, hr, cs), lambda i, s: (0, 0, 0)) for hr, cs in halves],
        [pl.BlockSpec((hr, cs), lambda i, s: (s[1], 0)) for hr, cs in halves],
        [jax.ShapeDtypeStruct((2 * hr, cs), F32) for hr, cs in halves], list(sums) + list(gots))


def _small_allreduce(pack):
    rows = pack.shape[0]
    half = rows // 2

    def body(in_ref, out_ref, sib_ref, slots_ref, s_a, r_a, s_b, r_b, s_c, r_c):
        x, y, c, chips = _place()
        k_me = 2 * x + y
        sib = (x, y, 1 - c)
        mine, theirs = _half(rows, c), _half(rows, 1 - c)
        first = _remote(in_ref.at[theirs, :], sib_ref.at[theirs, :], s_a, r_a, sib)
        first.start()
        first.wait_send()
        landed = sib_ref.at[mine, :]
        _remote(landed, landed, s_a, r_a, sib).wait_recv()
        slots_ref[k_me] = in_ref[mine, :] + sib_ref[mine, :]
        cps = [_remote(slots_ref.at[k_me], slots_ref.at[k_me], s_b.at[j], r_b.at[j], (*ch, c))
               for j, ch in enumerate(chips)]
        for cp in cps:
            cp.start()
        for j, ch in enumerate(chips):
            slot = slots_ref.at[_chip_index(ch)]
            _remote(slot, slot, s_b.at[j], r_b.at[j], (*ch, c)).wait_recv()
        for cp in cps:
            cp.wait_send()
        out_ref[mine, :] = ((slots_ref[0] + slots_ref[1]) + slots_ref[2]) + slots_ref[3]
        last = _remote(out_ref.at[mine, :], out_ref.at[mine, :], s_c, r_c, sib)
        last.start()
        other = out_ref.at[theirs, :]
        _remote(other, other, s_c, r_c, sib).wait_recv()
        last.wait_send()

    return pl.pallas_call(
        body, name="small_allreduce", in_specs=[_whole()], out_specs=_whole(),
        out_shape=jax.ShapeDtypeStruct(pack.shape, F32),
        scratch_shapes=[pltpu.VMEM(pack.shape, F32), pltpu.VMEM((N_CHIP, half, LANE), F32),
                        pltpu.SemaphoreType.DMA, pltpu.SemaphoreType.DMA,
                        pltpu.SemaphoreType.DMA((3,)), pltpu.SemaphoreType.DMA((3,)),
                        pltpu.SemaphoreType.DMA, pltpu.SemaphoreType.DMA],
        compiler_params=_params(0),
    )(pack)


def _adamw_update(w_ref, g_ref, m_ref, v_ref, d_ref, mo_ref, vo_ref):
    gv = g_ref[...]
    mn = ADAM_B1 * m_ref[...] + (1.0 - ADAM_B1) * gv
    vn = ADAM_B2 * v_ref[...] + (1.0 - ADAM_B2) * (gv * gv)
    mo_ref[...] = mn
    vo_ref[...] = vn
    m_hat = mn / (1.0 - ADAM_B1 ** ADAM_STEP)
    v_hat = vn / (1.0 - ADAM_B2 ** ADAM_STEP)
    d_ref[...] = -ADAM_LR * (m_hat / (jnp.sqrt(v_hat) + ADAM_EPS) + ADAM_WD * w_ref[...])


def _adamw(w, g, m, v, name, tr):
    rows, cols = w.shape
    blk = _rows(tr, cols)

    def body(w_ref, g_ref, m_ref, v_ref, go_ref, d_ref, mo_ref, vo_ref):
        go_ref[...] = g_ref[...]
        _adamw_update(w_ref, g_ref, m_ref, v_ref, d_ref, mo_ref, vo_ref)

    return _call(body, name, (rows // tr,), [blk] * 4, [blk] * 4,
                 [jax.ShapeDtypeStruct(w.shape, F32)] * 4, [w, g, m, v])[0]


def _adamw_many(ws, gs, ms, vs, name):
    n = len(ws)

    def body(*refs):
        for t in range(n):
            _adamw_update(*[refs[q * n + t] for q in range(7)])

    specs = [pl.BlockSpec(a.shape, lambda i, nd=a.ndim: (0,) * nd) for a in ws]
    outs = pl.pallas_call(
        body, name=name, grid=(1,), in_specs=specs * 4, out_specs=specs * 3,
        out_shape=[jax.ShapeDtypeStruct(a.shape, F32) for _ in range(3) for a in ws],
        compiler_params=_params(1),
    )(*ws, *gs, *ms, *vs)
    return outs[:n], outs[n:2 * n], outs[2 * n:]


def _ssm_discretize(a_re, a_im, log_dt, b_re, b_im):
    dt = jnp.exp(log_dt)[:, None]
    mag = jnp.exp(dt * a_re)
    abr = mag * jnp.cos(dt * a_im)
    abi = mag * jnp.sin(dt * a_im)
    den = a_re * a_re + a_im * a_im
    nr = abr - 1.0
    ni = abi
    f_re = (nr * a_re + ni * a_im) / den
    f_im = (ni * a_re - nr * a_im) / den
    bbr = f_re[..., None] * b_re - f_im[..., None] * b_im
    bbi = f_re[..., None] * b_im + f_im[..., None] * b_re
    return abr, abi, bbr, bbi


def _scan_tables(abr, abi):
    ar = abr.reshape(1, N_STATE)
    ai = abi.reshape(1, N_STATE)
    pr, pi = [ar], [ai]
    for _ in range(SUBLANE - 1):
        pr, pi = pr + [pr[-1] * ar - pi[-1] * ai], pi + [pr[-1] * ai + pi[-1] * ar]
    row = jnp.arange(SUBLANE)[:, None]
    tabs = []
    for d in (1, 2, 4):
        tabs.append(jnp.where(row >= d, pr[d - 1], 0.0))
        tabs.append(jnp.where(row >= d, pi[d - 1], 0.0))
    tabs.append(jnp.concatenate(pr, axis=0))
    tabs.append(jnp.concatenate(pi, axis=0))
    fwd = jnp.stack(tabs)
    sign = jnp.array([1.0, -1.0] * 4, F32)[:, None, None]
    return fwd, fwd[:, ::-1, :] * sign


def _block_diag_b(bb):
    strip = bb.transpose(2, 0, 1).reshape(SSM_H, N_STATE)
    rows = lax.broadcasted_iota(jnp.int32, (SSM_W, N_STATE), 0) // SSM_H
    cols = lax.broadcasted_iota(jnp.int32, (SSM_W, N_STATE), 1) // SSM_P
    return jnp.where(rows == cols, jnp.tile(strip, (SSM_G, 1)), 0.0).astype(BF16)


def _block_diag_c(cc):
    strip = cc.transpose(0, 2, 1).reshape(N_STATE, SSM_H)
    rows = lax.broadcasted_iota(jnp.int32, (N_STATE, SSM_W), 0) // SSM_P
    cols = lax.broadcasted_iota(jnp.int32, (N_STATE, SSM_W), 1) // SSM_H
    return jnp.where(rows == cols, jnp.tile(strip, (1, SSM_G)), 0.0).astype(BF16)


SMALL_SHAPES = {
    "g_mix": (D_MODEL,), "a_re": (SSM_G, SSM_P), "a_im": (SSM_G, SSM_P), "log_dt": (SSM_G,),
    "b_re": (SSM_G, SSM_P, SSM_H), "b_im": (SSM_G, SSM_P, SSM_H),
    "c_re": (SSM_G, SSM_H, SSM_P), "c_im": (SSM_G, SSM_H, SSM_P),
    "d_skip": (SSM_W,), "b_glu": (SSM_W,), "g_sgu": (SGU_W,), "w_s": (SGU_G, CHUNK, CHUNK),
    "b_s": (SGU_G, CHUNK), "g_ffn": (D_MODEL,), "conv_b": (2 * D_FF,), "g_final": (D_MODEL,),
}
PACK_ITEMS = [("loss", (1,))] + [(n, SMALL_SHAPES[n]) for n in SMALL] + [("conv_w", (3, 2 * D_FF))]
TILE = SUBLANE * LANE


def _item_rows(shape):
    return -(-math.prod(shape) // TILE) * SUBLANE


PACK_ROWS = -(-sum(_item_rows(s) for _, s in PACK_ITEMS) // (2 * SUBLANE)) * (2 * SUBLANE)


def _pack(values):
    parts, used = [], 0
    for name, shape in PACK_ITEMS:
        size, rows = math.prod(shape), _item_rows(shape)
        if name in values:
            flat = values[name].astype(F32).reshape(size)
            if rows * LANE > size:
                flat = jnp.pad(flat, (0, rows * LANE - size))
            parts.append(flat.reshape(rows, LANE))
        else:
            parts.append(jnp.zeros((rows, LANE), F32))
        used += rows
    if PACK_ROWS > used:
        parts.append(jnp.zeros((PACK_ROWS - used, LANE), F32))
    return jnp.concatenate(parts, axis=0)


def _unpack(pack):
    out, off = {}, 0
    for name, shape in PACK_ITEMS:
        rows = _item_rows(shape)
        out[name] = pack[off:off + rows].reshape(rows * LANE)[:math.prod(shape)].reshape(shape)
        off += rows
    return out


PLACE_ROWS = {"w_in": 256, "w_up": 256, "w_down": 352, "w_out": 256, "w_proj_a": 256,
              "w_proj_b": 256, "w_glu": 128}


def kernel(x, g_mix, w_in, a_re, a_im, log_dt, b_re, b_im, c_re, c_im, d_skip, w_glu, b_glu, w_proj_a, g_sgu, w_s, b_s, w_proj_b, w_out, g_ffn, w_up, conv_w, conv_b, w_down, g_final, loss_target, m_g_mix, m_w_in, m_a_re, m_a_im, m_log_dt, m_b_re, m_b_im, m_c_re, m_c_im, m_d_skip, m_w_glu, m_b_glu, m_w_proj_a, m_g_sgu, m_w_s, m_b_s, m_w_proj_b, m_w_out, m_g_ffn, m_w_up, m_conv_w, m_conv_b, m_w_down, m_g_final, v_g_mix, v_w_in, v_a_re, v_a_im, v_log_dt, v_b_re, v_b_im, v_c_re, v_c_im, v_d_skip, v_w_glu, v_b_glu, v_w_proj_a, v_g_sgu, v_w_s, v_b_s, v_w_proj_b, v_w_out, v_g_ffn, v_w_up, v_conv_w, v_conv_b, v_w_down, v_g_final):
    given = dict(locals())
    w = {n: given[n] for n in WEIGHTS}
    m = {n: given["m_" + n] for n in WEIGHTS}
    v = {n: given["v_" + n] for n in WEIGHTS}

    def shard2d(a):
        return a.reshape(a.shape[-2], a.shape[-1])

    chip = 2 * lax.axis_index("x") + lax.axis_index("y")
    where = jnp.stack([chip, lax.axis_index("c")]).astype(jnp.int32)
    xs, target = x[0], loss_target[0]
    small = {n: w[n].reshape(SMALL_SHAPES[n]) for n in SMALL}

    (abr, abi, bbr, bbi), disc_vjp = jax.vjp(_ssm_discretize, small["a_re"], small["a_im"],
                                             small["log_dt"], small["b_re"], small["b_im"])
    tab_f, tab_r = _scan_tables(abr, abi)
    bre = _block_diag_b(bbr)
    bim = _block_diag_b(bbi)
    cre = _block_diag_c(small["c_re"])
    cim = _block_diag_c(small["c_im"])
    tril = jnp.tril(jnp.ones((CHUNK, CHUNK), dtype=bool))
    ws = jnp.where(tril[None], small["w_s"], 0.0)
    ws_st = ws.reshape(SGU_G // 2, 2 * CHUNK, CHUNK).astype(BF16)
    wst_st = ws.transpose(0, 2, 1).reshape(SGU_G // 2, 2 * CHUNK, CHUNK).astype(BF16)
    bmat = jnp.repeat(small["b_s"].T, SGU_D, axis=1)
    g_mix2 = small["g_mix"].reshape(1, D_MODEL)
    g_ffn2 = small["g_ffn"].reshape(1, D_MODEL)
    g_final2 = small["g_final"].reshape(1, D_MODEL)
    g_sgu2 = small["g_sgu"].reshape(1, SGU_W)
    d_skip2 = small["d_skip"].reshape(1, SSM_W)
    b_glu2 = small["b_glu"].reshape(1, SSM_W)
    conv_b2 = small["conv_b"].reshape(1, 2 * D_FF)

    gat = {"w_in": _place_shard(shard2d(w["w_in"]), where, "place_w_in", BF16, PLACE_ROWS["w_in"])}
    gat.update(zip(BIG[1:], _place_shards([shard2d(w[n]) for n in BIG[1:]], where, "place_rest", BF16)))
    gat["conv_w"] = _place_shard(shard2d(w["conv_w"]), where, "place_conv_w", F32, 3)
    (gat["w_in"],), = _comm("gather_in", [_job_gather_now(gat["w_in"])])
    mixers = ["w_glu", "w_proj_a", "w_proj_b", "w_out"]
    rows = {n: (0, gat[n].shape[1]) for n in mixers}
    down_a, down_b = (0, D_FF // 8), (D_FF // 8, D_FF // 8)
    up_a, up_b = (0, 3 * D_MODEL // 8), (3 * D_MODEL // 8, 5 * D_MODEL // 8)
    span = (0.0, 1.0)

    names = mixers + ["conv_w", "w_down"]
    (p, h1, bur, bui), (got,) = _fwd_in(
        xs, g_mix2, gat["w_in"], bre, bim,
        [_job_gather([gat[n] for n in names],
                     [(i, rows[n], ICI, span) for i, n in enumerate(mixers)]
                     + [(4, None, ICI, span), (5, down_a, ICI, span)])])
    gat.update(zip(names, got))
    names = mixers + ["w_down", "w_up"]
    (str_, sti), (got,) = _scan_fwd(
        bur, bui, tab_f,
        [_job_gather([gat[n] for n in names],
                     [(i, rows[n], SIBLING, span) for i, n in enumerate(mixers)]
                     + [(4, down_a, SIBLING, span), (4, down_b, ICI, span), (5, up_a, ICI, span)])])
    gat.update(zip(names, got))
    w_glu_f = gat["w_glu"].reshape(SSM_W, SSM_W)
    w_out_f = gat["w_out"].reshape(D_MODEL, D_MODEL)
    conv_w_f = gat["conv_w"].transpose(1, 0, 2).reshape(3, 2 * D_FF)
    (x2, y0, z, mixed, ya, yb), ((gat["w_down"], gat["w_up"]),) = _fwd_mix(
        xs, p, str_, sti, cre, cim, d_skip2, w_glu_f, b_glu2, gat["w_proj_a"], g_sgu2, ws_st, bmat,
        gat["w_proj_b"], w_out_f,
        [_job_gather([gat["w_down"], gat["w_up"]],
                     [(0, down_b, SIBLING, span), (1, up_a, SIBLING, span),
                      (1, up_b, ICI, (0.0, 0.75)), (1, up_b, SIBLING, (0.75, 1.0))])])
    w_down_f = gat["w_down"].reshape(D_FF, D_MODEL)
    up, act, f, h2, dx3, sm_ffn = _fwd_ffn(x2, target, g_ffn2, gat["w_up"], conv_w_f, conv_b2,
                                           w_down_f, g_final2)

    def leg1_done(names, got):
        return _add_sibling([part[n] for n in names], got, where, "add_sibling_" + names[0])

    def leg2_done(names, sums, got):
        return _add_chips(sums, got, where, "add_chips_" + names[0])

    part, red = {}, {}
    part["w_down"] = _dw_rows(f, dx3, "dw_down", D_FF // 2, 4 * TK).reshape(
        N_CHIP, D_FF // N_CHIP, D_MODEL)
    (dx2, dup, sm_conv, sm_gffn), (got,) = _bwd_ffn(
        dx3, up, act, x2, g_ffn2, gat["w_up"], conv_w_f, w_down_f,
        [_job_sibling_halves([part["w_down"]])])
    sum_down = leg1_done(["w_down"], got)
    part["w_up"], (got,) = _dw_tiles(h2, dup, "dw_up", D_MODEL // 2, 2 * D_FF // N_CHIP,
                                     [_job_to_owner(sum_down)])
    red_down = leg2_done(["w_down"], sum_down, got)
    ((dsr, dsi, du_part, drest, mrg, dya, dyb, yap, dz, y1, sgu, dy0, sm_mix, dbm, dws),
     (got, (red["w_down"],))) = _bwd_mix(
        dx2, p, y0, z, mixed, ya, yb, w_out_f, gat["w_proj_a"], gat["w_proj_b"], w_glu_f, cre, cim,
        ws_st, wst_st, d_skip2, g_sgu2,
        [_job_sibling_halves([part["w_up"]]), _job_swap_halves(red_down)])
    sum_up = leg1_done(["w_up"], got)
    (lam_r, lam_i, dar8, dai8), (got,) = _scan_bwd(dsr, dsi, str_, sti, tab_r, [_job_to_owner(sum_up)])
    red_up = leg2_done(["w_up"], sum_up, got)
    mix4 = ["w_out", "w_proj_a", "w_proj_b", "w_glu"]
    part["w_out"] = _dw_cols(mrg, dx2, "dw_out", D_MODEL // 2, False)[0].reshape(
        N_CHIP, D_MODEL // N_CHIP, D_MODEL)
    part["w_proj_a"] = _dw_cols(yap, dya, "dw_proj_a", D_MODEL // N_CHIP, True)[0]
    part["w_proj_b"] = _dw_cols(sgu, dyb, "dw_proj_b", D_MODEL // N_CHIP, True)[0]
    part["w_glu"] = _dw_cols(y1, dz, "dw_glu", SSM_W, False)[0].reshape(
        N_CHIP, SSM_W // N_CHIP, SSM_W)
    got, (red["w_up"],) = _comm(
        "mixer_sibling_halves", [_job_sibling_halves([part[n] for n in mix4]), _job_swap_halves(red_up)])
    (grad_x, dp, sm_gmix), _ = _bwd_in(
        lam_r, lam_i, du_part, drest, xs, dx2, g_mix2, gat["w_in"], bre, bim)
    sums_m = leg1_done(mix4, got)
    part["w_in"], (got,) = _dw_cols(h1, dp, "dw_in", IN_COLS // N_CHIP, True, [_job_to_owner(sums_m)])
    red_m = leg2_done(mix4, sums_m, got)
    (dbd_r, dbd_i), (got, done_m) = _dw_pair(
        p, SSM_W, lam_r, lam_i, "db_bar",
        [_job_sibling_halves([part["w_in"]]), _job_swap_halves(red_m)])
    red.update(zip(mix4, done_m))
    sum_in = leg1_done(["w_in"], got)
    (dcd_r, dcd_i), (got,) = _dw_pair(dy0, SSM_W, str_, sti, "dc", [_job_to_owner(sum_in)])
    red_in = leg2_done(["w_in"], sum_in, got)
    (red["w_in"],), = _comm("swap_w_in", [_job_swap_halves(red_in)])

    def pick_c(slabs):
        two = LANE // SSM_P
        return jnp.einsum("jshsp->jshp", slabs.reshape(SSM_G // two, two, SSM_H, two, SSM_P)
                          ).reshape(SSM_G, SSM_H, SSM_P)

    def pick_b(slabs):
        return pick_c(slabs).transpose(0, 2, 1)

    dabr = jnp.sum(dar8, axis=0).reshape(SSM_G, SSM_P)
    dabi = jnp.sum(dai8, axis=0).reshape(SSM_G, SSM_P)
    d_a_re, d_a_im, d_log_dt, d_b_re, d_b_im = disc_vjp((dabr, dabi, pick_b(dbd_r), pick_b(dbd_i)))
    gsmall = {
        "g_mix": sm_gmix[0], "a_re": d_a_re, "a_im": d_a_im, "log_dt": d_log_dt,
        "b_re": d_b_re, "b_im": d_b_im, "c_re": pick_c(dcd_r), "c_im": -pick_c(dcd_i),
        "d_skip": sm_mix[0], "b_glu": sm_mix[1], "g_sgu": sm_mix[2],
        "w_s": jnp.where(tril[None], dws, 0.0),
        "b_s": dbm.reshape(CHUNK, SGU_G, SGU_D).sum(-1).T,
        "g_ffn": sm_gffn[0], "conv_b": sm_conv[3], "g_final": sm_ffn[0],
        "conv_w": sm_conv[0:3], "loss": sm_ffn[1, 0:1],
    }

    total_pack = _small_allreduce(_pack(gsmall))
    total = _unpack(total_pack)
    grads = dict(red)
    cs = 2 * D_FF // N_CHIP
    grads["conv_w"] = lax.dynamic_slice(total["conv_w"], (0, chip * cs), (3, cs))
    delta, new_m, new_v = {}, {}, {}
    for n in BIG + ("conv_w",):
        grads[n], delta[n], new_m[n], new_v[n] = _adamw(
            shard2d(w[n]), grads[n], shard2d(m[n]), shard2d(v[n]), "adamw_" + n, PLACE_ROWS.get(n, 3))
    for n in SMALL:
        grads[n] = total[n].reshape(w[n].shape)
    ud, um, uv = _adamw_many(*[[d[n] for n in SMALL] for d in (w, grads, m, v)], "adamw_small")
    for i, n in enumerate(SMALL):
        delta[n], new_m[n], new_v[n] = ud[i], um[i], uv[i]

    def like(d):
        return [d[n].reshape(w[n].shape) for n in WEIGHTS]

    return (total["loss"].reshape(()), grad_x.reshape(x.shape), *like(grads), *like(delta),
            *like(new_m), *like(new_v))
```

```python
import math

import jax
import jax.numpy as jnp
from jax import lax
from jax.experimental import pallas as pl
from jax.experimental.pallas import tpu as pltpu

F32 = jnp.float32
BF16 = jnp.bfloat16
MESH = pl.DeviceIdType.MESH

D_MODEL = 1024
SSM_W = 512
SSM_G = 32
SSM_H = 16
SSM_P = 64
N_STATE = SSM_G * SSM_P
DIAG_N = 128 * SSM_P // SSM_H
SGU_W = 512
SGU_G = 8
SGU_D = 64
CHUNK = 128
D_FF = 2816
IN_COLS = 3584
EPS = 1e-6
N_CHIP = 4

ADAM_LR = 0.001
ADAM_B1 = 0.9
ADAM_B2 = 0.999
ADAM_EPS = 1e-08
ADAM_WD = 0.01
ADAM_STEP = 10

SUBLANE = 8
LANE = 128
VMEM_LIMIT = 56 * 1024 * 1024
TB = 256
TB_WIDE = 512
TK = 512
SCAN_LANES = 256
SCAN_UNROLL = 4
HALO = SUBLANE

BIG = ("w_in", "w_up", "w_down", "w_out", "w_proj_a", "w_proj_b", "w_glu")
SMALL = ("g_mix", "a_re", "a_im", "log_dt", "b_re", "b_im", "c_re", "c_im", "d_skip", "b_glu",
         "g_sgu", "w_s", "b_s", "g_ffn", "conv_b", "g_final")
WEIGHTS = ("g_mix", "w_in", "a_re", "a_im", "log_dt", "b_re", "b_im", "c_re", "c_im", "d_skip",
           "w_glu", "b_glu", "w_proj_a", "g_sgu", "w_s", "b_s", "w_proj_b", "w_out", "g_ffn",
           "w_up", "conv_w", "conv_b", "w_down", "g_final")

ANY = pl.BlockSpec(memory_space=pl.ANY)


def _params(n_grid):
    return pltpu.CompilerParams(dimension_semantics=("arbitrary",) * n_grid if n_grid else None,
                                vmem_limit_bytes=VMEM_LIMIT)


def _whole():
    return pl.BlockSpec(memory_space=pltpu.VMEM)


def _rows(tb, ncol):
    return pl.BlockSpec((tb, ncol), lambda i: (i, 0))


def _acc(nrow, ncol):
    return pl.BlockSpec((nrow, ncol), lambda i: (0, 0))


def _dot(a, b):
    return jnp.dot(a.astype(BF16), b.astype(BF16), preferred_element_type=F32)


def _dot_nt(a, b):
    return lax.dot_general(a.astype(BF16), b.astype(BF16), (((1,), (1,)), ((), ())),
                           preferred_element_type=F32)


def _sigmoid(v):
    return 0.5 * jnp.tanh(0.5 * v) + 0.5


_GELU_C = math.sqrt(2.0 / math.pi)


def _gelu(v):
    return 0.5 * v * (1.0 + jnp.tanh(_GELU_C * (v + 0.044715 * v * v * v)))


def _gelu_and_grad(v):
    v2 = v * v
    t = jnp.tanh(_GELU_C * v * (1.0 + 0.044715 * v2))
    half = 0.5 * (1.0 + t)
    return v * half, half + 0.5 * v * (1.0 - t * t) * _GELU_C * (1.0 + 3.0 * 0.044715 * v2)


def _rms_stats(v):
    r = lax.rsqrt(jnp.mean(v * v, axis=-1, keepdims=True) + EPS)
    return r, v * r


def _rms_bwd(dxh, xh, r):
    return r * (dxh - xh * jnp.mean(dxh * xh, axis=-1, keepdims=True))


def _place():
    x, y, c = lax.axis_index("x"), lax.axis_index("y"), lax.axis_index("c")
    chips = [(1 - x, y), (x, 1 - y), (1 - x, 1 - y)]
    return x, y, c, chips


def _chip_index(chip):
    return 2 * chip[0] + chip[1]


def _remote(src, dst, send_sem, recv_sem, device):
    return pltpu.make_async_remote_copy(src_ref=src, dst_ref=dst, send_sem=send_sem,
                                        recv_sem=recv_sem, device_id=device, device_id_type=MESH)


def _half(ref_rows, c):
    hr = ref_rows // 2
    return pl.ds(pl.multiple_of(c * hr, SUBLANE), hr)


class _Job:
    def __init__(self, hooks, n_sem, ins=(), inouts=(), outs=()):
        self.hooks, self.n_sem = list(hooks), n_sem
        self.ins, self.inouts, self.outs = list(ins), list(inouts), list(outs)


def _whole_span(start, finish):
    return [(0.0, "start", start), (1.0, "finish", finish)]


ICI, SIBLING = "ici", "sibling"


def _job_gather(bufs, legs):
    def copies(io, leg, first):
        b, window, kind, _ = legs[leg]
        x, y, c, chips = _place()
        k_me = 2 * x + y
        out = []
        for j, ch in enumerate(chips):
            k = _chip_index(ch)
            if window is None:
                src, land, dev = io[b].at[k_me], io[b].at[k], (*ch, c)
            else:
                r0, rows = window
                mine = pl.ds(pl.multiple_of(r0 + c * (rows // 2), SUBLANE), rows // 2)
                theirs = pl.ds(pl.multiple_of(r0 + (1 - c) * (rows // 2), SUBLANE), rows // 2)
                if kind == ICI:
                    src, land, dev = io[b].at[k_me, mine, :], io[b].at[k, mine, :], (*ch, c)
                else:
                    src, land, dev = io[b].at[k, mine, :], io[b].at[k, theirs, :], (x, y, 1 - c)
            out.append((src, land, first + j, dev))
        return out

    def starter(leg):
        def start(ins, io, outs, ssem, rsem):
            for src, _, i, dev in copies(io, leg, 3 * leg):
                _remote(src, src, ssem(i), rsem(i), dev).start()
        return start

    def finisher(leg):
        def finish(ins, io, outs, ssem, rsem):
            cps = copies(io, leg, 3 * leg)
            for _, land, i, dev in cps:
                _remote(land, land, ssem(i), rsem(i), dev).wait_recv()
            for src, _, i, dev in cps:
                _remote(src, src, ssem(i), rsem(i), dev).wait_send()
        return finish

    hooks = []
    for leg, (_, _, _, (begin, end)) in enumerate(legs):
        hooks += [(begin, "start", starter(leg)), (end, "finish", finisher(leg))]
    return _Job(hooks, 3 * len(legs), inouts=bufs)


def _job_gather_now(buf):
    rows = buf.shape[1]

    def run(ins, io, outs, ssem, rsem):
        x, y, c, chips = _place()
        k_me = 2 * x + y
        sib = (x, y, 1 - c)
        mine, theirs = _half(rows, c), _half(rows, 1 - c)
        own = io[0].at[k_me, mine, :]
        sends = [_remote(own, own, ssem(j), rsem(j), (*ch, c)) for j, ch in enumerate(chips)]
        for cp in sends:
            cp.start()
        passed = []
        for j, ch in enumerate(chips):
            landed = io[0].at[_chip_index(ch), mine, :]
            _remote(landed, landed, ssem(j), rsem(j), sib).wait_recv()
            cp = _remote(landed, landed, ssem(3 + j), rsem(3 + j), sib)
            cp.start()
            passed.append(cp)
        for j, ch in enumerate(chips):
            landed = io[0].at[_chip_index(ch), theirs, :]
            _remote(landed, landed, ssem(3 + j), rsem(3 + j), sib).wait_recv()
        for cp in sends + passed:
            cp.wait_send()

    return _Job([(0.0, "start", run)], 6, inouts=[buf])


def _job_sibling_halves(grads):
    n = len(grads)

    def build(ins, outs, ssem, rsem):
        x, y, c, _ = _place()
        return [_remote(ins[t].at[:, _half(grads[t].shape[1], 1 - c), :], outs[t], ssem(t), rsem(t),
                        (x, y, 1 - c)) for t in range(n)]

    def start(ins, io, outs, ssem, rsem):
        for cp in build(ins, outs, ssem, rsem):
            cp.start()

    def finish(ins, io, outs, ssem, rsem):
        for cp in build(ins, outs, ssem, rsem):
            cp.wait()

    return _Job(_whole_span(start, finish), n, ins=grads,
                outs=[jax.ShapeDtypeStruct((N_CHIP, g.shape[1] // 2, g.shape[2]), F32) for g in grads])


def _job_to_owner(sums):
    n = len(sums)

    def build(ins, outs, ssem, rsem):
        x, y, c, chips = _place()
        return [_remote(ins[t].at[_chip_index(ch)], outs[t].at[j], ssem(3 * t + j), rsem(3 * t + j),
                        (*ch, c)) for t in range(n) for j, ch in enumerate(chips)]

    def start(ins, io, outs, ssem, rsem):
        for cp in build(ins, outs, ssem, rsem):
            cp.start()

    def finish(ins, io, outs, ssem, rsem):
        for cp in build(ins, outs, ssem, rsem):
            cp.wait()

    return _Job(_whole_span(start, finish), 3 * n, ins=sums,
                outs=[jax.ShapeDtypeStruct((3,) + s.shape[1:], s.dtype) for s in sums])


def _job_swap_halves(bufs):
    n = len(bufs)

    def start(ins, io, outs, ssem, rsem):
        x, y, c, _ = _place()
        for t in range(n):
            mine = io[t].at[_half(bufs[t].shape[0], c), :]
            _remote(mine, mine, ssem(t), rsem(t), (x, y, 1 - c)).start()

    def finish(ins, io, outs, ssem, rsem):
        x, y, c, _ = _place()
        for t in range(n):
            theirs = io[t].at[_half(bufs[t].shape[0], 1 - c), :]
            _remote(theirs, theirs, ssem(t), rsem(t), (x, y, 1 - c)).wait_recv()
        for t in range(n):
            mine = io[t].at[_half(bufs[t].shape[0], c), :]
            _remote(mine, mine, ssem(t), rsem(t), (x, y, 1 - c)).wait_send()

    return _Job(_whole_span(start, finish), n, inouts=bufs)


def _call(body, name, grid, in_specs, out_specs, out_shape, args, jobs=(), scratch=()):
    n_in, n_out, n_scr = len(args), len(out_shape), len(scratch)
    job_in = [a for jb in jobs for a in jb.ins + jb.inouts]
    job_out = [s for jb in jobs
               for s in [jax.ShapeDtypeStruct(a.shape, a.dtype) for a in jb.inouts] + jb.outs]
    aliases, pos_in, pos_out = {}, n_in, n_out
    for jb in jobs:
        pos_in += len(jb.ins)
        for _ in jb.inouts:
            aliases[pos_in] = pos_out
            pos_in += 1
            pos_out += 1
        pos_out += len(jb.outs)
    n_sem = sum(jb.n_sem for jb in jobs)

    def wrapped(*refs):
        c_in = refs[:n_in]
        j_in = refs[n_in:n_in + len(job_in)]
        c_out = refs[n_in + len(job_in):n_in + len(job_in) + n_out]
        j_out = refs[n_in + len(job_in) + n_out:n_in + len(job_in) + n_out + len(job_out)]
        rest = refs[n_in + len(job_in) + n_out + len(job_out):]
        c_scr = rest[:n_scr]
        views, pi, po, ps = [], 0, 0, 0
        for jb in jobs:
            ins = j_in[pi:pi + len(jb.ins)]
            pi += len(jb.ins) + len(jb.inouts)
            io = j_out[po:po + len(jb.inouts)]
            new = j_out[po + len(jb.inouts):po + len(jb.inouts) + len(jb.outs)]
            po += len(jb.inouts) + len(jb.outs)
            send = (lambda i, o=ps: rest[n_scr].at[o + i])
            recv = (lambda i, o=ps: rest[n_scr + 1].at[o + i])
            ps += jb.n_sem
            views.append((ins, io, new, send, recv))

        def run(frac):
            for kind in ("finish", "start"):
                for jb, vw in zip(jobs, views):
                    for at, what, fn in jb.hooks:
                        if at == frac and what == kind:
                            fn(*vw)

        fracs = sorted({at for jb in jobs for at, _, _ in jb.hooks})
        if not grid:
            for frac in fracs:
                run(frac)
            return
        if jobs:
            assert len(grid) == 1 or set(fracs) <= {0.0, 1.0}
            first = pl.program_id(0) == 0
            last = pl.program_id(0) == grid[0] - 1
            for d in range(1, len(grid)):
                first = jnp.logical_and(first, pl.program_id(d) == 0)
                last = jnp.logical_and(last, pl.program_id(d) == grid[d] - 1)
            for frac in fracs:
                if frac < 1.0:
                    at_step = first if frac == 0.0 else pl.program_id(0) == int(frac * grid[0])
                    pl.when(at_step)(lambda frac=frac: run(frac))
        body(*c_in, *c_out, *c_scr)
        if jobs and 1.0 in fracs:
            pl.when(last)(lambda: run(1.0))

    sems = [pltpu.SemaphoreType.DMA((n_sem,)), pltpu.SemaphoreType.DMA((n_sem,))] if jobs else []
    kwargs = dict(grid=grid) if grid else {}
    res = pl.pallas_call(
        wrapped, name=name, in_specs=list(in_specs) + [ANY] * len(job_in),
        out_specs=list(out_specs) + [ANY] * len(job_out),
        out_shape=list(out_shape) + job_out, scratch_shapes=list(scratch) + sems,
        input_output_aliases=aliases, compiler_params=_params(len(grid)), **kwargs,
    )(*args, *job_in)
    outs, pos, per_job = list(res[:n_out]), n_out, []
    for jb in jobs:
        k = len(jb.inouts) + len(jb.outs)
        per_job.append(list(res[pos:pos + k]))
        pos += k
    return outs, per_job


def _comm(name, jobs):
    return _call(None, name, (), [], [], [], [], jobs)[1]


def _fwd_in(x, g_mix, w_in, bre, bim, jobs=()):
    t_len = x.shape[0]
    cs = IN_COLS // N_CHIP

    def body(x_ref, g_ref, w_ref, bre_ref, bim_ref, p_ref, h_ref, bur_ref, bui_ref):
        xv = x_ref[...]
        r, xh = _rms_stats(xv)
        h = (xh * g_ref[...]).astype(BF16)
        h_ref[...] = h
        for k in range(N_CHIP):
            p_ref[:, k * cs:(k + 1) * cs] = jnp.dot(h, w_ref[k],
                                                    preferred_element_type=F32).astype(BF16)
        u = p_ref[:, 0:SSM_W]
        for i in range(SSM_W // LANE):
            rows, cols = slice(i * LANE, (i + 1) * LANE), slice(i * DIAG_N, (i + 1) * DIAG_N)
            bur_ref[:, cols] = jnp.dot(u[:, rows], bre_ref[rows, cols],
                                       preferred_element_type=F32).astype(BF16)
            bui_ref[:, cols] = jnp.dot(u[:, rows], bim_ref[rows, cols],
                                       preferred_element_type=F32).astype(BF16)

    tb = min(TB_WIDE, t_len)
    return _call(
        body, "fwd_in", (t_len // tb,),
        [_rows(tb, D_MODEL), _whole(), _whole(), _whole(), _whole()],
        [_rows(tb, IN_COLS), _rows(tb, D_MODEL), _rows(tb, N_STATE), _rows(tb, N_STATE)],
        [jax.ShapeDtypeStruct((t_len, IN_COLS), BF16), jax.ShapeDtypeStruct((t_len, D_MODEL), BF16),
         jax.ShapeDtypeStruct((t_len, N_STATE), BF16), jax.ShapeDtypeStruct((t_len, N_STATE), BF16)],
        [x, g_mix, w_in, bre, bim], jobs)


def _scan_local(xr, xi, tab, shifts):
    for q, s in enumerate(shifts):
        ar, ai = tab[2 * q], tab[2 * q + 1]
        rr = pltpu.roll(xr, s, 0)
        ri = pltpu.roll(xi, s, 0)
        xr, xi = xr + ar * rr - ai * ri, xi + ar * ri + ai * rr
    return xr, xi


def _scan_carry(xr, xi, tab, cr, ci):
    pr, pi = tab[6], tab[7]
    return xr + pr * cr - pi * ci, xi + pr * ci + pi * cr


BF16_TILE = 2 * SUBLANE


def _load_blocks(r_ref, i_ref, base):
    out = []
    for q in range(SCAN_UNROLL // 2):
        rows = pl.ds(pl.multiple_of(base + q * BF16_TILE, BF16_TILE), BF16_TILE)
        vr, vi = r_ref[rows, :].astype(F32), i_ref[rows, :].astype(F32)
        out += [(vr[:SUBLANE], vi[:SUBLANE]), (vr[SUBLANE:], vi[SUBLANE:])]
    return out


def _store_blocks(r_ref, i_ref, base, blocks):
    for q in range(SCAN_UNROLL // 2):
        rows = pl.ds(pl.multiple_of(base + q * BF16_TILE, BF16_TILE), BF16_TILE)
        r_ref[rows, :] = jnp.concatenate([blocks[2 * q][0], blocks[2 * q + 1][0]], 0).astype(r_ref.dtype)
        i_ref[rows, :] = jnp.concatenate([blocks[2 * q][1], blocks[2 * q + 1][1]], 0).astype(i_ref.dtype)


def _scan_fwd(bur, bui, tab, jobs=()):
    t_len = bur.shape[0]
    nblk = t_len // SUBLANE
    lb = SCAN_LANES

    def body(br_ref, bi_ref, tab_ref, sr_ref, si_ref):
        tab_v = [tab_ref[q] for q in range(8)]

        def step(k, carry):
            cr, ci = carry
            base = pl.multiple_of(k * SCAN_UNROLL * SUBLANE, SCAN_UNROLL * SUBLANE)
            local = [_scan_local(xr, xi, tab_v, (1, 2, 4))
                     for xr, xi in _load_blocks(br_ref, bi_ref, base)]
            done = []
            for xr, xi in local:
                xr, xi = _scan_carry(xr, xi, tab_v, cr, ci)
                done.append((xr, xi))
                cr, ci = xr[SUBLANE - 1:SUBLANE, :], xi[SUBLANE - 1:SUBLANE, :]
            _store_blocks(sr_ref, si_ref, base, done)
            return cr, ci

        zero = jnp.zeros((1, lb), F32)
        lax.fori_loop(0, nblk // SCAN_UNROLL, step, (zero, zero))

    col = pl.BlockSpec((t_len, lb), lambda j: (0, j))
    return _call(
        body, "scan_fwd", (N_STATE // lb,),
        [col, col, pl.BlockSpec((8, SUBLANE, lb), lambda j: (0, 0, j))], [col, col],
        [jax.ShapeDtypeStruct((t_len, N_STATE), BF16)] * 2, [bur, bui, tab], jobs)


def _sgu_mix(v, ws_ref, lane_lo):
    rows = []
    for c0 in range(0, v.shape[0], CHUNK):
        slabs = []
        for j in range(SGU_W // LANE):
            prod = jnp.dot(ws_ref[j], v[c0:c0 + CHUNK, j * LANE:(j + 1) * LANE].astype(BF16),
                           preferred_element_type=F32)
            slabs.append(jnp.where(lane_lo, prod[:CHUNK], prod[CHUNK:]))
        rows.append(jnp.concatenate(slabs, axis=1))
    return jnp.concatenate(rows, axis=0) if len(rows) > 1 else rows[0]


def _fwd_mix(x, p, str_, sti, cre, cim, d_skip, w_glu, b_glu, w_pa, g_sgu, ws_st, bmat, w_pb, w_out,
             jobs=()):
    t_len = x.shape[0]

    def body(x_ref, p_ref, sr_ref, si_ref, cre_ref, cim_ref, dsk_ref, wg_ref, bg_ref, wpa_ref,
             gs_ref, ws_ref, bm_ref, wpb_ref, wo_ref,
             x2_ref, y0_ref, z_ref, mx_ref, ya_ref, yb_ref):
        u = p_ref[:, 0:SSM_W].astype(F32)
        y0 = jnp.concatenate(
            [_dot(sr_ref[:, i * DIAG_N:(i + 1) * DIAG_N],
                  cre_ref[i * DIAG_N:(i + 1) * DIAG_N, i * LANE:(i + 1) * LANE])
             - _dot(si_ref[:, i * DIAG_N:(i + 1) * DIAG_N],
                    cim_ref[i * DIAG_N:(i + 1) * DIAG_N, i * LANE:(i + 1) * LANE])
             for i in range(SSM_W // LANE)], axis=1) + dsk_ref[...] * u
        y0_ref[...] = y0.astype(BF16)
        y1 = _gelu(y0)
        z = _dot(y1, wg_ref[...]) + bg_ref[...]
        z_ref[...] = z.astype(BF16)
        ya_pre = (y1 * _sigmoid(z)).astype(BF16)
        ya = jnp.concatenate([jnp.dot(ya_pre, wpa_ref[k], preferred_element_type=F32)
                              for k in range(N_CHIP)], axis=1)
        ya_ref[...] = ya.astype(BF16)

        uvg = _gelu(p_ref[:, SSM_W:SSM_W + 2 * SGU_W].astype(F32))
        u2 = uvg[:, :SGU_W]
        _, vh = _rms_stats(uvg[:, SGU_W:])
        v3 = vh * gs_ref[...]
        lane_lo = lax.broadcasted_iota(jnp.int32, (CHUNK, LANE), 1) < SGU_D
        bias = jnp.concatenate([bm_ref[...]] * (TB // CHUNK), axis=0)
        mixed = _sgu_mix(v3, ws_ref, lane_lo) + bias
        mx_ref[...] = mixed.astype(BF16)
        sgu = (u2 * mixed).astype(BF16)
        yb = jnp.concatenate([jnp.dot(sgu, wpb_ref[k], preferred_element_type=F32)
                              for k in range(N_CHIP)], axis=1)
        yb_ref[...] = yb.astype(BF16)

        lg0 = SSM_W + 2 * SGU_W
        ga = _sigmoid(p_ref[:, lg0:lg0 + D_MODEL].astype(F32))
        gb = _sigmoid(p_ref[:, lg0 + D_MODEL:lg0 + 2 * D_MODEL].astype(F32))
        mrg = ga * ya + gb * yb
        x2_ref[...] = x_ref[...] + _dot(mrg, wo_ref[...])

    return _call(
        body, "fwd_mix", (t_len // TB,),
        [_rows(TB, D_MODEL), _rows(TB, IN_COLS), _rows(TB, N_STATE), _rows(TB, N_STATE)]
        + [_whole()] * 11,
        [_rows(TB, D_MODEL), _rows(TB, SSM_W), _rows(TB, SSM_W), _rows(TB, SGU_W),
         _rows(TB, D_MODEL), _rows(TB, D_MODEL)],
        [jax.ShapeDtypeStruct((t_len, D_MODEL), F32), jax.ShapeDtypeStruct((t_len, SSM_W), BF16),
         jax.ShapeDtypeStruct((t_len, SSM_W), BF16), jax.ShapeDtypeStruct((t_len, SGU_W), BF16),
         jax.ShapeDtypeStruct((t_len, D_MODEL), BF16), jax.ShapeDtypeStruct((t_len, D_MODEL), BF16)],
        [x, p, str_, sti, cre, cim, d_skip, w_glu, b_glu, w_pa, g_sgu, ws_st, bmat, w_pb, w_out], jobs)


def _conv_taps(v, cw_ref, c0, width):
    w0 = cw_ref[0:1, c0:c0 + width]
    w1 = cw_ref[1:2, c0:c0 + width]
    w2 = cw_ref[2:3, c0:c0 + width]
    return w0 * pltpu.roll(v, 2, 0) + w1 * pltpu.roll(v, 1, 0) + w2 * v


def _fwd_ffn(x2, target, g_ffn, w_up, conv_w, conv_b, w_down, g_final):
    t_len = x2.shape[0]
    half = D_FF // 2
    blocks_per_halo = TB // HALO

    def body(x2_ref, xp_ref, tg_ref, gf_ref, wu_ref, cw_ref, cb_ref, wd_ref, gl_ref,
             up_ref, act_ref, f_ref, h2_ref, dx3_ref, sm_ref):
        i = pl.program_id(0)
        xe = jnp.concatenate([xp_ref[...] * jnp.where(i == 0, 0.0, 1.0), x2_ref[...]], axis=0)
        _, xh = _rms_stats(xe)
        h2 = (xh * gf_ref[...]).astype(BF16)
        h2_ref[...] = h2[HALO:]
        acc = jnp.zeros((TB, D_MODEL), F32)
        ups = [jnp.dot(h2, wu_ref[k], preferred_element_type=F32) for k in range(N_CHIP)]
        for hc in range(2):
            ca = hc * half
            cb = D_FF + hc * half
            ua, ub = ups[hc], ups[2 + hc]
            up_ref[:, ca:ca + half] = ua[HALO:].astype(BF16)
            up_ref[:, cb:cb + half] = ub[HALO:].astype(BF16)
            ac = _conv_taps(ua, cw_ref, ca, half)[HALO:] + cb_ref[:, ca:ca + half]
            bc = _conv_taps(ub, cw_ref, cb, half)[HALO:] + cb_ref[:, cb:cb + half]
            act_ref[:, ca:ca + half] = ac.astype(BF16)
            act_ref[:, cb:cb + half] = bc.astype(BF16)
            f = (ac * _sigmoid(ac) * bc).astype(BF16)
            f_ref[:, ca:ca + half] = f
            acc = acc + jnp.dot(f, wd_ref[ca:ca + half, :], preferred_element_type=F32)
        x3 = x2_ref[...] + acc
        r3, xh3 = _rms_stats(x3)
        err = xh3 * gl_ref[...] - tg_ref[...]
        dout = err * (1.0 / D_MODEL)
        dx3_ref[...] = _rms_bwd(dout * gl_ref[...], xh3, r3)
        dgl = jnp.sum(dout * xh3, axis=0, keepdims=True)
        loss = 0.5 * jnp.sum(jnp.mean(err * err, axis=-1, keepdims=True), axis=0, keepdims=True)
        upd = jnp.concatenate([dgl, jnp.broadcast_to(loss, (1, D_MODEL)),
                               jnp.zeros((SUBLANE - 2, D_MODEL), F32)], axis=0)

        @pl.when(i == 0)
        def _():
            sm_ref[...] = upd

        @pl.when(i > 0)
        def _():
            sm_ref[...] += upd

    prev = pl.BlockSpec((HALO, D_MODEL), lambda i: (jnp.maximum(i * blocks_per_halo - 1, 0), 0))
    return _call(
        body, "fwd_ffn", (t_len // TB,),
        [_rows(TB, D_MODEL), prev, _rows(TB, D_MODEL)] + [_whole()] * 6,
        [_rows(TB, 2 * D_FF), _rows(TB, 2 * D_FF), _rows(TB, D_FF), _rows(TB, D_MODEL),
         _rows(TB, D_MODEL), _acc(SUBLANE, D_MODEL)],
        [jax.ShapeDtypeStruct((t_len, 2 * D_FF), BF16), jax.ShapeDtypeStruct((t_len, 2 * D_FF), BF16),
         jax.ShapeDtypeStruct((t_len, D_FF), BF16), jax.ShapeDtypeStruct((t_len, D_MODEL), BF16),
         jax.ShapeDtypeStruct((t_len, D_MODEL), F32), jax.ShapeDtypeStruct((SUBLANE, D_MODEL), F32)],
        [x2, x2, target, g_ffn, w_up, conv_w, conv_b, w_down, g_final])[0]


def _bwd_ffn(dx3, up, act, x2, g_ffn, w_up, conv_w, w_down, jobs=()):
    t_len = x2.shape[0]
    half = D_FF // 2
    nblk = t_len // TB
    halo_b = 2 * HALO
    n_e = TB + HALO

    def body(dx_ref, dxn_ref, up_ref, act_ref, actn_ref, x2_ref, gf_ref, wu_ref, cw_ref,
             wd_ref, dx2_ref, dup_ref, smw_ref, smg_ref):
        i = pl.program_id(0)
        keep_last = jnp.where(i == nblk - 1, 0.0, 1.0)
        dxe = jnp.concatenate([dx_ref[...], dxn_ref[...] * keep_last], axis=0).astype(BF16)
        dh2 = jnp.zeros((TB, D_MODEL), F32)
        zpad = jnp.zeros((1, half), F32)
        dfs = [lax.dot_general(dxe, wd_ref[hc * half:(hc + 1) * half, :], (((1,), (1,)), ((), ())),
                               preferred_element_type=F32) for hc in range(2)]
        for hc in range(2):
            ca = hc * half
            cb = D_FF + hc * half
            ac = jnp.concatenate([act_ref[:, ca:ca + half].astype(F32),
                                  actn_ref[:, ca:ca + half].astype(F32)[:HALO]], axis=0)
            bc = jnp.concatenate([act_ref[:, cb:cb + half].astype(F32),
                                  actn_ref[:, cb:cb + half].astype(F32)[:HALO]], axis=0)
            wa = [cw_ref[k:k + 1, ca:ca + half] for k in range(3)]
            wb = [cw_ref[k:k + 1, cb:cb + half] for k in range(3)]
            df = dfs[hc]
            sg = _sigmoid(ac)
            da = df * bc * sg * (1.0 + ac * (1.0 - sg))
            db = df * ac * sg
            da1, da2 = pltpu.roll(da, n_e - 1, 0), pltpu.roll(da, n_e - 2, 0)
            db1, db2 = pltpu.roll(db, n_e - 1, 0), pltpu.roll(db, n_e - 2, 0)
            dua = (wa[2] * da + wa[1] * da1 + wa[0] * da2)[:TB]
            dub = (wb[2] * db + wb[1] * db1 + wb[0] * db2)[:TB]
            dup_ref[:, ca:ca + half] = dua.astype(BF16)
            dup_ref[:, cb:cb + half] = dub.astype(BF16)
            dh2 = dh2 + _dot_nt(dua, wu_ref[hc]) + _dot_nt(dub, wu_ref[2 + hc])
            rows = []
            for u_, d0, d1, d2 in ((up_ref[:, ca:ca + half].astype(F32), da, da1, da2),
                                   (up_ref[:, cb:cb + half].astype(F32), db, db1, db2)):
                rows.append([jnp.sum(u_ * d2[:TB], axis=0, keepdims=True),
                             jnp.sum(u_ * d1[:TB], axis=0, keepdims=True),
                             jnp.sum(u_ * d0[:TB], axis=0, keepdims=True),
                             jnp.sum(d0[:TB], axis=0, keepdims=True)])
            for c0, rws in ((ca, rows[0]), (cb, rows[1])):
                upd = jnp.concatenate(rws + [zpad] * (SUBLANE - 4), axis=0)

                @pl.when(i == 0)
                def _(upd=upd, c0=c0):
                    smw_ref[:, c0:c0 + half] = upd

                @pl.when(i > 0)
                def _(upd=upd, c0=c0):
                    smw_ref[:, c0:c0 + half] += upd

        r2, xh2 = _rms_stats(x2_ref[...])
        dx2_ref[...] = dx_ref[...] + _rms_bwd(dh2 * gf_ref[...], xh2, r2)
        updg = jnp.concatenate([jnp.sum(dh2 * xh2, axis=0, keepdims=True),
                                jnp.zeros((SUBLANE - 1, D_MODEL), F32)], axis=0)

        @pl.when(i == 0)
        def _():
            smg_ref[...] = updg

        @pl.when(i > 0)
        def _():
            smg_ref[...] += updg

    nxt_d = pl.BlockSpec((HALO, D_MODEL),
                         lambda i: (jnp.minimum((i + 1) * (TB // HALO), t_len // HALO - 1), 0))
    nxt_a = pl.BlockSpec((halo_b, 2 * D_FF),
                         lambda i: (jnp.minimum((i + 1) * (TB // halo_b), t_len // halo_b - 1), 0))
    return _call(
        body, "bwd_ffn", (nblk,),
        [_rows(TB, D_MODEL), nxt_d, _rows(TB, 2 * D_FF), _rows(TB, 2 * D_FF), nxt_a,
         _rows(TB, D_MODEL)] + [_whole()] * 4,
        [_rows(TB, D_MODEL), _rows(TB, 2 * D_FF), _acc(SUBLANE, 2 * D_FF), _acc(SUBLANE, D_MODEL)],
        [jax.ShapeDtypeStruct((t_len, D_MODEL), F32), jax.ShapeDtypeStruct((t_len, 2 * D_FF), BF16),
         jax.ShapeDtypeStruct((SUBLANE, 2 * D_FF), F32), jax.ShapeDtypeStruct((SUBLANE, D_MODEL), F32)],
        [dx3, dx3, up, act, act, x2, g_ffn, w_up, conv_w, w_down], jobs)


def _bwd_mix(dx2, p, y0, z, mixed, ya, yb, w_out, w_pa, w_pb, w_glu, cre, cim, ws_st, wst_st,
             d_skip, g_sgu, jobs=()):
    t_len = dx2.shape[0]
    pc = D_MODEL // N_CHIP
    n_slab = SGU_W // LANE

    def body(dx_ref, p_ref, y0_ref, z_ref, mx_ref, ya_ref, yb_ref, wo_ref, wpa_ref, wpb_ref,
             wg_ref, cre_ref, cim_ref, ws_ref, wst_ref, dsk_ref, gs_ref,
             dsr_ref, dsi_ref, du_ref, drest_ref, mrg_ref, dya_ref, dyb_ref, yap_ref, dz_ref,
             y1_ref, sgu_ref, dy0_ref, sm_ref, dbm_ref, dws_ref):
        i = pl.program_id(0)
        first = i == 0
        lg0 = SSM_W + 2 * SGU_W
        y0v = y0_ref[...].astype(F32)
        y1, y1_grad = _gelu_and_grad(y0v)
        sz = _sigmoid(z_ref[...].astype(F32))
        y1_ref[...] = y1.astype(BF16)
        yap_ref[...] = (y1 * sz).astype(BF16)

        dxb = dx_ref[...].astype(BF16)
        dyap = jnp.zeros((TB, SSM_W), F32)
        dsgu = jnp.zeros((TB, SGU_W), F32)
        wide = D_MODEL // 2
        for h in range(2):
            cols = slice(h * wide, (h + 1) * wide)
            ga = _sigmoid(p_ref[:, lg0 + h * wide:lg0 + (h + 1) * wide].astype(F32))
            gb = _sigmoid(
                p_ref[:, lg0 + D_MODEL + h * wide:lg0 + D_MODEL + (h + 1) * wide].astype(F32))
            yav = ya_ref[:, cols].astype(F32)
            ybv = yb_ref[:, cols].astype(F32)
            mrg_ref[:, cols] = (ga * yav + gb * ybv).astype(BF16)
            dmrg = _dot_nt(dxb, wo_ref[cols, :])
            drest_ref[:, 2 * SGU_W + h * wide:2 * SGU_W + (h + 1) * wide] = (
                dmrg * yav * ga * (1.0 - ga)).astype(BF16)
            drest_ref[:, 2 * SGU_W + D_MODEL + h * wide:2 * SGU_W + D_MODEL + (h + 1) * wide] = (
                dmrg * ybv * gb * (1.0 - gb)).astype(BF16)
            dya = (dmrg * ga).astype(BF16)
            dyb = (dmrg * gb).astype(BF16)
            dya_ref[:, cols] = dya
            dyb_ref[:, cols] = dyb
            for k in range(wide // pc):
                shard = h * (wide // pc) + k
                dyap = dyap + _dot_nt(dya[:, k * pc:(k + 1) * pc], wpa_ref[shard])
                dsgu = dsgu + _dot_nt(dyb[:, k * pc:(k + 1) * pc], wpb_ref[shard])

        dz = dyap * y1 * sz * (1.0 - sz)
        dz_ref[...] = dz.astype(BF16)
        dy0 = (dyap * sz + _dot_nt(dz, wg_ref[...])) * y1_grad
        dy0_ref[...] = dy0.astype(BF16)
        u = p_ref[:, 0:SSM_W].astype(F32)
        du_ref[...] = dy0 * dsk_ref[...]
        for q in range(SSM_W // LANE):
            rows, cols = slice(q * DIAG_N, (q + 1) * DIAG_N), slice(q * LANE, (q + 1) * LANE)
            dsr_ref[:, rows] = _dot_nt(dy0[:, cols], cre_ref[rows, cols]).astype(BF16)
            dsi_ref[:, rows] = (-_dot_nt(dy0[:, cols], cim_ref[rows, cols])).astype(BF16)

        u2, u_grad = _gelu_and_grad(p_ref[:, SSM_W:SSM_W + SGU_W].astype(F32))
        mixed = mx_ref[...].astype(F32)
        sgu_ref[...] = (u2 * mixed).astype(BF16)
        drest_ref[:, 0:SGU_W] = (dsgu * mixed * u_grad).astype(BF16)
        dmix = dsgu * u2
        v2, v_grad = _gelu_and_grad(p_ref[:, SSM_W + SGU_W:lg0].astype(F32))
        rv, vh = _rms_stats(v2)
        v3 = vh * gs_ref[...]
        lane_lo = lax.broadcasted_iota(jnp.int32, (CHUNK, LANE), 1) < SGU_D
        dv3 = _sgu_mix(dmix, wst_ref, lane_lo)
        dbm = jnp.zeros((CHUNK, SGU_W), F32)
        for c0 in range(0, TB, CHUNK):
            dbm = dbm + dmix[c0:c0 + CHUNK]
        for j in range(n_slab):
            lo = jnp.zeros((CHUNK, CHUNK), F32)
            hi = jnp.zeros((CHUNK, CHUNK), F32)
            for c0 in range(0, TB, CHUNK):
                dsl = dmix[c0:c0 + CHUNK, j * LANE:(j + 1) * LANE]
                vsl = v3[c0:c0 + CHUNK, j * LANE:(j + 1) * LANE]
                lo = lo + _dot_nt(jnp.where(lane_lo, dsl, 0.0), vsl)
                hi = hi + _dot_nt(jnp.where(lane_lo, 0.0, dsl), vsl)

            @pl.when(first)
            def _(lo=lo, hi=hi, j=j):
                dws_ref[2 * j] = lo
                dws_ref[2 * j + 1] = hi

            @pl.when(jnp.logical_not(first))
            def _(lo=lo, hi=hi, j=j):
                dws_ref[2 * j] += lo
                dws_ref[2 * j + 1] += hi

        dv2 = _rms_bwd(dv3 * gs_ref[...], vh, rv)
        drest_ref[:, SGU_W:2 * SGU_W] = (dv2 * v_grad).astype(BF16)

        upd = jnp.concatenate([jnp.sum(dy0 * u, axis=0, keepdims=True),
                               jnp.sum(dz, axis=0, keepdims=True),
                               jnp.sum(dv3 * vh, axis=0, keepdims=True),
                               jnp.zeros((SUBLANE - 3, SSM_W), F32)], axis=0)

        @pl.when(first)
        def _():
            sm_ref[...] = upd
            dbm_ref[...] = dbm

        @pl.when(jnp.logical_not(first))
        def _():
            sm_ref[...] += upd
            dbm_ref[...] += dbm

    rest = 2 * SGU_W + 2 * D_MODEL
    bf_d, bf_s = jax.ShapeDtypeStruct((t_len, D_MODEL), BF16), jax.ShapeDtypeStruct((t_len, SSM_W), BF16)
    return _call(
        body, "bwd_mix", (t_len // TB,),
        [_rows(TB, D_MODEL), _rows(TB, IN_COLS), _rows(TB, SSM_W), _rows(TB, SSM_W),
         _rows(TB, SGU_W), _rows(TB, D_MODEL), _rows(TB, D_MODEL)] + [_whole()] * 10,
        [_rows(TB, N_STATE), _rows(TB, N_STATE), _rows(TB, SSM_W), _rows(TB, rest),
         _rows(TB, D_MODEL), _rows(TB, D_MODEL), _rows(TB, D_MODEL), _rows(TB, SSM_W),
         _rows(TB, SSM_W), _rows(TB, SSM_W), _rows(TB, SGU_W), _rows(TB, SSM_W),
         _acc(SUBLANE, SSM_W), _acc(CHUNK, SGU_W),
         pl.BlockSpec((SGU_G, CHUNK, CHUNK), lambda i: (0, 0, 0))],
        [jax.ShapeDtypeStruct((t_len, N_STATE), BF16), jax.ShapeDtypeStruct((t_len, N_STATE), BF16),
         jax.ShapeDtypeStruct((t_len, SSM_W), F32), jax.ShapeDtypeStruct((t_len, rest), BF16),
         bf_d, bf_d, bf_d, bf_s, bf_s, bf_s, bf_s, bf_s,
         jax.ShapeDtypeStruct((SUBLANE, SSM_W), F32), jax.ShapeDtypeStruct((CHUNK, SGU_W), F32),
         jax.ShapeDtypeStruct((SGU_G, CHUNK, CHUNK), F32)],
        [dx2, p, y0, z, mixed, ya, yb, w_out, w_pa, w_pb, w_glu, cre, cim, ws_st, wst_st, d_skip,
         g_sgu], jobs)


def _scan_bwd(dsr, dsi, str_, sti, tab_rev, jobs=()):
    t_len = dsr.shape[0]
    nblk = t_len // SUBLANE
    lb = SCAN_LANES

    def body(dr_ref, di_ref, sr_ref, si_ref, tab_ref, lr_ref, li_ref, dar_ref, dai_ref):
        tab_v = [tab_ref[q] for q in range(8)]
        row0 = lax.broadcasted_iota(jnp.int32, (SUBLANE, lb), 0) == 0
        tile = BF16_TILE

        def step(k, carry):
            cr, ci, acr, aci = carry
            base = pl.multiple_of((nblk - (k + 1) * SCAN_UNROLL) * SUBLANE, SCAN_UNROLL * SUBLANE)
            state = _load_blocks(sr_ref, si_ref, base)
            before = pl.ds(pl.multiple_of(jnp.maximum(base - tile, 0), tile), tile)
            has_before = jnp.where(base > 0, 1.0, 0.0)
            prev = (sr_ref[before, :].astype(F32)[tile - 1:tile] * has_before,
                    si_ref[before, :].astype(F32)[tile - 1:tile] * has_before)
            local = [_scan_local(xr, xi, tab_v, (7, 6, 4))
                     for xr, xi in _load_blocks(dr_ref, di_ref, base)]
            lam = [None] * SCAN_UNROLL
            for b in reversed(range(SCAN_UNROLL)):
                xr, xi = _scan_carry(*local[b], tab_v, cr, ci)
                lam[b] = (xr, xi)
                cr, ci = xr[0:1, :], xi[0:1, :]
                pr, pi = prev if b == 0 else (state[b - 1][0][SUBLANE - 1:], state[b - 1][1][SUBLANE - 1:])
                s_r = jnp.where(row0, pr, pltpu.roll(state[b][0], 1, 0))
                s_i = jnp.where(row0, pi, pltpu.roll(state[b][1], 1, 0))
                acr = acr + xr * s_r + xi * s_i
                aci = aci + xi * s_r - xr * s_i
            _store_blocks(lr_ref, li_ref, base, lam)
            return cr, ci, acr, aci

        zero = jnp.zeros((1, lb), F32)
        zacc = jnp.zeros((SUBLANE, lb), F32)
        _, _, acr, aci = lax.fori_loop(0, nblk // SCAN_UNROLL, step, (zero, zero, zacc, zacc))
        dar_ref[...] = acr
        dai_ref[...] = aci

    col = pl.BlockSpec((t_len, lb), lambda j: (0, j))
    small = pl.BlockSpec((SUBLANE, lb), lambda j: (0, j))
    return _call(
        body, "scan_bwd", (N_STATE // lb,),
        [col, col, col, col, pl.BlockSpec((8, SUBLANE, lb), lambda j: (0, 0, j))],
        [col, col, small, small],
        [jax.ShapeDtypeStruct((t_len, N_STATE), BF16)] * 2
        + [jax.ShapeDtypeStruct((SUBLANE, N_STATE), F32)] * 2,
        [dsr, dsi, str_, sti, tab_rev], jobs)


def _bwd_in(lam_r, lam_i, du_part, drest, x, dx2, g_mix, w_in, bre, bim, jobs=()):
    t_len = x.shape[0]
    cs = IN_COLS // N_CHIP

    def body(lr_ref, li_ref, du_ref, dr_ref, x_ref, dx2_ref, g_ref, w_ref, bre_ref, bim_ref,
             gx_ref, dp_ref, sm_ref):
        i = pl.program_id(0)
        du = du_ref[...] + jnp.concatenate(
            [_dot_nt(lr_ref[:, i * DIAG_N:(i + 1) * DIAG_N],
                     bre_ref[i * LANE:(i + 1) * LANE, i * DIAG_N:(i + 1) * DIAG_N])
             + _dot_nt(li_ref[:, i * DIAG_N:(i + 1) * DIAG_N],
                       bim_ref[i * LANE:(i + 1) * LANE, i * DIAG_N:(i + 1) * DIAG_N])
             for i in range(SSM_W // LANE)], axis=1)
        dp_ref[:, 0:SSM_W] = du.astype(BF16)
        dp_ref[:, SSM_W:] = dr_ref[...]
        dh = jnp.zeros(x_ref.shape, F32)
        for k in range(N_CHIP):
            dh = dh + _dot_nt(dp_ref[:, k * cs:(k + 1) * cs], w_ref[k])
        r, xh = _rms_stats(x_ref[...])
        gx_ref[...] = dx2_ref[...] + _rms_bwd(dh * g_ref[...], xh, r)
        upd = jnp.concatenate([jnp.sum(dh * xh, axis=0, keepdims=True),
                               jnp.zeros((SUBLANE - 1, D_MODEL), F32)], axis=0)

        @pl.when(i == 0)
        def _():
            sm_ref[...] = upd

        @pl.when(i > 0)
        def _():
            sm_ref[...] += upd

    tb = min(TB_WIDE, t_len)
    return _call(
        body, "bwd_in", (t_len // tb,),
        [_rows(tb, N_STATE), _rows(tb, N_STATE), _rows(tb, SSM_W), _rows(tb, IN_COLS - SSM_W),
         _rows(tb, D_MODEL), _rows(tb, D_MODEL)] + [_whole()] * 4,
        [_rows(tb, D_MODEL), _rows(tb, IN_COLS), _acc(SUBLANE, D_MODEL)],
        [jax.ShapeDtypeStruct((t_len, D_MODEL), F32), jax.ShapeDtypeStruct((t_len, IN_COLS), BF16),
         jax.ShapeDtypeStruct((SUBLANE, D_MODEL), F32)],
        [lam_r, lam_i, du_part, drest, x, dx2, g_mix, w_in, bre, bim], jobs)


def _matmul_tn(a, b, name, out_shape, grid_ij, a_blk, a_map, b_blk, b_map, o_blk, o_map, jobs=()):
    tk = a_blk[0]
    nk = a.shape[0] // tk
    assert nk * tk == a.shape[0] and nk > 0

    def body(a_ref, b_ref, o_ref, acc_ref):
        k = pl.program_id(2)

        @pl.when(k == 0)
        def _():
            acc_ref[...] = jnp.zeros_like(acc_ref)

        acc_ref[...] += lax.dot_general(a_ref[...].astype(BF16), b_ref[...].astype(BF16),
                                        (((0,), (0,)), ((), ())), preferred_element_type=F32)

        @pl.when(k == nk - 1)
        def _():
            o_ref[...] = acc_ref[...]

    outs, per_job = _call(
        body, name, (grid_ij[0], grid_ij[1], nk),
        [pl.BlockSpec(a_blk, a_map), pl.BlockSpec(b_blk, b_map)], [pl.BlockSpec(o_blk, o_map)],
        [jax.ShapeDtypeStruct(out_shape, F32)], [a, b], jobs,
        scratch=[pltpu.VMEM((a_blk[1], b_blk[1]), F32)])
    return outs[0], per_job


def _dw_rows(a, b, name, tm, tk):
    m, n = a.shape[1], b.shape[1]
    tk = min(tk, a.shape[0])
    return _matmul_tn(a, b, name, (m, n), (m // tm, 1),
                      (tk, tm), lambda i, j, k: (k, i), (tk, n), lambda i, j, k: (k, 0),
                      (tm, n), lambda i, j, k: (i, 0))[0]


def _dw_cols(a, b, name, tn, sharded, jobs=()):
    t_len, m = a.shape
    n = b.shape[1]

    def body(a_ref, b_ref, o_ref):
        o_ref[...] = lax.dot_general(a_ref[...].astype(BF16), b_ref[...].astype(BF16),
                                     (((0,), (0,)), ((), ())), preferred_element_type=F32)

    if sharded:
        o_spec, o_shape = pl.BlockSpec((None, m, tn), lambda j: (j, 0, 0)), (n // tn, m, tn)
    else:
        o_spec, o_shape = pl.BlockSpec((m, tn), lambda j: (0, j)), (m, n)
    outs, per_job = _call(body, name, (n // tn,),
                          [_whole(), pl.BlockSpec((t_len, tn), lambda j: (0, j))], [o_spec],
                          [jax.ShapeDtypeStruct(o_shape, F32)], [a, b], jobs)
    return outs[0], per_job


def _dw_tiles(a, b, name, tm, tn, jobs=()):
    t_len, m = a.shape
    n = b.shape[1]

    def body(a_ref, b_ref, o_ref):
        o_ref[...] = lax.dot_general(a_ref[...].astype(BF16), b_ref[...].astype(BF16),
                                     (((0,), (0,)), ((), ())), preferred_element_type=F32)

    outs, per_job = _call(body, name, (n // tn, m // tm),
                          [pl.BlockSpec((t_len, tm), lambda j, i: (0, i)),
                           pl.BlockSpec((t_len, tn), lambda j, i: (0, j))],
                          [pl.BlockSpec((None, tm, tn), lambda j, i: (j, i, 0))],
                          [jax.ShapeDtypeStruct((n // tn, m, tn), F32)], [a, b], jobs)
    return outs[0], per_job


def _dw_pair(a, m, b1, b2, name, jobs=()):
    t_len = a.shape[0]
    n_slab = DIAG_N // LANE
    rows_per_slab = LANE // n_slab

    def body(a_ref, b1_ref, b2_ref, o1_ref, o2_ref):
        for b_ref, o_ref in ((b1_ref, o1_ref), (b2_ref, o2_ref)):
            prod = lax.dot_general(a_ref[...].astype(BF16), b_ref[...].astype(BF16),
                                   (((0,), (0,)), ((), ())), preferred_element_type=F32)
            for j in range(n_slab):
                rows = slice(j * rows_per_slab, (j + 1) * rows_per_slab)
                o_ref[rows, :] = prod[rows, j * LANE:(j + 1) * LANE]

    tok = pl.BlockSpec((t_len, DIAG_N), lambda i: (0, i))
    out = pl.BlockSpec((LANE, LANE), lambda i: (i, 0))
    return _call(body, name, (m // LANE,),
                 [pl.BlockSpec((t_len, LANE), lambda i: (0, i)), tok, tok], [out, out],
                 [jax.ShapeDtypeStruct((m, LANE), F32)] * 2, [a, b1, b2], jobs)


def _prefetch_call(body, name, grid, scalars, in_specs, out_specs, out_shape, args):
    return pl.pallas_call(
        body, name=name,
        grid_spec=pltpu.PrefetchScalarGridSpec(num_scalar_prefetch=1, grid=grid, in_specs=in_specs,
                                               out_specs=out_specs),
        out_shape=out_shape, compiler_params=_params(len(grid)),
    )(scalars, *args)


def _place_shard(w, where, name, dtype, tr):
    rows, cols = w.shape

    def body(s_ref, w_ref, o_ref):
        o_ref[...] = w_ref[...].astype(dtype)

    return _prefetch_call(
        body, name, (rows // tr,), where,
        [pl.BlockSpec((tr, cols), lambda i, s: (i, 0))],
        pl.BlockSpec((None, tr, cols), lambda i, s: (s[0], i, 0)),
        jax.ShapeDtypeStruct((N_CHIP, rows, cols), dtype), [w])


def _place_shards(ws, where, name, dtype):
    n = len(ws)

    def body(s_ref, *refs):
        for t in range(n):
            refs[n + t][...] = refs[t][...].astype(dtype)

    return _prefetch_call(
        body, name, (1,), where,
        [pl.BlockSpec(w.shape, lambda i, s: (0, 0)) for w in ws],
        [pl.BlockSpec((None,) + w.shape, lambda i, s: (s[0], 0, 0)) for w in ws],
        [jax.ShapeDtypeStruct((N_CHIP,) + w.shape, dtype) for w in ws], ws)


def _add_sibling(gs, gots, where, name):
    n = len(gs)
    halves = [(g.shape[1] // 2, g.shape[2]) for g in gs]

    def body(s_ref, *refs):
        for t in range(n):
            refs[2 * n + t][...] = (refs[t][...] + refs[n + t][...]).astype(BF16)

    return _prefetch_call(
        body, name, (N_CHIP,), where,
        [pl.BlockSpec((None, hr, cs), lambda k, s: (k, s[1], 0)) for hr, cs in halves]
        + [pl.BlockSpec((None, hr, cs), lambda k, s: (k, 0, 0)) for hr, cs in halves],
        [pl.BlockSpec((None, hr, cs), lambda k, s: (k, 0, 0)) for hr, cs in halves],
        [jax.ShapeDtypeStruct((N_CHIP, hr, cs), BF16) for hr, cs in halves], list(gs) + list(gots))


def _add_chips(sums, gots, where, name):
    n = len(sums)
    halves = [s.shape[1:] for s in sums]

    def body(s_ref, *refs):
        for t in range(n):
            own_ref, got_ref = refs[t], refs[n + t]
            refs[2 * n + t][...] = ((own_ref[...].astype(F32) + got_ref[0].astype(F32))
                                    + got_ref[1].astype(F32)) + got_ref[2].astype(F32)

    return _prefetch_call(
        body, name, (1,), where,
        [pl.BlockSpec((None, hr, cs), lambda i, s: (s[0], 0, 0)) for hr, cs in halves]
        + [pl.BlockSpec((3, hr, cs), lambda i, s: (0, 0, 0)) for hr, cs in halves],
        [pl.BlockSpec((hr, cs), lambda i, s: (s[1], 0)) for hr, cs in halves],
        [jax.ShapeDtypeStruct((2 * hr, cs), F32) for hr, cs in halves], list(sums) + list(gots))


def _small_allreduce(pack):
    rows = pack.shape[0]
    half = rows // 2

    def body(in_ref, out_ref, sib_ref, slots_ref, s_a, r_a, s_b, r_b, s_c, r_c):
        x, y, c, chips = _place()
        k_me = 2 * x + y
        sib = (x, y, 1 - c)
        mine, theirs = _half(rows, c), _half(rows, 1 - c)
        first = _remote(in_ref.at[theirs, :], sib_ref.at[theirs, :], s_a, r_a, sib)
        first.start()
        first.wait_send()
        landed = sib_ref.at[mine, :]
        _remote(landed, landed, s_a, r_a, sib).wait_recv()
        slots_ref[k_me] = in_ref[mine, :] + sib_ref[mine, :]
        cps = [_remote(slots_ref.at[k_me], slots_ref.at[k_me], s_b.at[j], r_b.at[j], (*ch, c))
               for j, ch in enumerate(chips)]
        for cp in cps:
            cp.start()
        for j, ch in enumerate(chips):
            slot = slots_ref.at[_chip_index(ch)]
            _remote(slot, slot, s_b.at[j], r_b.at[j], (*ch, c)).wait_recv()
        for cp in cps:
            cp.wait_send()
        out_ref[mine, :] = ((slots_ref[0] + slots_ref[1]) + slots_ref[2]) + slots_ref[3]
        last = _remote(out_ref.at[mine, :], out_ref.at[mine, :], s_c, r_c, sib)
        last.start()
        other = out_ref.at[theirs, :]
        _remote(other, other, s_c, r_c, sib).wait_recv()
        last.wait_send()

    return pl.pallas_call(
        body, name="small_allreduce", in_specs=[_whole()], out_specs=_whole(),
        out_shape=jax.ShapeDtypeStruct(pack.shape, F32),
        scratch_shapes=[pltpu.VMEM(pack.shape, F32), pltpu.VMEM((N_CHIP, half, LANE), F32),
                        pltpu.SemaphoreType.DMA, pltpu.SemaphoreType.DMA,
                        pltpu.SemaphoreType.DMA((3,)), pltpu.SemaphoreType.DMA((3,)),
                        pltpu.SemaphoreType.DMA, pltpu.SemaphoreType.DMA],
        compiler_params=_params(0),
    )(pack)


def _adamw_update(w_ref, g_ref, m_ref, v_ref, d_ref, mo_ref, vo_ref):
    gv = g_ref[...]
    mn = ADAM_B1 * m_ref[...] + (1.0 - ADAM_B1) * gv
    vn = ADAM_B2 * v_ref[...] + (1.0 - ADAM_B2) * (gv * gv)
    mo_ref[...] = mn
    vo_ref[...] = vn
    m_hat = mn / (1.0 - ADAM_B1 ** ADAM_STEP)
    v_hat = vn / (1.0 - ADAM_B2 ** ADAM_STEP)
    d_ref[...] = -ADAM_LR * (m_hat / (jnp.sqrt(v_hat) + ADAM_EPS) + ADAM_WD * w_ref[...])


def _adamw(w, g, m, v, name, tr):
    rows, cols = w.shape
    blk = _rows(tr, cols)

    def body(w_ref, g_ref, m_ref, v_ref, go_ref, d_ref, mo_ref, vo_ref):
        go_ref[...] = g_ref[...]
        _adamw_update(w_ref, g_ref, m_ref, v_ref, d_ref, mo_ref, vo_ref)

    return _call(body, name, (rows // tr,), [blk] * 4, [blk] * 4,
                 [jax.ShapeDtypeStruct(w.shape, F32)] * 4, [w, g, m, v])[0]


def _adamw_many(ws, gs, ms, vs, name):
    n = len(ws)

    def body(*refs):
        for t in range(n):
            _adamw_update(*[refs[q * n + t] for q in range(7)])

    specs = [pl.BlockSpec(a.shape, lambda i, nd=a.ndim: (0,) * nd) for a in ws]
    outs = pl.pallas_call(
        body, name=name, grid=(1,), in_specs=specs * 4, out_specs=specs * 3,
        out_shape=[jax.ShapeDtypeStruct(a.shape, F32) for _ in range(3) for a in ws],
        compiler_params=_params(1),
    )(*ws, *gs, *ms, *vs)
    return outs[:n], outs[n:2 * n], outs[2 * n:]


def _ssm_discretize(a_re, a_im, log_dt, b_re, b_im):
    dt = jnp.exp(log_dt)[:, None]
    mag = jnp.exp(dt * a_re)
    abr = mag * jnp.cos(dt * a_im)
    abi = mag * jnp.sin(dt * a_im)
    den = a_re * a_re + a_im * a_im
    nr = abr - 1.0
    ni = abi
    f_re = (nr * a_re + ni * a_im) / den
    f_im = (ni * a_re - nr * a_im) / den
    bbr = f_re[..., None] * b_re - f_im[..., None] * b_im
    bbi = f_re[..., None] * b_im + f_im[..., None] * b_re
    return abr, abi, bbr, bbi


def _scan_tables(abr, abi):
    ar = abr.reshape(1, N_STATE)
    ai = abi.reshape(1, N_STATE)
    pr, pi = [ar], [ai]
    for _ in range(SUBLANE - 1):
        pr, pi = pr + [pr[-1] * ar - pi[-1] * ai], pi + [pr[-1] * ai + pi[-1] * ar]
    row = jnp.arange(SUBLANE)[:, None]
    tabs = []
    for d in (1, 2, 4):
        tabs.append(jnp.where(row >= d, pr[d - 1], 0.0))
        tabs.append(jnp.where(row >= d, pi[d - 1], 0.0))
    tabs.append(jnp.concatenate(pr, axis=0))
    tabs.append(jnp.concatenate(pi, axis=0))
    fwd = jnp.stack(tabs)
    sign = jnp.array([1.0, -1.0] * 4, F32)[:, None, None]
    return fwd, fwd[:, ::-1, :] * sign


def _block_diag_b(bb):
    strip = bb.transpose(2, 0, 1).reshape(SSM_H, N_STATE)
    rows = lax.broadcasted_iota(jnp.int32, (SSM_W, N_STATE), 0) // SSM_H
    cols = lax.broadcasted_iota(jnp.int32, (SSM_W, N_STATE), 1) // SSM_P
    return jnp.where(rows == cols, jnp.tile(strip, (SSM_G, 1)), 0.0).astype(BF16)


def _block_diag_c(cc):
    strip = cc.transpose(0, 2, 1).reshape(N_STATE, SSM_H)
    rows = lax.broadcasted_iota(jnp.int32, (N_STATE, SSM_W), 0) // SSM_P
    cols = lax.broadcasted_iota(jnp.int32, (N_STATE, SSM_W), 1) // SSM_H
    return jnp.where(rows == cols, jnp.tile(strip, (1, SSM_G)), 0.0).astype(BF16)


SMALL_SHAPES = {
    "g_mix": (D_MODEL,), "a_re": (SSM_G, SSM_P), "a_im": (SSM_G, SSM_P), "log_dt": (SSM_G,),
    "b_re": (SSM_G, SSM_P, SSM_H), "b_im": (SSM_G, SSM_P, SSM_H),
    "c_re": (SSM_G, SSM_H, SSM_P), "c_im": (SSM_G, SSM_H, SSM_P),
    "d_skip": (SSM_W,), "b_glu": (SSM_W,), "g_sgu": (SGU_W,), "w_s": (SGU_G, CHUNK, CHUNK),
    "b_s": (SGU_G, CHUNK), "g_ffn": (D_MODEL,), "conv_b": (2 * D_FF,), "g_final": (D_MODEL,),
}
PACK_ITEMS = [("loss", (1,))] + [(n, SMALL_SHAPES[n]) for n in SMALL] + [("conv_w", (3, 2 * D_FF))]
TILE = SUBLANE * LANE


def _item_rows(shape):
    return -(-math.prod(shape) // TILE) * SUBLANE


PACK_ROWS = -(-sum(_item_rows(s) for _, s in PACK_ITEMS) // (2 * SUBLANE)) * (2 * SUBLANE)


def _pack(values):
    parts, used = [], 0
    for name, shape in PACK_ITEMS:
        size, rows = math.prod(shape), _item_rows(shape)
        if name in values:
            flat = values[name].astype(F32).reshape(size)
            if rows * LANE > size:
                flat = jnp.pad(flat, (0, rows * LANE - size))
            parts.append(flat.reshape(rows, LANE))
        else:
            parts.append(jnp.zeros((rows, LANE), F32))
        used += rows
    if PACK_ROWS > used:
        parts.append(jnp.zeros((PACK_ROWS - used, LANE), F32))
    return jnp.concatenate(parts, axis=0)


def _unpack(pack):
    out, off = {}, 0
    for name, shape in PACK_ITEMS:
        rows = _item_rows(shape)
        out[name] = pack[off:off + rows].reshape(rows * LANE)[:math.prod(shape)].reshape(shape)
        off += rows
    return out


PLACE_ROWS = {"w_in": 256, "w_up": 256, "w_down": 352, "w_out": 256, "w_proj_a": 256,
              "w_proj_b": 256, "w_glu": 128}


def kernel(x, g_mix, w_in, a_re, a_im, log_dt, b_re, b_im, c_re, c_im, d_skip, w_glu, b_glu, w_proj_a, g_sgu, w_s, b_s, w_proj_b, w_out, g_ffn, w_up, conv_w, conv_b, w_down, g_final, loss_target, m_g_mix, m_w_in, m_a_re, m_a_im, m_log_dt, m_b_re, m_b_im, m_c_re, m_c_im, m_d_skip, m_w_glu, m_b_glu, m_w_proj_a, m_g_sgu, m_w_s, m_b_s, m_w_proj_b, m_w_out, m_g_ffn, m_w_up, m_conv_w, m_conv_b, m_w_down, m_g_final, v_g_mix, v_w_in, v_a_re, v_a_im, v_log_dt, v_b_re, v_b_im, v_c_re, v_c_im, v_d_skip, v_w_glu, v_b_glu, v_w_proj_a, v_g_sgu, v_w_s, v_b_s, v_w_proj_b, v_w_out, v_g_ffn, v_w_up, v_conv_w, v_conv_b, v_w_down, v_g_final):
    given = dict(locals())
    w = {n: given[n] for n in WEIGHTS}
    m = {n: given["m_" + n] for n in WEIGHTS}
    v = {n: given["v_" + n] for n in WEIGHTS}

    def shard2d(a):
        return a.reshape(a.shape[-2], a.shape[-1])

    chip = 2 * lax.axis_index("x") + lax.axis_index("y")
    where = jnp.stack([chip, lax.axis_index("c")]).astype(jnp.int32)
    xs, target = x[0], loss_target[0]
    small = {n: w[n].reshape(SMALL_SHAPES[n]) for n in SMALL}

    (abr, abi, bbr, bbi), disc_vjp = jax.vjp(_ssm_discretize, small["a_re"], small["a_im"],
                                             small["log_dt"], small["b_re"], small["b_im"])
    tab_f, tab_r = _scan_tables(abr, abi)
    bre = _block_diag_b(bbr)
    bim = _block_diag_b(bbi)
    cre = _block_diag_c(small["c_re"])
    cim = _block_diag_c(small["c_im"])
    tril = jnp.tril(jnp.ones((CHUNK, CHUNK), dtype=bool))
    ws = jnp.where(tril[None], small["w_s"], 0.0)
    ws_st = ws.reshape(SGU_G // 2, 2 * CHUNK, CHUNK).astype(BF16)
    wst_st = ws.transpose(0, 2, 1).reshape(SGU_G // 2, 2 * CHUNK, CHUNK).astype(BF16)
    bmat = jnp.repeat(small["b_s"].T, SGU_D, axis=1)
    g_mix2 = small["g_mix"].reshape(1, D_MODEL)
    g_ffn2 = small["g_ffn"].reshape(1, D_MODEL)
    g_final2 = small["g_final"].reshape(1, D_MODEL)
    g_sgu2 = small["g_sgu"].reshape(1, SGU_W)
    d_skip2 = small["d_skip"].reshape(1, SSM_W)
    b_glu2 = small["b_glu"].reshape(1, SSM_W)
    conv_b2 = small["conv_b"].reshape(1, 2 * D_FF)

    gat = {"w_in": _place_shard(shard2d(w["w_in"]), where, "place_w_in", BF16, PLACE_ROWS["w_in"])}
    gat.update(zip(BIG[1:], _place_shards([shard2d(w[n]) for n in BIG[1:]], where, "place_rest", BF16)))
    gat["conv_w"] = _place_shard(shard2d(w["conv_w"]), where, "place_conv_w", F32, 3)
    (gat["w_in"],), = _comm("gather_in", [_job_gather_now(gat["w_in"])])
    mixers = ["w_glu", "w_proj_a", "w_proj_b", "w_out"]
    rows = {n: (0, gat[n].shape[1]) for n in mixers}
    down_a, down_b = (0, D_FF // 8), (D_FF // 8, D_FF // 8)
    up_a, up_b = (0, 3 * D_MODEL // 8), (3 * D_MODEL // 8, 5 * D_MODEL // 8)
    span = (0.0, 1.0)

    names = mixers + ["conv_w", "w_down"]
    (p, h1, bur, bui), (got,) = _fwd_in(
        xs, g_mix2, gat["w_in"], bre, bim,
        [_job_gather([gat[n] for n in names],
                     [(i, rows[n], ICI, span) for i, n in enumerate(mixers)]
                     + [(4, None, ICI, span), (5, down_a, ICI, span)])])
    gat.update(zip(names, got))
    names = mixers + ["w_down", "w_up"]
    (str_, sti), (got,) = _scan_fwd(
        bur, bui, tab_f,
        [_job_gather([gat[n] for n in names],
                     [(i, rows[n], SIBLING, span) for i, n in enumerate(mixers)]
                     + [(4, down_a, SIBLING, span), (4, down_b, ICI, span), (5, up_a, ICI, span)])])
    gat.update(zip(names, got))
    w_glu_f = gat["w_glu"].reshape(SSM_W, SSM_W)
    w_out_f = gat["w_out"].reshape(D_MODEL, D_MODEL)
    conv_w_f = gat["conv_w"].transpose(1, 0, 2).reshape(3, 2 * D_FF)
    (x2, y0, z, mixed, ya, yb), ((gat["w_down"], gat["w_up"]),) = _fwd_mix(
        xs, p, str_, sti, cre, cim, d_skip2, w_glu_f, b_glu2, gat["w_proj_a"], g_sgu2, ws_st, bmat,
        gat["w_proj_b"], w_out_f,
        [_job_gather([gat["w_down"], gat["w_up"]],
                     [(0, down_b, SIBLING, span), (1, up_a, SIBLING, span),
                      (1, up_b, ICI, (0.0, 0.75)), (1, up_b, SIBLING, (0.75, 1.0))])])
    w_down_f = gat["w_down"].reshape(D_FF, D_MODEL)
    up, act, f, h2, dx3, sm_ffn = _fwd_ffn(x2, target, g_ffn2, gat["w_up"], conv_w_f, conv_b2,
                                           w_down_f, g_final2)

    def leg1_done(names, got):
        return _add_sibling([part[n] for n in names], got, where, "add_sibling_" + names[0])

    def leg2_done(names, sums, got):
        return _add_chips(sums, got, where, "add_chips_" + names[0])

    part, red = {}, {}
    part["w_down"] = _dw_rows(f, dx3, "dw_down", D_FF // 2, 4 * TK).reshape(
        N_CHIP, D_FF // N_CHIP, D_MODEL)
    (dx2, dup, sm_conv, sm_gffn), (got,) = _bwd_ffn(
        dx3, up, act, x2, g_ffn2, gat["w_up"], conv_w_f, w_down_f,
        [_job_sibling_halves([part["w_down"]])])
    sum_down = leg1_done(["w_down"], got)
    part["w_up"], (got,) = _dw_tiles(h2, dup, "dw_up", D_MODEL // 2, 2 * D_FF // N_CHIP,
                                     [_job_to_owner(sum_down)])
    red_down = leg2_done(["w_down"], sum_down, got)
    ((dsr, dsi, du_part, drest, mrg, dya, dyb, yap, dz, y1, sgu, dy0, sm_mix, dbm, dws),
     (got, (red["w_down"],))) = _bwd_mix(
        dx2, p, y0, z, mixed, ya, yb, w_out_f, gat["w_proj_a"], gat["w_proj_b"], w_glu_f, cre, cim,
        ws_st, wst_st, d_skip2, g_sgu2,
        [_job_sibling_halves([part["w_up"]]), _job_swap_halves(red_down)])
    sum_up = leg1_done(["w_up"], got)
    (lam_r, lam_i, dar8, dai8), (got,) = _scan_bwd(dsr, dsi, str_, sti, tab_r, [_job_to_owner(sum_up)])
    red_up = leg2_done(["w_up"], sum_up, got)
    mix4 = ["w_out", "w_proj_a", "w_proj_b", "w_glu"]
    part["w_out"] = _dw_cols(mrg, dx2, "dw_out", D_MODEL // 2, False)[0].reshape(
        N_CHIP, D_MODEL // N_CHIP, D_MODEL)
    part["w_proj_a"] = _dw_cols(yap, dya, "dw_proj_a", D_MODEL // N_CHIP, True)[0]
    part["w_proj_b"] = _dw_cols(sgu, dyb, "dw_proj_b", D_MODEL // N_CHIP, True)[0]
    part["w_glu"] = _dw_cols(y1, dz, "dw_glu", SSM_W, False)[0].reshape(
        N_CHIP, SSM_W // N_CHIP, SSM_W)
    got, (red["w_up"],) = _comm(
        "mixer_sibling_halves", [_job_sibling_halves([part[n] for n in mix4]), _job_swap_halves(red_up)])
    (grad_x, dp, sm_gmix), _ = _bwd_in(
        lam_r, lam_i, du_part, drest, xs, dx2, g_mix2, gat["w_in"], bre, bim)
    sums_m = leg1_done(mix4, got)
    part["w_in"], (got,) = _dw_cols(h1, dp, "dw_in", IN_COLS // N_CHIP, True, [_job_to_owner(sums_m)])
    red_m = leg2_done(mix4, sums_m, got)
    (dbd_r, dbd_i), (got, done_m) = _dw_pair(
        p, SSM_W, lam_r, lam_i, "db_bar",
        [_job_sibling_halves([part["w_in"]]), _job_swap_halves(red_m)])
    red.update(zip(mix4, done_m))
    sum_in = leg1_done(["w_in"], got)
    (dcd_r, dcd_i), (got,) = _dw_pair(dy0, SSM_W, str_, sti, "dc", [_job_to_owner(sum_in)])
    red_in = leg2_done(["w_in"], sum_in, got)
    (red["w_in"],), = _comm("swap_w_in", [_job_swap_halves(red_in)])

    def pick_c(slabs):
        two = LANE // SSM_P
        return jnp.einsum("jshsp->jshp", slabs.reshape(SSM_G // two, two, SSM_H, two, SSM_P)
                          ).reshape(SSM_G, SSM_H, SSM_P)

    def pick_b(slabs):
        return pick_c(slabs).transpose(0, 2, 1)

    dabr = jnp.sum(dar8, axis=0).reshape(SSM_G, SSM_P)
    dabi = jnp.sum(dai8, axis=0).reshape(SSM_G, SSM_P)
    d_a_re, d_a_im, d_log_dt, d_b_re, d_b_im = disc_vjp((dabr, dabi, pick_b(dbd_r), pick_b(dbd_i)))
    gsmall = {
        "g_mix": sm_gmix[0], "a_re": d_a_re, "a_im": d_a_im, "log_dt": d_log_dt,
        "b_re": d_b_re, "b_im": d_b_im, "c_re": pick_c(dcd_r), "c_im": -pick_c(dcd_i),
        "d_skip": sm_mix[0], "b_glu": sm_mix[1], "g_sgu": sm_mix[2],
        "w_s": jnp.where(tril[None], dws, 0.0),
        "b_s": dbm.reshape(CHUNK, SGU_G, SGU_D).sum(-1).T,
        "g_ffn": sm_gffn[0], "conv_b": sm_conv[3], "g_final": sm_ffn[0],
        "conv_w": sm_conv[0:3], "loss": sm_ffn[1, 0:1],
    }

    total_pack = _small_allreduce(_pack(gsmall))
    total = _unpack(total_pack)
    grads = dict(red)
    cs = 2 * D_FF // N_CHIP
    grads["conv_w"] = lax.dynamic_slice(total["conv_w"], (0, chip * cs), (3, cs))
    delta, new_m, new_v = {}, {}, {}
    for n in BIG + ("conv_w",):
        grads[n], delta[n], new_m[n], new_v[n] = _adamw(
            shard2d(w[n]), grads[n], shard2d(m[n]), shard2d(v[n]), "adamw_" + n, PLACE_ROWS.get(n, 3))
    for n in SMALL:
        grads[n] = total[n].reshape(w[n].shape)
    ud, um, uv = _adamw_many(*[[d[n] for n in SMALL] for d in (w, grads, m, v)], "adamw_small")
    for i, n in enumerate(SMALL):
        delta[n], new_m[n], new_v[n] = ud[i], um[i], uv[i]

    def like(d):
        return [d[n].reshape(w[n].shape) for n in WEIGHTS]

    return (total["loss"].reshape(()), grad_x.reshape(x.shape), *like(grads), *like(delta),
            *like(new_m), *like(new_v))
```

```python
import math

import jax
import jax.numpy as jnp
from jax import lax
from jax.experimental import pallas as pl
from jax.experimental.pallas import tpu as pltpu

F32 = jnp.float32
BF16 = jnp.bfloat16
MESH = pl.DeviceIdType.MESH

D_MODEL = 1024
SSM_W = 512
SSM_G = 32
SSM_H = 16
SSM_P = 64
N_STATE = SSM_G * SSM_P
DIAG_N = 128 * SSM_P // SSM_H
SGU_W = 512
SGU_G = 8
SGU_D = 64
CHUNK = 128
D_FF = 2816
IN_COLS = 3584
EPS = 1e-6
N_CHIP = 4

ADAM_LR = 0.001
ADAM_B1 = 0.9
ADAM_B2 = 0.999
ADAM_EPS = 1e-08
ADAM_WD = 0.01
ADAM_STEP = 10

SUBLANE = 8
LANE = 128
VMEM_LIMIT = 56 * 1024 * 1024
TB = 256
TB_WIDE = 512
TK = 512
SCAN_LANES = 256
SCAN_UNROLL = 4
HALO = SUBLANE

BIG = ("w_in", "w_up", "w_down", "w_out", "w_proj_a", "w_proj_b", "w_glu")
SMALL = ("g_mix", "a_re", "a_im", "log_dt", "b_re", "b_im", "c_re", "c_im", "d_skip", "b_glu",
         "g_sgu", "w_s", "b_s", "g_ffn", "conv_b", "g_final")
WEIGHTS = ("g_mix", "w_in", "a_re", "a_im", "log_dt", "b_re", "b_im", "c_re", "c_im", "d_skip",
           "w_glu", "b_glu", "w_proj_a", "g_sgu", "w_s", "b_s", "w_proj_b", "w_out", "g_ffn",
           "w_up", "conv_w", "conv_b", "w_down", "g_final")

ANY = pl.BlockSpec(memory_space=pl.ANY)


def _params(n_grid):
    return pltpu.CompilerParams(dimension_semantics=("arbitrary",) * n_grid if n_grid else None,
                                vmem_limit_bytes=VMEM_LIMIT)


def _whole():
    return pl.BlockSpec(memory_space=pltpu.VMEM)


def _rows(tb, ncol):
    return pl.BlockSpec((tb, ncol), lambda i: (i, 0))


def _acc(nrow, ncol):
    return pl.BlockSpec((nrow, ncol), lambda i: (0, 0))


def _dot(a, b):
    return jnp.dot(a.astype(BF16), b.astype(BF16), preferred_element_type=F32)


def _dot_nt(a, b):
    return lax.dot_general(a.astype(BF16), b.astype(BF16), (((1,), (1,)), ((), ())),
                           preferred_element_type=F32)


def _sigmoid(v):
    return 0.5 * jnp.tanh(0.5 * v) + 0.5


_GELU_C = math.sqrt(2.0 / math.pi)


def _gelu(v):
    return 0.5 * v * (1.0 + jnp.tanh(_GELU_C * (v + 0.044715 * v * v * v)))


def _gelu_and_grad(v):
    v2 = v * v
    t = jnp.tanh(_GELU_C * v * (1.0 + 0.044715 * v2))
    half = 0.5 * (1.0 + t)
    return v * half, half + 0.5 * v * (1.0 - t * t) * _GELU_C * (1.0 + 3.0 * 0.044715 * v2)


def _rms_stats(v):
    r = lax.rsqrt(jnp.mean(v * v, axis=-1, keepdims=True) + EPS)
    return r, v * r


def _rms_bwd(dxh, xh, r):
    return r * (dxh - xh * jnp.mean(dxh * xh, axis=-1, keepdims=True))


def _place():
    x, y, c = lax.axis_index("x"), lax.axis_index("y"), lax.axis_index("c")
    chips = [(1 - x, y), (x, 1 - y), (1 - x, 1 - y)]
    return x, y, c, chips


def _chip_index(chip):
    return 2 * chip[0] + chip[1]


def _remote(src, dst, send_sem, recv_sem, device):
    return pltpu.make_async_remote_copy(src_ref=src, dst_ref=dst, send_sem=send_sem,
                                        recv_sem=recv_sem, device_id=device, device_id_type=MESH)


def _half(ref_rows, c):
    hr = ref_rows // 2
    return pl.ds(pl.multiple_of(c * hr, SUBLANE), hr)


class _Job:
    def __init__(self, hooks, n_sem, ins=(), inouts=(), outs=()):
        self.hooks, self.n_sem = list(hooks), n_sem
        self.ins, self.inouts, self.outs = list(ins), list(inouts), list(outs)


def _whole_span(start, finish):
    return [(0.0, "start", start), (1.0, "finish", finish)]


ICI, SIBLING = "ici", "sibling"


def _job_gather(bufs, legs):
    def copies(io, leg, first):
        b, window, kind, _ = legs[leg]
        x, y, c, chips = _place()
        k_me = 2 * x + y
        out = []
        for j, ch in enumerate(chips):
            k = _chip_index(ch)
            if window is None:
                src, land, dev = io[b].at[k_me], io[b].at[k], (*ch, c)
            else:
                r0, rows = window
                mine = pl.ds(pl.multiple_of(r0 + c * (rows // 2), SUBLANE), rows // 2)
                theirs = pl.ds(pl.multiple_of(r0 + (1 - c) * (rows // 2), SUBLANE), rows // 2)
                if kind == ICI:
                    src, land, dev = io[b].at[k_me, mine, :], io[b].at[k, mine, :], (*ch, c)
                else:
                    src, land, dev = io[b].at[k, mine, :], io[b].at[k, theirs, :], (x, y, 1 - c)
            out.append((src, land, first + j, dev))
        return out

    def starter(leg):
        def start(ins, io, outs, ssem, rsem):
            for src, _, i, dev in copies(io, leg, 3 * leg):
                _remote(src, src, ssem(i), rsem(i), dev).start()
        return start

    def finisher(leg):
        def finish(ins, io, outs, ssem, rsem):
            cps = copies(io, leg, 3 * leg)
            for _, land, i, dev in cps:
                _remote(land, land, ssem(i), rsem(i), dev).wait_recv()
            for src, _, i, dev in cps:
                _remote(src, src, ssem(i), rsem(i), dev).wait_send()
        return finish

    hooks = []
    for leg, (_, _, _, (begin, end)) in enumerate(legs):
        hooks += [(begin, "start", starter(leg)), (end, "finish", finisher(leg))]
    return _Job(hooks, 3 * len(legs), inouts=bufs)


def _job_gather_now(buf):
    rows = buf.shape[1]

    def run(ins, io, outs, ssem, rsem):
        x, y, c, chips = _place()
        k_me = 2 * x + y
        sib = (x, y, 1 - c)
        mine, theirs = _half(rows, c), _half(rows, 1 - c)
        own = io[0].at[k_me, mine, :]
        sends = [_remote(own, own, ssem(j), rsem(j), (*ch, c)) for j, ch in enumerate(chips)]
        for cp in sends:
            cp.start()
        passed = []
        for j, ch in enumerate(chips):
            landed = io[0].at[_chip_index(ch), mine, :]
            _remote(landed, landed, ssem(j), rsem(j), sib).wait_recv()
            cp = _remote(landed, landed, ssem(3 + j), rsem(3 + j), sib)
            cp.start()
            passed.append(cp)
        for j, ch in enumerate(chips):
            landed = io[0].at[_chip_index(ch), theirs, :]
            _remote(landed, landed, ssem(3 + j), rsem(3 + j), sib).wait_recv()
        for cp in sends + passed:
            cp.wait_send()

    return _Job([(0.0, "start", run)], 6, inouts=[buf])


def _job_sibling_halves(grads):
    n = len(grads)

    def build(ins, outs, ssem, rsem):
        x, y, c, _ = _place()
        return [_remote(ins[t].at[:, _half(grads[t].shape[1], 1 - c), :], outs[t], ssem(t), rsem(t),
                        (x, y, 1 - c)) for t in range(n)]

    def start(ins, io, outs, ssem, rsem):
        for cp in build(ins, outs, ssem, rsem):
            cp.start()

    def finish(ins, io, outs, ssem, rsem):
        for cp in build(ins, outs, ssem, rsem):
            cp.wait()

    return _Job(_whole_span(start, finish), n, ins=grads,
                outs=[jax.ShapeDtypeStruct((N_CHIP, g.shape[1] // 2, g.shape[2]), F32) for g in grads])


def _job_to_owner(sums):
    n = len(sums)

    def build(ins, outs, ssem, rsem):
        x, y, c, chips = _place()
        return [_remote(ins[t].at[_chip_index(ch)], outs[t].at[j], ssem(3 * t + j), rsem(3 * t + j),
                        (*ch, c)) for t in range(n) for j, ch in enumerate(chips)]

    def start(ins, io, outs, ssem, rsem):
        for cp in build(ins, outs, ssem, rsem):
            cp.start()

    def finish(ins, io, outs, ssem, rsem):
        for cp in build(ins, outs, ssem, rsem):
            cp.wait()

    return _Job(_whole_span(start, finish), 3 * n, ins=sums,
                outs=[jax.ShapeDtypeStruct((3,) + s.shape[1:], s.dtype) for s in sums])


def _job_swap_halves(bufs):
    n = len(bufs)

    def start(ins, io, outs, ssem, rsem):
        x, y, c, _ = _place()
        for t in range(n):
            mine = io[t].at[_half(bufs[t].shape[0], c), :]
            _remote(mine, mine, ssem(t), rsem(t), (x, y, 1 - c)).start()

    def finish(ins, io, outs, ssem, rsem):
        x, y, c, _ = _place()
        for t in range(n):
            theirs = io[t].at[_half(bufs[t].shape[0], 1 - c), :]
            _remote(theirs, theirs, ssem(t), rsem(t), (x, y, 1 - c)).wait_recv()
        for t in range(n):
            mine = io[t].at[_half(bufs[t].shape[0], c), :]
            _remote(mine, mine, ssem(t), rsem(t), (x, y, 1 - c)).wait_send()

    return _Job(_whole_span(start, finish), n, inouts=bufs)


def _call(body, name, grid, in_specs, out_specs, out_shape, args, jobs=(), scratch=()):
    n_in, n_out, n_scr = len(args), len(out_shape), len(scratch)
    job_in = [a for jb in jobs for a in jb.ins + jb.inouts]
    job_out = [s for jb in jobs
               for s in [jax.ShapeDtypeStruct(a.shape, a.dtype) for a in jb.inouts] + jb.outs]
    aliases, pos_in, pos_out = {}, n_in, n_out
    for jb in jobs:
        pos_in += len(jb.ins)
        for _ in jb.inouts:
            aliases[pos_in] = pos_out
            pos_in += 1
            pos_out += 1
        pos_out += len(jb.outs)
    n_sem = sum(jb.n_sem for jb in jobs)

    def wrapped(*refs):
        c_in = refs[:n_in]
        j_in = refs[n_in:n_in + len(job_in)]
        c_out = refs[n_in + len(job_in):n_in + len(job_in) + n_out]
        j_out = refs[n_in + len(job_in) + n_out:n_in + len(job_in) + n_out + len(job_out)]
        rest = refs[n_in + len(job_in) + n_out + len(job_out):]
        c_scr = rest[:n_scr]
        views, pi, po, ps = [], 0, 0, 0
        for jb in jobs:
            ins = j_in[pi:pi + len(jb.ins)]
            pi += len(jb.ins) + len(jb.inouts)
            io = j_out[po:po + len(jb.inouts)]
            new = j_out[po + len(jb.inouts):po + len(jb.inouts) + len(jb.outs)]
            po += len(jb.inouts) + len(jb.outs)
            send = (lambda i, o=ps: rest[n_scr].at[o + i])
            recv = (lambda i, o=ps: rest[n_scr + 1].at[o + i])
            ps += jb.n_sem
            views.append((ins, io, new, send, recv))

        def run(frac):
            for kind in ("finish", "start"):
                for jb, vw in zip(jobs, views):
                    for at, what, fn in jb.hooks:
                        if at == frac and what == kind:
                            fn(*vw)

        fracs = sorted({at for jb in jobs for at, _, _ in jb.hooks})
        if not grid:
            for frac in fracs:
                run(frac)
            return
        if jobs:
            assert len(grid) == 1 or set(fracs) <= {0.0, 1.0}
            first = pl.program_id(0) == 0
            last = pl.program_id(0) == grid[0] - 1
            for d in range(1, len(grid)):
                first = jnp.logical_and(first, pl.program_id(d) == 0)
                last = jnp.logical_and(last, pl.program_id(d) == grid[d] - 1)
            for frac in fracs:
                if frac < 1.0:
                    at_step = first if frac == 0.0 else pl.program_id(0) == int(frac * grid[0])
                    pl.when(at_step)(lambda frac=frac: run(frac))
        body(*c_in, *c_out, *c_scr)
        if jobs and 1.0 in fracs:
            pl.when(last)(lambda: run(1.0))

    sems = [pltpu.SemaphoreType.DMA((n_sem,)), pltpu.SemaphoreType.DMA((n_sem,))] if jobs else []
    kwargs = dict(grid=grid) if grid else {}
    res = pl.pallas_call(
        wrapped, name=name, in_specs=list(in_specs) + [ANY] * len(job_in),
        out_specs=list(out_specs) + [ANY] * len(job_out),
        out_shape=list(out_shape) + job_out, scratch_shapes=list(scratch) + sems,
        input_output_aliases=aliases, compiler_params=_params(len(grid)), **kwargs,
    )(*args, *job_in)
    outs, pos, per_job = list(res[:n_out]), n_out, []
    for jb in jobs:
        k = len(jb.inouts) + len(jb.outs)
        per_job.append(list(res[pos:pos + k]))
        pos += k
    return outs, per_job


def _comm(name, jobs):
    return _call(None, name, (), [], [], [], [], jobs)[1]


def _fwd_in(x, g_mix, w_in, bre, bim, jobs=()):
    t_len = x.shape[0]
    cs = IN_COLS // N_CHIP

    def body(x_ref, g_ref, w_ref, bre_ref, bim_ref, p_ref, h_ref, bur_ref, bui_ref):
        xv = x_ref[...]
        r, xh = _rms_stats(xv)
        h = (xh * g_ref[...]).astype(BF16)
        h_ref[...] = h
        for k in range(N_CHIP):
            p_ref[:, k * cs:(k + 1) * cs] = jnp.dot(h, w_ref[k],
                                                    preferred_element_type=F32).astype(BF16)
        u = p_ref[:, 0:SSM_W]
        for i in range(SSM_W // LANE):
            rows, cols = slice(i * LANE, (i + 1) * LANE), slice(i * DIAG_N, (i + 1) * DIAG_N)
            bur_ref[:, cols] = jnp.dot(u[:, rows], bre_ref[rows, cols],
                                       preferred_element_type=F32).astype(BF16)
            bui_ref[:, cols] = jnp.dot(u[:, rows], bim_ref[rows, cols],
                                       preferred_element_type=F32).astype(BF16)

    tb = min(TB_WIDE, t_len)
    return _call(
        body, "fwd_in", (t_len // tb,),
        [_rows(tb, D_MODEL), _whole(), _whole(), _whole(), _whole()],
        [_rows(tb, IN_COLS), _rows(tb, D_MODEL), _rows(tb, N_STATE), _rows(tb, N_STATE)],
        [jax.ShapeDtypeStruct((t_len, IN_COLS), BF16), jax.ShapeDtypeStruct((t_len, D_MODEL), BF16),
         jax.ShapeDtypeStruct((t_len, N_STATE), BF16), jax.ShapeDtypeStruct((t_len, N_STATE), BF16)],
        [x, g_mix, w_in, bre, bim], jobs)


def _scan_local(xr, xi, tab, shifts):
    for q, s in enumerate(shifts):
        ar, ai = tab[2 * q], tab[2 * q + 1]
        rr = pltpu.roll(xr, s, 0)
        ri = pltpu.roll(xi, s, 0)
        xr, xi = xr + ar * rr - ai * ri, xi + ar * ri + ai * rr
    return xr, xi


def _scan_carry(xr, xi, tab, cr, ci):
    pr, pi = tab[6], tab[7]
    return xr + pr * cr - pi * ci, xi + pr * ci + pi * cr


BF16_TILE = 2 * SUBLANE


def _load_blocks(r_ref, i_ref, base):
    out = []
    for q in range(SCAN_UNROLL // 2):
        rows = pl.ds(pl.multiple_of(base + q * BF16_TILE, BF16_TILE), BF16_TILE)
        vr, vi = r_ref[rows, :].astype(F32), i_ref[rows, :].astype(F32)
        out += [(vr[:SUBLANE], vi[:SUBLANE]), (vr[SUBLANE:], vi[SUBLANE:])]
    return out


def _store_blocks(r_ref, i_ref, base, blocks):
    for q in range(SCAN_UNROLL // 2):
        rows = pl.ds(pl.multiple_of(base + q * BF16_TILE, BF16_TILE), BF16_TILE)
        r_ref[rows, :] = jnp.concatenate([blocks[2 * q][0], blocks[2 * q + 1][0]], 0).astype(r_ref.dtype)
        i_ref[rows, :] = jnp.concatenate([blocks[2 * q][1], blocks[2 * q + 1][1]], 0).astype(i_ref.dtype)


def _scan_fwd(bur, bui, tab, jobs=()):
    t_len = bur.shape[0]
    nblk = t_len // SUBLANE
    lb = SCAN_LANES

    def body(br_ref, bi_ref, tab_ref, sr_ref, si_ref):
        tab_v = [tab_ref[q] for q in range(8)]

        def step(k, carry):
            cr, ci = carry
            base = pl.multiple_of(k * SCAN_UNROLL * SUBLANE, SCAN_UNROLL * SUBLANE)
            local = [_scan_local(xr, xi, tab_v, (1, 2, 4))
                     for xr, xi in _load_blocks(br_ref, bi_ref, base)]
            done = []
            for xr, xi in local:
                xr, xi = _scan_carry(xr, xi, tab_v, cr, ci)
                done.append((xr, xi))
                cr, ci = xr[SUBLANE - 1:SUBLANE, :], xi[SUBLANE - 1:SUBLANE, :]
            _store_blocks(sr_ref, si_ref, base, done)
            return cr, ci

        zero = jnp.zeros((1, lb), F32)
        lax.fori_loop(0, nblk // SCAN_UNROLL, step, (zero, zero))

    col = pl.BlockSpec((t_len, lb), lambda j: (0, j))
    return _call(
        body, "scan_fwd", (N_STATE // lb,),
        [col, col, pl.BlockSpec((8, SUBLANE, lb), lambda j: (0, 0, j))], [col, col],
        [jax.ShapeDtypeStruct((t_len, N_STATE), BF16)] * 2, [bur, bui, tab], jobs)


def _sgu_mix(v, ws_ref, lane_lo):
    rows = []
    for c0 in range(0, v.shape[0], CHUNK):
        slabs = []
        for j in range(SGU_W // LANE):
            prod = jnp.dot(ws_ref[j], v[c0:c0 + CHUNK, j * LANE:(j + 1) * LANE].astype(BF16),
                           preferred_element_type=F32)
            slabs.append(jnp.where(lane_lo, prod[:CHUNK], prod[CHUNK:]))
        rows.append(jnp.concatenate(slabs, axis=1))
    return jnp.concatenate(rows, axis=0) if len(rows) > 1 else rows[0]


def _fwd_mix(x, p, str_, sti, cre, cim, d_skip, w_glu, b_glu, w_pa, g_sgu, ws_st, bmat, w_pb, w_out,
             jobs=()):
    t_len = x.shape[0]

    def body(x_ref, p_ref, sr_ref, si_ref, cre_ref, cim_ref, dsk_ref, wg_ref, bg_ref, wpa_ref,
             gs_ref, ws_ref, bm_ref, wpb_ref, wo_ref,
             x2_ref, y0_ref, z_ref, mx_ref, ya_ref, yb_ref):
        u = p_ref[:, 0:SSM_W].astype(F32)
        y0 = jnp.concatenate(
            [_dot(sr_ref[:, i * DIAG_N:(i + 1) * DIAG_N],
                  cre_ref[i * DIAG_N:(i + 1) * DIAG_N, i * LANE:(i + 1) * LANE])
             - _dot(si_ref[:, i * DIAG_N:(i + 1) * DIAG_N],
                    cim_ref[i * DIAG_N:(i + 1) * DIAG_N, i * LANE:(i + 1) * LANE])
             for i in range(SSM_W // LANE)], axis=1) + dsk_ref[...] * u
        y0_ref[...] = y0.astype(BF16)
        y1 = _gelu(y0)
        z = _dot(y1, wg_ref[...]) + bg_ref[...]
        z_ref[...] = z.astype(BF16)
        ya_pre = (y1 * _sigmoid(z)).astype(BF16)
        ya = jnp.concatenate([jnp.dot(ya_pre, wpa_ref[k], preferred_element_type=F32)
                              for k in range(N_CHIP)], axis=1)
        ya_ref[...] = ya.astype(BF16)

        uvg = _gelu(p_ref[:, SSM_W:SSM_W + 2 * SGU_W].astype(F32))
        u2 = uvg[:, :SGU_W]
        _, vh = _rms_stats(uvg[:, SGU_W:])
        v3 = vh * gs_ref[...]
        lane_lo = lax.broadcasted_iota(jnp.int32, (CHUNK, LANE), 1) < SGU_D
        bias = jnp.concatenate([bm_ref[...]] * (TB // CHUNK), axis=0)
        mixed = _sgu_mix(v3, ws_ref, lane_lo) + bias
        mx_ref[...] = mixed.astype(BF16)
        sgu = (u2 * mixed).astype(BF16)
        yb = jnp.concatenate([jnp.dot(sgu, wpb_ref[k], preferred_element_type=F32)
                              for k in range(N_CHIP)], axis=1)
        yb_ref[...] = yb.astype(BF16)

        lg0 = SSM_W + 2 * SGU_W
        ga = _sigmoid(p_ref[:, lg0:lg0 + D_MODEL].astype(F32))
        gb = _sigmoid(p_ref[:, lg0 + D_MODEL:lg0 + 2 * D_MODEL].astype(F32))
        mrg = ga * ya + gb * yb
        x2_ref[...] = x_ref[...] + _dot(mrg, wo_ref[...])

    return _call(
        body, "fwd_mix", (t_len // TB,),
        [_rows(TB, D_MODEL), _rows(TB, IN_COLS), _rows(TB, N_STATE), _rows(TB, N_STATE)]
        + [_whole()] * 11,
        [_rows(TB, D_MODEL), _rows(TB, SSM_W), _rows(TB, SSM_W), _rows(TB, SGU_W),
         _rows(TB, D_MODEL), _rows(TB, D_MODEL)],
        [jax.ShapeDtypeStruct((t_len, D_MODEL), F32), jax.ShapeDtypeStruct((t_len, SSM_W), BF16),
         jax.ShapeDtypeStruct((t_len, SSM_W), BF16), jax.ShapeDtypeStruct((t_len, SGU_W), BF16),
         jax.ShapeDtypeStruct((t_len, D_MODEL), BF16), jax.ShapeDtypeStruct((t_len, D_MODEL), BF16)],
        [x, p, str_, sti, cre, cim, d_skip, w_glu, b_glu, w_pa, g_sgu, ws_st, bmat, w_pb, w_out], jobs)


def _conv_taps(v, cw_ref, c0, width):
    w0 = cw_ref[0:1, c0:c0 + width]
    w1 = cw_ref[1:2, c0:c0 + width]
    w2 = cw_ref[2:3, c0:c0 + width]
    return w0 * pltpu.roll(v, 2, 0) + w1 * pltpu.roll(v, 1, 0) + w2 * v


def _fwd_ffn(x2, target, g_ffn, w_up, conv_w, conv_b, w_down, g_final):
    t_len = x2.shape[0]
    half = D_FF // 2
    blocks_per_halo = TB // HALO

    def body(x2_ref, xp_ref, tg_ref, gf_ref, wu_ref, cw_ref, cb_ref, wd_ref, gl_ref,
             up_ref, act_ref, f_ref, h2_ref, dx3_ref, sm_ref):
        i = pl.program_id(0)
        xe = jnp.concatenate([xp_ref[...] * jnp.where(i == 0, 0.0, 1.0), x2_ref[...]], axis=0)
        _, xh = _rms_stats(xe)
        h2 = (xh * gf_ref[...]).astype(BF16)
        h2_ref[...] = h2[HALO:]
        acc = jnp.zeros((TB, D_MODEL), F32)
        ups = [jnp.dot(h2, wu_ref[k], preferred_element_type=F32) for k in range(N_CHIP)]
        for hc in range(2):
            ca = hc * half
            cb = D_FF + hc * half
            ua, ub = ups[hc], ups[2 + hc]
            up_ref[:, ca:ca + half] = ua[HALO:].astype(BF16)
            up_ref[:, cb:cb + half] = ub[HALO:].astype(BF16)
            ac = _conv_taps(ua, cw_ref, ca, half)[HALO:] + cb_ref[:, ca:ca + half]
            bc = _conv_taps(ub, cw_ref, cb, half)[HALO:] + cb_ref[:, cb:cb + half]
            act_ref[:, ca:ca + half] = ac.astype(BF16)
            act_ref[:, cb:cb + half] = bc.astype(BF16)
            f = (ac * _sigmoid(ac) * bc).astype(BF16)
            f_ref[:, ca:ca + half] = f
            acc = acc + jnp.dot(f, wd_ref[ca:ca + half, :], preferred_element_type=F32)
        x3 = x2_ref[...] + acc
        r3, xh3 = _rms_stats(x3)
        err = xh3 * gl_ref[...] - tg_ref[...]
        dout = err * (1.0 / D_MODEL)
        dx3_ref[...] = _rms_bwd(dout * gl_ref[...], xh3, r3)
        dgl = jnp.sum(dout * xh3, axis=0, keepdims=True)
        loss = 0.5 * jnp.sum(jnp.mean(err * err, axis=-1, keepdims=True), axis=0, keepdims=True)
        upd = jnp.concatenate([dgl, jnp.broadcast_to(loss, (1, D_MODEL)),
                               jnp.zeros((SUBLANE - 2, D_MODEL), F32)], axis=0)

        @pl.when(i == 0)
        def _():
            sm_ref[...] = upd

        @pl.when(i > 0)
        def _():
            sm_ref[...] += upd

    prev = pl.BlockSpec((HALO, D_MODEL), lambda i: (jnp.maximum(i * blocks_per_halo - 1, 0), 0))
    return _call(
        body, "fwd_ffn", (t_len // TB,),
        [_rows(TB, D_MODEL), prev, _rows(TB, D_MODEL)] + [_whole()] * 6,
        [_rows(TB, 2 * D_FF), _rows(TB, 2 * D_FF), _rows(TB, D_FF), _rows(TB, D_MODEL),
         _rows(TB, D_MODEL), _acc(SUBLANE, D_MODEL)],
        [jax.ShapeDtypeStruct((t_len, 2 * D_FF), BF16), jax.ShapeDtypeStruct((t_len, 2 * D_FF), BF16),
         jax.ShapeDtypeStruct((t_len, D_FF), BF16), jax.ShapeDtypeStruct((t_len, D_MODEL), BF16),
         jax.ShapeDtypeStruct((t_len, D_MODEL), F32), jax.ShapeDtypeStruct((SUBLANE, D_MODEL), F32)],
        [x2, x2, target, g_ffn, w_up, conv_w, conv_b, w_down, g_final])[0]


def _bwd_ffn(dx3, up, act, x2, g_ffn, w_up, conv_w, w_down, jobs=()):
    t_len = x2.shape[0]
    half = D_FF // 2
    nblk = t_len // TB
    halo_b = 2 * HALO
    n_e = TB + HALO

    def body(dx_ref, dxn_ref, up_ref, act_ref, actn_ref, x2_ref, gf_ref, wu_ref, cw_ref,
             wd_ref, dx2_ref, dup_ref, smw_ref, smg_ref):
        i = pl.program_id(0)

        @pl.when(i == 0)
        def _():
            smw_ref[...] = jnp.zeros_like(smw_ref)
            smg_ref[...] = jnp.zeros_like(smg_ref)

        keep_last = jnp.where(i == nblk - 1, 0.0, 1.0)
        dxe = jnp.concatenate([dx_ref[...], dxn_ref[...] * keep_last], axis=0).astype(BF16)
        dh2 = jnp.zeros((TB, D_MODEL), F32)
        zpad = jnp.zeros((1, half), F32)
        dfs = [lax.dot_general(dxe, wd_ref[hc * half:(hc + 1) * half, :], (((1,), (1,)), ((), ())),
                               preferred_element_type=F32) for hc in range(2)]
        for hc in range(2):
            ca = hc * half
            cb = D_FF + hc * half
            ac = jnp.concatenate([act_ref[:, ca:ca + half].astype(F32),
                                  actn_ref[:, ca:ca + half].astype(F32)[:HALO]], axis=0)
            bc = jnp.concatenate([act_ref[:, cb:cb + half].astype(F32),
                                  actn_ref[:, cb:cb + half].astype(F32)[:HALO]], axis=0)
            wa = [cw_ref[k:k + 1, ca:ca + half] for k in range(3)]
            wb = [cw_ref[k:k + 1, cb:cb + half] for k in range(3)]
            df = dfs[hc]
            sg = _sigmoid(ac)
            da = df * bc * sg * (1.0 + ac * (1.0 - sg))
            db = df * ac * sg
            da1, da2 = pltpu.roll(da, n_e - 1, 0), pltpu.roll(da, n_e - 2, 0)
            db1, db2 = pltpu.roll(db, n_e - 1, 0), pltpu.roll(db, n_e - 2, 0)
            dua = (wa[2] * da + wa[1] * da1 + wa[0] * da2)[:TB]
            dub = (wb[2] * db + wb[1] * db1 + wb[0] * db2)[:TB]
            dup_ref[:, ca:ca + half] = dua.astype(BF16)
            dup_ref[:, cb:cb + half] = dub.astype(BF16)
            dh2 = dh2 + _dot_nt(dua, wu_ref[hc]) + _dot_nt(dub, wu_ref[2 + hc])
            rows = []
            for u_, d0, d1, d2 in ((up_ref[:, ca:ca + half].astype(F32), da, da1, da2),
                                   (up_ref[:, cb:cb + half].astype(F32), db, db1, db2)):
                rows.append([jnp.sum(u_ * d2[:TB], axis=0, keepdims=True),
                             jnp.sum(u_ * d1[:TB], axis=0, keepdims=True),
                             jnp.sum(u_ * d0[:TB], axis=0, keepdims=True),
                             jnp.sum(d0[:TB], axis=0, keepdims=True)])
            for c0, rws in ((ca, rows[0]), (cb, rows[1])):
                upd = jnp.concatenate(rws + [zpad] * (SUBLANE - 4), axis=0)
                smw_ref[:, c0:c0 + half] += upd

        r2, xh2 = _rms_stats(x2_ref[...])
        dx2_ref[...] = dx_ref[...] + _rms_bwd(dh2 * gf_ref[...], xh2, r2)
        updg = jnp.concatenate([jnp.sum(dh2 * xh2, axis=0, keepdims=True),
                                jnp.zeros((SUBLANE - 1, D_MODEL), F32)], axis=0)

        smg_ref[...] += updg

    nxt_d = pl.BlockSpec((HALO, D_MODEL),
                         lambda i: (jnp.minimum((i + 1) * (TB // HALO), t_len // HALO - 1), 0))
    nxt_a = pl.BlockSpec((halo_b, 2 * D_FF),
                         lambda i: (jnp.minimum((i + 1) * (TB // halo_b), t_len // halo_b - 1), 0))
    return _call(
        body, "bwd_ffn", (nblk,),
        [_rows(TB, D_MODEL), nxt_d, _rows(TB, 2 * D_FF), _rows(TB, 2 * D_FF), nxt_a,
         _rows(TB, D_MODEL)] + [_whole()] * 4,
        [_rows(TB, D_MODEL), _rows(TB, 2 * D_FF), _acc(SUBLANE, 2 * D_FF), _acc(SUBLANE, D_MODEL)],
        [jax.ShapeDtypeStruct((t_len, D_MODEL), F32), jax.ShapeDtypeStruct((t_len, 2 * D_FF), BF16),
         jax.ShapeDtypeStruct((SUBLANE, 2 * D_FF), F32), jax.ShapeDtypeStruct((SUBLANE, D_MODEL), F32)],
        [dx3, dx3, up, act, act, x2, g_ffn, w_up, conv_w, w_down], jobs)


def _bwd_mix(dx2, p, y0, z, mixed, ya, yb, w_out, w_pa, w_pb, w_glu, cre, cim, ws_st, wst_st,
             d_skip, g_sgu, jobs=()):
    t_len = dx2.shape[0]
    pc = D_MODEL // N_CHIP
    n_slab = SGU_W // LANE

    def body(dx_ref, p_ref, y0_ref, z_ref, mx_ref, ya_ref, yb_ref, wo_ref, wpa_ref, wpb_ref,
             wg_ref, cre_ref, cim_ref, ws_ref, wst_ref, dsk_ref, gs_ref,
             dsr_ref, dsi_ref, du_ref, drest_ref, mrg_ref, dya_ref, dyb_ref, yap_ref, dz_ref,
             y1_ref, sgu_ref, dy0_ref, sm_ref, dbm_ref, dws_ref):
        @pl.when(pl.program_id(0) == 0)
        def _():
            sm_ref[...] = jnp.zeros_like(sm_ref)
            dbm_ref[...] = jnp.zeros_like(dbm_ref)
            dws_ref[...] = jnp.zeros_like(dws_ref)

        lg0 = SSM_W + 2 * SGU_W
        y0v = y0_ref[...].astype(F32)
        y1, y1_grad = _gelu_and_grad(y0v)
        sz = _sigmoid(z_ref[...].astype(F32))
        y1_ref[...] = y1.astype(BF16)
        yap_ref[...] = (y1 * sz).astype(BF16)

        dxb = dx_ref[...].astype(BF16)
        dyap = jnp.zeros((TB, SSM_W), F32)
        dsgu = jnp.zeros((TB, SGU_W), F32)
        wide = D_MODEL // 2
        for h in range(2):
            cols = slice(h * wide, (h + 1) * wide)
            ga = _sigmoid(p_ref[:, lg0 + h * wide:lg0 + (h + 1) * wide].astype(F32))
            gb = _sigmoid(
                p_ref[:, lg0 + D_MODEL + h * wide:lg0 + D_MODEL + (h + 1) * wide].astype(F32))
            yav = ya_ref[:, cols].astype(F32)
            ybv = yb_ref[:, cols].astype(F32)
            mrg_ref[:, cols] = (ga * yav + gb * ybv).astype(BF16)
            dmrg = _dot_nt(dxb, wo_ref[cols, :])
            drest_ref[:, 2 * SGU_W + h * wide:2 * SGU_W + (h + 1) * wide] = (
                dmrg * yav * ga * (1.0 - ga)).astype(BF16)
            drest_ref[:, 2 * SGU_W + D_MODEL + h * wide:2 * SGU_W + D_MODEL + (h + 1) * wide] = (
                dmrg * ybv * gb * (1.0 - gb)).astype(BF16)
            dya = (dmrg * ga).astype(BF16)
            dyb = (dmrg * gb).astype(BF16)
            dya_ref[:, cols] = dya
            dyb_ref[:, cols] = dyb
            for k in range(wide // pc):
                shard = h * (wide // pc) + k
                dyap = dyap + _dot_nt(dya[:, k * pc:(k + 1) * pc], wpa_ref[shard])
                dsgu = dsgu + _dot_nt(dyb[:, k * pc:(k + 1) * pc], wpb_ref[shard])

        dz = dyap * y1 * sz * (1.0 - sz)
        dz_ref[...] = dz.astype(BF16)
        dy0 = (dyap * sz + _dot_nt(dz, wg_ref[...])) * y1_grad
        dy0_ref[...] = dy0.astype(BF16)
        u = p_ref[:, 0:SSM_W].astype(F32)
        du_ref[...] = dy0 * dsk_ref[...]
        for q in range(SSM_W // LANE):
            rows, cols = slice(q * DIAG_N, (q + 1) * DIAG_N), slice(q * LANE, (q + 1) * LANE)
            dsr_ref[:, rows] = _dot_nt(dy0[:, cols], cre_ref[rows, cols]).astype(BF16)
            dsi_ref[:, rows] = (-_dot_nt(dy0[:, cols], cim_ref[rows, cols])).astype(BF16)

        u2, u_grad = _gelu_and_grad(p_ref[:, SSM_W:SSM_W + SGU_W].astype(F32))
        mixed = mx_ref[...].astype(F32)
        sgu_ref[...] = (u2 * mixed).astype(BF16)
        drest_ref[:, 0:SGU_W] = (dsgu * mixed * u_grad).astype(BF16)
        dmix = dsgu * u2
        v2, v_grad = _gelu_and_grad(p_ref[:, SSM_W + SGU_W:lg0].astype(F32))
        rv, vh = _rms_stats(v2)
        v3 = vh * gs_ref[...]
        lane_lo = lax.broadcasted_iota(jnp.int32, (CHUNK, LANE), 1) < SGU_D
        dv3 = _sgu_mix(dmix, wst_ref, lane_lo)
        dbm = jnp.zeros((CHUNK, SGU_W), F32)
        for c0 in range(0, TB, CHUNK):
            dbm = dbm + dmix[c0:c0 + CHUNK]
        for j in range(n_slab):
            lo = jnp.zeros((CHUNK, CHUNK), F32)
            hi = jnp.zeros((CHUNK, CHUNK), F32)
            for c0 in range(0, TB, CHUNK):
                dsl = dmix[c0:c0 + CHUNK, j * LANE:(j + 1) * LANE]
                vsl = v3[c0:c0 + CHUNK, j * LANE:(j + 1) * LANE]
                lo = lo + _dot_nt(jnp.where(lane_lo, dsl, 0.0), vsl)
                hi = hi + _dot_nt(jnp.where(lane_lo, 0.0, dsl), vsl)

            dws_ref[2 * j] += lo
            dws_ref[2 * j + 1] += hi

        dv2 = _rms_bwd(dv3 * gs_ref[...], vh, rv)
        drest_ref[:, SGU_W:2 * SGU_W] = (dv2 * v_grad).astype(BF16)

        upd = jnp.concatenate([jnp.sum(dy0 * u, axis=0, keepdims=True),
                               jnp.sum(dz, axis=0, keepdims=True),
                               jnp.sum(dv3 * vh, axis=0, keepdims=True),
                               jnp.zeros((SUBLANE - 3, SSM_W), F32)], axis=0)

        sm_ref[...] += upd
        dbm_ref[...] += dbm

    rest = 2 * SGU_W + 2 * D_MODEL
    bf_d, bf_s = jax.ShapeDtypeStruct((t_len, D_MODEL), BF16), jax.ShapeDtypeStruct((t_len, SSM_W), BF16)
    return _call(
        body, "bwd_mix", (t_len // TB,),
        [_rows(TB, D_MODEL), _rows(TB, IN_COLS), _rows(TB, SSM_W), _rows(TB, SSM_W),
         _rows(TB, SGU_W), _rows(TB, D_MODEL), _rows(TB, D_MODEL)] + [_whole()] * 10,
        [_rows(TB, N_STATE), _rows(TB, N_STATE), _rows(TB, SSM_W), _rows(TB, rest),
         _rows(TB, D_MODEL), _rows(TB, D_MODEL), _rows(TB, D_MODEL), _rows(TB, SSM_W),
         _rows(TB, SSM_W), _rows(TB, SSM_W), _rows(TB, SGU_W), _rows(TB, SSM_W),
         _acc(SUBLANE, SSM_W), _acc(CHUNK, SGU_W),
         pl.BlockSpec((SGU_G, CHUNK, CHUNK), lambda i: (0, 0, 0))],
        [jax.ShapeDtypeStruct((t_len, N_STATE), BF16), jax.ShapeDtypeStruct((t_len, N_STATE), BF16),
         jax.ShapeDtypeStruct((t_len, SSM_W), F32), jax.ShapeDtypeStruct((t_len, rest), BF16),
         bf_d, bf_d, bf_d, bf_s, bf_s, bf_s, bf_s, bf_s,
         jax.ShapeDtypeStruct((SUBLANE, SSM_W), F32), jax.ShapeDtypeStruct((CHUNK, SGU_W), F32),
         jax.ShapeDtypeStruct((SGU_G, CHUNK, CHUNK), F32)],
        [dx2, p, y0, z, mixed, ya, yb, w_out, w_pa, w_pb, w_glu, cre, cim, ws_st, wst_st, d_skip,
         g_sgu], jobs)


def _scan_bwd(dsr, dsi, str_, sti, tab_rev, jobs=()):
    t_len = dsr.shape[0]
    nblk = t_len // SUBLANE
    lb = SCAN_LANES

    def body(dr_ref, di_ref, sr_ref, si_ref, tab_ref, lr_ref, li_ref, dar_ref, dai_ref):
        tab_v = [tab_ref[q] for q in range(8)]
        row0 = lax.broadcasted_iota(jnp.int32, (SUBLANE, lb), 0) == 0
        tile = BF16_TILE

        def step(k, carry):
            cr, ci, acr, aci = carry
            base = pl.multiple_of((nblk - (k + 1) * SCAN_UNROLL) * SUBLANE, SCAN_UNROLL * SUBLANE)
            state = _load_blocks(sr_ref, si_ref, base)
            before = pl.ds(pl.multiple_of(jnp.maximum(base - tile, 0), tile), tile)
            has_before = jnp.where(base > 0, 1.0, 0.0)
            prev = (sr_ref[before, :].astype(F32)[tile - 1:tile] * has_before,
                    si_ref[before, :].astype(F32)[tile - 1:tile] * has_before)
            local = [_scan_local(xr, xi, tab_v, (7, 6, 4))
                     for xr, xi in _load_blocks(dr_ref, di_ref, base)]
            lam = [None] * SCAN_UNROLL
            for b in reversed(range(SCAN_UNROLL)):
                xr, xi = _scan_carry(*local[b], tab_v, cr, ci)
                lam[b] = (xr, xi)
                cr, ci = xr[0:1, :], xi[0:1, :]
                pr, pi = prev if b == 0 else (state[b - 1][0][SUBLANE - 1:], state[b - 1][1][SUBLANE - 1:])
                s_r = jnp.where(row0, pr, pltpu.roll(state[b][0], 1, 0))
                s_i = jnp.where(row0, pi, pltpu.roll(state[b][1], 1, 0))
                acr = acr + xr * s_r + xi * s_i
                aci = aci + xi * s_r - xr * s_i
            _store_blocks(lr_ref, li_ref, base, lam)
            return cr, ci, acr, aci

        zero = jnp.zeros((1, lb), F32)
        zacc = jnp.zeros((SUBLANE, lb), F32)
        _, _, acr, aci = lax.fori_loop(0, nblk // SCAN_UNROLL, step, (zero, zero, zacc, zacc))
        dar_ref[...] = acr
        dai_ref[...] = aci

    col = pl.BlockSpec((t_len, lb), lambda j: (0, j))
    small = pl.BlockSpec((SUBLANE, lb), lambda j: (0, j))
    return _call(
        body, "scan_bwd", (N_STATE // lb,),
        [col, col, col, col, pl.BlockSpec((8, SUBLANE, lb), lambda j: (0, 0, j))],
        [col, col, small, small],
        [jax.ShapeDtypeStruct((t_len, N_STATE), BF16)] * 2
        + [jax.ShapeDtypeStruct((SUBLANE, N_STATE), F32)] * 2,
        [dsr, dsi, str_, sti, tab_rev], jobs)


def _bwd_in(lam_r, lam_i, du_part, drest, x, dx2, g_mix, w_in, bre, bim, jobs=()):
    t_len = x.shape[0]
    cs = IN_COLS // N_CHIP

    def body(lr_ref, li_ref, du_ref, dr_ref, x_ref, dx2_ref, g_ref, w_ref, bre_ref, bim_ref,
             gx_ref, dp_ref, sm_ref):
        i = pl.program_id(0)
        du = du_ref[...] + jnp.concatenate(
            [_dot_nt(lr_ref[:, i * DIAG_N:(i + 1) * DIAG_N],
                     bre_ref[i * LANE:(i + 1) * LANE, i * DIAG_N:(i + 1) * DIAG_N])
             + _dot_nt(li_ref[:, i * DIAG_N:(i + 1) * DIAG_N],
                       bim_ref[i * LANE:(i + 1) * LANE, i * DIAG_N:(i + 1) * DIAG_N])
             for i in range(SSM_W // LANE)], axis=1)
        dp_ref[:, 0:SSM_W] = du.astype(BF16)
        dp_ref[:, SSM_W:] = dr_ref[...]
        dh = jnp.zeros(x_ref.shape, F32)
        for k in range(N_CHIP):
            dh = dh + _dot_nt(dp_ref[:, k * cs:(k + 1) * cs], w_ref[k])
        r, xh = _rms_stats(x_ref[...])
        gx_ref[...] = dx2_ref[...] + _rms_bwd(dh * g_ref[...], xh, r)
        upd = jnp.concatenate([jnp.sum(dh * xh, axis=0, keepdims=True),
                               jnp.zeros((SUBLANE - 1, D_MODEL), F32)], axis=0)

        @pl.when(i == 0)
        def _():
            sm_ref[...] = upd

        @pl.when(i > 0)
        def _():
            sm_ref[...] += upd

    tb = min(TB_WIDE, t_len)
    return _call(
        body, "bwd_in", (t_len // tb,),
        [_rows(tb, N_STATE), _rows(tb, N_STATE), _rows(tb, SSM_W), _rows(tb, IN_COLS - SSM_W),
         _rows(tb, D_MODEL), _rows(tb, D_MODEL)] + [_whole()] * 4,
        [_rows(tb, D_MODEL), _rows(tb, IN_COLS), _acc(SUBLANE, D_MODEL)],
        [jax.ShapeDtypeStruct((t_len, D_MODEL), F32), jax.ShapeDtypeStruct((t_len, IN_COLS), BF16),
         jax.ShapeDtypeStruct((SUBLANE, D_MODEL), F32)],
        [lam_r, lam_i, du_part, drest, x, dx2, g_mix, w_in, bre, bim], jobs)


def _matmul_tn(a, b, name, out_shape, grid_ij, a_blk, a_map, b_blk, b_map, o_blk, o_map, jobs=()):
    tk = a_blk[0]
    nk = a.shape[0] // tk
    assert nk * tk == a.shape[0] and nk > 0

    def body(a_ref, b_ref, o_ref, acc_ref):
        k = pl.program_id(2)

        @pl.when(k == 0)
        def _():
            acc_ref[...] = jnp.zeros_like(acc_ref)

        acc_ref[...] += lax.dot_general(a_ref[...].astype(BF16), b_ref[...].astype(BF16),
                                        (((0,), (0,)), ((), ())), preferred_element_type=F32)

        @pl.when(k == nk - 1)
        def _():
            o_ref[...] = acc_ref[...]

    outs, per_job = _call(
        body, name, (grid_ij[0], grid_ij[1], nk),
        [pl.BlockSpec(a_blk, a_map), pl.BlockSpec(b_blk, b_map)], [pl.BlockSpec(o_blk, o_map)],
        [jax.ShapeDtypeStruct(out_shape, F32)], [a, b], jobs,
        scratch=[pltpu.VMEM((a_blk[1], b_blk[1]), F32)])
    return outs[0], per_job


def _dw_rows(a, b, name, tm, tk):
    m, n = a.shape[1], b.shape[1]
    tk = min(tk, a.shape[0])
    return _matmul_tn(a, b, name, (m, n), (m // tm, 1),
                      (tk, tm), lambda i, j, k: (k, i), (tk, n), lambda i, j, k: (k, 0),
                      (tm, n), lambda i, j, k: (i, 0))[0]


def _dw_cols(a, b, name, tn, sharded, jobs=()):
    t_len, m = a.shape
    n = b.shape[1]

    def body(a_ref, b_ref, o_ref):
        o_ref[...] = lax.dot_general(a_ref[...].astype(BF16), b_ref[...].astype(BF16),
                                     (((0,), (0,)), ((), ())), preferred_element_type=F32)

    if sharded:
        o_spec, o_shape = pl.BlockSpec((None, m, tn), lambda j: (j, 0, 0)), (n // tn, m, tn)
    else:
        o_spec, o_shape = pl.BlockSpec((m, tn), lambda j: (0, j)), (m, n)
    outs, per_job = _call(body, name, (n // tn,),
                          [_whole(), pl.BlockSpec((t_len, tn), lambda j: (0, j))], [o_spec],
                          [jax.ShapeDtypeStruct(o_shape, F32)], [a, b], jobs)
    return outs[0], per_job


def _dw_tiles(a, b, name, tm, tn, jobs=()):
    t_len, m = a.shape
    n = b.shape[1]

    def body(a_ref, b_ref, o_ref):
        o_ref[...] = lax.dot_general(a_ref[...].astype(BF16), b_ref[...].astype(BF16),
                                     (((0,), (0,)), ((), ())), preferred_element_type=F32)

    outs, per_job = _call(body, name, (n // tn, m // tm),
                          [pl.BlockSpec((t_len, tm), lambda j, i: (0, i)),
                           pl.BlockSpec((t_len, tn), lambda j, i: (0, j))],
                          [pl.BlockSpec((None, tm, tn), lambda j, i: (j, i, 0))],
                          [jax.ShapeDtypeStruct((n // tn, m, tn), F32)], [a, b], jobs)
    return outs[0], per_job


def _dw_pair(a, m, b1, b2, name, jobs=()):
    t_len = a.shape[0]
    n_slab = DIAG_N // LANE
    rows_per_slab = LANE // n_slab

    def body(a_ref, b1_ref, b2_ref, o1_ref, o2_ref):
        for b_ref, o_ref in ((b1_ref, o1_ref), (b2_ref, o2_ref)):
            prod = lax.dot_general(a_ref[...].astype(BF16), b_ref[...].astype(BF16),
                                   (((0,), (0,)), ((), ())), preferred_element_type=F32)
            for j in range(n_slab):
                rows = slice(j * rows_per_slab, (j + 1) * rows_per_slab)
                o_ref[rows, :] = prod[rows, j * LANE:(j + 1) * LANE]

    tok = pl.BlockSpec((t_len, DIAG_N), lambda i: (0, i))
    out = pl.BlockSpec((LANE, LANE), lambda i: (i, 0))
    return _call(body, name, (m // LANE,),
                 [pl.BlockSpec((t_len, LANE), lambda i: (0, i)), tok, tok], [out, out],
                 [jax.ShapeDtypeStruct((m, LANE), F32)] * 2, [a, b1, b2], jobs)


def _prefetch_call(body, name, grid, scalars, in_specs, out_specs, out_shape, args):
    return pl.pallas_call(
        body, name=name,
        grid_spec=pltpu.PrefetchScalarGridSpec(num_scalar_prefetch=1, grid=grid, in_specs=in_specs,
                                               out_specs=out_specs),
        out_shape=out_shape, compiler_params=_params(len(grid)),
    )(scalars, *args)


def _place_shard(w, where, name, dtype, tr):
    rows, cols = w.shape

    def body(s_ref, w_ref, o_ref):
        o_ref[...] = w_ref[...].astype(dtype)

    return _prefetch_call(
        body, name, (rows // tr,), where,
        [pl.BlockSpec((tr, cols), lambda i, s: (i, 0))],
        pl.BlockSpec((None, tr, cols), lambda i, s: (s[0], i, 0)),
        jax.ShapeDtypeStruct((N_CHIP, rows, cols), dtype), [w])


def _place_shards(ws, where, name, dtype):
    n = len(ws)

    def body(s_ref, *refs):
        for t in range(n):
            refs[n + t][...] = refs[t][...].astype(dtype)

    return _prefetch_call(
        body, name, (1,), where,
        [pl.BlockSpec(w.shape, lambda i, s: (0, 0)) for w in ws],
        [pl.BlockSpec((None,) + w.shape, lambda i, s: (s[0], 0, 0)) for w in ws],
        [jax.ShapeDtypeStruct((N_CHIP,) + w.shape, dtype) for w in ws], ws)


def _add_sibling(gs, gots, where, name):
    n = len(gs)
    halves = [(g.shape[1] // 2, g.shape[2]) for g in gs]

    def body(s_ref, *refs):
        for t in range(n):
            refs[2 * n + t][...] = (refs[t][...] + refs[n + t][...]).astype(BF16)

    return _prefetch_call(
        body, name, (N_CHIP,), where,
        [pl.BlockSpec((None, hr, cs), lambda k, s: (k, s[1], 0)) for hr, cs in halves]
        + [pl.BlockSpec((None, hr, cs), lambda k, s: (k, 0, 0)) for hr, cs in halves],
        [pl.BlockSpec((None, hr, cs), lambda k, s: (k, 0, 0)) for hr, cs in halves],
        [jax.ShapeDtypeStruct((N_CHIP, hr, cs), BF16) for hr, cs in halves], list(gs) + list(gots))


def _add_chips(sums, gots, where, name):
    n = len(sums)
    halves = [s.shape[1:] for s in sums]

    def body(s_ref, *refs):
        for t in range(n):
            own_ref, got_ref = refs[t], refs[n + t]
            refs[2 * n + t][...] = ((own_ref[...].astype(F32) + got_ref[0].astype(F32))
                                    + got_ref[1].astype(F32)) + got_ref[2].astype(F32)

    return _prefetch_call(
        body, name, (1,), where,
        [pl.BlockSpec((None, hr, cs), lambda i, s: (s[0], 0, 0)) for hr, cs in halves]
        + [pl.BlockSpec((3, hr, cs), lambda i, s: (0, 0, 0)) for hr, cs in halves],
        [pl.BlockSpec((hr, cs), lambda i, s: (s[1], 0)) for hr, cs in halves],
        [jax.ShapeDtypeStruct((2 * hr, cs), F32) for hr, cs in halves], list(sums) + list(gots))


def _small_allreduce(pack):
    rows = pack.shape[0]
    half = rows // 2

    def body(in_ref, out_ref, sib_ref, slots_ref, s_a, r_a, s_b, r_b, s_c, r_c):
        x, y, c, chips = _place()
        k_me = 2 * x + y
        sib = (x, y, 1 - c)
        mine, theirs = _half(rows, c), _half(rows, 1 - c)
        first = _remote(in_ref.at[theirs, :], sib_ref.at[theirs, :], s_a, r_a, sib)
        first.start()
        first.wait_send()
        landed = sib_ref.at[mine, :]
        _remote(landed, landed, s_a, r_a, sib).wait_recv()
        slots_ref[k_me] = in_ref[mine, :] + sib_ref[mine, :]
        cps = [_remote(slots_ref.at[k_me], slots_ref.at[k_me], s_b.at[j], r_b.at[j], (*ch, c))
               for j, ch in enumerate(chips)]
        for cp in cps:
            cp.start()
        for j, ch in enumerate(chips):
            slot = slots_ref.at[_chip_index(ch)]
            _remote(slot, slot, s_b.at[j], r_b.at[j], (*ch, c)).wait_recv()
        for cp in cps:
            cp.wait_send()
        out_ref[mine, :] = ((slots_ref[0] + slots_ref[1]) + slots_ref[2]) + slots_ref[3]
        last = _remote(out_ref.at[mine, :], out_ref.at[mine, :], s_c, r_c, sib)
        last.start()
        other = out_ref.at[theirs, :]
        _remote(other, other, s_c, r_c, sib).wait_recv()
        last.wait_send()

    return pl.pallas_call(
        body, name="small_allreduce", in_specs=[_whole()], out_specs=_whole(),
        out_shape=jax.ShapeDtypeStruct(pack.shape, F32),
        scratch_shapes=[pltpu.VMEM(pack.shape, F32), pltpu.VMEM((N_CHIP, half, LANE), F32),
                        pltpu.SemaphoreType.DMA, pltpu.SemaphoreType.DMA,
                        pltpu.SemaphoreType.DMA((3,)), pltpu.SemaphoreType.DMA((3,)),
                        pltpu.SemaphoreType.DMA, pltpu.SemaphoreType.DMA],
        compiler_params=_params(0),
    )(pack)


def _adamw_update(w_ref, g_ref, m_ref, v_ref, d_ref, mo_ref, vo_ref):
    gv = g_ref[...]
    mn = ADAM_B1 * m_ref[...] + (1.0 - ADAM_B1) * gv
    vn = ADAM_B2 * v_ref[...] + (1.0 - ADAM_B2) * (gv * gv)
    mo_ref[...] = mn
    vo_ref[...] = vn
    m_hat = mn / (1.0 - ADAM_B1 ** ADAM_STEP)
    v_hat = vn / (1.0 - ADAM_B2 ** ADAM_STEP)
    d_ref[...] = -ADAM_LR * (m_hat / (jnp.sqrt(v_hat) + ADAM_EPS) + ADAM_WD * w_ref[...])


def _adamw(w, g, m, v, name, tr):
    rows, cols = w.shape
    blk = _rows(tr, cols)

    def body(w_ref, g_ref, m_ref, v_ref, go_ref, d_ref, mo_ref, vo_ref):
        go_ref[...] = g_ref[...]
        _adamw_update(w_ref, g_ref, m_ref, v_ref, d_ref, mo_ref, vo_ref)

    return _call(body, name, (rows // tr,), [blk] * 4, [blk] * 4,
                 [jax.ShapeDtypeStruct(w.shape, F32)] * 4, [w, g, m, v])[0]


def _adamw_many(ws, gs, ms, vs, name):
    n = len(ws)

    def body(*refs):
        for t in range(n):
            _adamw_update(*[refs[q * n + t] for q in range(7)])

    specs = [pl.BlockSpec(a.shape, lambda i, nd=a.ndim: (0,) * nd) for a in ws]
    outs = pl.pallas_call(
        body, name=name, grid=(1,), in_specs=specs * 4, out_specs=specs * 3,
        out_shape=[jax.ShapeDtypeStruct(a.shape, F32) for _ in range(3) for a in ws],
        compiler_params=_params(1),
    )(*ws, *gs, *ms, *vs)
    return outs[:n], outs[n:2 * n], outs[2 * n:]


def _ssm_discretize(a_re, a_im, log_dt, b_re, b_im):
    dt = jnp.exp(log_dt)[:, None]
    mag = jnp.exp(dt * a_re)
    abr = mag * jnp.cos(dt * a_im)
    abi = mag * jnp.sin(dt * a_im)
    den = a_re * a_re + a_im * a_im
    nr = abr - 1.0
    ni = abi
    f_re = (nr * a_re + ni * a_im) / den
    f_im = (ni * a_re - nr * a_im) / den
    bbr = f_re[..., None] * b_re - f_im[..., None] * b_im
    bbi = f_re[..., None] * b_im + f_im[..., None] * b_re
    return abr, abi, bbr, bbi


def _scan_tables(abr, abi):
    ar = abr.reshape(1, N_STATE)
    ai = abi.reshape(1, N_STATE)
    pr, pi = [ar], [ai]
    for _ in range(SUBLANE - 1):
        pr, pi = pr + [pr[-1] * ar - pi[-1] * ai], pi + [pr[-1] * ai + pi[-1] * ar]
    row = jnp.arange(SUBLANE)[:, None]
    tabs = []
    for d in (1, 2, 4):
        tabs.append(jnp.where(row >= d, pr[d - 1], 0.0))
        tabs.append(jnp.where(row >= d, pi[d - 1], 0.0))
    tabs.append(jnp.concatenate(pr, axis=0))
    tabs.append(jnp.concatenate(pi, axis=0))
    fwd = jnp.stack(tabs)
    sign = jnp.array([1.0, -1.0] * 4, F32)[:, None, None]
    return fwd, fwd[:, ::-1, :] * sign


def _block_diag_b(bb):
    strip = bb.transpose(2, 0, 1).reshape(SSM_H, N_STATE)
    rows = lax.broadcasted_iota(jnp.int32, (SSM_W, N_STATE), 0) // SSM_H
    cols = lax.broadcasted_iota(jnp.int32, (SSM_W, N_STATE), 1) // SSM_P
    return jnp.where(rows == cols, jnp.tile(strip, (SSM_G, 1)), 0.0).astype(BF16)


def _block_diag_c(cc):
    strip = cc.transpose(0, 2, 1).reshape(N_STATE, SSM_H)
    rows = lax.broadcasted_iota(jnp.int32, (N_STATE, SSM_W), 0) // SSM_P
    cols = lax.broadcasted_iota(jnp.int32, (N_STATE, SSM_W), 1) // SSM_H
    return jnp.where(rows == cols, jnp.tile(strip, (1, SSM_G)), 0.0).astype(BF16)


SMALL_SHAPES = {
    "g_mix": (D_MODEL,), "a_re": (SSM_G, SSM_P), "a_im": (SSM_G, SSM_P), "log_dt": (SSM_G,),
    "b_re": (SSM_G, SSM_P, SSM_H), "b_im": (SSM_G, SSM_P, SSM_H),
    "c_re": (SSM_G, SSM_H, SSM_P), "c_im": (SSM_G, SSM_H, SSM_P),
    "d_skip": (SSM_W,), "b_glu": (SSM_W,), "g_sgu": (SGU_W,), "w_s": (SGU_G, CHUNK, CHUNK),
    "b_s": (SGU_G, CHUNK), "g_ffn": (D_MODEL,), "conv_b": (2 * D_FF,), "g_final": (D_MODEL,),
}
PACK_ITEMS = [("loss", (1,))] + [(n, SMALL_SHAPES[n]) for n in SMALL] + [("conv_w", (3, 2 * D_FF))]
TILE = SUBLANE * LANE


def _item_rows(shape):
    return -(-math.prod(shape) // TILE) * SUBLANE


PACK_ROWS = -(-sum(_item_rows(s) for _, s in PACK_ITEMS) // (2 * SUBLANE)) * (2 * SUBLANE)


def _pack(values):
    parts, used = [], 0
    for name, shape in PACK_ITEMS:
        size, rows = math.prod(shape), _item_rows(shape)
        if name in values:
            flat = values[name].astype(F32).reshape(size)
            if rows * LANE > size:
                flat = jnp.pad(flat, (0, rows * LANE - size))
            parts.append(flat.reshape(rows, LANE))
        else:
            parts.append(jnp.zeros((rows, LANE), F32))
        used += rows
    if PACK_ROWS > used:
        parts.append(jnp.zeros((PACK_ROWS - used, LANE), F32))
    return jnp.concatenate(parts, axis=0)


def _unpack(pack):
    out, off = {}, 0
    for name, shape in PACK_ITEMS:
        rows = _item_rows(shape)
        out[name] = pack[off:off + rows].reshape(rows * LANE)[:math.prod(shape)].reshape(shape)
        off += rows
    return out


PLACE_ROWS = {"w_in": 256, "w_up": 256, "w_down": 352, "w_out": 256, "w_proj_a": 256,
              "w_proj_b": 256, "w_glu": 128}


def kernel(x, g_mix, w_in, a_re, a_im, log_dt, b_re, b_im, c_re, c_im, d_skip, w_glu, b_glu, w_proj_a, g_sgu, w_s, b_s, w_proj_b, w_out, g_ffn, w_up, conv_w, conv_b, w_down, g_final, loss_target, m_g_mix, m_w_in, m_a_re, m_a_im, m_log_dt, m_b_re, m_b_im, m_c_re, m_c_im, m_d_skip, m_w_glu, m_b_glu, m_w_proj_a, m_g_sgu, m_w_s, m_b_s, m_w_proj_b, m_w_out, m_g_ffn, m_w_up, m_conv_w, m_conv_b, m_w_down, m_g_final, v_g_mix, v_w_in, v_a_re, v_a_im, v_log_dt, v_b_re, v_b_im, v_c_re, v_c_im, v_d_skip, v_w_glu, v_b_glu, v_w_proj_a, v_g_sgu, v_w_s, v_b_s, v_w_proj_b, v_w_out, v_g_ffn, v_w_up, v_conv_w, v_conv_b, v_w_down, v_g_final):
    given = dict(locals())
    w = {n: given[n] for n in WEIGHTS}
    m = {n: given["m_" + n] for n in WEIGHTS}
    v = {n: given["v_" + n] for n in WEIGHTS}

    def shard2d(a):
        return a.reshape(a.shape[-2], a.shape[-1])

    chip = 2 * lax.axis_index("x") + lax.axis_index("y")
    where = jnp.stack([chip, lax.axis_index("c")]).astype(jnp.int32)
    xs, target = x[0], loss_target[0]
    small = {n: w[n].reshape(SMALL_SHAPES[n]) for n in SMALL}

    (abr, abi, bbr, bbi), disc_vjp = jax.vjp(_ssm_discretize, small["a_re"], small["a_im"],
                                             small["log_dt"], small["b_re"], small["b_im"])
    tab_f, tab_r = _scan_tables(abr, abi)
    bre = _block_diag_b(bbr)
    bim = _block_diag_b(bbi)
    cre = _block_diag_c(small["c_re"])
    cim = _block_diag_c(small["c_im"])
    tril = jnp.tril(jnp.ones((CHUNK, CHUNK), dtype=bool))
    ws = jnp.where(tril[None], small["w_s"], 0.0)
    ws_st = ws.reshape(SGU_G // 2, 2 * CHUNK, CHUNK).astype(BF16)
    wst_st = ws.transpose(0, 2, 1).reshape(SGU_G // 2, 2 * CHUNK, CHUNK).astype(BF16)
    bmat = jnp.repeat(small["b_s"].T, SGU_D, axis=1)
    g_mix2 = small["g_mix"].reshape(1, D_MODEL)
    g_ffn2 = small["g_ffn"].reshape(1, D_MODEL)
    g_final2 = small["g_final"].reshape(1, D_MODEL)
    g_sgu2 = small["g_sgu"].reshape(1, SGU_W)
    d_skip2 = small["d_skip"].reshape(1, SSM_W)
    b_glu2 = small["b_glu"].reshape(1, SSM_W)
    conv_b2 = small["conv_b"].reshape(1, 2 * D_FF)

    gat = {"w_in": _place_shard(shard2d(w["w_in"]), where, "place_w_in", BF16, PLACE_ROWS["w_in"])}
    gat.update(zip(BIG[1:], _place_shards([shard2d(w[n]) for n in BIG[1:]], where, "place_rest", BF16)))
    gat["conv_w"] = _place_shard(shard2d(w["conv_w"]), where, "place_conv_w", F32, 3)
    (gat["w_in"],), = _comm("gather_in", [_job_gather_now(gat["w_in"])])
    mixers = ["w_glu", "w_proj_a", "w_proj_b", "w_out"]
    rows = {n: (0, gat[n].shape[1]) for n in mixers}
    down_a, down_b = (0, D_FF // 8), (D_FF // 8, D_FF // 8)
    up_a, up_b = (0, 3 * D_MODEL // 8), (3 * D_MODEL // 8, 5 * D_MODEL // 8)
    span = (0.0, 1.0)

    names = mixers + ["conv_w", "w_down"]
    (p, h1, bur, bui), (got,) = _fwd_in(
        xs, g_mix2, gat["w_in"], bre, bim,
        [_job_gather([gat[n] for n in names],
                     [(i, rows[n], ICI, span) for i, n in enumerate(mixers)]
                     + [(4, None, ICI, span), (5, down_a, ICI, span)])])
    gat.update(zip(names, got))
    names = mixers + ["w_down", "w_up"]
    (str_, sti), (got,) = _scan_fwd(
        bur, bui, tab_f,
        [_job_gather([gat[n] for n in names],
                     [(i, rows[n], SIBLING, span) for i, n in enumerate(mixers)]
                     + [(4, down_a, SIBLING, span), (4, down_b, ICI, span), (5, up_a, ICI, span)])])
    gat.update(zip(names, got))
    w_glu_f = gat["w_glu"].reshape(SSM_W, SSM_W)
    w_out_f = gat["w_out"].reshape(D_MODEL, D_MODEL)
    conv_w_f = gat["conv_w"].transpose(1, 0, 2).reshape(3, 2 * D_FF)
    (x2, y0, z, mixed, ya, yb), ((gat["w_down"], gat["w_up"]),) = _fwd_mix(
        xs, p, str_, sti, cre, cim, d_skip2, w_glu_f, b_glu2, gat["w_proj_a"], g_sgu2, ws_st, bmat,
        gat["w_proj_b"], w_out_f,
        [_job_gather([gat["w_down"], gat["w_up"]],
                     [(0, down_b, SIBLING, span), (1, up_a, SIBLING, span),
                      (1, up_b, ICI, (0.0, 0.75)), (1, up_b, SIBLING, (0.75, 1.0))])])
    w_down_f = gat["w_down"].reshape(D_FF, D_MODEL)
    up, act, f, h2, dx3, sm_ffn = _fwd_ffn(x2, target, g_ffn2, gat["w_up"], conv_w_f, conv_b2,
                                           w_down_f, g_final2)

    def leg1_done(names, got):
        return _add_sibling([part[n] for n in names], got, where, "add_sibling_" + names[0])

    def leg2_done(names, sums, got):
        return _add_chips(sums, got, where, "add_chips_" + names[0])

    part, red = {}, {}
    part["w_down"] = _dw_rows(f, dx3, "dw_down", D_FF // 2, 4 * TK).reshape(
        N_CHIP, D_FF // N_CHIP, D_MODEL)
    (dx2, dup, sm_conv, sm_gffn), (got,) = _bwd_ffn(
        dx3, up, act, x2, g_ffn2, gat["w_up"], conv_w_f, w_down_f,
        [_job_sibling_halves([part["w_down"]])])
    sum_down = leg1_done(["w_down"], got)
    part["w_up"], (got,) = _dw_tiles(h2, dup, "dw_up", D_MODEL // 2, 2 * D_FF // N_CHIP,
                                     [_job_to_owner(sum_down)])
    red_down = leg2_done(["w_down"], sum_down, got)
    ((dsr, dsi, du_part, drest, mrg, dya, dyb, yap, dz, y1, sgu, dy0, sm_mix, dbm, dws),
     (got, (red["w_down"],))) = _bwd_mix(
        dx2, p, y0, z, mixed, ya, yb, w_out_f, gat["w_proj_a"], gat["w_proj_b"], w_glu_f, cre, cim,
        ws_st, wst_st, d_skip2, g_sgu2,
        [_job_sibling_halves([part["w_up"]]), _job_swap_halves(red_down)])
    sum_up = leg1_done(["w_up"], got)
    (lam_r, lam_i, dar8, dai8), (got,) = _scan_bwd(dsr, dsi, str_, sti, tab_r, [_job_to_owner(sum_up)])
    red_up = leg2_done(["w_up"], sum_up, got)
    mix4 = ["w_out", "w_proj_a", "w_proj_b", "w_glu"]
    part["w_out"] = _dw_cols(mrg, dx2, "dw_out", D_MODEL // 2, False)[0].reshape(
        N_CHIP, D_MODEL // N_CHIP, D_MODEL)
    part["w_proj_a"] = _dw_cols(yap, dya, "dw_proj_a", D_MODEL // N_CHIP, True)[0]
    part["w_proj_b"] = _dw_cols(sgu, dyb, "dw_proj_b", D_MODEL // N_CHIP, True)[0]
    part["w_glu"] = _dw_cols(y1, dz, "dw_glu", SSM_W, False)[0].reshape(
        N_CHIP, SSM_W // N_CHIP, SSM_W)
    got, (red["w_up"],) = _comm(
        "mixer_sibling_halves", [_job_sibling_halves([part[n] for n in mix4]), _job_swap_halves(red_up)])
    (grad_x, dp, sm_gmix), _ = _bwd_in(
        lam_r, lam_i, du_part, drest, xs, dx2, g_mix2, gat["w_in"], bre, bim)
    sums_m = leg1_done(mix4, got)
    part["w_in"], (got,) = _dw_cols(h1, dp, "dw_in", IN_COLS // N_CHIP, True, [_job_to_owner(sums_m)])
    red_m = leg2_done(mix4, sums_m, got)
    (dbd_r, dbd_i), (got, done_m) = _dw_pair(
        p, SSM_W, lam_r, lam_i, "db_bar",
        [_job_sibling_halves([part["w_in"]]), _job_swap_halves(red_m)])
    red.update(zip(mix4, done_m))
    sum_in = leg1_done(["w_in"], got)
    (dcd_r, dcd_i), (got,) = _dw_pair(dy0, SSM_W, str_, sti, "dc", [_job_to_owner(sum_in)])
    red_in = leg2_done(["w_in"], sum_in, got)
    (red["w_in"],), = _comm("swap_w_in", [_job_swap_halves(red_in)])

    def pick_c(slabs):
        two = LANE // SSM_P
        return jnp.einsum("jshsp->jshp", slabs.reshape(SSM_G // two, two, SSM_H, two, SSM_P)
                          ).reshape(SSM_G, SSM_H, SSM_P)

    def pick_b(slabs):
        return pick_c(slabs).transpose(0, 2, 1)

    dabr = jnp.sum(dar8, axis=0).reshape(SSM_G, SSM_P)
    dabi = jnp.sum(dai8, axis=0).reshape(SSM_G, SSM_P)
    d_a_re, d_a_im, d_log_dt, d_b_re, d_b_im = disc_vjp((dabr, dabi, pick_b(dbd_r), pick_b(dbd_i)))
    gsmall = {
        "g_mix": sm_gmix[0], "a_re": d_a_re, "a_im": d_a_im, "log_dt": d_log_dt,
        "b_re": d_b_re, "b_im": d_b_im, "c_re": pick_c(dcd_r), "c_im": -pick_c(dcd_i),
        "d_skip": sm_mix[0], "b_glu": sm_mix[1], "g_sgu": sm_mix[2],
        "w_s": jnp.where(tril[None], dws, 0.0),
        "b_s": dbm.reshape(CHUNK, SGU_G, SGU_D).sum(-1).T,
        "g_ffn": sm_gffn[0], "conv_b": sm_conv[3], "g_final": sm_ffn[0],
        "conv_w": sm_conv[0:3], "loss": sm_ffn[1, 0:1],
    }

    total_pack = _small_allreduce(_pack(gsmall))
    total = _unpack(total_pack)
    grads = dict(red)
    cs = 2 * D_FF // N_CHIP
    grads["conv_w"] = lax.dynamic_slice(total["conv_w"], (0, chip * cs), (3, cs))
    delta, new_m, new_v = {}, {}, {}
    for n in BIG + ("conv_w",):
        grads[n], delta[n], new_m[n], new_v[n] = _adamw(
            shard2d(w[n]), grads[n], shard2d(m[n]), shard2d(v[n]), "adamw_" + n, PLACE_ROWS.get(n, 3))
    for n in SMALL:
        grads[n] = total[n].reshape(w[n].shape)
    ud, um, uv = _adamw_many(*[[d[n] for n in SMALL] for d in (w, grads, m, v)], "adamw_small")
    for i, n in enumerate(SMALL):
        delta[n], new_m[n], new_v[n] = ud[i], um[i], uv[i]

    def like(d):
        return [d[n].reshape(w[n].shape) for n in WEIGHTS]

    return (total["loss"].reshape(()), grad_x.reshape(x.shape), *like(grads), *like(delta),
            *like(new_m), *like(new_v))
```

```python
import math

import jax
import jax.numpy as jnp
from jax import lax
from jax.experimental import pallas as pl
from jax.experimental.pallas import tpu as pltpu

F32 = jnp.float32
BF16 = jnp.bfloat16
MESH = pl.DeviceIdType.MESH

D_MODEL = 1024
SSM_W = 512
SSM_G = 32
SSM_H = 16
SSM_P = 64
N_STATE = SSM_G * SSM_P
DIAG_N = 128 * SSM_P // SSM_H
SGU_W = 512
SGU_G = 8
SGU_D = 64
CHUNK = 128
D_FF = 2816
IN_COLS = 3584
EPS = 1e-6
N_CHIP = 4

ADAM_LR = 0.001
ADAM_B1 = 0.9
ADAM_B2 = 0.999
ADAM_EPS = 1e-08
ADAM_WD = 0.01
ADAM_STEP = 10

SUBLANE = 8
LANE = 128
VMEM_LIMIT = 56 * 1024 * 1024
TB = 256
TB_WIDE = 512
TK = 512
SCAN_LANES = 256
SCAN_UNROLL = 4
HALO = SUBLANE

BIG = ("w_in", "w_up", "w_down", "w_out", "w_proj_a", "w_proj_b", "w_glu")
SMALL = ("g_mix", "a_re", "a_im", "log_dt", "b_re", "b_im", "c_re", "c_im", "d_skip", "b_glu",
         "g_sgu", "w_s", "b_s", "g_ffn", "conv_b", "g_final")
WEIGHTS = ("g_mix", "w_in", "a_re", "a_im", "log_dt", "b_re", "b_im", "c_re", "c_im", "d_skip",
           "w_glu", "b_glu", "w_proj_a", "g_sgu", "w_s", "b_s", "w_proj_b", "w_out", "g_ffn",
           "w_up", "conv_w", "conv_b", "w_down", "g_final")

ANY = pl.BlockSpec(memory_space=pl.ANY)


def _params(n_grid):
    return pltpu.CompilerParams(dimension_semantics=("arbitrary",) * n_grid if n_grid else None,
                                vmem_limit_bytes=VMEM_LIMIT)


def _whole():
    return pl.BlockSpec(memory_space=pltpu.VMEM)


def _rows(tb, ncol):
    return pl.BlockSpec((tb, ncol), lambda i: (i, 0))


def _acc(nrow, ncol):
    return pl.BlockSpec((nrow, ncol), lambda i: (0, 0))


def _dot(a, b):
    return jnp.dot(a.astype(BF16), b.astype(BF16), preferred_element_type=F32)


def _dot_nt(a, b):
    return lax.dot_general(a.astype(BF16), b.astype(BF16), (((1,), (1,)), ((), ())),
                           preferred_element_type=F32)


def _sigmoid(v):
    return 0.5 * jnp.tanh(0.5 * v) + 0.5


_GELU_C = math.sqrt(2.0 / math.pi)


def _gelu(v):
    return 0.5 * v * (1.0 + jnp.tanh(_GELU_C * (v + 0.044715 * v * v * v)))


def _gelu_and_grad(v):
    v2 = v * v
    t = jnp.tanh(_GELU_C * v * (1.0 + 0.044715 * v2))
    half = 0.5 * (1.0 + t)
    return v * half, half + 0.5 * v * (1.0 - t * t) * _GELU_C * (1.0 + 3.0 * 0.044715 * v2)


def _rms_stats(v):
    r = lax.rsqrt(jnp.mean(v * v, axis=-1, keepdims=True) + EPS)
    return r, v * r


def _rms_bwd(dxh, xh, r):
    return r * (dxh - xh * jnp.mean(dxh * xh, axis=-1, keepdims=True))


def _place():
    x, y, c = lax.axis_index("x"), lax.axis_index("y"), lax.axis_index("c")
    chips = [(1 - x, y), (x, 1 - y), (1 - x, 1 - y)]
    return x, y, c, chips


def _chip_index(chip):
    return 2 * chip[0] + chip[1]


def _remote(src, dst, send_sem, recv_sem, device):
    return pltpu.make_async_remote_copy(src_ref=src, dst_ref=dst, send_sem=send_sem,
                                        recv_sem=recv_sem, device_id=device, device_id_type=MESH)


def _half(ref_rows, c):
    hr = ref_rows // 2
    return pl.ds(pl.multiple_of(c * hr, SUBLANE), hr)


class _Job:
    def __init__(self, hooks, n_sem, ins=(), inouts=(), outs=()):
        self.hooks, self.n_sem = list(hooks), n_sem
        self.ins, self.inouts, self.outs = list(ins), list(inouts), list(outs)


def _whole_span(start, finish):
    return [(0.0, "start", start), (1.0, "finish", finish)]


ICI, SIBLING = "ici", "sibling"


def _job_gather(bufs, legs):
    def copies(io, leg, first):
        b, window, kind, _ = legs[leg]
        x, y, c, chips = _place()
        k_me = 2 * x + y
        out = []
        for j, ch in enumerate(chips):
            k = _chip_index(ch)
            if window is None:
                src, land, dev = io[b].at[k_me], io[b].at[k], (*ch, c)
            else:
                r0, rows = window
                mine = pl.ds(pl.multiple_of(r0 + c * (rows // 2), SUBLANE), rows // 2)
                theirs = pl.ds(pl.multiple_of(r0 + (1 - c) * (rows // 2), SUBLANE), rows // 2)
                if kind == ICI:
                    src, land, dev = io[b].at[k_me, mine, :], io[b].at[k, mine, :], (*ch, c)
                else:
                    src, land, dev = io[b].at[k, mine, :], io[b].at[k, theirs, :], (x, y, 1 - c)
            out.append((src, land, first + j, dev))
        return out

    def starter(leg):
        def start(ins, io, outs, ssem, rsem):
            for src, _, i, dev in copies(io, leg, 3 * leg):
                _remote(src, src, ssem(i), rsem(i), dev).start()
        return start

    def finisher(leg):
        def finish(ins, io, outs, ssem, rsem):
            cps = copies(io, leg, 3 * leg)
            for _, land, i, dev in cps:
                _remote(land, land, ssem(i), rsem(i), dev).wait_recv()
            for src, _, i, dev in cps:
                _remote(src, src, ssem(i), rsem(i), dev).wait_send()
        return finish

    hooks = []
    for leg, (_, _, _, (begin, end)) in enumerate(legs):
        hooks += [(begin, "start", starter(leg)), (end, "finish", finisher(leg))]
    return _Job(hooks, 3 * len(legs), inouts=bufs)


def _job_gather_now(buf):
    rows = buf.shape[1]

    def run(ins, io, outs, ssem, rsem):
        x, y, c, chips = _place()
        k_me = 2 * x + y
        sib = (x, y, 1 - c)
        mine, theirs = _half(rows, c), _half(rows, 1 - c)
        own = io[0].at[k_me, mine, :]
        sends = [_remote(own, own, ssem(j), rsem(j), (*ch, c)) for j, ch in enumerate(chips)]
        for cp in sends:
            cp.start()
        passed = []
        for j, ch in enumerate(chips):
            landed = io[0].at[_chip_index(ch), mine, :]
            _remote(landed, landed, ssem(j), rsem(j), sib).wait_recv()
            cp = _remote(landed, landed, ssem(3 + j), rsem(3 + j), sib)
            cp.start()
            passed.append(cp)
        for j, ch in enumerate(chips):
            landed = io[0].at[_chip_index(ch), theirs, :]
            _remote(landed, landed, ssem(3 + j), rsem(3 + j), sib).wait_recv()
        for cp in sends + passed:
            cp.wait_send()

    return _Job([(0.0, "start", run)], 6, inouts=[buf])


def _job_sibling_halves(grads):
    n = len(grads)

    def build(ins, outs, ssem, rsem):
        x, y, c, _ = _place()
        return [_remote(ins[t].at[:, _half(grads[t].shape[1], 1 - c), :], outs[t], ssem(t), rsem(t),
                        (x, y, 1 - c)) for t in range(n)]

    def start(ins, io, outs, ssem, rsem):
        for cp in build(ins, outs, ssem, rsem):
            cp.start()

    def finish(ins, io, outs, ssem, rsem):
        for cp in build(ins, outs, ssem, rsem):
            cp.wait()

    return _Job(_whole_span(start, finish), n, ins=grads,
                outs=[jax.ShapeDtypeStruct((N_CHIP, g.shape[1] // 2, g.shape[2]), F32) for g in grads])


def _job_to_owner(sums):
    n = len(sums)

    def build(ins, outs, ssem, rsem):
        x, y, c, chips = _place()
        return [_remote(ins[t].at[_chip_index(ch)], outs[t].at[j], ssem(3 * t + j), rsem(3 * t + j),
                        (*ch, c)) for t in range(n) for j, ch in enumerate(chips)]

    def start(ins, io, outs, ssem, rsem):
        for cp in build(ins, outs, ssem, rsem):
            cp.start()

    def finish(ins, io, outs, ssem, rsem):
        for cp in build(ins, outs, ssem, rsem):
            cp.wait()

    return _Job(_whole_span(start, finish), 3 * n, ins=sums,
                outs=[jax.ShapeDtypeStruct((3,) + s.shape[1:], s.dtype) for s in sums])


def _job_swap_halves(bufs):
    n = len(bufs)

    def start(ins, io, outs, ssem, rsem):
        x, y, c, _ = _place()
        for t in range(n):
            mine = io[t].at[_half(bufs[t].shape[0], c), :]
            _remote(mine, mine, ssem(t), rsem(t), (x, y, 1 - c)).start()

    def finish(ins, io, outs, ssem, rsem):
        x, y, c, _ = _place()
        for t in range(n):
            theirs = io[t].at[_half(bufs[t].shape[0], 1 - c), :]
            _remote(theirs, theirs, ssem(t), rsem(t), (x, y, 1 - c)).wait_recv()
        for t in range(n):
            mine = io[t].at[_half(bufs[t].shape[0], c), :]
            _remote(mine, mine, ssem(t), rsem(t), (x, y, 1 - c)).wait_send()

    return _Job(_whole_span(start, finish), n, inouts=bufs)


def _call(body, name, grid, in_specs, out_specs, out_shape, args, jobs=(), scratch=()):
    n_in, n_out, n_scr = len(args), len(out_shape), len(scratch)
    job_in = [a for jb in jobs for a in jb.ins + jb.inouts]
    job_out = [s for jb in jobs
               for s in [jax.ShapeDtypeStruct(a.shape, a.dtype) for a in jb.inouts] + jb.outs]
    aliases, pos_in, pos_out = {}, n_in, n_out
    for jb in jobs:
        pos_in += len(jb.ins)
        for _ in jb.inouts:
            aliases[pos_in] = pos_out
            pos_in += 1
            pos_out += 1
        pos_out += len(jb.outs)
    n_sem = sum(jb.n_sem for jb in jobs)

    def wrapped(*refs):
        c_in = refs[:n_in]
        j_in = refs[n_in:n_in + len(job_in)]
        c_out = refs[n_in + len(job_in):n_in + len(job_in) + n_out]
        j_out = refs[n_in + len(job_in) + n_out:n_in + len(job_in) + n_out + len(job_out)]
        rest = refs[n_in + len(job_in) + n_out + len(job_out):]
        c_scr = rest[:n_scr]
        views, pi, po, ps = [], 0, 0, 0
        for jb in jobs:
            ins = j_in[pi:pi + len(jb.ins)]
            pi += len(jb.ins) + len(jb.inouts)
            io = j_out[po:po + len(jb.inouts)]
            new = j_out[po + len(jb.inouts):po + len(jb.inouts) + len(jb.outs)]
            po += len(jb.inouts) + len(jb.outs)
            send = (lambda i, o=ps: rest[n_scr].at[o + i])
            recv = (lambda i, o=ps: rest[n_scr + 1].at[o + i])
            ps += jb.n_sem
            views.append((ins, io, new, send, recv))

        def run(frac):
            for kind in ("finish", "start"):
                for jb, vw in zip(jobs, views):
                    for at, what, fn in jb.hooks:
                        if at == frac and what == kind:
                            fn(*vw)

        fracs = sorted({at for jb in jobs for at, _, _ in jb.hooks})
        if not grid:
            for frac in fracs:
                run(frac)
            return
        if jobs:
            assert len(grid) == 1 or set(fracs) <= {0.0, 1.0}
            first = pl.program_id(0) == 0
            last = pl.program_id(0) == grid[0] - 1
            for d in range(1, len(grid)):
                first = jnp.logical_and(first, pl.program_id(d) == 0)
                last = jnp.logical_and(last, pl.program_id(d) == grid[d] - 1)
            for frac in fracs:
                if frac < 1.0:
                    at_step = first if frac == 0.0 else pl.program_id(0) == int(frac * grid[0])
                    pl.when(at_step)(lambda frac=frac: run(frac))
        body(*c_in, *c_out, *c_scr)
        if jobs and 1.0 in fracs:
            pl.when(last)(lambda: run(1.0))

    sems = [pltpu.SemaphoreType.DMA((n_sem,)), pltpu.SemaphoreType.DMA((n_sem,))] if jobs else []
    kwargs = dict(grid=grid) if grid else {}
    res = pl.pallas_call(
        wrapped, name=name, in_specs=list(in_specs) + [ANY] * len(job_in),
        out_specs=list(out_specs) + [ANY] * len(job_out),
        out_shape=list(out_shape) + job_out, scratch_shapes=list(scratch) + sems,
        input_output_aliases=aliases, compiler_params=_params(len(grid)), **kwargs,
    )(*args, *job_in)
    outs, pos, per_job = list(res[:n_out]), n_out, []
    for jb in jobs:
        k = len(jb.inouts) + len(jb.outs)
        per_job.append(list(res[pos:pos + k]))
        pos += k
    return outs, per_job


def _comm(name, jobs):
    return _call(None, name, (), [], [], [], [], jobs)[1]


def _fwd_in(x, g_mix, w_in, bre, bim, jobs=()):
    t_len = x.shape[0]
    cs = IN_COLS // N_CHIP

    def body(x_ref, g_ref, w_ref, bre_ref, bim_ref, p_ref, h_ref, bur_ref, bui_ref):
        xv = x_ref[...]
        r, xh = _rms_stats(xv)
        h = (xh * g_ref[...]).astype(BF16)
        h_ref[...] = h
        for k in range(N_CHIP):
            p_ref[:, k * cs:(k + 1) * cs] = jnp.dot(h, w_ref[k],
                                                    preferred_element_type=F32).astype(BF16)
        u = p_ref[:, 0:SSM_W]
        for i in range(SSM_W // LANE):
            rows, cols = slice(i * LANE, (i + 1) * LANE), slice(i * DIAG_N, (i + 1) * DIAG_N)
            bur_ref[:, cols] = jnp.dot(u[:, rows], bre_ref[rows, cols],
                                       preferred_element_type=F32).astype(BF16)
            bui_ref[:, cols] = jnp.dot(u[:, rows], bim_ref[rows, cols],
                                       preferred_element_type=F32).astype(BF16)

    tb = min(TB_WIDE, t_len)
    return _call(
        body, "fwd_in", (t_len // tb,),
        [_rows(tb, D_MODEL), _whole(), _whole(), _whole(), _whole()],
        [_rows(tb, IN_COLS), _rows(tb, D_MODEL), _rows(tb, N_STATE), _rows(tb, N_STATE)],
        [jax.ShapeDtypeStruct((t_len, IN_COLS), BF16), jax.ShapeDtypeStruct((t_len, D_MODEL), BF16),
         jax.ShapeDtypeStruct((t_len, N_STATE), BF16), jax.ShapeDtypeStruct((t_len, N_STATE), BF16)],
        [x, g_mix, w_in, bre, bim], jobs)


def _scan_local(xr, xi, tab, shifts):
    for q, s in enumerate(shifts):
        ar, ai = tab[2 * q], tab[2 * q + 1]
        rr = pltpu.roll(xr, s, 0)
        ri = pltpu.roll(xi, s, 0)
        xr, xi = xr + ar * rr - ai * ri, xi + ar * ri + ai * rr
    return xr, xi


def _scan_carry(xr, xi, tab, cr, ci):
    pr, pi = tab[6], tab[7]
    return xr + pr * cr - pi * ci, xi + pr * ci + pi * cr


BF16_TILE = 2 * SUBLANE


def _load_blocks(r_ref, i_ref, base):
    out = []
    for q in range(SCAN_UNROLL // 2):
        rows = pl.ds(pl.multiple_of(base + q * BF16_TILE, BF16_TILE), BF16_TILE)
        vr, vi = r_ref[rows, :].astype(F32), i_ref[rows, :].astype(F32)
        out += [(vr[:SUBLANE], vi[:SUBLANE]), (vr[SUBLANE:], vi[SUBLANE:])]
    return out


def _store_blocks(r_ref, i_ref, base, blocks):
    for q in range(SCAN_UNROLL // 2):
        rows = pl.ds(pl.multiple_of(base + q * BF16_TILE, BF16_TILE), BF16_TILE)
        r_ref[rows, :] = jnp.concatenate([blocks[2 * q][0], blocks[2 * q + 1][0]], 0).astype(r_ref.dtype)
        i_ref[rows, :] = jnp.concatenate([blocks[2 * q][1], blocks[2 * q + 1][1]], 0).astype(i_ref.dtype)


def _scan_fwd(bur, bui, tab, jobs=()):
    t_len = bur.shape[0]
    nblk = t_len // SUBLANE
    lb = SCAN_LANES

    def body(br_ref, bi_ref, tab_ref, sr_ref, si_ref):
        tab_v = [tab_ref[q] for q in range(8)]

        def step(k, carry):
            cr, ci = carry
            base = pl.multiple_of(k * SCAN_UNROLL * SUBLANE, SCAN_UNROLL * SUBLANE)
            local = [_scan_local(xr, xi, tab_v, (1, 2, 4))
                     for xr, xi in _load_blocks(br_ref, bi_ref, base)]
            done = []
            for xr, xi in local:
                xr, xi = _scan_carry(xr, xi, tab_v, cr, ci)
                done.append((xr, xi))
                cr, ci = xr[SUBLANE - 1:SUBLANE, :], xi[SUBLANE - 1:SUBLANE, :]
            _store_blocks(sr_ref, si_ref, base, done)
            return cr, ci

        zero = jnp.zeros((1, lb), F32)
        lax.fori_loop(0, nblk // SCAN_UNROLL, step, (zero, zero))

    col = pl.BlockSpec((t_len, lb), lambda j: (0, j))
    return _call(
        body, "scan_fwd", (N_STATE // lb,),
        [col, col, pl.BlockSpec((8, SUBLANE, lb), lambda j: (0, 0, j))], [col, col],
        [jax.ShapeDtypeStruct((t_len, N_STATE), BF16)] * 2, [bur, bui, tab], jobs)


def _sgu_mix(v, ws_ref, lane_lo):
    rows = []
    for c0 in range(0, v.shape[0], CHUNK):
        slabs = []
        for j in range(SGU_W // LANE):
            prod = jnp.dot(ws_ref[j], v[c0:c0 + CHUNK, j * LANE:(j + 1) * LANE].astype(BF16),
                           preferred_element_type=F32)
            slabs.append(jnp.where(lane_lo, prod[:CHUNK], prod[CHUNK:]))
        rows.append(jnp.concatenate(slabs, axis=1))
    return jnp.concatenate(rows, axis=0) if len(rows) > 1 else rows[0]


def _fwd_mix(x, p, str_, sti, cre, cim, d_skip, w_glu, b_glu, w_pa, g_sgu, ws_st, bmat, w_pb, w_out,
             jobs=()):
    t_len = x.shape[0]

    def body(x_ref, p_ref, sr_ref, si_ref, cre_ref, cim_ref, dsk_ref, wg_ref, bg_ref, wpa_ref,
             gs_ref, ws_ref, bm_ref, wpb_ref, wo_ref,
             x2_ref, y0_ref, z_ref, mx_ref, ya_ref, yb_ref):
        u = p_ref[:, 0:SSM_W].astype(F32)
        y0 = jnp.concatenate(
            [_dot(sr_ref[:, i * DIAG_N:(i + 1) * DIAG_N],
                  cre_ref[i * DIAG_N:(i + 1) * DIAG_N, i * LANE:(i + 1) * LANE])
             - _dot(si_ref[:, i * DIAG_N:(i + 1) * DIAG_N],
                    cim_ref[i * DIAG_N:(i + 1) * DIAG_N, i * LANE:(i + 1) * LANE])
             for i in range(SSM_W // LANE)], axis=1) + dsk_ref[...] * u
        y0_ref[...] = y0.astype(BF16)
        y1 = _gelu(y0)
        z = _dot(y1, wg_ref[...]) + bg_ref[...]
        z_ref[...] = z.astype(BF16)
        ya_pre = (y1 * _sigmoid(z)).astype(BF16)
        ya = jnp.concatenate([jnp.dot(ya_pre, wpa_ref[k], preferred_element_type=F32)
                              for k in range(N_CHIP)], axis=1)
        ya_ref[...] = ya.astype(BF16)

        uvg = _gelu(p_ref[:, SSM_W:SSM_W + 2 * SGU_W].astype(F32))
        u2 = uvg[:, :SGU_W]
        _, vh = _rms_stats(uvg[:, SGU_W:])
        v3 = vh * gs_ref[...]
        lane_lo = lax.broadcasted_iota(jnp.int32, (CHUNK, LANE), 1) < SGU_D
        bias = jnp.concatenate([bm_ref[...]] * (TB // CHUNK), axis=0)
        mixed = _sgu_mix(v3, ws_ref, lane_lo) + bias
        mx_ref[...] = mixed.astype(BF16)
        sgu = (u2 * mixed).astype(BF16)
        yb = jnp.concatenate([jnp.dot(sgu, wpb_ref[k], preferred_element_type=F32)
                              for k in range(N_CHIP)], axis=1)
        yb_ref[...] = yb.astype(BF16)

        lg0 = SSM_W + 2 * SGU_W
        ga = _sigmoid(p_ref[:, lg0:lg0 + D_MODEL].astype(F32))
        gb = _sigmoid(p_ref[:, lg0 + D_MODEL:lg0 + 2 * D_MODEL].astype(F32))
        mrg = ga * ya + gb * yb
        x2_ref[...] = x_ref[...] + _dot(mrg, wo_ref[...])

    return _call(
        body, "fwd_mix", (t_len // TB,),
        [_rows(TB, D_MODEL), _rows(TB, IN_COLS), _rows(TB, N_STATE), _rows(TB, N_STATE)]
        + [_whole()] * 11,
        [_rows(TB, D_MODEL), _rows(TB, SSM_W), _rows(TB, SSM_W), _rows(TB, SGU_W),
         _rows(TB, D_MODEL), _rows(TB, D_MODEL)],
        [jax.ShapeDtypeStruct((t_len, D_MODEL), F32), jax.ShapeDtypeStruct((t_len, SSM_W), BF16),
         jax.ShapeDtypeStruct((t_len, SSM_W), BF16), jax.ShapeDtypeStruct((t_len, SGU_W), BF16),
         jax.ShapeDtypeStruct((t_len, D_MODEL), BF16), jax.ShapeDtypeStruct((t_len, D_MODEL), BF16)],
        [x, p, str_, sti, cre, cim, d_skip, w_glu, b_glu, w_pa, g_sgu, ws_st, bmat, w_pb, w_out], jobs)


def _conv_taps(v, cw_ref, c0, width):
    w0 = cw_ref[0:1, c0:c0 + width]
    w1 = cw_ref[1:2, c0:c0 + width]
    w2 = cw_ref[2:3, c0:c0 + width]
    return w0 * pltpu.roll(v, 2, 0) + w1 * pltpu.roll(v, 1, 0) + w2 * v


def _fwd_ffn(x2, target, g_ffn, w_up, conv_w, conv_b, w_down, g_final):
    t_len = x2.shape[0]
    half = D_FF // 2
    blocks_per_halo = TB // HALO

    def body(x2_ref, xp_ref, tg_ref, gf_ref, wu_ref, cw_ref, cb_ref, wd_ref, gl_ref,
             up_ref, act_ref, f_ref, h2_ref, dx3_ref, sm_ref):
        i = pl.program_id(0)

        @pl.when(i == 0)
        def _():
            sm_ref[...] = jnp.zeros_like(sm_ref)

        xe = jnp.concatenate([xp_ref[...] * jnp.where(i == 0, 0.0, 1.0), x2_ref[...]], axis=0)
        _, xh = _rms_stats(xe)
        h2 = (xh * gf_ref[...]).astype(BF16)
        h2_ref[...] = h2[HALO:]
        acc = jnp.zeros((TB, D_MODEL), F32)
        ups = [jnp.dot(h2, wu_ref[k], preferred_element_type=F32) for k in range(N_CHIP)]
        for hc in range(2):
            ca = hc * half
            cb = D_FF + hc * half
            ua, ub = ups[hc], ups[2 + hc]
            up_ref[:, ca:ca + half] = ua[HALO:].astype(BF16)
            up_ref[:, cb:cb + half] = ub[HALO:].astype(BF16)
            ac = _conv_taps(ua, cw_ref, ca, half)[HALO:] + cb_ref[:, ca:ca + half]
            bc = _conv_taps(ub, cw_ref, cb, half)[HALO:] + cb_ref[:, cb:cb + half]
            act_ref[:, ca:ca + half] = ac.astype(BF16)
            act_ref[:, cb:cb + half] = bc.astype(BF16)
            f = (ac * _sigmoid(ac) * bc).astype(BF16)
            f_ref[:, ca:ca + half] = f
            acc = acc + jnp.dot(f, wd_ref[ca:ca + half, :], preferred_element_type=F32)
        x3 = x2_ref[...] + acc
        r3, xh3 = _rms_stats(x3)
        err = xh3 * gl_ref[...] - tg_ref[...]
        dout = err * (1.0 / D_MODEL)
        dx3_ref[...] = _rms_bwd(dout * gl_ref[...], xh3, r3)
        dgl = jnp.sum(dout * xh3, axis=0, keepdims=True)
        loss = 0.5 * jnp.sum(jnp.mean(err * err, axis=-1, keepdims=True), axis=0, keepdims=True)
        upd = jnp.concatenate([dgl, jnp.broadcast_to(loss, (1, D_MODEL)),
                               jnp.zeros((SUBLANE - 2, D_MODEL), F32)], axis=0)

        sm_ref[...] += upd

    prev = pl.BlockSpec((HALO, D_MODEL), lambda i: (jnp.maximum(i * blocks_per_halo - 1, 0), 0))
    return _call(
        body, "fwd_ffn", (t_len // TB,),
        [_rows(TB, D_MODEL), prev, _rows(TB, D_MODEL)] + [_whole()] * 6,
        [_rows(TB, 2 * D_FF), _rows(TB, 2 * D_FF), _rows(TB, D_FF), _rows(TB, D_MODEL),
         _rows(TB, D_MODEL), _acc(SUBLANE, D_MODEL)],
        [jax.ShapeDtypeStruct((t_len, 2 * D_FF), BF16), jax.ShapeDtypeStruct((t_len, 2 * D_FF), BF16),
         jax.ShapeDtypeStruct((t_len, D_FF), BF16), jax.ShapeDtypeStruct((t_len, D_MODEL), BF16),
         jax.ShapeDtypeStruct((t_len, D_MODEL), F32), jax.ShapeDtypeStruct((SUBLANE, D_MODEL), F32)],
        [x2, x2, target, g_ffn, w_up, conv_w, conv_b, w_down, g_final])[0]


def _bwd_ffn(dx3, up, act, x2, g_ffn, w_up, conv_w, w_down, jobs=()):
    t_len = x2.shape[0]
    half = D_FF // 2
    nblk = t_len // TB
    halo_b = 2 * HALO
    n_e = TB + HALO

    def body(dx_ref, dxn_ref, up_ref, act_ref, actn_ref, x2_ref, gf_ref, wu_ref, cw_ref,
             wd_ref, dx2_ref, dup_ref, smw_ref, smg_ref):
        i = pl.program_id(0)

        @pl.when(i == 0)
        def _():
            smw_ref[...] = jnp.zeros_like(smw_ref)
            smg_ref[...] = jnp.zeros_like(smg_ref)

        keep_last = jnp.where(i == nblk - 1, 0.0, 1.0)
        dxe = jnp.concatenate([dx_ref[...], dxn_ref[...] * keep_last], axis=0).astype(BF16)
        dh2 = jnp.zeros((TB, D_MODEL), F32)
        zpad = jnp.zeros((1, half), F32)
        dfs = [lax.dot_general(dxe, wd_ref[hc * half:(hc + 1) * half, :], (((1,), (1,)), ((), ())),
                               preferred_element_type=F32) for hc in range(2)]
        for hc in range(2):
            ca = hc * half
            cb = D_FF + hc * half
            ac = jnp.concatenate([act_ref[:, ca:ca + half].astype(F32),
                                  actn_ref[:, ca:ca + half].astype(F32)[:HALO]], axis=0)
            bc = jnp.concatenate([act_ref[:, cb:cb + half].astype(F32),
                                  actn_ref[:, cb:cb + half].astype(F32)[:HALO]], axis=0)
            wa = [cw_ref[k:k + 1, ca:ca + half] for k in range(3)]
            wb = [cw_ref[k:k + 1, cb:cb + half] for k in range(3)]
            df = dfs[hc]
            sg = _sigmoid(ac)
            da = df * bc * sg * (1.0 + ac * (1.0 - sg))
            db = df * ac * sg
            da1, da2 = pltpu.roll(da, n_e - 1, 0), pltpu.roll(da, n_e - 2, 0)
            db1, db2 = pltpu.roll(db, n_e - 1, 0), pltpu.roll(db, n_e - 2, 0)
            dua = (wa[2] * da + wa[1] * da1 + wa[0] * da2)[:TB]
            dub = (wb[2] * db + wb[1] * db1 + wb[0] * db2)[:TB]
            dup_ref[:, ca:ca + half] = dua.astype(BF16)
            dup_ref[:, cb:cb + half] = dub.astype(BF16)
            dh2 = dh2 + _dot_nt(dua, wu_ref[hc]) + _dot_nt(dub, wu_ref[2 + hc])
            rows = []
            for u_, d0, d1, d2 in ((up_ref[:, ca:ca + half].astype(F32), da, da1, da2),
                                   (up_ref[:, cb:cb + half].astype(F32), db, db1, db2)):
                rows.append([jnp.sum(u_ * d2[:TB], axis=0, keepdims=True),
                             jnp.sum(u_ * d1[:TB], axis=0, keepdims=True),
                             jnp.sum(u_ * d0[:TB], axis=0, keepdims=True),
                             jnp.sum(d0[:TB], axis=0, keepdims=True)])
            for c0, rws in ((ca, rows[0]), (cb, rows[1])):
                upd = jnp.concatenate(rws + [zpad] * (SUBLANE - 4), axis=0)
                smw_ref[:, c0:c0 + half] += upd

        r2, xh2 = _rms_stats(x2_ref[...])
        dx2_ref[...] = dx_ref[...] + _rms_bwd(dh2 * gf_ref[...], xh2, r2)
        updg = jnp.concatenate([jnp.sum(dh2 * xh2, axis=0, keepdims=True),
                                jnp.zeros((SUBLANE - 1, D_MODEL), F32)], axis=0)

        smg_ref[...] += updg

    nxt_d = pl.BlockSpec((HALO, D_MODEL),
                         lambda i: (jnp.minimum((i + 1) * (TB // HALO), t_len // HALO - 1), 0))
    nxt_a = pl.BlockSpec((halo_b, 2 * D_FF),
                         lambda i: (jnp.minimum((i + 1) * (TB // halo_b), t_len // halo_b - 1), 0))
    return _call(
        body, "bwd_ffn", (nblk,),
        [_rows(TB, D_MODEL), nxt_d, _rows(TB, 2 * D_FF), _rows(TB, 2 * D_FF), nxt_a,
         _rows(TB, D_MODEL)] + [_whole()] * 4,
        [_rows(TB, D_MODEL), _rows(TB, 2 * D_FF), _acc(SUBLANE, 2 * D_FF), _acc(SUBLANE, D_MODEL)],
        [jax.ShapeDtypeStruct((t_len, D_MODEL), F32), jax.ShapeDtypeStruct((t_len, 2 * D_FF), BF16),
         jax.ShapeDtypeStruct((SUBLANE, 2 * D_FF), F32), jax.ShapeDtypeStruct((SUBLANE, D_MODEL), F32)],
        [dx3, dx3, up, act, act, x2, g_ffn, w_up, conv_w, w_down], jobs)


def _bwd_mix(dx2, p, y0, z, mixed, ya, yb, w_out, w_pa, w_pb, w_glu, cre, cim, ws_st, wst_st,
             d_skip, g_sgu, jobs=()):
    t_len = dx2.shape[0]
    pc = D_MODEL // N_CHIP
    n_slab = SGU_W // LANE

    def body(dx_ref, p_ref, y0_ref, z_ref, mx_ref, ya_ref, yb_ref, wo_ref, wpa_ref, wpb_ref,
             wg_ref, cre_ref, cim_ref, ws_ref, wst_ref, dsk_ref, gs_ref,
             dsr_ref, dsi_ref, du_ref, drest_ref, mrg_ref, dya_ref, dyb_ref, yap_ref, dz_ref,
             y1_ref, sgu_ref, dy0_ref, sm_ref, dbm_ref, dws_ref):
        @pl.when(pl.program_id(0) == 0)
        def _():
            sm_ref[...] = jnp.zeros_like(sm_ref)
            dbm_ref[...] = jnp.zeros_like(dbm_ref)
            dws_ref[...] = jnp.zeros_like(dws_ref)

        lg0 = SSM_W + 2 * SGU_W
        y0v = y0_ref[...].astype(F32)
        y1, y1_grad = _gelu_and_grad(y0v)
        sz = _sigmoid(z_ref[...].astype(F32))
        y1_ref[...] = y1.astype(BF16)
        yap_ref[...] = (y1 * sz).astype(BF16)

        dxb = dx_ref[...].astype(BF16)
        dyap = jnp.zeros((TB, SSM_W), F32)
        dsgu = jnp.zeros((TB, SGU_W), F32)
        wide = D_MODEL // 2
        for h in range(2):
            cols = slice(h * wide, (h + 1) * wide)
            ga = _sigmoid(p_ref[:, lg0 + h * wide:lg0 + (h + 1) * wide].astype(F32))
            gb = _sigmoid(
                p_ref[:, lg0 + D_MODEL + h * wide:lg0 + D_MODEL + (h + 1) * wide].astype(F32))
            yav = ya_ref[:, cols].astype(F32)
            ybv = yb_ref[:, cols].astype(F32)
            mrg_ref[:, cols] = (ga * yav + gb * ybv).astype(BF16)
            dmrg = _dot_nt(dxb, wo_ref[cols, :])
            drest_ref[:, 2 * SGU_W + h * wide:2 * SGU_W + (h + 1) * wide] = (
                dmrg * yav * ga * (1.0 - ga)).astype(BF16)
            drest_ref[:, 2 * SGU_W + D_MODEL + h * wide:2 * SGU_W + D_MODEL + (h + 1) * wide] = (
                dmrg * ybv * gb * (1.0 - gb)).astype(BF16)
            dya = (dmrg * ga).astype(BF16)
            dyb = (dmrg * gb).astype(BF16)
            dya_ref[:, cols] = dya
            dyb_ref[:, cols] = dyb
            for k in range(wide // pc):
                shard = h * (wide // pc) + k
                dyap = dyap + _dot_nt(dya[:, k * pc:(k + 1) * pc], wpa_ref[shard])
                dsgu = dsgu + _dot_nt(dyb[:, k * pc:(k + 1) * pc], wpb_ref[shard])

        dz = dyap * y1 * sz * (1.0 - sz)
        dz_ref[...] = dz.astype(BF16)
        dy0 = (dyap * sz + _dot_nt(dz, wg_ref[...])) * y1_grad
        dy0_ref[...] = dy0.astype(BF16)
        u = p_ref[:, 0:SSM_W].astype(F32)
        du_ref[...] = dy0 * dsk_ref[...]
        for q in range(SSM_W // LANE):
            rows, cols = slice(q * DIAG_N, (q + 1) * DIAG_N), slice(q * LANE, (q + 1) * LANE)
            dsr_ref[:, rows] = _dot_nt(dy0[:, cols], cre_ref[rows, cols]).astype(BF16)
            dsi_ref[:, rows] = (-_dot_nt(dy0[:, cols], cim_ref[rows, cols])).astype(BF16)

        u2, u_grad = _gelu_and_grad(p_ref[:, SSM_W:SSM_W + SGU_W].astype(F32))
        mixed = mx_ref[...].astype(F32)
        sgu_ref[...] = (u2 * mixed).astype(BF16)
        drest_ref[:, 0:SGU_W] = (dsgu * mixed * u_grad).astype(BF16)
        dmix = dsgu * u2
        v2, v_grad = _gelu_and_grad(p_ref[:, SSM_W + SGU_W:lg0].astype(F32))
        rv, vh = _rms_stats(v2)
        v3 = vh * gs_ref[...]
        lane_lo = lax.broadcasted_iota(jnp.int32, (CHUNK, LANE), 1) < SGU_D
        dv3 = _sgu_mix(dmix, wst_ref, lane_lo)
        dbm = jnp.zeros((CHUNK, SGU_W), F32)
        for c0 in range(0, TB, CHUNK):
            dbm = dbm + dmix[c0:c0 + CHUNK]
        for j in range(n_slab):
            lo = jnp.zeros((CHUNK, CHUNK), F32)
            hi = jnp.zeros((CHUNK, CHUNK), F32)
            for c0 in range(0, TB, CHUNK):
                dsl = dmix[c0:c0 + CHUNK, j * LANE:(j + 1) * LANE]
                vsl = v3[c0:c0 + CHUNK, j * LANE:(j + 1) * LANE]
                lo = lo + _dot_nt(jnp.where(lane_lo, dsl, 0.0), vsl)
                hi = hi + _dot_nt(jnp.where(lane_lo, 0.0, dsl), vsl)

            dws_ref[2 * j] += lo
            dws_ref[2 * j + 1] += hi

        dv2 = _rms_bwd(dv3 * gs_ref[...], vh, rv)
        drest_ref[:, SGU_W:2 * SGU_W] = (dv2 * v_grad).astype(BF16)

        upd = jnp.concatenate([jnp.sum(dy0 * u, axis=0, keepdims=True),
                               jnp.sum(dz, axis=0, keepdims=True),
                               jnp.sum(dv3 * vh, axis=0, keepdims=True),
                               jnp.zeros((SUBLANE - 3, SSM_W), F32)], axis=0)

        sm_ref[...] += upd
        dbm_ref[...] += dbm

    rest = 2 * SGU_W + 2 * D_MODEL
    bf_d, bf_s = jax.ShapeDtypeStruct((t_len, D_MODEL), BF16), jax.ShapeDtypeStruct((t_len, SSM_W), BF16)
    return _call(
        body, "bwd_mix", (t_len // TB,),
        [_rows(TB, D_MODEL), _rows(TB, IN_COLS), _rows(TB, SSM_W), _rows(TB, SSM_W),
         _rows(TB, SGU_W), _rows(TB, D_MODEL), _rows(TB, D_MODEL)] + [_whole()] * 10,
        [_rows(TB, N_STATE), _rows(TB, N_STATE), _rows(TB, SSM_W), _rows(TB, rest),
         _rows(TB, D_MODEL), _rows(TB, D_MODEL), _rows(TB, D_MODEL), _rows(TB, SSM_W),
         _rows(TB, SSM_W), _rows(TB, SSM_W), _rows(TB, SGU_W), _rows(TB, SSM_W),
         _acc(SUBLANE, SSM_W), _acc(CHUNK, SGU_W),
         pl.BlockSpec((SGU_G, CHUNK, CHUNK), lambda i: (0, 0, 0))],
        [jax.ShapeDtypeStruct((t_len, N_STATE), BF16), jax.ShapeDtypeStruct((t_len, N_STATE), BF16),
         jax.ShapeDtypeStruct((t_len, SSM_W), F32), jax.ShapeDtypeStruct((t_len, rest), BF16),
         bf_d, bf_d, bf_d, bf_s, bf_s, bf_s, bf_s, bf_s,
         jax.ShapeDtypeStruct((SUBLANE, SSM_W), F32), jax.ShapeDtypeStruct((CHUNK, SGU_W), F32),
         jax.ShapeDtypeStruct((SGU_G, CHUNK, CHUNK), F32)],
        [dx2, p, y0, z, mixed, ya, yb, w_out, w_pa, w_pb, w_glu, cre, cim, ws_st, wst_st, d_skip,
         g_sgu], jobs)


def _scan_bwd(dsr, dsi, str_, sti, tab_rev, jobs=()):
    t_len = dsr.shape[0]
    nblk = t_len // SUBLANE
    lb = SCAN_LANES

    def body(dr_ref, di_ref, sr_ref, si_ref, tab_ref, lr_ref, li_ref, dar_ref, dai_ref):
        tab_v = [tab_ref[q] for q in range(8)]
        row0 = lax.broadcasted_iota(jnp.int32, (SUBLANE, lb), 0) == 0
        tile = BF16_TILE

        def step(k, carry):
            cr, ci, acr, aci = carry
            base = pl.multiple_of((nblk - (k + 1) * SCAN_UNROLL) * SUBLANE, SCAN_UNROLL * SUBLANE)
            state = _load_blocks(sr_ref, si_ref, base)
            before = pl.ds(pl.multiple_of(jnp.maximum(base - tile, 0), tile), tile)
            has_before = jnp.where(base > 0, 1.0, 0.0)
            prev = (sr_ref[before, :].astype(F32)[tile - 1:tile] * has_before,
                    si_ref[before, :].astype(F32)[tile - 1:tile] * has_before)
            local = [_scan_local(xr, xi, tab_v, (7, 6, 4))
                     for xr, xi in _load_blocks(dr_ref, di_ref, base)]
            lam = [None] * SCAN_UNROLL
            for b in reversed(range(SCAN_UNROLL)):
                xr, xi = _scan_carry(*local[b], tab_v, cr, ci)
                lam[b] = (xr, xi)
                cr, ci = xr[0:1, :], xi[0:1, :]
                pr, pi = prev if b == 0 else (state[b - 1][0][SUBLANE - 1:], state[b - 1][1][SUBLANE - 1:])
                s_r = jnp.where(row0, pr, pltpu.roll(state[b][0], 1, 0))
                s_i = jnp.where(row0, pi, pltpu.roll(state[b][1], 1, 0))
                acr = acr + xr * s_r + xi * s_i
                aci = aci + xi * s_r - xr * s_i
            _store_blocks(lr_ref, li_ref, base, lam)
            return cr, ci, acr, aci

        zero = jnp.zeros((1, lb), F32)
        zacc = jnp.zeros((SUBLANE, lb), F32)
        _, _, acr, aci = lax.fori_loop(0, nblk // SCAN_UNROLL, step, (zero, zero, zacc, zacc))
        dar_ref[...] = acr
        dai_ref[...] = aci

    col = pl.BlockSpec((t_len, lb), lambda j: (0, j))
    small = pl.BlockSpec((SUBLANE, lb), lambda j: (0, j))
    return _call(
        body, "scan_bwd", (N_STATE // lb,),
        [col, col, col, col, pl.BlockSpec((8, SUBLANE, lb), lambda j: (0, 0, j))],
        [col, col, small, small],
        [jax.ShapeDtypeStruct((t_len, N_STATE), BF16)] * 2
        + [jax.ShapeDtypeStruct((SUBLANE, N_STATE), F32)] * 2,
        [dsr, dsi, str_, sti, tab_rev], jobs)


def _bwd_in(lam_r, lam_i, du_part, drest, x, dx2, g_mix, w_in, bre, bim, jobs=()):
    t_len = x.shape[0]
    cs = IN_COLS // N_CHIP

    def body(lr_ref, li_ref, du_ref, dr_ref, x_ref, dx2_ref, g_ref, w_ref, bre_ref, bim_ref,
             gx_ref, dp_ref, sm_ref):
        @pl.when(pl.program_id(0) == 0)
        def _():
            sm_ref[...] = jnp.zeros_like(sm_ref)

        du = du_ref[...] + jnp.concatenate(
            [_dot_nt(lr_ref[:, i * DIAG_N:(i + 1) * DIAG_N],
                     bre_ref[i * LANE:(i + 1) * LANE, i * DIAG_N:(i + 1) * DIAG_N])
             + _dot_nt(li_ref[:, i * DIAG_N:(i + 1) * DIAG_N],
                       bim_ref[i * LANE:(i + 1) * LANE, i * DIAG_N:(i + 1) * DIAG_N])
             for i in range(SSM_W // LANE)], axis=1)
        dp_ref[:, 0:SSM_W] = du.astype(BF16)
        dp_ref[:, SSM_W:] = dr_ref[...]
        dh = jnp.zeros(x_ref.shape, F32)
        for k in range(N_CHIP):
            dh = dh + _dot_nt(dp_ref[:, k * cs:(k + 1) * cs], w_ref[k])
        r, xh = _rms_stats(x_ref[...])
        gx_ref[...] = dx2_ref[...] + _rms_bwd(dh * g_ref[...], xh, r)
        upd = jnp.concatenate([jnp.sum(dh * xh, axis=0, keepdims=True),
                               jnp.zeros((SUBLANE - 1, D_MODEL), F32)], axis=0)

        sm_ref[...] += upd

    tb = min(TB_WIDE, t_len)
    return _call(
        body, "bwd_in", (t_len // tb,),
        [_rows(tb, N_STATE), _rows(tb, N_STATE), _rows(tb, SSM_W), _rows(tb, IN_COLS - SSM_W),
         _rows(tb, D_MODEL), _rows(tb, D_MODEL)] + [_whole()] * 4,
        [_rows(tb, D_MODEL), _rows(tb, IN_COLS), _acc(SUBLANE, D_MODEL)],
        [jax.ShapeDtypeStruct((t_len, D_MODEL), F32), jax.ShapeDtypeStruct((t_len, IN_COLS), BF16),
         jax.ShapeDtypeStruct((SUBLANE, D_MODEL), F32)],
        [lam_r, lam_i, du_part, drest, x, dx2, g_mix, w_in, bre, bim], jobs)


def _matmul_tn(a, b, name, out_shape, grid_ij, a_blk, a_map, b_blk, b_map, o_blk, o_map, jobs=()):
    tk = a_blk[0]
    nk = a.shape[0] // tk
    assert nk * tk == a.shape[0] and nk > 0

    def body(a_ref, b_ref, o_ref, acc_ref):
        k = pl.program_id(2)

        @pl.when(k == 0)
        def _():
            acc_ref[...] = jnp.zeros_like(acc_ref)

        acc_ref[...] += lax.dot_general(a_ref[...].astype(BF16), b_ref[...].astype(BF16),
                                        (((0,), (0,)), ((), ())), preferred_element_type=F32)

        @pl.when(k == nk - 1)
        def _():
            o_ref[...] = acc_ref[...]

    outs, per_job = _call(
        body, name, (grid_ij[0], grid_ij[1], nk),
        [pl.BlockSpec(a_blk, a_map), pl.BlockSpec(b_blk, b_map)], [pl.BlockSpec(o_blk, o_map)],
        [jax.ShapeDtypeStruct(out_shape, F32)], [a, b], jobs,
        scratch=[pltpu.VMEM((a_blk[1], b_blk[1]), F32)])
    return outs[0], per_job


def _dw_rows(a, b, name, tm, tk):
    m, n = a.shape[1], b.shape[1]
    tk = min(tk, a.shape[0])
    return _matmul_tn(a, b, name, (m, n), (m // tm, 1),
                      (tk, tm), lambda i, j, k: (k, i), (tk, n), lambda i, j, k: (k, 0),
                      (tm, n), lambda i, j, k: (i, 0))[0]


def _dw_cols(a, b, name, tn, sharded, jobs=()):
    t_len, m = a.shape
    n = b.shape[1]

    def body(a_ref, b_ref, o_ref):
        o_ref[...] = lax.dot_general(a_ref[...].astype(BF16), b_ref[...].astype(BF16),
                                     (((0,), (0,)), ((), ())), preferred_element_type=F32)

    if sharded:
        o_spec, o_shape = pl.BlockSpec((None, m, tn), lambda j: (j, 0, 0)), (n // tn, m, tn)
    else:
        o_spec, o_shape = pl.BlockSpec((m, tn), lambda j: (0, j)), (m, n)
    outs, per_job = _call(body, name, (n // tn,),
                          [_whole(), pl.BlockSpec((t_len, tn), lambda j: (0, j))], [o_spec],
                          [jax.ShapeDtypeStruct(o_shape, F32)], [a, b], jobs)
    return outs[0], per_job


def _dw_tiles(a, b, name, tm, tn, jobs=()):
    t_len, m = a.shape
    n = b.shape[1]

    def body(a_ref, b_ref, o_ref):
        o_ref[...] = lax.dot_general(a_ref[...].astype(BF16), b_ref[...].astype(BF16),
                                     (((0,), (0,)), ((), ())), preferred_element_type=F32)

    outs, per_job = _call(body, name, (n // tn, m // tm),
                          [pl.BlockSpec((t_len, tm), lambda j, i: (0, i)),
                           pl.BlockSpec((t_len, tn), lambda j, i: (0, j))],
                          [pl.BlockSpec((None, tm, tn), lambda j, i: (j, i, 0))],
                          [jax.ShapeDtypeStruct((n // tn, m, tn), F32)], [a, b], jobs)
    return outs[0], per_job


def _dw_pair(a, m, b1, b2, name, jobs=()):
    t_len = a.shape[0]
    n_slab = DIAG_N // LANE
    rows_per_slab = LANE // n_slab

    def body(a_ref, b1_ref, b2_ref, o1_ref, o2_ref):
        for b_ref, o_ref in ((b1_ref, o1_ref), (b2_ref, o2_ref)):
            prod = lax.dot_general(a_ref[...].astype(BF16), b_ref[...].astype(BF16),
                                   (((0,), (0,)), ((), ())), preferred_element_type=F32)
            for j in range(n_slab):
                rows = slice(j * rows_per_slab, (j + 1) * rows_per_slab)
                o_ref[rows, :] = prod[rows, j * LANE:(j + 1) * LANE]

    tok = pl.BlockSpec((t_len, DIAG_N), lambda i: (0, i))
    out = pl.BlockSpec((LANE, LANE), lambda i: (i, 0))
    return _call(body, name, (m // LANE,),
                 [pl.BlockSpec((t_len, LANE), lambda i: (0, i)), tok, tok], [out, out],
                 [jax.ShapeDtypeStruct((m, LANE), F32)] * 2, [a, b1, b2], jobs)


def _prefetch_call(body, name, grid, scalars, in_specs, out_specs, out_shape, args):
    return pl.pallas_call(
        body, name=name,
        grid_spec=pltpu.PrefetchScalarGridSpec(num_scalar_prefetch=1, grid=grid, in_specs=in_specs,
                                               out_specs=out_specs),
        out_shape=out_shape, compiler_params=_params(len(grid)),
    )(scalars, *args)


def _place_shard(w, where, name, dtype, tr):
    rows, cols = w.shape

    def body(s_ref, w_ref, o_ref):
        o_ref[...] = w_ref[...].astype(dtype)

    return _prefetch_call(
        body, name, (rows // tr,), where,
        [pl.BlockSpec((tr, cols), lambda i, s: (i, 0))],
        pl.BlockSpec((None, tr, cols), lambda i, s: (s[0], i, 0)),
        jax.ShapeDtypeStruct((N_CHIP, rows, cols), dtype), [w])


def _place_shards(ws, where, name, dtype):
    n = len(ws)

    def body(s_ref, *refs):
        for t in range(n):
            refs[n + t][...] = refs[t][...].astype(dtype)

    return _prefetch_call(
        body, name, (1,), where,
        [pl.BlockSpec(w.shape, lambda i, s: (0, 0)) for w in ws],
        [pl.BlockSpec((None,) + w.shape, lambda i, s: (s[0], 0, 0)) for w in ws],
        [jax.ShapeDtypeStruct((N_CHIP,) + w.shape, dtype) for w in ws], ws)


def _add_sibling(gs, gots, where, name):
    n = len(gs)
    halves = [(g.shape[1] // 2, g.shape[2]) for g in gs]

    def body(s_ref, *refs):
        for t in range(n):
            refs[2 * n + t][...] = (refs[t][...] + refs[n + t][...]).astype(BF16)

    return _prefetch_call(
        body, name, (N_CHIP,), where,
        [pl.BlockSpec((None, hr, cs), lambda k, s: (k, s[1], 0)) for hr, cs in halves]
        + [pl.BlockSpec((None, hr, cs), lambda k, s: (k, 0, 0)) for hr, cs in halves],
        [pl.BlockSpec((None, hr, cs), lambda k, s: (k, 0, 0)) for hr, cs in halves],
        [jax.ShapeDtypeStruct((N_CHIP, hr, cs), BF16) for hr, cs in halves], list(gs) + list(gots))


def _add_chips(sums, gots, where, name):
    n = len(sums)
    halves = [s.shape[1:] for s in sums]

    def body(s_ref, *refs):
        for t in range(n):
            own_ref, got_ref = refs[t], refs[n + t]
            refs[2 * n + t][...] = ((own_ref[...].astype(F32) + got_ref[0].astype(F32))
                                    + got_ref[1].astype(F32)) + got_ref[2].astype(F32)

    return _prefetch_call(
        body, name, (1,), where,
        [pl.BlockSpec((None, hr, cs), lambda i, s: (s[0], 0, 0)) for hr, cs in halves]
        + [pl.BlockSpec((3, hr, cs), lambda i, s: (0, 0, 0)) for hr, cs in halves],
        [pl.BlockSpec((hr, cs), lambda i, s: (s[1], 0)) for hr, cs in halves],
        [jax.ShapeDtypeStruct((2 * hr, cs), F32) for hr, cs in halves], list(sums) + list(gots))


def _small_allreduce(pack):
    rows = pack.shape[0]
    half = rows // 2

    def body(in_ref, out_ref, sib_ref, slots_ref, s_a, r_a, s_b, r_b, s_c, r_c):
        x, y, c, chips = _place()
        k_me = 2 * x + y
        sib = (x, y, 1 - c)
        mine, theirs = _half(rows, c), _half(rows, 1 - c)
        first = _remote(in_ref.at[theirs, :], sib_ref.at[theirs, :], s_a, r_a, sib)
        first.start()
        first.wait_send()
        landed = sib_ref.at[mine, :]
        _remote(landed, landed, s_a, r_a, sib).wait_recv()
        slots_ref[k_me] = in_ref[mine, :] + sib_ref[mine, :]
        cps = [_remote(slots_ref.at[k_me], slots_ref.at[k_me], s_b.at[j], r_b.at[j], (*ch, c))
               for j, ch in enumerate(chips)]
        for cp in cps:
            cp.start()
        for j, ch in enumerate(chips):
            slot = slots_ref.at[_chip_index(ch)]
            _remote(slot, slot, s_b.at[j], r_b.at[j], (*ch, c)).wait_recv()
        for cp in cps:
            cp.wait_send()
        out_ref[mine, :] = ((slots_ref[0] + slots_ref[1]) + slots_ref[2]) + slots_ref[3]
        last = _remote(out_ref.at[mine, :], out_ref.at[mine, :], s_c, r_c, sib)
        last.start()
        other = out_ref.at[theirs, :]
        _remote(other, other, s_c, r_c, sib).wait_recv()
        last.wait_send()

    return pl.pallas_call(
        body, name="small_allreduce", in_specs=[_whole()], out_specs=_whole(),
        out_shape=jax.ShapeDtypeStruct(pack.shape, F32),
        scratch_shapes=[pltpu.VMEM(pack.shape, F32), pltpu.VMEM((N_CHIP, half, LANE), F32),
                        pltpu.SemaphoreType.DMA, pltpu.SemaphoreType.DMA,
                        pltpu.SemaphoreType.DMA((3,)), pltpu.SemaphoreType.DMA((3,)),
                        pltpu.SemaphoreType.DMA, pltpu.SemaphoreType.DMA],
        compiler_params=_params(0),
    )(pack)


def _adamw_update(w_ref, g_ref, m_ref, v_ref, d_ref, mo_ref, vo_ref):
    gv = g_ref[...]
    mn = ADAM_B1 * m_ref[...] + (1.0 - ADAM_B1) * gv
    vn = ADAM_B2 * v_ref[...] + (1.0 - ADAM_B2) * (gv * gv)
    mo_ref[...] = mn
    vo_ref[...] = vn
    m_hat = mn / (1.0 - ADAM_B1 ** ADAM_STEP)
    v_hat = vn / (1.0 - ADAM_B2 ** ADAM_STEP)
    d_ref[...] = -ADAM_LR * (m_hat / (jnp.sqrt(v_hat) + ADAM_EPS) + ADAM_WD * w_ref[...])


def _adamw(w, g, m, v, name, tr):
    rows, cols = w.shape
    blk = _rows(tr, cols)

    def body(w_ref, g_ref, m_ref, v_ref, go_ref, d_ref, mo_ref, vo_ref):
        go_ref[...] = g_ref[...]
        _adamw_update(w_ref, g_ref, m_ref, v_ref, d_ref, mo_ref, vo_ref)

    return _call(body, name, (rows // tr,), [blk] * 4, [blk] * 4,
                 [jax.ShapeDtypeStruct(w.shape, F32)] * 4, [w, g, m, v])[0]


def _adamw_many(ws, gs, ms, vs, name):
    n = len(ws)

    def body(*refs):
        for t in range(n):
            _adamw_update(*[refs[q * n + t] for q in range(7)])

    specs = [pl.BlockSpec(a.shape, lambda i, nd=a.ndim: (0,) * nd) for a in ws]
    outs = pl.pallas_call(
        body, name=name, grid=(1,), in_specs=specs * 4, out_specs=specs * 3,
        out_shape=[jax.ShapeDtypeStruct(a.shape, F32) for _ in range(3) for a in ws],
        compiler_params=_params(1),
    )(*ws, *gs, *ms, *vs)
    return outs[:n], outs[n:2 * n], outs[2 * n:]


def _ssm_discretize(a_re, a_im, log_dt, b_re, b_im):
    dt = jnp.exp(log_dt)[:, None]
    mag = jnp.exp(dt * a_re)
    abr = mag * jnp.cos(dt * a_im)
    abi = mag * jnp.sin(dt * a_im)
    den = a_re * a_re + a_im * a_im
    nr = abr - 1.0
    ni = abi
    f_re = (nr * a_re + ni * a_im) / den
    f_im = (ni * a_re - nr * a_im) / den
    bbr = f_re[..., None] * b_re - f_im[..., None] * b_im
    bbi = f_re[..., None] * b_im + f_im[..., None] * b_re
    return abr, abi, bbr, bbi


def _scan_tables(abr, abi):
    ar = abr.reshape(1, N_STATE)
    ai = abi.reshape(1, N_STATE)
    pr, pi = [ar], [ai]
    for _ in range(SUBLANE - 1):
        pr, pi = pr + [pr[-1] * ar - pi[-1] * ai], pi + [pr[-1] * ai + pi[-1] * ar]
    row = jnp.arange(SUBLANE)[:, None]
    tabs = []
    for d in (1, 2, 4):
        tabs.append(jnp.where(row >= d, pr[d - 1], 0.0))
        tabs.append(jnp.where(row >= d, pi[d - 1], 0.0))
    tabs.append(jnp.concatenate(pr, axis=0))
    tabs.append(jnp.concatenate(pi, axis=0))
    fwd = jnp.stack(tabs)
    sign = jnp.array([1.0, -1.0] * 4, F32)[:, None, None]
    return fwd, fwd[:, ::-1, :] * sign


def _block_diag_b(bb):
    strip = bb.transpose(2, 0, 1).reshape(SSM_H, N_STATE)
    rows = lax.broadcasted_iota(jnp.int32, (SSM_W, N_STATE), 0) // SSM_H
    cols = lax.broadcasted_iota(jnp.int32, (SSM_W, N_STATE), 1) // SSM_P
    return jnp.where(rows == cols, jnp.tile(strip, (SSM_G, 1)), 0.0).astype(BF16)


def _block_diag_c(cc):
    strip = cc.transpose(0, 2, 1).reshape(N_STATE, SSM_H)
    rows = lax.broadcasted_iota(jnp.int32, (N_STATE, SSM_W), 0) // SSM_P
    cols = lax.broadcasted_iota(jnp.int32, (N_STATE, SSM_W), 1) // SSM_H
    return jnp.where(rows == cols, jnp.tile(strip, (1, SSM_G)), 0.0).astype(BF16)


SMALL_SHAPES = {
    "g_mix": (D_MODEL,), "a_re": (SSM_G, SSM_P), "a_im": (SSM_G, SSM_P), "log_dt": (SSM_G,),
    "b_re": (SSM_G, SSM_P, SSM_H), "b_im": (SSM_G, SSM_P, SSM_H),
    "c_re": (SSM_G, SSM_H, SSM_P), "c_im": (SSM_G, SSM_H, SSM_P),
    "d_skip": (SSM_W,), "b_glu": (SSM_W,), "g_sgu": (SGU_W,), "w_s": (SGU_G, CHUNK, CHUNK),
    "b_s": (SGU_G, CHUNK), "g_ffn": (D_MODEL,), "conv_b": (2 * D_FF,), "g_final": (D_MODEL,),
}
PACK_ITEMS = [("loss", (1,))] + [(n, SMALL_SHAPES[n]) for n in SMALL] + [("conv_w", (3, 2 * D_FF))]
TILE = SUBLANE * LANE


def _item_rows(shape):
    return -(-math.prod(shape) // TILE) * SUBLANE


PACK_ROWS = -(-sum(_item_rows(s) for _, s in PACK_ITEMS) // (2 * SUBLANE)) * (2 * SUBLANE)


def _pack(values):
    parts, used = [], 0
    for name, shape in PACK_ITEMS:
        size, rows = math.prod(shape), _item_rows(shape)
        if name in values:
            flat = values[name].astype(F32).reshape(size)
            if rows * LANE > size:
                flat = jnp.pad(flat, (0, rows * LANE - size))
            parts.append(flat.reshape(rows, LANE))
        else:
            parts.append(jnp.zeros((rows, LANE), F32))
        used += rows
    if PACK_ROWS > used:
        parts.append(jnp.zeros((PACK_ROWS - used, LANE), F32))
    return jnp.concatenate(parts, axis=0)


def _unpack(pack):
    out, off = {}, 0
    for name, shape in PACK_ITEMS:
        rows = _item_rows(shape)
        out[name] = pack[off:off + rows].reshape(rows * LANE)[:math.prod(shape)].reshape(shape)
        off += rows
    return out


PLACE_ROWS = {"w_in": 256, "w_up": 256, "w_down": 352, "w_out": 256, "w_proj_a": 256,
              "w_proj_b": 256, "w_glu": 128}


def kernel(x, g_mix, w_in, a_re, a_im, log_dt, b_re, b_im, c_re, c_im, d_skip, w_glu, b_glu, w_proj_a, g_sgu, w_s, b_s, w_proj_b, w_out, g_ffn, w_up, conv_w, conv_b, w_down, g_final, loss_target, m_g_mix, m_w_in, m_a_re, m_a_im, m_log_dt, m_b_re, m_b_im, m_c_re, m_c_im, m_d_skip, m_w_glu, m_b_glu, m_w_proj_a, m_g_sgu, m_w_s, m_b_s, m_w_proj_b, m_w_out, m_g_ffn, m_w_up, m_conv_w, m_conv_b, m_w_down, m_g_final, v_g_mix, v_w_in, v_a_re, v_a_im, v_log_dt, v_b_re, v_b_im, v_c_re, v_c_im, v_d_skip, v_w_glu, v_b_glu, v_w_proj_a, v_g_sgu, v_w_s, v_b_s, v_w_proj_b, v_w_out, v_g_ffn, v_w_up, v_conv_w, v_conv_b, v_w_down, v_g_final):
    given = dict(locals())
    w = {n: given[n] for n in WEIGHTS}
    m = {n: given["m_" + n] for n in WEIGHTS}
    v = {n: given["v_" + n] for n in WEIGHTS}

    def shard2d(a):
        return a.reshape(a.shape[-2], a.shape[-1])

    chip = 2 * lax.axis_index("x") + lax.axis_index("y")
    where = jnp.stack([chip, lax.axis_index("c")]).astype(jnp.int32)
    xs, target = x[0], loss_target[0]
    small = {n: w[n].reshape(SMALL_SHAPES[n]) for n in SMALL}

    (abr, abi, bbr, bbi), disc_vjp = jax.vjp(_ssm_discretize, small["a_re"], small["a_im"],
                                             small["log_dt"], small["b_re"], small["b_im"])
    tab_f, tab_r = _scan_tables(abr, abi)
    bre = _block_diag_b(bbr)
    bim = _block_diag_b(bbi)
    cre = _block_diag_c(small["c_re"])
    cim = _block_diag_c(small["c_im"])
    tril = jnp.tril(jnp.ones((CHUNK, CHUNK), dtype=bool))
    ws = jnp.where(tril[None], small["w_s"], 0.0)
    ws_st = ws.reshape(SGU_G // 2, 2 * CHUNK, CHUNK).astype(BF16)
    wst_st = ws.transpose(0, 2, 1).reshape(SGU_G // 2, 2 * CHUNK, CHUNK).astype(BF16)
    bmat = jnp.repeat(small["b_s"].T, SGU_D, axis=1)
    g_mix2 = small["g_mix"].reshape(1, D_MODEL)
    g_ffn2 = small["g_ffn"].reshape(1, D_MODEL)
    g_final2 = small["g_final"].reshape(1, D_MODEL)
    g_sgu2 = small["g_sgu"].reshape(1, SGU_W)
    d_skip2 = small["d_skip"].reshape(1, SSM_W)
    b_glu2 = small["b_glu"].reshape(1, SSM_W)
    conv_b2 = small["conv_b"].reshape(1, 2 * D_FF)

    gat = {"w_in": _place_shard(shard2d(w["w_in"]), where, "place_w_in", BF16, PLACE_ROWS["w_in"])}
    gat.update(zip(BIG[1:], _place_shards([shard2d(w[n]) for n in BIG[1:]], where, "place_rest", BF16)))
    gat["conv_w"] = _place_shard(shard2d(w["conv_w"]), where, "place_conv_w", F32, 3)
    (gat["w_in"],), = _comm("gather_in", [_job_gather_now(gat["w_in"])])
    mixers = ["w_glu", "w_proj_a", "w_proj_b", "w_out"]
    rows = {n: (0, gat[n].shape[1]) for n in mixers}
    down_a, down_b = (0, D_FF // 8), (D_FF // 8, D_FF // 8)
    up_a, up_b = (0, 3 * D_MODEL // 8), (3 * D_MODEL // 8, 5 * D_MODEL // 8)
    span = (0.0, 1.0)

    names = mixers + ["conv_w", "w_down"]
    (p, h1, bur, bui), (got,) = _fwd_in(
        xs, g_mix2, gat["w_in"], bre, bim,
        [_job_gather([gat[n] for n in names],
                     [(i, rows[n], ICI, span) for i, n in enumerate(mixers)]
                     + [(4, None, ICI, span), (5, down_a, ICI, span)])])
    gat.update(zip(names, got))
    names = mixers + ["w_down", "w_up"]
    (str_, sti), (got,) = _scan_fwd(
        bur, bui, tab_f,
        [_job_gather([gat[n] for n in names],
                     [(i, rows[n], SIBLING, span) for i, n in enumerate(mixers)]
                     + [(4, down_a, SIBLING, span), (4, down_b, ICI, span), (5, up_a, ICI, span)])])
    gat.update(zip(names, got))
    w_glu_f = gat["w_glu"].reshape(SSM_W, SSM_W)
    w_out_f = gat["w_out"].reshape(D_MODEL, D_MODEL)
    conv_w_f = gat["conv_w"].transpose(1, 0, 2).reshape(3, 2 * D_FF)
    (x2, y0, z, mixed, ya, yb), ((gat["w_down"], gat["w_up"]),) = _fwd_mix(
        xs, p, str_, sti, cre, cim, d_skip2, w_glu_f, b_glu2, gat["w_proj_a"], g_sgu2, ws_st, bmat,
        gat["w_proj_b"], w_out_f,
        [_job_gather([gat["w_down"], gat["w_up"]],
                     [(0, down_b, SIBLING, span), (1, up_a, SIBLING, span),
                      (1, up_b, ICI, (0.0, 0.75)), (1, up_b, SIBLING, (0.75, 1.0))])])
    w_down_f = gat["w_down"].reshape(D_FF, D_MODEL)
    up, act, f, h2, dx3, sm_ffn = _fwd_ffn(x2, target, g_ffn2, gat["w_up"], conv_w_f, conv_b2,
                                           w_down_f, g_final2)

    def leg1_done(names, got):
        return _add_sibling([part[n] for n in names], got, where, "add_sibling_" + names[0])

    def leg2_done(names, sums, got):
        return _add_chips(sums, got, where, "add_chips_" + names[0])

    part, red = {}, {}
    part["w_down"] = _dw_rows(f, dx3, "dw_down", D_FF // 2, 4 * TK).reshape(
        N_CHIP, D_FF // N_CHIP, D_MODEL)
    (dx2, dup, sm_conv, sm_gffn), (got,) = _bwd_ffn(
        dx3, up, act, x2, g_ffn2, gat["w_up"], conv_w_f, w_down_f,
        [_job_sibling_halves([part["w_down"]])])
    sum_down = leg1_done(["w_down"], got)
    part["w_up"], (got,) = _dw_tiles(h2, dup, "dw_up", D_MODEL // 2, 2 * D_FF // N_CHIP,
                                     [_job_to_owner(sum_down)])
    red_down = leg2_done(["w_down"], sum_down, got)
    ((dsr, dsi, du_part, drest, mrg, dya, dyb, yap, dz, y1, sgu, dy0, sm_mix, dbm, dws),
     (got, (red["w_down"],))) = _bwd_mix(
        dx2, p, y0, z, mixed, ya, yb, w_out_f, gat["w_proj_a"], gat["w_proj_b"], w_glu_f, cre, cim,
        ws_st, wst_st, d_skip2, g_sgu2,
        [_job_sibling_halves([part["w_up"]]), _job_swap_halves(red_down)])
    sum_up = leg1_done(["w_up"], got)
    (lam_r, lam_i, dar8, dai8), (got,) = _scan_bwd(dsr, dsi, str_, sti, tab_r, [_job_to_owner(sum_up)])
    red_up = leg2_done(["w_up"], sum_up, got)
    mix4 = ["w_out", "w_proj_a", "w_proj_b", "w_glu"]
    part["w_out"] = _dw_cols(mrg, dx2, "dw_out", D_MODEL // 2, False)[0].reshape(
        N_CHIP, D_MODEL // N_CHIP, D_MODEL)
    part["w_proj_a"] = _dw_cols(yap, dya, "dw_proj_a", D_MODEL // N_CHIP, True)[0]
    part["w_proj_b"] = _dw_cols(sgu, dyb, "dw_proj_b", D_MODEL // N_CHIP, True)[0]
    part["w_glu"] = _dw_cols(y1, dz, "dw_glu", SSM_W, False)[0].reshape(
        N_CHIP, SSM_W // N_CHIP, SSM_W)
    got, (red["w_up"],) = _comm(
        "mixer_sibling_halves", [_job_sibling_halves([part[n] for n in mix4]), _job_swap_halves(red_up)])
    (grad_x, dp, sm_gmix), _ = _bwd_in(
        lam_r, lam_i, du_part, drest, xs, dx2, g_mix2, gat["w_in"], bre, bim)
    sums_m = leg1_done(mix4, got)
    part["w_in"], (got,) = _dw_cols(h1, dp, "dw_in", IN_COLS // N_CHIP, True, [_job_to_owner(sums_m)])
    red_m = leg2_done(mix4, sums_m, got)
    (dbd_r, dbd_i), (got, done_m) = _dw_pair(
        p, SSM_W, lam_r, lam_i, "db_bar",
        [_job_sibling_halves([part["w_in"]]), _job_swap_halves(red_m)])
    red.update(zip(mix4, done_m))
    sum_in = leg1_done(["w_in"], got)
    (dcd_r, dcd_i), (got,) = _dw_pair(dy0, SSM_W, str_, sti, "dc", [_job_to_owner(sum_in)])
    red_in = leg2_done(["w_in"], sum_in, got)
    (red["w_in"],), = _comm("swap_w_in", [_job_swap_halves(red_in)])

    def pick_c(slabs):
        two = LANE // SSM_P
        return jnp.einsum("jshsp->jshp", slabs.reshape(SSM_G // two, two, SSM_H, two, SSM_P)
                          ).reshape(SSM_G, SSM_H, SSM_P)

    def pick_b(slabs):
        return pick_c(slabs).transpose(0, 2, 1)

    dabr = jnp.sum(dar8, axis=0).reshape(SSM_G, SSM_P)
    dabi = jnp.sum(dai8, axis=0).reshape(SSM_G, SSM_P)
    d_a_re, d_a_im, d_log_dt, d_b_re, d_b_im = disc_vjp((dabr, dabi, pick_b(dbd_r), pick_b(dbd_i)))
    gsmall = {
        "g_mix": sm_gmix[0], "a_re": d_a_re, "a_im": d_a_im, "log_dt": d_log_dt,
        "b_re": d_b_re, "b_im": d_b_im, "c_re": pick_c(dcd_r), "c_im": -pick_c(dcd_i),
        "d_skip": sm_mix[0], "b_glu": sm_mix[1], "g_sgu": sm_mix[2],
        "w_s": jnp.where(tril[None], dws, 0.0),
        "b_s": dbm.reshape(CHUNK, SGU_G, SGU_D).sum(-1).T,
        "g_ffn": sm_gffn[0], "conv_b": sm_conv[3], "g_final": sm_ffn[0],
        "conv_w": sm_conv[0:3], "loss": sm_ffn[1, 0:1],
    }

    total_pack = _small_allreduce(_pack(gsmall))
    total = _unpack(total_pack)
    grads = dict(red)
    cs = 2 * D_FF // N_CHIP
    grads["conv_w"] = lax.dynamic_slice(total["conv_w"], (0, chip * cs), (3, cs))
    delta, new_m, new_v = {}, {}, {}
    for n in BIG + ("conv_w",):
        grads[n], delta[n], new_m[n], new_v[n] = _adamw(
            shard2d(w[n]), grads[n], shard2d(m[n]), shard2d(v[n]), "adamw_" + n, PLACE_ROWS.get(n, 3))
    for n in SMALL:
        grads[n] = total[n].reshape(w[n].shape)
    ud, um, uv = _adamw_many(*[[d[n] for n in SMALL] for d in (w, grads, m, v)], "adamw_small")
    for i, n in enumerate(SMALL):
        delta[n], new_m[n], new_v[n] = ud[i], um[i], uv[i]

    def like(d):
        return [d[n].reshape(w[n].shape) for n in WEIGHTS]

    return (total["loss"].reshape(()), grad_x.reshape(x.shape), *like(grads), *like(delta),
            *like(new_m), *like(new_v))
```
